```python
import jax, jax.numpy as jnp
from jax import lax
import numpy as np

D_MODEL = 1024
BATCH = 8
SEQ = 4096
DEPTH = 1

CHUNK = 64
SGU_BLOCK = 128
SGU_WIDTH = 1024
SGU_GROUPS = 8
SGU_GROUP_DIM = SGU_WIDTH // SGU_GROUPS
HGRN_WIDTH = 1024
HGRN_EXPAND = 128
HGRN_HEADS = HGRN_WIDTH // HGRN_EXPAND
N_BRANCH = 2
D_FF = 2816
CONV_WIDTH = 3
PLE_DIM = 256
LN_EPS = 1e-5
RMS_EPS = 1e-6
ALPHA = (2 * DEPTH) ** 0.25
BETA = (8 * DEPTH) ** -0.25
IN_COLS = 2 * SGU_WIDTH + 4 * HGRN_WIDTH + N_BRANCH * D_MODEL

kernel_name = "chunk_causal_sgu_hgrn2_hybrid"


def layer_norm(x, g, b):
    xf = x.astype(jnp.float32)
    mu = jnp.mean(xf, axis=-1, keepdims=True)
    var = jnp.mean(jnp.square(xf - mu), axis=-1, keepdims=True)
    return ((xf - mu) * lax.rsqrt(var + LN_EPS)).astype(x.dtype) * g + b


def sgu_mixer(u, v, w_s, b_s, g_v, b_v):
    bsz, t_len, _ = u.shape
    n_blk = t_len // SGU_BLOCK
    v = layer_norm(v, g_v, b_v).reshape(bsz, n_blk, SGU_BLOCK, SGU_GROUPS, SGU_GROUP_DIM)
    chunk_id = jnp.arange(SGU_BLOCK) // CHUNK
    mask = chunk_id[:, None] >= chunk_id[None, :]
    w = jnp.where(mask[None], w_s, jnp.zeros((), w_s.dtype))
    mixed = jnp.einsum('gts,bnsgc->bntgc', w, v) + b_s.T[None, None, :, :, None]
    return u * mixed.reshape(bsz, t_len, SGU_WIDTH)


def hgrn2_mixer(q, f_pre, i_in, og, lb, g_norm):
    bsz, t_len, _ = q.shape
    n_chunk = t_len // CHUNK
    f32 = jnp.float32
    qf = jax.nn.silu(q.astype(f32))
    f = lb + (1.0 - lb) * jax.nn.sigmoid(f_pre.astype(f32))
    kf = 1.0 - f
    logf = jnp.log(f)

    def to_chunks(z):
        return z.reshape(bsz, n_chunk, CHUNK, HGRN_HEADS, HGRN_EXPAND).transpose(1, 0, 3, 2, 4)

    qc, kc, ic = to_chunks(qf), to_chunks(kf), to_chunks(i_in.astype(f32))
    cc = jnp.cumsum(to_chunks(logf), axis=3)
    tri = jnp.tril(jnp.ones((CHUNK, CHUNK), dtype=bool))

    def step(state, inp):
        qk, kk, ik, ck = inp
        diff = ck[:, :, :, None, :] - ck[:, :, None, :, :]
        decay = jnp.exp(jnp.where(tri[None, None, :, :, None], diff, -jnp.inf))
        attn = jnp.einsum('bhte,bhse,bhtse->bhts', qk, kk, decay)
        o = (jnp.einsum('bhts,bhsv->bhtv', attn, ik)
             + jnp.einsum('bhte,bhev->bhtv', qk * jnp.exp(ck), state))
        c_last = ck[:, :, -1:, :]
        state = (jnp.exp(c_last[:, :, 0, :])[..., None] * state
                 + jnp.einsum('bhse,bhsv->bhev', kk * jnp.exp(c_last - ck), ik))
        return state, o

    s0 = jnp.zeros((bsz, HGRN_HEADS, HGRN_EXPAND, HGRN_EXPAND), f32)
    _, o = lax.scan(step, s0, (qc, kc, ic, cc))
    o = o.transpose(1, 0, 3, 2, 4).reshape(bsz, t_len, HGRN_HEADS, HGRN_EXPAND)
    o = o * lax.rsqrt(jnp.mean(jnp.square(o), axis=-1, keepdims=True) + RMS_EPS)
    o = o.reshape(bsz, t_len, HGRN_WIDTH).astype(q.dtype) * g_norm
    return o * jax.nn.silu(og)


def causal_dwconv(x, w, b):
    y = lax.conv_general_dilated(
        x, w[:, None, :], window_strides=(1,), padding=[(CONV_WIDTH - 1, 0)],
        dimension_numbers=('NWC', 'WIO', 'NWC'), feature_group_count=x.shape[-1])
    return y + b


def conv_ffn(x, w_up, conv_w, conv_b, w_down):
    h = x @ w_up
    gate, val = h[..., :D_FF], h[..., D_FF:]
    gate = causal_dwconv(gate, conv_w, conv_b)
    return (jax.nn.gelu(gate) * val) @ w_down


def _fwd_setup_inputs(seed: int = 0) -> dict:
    key = jax.random.key(seed)
    ks = jax.random.split(key, 24)
    n = lambda k, s, sc: jax.random.normal(k, s, jnp.float32) * sc
    L, D = DEPTH, D_MODEL
    return {
        "x": n(ks[0], (BATCH, SEQ, D), 1.0),
        "p": n(ks[1], (L, BATCH, SEQ, PLE_DIM), 1.0),
        "w_in": n(ks[2], (L, D, IN_COLS), D ** -0.5),
        "sgu_w_s": n(ks[3], (L, SGU_GROUPS, SGU_BLOCK, SGU_BLOCK), SGU_BLOCK ** -0.5),
        "sgu_b_s": 1.0 + n(ks[4], (L, SGU_GROUPS, SGU_BLOCK), 0.02),
        "sgu_norm_g": 1.0 + n(ks[5], (L, SGU_WIDTH), 0.02),
        "sgu_norm_b": n(ks[6], (L, SGU_WIDTH), 0.02),
        "hgrn_lb_logits": n(ks[7], (L + 1, HGRN_WIDTH), 0.1),
        "hgrn_norm_g": 1.0 + n(ks[8], (L, HGRN_WIDTH), 0.02),
        "w_branch": n(ks[9], (L, N_BRANCH, SGU_WIDTH, D), SGU_WIDTH ** -0.5),
        "w_out": n(ks[10], (L, D, D), D ** -0.5 * BETA),
        "ln1_g": 1.0 + n(ks[11], (L, D), 0.02),
        "ln1_b": n(ks[12], (L, D), 0.02),
        "ffn_w_up": n(ks[13], (L, D, 2 * D_FF), D ** -0.5),
        "ffn_conv_w": n(ks[14], (L, CONV_WIDTH, D_FF), CONV_WIDTH ** -0.5),
        "ffn_conv_b": n(ks[15], (L, D_FF), 0.02),
        "ffn_w_down": n(ks[16], (L, D_FF, D), D_FF ** -0.5 * BETA),
        "ln2_g": 1.0 + n(ks[17], (L, D), 0.02),
        "ln2_b": n(ks[18], (L, D), 0.02),
        "ple_w_proj": n(ks[19], (L, PLE_DIM, D), PLE_DIM ** -0.5 * BETA),
        "ple_w_gate": n(ks[20], (L, D, D), D ** -0.5),
    }


def _fwd_reference(x, p, w_in, sgu_w_s, sgu_b_s, sgu_norm_g, sgu_norm_b, hgrn_lb_logits,
              hgrn_norm_g, w_branch, w_out, ln1_g, ln1_b, ffn_w_up, ffn_conv_w,
              ffn_conv_b, ffn_w_down, ln2_g, ln2_b, ple_w_proj, ple_w_gate):
    splits = [SGU_WIDTH, 2 * SGU_WIDTH,
              2 * SGU_WIDTH + HGRN_WIDTH, 2 * SGU_WIDTH + 2 * HGRN_WIDTH,
              2 * SGU_WIDTH + 3 * HGRN_WIDTH, 2 * SGU_WIDTH + 4 * HGRN_WIDTH,
              2 * SGU_WIDTH + 4 * HGRN_WIDTH + D_MODEL]
    lb_all = jnp.cumsum(jax.nn.softmax(hgrn_lb_logits.astype(jnp.float32), axis=0), axis=0)
    for l in range(DEPTH):
        h = x @ w_in[l]
        u, v, q, f_pre, i_in, og, g_a, g_b = jnp.split(h, splits, axis=-1)
        y_a = sgu_mixer(jax.nn.gelu(u), jax.nn.gelu(v), sgu_w_s[l], sgu_b_s[l],
                        sgu_norm_g[l], sgu_norm_b[l])
        y_b = hgrn2_mixer(q, f_pre, i_in, og, lb_all[l], hgrn_norm_g[l])
        merged = (jax.nn.sigmoid(g_a) * (y_a @ w_branch[l, 0])
                  + jax.nn.sigmoid(g_b) * (y_b @ w_branch[l, 1]))
        x = layer_norm(ALPHA * x + merged @ w_out[l], ln1_g[l], ln1_b[l])
        ffn = conv_ffn(x, ffn_w_up[l], ffn_conv_w[l], ffn_conv_b[l], ffn_w_down[l])
        ple = jax.nn.sigmoid(x @ ple_w_gate[l]) * (p[l] @ ple_w_proj[l])
        x = layer_norm(ALPHA * x + ffn + ple, ln2_g[l], ln2_b[l])
    return x


import jax as _jax
import jax.numpy as _jnp

TWIN_FORMAT = 'train_step'
FWD_PARAMS = ['x', 'p', 'w_in', 'sgu_w_s', 'sgu_b_s', 'sgu_norm_g', 'sgu_norm_b', 'hgrn_lb_logits', 'hgrn_norm_g', 'w_branch', 'w_out', 'ln1_g', 'ln1_b', 'ffn_w_up', 'ffn_conv_w', 'ffn_conv_b', 'ffn_w_down', 'ln2_g', 'ln2_b', 'ple_w_proj', 'ple_w_gate']
TWIN_WEIGHTS = ['w_in', 'sgu_w_s', 'sgu_b_s', 'sgu_norm_g', 'sgu_norm_b', 'hgrn_lb_logits', 'hgrn_norm_g', 'w_branch', 'w_out', 'ln1_g', 'ln1_b', 'ffn_w_up', 'ffn_conv_w', 'ffn_conv_b', 'ffn_w_down', 'ln2_g', 'ln2_b', 'ple_w_proj', 'ple_w_gate']
TWIN_DIFF_INPUT = 'x'
TWIN_INPUTS = ['x', 'p', 'w_in', 'sgu_w_s', 'sgu_b_s', 'sgu_norm_g', 'sgu_norm_b', 'hgrn_lb_logits', 'hgrn_norm_g', 'w_branch', 'w_out', 'ln1_g', 'ln1_b', 'ffn_w_up', 'ffn_conv_w', 'ffn_conv_b', 'ffn_w_down', 'ln2_g', 'ln2_b', 'ple_w_proj', 'ple_w_gate', 'loss_target', 'm_w_in', 'm_sgu_w_s', 'm_sgu_b_s', 'm_sgu_norm_g', 'm_sgu_norm_b', 'm_hgrn_lb_logits', 'm_hgrn_norm_g', 'm_w_branch', 'm_w_out', 'm_ln1_g', 'm_ln1_b', 'm_ffn_w_up', 'm_ffn_conv_w', 'm_ffn_conv_b', 'm_ffn_w_down', 'm_ln2_g', 'm_ln2_b', 'm_ple_w_proj', 'm_ple_w_gate', 'v_w_in', 'v_sgu_w_s', 'v_sgu_b_s', 'v_sgu_norm_g', 'v_sgu_norm_b', 'v_hgrn_lb_logits', 'v_hgrn_norm_g', 'v_w_branch', 'v_w_out', 'v_ln1_g', 'v_ln1_b', 'v_ffn_w_up', 'v_ffn_conv_w', 'v_ffn_conv_b', 'v_ffn_w_down', 'v_ln2_g', 'v_ln2_b', 'v_ple_w_proj', 'v_ple_w_gate']
TWIN_OUTPUTS = ['loss', 'grad_x', 'grad_w_in', 'grad_sgu_w_s', 'grad_sgu_b_s', 'grad_sgu_norm_g', 'grad_sgu_norm_b', 'grad_hgrn_lb_logits', 'grad_hgrn_norm_g', 'grad_w_branch', 'grad_w_out', 'grad_ln1_g', 'grad_ln1_b', 'grad_ffn_w_up', 'grad_ffn_conv_w', 'grad_ffn_conv_b', 'grad_ffn_w_down', 'grad_ln2_g', 'grad_ln2_b', 'grad_ple_w_proj', 'grad_ple_w_gate', 'delta_w_in', 'delta_sgu_w_s', 'delta_sgu_b_s', 'delta_sgu_norm_g', 'delta_sgu_norm_b', 'delta_hgrn_lb_logits', 'delta_hgrn_norm_g', 'delta_w_branch', 'delta_w_out', 'delta_ln1_g', 'delta_ln1_b', 'delta_ffn_w_up', 'delta_ffn_conv_w', 'delta_ffn_conv_b', 'delta_ffn_w_down', 'delta_ln2_g', 'delta_ln2_b', 'delta_ple_w_proj', 'delta_ple_w_gate', 'new_m_w_in', 'new_m_sgu_w_s', 'new_m_sgu_b_s', 'new_m_sgu_norm_g', 'new_m_sgu_norm_b', 'new_m_hgrn_lb_logits', 'new_m_hgrn_norm_g', 'new_m_w_branch', 'new_m_w_out', 'new_m_ln1_g', 'new_m_ln1_b', 'new_m_ffn_w_up', 'new_m_ffn_conv_w', 'new_m_ffn_conv_b', 'new_m_ffn_w_down', 'new_m_ln2_g', 'new_m_ln2_b', 'new_m_ple_w_proj', 'new_m_ple_w_gate', 'new_v_w_in', 'new_v_sgu_w_s', 'new_v_sgu_b_s', 'new_v_sgu_norm_g', 'new_v_sgu_norm_b', 'new_v_hgrn_lb_logits', 'new_v_hgrn_norm_g', 'new_v_w_branch', 'new_v_w_out', 'new_v_ln1_g', 'new_v_ln1_b', 'new_v_ffn_w_up', 'new_v_ffn_conv_w', 'new_v_ffn_conv_b', 'new_v_ffn_w_down', 'new_v_ln2_g', 'new_v_ln2_b', 'new_v_ple_w_proj', 'new_v_ple_w_gate']
TWIN_LEAF_KINDS = {'loss': 'loss', 'grad_x': 'grad_x', 'grad_w_in': 'grad_w', 'grad_sgu_w_s': 'grad_w', 'grad_sgu_b_s': 'grad_w', 'grad_sgu_norm_g': 'grad_w', 'grad_sgu_norm_b': 'grad_w', 'grad_hgrn_lb_logits': 'grad_w', 'grad_hgrn_norm_g': 'grad_w', 'grad_w_branch': 'grad_w', 'grad_w_out': 'grad_w', 'grad_ln1_g': 'grad_w', 'grad_ln1_b': 'grad_w', 'grad_ffn_w_up': 'grad_w', 'grad_ffn_conv_w': 'grad_w', 'grad_ffn_conv_b': 'grad_w', 'grad_ffn_w_down': 'grad_w', 'grad_ln2_g': 'grad_w', 'grad_ln2_b': 'grad_w', 'grad_ple_w_proj': 'grad_w', 'grad_ple_w_gate': 'grad_w', 'delta_w_in': 'delta_w', 'delta_sgu_w_s': 'delta_w', 'delta_sgu_b_s': 'delta_w', 'delta_sgu_norm_g': 'delta_w', 'delta_sgu_norm_b': 'delta_w', 'delta_hgrn_lb_logits': 'delta_w', 'delta_hgrn_norm_g': 'delta_w', 'delta_w_branch': 'delta_w', 'delta_w_out': 'delta_w', 'delta_ln1_g': 'delta_w', 'delta_ln1_b': 'delta_w', 'delta_ffn_w_up': 'delta_w', 'delta_ffn_conv_w': 'delta_w', 'delta_ffn_conv_b': 'delta_w', 'delta_ffn_w_down': 'delta_w', 'delta_ln2_g': 'delta_w', 'delta_ln2_b': 'delta_w', 'delta_ple_w_proj': 'delta_w', 'delta_ple_w_gate': 'delta_w', 'new_m_w_in': 'new_m', 'new_m_sgu_w_s': 'new_m', 'new_m_sgu_b_s': 'new_m', 'new_m_sgu_norm_g': 'new_m', 'new_m_sgu_norm_b': 'new_m', 'new_m_hgrn_lb_logits': 'new_m', 'new_m_hgrn_norm_g': 'new_m', 'new_m_w_branch': 'new_m', 'new_m_w_out': 'new_m', 'new_m_ln1_g': 'new_m', 'new_m_ln1_b': 'new_m', 'new_m_ffn_w_up': 'new_m', 'new_m_ffn_conv_w': 'new_m', 'new_m_ffn_conv_b': 'new_m', 'new_m_ffn_w_down': 'new_m', 'new_m_ln2_g': 'new_m', 'new_m_ln2_b': 'new_m', 'new_m_ple_w_proj': 'new_m', 'new_m_ple_w_gate': 'new_m', 'new_v_w_in': 'new_v', 'new_v_sgu_w_s': 'new_v', 'new_v_sgu_b_s': 'new_v', 'new_v_sgu_norm_g': 'new_v', 'new_v_sgu_norm_b': 'new_v', 'new_v_hgrn_lb_logits': 'new_v', 'new_v_hgrn_norm_g': 'new_v', 'new_v_w_branch': 'new_v', 'new_v_w_out': 'new_v', 'new_v_ln1_g': 'new_v', 'new_v_ln1_b': 'new_v', 'new_v_ffn_w_up': 'new_v', 'new_v_ffn_conv_w': 'new_v', 'new_v_ffn_conv_b': 'new_v', 'new_v_ffn_w_down': 'new_v', 'new_v_ln2_g': 'new_v', 'new_v_ln2_b': 'new_v', 'new_v_ple_w_proj': 'new_v', 'new_v_ple_w_gate': 'new_v'}


def _forward(args):
    return _fwd_reference(*[args[k] for k in FWD_PARAMS])


def _output_shape():
    out = _jax.eval_shape(lambda: _forward(_fwd_setup_inputs(0)))
    return out.shape, out.dtype

N_MICROBATCH = 1
ADAM_LR = 0.001
ADAM_B1 = 0.9
ADAM_B2 = 0.999
ADAM_EPS = 1e-08
ADAM_WD = 0.01
ADAM_STEP = 10
PER_EXAMPLE_BATCH_AXIS = {'x': 0, 'p': 1, 'loss_target': 0}
SHARED_INPUTS = []
_WEIGHT_DTYPES = {'w_in': _jnp.float32, 'sgu_w_s': _jnp.float32, 'sgu_b_s': _jnp.float32, 'sgu_norm_g': _jnp.float32, 'sgu_norm_b': _jnp.float32, 'hgrn_lb_logits': _jnp.float32, 'hgrn_norm_g': _jnp.float32, 'w_branch': _jnp.float32, 'w_out': _jnp.float32, 'ln1_g': _jnp.float32, 'ln1_b': _jnp.float32, 'ffn_w_up': _jnp.float32, 'ffn_conv_w': _jnp.float32, 'ffn_conv_b': _jnp.float32, 'ffn_w_down': _jnp.float32, 'ln2_g': _jnp.float32, 'ln2_b': _jnp.float32, 'ple_w_proj': _jnp.float32, 'ple_w_gate': _jnp.float32}
MOMENT_SCALE = {'w_in': 2.437981e-02, 'sgu_w_s': 2.663228e-02, 'sgu_b_s': 3.155628e-02, 'sgu_norm_g': 2.736476e-02, 'sgu_norm_b': 2.967293e-02, 'hgrn_lb_logits': 2.665637e-03, 'hgrn_norm_g': 2.959857e-02, 'w_branch': 3.842009e-02, 'w_out': 9.078391e-02, 'ln1_g': 8.348360e-01, 'ln1_b': 4.253877e-01, 'ffn_w_up': 3.296666e-02, 'ffn_conv_w': 3.451657e-02, 'ffn_conv_b': 3.494346e-02, 'ffn_w_down': 9.060763e-02, 'ln2_g': 3.200780e+01, 'ln2_b': 1.892418e+00, 'ple_w_proj': 7.453002e-02, 'ple_w_gate': 1.723128e-02}


def _to_microbatches(a, axis):
    t = _jnp.moveaxis(a, axis, 0)
    t = t.reshape((N_MICROBATCH, t.shape[0] // N_MICROBATCH) + t.shape[1:])
    return _jnp.moveaxis(t, 1, axis + 1)


def setup_inputs(seed: int = 0) -> dict:
    inp = _fwd_setup_inputs(seed)
    key = _jax.random.fold_in(_jax.random.key(seed), 7919)
    shape, _ = _output_shape()
    out = dict(inp)
    out["loss_target"] = _jax.random.normal(_jax.random.fold_in(key, 0), shape, _jnp.float32)
    for i, name in enumerate(TWIN_WEIGHTS):
        w = inp[name].astype(_jnp.float32)
        if MOMENT_SCALE is None:
            s = _jnp.sqrt(_jnp.mean(_jnp.square(w)) + 1e-30)
        else:
            s = MOMENT_SCALE[name]
        km, kv = _jax.random.split(_jax.random.fold_in(key, i + 1))
        out[name] = w
        out["m_" + name] = s * _jax.random.normal(km, w.shape, _jnp.float32)
        out["v_" + name] = (s * s) * _jax.random.uniform(kv, w.shape, _jnp.float32, 0.5, 1.5)
    if N_MICROBATCH > 1:
        for name, axis in PER_EXAMPLE_BATCH_AXIS.items():
            out[name] = _to_microbatches(out[name], axis)
    return {'x': out['x'], 'p': out['p'], 'w_in': out['w_in'], 'sgu_w_s': out['sgu_w_s'], 'sgu_b_s': out['sgu_b_s'], 'sgu_norm_g': out['sgu_norm_g'], 'sgu_norm_b': out['sgu_norm_b'], 'hgrn_lb_logits': out['hgrn_lb_logits'], 'hgrn_norm_g': out['hgrn_norm_g'], 'w_branch': out['w_branch'], 'w_out': out['w_out'], 'ln1_g': out['ln1_g'], 'ln1_b': out['ln1_b'], 'ffn_w_up': out['ffn_w_up'], 'ffn_conv_w': out['ffn_conv_w'], 'ffn_conv_b': out['ffn_conv_b'], 'ffn_w_down': out['ffn_w_down'], 'ln2_g': out['ln2_g'], 'ln2_b': out['ln2_b'], 'ple_w_proj': out['ple_w_proj'], 'ple_w_gate': out['ple_w_gate'], 'loss_target': out['loss_target'], 'm_w_in': out['m_w_in'], 'm_sgu_w_s': out['m_sgu_w_s'], 'm_sgu_b_s': out['m_sgu_b_s'], 'm_sgu_norm_g': out['m_sgu_norm_g'], 'm_sgu_norm_b': out['m_sgu_norm_b'], 'm_hgrn_lb_logits': out['m_hgrn_lb_logits'], 'm_hgrn_norm_g': out['m_hgrn_norm_g'], 'm_w_branch': out['m_w_branch'], 'm_w_out': out['m_w_out'], 'm_ln1_g': out['m_ln1_g'], 'm_ln1_b': out['m_ln1_b'], 'm_ffn_w_up': out['m_ffn_w_up'], 'm_ffn_conv_w': out['m_ffn_conv_w'], 'm_ffn_conv_b': out['m_ffn_conv_b'], 'm_ffn_w_down': out['m_ffn_w_down'], 'm_ln2_g': out['m_ln2_g'], 'm_ln2_b': out['m_ln2_b'], 'm_ple_w_proj': out['m_ple_w_proj'], 'm_ple_w_gate': out['m_ple_w_gate'], 'v_w_in': out['v_w_in'], 'v_sgu_w_s': out['v_sgu_w_s'], 'v_sgu_b_s': out['v_sgu_b_s'], 'v_sgu_norm_g': out['v_sgu_norm_g'], 'v_sgu_norm_b': out['v_sgu_norm_b'], 'v_hgrn_lb_logits': out['v_hgrn_lb_logits'], 'v_hgrn_norm_g': out['v_hgrn_norm_g'], 'v_w_branch': out['v_w_branch'], 'v_w_out': out['v_w_out'], 'v_ln1_g': out['v_ln1_g'], 'v_ln1_b': out['v_ln1_b'], 'v_ffn_w_up': out['v_ffn_w_up'], 'v_ffn_conv_w': out['v_ffn_conv_w'], 'v_ffn_conv_b': out['v_ffn_conv_b'], 'v_ffn_w_down': out['v_ffn_w_down'], 'v_ln2_g': out['v_ln2_g'], 'v_ln2_b': out['v_ln2_b'], 'v_ple_w_proj': out['v_ple_w_proj'], 'v_ple_w_gate': out['v_ple_w_gate']}


def _loss(weights, diff, rest, loss_target):
    with _jax.named_scope("forward"):
        args = {**rest, TWIN_DIFF_INPUT: diff, **{k: w.astype(_WEIGHT_DTYPES[k]) for k, w in weights.items()}}
        y = _forward(args)
    with _jax.named_scope("loss_head"):
        err = _jnp.square(y.astype(_jnp.float32) - loss_target)
        return 0.5 * _jnp.sum(_jnp.mean(err, axis=-1)) if err.ndim else 0.5 * err


def _adamw(w, g, m, v):
    m = ADAM_B1 * m + (1.0 - ADAM_B1) * g
    v = ADAM_B2 * v + (1.0 - ADAM_B2) * _jnp.square(g)
    m_hat = m / (1.0 - ADAM_B1 ** ADAM_STEP)
    v_hat = v / (1.0 - ADAM_B2 ** ADAM_STEP)
    delta = -ADAM_LR * (m_hat / (_jnp.sqrt(v_hat) + ADAM_EPS) + ADAM_WD * w)
    return delta, m, v


def reference(x, p, w_in, sgu_w_s, sgu_b_s, sgu_norm_g, sgu_norm_b, hgrn_lb_logits, hgrn_norm_g, w_branch, w_out, ln1_g, ln1_b, ffn_w_up, ffn_conv_w, ffn_conv_b, ffn_w_down, ln2_g, ln2_b, ple_w_proj, ple_w_gate, loss_target, m_w_in, m_sgu_w_s, m_sgu_b_s, m_sgu_norm_g, m_sgu_norm_b, m_hgrn_lb_logits, m_hgrn_norm_g, m_w_branch, m_w_out, m_ln1_g, m_ln1_b, m_ffn_w_up, m_ffn_conv_w, m_ffn_conv_b, m_ffn_w_down, m_ln2_g, m_ln2_b, m_ple_w_proj, m_ple_w_gate, v_w_in, v_sgu_w_s, v_sgu_b_s, v_sgu_norm_g, v_sgu_norm_b, v_hgrn_lb_logits, v_hgrn_norm_g, v_w_branch, v_w_out, v_ln1_g, v_ln1_b, v_ffn_w_up, v_ffn_conv_w, v_ffn_conv_b, v_ffn_w_down, v_ln2_g, v_ln2_b, v_ple_w_proj, v_ple_w_gate):
    given = dict(x=x, p=p, w_in=w_in, sgu_w_s=sgu_w_s, sgu_b_s=sgu_b_s, sgu_norm_g=sgu_norm_g, sgu_norm_b=sgu_norm_b, hgrn_lb_logits=hgrn_lb_logits, hgrn_norm_g=hgrn_norm_g, w_branch=w_branch, w_out=w_out, ln1_g=ln1_g, ln1_b=ln1_b, ffn_w_up=ffn_w_up, ffn_conv_w=ffn_conv_w, ffn_conv_b=ffn_conv_b, ffn_w_down=ffn_w_down, ln2_g=ln2_g, ln2_b=ln2_b, ple_w_proj=ple_w_proj, ple_w_gate=ple_w_gate, loss_target=loss_target, m_w_in=m_w_in, m_sgu_w_s=m_sgu_w_s, m_sgu_b_s=m_sgu_b_s, m_sgu_norm_g=m_sgu_norm_g, m_sgu_norm_b=m_sgu_norm_b, m_hgrn_lb_logits=m_hgrn_lb_logits, m_hgrn_norm_g=m_hgrn_norm_g, m_w_branch=m_w_branch, m_w_out=m_w_out, m_ln1_g=m_ln1_g, m_ln1_b=m_ln1_b, m_ffn_w_up=m_ffn_w_up, m_ffn_conv_w=m_ffn_conv_w, m_ffn_conv_b=m_ffn_conv_b, m_ffn_w_down=m_ffn_w_down, m_ln2_g=m_ln2_g, m_ln2_b=m_ln2_b, m_ple_w_proj=m_ple_w_proj, m_ple_w_gate=m_ple_w_gate, v_w_in=v_w_in, v_sgu_w_s=v_sgu_w_s, v_sgu_b_s=v_sgu_b_s, v_sgu_norm_g=v_sgu_norm_g, v_sgu_norm_b=v_sgu_norm_b, v_hgrn_lb_logits=v_hgrn_lb_logits, v_hgrn_norm_g=v_hgrn_norm_g, v_w_branch=v_w_branch, v_w_out=v_w_out, v_ln1_g=v_ln1_g, v_ln1_b=v_ln1_b, v_ffn_w_up=v_ffn_w_up, v_ffn_conv_w=v_ffn_conv_w, v_ffn_conv_b=v_ffn_conv_b, v_ffn_w_down=v_ffn_w_down, v_ln2_g=v_ln2_g, v_ln2_b=v_ln2_b, v_ple_w_proj=v_ple_w_proj, v_ple_w_gate=v_ple_w_gate)
    weights = {n: given[n] for n in TWIN_WEIGHTS}
    shared = {n: given[n] for n in SHARED_INPUTS}
    per_example = {n: given[n] for n in ['x', 'p']}
    grad_fn = _jax.value_and_grad(_loss, argnums=(0, 1))

    def one_microbatch(ex, loss_target):
        ex = dict(ex)
        diff = ex.pop(TWIN_DIFF_INPUT)
        return grad_fn(weights, diff, {**shared, **ex}, loss_target)

    if N_MICROBATCH == 1:
        loss, (grad_w, grad_x) = one_microbatch(per_example, given["loss_target"])
    else:
        def body(carry, xs):
            loss_sum, grad_sum = carry
            l_k, (gw_k, gx_k) = one_microbatch(xs[0], xs[1])
            with _jax.named_scope("update"):
                return (loss_sum + l_k, _jax.tree.map(_jnp.add, grad_sum, gw_k)), gx_k

        init = (_jnp.zeros((), _jnp.float32), _jax.tree.map(_jnp.zeros_like, weights))
        (loss, grad_w), grad_x = _jax.lax.scan(body, init, (per_example, given["loss_target"]))
    with _jax.named_scope("update"):
        delta_w, new_m, new_v = {}, {}, {}
        for n in TWIN_WEIGHTS:
            delta_w[n], new_m[n], new_v[n] = _adamw(weights[n], grad_w[n], given["m_" + n], given["v_" + n])
    return (loss, grad_x, *[grad_w[n] for n in TWIN_WEIGHTS], *[delta_w[n] for n in TWIN_WEIGHTS],
            *[new_m[n] for n in TWIN_WEIGHTS], *[new_v[n] for n in TWIN_WEIGHTS])
```

```python
import functools

import jax
import jax.numpy as jnp
from jax import lax
from jax.experimental import pallas as pl
from jax.experimental.pallas import tpu as pltpu

F32 = jnp.float32
BF16 = jnp.bfloat16
HIGHEST = lax.Precision.HIGHEST
MESH = pl.DeviceIdType.MESH

D_MODEL = 1024
CHUNK = 64
SGU_BLOCK = 128
N_GROUP = 8
N_HEAD = 8
HEAD_DIM = 128
D_FF = 2816
PLE_DIM = 256
IN_COLS = 8192
LN_EPS = 1e-5
RMS_EPS = 1e-6
ALPHA = 2.0 ** 0.25
N_CHIP = 4
N_DEV = 8

ADAM_LR = 0.001
ADAM_B1 = 0.9
ADAM_B2 = 0.999
ADAM_EPS = 1e-08
ADAM_WD = 0.01
ADAM_STEP = 10

VMEM_LIMIT = 56 * 1024 * 1024

NN = (((1,), (0,)), ((), ()))
NT = (((1,), (1,)), ((), ()))
TN = (((0,), (0,)), ((), ()))


def _pc(body, *, name, out_shape, grid=None, in_specs=None, out_specs=None, scratch=(),
        sem=None, nsp=0, vmem=VMEM_LIMIT):
    params = dict(vmem_limit_bytes=vmem)
    if sem is not None:
        params["dimension_semantics"] = sem
    kw = dict(name=name, out_shape=out_shape, compiler_params=pltpu.CompilerParams(**params))
    if nsp:
        kw["grid_spec"] = pltpu.PrefetchScalarGridSpec(
            num_scalar_prefetch=nsp, grid=grid, in_specs=in_specs, out_specs=out_specs,
            scratch_shapes=list(scratch))
    else:
        if grid is not None:
            kw["grid"] = grid
        if in_specs is not None:
            kw["in_specs"] = in_specs
            kw["out_specs"] = out_specs
        kw["scratch_shapes"] = list(scratch)
    return pl.pallas_call(body, **kw)


def _dot(a, b, dims=NN):
    return lax.dot_general(a.astype(BF16), b.astype(BF16), dims, preferred_element_type=F32)


def _dot32(a, b, dims=NN):
    return lax.dot_general(a, b, dims, precision=HIGHEST, preferred_element_type=F32)


def _sig(x):
    return 1.0 / (1.0 + jnp.exp(-x))


_GC = 0.7978845608028654
_GA = 0.044715


def _gelu(x):
    return 0.5 * x * (1.0 + jnp.tanh(_GC * (x + _GA * x * x * x)))


def _gelu_and_grad(x):
    t = jnp.tanh(_GC * (x + _GA * x * x * x))
    g = 0.5 * x * (1.0 + t)
    dg = 0.5 * (1.0 + t) + 0.5 * x * (1.0 - t * t) * _GC * (1.0 + 3.0 * _GA * x * x)
    return g, dg


def _ln_stats(r):
    mu = jnp.mean(r, axis=-1, keepdims=True)
    xc = r - mu
    var = jnp.mean(xc * xc, axis=-1, keepdims=True)
    rstd = lax.rsqrt(var + LN_EPS)
    return xc * rstd, rstd


def _ln_bwd(dxh, xh, rstd):
    m1 = jnp.mean(dxh, axis=-1, keepdims=True)
    m2 = jnp.mean(dxh * xh, axis=-1, keepdims=True)
    return rstd * (dxh - m1 - xh * m2)


def _colsum8(v):
    return jnp.broadcast_to(jnp.sum(v, axis=0, keepdims=True), (8, v.shape[1]))


def _adamw(w, g, m, v):
    m2 = ADAM_B1 * m + (1.0 - ADAM_B1) * g
    v2 = ADAM_B2 * v + (1.0 - ADAM_B2) * (g * g)
    m_hat = m2 / (1.0 - ADAM_B1 ** ADAM_STEP)
    v_hat = v2 / (1.0 - ADAM_B2 ** ADAM_STEP)
    delta = -ADAM_LR * (m_hat / (jnp.sqrt(v_hat) + ADAM_EPS) + ADAM_WD * w)
    return delta, m2, v2


def _row_tile(rows, cols, itemsize=4, budget=1 << 20):
    best = 8
    for tr in range(8, rows + 1, 8):
        if rows % tr == 0 and tr * cols * itemsize <= budget:
            best = tr
    return best


def _mm(name, a, b, dims, grid, a_spec, b_spec, out_shape, o_spec, add=None, add_spec=None,
        add_scale=1.0):
    nk = grid[2]
    has_add = add is not None
    out_dtype = out_shape.dtype

    def body(*refs):
        if has_add:
            a_ref, b_ref, add_ref, o_ref = refs[:4]
            rest = refs[4:]
        else:
            a_ref, b_ref, o_ref = refs[:3]
            add_ref = None
            rest = refs[3:]
        prod = _dot(a_ref[...], b_ref[...], dims)

        def finish(acc):
            if has_add:
                acc = acc + add_scale * add_ref[...]
            o_ref[...] = acc.astype(out_dtype)

        if nk == 1:
            finish(prod)
        else:
            acc_ref = rest[0]
            k = pl.program_id(2)

            @pl.when(k == 0)
            def _():
                acc_ref[...] = prod

            @pl.when(k > 0)
            def _():
                acc_ref[...] += prod

            @pl.when(k == nk - 1)
            def _():
                finish(acc_ref[...])

    in_specs = [a_spec, b_spec] + ([add_spec] if has_add else [])
    args = [a, b] + ([add] if has_add else [])
    scratch = []
    if nk > 1:
        blk = [d for d in o_spec.block_shape if d is not None]
        scratch = [pltpu.VMEM(tuple(blk), F32)]
    return _pc(body, name=name, out_shape=out_shape, grid=grid, in_specs=in_specs,
               out_specs=o_spec, scratch=scratch,
               sem=("parallel", "parallel", "arbitrary"))(*args)


def _mm_nn_stacked(name, a, w_st, tm):
    t, k = a.shape
    _, _, c = w_st.shape
    return _mm(name, a, w_st, NN, (t // tm, N_CHIP, 1),
               pl.BlockSpec((tm, k), lambda i, j, kk: (i, 0)),
               pl.BlockSpec((None, k, c), lambda i, j, kk: (j, 0, 0)),
               jax.ShapeDtypeStruct((t, N_CHIP * c), F32),
               pl.BlockSpec((tm, c), lambda i, j, kk: (i, j)))


def _mm_tn(name, a, b, tm, tn, tk, stacked=False):
    t, m = a.shape
    _, n = b.shape
    if stacked:
        assert tm == m
        out_shape = jax.ShapeDtypeStruct((n // tn, m, tn), F32)
        o_spec = pl.BlockSpec((None, tm, tn), lambda i, j, kk: (j, 0, 0))
    else:
        out_shape = jax.ShapeDtypeStruct((m, n), F32)
        o_spec = pl.BlockSpec((tm, tn), lambda i, j, kk: (i, j))
    return _mm(name, a, b, TN, (m // tm, n // tn, t // tk),
               pl.BlockSpec((tk, tm), lambda i, j, kk: (kk, i)),
               pl.BlockSpec((tk, tn), lambda i, j, kk: (kk, j)),
               out_shape, o_spec)


def _sgu_mixed(v, wm_ref, bsb_ref, gv, bv):
    gl, dgl = _gelu_and_grad(v)
    vh, rstd = _ln_stats(gl)
    vn = vh * gv + bv
    mixed = []
    for g in range(N_GROUP):
        sl = slice(g * 128, (g + 1) * 128)
        mixed.append(_dot(wm_ref[g], vn[:, sl]) + bsb_ref[g])
    return dgl, vh, rstd, vn, mixed


def _sgu_fwd(h, wm, bsb, gv, bv):
    t = h.shape[0]

    def body(u_ref, v_ref, wm_ref, bsb_ref, gv_ref, bv_ref, ya_ref):
        u = u_ref[...]
        _, _, _, _, mixed = _sgu_mixed(v_ref[...], wm_ref, bsb_ref, gv_ref[...], bv_ref[...])
        gu = _gelu(u)
        for g in range(N_GROUP):
            sl = slice(g * 128, (g + 1) * 128)
            ya_ref[:, sl] = (gu[:, sl] * mixed[g]).astype(BF16)

    full3 = pl.BlockSpec((N_GROUP, 128, 128), lambda i: (0, 0, 0))
    vec = pl.BlockSpec((1, D_MODEL), lambda i: (0, 0))
    return _pc(body, name="sgu_fwd", out_shape=jax.ShapeDtypeStruct((t, D_MODEL), BF16),
               grid=(t // SGU_BLOCK,),
               in_specs=[pl.BlockSpec((SGU_BLOCK, D_MODEL), lambda i: (i, 0)),
                         pl.BlockSpec((SGU_BLOCK, D_MODEL), lambda i: (i, 1)),
                         full3, full3, vec, vec],
               out_specs=pl.BlockSpec((SGU_BLOCK, D_MODEL), lambda i: (i, 0)),
               sem=("parallel",))(h, h, wm, bsb, gv, bv)


def _sgu_bwd(h, dya, wm, wmt, bsb, gv, bv, maskf):
    t = h.shape[0]
    nb = t // SGU_BLOCK

    def body(u_ref, v_ref, dya_ref, wm_ref, wmt_ref, bsb_ref, gv_ref, bv_ref, mask_ref,
             dh_ref, dws_ref, dbs_ref, dgv_ref, dbv_ref, dmix_acc):
        i = pl.program_id(0)

        @pl.when(i == 0)
        def _():
            dws_ref[...] = jnp.zeros_like(dws_ref)
            dgv_ref[...] = jnp.zeros_like(dgv_ref)
            dbv_ref[...] = jnp.zeros_like(dbv_ref)
            dmix_acc[...] = jnp.zeros_like(dmix_acc)

        u = u_ref[...]
        gvv = gv_ref[...]
        dgl_v, vh, rstd, vn, mixed = _sgu_mixed(v_ref[...], wm_ref, bsb_ref, gvv, bv_ref[...])
        gu, dgl_u = _gelu_and_grad(u)
        dya_v = dya_ref[...]
        dvn_parts = []
        for g in range(N_GROUP):
            sl = slice(g * 128, (g + 1) * 128)
            d_y = dya_v[:, sl]
            dh_ref[:, sl] = (d_y * mixed[g] * dgl_u[:, sl]).astype(BF16)
            d_mixed = d_y * gu[:, sl]
            dmix_acc[g] += d_mixed
            dws_ref[g] += _dot(d_mixed, vn[:, sl], NT) * mask_ref[...]
            dvn_parts.append(_dot(wmt_ref[g], d_mixed))
        dvn = jnp.concatenate(dvn_parts, axis=1)
        dgv_ref[...] += _colsum8(dvn * vh)
        dbv_ref[...] += _colsum8(dvn)
        d_gl = _ln_bwd(dvn * gvv, vh, rstd)
        dh_ref[:, D_MODEL:] = (d_gl * dgl_v).astype(BF16)

        @pl.when(i == nb - 1)
        def _():
            rowid = lax.broadcasted_iota(jnp.int32, (8, 128), 0)
            ones = jnp.ones((8, 128), F32)
            acc = jnp.zeros((8, 128), F32)
            for g in range(N_GROUP):
                rs = _dot32(ones, dmix_acc[g], NT)
                acc = jnp.where(rowid == g, rs, acc)
            dbs_ref[...] = acc

    full3 = pl.BlockSpec((N_GROUP, 128, 128), lambda i: (0, 0, 0))
    vec = pl.BlockSpec((1, D_MODEL), lambda i: (0, 0))
    acc8 = pl.BlockSpec((8, D_MODEL), lambda i: (0, 0))
    return _pc(body, name="sgu_bwd",
               out_shape=(jax.ShapeDtypeStruct((t, 2 * D_MODEL), BF16),
                          jax.ShapeDtypeStruct((N_GROUP, 128, 128), F32),
                          jax.ShapeDtypeStruct((8, 128), F32),
                          jax.ShapeDtypeStruct((8, D_MODEL), F32),
                          jax.ShapeDtypeStruct((8, D_MODEL), F32)),
               grid=(nb,),
               in_specs=[pl.BlockSpec((SGU_BLOCK, D_MODEL), lambda i: (i, 0)),
                         pl.BlockSpec((SGU_BLOCK, D_MODEL), lambda i: (i, 1)),
                         pl.BlockSpec((SGU_BLOCK, D_MODEL), lambda i: (i, 0)),
                         full3, full3, full3, vec, vec,
                         pl.BlockSpec((128, 128), lambda i: (0, 0))],
               out_specs=(pl.BlockSpec((SGU_BLOCK, 2 * D_MODEL), lambda i: (i, 0)),
                          full3, pl.BlockSpec((8, 128), lambda i: (0, 0)), acc8, acc8),
               scratch=[pltpu.VMEM((N_GROUP, 128, 128), F32)],
               sem=("arbitrary",))(h, h, dya, wm, wmt, bsb, gv, bv, maskf)


def _tri_masks():
    row = lax.broadcasted_iota(jnp.int32, (CHUNK, CHUNK), 0)
    col = lax.broadcasted_iota(jnp.int32, (CHUNK, CHUNK), 1)
    return col <= row, col >= row


def _lower_bound(logit_ref, sl):
    return _sig(logit_ref[0:1, sl] - logit_ref[1:2, sl])


def _hgrn_chunk(q, fp, ii, lb, st, causal):
    sg = _sig(fp)
    f = lb + (1.0 - lb) * sg
    k = 1.0 - f
    c = _dot32(causal.astype(F32), jnp.log(f))
    ec = jnp.exp(c)
    en = jnp.exp(-c)
    sq = _sig(q)
    qt = q * sq * ec
    kt = k * en
    attn = jnp.where(causal, _dot(qt, kt, NT), 0.0)
    o = _dot(attn, ii) + _dot(qt, st, NT)
    ecl = jnp.exp(c[CHUNK - 1:CHUNK, :])
    kk = kt * ecl
    st_new = st * ecl + _dot(ii, kk, TN)
    return dict(sg=sg, f=f, k=k, ec=ec, en=en, sq=sq, qt=qt, kt=kt, attn=attn, o=o, ecl=ecl,
                kk=kk, st_new=st_new)


def _hgrn_fwd(h, logits, gn):
    t = h.shape[0]
    nc = t // CHUNK

    def body(q_ref, f_ref, i_ref, og_ref, lg_ref, gn_ref, yb_ref, st_ref, state):
        ci = pl.program_id(0)

        @pl.when(ci == 0)
        def _():
            state[...] = jnp.zeros_like(state)

        causal, _ = _tri_masks()
        for hd in range(N_HEAD):
            sl = slice(hd * HEAD_DIM, (hd + 1) * HEAD_DIM)
            st = state[hd]
            st_ref[0, hd] = st
            r = _hgrn_chunk(q_ref[:, sl], f_ref[:, sl], i_ref[:, sl], _lower_bound(lg_ref, sl),
                            st, causal)
            state[hd] = r["st_new"]
            o = r["o"]
            on = o * lax.rsqrt(jnp.mean(o * o, axis=-1, keepdims=True) + RMS_EPS)
            og = og_ref[:, sl]
            yb_ref[:, sl] = (on * gn_ref[:, sl] * (og * _sig(og))).astype(BF16)

    def col(k):
        return pl.BlockSpec((CHUNK, D_MODEL), lambda ci: (ci, k))

    return _pc(body, name="hgrn_fwd",
               out_shape=(jax.ShapeDtypeStruct((t, D_MODEL), BF16),
                          jax.ShapeDtypeStruct((nc, N_HEAD, HEAD_DIM, HEAD_DIM), F32)),
               grid=(nc,),
               in_specs=[col(2), col(3), col(4), col(5),
                         pl.BlockSpec((2, D_MODEL), lambda ci: (0, 0)),
                         pl.BlockSpec((1, D_MODEL), lambda ci: (0, 0))],
               out_specs=(pl.BlockSpec((CHUNK, D_MODEL), lambda ci: (ci, 0)),
                          pl.BlockSpec((1, N_HEAD, HEAD_DIM, HEAD_DIM), lambda ci: (ci, 0, 0, 0))),
               scratch=[pltpu.VMEM((N_HEAD, HEAD_DIM, HEAD_DIM), F32)],
               sem=("arbitrary",))(h, h, h, h, logits, gn)


def _hgrn_bwd(h, dyb, st_all, logits, gn):
    t = h.shape[0]
    nc = t // CHUNK

    def body(q_ref, f_ref, i_ref, og_ref, dyb_ref, st_ref, lg_ref, gn_ref,
             dh1_ref, dh2_ref, dlb_ref, dgn_ref, dstate):
        ci = pl.program_id(0)

        @pl.when(ci == 0)
        def _():
            dstate[...] = jnp.zeros_like(dstate)
            dlb_ref[...] = jnp.zeros_like(dlb_ref)
            dgn_ref[...] = jnp.zeros_like(dgn_ref)

        causal, anti = _tri_masks()
        rowid = lax.broadcasted_iota(jnp.int32, (CHUNK, HEAD_DIM), 0)
        for hd in range(N_HEAD):
            sl = slice(hd * HEAD_DIM, (hd + 1) * HEAD_DIM)
            q = q_ref[:, sl]
            ii = i_ref[:, sl]
            og = og_ref[:, sl]
            lb = _lower_bound(lg_ref, sl)
            gnv = gn_ref[:, sl]
            st = st_ref[0, hd]
            dsn = dstate[hd]
            r = _hgrn_chunk(q, f_ref[:, sl], ii, lb, st, causal)
            o = r["o"]
            rinv = lax.rsqrt(jnp.mean(o * o, axis=-1, keepdims=True) + RMS_EPS)
            on = o * rinv
            so = _sig(og)
            sil = og * so
            dy = dyb_ref[:, sl]
            d_og = dy * on * gnv * (so * (1.0 + og * (1.0 - so)))
            dgn_ref[:, sl] += _colsum8(dy * on * sil)
            d_on = dy * gnv * sil
            d_o = rinv * (d_on - on * jnp.mean(d_on * on, axis=-1, keepdims=True))
            d_attn = jnp.where(causal, _dot(d_o, ii, NT), 0.0)
            d_i = _dot(r["attn"], d_o, TN) + _dot(r["kk"], dsn, NT)
            d_qt = _dot(d_attn, r["kt"]) + _dot(d_o, st)
            d_kt = _dot(d_attn, r["qt"], TN)
            d_kk = _dot(ii, dsn)
            dstate[hd] = _dot(d_o, r["qt"], TN) + dsn * r["ecl"]
            d_cl = (r["ecl"] * jnp.sum(st * dsn, axis=0, keepdims=True)
                    + jnp.sum(r["kk"] * d_kk, axis=0, keepdims=True))
            d_k = (d_kk * r["ecl"] + d_kt) * r["en"]
            d_c = (d_qt * r["qt"].astype(BF16).astype(F32) - d_kt * r["kt"].astype(BF16).astype(F32)
                   - d_kk * r["kk"])
            d_c = d_c + jnp.where(rowid == CHUNK - 1, d_cl, 0.0)
            d_lf = _dot32(anti.astype(F32), d_c)
            d_f = d_lf / r["f"] - d_k
            sg = r["sg"]
            dlb_ref[:, sl] += _colsum8(d_f * (1.0 - sg))
            d_fp = d_f * (1.0 - lb) * sg * (1.0 - sg)
            sq = r["sq"]
            d_q = d_qt * r["ec"] * (sq * (1.0 + q * (1.0 - sq)))
            dh1_ref[:, sl] = d_q.astype(BF16)
            dh1_ref[:, D_MODEL + hd * HEAD_DIM:D_MODEL + (hd + 1) * HEAD_DIM] = d_fp.astype(BF16)
            dh2_ref[:, sl] = d_i.astype(BF16)
            dh2_ref[:, D_MODEL + hd * HEAD_DIM:D_MODEL + (hd + 1) * HEAD_DIM] = d_og.astype(BF16)

    def col(k):
        return pl.BlockSpec((CHUNK, D_MODEL), lambda ci: (nc - 1 - ci, k))

    acc8 = pl.BlockSpec((8, D_MODEL), lambda ci: (0, 0))
    pair = pl.BlockSpec((CHUNK, 2 * D_MODEL), lambda ci: (nc - 1 - ci, 0))
    return _pc(body, name="hgrn_bwd",
               out_shape=(jax.ShapeDtypeStruct((t, 2 * D_MODEL), BF16),
                          jax.ShapeDtypeStruct((t, 2 * D_MODEL), BF16),
                          jax.ShapeDtypeStruct((8, D_MODEL), F32),
                          jax.ShapeDtypeStruct((8, D_MODEL), F32)),
               grid=(nc,),
               in_specs=[col(2), col(3), col(4), col(5),
                         pl.BlockSpec((CHUNK, D_MODEL), lambda ci: (nc - 1 - ci, 0)),
                         pl.BlockSpec((1, N_HEAD, HEAD_DIM, HEAD_DIM),
                                      lambda ci: (nc - 1 - ci, 0, 0, 0)),
                         pl.BlockSpec((2, D_MODEL), lambda ci: (0, 0)),
                         pl.BlockSpec((1, D_MODEL), lambda ci: (0, 0))],
               out_specs=(pair, pair, acc8, acc8),
               scratch=[pltpu.VMEM((N_HEAD, HEAD_DIM, HEAD_DIM), F32)],
               sem=("arbitrary",))(h, h, h, h, dyb, st_all, logits, gn)


def _mix_fwd(ya, yb, h, x, wb0, wb1, wo, g1, b1, tm):
    t = x.shape[0]

    def body(ya_ref, yb_ref, ga_ref, gb_ref, x_ref, wb0_ref, wb1_ref, wo_ref, g1_ref, b1_ref,
             r1_ref, a_ref, b_ref, m_ref, x1_ref):
        a = _dot(ya_ref[...], wb0_ref[...])
        b = _dot(yb_ref[...], wb1_ref[...])
        m = _sig(ga_ref[...]) * a + _sig(gb_ref[...]) * b
        r1 = ALPHA * x_ref[...] + _dot(m, wo_ref[...])
        xh, _ = _ln_stats(r1)
        r1_ref[...] = r1
        a_ref[...] = a
        b_ref[...] = b
        m_ref[...] = m.astype(BF16)
        x1_ref[...] = (xh * g1_ref[...] + b1_ref[...]).astype(BF16)

    tile = pl.BlockSpec((tm, D_MODEL), lambda i: (i, 0))
    wsp = pl.BlockSpec((D_MODEL, D_MODEL), lambda i: (0, 0))
    vec = pl.BlockSpec((1, D_MODEL), lambda i: (0, 0))
    f32o = jax.ShapeDtypeStruct((t, D_MODEL), F32)
    bfo = jax.ShapeDtypeStruct((t, D_MODEL), BF16)
    return _pc(body, name="mix_fwd", out_shape=(f32o, f32o, f32o, bfo, bfo), grid=(t // tm,),
               in_specs=[tile, tile,
                         pl.BlockSpec((tm, D_MODEL), lambda i: (i, 6)),
                         pl.BlockSpec((tm, D_MODEL), lambda i: (i, 7)),
                         tile, wsp, wsp, wsp, vec, vec],
               out_specs=(tile, tile, tile, tile, tile),
               sem=("parallel",))(ya, yb, h, h, x, wb0, wb1, wo, g1, b1)


def _mix_bwd(dr1, h, a, b, wo, wb0, wb1, tm):
    t = dr1.shape[0]

    def body(dr1_ref, ga_ref, gb_ref, a_ref, b_ref, wo_ref, wb0_ref, wb1_ref,
             da_ref, db_ref, dh3_ref, dya_ref, dyb_ref):
        d_m = _dot(dr1_ref[...], wo_ref[...], NT)
        sa = _sig(ga_ref[...])
        sb = _sig(gb_ref[...])
        d_a = (d_m * sa).astype(BF16)
        d_b = (d_m * sb).astype(BF16)
        da_ref[...] = d_a
        db_ref[...] = d_b
        dh3_ref[:, :D_MODEL] = (d_m * a_ref[...] * sa * (1.0 - sa)).astype(BF16)
        dh3_ref[:, D_MODEL:] = (d_m * b_ref[...] * sb * (1.0 - sb)).astype(BF16)
        dya_ref[...] = _dot(d_a, wb0_ref[...], NT)
        dyb_ref[...] = _dot(d_b, wb1_ref[...], NT)

    tile = pl.BlockSpec((tm, D_MODEL), lambda i: (i, 0))
    wsp = pl.BlockSpec((D_MODEL, D_MODEL), lambda i: (0, 0))
    f32o = jax.ShapeDtypeStruct((t, D_MODEL), F32)
    bfo = jax.ShapeDtypeStruct((t, D_MODEL), BF16)
    return _pc(body, name="mix_bwd",
               out_shape=(bfo, bfo, jax.ShapeDtypeStruct((t, 2 * D_MODEL), BF16), f32o, f32o),
               grid=(t // tm,),
               in_specs=[tile,
                         pl.BlockSpec((tm, D_MODEL), lambda i: (i, 6)),
                         pl.BlockSpec((tm, D_MODEL), lambda i: (i, 7)),
                         tile, tile, wsp, wsp, wsp],
               out_specs=(tile, tile, pl.BlockSpec((tm, 2 * D_MODEL), lambda i: (i, 0)),
                          tile, tile),
               sem=("parallel",))(dr1, h, h, a, b, wo, wb0, wb1)


FF_TILE = 1408
FF_NJ = D_FF // FF_TILE


def _shift_down(v, k):
    return pltpu.roll(v, k, 0)


def _shift_up(v, k):
    return pltpu.roll(v, v.shape[0] - k, 0)


def _conv_gate(ext, cw_ref, cb_ref):
    return (cw_ref[0:1, :] * _shift_down(ext, 2) + cw_ref[1:2, :] * _shift_down(ext, 1)
            + cw_ref[2:3, :] * ext + cb_ref[...])


def _ffn_act_fwd(h2, convw, convb, tm):
    t = h2.shape[0]
    nt8 = tm // 8

    def body(g_ref, gp_ref, v_ref, cw_ref, cb_ref, act_ref):
        i = pl.program_id(1)
        prev = gp_ref[...] * (i > 0).astype(F32)
        ext = jnp.concatenate([prev, g_ref[...]], axis=0)
        gc = _conv_gate(ext, cw_ref, cb_ref)[8:, :]
        act_ref[...] = (_gelu(gc) * v_ref[...]).astype(BF16)

    return _pc(body, name="ffn_act_fwd", out_shape=jax.ShapeDtypeStruct((t, D_FF), BF16),
               grid=(FF_NJ, t // tm),
               in_specs=[pl.BlockSpec((tm, FF_TILE), lambda j, i: (i, j)),
                         pl.BlockSpec((8, FF_TILE), lambda j, i: (jnp.maximum(i * nt8 - 1, 0), j)),
                         pl.BlockSpec((tm, FF_TILE), lambda j, i: (i, j + FF_NJ)),
                         pl.BlockSpec((3, FF_TILE), lambda j, i: (0, j)),
                         pl.BlockSpec((1, FF_TILE), lambda j, i: (0, j))],
               out_specs=pl.BlockSpec((tm, FF_TILE), lambda j, i: (i, j)),
               sem=("parallel", "parallel"))(h2, h2, h2, convw, convb)


def _ffn_act_bwd(h2, dact, convw, convb, tm):
    t = h2.shape[0]
    nt8 = tm // 8
    ni = t // tm
    last8 = t // 8 - 1

    def body(g_ref, gp_ref, gn_ref, v_ref, vn_ref, da_ref, dan_ref, cw_ref, cb_ref,
             dh2_ref, dcw_ref, dcb_ref):
        i = pl.program_id(1)

        @pl.when(i == 0)
        def _():
            dcw_ref[...] = jnp.zeros_like(dcw_ref)
            dcb_ref[...] = jnp.zeros_like(dcb_ref)

        prev = gp_ref[...] * (i > 0).astype(F32)
        ext = jnp.concatenate([prev, g_ref[...], gn_ref[...]], axis=0)
        vext = jnp.concatenate([jnp.zeros((8, FF_TILE), F32), v_ref[...], vn_ref[...]], axis=0)
        dnext = dan_ref[...] * (i < ni - 1).astype(F32)
        dext = jnp.concatenate([jnp.zeros((8, FF_TILE), F32), da_ref[...], dnext], axis=0)
        g2 = _shift_down(ext, 2)
        g1 = _shift_down(ext, 1)
        gc = cw_ref[0:1, :] * g2 + cw_ref[1:2, :] * g1 + cw_ref[2:3, :] * ext + cb_ref[...]
        gl, dgl = _gelu_and_grad(gc)
        d_gc = dext * vext * dgl
        d_gate = (cw_ref[2:3, :] * d_gc + cw_ref[1:2, :] * _shift_up(d_gc, 1)
                  + cw_ref[0:1, :] * _shift_up(d_gc, 2))
        dh2_ref[0] = d_gate[8:8 + tm, :].astype(BF16)
        dh2_ref[1] = (da_ref[...] * gl[8:8 + tm, :]).astype(BF16)
        dm = d_gc[8:8 + tm, :]
        s0 = jnp.sum(dm * g2[8:8 + tm, :], axis=0, keepdims=True)
        s1 = jnp.sum(dm * g1[8:8 + tm, :], axis=0, keepdims=True)
        s2 = jnp.sum(dm * ext[8:8 + tm, :], axis=0, keepdims=True)
        rowid = lax.broadcasted_iota(jnp.int32, (8, FF_TILE), 0)
        dcw_ref[...] += jnp.where(rowid == 0, s0, jnp.where(rowid == 1, s1,
                                                            jnp.where(rowid == 2, s2, 0.0)))
        dcb_ref[...] += _colsum8(dm)

    def prev8(off):
        return pl.BlockSpec((8, FF_TILE), lambda j, i: (jnp.maximum(i * nt8 - 1, 0), j + off))

    def next8(off):
        return pl.BlockSpec((8, FF_TILE), lambda j, i: (jnp.minimum((i + 1) * nt8, last8), j + off))

    def main(off):
        return pl.BlockSpec((tm, FF_TILE), lambda j, i: (i, j + off))

    acc = pl.BlockSpec((8, FF_TILE), lambda j, i: (0, j))
    return _pc(body, name="ffn_act_bwd",
               out_shape=(jax.ShapeDtypeStruct((2, t, D_FF), BF16),
                          jax.ShapeDtypeStruct((8, D_FF), F32),
                          jax.ShapeDtypeStruct((8, D_FF), F32)),
               grid=(FF_NJ, ni),
               in_specs=[main(0), prev8(0), next8(0), main(FF_NJ), next8(FF_NJ),
                         pl.BlockSpec((tm, FF_TILE), lambda j, i: (i, j)),
                         pl.BlockSpec((8, FF_TILE), lambda j, i: (jnp.minimum((i + 1) * nt8, last8), j)),
                         pl.BlockSpec((3, FF_TILE), lambda j, i: (0, j)),
                         pl.BlockSpec((1, FF_TILE), lambda j, i: (0, j))],
               out_specs=(pl.BlockSpec((2, tm, FF_TILE), lambda j, i: (0, i, j)), acc, acc),
               sem=("parallel", "arbitrary"))(h2, h2, h2, h2, h2, dact, dact, convw, convb)


def _out_fwd_bwd(act, x1b, r1, p2, tgt, wd, wpg, wpp, g1, b1, g2, b2, tm):
    t = r1.shape[0]

    def body(act_ref, x1b_ref, r1_ref, p_ref, tgt_ref, wd_ref, wpg_ref, wpp_ref,
             g1_ref, b1_ref, g2_ref, b2_ref,
             dr2_ref, dpg_ref, dpp_ref, loss_ref, dg2_ref, db2_ref):
        i = pl.program_id(0)

        @pl.when(i == 0)
        def _():
            loss_ref[...] = jnp.zeros_like(loss_ref)
            dg2_ref[...] = jnp.zeros_like(dg2_ref)
            db2_ref[...] = jnp.zeros_like(db2_ref)

        ffn = _dot(act_ref[...], wd_ref[...])
        pg = _dot(x1b_ref[...], wpg_ref[...])
        pp = _dot(p_ref[...], wpp_ref[...])
        s = _sig(pg)
        xh1, _ = _ln_stats(r1_ref[...])
        x1 = xh1 * g1_ref[...] + b1_ref[...]
        r2 = ALPHA * x1 + ffn + s * pp
        xh2, rstd2 = _ln_stats(r2)
        g2v = g2_ref[...]
        diff = xh2 * g2v + b2_ref[...] - tgt_ref[...]
        part = jnp.sum(jnp.sum(diff * diff, axis=1, keepdims=True), axis=0, keepdims=True)
        loss_ref[...] += jnp.broadcast_to(part * (0.5 / D_MODEL), loss_ref.shape)
        dy = diff * (1.0 / D_MODEL)
        dg2_ref[...] += _colsum8(dy * xh2)
        db2_ref[...] += _colsum8(dy)
        dr2 = _ln_bwd(dy * g2v, xh2, rstd2)
        dr2_ref[...] = dr2
        dpg_ref[...] = (dr2 * pp * s * (1.0 - s)).astype(BF16)
        dpp_ref[...] = (dr2 * s).astype(BF16)

    tile = pl.BlockSpec((tm, D_MODEL), lambda i: (i, 0))
    vec = pl.BlockSpec((1, D_MODEL), lambda i: (0, 0))
    acc8 = pl.BlockSpec((8, D_MODEL), lambda i: (0, 0))
    acc_shape = jax.ShapeDtypeStruct((8, D_MODEL), F32)
    return _pc(body, name="out_fwd_bwd",
               out_shape=(jax.ShapeDtypeStruct((t, D_MODEL), F32),
                          jax.ShapeDtypeStruct((t, D_MODEL), BF16),
                          jax.ShapeDtypeStruct((t, D_MODEL), BF16),
                          acc_shape, acc_shape, acc_shape),
               grid=(t // tm,),
               in_specs=[pl.BlockSpec((tm, D_FF), lambda i: (i, 0)), tile, tile,
                         pl.BlockSpec((tm, PLE_DIM), lambda i: (i, 0)), tile,
                         pl.BlockSpec((D_FF, D_MODEL), lambda i: (0, 0)),
                         pl.BlockSpec((D_MODEL, D_MODEL), lambda i: (0, 0)),
                         pl.BlockSpec((PLE_DIM, D_MODEL), lambda i: (0, 0)),
                         vec, vec, vec, vec],
               out_specs=(tile, tile, tile, acc8, acc8, acc8),
               sem=("arbitrary",))(act, x1b, r1, p2, tgt, wd, wpg, wpp, g1, b1, g2, b2)


def _ffn_in_bwd(dh2, wup_st, dpg, wpg, dr2, r1, g1, tm):
    t = r1.shape[0]
    ni = t // tm

    def body(dh2_ref, wup_ref, dpg_ref, wpg_ref, dr2_ref, r1_ref, g1_ref,
             dr1_ref, dg1_ref, db1_ref, acc):
        i = pl.program_id(0)
        j = pl.program_id(1)

        @pl.when((i == 0) & (j == 0))
        def _():
            dg1_ref[...] = jnp.zeros_like(dg1_ref)
            db1_ref[...] = jnp.zeros_like(db1_ref)

        @pl.when(j == 0)
        def _():
            acc[...] = _dot(dh2_ref[...], wup_ref[...], NT)

        @pl.when(j > 0)
        def _():
            acc[...] += _dot(dh2_ref[...], wup_ref[...], NT)

        @pl.when(j == N_CHIP - 1)
        def _():
            d_x1 = acc[...] + _dot(dpg_ref[...], wpg_ref[...], NT) + ALPHA * dr2_ref[...]
            xh, rstd = _ln_stats(r1_ref[...])
            dg1_ref[...] += _colsum8(d_x1 * xh)
            db1_ref[...] += _colsum8(d_x1)
            dr1_ref[...] = _ln_bwd(d_x1 * g1_ref[...], xh, rstd)

    tile = pl.BlockSpec((tm, D_MODEL), lambda i, j: (i, 0))
    acc8 = pl.BlockSpec((8, D_MODEL), lambda i, j: (0, 0))
    acc_shape = jax.ShapeDtypeStruct((8, D_MODEL), F32)
    return _pc(body, name="ffn_in_bwd",
               out_shape=(jax.ShapeDtypeStruct((t, D_MODEL), F32), acc_shape, acc_shape),
               grid=(ni, N_CHIP),
               in_specs=[pl.BlockSpec((None, tm, FF_TILE), lambda i, j: (j // FF_NJ, i, j % FF_NJ)),
                         pl.BlockSpec((None, D_MODEL, FF_TILE), lambda i, j: (j, 0, 0)),
                         tile, pl.BlockSpec((D_MODEL, D_MODEL), lambda i, j: (0, 0)),
                         tile, tile, pl.BlockSpec((1, D_MODEL), lambda i, j: (0, 0))],
               out_specs=(tile, acc8, acc8),
               scratch=[pltpu.VMEM((tm, D_MODEL), F32)],
               sem=("arbitrary", "arbitrary"))(dh2, wup_st, dpg, wpg, dr2, r1, g1)


ANY = pl.BlockSpec(memory_space=pl.ANY)


def _chip_peers():
    x, y, c = lax.axis_index("x"), lax.axis_index("y"), lax.axis_index("c")
    return x, y, c, [(1 - x, y), (x, 1 - y), (1 - x, 1 - y)]


def _gather_weights(shards):
    n = len(shards)

    def body(*refs):
        ins, outs = refs[:n], refs[n:2 * n]
        send_sems, recv_sems, local_sems = refs[2 * n:]
        x, y, c, peers = _chip_peers()
        me = 2 * x + y
        locals_, sends = [], []
        for ti in range(n):
            lc = pltpu.make_async_copy(ins[ti], outs[ti].at[me], local_sems.at[ti])
            lc.start()
            locals_.append(lc)
            for k, (px, py) in enumerate(peers):
                cp = pltpu.make_async_remote_copy(
                    src_ref=ins[ti], dst_ref=outs[ti].at[me],
                    send_sem=send_sems.at[ti * 3 + k], recv_sem=recv_sems.at[ti * 3 + k],
                    device_id=(px, py, c), device_id_type=MESH)
                cp.start()
                sends.append(cp)
        for ti in range(n):
            for k, (px, py) in enumerate(peers):
                pltpu.make_async_remote_copy(
                    src_ref=ins[ti], dst_ref=outs[ti].at[2 * px + py],
                    send_sem=send_sems.at[ti * 3 + k], recv_sem=recv_sems.at[ti * 3 + k],
                    device_id=(px, py, c), device_id_type=MESH).wait_recv()
        for cp in sends:
            cp.wait_send()
        for lc in locals_:
            lc.wait()

    return _pc(body, name="gather_weights",
               out_shape=tuple(jax.ShapeDtypeStruct((N_CHIP,) + s.shape, s.dtype) for s in shards),
               in_specs=[ANY] * n, out_specs=tuple([ANY] * n),
               scratch=[pltpu.SemaphoreType.DMA((3 * n,)), pltpu.SemaphoreType.DMA((3 * n,)),
                        pltpu.SemaphoreType.DMA((n,))])(*shards)


def _rs_sibling_exchange(grads):
    n = len(grads)

    def body(*refs):
        ins, outs = refs[:n], refs[n:2 * n]
        send_sems, recv_sems = refs[2 * n:]
        x, y, c = lax.axis_index("x"), lax.axis_index("y"), lax.axis_index("c")
        sends = []
        for ti in range(n):
            half = ins[ti].shape[1] // 2
            cp = pltpu.make_async_remote_copy(
                src_ref=ins[ti].at[:, pl.ds(pl.multiple_of((1 - c) * half, 8), half), :],
                dst_ref=outs[ti],
                send_sem=send_sems.at[ti], recv_sem=recv_sems.at[ti],
                device_id=(x, y, 1 - c), device_id_type=MESH)
            cp.start()
            sends.append(cp)
        for cp in sends:
            cp.wait()

    return _pc(body, name="rs_sibling_exchange",
               out_shape=tuple(jax.ShapeDtypeStruct((N_CHIP, g.shape[1] // 2, g.shape[2]), g.dtype)
                               for g in grads),
               in_specs=[ANY] * n, out_specs=tuple([ANY] * n),
               scratch=[pltpu.SemaphoreType.DMA((n,)), pltpu.SemaphoreType.DMA((n,))])(*grads)


def _rs_add_halves(name, grad, recv, core):
    _, r, cdim = grad.shape
    half = r // 2
    tr = _row_tile(half, cdim)
    nr = half // tr

    def body(c_ref, g_ref, r_ref, o_ref):
        o_ref[...] = g_ref[...] + r_ref[...]

    return _pc(body, name=name, out_shape=jax.ShapeDtypeStruct((N_CHIP, half, cdim), F32),
               grid=(N_CHIP, nr), nsp=1,
               in_specs=[pl.BlockSpec((None, tr, cdim), lambda j, i, c_ref: (j, c_ref[0] * nr + i, 0)),
                         pl.BlockSpec((None, tr, cdim), lambda j, i, c_ref: (j, i, 0))],
               out_specs=pl.BlockSpec((None, tr, cdim), lambda j, i, c_ref: (j, i, 0)),
               sem=("parallel", "parallel"))(core, grad, recv)


def _rs_chip_exchange(parts):
    n = len(parts)

    def body(*refs):
        ins, outs = refs[:n], refs[n:2 * n]
        send_sems, recv_sems, local_sems = refs[2 * n:]
        x, y, c, peers = _chip_peers()
        me = 2 * x + y
        locals_, sends = [], []
        for ti in range(n):
            lc = pltpu.make_async_copy(ins[ti].at[me], outs[ti].at[me], local_sems.at[ti])
            lc.start()
            locals_.append(lc)
            for k, (px, py) in enumerate(peers):
                cp = pltpu.make_async_remote_copy(
                    src_ref=ins[ti].at[2 * px + py], dst_ref=outs[ti].at[me],
                    send_sem=send_sems.at[ti * 3 + k], recv_sem=recv_sems.at[ti * 3 + k],
                    device_id=(px, py, c), device_id_type=MESH)
                cp.start()
                sends.append(cp)
        for ti in range(n):
            for k, (px, py) in enumerate(peers):
                pltpu.make_async_remote_copy(
                    src_ref=ins[ti].at[me], dst_ref=outs[ti].at[2 * px + py],
                    send_sem=send_sems.at[ti * 3 + k], recv_sem=recv_sems.at[ti * 3 + k],
                    device_id=(px, py, c), device_id_type=MESH).wait_recv()
        for cp in sends:
            cp.wait_send()
        for lc in locals_:
            lc.wait()

    return _pc(body, name="rs_chip_exchange",
               out_shape=tuple(jax.ShapeDtypeStruct(p.shape, p.dtype) for p in parts),
               in_specs=[ANY] * n, out_specs=tuple([ANY] * n),
               scratch=[pltpu.SemaphoreType.DMA((3 * n,)), pltpu.SemaphoreType.DMA((3 * n,)),
                        pltpu.SemaphoreType.DMA((n,))])(*parts)


def _rs_sum_chips(name, recv):
    _, half, cdim = recv.shape
    tr = _row_tile(half, cdim)

    def body(r_ref, o_ref):
        o_ref[...] = ((r_ref[0] + r_ref[1]) + r_ref[2]) + r_ref[3]

    return _pc(body, name=name, out_shape=jax.ShapeDtypeStruct((half, cdim), F32),
               grid=(half // tr,),
               in_specs=[pl.BlockSpec((N_CHIP, tr, cdim), lambda i: (0, i, 0))],
               out_specs=pl.BlockSpec((tr, cdim), lambda i: (i, 0)),
               sem=("parallel",))(recv)


def _rs_join_halves(halves):
    n = len(halves)

    def body(*refs):
        ins, outs = refs[:n], refs[n:2 * n]
        send_sems, recv_sems, local_sems = refs[2 * n:]
        x, y, c = lax.axis_index("x"), lax.axis_index("y"), lax.axis_index("c")
        locals_, sends = [], []
        for ti in range(n):
            half = ins[ti].shape[0]
            mine = outs[ti].at[pl.ds(pl.multiple_of(c * half, 8), half), :]
            lc = pltpu.make_async_copy(ins[ti], mine, local_sems.at[ti])
            lc.start()
            locals_.append(lc)
            cp = pltpu.make_async_remote_copy(
                src_ref=ins[ti], dst_ref=mine,
                send_sem=send_sems.at[ti], recv_sem=recv_sems.at[ti],
                device_id=(x, y, 1 - c), device_id_type=MESH)
            cp.start()
            sends.append(cp)
        for ti in range(n):
            half = ins[ti].shape[0]
            theirs = outs[ti].at[pl.ds(pl.multiple_of((1 - c) * half, 8), half), :]
            pltpu.make_async_remote_copy(
                src_ref=ins[ti], dst_ref=theirs,
                send_sem=send_sems.at[ti], recv_sem=recv_sems.at[ti],
                device_id=(x, y, 1 - c), device_id_type=MESH).wait_recv()
        for cp in sends:
            cp.wait_send()
        for lc in locals_:
            lc.wait()

    return _pc(body, name="rs_join_halves",
               out_shape=tuple(jax.ShapeDtypeStruct((2 * hv.shape[0], hv.shape[1]), hv.dtype)
                               for hv in halves),
               in_specs=[ANY] * n, out_specs=tuple([ANY] * n),
               scratch=[pltpu.SemaphoreType.DMA((n,)), pltpu.SemaphoreType.DMA((n,)),
                        pltpu.SemaphoreType.DMA((n,))])(*halves)


def _adamw_rows(name, g, w, m, v):
    r, cdim = g.shape
    tr = _row_tile(r, cdim, budget=1 << 19)

    def body(g_ref, w_ref, m_ref, v_ref, d_ref, m2_ref, v2_ref):
        d, m2, v2 = _adamw(w_ref[...], g_ref[...], m_ref[...], v_ref[...])
        d_ref[...] = d
        m2_ref[...] = m2
        v2_ref[...] = v2

    tile = pl.BlockSpec((tr, cdim), lambda i: (i, 0))
    shp = jax.ShapeDtypeStruct((r, cdim), F32)
    return _pc(body, name=name, out_shape=(shp, shp, shp), grid=(r // tr,),
               in_specs=[tile] * 4, out_specs=(tile, tile, tile),
               sem=("parallel",))(g, w, m, v)


def _adamw_whole(name, g, w, m, v):
    def body(g_ref, w_ref, m_ref, v_ref, d_ref, m2_ref, v2_ref):
        d, m2, v2 = _adamw(w_ref[...], g_ref[...], m_ref[...], v_ref[...])
        d_ref[...] = d
        m2_ref[...] = m2
        v2_ref[...] = v2

    shp = jax.ShapeDtypeStruct(g.shape, F32)
    return _pc(body, name=name, out_shape=(shp, shp, shp))(g, w, m, v)


SMALL_LAYOUT = (
    ("sgu_w_s", 1024, 1, 0),
    ("sgu_b_s", 8, 1, 1024),
    ("sgu_norm_g", 1, 0, 0),
    ("sgu_norm_b", 1, 0, 8),
    ("hgrn_norm_g", 1, 0, 24),
    ("ln1_g", 1, 0, 32),
    ("ln1_b", 1, 0, 40),
    ("ffn_conv_b", 1, 2, 8),
    ("ln2_g", 1, 0, 48),
    ("ln2_b", 1, 0, 56),
)
LB_ROW = 16
LOSS_ROW = 64


def _small_allreduce_adamw(bufs, logits, m_logits, v_logits, small_w, small_m, small_v):
    ns = len(SMALL_LAYOUT)
    nb = len(bufs)

    def body(*refs):
        buf_refs = refs[:nb]
        lg_ref, mlg_ref, vlg_ref = refs[nb:nb + 3]
        pos = nb + 3
        w_refs = refs[pos:pos + ns]
        m_refs = refs[pos + ns:pos + 2 * ns]
        v_refs = refs[pos + 2 * ns:pos + 3 * ns]
        pos += 3 * ns
        loss_ref, dcw_ref = refs[pos:pos + 2]
        lg_outs = refs[pos + 2:pos + 6]
        pos += 6
        outs = refs[pos:pos + 4 * ns]
        pos += 4 * ns
        gath = refs[pos:pos + nb]
        send_sems, recv_sems = refs[pos + nb:pos + nb + 2]

        x, y, c = lax.axis_index("x"), lax.axis_index("y"), lax.axis_index("c")
        me = 4 * x + 2 * y + c
        sends = []
        for d in range(1, N_DEV):
            peer = (x ^ (d >> 2), y ^ ((d >> 1) & 1), c ^ (d & 1))
            for b in range(nb):
                cp = pltpu.make_async_remote_copy(
                    src_ref=buf_refs[b], dst_ref=gath[b].at[me],
                    send_sem=send_sems.at[(d - 1) * nb + b], recv_sem=recv_sems.at[(d - 1) * nb + b],
                    device_id=peer, device_id_type=MESH)
                cp.start()
                sends.append(cp)
        for b in range(nb):
            gath[b][me] = buf_refs[b][...]
        for d in range(1, N_DEV):
            peer = (x ^ (d >> 2), y ^ ((d >> 1) & 1), c ^ (d & 1))
            src = 4 * peer[0] + 2 * peer[1] + peer[2]
            for b in range(nb):
                pltpu.make_async_remote_copy(
                    src_ref=buf_refs[b], dst_ref=gath[b].at[src],
                    send_sem=send_sems.at[(d - 1) * nb + b], recv_sem=recv_sems.at[(d - 1) * nb + b],
                    device_id=peer, device_id_type=MESH).wait_recv()
        for cp in sends:
            cp.wait_send()

        tot = []
        for b in range(nb):
            s = gath[b][0]
            for d in range(1, N_DEV):
                s = s + gath[b][d]
            tot.append(s)

        loss_ref[...] = tot[0][LOSS_ROW:LOSS_ROW + 8, :]
        dcw_ref[...] = tot[2][0:8, :]
        lb = _sig(lg_ref[0:1, :] - lg_ref[1:2, :])
        d0 = tot[0][LB_ROW:LB_ROW + 1, :] * lb * (1.0 - lb)
        rowid = lax.broadcasted_iota(jnp.int32, (2, D_MODEL), 0)
        g_lg = jnp.where(rowid == 0, d0, -d0)
        dl, ml, vl = _adamw(lg_ref[...], g_lg, mlg_ref[...], vlg_ref[...])
        lg_outs[0][...] = g_lg
        lg_outs[1][...] = dl
        lg_outs[2][...] = ml
        lg_outs[3][...] = vl
        for si, (_, rows, b, r0) in enumerate(SMALL_LAYOUT):
            g = tot[b][r0:r0 + rows, :]
            dl, ml, vl = _adamw(w_refs[si][...], g, m_refs[si][...], v_refs[si][...])
            outs[4 * si][...] = g
            outs[4 * si + 1][...] = dl
            outs[4 * si + 2][...] = ml
            outs[4 * si + 3][...] = vl

    shapes = [jax.ShapeDtypeStruct((8, D_MODEL), F32), jax.ShapeDtypeStruct((8, D_FF), F32)]
    shapes += [jax.ShapeDtypeStruct((2, D_MODEL), F32)] * 4
    for w in small_w:
        shapes += [jax.ShapeDtypeStruct(w.shape, F32)] * 4
    scratch = [pltpu.VMEM((N_DEV,) + b.shape, F32) for b in bufs]
    scratch += [pltpu.SemaphoreType.DMA(((N_DEV - 1) * nb,)), pltpu.SemaphoreType.DMA(((N_DEV - 1) * nb,))]
    vm = pl.BlockSpec(memory_space=pltpu.VMEM)
    n_in = nb + 3 + 3 * ns
    res = _pc(body, name="small_allreduce_adamw", out_shape=tuple(shapes),
              in_specs=[vm] * n_in, out_specs=tuple([vm] * len(shapes)),
              scratch=scratch)(*bufs, logits, m_logits, v_logits, *small_w, *small_m, *small_v)
    return res[0], res[1], res[2:6], res[6:]


def kernel(x, p, w_in, sgu_w_s, sgu_b_s, sgu_norm_g, sgu_norm_b, hgrn_lb_logits, hgrn_norm_g, w_branch, w_out, ln1_g, ln1_b, ffn_w_up, ffn_conv_w, ffn_conv_b, ffn_w_down, ln2_g, ln2_b, ple_w_proj, ple_w_gate, loss_target, m_w_in, m_sgu_w_s, m_sgu_b_s, m_sgu_norm_g, m_sgu_norm_b, m_hgrn_lb_logits, m_hgrn_norm_g, m_w_branch, m_w_out, m_ln1_g, m_ln1_b, m_ffn_w_up, m_ffn_conv_w, m_ffn_conv_b, m_ffn_w_down, m_ln2_g, m_ln2_b, m_ple_w_proj, m_ple_w_gate, v_w_in, v_sgu_w_s, v_sgu_b_s, v_sgu_norm_g, v_sgu_norm_b, v_hgrn_lb_logits, v_hgrn_norm_g, v_w_branch, v_w_out, v_ln1_g, v_ln1_b, v_ffn_w_up, v_ffn_conv_w, v_ffn_conv_b, v_ffn_w_down, v_ln2_g, v_ln2_b, v_ple_w_proj, v_ple_w_gate):
    t = x.shape[1]
    x2 = x.reshape(t, D_MODEL)
    p2 = p.reshape(t, PLE_DIM)
    tgt = loss_target.reshape(t, D_MODEL)
    core = lax.axis_index("c").astype(jnp.int32).reshape(1)

    big_w = [w_in[0], w_branch[0, 0], w_branch[0, 1], w_out[0], ffn_w_up[0], ffn_w_down[0],
             ple_w_proj[0], ple_w_gate[0]]
    big_m = [m_w_in[0], m_w_branch[0, 0], m_w_branch[0, 1], m_w_out[0], m_ffn_w_up[0],
             m_ffn_w_down[0], m_ple_w_proj[0], m_ple_w_gate[0]]
    big_v = [v_w_in[0], v_w_branch[0, 0], v_w_branch[0, 1], v_w_out[0], v_ffn_w_up[0],
             v_ffn_w_down[0], v_ple_w_proj[0], v_ple_w_gate[0]]
    gathered = _gather_weights([w.astype(BF16) for w in big_w] + [ffn_conv_w[0]])
    win_st = gathered[0]
    wb0 = gathered[1].reshape(D_MODEL, D_MODEL)
    wb1 = gathered[2].reshape(D_MODEL, D_MODEL)
    wo = gathered[3].reshape(D_MODEL, D_MODEL)
    wup_st = gathered[4]
    wd = gathered[5].reshape(D_FF, D_MODEL)
    wpp = jnp.transpose(gathered[6], (1, 0, 2)).reshape(PLE_DIM, D_MODEL)
    wpg = gathered[7].reshape(D_MODEL, D_MODEL)
    convw = jnp.transpose(gathered[8], (1, 0, 2)).reshape(3, D_FF)

    cid = jnp.arange(SGU_BLOCK) // CHUNK
    maskf = (cid[:, None] >= cid[None, :]).astype(F32)
    ws_masked = sgu_w_s[0] * maskf[None]
    wm = ws_masked.astype(BF16)
    wmt = jnp.transpose(ws_masked, (0, 2, 1)).astype(BF16)
    bsb = jnp.broadcast_to(sgu_b_s[0][:, :, None], (N_GROUP, SGU_BLOCK, 128))

    h = _mm_nn_stacked("in_proj", x2, win_st, 512)
    ya = _sgu_fwd(h, wm, bsb, sgu_norm_g, sgu_norm_b)
    yb, st_all = _hgrn_fwd(h, hgrn_lb_logits, hgrn_norm_g)
    r1, a_br, b_br, m_bf, x1b = _mix_fwd(ya, yb, h, x2, wb0, wb1, wo, ln1_g, ln1_b, 256)
    h2 = _mm_nn_stacked("ffn_up", x1b, wup_st, 512)
    act = _ffn_act_fwd(h2, convw, ffn_conv_b, 256)
    dr2, dpg, dpp, loss_acc, dg2, db2 = _out_fwd_bwd(
        act, x1b, r1, p2, tgt, wd, wpg, wpp, ln1_g, ln1_b, ln2_g, ln2_b, 256)

    dact = _mm("ffn_down_bwd", dr2, wd, NT, (t // 512, FF_NJ, 1),
               pl.BlockSpec((512, D_MODEL), lambda i, j, k: (i, 0)),
               pl.BlockSpec((FF_TILE, D_MODEL), lambda i, j, k: (j, 0)),
               jax.ShapeDtypeStruct((t, D_FF), F32),
               pl.BlockSpec((512, FF_TILE), lambda i, j, k: (i, j)))
    dh2, dcw, dcb = _ffn_act_bwd(h2, dact, convw, ffn_conv_b, 256)
    d_wd = _mm_tn("ffn_down_wgrad", act, dr2, FF_TILE, D_MODEL, 512)
    d_wpg = _mm_tn("ple_gate_wgrad", x1b, dpg, D_MODEL, D_MODEL, 512)
    d_wpp_st = _mm_tn("ple_proj_wgrad", p2, dpp, PLE_DIM, PLE_DIM, 512, stacked=True)
    d_wup_st = _mm("ffn_up_wgrad", x1b, dh2, TN, (1, N_CHIP, t // 512),
                   pl.BlockSpec((512, D_MODEL), lambda i, j, k: (k, 0)),
                   pl.BlockSpec((None, 512, FF_TILE), lambda i, j, k: (j // FF_NJ, k, j % FF_NJ)),
                   jax.ShapeDtypeStruct((N_CHIP, D_MODEL, FF_TILE), F32),
                   pl.BlockSpec((None, D_MODEL, FF_TILE), lambda i, j, k: (j, 0, 0)))
    dr1, dg1, db1 = _ffn_in_bwd(dh2, wup_st, dpg, wpg, dr2, r1, ln1_g, 512)
    da_bf, db_bf, dh3, dya, dyb = _mix_bwd(dr1, h, a_br, b_br, wo, wb0, wb1, 256)
    d_wo = _mm_tn("out_proj_wgrad", m_bf, dr1, D_MODEL, D_MODEL, 512)
    d_wb0 = _mm_tn("branch0_wgrad", ya, da_bf, D_MODEL, D_MODEL, 512)
    d_wb1 = _mm_tn("branch1_wgrad", yb, db_bf, D_MODEL, D_MODEL, 512)
    dh1, dh2h, dlb, dgn = _hgrn_bwd(h, dyb, st_all, hgrn_lb_logits, hgrn_norm_g)
    dh0, dws, dbs, dgv, dbv = _sgu_bwd(h, dya, wm, wmt, bsb, sgu_norm_g, sgu_norm_b, maskf)
    dh_parts = [dh0, dh1, dh2h, dh3]
    d_win = [_mm_tn("in_proj_wgrad%d" % j, x2, dh_parts[j], D_MODEL, 1024, 512) for j in range(4)]
    gx = dr1
    scale = ALPHA
    for j in range(4):
        gx = _mm("in_proj_xgrad%d" % j, dh_parts[j], win_st, NT, (t // 512, 1, 1),
                 pl.BlockSpec((512, 2 * D_MODEL), lambda i, jj, k: (i, 0)),
                 pl.BlockSpec((None, D_MODEL, 2 * D_MODEL), lambda i, jj, k, j=j: (j, 0, 0)),
                 jax.ShapeDtypeStruct((t, D_MODEL), F32),
                 pl.BlockSpec((512, D_MODEL), lambda i, jj, k: (i, 0)),
                 add=gx, add_spec=pl.BlockSpec((512, D_MODEL), lambda i, jj, k: (i, 0)),
                 add_scale=scale)
        scale = 1.0

    grads_st = [jnp.stack(d_win), d_wb0.reshape(4, 256, D_MODEL), d_wb1.reshape(4, 256, D_MODEL),
                d_wo.reshape(4, 256, D_MODEL), d_wup_st, d_wd.reshape(4, D_FF // 4, D_MODEL),
                d_wpp_st, d_wpg.reshape(4, 256, D_MODEL)]
    recv_a = _rs_sibling_exchange(grads_st)
    parts = [_rs_add_halves("rs_add_halves%d" % i, g, r, core)
             for i, (g, r) in enumerate(zip(grads_st, recv_a))]
    recv_b = _rs_chip_exchange(parts)
    halves = [_rs_sum_chips("rs_sum_chips%d" % i, r) for i, r in enumerate(recv_b)]
    full = _rs_join_halves(halves)
    big_out = [_adamw_rows("adamw_big%d" % i, full[i], big_w[i], big_m[i], big_v[i])
               for i in range(len(full))]

    buf0 = jnp.concatenate([dgv, dbv, dlb, dgn, dg1, db1, dg2, db2, loss_acc], axis=0)
    buf1 = jnp.concatenate([dws.reshape(N_GROUP * 128, 128), dbs], axis=0)
    buf2 = jnp.concatenate([dcw, dcb], axis=0)
    small_in = dict(sgu_w_s=(sgu_w_s, m_sgu_w_s, v_sgu_w_s), sgu_b_s=(sgu_b_s, m_sgu_b_s, v_sgu_b_s),
                    sgu_norm_g=(sgu_norm_g, m_sgu_norm_g, v_sgu_norm_g),
                    sgu_norm_b=(sgu_norm_b, m_sgu_norm_b, v_sgu_norm_b),
                    hgrn_norm_g=(hgrn_norm_g, m_hgrn_norm_g, v_hgrn_norm_g),
                    ln1_g=(ln1_g, m_ln1_g, v_ln1_g), ln1_b=(ln1_b, m_ln1_b, v_ln1_b),
                    ffn_conv_b=(ffn_conv_b, m_ffn_conv_b, v_ffn_conv_b),
                    ln2_g=(ln2_g, m_ln2_g, v_ln2_g), ln2_b=(ln2_b, m_ln2_b, v_ln2_b))

    def flat(name, arr):
        rows = dict((n, r) for n, r, _, _ in SMALL_LAYOUT)[name]
        return arr.reshape(rows, arr.size // rows)

    names = [n for n, _, _, _ in SMALL_LAYOUT]
    sw = [flat(n, small_in[n][0]) for n in names]
    sm = [flat(n, small_in[n][1]) for n in names]
    sv = [flat(n, small_in[n][2]) for n in names]
    loss_rows, dcw_tot, lg_out, small_out = _small_allreduce_adamw(
        [buf0, buf1, buf2], hgrn_lb_logits, m_hgrn_lb_logits, v_hgrn_lb_logits, sw, sm, sv)
    loss = loss_rows[0, 0]

    chip = 2 * lax.axis_index("x") + lax.axis_index("y")
    g_cw = lax.dynamic_slice(dcw_tot, (0, chip * (D_FF // 4)), (3, D_FF // 4))
    cw_out = _adamw_whole("adamw_conv_w", g_cw, ffn_conv_w[0], m_ffn_conv_w[0], v_ffn_conv_w[0])

    res = {}
    for si, n in enumerate(names):
        shp = small_in[n][0].shape
        res[n] = tuple(small_out[4 * si + k].reshape(shp) for k in range(4))
    res["hgrn_lb_logits"] = tuple(lg_out)
    res["ffn_conv_w"] = (g_cw[None],) + tuple(o[None] for o in cw_out)

    def big(i):
        return (full[i],) + tuple(big_out[i])

    res["w_in"] = tuple(o[None] for o in big(0))
    res["w_branch"] = tuple(jnp.stack([o0, o1])[None] for o0, o1 in zip(big(1), big(2)))
    res["w_out"] = tuple(o[None] for o in big(3))
    res["ffn_w_up"] = tuple(o[None] for o in big(4))
    res["ffn_w_down"] = tuple(o[None] for o in big(5))
    res["ple_w_proj"] = tuple(o[None] for o in big(6))
    res["ple_w_gate"] = tuple(o[None] for o in big(7))

    order = ["w_in", "sgu_w_s", "sgu_b_s", "sgu_norm_g", "sgu_norm_b", "hgrn_lb_logits",
             "hgrn_norm_g", "w_branch", "w_out", "ln1_g", "ln1_b", "ffn_w_up", "ffn_conv_w",
             "ffn_conv_b", "ffn_w_down", "ln2_g", "ln2_b", "ple_w_proj", "ple_w_gate"]
    outs = [loss, gx.reshape(1, t, D_MODEL)]
    for k in range(4):
        outs += [res[n][k] for n in order]
    return tuple(outs)
```

```python
import functools

import jax
import jax.numpy as jnp
from jax import lax
from jax.experimental import pallas as pl
from jax.experimental.pallas import tpu as pltpu

F32 = jnp.float32
BF16 = jnp.bfloat16
HIGHEST = lax.Precision.HIGHEST
MESH = pl.DeviceIdType.MESH

D_MODEL = 1024
CHUNK = 64
SGU_BLOCK = 128
N_GROUP = 8
N_HEAD = 8
HEAD_DIM = 128
D_FF = 2816
PLE_DIM = 256
IN_COLS = 8192
LN_EPS = 1e-5
RMS_EPS = 1e-6
ALPHA = 2.0 ** 0.25
N_CHIP = 4
N_DEV = 8

ADAM_LR = 0.001
ADAM_B1 = 0.9
ADAM_B2 = 0.999
ADAM_EPS = 1e-08
ADAM_WD = 0.01
ADAM_STEP = 10

VMEM_LIMIT = 56 * 1024 * 1024

NN = (((1,), (0,)), ((), ()))
NT = (((1,), (1,)), ((), ()))
TN = (((0,), (0,)), ((), ()))


def _pc(body, *, name, out_shape, grid=None, in_specs=None, out_specs=None, scratch=(),
        sem=None, nsp=0, vmem=VMEM_LIMIT):
    params = dict(vmem_limit_bytes=vmem)
    if sem is not None:
        params["dimension_semantics"] = sem
    kw = dict(name=name, out_shape=out_shape, compiler_params=pltpu.CompilerParams(**params))
    if nsp:
        kw["grid_spec"] = pltpu.PrefetchScalarGridSpec(
            num_scalar_prefetch=nsp, grid=grid, in_specs=in_specs, out_specs=out_specs,
            scratch_shapes=list(scratch))
    else:
        if grid is not None:
            kw["grid"] = grid
        if in_specs is not None:
            kw["in_specs"] = in_specs
            kw["out_specs"] = out_specs
        kw["scratch_shapes"] = list(scratch)
    return pl.pallas_call(body, **kw)


def _dot(a, b, dims=NN):
    return lax.dot_general(a.astype(BF16), b.astype(BF16), dims, preferred_element_type=F32)


def _dot32(a, b, dims=NN):
    return lax.dot_general(a, b, dims, precision=HIGHEST, preferred_element_type=F32)


def _sig(x):
    return 1.0 / (1.0 + jnp.exp(-x))


_GC = 0.7978845608028654
_GA = 0.044715


def _gelu(x):
    return 0.5 * x * (1.0 + jnp.tanh(_GC * (x + _GA * x * x * x)))


def _gelu_and_grad(x):
    t = jnp.tanh(_GC * (x + _GA * x * x * x))
    g = 0.5 * x * (1.0 + t)
    dg = 0.5 * (1.0 + t) + 0.5 * x * (1.0 - t * t) * _GC * (1.0 + 3.0 * _GA * x * x)
    return g, dg


def _ln_stats(r):
    mu = jnp.mean(r, axis=-1, keepdims=True)
    xc = r - mu
    var = jnp.mean(xc * xc, axis=-1, keepdims=True)
    rstd = lax.rsqrt(var + LN_EPS)
    return xc * rstd, rstd


def _ln_bwd(dxh, xh, rstd):
    m1 = jnp.mean(dxh, axis=-1, keepdims=True)
    m2 = jnp.mean(dxh * xh, axis=-1, keepdims=True)
    return rstd * (dxh - m1 - xh * m2)


def _colsum8(v):
    return jnp.broadcast_to(jnp.sum(v, axis=0, keepdims=True), (8, v.shape[1]))


def _adamw(w, g, m, v):
    m2 = ADAM_B1 * m + (1.0 - ADAM_B1) * g
    v2 = ADAM_B2 * v + (1.0 - ADAM_B2) * (g * g)
    m_hat = m2 / (1.0 - ADAM_B1 ** ADAM_STEP)
    v_hat = v2 / (1.0 - ADAM_B2 ** ADAM_STEP)
    delta = -ADAM_LR * (m_hat / (jnp.sqrt(v_hat) + ADAM_EPS) + ADAM_WD * w)
    return delta, m2, v2


def _row_tile(rows, cols, itemsize=4, budget=1 << 20, mult=8):
    best = mult
    for tr in range(mult, rows + 1, mult):
        if rows % tr == 0 and tr * cols * itemsize <= budget:
            best = tr
    return best


def _mm(name, a, b, dims, grid, a_spec, b_spec, out_shape, o_spec, add=None, add_spec=None,
        add_scale=1.0):
    nk = grid[2]
    has_add = add is not None
    out_dtype = out_shape.dtype

    def body(*refs):
        if has_add:
            a_ref, b_ref, add_ref, o_ref = refs[:4]
            rest = refs[4:]
        else:
            a_ref, b_ref, o_ref = refs[:3]
            add_ref = None
            rest = refs[3:]
        prod = _dot(a_ref[...], b_ref[...], dims)

        def finish(acc):
            if has_add:
                acc = acc + add_scale * add_ref[...]
            o_ref[...] = acc.astype(out_dtype)

        if nk == 1:
            finish(prod)
        else:
            acc_ref = rest[0]
            k = pl.program_id(2)

            @pl.when(k == 0)
            def _():
                acc_ref[...] = prod

            @pl.when(k > 0)
            def _():
                acc_ref[...] += prod

            @pl.when(k == nk - 1)
            def _():
                finish(acc_ref[...])

    in_specs = [a_spec, b_spec] + ([add_spec] if has_add else [])
    args = [a, b] + ([add] if has_add else [])
    scratch = []
    if nk > 1:
        blk = [d for d in o_spec.block_shape if d is not None]
        scratch = [pltpu.VMEM(tuple(blk), F32)]
    return _pc(body, name=name, out_shape=out_shape, grid=grid, in_specs=in_specs,
               out_specs=o_spec, scratch=scratch,
               sem=("parallel", "parallel", "arbitrary"))(*args)


def _mm_nn_stacked(name, a, w_st, tm):
    t, k = a.shape
    _, _, c = w_st.shape
    return _mm(name, a, w_st, NN, (t // tm, N_CHIP, 1),
               pl.BlockSpec((tm, k), lambda i, j, kk: (i, 0)),
               pl.BlockSpec((None, k, c), lambda i, j, kk: (j, 0, 0)),
               jax.ShapeDtypeStruct((t, N_CHIP * c), F32),
               pl.BlockSpec((tm, c), lambda i, j, kk: (i, j)))


def _mm_tn(name, a, b, tm, tn, tk, stacked=False):
    t, m = a.shape
    _, n = b.shape
    if stacked:
        assert tm == m
        out_shape = jax.ShapeDtypeStruct((n // tn, m, tn), F32)
        o_spec = pl.BlockSpec((None, tm, tn), lambda i, j, kk: (j, 0, 0))
    else:
        out_shape = jax.ShapeDtypeStruct((m, n), F32)
        o_spec = pl.BlockSpec((tm, tn), lambda i, j, kk: (i, j))
    return _mm(name, a, b, TN, (m // tm, n // tn, t // tk),
               pl.BlockSpec((tk, tm), lambda i, j, kk: (kk, i)),
               pl.BlockSpec((tk, tn), lambda i, j, kk: (kk, j)),
               out_shape, o_spec)


def _sgu_mixed(v, wm_ref, bsb_ref, gv, bv):
    gl, dgl = _gelu_and_grad(v)
    vh, rstd = _ln_stats(gl)
    vn = vh * gv + bv
    mixed = []
    for g in range(N_GROUP):
        sl = slice(g * 128, (g + 1) * 128)
        mixed.append(_dot(wm_ref[g], vn[:, sl]) + bsb_ref[g])
    return dgl, vh, rstd, vn, mixed


def _sgu_fwd(h, wm, bsb, gv, bv):
    t = h.shape[0]

    def body(u_ref, v_ref, wm_ref, bsb_ref, gv_ref, bv_ref, ya_ref):
        u = u_ref[...]
        _, _, _, _, mixed = _sgu_mixed(v_ref[...], wm_ref, bsb_ref, gv_ref[...], bv_ref[...])
        gu = _gelu(u)
        for g in range(N_GROUP):
            sl = slice(g * 128, (g + 1) * 128)
            ya_ref[:, sl] = (gu[:, sl] * mixed[g]).astype(BF16)

    full3 = pl.BlockSpec((N_GROUP, 128, 128), lambda i: (0, 0, 0))
    vec = pl.BlockSpec((1, D_MODEL), lambda i: (0, 0))
    return _pc(body, name="sgu_fwd", out_shape=jax.ShapeDtypeStruct((t, D_MODEL), BF16),
               grid=(t // SGU_BLOCK,),
               in_specs=[pl.BlockSpec((SGU_BLOCK, D_MODEL), lambda i: (i, 0)),
                         pl.BlockSpec((SGU_BLOCK, D_MODEL), lambda i: (i, 1)),
                         full3, full3, vec, vec],
               out_specs=pl.BlockSpec((SGU_BLOCK, D_MODEL), lambda i: (i, 0)),
               sem=("parallel",))(h, h, wm, bsb, gv, bv)


def _sgu_bwd(h, dya, wm, wmt, bsb, gv, bv, maskf):
    t = h.shape[0]
    nb = t // SGU_BLOCK

    def body(u_ref, v_ref, dya_ref, wm_ref, wmt_ref, bsb_ref, gv_ref, bv_ref, mask_ref,
             dh_ref, dws_ref, dbs_ref, dgv_ref, dbv_ref, dmix_acc):
        i = pl.program_id(0)

        @pl.when(i == 0)
        def _():
            dws_ref[...] = jnp.zeros_like(dws_ref)
            dgv_ref[...] = jnp.zeros_like(dgv_ref)
            dbv_ref[...] = jnp.zeros_like(dbv_ref)
            dmix_acc[...] = jnp.zeros_like(dmix_acc)

        u = u_ref[...]
        gvv = gv_ref[...]
        dgl_v, vh, rstd, vn, mixed = _sgu_mixed(v_ref[...], wm_ref, bsb_ref, gvv, bv_ref[...])
        gu, dgl_u = _gelu_and_grad(u)
        dya_v = dya_ref[...]
        dvn_parts = []
        for g in range(N_GROUP):
            sl = slice(g * 128, (g + 1) * 128)
            d_y = dya_v[:, sl]
            dh_ref[:, sl] = (d_y * mixed[g] * dgl_u[:, sl]).astype(BF16)
            d_mixed = d_y * gu[:, sl]
            dmix_acc[g] += d_mixed
            dws_ref[g] += _dot(d_mixed, vn[:, sl], NT) * mask_ref[...]
            dvn_parts.append(_dot(wmt_ref[g], d_mixed))
        dvn = jnp.concatenate(dvn_parts, axis=1)
        dgv_ref[...] += _colsum8(dvn * vh)
        dbv_ref[...] += _colsum8(dvn)
        d_gl = _ln_bwd(dvn * gvv, vh, rstd)
        dh_ref[:, D_MODEL:] = (d_gl * dgl_v).astype(BF16)

        @pl.when(i == nb - 1)
        def _():
            rowid = lax.broadcasted_iota(jnp.int32, (8, 128), 0)
            ones = jnp.ones((8, 128), F32)
            acc = jnp.zeros((8, 128), F32)
            for g in range(N_GROUP):
                rs = _dot32(ones, dmix_acc[g], NT)
                acc = jnp.where(rowid == g, rs, acc)
            dbs_ref[...] = acc

    full3 = pl.BlockSpec((N_GROUP, 128, 128), lambda i: (0, 0, 0))
    vec = pl.BlockSpec((1, D_MODEL), lambda i: (0, 0))
    acc8 = pl.BlockSpec((8, D_MODEL), lambda i: (0, 0))
    return _pc(body, name="sgu_bwd",
               out_shape=(jax.ShapeDtypeStruct((t, 2 * D_MODEL), BF16),
                          jax.ShapeDtypeStruct((N_GROUP, 128, 128), F32),
                          jax.ShapeDtypeStruct((8, 128), F32),
                          jax.ShapeDtypeStruct((8, D_MODEL), F32),
                          jax.ShapeDtypeStruct((8, D_MODEL), F32)),
               grid=(nb,),
               in_specs=[pl.BlockSpec((SGU_BLOCK, D_MODEL), lambda i: (i, 0)),
                         pl.BlockSpec((SGU_BLOCK, D_MODEL), lambda i: (i, 1)),
                         pl.BlockSpec((SGU_BLOCK, D_MODEL), lambda i: (i, 0)),
                         full3, full3, full3, vec, vec,
                         pl.BlockSpec((128, 128), lambda i: (0, 0))],
               out_specs=(pl.BlockSpec((SGU_BLOCK, 2 * D_MODEL), lambda i: (i, 0)),
                          full3, pl.BlockSpec((8, 128), lambda i: (0, 0)), acc8, acc8),
               scratch=[pltpu.VMEM((N_GROUP, 128, 128), F32)],
               sem=("arbitrary",))(h, h, dya, wm, wmt, bsb, gv, bv, maskf)


def _tri_masks():
    row = lax.broadcasted_iota(jnp.int32, (CHUNK, CHUNK), 0)
    col = lax.broadcasted_iota(jnp.int32, (CHUNK, CHUNK), 1)
    return col <= row, col >= row


def _lower_bound(logit_ref, sl):
    return _sig(logit_ref[0:1, sl] - logit_ref[1:2, sl])


def _hgrn_chunk(q, fp, ii, lb, st, causal):
    sg = _sig(fp)
    f = lb + (1.0 - lb) * sg
    k = 1.0 - f
    c = _dot32(causal.astype(F32), jnp.log(f))
    ec = jnp.exp(c)
    en = jnp.exp(-c)
    sq = _sig(q)
    qt = q * sq * ec
    kt = k * en
    attn = jnp.where(causal, _dot(qt, kt, NT), 0.0)
    o = _dot(attn, ii) + _dot(qt, st, NT)
    ecl = jnp.exp(c[CHUNK - 1:CHUNK, :])
    kk = kt * ecl
    st_new = st * ecl + _dot(ii, kk, TN)
    return dict(sg=sg, f=f, k=k, ec=ec, en=en, sq=sq, qt=qt, kt=kt, attn=attn, o=o, ecl=ecl,
                kk=kk, st_new=st_new)


def _hgrn_fwd(h, logits, gn):
    t = h.shape[0]
    nc = t // CHUNK

    def body(q_ref, f_ref, i_ref, og_ref, lg_ref, gn_ref, yb_ref, st_ref, state):
        ci = pl.program_id(0)

        @pl.when(ci == 0)
        def _():
            state[...] = jnp.zeros_like(state)

        causal, _ = _tri_masks()
        for hd in range(N_HEAD):
            sl = slice(hd * HEAD_DIM, (hd + 1) * HEAD_DIM)
            st = state[hd]
            st_ref[0, hd] = st
            r = _hgrn_chunk(q_ref[:, sl], f_ref[:, sl], i_ref[:, sl], _lower_bound(lg_ref, sl),
                            st, causal)
            state[hd] = r["st_new"]
            o = r["o"]
            on = o * lax.rsqrt(jnp.mean(o * o, axis=-1, keepdims=True) + RMS_EPS)
            og = og_ref[:, sl]
            yb_ref[:, sl] = (on * gn_ref[:, sl] * (og * _sig(og))).astype(BF16)

    def col(k):
        return pl.BlockSpec((CHUNK, D_MODEL), lambda ci: (ci, k))

    return _pc(body, name="hgrn_fwd",
               out_shape=(jax.ShapeDtypeStruct((t, D_MODEL), BF16),
                          jax.ShapeDtypeStruct((nc, N_HEAD, HEAD_DIM, HEAD_DIM), F32)),
               grid=(nc,),
               in_specs=[col(2), col(3), col(4), col(5),
                         pl.BlockSpec((2, D_MODEL), lambda ci: (0, 0)),
                         pl.BlockSpec((1, D_MODEL), lambda ci: (0, 0))],
               out_specs=(pl.BlockSpec((CHUNK, D_MODEL), lambda ci: (ci, 0)),
                          pl.BlockSpec((1, N_HEAD, HEAD_DIM, HEAD_DIM), lambda ci: (ci, 0, 0, 0))),
               scratch=[pltpu.VMEM((N_HEAD, HEAD_DIM, HEAD_DIM), F32)],
               sem=("arbitrary",))(h, h, h, h, logits, gn)


def _hgrn_bwd(h, dyb, st_all, logits, gn):
    t = h.shape[0]
    nc = t // CHUNK

    def body(q_ref, f_ref, i_ref, og_ref, dyb_ref, st_ref, lg_ref, gn_ref,
             dh1_ref, dh2_ref, dlb_ref, dgn_ref, dstate):
        ci = pl.program_id(0)

        @pl.when(ci == 0)
        def _():
            dstate[...] = jnp.zeros_like(dstate)
            dlb_ref[...] = jnp.zeros_like(dlb_ref)
            dgn_ref[...] = jnp.zeros_like(dgn_ref)

        causal, anti = _tri_masks()
        rowid = lax.broadcasted_iota(jnp.int32, (CHUNK, HEAD_DIM), 0)
        for hd in range(N_HEAD):
            sl = slice(hd * HEAD_DIM, (hd + 1) * HEAD_DIM)
            q = q_ref[:, sl]
            ii = i_ref[:, sl]
            og = og_ref[:, sl]
            lb = _lower_bound(lg_ref, sl)
            gnv = gn_ref[:, sl]
            st = st_ref[0, hd]
            dsn = dstate[hd]
            r = _hgrn_chunk(q, f_ref[:, sl], ii, lb, st, causal)
            o = r["o"]
            rinv = lax.rsqrt(jnp.mean(o * o, axis=-1, keepdims=True) + RMS_EPS)
            on = o * rinv
            so = _sig(og)
            sil = og * so
            dy = dyb_ref[:, sl]
            d_og = dy * on * gnv * (so * (1.0 + og * (1.0 - so)))
            dgn_ref[:, sl] += _colsum8(dy * on * sil)
            d_on = dy * gnv * sil
            d_o = rinv * (d_on - on * jnp.mean(d_on * on, axis=-1, keepdims=True))
            d_attn = jnp.where(causal, _dot(d_o, ii, NT), 0.0)
            d_i = _dot(r["attn"], d_o, TN) + _dot(r["kk"], dsn, NT)
            d_qt = _dot(d_attn, r["kt"]) + _dot(d_o, st)
            d_kt = _dot(d_attn, r["qt"], TN)
            d_kk = _dot(ii, dsn)
            dstate[hd] = _dot(d_o, r["qt"], TN) + dsn * r["ecl"]
            d_cl = (r["ecl"] * jnp.sum(st * dsn, axis=0, keepdims=True)
                    + jnp.sum(r["kk"] * d_kk, axis=0, keepdims=True))
            d_k = (d_kk * r["ecl"] + d_kt) * r["en"]
            d_c = (d_qt * r["qt"].astype(BF16).astype(F32) - d_kt * r["kt"].astype(BF16).astype(F32)
                   - d_kk * r["kk"])
            d_c = d_c + jnp.where(rowid == CHUNK - 1, d_cl, 0.0)
            d_lf = _dot32(anti.astype(F32), d_c)
            d_f = d_lf / r["f"] - d_k
            sg = r["sg"]
            dlb_ref[:, sl] += _colsum8(d_f * (1.0 - sg))
            d_fp = d_f * (1.0 - lb) * sg * (1.0 - sg)
            sq = r["sq"]
            d_q = d_qt * r["ec"] * (sq * (1.0 + q * (1.0 - sq)))
            dh1_ref[:, sl] = d_q.astype(BF16)
            dh1_ref[:, D_MODEL + hd * HEAD_DIM:D_MODEL + (hd + 1) * HEAD_DIM] = d_fp.astype(BF16)
            dh2_ref[:, sl] = d_i.astype(BF16)
            dh2_ref[:, D_MODEL + hd * HEAD_DIM:D_MODEL + (hd + 1) * HEAD_DIM] = d_og.astype(BF16)

    def col(k):
        return pl.BlockSpec((CHUNK, D_MODEL), lambda ci: (nc - 1 - ci, k))

    acc8 = pl.BlockSpec((8, D_MODEL), lambda ci: (0, 0))
    pair = pl.BlockSpec((CHUNK, 2 * D_MODEL), lambda ci: (nc - 1 - ci, 0))
    return _pc(body, name="hgrn_bwd",
               out_shape=(jax.ShapeDtypeStruct((t, 2 * D_MODEL), BF16),
                          jax.ShapeDtypeStruct((t, 2 * D_MODEL), BF16),
                          jax.ShapeDtypeStruct((8, D_MODEL), F32),
                          jax.ShapeDtypeStruct((8, D_MODEL), F32)),
               grid=(nc,),
               in_specs=[col(2), col(3), col(4), col(5),
                         pl.BlockSpec((CHUNK, D_MODEL), lambda ci: (nc - 1 - ci, 0)),
                         pl.BlockSpec((1, N_HEAD, HEAD_DIM, HEAD_DIM),
                                      lambda ci: (nc - 1 - ci, 0, 0, 0)),
                         pl.BlockSpec((2, D_MODEL), lambda ci: (0, 0)),
                         pl.BlockSpec((1, D_MODEL), lambda ci: (0, 0))],
               out_specs=(pair, pair, acc8, acc8),
               scratch=[pltpu.VMEM((N_HEAD, HEAD_DIM, HEAD_DIM), F32)],
               sem=("arbitrary",))(h, h, h, h, dyb, st_all, logits, gn)


def _mix_fwd(ya, yb, h, x, wb0, wb1, wo, g1, b1, tm):
    t = x.shape[0]

    def body(ya_ref, yb_ref, ga_ref, gb_ref, x_ref, wb0_ref, wb1_ref, wo_ref, g1_ref, b1_ref,
             r1_ref, a_ref, b_ref, m_ref, x1_ref):
        a = _dot(ya_ref[...], wb0_ref[...])
        b = _dot(yb_ref[...], wb1_ref[...])
        m = _sig(ga_ref[...]) * a + _sig(gb_ref[...]) * b
        r1 = ALPHA * x_ref[...] + _dot(m, wo_ref[...])
        xh, _ = _ln_stats(r1)
        r1_ref[...] = r1
        a_ref[...] = a
        b_ref[...] = b
        m_ref[...] = m.astype(BF16)
        x1_ref[...] = (xh * g1_ref[...] + b1_ref[...]).astype(BF16)

    tile = pl.BlockSpec((tm, D_MODEL), lambda i: (i, 0))
    wsp = pl.BlockSpec((D_MODEL, D_MODEL), lambda i: (0, 0))
    vec = pl.BlockSpec((1, D_MODEL), lambda i: (0, 0))
    f32o = jax.ShapeDtypeStruct((t, D_MODEL), F32)
    bfo = jax.ShapeDtypeStruct((t, D_MODEL), BF16)
    return _pc(body, name="mix_fwd", out_shape=(f32o, f32o, f32o, bfo, bfo), grid=(t // tm,),
               in_specs=[tile, tile,
                         pl.BlockSpec((tm, D_MODEL), lambda i: (i, 6)),
                         pl.BlockSpec((tm, D_MODEL), lambda i: (i, 7)),
                         tile, wsp, wsp, wsp, vec, vec],
               out_specs=(tile, tile, tile, tile, tile),
               sem=("parallel",))(ya, yb, h, h, x, wb0, wb1, wo, g1, b1)


def _mix_bwd(dr1, h, a, b, wo, wb0, wb1, tm):
    t = dr1.shape[0]

    def body(dr1_ref, ga_ref, gb_ref, a_ref, b_ref, wo_ref, wb0_ref, wb1_ref,
             da_ref, db_ref, dh3_ref, dya_ref, dyb_ref):
        d_m = _dot(dr1_ref[...], wo_ref[...], NT)
        sa = _sig(ga_ref[...])
        sb = _sig(gb_ref[...])
        d_a = (d_m * sa).astype(BF16)
        d_b = (d_m * sb).astype(BF16)
        da_ref[...] = d_a
        db_ref[...] = d_b
        dh3_ref[:, :D_MODEL] = (d_m * a_ref[...] * sa * (1.0 - sa)).astype(BF16)
        dh3_ref[:, D_MODEL:] = (d_m * b_ref[...] * sb * (1.0 - sb)).astype(BF16)
        dya_ref[...] = _dot(d_a, wb0_ref[...], NT)
        dyb_ref[...] = _dot(d_b, wb1_ref[...], NT)

    tile = pl.BlockSpec((tm, D_MODEL), lambda i: (i, 0))
    wsp = pl.BlockSpec((D_MODEL, D_MODEL), lambda i: (0, 0))
    f32o = jax.ShapeDtypeStruct((t, D_MODEL), F32)
    bfo = jax.ShapeDtypeStruct((t, D_MODEL), BF16)
    return _pc(body, name="mix_bwd",
               out_shape=(bfo, bfo, jax.ShapeDtypeStruct((t, 2 * D_MODEL), BF16), f32o, f32o),
               grid=(t // tm,),
               in_specs=[tile,
                         pl.BlockSpec((tm, D_MODEL), lambda i: (i, 6)),
                         pl.BlockSpec((tm, D_MODEL), lambda i: (i, 7)),
                         tile, tile, wsp, wsp, wsp],
               out_specs=(tile, tile, pl.BlockSpec((tm, 2 * D_MODEL), lambda i: (i, 0)),
                          tile, tile),
               sem=("parallel",))(dr1, h, h, a, b, wo, wb0, wb1)


FF_TILE = 1408
FF_NJ = D_FF // FF_TILE


def _shift_down(v, k):
    return pltpu.roll(v, k, 0)


def _shift_up(v, k):
    return pltpu.roll(v, v.shape[0] - k, 0)


def _conv_gate(ext, cw_ref, cb_ref):
    return (cw_ref[0:1, :] * _shift_down(ext, 2) + cw_ref[1:2, :] * _shift_down(ext, 1)
            + cw_ref[2:3, :] * ext + cb_ref[...])


def _ffn_act_fwd(h2, convw, convb, tm):
    t = h2.shape[0]
    nt8 = tm // 8

    def body(g_ref, gp_ref, v_ref, cw_ref, cb_ref, act_ref):
        i = pl.program_id(1)
        prev = gp_ref[...] * (i > 0).astype(F32)
        ext = jnp.concatenate([prev, g_ref[...]], axis=0)
        gc = _conv_gate(ext, cw_ref, cb_ref)[8:, :]
        act_ref[...] = (_gelu(gc) * v_ref[...]).astype(BF16)

    return _pc(body, name="ffn_act_fwd", out_shape=jax.ShapeDtypeStruct((t, D_FF), BF16),
               grid=(FF_NJ, t // tm),
               in_specs=[pl.BlockSpec((tm, FF_TILE), lambda j, i: (i, j)),
                         pl.BlockSpec((8, FF_TILE), lambda j, i: (jnp.maximum(i * nt8 - 1, 0), j)),
                         pl.BlockSpec((tm, FF_TILE), lambda j, i: (i, j + FF_NJ)),
                         pl.BlockSpec((3, FF_TILE), lambda j, i: (0, j)),
                         pl.BlockSpec((1, FF_TILE), lambda j, i: (0, j))],
               out_specs=pl.BlockSpec((tm, FF_TILE), lambda j, i: (i, j)),
               sem=("parallel", "parallel"))(h2, h2, h2, convw, convb)


def _ffn_act_bwd(h2, dact, convw, convb, tm):
    t = h2.shape[0]
    nt8 = tm // 8
    ni = t // tm
    last8 = t // 8 - 1

    def body(g_ref, gp_ref, gn_ref, v_ref, vn_ref, da_ref, dan_ref, cw_ref, cb_ref,
             dh2_ref, dcw_ref, dcb_ref):
        i = pl.program_id(1)

        @pl.when(i == 0)
        def _():
            dcw_ref[...] = jnp.zeros_like(dcw_ref)
            dcb_ref[...] = jnp.zeros_like(dcb_ref)

        prev = gp_ref[...] * (i > 0).astype(F32)
        ext = jnp.concatenate([prev, g_ref[...], gn_ref[...]], axis=0)
        vext = jnp.concatenate([jnp.zeros((8, FF_TILE), F32), v_ref[...], vn_ref[...]], axis=0)
        dnext = dan_ref[...] * (i < ni - 1).astype(F32)
        dext = jnp.concatenate([jnp.zeros((8, FF_TILE), F32), da_ref[...], dnext], axis=0)
        g2 = _shift_down(ext, 2)
        g1 = _shift_down(ext, 1)
        gc = cw_ref[0:1, :] * g2 + cw_ref[1:2, :] * g1 + cw_ref[2:3, :] * ext + cb_ref[...]
        gl, dgl = _gelu_and_grad(gc)
        d_gc = dext * vext * dgl
        d_gate = (cw_ref[2:3, :] * d_gc + cw_ref[1:2, :] * _shift_up(d_gc, 1)
                  + cw_ref[0:1, :] * _shift_up(d_gc, 2))
        dh2_ref[0] = d_gate[8:8 + tm, :].astype(BF16)
        dh2_ref[1] = (da_ref[...] * gl[8:8 + tm, :]).astype(BF16)
        dm = d_gc[8:8 + tm, :]
        s0 = jnp.sum(dm * g2[8:8 + tm, :], axis=0, keepdims=True)
        s1 = jnp.sum(dm * g1[8:8 + tm, :], axis=0, keepdims=True)
        s2 = jnp.sum(dm * ext[8:8 + tm, :], axis=0, keepdims=True)
        rowid = lax.broadcasted_iota(jnp.int32, (8, FF_TILE), 0)
        dcw_ref[...] += jnp.where(rowid == 0, s0, jnp.where(rowid == 1, s1,
                                                            jnp.where(rowid == 2, s2, 0.0)))
        dcb_ref[...] += _colsum8(dm)

    def prev8(off):
        return pl.BlockSpec((8, FF_TILE), lambda j, i: (jnp.maximum(i * nt8 - 1, 0), j + off))

    def next8(off):
        return pl.BlockSpec((8, FF_TILE), lambda j, i: (jnp.minimum((i + 1) * nt8, last8), j + off))

    def main(off):
        return pl.BlockSpec((tm, FF_TILE), lambda j, i: (i, j + off))

    acc = pl.BlockSpec((8, FF_TILE), lambda j, i: (0, j))
    return _pc(body, name="ffn_act_bwd",
               out_shape=(jax.ShapeDtypeStruct((2, t, D_FF), BF16),
                          jax.ShapeDtypeStruct((8, D_FF), F32),
                          jax.ShapeDtypeStruct((8, D_FF), F32)),
               grid=(FF_NJ, ni),
               in_specs=[main(0), prev8(0), next8(0), main(FF_NJ), next8(FF_NJ),
                         pl.BlockSpec((tm, FF_TILE), lambda j, i: (i, j)),
                         pl.BlockSpec((8, FF_TILE), lambda j, i: (jnp.minimum((i + 1) * nt8, last8), j)),
                         pl.BlockSpec((3, FF_TILE), lambda j, i: (0, j)),
                         pl.BlockSpec((1, FF_TILE), lambda j, i: (0, j))],
               out_specs=(pl.BlockSpec((2, tm, FF_TILE), lambda j, i: (0, i, j)), acc, acc),
               sem=("parallel", "arbitrary"))(h2, h2, h2, h2, h2, dact, dact, convw, convb)


def _out_fwd_bwd(act, x1b, r1, p2, tgt, wd, wpg, wpp, g1, b1, g2, b2, tm):
    t = r1.shape[0]

    def body(act_ref, x1b_ref, r1_ref, p_ref, tgt_ref, wd_ref, wpg_ref, wpp_ref,
             g1_ref, b1_ref, g2_ref, b2_ref,
             dr2_ref, dpg_ref, dpp_ref, loss_ref, dg2_ref, db2_ref):
        i = pl.program_id(0)

        @pl.when(i == 0)
        def _():
            loss_ref[...] = jnp.zeros_like(loss_ref)
            dg2_ref[...] = jnp.zeros_like(dg2_ref)
            db2_ref[...] = jnp.zeros_like(db2_ref)

        ffn = _dot(act_ref[...], wd_ref[...])
        pg = _dot(x1b_ref[...], wpg_ref[...])
        pp = _dot(p_ref[...], wpp_ref[...])
        s = _sig(pg)
        xh1, _ = _ln_stats(r1_ref[...])
        x1 = xh1 * g1_ref[...] + b1_ref[...]
        r2 = ALPHA * x1 + ffn + s * pp
        xh2, rstd2 = _ln_stats(r2)
        g2v = g2_ref[...]
        diff = xh2 * g2v + b2_ref[...] - tgt_ref[...]
        part = jnp.sum(jnp.sum(diff * diff, axis=1, keepdims=True), axis=0, keepdims=True)
        loss_ref[...] += jnp.broadcast_to(part * (0.5 / D_MODEL), loss_ref.shape)
        dy = diff * (1.0 / D_MODEL)
        dg2_ref[...] += _colsum8(dy * xh2)
        db2_ref[...] += _colsum8(dy)
        dr2 = _ln_bwd(dy * g2v, xh2, rstd2)
        dr2_ref[...] = dr2
        dpg_ref[...] = (dr2 * pp * s * (1.0 - s)).astype(BF16)
        dpp_ref[...] = (dr2 * s).astype(BF16)

    tile = pl.BlockSpec((tm, D_MODEL), lambda i: (i, 0))
    vec = pl.BlockSpec((1, D_MODEL), lambda i: (0, 0))
    acc8 = pl.BlockSpec((8, D_MODEL), lambda i: (0, 0))
    acc_shape = jax.ShapeDtypeStruct((8, D_MODEL), F32)
    return _pc(body, name="out_fwd_bwd",
               out_shape=(jax.ShapeDtypeStruct((t, D_MODEL), F32),
                          jax.ShapeDtypeStruct((t, D_MODEL), BF16),
                          jax.ShapeDtypeStruct((t, D_MODEL), BF16),
                          acc_shape, acc_shape, acc_shape),
               grid=(t // tm,),
               in_specs=[pl.BlockSpec((tm, D_FF), lambda i: (i, 0)), tile, tile,
                         pl.BlockSpec((tm, PLE_DIM), lambda i: (i, 0)), tile,
                         pl.BlockSpec((D_FF, D_MODEL), lambda i: (0, 0)),
                         pl.BlockSpec((D_MODEL, D_MODEL), lambda i: (0, 0)),
                         pl.BlockSpec((PLE_DIM, D_MODEL), lambda i: (0, 0)),
                         vec, vec, vec, vec],
               out_specs=(tile, tile, tile, acc8, acc8, acc8),
               sem=("arbitrary",))(act, x1b, r1, p2, tgt, wd, wpg, wpp, g1, b1, g2, b2)


def _ffn_in_bwd(dh2, wup_st, dpg, wpg, dr2, r1, g1, tm):
    t = r1.shape[0]
    ni = t // tm

    def body(dh2_ref, wup_ref, dpg_ref, wpg_ref, dr2_ref, r1_ref, g1_ref,
             dr1_ref, dg1_ref, db1_ref, acc):
        i = pl.program_id(0)
        j = pl.program_id(1)

        @pl.when((i == 0) & (j == 0))
        def _():
            dg1_ref[...] = jnp.zeros_like(dg1_ref)
            db1_ref[...] = jnp.zeros_like(db1_ref)

        @pl.when(j == 0)
        def _():
            acc[...] = _dot(dh2_ref[...], wup_ref[...], NT)

        @pl.when(j > 0)
        def _():
            acc[...] += _dot(dh2_ref[...], wup_ref[...], NT)

        @pl.when(j == N_CHIP - 1)
        def _():
            d_x1 = acc[...] + _dot(dpg_ref[...], wpg_ref[...], NT) + ALPHA * dr2_ref[...]
            xh, rstd = _ln_stats(r1_ref[...])
            dg1_ref[...] += _colsum8(d_x1 * xh)
            db1_ref[...] += _colsum8(d_x1)
            dr1_ref[...] = _ln_bwd(d_x1 * g1_ref[...], xh, rstd)

    tile = pl.BlockSpec((tm, D_MODEL), lambda i, j: (i, 0))
    acc8 = pl.BlockSpec((8, D_MODEL), lambda i, j: (0, 0))
    acc_shape = jax.ShapeDtypeStruct((8, D_MODEL), F32)
    return _pc(body, name="ffn_in_bwd",
               out_shape=(jax.ShapeDtypeStruct((t, D_MODEL), F32), acc_shape, acc_shape),
               grid=(ni, N_CHIP),
               in_specs=[pl.BlockSpec((None, tm, FF_TILE), lambda i, j: (j // FF_NJ, i, j % FF_NJ)),
                         pl.BlockSpec((None, D_MODEL, FF_TILE), lambda i, j: (j, 0, 0)),
                         tile, pl.BlockSpec((D_MODEL, D_MODEL), lambda i, j: (0, 0)),
                         tile, tile, pl.BlockSpec((1, D_MODEL), lambda i, j: (0, 0))],
               out_specs=(tile, acc8, acc8),
               scratch=[pltpu.VMEM((tm, D_MODEL), F32)],
               sem=("arbitrary", "arbitrary"))(dh2, wup_st, dpg, wpg, dr2, r1, g1)


ANY = pl.BlockSpec(memory_space=pl.ANY)


def _chip_peers():
    x, y, c = lax.axis_index("x"), lax.axis_index("y"), lax.axis_index("c")
    return x, y, c, [(1 - x, y), (x, 1 - y), (1 - x, 1 - y)]


def _gather_weights(halved, whole):
    n, nw = len(halved), len(whole)

    def body(*refs):
        ins, wins = refs[:n], refs[n:n + nw]
        outs, wouts = refs[n + nw:2 * n + nw], refs[2 * n + nw:2 * (n + nw)]
        ici_send, ici_recv, d2d_send, d2d_recv, local_sems = refs[2 * (n + nw):]
        x, y, c, peers = _chip_peers()
        me = 2 * x + y
        sibling = (x, y, 1 - c)
        locals_, sends = [], []
        for ti in range(n + nw):
            src, dst = (ins[ti], outs[ti]) if ti < n else (wins[ti - n], wouts[ti - n])
            lc = pltpu.make_async_copy(src, dst.at[me], local_sems.at[ti])
            lc.start()
            locals_.append(lc)
            for k, (px, py) in enumerate(peers):
                cp = pltpu.make_async_remote_copy(
                    src_ref=src.at[c] if ti < n else src,
                    dst_ref=dst.at[me, c] if ti < n else dst.at[me],
                    send_sem=ici_send.at[ti * 3 + k], recv_sem=ici_recv.at[ti * 3 + k],
                    device_id=(px, py, c), device_id_type=MESH)
                cp.start()
                sends.append(cp)
        for ti in range(n + nw):
            src, dst = (ins[ti], outs[ti]) if ti < n else (wins[ti - n], wouts[ti - n])
            for k, (px, py) in enumerate(peers):
                pk = 2 * px + py
                pltpu.make_async_remote_copy(
                    src_ref=src.at[c] if ti < n else src,
                    dst_ref=dst.at[pk, c] if ti < n else dst.at[pk],
                    send_sem=ici_send.at[ti * 3 + k], recv_sem=ici_recv.at[ti * 3 + k],
                    device_id=(px, py, c), device_id_type=MESH).wait_recv()
                if ti < n:
                    fw = pltpu.make_async_remote_copy(
                        src_ref=dst.at[pk, c], dst_ref=dst.at[pk, c],
                        send_sem=d2d_send.at[ti * 3 + k], recv_sem=d2d_recv.at[ti * 3 + k],
                        device_id=sibling, device_id_type=MESH)
                    fw.start()
                    sends.append(fw)
        for ti in range(n):
            for k, (px, py) in enumerate(peers):
                pk = 2 * px + py
                pltpu.make_async_remote_copy(
                    src_ref=outs[ti].at[pk, 1 - c], dst_ref=outs[ti].at[pk, 1 - c],
                    send_sem=d2d_send.at[ti * 3 + k], recv_sem=d2d_recv.at[ti * 3 + k],
                    device_id=sibling, device_id_type=MESH).wait_recv()
        for cp in sends:
            cp.wait_send()
        for lc in locals_:
            lc.wait()

    out_shape = tuple(jax.ShapeDtypeStruct((N_CHIP,) + s.shape, s.dtype) for s in list(halved) + list(whole))
    return _pc(body, name="gather_weights", out_shape=out_shape,
               in_specs=[ANY] * (n + nw), out_specs=tuple([ANY] * (n + nw)),
               scratch=[pltpu.SemaphoreType.DMA((3 * (n + nw),)), pltpu.SemaphoreType.DMA((3 * (n + nw),)),
                        pltpu.SemaphoreType.DMA((3 * n,)), pltpu.SemaphoreType.DMA((3 * n,)),
                        pltpu.SemaphoreType.DMA((n + nw,))])(*halved, *whole)


def _rs_sibling_exchange(grads):
    n = len(grads)

    def body(*refs):
        ins, outs = refs[:n], refs[n:2 * n]
        send_sems, recv_sems = refs[2 * n:]
        x, y, c = lax.axis_index("x"), lax.axis_index("y"), lax.axis_index("c")
        sends = []
        for ti in range(n):
            half = ins[ti].shape[1] // 2
            cp = pltpu.make_async_remote_copy(
                src_ref=ins[ti].at[:, pl.ds(pl.multiple_of((1 - c) * half, 8), half), :],
                dst_ref=outs[ti],
                send_sem=send_sems.at[ti], recv_sem=recv_sems.at[ti],
                device_id=(x, y, 1 - c), device_id_type=MESH)
            cp.start()
            sends.append(cp)
        for cp in sends:
            cp.wait()

    return _pc(body, name="rs_sibling_exchange",
               out_shape=tuple(jax.ShapeDtypeStruct((N_CHIP, g.shape[1] // 2, g.shape[2]), g.dtype)
                               for g in grads),
               in_specs=[ANY] * n, out_specs=tuple([ANY] * n),
               scratch=[pltpu.SemaphoreType.DMA((n,)), pltpu.SemaphoreType.DMA((n,))])(*grads)


def _rs_add_halves(name, grad, recv, core):
    _, r, cdim = grad.shape
    half = r // 2
    tr = _row_tile(half, cdim, mult=16)
    nr = half // tr

    def body(c_ref, g_ref, r_ref, o_ref):
        o_ref[...] = (g_ref[...] + r_ref[...]).astype(BF16)

    return _pc(body, name=name, out_shape=jax.ShapeDtypeStruct((N_CHIP, half, cdim), BF16),
               grid=(N_CHIP, nr), nsp=1,
               in_specs=[pl.BlockSpec((None, tr, cdim), lambda j, i, c_ref: (j, c_ref[0] * nr + i, 0)),
                         pl.BlockSpec((None, tr, cdim), lambda j, i, c_ref: (j, i, 0))],
               out_specs=pl.BlockSpec((None, tr, cdim), lambda j, i, c_ref: (j, i, 0)),
               sem=("parallel", "parallel"))(core, grad, recv)


def _rs_chip_exchange(parts):
    n = len(parts)

    def body(*refs):
        ins, outs = refs[:n], refs[n:2 * n]
        send_sems, recv_sems, local_sems = refs[2 * n:]
        x, y, c, peers = _chip_peers()
        me = 2 * x + y
        locals_, sends = [], []
        for ti in range(n):
            lc = pltpu.make_async_copy(ins[ti].at[me], outs[ti].at[me], local_sems.at[ti])
            lc.start()
            locals_.append(lc)
            for k, (px, py) in enumerate(peers):
                cp = pltpu.make_async_remote_copy(
                    src_ref=ins[ti].at[2 * px + py], dst_ref=outs[ti].at[me],
                    send_sem=send_sems.at[ti * 3 + k], recv_sem=recv_sems.at[ti * 3 + k],
                    device_id=(px, py, c), device_id_type=MESH)
                cp.start()
                sends.append(cp)
        for ti in range(n):
            for k, (px, py) in enumerate(peers):
                pltpu.make_async_remote_copy(
                    src_ref=ins[ti].at[me], dst_ref=outs[ti].at[2 * px + py],
                    send_sem=send_sems.at[ti * 3 + k], recv_sem=recv_sems.at[ti * 3 + k],
                    device_id=(px, py, c), device_id_type=MESH).wait_recv()
        for cp in sends:
            cp.wait_send()
        for lc in locals_:
            lc.wait()

    return _pc(body, name="rs_chip_exchange",
               out_shape=tuple(jax.ShapeDtypeStruct(p.shape, p.dtype) for p in parts),
               in_specs=[ANY] * n, out_specs=tuple([ANY] * n),
               scratch=[pltpu.SemaphoreType.DMA((3 * n,)), pltpu.SemaphoreType.DMA((3 * n,)),
                        pltpu.SemaphoreType.DMA((n,))])(*parts)


def _rs_sum_chips(name, recv):
    _, half, cdim = recv.shape
    tr = _row_tile(half, cdim, mult=16)

    def body(r_ref, o_ref):
        o_ref[...] = ((r_ref[0].astype(F32) + r_ref[1].astype(F32)) + r_ref[2].astype(F32)
                      ) + r_ref[3].astype(F32)

    return _pc(body, name=name, out_shape=jax.ShapeDtypeStruct((half, cdim), F32),
               grid=(half // tr,),
               in_specs=[pl.BlockSpec((N_CHIP, tr, cdim), lambda i: (0, i, 0))],
               out_specs=pl.BlockSpec((tr, cdim), lambda i: (i, 0)),
               sem=("parallel",))(recv)


def _rs_join_halves(halves):
    n = len(halves)

    def body(*refs):
        ins, outs = refs[:n], refs[n:2 * n]
        send_sems, recv_sems, local_sems = refs[2 * n:]
        x, y, c = lax.axis_index("x"), lax.axis_index("y"), lax.axis_index("c")
        locals_, sends = [], []
        for ti in range(n):
            mine = outs[ti].at[c]
            lc = pltpu.make_async_copy(ins[ti], mine, local_sems.at[ti])
            lc.start()
            locals_.append(lc)
            cp = pltpu.make_async_remote_copy(
                src_ref=ins[ti], dst_ref=mine,
                send_sem=send_sems.at[ti], recv_sem=recv_sems.at[ti],
                device_id=(x, y, 1 - c), device_id_type=MESH)
            cp.start()
            sends.append(cp)
        for ti in range(n):
            theirs = outs[ti].at[1 - c]
            pltpu.make_async_remote_copy(
                src_ref=ins[ti], dst_ref=theirs,
                send_sem=send_sems.at[ti], recv_sem=recv_sems.at[ti],
                device_id=(x, y, 1 - c), device_id_type=MESH).wait_recv()
        for cp in sends:
            cp.wait_send()
        for lc in locals_:
            lc.wait()

    return _pc(body, name="rs_join_halves",
               out_shape=tuple(jax.ShapeDtypeStruct((2,) + hv.shape, hv.dtype) for hv in halves),
               in_specs=[ANY] * n, out_specs=tuple([ANY] * n),
               scratch=[pltpu.SemaphoreType.DMA((n,)), pltpu.SemaphoreType.DMA((n,)),
                        pltpu.SemaphoreType.DMA((n,))])(*halves)


def _adamw_rows(name, g, w, m, v):
    r, cdim = g.shape
    tr = _row_tile(r, cdim, budget=1 << 19)

    def body(g_ref, w_ref, m_ref, v_ref, d_ref, m2_ref, v2_ref):
        d, m2, v2 = _adamw(w_ref[...], g_ref[...], m_ref[...], v_ref[...])
        d_ref[...] = d
        m2_ref[...] = m2
        v2_ref[...] = v2

    tile = pl.BlockSpec((tr, cdim), lambda i: (i, 0))
    shp = jax.ShapeDtypeStruct((r, cdim), F32)
    return _pc(body, name=name, out_shape=(shp, shp, shp), grid=(r // tr,),
               in_specs=[tile] * 4, out_specs=(tile, tile, tile),
               sem=("parallel",))(g, w, m, v)


def _adamw_whole(name, g, w, m, v):
    def body(g_ref, w_ref, m_ref, v_ref, d_ref, m2_ref, v2_ref):
        d, m2, v2 = _adamw(w_ref[...], g_ref[...], m_ref[...], v_ref[...])
        d_ref[...] = d
        m2_ref[...] = m2
        v2_ref[...] = v2

    shp = jax.ShapeDtypeStruct(g.shape, F32)
    return _pc(body, name=name, out_shape=(shp, shp, shp))(g, w, m, v)


SMALL_LAYOUT = (
    ("sgu_w_s", 1024, 1, 0),
    ("sgu_b_s", 8, 1, 1024),
    ("sgu_norm_g", 1, 0, 0),
    ("sgu_norm_b", 1, 0, 8),
    ("hgrn_norm_g", 1, 0, 24),
    ("ln1_g", 1, 0, 32),
    ("ln1_b", 1, 0, 40),
    ("ffn_conv_b", 1, 2, 8),
    ("ln2_g", 1, 0, 48),
    ("ln2_b", 1, 0, 56),
)
LB_ROW = 16
LOSS_ROW = 64


def _small_allreduce_adamw(bufs, logits, m_logits, v_logits, small_w, small_m, small_v):
    ns = len(SMALL_LAYOUT)
    nb = len(bufs)

    def body(*refs):
        buf_refs = refs[:nb]
        lg_ref, mlg_ref, vlg_ref = refs[nb:nb + 3]
        pos = nb + 3
        w_refs = refs[pos:pos + ns]
        m_refs = refs[pos + ns:pos + 2 * ns]
        v_refs = refs[pos + 2 * ns:pos + 3 * ns]
        pos += 3 * ns
        loss_ref, dcw_ref = refs[pos:pos + 2]
        lg_outs = refs[pos + 2:pos + 6]
        pos += 6
        outs = refs[pos:pos + 4 * ns]
        pos += 4 * ns
        gath = refs[pos:pos + nb]
        send_sems, recv_sems = refs[pos + nb:pos + nb + 2]

        x, y, c = lax.axis_index("x"), lax.axis_index("y"), lax.axis_index("c")
        me = 4 * x + 2 * y + c
        sends = []
        for d in range(1, N_DEV):
            peer = (x ^ (d >> 2), y ^ ((d >> 1) & 1), c ^ (d & 1))
            for b in range(nb):
                cp = pltpu.make_async_remote_copy(
                    src_ref=buf_refs[b], dst_ref=gath[b].at[me],
                    send_sem=send_sems.at[(d - 1) * nb + b], recv_sem=recv_sems.at[(d - 1) * nb + b],
                    device_id=peer, device_id_type=MESH)
                cp.start()
                sends.append(cp)
        for b in range(nb):
            gath[b][me] = buf_refs[b][...]
        for d in range(1, N_DEV):
            peer = (x ^ (d >> 2), y ^ ((d >> 1) & 1), c ^ (d & 1))
            src = 4 * peer[0] + 2 * peer[1] + peer[2]
            for b in range(nb):
                pltpu.make_async_remote_copy(
                    src_ref=buf_refs[b], dst_ref=gath[b].at[src],
                    send_sem=send_sems.at[(d - 1) * nb + b], recv_sem=recv_sems.at[(d - 1) * nb + b],
                    device_id=peer, device_id_type=MESH).wait_recv()
        for cp in sends:
            cp.wait_send()

        tot = []
        for b in range(nb):
            s = gath[b][0]
            for d in range(1, N_DEV):
                s = s + gath[b][d]
            tot.append(s)

        loss_ref[...] = tot[0][LOSS_ROW:LOSS_ROW + 8, :]
        dcw_ref[...] = tot[2][0:8, :]
        lb = _sig(lg_ref[0:1, :] - lg_ref[1:2, :])
        d0 = tot[0][LB_ROW:LB_ROW + 1, :] * lb * (1.0 - lb)
        rowid = lax.broadcasted_iota(jnp.int32, (2, D_MODEL), 0)
        g_lg = jnp.where(rowid == 0, d0, -d0)
        dl, ml, vl = _adamw(lg_ref[...], g_lg, mlg_ref[...], vlg_ref[...])
        lg_outs[0][...] = g_lg
        lg_outs[1][...] = dl
        lg_outs[2][...] = ml
        lg_outs[3][...] = vl
        for si, (_, rows, b, r0) in enumerate(SMALL_LAYOUT):
            g = tot[b][r0:r0 + rows, :]
            dl, ml, vl = _adamw(w_refs[si][...], g, m_refs[si][...], v_refs[si][...])
            outs[4 * si][...] = g
            outs[4 * si + 1][...] = dl
            outs[4 * si + 2][...] = ml
            outs[4 * si + 3][...] = vl

    shapes = [jax.ShapeDtypeStruct((8, D_MODEL), F32), jax.ShapeDtypeStruct((8, D_FF), F32)]
    shapes += [jax.ShapeDtypeStruct((2, D_MODEL), F32)] * 4
    for w in small_w:
        shapes += [jax.ShapeDtypeStruct(w.shape, F32)] * 4
    scratch = [pltpu.VMEM((N_DEV,) + b.shape, F32) for b in bufs]
    scratch += [pltpu.SemaphoreType.DMA(((N_DEV - 1) * nb,)), pltpu.SemaphoreType.DMA(((N_DEV - 1) * nb,))]
    vm = pl.BlockSpec(memory_space=pltpu.VMEM)
    n_in = nb + 3 + 3 * ns
    res = _pc(body, name="small_allreduce_adamw", out_shape=tuple(shapes),
              in_specs=[vm] * n_in, out_specs=tuple([vm] * len(shapes)),
              scratch=scratch)(*bufs, logits, m_logits, v_logits, *small_w, *small_m, *small_v)
    return res[0], res[1], res[2:6], res[6:]


def kernel(x, p, w_in, sgu_w_s, sgu_b_s, sgu_norm_g, sgu_norm_b, hgrn_lb_logits, hgrn_norm_g, w_branch, w_out, ln1_g, ln1_b, ffn_w_up, ffn_conv_w, ffn_conv_b, ffn_w_down, ln2_g, ln2_b, ple_w_proj, ple_w_gate, loss_target, m_w_in, m_sgu_w_s, m_sgu_b_s, m_sgu_norm_g, m_sgu_norm_b, m_hgrn_lb_logits, m_hgrn_norm_g, m_w_branch, m_w_out, m_ln1_g, m_ln1_b, m_ffn_w_up, m_ffn_conv_w, m_ffn_conv_b, m_ffn_w_down, m_ln2_g, m_ln2_b, m_ple_w_proj, m_ple_w_gate, v_w_in, v_sgu_w_s, v_sgu_b_s, v_sgu_norm_g, v_sgu_norm_b, v_hgrn_lb_logits, v_hgrn_norm_g, v_w_branch, v_w_out, v_ln1_g, v_ln1_b, v_ffn_w_up, v_ffn_conv_w, v_ffn_conv_b, v_ffn_w_down, v_ln2_g, v_ln2_b, v_ple_w_proj, v_ple_w_gate):
    t = x.shape[1]
    x2 = x.reshape(t, D_MODEL)
    p2 = p.reshape(t, PLE_DIM)
    tgt = loss_target.reshape(t, D_MODEL)
    core = lax.axis_index("c").astype(jnp.int32).reshape(1)

    big_w = [w_in[0], w_branch[0, 0], w_branch[0, 1], w_out[0], ffn_w_up[0], ffn_w_down[0],
             ple_w_proj[0], ple_w_gate[0]]
    big_m = [m_w_in[0], m_w_branch[0, 0], m_w_branch[0, 1], m_w_out[0], m_ffn_w_up[0],
             m_ffn_w_down[0], m_ple_w_proj[0], m_ple_w_gate[0]]
    big_v = [v_w_in[0], v_w_branch[0, 0], v_w_branch[0, 1], v_w_out[0], v_ffn_w_up[0],
             v_ffn_w_down[0], v_ple_w_proj[0], v_ple_w_gate[0]]
    gathered = _gather_weights(
        [w.astype(BF16).reshape(2, w.shape[0] // 2, w.shape[1]) for w in big_w], [ffn_conv_w[0]])
    gathered = [g.reshape(N_CHIP, w.shape[0], w.shape[1]) for g, w in zip(gathered, big_w)] + [gathered[-1]]
    win_st = gathered[0]
    wb0 = gathered[1].reshape(D_MODEL, D_MODEL)
    wb1 = gathered[2].reshape(D_MODEL, D_MODEL)
    wo = gathered[3].reshape(D_MODEL, D_MODEL)
    wup_st = gathered[4]
    wd = gathered[5].reshape(D_FF, D_MODEL)
    wpp = jnp.transpose(gathered[6], (1, 0, 2)).reshape(PLE_DIM, D_MODEL)
    wpg = gathered[7].reshape(D_MODEL, D_MODEL)
    convw = jnp.transpose(gathered[8], (1, 0, 2)).reshape(3, D_FF)

    cid = jnp.arange(SGU_BLOCK) // CHUNK
    maskf = (cid[:, None] >= cid[None, :]).astype(F32)
    ws_masked = sgu_w_s[0] * maskf[None]
    wm = ws_masked.astype(BF16)
    wmt = jnp.transpose(ws_masked, (0, 2, 1)).astype(BF16)
    bsb = jnp.broadcast_to(sgu_b_s[0][:, :, None], (N_GROUP, SGU_BLOCK, 128))

    h = _mm_nn_stacked("in_proj", x2, win_st, 512)
    ya = _sgu_fwd(h, wm, bsb, sgu_norm_g, sgu_norm_b)
    yb, st_all = _hgrn_fwd(h, hgrn_lb_logits, hgrn_norm_g)
    r1, a_br, b_br, m_bf, x1b = _mix_fwd(ya, yb, h, x2, wb0, wb1, wo, ln1_g, ln1_b, 256)
    h2 = _mm_nn_stacked("ffn_up", x1b, wup_st, 512)
    act = _ffn_act_fwd(h2, convw, ffn_conv_b, 256)
    dr2, dpg, dpp, loss_acc, dg2, db2 = _out_fwd_bwd(
        act, x1b, r1, p2, tgt, wd, wpg, wpp, ln1_g, ln1_b, ln2_g, ln2_b, 256)

    dact = _mm("ffn_down_bwd", dr2, wd, NT, (t // 512, FF_NJ, 1),
               pl.BlockSpec((512, D_MODEL), lambda i, j, k: (i, 0)),
               pl.BlockSpec((FF_TILE, D_MODEL), lambda i, j, k: (j, 0)),
               jax.ShapeDtypeStruct((t, D_FF), F32),
               pl.BlockSpec((512, FF_TILE), lambda i, j, k: (i, j)))
    dh2, dcw, dcb = _ffn_act_bwd(h2, dact, convw, ffn_conv_b, 256)
    d_wd = _mm_tn("ffn_down_wgrad", act, dr2, FF_TILE, D_MODEL, 512)
    d_wpg = _mm_tn("ple_gate_wgrad", x1b, dpg, D_MODEL, D_MODEL, 512)
    d_wpp_st = _mm_tn("ple_proj_wgrad", p2, dpp, PLE_DIM, PLE_DIM, 512, stacked=True)
    d_wup_st = _mm("ffn_up_wgrad", x1b, dh2, TN, (1, N_CHIP, t // 512),
                   pl.BlockSpec((512, D_MODEL), lambda i, j, k: (k, 0)),
                   pl.BlockSpec((None, 512, FF_TILE), lambda i, j, k: (j // FF_NJ, k, j % FF_NJ)),
                   jax.ShapeDtypeStruct((N_CHIP, D_MODEL, FF_TILE), F32),
                   pl.BlockSpec((None, D_MODEL, FF_TILE), lambda i, j, k: (j, 0, 0)))
    dr1, dg1, db1 = _ffn_in_bwd(dh2, wup_st, dpg, wpg, dr2, r1, ln1_g, 512)
    da_bf, db_bf, dh3, dya, dyb = _mix_bwd(dr1, h, a_br, b_br, wo, wb0, wb1, 256)
    d_wo = _mm_tn("out_proj_wgrad", m_bf, dr1, D_MODEL, D_MODEL, 512)
    d_wb0 = _mm_tn("branch0_wgrad", ya, da_bf, D_MODEL, D_MODEL, 512)
    d_wb1 = _mm_tn("branch1_wgrad", yb, db_bf, D_MODEL, D_MODEL, 512)
    dh1, dh2h, dlb, dgn = _hgrn_bwd(h, dyb, st_all, hgrn_lb_logits, hgrn_norm_g)
    dh0, dws, dbs, dgv, dbv = _sgu_bwd(h, dya, wm, wmt, bsb, sgu_norm_g, sgu_norm_b, maskf)
    dh_parts = [dh0, dh1, dh2h, dh3]
    d_win = [_mm_tn("in_proj_wgrad%d" % j, x2, dh_parts[j], D_MODEL, 1024, 512) for j in range(4)]
    gx = dr1
    scale = ALPHA
    for j in range(4):
        gx = _mm("in_proj_xgrad%d" % j, dh_parts[j], win_st, NT, (t // 512, 1, 1),
                 pl.BlockSpec((512, 2 * D_MODEL), lambda i, jj, k: (i, 0)),
                 pl.BlockSpec((None, D_MODEL, 2 * D_MODEL), lambda i, jj, k, j=j: (j, 0, 0)),
                 jax.ShapeDtypeStruct((t, D_MODEL), F32),
                 pl.BlockSpec((512, D_MODEL), lambda i, jj, k: (i, 0)),
                 add=gx, add_spec=pl.BlockSpec((512, D_MODEL), lambda i, jj, k: (i, 0)),
                 add_scale=scale)
        scale = 1.0

    grads_st = [jnp.stack(d_win), d_wb0.reshape(4, 256, D_MODEL), d_wb1.reshape(4, 256, D_MODEL),
                d_wo.reshape(4, 256, D_MODEL), d_wup_st, d_wd.reshape(4, D_FF // 4, D_MODEL),
                d_wpp_st, d_wpg.reshape(4, 256, D_MODEL)]
    recv_a = _rs_sibling_exchange(grads_st)
    parts = [_rs_add_halves("rs_add_halves%d" % i, g, r, core)
             for i, (g, r) in enumerate(zip(grads_st, recv_a))]
    recv_b = _rs_chip_exchange(parts)
    halves = [_rs_sum_chips("rs_sum_chips%d" % i, r) for i, r in enumerate(recv_b)]
    full = [f.reshape(2 * f.shape[1], f.shape[2]) for f in _rs_join_halves(halves)]
    big_out = [_adamw_rows("adamw_big%d" % i, full[i], big_w[i], big_m[i], big_v[i])
               for i in range(len(full))]

    buf0 = jnp.concatenate([dgv, dbv, dlb, dgn, dg1, db1, dg2, db2, loss_acc], axis=0)
    buf1 = jnp.concatenate([dws.reshape(N_GROUP * 128, 128), dbs], axis=0)
    buf2 = jnp.concatenate([dcw, dcb], axis=0)
    small_in = dict(sgu_w_s=(sgu_w_s, m_sgu_w_s, v_sgu_w_s), sgu_b_s=(sgu_b_s, m_sgu_b_s, v_sgu_b_s),
                    sgu_norm_g=(sgu_norm_g, m_sgu_norm_g, v_sgu_norm_g),
                    sgu_norm_b=(sgu_norm_b, m_sgu_norm_b, v_sgu_norm_b),
                    hgrn_norm_g=(hgrn_norm_g, m_hgrn_norm_g, v_hgrn_norm_g),
                    ln1_g=(ln1_g, m_ln1_g, v_ln1_g), ln1_b=(ln1_b, m_ln1_b, v_ln1_b),
                    ffn_conv_b=(ffn_conv_b, m_ffn_conv_b, v_ffn_conv_b),
                    ln2_g=(ln2_g, m_ln2_g, v_ln2_g), ln2_b=(ln2_b, m_ln2_b, v_ln2_b))

    def flat(name, arr):
        rows = dict((n, r) for n, r, _, _ in SMALL_LAYOUT)[name]
        return arr.reshape(rows, arr.size // rows)

    names = [n for n, _, _, _ in SMALL_LAYOUT]
    sw = [flat(n, small_in[n][0]) for n in names]
    sm = [flat(n, small_in[n][1]) for n in names]
    sv = [flat(n, small_in[n][2]) for n in names]
    loss_rows, dcw_tot, lg_out, small_out = _small_allreduce_adamw(
        [buf0, buf1, buf2], hgrn_lb_logits, m_hgrn_lb_logits, v_hgrn_lb_logits, sw, sm, sv)
    loss = loss_rows[0, 0]

    chip = 2 * lax.axis_index("x") + lax.axis_index("y")
    g_cw = lax.dynamic_slice(dcw_tot, (0, chip * (D_FF // 4)), (3, D_FF // 4))
    cw_out = _adamw_whole("adamw_conv_w", g_cw, ffn_conv_w[0], m_ffn_conv_w[0], v_ffn_conv_w[0])

    res = {}
    for si, n in enumerate(names):
        shp = small_in[n][0].shape
        res[n] = tuple(small_out[4 * si + k].reshape(shp) for k in range(4))
    res["hgrn_lb_logits"] = tuple(lg_out)
    res["ffn_conv_w"] = (g_cw[None],) + tuple(o[None] for o in cw_out)

    def big(i):
        return (full[i],) + tuple(big_out[i])

    res["w_in"] = tuple(o[None] for o in big(0))
    res["w_branch"] = tuple(jnp.stack([o0, o1])[None] for o0, o1 in zip(big(1), big(2)))
    res["w_out"] = tuple(o[None] for o in big(3))
    res["ffn_w_up"] = tuple(o[None] for o in big(4))
    res["ffn_w_down"] = tuple(o[None] for o in big(5))
    res["ple_w_proj"] = tuple(o[None] for o in big(6))
    res["ple_w_gate"] = tuple(o[None] for o in big(7))

    order = ["w_in", "sgu_w_s", "sgu_b_s", "sgu_norm_g", "sgu_norm_b", "hgrn_lb_logits",
             "hgrn_norm_g", "w_branch", "w_out", "ln1_g", "ln1_b", "ffn_w_up", "ffn_conv_w",
             "ffn_conv_b", "ffn_w_down", "ln2_g", "ln2_b", "ple_w_proj", "ple_w_gate"]
    outs = [loss, gx.reshape(1, t, D_MODEL)]
    for k in range(4):
        outs += [res[n][k] for n in order]
    return tuple(outs)
```

```python
import functools

import jax
import jax.numpy as jnp
from jax import lax
from jax.experimental import pallas as pl
from jax.experimental.pallas import tpu as pltpu

F32 = jnp.float32
BF16 = jnp.bfloat16
HIGHEST = lax.Precision.HIGHEST
MESH = pl.DeviceIdType.MESH

D_MODEL = 1024
CHUNK = 64
SGU_BLOCK = 128
N_GROUP = 8
N_HEAD = 8
HEAD_DIM = 128
D_FF = 2816
PLE_DIM = 256
IN_COLS = 8192
LN_EPS = 1e-5
RMS_EPS = 1e-6
ALPHA = 2.0 ** 0.25
N_CHIP = 4
N_DEV = 8

ADAM_LR = 0.001
ADAM_B1 = 0.9
ADAM_B2 = 0.999
ADAM_EPS = 1e-08
ADAM_WD = 0.01
ADAM_STEP = 10

VMEM_LIMIT = 56 * 1024 * 1024

NN = (((1,), (0,)), ((), ()))
NT = (((1,), (1,)), ((), ()))
TN = (((0,), (0,)), ((), ()))


def _pc(body, *, name, out_shape, grid=None, in_specs=None, out_specs=None, scratch=(),
        sem=None, nsp=0, vmem=VMEM_LIMIT):
    params = dict(vmem_limit_bytes=vmem)
    if sem is not None:
        params["dimension_semantics"] = sem
    kw = dict(name=name, out_shape=out_shape, compiler_params=pltpu.CompilerParams(**params))
    if nsp:
        kw["grid_spec"] = pltpu.PrefetchScalarGridSpec(
            num_scalar_prefetch=nsp, grid=grid, in_specs=in_specs, out_specs=out_specs,
            scratch_shapes=list(scratch))
    else:
        if grid is not None:
            kw["grid"] = grid
        if in_specs is not None:
            kw["in_specs"] = in_specs
            kw["out_specs"] = out_specs
        kw["scratch_shapes"] = list(scratch)
    return pl.pallas_call(body, **kw)


def _dot(a, b, dims=NN):
    return lax.dot_general(a.astype(BF16), b.astype(BF16), dims, preferred_element_type=F32)


def _dot32(a, b, dims=NN):
    return lax.dot_general(a, b, dims, precision=HIGHEST, preferred_element_type=F32)


def _sig(x):
    return 1.0 / (1.0 + jnp.exp(-x))


_GC = 0.7978845608028654
_GA = 0.044715


def _gelu(x):
    return 0.5 * x * (1.0 + jnp.tanh(_GC * (x + _GA * x * x * x)))


def _gelu_and_grad(x):
    t = jnp.tanh(_GC * (x + _GA * x * x * x))
    g = 0.5 * x * (1.0 + t)
    dg = 0.5 * (1.0 + t) + 0.5 * x * (1.0 - t * t) * _GC * (1.0 + 3.0 * _GA * x * x)
    return g, dg


def _ln_stats(r):
    mu = jnp.mean(r, axis=-1, keepdims=True)
    xc = r - mu
    var = jnp.mean(xc * xc, axis=-1, keepdims=True)
    rstd = lax.rsqrt(var + LN_EPS)
    return xc * rstd, rstd


def _ln_bwd(dxh, xh, rstd):
    m1 = jnp.mean(dxh, axis=-1, keepdims=True)
    m2 = jnp.mean(dxh * xh, axis=-1, keepdims=True)
    return rstd * (dxh - m1 - xh * m2)


def _colsum8(v):
    return jnp.broadcast_to(jnp.sum(v, axis=0, keepdims=True), (8, v.shape[1]))


def _adamw(w, g, m, v):
    m2 = ADAM_B1 * m + (1.0 - ADAM_B1) * g
    v2 = ADAM_B2 * v + (1.0 - ADAM_B2) * (g * g)
    m_hat = m2 / (1.0 - ADAM_B1 ** ADAM_STEP)
    v_hat = v2 / (1.0 - ADAM_B2 ** ADAM_STEP)
    delta = -ADAM_LR * (m_hat / (jnp.sqrt(v_hat) + ADAM_EPS) + ADAM_WD * w)
    return delta, m2, v2


def _row_tile(rows, cols, itemsize=4, budget=1 << 20, mult=8):
    best = mult
    for tr in range(mult, rows + 1, mult):
        if rows % tr == 0 and tr * cols * itemsize <= budget:
            best = tr
    return best


def _mm(name, a, b, dims, grid, a_spec, b_spec, out_shape, o_spec, add=None, add_spec=None,
        add_scale=1.0):
    nk = grid[2]
    has_add = add is not None
    out_dtype = out_shape.dtype

    def body(*refs):
        if has_add:
            a_ref, b_ref, add_ref, o_ref = refs[:4]
            rest = refs[4:]
        else:
            a_ref, b_ref, o_ref = refs[:3]
            add_ref = None
            rest = refs[3:]
        prod = _dot(a_ref[...], b_ref[...], dims)

        def finish(acc):
            if has_add:
                acc = acc + add_scale * add_ref[...]
            o_ref[...] = acc.astype(out_dtype)

        if nk == 1:
            finish(prod)
        else:
            acc_ref = rest[0]
            k = pl.program_id(2)

            @pl.when(k == 0)
            def _():
                acc_ref[...] = prod

            @pl.when(k > 0)
            def _():
                acc_ref[...] += prod

            @pl.when(k == nk - 1)
            def _():
                finish(acc_ref[...])

    in_specs = [a_spec, b_spec] + ([add_spec] if has_add else [])
    args = [a, b] + ([add] if has_add else [])
    scratch = []
    if nk > 1:
        blk = [d for d in o_spec.block_shape if d is not None]
        scratch = [pltpu.VMEM(tuple(blk), F32)]
    return _pc(body, name=name, out_shape=out_shape, grid=grid, in_specs=in_specs,
               out_specs=o_spec, scratch=scratch,
               sem=("parallel", "parallel", "arbitrary"))(*args)


def _mm_nn_stacked(name, a, w_st, tm):
    t, k = a.shape
    _, _, c = w_st.shape
    return _mm(name, a, w_st, NN, (t // tm, N_CHIP, 1),
               pl.BlockSpec((tm, k), lambda i, j, kk: (i, 0)),
               pl.BlockSpec((None, k, c), lambda i, j, kk: (j, 0, 0)),
               jax.ShapeDtypeStruct((t, N_CHIP * c), F32),
               pl.BlockSpec((tm, c), lambda i, j, kk: (i, j)))


def _mm_tn(name, a, b, tm, tn, tk, stacked=False):
    t, m = a.shape
    _, n = b.shape
    if stacked:
        assert tm == m
        out_shape = jax.ShapeDtypeStruct((n // tn, m, tn), F32)
        o_spec = pl.BlockSpec((None, tm, tn), lambda i, j, kk: (j, 0, 0))
    else:
        out_shape = jax.ShapeDtypeStruct((m, n), F32)
        o_spec = pl.BlockSpec((tm, tn), lambda i, j, kk: (i, j))
    return _mm(name, a, b, TN, (m // tm, n // tn, t // tk),
               pl.BlockSpec((tk, tm), lambda i, j, kk: (kk, i)),
               pl.BlockSpec((tk, tn), lambda i, j, kk: (kk, j)),
               out_shape, o_spec)


def _sgu_mixed(v, wm_ref, bsb_ref, gv, bv):
    gl, dgl = _gelu_and_grad(v)
    vh, rstd = _ln_stats(gl)
    vn = vh * gv + bv
    mixed = []
    for g in range(N_GROUP):
        sl = slice(g * 128, (g + 1) * 128)
        mixed.append(_dot(wm_ref[g], vn[:, sl]) + bsb_ref[g])
    return dgl, vh, rstd, vn, mixed


def _sgu_fwd(h, wm, bsb, gv, bv):
    t = h.shape[0]

    def body(u_ref, v_ref, wm_ref, bsb_ref, gv_ref, bv_ref, ya_ref):
        u = u_ref[...]
        _, _, _, _, mixed = _sgu_mixed(v_ref[...], wm_ref, bsb_ref, gv_ref[...], bv_ref[...])
        gu = _gelu(u)
        for g in range(N_GROUP):
            sl = slice(g * 128, (g + 1) * 128)
            ya_ref[:, sl] = (gu[:, sl] * mixed[g]).astype(BF16)

    full3 = pl.BlockSpec((N_GROUP, 128, 128), lambda i: (0, 0, 0))
    vec = pl.BlockSpec((1, D_MODEL), lambda i: (0, 0))
    return _pc(body, name="sgu_fwd", out_shape=jax.ShapeDtypeStruct((t, D_MODEL), BF16),
               grid=(t // SGU_BLOCK,),
               in_specs=[pl.BlockSpec((SGU_BLOCK, D_MODEL), lambda i: (i, 0)),
                         pl.BlockSpec((SGU_BLOCK, D_MODEL), lambda i: (i, 1)),
                         full3, full3, vec, vec],
               out_specs=pl.BlockSpec((SGU_BLOCK, D_MODEL), lambda i: (i, 0)),
               sem=("parallel",))(h, h, wm, bsb, gv, bv)


def _sgu_bwd(h, dya, wm, wmt, bsb, gv, bv, maskf):
    t = h.shape[0]
    nb = t // SGU_BLOCK

    def body(u_ref, v_ref, dya_ref, wm_ref, wmt_ref, bsb_ref, gv_ref, bv_ref, mask_ref,
             dh_ref, dws_ref, dbs_ref, dgv_ref, dbv_ref, dmix_acc):
        i = pl.program_id(0)

        @pl.when(i == 0)
        def _():
            dws_ref[...] = jnp.zeros_like(dws_ref)
            dgv_ref[...] = jnp.zeros_like(dgv_ref)
            dbv_ref[...] = jnp.zeros_like(dbv_ref)
            dmix_acc[...] = jnp.zeros_like(dmix_acc)

        u = u_ref[...]
        gvv = gv_ref[...]
        dgl_v, vh, rstd, vn, mixed = _sgu_mixed(v_ref[...], wm_ref, bsb_ref, gvv, bv_ref[...])
        gu, dgl_u = _gelu_and_grad(u)
        dya_v = dya_ref[...]
        dvn_parts = []
        for g in range(N_GROUP):
            sl = slice(g * 128, (g + 1) * 128)
            d_y = dya_v[:, sl]
            dh_ref[:, sl] = (d_y * mixed[g] * dgl_u[:, sl]).astype(BF16)
            d_mixed = d_y * gu[:, sl]
            dmix_acc[g] += d_mixed
            dws_ref[g] += _dot(d_mixed, vn[:, sl], NT) * mask_ref[...]
            dvn_parts.append(_dot(wmt_ref[g], d_mixed))
        dvn = jnp.concatenate(dvn_parts, axis=1)
        dgv_ref[...] += _colsum8(dvn * vh)
        dbv_ref[...] += _colsum8(dvn)
        d_gl = _ln_bwd(dvn * gvv, vh, rstd)
        dh_ref[:, D_MODEL:] = (d_gl * dgl_v).astype(BF16)

        @pl.when(i == nb - 1)
        def _():
            rowid = lax.broadcasted_iota(jnp.int32, (8, 128), 0)
            ones = jnp.ones((8, 128), F32)
            acc = jnp.zeros((8, 128), F32)
            for g in range(N_GROUP):
                rs = _dot32(ones, dmix_acc[g], NT)
                acc = jnp.where(rowid == g, rs, acc)
            dbs_ref[...] = acc

    full3 = pl.BlockSpec((N_GROUP, 128, 128), lambda i: (0, 0, 0))
    vec = pl.BlockSpec((1, D_MODEL), lambda i: (0, 0))
    acc8 = pl.BlockSpec((8, D_MODEL), lambda i: (0, 0))
    return _pc(body, name="sgu_bwd",
               out_shape=(jax.ShapeDtypeStruct((t, 2 * D_MODEL), BF16),
                          jax.ShapeDtypeStruct((N_GROUP, 128, 128), F32),
                          jax.ShapeDtypeStruct((8, 128), F32),
                          jax.ShapeDtypeStruct((8, D_MODEL), F32),
                          jax.ShapeDtypeStruct((8, D_MODEL), F32)),
               grid=(nb,),
               in_specs=[pl.BlockSpec((SGU_BLOCK, D_MODEL), lambda i: (i, 0)),
                         pl.BlockSpec((SGU_BLOCK, D_MODEL), lambda i: (i, 1)),
                         pl.BlockSpec((SGU_BLOCK, D_MODEL), lambda i: (i, 0)),
                         full3, full3, full3, vec, vec,
                         pl.BlockSpec((128, 128), lambda i: (0, 0))],
               out_specs=(pl.BlockSpec((SGU_BLOCK, 2 * D_MODEL), lambda i: (i, 0)),
                          full3, pl.BlockSpec((8, 128), lambda i: (0, 0)), acc8, acc8),
               scratch=[pltpu.VMEM((N_GROUP, 128, 128), F32)],
               sem=("arbitrary",))(h, h, dya, wm, wmt, bsb, gv, bv, maskf)


def _tri_masks():
    row = lax.broadcasted_iota(jnp.int32, (CHUNK, CHUNK), 0)
    col = lax.broadcasted_iota(jnp.int32, (CHUNK, CHUNK), 1)
    return col <= row, col >= row


def _lower_bound(logit_ref, sl):
    return _sig(logit_ref[0:1, sl] - logit_ref[1:2, sl])


def _hgrn_chunk(q, fp, ii, lb, st, causal):
    sg = _sig(fp)
    f = lb + (1.0 - lb) * sg
    k = 1.0 - f
    c = _dot32(causal.astype(F32), jnp.log(f))
    ec = jnp.exp(c)
    en = jnp.exp(-c)
    sq = _sig(q)
    qt = q * sq * ec
    kt = k * en
    attn = jnp.where(causal, _dot(qt, kt, NT), 0.0)
    o = _dot(attn, ii) + _dot(qt, st, NT)
    ecl = jnp.exp(c[CHUNK - 1:CHUNK, :])
    kk = kt * ecl
    st_new = st * ecl + _dot(ii, kk, TN)
    return dict(sg=sg, f=f, k=k, ec=ec, en=en, sq=sq, qt=qt, kt=kt, attn=attn, o=o, ecl=ecl,
                kk=kk, st_new=st_new)


def _hgrn_fwd(h, logits, gn):
    t = h.shape[0]
    nc = t // CHUNK

    def body(q_ref, f_ref, i_ref, og_ref, lg_ref, gn_ref, yb_ref, st_ref, state):
        ci = pl.program_id(0)

        @pl.when(ci == 0)
        def _():
            state[...] = jnp.zeros_like(state)

        causal, _ = _tri_masks()
        for hd in range(N_HEAD):
            sl = slice(hd * HEAD_DIM, (hd + 1) * HEAD_DIM)
            st = state[hd]
            st_ref[0, hd] = st
            r = _hgrn_chunk(q_ref[:, sl], f_ref[:, sl], i_ref[:, sl], _lower_bound(lg_ref, sl),
                            st, causal)
            state[hd] = r["st_new"]
            o = r["o"]
            on = o * lax.rsqrt(jnp.mean(o * o, axis=-1, keepdims=True) + RMS_EPS)
            og = og_ref[:, sl]
            yb_ref[:, sl] = (on * gn_ref[:, sl] * (og * _sig(og))).astype(BF16)

    def col(k):
        return pl.BlockSpec((CHUNK, D_MODEL), lambda ci: (ci, k))

    return _pc(body, name="hgrn_fwd",
               out_shape=(jax.ShapeDtypeStruct((t, D_MODEL), BF16),
                          jax.ShapeDtypeStruct((nc, N_HEAD, HEAD_DIM, HEAD_DIM), F32)),
               grid=(nc,),
               in_specs=[col(2), col(3), col(4), col(5),
                         pl.BlockSpec((2, D_MODEL), lambda ci: (0, 0)),
                         pl.BlockSpec((1, D_MODEL), lambda ci: (0, 0))],
               out_specs=(pl.BlockSpec((CHUNK, D_MODEL), lambda ci: (ci, 0)),
                          pl.BlockSpec((1, N_HEAD, HEAD_DIM, HEAD_DIM), lambda ci: (ci, 0, 0, 0))),
               scratch=[pltpu.VMEM((N_HEAD, HEAD_DIM, HEAD_DIM), F32)],
               sem=("arbitrary",))(h, h, h, h, logits, gn)


def _hgrn_bwd(h, dyb, st_all, logits, gn):
    t = h.shape[0]
    nc = t // CHUNK

    def body(q_ref, f_ref, i_ref, og_ref, dyb_ref, st_ref, lg_ref, gn_ref,
             dh1_ref, dh2_ref, dlb_ref, dgn_ref, dstate):
        ci = pl.program_id(0)

        @pl.when(ci == 0)
        def _():
            dstate[...] = jnp.zeros_like(dstate)
            dlb_ref[...] = jnp.zeros_like(dlb_ref)
            dgn_ref[...] = jnp.zeros_like(dgn_ref)

        causal, anti = _tri_masks()
        rowid = lax.broadcasted_iota(jnp.int32, (CHUNK, HEAD_DIM), 0)
        for hd in range(N_HEAD):
            sl = slice(hd * HEAD_DIM, (hd + 1) * HEAD_DIM)
            q = q_ref[:, sl]
            ii = i_ref[:, sl]
            og = og_ref[:, sl]
            lb = _lower_bound(lg_ref, sl)
            gnv = gn_ref[:, sl]
            st = st_ref[0, hd]
            dsn = dstate[hd]
            r = _hgrn_chunk(q, f_ref[:, sl], ii, lb, st, causal)
            o = r["o"]
            rinv = lax.rsqrt(jnp.mean(o * o, axis=-1, keepdims=True) + RMS_EPS)
            on = o * rinv
            so = _sig(og)
            sil = og * so
            dy = dyb_ref[:, sl]
            d_og = dy * on * gnv * (so * (1.0 + og * (1.0 - so)))
            dgn_ref[:, sl] += _colsum8(dy * on * sil)
            d_on = dy * gnv * sil
            d_o = rinv * (d_on - on * jnp.mean(d_on * on, axis=-1, keepdims=True))
            d_attn = jnp.where(causal, _dot(d_o, ii, NT), 0.0)
            d_i = _dot(r["attn"], d_o, TN) + _dot(r["kk"], dsn, NT)
            d_qt = _dot(d_attn, r["kt"]) + _dot(d_o, st)
            d_kt = _dot(d_attn, r["qt"], TN)
            d_kk = _dot(ii, dsn)
            dstate[hd] = _dot(d_o, r["qt"], TN) + dsn * r["ecl"]
            d_cl = (r["ecl"] * jnp.sum(st * dsn, axis=0, keepdims=True)
                    + jnp.sum(r["kk"] * d_kk, axis=0, keepdims=True))
            d_k = (d_kk * r["ecl"] + d_kt) * r["en"]
            d_c = (d_qt * r["qt"].astype(BF16).astype(F32) - d_kt * r["kt"].astype(BF16).astype(F32)
                   - d_kk * r["kk"])
            d_c = d_c + jnp.where(rowid == CHUNK - 1, d_cl, 0.0)
            d_lf = _dot32(anti.astype(F32), d_c)
            d_f = d_lf / r["f"] - d_k
            sg = r["sg"]
            dlb_ref[:, sl] += _colsum8(d_f * (1.0 - sg))
            d_fp = d_f * (1.0 - lb) * sg * (1.0 - sg)
            sq = r["sq"]
            d_q = d_qt * r["ec"] * (sq * (1.0 + q * (1.0 - sq)))
            dh1_ref[:, sl] = d_q.astype(BF16)
            dh1_ref[:, D_MODEL + hd * HEAD_DIM:D_MODEL + (hd + 1) * HEAD_DIM] = d_fp.astype(BF16)
            dh2_ref[:, sl] = d_i.astype(BF16)
            dh2_ref[:, D_MODEL + hd * HEAD_DIM:D_MODEL + (hd + 1) * HEAD_DIM] = d_og.astype(BF16)

    def col(k):
        return pl.BlockSpec((CHUNK, D_MODEL), lambda ci: (nc - 1 - ci, k))

    acc8 = pl.BlockSpec((8, D_MODEL), lambda ci: (0, 0))
    pair = pl.BlockSpec((CHUNK, 2 * D_MODEL), lambda ci: (nc - 1 - ci, 0))
    return _pc(body, name="hgrn_bwd",
               out_shape=(jax.ShapeDtypeStruct((t, 2 * D_MODEL), BF16),
                          jax.ShapeDtypeStruct((t, 2 * D_MODEL), BF16),
                          jax.ShapeDtypeStruct((8, D_MODEL), F32),
                          jax.ShapeDtypeStruct((8, D_MODEL), F32)),
               grid=(nc,),
               in_specs=[col(2), col(3), col(4), col(5),
                         pl.BlockSpec((CHUNK, D_MODEL), lambda ci: (nc - 1 - ci, 0)),
                         pl.BlockSpec((1, N_HEAD, HEAD_DIM, HEAD_DIM),
                                      lambda ci: (nc - 1 - ci, 0, 0, 0)),
                         pl.BlockSpec((2, D_MODEL), lambda ci: (0, 0)),
                         pl.BlockSpec((1, D_MODEL), lambda ci: (0, 0))],
               out_specs=(pair, pair, acc8, acc8),
               scratch=[pltpu.VMEM((N_HEAD, HEAD_DIM, HEAD_DIM), F32)],
               sem=("arbitrary",))(h, h, h, h, dyb, st_all, logits, gn)


def _mix_fwd(ya, yb, h, x, wb0, wb1, wo, g1, b1, tm):
    t = x.shape[0]

    def body(ya_ref, yb_ref, ga_ref, gb_ref, x_ref, wb0_ref, wb1_ref, wo_ref, g1_ref, b1_ref,
             r1_ref, a_ref, b_ref, m_ref, x1_ref):
        a = _dot(ya_ref[...], wb0_ref[...])
        b = _dot(yb_ref[...], wb1_ref[...])
        m = _sig(ga_ref[...]) * a + _sig(gb_ref[...]) * b
        r1 = ALPHA * x_ref[...] + _dot(m, wo_ref[...])
        xh, _ = _ln_stats(r1)
        r1_ref[...] = r1
        a_ref[...] = a
        b_ref[...] = b
        m_ref[...] = m.astype(BF16)
        x1_ref[...] = (xh * g1_ref[...] + b1_ref[...]).astype(BF16)

    tile = pl.BlockSpec((tm, D_MODEL), lambda i: (i, 0))
    wsp = pl.BlockSpec((D_MODEL, D_MODEL), lambda i: (0, 0))
    vec = pl.BlockSpec((1, D_MODEL), lambda i: (0, 0))
    f32o = jax.ShapeDtypeStruct((t, D_MODEL), F32)
    bfo = jax.ShapeDtypeStruct((t, D_MODEL), BF16)
    return _pc(body, name="mix_fwd", out_shape=(f32o, f32o, f32o, bfo, bfo), grid=(t // tm,),
               in_specs=[tile, tile,
                         pl.BlockSpec((tm, D_MODEL), lambda i: (i, 6)),
                         pl.BlockSpec((tm, D_MODEL), lambda i: (i, 7)),
                         tile, wsp, wsp, wsp, vec, vec],
               out_specs=(tile, tile, tile, tile, tile),
               sem=("parallel",))(ya, yb, h, h, x, wb0, wb1, wo, g1, b1)


def _mix_bwd(dr1, h, a, b, wo, wb0, wb1, tm):
    t = dr1.shape[0]

    def body(dr1_ref, ga_ref, gb_ref, a_ref, b_ref, wo_ref, wb0_ref, wb1_ref,
             da_ref, db_ref, dh3_ref, dya_ref, dyb_ref):
        d_m = _dot(dr1_ref[...], wo_ref[...], NT)
        sa = _sig(ga_ref[...])
        sb = _sig(gb_ref[...])
        d_a = (d_m * sa).astype(BF16)
        d_b = (d_m * sb).astype(BF16)
        da_ref[...] = d_a
        db_ref[...] = d_b
        dh3_ref[:, :D_MODEL] = (d_m * a_ref[...] * sa * (1.0 - sa)).astype(BF16)
        dh3_ref[:, D_MODEL:] = (d_m * b_ref[...] * sb * (1.0 - sb)).astype(BF16)
        dya_ref[...] = _dot(d_a, wb0_ref[...], NT)
        dyb_ref[...] = _dot(d_b, wb1_ref[...], NT)

    tile = pl.BlockSpec((tm, D_MODEL), lambda i: (i, 0))
    wsp = pl.BlockSpec((D_MODEL, D_MODEL), lambda i: (0, 0))
    f32o = jax.ShapeDtypeStruct((t, D_MODEL), F32)
    bfo = jax.ShapeDtypeStruct((t, D_MODEL), BF16)
    return _pc(body, name="mix_bwd",
               out_shape=(bfo, bfo, jax.ShapeDtypeStruct((t, 2 * D_MODEL), BF16), f32o, f32o),
               grid=(t // tm,),
               in_specs=[tile,
                         pl.BlockSpec((tm, D_MODEL), lambda i: (i, 6)),
                         pl.BlockSpec((tm, D_MODEL), lambda i: (i, 7)),
                         tile, tile, wsp, wsp, wsp],
               out_specs=(tile, tile, pl.BlockSpec((tm, 2 * D_MODEL), lambda i: (i, 0)),
                          tile, tile),
               sem=("parallel",))(dr1, h, h, a, b, wo, wb0, wb1)


FF_TILE = 1408
FF_NJ = D_FF // FF_TILE


def _shift_down(v, k):
    return pltpu.roll(v, k, 0)


def _shift_up(v, k):
    return pltpu.roll(v, v.shape[0] - k, 0)


def _conv_gate(ext, cw_ref, cb_ref):
    return (cw_ref[0:1, :] * _shift_down(ext, 2) + cw_ref[1:2, :] * _shift_down(ext, 1)
            + cw_ref[2:3, :] * ext + cb_ref[...])


def _ffn_act_fwd(h2, convw, convb, tm):
    t = h2.shape[0]
    nt8 = tm // 8

    def body(g_ref, gp_ref, v_ref, cw_ref, cb_ref, act_ref):
        i = pl.program_id(1)
        prev = gp_ref[...] * (i > 0).astype(F32)
        ext = jnp.concatenate([prev, g_ref[...]], axis=0)
        gc = _conv_gate(ext, cw_ref, cb_ref)[8:, :]
        act_ref[...] = (_gelu(gc) * v_ref[...]).astype(BF16)

    return _pc(body, name="ffn_act_fwd", out_shape=jax.ShapeDtypeStruct((t, D_FF), BF16),
               grid=(FF_NJ, t // tm),
               in_specs=[pl.BlockSpec((tm, FF_TILE), lambda j, i: (i, j)),
                         pl.BlockSpec((8, FF_TILE), lambda j, i: (jnp.maximum(i * nt8 - 1, 0), j)),
                         pl.BlockSpec((tm, FF_TILE), lambda j, i: (i, j + FF_NJ)),
                         pl.BlockSpec((3, FF_TILE), lambda j, i: (0, j)),
                         pl.BlockSpec((1, FF_TILE), lambda j, i: (0, j))],
               out_specs=pl.BlockSpec((tm, FF_TILE), lambda j, i: (i, j)),
               sem=("parallel", "parallel"))(h2, h2, h2, convw, convb)


def _ffn_act_bwd(h2, dact, convw, convb, tm):
    t = h2.shape[0]
    nt8 = tm // 8
    ni = t // tm
    last8 = t // 8 - 1

    def body(g_ref, gp_ref, gn_ref, v_ref, vn_ref, da_ref, dan_ref, cw_ref, cb_ref,
             dh2_ref, dcw_ref, dcb_ref):
        i = pl.program_id(1)

        @pl.when(i == 0)
        def _():
            dcw_ref[...] = jnp.zeros_like(dcw_ref)
            dcb_ref[...] = jnp.zeros_like(dcb_ref)

        prev = gp_ref[...] * (i > 0).astype(F32)
        ext = jnp.concatenate([prev, g_ref[...], gn_ref[...]], axis=0)
        vext = jnp.concatenate([jnp.zeros((8, FF_TILE), F32), v_ref[...], vn_ref[...]], axis=0)
        dnext = dan_ref[...] * (i < ni - 1).astype(F32)
        dext = jnp.concatenate([jnp.zeros((8, FF_TILE), F32), da_ref[...], dnext], axis=0)
        g2 = _shift_down(ext, 2)
        g1 = _shift_down(ext, 1)
        gc = cw_ref[0:1, :] * g2 + cw_ref[1:2, :] * g1 + cw_ref[2:3, :] * ext + cb_ref[...]
        gl, dgl = _gelu_and_grad(gc)
        d_gc = dext * vext * dgl
        d_gate = (cw_ref[2:3, :] * d_gc + cw_ref[1:2, :] * _shift_up(d_gc, 1)
                  + cw_ref[0:1, :] * _shift_up(d_gc, 2))
        dh2_ref[0] = d_gate[8:8 + tm, :].astype(BF16)
        dh2_ref[1] = (da_ref[...] * gl[8:8 + tm, :]).astype(BF16)
        dm = d_gc[8:8 + tm, :]
        s0 = jnp.sum(dm * g2[8:8 + tm, :], axis=0, keepdims=True)
        s1 = jnp.sum(dm * g1[8:8 + tm, :], axis=0, keepdims=True)
        s2 = jnp.sum(dm * ext[8:8 + tm, :], axis=0, keepdims=True)
        rowid = lax.broadcasted_iota(jnp.int32, (8, FF_TILE), 0)
        dcw_ref[...] += jnp.where(rowid == 0, s0, jnp.where(rowid == 1, s1,
                                                            jnp.where(rowid == 2, s2, 0.0)))
        dcb_ref[...] += _colsum8(dm)

    def prev8(off):
        return pl.BlockSpec((8, FF_TILE), lambda j, i: (jnp.maximum(i * nt8 - 1, 0), j + off))

    def next8(off):
        return pl.BlockSpec((8, FF_TILE), lambda j, i: (jnp.minimum((i + 1) * nt8, last8), j + off))

    def main(off):
        return pl.BlockSpec((tm, FF_TILE), lambda j, i: (i, j + off))

    acc = pl.BlockSpec((8, FF_TILE), lambda j, i: (0, j))
    return _pc(body, name="ffn_act_bwd",
               out_shape=(jax.ShapeDtypeStruct((2, t, D_FF), BF16),
                          jax.ShapeDtypeStruct((8, D_FF), F32),
                          jax.ShapeDtypeStruct((8, D_FF), F32)),
               grid=(FF_NJ, ni),
               in_specs=[main(0), prev8(0), next8(0), main(FF_NJ), next8(FF_NJ),
                         pl.BlockSpec((tm, FF_TILE), lambda j, i: (i, j)),
                         pl.BlockSpec((8, FF_TILE), lambda j, i: (jnp.minimum((i + 1) * nt8, last8), j)),
                         pl.BlockSpec((3, FF_TILE), lambda j, i: (0, j)),
                         pl.BlockSpec((1, FF_TILE), lambda j, i: (0, j))],
               out_specs=(pl.BlockSpec((2, tm, FF_TILE), lambda j, i: (0, i, j)), acc, acc),
               sem=("parallel", "arbitrary"))(h2, h2, h2, h2, h2, dact, dact, convw, convb)


def _out_fwd_bwd(act, x1b, r1, p2, tgt, wd, wpg, wpp, g1, b1, g2, b2, tm):
    t = r1.shape[0]

    def body(act_ref, x1b_ref, r1_ref, p_ref, tgt_ref, wd_ref, wpg_ref, wpp_ref,
             g1_ref, b1_ref, g2_ref, b2_ref,
             dr2_ref, dpg_ref, dpp_ref, loss_ref, dg2_ref, db2_ref):
        i = pl.program_id(0)

        @pl.when(i == 0)
        def _():
            loss_ref[...] = jnp.zeros_like(loss_ref)
            dg2_ref[...] = jnp.zeros_like(dg2_ref)
            db2_ref[...] = jnp.zeros_like(db2_ref)

        ffn = _dot(act_ref[...], wd_ref[...])
        pg = _dot(x1b_ref[...], wpg_ref[...])
        pp = _dot(p_ref[...], wpp_ref[...])
        s = _sig(pg)
        xh1, _ = _ln_stats(r1_ref[...])
        x1 = xh1 * g1_ref[...] + b1_ref[...]
        r2 = ALPHA * x1 + ffn + s * pp
        xh2, rstd2 = _ln_stats(r2)
        g2v = g2_ref[...]
        diff = xh2 * g2v + b2_ref[...] - tgt_ref[...]
        part = jnp.sum(jnp.sum(diff * diff, axis=1, keepdims=True), axis=0, keepdims=True)
        loss_ref[...] += jnp.broadcast_to(part * (0.5 / D_MODEL), loss_ref.shape)
        dy = diff * (1.0 / D_MODEL)
        dg2_ref[...] += _colsum8(dy * xh2)
        db2_ref[...] += _colsum8(dy)
        dr2 = _ln_bwd(dy * g2v, xh2, rstd2)
        dr2_ref[...] = dr2
        dpg_ref[...] = (dr2 * pp * s * (1.0 - s)).astype(BF16)
        dpp_ref[...] = (dr2 * s).astype(BF16)

    tile = pl.BlockSpec((tm, D_MODEL), lambda i: (i, 0))
    vec = pl.BlockSpec((1, D_MODEL), lambda i: (0, 0))
    acc8 = pl.BlockSpec((8, D_MODEL), lambda i: (0, 0))
    acc_shape = jax.ShapeDtypeStruct((8, D_MODEL), F32)
    return _pc(body, name="out_fwd_bwd",
               out_shape=(jax.ShapeDtypeStruct((t, D_MODEL), F32),
                          jax.ShapeDtypeStruct((t, D_MODEL), BF16),
                          jax.ShapeDtypeStruct((t, D_MODEL), BF16),
                          acc_shape, acc_shape, acc_shape),
               grid=(t // tm,),
               in_specs=[pl.BlockSpec((tm, D_FF), lambda i: (i, 0)), tile, tile,
                         pl.BlockSpec((tm, PLE_DIM), lambda i: (i, 0)), tile,
                         pl.BlockSpec((D_FF, D_MODEL), lambda i: (0, 0)),
                         pl.BlockSpec((D_MODEL, D_MODEL), lambda i: (0, 0)),
                         pl.BlockSpec((PLE_DIM, D_MODEL), lambda i: (0, 0)),
                         vec, vec, vec, vec],
               out_specs=(tile, tile, tile, acc8, acc8, acc8),
               sem=("arbitrary",))(act, x1b, r1, p2, tgt, wd, wpg, wpp, g1, b1, g2, b2)


def _ffn_in_bwd(dh2, wup_st, dpg, wpg, dr2, r1, g1, tm):
    t = r1.shape[0]
    ni = t // tm

    def body(dh2_ref, wup_ref, dpg_ref, wpg_ref, dr2_ref, r1_ref, g1_ref,
             dr1_ref, dg1_ref, db1_ref, acc):
        i = pl.program_id(0)
        j = pl.program_id(1)

        @pl.when((i == 0) & (j == 0))
        def _():
            dg1_ref[...] = jnp.zeros_like(dg1_ref)
            db1_ref[...] = jnp.zeros_like(db1_ref)

        @pl.when(j == 0)
        def _():
            acc[...] = _dot(dh2_ref[...], wup_ref[...], NT)

        @pl.when(j > 0)
        def _():
            acc[...] += _dot(dh2_ref[...], wup_ref[...], NT)

        @pl.when(j == N_CHIP - 1)
        def _():
            d_x1 = acc[...] + _dot(dpg_ref[...], wpg_ref[...], NT) + ALPHA * dr2_ref[...]
            xh, rstd = _ln_stats(r1_ref[...])
            dg1_ref[...] += _colsum8(d_x1 * xh)
            db1_ref[...] += _colsum8(d_x1)
            dr1_ref[...] = _ln_bwd(d_x1 * g1_ref[...], xh, rstd)

    tile = pl.BlockSpec((tm, D_MODEL), lambda i, j: (i, 0))
    acc8 = pl.BlockSpec((8, D_MODEL), lambda i, j: (0, 0))
    acc_shape = jax.ShapeDtypeStruct((8, D_MODEL), F32)
    return _pc(body, name="ffn_in_bwd",
               out_shape=(jax.ShapeDtypeStruct((t, D_MODEL), F32), acc_shape, acc_shape),
               grid=(ni, N_CHIP),
               in_specs=[pl.BlockSpec((None, tm, FF_TILE), lambda i, j: (j // FF_NJ, i, j % FF_NJ)),
                         pl.BlockSpec((None, D_MODEL, FF_TILE), lambda i, j: (j, 0, 0)),
                         tile, pl.BlockSpec((D_MODEL, D_MODEL), lambda i, j: (0, 0)),
                         tile, tile, pl.BlockSpec((1, D_MODEL), lambda i, j: (0, 0))],
               out_specs=(tile, acc8, acc8),
               scratch=[pltpu.VMEM((tm, D_MODEL), F32)],
               sem=("arbitrary", "arbitrary"))(dh2, wup_st, dpg, wpg, dr2, r1, g1)


ANY = pl.BlockSpec(memory_space=pl.ANY)


def _chip_peers():
    x, y, c = lax.axis_index("x"), lax.axis_index("y"), lax.axis_index("c")
    return x, y, c, [(1 - x, y), (x, 1 - y), (1 - x, 1 - y)]


def _gather_weights(halved, whole):
    n, nw = len(halved), len(whole)

    def body(*refs):
        ins, wins = refs[:n], refs[n:n + nw]
        outs, wouts = refs[n + nw:2 * n + nw], refs[2 * n + nw:2 * (n + nw)]
        ici_send, ici_recv, d2d_send, d2d_recv, own_send, own_recv = refs[2 * (n + nw):]
        x, y, c, peers = _chip_peers()
        me = 2 * x + y
        sibling = (x, y, 1 - c)
        sends = []
        for ti in range(n + nw):
            src, dst = (ins[ti], outs[ti]) if ti < n else (wins[ti - n], wouts[ti - n])
            own = pltpu.make_async_remote_copy(
                src_ref=src, dst_ref=dst.at[me], send_sem=own_send.at[ti], recv_sem=own_recv.at[ti],
                device_id=sibling, device_id_type=MESH)
            own.start()
            sends.append(own)
            for k, (px, py) in enumerate(peers):
                cp = pltpu.make_async_remote_copy(
                    src_ref=src.at[c] if ti < n else src,
                    dst_ref=dst.at[me, c] if ti < n else dst.at[me],
                    send_sem=ici_send.at[ti * 3 + k], recv_sem=ici_recv.at[ti * 3 + k],
                    device_id=(px, py, c), device_id_type=MESH)
                cp.start()
                sends.append(cp)
        for ti in range(n + nw):
            src, dst = (ins[ti], outs[ti]) if ti < n else (wins[ti - n], wouts[ti - n])
            for k, (px, py) in enumerate(peers):
                pk = 2 * px + py
                pltpu.make_async_remote_copy(
                    src_ref=src.at[c] if ti < n else src,
                    dst_ref=dst.at[pk, c] if ti < n else dst.at[pk],
                    send_sem=ici_send.at[ti * 3 + k], recv_sem=ici_recv.at[ti * 3 + k],
                    device_id=(px, py, c), device_id_type=MESH).wait_recv()
                if ti < n:
                    fw = pltpu.make_async_remote_copy(
                        src_ref=dst.at[pk, c], dst_ref=dst.at[pk, c],
                        send_sem=d2d_send.at[ti * 3 + k], recv_sem=d2d_recv.at[ti * 3 + k],
                        device_id=sibling, device_id_type=MESH)
                    fw.start()
                    sends.append(fw)
        for ti in range(n):
            for k, (px, py) in enumerate(peers):
                pk = 2 * px + py
                pltpu.make_async_remote_copy(
                    src_ref=outs[ti].at[pk, 1 - c], dst_ref=outs[ti].at[pk, 1 - c],
                    send_sem=d2d_send.at[ti * 3 + k], recv_sem=d2d_recv.at[ti * 3 + k],
                    device_id=sibling, device_id_type=MESH).wait_recv()
        for ti in range(n + nw):
            src, dst = (ins[ti], outs[ti]) if ti < n else (wins[ti - n], wouts[ti - n])
            pltpu.make_async_remote_copy(
                src_ref=src, dst_ref=dst.at[me], send_sem=own_send.at[ti], recv_sem=own_recv.at[ti],
                device_id=sibling, device_id_type=MESH).wait_recv()
        for cp in sends:
            cp.wait_send()

    out_shape = tuple(jax.ShapeDtypeStruct((N_CHIP,) + s.shape, s.dtype) for s in list(halved) + list(whole))
    return _pc(body, name="gather_weights", out_shape=out_shape,
               in_specs=[ANY] * (n + nw), out_specs=tuple([ANY] * (n + nw)),
               scratch=[pltpu.SemaphoreType.DMA((3 * (n + nw),)), pltpu.SemaphoreType.DMA((3 * (n + nw),)),
                        pltpu.SemaphoreType.DMA((3 * n,)), pltpu.SemaphoreType.DMA((3 * n,)),
                        pltpu.SemaphoreType.DMA((n + nw,)), pltpu.SemaphoreType.DMA((n + nw,))])(*halved, *whole)


def _rs_sibling_exchange(grads):
    n = len(grads)

    def body(*refs):
        ins, outs = refs[:n], refs[n:2 * n]
        send_sems, recv_sems = refs[2 * n:]
        x, y, c = lax.axis_index("x"), lax.axis_index("y"), lax.axis_index("c")
        sends = []
        for ti in range(n):
            half = ins[ti].shape[1] // 2
            cp = pltpu.make_async_remote_copy(
                src_ref=ins[ti].at[:, pl.ds(pl.multiple_of((1 - c) * half, 8), half), :],
                dst_ref=outs[ti],
                send_sem=send_sems.at[ti], recv_sem=recv_sems.at[ti],
                device_id=(x, y, 1 - c), device_id_type=MESH)
            cp.start()
            sends.append(cp)
        for cp in sends:
            cp.wait()

    return _pc(body, name="rs_sibling_exchange",
               out_shape=tuple(jax.ShapeDtypeStruct((N_CHIP, g.shape[1] // 2, g.shape[2]), g.dtype)
                               for g in grads),
               in_specs=[ANY] * n, out_specs=tuple([ANY] * n),
               scratch=[pltpu.SemaphoreType.DMA((n,)), pltpu.SemaphoreType.DMA((n,))])(*grads)


def _rs_add_halves(name, grad, recv, core):
    _, r, cdim = grad.shape
    half = r // 2
    tr = _row_tile(half, cdim, mult=16)
    nr = half // tr

    def body(c_ref, g_ref, r_ref, o_ref):
        o_ref[...] = (g_ref[...] + r_ref[...]).astype(BF16)

    return _pc(body, name=name, out_shape=jax.ShapeDtypeStruct((N_CHIP, half, cdim), BF16),
               grid=(N_CHIP, nr), nsp=1,
               in_specs=[pl.BlockSpec((None, tr, cdim), lambda j, i, c_ref: (j, c_ref[0] * nr + i, 0)),
                         pl.BlockSpec((None, tr, cdim), lambda j, i, c_ref: (j, i, 0))],
               out_specs=pl.BlockSpec((None, tr, cdim), lambda j, i, c_ref: (j, i, 0)),
               sem=("parallel", "parallel"))(core, grad, recv)


def _rs_chip_exchange(parts):
    n = len(parts)

    def body(*refs):
        ins, outs = refs[:n], refs[n:2 * n]
        send_sems, recv_sems = refs[2 * n:]
        x, y, c, peers = _chip_peers()
        sends = []
        for ti in range(n):
            for k, (px, py) in enumerate(peers):
                cp = pltpu.make_async_remote_copy(
                    src_ref=ins[ti].at[2 * px + py], dst_ref=outs[ti].at[k],
                    send_sem=send_sems.at[ti * 3 + k], recv_sem=recv_sems.at[ti * 3 + k],
                    device_id=(px, py, c), device_id_type=MESH)
                cp.start()
                sends.append(cp)
        for cp in sends:
            cp.wait()

    return _pc(body, name="rs_chip_exchange",
               out_shape=tuple(jax.ShapeDtypeStruct((3,) + p.shape[1:], p.dtype) for p in parts),
               in_specs=[ANY] * n, out_specs=tuple([ANY] * n),
               scratch=[pltpu.SemaphoreType.DMA((3 * n,)), pltpu.SemaphoreType.DMA((3 * n,))])(*parts)


def _rs_sum_chips(name, part, recv, chip):
    _, half, cdim = recv.shape
    tr = _row_tile(half, cdim, mult=16)

    def body(chip_ref, p_ref, r_ref, o_ref):
        o_ref[...] = ((p_ref[...].astype(F32) + r_ref[0].astype(F32)) + r_ref[1].astype(F32)
                      ) + r_ref[2].astype(F32)

    return _pc(body, name=name, out_shape=jax.ShapeDtypeStruct((half, cdim), F32),
               grid=(half // tr,), nsp=1,
               in_specs=[pl.BlockSpec((None, tr, cdim), lambda i, chip_ref: (chip_ref[0], i, 0)),
                         pl.BlockSpec((3, tr, cdim), lambda i, chip_ref: (0, i, 0))],
               out_specs=pl.BlockSpec((tr, cdim), lambda i, chip_ref: (i, 0)),
               sem=("parallel",))(chip, part, recv)


def _rs_send_halves(halves):
    n = len(halves)

    def body(*refs):
        ins, outs = refs[:n], refs[n:2 * n]
        send_sems, recv_sems = refs[2 * n:]
        x, y, c = lax.axis_index("x"), lax.axis_index("y"), lax.axis_index("c")
        sends = []
        for ti in range(n):
            cp = pltpu.make_async_remote_copy(
                src_ref=ins[ti], dst_ref=outs[ti],
                send_sem=send_sems.at[ti], recv_sem=recv_sems.at[ti],
                device_id=(x, y, 1 - c), device_id_type=MESH)
            cp.start()
            sends.append(cp)
        for cp in sends:
            cp.wait()

    return _pc(body, name="rs_send_halves",
               out_shape=tuple(jax.ShapeDtypeStruct(hv.shape, hv.dtype) for hv in halves),
               in_specs=[ANY] * n, out_specs=tuple([ANY] * n),
               scratch=[pltpu.SemaphoreType.DMA((n,)), pltpu.SemaphoreType.DMA((n,))])(*halves)


def _adamw_rows(name, mine, theirs, w, m, v, core):
    half, cdim = mine.shape
    tr = _row_tile(half, cdim, budget=1 << 19)
    nrh = half // tr

    def body(c_ref, mine_ref, theirs_ref, w_ref, m_ref, v_ref, g_ref, d_ref, m2_ref, v2_ref):
        is_mine = (pl.program_id(0) // nrh) == c_ref[0]
        g = jnp.where(is_mine, mine_ref[...], theirs_ref[...])
        d, m2, v2 = _adamw(w_ref[...], g, m_ref[...], v_ref[...])
        g_ref[...] = g
        d_ref[...] = d
        m2_ref[...] = m2
        v2_ref[...] = v2

    htile = pl.BlockSpec((tr, cdim), lambda i, c_ref: (i % nrh, 0))
    tile = pl.BlockSpec((tr, cdim), lambda i, c_ref: (i, 0))
    shp = jax.ShapeDtypeStruct((2 * half, cdim), F32)
    return _pc(body, name=name, out_shape=(shp, shp, shp, shp), grid=(2 * nrh,), nsp=1,
               in_specs=[htile, htile, tile, tile, tile], out_specs=(tile, tile, tile, tile),
               sem=("parallel",))(core, mine, theirs, w, m, v)


def _adamw_whole(name, g, w, m, v):
    def body(g_ref, w_ref, m_ref, v_ref, d_ref, m2_ref, v2_ref):
        d, m2, v2 = _adamw(w_ref[...], g_ref[...], m_ref[...], v_ref[...])
        d_ref[...] = d
        m2_ref[...] = m2
        v2_ref[...] = v2

    shp = jax.ShapeDtypeStruct(g.shape, F32)
    return _pc(body, name=name, out_shape=(shp, shp, shp))(g, w, m, v)


SMALL_LAYOUT = (
    ("sgu_w_s", 1024, 1, 0),
    ("sgu_b_s", 8, 1, 1024),
    ("sgu_norm_g", 1, 0, 0),
    ("sgu_norm_b", 1, 0, 8),
    ("hgrn_norm_g", 1, 0, 24),
    ("ln1_g", 1, 0, 32),
    ("ln1_b", 1, 0, 40),
    ("ffn_conv_b", 1, 2, 8),
    ("ln2_g", 1, 0, 48),
    ("ln2_b", 1, 0, 56),
)
LB_ROW = 16
LOSS_ROW = 64


def _small_allreduce_adamw(bufs, logits, m_logits, v_logits, small_w, small_m, small_v):
    ns = len(SMALL_LAYOUT)
    nb = len(bufs)

    def body(*refs):
        buf_refs = refs[:nb]
        lg_ref, mlg_ref, vlg_ref = refs[nb:nb + 3]
        pos = nb + 3
        w_refs = refs[pos:pos + ns]
        m_refs = refs[pos + ns:pos + 2 * ns]
        v_refs = refs[pos + 2 * ns:pos + 3 * ns]
        pos += 3 * ns
        loss_ref, dcw_ref = refs[pos:pos + 2]
        lg_outs = refs[pos + 2:pos + 6]
        pos += 6
        outs = refs[pos:pos + 4 * ns]
        pos += 4 * ns
        gath = refs[pos:pos + nb]
        send_sems, recv_sems = refs[pos + nb:pos + nb + 2]

        x, y, c = lax.axis_index("x"), lax.axis_index("y"), lax.axis_index("c")
        me = 4 * x + 2 * y + c
        sends = []
        for d in range(1, N_DEV):
            peer = (x ^ (d >> 2), y ^ ((d >> 1) & 1), c ^ (d & 1))
            for b in range(nb):
                cp = pltpu.make_async_remote_copy(
                    src_ref=buf_refs[b], dst_ref=gath[b].at[me],
                    send_sem=send_sems.at[(d - 1) * nb + b], recv_sem=recv_sems.at[(d - 1) * nb + b],
                    device_id=peer, device_id_type=MESH)
                cp.start()
                sends.append(cp)
        for b in range(nb):
            gath[b][me] = buf_refs[b][...]
        for d in range(1, N_DEV):
            peer = (x ^ (d >> 2), y ^ ((d >> 1) & 1), c ^ (d & 1))
            src = 4 * peer[0] + 2 * peer[1] + peer[2]
            for b in range(nb):
                pltpu.make_async_remote_copy(
                    src_ref=buf_refs[b], dst_ref=gath[b].at[src],
                    send_sem=send_sems.at[(d - 1) * nb + b], recv_sem=recv_sems.at[(d - 1) * nb + b],
                    device_id=peer, device_id_type=MESH).wait_recv()
        for cp in sends:
            cp.wait_send()

        tot = []
        for b in range(nb):
            s = gath[b][0]
            for d in range(1, N_DEV):
                s = s + gath[b][d]
            tot.append(s)

        loss_ref[...] = tot[0][LOSS_ROW:LOSS_ROW + 8, :]
        dcw_ref[...] = tot[2][0:8, :]
        lb = _sig(lg_ref[0:1, :] - lg_ref[1:2, :])
        d0 = tot[0][LB_ROW:LB_ROW + 1, :] * lb * (1.0 - lb)
        rowid = lax.broadcasted_iota(jnp.int32, (2, D_MODEL), 0)
        g_lg = jnp.where(rowid == 0, d0, -d0)
        dl, ml, vl = _adamw(lg_ref[...], g_lg, mlg_ref[...], vlg_ref[...])
        lg_outs[0][...] = g_lg
        lg_outs[1][...] = dl
        lg_outs[2][...] = ml
        lg_outs[3][...] = vl
        for si, (_, rows, b, r0) in enumerate(SMALL_LAYOUT):
            g = tot[b][r0:r0 + rows, :]
            dl, ml, vl = _adamw(w_refs[si][...], g, m_refs[si][...], v_refs[si][...])
            outs[4 * si][...] = g
            outs[4 * si + 1][...] = dl
            outs[4 * si + 2][...] = ml
            outs[4 * si + 3][...] = vl

    shapes = [jax.ShapeDtypeStruct((8, D_MODEL), F32), jax.ShapeDtypeStruct((8, D_FF), F32)]
    shapes += [jax.ShapeDtypeStruct((2, D_MODEL), F32)] * 4
    for w in small_w:
        shapes += [jax.ShapeDtypeStruct(w.shape, F32)] * 4
    scratch = [pltpu.VMEM((N_DEV,) + b.shape, F32) for b in bufs]
    scratch += [pltpu.SemaphoreType.DMA(((N_DEV - 1) * nb,)), pltpu.SemaphoreType.DMA(((N_DEV - 1) * nb,))]
    vm = pl.BlockSpec(memory_space=pltpu.VMEM)
    n_in = nb + 3 + 3 * ns
    res = _pc(body, name="small_allreduce_adamw", out_shape=tuple(shapes),
              in_specs=[vm] * n_in, out_specs=tuple([vm] * len(shapes)),
              scratch=scratch)(*bufs, logits, m_logits, v_logits, *small_w, *small_m, *small_v)
    return res[0], res[1], res[2:6], res[6:]


def kernel(x, p, w_in, sgu_w_s, sgu_b_s, sgu_norm_g, sgu_norm_b, hgrn_lb_logits, hgrn_norm_g, w_branch, w_out, ln1_g, ln1_b, ffn_w_up, ffn_conv_w, ffn_conv_b, ffn_w_down, ln2_g, ln2_b, ple_w_proj, ple_w_gate, loss_target, m_w_in, m_sgu_w_s, m_sgu_b_s, m_sgu_norm_g, m_sgu_norm_b, m_hgrn_lb_logits, m_hgrn_norm_g, m_w_branch, m_w_out, m_ln1_g, m_ln1_b, m_ffn_w_up, m_ffn_conv_w, m_ffn_conv_b, m_ffn_w_down, m_ln2_g, m_ln2_b, m_ple_w_proj, m_ple_w_gate, v_w_in, v_sgu_w_s, v_sgu_b_s, v_sgu_norm_g, v_sgu_norm_b, v_hgrn_lb_logits, v_hgrn_norm_g, v_w_branch, v_w_out, v_ln1_g, v_ln1_b, v_ffn_w_up, v_ffn_conv_w, v_ffn_conv_b, v_ffn_w_down, v_ln2_g, v_ln2_b, v_ple_w_proj, v_ple_w_gate):
    t = x.shape[1]
    x2 = x.reshape(t, D_MODEL)
    p2 = p.reshape(t, PLE_DIM)
    tgt = loss_target.reshape(t, D_MODEL)
    core = lax.axis_index("c").astype(jnp.int32).reshape(1)
    chip_id = (2 * lax.axis_index("x") + lax.axis_index("y")).astype(jnp.int32).reshape(1)

    big_w = [w_in[0], w_branch[0, 0], w_branch[0, 1], w_out[0], ffn_w_up[0], ffn_w_down[0],
             ple_w_proj[0], ple_w_gate[0]]
    big_m = [m_w_in[0], m_w_branch[0, 0], m_w_branch[0, 1], m_w_out[0], m_ffn_w_up[0],
             m_ffn_w_down[0], m_ple_w_proj[0], m_ple_w_gate[0]]
    big_v = [v_w_in[0], v_w_branch[0, 0], v_w_branch[0, 1], v_w_out[0], v_ffn_w_up[0],
             v_ffn_w_down[0], v_ple_w_proj[0], v_ple_w_gate[0]]
    gathered = _gather_weights(
        [w.astype(BF16).reshape(2, w.shape[0] // 2, w.shape[1]) for w in big_w], [ffn_conv_w[0]])
    gathered = [g.reshape(N_CHIP, w.shape[0], w.shape[1]) for g, w in zip(gathered, big_w)] + [gathered[-1]]
    win_st = gathered[0]
    wb0 = gathered[1].reshape(D_MODEL, D_MODEL)
    wb1 = gathered[2].reshape(D_MODEL, D_MODEL)
    wo = gathered[3].reshape(D_MODEL, D_MODEL)
    wup_st = gathered[4]
    wd = gathered[5].reshape(D_FF, D_MODEL)
    wpp = jnp.transpose(gathered[6], (1, 0, 2)).reshape(PLE_DIM, D_MODEL)
    wpg = gathered[7].reshape(D_MODEL, D_MODEL)
    convw = jnp.transpose(gathered[8], (1, 0, 2)).reshape(3, D_FF)

    cid = jnp.arange(SGU_BLOCK) // CHUNK
    maskf = (cid[:, None] >= cid[None, :]).astype(F32)
    ws_masked = sgu_w_s[0] * maskf[None]
    wm = ws_masked.astype(BF16)
    wmt = jnp.transpose(ws_masked, (0, 2, 1)).astype(BF16)
    bsb = jnp.broadcast_to(sgu_b_s[0][:, :, None], (N_GROUP, SGU_BLOCK, 128))

    h = _mm_nn_stacked("in_proj", x2, win_st, 512)
    ya = _sgu_fwd(h, wm, bsb, sgu_norm_g, sgu_norm_b)
    yb, st_all = _hgrn_fwd(h, hgrn_lb_logits, hgrn_norm_g)
    r1, a_br, b_br, m_bf, x1b = _mix_fwd(ya, yb, h, x2, wb0, wb1, wo, ln1_g, ln1_b, 256)
    h2 = _mm_nn_stacked("ffn_up", x1b, wup_st, 512)
    act = _ffn_act_fwd(h2, convw, ffn_conv_b, 256)
    dr2, dpg, dpp, loss_acc, dg2, db2 = _out_fwd_bwd(
        act, x1b, r1, p2, tgt, wd, wpg, wpp, ln1_g, ln1_b, ln2_g, ln2_b, 256)

    dact = _mm("ffn_down_bwd", dr2, wd, NT, (t // 512, FF_NJ, 1),
               pl.BlockSpec((512, D_MODEL), lambda i, j, k: (i, 0)),
               pl.BlockSpec((FF_TILE, D_MODEL), lambda i, j, k: (j, 0)),
               jax.ShapeDtypeStruct((t, D_FF), F32),
               pl.BlockSpec((512, FF_TILE), lambda i, j, k: (i, j)))
    dh2, dcw, dcb = _ffn_act_bwd(h2, dact, convw, ffn_conv_b, 256)
    d_wd = _mm_tn("ffn_down_wgrad", act, dr2, FF_TILE, D_MODEL, 512)
    d_wpg = _mm_tn("ple_gate_wgrad", x1b, dpg, D_MODEL, D_MODEL, 512)
    d_wpp_st = _mm_tn("ple_proj_wgrad", p2, dpp, PLE_DIM, PLE_DIM, 512, stacked=True)
    d_wup_st = _mm("ffn_up_wgrad", x1b, dh2, TN, (1, N_CHIP, t // 512),
                   pl.BlockSpec((512, D_MODEL), lambda i, j, k: (k, 0)),
                   pl.BlockSpec((None, 512, FF_TILE), lambda i, j, k: (j // FF_NJ, k, j % FF_NJ)),
                   jax.ShapeDtypeStruct((N_CHIP, D_MODEL, FF_TILE), F32),
                   pl.BlockSpec((None, D_MODEL, FF_TILE), lambda i, j, k: (j, 0, 0)))
    dr1, dg1, db1 = _ffn_in_bwd(dh2, wup_st, dpg, wpg, dr2, r1, ln1_g, 512)
    da_bf, db_bf, dh3, dya, dyb = _mix_bwd(dr1, h, a_br, b_br, wo, wb0, wb1, 256)
    d_wo = _mm_tn("out_proj_wgrad", m_bf, dr1, D_MODEL, D_MODEL, 512)
    d_wb0 = _mm_tn("branch0_wgrad", ya, da_bf, D_MODEL, D_MODEL, 512)
    d_wb1 = _mm_tn("branch1_wgrad", yb, db_bf, D_MODEL, D_MODEL, 512)
    dh1, dh2h, dlb, dgn = _hgrn_bwd(h, dyb, st_all, hgrn_lb_logits, hgrn_norm_g)
    dh0, dws, dbs, dgv, dbv = _sgu_bwd(h, dya, wm, wmt, bsb, sgu_norm_g, sgu_norm_b, maskf)
    dh_parts = [dh0, dh1, dh2h, dh3]
    d_win = [_mm_tn("in_proj_wgrad%d" % j, x2, dh_parts[j], D_MODEL, 1024, 512) for j in range(4)]
    gx = dr1
    scale = ALPHA
    for j in range(4):
        gx = _mm("in_proj_xgrad%d" % j, dh_parts[j], win_st, NT, (t // 512, 1, 1),
                 pl.BlockSpec((512, 2 * D_MODEL), lambda i, jj, k: (i, 0)),
                 pl.BlockSpec((None, D_MODEL, 2 * D_MODEL), lambda i, jj, k, j=j: (j, 0, 0)),
                 jax.ShapeDtypeStruct((t, D_MODEL), F32),
                 pl.BlockSpec((512, D_MODEL), lambda i, jj, k: (i, 0)),
                 add=gx, add_spec=pl.BlockSpec((512, D_MODEL), lambda i, jj, k: (i, 0)),
                 add_scale=scale)
        scale = 1.0

    grads_st = [jnp.stack(d_win), d_wb0.reshape(4, 256, D_MODEL), d_wb1.reshape(4, 256, D_MODEL),
                d_wo.reshape(4, 256, D_MODEL), d_wup_st, d_wd.reshape(4, D_FF // 4, D_MODEL),
                d_wpp_st, d_wpg.reshape(4, 256, D_MODEL)]
    recv_a = _rs_sibling_exchange(grads_st)
    parts = [_rs_add_halves("rs_add_halves%d" % i, g, r, core)
             for i, (g, r) in enumerate(zip(grads_st, recv_a))]
    recv_b = _rs_chip_exchange(parts)
    halves = [_rs_sum_chips("rs_sum_chips%d" % i, pt, r, chip_id)
              for i, (pt, r) in enumerate(zip(parts, recv_b))]
    theirs = _rs_send_halves(halves)
    big_out = [_adamw_rows("adamw_big%d" % i, halves[i], theirs[i], big_w[i], big_m[i], big_v[i], core)
               for i in range(len(halves))]

    buf0 = jnp.concatenate([dgv, dbv, dlb, dgn, dg1, db1, dg2, db2, loss_acc], axis=0)
    buf1 = jnp.concatenate([dws.reshape(N_GROUP * 128, 128), dbs], axis=0)
    buf2 = jnp.concatenate([dcw, dcb], axis=0)
    small_in = dict(sgu_w_s=(sgu_w_s, m_sgu_w_s, v_sgu_w_s), sgu_b_s=(sgu_b_s, m_sgu_b_s, v_sgu_b_s),
                    sgu_norm_g=(sgu_norm_g, m_sgu_norm_g, v_sgu_norm_g),
                    sgu_norm_b=(sgu_norm_b, m_sgu_norm_b, v_sgu_norm_b),
                    hgrn_norm_g=(hgrn_norm_g, m_hgrn_norm_g, v_hgrn_norm_g),
                    ln1_g=(ln1_g, m_ln1_g, v_ln1_g), ln1_b=(ln1_b, m_ln1_b, v_ln1_b),
                    ffn_conv_b=(ffn_conv_b, m_ffn_conv_b, v_ffn_conv_b),
                    ln2_g=(ln2_g, m_ln2_g, v_ln2_g), ln2_b=(ln2_b, m_ln2_b, v_ln2_b))

    def flat(name, arr):
        rows = dict((n, r) for n, r, _, _ in SMALL_LAYOUT)[name]
        return arr.reshape(rows, arr.size // rows)

    names = [n for n, _, _, _ in SMALL_LAYOUT]
    sw = [flat(n, small_in[n][0]) for n in names]
    sm = [flat(n, small_in[n][1]) for n in names]
    sv = [flat(n, small_in[n][2]) for n in names]
    loss_rows, dcw_tot, lg_out, small_out = _small_allreduce_adamw(
        [buf0, buf1, buf2], hgrn_lb_logits, m_hgrn_lb_logits, v_hgrn_lb_logits, sw, sm, sv)
    loss = loss_rows[0, 0]

    chip = 2 * lax.axis_index("x") + lax.axis_index("y")
    g_cw = lax.dynamic_slice(dcw_tot, (0, chip * (D_FF // 4)), (3, D_FF // 4))
    cw_out = _adamw_whole("adamw_conv_w", g_cw, ffn_conv_w[0], m_ffn_conv_w[0], v_ffn_conv_w[0])

    res = {}
    for si, n in enumerate(names):
        shp = small_in[n][0].shape
        res[n] = tuple(small_out[4 * si + k].reshape(shp) for k in range(4))
    res["hgrn_lb_logits"] = tuple(lg_out)
    res["ffn_conv_w"] = (g_cw[None],) + tuple(o[None] for o in cw_out)

    def big(i):
        return tuple(big_out[i])

    res["w_in"] = tuple(o[None] for o in big(0))
    res["w_branch"] = tuple(jnp.stack([o0, o1])[None] for o0, o1 in zip(big(1), big(2)))
    res["w_out"] = tuple(o[None] for o in big(3))
    res["ffn_w_up"] = tuple(o[None] for o in big(4))
    res["ffn_w_down"] = tuple(o[None] for o in big(5))
    res["ple_w_proj"] = tuple(o[None] for o in big(6))
    res["ple_w_gate"] = tuple(o[None] for o in big(7))

    order = ["w_in", "sgu_w_s", "sgu_b_s", "sgu_norm_g", "sgu_norm_b", "hgrn_lb_logits",
             "hgrn_norm_g", "w_branch", "w_out", "ln1_g", "ln1_b", "ffn_w_up", "ffn_conv_w",
             "ffn_conv_b", "ffn_w_down", "ln2_g", "ln2_b", "ple_w_proj", "ple_w_gate"]
    outs = [loss, gx.reshape(1, t, D_MODEL)]
    for k in range(4):
        outs += [res[n][k] for n in order]
    return tuple(outs)
```

```python
import functools

import jax
import jax.numpy as jnp
from jax import lax
from jax.experimental import pallas as pl
from jax.experimental.pallas import tpu as pltpu

F32 = jnp.float32
BF16 = jnp.bfloat16
HIGHEST = lax.Precision.HIGHEST
MESH = pl.DeviceIdType.MESH

D_MODEL = 1024
CHUNK = 64
SGU_BLOCK = 128
N_GROUP = 8
N_HEAD = 8
HEAD_DIM = 128
D_FF = 2816
PLE_DIM = 256
IN_COLS = 8192
LN_EPS = 1e-5
RMS_EPS = 1e-6
ALPHA = 2.0 ** 0.25
N_CHIP = 4
N_DEV = 8

ADAM_LR = 0.001
ADAM_B1 = 0.9
ADAM_B2 = 0.999
ADAM_EPS = 1e-08
ADAM_WD = 0.01
ADAM_STEP = 10

VMEM_LIMIT = 56 * 1024 * 1024

NN = (((1,), (0,)), ((), ()))
NT = (((1,), (1,)), ((), ()))
TN = (((0,), (0,)), ((), ()))


def _pc(body, *, name, out_shape, grid=None, in_specs=None, out_specs=None, scratch=(),
        sem=None, nsp=0, vmem=VMEM_LIMIT):
    params = dict(vmem_limit_bytes=vmem)
    if sem is not None:
        params["dimension_semantics"] = sem
    kw = dict(name=name, out_shape=out_shape, compiler_params=pltpu.CompilerParams(**params))
    if nsp:
        kw["grid_spec"] = pltpu.PrefetchScalarGridSpec(
            num_scalar_prefetch=nsp, grid=grid, in_specs=in_specs, out_specs=out_specs,
            scratch_shapes=list(scratch))
    else:
        if grid is not None:
            kw["grid"] = grid
        if in_specs is not None:
            kw["in_specs"] = in_specs
            kw["out_specs"] = out_specs
        kw["scratch_shapes"] = list(scratch)
    return pl.pallas_call(body, **kw)


def _dot(a, b, dims=NN):
    return lax.dot_general(a.astype(BF16), b.astype(BF16), dims, preferred_element_type=F32)


def _dot32(a, b, dims=NN):
    return lax.dot_general(a, b, dims, precision=HIGHEST, preferred_element_type=F32)


def _sig(x):
    return 1.0 / (1.0 + jnp.exp(-x))


_GC = 0.7978845608028654
_GA = 0.044715


def _gelu(x):
    return 0.5 * x * (1.0 + jnp.tanh(_GC * (x + _GA * x * x * x)))


def _gelu_and_grad(x):
    t = jnp.tanh(_GC * (x + _GA * x * x * x))
    g = 0.5 * x * (1.0 + t)
    dg = 0.5 * (1.0 + t) + 0.5 * x * (1.0 - t * t) * _GC * (1.0 + 3.0 * _GA * x * x)
    return g, dg


def _ln_stats(r):
    mu = jnp.mean(r, axis=-1, keepdims=True)
    xc = r - mu
    var = jnp.mean(xc * xc, axis=-1, keepdims=True)
    rstd = lax.rsqrt(var + LN_EPS)
    return xc * rstd, rstd


def _ln_bwd(dxh, xh, rstd):
    m1 = jnp.mean(dxh, axis=-1, keepdims=True)
    m2 = jnp.mean(dxh * xh, axis=-1, keepdims=True)
    return rstd * (dxh - m1 - xh * m2)


def _colsum8(v):
    return jnp.broadcast_to(jnp.sum(v, axis=0, keepdims=True), (8, v.shape[1]))


def _adamw(w, g, m, v):
    m2 = ADAM_B1 * m + (1.0 - ADAM_B1) * g
    v2 = ADAM_B2 * v + (1.0 - ADAM_B2) * (g * g)
    m_hat = m2 / (1.0 - ADAM_B1 ** ADAM_STEP)
    v_hat = v2 / (1.0 - ADAM_B2 ** ADAM_STEP)
    delta = -ADAM_LR * (m_hat / (jnp.sqrt(v_hat) + ADAM_EPS) + ADAM_WD * w)
    return delta, m2, v2


def _row_tile(rows, cols, itemsize=4, budget=1 << 20, mult=8):
    best = mult
    for tr in range(mult, rows + 1, mult):
        if rows % tr == 0 and tr * cols * itemsize <= budget:
            best = tr
    return best


def _mm(name, a, b, dims, grid, a_spec, b_spec, out_shape, o_spec, add=None, add_spec=None,
        add_scale=1.0, comm=None):
    nk = grid[2]
    has_add = add is not None
    out_dtype = out_shape.dtype

    def body(*refs):
        if has_add:
            a_ref, b_ref, add_ref, o_ref = refs[:4]
            rest = refs[4:]
        else:
            a_ref, b_ref, o_ref = refs[:3]
            add_ref = None
            rest = refs[3:]
        prod = _dot(a_ref[...], b_ref[...], dims)

        def finish(acc):
            if has_add:
                acc = acc + add_scale * add_ref[...]
            o_ref[...] = acc.astype(out_dtype)

        if nk == 1:
            finish(prod)
        else:
            acc_ref = rest[0]
            k = pl.program_id(2)

            @pl.when(k == 0)
            def _():
                acc_ref[...] = prod

            @pl.when(k > 0)
            def _():
                acc_ref[...] += prod

            @pl.when(k == nk - 1)
            def _():
                finish(acc_ref[...])

    in_specs = [a_spec, b_spec] + ([add_spec] if has_add else [])
    args = [a, b] + ([add] if has_add else [])
    scratch = []
    if nk > 1:
        blk = [d for d in o_spec.block_shape if d is not None]
        scratch = [pltpu.VMEM(tuple(blk), F32)]
    if comm is None:
        return _pc(body, name=name, out_shape=out_shape, grid=grid, in_specs=in_specs,
                   out_specs=o_spec, scratch=scratch,
                   sem=("parallel", "parallel", "arbitrary"))(*args)

    def first():
        return (pl.program_id(0) == 0) & (pl.program_id(1) == 0) & (pl.program_id(2) == 0)

    def last():
        return ((pl.program_id(0) == grid[0] - 1) & (pl.program_id(1) == grid[1] - 1)
                & (pl.program_id(2) == grid[2] - 1))

    res = _hosted_call(body, comm, first, last, name=name, out_shape=(out_shape,), grid=grid,
                       in_specs=in_specs, out_specs=(o_spec,), scratch=scratch,
                       sem=("arbitrary", "arbitrary", "arbitrary"), args=args)
    return res[0], res[1:]


class _Comm:
    def __init__(self, ins, out_shapes, sems, start, finish):
        self.ins, self.out_shapes, self.sems = list(ins), list(out_shapes), list(sems)
        self.start, self.finish = start, finish


def _hosted_call(body, comm, first, last, *, name, out_shape, grid, in_specs, out_specs, scratch, sem,
                 args):
    n_in, n_out, n_scr = len(in_specs), len(out_shape), len(scratch)
    nci, nco = len(comm.ins), len(comm.out_shapes)

    def wrapped(*refs):
        pos = n_in
        own_in, c_in = refs[:pos], refs[pos:pos + nci]
        pos += nci
        own_out, c_out = refs[pos:pos + n_out], refs[pos + n_out:pos + n_out + nco]
        pos += n_out + nco
        own_scr, c_sem = refs[pos:pos + n_scr], refs[pos + n_scr:]

        @pl.when(first())
        def _():
            comm.start(c_in, c_out, c_sem)

        body(*own_in, *own_out, *own_scr)

        @pl.when(last())
        def _():
            comm.finish(c_in, c_out, c_sem)

    return _pc(wrapped, name=name, out_shape=tuple(out_shape) + tuple(comm.out_shapes), grid=grid,
               in_specs=list(in_specs) + [ANY] * nci, out_specs=tuple(out_specs) + tuple([ANY] * nco),
               scratch=list(scratch) + comm.sems, sem=sem)(*args, *comm.ins)


def _grid1_call(body, comm, n, *, name, out_shape, in_specs, out_specs, scratch, args):
    if comm is None:
        return _pc(body, name=name, out_shape=out_shape, grid=(n,), in_specs=in_specs,
                   out_specs=out_specs, scratch=scratch, sem=("arbitrary",))(*args), ()
    res = _hosted_call(body, comm, lambda: pl.program_id(0) == 0, lambda: pl.program_id(0) == n - 1,
                       name=name, out_shape=out_shape, grid=(n,), in_specs=in_specs,
                       out_specs=out_specs, scratch=scratch, sem=("arbitrary",), args=args)
    return res[:len(out_shape)], res[len(out_shape):]


def _run_comm(name, comm):
    nci, nco = len(comm.ins), len(comm.out_shapes)

    def body(*refs):
        c_in, c_out, c_sem = refs[:nci], refs[nci:nci + nco], refs[nci + nco:]
        comm.start(c_in, c_out, c_sem)
        comm.finish(c_in, c_out, c_sem)

    return _pc(body, name=name, out_shape=tuple(comm.out_shapes), in_specs=[ANY] * nci,
               out_specs=tuple([ANY] * nco), scratch=comm.sems)(*comm.ins)


def _mm_nn_stacked(name, a, w_st, tm, comm=None):
    t, k = a.shape
    _, _, c = w_st.shape
    return _mm(name, a, w_st, NN, (t // tm, N_CHIP, 1),
               pl.BlockSpec((tm, k), lambda i, j, kk: (i, 0)),
               pl.BlockSpec((None, k, c), lambda i, j, kk: (j, 0, 0)),
               jax.ShapeDtypeStruct((t, N_CHIP * c), F32),
               pl.BlockSpec((tm, c), lambda i, j, kk: (i, j)), comm=comm)


def _mm_tn(name, a, b, tm, tn, tk, stacked=False):
    t, m = a.shape
    _, n = b.shape
    if stacked:
        assert tm == m
        out_shape = jax.ShapeDtypeStruct((n // tn, m, tn), F32)
        o_spec = pl.BlockSpec((None, tm, tn), lambda i, j, kk: (j, 0, 0))
    else:
        out_shape = jax.ShapeDtypeStruct((m, n), F32)
        o_spec = pl.BlockSpec((tm, tn), lambda i, j, kk: (i, j))
    return _mm(name, a, b, TN, (m // tm, n // tn, t // tk),
               pl.BlockSpec((tk, tm), lambda i, j, kk: (kk, i)),
               pl.BlockSpec((tk, tn), lambda i, j, kk: (kk, j)),
               out_shape, o_spec)


def _sgu_mixed(v, wm_ref, bsb_ref, gv, bv):
    gl, dgl = _gelu_and_grad(v)
    vh, rstd = _ln_stats(gl)
    vn = vh * gv + bv
    mixed = []
    for g in range(N_GROUP):
        sl = slice(g * 128, (g + 1) * 128)
        mixed.append(_dot(wm_ref[g], vn[:, sl]) + bsb_ref[g])
    return dgl, vh, rstd, vn, mixed


def _sgu_fwd(h, wm, bsb, gv, bv):
    t = h.shape[0]

    def body(u_ref, v_ref, wm_ref, bsb_ref, gv_ref, bv_ref, ya_ref):
        u = u_ref[...]
        _, _, _, _, mixed = _sgu_mixed(v_ref[...], wm_ref, bsb_ref, gv_ref[...], bv_ref[...])
        gu = _gelu(u)
        for g in range(N_GROUP):
            sl = slice(g * 128, (g + 1) * 128)
            ya_ref[:, sl] = (gu[:, sl] * mixed[g]).astype(BF16)

    full3 = pl.BlockSpec((N_GROUP, 128, 128), lambda i: (0, 0, 0))
    vec = pl.BlockSpec((1, D_MODEL), lambda i: (0, 0))
    return _pc(body, name="sgu_fwd", out_shape=jax.ShapeDtypeStruct((t, D_MODEL), BF16),
               grid=(t // SGU_BLOCK,),
               in_specs=[pl.BlockSpec((SGU_BLOCK, D_MODEL), lambda i: (i, 0)),
                         pl.BlockSpec((SGU_BLOCK, D_MODEL), lambda i: (i, 1)),
                         full3, full3, vec, vec],
               out_specs=pl.BlockSpec((SGU_BLOCK, D_MODEL), lambda i: (i, 0)),
               sem=("parallel",))(h, h, wm, bsb, gv, bv)


def _sgu_bwd(h, dya, wm, wmt, bsb, gv, bv, maskf):
    t = h.shape[0]
    nb = t // SGU_BLOCK

    def body(u_ref, v_ref, dya_ref, wm_ref, wmt_ref, bsb_ref, gv_ref, bv_ref, mask_ref,
             dh_ref, dws_ref, dbs_ref, dgv_ref, dbv_ref, dmix_acc):
        i = pl.program_id(0)

        @pl.when(i == 0)
        def _():
            dws_ref[...] = jnp.zeros_like(dws_ref)
            dgv_ref[...] = jnp.zeros_like(dgv_ref)
            dbv_ref[...] = jnp.zeros_like(dbv_ref)
            dmix_acc[...] = jnp.zeros_like(dmix_acc)

        u = u_ref[...]
        gvv = gv_ref[...]
        dgl_v, vh, rstd, vn, mixed = _sgu_mixed(v_ref[...], wm_ref, bsb_ref, gvv, bv_ref[...])
        gu, dgl_u = _gelu_and_grad(u)
        dya_v = dya_ref[...]
        dvn_parts = []
        for g in range(N_GROUP):
            sl = slice(g * 128, (g + 1) * 128)
            d_y = dya_v[:, sl]
            dh_ref[:, sl] = (d_y * mixed[g] * dgl_u[:, sl]).astype(BF16)
            d_mixed = d_y * gu[:, sl]
            dmix_acc[g] += d_mixed
            dws_ref[g] += _dot(d_mixed, vn[:, sl], NT) * mask_ref[...]
            dvn_parts.append(_dot(wmt_ref[g], d_mixed))
        dvn = jnp.concatenate(dvn_parts, axis=1)
        dgv_ref[...] += _colsum8(dvn * vh)
        dbv_ref[...] += _colsum8(dvn)
        d_gl = _ln_bwd(dvn * gvv, vh, rstd)
        dh_ref[:, D_MODEL:] = (d_gl * dgl_v).astype(BF16)

        @pl.when(i == nb - 1)
        def _():
            rowid = lax.broadcasted_iota(jnp.int32, (8, 128), 0)
            ones = jnp.ones((8, 128), F32)
            acc = jnp.zeros((8, 128), F32)
            for g in range(N_GROUP):
                rs = _dot32(ones, dmix_acc[g], NT)
                acc = jnp.where(rowid == g, rs, acc)
            dbs_ref[...] = acc

    full3 = pl.BlockSpec((N_GROUP, 128, 128), lambda i: (0, 0, 0))
    vec = pl.BlockSpec((1, D_MODEL), lambda i: (0, 0))
    acc8 = pl.BlockSpec((8, D_MODEL), lambda i: (0, 0))
    return _pc(body, name="sgu_bwd",
               out_shape=(jax.ShapeDtypeStruct((t, 2 * D_MODEL), BF16),
                          jax.ShapeDtypeStruct((N_GROUP, 128, 128), F32),
                          jax.ShapeDtypeStruct((8, 128), F32),
                          jax.ShapeDtypeStruct((8, D_MODEL), F32),
                          jax.ShapeDtypeStruct((8, D_MODEL), F32)),
               grid=(nb,),
               in_specs=[pl.BlockSpec((SGU_BLOCK, D_MODEL), lambda i: (i, 0)),
                         pl.BlockSpec((SGU_BLOCK, D_MODEL), lambda i: (i, 1)),
                         pl.BlockSpec((SGU_BLOCK, D_MODEL), lambda i: (i, 0)),
                         full3, full3, full3, vec, vec,
                         pl.BlockSpec((128, 128), lambda i: (0, 0))],
               out_specs=(pl.BlockSpec((SGU_BLOCK, 2 * D_MODEL), lambda i: (i, 0)),
                          full3, pl.BlockSpec((8, 128), lambda i: (0, 0)), acc8, acc8),
               scratch=[pltpu.VMEM((N_GROUP, 128, 128), F32)],
               sem=("arbitrary",))(h, h, dya, wm, wmt, bsb, gv, bv, maskf)


def _tri_masks():
    row = lax.broadcasted_iota(jnp.int32, (CHUNK, CHUNK), 0)
    col = lax.broadcasted_iota(jnp.int32, (CHUNK, CHUNK), 1)
    return col <= row, col >= row


def _lower_bound(logit_ref, sl):
    return _sig(logit_ref[0:1, sl] - logit_ref[1:2, sl])


def _hgrn_chunk(q, fp, ii, lb, st, causal):
    sg = _sig(fp)
    f = lb + (1.0 - lb) * sg
    k = 1.0 - f
    c = _dot32(causal.astype(F32), jnp.log(f))
    ec = jnp.exp(c)
    en = jnp.exp(-c)
    sq = _sig(q)
    qt = q * sq * ec
    kt = k * en
    attn = jnp.where(causal, _dot(qt, kt, NT), 0.0)
    o = _dot(attn, ii) + _dot(qt, st, NT)
    ecl = jnp.exp(c[CHUNK - 1:CHUNK, :])
    kk = kt * ecl
    st_new = st * ecl + _dot(ii, kk, TN)
    return dict(sg=sg, f=f, k=k, ec=ec, en=en, sq=sq, qt=qt, kt=kt, attn=attn, o=o, ecl=ecl,
                kk=kk, st_new=st_new)


def _hgrn_fwd(h, logits, gn, comm=None):
    t = h.shape[0]
    nc = t // CHUNK

    def body(q_ref, f_ref, i_ref, og_ref, lg_ref, gn_ref, yb_ref, st_ref, state):
        ci = pl.program_id(0)

        @pl.when(ci == 0)
        def _():
            state[...] = jnp.zeros_like(state)

        causal, _ = _tri_masks()
        for hd in range(N_HEAD):
            sl = slice(hd * HEAD_DIM, (hd + 1) * HEAD_DIM)
            st = state[hd]
            st_ref[0, hd] = st
            r = _hgrn_chunk(q_ref[:, sl], f_ref[:, sl], i_ref[:, sl], _lower_bound(lg_ref, sl),
                            st, causal)
            state[hd] = r["st_new"]
            o = r["o"]
            on = o * lax.rsqrt(jnp.mean(o * o, axis=-1, keepdims=True) + RMS_EPS)
            og = og_ref[:, sl]
            yb_ref[:, sl] = (on * gn_ref[:, sl] * (og * _sig(og))).astype(BF16)

    def col(k):
        return pl.BlockSpec((CHUNK, D_MODEL), lambda ci: (ci, k))

    return _grid1_call(body, comm, nc, name="hgrn_fwd",
                       out_shape=(jax.ShapeDtypeStruct((t, D_MODEL), BF16),
                                  jax.ShapeDtypeStruct((nc, N_HEAD, HEAD_DIM, HEAD_DIM), F32)),
                       in_specs=[col(2), col(3), col(4), col(5),
                                 pl.BlockSpec((2, D_MODEL), lambda ci: (0, 0)),
                                 pl.BlockSpec((1, D_MODEL), lambda ci: (0, 0))],
                       out_specs=(pl.BlockSpec((CHUNK, D_MODEL), lambda ci: (ci, 0)),
                                  pl.BlockSpec((1, N_HEAD, HEAD_DIM, HEAD_DIM), lambda ci: (ci, 0, 0, 0))),
                       scratch=[pltpu.VMEM((N_HEAD, HEAD_DIM, HEAD_DIM), F32)],
                       args=(h, h, h, h, logits, gn))


def _hgrn_bwd(h, dyb, st_all, logits, gn, comm=None):
    t = h.shape[0]
    nc = t // CHUNK

    def body(q_ref, f_ref, i_ref, og_ref, dyb_ref, st_ref, lg_ref, gn_ref,
             dh1_ref, dh2_ref, dlb_ref, dgn_ref, dstate):
        ci = pl.program_id(0)

        @pl.when(ci == 0)
        def _():
            dstate[...] = jnp.zeros_like(dstate)
            dlb_ref[...] = jnp.zeros_like(dlb_ref)
            dgn_ref[...] = jnp.zeros_like(dgn_ref)

        causal, anti = _tri_masks()
        rowid = lax.broadcasted_iota(jnp.int32, (CHUNK, HEAD_DIM), 0)
        for hd in range(N_HEAD):
            sl = slice(hd * HEAD_DIM, (hd + 1) * HEAD_DIM)
            q = q_ref[:, sl]
            ii = i_ref[:, sl]
            og = og_ref[:, sl]
            lb = _lower_bound(lg_ref, sl)
            gnv = gn_ref[:, sl]
            st = st_ref[0, hd]
            dsn = dstate[hd]
            r = _hgrn_chunk(q, f_ref[:, sl], ii, lb, st, causal)
            o = r["o"]
            rinv = lax.rsqrt(jnp.mean(o * o, axis=-1, keepdims=True) + RMS_EPS)
            on = o * rinv
            so = _sig(og)
            sil = og * so
            dy = dyb_ref[:, sl]
            d_og = dy * on * gnv * (so * (1.0 + og * (1.0 - so)))
            dgn_ref[:, sl] += _colsum8(dy * on * sil)
            d_on = dy * gnv * sil
            d_o = rinv * (d_on - on * jnp.mean(d_on * on, axis=-1, keepdims=True))
            d_attn = jnp.where(causal, _dot(d_o, ii, NT), 0.0)
            d_i = _dot(r["attn"], d_o, TN) + _dot(r["kk"], dsn, NT)
            d_qt = _dot(d_attn, r["kt"]) + _dot(d_o, st)
            d_kt = _dot(d_attn, r["qt"], TN)
            d_kk = _dot(ii, dsn)
            dstate[hd] = _dot(d_o, r["qt"], TN) + dsn * r["ecl"]
            d_cl = (r["ecl"] * jnp.sum(st * dsn, axis=0, keepdims=True)
                    + jnp.sum(r["kk"] * d_kk, axis=0, keepdims=True))
            d_k = (d_kk * r["ecl"] + d_kt) * r["en"]
            d_c = (d_qt * r["qt"].astype(BF16).astype(F32) - d_kt * r["kt"].astype(BF16).astype(F32)
                   - d_kk * r["kk"])
            d_c = d_c + jnp.where(rowid == CHUNK - 1, d_cl, 0.0)
            d_lf = _dot32(anti.astype(F32), d_c)
            d_f = d_lf / r["f"] - d_k
            sg = r["sg"]
            dlb_ref[:, sl] += _colsum8(d_f * (1.0 - sg))
            d_fp = d_f * (1.0 - lb) * sg * (1.0 - sg)
            sq = r["sq"]
            d_q = d_qt * r["ec"] * (sq * (1.0 + q * (1.0 - sq)))
            dh1_ref[:, sl] = d_q.astype(BF16)
            dh1_ref[:, D_MODEL + hd * HEAD_DIM:D_MODEL + (hd + 1) * HEAD_DIM] = d_fp.astype(BF16)
            dh2_ref[:, sl] = d_i.astype(BF16)
            dh2_ref[:, D_MODEL + hd * HEAD_DIM:D_MODEL + (hd + 1) * HEAD_DIM] = d_og.astype(BF16)

    def col(k):
        return pl.BlockSpec((CHUNK, D_MODEL), lambda ci: (nc - 1 - ci, k))

    acc8 = pl.BlockSpec((8, D_MODEL), lambda ci: (0, 0))
    pair = pl.BlockSpec((CHUNK, 2 * D_MODEL), lambda ci: (nc - 1 - ci, 0))
    return _grid1_call(body, comm, nc, name="hgrn_bwd",
                       out_shape=(jax.ShapeDtypeStruct((t, 2 * D_MODEL), BF16),
                                  jax.ShapeDtypeStruct((t, 2 * D_MODEL), BF16),
                                  jax.ShapeDtypeStruct((8, D_MODEL), F32),
                                  jax.ShapeDtypeStruct((8, D_MODEL), F32)),
                       in_specs=[col(2), col(3), col(4), col(5),
                                 pl.BlockSpec((CHUNK, D_MODEL), lambda ci: (nc - 1 - ci, 0)),
                                 pl.BlockSpec((1, N_HEAD, HEAD_DIM, HEAD_DIM),
                                              lambda ci: (nc - 1 - ci, 0, 0, 0)),
                                 pl.BlockSpec((2, D_MODEL), lambda ci: (0, 0)),
                                 pl.BlockSpec((1, D_MODEL), lambda ci: (0, 0))],
                       out_specs=(pair, pair, acc8, acc8),
                       scratch=[pltpu.VMEM((N_HEAD, HEAD_DIM, HEAD_DIM), F32)],
                       args=(h, h, h, h, dyb, st_all, logits, gn))


def _mix_fwd(ya, yb, h, x, wb0, wb1, wo, g1, b1, tm):
    t = x.shape[0]

    def body(ya_ref, yb_ref, ga_ref, gb_ref, x_ref, wb0_ref, wb1_ref, wo_ref, g1_ref, b1_ref,
             r1_ref, a_ref, b_ref, m_ref, x1_ref):
        a = _dot(ya_ref[...], wb0_ref[...])
        b = _dot(yb_ref[...], wb1_ref[...])
        m = _sig(ga_ref[...]) * a + _sig(gb_ref[...]) * b
        r1 = ALPHA * x_ref[...] + _dot(m, wo_ref[...])
        xh, _ = _ln_stats(r1)
        r1_ref[...] = r1
        a_ref[...] = a
        b_ref[...] = b
        m_ref[...] = m.astype(BF16)
        x1_ref[...] = (xh * g1_ref[...] + b1_ref[...]).astype(BF16)

    tile = pl.BlockSpec((tm, D_MODEL), lambda i: (i, 0))
    wsp = pl.BlockSpec((D_MODEL, D_MODEL), lambda i: (0, 0))
    vec = pl.BlockSpec((1, D_MODEL), lambda i: (0, 0))
    f32o = jax.ShapeDtypeStruct((t, D_MODEL), F32)
    bfo = jax.ShapeDtypeStruct((t, D_MODEL), BF16)
    return _pc(body, name="mix_fwd", out_shape=(f32o, f32o, f32o, bfo, bfo), grid=(t // tm,),
               in_specs=[tile, tile,
                         pl.BlockSpec((tm, D_MODEL), lambda i: (i, 6)),
                         pl.BlockSpec((tm, D_MODEL), lambda i: (i, 7)),
                         tile, wsp, wsp, wsp, vec, vec],
               out_specs=(tile, tile, tile, tile, tile),
               sem=("parallel",))(ya, yb, h, h, x, wb0, wb1, wo, g1, b1)


def _mix_bwd(dr1, h, a, b, wo, wb0, wb1, tm):
    t = dr1.shape[0]

    def body(dr1_ref, ga_ref, gb_ref, a_ref, b_ref, wo_ref, wb0_ref, wb1_ref,
             da_ref, db_ref, dh3_ref, dya_ref, dyb_ref):
        d_m = _dot(dr1_ref[...], wo_ref[...], NT)
        sa = _sig(ga_ref[...])
        sb = _sig(gb_ref[...])
        d_a = (d_m * sa).astype(BF16)
        d_b = (d_m * sb).astype(BF16)
        da_ref[...] = d_a
        db_ref[...] = d_b
        dh3_ref[:, :D_MODEL] = (d_m * a_ref[...] * sa * (1.0 - sa)).astype(BF16)
        dh3_ref[:, D_MODEL:] = (d_m * b_ref[...] * sb * (1.0 - sb)).astype(BF16)
        dya_ref[...] = _dot(d_a, wb0_ref[...], NT)
        dyb_ref[...] = _dot(d_b, wb1_ref[...], NT)

    tile = pl.BlockSpec((tm, D_MODEL), lambda i: (i, 0))
    wsp = pl.BlockSpec((D_MODEL, D_MODEL), lambda i: (0, 0))
    f32o = jax.ShapeDtypeStruct((t, D_MODEL), F32)
    bfo = jax.ShapeDtypeStruct((t, D_MODEL), BF16)
    return _pc(body, name="mix_bwd",
               out_shape=(bfo, bfo, jax.ShapeDtypeStruct((t, 2 * D_MODEL), BF16), f32o, f32o),
               grid=(t // tm,),
               in_specs=[tile,
                         pl.BlockSpec((tm, D_MODEL), lambda i: (i, 6)),
                         pl.BlockSpec((tm, D_MODEL), lambda i: (i, 7)),
                         tile, tile, wsp, wsp, wsp],
               out_specs=(tile, tile, pl.BlockSpec((tm, 2 * D_MODEL), lambda i: (i, 0)),
                          tile, tile),
               sem=("parallel",))(dr1, h, h, a, b, wo, wb0, wb1)


FF_TILE = 1408
FF_NJ = D_FF // FF_TILE


def _shift_down(v, k):
    return pltpu.roll(v, k, 0)


def _shift_up(v, k):
    return pltpu.roll(v, v.shape[0] - k, 0)


def _conv_gate(ext, cw_ref, cb_ref):
    return (cw_ref[0:1, :] * _shift_down(ext, 2) + cw_ref[1:2, :] * _shift_down(ext, 1)
            + cw_ref[2:3, :] * ext + cb_ref[...])


def _ffn_act_fwd(h2, convw, convb, tm):
    t = h2.shape[0]
    nt8 = tm // 8

    def body(g_ref, gp_ref, v_ref, cw_ref, cb_ref, act_ref):
        i = pl.program_id(1)
        prev = gp_ref[...] * (i > 0).astype(F32)
        ext = jnp.concatenate([prev, g_ref[...]], axis=0)
        gc = _conv_gate(ext, cw_ref, cb_ref)[8:, :]
        act_ref[...] = (_gelu(gc) * v_ref[...]).astype(BF16)

    return _pc(body, name="ffn_act_fwd", out_shape=jax.ShapeDtypeStruct((t, D_FF), BF16),
               grid=(FF_NJ, t // tm),
               in_specs=[pl.BlockSpec((tm, FF_TILE), lambda j, i: (i, j)),
                         pl.BlockSpec((8, FF_TILE), lambda j, i: (jnp.maximum(i * nt8 - 1, 0), j)),
                         pl.BlockSpec((tm, FF_TILE), lambda j, i: (i, j + FF_NJ)),
                         pl.BlockSpec((3, FF_TILE), lambda j, i: (0, j)),
                         pl.BlockSpec((1, FF_TILE), lambda j, i: (0, j))],
               out_specs=pl.BlockSpec((tm, FF_TILE), lambda j, i: (i, j)),
               sem=("parallel", "parallel"))(h2, h2, h2, convw, convb)


def _ffn_act_bwd(h2, dact, convw, convb, tm):
    t = h2.shape[0]
    nt8 = tm // 8
    ni = t // tm
    last8 = t // 8 - 1

    def body(g_ref, gp_ref, gn_ref, v_ref, vn_ref, da_ref, dan_ref, cw_ref, cb_ref,
             dh2_ref, dcw_ref, dcb_ref):
        i = pl.program_id(1)

        @pl.when(i == 0)
        def _():
            dcw_ref[...] = jnp.zeros_like(dcw_ref)
            dcb_ref[...] = jnp.zeros_like(dcb_ref)

        prev = gp_ref[...] * (i > 0).astype(F32)
        ext = jnp.concatenate([prev, g_ref[...], gn_ref[...]], axis=0)
        vext = jnp.concatenate([jnp.zeros((8, FF_TILE), F32), v_ref[...], vn_ref[...]], axis=0)
        dnext = dan_ref[...] * (i < ni - 1).astype(F32)
        dext = jnp.concatenate([jnp.zeros((8, FF_TILE), F32), da_ref[...], dnext], axis=0)
        g2 = _shift_down(ext, 2)
        g1 = _shift_down(ext, 1)
        gc = cw_ref[0:1, :] * g2 + cw_ref[1:2, :] * g1 + cw_ref[2:3, :] * ext + cb_ref[...]
        gl, dgl = _gelu_and_grad(gc)
        d_gc = dext * vext * dgl
        d_gate = (cw_ref[2:3, :] * d_gc + cw_ref[1:2, :] * _shift_up(d_gc, 1)
                  + cw_ref[0:1, :] * _shift_up(d_gc, 2))
        dh2_ref[0] = d_gate[8:8 + tm, :].astype(BF16)
        dh2_ref[1] = (da_ref[...] * gl[8:8 + tm, :]).astype(BF16)
        dm = d_gc[8:8 + tm, :]
        s0 = jnp.sum(dm * g2[8:8 + tm, :], axis=0, keepdims=True)
        s1 = jnp.sum(dm * g1[8:8 + tm, :], axis=0, keepdims=True)
        s2 = jnp.sum(dm * ext[8:8 + tm, :], axis=0, keepdims=True)
        rowid = lax.broadcasted_iota(jnp.int32, (8, FF_TILE), 0)
        dcw_ref[...] += jnp.where(rowid == 0, s0, jnp.where(rowid == 1, s1,
                                                            jnp.where(rowid == 2, s2, 0.0)))
        dcb_ref[...] += _colsum8(dm)

    def prev8(off):
        return pl.BlockSpec((8, FF_TILE), lambda j, i: (jnp.maximum(i * nt8 - 1, 0), j + off))

    def next8(off):
        return pl.BlockSpec((8, FF_TILE), lambda j, i: (jnp.minimum((i + 1) * nt8, last8), j + off))

    def main(off):
        return pl.BlockSpec((tm, FF_TILE), lambda j, i: (i, j + off))

    acc = pl.BlockSpec((8, FF_TILE), lambda j, i: (0, j))
    return _pc(body, name="ffn_act_bwd",
               out_shape=(jax.ShapeDtypeStruct((2, t, D_FF), BF16),
                          jax.ShapeDtypeStruct((8, D_FF), F32),
                          jax.ShapeDtypeStruct((8, D_FF), F32)),
               grid=(FF_NJ, ni),
               in_specs=[main(0), prev8(0), next8(0), main(FF_NJ), next8(FF_NJ),
                         pl.BlockSpec((tm, FF_TILE), lambda j, i: (i, j)),
                         pl.BlockSpec((8, FF_TILE), lambda j, i: (jnp.minimum((i + 1) * nt8, last8), j)),
                         pl.BlockSpec((3, FF_TILE), lambda j, i: (0, j)),
                         pl.BlockSpec((1, FF_TILE), lambda j, i: (0, j))],
               out_specs=(pl.BlockSpec((2, tm, FF_TILE), lambda j, i: (0, i, j)), acc, acc),
               sem=("parallel", "arbitrary"))(h2, h2, h2, h2, h2, dact, dact, convw, convb)


def _out_fwd_bwd(act, x1b, r1, p2, tgt, wd, wpg, wpp, g1, b1, g2, b2, tm):
    t = r1.shape[0]

    def body(act_ref, x1b_ref, r1_ref, p_ref, tgt_ref, wd_ref, wpg_ref, wpp_ref,
             g1_ref, b1_ref, g2_ref, b2_ref,
             dr2_ref, dpg_ref, dpp_ref, loss_ref, dg2_ref, db2_ref):
        i = pl.program_id(0)

        @pl.when(i == 0)
        def _():
            loss_ref[...] = jnp.zeros_like(loss_ref)
            dg2_ref[...] = jnp.zeros_like(dg2_ref)
            db2_ref[...] = jnp.zeros_like(db2_ref)

        ffn = _dot(act_ref[...], wd_ref[...])
        pg = _dot(x1b_ref[...], wpg_ref[...])
        pp = _dot(p_ref[...], wpp_ref[...])
        s = _sig(pg)
        xh1, _ = _ln_stats(r1_ref[...])
        x1 = xh1 * g1_ref[...] + b1_ref[...]
        r2 = ALPHA * x1 + ffn + s * pp
        xh2, rstd2 = _ln_stats(r2)
        g2v = g2_ref[...]
        diff = xh2 * g2v + b2_ref[...] - tgt_ref[...]
        part = jnp.sum(jnp.sum(diff * diff, axis=1, keepdims=True), axis=0, keepdims=True)
        loss_ref[...] += jnp.broadcast_to(part * (0.5 / D_MODEL), loss_ref.shape)
        dy = diff * (1.0 / D_MODEL)
        dg2_ref[...] += _colsum8(dy * xh2)
        db2_ref[...] += _colsum8(dy)
        dr2 = _ln_bwd(dy * g2v, xh2, rstd2)
        dr2_ref[...] = dr2
        dpg_ref[...] = (dr2 * pp * s * (1.0 - s)).astype(BF16)
        dpp_ref[...] = (dr2 * s).astype(BF16)

    tile = pl.BlockSpec((tm, D_MODEL), lambda i: (i, 0))
    vec = pl.BlockSpec((1, D_MODEL), lambda i: (0, 0))
    acc8 = pl.BlockSpec((8, D_MODEL), lambda i: (0, 0))
    acc_shape = jax.ShapeDtypeStruct((8, D_MODEL), F32)
    return _pc(body, name="out_fwd_bwd",
               out_shape=(jax.ShapeDtypeStruct((t, D_MODEL), F32),
                          jax.ShapeDtypeStruct((t, D_MODEL), BF16),
                          jax.ShapeDtypeStruct((t, D_MODEL), BF16),
                          acc_shape, acc_shape, acc_shape),
               grid=(t // tm,),
               in_specs=[pl.BlockSpec((tm, D_FF), lambda i: (i, 0)), tile, tile,
                         pl.BlockSpec((tm, PLE_DIM), lambda i: (i, 0)), tile,
                         pl.BlockSpec((D_FF, D_MODEL), lambda i: (0, 0)),
                         pl.BlockSpec((D_MODEL, D_MODEL), lambda i: (0, 0)),
                         pl.BlockSpec((PLE_DIM, D_MODEL), lambda i: (0, 0)),
                         vec, vec, vec, vec],
               out_specs=(tile, tile, tile, acc8, acc8, acc8),
               sem=("arbitrary",))(act, x1b, r1, p2, tgt, wd, wpg, wpp, g1, b1, g2, b2)


def _ffn_in_bwd(dh2, wup_st, dpg, wpg, dr2, r1, g1, tm):
    t = r1.shape[0]
    ni = t // tm

    def body(dh2_ref, wup_ref, dpg_ref, wpg_ref, dr2_ref, r1_ref, g1_ref,
             dr1_ref, dg1_ref, db1_ref, acc):
        i = pl.program_id(0)
        j = pl.program_id(1)

        @pl.when((i == 0) & (j == 0))
        def _():
            dg1_ref[...] = jnp.zeros_like(dg1_ref)
            db1_ref[...] = jnp.zeros_like(db1_ref)

        @pl.when(j == 0)
        def _():
            acc[...] = _dot(dh2_ref[...], wup_ref[...], NT)

        @pl.when(j > 0)
        def _():
            acc[...] += _dot(dh2_ref[...], wup_ref[...], NT)

        @pl.when(j == N_CHIP - 1)
        def _():
            d_x1 = acc[...] + _dot(dpg_ref[...], wpg_ref[...], NT) + ALPHA * dr2_ref[...]
            xh, rstd = _ln_stats(r1_ref[...])
            dg1_ref[...] += _colsum8(d_x1 * xh)
            db1_ref[...] += _colsum8(d_x1)
            dr1_ref[...] = _ln_bwd(d_x1 * g1_ref[...], xh, rstd)

    tile = pl.BlockSpec((tm, D_MODEL), lambda i, j: (i, 0))
    acc8 = pl.BlockSpec((8, D_MODEL), lambda i, j: (0, 0))
    acc_shape = jax.ShapeDtypeStruct((8, D_MODEL), F32)
    return _pc(body, name="ffn_in_bwd",
               out_shape=(jax.ShapeDtypeStruct((t, D_MODEL), F32), acc_shape, acc_shape),
               grid=(ni, N_CHIP),
               in_specs=[pl.BlockSpec((None, tm, FF_TILE), lambda i, j: (j // FF_NJ, i, j % FF_NJ)),
                         pl.BlockSpec((None, D_MODEL, FF_TILE), lambda i, j: (j, 0, 0)),
                         tile, pl.BlockSpec((D_MODEL, D_MODEL), lambda i, j: (0, 0)),
                         tile, tile, pl.BlockSpec((1, D_MODEL), lambda i, j: (0, 0))],
               out_specs=(tile, acc8, acc8),
               scratch=[pltpu.VMEM((tm, D_MODEL), F32)],
               sem=("arbitrary", "arbitrary"))(dh2, wup_st, dpg, wpg, dr2, r1, g1)


ANY = pl.BlockSpec(memory_space=pl.ANY)


def _chip_peers():
    x, y, c = lax.axis_index("x"), lax.axis_index("y"), lax.axis_index("c")
    return x, y, c, [(1 - x, y), (x, 1 - y), (1 - x, 1 - y)]


def _gather_comm(halved, whole=()):
    n, nw = len(halved), len(whole)

    def copies(ins, outs, sems):
        ici_send, ici_recv, d2d_send, d2d_recv, own_send, own_recv = sems
        x, y, c, peers = _chip_peers()
        me = 2 * x + y
        sibling = (x, y, 1 - c)
        own, ici, ici_wait, fwd, fwd_wait = [], [], [], [], []
        for ti in range(n + nw):
            src, dst = ins[ti], outs[ti]
            own.append(pltpu.make_async_remote_copy(
                src_ref=src, dst_ref=dst.at[me], send_sem=own_send.at[ti], recv_sem=own_recv.at[ti],
                device_id=sibling, device_id_type=MESH))
            for k, (px, py) in enumerate(peers):
                pk = 2 * px + py
                sem = dict(send_sem=ici_send.at[ti * 3 + k], recv_sem=ici_recv.at[ti * 3 + k],
                           device_id=(px, py, c), device_id_type=MESH)
                if ti < n:
                    ici.append(pltpu.make_async_remote_copy(src_ref=src.at[c], dst_ref=dst.at[me, c], **sem))
                    ici_wait.append(pltpu.make_async_remote_copy(src_ref=src.at[c], dst_ref=dst.at[pk, c], **sem))
                    dsem = dict(send_sem=d2d_send.at[ti * 3 + k], recv_sem=d2d_recv.at[ti * 3 + k],
                                device_id=sibling, device_id_type=MESH)
                    fwd.append(pltpu.make_async_remote_copy(src_ref=dst.at[pk, c], dst_ref=dst.at[pk, c], **dsem))
                    fwd_wait.append(pltpu.make_async_remote_copy(
                        src_ref=dst.at[pk, 1 - c], dst_ref=dst.at[pk, 1 - c], **dsem))
                else:
                    ici.append(pltpu.make_async_remote_copy(src_ref=src, dst_ref=dst.at[me], **sem))
                    ici_wait.append(pltpu.make_async_remote_copy(src_ref=src, dst_ref=dst.at[pk], **sem))
        return own, ici, ici_wait, fwd, fwd_wait

    def start(ins, outs, sems):
        own, ici, _, _, _ = copies(ins, outs, sems)
        for cp in own + ici:
            cp.start()

    def finish(ins, outs, sems):
        own, ici, ici_wait, fwd, fwd_wait = copies(ins, outs, sems)
        for i, cp in enumerate(ici_wait):
            cp.wait_recv()
            if i < len(fwd):
                fwd[i].start()
        for cp in fwd_wait + own:
            cp.wait_recv()
        for cp in own + ici + fwd:
            cp.wait_send()

    srcs = list(halved) + list(whole)
    return _Comm(srcs, [jax.ShapeDtypeStruct((N_CHIP,) + s.shape, s.dtype) for s in srcs],
                 [pltpu.SemaphoreType.DMA((3 * (n + nw),)), pltpu.SemaphoreType.DMA((3 * (n + nw),)),
                  pltpu.SemaphoreType.DMA((max(3 * n, 1),)), pltpu.SemaphoreType.DMA((max(3 * n, 1),)),
                  pltpu.SemaphoreType.DMA((n + nw,)), pltpu.SemaphoreType.DMA((n + nw,))],
                 start, finish)


def _rs_sibling_exchange(name, grads):
    n = len(grads)

    def body(*refs):
        ins, outs = refs[:n], refs[n:2 * n]
        send_sems, recv_sems = refs[2 * n:]
        x, y, c = lax.axis_index("x"), lax.axis_index("y"), lax.axis_index("c")
        sends = []
        for ti in range(n):
            half = ins[ti].shape[1] // 2
            cp = pltpu.make_async_remote_copy(
                src_ref=ins[ti].at[:, pl.ds(pl.multiple_of((1 - c) * half, 8), half), :],
                dst_ref=outs[ti],
                send_sem=send_sems.at[ti], recv_sem=recv_sems.at[ti],
                device_id=(x, y, 1 - c), device_id_type=MESH)
            cp.start()
            sends.append(cp)
        for cp in sends:
            cp.wait()

    return _pc(body, name=name,
               out_shape=tuple(jax.ShapeDtypeStruct((N_CHIP, g.shape[1] // 2, g.shape[2]), g.dtype)
                               for g in grads),
               in_specs=[ANY] * n, out_specs=tuple([ANY] * n),
               scratch=[pltpu.SemaphoreType.DMA((n,)), pltpu.SemaphoreType.DMA((n,))])(*grads)


def _rs_add_halves(name, grad, recv, core):
    _, r, cdim = grad.shape
    half = r // 2
    tr = _row_tile(half, cdim, mult=16)
    nr = half // tr

    def body(c_ref, g_ref, r_ref, o_ref):
        o_ref[...] = (g_ref[...] + r_ref[...]).astype(BF16)

    return _pc(body, name=name, out_shape=jax.ShapeDtypeStruct((N_CHIP, half, cdim), BF16),
               grid=(N_CHIP, nr), nsp=1,
               in_specs=[pl.BlockSpec((None, tr, cdim), lambda j, i, c_ref: (j, c_ref[0] * nr + i, 0)),
                         pl.BlockSpec((None, tr, cdim), lambda j, i, c_ref: (j, i, 0))],
               out_specs=pl.BlockSpec((None, tr, cdim), lambda j, i, c_ref: (j, i, 0)),
               sem=("parallel", "parallel"))(core, grad, recv)


def _chip_exchange_comm(parts):
    n = len(parts)

    def copies(ins, outs, sems):
        send_sems, recv_sems = sems
        x, y, c, peers = _chip_peers()
        return [pltpu.make_async_remote_copy(
            src_ref=ins[ti].at[2 * px + py], dst_ref=outs[ti].at[k],
            send_sem=send_sems.at[ti * 3 + k], recv_sem=recv_sems.at[ti * 3 + k],
            device_id=(px, py, c), device_id_type=MESH)
            for ti in range(n) for k, (px, py) in enumerate(peers)]

    def start(ins, outs, sems):
        for cp in copies(ins, outs, sems):
            cp.start()

    def finish(ins, outs, sems):
        for cp in copies(ins, outs, sems):
            cp.wait()

    return _Comm(parts, [jax.ShapeDtypeStruct((3,) + p.shape[1:], p.dtype) for p in parts],
                 [pltpu.SemaphoreType.DMA((3 * n,)), pltpu.SemaphoreType.DMA((3 * n,))], start, finish)


def _rs_sum_chips(name, part, recv, chip):
    _, half, cdim = recv.shape
    tr = _row_tile(half, cdim, mult=16)

    def body(chip_ref, p_ref, r_ref, o_ref):
        o_ref[...] = ((p_ref[...].astype(F32) + r_ref[0].astype(F32)) + r_ref[1].astype(F32)
                      ) + r_ref[2].astype(F32)

    return _pc(body, name=name, out_shape=jax.ShapeDtypeStruct((half, cdim), F32),
               grid=(half // tr,), nsp=1,
               in_specs=[pl.BlockSpec((None, tr, cdim), lambda i, chip_ref: (chip_ref[0], i, 0)),
                         pl.BlockSpec((3, tr, cdim), lambda i, chip_ref: (0, i, 0))],
               out_specs=pl.BlockSpec((tr, cdim), lambda i, chip_ref: (i, 0)),
               sem=("parallel",))(chip, part, recv)


def _rs_send_halves(halves):
    n = len(halves)

    def body(*refs):
        ins, outs = refs[:n], refs[n:2 * n]
        send_sems, recv_sems = refs[2 * n:]
        x, y, c = lax.axis_index("x"), lax.axis_index("y"), lax.axis_index("c")
        sends = []
        for ti in range(n):
            cp = pltpu.make_async_remote_copy(
                src_ref=ins[ti], dst_ref=outs[ti],
                send_sem=send_sems.at[ti], recv_sem=recv_sems.at[ti],
                device_id=(x, y, 1 - c), device_id_type=MESH)
            cp.start()
            sends.append(cp)
        for cp in sends:
            cp.wait()

    return _pc(body, name="rs_send_halves",
               out_shape=tuple(jax.ShapeDtypeStruct(hv.shape, hv.dtype) for hv in halves),
               in_specs=[ANY] * n, out_specs=tuple([ANY] * n),
               scratch=[pltpu.SemaphoreType.DMA((n,)), pltpu.SemaphoreType.DMA((n,))])(*halves)


def _adamw_rows(name, mine, theirs, w, m, v, core):
    half, cdim = mine.shape
    tr = _row_tile(half, cdim, budget=1 << 19)
    nrh = half // tr

    def body(c_ref, mine_ref, theirs_ref, w_ref, m_ref, v_ref, g_ref, d_ref, m2_ref, v2_ref):
        is_mine = (pl.program_id(0) // nrh) == c_ref[0]
        g = jnp.where(is_mine, mine_ref[...], theirs_ref[...])
        d, m2, v2 = _adamw(w_ref[...], g, m_ref[...], v_ref[...])
        g_ref[...] = g
        d_ref[...] = d
        m2_ref[...] = m2
        v2_ref[...] = v2

    htile = pl.BlockSpec((tr, cdim), lambda i, c_ref: (i % nrh, 0))
    tile = pl.BlockSpec((tr, cdim), lambda i, c_ref: (i, 0))
    shp = jax.ShapeDtypeStruct((2 * half, cdim), F32)
    return _pc(body, name=name, out_shape=(shp, shp, shp, shp), grid=(2 * nrh,), nsp=1,
               in_specs=[htile, htile, tile, tile, tile], out_specs=(tile, tile, tile, tile),
               sem=("parallel",))(core, mine, theirs, w, m, v)


def _adamw_whole(name, g, w, m, v):
    def body(g_ref, w_ref, m_ref, v_ref, d_ref, m2_ref, v2_ref):
        d, m2, v2 = _adamw(w_ref[...], g_ref[...], m_ref[...], v_ref[...])
        d_ref[...] = d
        m2_ref[...] = m2
        v2_ref[...] = v2

    shp = jax.ShapeDtypeStruct(g.shape, F32)
    return _pc(body, name=name, out_shape=(shp, shp, shp))(g, w, m, v)


SMALL_LAYOUT = (
    ("sgu_w_s", 1024, 1, 0),
    ("sgu_b_s", 8, 1, 1024),
    ("sgu_norm_g", 1, 0, 0),
    ("sgu_norm_b", 1, 0, 8),
    ("hgrn_norm_g", 1, 0, 24),
    ("ln1_g", 1, 0, 32),
    ("ln1_b", 1, 0, 40),
    ("ffn_conv_b", 1, 2, 8),
    ("ln2_g", 1, 0, 48),
    ("ln2_b", 1, 0, 56),
)
LB_ROW = 16
LOSS_ROW = 64


def _small_allreduce_adamw(bufs, logits, m_logits, v_logits, small_w, small_m, small_v):
    ns = len(SMALL_LAYOUT)
    nb = len(bufs)

    def body(*refs):
        buf_refs = refs[:nb]
        lg_ref, mlg_ref, vlg_ref = refs[nb:nb + 3]
        pos = nb + 3
        w_refs = refs[pos:pos + ns]
        m_refs = refs[pos + ns:pos + 2 * ns]
        v_refs = refs[pos + 2 * ns:pos + 3 * ns]
        pos += 3 * ns
        loss_ref, dcw_ref = refs[pos:pos + 2]
        lg_outs = refs[pos + 2:pos + 6]
        pos += 6
        outs = refs[pos:pos + 4 * ns]
        pos += 4 * ns
        gath = refs[pos:pos + nb]
        send_sems, recv_sems = refs[pos + nb:pos + nb + 2]

        x, y, c = lax.axis_index("x"), lax.axis_index("y"), lax.axis_index("c")
        me = 4 * x + 2 * y + c
        sends = []
        for d in range(1, N_DEV):
            peer = (x ^ (d >> 2), y ^ ((d >> 1) & 1), c ^ (d & 1))
            for b in range(nb):
                cp = pltpu.make_async_remote_copy(
                    src_ref=buf_refs[b], dst_ref=gath[b].at[me],
                    send_sem=send_sems.at[(d - 1) * nb + b], recv_sem=recv_sems.at[(d - 1) * nb + b],
                    device_id=peer, device_id_type=MESH)
                cp.start()
                sends.append(cp)
        for b in range(nb):
            gath[b][me] = buf_refs[b][...]
        for d in range(1, N_DEV):
            peer = (x ^ (d >> 2), y ^ ((d >> 1) & 1), c ^ (d & 1))
            src = 4 * peer[0] + 2 * peer[1] + peer[2]
            for b in range(nb):
                pltpu.make_async_remote_copy(
                    src_ref=buf_refs[b], dst_ref=gath[b].at[src],
                    send_sem=send_sems.at[(d - 1) * nb + b], recv_sem=recv_sems.at[(d - 1) * nb + b],
                    device_id=peer, device_id_type=MESH).wait_recv()
        for cp in sends:
            cp.wait_send()

        tot = []
        for b in range(nb):
            s = gath[b][0]
            for d in range(1, N_DEV):
                s = s + gath[b][d]
            tot.append(s)

        loss_ref[...] = tot[0][LOSS_ROW:LOSS_ROW + 8, :]
        dcw_ref[...] = tot[2][0:8, :]
        lb = _sig(lg_ref[0:1, :] - lg_ref[1:2, :])
        d0 = tot[0][LB_ROW:LB_ROW + 1, :] * lb * (1.0 - lb)
        rowid = lax.broadcasted_iota(jnp.int32, (2, D_MODEL), 0)
        g_lg = jnp.where(rowid == 0, d0, -d0)
        dl, ml, vl = _adamw(lg_ref[...], g_lg, mlg_ref[...], vlg_ref[...])
        lg_outs[0][...] = g_lg
        lg_outs[1][...] = dl
        lg_outs[2][...] = ml
        lg_outs[3][...] = vl
        for si, (_, rows, b, r0) in enumerate(SMALL_LAYOUT):
            g = tot[b][r0:r0 + rows, :]
            dl, ml, vl = _adamw(w_refs[si][...], g, m_refs[si][...], v_refs[si][...])
            outs[4 * si][...] = g
            outs[4 * si + 1][...] = dl
            outs[4 * si + 2][...] = ml
            outs[4 * si + 3][...] = vl

    shapes = [jax.ShapeDtypeStruct((8, D_MODEL), F32), jax.ShapeDtypeStruct((8, D_FF), F32)]
    shapes += [jax.ShapeDtypeStruct((2, D_MODEL), F32)] * 4
    for w in small_w:
        shapes += [jax.ShapeDtypeStruct(w.shape, F32)] * 4
    scratch = [pltpu.VMEM((N_DEV,) + b.shape, F32) for b in bufs]
    scratch += [pltpu.SemaphoreType.DMA(((N_DEV - 1) * nb,)), pltpu.SemaphoreType.DMA(((N_DEV - 1) * nb,))]
    vm = pl.BlockSpec(memory_space=pltpu.VMEM)
    n_in = nb + 3 + 3 * ns
    res = _pc(body, name="small_allreduce_adamw", out_shape=tuple(shapes),
              in_specs=[vm] * n_in, out_specs=tuple([vm] * len(shapes)),
              scratch=scratch)(*bufs, logits, m_logits, v_logits, *small_w, *small_m, *small_v)
    return res[0], res[1], res[2:6], res[6:]


def kernel(x, p, w_in, sgu_w_s, sgu_b_s, sgu_norm_g, sgu_norm_b, hgrn_lb_logits, hgrn_norm_g, w_branch, w_out, ln1_g, ln1_b, ffn_w_up, ffn_conv_w, ffn_conv_b, ffn_w_down, ln2_g, ln2_b, ple_w_proj, ple_w_gate, loss_target, m_w_in, m_sgu_w_s, m_sgu_b_s, m_sgu_norm_g, m_sgu_norm_b, m_hgrn_lb_logits, m_hgrn_norm_g, m_w_branch, m_w_out, m_ln1_g, m_ln1_b, m_ffn_w_up, m_ffn_conv_w, m_ffn_conv_b, m_ffn_w_down, m_ln2_g, m_ln2_b, m_ple_w_proj, m_ple_w_gate, v_w_in, v_sgu_w_s, v_sgu_b_s, v_sgu_norm_g, v_sgu_norm_b, v_hgrn_lb_logits, v_hgrn_norm_g, v_w_branch, v_w_out, v_ln1_g, v_ln1_b, v_ffn_w_up, v_ffn_conv_w, v_ffn_conv_b, v_ffn_w_down, v_ln2_g, v_ln2_b, v_ple_w_proj, v_ple_w_gate):
    t = x.shape[1]
    x2 = x.reshape(t, D_MODEL)
    p2 = p.reshape(t, PLE_DIM)
    tgt = loss_target.reshape(t, D_MODEL)
    core = lax.axis_index("c").astype(jnp.int32).reshape(1)
    chip_id = (2 * lax.axis_index("x") + lax.axis_index("y")).astype(jnp.int32).reshape(1)

    big_w = [w_in[0], w_branch[0, 0], w_branch[0, 1], w_out[0], ffn_w_up[0], ffn_w_down[0],
             ple_w_proj[0], ple_w_gate[0]]
    big_m = [m_w_in[0], m_w_branch[0, 0], m_w_branch[0, 1], m_w_out[0], m_ffn_w_up[0],
             m_ffn_w_down[0], m_ple_w_proj[0], m_ple_w_gate[0]]
    big_v = [v_w_in[0], v_w_branch[0, 0], v_w_branch[0, 1], v_w_out[0], v_ffn_w_up[0],
             v_ffn_w_down[0], v_ple_w_proj[0], v_ple_w_gate[0]]
    def halves_of(i):
        w = big_w[i]
        return w.astype(BF16).reshape(2, w.shape[0] // 2, w.shape[1])

    def stacked(g, i):
        return g.reshape(N_CHIP, big_w[i].shape[0], big_w[i].shape[1])

    win_g, convw_g = _run_comm("gather_w_in", _gather_comm([halves_of(0)], [ffn_conv_w[0]]))
    win_st = stacked(win_g, 0)
    convw = jnp.transpose(convw_g, (1, 0, 2)).reshape(3, D_FF)

    cid = jnp.arange(SGU_BLOCK) // CHUNK
    maskf = (cid[:, None] >= cid[None, :]).astype(F32)
    ws_masked = sgu_w_s[0] * maskf[None]
    wm = ws_masked.astype(BF16)
    wmt = jnp.transpose(ws_masked, (0, 2, 1)).astype(BF16)
    bsb = jnp.broadcast_to(sgu_b_s[0][:, :, None], (N_GROUP, SGU_BLOCK, 128))

    h, (wup_g,) = _mm_nn_stacked("in_proj", x2, win_st, 512, comm=_gather_comm([halves_of(4)]))
    wup_st = stacked(wup_g, 4)
    ya = _sgu_fwd(h, wm, bsb, sgu_norm_g, sgu_norm_b)
    rest = (1, 2, 3, 5, 6, 7)
    (yb, st_all), rest_g = _hgrn_fwd(h, hgrn_lb_logits, hgrn_norm_g,
                                      comm=_gather_comm([halves_of(i) for i in rest]))
    rest_g = [stacked(g, i) for g, i in zip(rest_g, rest)]
    wb0 = rest_g[0].reshape(D_MODEL, D_MODEL)
    wb1 = rest_g[1].reshape(D_MODEL, D_MODEL)
    wo = rest_g[2].reshape(D_MODEL, D_MODEL)
    wd = rest_g[3].reshape(D_FF, D_MODEL)
    wpp = jnp.transpose(rest_g[4], (1, 0, 2)).reshape(PLE_DIM, D_MODEL)
    wpg = rest_g[5].reshape(D_MODEL, D_MODEL)
    r1, a_br, b_br, m_bf, x1b = _mix_fwd(ya, yb, h, x2, wb0, wb1, wo, ln1_g, ln1_b, 256)
    h2 = _mm_nn_stacked("ffn_up", x1b, wup_st, 512)
    act = _ffn_act_fwd(h2, convw, ffn_conv_b, 256)
    dr2, dpg, dpp, loss_acc, dg2, db2 = _out_fwd_bwd(
        act, x1b, r1, p2, tgt, wd, wpg, wpp, ln1_g, ln1_b, ln2_g, ln2_b, 256)

    dact = _mm("ffn_down_bwd", dr2, wd, NT, (t // 512, FF_NJ, 1),
               pl.BlockSpec((512, D_MODEL), lambda i, j, k: (i, 0)),
               pl.BlockSpec((FF_TILE, D_MODEL), lambda i, j, k: (j, 0)),
               jax.ShapeDtypeStruct((t, D_FF), F32),
               pl.BlockSpec((512, FF_TILE), lambda i, j, k: (i, j)))
    dh2, dcw, dcb = _ffn_act_bwd(h2, dact, convw, ffn_conv_b, 256)
    d_wd = _mm_tn("ffn_down_wgrad", act, dr2, FF_TILE, D_MODEL, 512)
    d_wpg = _mm_tn("ple_gate_wgrad", x1b, dpg, D_MODEL, D_MODEL, 512)
    d_wpp_st = _mm_tn("ple_proj_wgrad", p2, dpp, PLE_DIM, PLE_DIM, 512, stacked=True)
    d_wup_st = _mm("ffn_up_wgrad", x1b, dh2, TN, (1, N_CHIP, t // 512),
                   pl.BlockSpec((512, D_MODEL), lambda i, j, k: (k, 0)),
                   pl.BlockSpec((None, 512, FF_TILE), lambda i, j, k: (j // FF_NJ, k, j % FF_NJ)),
                   jax.ShapeDtypeStruct((N_CHIP, D_MODEL, FF_TILE), F32),
                   pl.BlockSpec((None, D_MODEL, FF_TILE), lambda i, j, k: (j, 0, 0)))
    dr1, dg1, db1 = _ffn_in_bwd(dh2, wup_st, dpg, wpg, dr2, r1, ln1_g, 512)
    da_bf, db_bf, dh3, dya, dyb = _mix_bwd(dr1, h, a_br, b_br, wo, wb0, wb1, 256)
    d_wo = _mm_tn("out_proj_wgrad", m_bf, dr1, D_MODEL, D_MODEL, 512)
    d_wb0 = _mm_tn("branch0_wgrad", ya, da_bf, D_MODEL, D_MODEL, 512)
    d_wb1 = _mm_tn("branch1_wgrad", yb, db_bf, D_MODEL, D_MODEL, 512)
    dh0, dws, dbs, dgv, dbv = _sgu_bwd(h, dya, wm, wmt, bsb, sgu_norm_g, sgu_norm_b, maskf)

    grads_1 = [d_wb0.reshape(4, 256, D_MODEL), d_wb1.reshape(4, 256, D_MODEL),
               d_wo.reshape(4, 256, D_MODEL), d_wup_st, d_wd.reshape(4, D_FF // 4, D_MODEL),
               d_wpp_st, d_wpg.reshape(4, 256, D_MODEL)]
    recv_a1 = _rs_sibling_exchange("rs_sibling_exchange1", grads_1)
    parts_1 = [_rs_add_halves("rs_add_halves%d" % (i + 1), g, r, core)
               for i, (g, r) in enumerate(zip(grads_1, recv_a1))]
    (dh1, dh2h, dlb, dgn), recv_b1 = _hgrn_bwd(h, dyb, st_all, hgrn_lb_logits, hgrn_norm_g,
                                                comm=_chip_exchange_comm(parts_1))
    dh_parts = [dh0, dh1, dh2h, dh3]
    d_win = [_mm_tn("in_proj_wgrad%d" % j, x2, dh_parts[j], D_MODEL, 1024, 512) for j in range(4)]
    gx = dr1
    scale = ALPHA
    for j in range(4):
        gx = _mm("in_proj_xgrad%d" % j, dh_parts[j], win_st, NT, (t // 512, 1, 1),
                 pl.BlockSpec((512, 2 * D_MODEL), lambda i, jj, k: (i, 0)),
                 pl.BlockSpec((None, D_MODEL, 2 * D_MODEL), lambda i, jj, k, j=j: (j, 0, 0)),
                 jax.ShapeDtypeStruct((t, D_MODEL), F32),
                 pl.BlockSpec((512, D_MODEL), lambda i, jj, k: (i, 0)),
                 add=gx, add_spec=pl.BlockSpec((512, D_MODEL), lambda i, jj, k: (i, 0)),
                 add_scale=scale)
        scale = 1.0

    grads_0 = [jnp.stack(d_win)]
    recv_a0 = _rs_sibling_exchange("rs_sibling_exchange0", grads_0)
    parts_0 = [_rs_add_halves("rs_add_halves0", grads_0[0], recv_a0[0], core)]
    recv_b0 = _run_comm("rs_chip_exchange0", _chip_exchange_comm(parts_0))
    parts = parts_0 + parts_1
    recv_b = list(recv_b0) + list(recv_b1)
    halves = [_rs_sum_chips("rs_sum_chips%d" % i, pt, r, chip_id)
              for i, (pt, r) in enumerate(zip(parts, recv_b))]
    theirs = _rs_send_halves(halves)
    big_out = [_adamw_rows("adamw_big%d" % i, halves[i], theirs[i], big_w[i], big_m[i], big_v[i], core)
               for i in range(len(halves))]

    buf0 = jnp.concatenate([dgv, dbv, dlb, dgn, dg1, db1, dg2, db2, loss_acc], axis=0)
    buf1 = jnp.concatenate([dws.reshape(N_GROUP * 128, 128), dbs], axis=0)
    buf2 = jnp.concatenate([dcw, dcb], axis=0)
    small_in = dict(sgu_w_s=(sgu_w_s, m_sgu_w_s, v_sgu_w_s), sgu_b_s=(sgu_b_s, m_sgu_b_s, v_sgu_b_s),
                    sgu_norm_g=(sgu_norm_g, m_sgu_norm_g, v_sgu_norm_g),
                    sgu_norm_b=(sgu_norm_b, m_sgu_norm_b, v_sgu_norm_b),
                    hgrn_norm_g=(hgrn_norm_g, m_hgrn_norm_g, v_hgrn_norm_g),
                    ln1_g=(ln1_g, m_ln1_g, v_ln1_g), ln1_b=(ln1_b, m_ln1_b, v_ln1_b),
                    ffn_conv_b=(ffn_conv_b, m_ffn_conv_b, v_ffn_conv_b),
                    ln2_g=(ln2_g, m_ln2_g, v_ln2_g), ln2_b=(ln2_b, m_ln2_b, v_ln2_b))

    def flat(name, arr):
        rows = dict((n, r) for n, r, _, _ in SMALL_LAYOUT)[name]
        return arr.reshape(rows, arr.size // rows)

    names = [n for n, _, _, _ in SMALL_LAYOUT]
    sw = [flat(n, small_in[n][0]) for n in names]
    sm = [flat(n, small_in[n][1]) for n in names]
    sv = [flat(n, small_in[n][2]) for n in names]
    loss_rows, dcw_tot, lg_out, small_out = _small_allreduce_adamw(
        [buf0, buf1, buf2], hgrn_lb_logits, m_hgrn_lb_logits, v_hgrn_lb_logits, sw, sm, sv)
    loss = loss_rows[0, 0]

    chip = 2 * lax.axis_index("x") + lax.axis_index("y")
    g_cw = lax.dynamic_slice(dcw_tot, (0, chip * (D_FF // 4)), (3, D_FF // 4))
    cw_out = _adamw_whole("adamw_conv_w", g_cw, ffn_conv_w[0], m_ffn_conv_w[0], v_ffn_conv_w[0])

    res = {}
    for si, n in enumerate(names):
        shp = small_in[n][0].shape
        res[n] = tuple(small_out[4 * si + k].reshape(shp) for k in range(4))
    res["hgrn_lb_logits"] = tuple(lg_out)
    res["ffn_conv_w"] = (g_cw[None],) + tuple(o[None] for o in cw_out)

    def big(i):
        return tuple(big_out[i])

    res["w_in"] = tuple(o[None] for o in big(0))
    res["w_branch"] = tuple(jnp.stack([o0, o1])[None] for o0, o1 in zip(big(1), big(2)))
    res["w_out"] = tuple(o[None] for o in big(3))
    res["ffn_w_up"] = tuple(o[None] for o in big(4))
    res["ffn_w_down"] = tuple(o[None] for o in big(5))
    res["ple_w_proj"] = tuple(o[None] for o in big(6))
    res["ple_w_gate"] = tuple(o[None] for o in big(7))

    order = ["w_in", "sgu_w_s", "sgu_b_s", "sgu_norm_g", "sgu_norm_b", "hgrn_lb_logits",
             "hgrn_norm_g", "w_branch", "w_out", "ln1_g", "ln1_b", "ffn_w_up", "ffn_conv_w",
             "ffn_conv_b", "ffn_w_down", "ln2_g", "ln2_b", "ple_w_proj", "ple_w_gate"]
    outs = [loss, gx.reshape(1, t, D_MODEL)]
    for k in range(4):
        outs += [res[n][k] for n in order]
    return tuple(outs)
```

```python
import functools

import jax
import jax.numpy as jnp
from jax import lax
from jax.experimental import pallas as pl
from jax.experimental.pallas import tpu as pltpu

F32 = jnp.float32
BF16 = jnp.bfloat16
HIGHEST = lax.Precision.HIGHEST
MESH = pl.DeviceIdType.MESH

D_MODEL = 1024
CHUNK = 64
SGU_BLOCK = 128
N_GROUP = 8
N_HEAD = 8
HEAD_DIM = 128
D_FF = 2816
PLE_DIM = 256
IN_COLS = 8192
LN_EPS = 1e-5
RMS_EPS = 1e-6
ALPHA = 2.0 ** 0.25
N_CHIP = 4
N_DEV = 8

ADAM_LR = 0.001
ADAM_B1 = 0.9
ADAM_B2 = 0.999
ADAM_EPS = 1e-08
ADAM_WD = 0.01
ADAM_STEP = 10

VMEM_LIMIT = 56 * 1024 * 1024

NN = (((1,), (0,)), ((), ()))
NT = (((1,), (1,)), ((), ()))
TN = (((0,), (0,)), ((), ()))


def _pc(body, *, name, out_shape, grid=None, in_specs=None, out_specs=None, scratch=(),
        sem=None, nsp=0, vmem=VMEM_LIMIT):
    params = dict(vmem_limit_bytes=vmem)
    if sem is not None:
        params["dimension_semantics"] = sem
    kw = dict(name=name, out_shape=out_shape, compiler_params=pltpu.CompilerParams(**params))
    if nsp:
        kw["grid_spec"] = pltpu.PrefetchScalarGridSpec(
            num_scalar_prefetch=nsp, grid=grid, in_specs=in_specs, out_specs=out_specs,
            scratch_shapes=list(scratch))
    else:
        if grid is not None:
            kw["grid"] = grid
        if in_specs is not None:
            kw["in_specs"] = in_specs
            kw["out_specs"] = out_specs
        kw["scratch_shapes"] = list(scratch)
    return pl.pallas_call(body, **kw)


def _dot(a, b, dims=NN):
    return lax.dot_general(a.astype(BF16), b.astype(BF16), dims, preferred_element_type=F32)


def _dot32(a, b, dims=NN):
    return lax.dot_general(a, b, dims, precision=HIGHEST, preferred_element_type=F32)


def _sig(x):
    return 1.0 / (1.0 + jnp.exp(-x))


_GC = 0.7978845608028654
_GA = 0.044715


def _gelu(x):
    return 0.5 * x * (1.0 + jnp.tanh(_GC * (x + _GA * x * x * x)))


def _gelu_and_grad(x):
    t = jnp.tanh(_GC * (x + _GA * x * x * x))
    g = 0.5 * x * (1.0 + t)
    dg = 0.5 * (1.0 + t) + 0.5 * x * (1.0 - t * t) * _GC * (1.0 + 3.0 * _GA * x * x)
    return g, dg


def _ln_stats(r):
    mu = jnp.mean(r, axis=-1, keepdims=True)
    xc = r - mu
    var = jnp.mean(xc * xc, axis=-1, keepdims=True)
    rstd = lax.rsqrt(var + LN_EPS)
    return xc * rstd, rstd


def _ln_bwd(dxh, xh, rstd):
    m1 = jnp.mean(dxh, axis=-1, keepdims=True)
    m2 = jnp.mean(dxh * xh, axis=-1, keepdims=True)
    return rstd * (dxh - m1 - xh * m2)


def _colsum8(v):
    return jnp.broadcast_to(jnp.sum(v, axis=0, keepdims=True), (8, v.shape[1]))


def _adamw(w, g, m, v):
    m2 = ADAM_B1 * m + (1.0 - ADAM_B1) * g
    v2 = ADAM_B2 * v + (1.0 - ADAM_B2) * (g * g)
    m_hat = m2 / (1.0 - ADAM_B1 ** ADAM_STEP)
    v_hat = v2 / (1.0 - ADAM_B2 ** ADAM_STEP)
    delta = -ADAM_LR * (m_hat / (jnp.sqrt(v_hat) + ADAM_EPS) + ADAM_WD * w)
    return delta, m2, v2


def _row_tile(rows, cols, itemsize=4, budget=1 << 20, mult=8):
    best = mult
    for tr in range(mult, rows + 1, mult):
        if rows % tr == 0 and tr * cols * itemsize <= budget:
            best = tr
    return best


def _mm(name, a, b, dims, grid, a_spec, b_spec, out_shape, o_spec, add=None, add_spec=None,
        add_scale=1.0, comm=None):
    nk = grid[2]
    has_add = add is not None
    out_dtype = out_shape.dtype

    def body(*refs):
        if has_add:
            a_ref, b_ref, add_ref, o_ref = refs[:4]
            rest = refs[4:]
        else:
            a_ref, b_ref, o_ref = refs[:3]
            add_ref = None
            rest = refs[3:]
        prod = _dot(a_ref[...], b_ref[...], dims)

        def finish(acc):
            if has_add:
                acc = acc + add_scale * add_ref[...]
            o_ref[...] = acc.astype(out_dtype)

        if nk == 1:
            finish(prod)
        else:
            acc_ref = rest[0]
            k = pl.program_id(2)

            @pl.when(k == 0)
            def _():
                acc_ref[...] = prod

            @pl.when(k > 0)
            def _():
                acc_ref[...] += prod

            @pl.when(k == nk - 1)
            def _():
                finish(acc_ref[...])

    in_specs = [a_spec, b_spec] + ([add_spec] if has_add else [])
    args = [a, b] + ([add] if has_add else [])
    scratch = []
    if nk > 1:
        blk = [d for d in o_spec.block_shape if d is not None]
        scratch = [pltpu.VMEM(tuple(blk), F32)]
    if comm is None:
        return _pc(body, name=name, out_shape=out_shape, grid=grid, in_specs=in_specs,
                   out_specs=o_spec, scratch=scratch,
                   sem=("parallel", "parallel", "arbitrary"))(*args)

    def first():
        return (pl.program_id(0) == 0) & (pl.program_id(1) == 0) & (pl.program_id(2) == 0)

    def last():
        return ((pl.program_id(0) == grid[0] - 1) & (pl.program_id(1) == grid[1] - 1)
                & (pl.program_id(2) == grid[2] - 1))

    res = _hosted_call(body, comm, first, last, name=name, out_shape=(out_shape,), grid=grid,
                       in_specs=in_specs, out_specs=(o_spec,), scratch=scratch,
                       sem=("arbitrary", "arbitrary", "arbitrary"), args=args)
    return res[0], res[1:]


class _Comm:
    def __init__(self, ins, out_shapes, sems, start, finish):
        self.ins, self.out_shapes, self.sems = list(ins), list(out_shapes), list(sems)
        self.start, self.finish = start, finish


def _hosted_call(body, comm, first, last, *, name, out_shape, grid, in_specs, out_specs, scratch, sem,
                 args):
    n_in, n_out, n_scr = len(in_specs), len(out_shape), len(scratch)
    nci, nco = len(comm.ins), len(comm.out_shapes)

    def wrapped(*refs):
        pos = n_in
        own_in, c_in = refs[:pos], refs[pos:pos + nci]
        pos += nci
        own_out, c_out = refs[pos:pos + n_out], refs[pos + n_out:pos + n_out + nco]
        pos += n_out + nco
        own_scr, c_sem = refs[pos:pos + n_scr], refs[pos + n_scr:]

        @pl.when(first())
        def _():
            comm.start(c_in, c_out, c_sem)

        body(*own_in, *own_out, *own_scr)

        @pl.when(last())
        def _():
            comm.finish(c_in, c_out, c_sem)

    return _pc(wrapped, name=name, out_shape=tuple(out_shape) + tuple(comm.out_shapes), grid=grid,
               in_specs=list(in_specs) + [ANY] * nci, out_specs=tuple(out_specs) + tuple([ANY] * nco),
               scratch=list(scratch) + comm.sems, sem=sem)(*args, *comm.ins)


def _grid1_call(body, comm, n, *, name, out_shape, in_specs, out_specs, scratch, args):
    if comm is None:
        return _pc(body, name=name, out_shape=out_shape, grid=(n,), in_specs=in_specs,
                   out_specs=out_specs, scratch=scratch, sem=("arbitrary",))(*args), ()
    res = _hosted_call(body, comm, lambda: pl.program_id(0) == 0, lambda: pl.program_id(0) == n - 1,
                       name=name, out_shape=out_shape, grid=(n,), in_specs=in_specs,
                       out_specs=out_specs, scratch=scratch, sem=("arbitrary",), args=args)
    return res[:len(out_shape)], res[len(out_shape):]


def _run_comm(name, comm):
    nci, nco = len(comm.ins), len(comm.out_shapes)

    def body(*refs):
        c_in, c_out, c_sem = refs[:nci], refs[nci:nci + nco], refs[nci + nco:]
        comm.start(c_in, c_out, c_sem)
        comm.finish(c_in, c_out, c_sem)

    return _pc(body, name=name, out_shape=tuple(comm.out_shapes), in_specs=[ANY] * nci,
               out_specs=tuple([ANY] * nco), scratch=comm.sems)(*comm.ins)


def _mm_nn_stacked(name, a, w_st, tm, comm=None):
    t, k = a.shape
    _, _, c = w_st.shape
    return _mm(name, a, w_st, NN, (t // tm, N_CHIP, 1),
               pl.BlockSpec((tm, k), lambda i, j, kk: (i, 0)),
               pl.BlockSpec((None, k, c), lambda i, j, kk: (j, 0, 0)),
               jax.ShapeDtypeStruct((t, N_CHIP * c), F32),
               pl.BlockSpec((tm, c), lambda i, j, kk: (i, j)), comm=comm)


def _mm_tn(name, a, b, tm, tn, tk, stacked=False):
    t, m = a.shape
    _, n = b.shape
    if stacked:
        assert tm == m
        out_shape = jax.ShapeDtypeStruct((n // tn, m, tn), F32)
        o_spec = pl.BlockSpec((None, tm, tn), lambda i, j, kk: (j, 0, 0))
    else:
        out_shape = jax.ShapeDtypeStruct((m, n), F32)
        o_spec = pl.BlockSpec((tm, tn), lambda i, j, kk: (i, j))
    return _mm(name, a, b, TN, (m // tm, n // tn, t // tk),
               pl.BlockSpec((tk, tm), lambda i, j, kk: (kk, i)),
               pl.BlockSpec((tk, tn), lambda i, j, kk: (kk, j)),
               out_shape, o_spec)


def _sgu_mixed(v, wm_ref, bsb_ref, gv, bv):
    gl, dgl = _gelu_and_grad(v)
    vh, rstd = _ln_stats(gl)
    vn = vh * gv + bv
    mixed = []
    for g in range(N_GROUP):
        sl = slice(g * 128, (g + 1) * 128)
        mixed.append(_dot(wm_ref[g], vn[:, sl]) + bsb_ref[g])
    return dgl, vh, rstd, vn, mixed


def _sgu_fwd(h, wm, bsb, gv, bv):
    t = h.shape[0]

    def body(u_ref, v_ref, wm_ref, bsb_ref, gv_ref, bv_ref, ya_ref):
        u = u_ref[...]
        _, _, _, _, mixed = _sgu_mixed(v_ref[...], wm_ref, bsb_ref, gv_ref[...], bv_ref[...])
        gu = _gelu(u)
        for g in range(N_GROUP):
            sl = slice(g * 128, (g + 1) * 128)
            ya_ref[:, sl] = (gu[:, sl] * mixed[g]).astype(BF16)

    full3 = pl.BlockSpec((N_GROUP, 128, 128), lambda i: (0, 0, 0))
    vec = pl.BlockSpec((1, D_MODEL), lambda i: (0, 0))
    return _pc(body, name="sgu_fwd", out_shape=jax.ShapeDtypeStruct((t, D_MODEL), BF16),
               grid=(t // SGU_BLOCK,),
               in_specs=[pl.BlockSpec((SGU_BLOCK, D_MODEL), lambda i: (i, 0)),
                         pl.BlockSpec((SGU_BLOCK, D_MODEL), lambda i: (i, 1)),
                         full3, full3, vec, vec],
               out_specs=pl.BlockSpec((SGU_BLOCK, D_MODEL), lambda i: (i, 0)),
               sem=("parallel",))(h, h, wm, bsb, gv, bv)


def _sgu_bwd(h, dya, wm, wmt, bsb, gv, bv, maskf):
    t = h.shape[0]
    nb = t // SGU_BLOCK

    def body(u_ref, v_ref, dya_ref, wm_ref, wmt_ref, bsb_ref, gv_ref, bv_ref, mask_ref,
             dh_ref, dws_ref, dbs_ref, dgv_ref, dbv_ref, dmix_acc):
        i = pl.program_id(0)

        @pl.when(i == 0)
        def _():
            dws_ref[...] = jnp.zeros_like(dws_ref)
            dgv_ref[...] = jnp.zeros_like(dgv_ref)
            dbv_ref[...] = jnp.zeros_like(dbv_ref)
            dmix_acc[...] = jnp.zeros_like(dmix_acc)

        u = u_ref[...]
        gvv = gv_ref[...]
        dgl_v, vh, rstd, vn, mixed = _sgu_mixed(v_ref[...], wm_ref, bsb_ref, gvv, bv_ref[...])
        gu, dgl_u = _gelu_and_grad(u)
        dya_v = dya_ref[...]
        dvn_parts = []
        for g in range(N_GROUP):
            sl = slice(g * 128, (g + 1) * 128)
            d_y = dya_v[:, sl]
            dh_ref[:, sl] = (d_y * mixed[g] * dgl_u[:, sl]).astype(BF16)
            d_mixed = d_y * gu[:, sl]
            dmix_acc[g] += d_mixed
            dws_ref[g] += _dot(d_mixed, vn[:, sl], NT) * mask_ref[...]
            dvn_parts.append(_dot(wmt_ref[g], d_mixed))
        dvn = jnp.concatenate(dvn_parts, axis=1)
        dgv_ref[...] += _colsum8(dvn * vh)
        dbv_ref[...] += _colsum8(dvn)
        d_gl = _ln_bwd(dvn * gvv, vh, rstd)
        dh_ref[:, D_MODEL:] = (d_gl * dgl_v).astype(BF16)

        @pl.when(i == nb - 1)
        def _():
            rowid = lax.broadcasted_iota(jnp.int32, (8, 128), 0)
            ones = jnp.ones((8, 128), F32)
            acc = jnp.zeros((8, 128), F32)
            for g in range(N_GROUP):
                rs = _dot32(ones, dmix_acc[g], NT)
                acc = jnp.where(rowid == g, rs, acc)
            dbs_ref[...] = acc

    full3 = pl.BlockSpec((N_GROUP, 128, 128), lambda i: (0, 0, 0))
    vec = pl.BlockSpec((1, D_MODEL), lambda i: (0, 0))
    acc8 = pl.BlockSpec((8, D_MODEL), lambda i: (0, 0))
    return _pc(body, name="sgu_bwd",
               out_shape=(jax.ShapeDtypeStruct((t, 2 * D_MODEL), BF16),
                          jax.ShapeDtypeStruct((N_GROUP, 128, 128), F32),
                          jax.ShapeDtypeStruct((8, 128), F32),
                          jax.ShapeDtypeStruct((8, D_MODEL), F32),
                          jax.ShapeDtypeStruct((8, D_MODEL), F32)),
               grid=(nb,),
               in_specs=[pl.BlockSpec((SGU_BLOCK, D_MODEL), lambda i: (i, 0)),
                         pl.BlockSpec((SGU_BLOCK, D_MODEL), lambda i: (i, 1)),
                         pl.BlockSpec((SGU_BLOCK, D_MODEL), lambda i: (i, 0)),
                         full3, full3, full3, vec, vec,
                         pl.BlockSpec((128, 128), lambda i: (0, 0))],
               out_specs=(pl.BlockSpec((SGU_BLOCK, 2 * D_MODEL), lambda i: (i, 0)),
                          full3, pl.BlockSpec((8, 128), lambda i: (0, 0)), acc8, acc8),
               scratch=[pltpu.VMEM((N_GROUP, 128, 128), F32)],
               sem=("arbitrary",))(h, h, dya, wm, wmt, bsb, gv, bv, maskf)


def _tri_masks():
    row = lax.broadcasted_iota(jnp.int32, (CHUNK, CHUNK), 0)
    col = lax.broadcasted_iota(jnp.int32, (CHUNK, CHUNK), 1)
    return col <= row, col >= row


def _heads(v):
    return [v[:, hd * HEAD_DIM:(hd + 1) * HEAD_DIM] for hd in range(N_HEAD)]


def _tri_cumsum(tri_bf, v):
    hi = v.astype(BF16)
    r = v - hi.astype(F32)
    mid = r.astype(BF16)
    lo = (r - mid.astype(F32)).astype(BF16)
    return _dot(tri_bf, hi) + _dot(tri_bf, mid) + _dot(tri_bf, lo)


def _hgrn_chunk(q, fp, ii, lb, st_heads, causal):
    sg = _sig(fp)
    f = lb + (1.0 - lb) * sg
    k = 1.0 - f
    c = _tri_cumsum(causal.astype(BF16), jnp.log(f))
    ec = jnp.exp(c)
    en = jnp.exp(-c)
    sq = _sig(q)
    qt = q * sq * ec
    kt = k * en
    ecl = jnp.exp(c[CHUNK - 1:CHUNK, :])
    kk = kt * ecl
    qtb, ktb, iib, kkb = qt.astype(BF16), kt.astype(BF16), ii.astype(BF16), kk.astype(BF16)
    attn, o = [], []
    for hd, (qh, kh, ih) in enumerate(zip(_heads(qtb), _heads(ktb), _heads(iib))):
        a = jnp.where(causal, _dot(qh, kh, NT), 0.0).astype(BF16)
        attn.append(a)
        o.append(_dot(a, ih) + _dot(qh, st_heads[hd], NT))
    return dict(sg=sg, f=f, k=k, ec=ec, en=en, sq=sq, ecl=ecl, kk=kk, qtb=qtb, ktb=ktb, iib=iib,
                kkb=kkb, attn=attn, o=o)


def _rms_heads(o_heads):
    rinv = [lax.rsqrt(jnp.mean(o * o, axis=-1, keepdims=True) + RMS_EPS) for o in o_heads]
    return rinv, jnp.concatenate([o * r for o, r in zip(o_heads, rinv)], axis=1)


def _hgrn_fwd(h, logits, gn, comm=None):
    t = h.shape[0]
    nc = t // CHUNK

    def body(q_ref, f_ref, i_ref, og_ref, lg_ref, gn_ref, yb_ref, st_ref, state):
        ci = pl.program_id(0)

        @pl.when(ci == 0)
        def _():
            state[...] = jnp.zeros_like(state)

        causal, _ = _tri_masks()
        st = [state[hd] for hd in range(N_HEAD)]
        og = og_ref[...]
        lb = _sig(lg_ref[0:1, :] - lg_ref[1:2, :])
        r = _hgrn_chunk(q_ref[...], f_ref[...], i_ref[...], lb, [s.astype(BF16) for s in st], causal)
        _, on = _rms_heads(r["o"])
        ecl = _heads(r["ecl"])
        new = [s * e + _dot(ih, kh, TN)
               for s, e, ih, kh in zip(st, ecl, _heads(r["iib"]), _heads(r["kkb"]))]
        yb_ref[...] = (on * gn_ref[...] * (og * _sig(og))).astype(BF16)
        for hd in range(N_HEAD):
            st_ref[0, hd] = st[hd]
            state[hd] = new[hd]

    def col(k):
        return pl.BlockSpec((CHUNK, D_MODEL), lambda ci: (ci, k))

    return _grid1_call(body, comm, nc, name="hgrn_fwd",
                       out_shape=(jax.ShapeDtypeStruct((t, D_MODEL), BF16),
                                  jax.ShapeDtypeStruct((nc, N_HEAD, HEAD_DIM, HEAD_DIM), F32)),
                       in_specs=[col(2), col(3), col(4), col(5),
                                 pl.BlockSpec((2, D_MODEL), lambda ci: (0, 0)),
                                 pl.BlockSpec((1, D_MODEL), lambda ci: (0, 0))],
                       out_specs=(pl.BlockSpec((CHUNK, D_MODEL), lambda ci: (ci, 0)),
                                  pl.BlockSpec((1, N_HEAD, HEAD_DIM, HEAD_DIM), lambda ci: (ci, 0, 0, 0))),
                       scratch=[pltpu.VMEM((N_HEAD, HEAD_DIM, HEAD_DIM), F32)],
                       args=(h, h, h, h, logits, gn))


def _hgrn_bwd(h, dyb, st_all, logits, gn, comm=None):
    t = h.shape[0]
    nc = t // CHUNK

    def body(q_ref, f_ref, i_ref, og_ref, dyb_ref, st_ref, lg_ref, gn_ref,
             dh1_ref, dh2_ref, dlb_ref, dgn_ref, dstate):
        ci = pl.program_id(0)

        @pl.when(ci == 0)
        def _():
            dstate[...] = jnp.zeros_like(dstate)
            dlb_ref[...] = jnp.zeros_like(dlb_ref)
            dgn_ref[...] = jnp.zeros_like(dgn_ref)

        causal, anti = _tri_masks()
        q, og, dy, gnv = q_ref[...], og_ref[...], dyb_ref[...], gn_ref[...]
        lb = _sig(lg_ref[0:1, :] - lg_ref[1:2, :])
        st = [st_ref[0, hd] for hd in range(N_HEAD)]
        dsn = [dstate[hd] for hd in range(N_HEAD)]
        stb = [s.astype(BF16) for s in st]
        dsnb = [s.astype(BF16) for s in dsn]
        r = _hgrn_chunk(q, f_ref[...], i_ref[...], lb, stb, causal)
        rinv, on = _rms_heads(r["o"])
        so = _sig(og)
        sil = og * so
        d_og = dy * on * gnv * (so * (1.0 + og * (1.0 - so)))
        d_on = dy * gnv * sil
        d_ob = jnp.concatenate(
            [ri * (dn - oh * jnp.mean(dn * oh, axis=-1, keepdims=True))
             for ri, dn, oh in zip(rinv, _heads(d_on), _heads(on))], axis=1).astype(BF16)
        d_i, d_qt, d_kt, d_kk, d_st, st_dsn = [], [], [], [], [], []
        ecl = _heads(r["ecl"])
        for hd, (dh, qh, kh, ih, kkh) in enumerate(zip(_heads(d_ob), _heads(r["qtb"]), _heads(r["ktb"]),
                                                       _heads(r["iib"]), _heads(r["kkb"]))):
            d_attn = jnp.where(causal, _dot(dh, ih, NT), 0.0).astype(BF16)
            d_i.append(_dot(r["attn"][hd], dh, TN) + _dot(kkh, dsnb[hd], NT))
            d_qt.append(_dot(d_attn, kh) + _dot(dh, stb[hd]))
            d_kt.append(_dot(d_attn, qh, TN))
            d_kk.append(_dot(ih, dsnb[hd]))
            d_st.append(_dot(dh, qh, TN) + dsn[hd] * ecl[hd])
            st_dsn.append(jnp.sum(st[hd] * dsn[hd], axis=0, keepdims=True))
        d_qt = jnp.concatenate(d_qt, axis=1)
        d_kt = jnp.concatenate(d_kt, axis=1)
        d_kk = jnp.concatenate(d_kk, axis=1)
        kk = r["kk"]
        d_cl = (r["ecl"] * jnp.concatenate(st_dsn, axis=1)
                + jnp.sum(kk * d_kk, axis=0, keepdims=True))
        d_k = (d_kk * r["ecl"] + d_kt) * r["en"]
        d_c = d_qt * r["qtb"].astype(F32) - d_kt * r["ktb"].astype(F32) - d_kk * kk
        rowid = lax.broadcasted_iota(jnp.int32, (CHUNK, D_MODEL), 0)
        d_c = d_c + jnp.where(rowid == CHUNK - 1, d_cl, 0.0)
        d_lf = _tri_cumsum(anti.astype(BF16), d_c)
        d_f = d_lf / r["f"] - d_k
        sg = r["sg"]
        sq = r["sq"]
        dgn_ref[...] += _colsum8(dy * on * sil)
        dlb_ref[...] += _colsum8(d_f * (1.0 - sg))
        dh1_ref[:, :D_MODEL] = (d_qt * r["ec"] * (sq * (1.0 + q * (1.0 - sq)))).astype(BF16)
        dh1_ref[:, D_MODEL:] = (d_f * (1.0 - lb) * sg * (1.0 - sg)).astype(BF16)
        dh2_ref[:, :D_MODEL] = jnp.concatenate(d_i, axis=1).astype(BF16)
        dh2_ref[:, D_MODEL:] = d_og.astype(BF16)
        for hd in range(N_HEAD):
            dstate[hd] = d_st[hd]

    def col(k):
        return pl.BlockSpec((CHUNK, D_MODEL), lambda ci: (nc - 1 - ci, k))

    acc8 = pl.BlockSpec((8, D_MODEL), lambda ci: (0, 0))
    pair = pl.BlockSpec((CHUNK, 2 * D_MODEL), lambda ci: (nc - 1 - ci, 0))
    return _grid1_call(body, comm, nc, name="hgrn_bwd",
                       out_shape=(jax.ShapeDtypeStruct((t, 2 * D_MODEL), BF16),
                                  jax.ShapeDtypeStruct((t, 2 * D_MODEL), BF16),
                                  jax.ShapeDtypeStruct((8, D_MODEL), F32),
                                  jax.ShapeDtypeStruct((8, D_MODEL), F32)),
                       in_specs=[col(2), col(3), col(4), col(5),
                                 pl.BlockSpec((CHUNK, D_MODEL), lambda ci: (nc - 1 - ci, 0)),
                                 pl.BlockSpec((1, N_HEAD, HEAD_DIM, HEAD_DIM),
                                              lambda ci: (nc - 1 - ci, 0, 0, 0)),
                                 pl.BlockSpec((2, D_MODEL), lambda ci: (0, 0)),
                                 pl.BlockSpec((1, D_MODEL), lambda ci: (0, 0))],
                       out_specs=(pair, pair, acc8, acc8),
                       scratch=[pltpu.VMEM((N_HEAD, HEAD_DIM, HEAD_DIM), F32)],
                       args=(h, h, h, h, dyb, st_all, logits, gn))


def _mix_fwd(ya, yb, h, x, wb0, wb1, wo, g1, b1, tm):
    t = x.shape[0]

    def body(ya_ref, yb_ref, ga_ref, gb_ref, x_ref, wb0_ref, wb1_ref, wo_ref, g1_ref, b1_ref,
             r1_ref, a_ref, b_ref, m_ref, x1_ref):
        a = _dot(ya_ref[...], wb0_ref[...])
        b = _dot(yb_ref[...], wb1_ref[...])
        m = _sig(ga_ref[...]) * a + _sig(gb_ref[...]) * b
        r1 = ALPHA * x_ref[...] + _dot(m, wo_ref[...])
        xh, _ = _ln_stats(r1)
        r1_ref[...] = r1
        a_ref[...] = a
        b_ref[...] = b
        m_ref[...] = m.astype(BF16)
        x1_ref[...] = (xh * g1_ref[...] + b1_ref[...]).astype(BF16)

    tile = pl.BlockSpec((tm, D_MODEL), lambda i: (i, 0))
    wsp = pl.BlockSpec((D_MODEL, D_MODEL), lambda i: (0, 0))
    vec = pl.BlockSpec((1, D_MODEL), lambda i: (0, 0))
    f32o = jax.ShapeDtypeStruct((t, D_MODEL), F32)
    bfo = jax.ShapeDtypeStruct((t, D_MODEL), BF16)
    return _pc(body, name="mix_fwd", out_shape=(f32o, f32o, f32o, bfo, bfo), grid=(t // tm,),
               in_specs=[tile, tile,
                         pl.BlockSpec((tm, D_MODEL), lambda i: (i, 6)),
                         pl.BlockSpec((tm, D_MODEL), lambda i: (i, 7)),
                         tile, wsp, wsp, wsp, vec, vec],
               out_specs=(tile, tile, tile, tile, tile),
               sem=("parallel",))(ya, yb, h, h, x, wb0, wb1, wo, g1, b1)


def _mix_bwd(dr1, h, a, b, wo, wb0, wb1, tm):
    t = dr1.shape[0]

    def body(dr1_ref, ga_ref, gb_ref, a_ref, b_ref, wo_ref, wb0_ref, wb1_ref,
             da_ref, db_ref, dh3_ref, dya_ref, dyb_ref):
        d_m = _dot(dr1_ref[...], wo_ref[...], NT)
        sa = _sig(ga_ref[...])
        sb = _sig(gb_ref[...])
        d_a = (d_m * sa).astype(BF16)
        d_b = (d_m * sb).astype(BF16)
        da_ref[...] = d_a
        db_ref[...] = d_b
        dh3_ref[:, :D_MODEL] = (d_m * a_ref[...] * sa * (1.0 - sa)).astype(BF16)
        dh3_ref[:, D_MODEL:] = (d_m * b_ref[...] * sb * (1.0 - sb)).astype(BF16)
        dya_ref[...] = _dot(d_a, wb0_ref[...], NT)
        dyb_ref[...] = _dot(d_b, wb1_ref[...], NT)

    tile = pl.BlockSpec((tm, D_MODEL), lambda i: (i, 0))
    wsp = pl.BlockSpec((D_MODEL, D_MODEL), lambda i: (0, 0))
    f32o = jax.ShapeDtypeStruct((t, D_MODEL), F32)
    bfo = jax.ShapeDtypeStruct((t, D_MODEL), BF16)
    return _pc(body, name="mix_bwd",
               out_shape=(bfo, bfo, jax.ShapeDtypeStruct((t, 2 * D_MODEL), BF16), f32o, f32o),
               grid=(t // tm,),
               in_specs=[tile,
                         pl.BlockSpec((tm, D_MODEL), lambda i: (i, 6)),
                         pl.BlockSpec((tm, D_MODEL), lambda i: (i, 7)),
                         tile, tile, wsp, wsp, wsp],
               out_specs=(tile, tile, pl.BlockSpec((tm, 2 * D_MODEL), lambda i: (i, 0)),
                          tile, tile),
               sem=("parallel",))(dr1, h, h, a, b, wo, wb0, wb1)


FF_TILE = 1408
FF_NJ = D_FF // FF_TILE


def _shift_down(v, k):
    return pltpu.roll(v, k, 0)


def _shift_up(v, k):
    return pltpu.roll(v, v.shape[0] - k, 0)


def _conv_gate(ext, cw_ref, cb_ref):
    return (cw_ref[0:1, :] * _shift_down(ext, 2) + cw_ref[1:2, :] * _shift_down(ext, 1)
            + cw_ref[2:3, :] * ext + cb_ref[...])


def _ffn_act_fwd(h2, convw, convb, tm):
    t = h2.shape[0]
    nt8 = tm // 8

    def body(g_ref, gp_ref, v_ref, cw_ref, cb_ref, act_ref):
        i = pl.program_id(1)
        prev = gp_ref[...] * (i > 0).astype(F32)
        ext = jnp.concatenate([prev, g_ref[...]], axis=0)
        gc = _conv_gate(ext, cw_ref, cb_ref)[8:, :]
        act_ref[...] = (_gelu(gc) * v_ref[...]).astype(BF16)

    return _pc(body, name="ffn_act_fwd", out_shape=jax.ShapeDtypeStruct((t, D_FF), BF16),
               grid=(FF_NJ, t // tm),
               in_specs=[pl.BlockSpec((tm, FF_TILE), lambda j, i: (i, j)),
                         pl.BlockSpec((8, FF_TILE), lambda j, i: (jnp.maximum(i * nt8 - 1, 0), j)),
                         pl.BlockSpec((tm, FF_TILE), lambda j, i: (i, j + FF_NJ)),
                         pl.BlockSpec((3, FF_TILE), lambda j, i: (0, j)),
                         pl.BlockSpec((1, FF_TILE), lambda j, i: (0, j))],
               out_specs=pl.BlockSpec((tm, FF_TILE), lambda j, i: (i, j)),
               sem=("parallel", "parallel"))(h2, h2, h2, convw, convb)


def _ffn_act_bwd(h2, dact, convw, convb, tm):
    t = h2.shape[0]
    nt8 = tm // 8
    ni = t // tm
    last8 = t // 8 - 1

    def body(g_ref, gp_ref, gn_ref, v_ref, vn_ref, da_ref, dan_ref, cw_ref, cb_ref,
             dh2_ref, dcw_ref, dcb_ref):
        i = pl.program_id(1)

        @pl.when(i == 0)
        def _():
            dcw_ref[...] = jnp.zeros_like(dcw_ref)
            dcb_ref[...] = jnp.zeros_like(dcb_ref)

        prev = gp_ref[...] * (i > 0).astype(F32)
        ext = jnp.concatenate([prev, g_ref[...], gn_ref[...]], axis=0)
        vext = jnp.concatenate([jnp.zeros((8, FF_TILE), F32), v_ref[...], vn_ref[...]], axis=0)
        dnext = dan_ref[...] * (i < ni - 1).astype(F32)
        dext = jnp.concatenate([jnp.zeros((8, FF_TILE), F32), da_ref[...], dnext], axis=0)
        g2 = _shift_down(ext, 2)
        g1 = _shift_down(ext, 1)
        gc = cw_ref[0:1, :] * g2 + cw_ref[1:2, :] * g1 + cw_ref[2:3, :] * ext + cb_ref[...]
        gl, dgl = _gelu_and_grad(gc)
        d_gc = dext * vext * dgl
        d_gate = (cw_ref[2:3, :] * d_gc + cw_ref[1:2, :] * _shift_up(d_gc, 1)
                  + cw_ref[0:1, :] * _shift_up(d_gc, 2))
        dh2_ref[0] = d_gate[8:8 + tm, :].astype(BF16)
        dh2_ref[1] = (da_ref[...] * gl[8:8 + tm, :]).astype(BF16)
        dm = d_gc[8:8 + tm, :]
        s0 = jnp.sum(dm * g2[8:8 + tm, :], axis=0, keepdims=True)
        s1 = jnp.sum(dm * g1[8:8 + tm, :], axis=0, keepdims=True)
        s2 = jnp.sum(dm * ext[8:8 + tm, :], axis=0, keepdims=True)
        rowid = lax.broadcasted_iota(jnp.int32, (8, FF_TILE), 0)
        dcw_ref[...] += jnp.where(rowid == 0, s0, jnp.where(rowid == 1, s1,
                                                            jnp.where(rowid == 2, s2, 0.0)))
        dcb_ref[...] += _colsum8(dm)

    def prev8(off):
        return pl.BlockSpec((8, FF_TILE), lambda j, i: (jnp.maximum(i * nt8 - 1, 0), j + off))

    def next8(off):
        return pl.BlockSpec((8, FF_TILE), lambda j, i: (jnp.minimum((i + 1) * nt8, last8), j + off))

    def main(off):
        return pl.BlockSpec((tm, FF_TILE), lambda j, i: (i, j + off))

    acc = pl.BlockSpec((8, FF_TILE), lambda j, i: (0, j))
    return _pc(body, name="ffn_act_bwd",
               out_shape=(jax.ShapeDtypeStruct((2, t, D_FF), BF16),
                          jax.ShapeDtypeStruct((8, D_FF), F32),
                          jax.ShapeDtypeStruct((8, D_FF), F32)),
               grid=(FF_NJ, ni),
               in_specs=[main(0), prev8(0), next8(0), main(FF_NJ), next8(FF_NJ),
                         pl.BlockSpec((tm, FF_TILE), lambda j, i: (i, j)),
                         pl.BlockSpec((8, FF_TILE), lambda j, i: (jnp.minimum((i + 1) * nt8, last8), j)),
                         pl.BlockSpec((3, FF_TILE), lambda j, i: (0, j)),
                         pl.BlockSpec((1, FF_TILE), lambda j, i: (0, j))],
               out_specs=(pl.BlockSpec((2, tm, FF_TILE), lambda j, i: (0, i, j)), acc, acc),
               sem=("parallel", "arbitrary"))(h2, h2, h2, h2, h2, dact, dact, convw, convb)


def _out_fwd_bwd(act, x1b, r1, p2, tgt, wd, wpg, wpp, g1, b1, g2, b2, tm):
    t = r1.shape[0]

    def body(act_ref, x1b_ref, r1_ref, p_ref, tgt_ref, wd_ref, wpg_ref, wpp_ref,
             g1_ref, b1_ref, g2_ref, b2_ref,
             dr2_ref, dpg_ref, dpp_ref, loss_ref, dg2_ref, db2_ref):
        i = pl.program_id(0)

        @pl.when(i == 0)
        def _():
            loss_ref[...] = jnp.zeros_like(loss_ref)
            dg2_ref[...] = jnp.zeros_like(dg2_ref)
            db2_ref[...] = jnp.zeros_like(db2_ref)

        ffn = _dot(act_ref[...], wd_ref[...])
        pg = _dot(x1b_ref[...], wpg_ref[...])
        pp = _dot(p_ref[...], wpp_ref[...])
        s = _sig(pg)
        xh1, _ = _ln_stats(r1_ref[...])
        x1 = xh1 * g1_ref[...] + b1_ref[...]
        r2 = ALPHA * x1 + ffn + s * pp
        xh2, rstd2 = _ln_stats(r2)
        g2v = g2_ref[...]
        diff = xh2 * g2v + b2_ref[...] - tgt_ref[...]
        part = jnp.sum(jnp.sum(diff * diff, axis=1, keepdims=True), axis=0, keepdims=True)
        loss_ref[...] += jnp.broadcast_to(part * (0.5 / D_MODEL), loss_ref.shape)
        dy = diff * (1.0 / D_MODEL)
        dg2_ref[...] += _colsum8(dy * xh2)
        db2_ref[...] += _colsum8(dy)
        dr2 = _ln_bwd(dy * g2v, xh2, rstd2)
        dr2_ref[...] = dr2
        dpg_ref[...] = (dr2 * pp * s * (1.0 - s)).astype(BF16)
        dpp_ref[...] = (dr2 * s).astype(BF16)

    tile = pl.BlockSpec((tm, D_MODEL), lambda i: (i, 0))
    vec = pl.BlockSpec((1, D_MODEL), lambda i: (0, 0))
    acc8 = pl.BlockSpec((8, D_MODEL), lambda i: (0, 0))
    acc_shape = jax.ShapeDtypeStruct((8, D_MODEL), F32)
    return _pc(body, name="out_fwd_bwd",
               out_shape=(jax.ShapeDtypeStruct((t, D_MODEL), F32),
                          jax.ShapeDtypeStruct((t, D_MODEL), BF16),
                          jax.ShapeDtypeStruct((t, D_MODEL), BF16),
                          acc_shape, acc_shape, acc_shape),
               grid=(t // tm,),
               in_specs=[pl.BlockSpec((tm, D_FF), lambda i: (i, 0)), tile, tile,
                         pl.BlockSpec((tm, PLE_DIM), lambda i: (i, 0)), tile,
                         pl.BlockSpec((D_FF, D_MODEL), lambda i: (0, 0)),
                         pl.BlockSpec((D_MODEL, D_MODEL), lambda i: (0, 0)),
                         pl.BlockSpec((PLE_DIM, D_MODEL), lambda i: (0, 0)),
                         vec, vec, vec, vec],
               out_specs=(tile, tile, tile, acc8, acc8, acc8),
               sem=("arbitrary",))(act, x1b, r1, p2, tgt, wd, wpg, wpp, g1, b1, g2, b2)


def _ffn_in_bwd(dh2, wup_st, dpg, wpg, dr2, r1, g1, tm):
    t = r1.shape[0]
    ni = t // tm

    def body(dh2_ref, wup_ref, dpg_ref, wpg_ref, dr2_ref, r1_ref, g1_ref,
             dr1_ref, dg1_ref, db1_ref, acc):
        i = pl.program_id(0)
        j = pl.program_id(1)

        @pl.when((i == 0) & (j == 0))
        def _():
            dg1_ref[...] = jnp.zeros_like(dg1_ref)
            db1_ref[...] = jnp.zeros_like(db1_ref)

        @pl.when(j == 0)
        def _():
            acc[...] = _dot(dh2_ref[...], wup_ref[...], NT)

        @pl.when(j > 0)
        def _():
            acc[...] += _dot(dh2_ref[...], wup_ref[...], NT)

        @pl.when(j == N_CHIP - 1)
        def _():
            d_x1 = acc[...] + _dot(dpg_ref[...], wpg_ref[...], NT) + ALPHA * dr2_ref[...]
            xh, rstd = _ln_stats(r1_ref[...])
            dg1_ref[...] += _colsum8(d_x1 * xh)
            db1_ref[...] += _colsum8(d_x1)
            dr1_ref[...] = _ln_bwd(d_x1 * g1_ref[...], xh, rstd)

    tile = pl.BlockSpec((tm, D_MODEL), lambda i, j: (i, 0))
    acc8 = pl.BlockSpec((8, D_MODEL), lambda i, j: (0, 0))
    acc_shape = jax.ShapeDtypeStruct((8, D_MODEL), F32)
    return _pc(body, name="ffn_in_bwd",
               out_shape=(jax.ShapeDtypeStruct((t, D_MODEL), F32), acc_shape, acc_shape),
               grid=(ni, N_CHIP),
               in_specs=[pl.BlockSpec((None, tm, FF_TILE), lambda i, j: (j // FF_NJ, i, j % FF_NJ)),
                         pl.BlockSpec((None, D_MODEL, FF_TILE), lambda i, j: (j, 0, 0)),
                         tile, pl.BlockSpec((D_MODEL, D_MODEL), lambda i, j: (0, 0)),
                         tile, tile, pl.BlockSpec((1, D_MODEL), lambda i, j: (0, 0))],
               out_specs=(tile, acc8, acc8),
               scratch=[pltpu.VMEM((tm, D_MODEL), F32)],
               sem=("arbitrary", "arbitrary"))(dh2, wup_st, dpg, wpg, dr2, r1, g1)


ANY = pl.BlockSpec(memory_space=pl.ANY)


def _chip_peers():
    x, y, c = lax.axis_index("x"), lax.axis_index("y"), lax.axis_index("c")
    return x, y, c, [(1 - x, y), (x, 1 - y), (1 - x, 1 - y)]


def _gather_comm(halved, whole=()):
    n, nw = len(halved), len(whole)

    def copies(ins, outs, sems):
        ici_send, ici_recv, d2d_send, d2d_recv, own_send, own_recv = sems
        x, y, c, peers = _chip_peers()
        me = 2 * x + y
        sibling = (x, y, 1 - c)
        own, ici, ici_wait, fwd, fwd_wait = [], [], [], [], []
        for ti in range(n + nw):
            src, dst = ins[ti], outs[ti]
            own.append(pltpu.make_async_remote_copy(
                src_ref=src, dst_ref=dst.at[me], send_sem=own_send.at[ti], recv_sem=own_recv.at[ti],
                device_id=sibling, device_id_type=MESH))
            for k, (px, py) in enumerate(peers):
                pk = 2 * px + py
                sem = dict(send_sem=ici_send.at[ti * 3 + k], recv_sem=ici_recv.at[ti * 3 + k],
                           device_id=(px, py, c), device_id_type=MESH)
                if ti < n:
                    ici.append(pltpu.make_async_remote_copy(src_ref=src.at[c], dst_ref=dst.at[me, c], **sem))
                    ici_wait.append(pltpu.make_async_remote_copy(src_ref=src.at[c], dst_ref=dst.at[pk, c], **sem))
                    dsem = dict(send_sem=d2d_send.at[ti * 3 + k], recv_sem=d2d_recv.at[ti * 3 + k],
                                device_id=sibling, device_id_type=MESH)
                    fwd.append(pltpu.make_async_remote_copy(src_ref=dst.at[pk, c], dst_ref=dst.at[pk, c], **dsem))
                    fwd_wait.append(pltpu.make_async_remote_copy(
                        src_ref=dst.at[pk, 1 - c], dst_ref=dst.at[pk, 1 - c], **dsem))
                else:
                    ici.append(pltpu.make_async_remote_copy(src_ref=src, dst_ref=dst.at[me], **sem))
                    ici_wait.append(pltpu.make_async_remote_copy(src_ref=src, dst_ref=dst.at[pk], **sem))
        return own, ici, ici_wait, fwd, fwd_wait

    def start(ins, outs, sems):
        own, ici, _, _, _ = copies(ins, outs, sems)
        for cp in own + ici:
            cp.start()

    def finish(ins, outs, sems):
        own, ici, ici_wait, fwd, fwd_wait = copies(ins, outs, sems)
        for i, cp in enumerate(ici_wait):
            cp.wait_recv()
            if i < len(fwd):
                fwd[i].start()
        for cp in fwd_wait + own:
            cp.wait_recv()
        for cp in own + ici + fwd:
            cp.wait_send()

    srcs = list(halved) + list(whole)
    return _Comm(srcs, [jax.ShapeDtypeStruct((N_CHIP,) + s.shape, s.dtype) for s in srcs],
                 [pltpu.SemaphoreType.DMA((3 * (n + nw),)), pltpu.SemaphoreType.DMA((3 * (n + nw),)),
                  pltpu.SemaphoreType.DMA((max(3 * n, 1),)), pltpu.SemaphoreType.DMA((max(3 * n, 1),)),
                  pltpu.SemaphoreType.DMA((n + nw,)), pltpu.SemaphoreType.DMA((n + nw,))],
                 start, finish)


def _rs_sibling_exchange(name, grads):
    n = len(grads)

    def body(*refs):
        ins, outs = refs[:n], refs[n:2 * n]
        send_sems, recv_sems = refs[2 * n:]
        x, y, c = lax.axis_index("x"), lax.axis_index("y"), lax.axis_index("c")
        sends = []
        for ti in range(n):
            half = ins[ti].shape[1] // 2
            cp = pltpu.make_async_remote_copy(
                src_ref=ins[ti].at[:, pl.ds(pl.multiple_of((1 - c) * half, 8), half), :],
                dst_ref=outs[ti],
                send_sem=send_sems.at[ti], recv_sem=recv_sems.at[ti],
                device_id=(x, y, 1 - c), device_id_type=MESH)
            cp.start()
            sends.append(cp)
        for cp in sends:
            cp.wait()

    return _pc(body, name=name,
               out_shape=tuple(jax.ShapeDtypeStruct((N_CHIP, g.shape[1] // 2, g.shape[2]), g.dtype)
                               for g in grads),
               in_specs=[ANY] * n, out_specs=tuple([ANY] * n),
               scratch=[pltpu.SemaphoreType.DMA((n,)), pltpu.SemaphoreType.DMA((n,))])(*grads)


def _rs_add_halves(name, grad, recv, core):
    _, r, cdim = grad.shape
    half = r // 2
    tr = _row_tile(half, cdim, mult=16)
    nr = half // tr

    def body(c_ref, g_ref, r_ref, o_ref):
        o_ref[...] = (g_ref[...] + r_ref[...]).astype(BF16)

    return _pc(body, name=name, out_shape=jax.ShapeDtypeStruct((N_CHIP, half, cdim), BF16),
               grid=(N_CHIP, nr), nsp=1,
               in_specs=[pl.BlockSpec((None, tr, cdim), lambda j, i, c_ref: (j, c_ref[0] * nr + i, 0)),
                         pl.BlockSpec((None, tr, cdim), lambda j, i, c_ref: (j, i, 0))],
               out_specs=pl.BlockSpec((None, tr, cdim), lambda j, i, c_ref: (j, i, 0)),
               sem=("parallel", "parallel"))(core, grad, recv)


def _chip_exchange_comm(parts):
    n = len(parts)

    def copies(ins, outs, sems):
        send_sems, recv_sems = sems
        x, y, c, peers = _chip_peers()
        return [pltpu.make_async_remote_copy(
            src_ref=ins[ti].at[2 * px + py], dst_ref=outs[ti].at[k],
            send_sem=send_sems.at[ti * 3 + k], recv_sem=recv_sems.at[ti * 3 + k],
            device_id=(px, py, c), device_id_type=MESH)
            for ti in range(n) for k, (px, py) in enumerate(peers)]

    def start(ins, outs, sems):
        for cp in copies(ins, outs, sems):
            cp.start()

    def finish(ins, outs, sems):
        for cp in copies(ins, outs, sems):
            cp.wait()

    return _Comm(parts, [jax.ShapeDtypeStruct((3,) + p.shape[1:], p.dtype) for p in parts],
                 [pltpu.SemaphoreType.DMA((3 * n,)), pltpu.SemaphoreType.DMA((3 * n,))], start, finish)


def _rs_sum_chips(name, part, recv, chip):
    _, half, cdim = recv.shape
    tr = _row_tile(half, cdim, mult=16)

    def body(chip_ref, p_ref, r_ref, o_ref):
        o_ref[...] = ((p_ref[...].astype(F32) + r_ref[0].astype(F32)) + r_ref[1].astype(F32)
                      ) + r_ref[2].astype(F32)

    return _pc(body, name=name, out_shape=jax.ShapeDtypeStruct((half, cdim), F32),
               grid=(half // tr,), nsp=1,
               in_specs=[pl.BlockSpec((None, tr, cdim), lambda i, chip_ref: (chip_ref[0], i, 0)),
                         pl.BlockSpec((3, tr, cdim), lambda i, chip_ref: (0, i, 0))],
               out_specs=pl.BlockSpec((tr, cdim), lambda i, chip_ref: (i, 0)),
               sem=("parallel",))(chip, part, recv)


def _rs_send_halves(halves):
    n = len(halves)

    def body(*refs):
        ins, outs = refs[:n], refs[n:2 * n]
        send_sems, recv_sems = refs[2 * n:]
        x, y, c = lax.axis_index("x"), lax.axis_index("y"), lax.axis_index("c")
        sends = []
        for ti in range(n):
            cp = pltpu.make_async_remote_copy(
                src_ref=ins[ti], dst_ref=outs[ti],
                send_sem=send_sems.at[ti], recv_sem=recv_sems.at[ti],
                device_id=(x, y, 1 - c), device_id_type=MESH)
            cp.start()
            sends.append(cp)
        for cp in sends:
            cp.wait()

    return _pc(body, name="rs_send_halves",
               out_shape=tuple(jax.ShapeDtypeStruct(hv.shape, hv.dtype) for hv in halves),
               in_specs=[ANY] * n, out_specs=tuple([ANY] * n),
               scratch=[pltpu.SemaphoreType.DMA((n,)), pltpu.SemaphoreType.DMA((n,))])(*halves)


def _adamw_rows(name, mine, theirs, w, m, v, core):
    half, cdim = mine.shape
    tr = _row_tile(half, cdim, budget=1 << 19)
    nrh = half // tr

    def body(c_ref, mine_ref, theirs_ref, w_ref, m_ref, v_ref, g_ref, d_ref, m2_ref, v2_ref):
        is_mine = (pl.program_id(0) // nrh) == c_ref[0]
        g = jnp.where(is_mine, mine_ref[...], theirs_ref[...])
        d, m2, v2 = _adamw(w_ref[...], g, m_ref[...], v_ref[...])
        g_ref[...] = g
        d_ref[...] = d
        m2_ref[...] = m2
        v2_ref[...] = v2

    htile = pl.BlockSpec((tr, cdim), lambda i, c_ref: (i % nrh, 0))
    tile = pl.BlockSpec((tr, cdim), lambda i, c_ref: (i, 0))
    shp = jax.ShapeDtypeStruct((2 * half, cdim), F32)
    return _pc(body, name=name, out_shape=(shp, shp, shp, shp), grid=(2 * nrh,), nsp=1,
               in_specs=[htile, htile, tile, tile, tile], out_specs=(tile, tile, tile, tile),
               sem=("parallel",))(core, mine, theirs, w, m, v)


def _adamw_whole(name, g, w, m, v):
    def body(g_ref, w_ref, m_ref, v_ref, d_ref, m2_ref, v2_ref):
        d, m2, v2 = _adamw(w_ref[...], g_ref[...], m_ref[...], v_ref[...])
        d_ref[...] = d
        m2_ref[...] = m2
        v2_ref[...] = v2

    shp = jax.ShapeDtypeStruct(g.shape, F32)
    return _pc(body, name=name, out_shape=(shp, shp, shp))(g, w, m, v)


SMALL_LAYOUT = (
    ("sgu_w_s", 1024, 1, 0),
    ("sgu_b_s", 8, 1, 1024),
    ("sgu_norm_g", 1, 0, 0),
    ("sgu_norm_b", 1, 0, 8),
    ("hgrn_norm_g", 1, 0, 24),
    ("ln1_g", 1, 0, 32),
    ("ln1_b", 1, 0, 40),
    ("ffn_conv_b", 1, 2, 8),
    ("ln2_g", 1, 0, 48),
    ("ln2_b", 1, 0, 56),
)
LB_ROW = 16
LOSS_ROW = 64


def _small_allreduce_adamw(bufs, logits, m_logits, v_logits, small_w, small_m, small_v):
    ns = len(SMALL_LAYOUT)
    nb = len(bufs)

    def body(*refs):
        buf_refs = refs[:nb]
        lg_ref, mlg_ref, vlg_ref = refs[nb:nb + 3]
        pos = nb + 3
        w_refs = refs[pos:pos + ns]
        m_refs = refs[pos + ns:pos + 2 * ns]
        v_refs = refs[pos + 2 * ns:pos + 3 * ns]
        pos += 3 * ns
        loss_ref, dcw_ref = refs[pos:pos + 2]
        lg_outs = refs[pos + 2:pos + 6]
        pos += 6
        outs = refs[pos:pos + 4 * ns]
        pos += 4 * ns
        gath = refs[pos:pos + nb]
        send_sems, recv_sems = refs[pos + nb:pos + nb + 2]

        x, y, c = lax.axis_index("x"), lax.axis_index("y"), lax.axis_index("c")
        me = 4 * x + 2 * y + c
        sends = []
        for d in range(1, N_DEV):
            peer = (x ^ (d >> 2), y ^ ((d >> 1) & 1), c ^ (d & 1))
            for b in range(nb):
                cp = pltpu.make_async_remote_copy(
                    src_ref=buf_refs[b], dst_ref=gath[b].at[me],
                    send_sem=send_sems.at[(d - 1) * nb + b], recv_sem=recv_sems.at[(d - 1) * nb + b],
                    device_id=peer, device_id_type=MESH)
                cp.start()
                sends.append(cp)
        for b in range(nb):
            gath[b][me] = buf_refs[b][...]
        for d in range(1, N_DEV):
            peer = (x ^ (d >> 2), y ^ ((d >> 1) & 1), c ^ (d & 1))
            src = 4 * peer[0] + 2 * peer[1] + peer[2]
            for b in range(nb):
                pltpu.make_async_remote_copy(
                    src_ref=buf_refs[b], dst_ref=gath[b].at[src],
                    send_sem=send_sems.at[(d - 1) * nb + b], recv_sem=recv_sems.at[(d - 1) * nb + b],
                    device_id=peer, device_id_type=MESH).wait_recv()
        for cp in sends:
            cp.wait_send()

        tot = []
        for b in range(nb):
            s = gath[b][0]
            for d in range(1, N_DEV):
                s = s + gath[b][d]
            tot.append(s)

        loss_ref[...] = tot[0][LOSS_ROW:LOSS_ROW + 8, :]
        dcw_ref[...] = tot[2][0:8, :]
        lb = _sig(lg_ref[0:1, :] - lg_ref[1:2, :])
        d0 = tot[0][LB_ROW:LB_ROW + 1, :] * lb * (1.0 - lb)
        rowid = lax.broadcasted_iota(jnp.int32, (2, D_MODEL), 0)
        g_lg = jnp.where(rowid == 0, d0, -d0)
        dl, ml, vl = _adamw(lg_ref[...], g_lg, mlg_ref[...], vlg_ref[...])
        lg_outs[0][...] = g_lg
        lg_outs[1][...] = dl
        lg_outs[2][...] = ml
        lg_outs[3][...] = vl
        for si, (_, rows, b, r0) in enumerate(SMALL_LAYOUT):
            g = tot[b][r0:r0 + rows, :]
            dl, ml, vl = _adamw(w_refs[si][...], g, m_refs[si][...], v_refs[si][...])
            outs[4 * si][...] = g
            outs[4 * si + 1][...] = dl
            outs[4 * si + 2][...] = ml
            outs[4 * si + 3][...] = vl

    shapes = [jax.ShapeDtypeStruct((8, D_MODEL), F32), jax.ShapeDtypeStruct((8, D_FF), F32)]
    shapes += [jax.ShapeDtypeStruct((2, D_MODEL), F32)] * 4
    for w in small_w:
        shapes += [jax.ShapeDtypeStruct(w.shape, F32)] * 4
    scratch = [pltpu.VMEM((N_DEV,) + b.shape, F32) for b in bufs]
    scratch += [pltpu.SemaphoreType.DMA(((N_DEV - 1) * nb,)), pltpu.SemaphoreType.DMA(((N_DEV - 1) * nb,))]
    vm = pl.BlockSpec(memory_space=pltpu.VMEM)
    n_in = nb + 3 + 3 * ns
    res = _pc(body, name="small_allreduce_adamw", out_shape=tuple(shapes),
              in_specs=[vm] * n_in, out_specs=tuple([vm] * len(shapes)),
              scratch=scratch)(*bufs, logits, m_logits, v_logits, *small_w, *small_m, *small_v)
    return res[0], res[1], res[2:6], res[6:]


def kernel(x, p, w_in, sgu_w_s, sgu_b_s, sgu_norm_g, sgu_norm_b, hgrn_lb_logits, hgrn_norm_g, w_branch, w_out, ln1_g, ln1_b, ffn_w_up, ffn_conv_w, ffn_conv_b, ffn_w_down, ln2_g, ln2_b, ple_w_proj, ple_w_gate, loss_target, m_w_in, m_sgu_w_s, m_sgu_b_s, m_sgu_norm_g, m_sgu_norm_b, m_hgrn_lb_logits, m_hgrn_norm_g, m_w_branch, m_w_out, m_ln1_g, m_ln1_b, m_ffn_w_up, m_ffn_conv_w, m_ffn_conv_b, m_ffn_w_down, m_ln2_g, m_ln2_b, m_ple_w_proj, m_ple_w_gate, v_w_in, v_sgu_w_s, v_sgu_b_s, v_sgu_norm_g, v_sgu_norm_b, v_hgrn_lb_logits, v_hgrn_norm_g, v_w_branch, v_w_out, v_ln1_g, v_ln1_b, v_ffn_w_up, v_ffn_conv_w, v_ffn_conv_b, v_ffn_w_down, v_ln2_g, v_ln2_b, v_ple_w_proj, v_ple_w_gate):
    t = x.shape[1]
    x2 = x.reshape(t, D_MODEL)
    p2 = p.reshape(t, PLE_DIM)
    tgt = loss_target.reshape(t, D_MODEL)
    core = lax.axis_index("c").astype(jnp.int32).reshape(1)
    chip_id = (2 * lax.axis_index("x") + lax.axis_index("y")).astype(jnp.int32).reshape(1)

    big_w = [w_in[0], w_branch[0, 0], w_branch[0, 1], w_out[0], ffn_w_up[0], ffn_w_down[0],
             ple_w_proj[0], ple_w_gate[0]]
    big_m = [m_w_in[0], m_w_branch[0, 0], m_w_branch[0, 1], m_w_out[0], m_ffn_w_up[0],
             m_ffn_w_down[0], m_ple_w_proj[0], m_ple_w_gate[0]]
    big_v = [v_w_in[0], v_w_branch[0, 0], v_w_branch[0, 1], v_w_out[0], v_ffn_w_up[0],
             v_ffn_w_down[0], v_ple_w_proj[0], v_ple_w_gate[0]]
    def halves_of(i):
        w = big_w[i]
        return w.astype(BF16).reshape(2, w.shape[0] // 2, w.shape[1])

    def stacked(g, i):
        return g.reshape(N_CHIP, big_w[i].shape[0], big_w[i].shape[1])

    win_g, convw_g = _run_comm("gather_w_in", _gather_comm([halves_of(0)], [ffn_conv_w[0]]))
    win_st = stacked(win_g, 0)
    convw = jnp.transpose(convw_g, (1, 0, 2)).reshape(3, D_FF)

    cid = jnp.arange(SGU_BLOCK) // CHUNK
    maskf = (cid[:, None] >= cid[None, :]).astype(F32)
    ws_masked = sgu_w_s[0] * maskf[None]
    wm = ws_masked.astype(BF16)
    wmt = jnp.transpose(ws_masked, (0, 2, 1)).astype(BF16)
    bsb = jnp.broadcast_to(sgu_b_s[0][:, :, None], (N_GROUP, SGU_BLOCK, 128))

    h, (wup_g,) = _mm_nn_stacked("in_proj", x2, win_st, 512, comm=_gather_comm([halves_of(4)]))
    wup_st = stacked(wup_g, 4)
    ya = _sgu_fwd(h, wm, bsb, sgu_norm_g, sgu_norm_b)
    rest = (1, 2, 3, 5, 6, 7)
    (yb, st_all), rest_g = _hgrn_fwd(h, hgrn_lb_logits, hgrn_norm_g,
                                      comm=_gather_comm([halves_of(i) for i in rest]))
    rest_g = [stacked(g, i) for g, i in zip(rest_g, rest)]
    wb0 = rest_g[0].reshape(D_MODEL, D_MODEL)
    wb1 = rest_g[1].reshape(D_MODEL, D_MODEL)
    wo = rest_g[2].reshape(D_MODEL, D_MODEL)
    wd = rest_g[3].reshape(D_FF, D_MODEL)
    wpp = jnp.transpose(rest_g[4], (1, 0, 2)).reshape(PLE_DIM, D_MODEL)
    wpg = rest_g[5].reshape(D_MODEL, D_MODEL)
    r1, a_br, b_br, m_bf, x1b = _mix_fwd(ya, yb, h, x2, wb0, wb1, wo, ln1_g, ln1_b, 256)
    h2 = _mm_nn_stacked("ffn_up", x1b, wup_st, 512)
    act = _ffn_act_fwd(h2, convw, ffn_conv_b, 256)
    dr2, dpg, dpp, loss_acc, dg2, db2 = _out_fwd_bwd(
        act, x1b, r1, p2, tgt, wd, wpg, wpp, ln1_g, ln1_b, ln2_g, ln2_b, 256)

    dact = _mm("ffn_down_bwd", dr2, wd, NT, (t // 512, FF_NJ, 1),
               pl.BlockSpec((512, D_MODEL), lambda i, j, k: (i, 0)),
               pl.BlockSpec((FF_TILE, D_MODEL), lambda i, j, k: (j, 0)),
               jax.ShapeDtypeStruct((t, D_FF), F32),
               pl.BlockSpec((512, FF_TILE), lambda i, j, k: (i, j)))
    dh2, dcw, dcb = _ffn_act_bwd(h2, dact, convw, ffn_conv_b, 256)
    d_wd = _mm_tn("ffn_down_wgrad", act, dr2, FF_TILE, D_MODEL, 512)
    d_wpg = _mm_tn("ple_gate_wgrad", x1b, dpg, D_MODEL, D_MODEL, 512)
    d_wpp_st = _mm_tn("ple_proj_wgrad", p2, dpp, PLE_DIM, PLE_DIM, 512, stacked=True)
    d_wup_st = _mm("ffn_up_wgrad", x1b, dh2, TN, (1, N_CHIP, t // 512),
                   pl.BlockSpec((512, D_MODEL), lambda i, j, k: (k, 0)),
                   pl.BlockSpec((None, 512, FF_TILE), lambda i, j, k: (j // FF_NJ, k, j % FF_NJ)),
                   jax.ShapeDtypeStruct((N_CHIP, D_MODEL, FF_TILE), F32),
                   pl.BlockSpec((None, D_MODEL, FF_TILE), lambda i, j, k: (j, 0, 0)))
    dr1, dg1, db1 = _ffn_in_bwd(dh2, wup_st, dpg, wpg, dr2, r1, ln1_g, 512)
    da_bf, db_bf, dh3, dya, dyb = _mix_bwd(dr1, h, a_br, b_br, wo, wb0, wb1, 256)
    d_wo = _mm_tn("out_proj_wgrad", m_bf, dr1, D_MODEL, D_MODEL, 512)
    d_wb0 = _mm_tn("branch0_wgrad", ya, da_bf, D_MODEL, D_MODEL, 512)
    d_wb1 = _mm_tn("branch1_wgrad", yb, db_bf, D_MODEL, D_MODEL, 512)
    dh0, dws, dbs, dgv, dbv = _sgu_bwd(h, dya, wm, wmt, bsb, sgu_norm_g, sgu_norm_b, maskf)

    grads_1 = [d_wb0.reshape(4, 256, D_MODEL), d_wb1.reshape(4, 256, D_MODEL),
               d_wo.reshape(4, 256, D_MODEL), d_wup_st, d_wd.reshape(4, D_FF // 4, D_MODEL),
               d_wpp_st, d_wpg.reshape(4, 256, D_MODEL)]
    recv_a1 = _rs_sibling_exchange("rs_sibling_exchange1", grads_1)
    parts_1 = [_rs_add_halves("rs_add_halves%d" % (i + 1), g, r, core)
               for i, (g, r) in enumerate(zip(grads_1, recv_a1))]
    (dh1, dh2h, dlb, dgn), recv_b1 = _hgrn_bwd(h, dyb, st_all, hgrn_lb_logits, hgrn_norm_g,
                                                comm=_chip_exchange_comm(parts_1))
    dh_parts = [dh0, dh1, dh2h, dh3]
    d_win = [_mm_tn("in_proj_wgrad%d" % j, x2, dh_parts[j], D_MODEL, 1024, 512) for j in range(4)]
    gx = dr1
    scale = ALPHA
    for j in range(4):
        gx = _mm("in_proj_xgrad%d" % j, dh_parts[j], win_st, NT, (t // 512, 1, 1),
                 pl.BlockSpec((512, 2 * D_MODEL), lambda i, jj, k: (i, 0)),
                 pl.BlockSpec((None, D_MODEL, 2 * D_MODEL), lambda i, jj, k, j=j: (j, 0, 0)),
                 jax.ShapeDtypeStruct((t, D_MODEL), F32),
                 pl.BlockSpec((512, D_MODEL), lambda i, jj, k: (i, 0)),
                 add=gx, add_spec=pl.BlockSpec((512, D_MODEL), lambda i, jj, k: (i, 0)),
                 add_scale=scale)
        scale = 1.0

    grads_0 = [jnp.stack(d_win)]
    recv_a0 = _rs_sibling_exchange("rs_sibling_exchange0", grads_0)
    parts_0 = [_rs_add_halves("rs_add_halves0", grads_0[0], recv_a0[0], core)]
    recv_b0 = _run_comm("rs_chip_exchange0", _chip_exchange_comm(parts_0))
    parts = parts_0 + parts_1
    recv_b = list(recv_b0) + list(recv_b1)
    halves = [_rs_sum_chips("rs_sum_chips%d" % i, pt, r, chip_id)
              for i, (pt, r) in enumerate(zip(parts, recv_b))]
    theirs = _rs_send_halves(halves)
    big_out = [_adamw_rows("adamw_big%d" % i, halves[i], theirs[i], big_w[i], big_m[i], big_v[i], core)
               for i in range(len(halves))]

    buf0 = jnp.concatenate([dgv, dbv, dlb, dgn, dg1, db1, dg2, db2, loss_acc], axis=0)
    buf1 = jnp.concatenate([dws.reshape(N_GROUP * 128, 128), dbs], axis=0)
    buf2 = jnp.concatenate([dcw, dcb], axis=0)
    small_in = dict(sgu_w_s=(sgu_w_s, m_sgu_w_s, v_sgu_w_s), sgu_b_s=(sgu_b_s, m_sgu_b_s, v_sgu_b_s),
                    sgu_norm_g=(sgu_norm_g, m_sgu_norm_g, v_sgu_norm_g),
                    sgu_norm_b=(sgu_norm_b, m_sgu_norm_b, v_sgu_norm_b),
                    hgrn_norm_g=(hgrn_norm_g, m_hgrn_norm_g, v_hgrn_norm_g),
                    ln1_g=(ln1_g, m_ln1_g, v_ln1_g), ln1_b=(ln1_b, m_ln1_b, v_ln1_b),
                    ffn_conv_b=(ffn_conv_b, m_ffn_conv_b, v_ffn_conv_b),
                    ln2_g=(ln2_g, m_ln2_g, v_ln2_g), ln2_b=(ln2_b, m_ln2_b, v_ln2_b))

    def flat(name, arr):
        rows = dict((n, r) for n, r, _, _ in SMALL_LAYOUT)[name]
        return arr.reshape(rows, arr.size // rows)

    names = [n for n, _, _, _ in SMALL_LAYOUT]
    sw = [flat(n, small_in[n][0]) for n in names]
    sm = [flat(n, small_in[n][1]) for n in names]
    sv = [flat(n, small_in[n][2]) for n in names]
    loss_rows, dcw_tot, lg_out, small_out = _small_allreduce_adamw(
        [buf0, buf1, buf2], hgrn_lb_logits, m_hgrn_lb_logits, v_hgrn_lb_logits, sw, sm, sv)
    loss = loss_rows[0, 0]

    chip = 2 * lax.axis_index("x") + lax.axis_index("y")
    g_cw = lax.dynamic_slice(dcw_tot, (0, chip * (D_FF // 4)), (3, D_FF // 4))
    cw_out = _adamw_whole("adamw_conv_w", g_cw, ffn_conv_w[0], m_ffn_conv_w[0], v_ffn_conv_w[0])

    res = {}
    for si, n in enumerate(names):
        shp = small_in[n][0].shape
        res[n] = tuple(small_out[4 * si + k].reshape(shp) for k in range(4))
    res["hgrn_lb_logits"] = tuple(lg_out)
    res["ffn_conv_w"] = (g_cw[None],) + tuple(o[None] for o in cw_out)

    def big(i):
        return tuple(big_out[i])

    res["w_in"] = tuple(o[None] for o in big(0))
    res["w_branch"] = tuple(jnp.stack([o0, o1])[None] for o0, o1 in zip(big(1), big(2)))
    res["w_out"] = tuple(o[None] for o in big(3))
    res["ffn_w_up"] = tuple(o[None] for o in big(4))
    res["ffn_w_down"] = tuple(o[None] for o in big(5))
    res["ple_w_proj"] = tuple(o[None] for o in big(6))
    res["ple_w_gate"] = tuple(o[None] for o in big(7))

    order = ["w_in", "sgu_w_s", "sgu_b_s", "sgu_norm_g", "sgu_norm_b", "hgrn_lb_logits",
             "hgrn_norm_g", "w_branch", "w_out", "ln1_g", "ln1_b", "ffn_w_up", "ffn_conv_w",
             "ffn_conv_b", "ffn_w_down", "ln2_g", "ln2_b", "ple_w_proj", "ple_w_gate"]
    outs = [loss, gx.reshape(1, t, D_MODEL)]
    for k in range(4):
        outs += [res[n][k] for n in order]
    return tuple(outs)
```

```python
import functools

import jax
import jax.numpy as jnp
from jax import lax
from jax.experimental import pallas as pl
from jax.experimental.pallas import tpu as pltpu

F32 = jnp.float32
BF16 = jnp.bfloat16
HIGHEST = lax.Precision.HIGHEST
MESH = pl.DeviceIdType.MESH

D_MODEL = 1024
CHUNK = 64
SGU_BLOCK = 128
N_GROUP = 8
N_HEAD = 8
HEAD_DIM = 128
D_FF = 2816
PLE_DIM = 256
IN_COLS = 8192
LN_EPS = 1e-5
RMS_EPS = 1e-6
ALPHA = 2.0 ** 0.25
N_CHIP = 4
N_DEV = 8

ADAM_LR = 0.001
ADAM_B1 = 0.9
ADAM_B2 = 0.999
ADAM_EPS = 1e-08
ADAM_WD = 0.01
ADAM_STEP = 10

VMEM_LIMIT = 56 * 1024 * 1024

NN = (((1,), (0,)), ((), ()))
NT = (((1,), (1,)), ((), ()))
TN = (((0,), (0,)), ((), ()))


def _pc(body, *, name, out_shape, grid=None, in_specs=None, out_specs=None, scratch=(),
        sem=None, nsp=0, vmem=VMEM_LIMIT):
    params = dict(vmem_limit_bytes=vmem)
    if sem is not None:
        params["dimension_semantics"] = sem
    kw = dict(name=name, out_shape=out_shape, compiler_params=pltpu.CompilerParams(**params))
    if nsp:
        kw["grid_spec"] = pltpu.PrefetchScalarGridSpec(
            num_scalar_prefetch=nsp, grid=grid, in_specs=in_specs, out_specs=out_specs,
            scratch_shapes=list(scratch))
    else:
        if grid is not None:
            kw["grid"] = grid
        if in_specs is not None:
            kw["in_specs"] = in_specs
            kw["out_specs"] = out_specs
        kw["scratch_shapes"] = list(scratch)
    return pl.pallas_call(body, **kw)


def _dot(a, b, dims=NN):
    return lax.dot_general(a.astype(BF16), b.astype(BF16), dims, preferred_element_type=F32)


def _dot32(a, b, dims=NN):
    return lax.dot_general(a, b, dims, precision=HIGHEST, preferred_element_type=F32)


def _sig(x):
    return 1.0 / (1.0 + jnp.exp(-x))


_GC = 0.7978845608028654
_GA = 0.044715


def _gelu(x):
    return 0.5 * x * (1.0 + jnp.tanh(_GC * (x + _GA * x * x * x)))


def _gelu_and_grad(x):
    t = jnp.tanh(_GC * (x + _GA * x * x * x))
    g = 0.5 * x * (1.0 + t)
    dg = 0.5 * (1.0 + t) + 0.5 * x * (1.0 - t * t) * _GC * (1.0 + 3.0 * _GA * x * x)
    return g, dg


def _ln_stats(r):
    mu = jnp.mean(r, axis=-1, keepdims=True)
    xc = r - mu
    var = jnp.mean(xc * xc, axis=-1, keepdims=True)
    rstd = lax.rsqrt(var + LN_EPS)
    return xc * rstd, rstd


def _ln_bwd(dxh, xh, rstd):
    m1 = jnp.mean(dxh, axis=-1, keepdims=True)
    m2 = jnp.mean(dxh * xh, axis=-1, keepdims=True)
    return rstd * (dxh - m1 - xh * m2)


def _colsum8(v):
    return jnp.broadcast_to(jnp.sum(v, axis=0, keepdims=True), (8, v.shape[1]))


def _adamw(w, g, m, v):
    m2 = ADAM_B1 * m + (1.0 - ADAM_B1) * g
    v2 = ADAM_B2 * v + (1.0 - ADAM_B2) * (g * g)
    m_hat = m2 / (1.0 - ADAM_B1 ** ADAM_STEP)
    v_hat = v2 / (1.0 - ADAM_B2 ** ADAM_STEP)
    delta = -ADAM_LR * (m_hat / (jnp.sqrt(v_hat) + ADAM_EPS) + ADAM_WD * w)
    return delta, m2, v2


def _row_tile(rows, cols, itemsize=4, budget=1 << 20, mult=8):
    best = mult
    for tr in range(mult, rows + 1, mult):
        if rows % tr == 0 and tr * cols * itemsize <= budget:
            best = tr
    return best


def _mm(name, a, b, dims, grid, a_spec, b_spec, out_shape, o_spec, add=None, add_spec=None,
        add_scale=1.0, comm=None):
    nk = grid[2]
    has_add = add is not None
    out_dtype = out_shape.dtype

    def body(*refs):
        if has_add:
            a_ref, b_ref, add_ref, o_ref = refs[:4]
            rest = refs[4:]
        else:
            a_ref, b_ref, o_ref = refs[:3]
            add_ref = None
            rest = refs[3:]
        prod = _dot(a_ref[...], b_ref[...], dims)

        def finish(acc):
            if has_add:
                acc = acc + add_scale * add_ref[...]
            o_ref[...] = acc.astype(out_dtype)

        if nk == 1:
            finish(prod)
        else:
            acc_ref = rest[0]
            k = pl.program_id(2)

            @pl.when(k == 0)
            def _():
                acc_ref[...] = prod

            @pl.when(k > 0)
            def _():
                acc_ref[...] += prod

            @pl.when(k == nk - 1)
            def _():
                finish(acc_ref[...])

    in_specs = [a_spec, b_spec] + ([add_spec] if has_add else [])
    args = [a, b] + ([add] if has_add else [])
    scratch = []
    if nk > 1:
        blk = [d for d in o_spec.block_shape if d is not None]
        scratch = [pltpu.VMEM(tuple(blk), F32)]
    if comm is None:
        return _pc(body, name=name, out_shape=out_shape, grid=grid, in_specs=in_specs,
                   out_specs=o_spec, scratch=scratch,
                   sem=("parallel", "parallel", "arbitrary"))(*args)

    def first():
        return (pl.program_id(0) == 0) & (pl.program_id(1) == 0) & (pl.program_id(2) == 0)

    def last():
        return ((pl.program_id(0) == grid[0] - 1) & (pl.program_id(1) == grid[1] - 1)
                & (pl.program_id(2) == grid[2] - 1))

    res = _hosted_call(body, comm, first, last, name=name, out_shape=(out_shape,), grid=grid,
                       in_specs=in_specs, out_specs=(o_spec,), scratch=scratch,
                       sem=("arbitrary", "arbitrary", "arbitrary"), args=args)
    return res[0], res[1:]


class _Comm:
    def __init__(self, ins, out_shapes, sems, start, finish):
        self.ins, self.out_shapes, self.sems = list(ins), list(out_shapes), list(sems)
        self.start, self.finish = start, finish


def _hosted_call(body, comm, first, last, *, name, out_shape, grid, in_specs, out_specs, scratch, sem,
                 args):
    n_in, n_out, n_scr = len(in_specs), len(out_shape), len(scratch)
    nci, nco = len(comm.ins), len(comm.out_shapes)

    def wrapped(*refs):
        pos = n_in
        own_in, c_in = refs[:pos], refs[pos:pos + nci]
        pos += nci
        own_out, c_out = refs[pos:pos + n_out], refs[pos + n_out:pos + n_out + nco]
        pos += n_out + nco
        own_scr, c_sem = refs[pos:pos + n_scr], refs[pos + n_scr:]

        @pl.when(first())
        def _():
            comm.start(c_in, c_out, c_sem)

        body(*own_in, *own_out, *own_scr)

        @pl.when(last())
        def _():
            comm.finish(c_in, c_out, c_sem)

    return _pc(wrapped, name=name, out_shape=tuple(out_shape) + tuple(comm.out_shapes), grid=grid,
               in_specs=list(in_specs) + [ANY] * nci, out_specs=tuple(out_specs) + tuple([ANY] * nco),
               scratch=list(scratch) + comm.sems, sem=sem)(*args, *comm.ins)


def _grid1_call(body, comm, n, *, name, out_shape, in_specs, out_specs, scratch, args):
    if comm is None:
        return _pc(body, name=name, out_shape=out_shape, grid=(n,), in_specs=in_specs,
                   out_specs=out_specs, scratch=scratch, sem=("arbitrary",))(*args), ()
    res = _hosted_call(body, comm, lambda: pl.program_id(0) == 0, lambda: pl.program_id(0) == n - 1,
                       name=name, out_shape=out_shape, grid=(n,), in_specs=in_specs,
                       out_specs=out_specs, scratch=scratch, sem=("arbitrary",), args=args)
    return res[:len(out_shape)], res[len(out_shape):]


def _run_comm(name, comm):
    nci, nco = len(comm.ins), len(comm.out_shapes)

    def body(*refs):
        c_in, c_out, c_sem = refs[:nci], refs[nci:nci + nco], refs[nci + nco:]
        comm.start(c_in, c_out, c_sem)
        comm.finish(c_in, c_out, c_sem)

    return _pc(body, name=name, out_shape=tuple(comm.out_shapes), in_specs=[ANY] * nci,
               out_specs=tuple([ANY] * nco), scratch=comm.sems)(*comm.ins)


def _mm_nn_stacked(name, a, w_st, tm, comm=None):
    t, k = a.shape
    _, _, c = w_st.shape
    return _mm(name, a, w_st, NN, (t // tm, N_CHIP, 1),
               pl.BlockSpec((tm, k), lambda i, j, kk: (i, 0)),
               pl.BlockSpec((None, k, c), lambda i, j, kk: (j, 0, 0)),
               jax.ShapeDtypeStruct((t, N_CHIP * c), F32),
               pl.BlockSpec((tm, c), lambda i, j, kk: (i, j)), comm=comm)


def _mm_tn(name, a, b, tm, tn, stacked=False):
    t, m = a.shape
    _, n = b.shape
    if stacked:
        assert tm == m
        out_shape = jax.ShapeDtypeStruct((n // tn, m, tn), F32)
        o_spec = pl.BlockSpec((None, tm, tn), lambda i, j, kk: (j, 0, 0))
    else:
        out_shape = jax.ShapeDtypeStruct((m, n), F32)
        o_spec = pl.BlockSpec((tm, tn), lambda i, j, kk: (i, j))
    return _mm(name, a, b, TN, (m // tm, n // tn, 1),
               pl.BlockSpec((t, tm), lambda i, j, kk: (0, i)),
               pl.BlockSpec((t, tn), lambda i, j, kk: (0, j)),
               out_shape, o_spec)


def _in_proj_xgrad(dh_parts, win_st, dr1, tm, comm=None):
    t = dr1.shape[0]
    ni = t // tm

    def body(a0, a1, a2, a3, b_ref, add_ref, o_ref, acc):
        j = pl.program_id(1)
        for jj, a_ref in enumerate((a0, a1, a2, a3)):
            @pl.when(j == jj)
            def _(jj=jj, a_ref=a_ref):
                prod = _dot(a_ref[...], b_ref[...], NT)
                if jj == 0:
                    acc[...] = prod + ALPHA * add_ref[...]
                elif jj < N_CHIP - 1:
                    acc[...] += prod
                else:
                    o_ref[...] = acc[...] + prod

    a_spec = pl.BlockSpec((tm, 2 * D_MODEL), lambda i, j: (i, 0))
    tile = pl.BlockSpec((tm, D_MODEL), lambda i, j: (i, 0))
    kw = dict(name="in_proj_xgrad", out_shape=(jax.ShapeDtypeStruct((t, D_MODEL), F32),),
              grid=(ni, N_CHIP),
              in_specs=[a_spec] * 4 + [pl.BlockSpec((None, D_MODEL, 2 * D_MODEL), lambda i, j: (j, 0, 0)),
                                       tile],
              out_specs=(tile,), scratch=[pltpu.VMEM((tm, D_MODEL), F32)],
              sem=("arbitrary", "arbitrary"))
    args = list(dh_parts) + [win_st, dr1]
    if comm is None:
        return _pc(body, **kw)(*args)[0], ()
    res = _hosted_call(body, comm,
                       lambda: (pl.program_id(0) == 0) & (pl.program_id(1) == 0),
                       lambda: (pl.program_id(0) == ni - 1) & (pl.program_id(1) == N_CHIP - 1),
                       args=args, **kw)
    return res[0], res[1:]


def _sgu_mixed(v, wm_ref, bsb_ref, gv, bv):
    gl, dgl = _gelu_and_grad(v)
    vh, rstd = _ln_stats(gl)
    vn = vh * gv + bv
    mixed = []
    for g in range(N_GROUP):
        sl = slice(g * 128, (g + 1) * 128)
        mixed.append(_dot(wm_ref[g], vn[:, sl]) + bsb_ref[g])
    return dgl, vh, rstd, vn, mixed


def _sgu_fwd(h, wm, bsb, gv, bv):
    t = h.shape[0]

    def body(u_ref, v_ref, wm_ref, bsb_ref, gv_ref, bv_ref, ya_ref):
        u = u_ref[...]
        _, _, _, _, mixed = _sgu_mixed(v_ref[...], wm_ref, bsb_ref, gv_ref[...], bv_ref[...])
        gu = _gelu(u)
        for g in range(N_GROUP):
            sl = slice(g * 128, (g + 1) * 128)
            ya_ref[:, sl] = (gu[:, sl] * mixed[g]).astype(BF16)

    full3 = pl.BlockSpec((N_GROUP, 128, 128), lambda i: (0, 0, 0))
    vec = pl.BlockSpec((1, D_MODEL), lambda i: (0, 0))
    return _pc(body, name="sgu_fwd", out_shape=jax.ShapeDtypeStruct((t, D_MODEL), BF16),
               grid=(t // SGU_BLOCK,),
               in_specs=[pl.BlockSpec((SGU_BLOCK, D_MODEL), lambda i: (i, 0)),
                         pl.BlockSpec((SGU_BLOCK, D_MODEL), lambda i: (i, 1)),
                         full3, full3, vec, vec],
               out_specs=pl.BlockSpec((SGU_BLOCK, D_MODEL), lambda i: (i, 0)),
               sem=("parallel",))(h, h, wm, bsb, gv, bv)


def _sgu_bwd(h, dya, wm, wmt, bsb, gv, bv, maskf):
    t = h.shape[0]
    nb = t // SGU_BLOCK

    def body(u_ref, v_ref, dya_ref, wm_ref, wmt_ref, bsb_ref, gv_ref, bv_ref, mask_ref,
             dh_ref, dws_ref, dbs_ref, dgv_ref, dbv_ref, dmix_acc):
        i = pl.program_id(0)

        @pl.when(i == 0)
        def _():
            dws_ref[...] = jnp.zeros_like(dws_ref)
            dgv_ref[...] = jnp.zeros_like(dgv_ref)
            dbv_ref[...] = jnp.zeros_like(dbv_ref)
            dmix_acc[...] = jnp.zeros_like(dmix_acc)

        u = u_ref[...]
        gvv = gv_ref[...]
        dgl_v, vh, rstd, vn, mixed = _sgu_mixed(v_ref[...], wm_ref, bsb_ref, gvv, bv_ref[...])
        gu, dgl_u = _gelu_and_grad(u)
        dya_v = dya_ref[...]
        dvn_parts = []
        for g in range(N_GROUP):
            sl = slice(g * 128, (g + 1) * 128)
            d_y = dya_v[:, sl]
            dh_ref[:, sl] = (d_y * mixed[g] * dgl_u[:, sl]).astype(BF16)
            d_mixed = d_y * gu[:, sl]
            dmix_acc[g] += d_mixed
            dws_ref[g] += _dot(d_mixed, vn[:, sl], NT) * mask_ref[...]
            dvn_parts.append(_dot(wmt_ref[g], d_mixed))
        dvn = jnp.concatenate(dvn_parts, axis=1)
        dgv_ref[...] += _colsum8(dvn * vh)
        dbv_ref[...] += _colsum8(dvn)
        d_gl = _ln_bwd(dvn * gvv, vh, rstd)
        dh_ref[:, D_MODEL:] = (d_gl * dgl_v).astype(BF16)

        @pl.when(i == nb - 1)
        def _():
            rowid = lax.broadcasted_iota(jnp.int32, (8, 128), 0)
            ones = jnp.ones((8, 128), F32)
            acc = jnp.zeros((8, 128), F32)
            for g in range(N_GROUP):
                rs = _dot32(ones, dmix_acc[g], NT)
                acc = jnp.where(rowid == g, rs, acc)
            dbs_ref[...] = acc

    full3 = pl.BlockSpec((N_GROUP, 128, 128), lambda i: (0, 0, 0))
    vec = pl.BlockSpec((1, D_MODEL), lambda i: (0, 0))
    acc8 = pl.BlockSpec((8, D_MODEL), lambda i: (0, 0))
    return _pc(body, name="sgu_bwd",
               out_shape=(jax.ShapeDtypeStruct((t, 2 * D_MODEL), BF16),
                          jax.ShapeDtypeStruct((N_GROUP, 128, 128), F32),
                          jax.ShapeDtypeStruct((8, 128), F32),
                          jax.ShapeDtypeStruct((8, D_MODEL), F32),
                          jax.ShapeDtypeStruct((8, D_MODEL), F32)),
               grid=(nb,),
               in_specs=[pl.BlockSpec((SGU_BLOCK, D_MODEL), lambda i: (i, 0)),
                         pl.BlockSpec((SGU_BLOCK, D_MODEL), lambda i: (i, 1)),
                         pl.BlockSpec((SGU_BLOCK, D_MODEL), lambda i: (i, 0)),
                         full3, full3, full3, vec, vec,
                         pl.BlockSpec((128, 128), lambda i: (0, 0))],
               out_specs=(pl.BlockSpec((SGU_BLOCK, 2 * D_MODEL), lambda i: (i, 0)),
                          full3, pl.BlockSpec((8, 128), lambda i: (0, 0)), acc8, acc8),
               scratch=[pltpu.VMEM((N_GROUP, 128, 128), F32)],
               sem=("arbitrary",))(h, h, dya, wm, wmt, bsb, gv, bv, maskf)


def _tri_masks():
    row = lax.broadcasted_iota(jnp.int32, (CHUNK, CHUNK), 0)
    col = lax.broadcasted_iota(jnp.int32, (CHUNK, CHUNK), 1)
    return col <= row, col >= row


def _heads(v):
    return [v[:, hd * HEAD_DIM:(hd + 1) * HEAD_DIM] for hd in range(N_HEAD)]


def _tri_cumsum(tri_bf, v):
    hi = v.astype(BF16)
    r = v - hi.astype(F32)
    mid = r.astype(BF16)
    lo = (r - mid.astype(F32)).astype(BF16)
    return _dot(tri_bf, hi) + _dot(tri_bf, mid) + _dot(tri_bf, lo)


def _hgrn_chunk(q, fp, ii, lb, st_heads, causal):
    sg = _sig(fp)
    f = lb + (1.0 - lb) * sg
    k = 1.0 - f
    c = _tri_cumsum(causal.astype(BF16), jnp.log(f))
    ec = jnp.exp(c)
    en = jnp.exp(-c)
    sq = _sig(q)
    qt = q * sq * ec
    kt = k * en
    ecl = jnp.exp(c[CHUNK - 1:CHUNK, :])
    kk = kt * ecl
    qtb, ktb, iib, kkb = qt.astype(BF16), kt.astype(BF16), ii.astype(BF16), kk.astype(BF16)
    attn, o = [], []
    for hd, (qh, kh, ih) in enumerate(zip(_heads(qtb), _heads(ktb), _heads(iib))):
        a = jnp.where(causal, _dot(qh, kh, NT), 0.0).astype(BF16)
        attn.append(a)
        o.append(_dot(a, ih) + _dot(qh, st_heads[hd], NT))
    return dict(sg=sg, f=f, k=k, ec=ec, en=en, sq=sq, ecl=ecl, kk=kk, qtb=qtb, ktb=ktb, iib=iib,
                kkb=kkb, attn=attn, o=o)


def _rms_heads(o_heads):
    rinv = [lax.rsqrt(jnp.mean(o * o, axis=-1, keepdims=True) + RMS_EPS) for o in o_heads]
    return rinv, jnp.concatenate([o * r for o, r in zip(o_heads, rinv)], axis=1)


def _hgrn_fwd(h, logits, gn, comm=None):
    t = h.shape[0]
    nc = t // CHUNK

    def body(q_ref, f_ref, i_ref, og_ref, lg_ref, gn_ref, yb_ref, st_ref, state):
        ci = pl.program_id(0)

        @pl.when(ci == 0)
        def _():
            state[...] = jnp.zeros_like(state)

        causal, _ = _tri_masks()
        st = [state[hd] for hd in range(N_HEAD)]
        og = og_ref[...]
        lb = _sig(lg_ref[0:1, :] - lg_ref[1:2, :])
        r = _hgrn_chunk(q_ref[...], f_ref[...], i_ref[...], lb, [s.astype(BF16) for s in st], causal)
        _, on = _rms_heads(r["o"])
        ecl = _heads(r["ecl"])
        new = [s * e + _dot(ih, kh, TN)
               for s, e, ih, kh in zip(st, ecl, _heads(r["iib"]), _heads(r["kkb"]))]
        yb_ref[...] = (on * gn_ref[...] * (og * _sig(og))).astype(BF16)
        for hd in range(N_HEAD):
            st_ref[0, hd] = st[hd]
            state[hd] = new[hd]

    def col(k):
        return pl.BlockSpec((CHUNK, D_MODEL), lambda ci: (ci, k))

    return _grid1_call(body, comm, nc, name="hgrn_fwd",
                       out_shape=(jax.ShapeDtypeStruct((t, D_MODEL), BF16),
                                  jax.ShapeDtypeStruct((nc, N_HEAD, HEAD_DIM, HEAD_DIM), F32)),
                       in_specs=[col(2), col(3), col(4), col(5),
                                 pl.BlockSpec((2, D_MODEL), lambda ci: (0, 0)),
                                 pl.BlockSpec((1, D_MODEL), lambda ci: (0, 0))],
                       out_specs=(pl.BlockSpec((CHUNK, D_MODEL), lambda ci: (ci, 0)),
                                  pl.BlockSpec((1, N_HEAD, HEAD_DIM, HEAD_DIM), lambda ci: (ci, 0, 0, 0))),
                       scratch=[pltpu.VMEM((N_HEAD, HEAD_DIM, HEAD_DIM), F32)],
                       args=(h, h, h, h, logits, gn))


def _hgrn_bwd(h, dyb, st_all, logits, gn, comm=None):
    t = h.shape[0]
    nc = t // CHUNK

    def body(q_ref, f_ref, i_ref, og_ref, dyb_ref, st_ref, lg_ref, gn_ref,
             dh1_ref, dh2_ref, dlb_ref, dgn_ref, dstate):
        ci = pl.program_id(0)

        @pl.when(ci == 0)
        def _():
            dstate[...] = jnp.zeros_like(dstate)
            dlb_ref[...] = jnp.zeros_like(dlb_ref)
            dgn_ref[...] = jnp.zeros_like(dgn_ref)

        causal, anti = _tri_masks()
        q, og, dy, gnv = q_ref[...], og_ref[...], dyb_ref[...], gn_ref[...]
        lb = _sig(lg_ref[0:1, :] - lg_ref[1:2, :])
        st = [st_ref[0, hd] for hd in range(N_HEAD)]
        dsn = [dstate[hd] for hd in range(N_HEAD)]
        stb = [s.astype(BF16) for s in st]
        dsnb = [s.astype(BF16) for s in dsn]
        r = _hgrn_chunk(q, f_ref[...], i_ref[...], lb, stb, causal)
        rinv, on = _rms_heads(r["o"])
        so = _sig(og)
        sil = og * so
        d_og = dy * on * gnv * (so * (1.0 + og * (1.0 - so)))
        d_on = dy * gnv * sil
        d_ob = jnp.concatenate(
            [ri * (dn - oh * jnp.mean(dn * oh, axis=-1, keepdims=True))
             for ri, dn, oh in zip(rinv, _heads(d_on), _heads(on))], axis=1).astype(BF16)
        d_i, d_qt, d_kt, d_kk, d_st, st_dsn = [], [], [], [], [], []
        ecl = _heads(r["ecl"])
        for hd, (dh, qh, kh, ih, kkh) in enumerate(zip(_heads(d_ob), _heads(r["qtb"]), _heads(r["ktb"]),
                                                       _heads(r["iib"]), _heads(r["kkb"]))):
            d_attn = jnp.where(causal, _dot(dh, ih, NT), 0.0).astype(BF16)
            d_i.append(_dot(r["attn"][hd], dh, TN) + _dot(kkh, dsnb[hd], NT))
            d_qt.append(_dot(d_attn, kh) + _dot(dh, stb[hd]))
            d_kt.append(_dot(d_attn, qh, TN))
            d_kk.append(_dot(ih, dsnb[hd]))
            d_st.append(_dot(dh, qh, TN) + dsn[hd] * ecl[hd])
            st_dsn.append(jnp.sum(st[hd] * dsn[hd], axis=0, keepdims=True))
        d_qt = jnp.concatenate(d_qt, axis=1)
        d_kt = jnp.concatenate(d_kt, axis=1)
        d_kk = jnp.concatenate(d_kk, axis=1)
        kk = r["kk"]
        d_cl = (r["ecl"] * jnp.concatenate(st_dsn, axis=1)
                + jnp.sum(kk * d_kk, axis=0, keepdims=True))
        d_k = (d_kk * r["ecl"] + d_kt) * r["en"]
        d_c = d_qt * r["qtb"].astype(F32) - d_kt * r["ktb"].astype(F32) - d_kk * kk
        rowid = lax.broadcasted_iota(jnp.int32, (CHUNK, D_MODEL), 0)
        d_c = d_c + jnp.where(rowid == CHUNK - 1, d_cl, 0.0)
        d_lf = _tri_cumsum(anti.astype(BF16), d_c)
        d_f = d_lf / r["f"] - d_k
        sg = r["sg"]
        sq = r["sq"]
        dgn_ref[...] += _colsum8(dy * on * sil)
        dlb_ref[...] += _colsum8(d_f * (1.0 - sg))
        dh1_ref[:, :D_MODEL] = (d_qt * r["ec"] * (sq * (1.0 + q * (1.0 - sq)))).astype(BF16)
        dh1_ref[:, D_MODEL:] = (d_f * (1.0 - lb) * sg * (1.0 - sg)).astype(BF16)
        dh2_ref[:, :D_MODEL] = jnp.concatenate(d_i, axis=1).astype(BF16)
        dh2_ref[:, D_MODEL:] = d_og.astype(BF16)
        for hd in range(N_HEAD):
            dstate[hd] = d_st[hd]

    def col(k):
        return pl.BlockSpec((CHUNK, D_MODEL), lambda ci: (nc - 1 - ci, k))

    acc8 = pl.BlockSpec((8, D_MODEL), lambda ci: (0, 0))
    pair = pl.BlockSpec((CHUNK, 2 * D_MODEL), lambda ci: (nc - 1 - ci, 0))
    return _grid1_call(body, comm, nc, name="hgrn_bwd",
                       out_shape=(jax.ShapeDtypeStruct((t, 2 * D_MODEL), BF16),
                                  jax.ShapeDtypeStruct((t, 2 * D_MODEL), BF16),
                                  jax.ShapeDtypeStruct((8, D_MODEL), F32),
                                  jax.ShapeDtypeStruct((8, D_MODEL), F32)),
                       in_specs=[col(2), col(3), col(4), col(5),
                                 pl.BlockSpec((CHUNK, D_MODEL), lambda ci: (nc - 1 - ci, 0)),
                                 pl.BlockSpec((1, N_HEAD, HEAD_DIM, HEAD_DIM),
                                              lambda ci: (nc - 1 - ci, 0, 0, 0)),
                                 pl.BlockSpec((2, D_MODEL), lambda ci: (0, 0)),
                                 pl.BlockSpec((1, D_MODEL), lambda ci: (0, 0))],
                       out_specs=(pair, pair, acc8, acc8),
                       scratch=[pltpu.VMEM((N_HEAD, HEAD_DIM, HEAD_DIM), F32)],
                       args=(h, h, h, h, dyb, st_all, logits, gn))


def _mix_fwd(ya, yb, h, x, wb0, wb1, wo, g1, b1, tm):
    t = x.shape[0]

    def body(ya_ref, yb_ref, ga_ref, gb_ref, x_ref, wb0_ref, wb1_ref, wo_ref, g1_ref, b1_ref,
             r1_ref, a_ref, b_ref, m_ref, x1_ref):
        a = _dot(ya_ref[...], wb0_ref[...])
        b = _dot(yb_ref[...], wb1_ref[...])
        m = _sig(ga_ref[...]) * a + _sig(gb_ref[...]) * b
        r1 = ALPHA * x_ref[...] + _dot(m, wo_ref[...])
        xh, _ = _ln_stats(r1)
        r1_ref[...] = r1
        a_ref[...] = a
        b_ref[...] = b
        m_ref[...] = m.astype(BF16)
        x1_ref[...] = (xh * g1_ref[...] + b1_ref[...]).astype(BF16)

    tile = pl.BlockSpec((tm, D_MODEL), lambda i: (i, 0))
    wsp = pl.BlockSpec((D_MODEL, D_MODEL), lambda i: (0, 0))
    vec = pl.BlockSpec((1, D_MODEL), lambda i: (0, 0))
    f32o = jax.ShapeDtypeStruct((t, D_MODEL), F32)
    bfo = jax.ShapeDtypeStruct((t, D_MODEL), BF16)
    return _pc(body, name="mix_fwd", out_shape=(f32o, f32o, f32o, bfo, bfo), grid=(t // tm,),
               in_specs=[tile, tile,
                         pl.BlockSpec((tm, D_MODEL), lambda i: (i, 6)),
                         pl.BlockSpec((tm, D_MODEL), lambda i: (i, 7)),
                         tile, wsp, wsp, wsp, vec, vec],
               out_specs=(tile, tile, tile, tile, tile),
               sem=("parallel",))(ya, yb, h, h, x, wb0, wb1, wo, g1, b1)


def _mix_bwd(dr1, h, a, b, wo, wb0, wb1, tm):
    t = dr1.shape[0]

    def body(dr1_ref, ga_ref, gb_ref, a_ref, b_ref, wo_ref, wb0_ref, wb1_ref,
             da_ref, db_ref, dh3_ref, dya_ref, dyb_ref):
        d_m = _dot(dr1_ref[...], wo_ref[...], NT)
        sa = _sig(ga_ref[...])
        sb = _sig(gb_ref[...])
        d_a = (d_m * sa).astype(BF16)
        d_b = (d_m * sb).astype(BF16)
        da_ref[...] = d_a
        db_ref[...] = d_b
        dh3_ref[:, :D_MODEL] = (d_m * a_ref[...] * sa * (1.0 - sa)).astype(BF16)
        dh3_ref[:, D_MODEL:] = (d_m * b_ref[...] * sb * (1.0 - sb)).astype(BF16)
        dya_ref[...] = _dot(d_a, wb0_ref[...], NT)
        dyb_ref[...] = _dot(d_b, wb1_ref[...], NT)

    tile = pl.BlockSpec((tm, D_MODEL), lambda i: (i, 0))
    wsp = pl.BlockSpec((D_MODEL, D_MODEL), lambda i: (0, 0))
    f32o = jax.ShapeDtypeStruct((t, D_MODEL), F32)
    bfo = jax.ShapeDtypeStruct((t, D_MODEL), BF16)
    return _pc(body, name="mix_bwd",
               out_shape=(bfo, bfo, jax.ShapeDtypeStruct((t, 2 * D_MODEL), BF16), f32o, f32o),
               grid=(t // tm,),
               in_specs=[tile,
                         pl.BlockSpec((tm, D_MODEL), lambda i: (i, 6)),
                         pl.BlockSpec((tm, D_MODEL), lambda i: (i, 7)),
                         tile, tile, wsp, wsp, wsp],
               out_specs=(tile, tile, pl.BlockSpec((tm, 2 * D_MODEL), lambda i: (i, 0)),
                          tile, tile),
               sem=("parallel",))(dr1, h, h, a, b, wo, wb0, wb1)


FF_TILE = 1408
FF_NJ = D_FF // FF_TILE


def _shift_down(v, k):
    return pltpu.roll(v, k, 0)


def _shift_up(v, k):
    return pltpu.roll(v, v.shape[0] - k, 0)


def _conv_gate(ext, cw_ref, cb_ref):
    return (cw_ref[0:1, :] * _shift_down(ext, 2) + cw_ref[1:2, :] * _shift_down(ext, 1)
            + cw_ref[2:3, :] * ext + cb_ref[...])


def _ffn_act_fwd(h2, convw, convb, tm):
    t = h2.shape[0]
    nt8 = tm // 8

    def body(g_ref, gp_ref, v_ref, cw_ref, cb_ref, act_ref):
        i = pl.program_id(1)
        prev = gp_ref[...] * (i > 0).astype(F32)
        ext = jnp.concatenate([prev, g_ref[...]], axis=0)
        gc = _conv_gate(ext, cw_ref, cb_ref)[8:, :]
        act_ref[...] = (_gelu(gc) * v_ref[...]).astype(BF16)

    return _pc(body, name="ffn_act_fwd", out_shape=jax.ShapeDtypeStruct((t, D_FF), BF16),
               grid=(FF_NJ, t // tm),
               in_specs=[pl.BlockSpec((tm, FF_TILE), lambda j, i: (i, j)),
                         pl.BlockSpec((8, FF_TILE), lambda j, i: (jnp.maximum(i * nt8 - 1, 0), j)),
                         pl.BlockSpec((tm, FF_TILE), lambda j, i: (i, j + FF_NJ)),
                         pl.BlockSpec((3, FF_TILE), lambda j, i: (0, j)),
                         pl.BlockSpec((1, FF_TILE), lambda j, i: (0, j))],
               out_specs=pl.BlockSpec((tm, FF_TILE), lambda j, i: (i, j)),
               sem=("parallel", "parallel"))(h2, h2, h2, convw, convb)


def _ffn_act_bwd(h2, dact, convw, convb, tm):
    t = h2.shape[0]
    nt8 = tm // 8
    ni = t // tm
    last8 = t // 8 - 1

    def body(g_ref, gp_ref, gn_ref, v_ref, vn_ref, da_ref, dan_ref, cw_ref, cb_ref,
             dh2_ref, dcw_ref, dcb_ref):
        i = pl.program_id(1)

        @pl.when(i == 0)
        def _():
            dcw_ref[...] = jnp.zeros_like(dcw_ref)
            dcb_ref[...] = jnp.zeros_like(dcb_ref)

        prev = gp_ref[...] * (i > 0).astype(F32)
        ext = jnp.concatenate([prev, g_ref[...], gn_ref[...]], axis=0)
        vext = jnp.concatenate([jnp.zeros((8, FF_TILE), F32), v_ref[...], vn_ref[...]], axis=0)
        dnext = dan_ref[...] * (i < ni - 1).astype(F32)
        dext = jnp.concatenate([jnp.zeros((8, FF_TILE), F32), da_ref[...], dnext], axis=0)
        g2 = _shift_down(ext, 2)
        g1 = _shift_down(ext, 1)
        gc = cw_ref[0:1, :] * g2 + cw_ref[1:2, :] * g1 + cw_ref[2:3, :] * ext + cb_ref[...]
        gl, dgl = _gelu_and_grad(gc)
        d_gc = dext * vext * dgl
        d_gate = (cw_ref[2:3, :] * d_gc + cw_ref[1:2, :] * _shift_up(d_gc, 1)
                  + cw_ref[0:1, :] * _shift_up(d_gc, 2))
        dh2_ref[0] = d_gate[8:8 + tm, :].astype(BF16)
        dh2_ref[1] = (da_ref[...] * gl[8:8 + tm, :]).astype(BF16)
        dm = d_gc[8:8 + tm, :]
        s0 = jnp.sum(dm * g2[8:8 + tm, :], axis=0, keepdims=True)
        s1 = jnp.sum(dm * g1[8:8 + tm, :], axis=0, keepdims=True)
        s2 = jnp.sum(dm * ext[8:8 + tm, :], axis=0, keepdims=True)
        rowid = lax.broadcasted_iota(jnp.int32, (8, FF_TILE), 0)
        dcw_ref[...] += jnp.where(rowid == 0, s0, jnp.where(rowid == 1, s1,
                                                            jnp.where(rowid == 2, s2, 0.0)))
        dcb_ref[...] += _colsum8(dm)

    def prev8(off):
        return pl.BlockSpec((8, FF_TILE), lambda j, i: (jnp.maximum(i * nt8 - 1, 0), j + off))

    def next8(off):
        return pl.BlockSpec((8, FF_TILE), lambda j, i: (jnp.minimum((i + 1) * nt8, last8), j + off))

    def main(off):
        return pl.BlockSpec((tm, FF_TILE), lambda j, i: (i, j + off))

    acc = pl.BlockSpec((8, FF_TILE), lambda j, i: (0, j))
    return _pc(body, name="ffn_act_bwd",
               out_shape=(jax.ShapeDtypeStruct((2, t, D_FF), BF16),
                          jax.ShapeDtypeStruct((8, D_FF), F32),
                          jax.ShapeDtypeStruct((8, D_FF), F32)),
               grid=(FF_NJ, ni),
               in_specs=[main(0), prev8(0), next8(0), main(FF_NJ), next8(FF_NJ),
                         pl.BlockSpec((tm, FF_TILE), lambda j, i: (i, j)),
                         pl.BlockSpec((8, FF_TILE), lambda j, i: (jnp.minimum((i + 1) * nt8, last8), j)),
                         pl.BlockSpec((3, FF_TILE), lambda j, i: (0, j)),
                         pl.BlockSpec((1, FF_TILE), lambda j, i: (0, j))],
               out_specs=(pl.BlockSpec((2, tm, FF_TILE), lambda j, i: (0, i, j)), acc, acc),
               sem=("parallel", "arbitrary"))(h2, h2, h2, h2, h2, dact, dact, convw, convb)


def _out_fwd_bwd(act, x1b, r1, p2, tgt, wd, wpg, wpp, g1, b1, g2, b2, tm):
    t = r1.shape[0]

    def body(act_ref, x1b_ref, r1_ref, p_ref, tgt_ref, wd_ref, wpg_ref, wpp_ref,
             g1_ref, b1_ref, g2_ref, b2_ref,
             dr2_ref, dpg_ref, dpp_ref, loss_ref, dg2_ref, db2_ref):
        i = pl.program_id(0)

        @pl.when(i == 0)
        def _():
            loss_ref[...] = jnp.zeros_like(loss_ref)
            dg2_ref[...] = jnp.zeros_like(dg2_ref)
            db2_ref[...] = jnp.zeros_like(db2_ref)

        ffn = _dot(act_ref[...], wd_ref[...])
        pg = _dot(x1b_ref[...], wpg_ref[...])
        pp = _dot(p_ref[...], wpp_ref[...])
        s = _sig(pg)
        xh1, _ = _ln_stats(r1_ref[...])
        x1 = xh1 * g1_ref[...] + b1_ref[...]
        r2 = ALPHA * x1 + ffn + s * pp
        xh2, rstd2 = _ln_stats(r2)
        g2v = g2_ref[...]
        diff = xh2 * g2v + b2_ref[...] - tgt_ref[...]
        part = jnp.sum(jnp.sum(diff * diff, axis=1, keepdims=True), axis=0, keepdims=True)
        loss_ref[...] += jnp.broadcast_to(part * (0.5 / D_MODEL), loss_ref.shape)
        dy = diff * (1.0 / D_MODEL)
        dg2_ref[...] += _colsum8(dy * xh2)
        db2_ref[...] += _colsum8(dy)
        dr2 = _ln_bwd(dy * g2v, xh2, rstd2)
        dr2_ref[...] = dr2
        dpg_ref[...] = (dr2 * pp * s * (1.0 - s)).astype(BF16)
        dpp_ref[...] = (dr2 * s).astype(BF16)

    tile = pl.BlockSpec((tm, D_MODEL), lambda i: (i, 0))
    vec = pl.BlockSpec((1, D_MODEL), lambda i: (0, 0))
    acc8 = pl.BlockSpec((8, D_MODEL), lambda i: (0, 0))
    acc_shape = jax.ShapeDtypeStruct((8, D_MODEL), F32)
    return _pc(body, name="out_fwd_bwd",
               out_shape=(jax.ShapeDtypeStruct((t, D_MODEL), F32),
                          jax.ShapeDtypeStruct((t, D_MODEL), BF16),
                          jax.ShapeDtypeStruct((t, D_MODEL), BF16),
                          acc_shape, acc_shape, acc_shape),
               grid=(t // tm,),
               in_specs=[pl.BlockSpec((tm, D_FF), lambda i: (i, 0)), tile, tile,
                         pl.BlockSpec((tm, PLE_DIM), lambda i: (i, 0)), tile,
                         pl.BlockSpec((D_FF, D_MODEL), lambda i: (0, 0)),
                         pl.BlockSpec((D_MODEL, D_MODEL), lambda i: (0, 0)),
                         pl.BlockSpec((PLE_DIM, D_MODEL), lambda i: (0, 0)),
                         vec, vec, vec, vec],
               out_specs=(tile, tile, tile, acc8, acc8, acc8),
               sem=("arbitrary",))(act, x1b, r1, p2, tgt, wd, wpg, wpp, g1, b1, g2, b2)


def _ffn_in_bwd(dh2, wup_st, dpg, wpg, dr2, r1, g1, tm):
    t = r1.shape[0]
    ni = t // tm

    def body(dh2_ref, wup_ref, dpg_ref, wpg_ref, dr2_ref, r1_ref, g1_ref,
             dr1_ref, dg1_ref, db1_ref, acc):
        i = pl.program_id(0)
        j = pl.program_id(1)

        @pl.when((i == 0) & (j == 0))
        def _():
            dg1_ref[...] = jnp.zeros_like(dg1_ref)
            db1_ref[...] = jnp.zeros_like(db1_ref)

        @pl.when(j == 0)
        def _():
            acc[...] = _dot(dh2_ref[...], wup_ref[...], NT)

        @pl.when(j > 0)
        def _():
            acc[...] += _dot(dh2_ref[...], wup_ref[...], NT)

        @pl.when(j == N_CHIP - 1)
        def _():
            d_x1 = acc[...] + _dot(dpg_ref[...], wpg_ref[...], NT) + ALPHA * dr2_ref[...]
            xh, rstd = _ln_stats(r1_ref[...])
            dg1_ref[...] += _colsum8(d_x1 * xh)
            db1_ref[...] += _colsum8(d_x1)
            dr1_ref[...] = _ln_bwd(d_x1 * g1_ref[...], xh, rstd)

    tile = pl.BlockSpec((tm, D_MODEL), lambda i, j: (i, 0))
    acc8 = pl.BlockSpec((8, D_MODEL), lambda i, j: (0, 0))
    acc_shape = jax.ShapeDtypeStruct((8, D_MODEL), F32)
    return _pc(body, name="ffn_in_bwd",
               out_shape=(jax.ShapeDtypeStruct((t, D_MODEL), F32), acc_shape, acc_shape),
               grid=(ni, N_CHIP),
               in_specs=[pl.BlockSpec((None, tm, FF_TILE), lambda i, j: (j // FF_NJ, i, j % FF_NJ)),
                         pl.BlockSpec((None, D_MODEL, FF_TILE), lambda i, j: (j, 0, 0)),
                         tile, pl.BlockSpec((D_MODEL, D_MODEL), lambda i, j: (0, 0)),
                         tile, tile, pl.BlockSpec((1, D_MODEL), lambda i, j: (0, 0))],
               out_specs=(tile, acc8, acc8),
               scratch=[pltpu.VMEM((tm, D_MODEL), F32)],
               sem=("arbitrary", "arbitrary"))(dh2, wup_st, dpg, wpg, dr2, r1, g1)


ANY = pl.BlockSpec(memory_space=pl.ANY)


def _chip_peers():
    x, y, c = lax.axis_index("x"), lax.axis_index("y"), lax.axis_index("c")
    return x, y, c, [(1 - x, y), (x, 1 - y), (1 - x, 1 - y)]


def _gather_comm(halved, whole=()):
    n, nw = len(halved), len(whole)

    def copies(ins, outs, sems):
        ici_send, ici_recv, d2d_send, d2d_recv, own_send, own_recv = sems
        x, y, c, peers = _chip_peers()
        me = 2 * x + y
        sibling = (x, y, 1 - c)
        own, ici, ici_wait, fwd, fwd_wait = [], [], [], [], []
        for ti in range(n + nw):
            src, dst = ins[ti], outs[ti]
            own.append(pltpu.make_async_remote_copy(
                src_ref=src, dst_ref=dst.at[me], send_sem=own_send.at[ti], recv_sem=own_recv.at[ti],
                device_id=sibling, device_id_type=MESH))
            for k, (px, py) in enumerate(peers):
                pk = 2 * px + py
                sem = dict(send_sem=ici_send.at[ti * 3 + k], recv_sem=ici_recv.at[ti * 3 + k],
                           device_id=(px, py, c), device_id_type=MESH)
                if ti < n:
                    ici.append(pltpu.make_async_remote_copy(src_ref=src.at[c], dst_ref=dst.at[me, c], **sem))
                    ici_wait.append(pltpu.make_async_remote_copy(src_ref=src.at[c], dst_ref=dst.at[pk, c], **sem))
                    dsem = dict(send_sem=d2d_send.at[ti * 3 + k], recv_sem=d2d_recv.at[ti * 3 + k],
                                device_id=sibling, device_id_type=MESH)
                    fwd.append(pltpu.make_async_remote_copy(src_ref=dst.at[pk, c], dst_ref=dst.at[pk, c], **dsem))
                    fwd_wait.append(pltpu.make_async_remote_copy(
                        src_ref=dst.at[pk, 1 - c], dst_ref=dst.at[pk, 1 - c], **dsem))
                else:
                    ici.append(pltpu.make_async_remote_copy(src_ref=src, dst_ref=dst.at[me], **sem))
                    ici_wait.append(pltpu.make_async_remote_copy(src_ref=src, dst_ref=dst.at[pk], **sem))
        return own, ici, ici_wait, fwd, fwd_wait

    def start(ins, outs, sems):
        own, ici, _, _, _ = copies(ins, outs, sems)
        for cp in own + ici:
            cp.start()

    def finish(ins, outs, sems):
        own, ici, ici_wait, fwd, fwd_wait = copies(ins, outs, sems)
        for i, cp in enumerate(ici_wait):
            cp.wait_recv()
            if i < len(fwd):
                fwd[i].start()
        for cp in fwd_wait + own:
            cp.wait_recv()
        for cp in own + ici + fwd:
            cp.wait_send()

    srcs = list(halved) + list(whole)
    return _Comm(srcs, [jax.ShapeDtypeStruct((N_CHIP,) + s.shape, s.dtype) for s in srcs],
                 [pltpu.SemaphoreType.DMA((3 * (n + nw),)), pltpu.SemaphoreType.DMA((3 * (n + nw),)),
                  pltpu.SemaphoreType.DMA((max(3 * n, 1),)), pltpu.SemaphoreType.DMA((max(3 * n, 1),)),
                  pltpu.SemaphoreType.DMA((n + nw,)), pltpu.SemaphoreType.DMA((n + nw,))],
                 start, finish)


def _rs_sibling_exchange(name, grads):
    n = len(grads)

    def body(*refs):
        ins, outs = refs[:n], refs[n:2 * n]
        send_sems, recv_sems = refs[2 * n:]
        x, y, c = lax.axis_index("x"), lax.axis_index("y"), lax.axis_index("c")
        sends = []
        for ti in range(n):
            half = ins[ti].shape[1] // 2
            cp = pltpu.make_async_remote_copy(
                src_ref=ins[ti].at[:, pl.ds(pl.multiple_of((1 - c) * half, 8), half), :],
                dst_ref=outs[ti],
                send_sem=send_sems.at[ti], recv_sem=recv_sems.at[ti],
                device_id=(x, y, 1 - c), device_id_type=MESH)
            cp.start()
            sends.append(cp)
        for cp in sends:
            cp.wait()

    return _pc(body, name=name,
               out_shape=tuple(jax.ShapeDtypeStruct((N_CHIP, g.shape[1] // 2, g.shape[2]), g.dtype)
                               for g in grads),
               in_specs=[ANY] * n, out_specs=tuple([ANY] * n),
               scratch=[pltpu.SemaphoreType.DMA((n,)), pltpu.SemaphoreType.DMA((n,))])(*grads)


def _rs_add_halves(name, grad, recv, core):
    _, r, cdim = grad.shape
    half = r // 2
    tr = _row_tile(half, cdim, mult=16)
    nr = half // tr

    def body(c_ref, g_ref, r_ref, o_ref):
        o_ref[...] = (g_ref[...] + r_ref[...]).astype(BF16)

    return _pc(body, name=name, out_shape=jax.ShapeDtypeStruct((N_CHIP, half, cdim), BF16),
               grid=(N_CHIP, nr), nsp=1,
               in_specs=[pl.BlockSpec((None, tr, cdim), lambda j, i, c_ref: (j, c_ref[0] * nr + i, 0)),
                         pl.BlockSpec((None, tr, cdim), lambda j, i, c_ref: (j, i, 0))],
               out_specs=pl.BlockSpec((None, tr, cdim), lambda j, i, c_ref: (j, i, 0)),
               sem=("parallel", "parallel"))(core, grad, recv)


def _chip_exchange_comm(parts):
    n = len(parts)

    def copies(ins, outs, sems):
        send_sems, recv_sems = sems
        x, y, c, peers = _chip_peers()
        return [pltpu.make_async_remote_copy(
            src_ref=ins[ti].at[2 * px + py], dst_ref=outs[ti].at[k],
            send_sem=send_sems.at[ti * 3 + k], recv_sem=recv_sems.at[ti * 3 + k],
            device_id=(px, py, c), device_id_type=MESH)
            for ti in range(n) for k, (px, py) in enumerate(peers)]

    def start(ins, outs, sems):
        for cp in copies(ins, outs, sems):
            cp.start()

    def finish(ins, outs, sems):
        for cp in copies(ins, outs, sems):
            cp.wait()

    return _Comm(parts, [jax.ShapeDtypeStruct((3,) + p.shape[1:], p.dtype) for p in parts],
                 [pltpu.SemaphoreType.DMA((3 * n,)), pltpu.SemaphoreType.DMA((3 * n,))], start, finish)


def _rs_sum_chips(name, part, recv, chip):
    _, half, cdim = recv.shape
    tr = _row_tile(half, cdim, mult=16)

    def body(chip_ref, p_ref, r_ref, o_ref):
        o_ref[...] = ((p_ref[...].astype(F32) + r_ref[0].astype(F32)) + r_ref[1].astype(F32)
                      ) + r_ref[2].astype(F32)

    return _pc(body, name=name, out_shape=jax.ShapeDtypeStruct((half, cdim), F32),
               grid=(half // tr,), nsp=1,
               in_specs=[pl.BlockSpec((None, tr, cdim), lambda i, chip_ref: (chip_ref[0], i, 0)),
                         pl.BlockSpec((3, tr, cdim), lambda i, chip_ref: (0, i, 0))],
               out_specs=pl.BlockSpec((tr, cdim), lambda i, chip_ref: (i, 0)),
               sem=("parallel",))(chip, part, recv)


def _rs_send_halves(halves):
    n = len(halves)

    def body(*refs):
        ins, outs = refs[:n], refs[n:2 * n]
        send_sems, recv_sems = refs[2 * n:]
        x, y, c = lax.axis_index("x"), lax.axis_index("y"), lax.axis_index("c")
        sends = []
        for ti in range(n):
            cp = pltpu.make_async_remote_copy(
                src_ref=ins[ti], dst_ref=outs[ti],
                send_sem=send_sems.at[ti], recv_sem=recv_sems.at[ti],
                device_id=(x, y, 1 - c), device_id_type=MESH)
            cp.start()
            sends.append(cp)
        for cp in sends:
            cp.wait()

    return _pc(body, name="rs_send_halves",
               out_shape=tuple(jax.ShapeDtypeStruct(hv.shape, hv.dtype) for hv in halves),
               in_specs=[ANY] * n, out_specs=tuple([ANY] * n),
               scratch=[pltpu.SemaphoreType.DMA((n,)), pltpu.SemaphoreType.DMA((n,))])(*halves)


def _adamw_rows(name, mine, theirs, w, m, v, core):
    half, cdim = mine.shape
    tr = _row_tile(half, cdim, budget=1 << 19)
    nrh = half // tr

    def body(c_ref, mine_ref, theirs_ref, w_ref, m_ref, v_ref, g_ref, d_ref, m2_ref, v2_ref):
        is_mine = (pl.program_id(0) // nrh) == c_ref[0]
        g = jnp.where(is_mine, mine_ref[...], theirs_ref[...])
        d, m2, v2 = _adamw(w_ref[...], g, m_ref[...], v_ref[...])
        g_ref[...] = g
        d_ref[...] = d
        m2_ref[...] = m2
        v2_ref[...] = v2

    htile = pl.BlockSpec((tr, cdim), lambda i, c_ref: (i % nrh, 0))
    tile = pl.BlockSpec((tr, cdim), lambda i, c_ref: (i, 0))
    shp = jax.ShapeDtypeStruct((2 * half, cdim), F32)
    return _pc(body, name=name, out_shape=(shp, shp, shp, shp), grid=(2 * nrh,), nsp=1,
               in_specs=[htile, htile, tile, tile, tile], out_specs=(tile, tile, tile, tile),
               sem=("parallel",))(core, mine, theirs, w, m, v)


def _adamw_whole(name, g, w, m, v):
    def body(g_ref, w_ref, m_ref, v_ref, d_ref, m2_ref, v2_ref):
        d, m2, v2 = _adamw(w_ref[...], g_ref[...], m_ref[...], v_ref[...])
        d_ref[...] = d
        m2_ref[...] = m2
        v2_ref[...] = v2

    shp = jax.ShapeDtypeStruct(g.shape, F32)
    return _pc(body, name=name, out_shape=(shp, shp, shp))(g, w, m, v)


SMALL_LAYOUT = (
    ("sgu_w_s", 1024, 1, 0),
    ("sgu_b_s", 8, 1, 1024),
    ("sgu_norm_g", 1, 0, 0),
    ("sgu_norm_b", 1, 0, 1),
    ("hgrn_norm_g", 1, 0, 3),
    ("ln1_g", 1, 0, 4),
    ("ln1_b", 1, 0, 5),
    ("ffn_conv_b", 1, 2, 3),
    ("ln2_g", 1, 0, 6),
    ("ln2_b", 1, 0, 7),
)
LB_ROW = 2
LOSS_ROW = 8
PACK_SHAPES = ((16, D_MODEL), (N_GROUP * 128 + 8, 128), (8, D_FF))


def _small_allreduce_adamw(rows1024, dws, dbs, dcw, dcb, logits, m_logits, v_logits,
                           small_w, small_m, small_v):
    ns = len(SMALL_LAYOUT)
    nr = len(rows1024)
    nb = len(PACK_SHAPES)

    def body(*refs):
        row_refs = refs[:nr]
        dws_ref, dbs_ref, dcw_ref, dcb_ref, lg_ref, mlg_ref, vlg_ref = refs[nr:nr + 7]
        pos = nr + 7
        w_refs = refs[pos:pos + ns]
        m_refs = refs[pos + ns:pos + 2 * ns]
        v_refs = refs[pos + 2 * ns:pos + 3 * ns]
        pos += 3 * ns
        loss_ref, dcw_out = refs[pos:pos + 2]
        lg_outs = refs[pos + 2:pos + 6]
        pos += 6
        outs = refs[pos:pos + 4 * ns]
        pos += 4 * ns
        pack = refs[pos:pos + nb]
        sib = refs[pos + nb:pos + 2 * nb]
        gath = refs[pos + 2 * nb:pos + 3 * nb]
        d2d_send, d2d_recv, ici_send, ici_recv = refs[pos + 3 * nb:]

        x, y, c, peers = _chip_peers()
        me = 2 * x + y
        sibling = (x, y, 1 - c)

        pack[0][...] = jnp.zeros(PACK_SHAPES[0], F32)
        for k in range(nr):
            pack[0][k:k + 1, :] = row_refs[k][0:1, :]
        pack[1][0:N_GROUP * 128, :] = dws_ref[...]
        pack[1][N_GROUP * 128:, :] = dbs_ref[...]
        pack[2][...] = jnp.zeros(PACK_SHAPES[2], F32)
        pack[2][0:3, :] = dcw_ref[0:3, :]
        pack[2][3:4, :] = dcb_ref[0:1, :]

        d2d = [pltpu.make_async_remote_copy(
            src_ref=pack[b], dst_ref=sib[b], send_sem=d2d_send.at[b], recv_sem=d2d_recv.at[b],
            device_id=sibling, device_id_type=MESH) for b in range(nb)]
        for cp in d2d:
            cp.start()
        for cp in d2d:
            cp.wait()
        for b in range(nb):
            gath[b][me] = pack[b][...] + sib[b][...]

        ici, ici_wait = [], []
        for b in range(nb):
            for k, (px, py) in enumerate(peers):
                sem = dict(send_sem=ici_send.at[b * 3 + k], recv_sem=ici_recv.at[b * 3 + k],
                           device_id=(px, py, c), device_id_type=MESH)
                ici.append(pltpu.make_async_remote_copy(src_ref=gath[b].at[me], dst_ref=gath[b].at[me], **sem))
                ici_wait.append(pltpu.make_async_remote_copy(
                    src_ref=gath[b].at[me], dst_ref=gath[b].at[2 * px + py], **sem))
        for cp in ici:
            cp.start()
        for cp in ici_wait:
            cp.wait_recv()
        for cp in ici:
            cp.wait_send()

        tot = pack
        for b in range(nb):
            tot[b][...] = ((gath[b][0] + gath[b][1]) + gath[b][2]) + gath[b][3]

        loss_ref[...] = tot[0][LOSS_ROW:LOSS_ROW + 1, :]
        dcw_out[...] = tot[2][...]
        lb = _sig(lg_ref[0:1, :] - lg_ref[1:2, :])
        d0 = tot[0][LB_ROW:LB_ROW + 1, :] * lb * (1.0 - lb)
        rowid = lax.broadcasted_iota(jnp.int32, (2, D_MODEL), 0)
        g_lg = jnp.where(rowid == 0, d0, -d0)
        dl, ml, vl = _adamw(lg_ref[...], g_lg, mlg_ref[...], vlg_ref[...])
        lg_outs[0][...] = g_lg
        lg_outs[1][...] = dl
        lg_outs[2][...] = ml
        lg_outs[3][...] = vl
        for si, (_, rows, b, r0) in enumerate(SMALL_LAYOUT):
            g = tot[b][r0:r0 + rows, :]
            dl, ml, vl = _adamw(w_refs[si][...], g, m_refs[si][...], v_refs[si][...])
            outs[4 * si][...] = g
            outs[4 * si + 1][...] = dl
            outs[4 * si + 2][...] = ml
            outs[4 * si + 3][...] = vl

    shapes = [jax.ShapeDtypeStruct((1, D_MODEL), F32), jax.ShapeDtypeStruct((8, D_FF), F32)]
    shapes += [jax.ShapeDtypeStruct((2, D_MODEL), F32)] * 4
    for w in small_w:
        shapes += [jax.ShapeDtypeStruct(w.shape, F32)] * 4
    scratch = [pltpu.VMEM(shp, F32) for shp in PACK_SHAPES]
    scratch += [pltpu.VMEM(shp, F32) for shp in PACK_SHAPES]
    scratch += [pltpu.VMEM((N_CHIP,) + shp, F32) for shp in PACK_SHAPES]
    scratch += [pltpu.SemaphoreType.DMA((nb,)), pltpu.SemaphoreType.DMA((nb,)),
                pltpu.SemaphoreType.DMA((3 * nb,)), pltpu.SemaphoreType.DMA((3 * nb,))]
    vm = pl.BlockSpec(memory_space=pltpu.VMEM)
    n_in = nr + 7 + 3 * ns
    res = _pc(body, name="small_allreduce_adamw", out_shape=tuple(shapes),
              in_specs=[vm] * n_in, out_specs=tuple([vm] * len(shapes)),
              scratch=scratch)(*rows1024, dws, dbs, dcw, dcb, logits, m_logits, v_logits,
                               *small_w, *small_m, *small_v)
    return res[0], res[1], res[2:6], res[6:]


def kernel(x, p, w_in, sgu_w_s, sgu_b_s, sgu_norm_g, sgu_norm_b, hgrn_lb_logits, hgrn_norm_g, w_branch, w_out, ln1_g, ln1_b, ffn_w_up, ffn_conv_w, ffn_conv_b, ffn_w_down, ln2_g, ln2_b, ple_w_proj, ple_w_gate, loss_target, m_w_in, m_sgu_w_s, m_sgu_b_s, m_sgu_norm_g, m_sgu_norm_b, m_hgrn_lb_logits, m_hgrn_norm_g, m_w_branch, m_w_out, m_ln1_g, m_ln1_b, m_ffn_w_up, m_ffn_conv_w, m_ffn_conv_b, m_ffn_w_down, m_ln2_g, m_ln2_b, m_ple_w_proj, m_ple_w_gate, v_w_in, v_sgu_w_s, v_sgu_b_s, v_sgu_norm_g, v_sgu_norm_b, v_hgrn_lb_logits, v_hgrn_norm_g, v_w_branch, v_w_out, v_ln1_g, v_ln1_b, v_ffn_w_up, v_ffn_conv_w, v_ffn_conv_b, v_ffn_w_down, v_ln2_g, v_ln2_b, v_ple_w_proj, v_ple_w_gate):
    t = x.shape[1]
    x2 = x.reshape(t, D_MODEL)
    x2b = x2.astype(BF16)
    p2 = p.reshape(t, PLE_DIM)
    tgt = loss_target.reshape(t, D_MODEL)
    core = lax.axis_index("c").astype(jnp.int32).reshape(1)
    chip_id = (2 * lax.axis_index("x") + lax.axis_index("y")).astype(jnp.int32).reshape(1)

    big_w = [w_in[0], w_branch[0, 0], w_branch[0, 1], w_out[0], ffn_w_up[0], ffn_w_down[0],
             ple_w_proj[0], ple_w_gate[0]]
    big_m = [m_w_in[0], m_w_branch[0, 0], m_w_branch[0, 1], m_w_out[0], m_ffn_w_up[0],
             m_ffn_w_down[0], m_ple_w_proj[0], m_ple_w_gate[0]]
    big_v = [v_w_in[0], v_w_branch[0, 0], v_w_branch[0, 1], v_w_out[0], v_ffn_w_up[0],
             v_ffn_w_down[0], v_ple_w_proj[0], v_ple_w_gate[0]]
    def halves_of(i):
        w = big_w[i]
        return w.astype(BF16).reshape(2, w.shape[0] // 2, w.shape[1])

    def stacked(g, i):
        return g.reshape(N_CHIP, big_w[i].shape[0], big_w[i].shape[1])

    win_g, convw_g = _run_comm("gather_w_in", _gather_comm([halves_of(0)], [ffn_conv_w[0]]))
    win_st = stacked(win_g, 0)
    convw = jnp.transpose(convw_g, (1, 0, 2)).reshape(3, D_FF)

    cid = jnp.arange(SGU_BLOCK) // CHUNK
    maskf = (cid[:, None] >= cid[None, :]).astype(F32)
    ws_masked = sgu_w_s[0] * maskf[None]
    wm = ws_masked.astype(BF16)
    wmt = jnp.transpose(ws_masked, (0, 2, 1)).astype(BF16)
    bsb = jnp.broadcast_to(sgu_b_s[0][:, :, None], (N_GROUP, SGU_BLOCK, 128))

    h, (wup_g,) = _mm_nn_stacked("in_proj", x2b, win_st, 512, comm=_gather_comm([halves_of(4)]))
    wup_st = stacked(wup_g, 4)
    ya = _sgu_fwd(h, wm, bsb, sgu_norm_g, sgu_norm_b)
    rest = (1, 2, 3, 5, 6, 7)
    (yb, st_all), rest_g = _hgrn_fwd(h, hgrn_lb_logits, hgrn_norm_g,
                                      comm=_gather_comm([halves_of(i) for i in rest]))
    rest_g = [stacked(g, i) for g, i in zip(rest_g, rest)]
    wb0 = rest_g[0].reshape(D_MODEL, D_MODEL)
    wb1 = rest_g[1].reshape(D_MODEL, D_MODEL)
    wo = rest_g[2].reshape(D_MODEL, D_MODEL)
    wd = rest_g[3].reshape(D_FF, D_MODEL)
    wpp = jnp.transpose(rest_g[4], (1, 0, 2)).reshape(PLE_DIM, D_MODEL)
    wpg = rest_g[5].reshape(D_MODEL, D_MODEL)
    r1, a_br, b_br, m_bf, x1b = _mix_fwd(ya, yb, h, x2, wb0, wb1, wo, ln1_g, ln1_b, 256)
    h2 = _mm_nn_stacked("ffn_up", x1b, wup_st, 512)
    act = _ffn_act_fwd(h2, convw, ffn_conv_b, 256)
    dr2, dpg, dpp, loss_acc, dg2, db2 = _out_fwd_bwd(
        act, x1b, r1, p2, tgt, wd, wpg, wpp, ln1_g, ln1_b, ln2_g, ln2_b, 256)

    dact = _mm("ffn_down_bwd", dr2, wd, NT, (t // 512, FF_NJ, 1),
               pl.BlockSpec((512, D_MODEL), lambda i, j, k: (i, 0)),
               pl.BlockSpec((FF_TILE, D_MODEL), lambda i, j, k: (j, 0)),
               jax.ShapeDtypeStruct((t, D_FF), F32),
               pl.BlockSpec((512, FF_TILE), lambda i, j, k: (i, j)))
    dh2, dcw, dcb = _ffn_act_bwd(h2, dact, convw, ffn_conv_b, 256)
    d_wd = _mm_tn("ffn_down_wgrad", act, dr2, FF_TILE, 512)
    d_wpg = _mm_tn("ple_gate_wgrad", x1b, dpg, 512, D_MODEL)
    d_wpp_st = _mm_tn("ple_proj_wgrad", p2, dpp, PLE_DIM, PLE_DIM, stacked=True)
    d_wup_st = _mm("ffn_up_wgrad", x1b, dh2, TN, (2, N_CHIP, 1),
                   pl.BlockSpec((t, 512), lambda i, j, k: (0, i)),
                   pl.BlockSpec((None, t, FF_TILE), lambda i, j, k: (j // FF_NJ, 0, j % FF_NJ)),
                   jax.ShapeDtypeStruct((N_CHIP, D_MODEL, FF_TILE), F32),
                   pl.BlockSpec((None, 512, FF_TILE), lambda i, j, k: (j, i, 0)))
    dr1, dg1, db1 = _ffn_in_bwd(dh2, wup_st, dpg, wpg, dr2, r1, ln1_g, 512)
    da_bf, db_bf, dh3, dya, dyb = _mix_bwd(dr1, h, a_br, b_br, wo, wb0, wb1, 256)
    d_wo = _mm_tn("out_proj_wgrad", m_bf, dr1, 512, 512)
    d_wb0 = _mm_tn("branch0_wgrad", ya, da_bf, 512, D_MODEL)
    d_wb1 = _mm_tn("branch1_wgrad", yb, db_bf, 512, D_MODEL)
    dh0, dws, dbs, dgv, dbv = _sgu_bwd(h, dya, wm, wmt, bsb, sgu_norm_g, sgu_norm_b, maskf)

    grads_1 = [d_wb0.reshape(4, 256, D_MODEL), d_wb1.reshape(4, 256, D_MODEL),
               d_wo.reshape(4, 256, D_MODEL), d_wup_st, d_wd.reshape(4, D_FF // 4, D_MODEL),
               d_wpp_st, d_wpg.reshape(4, 256, D_MODEL)]
    recv_a1 = _rs_sibling_exchange("rs_sibling_exchange1", grads_1)
    parts_1 = [_rs_add_halves("rs_add_halves%d" % (i + 1), g, r, core)
               for i, (g, r) in enumerate(zip(grads_1, recv_a1))]
    (dh1, dh2h, dlb, dgn), recv_b1 = _hgrn_bwd(h, dyb, st_all, hgrn_lb_logits, hgrn_norm_g,
                                                comm=_chip_exchange_comm(parts_1))
    dh_parts = [dh0, dh1, dh2h, dh3]
    d_win = [_mm_tn("in_proj_wgrad%d" % j, x2b, dh_parts[j], 512, D_MODEL) for j in range(4)]

    grads_0 = [jnp.stack(d_win)]
    recv_a0 = _rs_sibling_exchange("rs_sibling_exchange0", grads_0)
    parts_0 = [_rs_add_halves("rs_add_halves0", grads_0[0], recv_a0[0], core)]
    gx, recv_b0 = _in_proj_xgrad(dh_parts, win_st, dr1, 512, comm=_chip_exchange_comm(parts_0))
    parts = parts_0 + parts_1
    recv_b = list(recv_b0) + list(recv_b1)
    halves = [_rs_sum_chips("rs_sum_chips%d" % i, pt, r, chip_id)
              for i, (pt, r) in enumerate(zip(parts, recv_b))]
    theirs = _rs_send_halves(halves)
    big_out = [_adamw_rows("adamw_big%d" % i, halves[i], theirs[i], big_w[i], big_m[i], big_v[i], core)
               for i in range(len(halves))]

    small_in = dict(sgu_w_s=(sgu_w_s, m_sgu_w_s, v_sgu_w_s), sgu_b_s=(sgu_b_s, m_sgu_b_s, v_sgu_b_s),
                    sgu_norm_g=(sgu_norm_g, m_sgu_norm_g, v_sgu_norm_g),
                    sgu_norm_b=(sgu_norm_b, m_sgu_norm_b, v_sgu_norm_b),
                    hgrn_norm_g=(hgrn_norm_g, m_hgrn_norm_g, v_hgrn_norm_g),
                    ln1_g=(ln1_g, m_ln1_g, v_ln1_g), ln1_b=(ln1_b, m_ln1_b, v_ln1_b),
                    ffn_conv_b=(ffn_conv_b, m_ffn_conv_b, v_ffn_conv_b),
                    ln2_g=(ln2_g, m_ln2_g, v_ln2_g), ln2_b=(ln2_b, m_ln2_b, v_ln2_b))

    def flat(name, arr):
        rows = dict((n, r) for n, r, _, _ in SMALL_LAYOUT)[name]
        return arr.reshape(rows, arr.size // rows)

    names = [n for n, _, _, _ in SMALL_LAYOUT]
    sw = [flat(n, small_in[n][0]) for n in names]
    sm = [flat(n, small_in[n][1]) for n in names]
    sv = [flat(n, small_in[n][2]) for n in names]
    loss_rows, dcw_tot, lg_out, small_out = _small_allreduce_adamw(
        [dgv, dbv, dlb, dgn, dg1, db1, dg2, db2, loss_acc], dws.reshape(N_GROUP * 128, 128), dbs, dcw, dcb,
        hgrn_lb_logits, m_hgrn_lb_logits, v_hgrn_lb_logits, sw, sm, sv)
    loss = loss_rows[0, 0]

    chip = 2 * lax.axis_index("x") + lax.axis_index("y")
    g_cw = lax.dynamic_slice(dcw_tot, (0, chip * (D_FF // 4)), (3, D_FF // 4))
    cw_out = _adamw_whole("adamw_conv_w", g_cw, ffn_conv_w[0], m_ffn_conv_w[0], v_ffn_conv_w[0])

    res = {}
    for si, n in enumerate(names):
        shp = small_in[n][0].shape
        res[n] = tuple(small_out[4 * si + k].reshape(shp) for k in range(4))
    res["hgrn_lb_logits"] = tuple(lg_out)
    res["ffn_conv_w"] = (g_cw[None],) + tuple(o[None] for o in cw_out)

    def big(i):
        return tuple(big_out[i])

    res["w_in"] = tuple(o[None] for o in big(0))
    res["w_branch"] = tuple(jnp.stack([o0, o1])[None] for o0, o1 in zip(big(1), big(2)))
    res["w_out"] = tuple(o[None] for o in big(3))
    res["ffn_w_up"] = tuple(o[None] for o in big(4))
    res["ffn_w_down"] = tuple(o[None] for o in big(5))
    res["ple_w_proj"] = tuple(o[None] for o in big(6))
    res["ple_w_gate"] = tuple(o[None] for o in big(7))

    order = ["w_in", "sgu_w_s", "sgu_b_s", "sgu_norm_g", "sgu_norm_b", "hgrn_lb_logits",
             "hgrn_norm_g", "w_branch", "w_out", "ln1_g", "ln1_b", "ffn_w_up", "ffn_conv_w",
             "ffn_conv_b", "ffn_w_down", "ln2_g", "ln2_b", "ple_w_proj", "ple_w_gate"]
    outs = [loss, gx.reshape(1, t, D_MODEL)]
    for k in range(4):
        outs += [res[n][k] for n in order]
    return tuple(outs)
```

```python
import functools

import jax
import jax.numpy as jnp
from jax import lax
from jax.experimental import pallas as pl
from jax.experimental.pallas import tpu as pltpu

F32 = jnp.float32
BF16 = jnp.bfloat16
HIGHEST = lax.Precision.HIGHEST
MESH = pl.DeviceIdType.MESH

D_MODEL = 1024
CHUNK = 64
SGU_BLOCK = 128
N_GROUP = 8
N_HEAD = 8
HEAD_DIM = 128
D_FF = 2816
PLE_DIM = 256
IN_COLS = 8192
LN_EPS = 1e-5
RMS_EPS = 1e-6
ALPHA = 2.0 ** 0.25
N_CHIP = 4
N_DEV = 8

ADAM_LR = 0.001
ADAM_B1 = 0.9
ADAM_B2 = 0.999
ADAM_EPS = 1e-08
ADAM_WD = 0.01
ADAM_STEP = 10

VMEM_LIMIT = 56 * 1024 * 1024

NN = (((1,), (0,)), ((), ()))
NT = (((1,), (1,)), ((), ()))
TN = (((0,), (0,)), ((), ()))


def _pc(body, *, name, out_shape, grid=None, in_specs=None, out_specs=None, scratch=(),
        sem=None, nsp=0, vmem=VMEM_LIMIT):
    params = dict(vmem_limit_bytes=vmem)
    if sem is not None:
        params["dimension_semantics"] = sem
    kw = dict(name=name, out_shape=out_shape, compiler_params=pltpu.CompilerParams(**params))
    if nsp:
        kw["grid_spec"] = pltpu.PrefetchScalarGridSpec(
            num_scalar_prefetch=nsp, grid=grid, in_specs=in_specs, out_specs=out_specs,
            scratch_shapes=list(scratch))
    else:
        if grid is not None:
            kw["grid"] = grid
        if in_specs is not None:
            kw["in_specs"] = in_specs
            kw["out_specs"] = out_specs
        kw["scratch_shapes"] = list(scratch)
    return pl.pallas_call(body, **kw)


def _dot(a, b, dims=NN):
    return lax.dot_general(a.astype(BF16), b.astype(BF16), dims, preferred_element_type=F32)


def _dot32(a, b, dims=NN):
    return lax.dot_general(a, b, dims, precision=HIGHEST, preferred_element_type=F32)


def _sig(x):
    return 1.0 / (1.0 + jnp.exp(-x))


_GC = 0.7978845608028654
_GA = 0.044715


def _gelu(x):
    return 0.5 * x * (1.0 + jnp.tanh(_GC * (x + _GA * x * x * x)))


def _gelu_and_grad(x):
    t = jnp.tanh(_GC * (x + _GA * x * x * x))
    g = 0.5 * x * (1.0 + t)
    dg = 0.5 * (1.0 + t) + 0.5 * x * (1.0 - t * t) * _GC * (1.0 + 3.0 * _GA * x * x)
    return g, dg


def _ln_stats(r):
    mu = jnp.mean(r, axis=-1, keepdims=True)
    xc = r - mu
    var = jnp.mean(xc * xc, axis=-1, keepdims=True)
    rstd = lax.rsqrt(var + LN_EPS)
    return xc * rstd, rstd


def _ln_bwd(dxh, xh, rstd):
    m1 = jnp.mean(dxh, axis=-1, keepdims=True)
    m2 = jnp.mean(dxh * xh, axis=-1, keepdims=True)
    return rstd * (dxh - m1 - xh * m2)


def _colsum8(v):
    return jnp.broadcast_to(jnp.sum(v, axis=0, keepdims=True), (8, v.shape[1]))


def _adamw(w, g, m, v):
    m2 = ADAM_B1 * m + (1.0 - ADAM_B1) * g
    v2 = ADAM_B2 * v + (1.0 - ADAM_B2) * (g * g)
    m_hat = m2 / (1.0 - ADAM_B1 ** ADAM_STEP)
    v_hat = v2 / (1.0 - ADAM_B2 ** ADAM_STEP)
    delta = -ADAM_LR * (m_hat / (jnp.sqrt(v_hat) + ADAM_EPS) + ADAM_WD * w)
    return delta, m2, v2


def _row_tile(rows, cols, itemsize=4, budget=1 << 20, mult=8):
    best = mult
    for tr in range(mult, rows + 1, mult):
        if rows % tr == 0 and tr * cols * itemsize <= budget:
            best = tr
    return best


def _mm(name, a, b, dims, grid, a_spec, b_spec, out_shape, o_spec, add=None, add_spec=None,
        add_scale=1.0, comm=None):
    nk = grid[2]
    has_add = add is not None
    out_dtype = out_shape.dtype

    def body(*refs):
        if has_add:
            a_ref, b_ref, add_ref, o_ref = refs[:4]
            rest = refs[4:]
        else:
            a_ref, b_ref, o_ref = refs[:3]
            add_ref = None
            rest = refs[3:]
        prod = _dot(a_ref[...], b_ref[...], dims)

        def finish(acc):
            if has_add:
                acc = acc + add_scale * add_ref[...]
            o_ref[...] = acc.astype(out_dtype)

        if nk == 1:
            finish(prod)
        else:
            acc_ref = rest[0]
            k = pl.program_id(2)

            @pl.when(k == 0)
            def _():
                acc_ref[...] = prod

            @pl.when(k > 0)
            def _():
                acc_ref[...] += prod

            @pl.when(k == nk - 1)
            def _():
                finish(acc_ref[...])

    in_specs = [a_spec, b_spec] + ([add_spec] if has_add else [])
    args = [a, b] + ([add] if has_add else [])
    scratch = []
    if nk > 1:
        blk = [d for d in o_spec.block_shape if d is not None]
        scratch = [pltpu.VMEM(tuple(blk), F32)]
    if comm is None:
        return _pc(body, name=name, out_shape=out_shape, grid=grid, in_specs=in_specs,
                   out_specs=o_spec, scratch=scratch,
                   sem=("parallel", "parallel", "arbitrary"))(*args)

    def first():
        return (pl.program_id(0) == 0) & (pl.program_id(1) == 0) & (pl.program_id(2) == 0)

    def last():
        return ((pl.program_id(0) == grid[0] - 1) & (pl.program_id(1) == grid[1] - 1)
                & (pl.program_id(2) == grid[2] - 1))

    res = _hosted_call(body, comm, first, last, name=name, out_shape=(out_shape,), grid=grid,
                       in_specs=in_specs, out_specs=(o_spec,), scratch=scratch,
                       sem=("arbitrary", "arbitrary", "arbitrary"), args=args)
    return res[0], res[1:]


class _Comm:
    def __init__(self, ins, out_shapes, sems, start, finish):
        self.ins, self.out_shapes, self.sems = list(ins), list(out_shapes), list(sems)
        self.start, self.finish = start, finish


def _hosted_call(body, comm, first, last, *, name, out_shape, grid, in_specs, out_specs, scratch, sem,
                 args):
    n_in, n_out, n_scr = len(in_specs), len(out_shape), len(scratch)
    nci, nco = len(comm.ins), len(comm.out_shapes)

    def wrapped(*refs):
        pos = n_in
        own_in, c_in = refs[:pos], refs[pos:pos + nci]
        pos += nci
        own_out, c_out = refs[pos:pos + n_out], refs[pos + n_out:pos + n_out + nco]
        pos += n_out + nco
        own_scr, c_sem = refs[pos:pos + n_scr], refs[pos + n_scr:]

        @pl.when(first())
        def _():
            comm.start(c_in, c_out, c_sem)

        body(*own_in, *own_out, *own_scr)

        @pl.when(last())
        def _():
            comm.finish(c_in, c_out, c_sem)

    return _pc(wrapped, name=name, out_shape=tuple(out_shape) + tuple(comm.out_shapes), grid=grid,
               in_specs=list(in_specs) + [ANY] * nci, out_specs=tuple(out_specs) + tuple([ANY] * nco),
               scratch=list(scratch) + comm.sems, sem=sem)(*args, *comm.ins)


def _grid1_call(body, comm, n, *, name, out_shape, in_specs, out_specs, scratch, args):
    if comm is None:
        return _pc(body, name=name, out_shape=out_shape, grid=(n,), in_specs=in_specs,
                   out_specs=out_specs, scratch=scratch, sem=("arbitrary",))(*args), ()
    res = _hosted_call(body, comm, lambda: pl.program_id(0) == 0, lambda: pl.program_id(0) == n - 1,
                       name=name, out_shape=out_shape, grid=(n,), in_specs=in_specs,
                       out_specs=out_specs, scratch=scratch, sem=("arbitrary",), args=args)
    return res[:len(out_shape)], res[len(out_shape):]


def _run_comm(name, comm):
    nci, nco = len(comm.ins), len(comm.out_shapes)

    def body(*refs):
        c_in, c_out, c_sem = refs[:nci], refs[nci:nci + nco], refs[nci + nco:]
        comm.start(c_in, c_out, c_sem)
        comm.finish(c_in, c_out, c_sem)

    return _pc(body, name=name, out_shape=tuple(comm.out_shapes), in_specs=[ANY] * nci,
               out_specs=tuple([ANY] * nco), scratch=comm.sems)(*comm.ins)


def _mm_nn_stacked(name, a, w_st, tm, comm=None):
    t, k = a.shape
    _, _, c = w_st.shape
    return _mm(name, a, w_st, NN, (t // tm, N_CHIP, 1),
               pl.BlockSpec((tm, k), lambda i, j, kk: (i, 0)),
               pl.BlockSpec((None, k, c), lambda i, j, kk: (j, 0, 0)),
               jax.ShapeDtypeStruct((t, N_CHIP * c), BF16),
               pl.BlockSpec((tm, c), lambda i, j, kk: (i, j)), comm=comm)


def _mm_tn(name, a, b, tm, tn, stacked=False):
    t, m = a.shape
    _, n = b.shape
    if stacked:
        assert tm == m
        out_shape = jax.ShapeDtypeStruct((n // tn, m, tn), F32)
        o_spec = pl.BlockSpec((None, tm, tn), lambda i, j, kk: (j, 0, 0))
    else:
        out_shape = jax.ShapeDtypeStruct((m, n), F32)
        o_spec = pl.BlockSpec((tm, tn), lambda i, j, kk: (i, j))
    return _mm(name, a, b, TN, (m // tm, n // tn, 1),
               pl.BlockSpec((t, tm), lambda i, j, kk: (0, i)),
               pl.BlockSpec((t, tn), lambda i, j, kk: (0, j)),
               out_shape, o_spec)


def _in_proj_xgrad(dh_parts, win_st, dr1, tm, comm=None):
    t = dr1.shape[0]
    ni = t // tm

    def body(a0, a1, a2, a3, b_ref, add_ref, o_ref, acc):
        j = pl.program_id(1)
        for jj, a_ref in enumerate((a0, a1, a2, a3)):
            @pl.when(j == jj)
            def _(jj=jj, a_ref=a_ref):
                prod = _dot(a_ref[...], b_ref[...], NT)
                if jj == 0:
                    acc[...] = prod + ALPHA * add_ref[...]
                elif jj < N_CHIP - 1:
                    acc[...] += prod
                else:
                    o_ref[...] = acc[...] + prod

    a_spec = pl.BlockSpec((tm, 2 * D_MODEL), lambda i, j: (i, 0))
    tile = pl.BlockSpec((tm, D_MODEL), lambda i, j: (i, 0))
    kw = dict(name="in_proj_xgrad", out_shape=(jax.ShapeDtypeStruct((t, D_MODEL), F32),),
              grid=(ni, N_CHIP),
              in_specs=[a_spec] * 4 + [pl.BlockSpec((None, D_MODEL, 2 * D_MODEL), lambda i, j: (j, 0, 0)),
                                       tile],
              out_specs=(tile,), scratch=[pltpu.VMEM((tm, D_MODEL), F32)],
              sem=("arbitrary", "arbitrary"))
    args = list(dh_parts) + [win_st, dr1]
    if comm is None:
        return _pc(body, **kw)(*args)[0], ()
    res = _hosted_call(body, comm,
                       lambda: (pl.program_id(0) == 0) & (pl.program_id(1) == 0),
                       lambda: (pl.program_id(0) == ni - 1) & (pl.program_id(1) == N_CHIP - 1),
                       args=args, **kw)
    return res[0], res[1:]


def _sgu_mixed(v, wm_ref, bsb_ref, gv, bv):
    gl, dgl = _gelu_and_grad(v)
    vh, rstd = _ln_stats(gl)
    vn = vh * gv + bv
    mixed = []
    for g in range(N_GROUP):
        sl = slice(g * 128, (g + 1) * 128)
        mixed.append(_dot(wm_ref[g], vn[:, sl]) + bsb_ref[g])
    return dgl, vh, rstd, vn, mixed


def _sgu_fwd(h, wm, bsb, gv, bv):
    t = h.shape[0]

    def body(u_ref, v_ref, wm_ref, bsb_ref, gv_ref, bv_ref, ya_ref):
        u = u_ref[...].astype(F32)
        _, _, _, _, mixed = _sgu_mixed(v_ref[...].astype(F32), wm_ref, bsb_ref, gv_ref[...], bv_ref[...])
        gu = _gelu(u)
        for g in range(N_GROUP):
            sl = slice(g * 128, (g + 1) * 128)
            ya_ref[:, sl] = (gu[:, sl] * mixed[g]).astype(BF16)

    full3 = pl.BlockSpec((N_GROUP, 128, 128), lambda i: (0, 0, 0))
    vec = pl.BlockSpec((1, D_MODEL), lambda i: (0, 0))
    return _pc(body, name="sgu_fwd", out_shape=jax.ShapeDtypeStruct((t, D_MODEL), BF16),
               grid=(t // SGU_BLOCK,),
               in_specs=[pl.BlockSpec((SGU_BLOCK, D_MODEL), lambda i: (i, 0)),
                         pl.BlockSpec((SGU_BLOCK, D_MODEL), lambda i: (i, 1)),
                         full3, full3, vec, vec],
               out_specs=pl.BlockSpec((SGU_BLOCK, D_MODEL), lambda i: (i, 0)),
               sem=("parallel",))(h, h, wm, bsb, gv, bv)


def _sgu_bwd(h, dya, wm, wmt, bsb, gv, bv, maskf, comm=None):
    t = h.shape[0]
    nb = t // SGU_BLOCK

    def body(u_ref, v_ref, dya_ref, wm_ref, wmt_ref, bsb_ref, gv_ref, bv_ref, mask_ref,
             dh_ref, dws_ref, dbs_ref, dgv_ref, dbv_ref, dmix_acc):
        i = pl.program_id(0)

        @pl.when(i == 0)
        def _():
            dws_ref[...] = jnp.zeros_like(dws_ref)
            dgv_ref[...] = jnp.zeros_like(dgv_ref)
            dbv_ref[...] = jnp.zeros_like(dbv_ref)
            dmix_acc[...] = jnp.zeros_like(dmix_acc)

        u = u_ref[...].astype(F32)
        gvv = gv_ref[...]
        dgl_v, vh, rstd, vn, mixed = _sgu_mixed(v_ref[...].astype(F32), wm_ref, bsb_ref, gvv, bv_ref[...])
        gu, dgl_u = _gelu_and_grad(u)
        dya_v = dya_ref[...]
        dvn_parts = []
        for g in range(N_GROUP):
            sl = slice(g * 128, (g + 1) * 128)
            d_y = dya_v[:, sl]
            dh_ref[:, sl] = (d_y * mixed[g] * dgl_u[:, sl]).astype(BF16)
            d_mixed = d_y * gu[:, sl]
            dmix_acc[g] += d_mixed
            dws_ref[g] += _dot(d_mixed, vn[:, sl], NT) * mask_ref[...]
            dvn_parts.append(_dot(wmt_ref[g], d_mixed))
        dvn = jnp.concatenate(dvn_parts, axis=1)
        dgv_ref[...] += _colsum8(dvn * vh)
        dbv_ref[...] += _colsum8(dvn)
        d_gl = _ln_bwd(dvn * gvv, vh, rstd)
        dh_ref[:, D_MODEL:] = (d_gl * dgl_v).astype(BF16)

        @pl.when(i == nb - 1)
        def _():
            rowid = lax.broadcasted_iota(jnp.int32, (8, 128), 0)
            ones = jnp.ones((8, 128), F32)
            acc = jnp.zeros((8, 128), F32)
            for g in range(N_GROUP):
                rs = _dot32(ones, dmix_acc[g], NT)
                acc = jnp.where(rowid == g, rs, acc)
            dbs_ref[...] = acc

    full3 = pl.BlockSpec((N_GROUP, 128, 128), lambda i: (0, 0, 0))
    vec = pl.BlockSpec((1, D_MODEL), lambda i: (0, 0))
    acc8 = pl.BlockSpec((8, D_MODEL), lambda i: (0, 0))
    return _grid1_call(
        body, comm, nb, name="sgu_bwd",
        out_shape=(jax.ShapeDtypeStruct((t, 2 * D_MODEL), BF16),
                   jax.ShapeDtypeStruct((N_GROUP, 128, 128), F32),
                   jax.ShapeDtypeStruct((8, 128), F32),
                   jax.ShapeDtypeStruct((8, D_MODEL), F32),
                   jax.ShapeDtypeStruct((8, D_MODEL), F32)),
        in_specs=[pl.BlockSpec((SGU_BLOCK, D_MODEL), lambda i: (i, 0)),
                  pl.BlockSpec((SGU_BLOCK, D_MODEL), lambda i: (i, 1)),
                  pl.BlockSpec((SGU_BLOCK, D_MODEL), lambda i: (i, 0)),
                  full3, full3, full3, vec, vec,
                  pl.BlockSpec((128, 128), lambda i: (0, 0))],
        out_specs=(pl.BlockSpec((SGU_BLOCK, 2 * D_MODEL), lambda i: (i, 0)),
                   full3, pl.BlockSpec((8, 128), lambda i: (0, 0)), acc8, acc8),
        scratch=[pltpu.VMEM((N_GROUP, 128, 128), F32)],
        args=(h, h, dya, wm, wmt, bsb, gv, bv, maskf))


def _tri_masks():
    row = lax.broadcasted_iota(jnp.int32, (CHUNK, CHUNK), 0)
    col = lax.broadcasted_iota(jnp.int32, (CHUNK, CHUNK), 1)
    return col <= row, col >= row


def _heads(v):
    return [v[:, hd * HEAD_DIM:(hd + 1) * HEAD_DIM] for hd in range(N_HEAD)]


def _tri_cumsum(tri_bf, v):
    hi = v.astype(BF16)
    r = v - hi.astype(F32)
    mid = r.astype(BF16)
    lo = (r - mid.astype(F32)).astype(BF16)
    return _dot(tri_bf, hi) + _dot(tri_bf, mid) + _dot(tri_bf, lo)


def _hgrn_chunk(q, fp, ii, lb, st_heads, causal):
    sg = _sig(fp)
    f = lb + (1.0 - lb) * sg
    k = 1.0 - f
    c = _tri_cumsum(causal.astype(BF16), jnp.log(f))
    ec = jnp.exp(c)
    en = jnp.exp(-c)
    sq = _sig(q)
    qt = q * sq * ec
    kt = k * en
    ecl = jnp.exp(c[CHUNK - 1:CHUNK, :])
    kk = kt * ecl
    qtb, ktb, iib, kkb = qt.astype(BF16), kt.astype(BF16), ii.astype(BF16), kk.astype(BF16)
    attn, o = [], []
    for hd, (qh, kh, ih) in enumerate(zip(_heads(qtb), _heads(ktb), _heads(iib))):
        a = jnp.where(causal, _dot(qh, kh, NT), 0.0).astype(BF16)
        attn.append(a)
        o.append(_dot(a, ih) + _dot(qh, st_heads[hd], NT))
    return dict(sg=sg, f=f, k=k, ec=ec, en=en, sq=sq, ecl=ecl, kk=kk, qtb=qtb, ktb=ktb, iib=iib,
                kkb=kkb, attn=attn, o=o)


def _rms_heads(o_heads):
    rinv = [lax.rsqrt(jnp.mean(o * o, axis=-1, keepdims=True) + RMS_EPS) for o in o_heads]
    return rinv, jnp.concatenate([o * r for o, r in zip(o_heads, rinv)], axis=1)


def _hgrn_fwd(h, logits, gn, comm=None):
    t = h.shape[0]
    nc = t // CHUNK

    def body(q_ref, f_ref, i_ref, og_ref, lg_ref, gn_ref, yb_ref, st_ref, state):
        ci = pl.program_id(0)

        @pl.when(ci == 0)
        def _():
            state[...] = jnp.zeros_like(state)

        causal, _ = _tri_masks()
        st = [state[hd] for hd in range(N_HEAD)]
        og = og_ref[...].astype(F32)
        lb = _sig(lg_ref[0:1, :] - lg_ref[1:2, :])
        r = _hgrn_chunk(q_ref[...].astype(F32), f_ref[...].astype(F32), i_ref[...].astype(F32), lb,
                        [s.astype(BF16) for s in st], causal)
        _, on = _rms_heads(r["o"])
        ecl = _heads(r["ecl"])
        new = [s * e + _dot(ih, kh, TN)
               for s, e, ih, kh in zip(st, ecl, _heads(r["iib"]), _heads(r["kkb"]))]
        yb_ref[...] = (on * gn_ref[...] * (og * _sig(og))).astype(BF16)
        for hd in range(N_HEAD):
            st_ref[0, hd] = st[hd]
            state[hd] = new[hd]

    def col(k):
        return pl.BlockSpec((CHUNK, D_MODEL), lambda ci: (ci, k))

    return _grid1_call(body, comm, nc, name="hgrn_fwd",
                       out_shape=(jax.ShapeDtypeStruct((t, D_MODEL), BF16),
                                  jax.ShapeDtypeStruct((nc, N_HEAD, HEAD_DIM, HEAD_DIM), F32)),
                       in_specs=[col(2), col(3), col(4), col(5),
                                 pl.BlockSpec((2, D_MODEL), lambda ci: (0, 0)),
                                 pl.BlockSpec((1, D_MODEL), lambda ci: (0, 0))],
                       out_specs=(pl.BlockSpec((CHUNK, D_MODEL), lambda ci: (ci, 0)),
                                  pl.BlockSpec((1, N_HEAD, HEAD_DIM, HEAD_DIM), lambda ci: (ci, 0, 0, 0))),
                       scratch=[pltpu.VMEM((N_HEAD, HEAD_DIM, HEAD_DIM), F32)],
                       args=(h, h, h, h, logits, gn))


def _hgrn_bwd(h, dyb, st_all, logits, gn, comm=None):
    t = h.shape[0]
    nc = t // CHUNK

    def body(q_ref, f_ref, i_ref, og_ref, dyb_ref, st_ref, lg_ref, gn_ref,
             dh1_ref, dh2_ref, dlb_ref, dgn_ref, dstate):
        ci = pl.program_id(0)

        @pl.when(ci == 0)
        def _():
            dstate[...] = jnp.zeros_like(dstate)
            dlb_ref[...] = jnp.zeros_like(dlb_ref)
            dgn_ref[...] = jnp.zeros_like(dgn_ref)

        causal, anti = _tri_masks()
        q, og = q_ref[...].astype(F32), og_ref[...].astype(F32)
        dy, gnv = dyb_ref[...], gn_ref[...]
        lb = _sig(lg_ref[0:1, :] - lg_ref[1:2, :])
        st = [st_ref[0, hd] for hd in range(N_HEAD)]
        dsn = [dstate[hd] for hd in range(N_HEAD)]
        stb = [s.astype(BF16) for s in st]
        dsnb = [s.astype(BF16) for s in dsn]
        r = _hgrn_chunk(q, f_ref[...].astype(F32), i_ref[...].astype(F32), lb, stb, causal)
        rinv, on = _rms_heads(r["o"])
        so = _sig(og)
        sil = og * so
        d_og = dy * on * gnv * (so * (1.0 + og * (1.0 - so)))
        d_on = dy * gnv * sil
        d_ob = jnp.concatenate(
            [ri * (dn - oh * jnp.mean(dn * oh, axis=-1, keepdims=True))
             for ri, dn, oh in zip(rinv, _heads(d_on), _heads(on))], axis=1).astype(BF16)
        d_i, d_qt, d_kt, d_kk, d_st, st_dsn = [], [], [], [], [], []
        ecl = _heads(r["ecl"])
        for hd, (dh, qh, kh, ih, kkh) in enumerate(zip(_heads(d_ob), _heads(r["qtb"]), _heads(r["ktb"]),
                                                       _heads(r["iib"]), _heads(r["kkb"]))):
            d_attn = jnp.where(causal, _dot(dh, ih, NT), 0.0).astype(BF16)
            d_i.append(_dot(r["attn"][hd], dh, TN) + _dot(kkh, dsnb[hd], NT))
            d_qt.append(_dot(d_attn, kh) + _dot(dh, stb[hd]))
            d_kt.append(_dot(d_attn, qh, TN))
            d_kk.append(_dot(ih, dsnb[hd]))
            d_st.append(_dot(dh, qh, TN) + dsn[hd] * ecl[hd])
            st_dsn.append(jnp.sum(st[hd] * dsn[hd], axis=0, keepdims=True))
        d_qt = jnp.concatenate(d_qt, axis=1)
        d_kt = jnp.concatenate(d_kt, axis=1)
        d_kk = jnp.concatenate(d_kk, axis=1)
        kk = r["kk"]
        d_cl = (r["ecl"] * jnp.concatenate(st_dsn, axis=1)
                + jnp.sum(kk * d_kk, axis=0, keepdims=True))
        d_k = (d_kk * r["ecl"] + d_kt) * r["en"]
        d_c = d_qt * r["qtb"].astype(F32) - d_kt * r["ktb"].astype(F32) - d_kk * kk
        rowid = lax.broadcasted_iota(jnp.int32, (CHUNK, D_MODEL), 0)
        d_c = d_c + jnp.where(rowid == CHUNK - 1, d_cl, 0.0)
        d_lf = _tri_cumsum(anti.astype(BF16), d_c)
        d_f = d_lf / r["f"] - d_k
        sg = r["sg"]
        sq = r["sq"]
        dgn_ref[...] += _colsum8(dy * on * sil)
        dlb_ref[...] += _colsum8(d_f * (1.0 - sg))
        dh1_ref[:, :D_MODEL] = (d_qt * r["ec"] * (sq * (1.0 + q * (1.0 - sq)))).astype(BF16)
        dh1_ref[:, D_MODEL:] = (d_f * (1.0 - lb) * sg * (1.0 - sg)).astype(BF16)
        dh2_ref[:, :D_MODEL] = jnp.concatenate(d_i, axis=1).astype(BF16)
        dh2_ref[:, D_MODEL:] = d_og.astype(BF16)
        for hd in range(N_HEAD):
            dstate[hd] = d_st[hd]

    def col(k):
        return pl.BlockSpec((CHUNK, D_MODEL), lambda ci: (nc - 1 - ci, k))

    acc8 = pl.BlockSpec((8, D_MODEL), lambda ci: (0, 0))
    pair = pl.BlockSpec((CHUNK, 2 * D_MODEL), lambda ci: (nc - 1 - ci, 0))
    return _grid1_call(body, comm, nc, name="hgrn_bwd",
                       out_shape=(jax.ShapeDtypeStruct((t, 2 * D_MODEL), BF16),
                                  jax.ShapeDtypeStruct((t, 2 * D_MODEL), BF16),
                                  jax.ShapeDtypeStruct((8, D_MODEL), F32),
                                  jax.ShapeDtypeStruct((8, D_MODEL), F32)),
                       in_specs=[col(2), col(3), col(4), col(5),
                                 pl.BlockSpec((CHUNK, D_MODEL), lambda ci: (nc - 1 - ci, 0)),
                                 pl.BlockSpec((1, N_HEAD, HEAD_DIM, HEAD_DIM),
                                              lambda ci: (nc - 1 - ci, 0, 0, 0)),
                                 pl.BlockSpec((2, D_MODEL), lambda ci: (0, 0)),
                                 pl.BlockSpec((1, D_MODEL), lambda ci: (0, 0))],
                       out_specs=(pair, pair, acc8, acc8),
                       scratch=[pltpu.VMEM((N_HEAD, HEAD_DIM, HEAD_DIM), F32)],
                       args=(h, h, h, h, dyb, st_all, logits, gn))


def _mix_fwd(ya, yb, h, x, wb0, wb1, wo, g1, b1, tm):
    t = x.shape[0]

    def body(ya_ref, yb_ref, ga_ref, gb_ref, x_ref, wb0_ref, wb1_ref, wo_ref, g1_ref, b1_ref,
             r1_ref, a_ref, b_ref, m_ref, x1_ref):
        a = _dot(ya_ref[...], wb0_ref[...])
        b = _dot(yb_ref[...], wb1_ref[...])
        m = _sig(ga_ref[...].astype(F32)) * a + _sig(gb_ref[...].astype(F32)) * b
        r1 = ALPHA * x_ref[...] + _dot(m, wo_ref[...])
        xh, _ = _ln_stats(r1)
        r1_ref[...] = r1
        a_ref[...] = a
        b_ref[...] = b
        m_ref[...] = m.astype(BF16)
        x1_ref[...] = (xh * g1_ref[...] + b1_ref[...]).astype(BF16)

    tile = pl.BlockSpec((tm, D_MODEL), lambda i: (i, 0))
    wsp = pl.BlockSpec((D_MODEL, D_MODEL), lambda i: (0, 0))
    vec = pl.BlockSpec((1, D_MODEL), lambda i: (0, 0))
    f32o = jax.ShapeDtypeStruct((t, D_MODEL), F32)
    bfo = jax.ShapeDtypeStruct((t, D_MODEL), BF16)
    return _pc(body, name="mix_fwd", out_shape=(f32o, f32o, f32o, bfo, bfo), grid=(t // tm,),
               in_specs=[tile, tile,
                         pl.BlockSpec((tm, D_MODEL), lambda i: (i, 6)),
                         pl.BlockSpec((tm, D_MODEL), lambda i: (i, 7)),
                         tile, wsp, wsp, wsp, vec, vec],
               out_specs=(tile, tile, tile, tile, tile),
               sem=("parallel",))(ya, yb, h, h, x, wb0, wb1, wo, g1, b1)


def _mix_bwd(dr1, h, a, b, wo, wb0, wb1, tm):
    t = dr1.shape[0]

    def body(dr1_ref, ga_ref, gb_ref, a_ref, b_ref, wo_ref, wb0_ref, wb1_ref,
             da_ref, db_ref, dh3_ref, dya_ref, dyb_ref):
        d_m = _dot(dr1_ref[...], wo_ref[...], NT)
        sa = _sig(ga_ref[...].astype(F32))
        sb = _sig(gb_ref[...].astype(F32))
        d_a = (d_m * sa).astype(BF16)
        d_b = (d_m * sb).astype(BF16)
        da_ref[...] = d_a
        db_ref[...] = d_b
        dh3_ref[:, :D_MODEL] = (d_m * a_ref[...] * sa * (1.0 - sa)).astype(BF16)
        dh3_ref[:, D_MODEL:] = (d_m * b_ref[...] * sb * (1.0 - sb)).astype(BF16)
        dya_ref[...] = _dot(d_a, wb0_ref[...], NT)
        dyb_ref[...] = _dot(d_b, wb1_ref[...], NT)

    tile = pl.BlockSpec((tm, D_MODEL), lambda i: (i, 0))
    wsp = pl.BlockSpec((D_MODEL, D_MODEL), lambda i: (0, 0))
    f32o = jax.ShapeDtypeStruct((t, D_MODEL), F32)
    bfo = jax.ShapeDtypeStruct((t, D_MODEL), BF16)
    return _pc(body, name="mix_bwd",
               out_shape=(bfo, bfo, jax.ShapeDtypeStruct((t, 2 * D_MODEL), BF16), f32o, f32o),
               grid=(t // tm,),
               in_specs=[tile,
                         pl.BlockSpec((tm, D_MODEL), lambda i: (i, 6)),
                         pl.BlockSpec((tm, D_MODEL), lambda i: (i, 7)),
                         tile, tile, wsp, wsp, wsp],
               out_specs=(tile, tile, pl.BlockSpec((tm, 2 * D_MODEL), lambda i: (i, 0)),
                          tile, tile),
               sem=("parallel",))(dr1, h, h, a, b, wo, wb0, wb1)


FF_TILE = 1408
FF_NJ = D_FF // FF_TILE


def _shift_down(v, k):
    return pltpu.roll(v, k, 0)


def _shift_up(v, k):
    return pltpu.roll(v, v.shape[0] - k, 0)


def _conv_gate(ext, cw_ref, cb_ref):
    return (cw_ref[0:1, :] * _shift_down(ext, 2) + cw_ref[1:2, :] * _shift_down(ext, 1)
            + cw_ref[2:3, :] * ext + cb_ref[...])


HALO = 16


def _ffn_act_fwd(h2, convw, convb, tm):
    t = h2.shape[0]
    nth = tm // HALO

    def body(g_ref, gp_ref, v_ref, cw_ref, cb_ref, act_ref):
        i = pl.program_id(1)
        prev = gp_ref[...].astype(F32) * (i > 0).astype(F32)
        ext = jnp.concatenate([prev, g_ref[...].astype(F32)], axis=0)
        gc = _conv_gate(ext, cw_ref, cb_ref)[HALO:, :]
        act_ref[...] = (_gelu(gc) * v_ref[...].astype(F32)).astype(BF16)

    return _pc(body, name="ffn_act_fwd", out_shape=jax.ShapeDtypeStruct((t, D_FF), BF16),
               grid=(FF_NJ, t // tm),
               in_specs=[pl.BlockSpec((tm, FF_TILE), lambda j, i: (i, j)),
                         pl.BlockSpec((HALO, FF_TILE), lambda j, i: (jnp.maximum(i * nth - 1, 0), j)),
                         pl.BlockSpec((tm, FF_TILE), lambda j, i: (i, j + FF_NJ)),
                         pl.BlockSpec((3, FF_TILE), lambda j, i: (0, j)),
                         pl.BlockSpec((1, FF_TILE), lambda j, i: (0, j))],
               out_specs=pl.BlockSpec((tm, FF_TILE), lambda j, i: (i, j)),
               sem=("parallel", "parallel"))(h2, h2, h2, convw, convb)


def _ffn_act_bwd(h2, dact, convw, convb, tm):
    t = h2.shape[0]
    nth = tm // HALO
    ni = t // tm
    last_halo = t // HALO - 1
    main_rows = slice(HALO, HALO + tm)

    def body(g_ref, gp_ref, gn_ref, v_ref, vn_ref, da_ref, dan_ref, cw_ref, cb_ref,
             dh2_ref, dcw_ref, dcb_ref):
        i = pl.program_id(1)

        @pl.when(i == 0)
        def _():
            dcw_ref[...] = jnp.zeros_like(dcw_ref)
            dcb_ref[...] = jnp.zeros_like(dcb_ref)

        zeros = jnp.zeros((HALO, FF_TILE), F32)
        da = da_ref[...].astype(F32)
        prev = gp_ref[...].astype(F32) * (i > 0).astype(F32)
        ext = jnp.concatenate([prev, g_ref[...].astype(F32), gn_ref[...].astype(F32)], axis=0)
        vext = jnp.concatenate([zeros, v_ref[...].astype(F32), vn_ref[...].astype(F32)], axis=0)
        dnext = dan_ref[...].astype(F32) * (i < ni - 1).astype(F32)
        dext = jnp.concatenate([zeros, da, dnext], axis=0)
        g2 = _shift_down(ext, 2)
        g1 = _shift_down(ext, 1)
        gc = cw_ref[0:1, :] * g2 + cw_ref[1:2, :] * g1 + cw_ref[2:3, :] * ext + cb_ref[...]
        gl, dgl = _gelu_and_grad(gc)
        d_gc = dext * vext * dgl
        d_gate = (cw_ref[2:3, :] * d_gc + cw_ref[1:2, :] * _shift_up(d_gc, 1)
                  + cw_ref[0:1, :] * _shift_up(d_gc, 2))
        dh2_ref[0] = d_gate[main_rows, :].astype(BF16)
        dh2_ref[1] = (da * gl[main_rows, :]).astype(BF16)
        dm = d_gc[main_rows, :]
        s0 = jnp.sum(dm * g2[main_rows, :], axis=0, keepdims=True)
        s1 = jnp.sum(dm * g1[main_rows, :], axis=0, keepdims=True)
        s2 = jnp.sum(dm * ext[main_rows, :], axis=0, keepdims=True)
        rowid = lax.broadcasted_iota(jnp.int32, (8, FF_TILE), 0)
        dcw_ref[...] += jnp.where(rowid == 0, s0, jnp.where(rowid == 1, s1,
                                                            jnp.where(rowid == 2, s2, 0.0)))
        dcb_ref[...] += _colsum8(dm)

    def prev8(off):
        return pl.BlockSpec((HALO, FF_TILE), lambda j, i: (jnp.maximum(i * nth - 1, 0), j + off))

    def next8(off):
        return pl.BlockSpec((HALO, FF_TILE), lambda j, i: (jnp.minimum((i + 1) * nth, last_halo), j + off))

    def main(off):
        return pl.BlockSpec((tm, FF_TILE), lambda j, i: (i, j + off))

    acc = pl.BlockSpec((8, FF_TILE), lambda j, i: (0, j))
    return _pc(body, name="ffn_act_bwd",
               out_shape=(jax.ShapeDtypeStruct((2, t, D_FF), BF16),
                          jax.ShapeDtypeStruct((8, D_FF), F32),
                          jax.ShapeDtypeStruct((8, D_FF), F32)),
               grid=(FF_NJ, ni),
               in_specs=[main(0), prev8(0), next8(0), main(FF_NJ), next8(FF_NJ),
                         pl.BlockSpec((tm, FF_TILE), lambda j, i: (i, j)),
                         next8(0),
                         pl.BlockSpec((3, FF_TILE), lambda j, i: (0, j)),
                         pl.BlockSpec((1, FF_TILE), lambda j, i: (0, j))],
               out_specs=(pl.BlockSpec((2, tm, FF_TILE), lambda j, i: (0, i, j)), acc, acc),
               sem=("parallel", "arbitrary"))(h2, h2, h2, h2, h2, dact, dact, convw, convb)


def _out_fwd_bwd(act, x1b, r1, p2, tgt, wd, wpg, wpp, g1, b1, g2, b2, tm):
    t = r1.shape[0]

    def body(act_ref, x1b_ref, r1_ref, p_ref, tgt_ref, wd_ref, wpg_ref, wpp_ref,
             g1_ref, b1_ref, g2_ref, b2_ref,
             dr2_ref, dpg_ref, dpp_ref, loss_ref, dg2_ref, db2_ref):
        i = pl.program_id(0)

        @pl.when(i == 0)
        def _():
            loss_ref[...] = jnp.zeros_like(loss_ref)
            dg2_ref[...] = jnp.zeros_like(dg2_ref)
            db2_ref[...] = jnp.zeros_like(db2_ref)

        ffn = _dot(act_ref[...], wd_ref[...])
        pg = _dot(x1b_ref[...], wpg_ref[...])
        pp = _dot(p_ref[...], wpp_ref[...])
        s = _sig(pg)
        xh1, _ = _ln_stats(r1_ref[...])
        x1 = xh1 * g1_ref[...] + b1_ref[...]
        r2 = ALPHA * x1 + ffn + s * pp
        xh2, rstd2 = _ln_stats(r2)
        g2v = g2_ref[...]
        diff = xh2 * g2v + b2_ref[...] - tgt_ref[...]
        part = jnp.sum(jnp.sum(diff * diff, axis=1, keepdims=True), axis=0, keepdims=True)
        loss_ref[...] += jnp.broadcast_to(part * (0.5 / D_MODEL), loss_ref.shape)
        dy = diff * (1.0 / D_MODEL)
        dg2_ref[...] += _colsum8(dy * xh2)
        db2_ref[...] += _colsum8(dy)
        dr2 = _ln_bwd(dy * g2v, xh2, rstd2)
        dr2_ref[...] = dr2
        dpg_ref[...] = (dr2 * pp * s * (1.0 - s)).astype(BF16)
        dpp_ref[...] = (dr2 * s).astype(BF16)

    tile = pl.BlockSpec((tm, D_MODEL), lambda i: (i, 0))
    vec = pl.BlockSpec((1, D_MODEL), lambda i: (0, 0))
    acc8 = pl.BlockSpec((8, D_MODEL), lambda i: (0, 0))
    acc_shape = jax.ShapeDtypeStruct((8, D_MODEL), F32)
    return _pc(body, name="out_fwd_bwd",
               out_shape=(jax.ShapeDtypeStruct((t, D_MODEL), F32),
                          jax.ShapeDtypeStruct((t, D_MODEL), BF16),
                          jax.ShapeDtypeStruct((t, D_MODEL), BF16),
                          acc_shape, acc_shape, acc_shape),
               grid=(t // tm,),
               in_specs=[pl.BlockSpec((tm, D_FF), lambda i: (i, 0)), tile, tile,
                         pl.BlockSpec((tm, PLE_DIM), lambda i: (i, 0)), tile,
                         pl.BlockSpec((D_FF, D_MODEL), lambda i: (0, 0)),
                         pl.BlockSpec((D_MODEL, D_MODEL), lambda i: (0, 0)),
                         pl.BlockSpec((PLE_DIM, D_MODEL), lambda i: (0, 0)),
                         vec, vec, vec, vec],
               out_specs=(tile, tile, tile, acc8, acc8, acc8),
               sem=("arbitrary",))(act, x1b, r1, p2, tgt, wd, wpg, wpp, g1, b1, g2, b2)


def _ffn_in_bwd(dh2, wup_st, dpg, wpg, dr2, r1, g1, tm):
    t = r1.shape[0]
    ni = t // tm

    def body(dh2_ref, wup_ref, dpg_ref, wpg_ref, dr2_ref, r1_ref, g1_ref,
             dr1_ref, dg1_ref, db1_ref, acc):
        i = pl.program_id(0)
        j = pl.program_id(1)

        @pl.when((i == 0) & (j == 0))
        def _():
            dg1_ref[...] = jnp.zeros_like(dg1_ref)
            db1_ref[...] = jnp.zeros_like(db1_ref)

        @pl.when(j == 0)
        def _():
            acc[...] = _dot(dh2_ref[...], wup_ref[...], NT)

        @pl.when(j > 0)
        def _():
            acc[...] += _dot(dh2_ref[...], wup_ref[...], NT)

        @pl.when(j == N_CHIP - 1)
        def _():
            d_x1 = acc[...] + _dot(dpg_ref[...], wpg_ref[...], NT) + ALPHA * dr2_ref[...]
            xh, rstd = _ln_stats(r1_ref[...])
            dg1_ref[...] += _colsum8(d_x1 * xh)
            db1_ref[...] += _colsum8(d_x1)
            dr1_ref[...] = _ln_bwd(d_x1 * g1_ref[...], xh, rstd)

    tile = pl.BlockSpec((tm, D_MODEL), lambda i, j: (i, 0))
    acc8 = pl.BlockSpec((8, D_MODEL), lambda i, j: (0, 0))
    acc_shape = jax.ShapeDtypeStruct((8, D_MODEL), F32)
    return _pc(body, name="ffn_in_bwd",
               out_shape=(jax.ShapeDtypeStruct((t, D_MODEL), F32), acc_shape, acc_shape),
               grid=(ni, N_CHIP),
               in_specs=[pl.BlockSpec((None, tm, FF_TILE), lambda i, j: (j // FF_NJ, i, j % FF_NJ)),
                         pl.BlockSpec((None, D_MODEL, FF_TILE), lambda i, j: (j, 0, 0)),
                         tile, pl.BlockSpec((D_MODEL, D_MODEL), lambda i, j: (0, 0)),
                         tile, tile, pl.BlockSpec((1, D_MODEL), lambda i, j: (0, 0))],
               out_specs=(tile, acc8, acc8),
               scratch=[pltpu.VMEM((tm, D_MODEL), F32)],
               sem=("arbitrary", "arbitrary"))(dh2, wup_st, dpg, wpg, dr2, r1, g1)


ANY = pl.BlockSpec(memory_space=pl.ANY)


def _chip_peers():
    x, y, c = lax.axis_index("x"), lax.axis_index("y"), lax.axis_index("c")
    return x, y, c, [(1 - x, y), (x, 1 - y), (1 - x, 1 - y)]


def _gather_comm(halved, whole=()):
    n, nw = len(halved), len(whole)

    def copies(ins, outs, sems):
        ici_send, ici_recv, d2d_send, d2d_recv, own_send, own_recv = sems
        x, y, c, peers = _chip_peers()
        me = 2 * x + y
        sibling = (x, y, 1 - c)
        own, ici, ici_wait, fwd, fwd_wait = [], [], [], [], []
        for ti in range(n + nw):
            src, dst = ins[ti], outs[ti]
            own.append(pltpu.make_async_remote_copy(
                src_ref=src, dst_ref=dst.at[me], send_sem=own_send.at[ti], recv_sem=own_recv.at[ti],
                device_id=sibling, device_id_type=MESH))
            for k, (px, py) in enumerate(peers):
                pk = 2 * px + py
                sem = dict(send_sem=ici_send.at[ti * 3 + k], recv_sem=ici_recv.at[ti * 3 + k],
                           device_id=(px, py, c), device_id_type=MESH)
                if ti < n:
                    ici.append(pltpu.make_async_remote_copy(src_ref=src.at[c], dst_ref=dst.at[me, c], **sem))
                    ici_wait.append(pltpu.make_async_remote_copy(src_ref=src.at[c], dst_ref=dst.at[pk, c], **sem))
                    dsem = dict(send_sem=d2d_send.at[ti * 3 + k], recv_sem=d2d_recv.at[ti * 3 + k],
                                device_id=sibling, device_id_type=MESH)
                    fwd.append(pltpu.make_async_remote_copy(src_ref=dst.at[pk, c], dst_ref=dst.at[pk, c], **dsem))
                    fwd_wait.append(pltpu.make_async_remote_copy(
                        src_ref=dst.at[pk, 1 - c], dst_ref=dst.at[pk, 1 - c], **dsem))
                else:
                    ici.append(pltpu.make_async_remote_copy(src_ref=src, dst_ref=dst.at[me], **sem))
                    ici_wait.append(pltpu.make_async_remote_copy(src_ref=src, dst_ref=dst.at[pk], **sem))
        return own, ici, ici_wait, fwd, fwd_wait

    def start(ins, outs, sems):
        own, ici, _, _, _ = copies(ins, outs, sems)
        for cp in own + ici:
            cp.start()

    def finish(ins, outs, sems):
        own, ici, ici_wait, fwd, fwd_wait = copies(ins, outs, sems)
        for i, cp in enumerate(ici_wait):
            cp.wait_recv()
            if i < len(fwd):
                fwd[i].start()
        for cp in fwd_wait + own:
            cp.wait_recv()
        for cp in own + ici + fwd:
            cp.wait_send()

    srcs = list(halved) + list(whole)
    return _Comm(srcs, [jax.ShapeDtypeStruct((N_CHIP,) + s.shape, s.dtype) for s in srcs],
                 [pltpu.SemaphoreType.DMA((3 * (n + nw),)), pltpu.SemaphoreType.DMA((3 * (n + nw),)),
                  pltpu.SemaphoreType.DMA((max(3 * n, 1),)), pltpu.SemaphoreType.DMA((max(3 * n, 1),)),
                  pltpu.SemaphoreType.DMA((n + nw,)), pltpu.SemaphoreType.DMA((n + nw,))],
                 start, finish)


def _sibling_exchange_comm(grads):
    n = len(grads)

    def copies(ins, outs, sems):
        send_sems, recv_sems = sems
        x, y, c = lax.axis_index("x"), lax.axis_index("y"), lax.axis_index("c")
        res = []
        for ti in range(n):
            half = ins[ti].shape[1] // 2
            res.append(pltpu.make_async_remote_copy(
                src_ref=ins[ti].at[:, pl.ds(pl.multiple_of((1 - c) * half, 8), half), :],
                dst_ref=outs[ti],
                send_sem=send_sems.at[ti], recv_sem=recv_sems.at[ti],
                device_id=(x, y, 1 - c), device_id_type=MESH))
        return res

    def start(ins, outs, sems):
        for cp in copies(ins, outs, sems):
            cp.start()

    def finish(ins, outs, sems):
        for cp in copies(ins, outs, sems):
            cp.wait()

    return _Comm(grads, [jax.ShapeDtypeStruct((N_CHIP, g.shape[1] // 2, g.shape[2]), g.dtype) for g in grads],
                 [pltpu.SemaphoreType.DMA((n,)), pltpu.SemaphoreType.DMA((n,))], start, finish)


def _rs_add_halves(name, grad, recv, core):
    _, r, cdim = grad.shape
    half = r // 2
    tr = _row_tile(half, cdim, mult=16)
    nr = half // tr

    def body(c_ref, g_ref, r_ref, o_ref):
        o_ref[...] = (g_ref[...] + r_ref[...]).astype(BF16)

    return _pc(body, name=name, out_shape=jax.ShapeDtypeStruct((N_CHIP, half, cdim), BF16),
               grid=(N_CHIP, nr), nsp=1,
               in_specs=[pl.BlockSpec((None, tr, cdim), lambda j, i, c_ref: (j, c_ref[0] * nr + i, 0)),
                         pl.BlockSpec((None, tr, cdim), lambda j, i, c_ref: (j, i, 0))],
               out_specs=pl.BlockSpec((None, tr, cdim), lambda j, i, c_ref: (j, i, 0)),
               sem=("parallel", "parallel"))(core, grad, recv)


def _chip_exchange_comm(parts):
    n = len(parts)

    def copies(ins, outs, sems):
        send_sems, recv_sems = sems
        x, y, c, peers = _chip_peers()
        return [pltpu.make_async_remote_copy(
            src_ref=ins[ti].at[2 * px + py], dst_ref=outs[ti].at[k],
            send_sem=send_sems.at[ti * 3 + k], recv_sem=recv_sems.at[ti * 3 + k],
            device_id=(px, py, c), device_id_type=MESH)
            for ti in range(n) for k, (px, py) in enumerate(peers)]

    def start(ins, outs, sems):
        for cp in copies(ins, outs, sems):
            cp.start()

    def finish(ins, outs, sems):
        for cp in copies(ins, outs, sems):
            cp.wait()

    return _Comm(parts, [jax.ShapeDtypeStruct((3,) + p.shape[1:], p.dtype) for p in parts],
                 [pltpu.SemaphoreType.DMA((3 * n,)), pltpu.SemaphoreType.DMA((3 * n,))], start, finish)


def _rs_sum_chips(name, part, recv, chip):
    _, half, cdim = recv.shape
    tr = _row_tile(half, cdim, mult=16)

    def body(chip_ref, p_ref, r_ref, o_ref):
        o_ref[...] = ((p_ref[...].astype(F32) + r_ref[0].astype(F32)) + r_ref[1].astype(F32)
                      ) + r_ref[2].astype(F32)

    return _pc(body, name=name, out_shape=jax.ShapeDtypeStruct((half, cdim), F32),
               grid=(half // tr,), nsp=1,
               in_specs=[pl.BlockSpec((None, tr, cdim), lambda i, chip_ref: (chip_ref[0], i, 0)),
                         pl.BlockSpec((3, tr, cdim), lambda i, chip_ref: (0, i, 0))],
               out_specs=pl.BlockSpec((tr, cdim), lambda i, chip_ref: (i, 0)),
               sem=("parallel",))(chip, part, recv)


def _rs_send_halves(halves):
    n = len(halves)

    def body(*refs):
        ins, outs = refs[:n], refs[n:2 * n]
        send_sems, recv_sems = refs[2 * n:]
        x, y, c = lax.axis_index("x"), lax.axis_index("y"), lax.axis_index("c")
        sends = []
        for ti in range(n):
            cp = pltpu.make_async_remote_copy(
                src_ref=ins[ti], dst_ref=outs[ti],
                send_sem=send_sems.at[ti], recv_sem=recv_sems.at[ti],
                device_id=(x, y, 1 - c), device_id_type=MESH)
            cp.start()
            sends.append(cp)
        for cp in sends:
            cp.wait()

    return _pc(body, name="rs_send_halves",
               out_shape=tuple(jax.ShapeDtypeStruct(hv.shape, hv.dtype) for hv in halves),
               in_specs=[ANY] * n, out_specs=tuple([ANY] * n),
               scratch=[pltpu.SemaphoreType.DMA((n,)), pltpu.SemaphoreType.DMA((n,))])(*halves)


def _adamw_rows(name, mine, theirs, w, m, v, core):
    half, cdim = mine.shape
    tr = _row_tile(half, cdim, budget=1 << 19)
    nrh = half // tr

    def body(c_ref, mine_ref, theirs_ref, w_ref, m_ref, v_ref, g_ref, d_ref, m2_ref, v2_ref):
        is_mine = (pl.program_id(0) // nrh) == c_ref[0]
        g = jnp.where(is_mine, mine_ref[...], theirs_ref[...])
        d, m2, v2 = _adamw(w_ref[...], g, m_ref[...], v_ref[...])
        g_ref[...] = g
        d_ref[...] = d
        m2_ref[...] = m2
        v2_ref[...] = v2

    htile = pl.BlockSpec((tr, cdim), lambda i, c_ref: (i % nrh, 0))
    tile = pl.BlockSpec((tr, cdim), lambda i, c_ref: (i, 0))
    shp = jax.ShapeDtypeStruct((2 * half, cdim), F32)
    return _pc(body, name=name, out_shape=(shp, shp, shp, shp), grid=(2 * nrh,), nsp=1,
               in_specs=[htile, htile, tile, tile, tile], out_specs=(tile, tile, tile, tile),
               sem=("parallel",))(core, mine, theirs, w, m, v)


def _adamw_whole(name, g, w, m, v):
    def body(g_ref, w_ref, m_ref, v_ref, d_ref, m2_ref, v2_ref):
        d, m2, v2 = _adamw(w_ref[...], g_ref[...], m_ref[...], v_ref[...])
        d_ref[...] = d
        m2_ref[...] = m2
        v2_ref[...] = v2

    shp = jax.ShapeDtypeStruct(g.shape, F32)
    return _pc(body, name=name, out_shape=(shp, shp, shp))(g, w, m, v)


SMALL_LAYOUT = (
    ("sgu_w_s", 1024, 1, 0),
    ("sgu_b_s", 8, 1, 1024),
    ("sgu_norm_g", 1, 0, 0),
    ("sgu_norm_b", 1, 0, 1),
    ("hgrn_norm_g", 1, 0, 3),
    ("ln1_g", 1, 0, 4),
    ("ln1_b", 1, 0, 5),
    ("ffn_conv_b", 1, 2, 3),
    ("ln2_g", 1, 0, 6),
    ("ln2_b", 1, 0, 7),
)
LB_ROW = 2
LOSS_ROW = 8
PACK_SHAPES = ((16, D_MODEL), (N_GROUP * 128 + 8, 128), (8, D_FF))


def _small_allreduce_adamw(rows1024, dws, dbs, dcw, dcb, logits, m_logits, v_logits,
                           small_w, small_m, small_v):
    ns = len(SMALL_LAYOUT)
    nr = len(rows1024)
    nb = len(PACK_SHAPES)

    def body(*refs):
        row_refs = refs[:nr]
        dws_ref, dbs_ref, dcw_ref, dcb_ref, lg_ref, mlg_ref, vlg_ref = refs[nr:nr + 7]
        pos = nr + 7
        w_refs = refs[pos:pos + ns]
        m_refs = refs[pos + ns:pos + 2 * ns]
        v_refs = refs[pos + 2 * ns:pos + 3 * ns]
        pos += 3 * ns
        loss_ref, dcw_out = refs[pos:pos + 2]
        lg_outs = refs[pos + 2:pos + 6]
        pos += 6
        outs = refs[pos:pos + 4 * ns]
        pos += 4 * ns
        pack = refs[pos:pos + nb]
        sib = refs[pos + nb:pos + 2 * nb]
        gath = refs[pos + 2 * nb:pos + 3 * nb]
        d2d_send, d2d_recv, ici_send, ici_recv = refs[pos + 3 * nb:]

        x, y, c, peers = _chip_peers()
        me = 2 * x + y
        sibling = (x, y, 1 - c)

        pack[0][...] = jnp.zeros(PACK_SHAPES[0], F32)
        for k in range(nr):
            pack[0][k:k + 1, :] = row_refs[k][0:1, :]
        pack[1][0:N_GROUP * 128, :] = dws_ref[...]
        pack[1][N_GROUP * 128:, :] = dbs_ref[...]
        pack[2][...] = jnp.zeros(PACK_SHAPES[2], F32)
        pack[2][0:3, :] = dcw_ref[0:3, :]
        pack[2][3:4, :] = dcb_ref[0:1, :]

        d2d = [pltpu.make_async_remote_copy(
            src_ref=pack[b], dst_ref=sib[b], send_sem=d2d_send.at[b], recv_sem=d2d_recv.at[b],
            device_id=sibling, device_id_type=MESH) for b in range(nb)]
        for cp in d2d:
            cp.start()
        for cp in d2d:
            cp.wait()
        for b in range(nb):
            gath[b][me] = pack[b][...] + sib[b][...]

        ici, ici_wait = [], []
        for b in range(nb):
            for k, (px, py) in enumerate(peers):
                sem = dict(send_sem=ici_send.at[b * 3 + k], recv_sem=ici_recv.at[b * 3 + k],
                           device_id=(px, py, c), device_id_type=MESH)
                ici.append(pltpu.make_async_remote_copy(src_ref=gath[b].at[me], dst_ref=gath[b].at[me], **sem))
                ici_wait.append(pltpu.make_async_remote_copy(
                    src_ref=gath[b].at[me], dst_ref=gath[b].at[2 * px + py], **sem))
        for cp in ici:
            cp.start()
        for cp in ici_wait:
            cp.wait_recv()
        for cp in ici:
            cp.wait_send()

        tot = pack
        for b in range(nb):
            tot[b][...] = ((gath[b][0] + gath[b][1]) + gath[b][2]) + gath[b][3]

        loss_ref[...] = tot[0][LOSS_ROW:LOSS_ROW + 1, :]
        dcw_out[...] = tot[2][...]
        lb = _sig(lg_ref[0:1, :] - lg_ref[1:2, :])
        d0 = tot[0][LB_ROW:LB_ROW + 1, :] * lb * (1.0 - lb)
        rowid = lax.broadcasted_iota(jnp.int32, (2, D_MODEL), 0)
        g_lg = jnp.where(rowid == 0, d0, -d0)
        dl, ml, vl = _adamw(lg_ref[...], g_lg, mlg_ref[...], vlg_ref[...])
        lg_outs[0][...] = g_lg
        lg_outs[1][...] = dl
        lg_outs[2][...] = ml
        lg_outs[3][...] = vl
        for si, (_, rows, b, r0) in enumerate(SMALL_LAYOUT):
            g = tot[b][r0:r0 + rows, :]
            dl, ml, vl = _adamw(w_refs[si][...], g, m_refs[si][...], v_refs[si][...])
            outs[4 * si][...] = g
            outs[4 * si + 1][...] = dl
            outs[4 * si + 2][...] = ml
            outs[4 * si + 3][...] = vl

    shapes = [jax.ShapeDtypeStruct((1, D_MODEL), F32), jax.ShapeDtypeStruct((8, D_FF), F32)]
    shapes += [jax.ShapeDtypeStruct((2, D_MODEL), F32)] * 4
    for w in small_w:
        shapes += [jax.ShapeDtypeStruct(w.shape, F32)] * 4
    scratch = [pltpu.VMEM(shp, F32) for shp in PACK_SHAPES]
    scratch += [pltpu.VMEM(shp, F32) for shp in PACK_SHAPES]
    scratch += [pltpu.VMEM((N_CHIP,) + shp, F32) for shp in PACK_SHAPES]
    scratch += [pltpu.SemaphoreType.DMA((nb,)), pltpu.SemaphoreType.DMA((nb,)),
                pltpu.SemaphoreType.DMA((3 * nb,)), pltpu.SemaphoreType.DMA((3 * nb,))]
    vm = pl.BlockSpec(memory_space=pltpu.VMEM)
    n_in = nr + 7 + 3 * ns
    res = _pc(body, name="small_allreduce_adamw", out_shape=tuple(shapes),
              in_specs=[vm] * n_in, out_specs=tuple([vm] * len(shapes)),
              scratch=scratch)(*rows1024, dws, dbs, dcw, dcb, logits, m_logits, v_logits,
                               *small_w, *small_m, *small_v)
    return res[0], res[1], res[2:6], res[6:]


def kernel(x, p, w_in, sgu_w_s, sgu_b_s, sgu_norm_g, sgu_norm_b, hgrn_lb_logits, hgrn_norm_g, w_branch, w_out, ln1_g, ln1_b, ffn_w_up, ffn_conv_w, ffn_conv_b, ffn_w_down, ln2_g, ln2_b, ple_w_proj, ple_w_gate, loss_target, m_w_in, m_sgu_w_s, m_sgu_b_s, m_sgu_norm_g, m_sgu_norm_b, m_hgrn_lb_logits, m_hgrn_norm_g, m_w_branch, m_w_out, m_ln1_g, m_ln1_b, m_ffn_w_up, m_ffn_conv_w, m_ffn_conv_b, m_ffn_w_down, m_ln2_g, m_ln2_b, m_ple_w_proj, m_ple_w_gate, v_w_in, v_sgu_w_s, v_sgu_b_s, v_sgu_norm_g, v_sgu_norm_b, v_hgrn_lb_logits, v_hgrn_norm_g, v_w_branch, v_w_out, v_ln1_g, v_ln1_b, v_ffn_w_up, v_ffn_conv_w, v_ffn_conv_b, v_ffn_w_down, v_ln2_g, v_ln2_b, v_ple_w_proj, v_ple_w_gate):
    t = x.shape[1]
    x2 = x.reshape(t, D_MODEL)
    x2b = x2.astype(BF16)
    p2 = p.reshape(t, PLE_DIM)
    tgt = loss_target.reshape(t, D_MODEL)
    core = lax.axis_index("c").astype(jnp.int32).reshape(1)
    chip_id = (2 * lax.axis_index("x") + lax.axis_index("y")).astype(jnp.int32).reshape(1)

    big_w = [w_in[0], w_branch[0, 0], w_branch[0, 1], w_out[0], ffn_w_up[0], ffn_w_down[0],
             ple_w_proj[0], ple_w_gate[0]]
    big_m = [m_w_in[0], m_w_branch[0, 0], m_w_branch[0, 1], m_w_out[0], m_ffn_w_up[0],
             m_ffn_w_down[0], m_ple_w_proj[0], m_ple_w_gate[0]]
    big_v = [v_w_in[0], v_w_branch[0, 0], v_w_branch[0, 1], v_w_out[0], v_ffn_w_up[0],
             v_ffn_w_down[0], v_ple_w_proj[0], v_ple_w_gate[0]]
    def halves_of(i):
        w = big_w[i]
        return w.astype(BF16).reshape(2, w.shape[0] // 2, w.shape[1])

    def stacked(g, i):
        return g.reshape(N_CHIP, big_w[i].shape[0], big_w[i].shape[1])

    win_g, convw_g = _run_comm("gather_w_in", _gather_comm([halves_of(0)], [ffn_conv_w[0]]))
    win_st = stacked(win_g, 0)
    convw = jnp.transpose(convw_g, (1, 0, 2)).reshape(3, D_FF)

    cid = jnp.arange(SGU_BLOCK) // CHUNK
    maskf = (cid[:, None] >= cid[None, :]).astype(F32)
    ws_masked = sgu_w_s[0] * maskf[None]
    wm = ws_masked.astype(BF16)
    wmt = jnp.transpose(ws_masked, (0, 2, 1)).astype(BF16)
    bsb = jnp.broadcast_to(sgu_b_s[0][:, :, None], (N_GROUP, SGU_BLOCK, 128))

    h, (wup_g,) = _mm_nn_stacked("in_proj", x2b, win_st, 512, comm=_gather_comm([halves_of(4)]))
    wup_st = stacked(wup_g, 4)
    ya = _sgu_fwd(h, wm, bsb, sgu_norm_g, sgu_norm_b)
    rest = (1, 2, 3, 5, 6, 7)
    (yb, st_all), rest_g = _hgrn_fwd(h, hgrn_lb_logits, hgrn_norm_g,
                                      comm=_gather_comm([halves_of(i) for i in rest]))
    rest_g = [stacked(g, i) for g, i in zip(rest_g, rest)]
    wb0 = rest_g[0].reshape(D_MODEL, D_MODEL)
    wb1 = rest_g[1].reshape(D_MODEL, D_MODEL)
    wo = rest_g[2].reshape(D_MODEL, D_MODEL)
    wd = rest_g[3].reshape(D_FF, D_MODEL)
    wpp = jnp.transpose(rest_g[4], (1, 0, 2)).reshape(PLE_DIM, D_MODEL)
    wpg = rest_g[5].reshape(D_MODEL, D_MODEL)
    r1, a_br, b_br, m_bf, x1b = _mix_fwd(ya, yb, h, x2, wb0, wb1, wo, ln1_g, ln1_b, 256)
    h2 = _mm_nn_stacked("ffn_up", x1b, wup_st, 512)
    act = _ffn_act_fwd(h2, convw, ffn_conv_b, 256)
    dr2, dpg, dpp, loss_acc, dg2, db2 = _out_fwd_bwd(
        act, x1b, r1, p2, tgt, wd, wpg, wpp, ln1_g, ln1_b, ln2_g, ln2_b, 256)

    dact = _mm("ffn_down_bwd", dr2, wd, NT, (t // 512, FF_NJ, 1),
               pl.BlockSpec((512, D_MODEL), lambda i, j, k: (i, 0)),
               pl.BlockSpec((FF_TILE, D_MODEL), lambda i, j, k: (j, 0)),
               jax.ShapeDtypeStruct((t, D_FF), BF16),
               pl.BlockSpec((512, FF_TILE), lambda i, j, k: (i, j)))
    dh2, dcw, dcb = _ffn_act_bwd(h2, dact, convw, ffn_conv_b, 256)
    d_wd = _mm_tn("ffn_down_wgrad", act, dr2, FF_TILE, 512)
    d_wpg = _mm_tn("ple_gate_wgrad", x1b, dpg, 512, D_MODEL)
    d_wpp_st = _mm_tn("ple_proj_wgrad", p2, dpp, PLE_DIM, PLE_DIM, stacked=True)
    d_wup_st = _mm("ffn_up_wgrad", x1b, dh2, TN, (2, N_CHIP, 1),
                   pl.BlockSpec((t, 512), lambda i, j, k: (0, i)),
                   pl.BlockSpec((None, t, FF_TILE), lambda i, j, k: (j // FF_NJ, 0, j % FF_NJ)),
                   jax.ShapeDtypeStruct((N_CHIP, D_MODEL, FF_TILE), F32),
                   pl.BlockSpec((None, 512, FF_TILE), lambda i, j, k: (j, i, 0)))
    dr1, dg1, db1 = _ffn_in_bwd(dh2, wup_st, dpg, wpg, dr2, r1, ln1_g, 512)
    da_bf, db_bf, dh3, dya, dyb = _mix_bwd(dr1, h, a_br, b_br, wo, wb0, wb1, 256)
    d_wo = _mm_tn("out_proj_wgrad", m_bf, dr1, 512, 512)
    d_wb0 = _mm_tn("branch0_wgrad", ya, da_bf, 512, D_MODEL)
    d_wb1 = _mm_tn("branch1_wgrad", yb, db_bf, 512, D_MODEL)
    grads_1 = [d_wb0.reshape(4, 256, D_MODEL), d_wb1.reshape(4, 256, D_MODEL),
               d_wo.reshape(4, 256, D_MODEL), d_wup_st, d_wd.reshape(4, D_FF // 4, D_MODEL),
               d_wpp_st, d_wpg.reshape(4, 256, D_MODEL)]
    (dh0, dws, dbs, dgv, dbv), recv_a1 = _sgu_bwd(h, dya, wm, wmt, bsb, sgu_norm_g, sgu_norm_b, maskf,
                                                  comm=_sibling_exchange_comm(grads_1))
    parts_1 = [_rs_add_halves("rs_add_halves%d" % (i + 1), g, r, core)
               for i, (g, r) in enumerate(zip(grads_1, recv_a1))]
    (dh1, dh2h, dlb, dgn), recv_b1 = _hgrn_bwd(h, dyb, st_all, hgrn_lb_logits, hgrn_norm_g,
                                                comm=_chip_exchange_comm(parts_1))
    dh_parts = [dh0, dh1, dh2h, dh3]
    d_win = [_mm_tn("in_proj_wgrad%d" % j, x2b, dh_parts[j], 512, D_MODEL) for j in range(4)]

    grads_0 = [jnp.stack(d_win)]
    recv_a0 = _run_comm("rs_sibling_exchange0", _sibling_exchange_comm(grads_0))
    parts_0 = [_rs_add_halves("rs_add_halves0", grads_0[0], recv_a0[0], core)]
    gx, recv_b0 = _in_proj_xgrad(dh_parts, win_st, dr1, 512, comm=_chip_exchange_comm(parts_0))
    parts = parts_0 + parts_1
    recv_b = list(recv_b0) + list(recv_b1)
    halves = [_rs_sum_chips("rs_sum_chips%d" % i, pt, r, chip_id)
              for i, (pt, r) in enumerate(zip(parts, recv_b))]
    theirs = _rs_send_halves(halves)
    big_out = [_adamw_rows("adamw_big%d" % i, halves[i], theirs[i], big_w[i], big_m[i], big_v[i], core)
               for i in range(len(halves))]

    small_in = dict(sgu_w_s=(sgu_w_s, m_sgu_w_s, v_sgu_w_s), sgu_b_s=(sgu_b_s, m_sgu_b_s, v_sgu_b_s),
                    sgu_norm_g=(sgu_norm_g, m_sgu_norm_g, v_sgu_norm_g),
                    sgu_norm_b=(sgu_norm_b, m_sgu_norm_b, v_sgu_norm_b),
                    hgrn_norm_g=(hgrn_norm_g, m_hgrn_norm_g, v_hgrn_norm_g),
                    ln1_g=(ln1_g, m_ln1_g, v_ln1_g), ln1_b=(ln1_b, m_ln1_b, v_ln1_b),
                    ffn_conv_b=(ffn_conv_b, m_ffn_conv_b, v_ffn_conv_b),
                    ln2_g=(ln2_g, m_ln2_g, v_ln2_g), ln2_b=(ln2_b, m_ln2_b, v_ln2_b))

    def flat(name, arr):
        rows = dict((n, r) for n, r, _, _ in SMALL_LAYOUT)[name]
        return arr.reshape(rows, arr.size // rows)

    names = [n for n, _, _, _ in SMALL_LAYOUT]
    sw = [flat(n, small_in[n][0]) for n in names]
    sm = [flat(n, small_in[n][1]) for n in names]
    sv = [flat(n, small_in[n][2]) for n in names]
    loss_rows, dcw_tot, lg_out, small_out = _small_allreduce_adamw(
        [dgv, dbv, dlb, dgn, dg1, db1, dg2, db2, loss_acc], dws.reshape(N_GROUP * 128, 128), dbs, dcw, dcb,
        hgrn_lb_logits, m_hgrn_lb_logits, v_hgrn_lb_logits, sw, sm, sv)
    loss = loss_rows[0, 0]

    chip = 2 * lax.axis_index("x") + lax.axis_index("y")
    g_cw = lax.dynamic_slice(dcw_tot, (0, chip * (D_FF // 4)), (3, D_FF // 4))
    cw_out = _adamw_whole("adamw_conv_w", g_cw, ffn_conv_w[0], m_ffn_conv_w[0], v_ffn_conv_w[0])

    res = {}
    for si, n in enumerate(names):
        shp = small_in[n][0].shape
        res[n] = tuple(small_out[4 * si + k].reshape(shp) for k in range(4))
    res["hgrn_lb_logits"] = tuple(lg_out)
    res["ffn_conv_w"] = (g_cw[None],) + tuple(o[None] for o in cw_out)

    def big(i):
        return tuple(big_out[i])

    res["w_in"] = tuple(o[None] for o in big(0))
    res["w_branch"] = tuple(jnp.stack([o0, o1])[None] for o0, o1 in zip(big(1), big(2)))
    res["w_out"] = tuple(o[None] for o in big(3))
    res["ffn_w_up"] = tuple(o[None] for o in big(4))
    res["ffn_w_down"] = tuple(o[None] for o in big(5))
    res["ple_w_proj"] = tuple(o[None] for o in big(6))
    res["ple_w_gate"] = tuple(o[None] for o in big(7))

    order = ["w_in", "sgu_w_s", "sgu_b_s", "sgu_norm_g", "sgu_norm_b", "hgrn_lb_logits",
             "hgrn_norm_g", "w_branch", "w_out", "ln1_g", "ln1_b", "ffn_w_up", "ffn_conv_w",
             "ffn_conv_b", "ffn_w_down", "ln2_g", "ln2_b", "ple_w_proj", "ple_w_gate"]
    outs = [loss, gx.reshape(1, t, D_MODEL)]
    for k in range(4):
        outs += [res[n][k] for n in order]
    return tuple(outs)
```

```python
import functools

import jax
import jax.numpy as jnp
from jax import lax
from jax.experimental import pallas as pl
from jax.experimental.pallas import tpu as pltpu

F32 = jnp.float32
BF16 = jnp.bfloat16
HIGHEST = lax.Precision.HIGHEST
MESH = pl.DeviceIdType.MESH

D_MODEL = 1024
CHUNK = 64
SGU_BLOCK = 128
N_GROUP = 8
N_HEAD = 8
HEAD_DIM = 128
D_FF = 2816
PLE_DIM = 256
IN_COLS = 8192
LN_EPS = 1e-5
RMS_EPS = 1e-6
ALPHA = 2.0 ** 0.25
N_CHIP = 4
N_DEV = 8

ADAM_LR = 0.001
ADAM_B1 = 0.9
ADAM_B2 = 0.999
ADAM_EPS = 1e-08
ADAM_WD = 0.01
ADAM_STEP = 10

VMEM_LIMIT = 56 * 1024 * 1024

NN = (((1,), (0,)), ((), ()))
NT = (((1,), (1,)), ((), ()))
TN = (((0,), (0,)), ((), ()))


def _pc(body, *, name, out_shape, grid=None, in_specs=None, out_specs=None, scratch=(),
        sem=None, nsp=0, vmem=VMEM_LIMIT):
    params = dict(vmem_limit_bytes=vmem)
    if sem is not None:
        params["dimension_semantics"] = sem
    kw = dict(name=name, out_shape=out_shape, compiler_params=pltpu.CompilerParams(**params))
    if nsp:
        kw["grid_spec"] = pltpu.PrefetchScalarGridSpec(
            num_scalar_prefetch=nsp, grid=grid, in_specs=in_specs, out_specs=out_specs,
            scratch_shapes=list(scratch))
    else:
        if grid is not None:
            kw["grid"] = grid
        if in_specs is not None:
            kw["in_specs"] = in_specs
            kw["out_specs"] = out_specs
        kw["scratch_shapes"] = list(scratch)
    return pl.pallas_call(body, **kw)


def _dot(a, b, dims=NN):
    return lax.dot_general(a.astype(BF16), b.astype(BF16), dims, preferred_element_type=F32)


def _dot32(a, b, dims=NN):
    return lax.dot_general(a, b, dims, precision=HIGHEST, preferred_element_type=F32)


def _sig(x):
    return 1.0 / (1.0 + jnp.exp(-x))


_GC = 0.7978845608028654
_GA = 0.044715


def _gelu(x):
    return 0.5 * x * (1.0 + jnp.tanh(_GC * (x + _GA * x * x * x)))


def _gelu_and_grad(x):
    t = jnp.tanh(_GC * (x + _GA * x * x * x))
    g = 0.5 * x * (1.0 + t)
    dg = 0.5 * (1.0 + t) + 0.5 * x * (1.0 - t * t) * _GC * (1.0 + 3.0 * _GA * x * x)
    return g, dg


def _ln_stats(r):
    mu = jnp.mean(r, axis=-1, keepdims=True)
    xc = r - mu
    var = jnp.mean(xc * xc, axis=-1, keepdims=True)
    rstd = lax.rsqrt(var + LN_EPS)
    return xc * rstd, rstd


def _ln_bwd(dxh, xh, rstd):
    m1 = jnp.mean(dxh, axis=-1, keepdims=True)
    m2 = jnp.mean(dxh * xh, axis=-1, keepdims=True)
    return rstd * (dxh - m1 - xh * m2)


def _colsum8(v):
    return jnp.broadcast_to(jnp.sum(v, axis=0, keepdims=True), (8, v.shape[1]))


def _adamw(w, g, m, v):
    m2 = ADAM_B1 * m + (1.0 - ADAM_B1) * g
    v2 = ADAM_B2 * v + (1.0 - ADAM_B2) * (g * g)
    m_hat = m2 / (1.0 - ADAM_B1 ** ADAM_STEP)
    v_hat = v2 / (1.0 - ADAM_B2 ** ADAM_STEP)
    delta = -ADAM_LR * (m_hat / (jnp.sqrt(v_hat) + ADAM_EPS) + ADAM_WD * w)
    return delta, m2, v2


def _row_tile(rows, cols, itemsize=4, budget=1 << 20, mult=8):
    best = mult
    for tr in range(mult, rows + 1, mult):
        if rows % tr == 0 and tr * cols * itemsize <= budget:
            best = tr
    return best


def _mm(name, a, b, dims, grid, a_spec, b_spec, out_shape, o_spec, add=None, add_spec=None,
        add_scale=1.0, comm=None):
    nk = grid[2]
    has_add = add is not None
    out_dtype = out_shape.dtype

    def body(*refs):
        if has_add:
            a_ref, b_ref, add_ref, o_ref = refs[:4]
            rest = refs[4:]
        else:
            a_ref, b_ref, o_ref = refs[:3]
            add_ref = None
            rest = refs[3:]
        prod = _dot(a_ref[...], b_ref[...], dims)

        def finish(acc):
            if has_add:
                acc = acc + add_scale * add_ref[...]
            o_ref[...] = acc.astype(out_dtype)

        if nk == 1:
            finish(prod)
        else:
            acc_ref = rest[0]
            k = pl.program_id(2)

            @pl.when(k == 0)
            def _():
                acc_ref[...] = prod

            @pl.when(k > 0)
            def _():
                acc_ref[...] += prod

            @pl.when(k == nk - 1)
            def _():
                finish(acc_ref[...])

    in_specs = [a_spec, b_spec] + ([add_spec] if has_add else [])
    args = [a, b] + ([add] if has_add else [])
    scratch = []
    if nk > 1:
        blk = [d for d in o_spec.block_shape if d is not None]
        scratch = [pltpu.VMEM(tuple(blk), F32)]
    if comm is None:
        return _pc(body, name=name, out_shape=out_shape, grid=grid, in_specs=in_specs,
                   out_specs=o_spec, scratch=scratch,
                   sem=("parallel", "parallel", "arbitrary"))(*args)

    def first():
        return (pl.program_id(0) == 0) & (pl.program_id(1) == 0) & (pl.program_id(2) == 0)

    def last():
        return ((pl.program_id(0) == grid[0] - 1) & (pl.program_id(1) == grid[1] - 1)
                & (pl.program_id(2) == grid[2] - 1))

    res = _hosted_call(body, comm, first, last, name=name, out_shape=(out_shape,), grid=grid,
                       in_specs=in_specs, out_specs=(o_spec,), scratch=scratch,
                       sem=("arbitrary", "arbitrary", "arbitrary"), args=args)
    return res[0], res[1:]


class _Comm:
    def __init__(self, ins, out_shapes, sems, start, finish):
        self.ins, self.out_shapes, self.sems = list(ins), list(out_shapes), list(sems)
        self.start, self.finish = start, finish


def _hosted_call(body, comm, first, last, *, name, out_shape, grid, in_specs, out_specs, scratch, sem,
                 args):
    n_in, n_out, n_scr = len(in_specs), len(out_shape), len(scratch)
    nci, nco = len(comm.ins), len(comm.out_shapes)

    def wrapped(*refs):
        pos = n_in
        own_in, c_in = refs[:pos], refs[pos:pos + nci]
        pos += nci
        own_out, c_out = refs[pos:pos + n_out], refs[pos + n_out:pos + n_out + nco]
        pos += n_out + nco
        own_scr, c_sem = refs[pos:pos + n_scr], refs[pos + n_scr:]

        @pl.when(first())
        def _():
            comm.start(c_in, c_out, c_sem)

        body(*own_in, *own_out, *own_scr)

        @pl.when(last())
        def _():
            comm.finish(c_in, c_out, c_sem)

    return _pc(wrapped, name=name, out_shape=tuple(out_shape) + tuple(comm.out_shapes), grid=grid,
               in_specs=list(in_specs) + [ANY] * nci, out_specs=tuple(out_specs) + tuple([ANY] * nco),
               scratch=list(scratch) + comm.sems, sem=sem)(*args, *comm.ins)


def _grid1_call(body, comm, n, *, name, out_shape, in_specs, out_specs, scratch, args):
    if comm is None:
        return _pc(body, name=name, out_shape=out_shape, grid=(n,), in_specs=in_specs,
                   out_specs=out_specs, scratch=scratch, sem=("arbitrary",))(*args), ()
    res = _hosted_call(body, comm, lambda: pl.program_id(0) == 0, lambda: pl.program_id(0) == n - 1,
                       name=name, out_shape=out_shape, grid=(n,), in_specs=in_specs,
                       out_specs=out_specs, scratch=scratch, sem=("arbitrary",), args=args)
    return res[:len(out_shape)], res[len(out_shape):]


def _run_comm(name, comm):
    nci, nco = len(comm.ins), len(comm.out_shapes)

    def body(*refs):
        c_in, c_out, c_sem = refs[:nci], refs[nci:nci + nco], refs[nci + nco:]
        comm.start(c_in, c_out, c_sem)
        comm.finish(c_in, c_out, c_sem)

    return _pc(body, name=name, out_shape=tuple(comm.out_shapes), in_specs=[ANY] * nci,
               out_specs=tuple([ANY] * nco), scratch=comm.sems)(*comm.ins)


def _mm_nn_stacked(name, a, w_st, tm, comm=None):
    t, k = a.shape
    _, _, c = w_st.shape
    return _mm(name, a, w_st, NN, (t // tm, N_CHIP, 1),
               pl.BlockSpec((tm, k), lambda i, j, kk: (i, 0)),
               pl.BlockSpec((None, k, c), lambda i, j, kk: (j, 0, 0)),
               jax.ShapeDtypeStruct((t, N_CHIP * c), BF16),
               pl.BlockSpec((tm, c), lambda i, j, kk: (i, j)), comm=comm)


def _mm_tn(name, a, b, tm, tn, stacked=False):
    t, m = a.shape
    _, n = b.shape
    if stacked:
        assert tm == m
        out_shape = jax.ShapeDtypeStruct((n // tn, m, tn), F32)
        o_spec = pl.BlockSpec((None, tm, tn), lambda i, j, kk: (j, 0, 0))
    else:
        out_shape = jax.ShapeDtypeStruct((m, n), F32)
        o_spec = pl.BlockSpec((tm, tn), lambda i, j, kk: (i, j))
    return _mm(name, a, b, TN, (m // tm, n // tn, 1),
               pl.BlockSpec((t, tm), lambda i, j, kk: (0, i)),
               pl.BlockSpec((t, tn), lambda i, j, kk: (0, j)),
               out_shape, o_spec)


def _in_proj_xgrad(dh_parts, win_st, dr1, tm, comm=None):
    t = dr1.shape[0]
    ni = t // tm

    def body(a0, a1, a2, a3, b_ref, add_ref, o_ref, acc):
        j = pl.program_id(1)
        for jj, a_ref in enumerate((a0, a1, a2, a3)):
            @pl.when(j == jj)
            def _(jj=jj, a_ref=a_ref):
                prod = _dot(a_ref[...], b_ref[...], NT)
                if jj == 0:
                    acc[...] = prod + ALPHA * add_ref[...]
                elif jj < N_CHIP - 1:
                    acc[...] += prod
                else:
                    o_ref[...] = acc[...] + prod

    a_spec = pl.BlockSpec((tm, 2 * D_MODEL), lambda i, j: (i, 0))
    tile = pl.BlockSpec((tm, D_MODEL), lambda i, j: (i, 0))
    kw = dict(name="in_proj_xgrad", out_shape=(jax.ShapeDtypeStruct((t, D_MODEL), F32),),
              grid=(ni, N_CHIP),
              in_specs=[a_spec] * 4 + [pl.BlockSpec((None, D_MODEL, 2 * D_MODEL), lambda i, j: (j, 0, 0)),
                                       tile],
              out_specs=(tile,), scratch=[pltpu.VMEM((tm, D_MODEL), F32)],
              sem=("arbitrary", "arbitrary"))
    args = list(dh_parts) + [win_st, dr1]
    if comm is None:
        return _pc(body, **kw)(*args)[0], ()
    res = _hosted_call(body, comm,
                       lambda: (pl.program_id(0) == 0) & (pl.program_id(1) == 0),
                       lambda: (pl.program_id(0) == ni - 1) & (pl.program_id(1) == N_CHIP - 1),
                       args=args, **kw)
    return res[0], res[1:]


def _sgu_mixed(v, wm_ref, bsb_ref, gv, bv):
    gl, dgl = _gelu_and_grad(v)
    vh, rstd = _ln_stats(gl)
    vn = vh * gv + bv
    mixed = []
    for g in range(N_GROUP):
        sl = slice(g * 128, (g + 1) * 128)
        mixed.append(_dot(wm_ref[g], vn[:, sl]) + bsb_ref[g])
    return dgl, vh, rstd, vn, mixed


def _sgu_fwd(h, wm, bsb, gv, bv):
    t = h.shape[0]

    def body(u_ref, v_ref, wm_ref, bsb_ref, gv_ref, bv_ref, ya_ref):
        u = u_ref[...].astype(F32)
        _, _, _, _, mixed = _sgu_mixed(v_ref[...].astype(F32), wm_ref, bsb_ref, gv_ref[...], bv_ref[...])
        gu = _gelu(u)
        for g in range(N_GROUP):
            sl = slice(g * 128, (g + 1) * 128)
            ya_ref[:, sl] = (gu[:, sl] * mixed[g]).astype(BF16)

    full3 = pl.BlockSpec((N_GROUP, 128, 128), lambda i: (0, 0, 0))
    vec = pl.BlockSpec((1, D_MODEL), lambda i: (0, 0))
    return _pc(body, name="sgu_fwd", out_shape=jax.ShapeDtypeStruct((t, D_MODEL), BF16),
               grid=(t // SGU_BLOCK,),
               in_specs=[pl.BlockSpec((SGU_BLOCK, D_MODEL), lambda i: (i, 0)),
                         pl.BlockSpec((SGU_BLOCK, D_MODEL), lambda i: (i, 1)),
                         full3, full3, vec, vec],
               out_specs=pl.BlockSpec((SGU_BLOCK, D_MODEL), lambda i: (i, 0)),
               sem=("parallel",))(h, h, wm, bsb, gv, bv)


def _sgu_bwd(h, dya, wm, wmt, bsb, gv, bv, maskf, comm=None):
    t = h.shape[0]
    nb = t // SGU_BLOCK

    def body(u_ref, v_ref, dya_ref, wm_ref, wmt_ref, bsb_ref, gv_ref, bv_ref, mask_ref,
             dh_ref, dws_ref, dbs_ref, dgv_ref, dbv_ref, dmix_acc):
        i = pl.program_id(0)

        @pl.when(i == 0)
        def _():
            dws_ref[...] = jnp.zeros_like(dws_ref)
            dgv_ref[...] = jnp.zeros_like(dgv_ref)
            dbv_ref[...] = jnp.zeros_like(dbv_ref)
            dmix_acc[...] = jnp.zeros_like(dmix_acc)

        u = u_ref[...].astype(F32)
        gvv = gv_ref[...]
        dgl_v, vh, rstd, vn, mixed = _sgu_mixed(v_ref[...].astype(F32), wm_ref, bsb_ref, gvv, bv_ref[...])
        gu, dgl_u = _gelu_and_grad(u)
        dya_v = dya_ref[...]
        dvn_parts = []
        for g in range(N_GROUP):
            sl = slice(g * 128, (g + 1) * 128)
            d_y = dya_v[:, sl]
            dh_ref[:, sl] = (d_y * mixed[g] * dgl_u[:, sl]).astype(BF16)
            d_mixed = d_y * gu[:, sl]
            dmix_acc[g] += d_mixed
            dws_ref[g] += _dot(d_mixed, vn[:, sl], NT) * mask_ref[...]
            dvn_parts.append(_dot(wmt_ref[g], d_mixed))
        dvn = jnp.concatenate(dvn_parts, axis=1)
        dgv_ref[...] += _colsum8(dvn * vh)
        dbv_ref[...] += _colsum8(dvn)
        d_gl = _ln_bwd(dvn * gvv, vh, rstd)
        dh_ref[:, D_MODEL:] = (d_gl * dgl_v).astype(BF16)

        @pl.when(i == nb - 1)
        def _():
            rowid = lax.broadcasted_iota(jnp.int32, (8, 128), 0)
            ones = jnp.ones((8, 128), F32)
            acc = jnp.zeros((8, 128), F32)
            for g in range(N_GROUP):
                rs = _dot32(ones, dmix_acc[g], NT)
                acc = jnp.where(rowid == g, rs, acc)
            dbs_ref[...] = acc

    full3 = pl.BlockSpec((N_GROUP, 128, 128), lambda i: (0, 0, 0))
    vec = pl.BlockSpec((1, D_MODEL), lambda i: (0, 0))
    acc8 = pl.BlockSpec((8, D_MODEL), lambda i: (0, 0))
    return _grid1_call(
        body, comm, nb, name="sgu_bwd",
        out_shape=(jax.ShapeDtypeStruct((t, 2 * D_MODEL), BF16),
                   jax.ShapeDtypeStruct((N_GROUP, 128, 128), F32),
                   jax.ShapeDtypeStruct((8, 128), F32),
                   jax.ShapeDtypeStruct((8, D_MODEL), F32),
                   jax.ShapeDtypeStruct((8, D_MODEL), F32)),
        in_specs=[pl.BlockSpec((SGU_BLOCK, D_MODEL), lambda i: (i, 0)),
                  pl.BlockSpec((SGU_BLOCK, D_MODEL), lambda i: (i, 1)),
                  pl.BlockSpec((SGU_BLOCK, D_MODEL), lambda i: (i, 0)),
                  full3, full3, full3, vec, vec,
                  pl.BlockSpec((128, 128), lambda i: (0, 0))],
        out_specs=(pl.BlockSpec((SGU_BLOCK, 2 * D_MODEL), lambda i: (i, 0)),
                   full3, pl.BlockSpec((8, 128), lambda i: (0, 0)), acc8, acc8),
        scratch=[pltpu.VMEM((N_GROUP, 128, 128), F32)],
        args=(h, h, dya, wm, wmt, bsb, gv, bv, maskf))


def _tri_masks():
    row = lax.broadcasted_iota(jnp.int32, (CHUNK, CHUNK), 0)
    col = lax.broadcasted_iota(jnp.int32, (CHUNK, CHUNK), 1)
    return col <= row, col >= row


def _heads(v):
    return [v[:, hd * HEAD_DIM:(hd + 1) * HEAD_DIM] for hd in range(N_HEAD)]


def _tri_cumsum(tri_bf, v):
    hi = v.astype(BF16)
    r = v - hi.astype(F32)
    mid = r.astype(BF16)
    lo = (r - mid.astype(F32)).astype(BF16)
    return _dot(tri_bf, hi) + _dot(tri_bf, mid) + _dot(tri_bf, lo)


def _hgrn_chunk(q, fp, ii, lb, st_heads, causal):
    sg = _sig(fp)
    f = lb + (1.0 - lb) * sg
    k = 1.0 - f
    c = _tri_cumsum(causal.astype(BF16), jnp.log(f))
    ec = jnp.exp(c)
    en = jnp.exp(-c)
    sq = _sig(q)
    qt = q * sq * ec
    kt = k * en
    ecl = jnp.exp(c[CHUNK - 1:CHUNK, :])
    kk = kt * ecl
    qtb, ktb, iib, kkb = qt.astype(BF16), kt.astype(BF16), ii.astype(BF16), kk.astype(BF16)
    attn, o = [], []
    for hd, (qh, kh, ih) in enumerate(zip(_heads(qtb), _heads(ktb), _heads(iib))):
        a = jnp.where(causal, _dot(qh, kh, NT), 0.0).astype(BF16)
        attn.append(a)
        o.append(_dot(a, ih) + _dot(qh, st_heads[hd], NT))
    return dict(sg=sg, f=f, k=k, ec=ec, en=en, sq=sq, ecl=ecl, kk=kk, qtb=qtb, ktb=ktb, iib=iib,
                kkb=kkb, attn=attn, o=o)


def _rms_heads(o_heads):
    rinv = [lax.rsqrt(jnp.mean(o * o, axis=-1, keepdims=True) + RMS_EPS) for o in o_heads]
    return rinv, jnp.concatenate([o * r for o, r in zip(o_heads, rinv)], axis=1)


def _hgrn_fwd(h, logits, gn, comm=None):
    t = h.shape[0]
    nc = t // CHUNK

    def body(q_ref, f_ref, i_ref, og_ref, lg_ref, gn_ref, yb_ref, st_ref, state):
        ci = pl.program_id(0)

        @pl.when(ci == 0)
        def _():
            state[...] = jnp.zeros_like(state)

        causal, _ = _tri_masks()
        st = [state[hd] for hd in range(N_HEAD)]
        og = og_ref[...].astype(F32)
        lb = _sig(lg_ref[0:1, :] - lg_ref[1:2, :])
        r = _hgrn_chunk(q_ref[...].astype(F32), f_ref[...].astype(F32), i_ref[...].astype(F32), lb,
                        [s.astype(BF16) for s in st], causal)
        _, on = _rms_heads(r["o"])
        ecl = _heads(r["ecl"])
        new = [s * e + _dot(ih, kh, TN)
               for s, e, ih, kh in zip(st, ecl, _heads(r["iib"]), _heads(r["kkb"]))]
        yb_ref[...] = (on * gn_ref[...] * (og * _sig(og))).astype(BF16)
        for hd in range(N_HEAD):
            st_ref[0, hd] = st[hd]
            state[hd] = new[hd]

    def col(k):
        return pl.BlockSpec((CHUNK, D_MODEL), lambda ci: (ci, k))

    return _grid1_call(body, comm, nc, name="hgrn_fwd",
                       out_shape=(jax.ShapeDtypeStruct((t, D_MODEL), BF16),
                                  jax.ShapeDtypeStruct((nc, N_HEAD, HEAD_DIM, HEAD_DIM), F32)),
                       in_specs=[col(2), col(3), col(4), col(5),
                                 pl.BlockSpec((2, D_MODEL), lambda ci: (0, 0)),
                                 pl.BlockSpec((1, D_MODEL), lambda ci: (0, 0))],
                       out_specs=(pl.BlockSpec((CHUNK, D_MODEL), lambda ci: (ci, 0)),
                                  pl.BlockSpec((1, N_HEAD, HEAD_DIM, HEAD_DIM), lambda ci: (ci, 0, 0, 0))),
                       scratch=[pltpu.VMEM((N_HEAD, HEAD_DIM, HEAD_DIM), F32)],
                       args=(h, h, h, h, logits, gn))


def _hgrn_bwd(h, dyb, st_all, logits, gn, comm=None):
    t = h.shape[0]
    nc = t // CHUNK

    def body(q_ref, f_ref, i_ref, og_ref, dyb_ref, st_ref, lg_ref, gn_ref,
             dh1_ref, dh2_ref, dlb_ref, dgn_ref, dstate):
        ci = pl.program_id(0)

        @pl.when(ci == 0)
        def _():
            dstate[...] = jnp.zeros_like(dstate)
            dlb_ref[...] = jnp.zeros_like(dlb_ref)
            dgn_ref[...] = jnp.zeros_like(dgn_ref)

        causal, anti = _tri_masks()
        q, og = q_ref[...].astype(F32), og_ref[...].astype(F32)
        dy, gnv = dyb_ref[...], gn_ref[...]
        lb = _sig(lg_ref[0:1, :] - lg_ref[1:2, :])
        st = [st_ref[0, hd] for hd in range(N_HEAD)]
        dsn = [dstate[hd] for hd in range(N_HEAD)]
        stb = [s.astype(BF16) for s in st]
        dsnb = [s.astype(BF16) for s in dsn]
        r = _hgrn_chunk(q, f_ref[...].astype(F32), i_ref[...].astype(F32), lb, stb, causal)
        rinv, on = _rms_heads(r["o"])
        so = _sig(og)
        sil = og * so
        d_og = dy * on * gnv * (so * (1.0 + og * (1.0 - so)))
        d_on = dy * gnv * sil
        d_ob = jnp.concatenate(
            [ri * (dn - oh * jnp.mean(dn * oh, axis=-1, keepdims=True))
             for ri, dn, oh in zip(rinv, _heads(d_on), _heads(on))], axis=1).astype(BF16)
        d_i, d_qt, d_kt, d_kk, d_st, st_dsn = [], [], [], [], [], []
        ecl = _heads(r["ecl"])
        for hd, (dh, qh, kh, ih, kkh) in enumerate(zip(_heads(d_ob), _heads(r["qtb"]), _heads(r["ktb"]),
                                                       _heads(r["iib"]), _heads(r["kkb"]))):
            d_attn = jnp.where(causal, _dot(dh, ih, NT), 0.0).astype(BF16)
            d_i.append(_dot(r["attn"][hd], dh, TN) + _dot(kkh, dsnb[hd], NT))
            d_qt.append(_dot(d_attn, kh) + _dot(dh, stb[hd]))
            d_kt.append(_dot(d_attn, qh, TN))
            d_kk.append(_dot(ih, dsnb[hd]))
            d_st.append(_dot(dh, qh, TN) + dsn[hd] * ecl[hd])
            st_dsn.append(jnp.sum(st[hd] * dsn[hd], axis=0, keepdims=True))
        d_qt = jnp.concatenate(d_qt, axis=1)
        d_kt = jnp.concatenate(d_kt, axis=1)
        d_kk = jnp.concatenate(d_kk, axis=1)
        kk = r["kk"]
        d_cl = (r["ecl"] * jnp.concatenate(st_dsn, axis=1)
                + jnp.sum(kk * d_kk, axis=0, keepdims=True))
        d_k = (d_kk * r["ecl"] + d_kt) * r["en"]
        d_c = d_qt * r["qtb"].astype(F32) - d_kt * r["ktb"].astype(F32) - d_kk * kk
        rowid = lax.broadcasted_iota(jnp.int32, (CHUNK, D_MODEL), 0)
        d_c = d_c + jnp.where(rowid == CHUNK - 1, d_cl, 0.0)
        d_lf = _tri_cumsum(anti.astype(BF16), d_c)
        d_f = d_lf / r["f"] - d_k
        sg = r["sg"]
        sq = r["sq"]
        dgn_ref[...] += _colsum8(dy * on * sil)
        dlb_ref[...] += _colsum8(d_f * (1.0 - sg))
        dh1_ref[:, :D_MODEL] = (d_qt * r["ec"] * (sq * (1.0 + q * (1.0 - sq)))).astype(BF16)
        dh1_ref[:, D_MODEL:] = (d_f * (1.0 - lb) * sg * (1.0 - sg)).astype(BF16)
        dh2_ref[:, :D_MODEL] = jnp.concatenate(d_i, axis=1).astype(BF16)
        dh2_ref[:, D_MODEL:] = d_og.astype(BF16)
        for hd in range(N_HEAD):
            dstate[hd] = d_st[hd]

    def col(k):
        return pl.BlockSpec((CHUNK, D_MODEL), lambda ci: (nc - 1 - ci, k))

    acc8 = pl.BlockSpec((8, D_MODEL), lambda ci: (0, 0))
    pair = pl.BlockSpec((CHUNK, 2 * D_MODEL), lambda ci: (nc - 1 - ci, 0))
    return _grid1_call(body, comm, nc, name="hgrn_bwd",
                       out_shape=(jax.ShapeDtypeStruct((t, 2 * D_MODEL), BF16),
                                  jax.ShapeDtypeStruct((t, 2 * D_MODEL), BF16),
                                  jax.ShapeDtypeStruct((8, D_MODEL), F32),
                                  jax.ShapeDtypeStruct((8, D_MODEL), F32)),
                       in_specs=[col(2), col(3), col(4), col(5),
                                 pl.BlockSpec((CHUNK, D_MODEL), lambda ci: (nc - 1 - ci, 0)),
                                 pl.BlockSpec((1, N_HEAD, HEAD_DIM, HEAD_DIM),
                                              lambda ci: (nc - 1 - ci, 0, 0, 0)),
                                 pl.BlockSpec((2, D_MODEL), lambda ci: (0, 0)),
                                 pl.BlockSpec((1, D_MODEL), lambda ci: (0, 0))],
                       out_specs=(pair, pair, acc8, acc8),
                       scratch=[pltpu.VMEM((N_HEAD, HEAD_DIM, HEAD_DIM), F32)],
                       args=(h, h, h, h, dyb, st_all, logits, gn))


def _mix_fwd(ya, yb, h, x, wb0, wb1, wo, g1, b1, tm):
    t = x.shape[0]

    def body(ya_ref, yb_ref, ga_ref, gb_ref, x_ref, wb0_ref, wb1_ref, wo_ref, g1_ref, b1_ref,
             r1_ref, a_ref, b_ref, m_ref, x1_ref):
        a = _dot(ya_ref[...], wb0_ref[...])
        b = _dot(yb_ref[...], wb1_ref[...])
        m = _sig(ga_ref[...].astype(F32)) * a + _sig(gb_ref[...].astype(F32)) * b
        r1 = ALPHA * x_ref[...] + _dot(m, wo_ref[...])
        xh, _ = _ln_stats(r1)
        r1_ref[...] = r1
        a_ref[...] = a
        b_ref[...] = b
        m_ref[...] = m.astype(BF16)
        x1_ref[...] = (xh * g1_ref[...] + b1_ref[...]).astype(BF16)

    tile = pl.BlockSpec((tm, D_MODEL), lambda i: (i, 0))
    wsp = pl.BlockSpec((D_MODEL, D_MODEL), lambda i: (0, 0))
    vec = pl.BlockSpec((1, D_MODEL), lambda i: (0, 0))
    f32o = jax.ShapeDtypeStruct((t, D_MODEL), F32)
    bfo = jax.ShapeDtypeStruct((t, D_MODEL), BF16)
    return _pc(body, name="mix_fwd", out_shape=(f32o, f32o, f32o, bfo, bfo), grid=(t // tm,),
               in_specs=[tile, tile,
                         pl.BlockSpec((tm, D_MODEL), lambda i: (i, 6)),
                         pl.BlockSpec((tm, D_MODEL), lambda i: (i, 7)),
                         tile, wsp, wsp, wsp, vec, vec],
               out_specs=(tile, tile, tile, tile, tile),
               sem=("parallel",))(ya, yb, h, h, x, wb0, wb1, wo, g1, b1)


def _mix_bwd(dr1, h, a, b, wo, wb0, wb1, tm):
    t = dr1.shape[0]

    def body(dr1_ref, ga_ref, gb_ref, a_ref, b_ref, wo_ref, wb0_ref, wb1_ref,
             da_ref, db_ref, dh3_ref, dya_ref, dyb_ref):
        d_m = _dot(dr1_ref[...], wo_ref[...], NT)
        sa = _sig(ga_ref[...].astype(F32))
        sb = _sig(gb_ref[...].astype(F32))
        d_a = (d_m * sa).astype(BF16)
        d_b = (d_m * sb).astype(BF16)
        da_ref[...] = d_a
        db_ref[...] = d_b
        dh3_ref[:, :D_MODEL] = (d_m * a_ref[...] * sa * (1.0 - sa)).astype(BF16)
        dh3_ref[:, D_MODEL:] = (d_m * b_ref[...] * sb * (1.0 - sb)).astype(BF16)
        dya_ref[...] = _dot(d_a, wb0_ref[...], NT)
        dyb_ref[...] = _dot(d_b, wb1_ref[...], NT)

    tile = pl.BlockSpec((tm, D_MODEL), lambda i: (i, 0))
    wsp = pl.BlockSpec((D_MODEL, D_MODEL), lambda i: (0, 0))
    f32o = jax.ShapeDtypeStruct((t, D_MODEL), F32)
    bfo = jax.ShapeDtypeStruct((t, D_MODEL), BF16)
    return _pc(body, name="mix_bwd",
               out_shape=(bfo, bfo, jax.ShapeDtypeStruct((t, 2 * D_MODEL), BF16), f32o, f32o),
               grid=(t // tm,),
               in_specs=[tile,
                         pl.BlockSpec((tm, D_MODEL), lambda i: (i, 6)),
                         pl.BlockSpec((tm, D_MODEL), lambda i: (i, 7)),
                         tile, tile, wsp, wsp, wsp],
               out_specs=(tile, tile, pl.BlockSpec((tm, 2 * D_MODEL), lambda i: (i, 0)),
                          tile, tile),
               sem=("parallel",))(dr1, h, h, a, b, wo, wb0, wb1)


FF_TILE = 1408
FF_NJ = D_FF // FF_TILE


def _shift_down(v, k):
    return pltpu.roll(v, k, 0)


def _shift_up(v, k):
    return pltpu.roll(v, v.shape[0] - k, 0)


def _conv_gate(ext, cw_ref, cb_ref):
    return (cw_ref[0:1, :] * _shift_down(ext, 2) + cw_ref[1:2, :] * _shift_down(ext, 1)
            + cw_ref[2:3, :] * ext + cb_ref[...])


HALO = 16


def _ffn_act_fwd(h2, convw, convb, tm):
    t = h2.shape[0]
    nth = tm // HALO

    def body(g_ref, gp_ref, v_ref, cw_ref, cb_ref, act_ref):
        i = pl.program_id(1)
        prev = gp_ref[...].astype(F32) * (i > 0).astype(F32)
        ext = jnp.concatenate([prev, g_ref[...].astype(F32)], axis=0)
        gc = _conv_gate(ext, cw_ref, cb_ref)[HALO:, :]
        act_ref[...] = (_gelu(gc) * v_ref[...].astype(F32)).astype(BF16)

    return _pc(body, name="ffn_act_fwd", out_shape=jax.ShapeDtypeStruct((t, D_FF), BF16),
               grid=(FF_NJ, t // tm),
               in_specs=[pl.BlockSpec((tm, FF_TILE), lambda j, i: (i, j)),
                         pl.BlockSpec((HALO, FF_TILE), lambda j, i: (jnp.maximum(i * nth - 1, 0), j)),
                         pl.BlockSpec((tm, FF_TILE), lambda j, i: (i, j + FF_NJ)),
                         pl.BlockSpec((3, FF_TILE), lambda j, i: (0, j)),
                         pl.BlockSpec((1, FF_TILE), lambda j, i: (0, j))],
               out_specs=pl.BlockSpec((tm, FF_TILE), lambda j, i: (i, j)),
               sem=("parallel", "parallel"))(h2, h2, h2, convw, convb)


def _ffn_act_bwd(h2, dact, convw, convb, tm):
    t = h2.shape[0]
    nth = tm // HALO
    ni = t // tm
    last_halo = t // HALO - 1
    main_rows = slice(HALO, HALO + tm)

    def body(g_ref, gp_ref, gn_ref, v_ref, vn_ref, da_ref, dan_ref, cw_ref, cb_ref,
             dh2_ref, dcw_ref, dcb_ref):
        i = pl.program_id(1)

        @pl.when(i == 0)
        def _():
            dcw_ref[...] = jnp.zeros_like(dcw_ref)
            dcb_ref[...] = jnp.zeros_like(dcb_ref)

        zeros = jnp.zeros((HALO, FF_TILE), F32)
        da = da_ref[...].astype(F32)
        prev = gp_ref[...].astype(F32) * (i > 0).astype(F32)
        ext = jnp.concatenate([prev, g_ref[...].astype(F32), gn_ref[...].astype(F32)], axis=0)
        vext = jnp.concatenate([zeros, v_ref[...].astype(F32), vn_ref[...].astype(F32)], axis=0)
        dnext = dan_ref[...].astype(F32) * (i < ni - 1).astype(F32)
        dext = jnp.concatenate([zeros, da, dnext], axis=0)
        g2 = _shift_down(ext, 2)
        g1 = _shift_down(ext, 1)
        gc = cw_ref[0:1, :] * g2 + cw_ref[1:2, :] * g1 + cw_ref[2:3, :] * ext + cb_ref[...]
        gl, dgl = _gelu_and_grad(gc)
        d_gc = dext * vext * dgl
        d_gate = (cw_ref[2:3, :] * d_gc + cw_ref[1:2, :] * _shift_up(d_gc, 1)
                  + cw_ref[0:1, :] * _shift_up(d_gc, 2))
        dh2_ref[0] = d_gate[main_rows, :].astype(BF16)
        dh2_ref[1] = (da * gl[main_rows, :]).astype(BF16)
        dm = d_gc[main_rows, :]
        s0 = jnp.sum(dm * g2[main_rows, :], axis=0, keepdims=True)
        s1 = jnp.sum(dm * g1[main_rows, :], axis=0, keepdims=True)
        s2 = jnp.sum(dm * ext[main_rows, :], axis=0, keepdims=True)
        rowid = lax.broadcasted_iota(jnp.int32, (8, FF_TILE), 0)
        dcw_ref[...] += jnp.where(rowid == 0, s0, jnp.where(rowid == 1, s1,
                                                            jnp.where(rowid == 2, s2, 0.0)))
        dcb_ref[...] += _colsum8(dm)

    def prev8(off):
        return pl.BlockSpec((HALO, FF_TILE), lambda j, i: (jnp.maximum(i * nth - 1, 0), j + off))

    def next8(off):
        return pl.BlockSpec((HALO, FF_TILE), lambda j, i: (jnp.minimum((i + 1) * nth, last_halo), j + off))

    def main(off):
        return pl.BlockSpec((tm, FF_TILE), lambda j, i: (i, j + off))

    acc = pl.BlockSpec((8, FF_TILE), lambda j, i: (0, j))
    return _pc(body, name="ffn_act_bwd",
               out_shape=(jax.ShapeDtypeStruct((2, t, D_FF), BF16),
                          jax.ShapeDtypeStruct((8, D_FF), F32),
                          jax.ShapeDtypeStruct((8, D_FF), F32)),
               grid=(FF_NJ, ni),
               in_specs=[main(0), prev8(0), next8(0), main(FF_NJ), next8(FF_NJ),
                         pl.BlockSpec((tm, FF_TILE), lambda j, i: (i, j)),
                         next8(0),
                         pl.BlockSpec((3, FF_TILE), lambda j, i: (0, j)),
                         pl.BlockSpec((1, FF_TILE), lambda j, i: (0, j))],
               out_specs=(pl.BlockSpec((2, tm, FF_TILE), lambda j, i: (0, i, j)), acc, acc),
               sem=("parallel", "arbitrary"))(h2, h2, h2, h2, h2, dact, dact, convw, convb)


def _out_fwd_bwd(act, x1b, r1, p2, tgt, wd, wpg, wpp, g1, b1, g2, b2, tm):
    t = r1.shape[0]

    def body(act_ref, x1b_ref, r1_ref, p_ref, tgt_ref, wd_ref, wpg_ref, wpp_ref,
             g1_ref, b1_ref, g2_ref, b2_ref,
             dr2_ref, dpg_ref, dpp_ref, loss_ref, dg2_ref, db2_ref):
        i = pl.program_id(0)

        @pl.when(i == 0)
        def _():
            loss_ref[...] = jnp.zeros_like(loss_ref)
            dg2_ref[...] = jnp.zeros_like(dg2_ref)
            db2_ref[...] = jnp.zeros_like(db2_ref)

        ffn = _dot(act_ref[...], wd_ref[...])
        pg = _dot(x1b_ref[...], wpg_ref[...])
        pp = _dot(p_ref[...], wpp_ref[...])
        s = _sig(pg)
        xh1, _ = _ln_stats(r1_ref[...])
        x1 = xh1 * g1_ref[...] + b1_ref[...]
        r2 = ALPHA * x1 + ffn + s * pp
        xh2, rstd2 = _ln_stats(r2)
        g2v = g2_ref[...]
        diff = xh2 * g2v + b2_ref[...] - tgt_ref[...]
        part = jnp.sum(jnp.sum(diff * diff, axis=1, keepdims=True), axis=0, keepdims=True)
        loss_ref[...] += jnp.broadcast_to(part * (0.5 / D_MODEL), loss_ref.shape)
        dy = diff * (1.0 / D_MODEL)
        dg2_ref[...] += _colsum8(dy * xh2)
        db2_ref[...] += _colsum8(dy)
        dr2 = _ln_bwd(dy * g2v, xh2, rstd2)
        dr2_ref[...] = dr2
        dpg_ref[...] = (dr2 * pp * s * (1.0 - s)).astype(BF16)
        dpp_ref[...] = (dr2 * s).astype(BF16)

    tile = pl.BlockSpec((tm, D_MODEL), lambda i: (i, 0))
    vec = pl.BlockSpec((1, D_MODEL), lambda i: (0, 0))
    acc8 = pl.BlockSpec((8, D_MODEL), lambda i: (0, 0))
    acc_shape = jax.ShapeDtypeStruct((8, D_MODEL), F32)
    return _pc(body, name="out_fwd_bwd",
               out_shape=(jax.ShapeDtypeStruct((t, D_MODEL), F32),
                          jax.ShapeDtypeStruct((t, D_MODEL), BF16),
                          jax.ShapeDtypeStruct((t, D_MODEL), BF16),
                          acc_shape, acc_shape, acc_shape),
               grid=(t // tm,),
               in_specs=[pl.BlockSpec((tm, D_FF), lambda i: (i, 0)), tile, tile,
                         pl.BlockSpec((tm, PLE_DIM), lambda i: (i, 0)), tile,
                         pl.BlockSpec((D_FF, D_MODEL), lambda i: (0, 0)),
                         pl.BlockSpec((D_MODEL, D_MODEL), lambda i: (0, 0)),
                         pl.BlockSpec((PLE_DIM, D_MODEL), lambda i: (0, 0)),
                         vec, vec, vec, vec],
               out_specs=(tile, tile, tile, acc8, acc8, acc8),
               sem=("arbitrary",))(act, x1b, r1, p2, tgt, wd, wpg, wpp, g1, b1, g2, b2)


def _ffn_in_bwd(dh2, wup_st, dpg, wpg, dr2, r1, g1, tm):
    t = r1.shape[0]
    ni = t // tm

    def body(dh2_ref, wup_ref, dpg_ref, wpg_ref, dr2_ref, r1_ref, g1_ref,
             dr1_ref, dg1_ref, db1_ref, acc):
        i = pl.program_id(0)
        j = pl.program_id(1)

        @pl.when((i == 0) & (j == 0))
        def _():
            dg1_ref[...] = jnp.zeros_like(dg1_ref)
            db1_ref[...] = jnp.zeros_like(db1_ref)

        @pl.when(j == 0)
        def _():
            acc[...] = _dot(dh2_ref[...], wup_ref[...], NT)

        @pl.when(j > 0)
        def _():
            acc[...] += _dot(dh2_ref[...], wup_ref[...], NT)

        @pl.when(j == N_CHIP - 1)
        def _():
            d_x1 = acc[...] + _dot(dpg_ref[...], wpg_ref[...], NT) + ALPHA * dr2_ref[...]
            xh, rstd = _ln_stats(r1_ref[...])
            dg1_ref[...] += _colsum8(d_x1 * xh)
            db1_ref[...] += _colsum8(d_x1)
            dr1_ref[...] = _ln_bwd(d_x1 * g1_ref[...], xh, rstd)

    tile = pl.BlockSpec((tm, D_MODEL), lambda i, j: (i, 0))
    acc8 = pl.BlockSpec((8, D_MODEL), lambda i, j: (0, 0))
    acc_shape = jax.ShapeDtypeStruct((8, D_MODEL), F32)
    return _pc(body, name="ffn_in_bwd",
               out_shape=(jax.ShapeDtypeStruct((t, D_MODEL), F32), acc_shape, acc_shape),
               grid=(ni, N_CHIP),
               in_specs=[pl.BlockSpec((None, tm, FF_TILE), lambda i, j: (j // FF_NJ, i, j % FF_NJ)),
                         pl.BlockSpec((None, D_MODEL, FF_TILE), lambda i, j: (j, 0, 0)),
                         tile, pl.BlockSpec((D_MODEL, D_MODEL), lambda i, j: (0, 0)),
                         tile, tile, pl.BlockSpec((1, D_MODEL), lambda i, j: (0, 0))],
               out_specs=(tile, acc8, acc8),
               scratch=[pltpu.VMEM((tm, D_MODEL), F32)],
               sem=("arbitrary", "arbitrary"))(dh2, wup_st, dpg, wpg, dr2, r1, g1)


ANY = pl.BlockSpec(memory_space=pl.ANY)


def _chip_peers():
    x, y, c = lax.axis_index("x"), lax.axis_index("y"), lax.axis_index("c")
    return x, y, c, [(1 - x, y), (x, 1 - y), (1 - x, 1 - y)]


def _gather_comm(halved, whole=()):
    n, nw = len(halved), len(whole)

    def copies(ins, outs, sems):
        ici_send, ici_recv, d2d_send, d2d_recv, own_send, own_recv = sems
        x, y, c, peers = _chip_peers()
        me = 2 * x + y
        sibling = (x, y, 1 - c)
        own, ici, ici_wait, fwd, fwd_wait = [], [], [], [], []
        for ti in range(n + nw):
            src, dst = ins[ti], outs[ti]
            own.append(pltpu.make_async_remote_copy(
                src_ref=src, dst_ref=dst.at[me], send_sem=own_send.at[ti], recv_sem=own_recv.at[ti],
                device_id=sibling, device_id_type=MESH))
            for k, (px, py) in enumerate(peers):
                pk = 2 * px + py
                sem = dict(send_sem=ici_send.at[ti * 3 + k], recv_sem=ici_recv.at[ti * 3 + k],
                           device_id=(px, py, c), device_id_type=MESH)
                if ti < n:
                    ici.append(pltpu.make_async_remote_copy(src_ref=src.at[c], dst_ref=dst.at[me, c], **sem))
                    ici_wait.append(pltpu.make_async_remote_copy(src_ref=src.at[c], dst_ref=dst.at[pk, c], **sem))
                    dsem = dict(send_sem=d2d_send.at[ti * 3 + k], recv_sem=d2d_recv.at[ti * 3 + k],
                                device_id=sibling, device_id_type=MESH)
                    fwd.append(pltpu.make_async_remote_copy(src_ref=dst.at[pk, c], dst_ref=dst.at[pk, c], **dsem))
                    fwd_wait.append(pltpu.make_async_remote_copy(
                        src_ref=dst.at[pk, 1 - c], dst_ref=dst.at[pk, 1 - c], **dsem))
                else:
                    ici.append(pltpu.make_async_remote_copy(src_ref=src, dst_ref=dst.at[me], **sem))
                    ici_wait.append(pltpu.make_async_remote_copy(src_ref=src, dst_ref=dst.at[pk], **sem))
        return own, ici, ici_wait, fwd, fwd_wait

    def start(ins, outs, sems):
        own, ici, _, _, _ = copies(ins, outs, sems)
        for cp in own + ici:
            cp.start()

    def finish(ins, outs, sems):
        own, ici, ici_wait, fwd, fwd_wait = copies(ins, outs, sems)
        for i, cp in enumerate(ici_wait):
            cp.wait_recv()
            if i < len(fwd):
                fwd[i].start()
        for cp in fwd_wait + own:
            cp.wait_recv()
        for cp in own + ici + fwd:
            cp.wait_send()

    srcs = list(halved) + list(whole)
    return _Comm(srcs, [jax.ShapeDtypeStruct((N_CHIP,) + s.shape, s.dtype) for s in srcs],
                 [pltpu.SemaphoreType.DMA((3 * (n + nw),)), pltpu.SemaphoreType.DMA((3 * (n + nw),)),
                  pltpu.SemaphoreType.DMA((max(3 * n, 1),)), pltpu.SemaphoreType.DMA((max(3 * n, 1),)),
                  pltpu.SemaphoreType.DMA((n + nw,)), pltpu.SemaphoreType.DMA((n + nw,))],
                 start, finish)


def _sibling_exchange_comm(grads):
    n = len(grads)

    def copies(ins, outs, sems):
        send_sems, recv_sems = sems
        x, y, c = lax.axis_index("x"), lax.axis_index("y"), lax.axis_index("c")
        res = []
        for ti in range(n):
            half = ins[ti].shape[1] // 2
            res.append(pltpu.make_async_remote_copy(
                src_ref=ins[ti].at[:, pl.ds(pl.multiple_of((1 - c) * half, 8), half), :],
                dst_ref=outs[ti],
                send_sem=send_sems.at[ti], recv_sem=recv_sems.at[ti],
                device_id=(x, y, 1 - c), device_id_type=MESH))
        return res

    def start(ins, outs, sems):
        for cp in copies(ins, outs, sems):
            cp.start()

    def finish(ins, outs, sems):
        for cp in copies(ins, outs, sems):
            cp.wait()

    return _Comm(grads, [jax.ShapeDtypeStruct((N_CHIP, g.shape[1] // 2, g.shape[2]), g.dtype) for g in grads],
                 [pltpu.SemaphoreType.DMA((n,)), pltpu.SemaphoreType.DMA((n,))], start, finish)


def _in_proj_gathering(x2b, own, chip, comm, tm):
    t = x2b.shape[0]
    ni = t // tm
    half, cols = own.shape[1], own.shape[2]
    nci, nco = len(comm.ins), len(comm.out_shapes)

    def body(chip_ref, x_ref, own_ref, own_hbm, *rest):
        c_in = rest[:nci]
        h_ref, win_out = rest[nci:nci + 2]
        c_out = rest[nci + 2:nci + 2 + nco]
        w_scr, ici_send, ici_recv, d2d_send, d2d_recv, own_sems, ld_sems = rest[nci + 2 + nco:nci + 9 + nco]
        c_sem = rest[nci + 9 + nco:]
        s, i = pl.program_id(0), pl.program_id(1)
        x, y, c, peers = _chip_peers()
        me = 2 * x + y
        sibling = (x, y, 1 - c)

        def ici(k, slot):
            px, py = peers[k]
            return pltpu.make_async_remote_copy(
                src_ref=own_hbm.at[c], dst_ref=win_out.at[slot, c],
                send_sem=ici_send.at[k], recv_sem=ici_recv.at[k],
                device_id=(px, py, c), device_id_type=MESH)

        def forward(k, core):
            pk = 2 * peers[k][0] + peers[k][1]
            return pltpu.make_async_remote_copy(
                src_ref=win_out.at[pk, core], dst_ref=win_out.at[pk, core],
                send_sem=d2d_send.at[k], recv_sem=d2d_recv.at[k],
                device_id=sibling, device_id_type=MESH)

        place_own = pltpu.make_async_remote_copy(
            src_ref=own_hbm, dst_ref=win_out.at[me], send_sem=own_sems.at[0], recv_sem=own_sems.at[1],
            device_id=sibling, device_id_type=MESH)

        @pl.when((s == 0) & (i == 0))
        def _():
            for k in range(3):
                ici(k, me).start()
            place_own.start()
            comm.start(c_in, c_out, c_sem)

        @pl.when(s == 0)
        def _():
            xv = x_ref[...]
            h_ref[...] = (_dot(xv[:, :half], own_ref[0]) + _dot(xv[:, half:], own_ref[1])).astype(BF16)

        for k in range(3):
            @pl.when((s == k + 1) & (i == 0))
            def _(k=k):
                pk = 2 * peers[k][0] + peers[k][1]
                ici(k, pk).wait_recv()
                forward(k, c).start()
                forward(k, 1 - c).wait_recv()
                loads = [pltpu.make_async_copy(win_out.at[pk, hh], w_scr.at[hh], ld_sems.at[hh])
                         for hh in range(2)]
                for ld in loads:
                    ld.start()
                for ld in loads:
                    ld.wait()

        @pl.when(s > 0)
        def _():
            xv = x_ref[...]
            h_ref[...] = (_dot(xv[:, :half], w_scr[0]) + _dot(xv[:, half:], w_scr[1])).astype(BF16)

        @pl.when((s == N_CHIP - 1) & (i == ni - 1))
        def _():
            place_own.wait()
            for k in range(3):
                ici(k, me).wait_send()
                forward(k, c).wait_send()
            comm.finish(c_in, c_out, c_sem)

    def shard_col(s, me):
        return jnp.where(s == 0, me, me ^ jnp.where(s == 1, 2, jnp.where(s == 2, 1, 3)))

    res = _pc(body, name="in_proj",
              out_shape=(jax.ShapeDtypeStruct((t, N_CHIP * cols), BF16),
                         jax.ShapeDtypeStruct((N_CHIP,) + own.shape, own.dtype)) + tuple(comm.out_shapes),
              grid=(N_CHIP, ni), nsp=1,
              in_specs=[pl.BlockSpec((tm, 2 * half), lambda s, i, chip_ref: (i, 0)),
                        pl.BlockSpec(own.shape, lambda s, i, chip_ref: (0, 0, 0)),
                        ANY] + [ANY] * nci,
              out_specs=(pl.BlockSpec((tm, cols), lambda s, i, chip_ref: (i, shard_col(s, chip_ref[0]))),
                         ANY) + tuple([ANY] * nco),
              scratch=[pltpu.VMEM(own.shape, own.dtype),
                       pltpu.SemaphoreType.DMA((3,)), pltpu.SemaphoreType.DMA((3,)),
                       pltpu.SemaphoreType.DMA((3,)), pltpu.SemaphoreType.DMA((3,)),
                       pltpu.SemaphoreType.DMA((2,)), pltpu.SemaphoreType.DMA((2,))] + comm.sems,
              sem=("arbitrary", "arbitrary"))(chip, x2b, own, own, *comm.ins)
    return res[0], res[1], res[2:]


def _rs_add_halves(name, grad, recv, core):
    _, r, cdim = grad.shape
    half = r // 2
    tr = _row_tile(half, cdim, mult=16)
    nr = half // tr

    def body(c_ref, g_ref, r_ref, o_ref):
        o_ref[...] = (g_ref[...] + r_ref[...]).astype(BF16)

    return _pc(body, name=name, out_shape=jax.ShapeDtypeStruct((N_CHIP, half, cdim), BF16),
               grid=(N_CHIP, nr), nsp=1,
               in_specs=[pl.BlockSpec((None, tr, cdim), lambda j, i, c_ref: (j, c_ref[0] * nr + i, 0)),
                         pl.BlockSpec((None, tr, cdim), lambda j, i, c_ref: (j, i, 0))],
               out_specs=pl.BlockSpec((None, tr, cdim), lambda j, i, c_ref: (j, i, 0)),
               sem=("parallel", "parallel"))(core, grad, recv)


def _chip_exchange_comm(parts):
    n = len(parts)

    def copies(ins, outs, sems):
        send_sems, recv_sems = sems
        x, y, c, peers = _chip_peers()
        return [pltpu.make_async_remote_copy(
            src_ref=ins[ti].at[2 * px + py], dst_ref=outs[ti].at[k],
            send_sem=send_sems.at[ti * 3 + k], recv_sem=recv_sems.at[ti * 3 + k],
            device_id=(px, py, c), device_id_type=MESH)
            for ti in range(n) for k, (px, py) in enumerate(peers)]

    def start(ins, outs, sems):
        for cp in copies(ins, outs, sems):
            cp.start()

    def finish(ins, outs, sems):
        for cp in copies(ins, outs, sems):
            cp.wait()

    return _Comm(parts, [jax.ShapeDtypeStruct((3,) + p.shape[1:], p.dtype) for p in parts],
                 [pltpu.SemaphoreType.DMA((3 * n,)), pltpu.SemaphoreType.DMA((3 * n,))], start, finish)


def _rs_sum_chips(name, part, recv, chip):
    _, half, cdim = recv.shape
    tr = _row_tile(half, cdim, mult=16)

    def body(chip_ref, p_ref, r_ref, o_ref):
        o_ref[...] = ((p_ref[...].astype(F32) + r_ref[0].astype(F32)) + r_ref[1].astype(F32)
                      ) + r_ref[2].astype(F32)

    return _pc(body, name=name, out_shape=jax.ShapeDtypeStruct((half, cdim), F32),
               grid=(half // tr,), nsp=1,
               in_specs=[pl.BlockSpec((None, tr, cdim), lambda i, chip_ref: (chip_ref[0], i, 0)),
                         pl.BlockSpec((3, tr, cdim), lambda i, chip_ref: (0, i, 0))],
               out_specs=pl.BlockSpec((tr, cdim), lambda i, chip_ref: (i, 0)),
               sem=("parallel",))(chip, part, recv)


def _rs_send_halves(halves):
    n = len(halves)

    def body(*refs):
        ins, outs = refs[:n], refs[n:2 * n]
        send_sems, recv_sems = refs[2 * n:]
        x, y, c = lax.axis_index("x"), lax.axis_index("y"), lax.axis_index("c")
        sends = []
        for ti in range(n):
            cp = pltpu.make_async_remote_copy(
                src_ref=ins[ti], dst_ref=outs[ti],
                send_sem=send_sems.at[ti], recv_sem=recv_sems.at[ti],
                device_id=(x, y, 1 - c), device_id_type=MESH)
            cp.start()
            sends.append(cp)
        for cp in sends:
            cp.wait()

    return _pc(body, name="rs_send_halves",
               out_shape=tuple(jax.ShapeDtypeStruct(hv.shape, hv.dtype) for hv in halves),
               in_specs=[ANY] * n, out_specs=tuple([ANY] * n),
               scratch=[pltpu.SemaphoreType.DMA((n,)), pltpu.SemaphoreType.DMA((n,))])(*halves)


def _adamw_rows(name, mine, theirs, w, m, v, core):
    half, cdim = mine.shape
    tr = _row_tile(half, cdim, budget=1 << 19)
    nrh = half // tr

    def body(c_ref, mine_ref, theirs_ref, w_ref, m_ref, v_ref, g_ref, d_ref, m2_ref, v2_ref):
        is_mine = (pl.program_id(0) // nrh) == c_ref[0]
        g = jnp.where(is_mine, mine_ref[...], theirs_ref[...])
        d, m2, v2 = _adamw(w_ref[...], g, m_ref[...], v_ref[...])
        g_ref[...] = g
        d_ref[...] = d
        m2_ref[...] = m2
        v2_ref[...] = v2

    htile = pl.BlockSpec((tr, cdim), lambda i, c_ref: (i % nrh, 0))
    tile = pl.BlockSpec((tr, cdim), lambda i, c_ref: (i, 0))
    shp = jax.ShapeDtypeStruct((2 * half, cdim), F32)
    return _pc(body, name=name, out_shape=(shp, shp, shp, shp), grid=(2 * nrh,), nsp=1,
               in_specs=[htile, htile, tile, tile, tile], out_specs=(tile, tile, tile, tile),
               sem=("parallel",))(core, mine, theirs, w, m, v)


def _adamw_whole(name, g, w, m, v):
    def body(g_ref, w_ref, m_ref, v_ref, d_ref, m2_ref, v2_ref):
        d, m2, v2 = _adamw(w_ref[...], g_ref[...], m_ref[...], v_ref[...])
        d_ref[...] = d
        m2_ref[...] = m2
        v2_ref[...] = v2

    shp = jax.ShapeDtypeStruct(g.shape, F32)
    return _pc(body, name=name, out_shape=(shp, shp, shp))(g, w, m, v)


SMALL_LAYOUT = (
    ("sgu_w_s", 1024, 1, 0),
    ("sgu_b_s", 8, 1, 1024),
    ("sgu_norm_g", 1, 0, 0),
    ("sgu_norm_b", 1, 0, 1),
    ("hgrn_norm_g", 1, 0, 3),
    ("ln1_g", 1, 0, 4),
    ("ln1_b", 1, 0, 5),
    ("ffn_conv_b", 1, 2, 3),
    ("ln2_g", 1, 0, 6),
    ("ln2_b", 1, 0, 7),
)
LB_ROW = 2
LOSS_ROW = 8
PACK_SHAPES = ((16, D_MODEL), (N_GROUP * 128 + 8, 128), (8, D_FF))


def _small_allreduce_adamw(rows1024, dws, dbs, dcw, dcb, logits, m_logits, v_logits,
                           small_w, small_m, small_v):
    ns = len(SMALL_LAYOUT)
    nr = len(rows1024)
    nb = len(PACK_SHAPES)

    def body(*refs):
        row_refs = refs[:nr]
        dws_ref, dbs_ref, dcw_ref, dcb_ref, lg_ref, mlg_ref, vlg_ref = refs[nr:nr + 7]
        pos = nr + 7
        w_refs = refs[pos:pos + ns]
        m_refs = refs[pos + ns:pos + 2 * ns]
        v_refs = refs[pos + 2 * ns:pos + 3 * ns]
        pos += 3 * ns
        loss_ref, dcw_out = refs[pos:pos + 2]
        lg_outs = refs[pos + 2:pos + 6]
        pos += 6
        outs = refs[pos:pos + 4 * ns]
        pos += 4 * ns
        pack = refs[pos:pos + nb]
        sib = refs[pos + nb:pos + 2 * nb]
        gath = refs[pos + 2 * nb:pos + 3 * nb]
        d2d_send, d2d_recv, ici_send, ici_recv = refs[pos + 3 * nb:]

        x, y, c, peers = _chip_peers()
        me = 2 * x + y
        sibling = (x, y, 1 - c)

        pack[0][...] = jnp.zeros(PACK_SHAPES[0], F32)
        for k in range(nr):
            pack[0][k:k + 1, :] = row_refs[k][0:1, :]
        pack[1][0:N_GROUP * 128, :] = dws_ref[...]
        pack[1][N_GROUP * 128:, :] = dbs_ref[...]
        pack[2][...] = jnp.zeros(PACK_SHAPES[2], F32)
        pack[2][0:3, :] = dcw_ref[0:3, :]
        pack[2][3:4, :] = dcb_ref[0:1, :]

        d2d = [pltpu.make_async_remote_copy(
            src_ref=pack[b], dst_ref=sib[b], send_sem=d2d_send.at[b], recv_sem=d2d_recv.at[b],
            device_id=sibling, device_id_type=MESH) for b in range(nb)]
        for cp in d2d:
            cp.start()
        for cp in d2d:
            cp.wait()
        for b in range(nb):
            gath[b][me] = pack[b][...] + sib[b][...]

        ici, ici_wait = [], []
        for b in range(nb):
            for k, (px, py) in enumerate(peers):
                sem = dict(send_sem=ici_send.at[b * 3 + k], recv_sem=ici_recv.at[b * 3 + k],
                           device_id=(px, py, c), device_id_type=MESH)
                ici.append(pltpu.make_async_remote_copy(src_ref=gath[b].at[me], dst_ref=gath[b].at[me], **sem))
                ici_wait.append(pltpu.make_async_remote_copy(
                    src_ref=gath[b].at[me], dst_ref=gath[b].at[2 * px + py], **sem))
        for cp in ici:
            cp.start()
        for cp in ici_wait:
            cp.wait_recv()
        for cp in ici:
            cp.wait_send()

        tot = pack
        for b in range(nb):
            tot[b][...] = ((gath[b][0] + gath[b][1]) + gath[b][2]) + gath[b][3]

        loss_ref[...] = tot[0][LOSS_ROW:LOSS_ROW + 1, :]
        dcw_out[...] = tot[2][...]
        lb = _sig(lg_ref[0:1, :] - lg_ref[1:2, :])
        d0 = tot[0][LB_ROW:LB_ROW + 1, :] * lb * (1.0 - lb)
        rowid = lax.broadcasted_iota(jnp.int32, (2, D_MODEL), 0)
        g_lg = jnp.where(rowid == 0, d0, -d0)
        dl, ml, vl = _adamw(lg_ref[...], g_lg, mlg_ref[...], vlg_ref[...])
        lg_outs[0][...] = g_lg
        lg_outs[1][...] = dl
        lg_outs[2][...] = ml
        lg_outs[3][...] = vl
        for si, (_, rows, b, r0) in enumerate(SMALL_LAYOUT):
            g = tot[b][r0:r0 + rows, :]
            dl, ml, vl = _adamw(w_refs[si][...], g, m_refs[si][...], v_refs[si][...])
            outs[4 * si][...] = g
            outs[4 * si + 1][...] = dl
            outs[4 * si + 2][...] = ml
            outs[4 * si + 3][...] = vl

    shapes = [jax.ShapeDtypeStruct((1, D_MODEL), F32), jax.ShapeDtypeStruct((8, D_FF), F32)]
    shapes += [jax.ShapeDtypeStruct((2, D_MODEL), F32)] * 4
    for w in small_w:
        shapes += [jax.ShapeDtypeStruct(w.shape, F32)] * 4
    scratch = [pltpu.VMEM(shp, F32) for shp in PACK_SHAPES]
    scratch += [pltpu.VMEM(shp, F32) for shp in PACK_SHAPES]
    scratch += [pltpu.VMEM((N_CHIP,) + shp, F32) for shp in PACK_SHAPES]
    scratch += [pltpu.SemaphoreType.DMA((nb,)), pltpu.SemaphoreType.DMA((nb,)),
                pltpu.SemaphoreType.DMA((3 * nb,)), pltpu.SemaphoreType.DMA((3 * nb,))]
    vm = pl.BlockSpec(memory_space=pltpu.VMEM)
    n_in = nr + 7 + 3 * ns
    res = _pc(body, name="small_allreduce_adamw", out_shape=tuple(shapes),
              in_specs=[vm] * n_in, out_specs=tuple([vm] * len(shapes)),
              scratch=scratch)(*rows1024, dws, dbs, dcw, dcb, logits, m_logits, v_logits,
                               *small_w, *small_m, *small_v)
    return res[0], res[1], res[2:6], res[6:]


def kernel(x, p, w_in, sgu_w_s, sgu_b_s, sgu_norm_g, sgu_norm_b, hgrn_lb_logits, hgrn_norm_g, w_branch, w_out, ln1_g, ln1_b, ffn_w_up, ffn_conv_w, ffn_conv_b, ffn_w_down, ln2_g, ln2_b, ple_w_proj, ple_w_gate, loss_target, m_w_in, m_sgu_w_s, m_sgu_b_s, m_sgu_norm_g, m_sgu_norm_b, m_hgrn_lb_logits, m_hgrn_norm_g, m_w_branch, m_w_out, m_ln1_g, m_ln1_b, m_ffn_w_up, m_ffn_conv_w, m_ffn_conv_b, m_ffn_w_down, m_ln2_g, m_ln2_b, m_ple_w_proj, m_ple_w_gate, v_w_in, v_sgu_w_s, v_sgu_b_s, v_sgu_norm_g, v_sgu_norm_b, v_hgrn_lb_logits, v_hgrn_norm_g, v_w_branch, v_w_out, v_ln1_g, v_ln1_b, v_ffn_w_up, v_ffn_conv_w, v_ffn_conv_b, v_ffn_w_down, v_ln2_g, v_ln2_b, v_ple_w_proj, v_ple_w_gate):
    t = x.shape[1]
    x2 = x.reshape(t, D_MODEL)
    x2b = x2.astype(BF16)
    p2 = p.reshape(t, PLE_DIM)
    tgt = loss_target.reshape(t, D_MODEL)
    core = lax.axis_index("c").astype(jnp.int32).reshape(1)
    chip_id = (2 * lax.axis_index("x") + lax.axis_index("y")).astype(jnp.int32).reshape(1)

    big_w = [w_in[0], w_branch[0, 0], w_branch[0, 1], w_out[0], ffn_w_up[0], ffn_w_down[0],
             ple_w_proj[0], ple_w_gate[0]]
    big_m = [m_w_in[0], m_w_branch[0, 0], m_w_branch[0, 1], m_w_out[0], m_ffn_w_up[0],
             m_ffn_w_down[0], m_ple_w_proj[0], m_ple_w_gate[0]]
    big_v = [v_w_in[0], v_w_branch[0, 0], v_w_branch[0, 1], v_w_out[0], v_ffn_w_up[0],
             v_ffn_w_down[0], v_ple_w_proj[0], v_ple_w_gate[0]]
    def halves_of(i):
        w = big_w[i]
        return w.astype(BF16).reshape(2, w.shape[0] // 2, w.shape[1])

    def stacked(g, i):
        return g.reshape(N_CHIP, big_w[i].shape[0], big_w[i].shape[1])


    cid = jnp.arange(SGU_BLOCK) // CHUNK
    maskf = (cid[:, None] >= cid[None, :]).astype(F32)
    ws_masked = sgu_w_s[0] * maskf[None]
    wm = ws_masked.astype(BF16)
    wmt = jnp.transpose(ws_masked, (0, 2, 1)).astype(BF16)
    bsb = jnp.broadcast_to(sgu_b_s[0][:, :, None], (N_GROUP, SGU_BLOCK, 128))

    h, win_g, (wup_g,) = _in_proj_gathering(x2b, halves_of(0), chip_id, _gather_comm([halves_of(4)]), 512)
    win_st = stacked(win_g, 0)
    wup_st = stacked(wup_g, 4)
    ya = _sgu_fwd(h, wm, bsb, sgu_norm_g, sgu_norm_b)
    rest = (1, 2, 3, 5, 6, 7)
    (yb, st_all), rest_g = _hgrn_fwd(h, hgrn_lb_logits, hgrn_norm_g,
                                      comm=_gather_comm([halves_of(i) for i in rest], [ffn_conv_w[0]]))
    convw = jnp.transpose(rest_g[-1], (1, 0, 2)).reshape(3, D_FF)
    rest_g = [stacked(g, i) for g, i in zip(rest_g, rest)]
    wb0 = rest_g[0].reshape(D_MODEL, D_MODEL)
    wb1 = rest_g[1].reshape(D_MODEL, D_MODEL)
    wo = rest_g[2].reshape(D_MODEL, D_MODEL)
    wd = rest_g[3].reshape(D_FF, D_MODEL)
    wpp = jnp.transpose(rest_g[4], (1, 0, 2)).reshape(PLE_DIM, D_MODEL)
    wpg = rest_g[5].reshape(D_MODEL, D_MODEL)
    r1, a_br, b_br, m_bf, x1b = _mix_fwd(ya, yb, h, x2, wb0, wb1, wo, ln1_g, ln1_b, 256)
    h2 = _mm_nn_stacked("ffn_up", x1b, wup_st, 512)
    act = _ffn_act_fwd(h2, convw, ffn_conv_b, 256)
    dr2, dpg, dpp, loss_acc, dg2, db2 = _out_fwd_bwd(
        act, x1b, r1, p2, tgt, wd, wpg, wpp, ln1_g, ln1_b, ln2_g, ln2_b, 256)

    dact = _mm("ffn_down_bwd", dr2, wd, NT, (t // 512, FF_NJ, 1),
               pl.BlockSpec((512, D_MODEL), lambda i, j, k: (i, 0)),
               pl.BlockSpec((FF_TILE, D_MODEL), lambda i, j, k: (j, 0)),
               jax.ShapeDtypeStruct((t, D_FF), BF16),
               pl.BlockSpec((512, FF_TILE), lambda i, j, k: (i, j)))
    dh2, dcw, dcb = _ffn_act_bwd(h2, dact, convw, ffn_conv_b, 256)
    d_wd = _mm_tn("ffn_down_wgrad", act, dr2, FF_TILE, 512)
    d_wpg = _mm_tn("ple_gate_wgrad", x1b, dpg, 512, D_MODEL)
    d_wpp_st = _mm_tn("ple_proj_wgrad", p2, dpp, PLE_DIM, PLE_DIM, stacked=True)
    d_wup_st = _mm("ffn_up_wgrad", x1b, dh2, TN, (2, N_CHIP, 1),
                   pl.BlockSpec((t, 512), lambda i, j, k: (0, i)),
                   pl.BlockSpec((None, t, FF_TILE), lambda i, j, k: (j // FF_NJ, 0, j % FF_NJ)),
                   jax.ShapeDtypeStruct((N_CHIP, D_MODEL, FF_TILE), F32),
                   pl.BlockSpec((None, 512, FF_TILE), lambda i, j, k: (j, i, 0)))
    dr1, dg1, db1 = _ffn_in_bwd(dh2, wup_st, dpg, wpg, dr2, r1, ln1_g, 512)
    da_bf, db_bf, dh3, dya, dyb = _mix_bwd(dr1, h, a_br, b_br, wo, wb0, wb1, 256)
    d_wo = _mm_tn("out_proj_wgrad", m_bf, dr1, 512, 512)
    d_wb0 = _mm_tn("branch0_wgrad", ya, da_bf, 512, D_MODEL)
    d_wb1 = _mm_tn("branch1_wgrad", yb, db_bf, 512, D_MODEL)
    grads_1 = [d_wb0.reshape(4, 256, D_MODEL), d_wb1.reshape(4, 256, D_MODEL),
               d_wo.reshape(4, 256, D_MODEL), d_wup_st, d_wd.reshape(4, D_FF // 4, D_MODEL),
               d_wpp_st, d_wpg.reshape(4, 256, D_MODEL)]
    (dh0, dws, dbs, dgv, dbv), recv_a1 = _sgu_bwd(h, dya, wm, wmt, bsb, sgu_norm_g, sgu_norm_b, maskf,
                                                  comm=_sibling_exchange_comm(grads_1))
    parts_1 = [_rs_add_halves("rs_add_halves%d" % (i + 1), g, r, core)
               for i, (g, r) in enumerate(zip(grads_1, recv_a1))]
    (dh1, dh2h, dlb, dgn), recv_b1 = _hgrn_bwd(h, dyb, st_all, hgrn_lb_logits, hgrn_norm_g,
                                                comm=_chip_exchange_comm(parts_1))
    dh_parts = [dh0, dh1, dh2h, dh3]
    d_win = [_mm_tn("in_proj_wgrad%d" % j, x2b, dh_parts[j], 512, D_MODEL) for j in range(4)]

    grads_0 = [jnp.stack(d_win)]
    recv_a0 = _run_comm("rs_sibling_exchange0", _sibling_exchange_comm(grads_0))
    parts_0 = [_rs_add_halves("rs_add_halves0", grads_0[0], recv_a0[0], core)]
    gx, recv_b0 = _in_proj_xgrad(dh_parts, win_st, dr1, 512, comm=_chip_exchange_comm(parts_0))
    parts = parts_0 + parts_1
    recv_b = list(recv_b0) + list(recv_b1)
    halves = [_rs_sum_chips("rs_sum_chips%d" % i, pt, r, chip_id)
              for i, (pt, r) in enumerate(zip(parts, recv_b))]
    theirs = _rs_send_halves(halves)
    big_out = [_adamw_rows("adamw_big%d" % i, halves[i], theirs[i], big_w[i], big_m[i], big_v[i], core)
               for i in range(len(halves))]

    small_in = dict(sgu_w_s=(sgu_w_s, m_sgu_w_s, v_sgu_w_s), sgu_b_s=(sgu_b_s, m_sgu_b_s, v_sgu_b_s),
                    sgu_norm_g=(sgu_norm_g, m_sgu_norm_g, v_sgu_norm_g),
                    sgu_norm_b=(sgu_norm_b, m_sgu_norm_b, v_sgu_norm_b),
                    hgrn_norm_g=(hgrn_norm_g, m_hgrn_norm_g, v_hgrn_norm_g),
                    ln1_g=(ln1_g, m_ln1_g, v_ln1_g), ln1_b=(ln1_b, m_ln1_b, v_ln1_b),
                    ffn_conv_b=(ffn_conv_b, m_ffn_conv_b, v_ffn_conv_b),
                    ln2_g=(ln2_g, m_ln2_g, v_ln2_g), ln2_b=(ln2_b, m_ln2_b, v_ln2_b))

    def flat(name, arr):
        rows = dict((n, r) for n, r, _, _ in SMALL_LAYOUT)[name]
        return arr.reshape(rows, arr.size // rows)

    names = [n for n, _, _, _ in SMALL_LAYOUT]
    sw = [flat(n, small_in[n][0]) for n in names]
    sm = [flat(n, small_in[n][1]) for n in names]
    sv = [flat(n, small_in[n][2]) for n in names]
    loss_rows, dcw_tot, lg_out, small_out = _small_allreduce_adamw(
        [dgv, dbv, dlb, dgn, dg1, db1, dg2, db2, loss_acc], dws.reshape(N_GROUP * 128, 128), dbs, dcw, dcb,
        hgrn_lb_logits, m_hgrn_lb_logits, v_hgrn_lb_logits, sw, sm, sv)
    loss = loss_rows[0, 0]

    chip = 2 * lax.axis_index("x") + lax.axis_index("y")
    g_cw = lax.dynamic_slice(dcw_tot, (0, chip * (D_FF // 4)), (3, D_FF // 4))
    cw_out = _adamw_whole("adamw_conv_w", g_cw, ffn_conv_w[0], m_ffn_conv_w[0], v_ffn_conv_w[0])

    res = {}
    for si, n in enumerate(names):
        shp = small_in[n][0].shape
        res[n] = tuple(small_out[4 * si + k].reshape(shp) for k in range(4))
    res["hgrn_lb_logits"] = tuple(lg_out)
    res["ffn_conv_w"] = (g_cw[None],) + tuple(o[None] for o in cw_out)

    def big(i):
        return tuple(big_out[i])

    res["w_in"] = tuple(o[None] for o in big(0))
    res["w_branch"] = tuple(jnp.stack([o0, o1])[None] for o0, o1 in zip(big(1), big(2)))
    res["w_out"] = tuple(o[None] for o in big(3))
    res["ffn_w_up"] = tuple(o[None] for o in big(4))
    res["ffn_w_down"] = tuple(o[None] for o in big(5))
    res["ple_w_proj"] = tuple(o[None] for o in big(6))
    res["ple_w_gate"] = tuple(o[None] for o in big(7))

    order = ["w_in", "sgu_w_s", "sgu_b_s", "sgu_norm_g", "sgu_norm_b", "hgrn_lb_logits",
             "hgrn_norm_g", "w_branch", "w_out", "ln1_g", "ln1_b", "ffn_w_up", "ffn_conv_w",
             "ffn_conv_b", "ffn_w_down", "ln2_g", "ln2_b", "ple_w_proj", "ple_w_gate"]
    outs = [loss, gx.reshape(1, t, D_MODEL)]
    for k in range(4):
        outs += [res[n][k] for n in order]
    return tuple(outs)
```

```python
import functools

import jax
import jax.numpy as jnp
from jax import lax
from jax.experimental import pallas as pl
from jax.experimental.pallas import tpu as pltpu

F32 = jnp.float32
BF16 = jnp.bfloat16
HIGHEST = lax.Precision.HIGHEST
MESH = pl.DeviceIdType.MESH

D_MODEL = 1024
CHUNK = 64
SGU_BLOCK = 128
N_GROUP = 8
N_HEAD = 8
HEAD_DIM = 128
D_FF = 2816
PLE_DIM = 256
IN_COLS = 8192
LN_EPS = 1e-5
RMS_EPS = 1e-6
ALPHA = 2.0 ** 0.25
N_CHIP = 4
N_DEV = 8

ADAM_LR = 0.001
ADAM_B1 = 0.9
ADAM_B2 = 0.999
ADAM_EPS = 1e-08
ADAM_WD = 0.01
ADAM_STEP = 10

VMEM_LIMIT = 56 * 1024 * 1024

NN = (((1,), (0,)), ((), ()))
NT = (((1,), (1,)), ((), ()))
TN = (((0,), (0,)), ((), ()))


def _pc(body, *, name, out_shape, grid=None, in_specs=None, out_specs=None, scratch=(),
        sem=None, nsp=0, vmem=VMEM_LIMIT):
    params = dict(vmem_limit_bytes=vmem)
    if sem is not None:
        params["dimension_semantics"] = sem
    kw = dict(name=name, out_shape=out_shape, compiler_params=pltpu.CompilerParams(**params))
    if nsp:
        kw["grid_spec"] = pltpu.PrefetchScalarGridSpec(
            num_scalar_prefetch=nsp, grid=grid, in_specs=in_specs, out_specs=out_specs,
            scratch_shapes=list(scratch))
    else:
        if grid is not None:
            kw["grid"] = grid
        if in_specs is not None:
            kw["in_specs"] = in_specs
            kw["out_specs"] = out_specs
        kw["scratch_shapes"] = list(scratch)
    return pl.pallas_call(body, **kw)


def _dot(a, b, dims=NN):
    return lax.dot_general(a.astype(BF16), b.astype(BF16), dims, preferred_element_type=F32)


def _dot32(a, b, dims=NN):
    return lax.dot_general(a, b, dims, precision=HIGHEST, preferred_element_type=F32)


def _sig(x):
    return 1.0 / (1.0 + jnp.exp(-x))


_GC = 0.7978845608028654
_GA = 0.044715


def _gelu(x):
    return 0.5 * x * (1.0 + jnp.tanh(_GC * (x + _GA * x * x * x)))


def _gelu_and_grad(x):
    t = jnp.tanh(_GC * (x + _GA * x * x * x))
    g = 0.5 * x * (1.0 + t)
    dg = 0.5 * (1.0 + t) + 0.5 * x * (1.0 - t * t) * _GC * (1.0 + 3.0 * _GA * x * x)
    return g, dg


def _ln_stats(r):
    mu = jnp.mean(r, axis=-1, keepdims=True)
    xc = r - mu
    var = jnp.mean(xc * xc, axis=-1, keepdims=True)
    rstd = lax.rsqrt(var + LN_EPS)
    return xc * rstd, rstd


def _ln_bwd(dxh, xh, rstd):
    m1 = jnp.mean(dxh, axis=-1, keepdims=True)
    m2 = jnp.mean(dxh * xh, axis=-1, keepdims=True)
    return rstd * (dxh - m1 - xh * m2)


def _colsum8(v):
    return jnp.broadcast_to(jnp.sum(v, axis=0, keepdims=True), (8, v.shape[1]))


def _adamw(w, g, m, v):
    m2 = ADAM_B1 * m + (1.0 - ADAM_B1) * g
    v2 = ADAM_B2 * v + (1.0 - ADAM_B2) * (g * g)
    m_hat = m2 / (1.0 - ADAM_B1 ** ADAM_STEP)
    v_hat = v2 / (1.0 - ADAM_B2 ** ADAM_STEP)
    delta = -ADAM_LR * (m_hat / (jnp.sqrt(v_hat) + ADAM_EPS) + ADAM_WD * w)
    return delta, m2, v2


def _row_tile(rows, cols, itemsize=4, budget=1 << 20, mult=8):
    best = mult
    for tr in range(mult, rows + 1, mult):
        if rows % tr == 0 and tr * cols * itemsize <= budget:
            best = tr
    return best


def _mm(name, a, b, dims, grid, a_spec, b_spec, out_shape, o_spec, add=None, add_spec=None,
        add_scale=1.0, comm=None):
    nk = grid[2]
    has_add = add is not None
    out_dtype = out_shape.dtype

    def body(*refs):
        if has_add:
            a_ref, b_ref, add_ref, o_ref = refs[:4]
            rest = refs[4:]
        else:
            a_ref, b_ref, o_ref = refs[:3]
            add_ref = None
            rest = refs[3:]
        prod = _dot(a_ref[...], b_ref[...], dims)

        def finish(acc):
            if has_add:
                acc = acc + add_scale * add_ref[...]
            o_ref[...] = acc.astype(out_dtype)

        if nk == 1:
            finish(prod)
        else:
            acc_ref = rest[0]
            k = pl.program_id(2)

            @pl.when(k == 0)
            def _():
                acc_ref[...] = prod

            @pl.when(k > 0)
            def _():
                acc_ref[...] += prod

            @pl.when(k == nk - 1)
            def _():
                finish(acc_ref[...])

    in_specs = [a_spec, b_spec] + ([add_spec] if has_add else [])
    args = [a, b] + ([add] if has_add else [])
    scratch = []
    if nk > 1:
        blk = [d for d in o_spec.block_shape if d is not None]
        scratch = [pltpu.VMEM(tuple(blk), F32)]
    if comm is None:
        return _pc(body, name=name, out_shape=out_shape, grid=grid, in_specs=in_specs,
                   out_specs=o_spec, scratch=scratch,
                   sem=("parallel", "parallel", "arbitrary"))(*args)

    def first():
        return (pl.program_id(0) == 0) & (pl.program_id(1) == 0) & (pl.program_id(2) == 0)

    def last():
        return ((pl.program_id(0) == grid[0] - 1) & (pl.program_id(1) == grid[1] - 1)
                & (pl.program_id(2) == grid[2] - 1))

    res = _hosted_call(body, comm, first, last, name=name, out_shape=(out_shape,), grid=grid,
                       in_specs=in_specs, out_specs=(o_spec,), scratch=scratch,
                       sem=("arbitrary", "arbitrary", "arbitrary"), args=args)
    return res[0], res[1:]


class _Comm:
    def __init__(self, ins, out_shapes, sems, start, finish):
        self.ins, self.out_shapes, self.sems = list(ins), list(out_shapes), list(sems)
        self.start, self.finish = start, finish


def _hosted_call(body, comm, first, last, *, name, out_shape, grid, in_specs, out_specs, scratch, sem,
                 args):
    n_in, n_out, n_scr = len(in_specs), len(out_shape), len(scratch)
    nci, nco = len(comm.ins), len(comm.out_shapes)

    def wrapped(*refs):
        pos = n_in
        own_in, c_in = refs[:pos], refs[pos:pos + nci]
        pos += nci
        own_out, c_out = refs[pos:pos + n_out], refs[pos + n_out:pos + n_out + nco]
        pos += n_out + nco
        own_scr, c_sem = refs[pos:pos + n_scr], refs[pos + n_scr:]

        @pl.when(first())
        def _():
            comm.start(c_in, c_out, c_sem)

        body(*own_in, *own_out, *own_scr)

        @pl.when(last())
        def _():
            comm.finish(c_in, c_out, c_sem)

    return _pc(wrapped, name=name, out_shape=tuple(out_shape) + tuple(comm.out_shapes), grid=grid,
               in_specs=list(in_specs) + [ANY] * nci, out_specs=tuple(out_specs) + tuple([ANY] * nco),
               scratch=list(scratch) + comm.sems, sem=sem)(*args, *comm.ins)


def _grid1_call(body, comm, n, *, name, out_shape, in_specs, out_specs, scratch, args):
    if comm is None:
        return _pc(body, name=name, out_shape=out_shape, grid=(n,), in_specs=in_specs,
                   out_specs=out_specs, scratch=scratch, sem=("arbitrary",))(*args), ()
    res = _hosted_call(body, comm, lambda: pl.program_id(0) == 0, lambda: pl.program_id(0) == n - 1,
                       name=name, out_shape=out_shape, grid=(n,), in_specs=in_specs,
                       out_specs=out_specs, scratch=scratch, sem=("arbitrary",), args=args)
    return res[:len(out_shape)], res[len(out_shape):]


def _run_comm(name, comm):
    nci, nco = len(comm.ins), len(comm.out_shapes)

    def body(*refs):
        c_in, c_out, c_sem = refs[:nci], refs[nci:nci + nco], refs[nci + nco:]
        comm.start(c_in, c_out, c_sem)
        comm.finish(c_in, c_out, c_sem)

    return _pc(body, name=name, out_shape=tuple(comm.out_shapes), in_specs=[ANY] * nci,
               out_specs=tuple([ANY] * nco), scratch=comm.sems)(*comm.ins)


def _mm_nn_stacked(name, a, w_st, tm, comm=None):
    t, k = a.shape
    _, _, c = w_st.shape
    return _mm(name, a, w_st, NN, (t // tm, N_CHIP, 1),
               pl.BlockSpec((tm, k), lambda i, j, kk: (i, 0)),
               pl.BlockSpec((None, k, c), lambda i, j, kk: (j, 0, 0)),
               jax.ShapeDtypeStruct((t, N_CHIP * c), BF16),
               pl.BlockSpec((tm, c), lambda i, j, kk: (i, j)), comm=comm)


def _mm_tn(name, a, b, tm, tn, stacked=False):
    t, m = a.shape
    _, n = b.shape
    if stacked:
        assert tm == m
        out_shape = jax.ShapeDtypeStruct((n // tn, m, tn), F32)
        o_spec = pl.BlockSpec((None, tm, tn), lambda i, j, kk: (j, 0, 0))
    else:
        out_shape = jax.ShapeDtypeStruct((m, n), F32)
        o_spec = pl.BlockSpec((tm, tn), lambda i, j, kk: (i, j))
    return _mm(name, a, b, TN, (m // tm, n // tn, 1),
               pl.BlockSpec((t, tm), lambda i, j, kk: (0, i)),
               pl.BlockSpec((t, tn), lambda i, j, kk: (0, j)),
               out_shape, o_spec)


def _in_proj_xgrad(dh_parts, win_st, dr1, tm, comm=None):
    t = dr1.shape[0]
    ni = t // tm

    def body(a0, a1, a2, a3, b_ref, add_ref, o_ref, acc):
        j = pl.program_id(1)
        for jj, a_ref in enumerate((a0, a1, a2, a3)):
            @pl.when(j == jj)
            def _(jj=jj, a_ref=a_ref):
                prod = _dot(a_ref[...], b_ref[...], NT)
                if jj == 0:
                    acc[...] = prod + ALPHA * add_ref[...]
                elif jj < N_CHIP - 1:
                    acc[...] += prod
                else:
                    o_ref[...] = acc[...] + prod

    a_spec = pl.BlockSpec((tm, 2 * D_MODEL), lambda i, j: (i, 0))
    tile = pl.BlockSpec((tm, D_MODEL), lambda i, j: (i, 0))
    kw = dict(name="in_proj_xgrad", out_shape=(jax.ShapeDtypeStruct((t, D_MODEL), F32),),
              grid=(ni, N_CHIP),
              in_specs=[a_spec] * 4 + [pl.BlockSpec((None, D_MODEL, 2 * D_MODEL), lambda i, j: (j, 0, 0)),
                                       tile],
              out_specs=(tile,), scratch=[pltpu.VMEM((tm, D_MODEL), F32)],
              sem=("arbitrary", "arbitrary"))
    args = list(dh_parts) + [win_st, dr1]
    if comm is None:
        return _pc(body, **kw)(*args)[0], ()
    res = _hosted_call(body, comm,
                       lambda: (pl.program_id(0) == 0) & (pl.program_id(1) == 0),
                       lambda: (pl.program_id(0) == ni - 1) & (pl.program_id(1) == N_CHIP - 1),
                       args=args, **kw)
    return res[0], res[1:]


def _sgu_mixed(v, wm_ref, bsb_ref, gv, bv):
    gl, dgl = _gelu_and_grad(v)
    vh, rstd = _ln_stats(gl)
    vn = vh * gv + bv
    mixed = []
    for g in range(N_GROUP):
        sl = slice(g * 128, (g + 1) * 128)
        mixed.append(_dot(wm_ref[g], vn[:, sl]) + bsb_ref[g])
    return dgl, vh, rstd, vn, mixed


def _sgu_fwd(h, wm, bsb, gv, bv, comm=None):
    t = h.shape[0]

    def body(u_ref, v_ref, wm_ref, bsb_ref, gv_ref, bv_ref, ya_ref):
        u = u_ref[...].astype(F32)
        _, _, _, _, mixed = _sgu_mixed(v_ref[...].astype(F32), wm_ref, bsb_ref, gv_ref[...], bv_ref[...])
        gu = _gelu(u)
        for g in range(N_GROUP):
            sl = slice(g * 128, (g + 1) * 128)
            ya_ref[:, sl] = (gu[:, sl] * mixed[g]).astype(BF16)

    full3 = pl.BlockSpec((N_GROUP, 128, 128), lambda i: (0, 0, 0))
    vec = pl.BlockSpec((1, D_MODEL), lambda i: (0, 0))
    (ya,), extra = _grid1_call(
        body, comm, t // SGU_BLOCK, name="sgu_fwd",
        out_shape=(jax.ShapeDtypeStruct((t, D_MODEL), BF16),),
        in_specs=[pl.BlockSpec((SGU_BLOCK, D_MODEL), lambda i: (i, 0)),
                  pl.BlockSpec((SGU_BLOCK, D_MODEL), lambda i: (i, 1)),
                  full3, full3, vec, vec],
        out_specs=(pl.BlockSpec((SGU_BLOCK, D_MODEL), lambda i: (i, 0)),),
        scratch=[], args=(h, h, wm, bsb, gv, bv))
    return ya, extra


def _sgu_bwd(h, dya, wm, wmt, bsb, gv, bv, maskf, comm=None):
    t = h.shape[0]
    nb = t // SGU_BLOCK

    def body(u_ref, v_ref, dya_ref, wm_ref, wmt_ref, bsb_ref, gv_ref, bv_ref, mask_ref,
             dh_ref, dws_ref, dbs_ref, dgv_ref, dbv_ref, dmix_acc):
        i = pl.program_id(0)

        @pl.when(i == 0)
        def _():
            dws_ref[...] = jnp.zeros_like(dws_ref)
            dgv_ref[...] = jnp.zeros_like(dgv_ref)
            dbv_ref[...] = jnp.zeros_like(dbv_ref)
            dmix_acc[...] = jnp.zeros_like(dmix_acc)

        u = u_ref[...].astype(F32)
        gvv = gv_ref[...]
        dgl_v, vh, rstd, vn, mixed = _sgu_mixed(v_ref[...].astype(F32), wm_ref, bsb_ref, gvv, bv_ref[...])
        gu, dgl_u = _gelu_and_grad(u)
        dya_v = dya_ref[...]
        dvn_parts = []
        for g in range(N_GROUP):
            sl = slice(g * 128, (g + 1) * 128)
            d_y = dya_v[:, sl]
            dh_ref[:, sl] = (d_y * mixed[g] * dgl_u[:, sl]).astype(BF16)
            d_mixed = d_y * gu[:, sl]
            dmix_acc[g] += d_mixed
            dws_ref[g] += _dot(d_mixed, vn[:, sl], NT) * mask_ref[...]
            dvn_parts.append(_dot(wmt_ref[g], d_mixed))
        dvn = jnp.concatenate(dvn_parts, axis=1)
        dgv_ref[...] += _colsum8(dvn * vh)
        dbv_ref[...] += _colsum8(dvn)
        d_gl = _ln_bwd(dvn * gvv, vh, rstd)
        dh_ref[:, D_MODEL:] = (d_gl * dgl_v).astype(BF16)

        @pl.when(i == nb - 1)
        def _():
            rowid = lax.broadcasted_iota(jnp.int32, (8, 128), 0)
            ones = jnp.ones((8, 128), F32)
            acc = jnp.zeros((8, 128), F32)
            for g in range(N_GROUP):
                rs = _dot32(ones, dmix_acc[g], NT)
                acc = jnp.where(rowid == g, rs, acc)
            dbs_ref[...] = acc

    full3 = pl.BlockSpec((N_GROUP, 128, 128), lambda i: (0, 0, 0))
    vec = pl.BlockSpec((1, D_MODEL), lambda i: (0, 0))
    acc8 = pl.BlockSpec((8, D_MODEL), lambda i: (0, 0))
    return _grid1_call(
        body, comm, nb, name="sgu_bwd",
        out_shape=(jax.ShapeDtypeStruct((t, 2 * D_MODEL), BF16),
                   jax.ShapeDtypeStruct((N_GROUP, 128, 128), F32),
                   jax.ShapeDtypeStruct((8, 128), F32),
                   jax.ShapeDtypeStruct((8, D_MODEL), F32),
                   jax.ShapeDtypeStruct((8, D_MODEL), F32)),
        in_specs=[pl.BlockSpec((SGU_BLOCK, D_MODEL), lambda i: (i, 0)),
                  pl.BlockSpec((SGU_BLOCK, D_MODEL), lambda i: (i, 1)),
                  pl.BlockSpec((SGU_BLOCK, D_MODEL), lambda i: (i, 0)),
                  full3, full3, full3, vec, vec,
                  pl.BlockSpec((128, 128), lambda i: (0, 0))],
        out_specs=(pl.BlockSpec((SGU_BLOCK, 2 * D_MODEL), lambda i: (i, 0)),
                   full3, pl.BlockSpec((8, 128), lambda i: (0, 0)), acc8, acc8),
        scratch=[pltpu.VMEM((N_GROUP, 128, 128), F32)],
        args=(h, h, dya, wm, wmt, bsb, gv, bv, maskf))


def _tri_masks():
    row = lax.broadcasted_iota(jnp.int32, (CHUNK, CHUNK), 0)
    col = lax.broadcasted_iota(jnp.int32, (CHUNK, CHUNK), 1)
    return col <= row, col >= row


def _heads(v):
    return [v[:, hd * HEAD_DIM:(hd + 1) * HEAD_DIM] for hd in range(N_HEAD)]


def _tri_cumsum(tri_bf, v):
    hi = v.astype(BF16)
    r = v - hi.astype(F32)
    mid = r.astype(BF16)
    lo = (r - mid.astype(F32)).astype(BF16)
    return _dot(tri_bf, hi) + _dot(tri_bf, mid) + _dot(tri_bf, lo)


def _hgrn_chunk(q, fp, ii, lb, st_heads, causal):
    sg = _sig(fp)
    f = lb + (1.0 - lb) * sg
    k = 1.0 - f
    c = _tri_cumsum(causal.astype(BF16), jnp.log(f))
    ec = jnp.exp(c)
    en = jnp.exp(-c)
    sq = _sig(q)
    qt = q * sq * ec
    kt = k * en
    ecl = jnp.exp(c[CHUNK - 1:CHUNK, :])
    kk = kt * ecl
    qtb, ktb, iib, kkb = qt.astype(BF16), kt.astype(BF16), ii.astype(BF16), kk.astype(BF16)
    attn, o = [], []
    for hd, (qh, kh, ih) in enumerate(zip(_heads(qtb), _heads(ktb), _heads(iib))):
        a = jnp.where(causal, _dot(qh, kh, NT), 0.0).astype(BF16)
        attn.append(a)
        o.append(_dot(a, ih) + _dot(qh, st_heads[hd], NT))
    return dict(sg=sg, f=f, k=k, ec=ec, en=en, sq=sq, ecl=ecl, kk=kk, qtb=qtb, ktb=ktb, iib=iib,
                kkb=kkb, attn=attn, o=o)


def _rms_heads(o_heads):
    rinv = [lax.rsqrt(jnp.mean(o * o, axis=-1, keepdims=True) + RMS_EPS) for o in o_heads]
    return rinv, jnp.concatenate([o * r for o, r in zip(o_heads, rinv)], axis=1)


def _hgrn_fwd(h, logits, gn, comm=None):
    t = h.shape[0]
    nc = t // CHUNK

    def body(q_ref, f_ref, i_ref, og_ref, lg_ref, gn_ref, yb_ref, st_ref, state):
        ci = pl.program_id(0)

        @pl.when(ci == 0)
        def _():
            state[...] = jnp.zeros_like(state)

        causal, _ = _tri_masks()
        st = [state[hd] for hd in range(N_HEAD)]
        og = og_ref[...].astype(F32)
        lb = _sig(lg_ref[0:1, :] - lg_ref[1:2, :])
        r = _hgrn_chunk(q_ref[...].astype(F32), f_ref[...].astype(F32), i_ref[...].astype(F32), lb,
                        [s.astype(BF16) for s in st], causal)
        _, on = _rms_heads(r["o"])
        ecl = _heads(r["ecl"])
        new = [s * e + _dot(ih, kh, TN)
               for s, e, ih, kh in zip(st, ecl, _heads(r["iib"]), _heads(r["kkb"]))]
        yb_ref[...] = (on * gn_ref[...] * (og * _sig(og))).astype(BF16)
        for hd in range(N_HEAD):
            st_ref[0, hd] = st[hd]
            state[hd] = new[hd]

    def col(k):
        return pl.BlockSpec((CHUNK, D_MODEL), lambda ci: (ci, k))

    return _grid1_call(body, comm, nc, name="hgrn_fwd",
                       out_shape=(jax.ShapeDtypeStruct((t, D_MODEL), BF16),
                                  jax.ShapeDtypeStruct((nc, N_HEAD, HEAD_DIM, HEAD_DIM), F32)),
                       in_specs=[col(2), col(3), col(4), col(5),
                                 pl.BlockSpec((2, D_MODEL), lambda ci: (0, 0)),
                                 pl.BlockSpec((1, D_MODEL), lambda ci: (0, 0))],
                       out_specs=(pl.BlockSpec((CHUNK, D_MODEL), lambda ci: (ci, 0)),
                                  pl.BlockSpec((1, N_HEAD, HEAD_DIM, HEAD_DIM), lambda ci: (ci, 0, 0, 0))),
                       scratch=[pltpu.VMEM((N_HEAD, HEAD_DIM, HEAD_DIM), F32)],
                       args=(h, h, h, h, logits, gn))


def _hgrn_bwd(h, dyb, st_all, logits, gn, comm=None):
    t = h.shape[0]
    nc = t // CHUNK

    def body(q_ref, f_ref, i_ref, og_ref, dyb_ref, st_ref, lg_ref, gn_ref,
             dh1_ref, dh2_ref, dlb_ref, dgn_ref, dstate):
        ci = pl.program_id(0)

        @pl.when(ci == 0)
        def _():
            dstate[...] = jnp.zeros_like(dstate)
            dlb_ref[...] = jnp.zeros_like(dlb_ref)
            dgn_ref[...] = jnp.zeros_like(dgn_ref)

        causal, anti = _tri_masks()
        q, og = q_ref[...].astype(F32), og_ref[...].astype(F32)
        dy, gnv = dyb_ref[...], gn_ref[...]
        lb = _sig(lg_ref[0:1, :] - lg_ref[1:2, :])
        st = [st_ref[0, hd] for hd in range(N_HEAD)]
        dsn = [dstate[hd] for hd in range(N_HEAD)]
        stb = [s.astype(BF16) for s in st]
        dsnb = [s.astype(BF16) for s in dsn]
        r = _hgrn_chunk(q, f_ref[...].astype(F32), i_ref[...].astype(F32), lb, stb, causal)
        rinv, on = _rms_heads(r["o"])
        so = _sig(og)
        sil = og * so
        d_og = dy * on * gnv * (so * (1.0 + og * (1.0 - so)))
        d_on = dy * gnv * sil
        d_ob = jnp.concatenate(
            [ri * (dn - oh * jnp.mean(dn * oh, axis=-1, keepdims=True))
             for ri, dn, oh in zip(rinv, _heads(d_on), _heads(on))], axis=1).astype(BF16)
        d_i, d_qt, d_kt, d_kk, d_st, st_dsn = [], [], [], [], [], []
        ecl = _heads(r["ecl"])
        for hd, (dh, qh, kh, ih, kkh) in enumerate(zip(_heads(d_ob), _heads(r["qtb"]), _heads(r["ktb"]),
                                                       _heads(r["iib"]), _heads(r["kkb"]))):
            d_attn = jnp.where(causal, _dot(dh, ih, NT), 0.0).astype(BF16)
            d_i.append(_dot(r["attn"][hd], dh, TN) + _dot(kkh, dsnb[hd], NT))
            d_qt.append(_dot(d_attn, kh) + _dot(dh, stb[hd]))
            d_kt.append(_dot(d_attn, qh, TN))
            d_kk.append(_dot(ih, dsnb[hd]))
            d_st.append(_dot(dh, qh, TN) + dsn[hd] * ecl[hd])
            st_dsn.append(jnp.sum(st[hd] * dsn[hd], axis=0, keepdims=True))
        d_qt = jnp.concatenate(d_qt, axis=1)
        d_kt = jnp.concatenate(d_kt, axis=1)
        d_kk = jnp.concatenate(d_kk, axis=1)
        kk = r["kk"]
        d_cl = (r["ecl"] * jnp.concatenate(st_dsn, axis=1)
                + jnp.sum(kk * d_kk, axis=0, keepdims=True))
        d_k = (d_kk * r["ecl"] + d_kt) * r["en"]
        d_c = d_qt * r["qtb"].astype(F32) - d_kt * r["ktb"].astype(F32) - d_kk * kk
        rowid = lax.broadcasted_iota(jnp.int32, (CHUNK, D_MODEL), 0)
        d_c = d_c + jnp.where(rowid == CHUNK - 1, d_cl, 0.0)
        d_lf = _tri_cumsum(anti.astype(BF16), d_c)
        d_f = d_lf / r["f"] - d_k
        sg = r["sg"]
        sq = r["sq"]
        dgn_ref[...] += _colsum8(dy * on * sil)
        dlb_ref[...] += _colsum8(d_f * (1.0 - sg))
        dh1_ref[:, :D_MODEL] = (d_qt * r["ec"] * (sq * (1.0 + q * (1.0 - sq)))).astype(BF16)
        dh1_ref[:, D_MODEL:] = (d_f * (1.0 - lb) * sg * (1.0 - sg)).astype(BF16)
        dh2_ref[:, :D_MODEL] = jnp.concatenate(d_i, axis=1).astype(BF16)
        dh2_ref[:, D_MODEL:] = d_og.astype(BF16)
        for hd in range(N_HEAD):
            dstate[hd] = d_st[hd]

    def col(k):
        return pl.BlockSpec((CHUNK, D_MODEL), lambda ci: (nc - 1 - ci, k))

    acc8 = pl.BlockSpec((8, D_MODEL), lambda ci: (0, 0))
    pair = pl.BlockSpec((CHUNK, 2 * D_MODEL), lambda ci: (nc - 1 - ci, 0))
    return _grid1_call(body, comm, nc, name="hgrn_bwd",
                       out_shape=(jax.ShapeDtypeStruct((t, 2 * D_MODEL), BF16),
                                  jax.ShapeDtypeStruct((t, 2 * D_MODEL), BF16),
                                  jax.ShapeDtypeStruct((8, D_MODEL), F32),
                                  jax.ShapeDtypeStruct((8, D_MODEL), F32)),
                       in_specs=[col(2), col(3), col(4), col(5),
                                 pl.BlockSpec((CHUNK, D_MODEL), lambda ci: (nc - 1 - ci, 0)),
                                 pl.BlockSpec((1, N_HEAD, HEAD_DIM, HEAD_DIM),
                                              lambda ci: (nc - 1 - ci, 0, 0, 0)),
                                 pl.BlockSpec((2, D_MODEL), lambda ci: (0, 0)),
                                 pl.BlockSpec((1, D_MODEL), lambda ci: (0, 0))],
                       out_specs=(pair, pair, acc8, acc8),
                       scratch=[pltpu.VMEM((N_HEAD, HEAD_DIM, HEAD_DIM), F32)],
                       args=(h, h, h, h, dyb, st_all, logits, gn))


def _mix_fwd(ya, yb, h, x, wb0, wb1, wo, g1, b1, tm, comm=None):
    t = x.shape[0]

    def body(ya_ref, yb_ref, ga_ref, gb_ref, x_ref, wb0_ref, wb1_ref, wo_ref, g1_ref, b1_ref,
             r1_ref, a_ref, b_ref, m_ref, x1_ref):
        a = _dot(ya_ref[...], wb0_ref[...])
        b = _dot(yb_ref[...], wb1_ref[...])
        m = _sig(ga_ref[...].astype(F32)) * a + _sig(gb_ref[...].astype(F32)) * b
        r1 = ALPHA * x_ref[...] + _dot(m, wo_ref[...])
        xh, _ = _ln_stats(r1)
        r1_ref[...] = r1
        a_ref[...] = a
        b_ref[...] = b
        m_ref[...] = m.astype(BF16)
        x1_ref[...] = (xh * g1_ref[...] + b1_ref[...]).astype(BF16)

    tile = pl.BlockSpec((tm, D_MODEL), lambda i: (i, 0))
    wsp = pl.BlockSpec((D_MODEL, D_MODEL), lambda i: (0, 0))
    vec = pl.BlockSpec((1, D_MODEL), lambda i: (0, 0))
    f32o = jax.ShapeDtypeStruct((t, D_MODEL), F32)
    bfo = jax.ShapeDtypeStruct((t, D_MODEL), BF16)
    return _grid1_call(body, comm, t // tm, name="mix_fwd", out_shape=(f32o, f32o, f32o, bfo, bfo),
                       in_specs=[tile, tile,
                                 pl.BlockSpec((tm, D_MODEL), lambda i: (i, 6)),
                                 pl.BlockSpec((tm, D_MODEL), lambda i: (i, 7)),
                                 tile, wsp, wsp, wsp, vec, vec],
                       out_specs=(tile, tile, tile, tile, tile),
                       scratch=[], args=(ya, yb, h, h, x, wb0, wb1, wo, g1, b1))


def _mix_bwd(dr1, h, a, b, wo, wb0, wb1, tm):
    t = dr1.shape[0]

    def body(dr1_ref, ga_ref, gb_ref, a_ref, b_ref, wo_ref, wb0_ref, wb1_ref,
             da_ref, db_ref, dh3_ref, dya_ref, dyb_ref):
        d_m = _dot(dr1_ref[...], wo_ref[...], NT)
        sa = _sig(ga_ref[...].astype(F32))
        sb = _sig(gb_ref[...].astype(F32))
        d_a = (d_m * sa).astype(BF16)
        d_b = (d_m * sb).astype(BF16)
        da_ref[...] = d_a
        db_ref[...] = d_b
        dh3_ref[:, :D_MODEL] = (d_m * a_ref[...] * sa * (1.0 - sa)).astype(BF16)
        dh3_ref[:, D_MODEL:] = (d_m * b_ref[...] * sb * (1.0 - sb)).astype(BF16)
        dya_ref[...] = _dot(d_a, wb0_ref[...], NT)
        dyb_ref[...] = _dot(d_b, wb1_ref[...], NT)

    tile = pl.BlockSpec((tm, D_MODEL), lambda i: (i, 0))
    wsp = pl.BlockSpec((D_MODEL, D_MODEL), lambda i: (0, 0))
    f32o = jax.ShapeDtypeStruct((t, D_MODEL), F32)
    bfo = jax.ShapeDtypeStruct((t, D_MODEL), BF16)
    return _pc(body, name="mix_bwd",
               out_shape=(bfo, bfo, jax.ShapeDtypeStruct((t, 2 * D_MODEL), BF16), f32o, f32o),
               grid=(t // tm,),
               in_specs=[tile,
                         pl.BlockSpec((tm, D_MODEL), lambda i: (i, 6)),
                         pl.BlockSpec((tm, D_MODEL), lambda i: (i, 7)),
                         tile, tile, wsp, wsp, wsp],
               out_specs=(tile, tile, pl.BlockSpec((tm, 2 * D_MODEL), lambda i: (i, 0)),
                          tile, tile),
               sem=("parallel",))(dr1, h, h, a, b, wo, wb0, wb1)


FF_TILE = 1408
FF_NJ = D_FF // FF_TILE


def _shift_down(v, k):
    return pltpu.roll(v, k, 0)


def _shift_up(v, k):
    return pltpu.roll(v, v.shape[0] - k, 0)


def _conv_gate(ext, cw_ref, cb_ref):
    return (cw_ref[0:1, :] * _shift_down(ext, 2) + cw_ref[1:2, :] * _shift_down(ext, 1)
            + cw_ref[2:3, :] * ext + cb_ref[...])


HALO = 16


def _ffn_act_fwd(h2, convw, convb, tm):
    t = h2.shape[0]
    nth = tm // HALO

    def body(g_ref, gp_ref, v_ref, cw_ref, cb_ref, act_ref):
        i = pl.program_id(1)
        prev = gp_ref[...].astype(F32) * (i > 0).astype(F32)
        ext = jnp.concatenate([prev, g_ref[...].astype(F32)], axis=0)
        gc = _conv_gate(ext, cw_ref, cb_ref)[HALO:, :]
        act_ref[...] = (_gelu(gc) * v_ref[...].astype(F32)).astype(BF16)

    return _pc(body, name="ffn_act_fwd", out_shape=jax.ShapeDtypeStruct((t, D_FF), BF16),
               grid=(FF_NJ, t // tm),
               in_specs=[pl.BlockSpec((tm, FF_TILE), lambda j, i: (i, j)),
                         pl.BlockSpec((HALO, FF_TILE), lambda j, i: (jnp.maximum(i * nth - 1, 0), j)),
                         pl.BlockSpec((tm, FF_TILE), lambda j, i: (i, j + FF_NJ)),
                         pl.BlockSpec((3, FF_TILE), lambda j, i: (0, j)),
                         pl.BlockSpec((1, FF_TILE), lambda j, i: (0, j))],
               out_specs=pl.BlockSpec((tm, FF_TILE), lambda j, i: (i, j)),
               sem=("parallel", "parallel"))(h2, h2, h2, convw, convb)


def _ffn_act_bwd(h2, dact, convw, convb, tm):
    t = h2.shape[0]
    nth = tm // HALO
    ni = t // tm
    last_halo = t // HALO - 1
    main_rows = slice(HALO, HALO + tm)

    def body(g_ref, gp_ref, gn_ref, v_ref, vn_ref, da_ref, dan_ref, cw_ref, cb_ref,
             dh2_ref, dcw_ref, dcb_ref):
        i = pl.program_id(1)

        @pl.when(i == 0)
        def _():
            dcw_ref[...] = jnp.zeros_like(dcw_ref)
            dcb_ref[...] = jnp.zeros_like(dcb_ref)

        zeros = jnp.zeros((HALO, FF_TILE), F32)
        da = da_ref[...].astype(F32)
        prev = gp_ref[...].astype(F32) * (i > 0).astype(F32)
        ext = jnp.concatenate([prev, g_ref[...].astype(F32), gn_ref[...].astype(F32)], axis=0)
        vext = jnp.concatenate([zeros, v_ref[...].astype(F32), vn_ref[...].astype(F32)], axis=0)
        dnext = dan_ref[...].astype(F32) * (i < ni - 1).astype(F32)
        dext = jnp.concatenate([zeros, da, dnext], axis=0)
        g2 = _shift_down(ext, 2)
        g1 = _shift_down(ext, 1)
        gc = cw_ref[0:1, :] * g2 + cw_ref[1:2, :] * g1 + cw_ref[2:3, :] * ext + cb_ref[...]
        gl, dgl = _gelu_and_grad(gc)
        d_gc = dext * vext * dgl
        d_gate = (cw_ref[2:3, :] * d_gc + cw_ref[1:2, :] * _shift_up(d_gc, 1)
                  + cw_ref[0:1, :] * _shift_up(d_gc, 2))
        dh2_ref[0] = d_gate[main_rows, :].astype(BF16)
        dh2_ref[1] = (da * gl[main_rows, :]).astype(BF16)
        dm = d_gc[main_rows, :]
        s0 = jnp.sum(dm * g2[main_rows, :], axis=0, keepdims=True)
        s1 = jnp.sum(dm * g1[main_rows, :], axis=0, keepdims=True)
        s2 = jnp.sum(dm * ext[main_rows, :], axis=0, keepdims=True)
        rowid = lax.broadcasted_iota(jnp.int32, (8, FF_TILE), 0)
        dcw_ref[...] += jnp.where(rowid == 0, s0, jnp.where(rowid == 1, s1,
                                                            jnp.where(rowid == 2, s2, 0.0)))
        dcb_ref[...] += _colsum8(dm)

    def prev8(off):
        return pl.BlockSpec((HALO, FF_TILE), lambda j, i: (jnp.maximum(i * nth - 1, 0), j + off))

    def next8(off):
        return pl.BlockSpec((HALO, FF_TILE), lambda j, i: (jnp.minimum((i + 1) * nth, last_halo), j + off))

    def main(off):
        return pl.BlockSpec((tm, FF_TILE), lambda j, i: (i, j + off))

    acc = pl.BlockSpec((8, FF_TILE), lambda j, i: (0, j))
    return _pc(body, name="ffn_act_bwd",
               out_shape=(jax.ShapeDtypeStruct((2, t, D_FF), BF16),
                          jax.ShapeDtypeStruct((8, D_FF), F32),
                          jax.ShapeDtypeStruct((8, D_FF), F32)),
               grid=(FF_NJ, ni),
               in_specs=[main(0), prev8(0), next8(0), main(FF_NJ), next8(FF_NJ),
                         pl.BlockSpec((tm, FF_TILE), lambda j, i: (i, j)),
                         next8(0),
                         pl.BlockSpec((3, FF_TILE), lambda j, i: (0, j)),
                         pl.BlockSpec((1, FF_TILE), lambda j, i: (0, j))],
               out_specs=(pl.BlockSpec((2, tm, FF_TILE), lambda j, i: (0, i, j)), acc, acc),
               sem=("parallel", "arbitrary"))(h2, h2, h2, h2, h2, dact, dact, convw, convb)


def _out_fwd_bwd(act, x1b, r1, p2, tgt, wd, wpg, wpp, g1, b1, g2, b2, tm):
    t = r1.shape[0]

    def body(act_ref, x1b_ref, r1_ref, p_ref, tgt_ref, wd_ref, wpg_ref, wpp_ref,
             g1_ref, b1_ref, g2_ref, b2_ref,
             dr2_ref, dpg_ref, dpp_ref, loss_ref, dg2_ref, db2_ref):
        i = pl.program_id(0)

        @pl.when(i == 0)
        def _():
            loss_ref[...] = jnp.zeros_like(loss_ref)
            dg2_ref[...] = jnp.zeros_like(dg2_ref)
            db2_ref[...] = jnp.zeros_like(db2_ref)

        ffn = _dot(act_ref[...], wd_ref[...])
        pg = _dot(x1b_ref[...], wpg_ref[...])
        pp = _dot(p_ref[...], wpp_ref[...])
        s = _sig(pg)
        xh1, _ = _ln_stats(r1_ref[...])
        x1 = xh1 * g1_ref[...] + b1_ref[...]
        r2 = ALPHA * x1 + ffn + s * pp
        xh2, rstd2 = _ln_stats(r2)
        g2v = g2_ref[...]
        diff = xh2 * g2v + b2_ref[...] - tgt_ref[...]
        part = jnp.sum(jnp.sum(diff * diff, axis=1, keepdims=True), axis=0, keepdims=True)
        loss_ref[...] += jnp.broadcast_to(part * (0.5 / D_MODEL), loss_ref.shape)
        dy = diff * (1.0 / D_MODEL)
        dg2_ref[...] += _colsum8(dy * xh2)
        db2_ref[...] += _colsum8(dy)
        dr2 = _ln_bwd(dy * g2v, xh2, rstd2)
        dr2_ref[...] = dr2
        dpg_ref[...] = (dr2 * pp * s * (1.0 - s)).astype(BF16)
        dpp_ref[...] = (dr2 * s).astype(BF16)

    tile = pl.BlockSpec((tm, D_MODEL), lambda i: (i, 0))
    vec = pl.BlockSpec((1, D_MODEL), lambda i: (0, 0))
    acc8 = pl.BlockSpec((8, D_MODEL), lambda i: (0, 0))
    acc_shape = jax.ShapeDtypeStruct((8, D_MODEL), F32)
    return _pc(body, name="out_fwd_bwd",
               out_shape=(jax.ShapeDtypeStruct((t, D_MODEL), F32),
                          jax.ShapeDtypeStruct((t, D_MODEL), BF16),
                          jax.ShapeDtypeStruct((t, D_MODEL), BF16),
                          acc_shape, acc_shape, acc_shape),
               grid=(t // tm,),
               in_specs=[pl.BlockSpec((tm, D_FF), lambda i: (i, 0)), tile, tile,
                         pl.BlockSpec((tm, PLE_DIM), lambda i: (i, 0)), tile,
                         pl.BlockSpec((D_FF, D_MODEL), lambda i: (0, 0)),
                         pl.BlockSpec((D_MODEL, D_MODEL), lambda i: (0, 0)),
                         pl.BlockSpec((PLE_DIM, D_MODEL), lambda i: (0, 0)),
                         vec, vec, vec, vec],
               out_specs=(tile, tile, tile, acc8, acc8, acc8),
               sem=("arbitrary",))(act, x1b, r1, p2, tgt, wd, wpg, wpp, g1, b1, g2, b2)


def _ffn_in_bwd(dh2, wup_st, dpg, wpg, dr2, r1, g1, tm):
    t = r1.shape[0]
    ni = t // tm

    def body(dh2_ref, wup_ref, dpg_ref, wpg_ref, dr2_ref, r1_ref, g1_ref,
             dr1_ref, dg1_ref, db1_ref, acc):
        i = pl.program_id(0)
        j = pl.program_id(1)

        @pl.when((i == 0) & (j == 0))
        def _():
            dg1_ref[...] = jnp.zeros_like(dg1_ref)
            db1_ref[...] = jnp.zeros_like(db1_ref)

        @pl.when(j == 0)
        def _():
            acc[...] = _dot(dh2_ref[...], wup_ref[...], NT)

        @pl.when(j > 0)
        def _():
            acc[...] += _dot(dh2_ref[...], wup_ref[...], NT)

        @pl.when(j == N_CHIP - 1)
        def _():
            d_x1 = acc[...] + _dot(dpg_ref[...], wpg_ref[...], NT) + ALPHA * dr2_ref[...]
            xh, rstd = _ln_stats(r1_ref[...])
            dg1_ref[...] += _colsum8(d_x1 * xh)
            db1_ref[...] += _colsum8(d_x1)
            dr1_ref[...] = _ln_bwd(d_x1 * g1_ref[...], xh, rstd)

    tile = pl.BlockSpec((tm, D_MODEL), lambda i, j: (i, 0))
    acc8 = pl.BlockSpec((8, D_MODEL), lambda i, j: (0, 0))
    acc_shape = jax.ShapeDtypeStruct((8, D_MODEL), F32)
    return _pc(body, name="ffn_in_bwd",
               out_shape=(jax.ShapeDtypeStruct((t, D_MODEL), F32), acc_shape, acc_shape),
               grid=(ni, N_CHIP),
               in_specs=[pl.BlockSpec((None, tm, FF_TILE), lambda i, j: (j // FF_NJ, i, j % FF_NJ)),
                         pl.BlockSpec((None, D_MODEL, FF_TILE), lambda i, j: (j, 0, 0)),
                         tile, pl.BlockSpec((D_MODEL, D_MODEL), lambda i, j: (0, 0)),
                         tile, tile, pl.BlockSpec((1, D_MODEL), lambda i, j: (0, 0))],
               out_specs=(tile, acc8, acc8),
               scratch=[pltpu.VMEM((tm, D_MODEL), F32)],
               sem=("arbitrary", "arbitrary"))(dh2, wup_st, dpg, wpg, dr2, r1, g1)


ANY = pl.BlockSpec(memory_space=pl.ANY)


def _chip_peers():
    x, y, c = lax.axis_index("x"), lax.axis_index("y"), lax.axis_index("c")
    return x, y, c, [(1 - x, y), (x, 1 - y), (1 - x, 1 - y)]


def _gather_comm(halved, whole=()):
    n, nw = len(halved), len(whole)

    def copies(ins, outs, sems):
        ici_send, ici_recv, d2d_send, d2d_recv, own_send, own_recv = sems
        x, y, c, peers = _chip_peers()
        me = 2 * x + y
        sibling = (x, y, 1 - c)
        own, ici, ici_wait, fwd, fwd_wait = [], [], [], [], []
        for ti in range(n + nw):
            src, dst = ins[ti], outs[ti]
            own.append(pltpu.make_async_remote_copy(
                src_ref=src, dst_ref=dst.at[me], send_sem=own_send.at[ti], recv_sem=own_recv.at[ti],
                device_id=sibling, device_id_type=MESH))
            for k, (px, py) in enumerate(peers):
                pk = 2 * px + py
                sem = dict(send_sem=ici_send.at[ti * 3 + k], recv_sem=ici_recv.at[ti * 3 + k],
                           device_id=(px, py, c), device_id_type=MESH)
                if ti < n:
                    ici.append(pltpu.make_async_remote_copy(src_ref=src.at[c], dst_ref=dst.at[me, c], **sem))
                    ici_wait.append(pltpu.make_async_remote_copy(src_ref=src.at[c], dst_ref=dst.at[pk, c], **sem))
                    dsem = dict(send_sem=d2d_send.at[ti * 3 + k], recv_sem=d2d_recv.at[ti * 3 + k],
                                device_id=sibling, device_id_type=MESH)
                    fwd.append(pltpu.make_async_remote_copy(src_ref=dst.at[pk, c], dst_ref=dst.at[pk, c], **dsem))
                    fwd_wait.append(pltpu.make_async_remote_copy(
                        src_ref=dst.at[pk, 1 - c], dst_ref=dst.at[pk, 1 - c], **dsem))
                else:
                    ici.append(pltpu.make_async_remote_copy(src_ref=src, dst_ref=dst.at[me], **sem))
                    ici_wait.append(pltpu.make_async_remote_copy(src_ref=src, dst_ref=dst.at[pk], **sem))
        return own, ici, ici_wait, fwd, fwd_wait

    def start(ins, outs, sems):
        own, ici, _, _, _ = copies(ins, outs, sems)
        for cp in own + ici:
            cp.start()

    def finish(ins, outs, sems):
        own, ici, ici_wait, fwd, fwd_wait = copies(ins, outs, sems)
        for i, cp in enumerate(ici_wait):
            cp.wait_recv()
            if i < len(fwd):
                fwd[i].start()
        for cp in fwd_wait + own:
            cp.wait_recv()
        for cp in own + ici + fwd:
            cp.wait_send()

    srcs = list(halved) + list(whole)
    return _Comm(srcs, [jax.ShapeDtypeStruct((N_CHIP,) + s.shape, s.dtype) for s in srcs],
                 [pltpu.SemaphoreType.DMA((3 * (n + nw),)), pltpu.SemaphoreType.DMA((3 * (n + nw),)),
                  pltpu.SemaphoreType.DMA((max(3 * n, 1),)), pltpu.SemaphoreType.DMA((max(3 * n, 1),)),
                  pltpu.SemaphoreType.DMA((n + nw,)), pltpu.SemaphoreType.DMA((n + nw,))],
                 start, finish)


def _sibling_exchange_comm(grads):
    n = len(grads)

    def copies(ins, outs, sems):
        send_sems, recv_sems = sems
        x, y, c = lax.axis_index("x"), lax.axis_index("y"), lax.axis_index("c")
        res = []
        for ti in range(n):
            half = ins[ti].shape[1] // 2
            res.append(pltpu.make_async_remote_copy(
                src_ref=ins[ti].at[:, pl.ds(pl.multiple_of((1 - c) * half, 8), half), :],
                dst_ref=outs[ti],
                send_sem=send_sems.at[ti], recv_sem=recv_sems.at[ti],
                device_id=(x, y, 1 - c), device_id_type=MESH))
        return res

    def start(ins, outs, sems):
        for cp in copies(ins, outs, sems):
            cp.start()

    def finish(ins, outs, sems):
        for cp in copies(ins, outs, sems):
            cp.wait()

    return _Comm(grads, [jax.ShapeDtypeStruct((N_CHIP, g.shape[1] // 2, g.shape[2]), g.dtype) for g in grads],
                 [pltpu.SemaphoreType.DMA((n,)), pltpu.SemaphoreType.DMA((n,))], start, finish)


def _in_proj_gathering(x2b, own, chip, tm):
    t = x2b.shape[0]
    ni = t // tm
    half, cols = own.shape[1], own.shape[2]

    def body(chip_ref, x_ref, own_ref, own_hbm, h_ref, win_out,
             w_scr, ici_send, ici_recv, d2d_send, d2d_recv, own_sems, ld_sems):
        s, i = pl.program_id(0), pl.program_id(1)
        x, y, c, peers = _chip_peers()
        me = 2 * x + y
        sibling = (x, y, 1 - c)

        def ici(k, slot):
            px, py = peers[k]
            return pltpu.make_async_remote_copy(
                src_ref=own_hbm.at[c], dst_ref=win_out.at[slot, c],
                send_sem=ici_send.at[k], recv_sem=ici_recv.at[k],
                device_id=(px, py, c), device_id_type=MESH)

        def forward(k, core):
            pk = 2 * peers[k][0] + peers[k][1]
            return pltpu.make_async_remote_copy(
                src_ref=win_out.at[pk, core], dst_ref=win_out.at[pk, core],
                send_sem=d2d_send.at[k], recv_sem=d2d_recv.at[k],
                device_id=sibling, device_id_type=MESH)

        place_own = pltpu.make_async_remote_copy(
            src_ref=own_hbm, dst_ref=win_out.at[me], send_sem=own_sems.at[0], recv_sem=own_sems.at[1],
            device_id=sibling, device_id_type=MESH)

        @pl.when((s == 0) & (i == 0))
        def _():
            for k in range(3):
                ici(k, me).start()
            place_own.start()

        @pl.when(s == 0)
        def _():
            xv = x_ref[...]
            h_ref[...] = (_dot(xv[:, :half], own_ref[0]) + _dot(xv[:, half:], own_ref[1])).astype(BF16)

        for k in range(3):
            @pl.when((s == k + 1) & (i == 0))
            def _(k=k):
                pk = 2 * peers[k][0] + peers[k][1]
                ici(k, pk).wait_recv()
                forward(k, c).start()
                forward(k, 1 - c).wait_recv()
                loads = [pltpu.make_async_copy(win_out.at[pk, hh], w_scr.at[hh], ld_sems.at[hh])
                         for hh in range(2)]
                for ld in loads:
                    ld.start()
                for ld in loads:
                    ld.wait()

        @pl.when(s > 0)
        def _():
            xv = x_ref[...]
            h_ref[...] = (_dot(xv[:, :half], w_scr[0]) + _dot(xv[:, half:], w_scr[1])).astype(BF16)

        @pl.when((s == N_CHIP - 1) & (i == ni - 1))
        def _():
            place_own.wait()
            for k in range(3):
                ici(k, me).wait_send()
                forward(k, c).wait_send()

    def shard_col(s, me):
        return jnp.where(s == 0, me, me ^ jnp.where(s == 1, 2, jnp.where(s == 2, 1, 3)))

    res = _pc(body, name="in_proj",
              out_shape=(jax.ShapeDtypeStruct((t, N_CHIP * cols), BF16),
                         jax.ShapeDtypeStruct((N_CHIP,) + own.shape, own.dtype)),
              grid=(N_CHIP, ni), nsp=1,
              in_specs=[pl.BlockSpec((tm, 2 * half), lambda s, i, chip_ref: (i, 0)),
                        pl.BlockSpec(own.shape, lambda s, i, chip_ref: (0, 0, 0)),
                        ANY],
              out_specs=(pl.BlockSpec((tm, cols), lambda s, i, chip_ref: (i, shard_col(s, chip_ref[0]))),
                         ANY),
              scratch=[pltpu.VMEM(own.shape, own.dtype),
                       pltpu.SemaphoreType.DMA((3,)), pltpu.SemaphoreType.DMA((3,)),
                       pltpu.SemaphoreType.DMA((3,)), pltpu.SemaphoreType.DMA((3,)),
                       pltpu.SemaphoreType.DMA((2,)), pltpu.SemaphoreType.DMA((2,))],
              sem=("arbitrary", "arbitrary"))(chip, x2b, own, own)
    return res[0], res[1]


def _rs_add_halves(name, grad, recv, core):
    _, r, cdim = grad.shape
    half = r // 2
    tr = _row_tile(half, cdim, mult=16)
    nr = half // tr

    def body(c_ref, g_ref, r_ref, o_ref):
        o_ref[...] = (g_ref[...] + r_ref[...]).astype(BF16)

    return _pc(body, name=name, out_shape=jax.ShapeDtypeStruct((N_CHIP, half, cdim), BF16),
               grid=(N_CHIP, nr), nsp=1,
               in_specs=[pl.BlockSpec((None, tr, cdim), lambda j, i, c_ref: (j, c_ref[0] * nr + i, 0)),
                         pl.BlockSpec((None, tr, cdim), lambda j, i, c_ref: (j, i, 0))],
               out_specs=pl.BlockSpec((None, tr, cdim), lambda j, i, c_ref: (j, i, 0)),
               sem=("parallel", "parallel"))(core, grad, recv)


def _chip_exchange_comm(parts):
    n = len(parts)

    def copies(ins, outs, sems):
        send_sems, recv_sems = sems
        x, y, c, peers = _chip_peers()
        return [pltpu.make_async_remote_copy(
            src_ref=ins[ti].at[2 * px + py], dst_ref=outs[ti].at[k],
            send_sem=send_sems.at[ti * 3 + k], recv_sem=recv_sems.at[ti * 3 + k],
            device_id=(px, py, c), device_id_type=MESH)
            for ti in range(n) for k, (px, py) in enumerate(peers)]

    def start(ins, outs, sems):
        for cp in copies(ins, outs, sems):
            cp.start()

    def finish(ins, outs, sems):
        for cp in copies(ins, outs, sems):
            cp.wait()

    return _Comm(parts, [jax.ShapeDtypeStruct((3,) + p.shape[1:], p.dtype) for p in parts],
                 [pltpu.SemaphoreType.DMA((3 * n,)), pltpu.SemaphoreType.DMA((3 * n,))], start, finish)


def _rs_sum_chips(name, part, recv, chip):
    _, half, cdim = recv.shape
    tr = _row_tile(half, cdim, mult=16)

    def body(chip_ref, p_ref, r_ref, o_ref):
        o_ref[...] = ((p_ref[...].astype(F32) + r_ref[0].astype(F32)) + r_ref[1].astype(F32)
                      ) + r_ref[2].astype(F32)

    return _pc(body, name=name, out_shape=jax.ShapeDtypeStruct((half, cdim), F32),
               grid=(half // tr,), nsp=1,
               in_specs=[pl.BlockSpec((None, tr, cdim), lambda i, chip_ref: (chip_ref[0], i, 0)),
                         pl.BlockSpec((3, tr, cdim), lambda i, chip_ref: (0, i, 0))],
               out_specs=pl.BlockSpec((tr, cdim), lambda i, chip_ref: (i, 0)),
               sem=("parallel",))(chip, part, recv)


def _rs_send_halves(halves):
    n = len(halves)

    def body(*refs):
        ins, outs = refs[:n], refs[n:2 * n]
        send_sems, recv_sems = refs[2 * n:]
        x, y, c = lax.axis_index("x"), lax.axis_index("y"), lax.axis_index("c")
        sends = []
        for ti in range(n):
            cp = pltpu.make_async_remote_copy(
                src_ref=ins[ti], dst_ref=outs[ti],
                send_sem=send_sems.at[ti], recv_sem=recv_sems.at[ti],
                device_id=(x, y, 1 - c), device_id_type=MESH)
            cp.start()
            sends.append(cp)
        for cp in sends:
            cp.wait()

    return _pc(body, name="rs_send_halves",
               out_shape=tuple(jax.ShapeDtypeStruct(hv.shape, hv.dtype) for hv in halves),
               in_specs=[ANY] * n, out_specs=tuple([ANY] * n),
               scratch=[pltpu.SemaphoreType.DMA((n,)), pltpu.SemaphoreType.DMA((n,))])(*halves)


def _adamw_rows(name, mine, theirs, w, m, v, core):
    half, cdim = mine.shape
    tr = _row_tile(half, cdim, budget=1 << 19)
    nrh = half // tr

    def body(c_ref, mine_ref, theirs_ref, w_ref, m_ref, v_ref, g_ref, d_ref, m2_ref, v2_ref):
        is_mine = (pl.program_id(0) // nrh) == c_ref[0]
        g = jnp.where(is_mine, mine_ref[...], theirs_ref[...])
        d, m2, v2 = _adamw(w_ref[...], g, m_ref[...], v_ref[...])
        g_ref[...] = g
        d_ref[...] = d
        m2_ref[...] = m2
        v2_ref[...] = v2

    htile = pl.BlockSpec((tr, cdim), lambda i, c_ref: (i % nrh, 0))
    tile = pl.BlockSpec((tr, cdim), lambda i, c_ref: (i, 0))
    shp = jax.ShapeDtypeStruct((2 * half, cdim), F32)
    return _pc(body, name=name, out_shape=(shp, shp, shp, shp), grid=(2 * nrh,), nsp=1,
               in_specs=[htile, htile, tile, tile, tile], out_specs=(tile, tile, tile, tile),
               sem=("parallel",))(core, mine, theirs, w, m, v)


def _adamw_whole(name, g, w, m, v):
    def body(g_ref, w_ref, m_ref, v_ref, d_ref, m2_ref, v2_ref):
        d, m2, v2 = _adamw(w_ref[...], g_ref[...], m_ref[...], v_ref[...])
        d_ref[...] = d
        m2_ref[...] = m2
        v2_ref[...] = v2

    shp = jax.ShapeDtypeStruct(g.shape, F32)
    return _pc(body, name=name, out_shape=(shp, shp, shp))(g, w, m, v)


SMALL_LAYOUT = (
    ("sgu_w_s", 1024, 1, 0),
    ("sgu_b_s", 8, 1, 1024),
    ("sgu_norm_g", 1, 0, 0),
    ("sgu_norm_b", 1, 0, 1),
    ("hgrn_norm_g", 1, 0, 3),
    ("ln1_g", 1, 0, 4),
    ("ln1_b", 1, 0, 5),
    ("ffn_conv_b", 1, 2, 3),
    ("ln2_g", 1, 0, 6),
    ("ln2_b", 1, 0, 7),
)
LB_ROW = 2
LOSS_ROW = 8
PACK_SHAPES = ((16, D_MODEL), (N_GROUP * 128 + 8, 128), (8, D_FF))


def _small_allreduce_adamw(rows1024, dws, dbs, dcw, dcb, logits, m_logits, v_logits,
                           small_w, small_m, small_v):
    ns = len(SMALL_LAYOUT)
    nr = len(rows1024)
    nb = len(PACK_SHAPES)

    def body(*refs):
        row_refs = refs[:nr]
        dws_ref, dbs_ref, dcw_ref, dcb_ref, lg_ref, mlg_ref, vlg_ref = refs[nr:nr + 7]
        pos = nr + 7
        w_refs = refs[pos:pos + ns]
        m_refs = refs[pos + ns:pos + 2 * ns]
        v_refs = refs[pos + 2 * ns:pos + 3 * ns]
        pos += 3 * ns
        loss_ref, dcw_out = refs[pos:pos + 2]
        lg_outs = refs[pos + 2:pos + 6]
        pos += 6
        outs = refs[pos:pos + 4 * ns]
        pos += 4 * ns
        pack = refs[pos:pos + nb]
        sib = refs[pos + nb:pos + 2 * nb]
        gath = refs[pos + 2 * nb:pos + 3 * nb]
        d2d_send, d2d_recv, ici_send, ici_recv = refs[pos + 3 * nb:]

        x, y, c, peers = _chip_peers()
        me = 2 * x + y
        sibling = (x, y, 1 - c)

        pack[0][...] = jnp.zeros(PACK_SHAPES[0], F32)
        for k in range(nr):
            pack[0][k:k + 1, :] = row_refs[k][0:1, :]
        pack[1][0:N_GROUP * 128, :] = dws_ref[...]
        pack[1][N_GROUP * 128:, :] = dbs_ref[...]
        pack[2][...] = jnp.zeros(PACK_SHAPES[2], F32)
        pack[2][0:3, :] = dcw_ref[0:3, :]
        pack[2][3:4, :] = dcb_ref[0:1, :]

        d2d = [pltpu.make_async_remote_copy(
            src_ref=pack[b], dst_ref=sib[b], send_sem=d2d_send.at[b], recv_sem=d2d_recv.at[b],
            device_id=sibling, device_id_type=MESH) for b in range(nb)]
        for cp in d2d:
            cp.start()
        for cp in d2d:
            cp.wait()
        for b in range(nb):
            gath[b][me] = pack[b][...] + sib[b][...]

        ici, ici_wait = [], []
        for b in range(nb):
            for k, (px, py) in enumerate(peers):
                sem = dict(send_sem=ici_send.at[b * 3 + k], recv_sem=ici_recv.at[b * 3 + k],
                           device_id=(px, py, c), device_id_type=MESH)
                ici.append(pltpu.make_async_remote_copy(src_ref=gath[b].at[me], dst_ref=gath[b].at[me], **sem))
                ici_wait.append(pltpu.make_async_remote_copy(
                    src_ref=gath[b].at[me], dst_ref=gath[b].at[2 * px + py], **sem))
        for cp in ici:
            cp.start()
        for cp in ici_wait:
            cp.wait_recv()
        for cp in ici:
            cp.wait_send()

        tot = pack
        for b in range(nb):
            tot[b][...] = ((gath[b][0] + gath[b][1]) + gath[b][2]) + gath[b][3]

        loss_ref[...] = tot[0][LOSS_ROW:LOSS_ROW + 1, :]
        dcw_out[...] = tot[2][...]
        lb = _sig(lg_ref[0:1, :] - lg_ref[1:2, :])
        d0 = tot[0][LB_ROW:LB_ROW + 1, :] * lb * (1.0 - lb)
        rowid = lax.broadcasted_iota(jnp.int32, (2, D_MODEL), 0)
        g_lg = jnp.where(rowid == 0, d0, -d0)
        dl, ml, vl = _adamw(lg_ref[...], g_lg, mlg_ref[...], vlg_ref[...])
        lg_outs[0][...] = g_lg
        lg_outs[1][...] = dl
        lg_outs[2][...] = ml
        lg_outs[3][...] = vl
        for si, (_, rows, b, r0) in enumerate(SMALL_LAYOUT):
            g = tot[b][r0:r0 + rows, :]
            dl, ml, vl = _adamw(w_refs[si][...], g, m_refs[si][...], v_refs[si][...])
            outs[4 * si][...] = g
            outs[4 * si + 1][...] = dl
            outs[4 * si + 2][...] = ml
            outs[4 * si + 3][...] = vl

    shapes = [jax.ShapeDtypeStruct((1, D_MODEL), F32), jax.ShapeDtypeStruct((8, D_FF), F32)]
    shapes += [jax.ShapeDtypeStruct((2, D_MODEL), F32)] * 4
    for w in small_w:
        shapes += [jax.ShapeDtypeStruct(w.shape, F32)] * 4
    scratch = [pltpu.VMEM(shp, F32) for shp in PACK_SHAPES]
    scratch += [pltpu.VMEM(shp, F32) for shp in PACK_SHAPES]
    scratch += [pltpu.VMEM((N_CHIP,) + shp, F32) for shp in PACK_SHAPES]
    scratch += [pltpu.SemaphoreType.DMA((nb,)), pltpu.SemaphoreType.DMA((nb,)),
                pltpu.SemaphoreType.DMA((3 * nb,)), pltpu.SemaphoreType.DMA((3 * nb,))]
    vm = pl.BlockSpec(memory_space=pltpu.VMEM)
    n_in = nr + 7 + 3 * ns
    res = _pc(body, name="small_allreduce_adamw", out_shape=tuple(shapes),
              in_specs=[vm] * n_in, out_specs=tuple([vm] * len(shapes)),
              scratch=scratch)(*rows1024, dws, dbs, dcw, dcb, logits, m_logits, v_logits,
                               *small_w, *small_m, *small_v)
    return res[0], res[1], res[2:6], res[6:]


def kernel(x, p, w_in, sgu_w_s, sgu_b_s, sgu_norm_g, sgu_norm_b, hgrn_lb_logits, hgrn_norm_g, w_branch, w_out, ln1_g, ln1_b, ffn_w_up, ffn_conv_w, ffn_conv_b, ffn_w_down, ln2_g, ln2_b, ple_w_proj, ple_w_gate, loss_target, m_w_in, m_sgu_w_s, m_sgu_b_s, m_sgu_norm_g, m_sgu_norm_b, m_hgrn_lb_logits, m_hgrn_norm_g, m_w_branch, m_w_out, m_ln1_g, m_ln1_b, m_ffn_w_up, m_ffn_conv_w, m_ffn_conv_b, m_ffn_w_down, m_ln2_g, m_ln2_b, m_ple_w_proj, m_ple_w_gate, v_w_in, v_sgu_w_s, v_sgu_b_s, v_sgu_norm_g, v_sgu_norm_b, v_hgrn_lb_logits, v_hgrn_norm_g, v_w_branch, v_w_out, v_ln1_g, v_ln1_b, v_ffn_w_up, v_ffn_conv_w, v_ffn_conv_b, v_ffn_w_down, v_ln2_g, v_ln2_b, v_ple_w_proj, v_ple_w_gate):
    t = x.shape[1]
    x2 = x.reshape(t, D_MODEL)
    x2b = x2.astype(BF16)
    p2 = p.reshape(t, PLE_DIM)
    tgt = loss_target.reshape(t, D_MODEL)
    core = lax.axis_index("c").astype(jnp.int32).reshape(1)
    chip_id = (2 * lax.axis_index("x") + lax.axis_index("y")).astype(jnp.int32).reshape(1)

    big_w = [w_in[0], w_branch[0, 0], w_branch[0, 1], w_out[0], ffn_w_up[0], ffn_w_down[0],
             ple_w_proj[0], ple_w_gate[0]]
    big_m = [m_w_in[0], m_w_branch[0, 0], m_w_branch[0, 1], m_w_out[0], m_ffn_w_up[0],
             m_ffn_w_down[0], m_ple_w_proj[0], m_ple_w_gate[0]]
    big_v = [v_w_in[0], v_w_branch[0, 0], v_w_branch[0, 1], v_w_out[0], v_ffn_w_up[0],
             v_ffn_w_down[0], v_ple_w_proj[0], v_ple_w_gate[0]]
    def halves_of(i):
        w = big_w[i]
        return w.astype(BF16).reshape(2, w.shape[0] // 2, w.shape[1])

    def stacked(g, i):
        return g.reshape(N_CHIP, big_w[i].shape[0], big_w[i].shape[1])


    cid = jnp.arange(SGU_BLOCK) // CHUNK
    maskf = (cid[:, None] >= cid[None, :]).astype(F32)
    ws_masked = sgu_w_s[0] * maskf[None]
    wm = ws_masked.astype(BF16)
    wmt = jnp.transpose(ws_masked, (0, 2, 1)).astype(BF16)
    bsb = jnp.broadcast_to(sgu_b_s[0][:, :, None], (N_GROUP, SGU_BLOCK, 128))

    h, win_g = _in_proj_gathering(x2b, halves_of(0), chip_id, 512)
    win_st = stacked(win_g, 0)
    ya, mix_g = _sgu_fwd(h, wm, bsb, sgu_norm_g, sgu_norm_b,
                         comm=_gather_comm([halves_of(i) for i in (1, 2, 3)]))
    wb0, wb1, wo = [stacked(g, i).reshape(D_MODEL, D_MODEL) for g, i in zip(mix_g, (1, 2, 3))]
    (yb, st_all), (wup_g,) = _hgrn_fwd(h, hgrn_lb_logits, hgrn_norm_g, comm=_gather_comm([halves_of(4)]))
    wup_st = stacked(wup_g, 4)
    (r1, a_br, b_br, m_bf, x1b), out_g = _mix_fwd(
        ya, yb, h, x2, wb0, wb1, wo, ln1_g, ln1_b, 256,
        comm=_gather_comm([halves_of(i) for i in (5, 6, 7)], [ffn_conv_w[0]]))
    wd = stacked(out_g[0], 5).reshape(D_FF, D_MODEL)
    wpp = jnp.transpose(stacked(out_g[1], 6), (1, 0, 2)).reshape(PLE_DIM, D_MODEL)
    wpg = stacked(out_g[2], 7).reshape(D_MODEL, D_MODEL)
    convw = jnp.transpose(out_g[3], (1, 0, 2)).reshape(3, D_FF)
    h2 = _mm_nn_stacked("ffn_up", x1b, wup_st, 512)
    act = _ffn_act_fwd(h2, convw, ffn_conv_b, 256)
    dr2, dpg, dpp, loss_acc, dg2, db2 = _out_fwd_bwd(
        act, x1b, r1, p2, tgt, wd, wpg, wpp, ln1_g, ln1_b, ln2_g, ln2_b, 256)

    dact = _mm("ffn_down_bwd", dr2, wd, NT, (t // 512, FF_NJ, 1),
               pl.BlockSpec((512, D_MODEL), lambda i, j, k: (i, 0)),
               pl.BlockSpec((FF_TILE, D_MODEL), lambda i, j, k: (j, 0)),
               jax.ShapeDtypeStruct((t, D_FF), BF16),
               pl.BlockSpec((512, FF_TILE), lambda i, j, k: (i, j)))
    dh2, dcw, dcb = _ffn_act_bwd(h2, dact, convw, ffn_conv_b, 256)
    d_wd = _mm_tn("ffn_down_wgrad", act, dr2, FF_TILE, 512)
    d_wpg = _mm_tn("ple_gate_wgrad", x1b, dpg, 512, D_MODEL)
    d_wpp_st = _mm_tn("ple_proj_wgrad", p2, dpp, PLE_DIM, PLE_DIM, stacked=True)
    d_wup_st = _mm("ffn_up_wgrad", x1b, dh2, TN, (2, N_CHIP, 1),
                   pl.BlockSpec((t, 512), lambda i, j, k: (0, i)),
                   pl.BlockSpec((None, t, FF_TILE), lambda i, j, k: (j // FF_NJ, 0, j % FF_NJ)),
                   jax.ShapeDtypeStruct((N_CHIP, D_MODEL, FF_TILE), F32),
                   pl.BlockSpec((None, 512, FF_TILE), lambda i, j, k: (j, i, 0)))
    dr1, dg1, db1 = _ffn_in_bwd(dh2, wup_st, dpg, wpg, dr2, r1, ln1_g, 512)
    da_bf, db_bf, dh3, dya, dyb = _mix_bwd(dr1, h, a_br, b_br, wo, wb0, wb1, 256)
    d_wo = _mm_tn("out_proj_wgrad", m_bf, dr1, 512, 512)
    d_wb0 = _mm_tn("branch0_wgrad", ya, da_bf, 512, D_MODEL)
    d_wb1 = _mm_tn("branch1_wgrad", yb, db_bf, 512, D_MODEL)
    grads_1 = [d_wb0.reshape(4, 256, D_MODEL), d_wb1.reshape(4, 256, D_MODEL),
               d_wo.reshape(4, 256, D_MODEL), d_wup_st, d_wd.reshape(4, D_FF // 4, D_MODEL),
               d_wpp_st, d_wpg.reshape(4, 256, D_MODEL)]
    (dh0, dws, dbs, dgv, dbv), recv_a1 = _sgu_bwd(h, dya, wm, wmt, bsb, sgu_norm_g, sgu_norm_b, maskf,
                                                  comm=_sibling_exchange_comm(grads_1))
    parts_1 = [_rs_add_halves("rs_add_halves%d" % (i + 1), g, r, core)
               for i, (g, r) in enumerate(zip(grads_1, recv_a1))]
    (dh1, dh2h, dlb, dgn), recv_b1 = _hgrn_bwd(h, dyb, st_all, hgrn_lb_logits, hgrn_norm_g,
                                                comm=_chip_exchange_comm(parts_1))
    dh_parts = [dh0, dh1, dh2h, dh3]
    d_win = [_mm_tn("in_proj_wgrad%d" % j, x2b, dh_parts[j], 512, D_MODEL) for j in range(4)]

    grads_0 = [jnp.stack(d_win)]
    recv_a0 = _run_comm("rs_sibling_exchange0", _sibling_exchange_comm(grads_0))
    parts_0 = [_rs_add_halves("rs_add_halves0", grads_0[0], recv_a0[0], core)]
    gx, recv_b0 = _in_proj_xgrad(dh_parts, win_st, dr1, 512, comm=_chip_exchange_comm(parts_0))
    parts = parts_0 + parts_1
    recv_b = list(recv_b0) + list(recv_b1)
    halves = [_rs_sum_chips("rs_sum_chips%d" % i, pt, r, chip_id)
              for i, (pt, r) in enumerate(zip(parts, recv_b))]
    theirs = _rs_send_halves(halves)
    big_out = [_adamw_rows("adamw_big%d" % i, halves[i], theirs[i], big_w[i], big_m[i], big_v[i], core)
               for i in range(len(halves))]

    small_in = dict(sgu_w_s=(sgu_w_s, m_sgu_w_s, v_sgu_w_s), sgu_b_s=(sgu_b_s, m_sgu_b_s, v_sgu_b_s),
                    sgu_norm_g=(sgu_norm_g, m_sgu_norm_g, v_sgu_norm_g),
                    sgu_norm_b=(sgu_norm_b, m_sgu_norm_b, v_sgu_norm_b),
                    hgrn_norm_g=(hgrn_norm_g, m_hgrn_norm_g, v_hgrn_norm_g),
                    ln1_g=(ln1_g, m_ln1_g, v_ln1_g), ln1_b=(ln1_b, m_ln1_b, v_ln1_b),
                    ffn_conv_b=(ffn_conv_b, m_ffn_conv_b, v_ffn_conv_b),
                    ln2_g=(ln2_g, m_ln2_g, v_ln2_g), ln2_b=(ln2_b, m_ln2_b, v_ln2_b))

    def flat(name, arr):
        rows = dict((n, r) for n, r, _, _ in SMALL_LAYOUT)[name]
        return arr.reshape(rows, arr.size // rows)

    names = [n for n, _, _, _ in SMALL_LAYOUT]
    sw = [flat(n, small_in[n][0]) for n in names]
    sm = [flat(n, small_in[n][1]) for n in names]
    sv = [flat(n, small_in[n][2]) for n in names]
    loss_rows, dcw_tot, lg_out, small_out = _small_allreduce_adamw(
        [dgv, dbv, dlb, dgn, dg1, db1, dg2, db2, loss_acc], dws.reshape(N_GROUP * 128, 128), dbs, dcw, dcb,
        hgrn_lb_logits, m_hgrn_lb_logits, v_hgrn_lb_logits, sw, sm, sv)
    loss = loss_rows[0, 0]

    chip = 2 * lax.axis_index("x") + lax.axis_index("y")
    g_cw = lax.dynamic_slice(dcw_tot, (0, chip * (D_FF // 4)), (3, D_FF // 4))
    cw_out = _adamw_whole("adamw_conv_w", g_cw, ffn_conv_w[0], m_ffn_conv_w[0], v_ffn_conv_w[0])

    res = {}
    for si, n in enumerate(names):
        shp = small_in[n][0].shape
        res[n] = tuple(small_out[4 * si + k].reshape(shp) for k in range(4))
    res["hgrn_lb_logits"] = tuple(lg_out)
    res["ffn_conv_w"] = (g_cw[None],) + tuple(o[None] for o in cw_out)

    def big(i):
        return tuple(big_out[i])

    res["w_in"] = tuple(o[None] for o in big(0))
    res["w_branch"] = tuple(jnp.stack([o0, o1])[None] for o0, o1 in zip(big(1), big(2)))
    res["w_out"] = tuple(o[None] for o in big(3))
    res["ffn_w_up"] = tuple(o[None] for o in big(4))
    res["ffn_w_down"] = tuple(o[None] for o in big(5))
    res["ple_w_proj"] = tuple(o[None] for o in big(6))
    res["ple_w_gate"] = tuple(o[None] for o in big(7))

    order = ["w_in", "sgu_w_s", "sgu_b_s", "sgu_norm_g", "sgu_norm_b", "hgrn_lb_logits",
             "hgrn_norm_g", "w_branch", "w_out", "ln1_g", "ln1_b", "ffn_w_up", "ffn_conv_w",
             "ffn_conv_b", "ffn_w_down", "ln2_g", "ln2_b", "ple_w_proj", "ple_w_gate"]
    outs = [loss, gx.reshape(1, t, D_MODEL)]
    for k in range(4):
        outs += [res[n][k] for n in order]
    return tuple(outs)
```

```python
import functools

import jax
import jax.numpy as jnp
from jax import lax
from jax.experimental import pallas as pl
from jax.experimental.pallas import tpu as pltpu

F32 = jnp.float32
BF16 = jnp.bfloat16
HIGHEST = lax.Precision.HIGHEST
MESH = pl.DeviceIdType.MESH

D_MODEL = 1024
CHUNK = 64
SGU_BLOCK = 128
N_GROUP = 8
N_HEAD = 8
HEAD_DIM = 128
D_FF = 2816
PLE_DIM = 256
IN_COLS = 8192
LN_EPS = 1e-5
RMS_EPS = 1e-6
ALPHA = 2.0 ** 0.25
N_CHIP = 4
N_DEV = 8

ADAM_LR = 0.001
ADAM_B1 = 0.9
ADAM_B2 = 0.999
ADAM_EPS = 1e-08
ADAM_WD = 0.01
ADAM_STEP = 10

VMEM_LIMIT = 56 * 1024 * 1024

NN = (((1,), (0,)), ((), ()))
NT = (((1,), (1,)), ((), ()))
TN = (((0,), (0,)), ((), ()))


def _pc(body, *, name, out_shape, grid=None, in_specs=None, out_specs=None, scratch=(),
        sem=None, nsp=0, vmem=VMEM_LIMIT):
    params = dict(vmem_limit_bytes=vmem)
    if sem is not None:
        params["dimension_semantics"] = sem
    kw = dict(name=name, out_shape=out_shape, compiler_params=pltpu.CompilerParams(**params))
    if nsp:
        kw["grid_spec"] = pltpu.PrefetchScalarGridSpec(
            num_scalar_prefetch=nsp, grid=grid, in_specs=in_specs, out_specs=out_specs,
            scratch_shapes=list(scratch))
    else:
        if grid is not None:
            kw["grid"] = grid
        if in_specs is not None:
            kw["in_specs"] = in_specs
            kw["out_specs"] = out_specs
        kw["scratch_shapes"] = list(scratch)
    return pl.pallas_call(body, **kw)


def _dot(a, b, dims=NN):
    return lax.dot_general(a.astype(BF16), b.astype(BF16), dims, preferred_element_type=F32)


def _dot32(a, b, dims=NN):
    return lax.dot_general(a, b, dims, precision=HIGHEST, preferred_element_type=F32)


def _sig(x):
    return 1.0 / (1.0 + jnp.exp(-x))


_GC = 0.7978845608028654
_GA = 0.044715


def _gelu(x):
    return 0.5 * x * (1.0 + jnp.tanh(_GC * (x + _GA * x * x * x)))


def _gelu_and_grad(x):
    t = jnp.tanh(_GC * (x + _GA * x * x * x))
    g = 0.5 * x * (1.0 + t)
    dg = 0.5 * (1.0 + t) + 0.5 * x * (1.0 - t * t) * _GC * (1.0 + 3.0 * _GA * x * x)
    return g, dg


def _ln_stats(r):
    mu = jnp.mean(r, axis=-1, keepdims=True)
    xc = r - mu
    var = jnp.mean(xc * xc, axis=-1, keepdims=True)
    rstd = lax.rsqrt(var + LN_EPS)
    return xc * rstd, rstd


def _ln_bwd(dxh, xh, rstd):
    m1 = jnp.mean(dxh, axis=-1, keepdims=True)
    m2 = jnp.mean(dxh * xh, axis=-1, keepdims=True)
    return rstd * (dxh - m1 - xh * m2)


def _colsum8(v):
    return jnp.broadcast_to(jnp.sum(v, axis=0, keepdims=True), (8, v.shape[1]))


def _adamw(w, g, m, v):
    m2 = ADAM_B1 * m + (1.0 - ADAM_B1) * g
    v2 = ADAM_B2 * v + (1.0 - ADAM_B2) * (g * g)
    m_hat = m2 / (1.0 - ADAM_B1 ** ADAM_STEP)
    v_hat = v2 / (1.0 - ADAM_B2 ** ADAM_STEP)
    delta = -ADAM_LR * (m_hat / (jnp.sqrt(v_hat) + ADAM_EPS) + ADAM_WD * w)
    return delta, m2, v2


def _row_tile(rows, cols, itemsize=4, budget=1 << 20, mult=8):
    best = mult
    for tr in range(mult, rows + 1, mult):
        if rows % tr == 0 and tr * cols * itemsize <= budget:
            best = tr
    return best


def _mm(name, a, b, dims, grid, a_spec, b_spec, out_shape, o_spec, add=None, add_spec=None,
        add_scale=1.0, comm=None):
    nk = grid[2]
    has_add = add is not None
    out_dtype = out_shape.dtype

    def body(*refs):
        if has_add:
            a_ref, b_ref, add_ref, o_ref = refs[:4]
            rest = refs[4:]
        else:
            a_ref, b_ref, o_ref = refs[:3]
            add_ref = None
            rest = refs[3:]
        prod = _dot(a_ref[...], b_ref[...], dims)

        def finish(acc):
            if has_add:
                acc = acc + add_scale * add_ref[...]
            o_ref[...] = acc.astype(out_dtype)

        if nk == 1:
            finish(prod)
        else:
            acc_ref = rest[0]
            k = pl.program_id(2)

            @pl.when(k == 0)
            def _():
                acc_ref[...] = prod

            @pl.when(k > 0)
            def _():
                acc_ref[...] += prod

            @pl.when(k == nk - 1)
            def _():
                finish(acc_ref[...])

    in_specs = [a_spec, b_spec] + ([add_spec] if has_add else [])
    args = [a, b] + ([add] if has_add else [])
    scratch = []
    if nk > 1:
        blk = [d for d in o_spec.block_shape if d is not None]
        scratch = [pltpu.VMEM(tuple(blk), F32)]
    if comm is None:
        return _pc(body, name=name, out_shape=out_shape, grid=grid, in_specs=in_specs,
                   out_specs=o_spec, scratch=scratch,
                   sem=("parallel", "parallel", "arbitrary"))(*args)

    def first():
        return (pl.program_id(0) == 0) & (pl.program_id(1) == 0) & (pl.program_id(2) == 0)

    def last():
        return ((pl.program_id(0) == grid[0] - 1) & (pl.program_id(1) == grid[1] - 1)
                & (pl.program_id(2) == grid[2] - 1))

    res = _hosted_call(body, comm, first, last, name=name, out_shape=(out_shape,), grid=grid,
                       in_specs=in_specs, out_specs=(o_spec,), scratch=scratch,
                       sem=("arbitrary", "arbitrary", "arbitrary"), args=args)
    return res[0], res[1:]


class _Comm:
    def __init__(self, ins, out_shapes, sems, start, finish):
        self.ins, self.out_shapes, self.sems = list(ins), list(out_shapes), list(sems)
        self.start, self.finish = start, finish


def _hosted_call(body, comm, first, last, *, name, out_shape, grid, in_specs, out_specs, scratch, sem,
                 args):
    n_in, n_out, n_scr = len(in_specs), len(out_shape), len(scratch)
    nci, nco = len(comm.ins), len(comm.out_shapes)

    def wrapped(*refs):
        pos = n_in
        own_in, c_in = refs[:pos], refs[pos:pos + nci]
        pos += nci
        own_out, c_out = refs[pos:pos + n_out], refs[pos + n_out:pos + n_out + nco]
        pos += n_out + nco
        own_scr, c_sem = refs[pos:pos + n_scr], refs[pos + n_scr:]

        @pl.when(first())
        def _():
            comm.start(c_in, c_out, c_sem)

        body(*own_in, *own_out, *own_scr)

        @pl.when(last())
        def _():
            comm.finish(c_in, c_out, c_sem)

    return _pc(wrapped, name=name, out_shape=tuple(out_shape) + tuple(comm.out_shapes), grid=grid,
               in_specs=list(in_specs) + [ANY] * nci, out_specs=tuple(out_specs) + tuple([ANY] * nco),
               scratch=list(scratch) + comm.sems, sem=sem)(*args, *comm.ins)


def _grid1_call(body, comm, n, *, name, out_shape, in_specs, out_specs, scratch, args):
    if comm is None:
        return _pc(body, name=name, out_shape=out_shape, grid=(n,), in_specs=in_specs,
                   out_specs=out_specs, scratch=scratch, sem=("arbitrary",))(*args), ()
    res = _hosted_call(body, comm, lambda: pl.program_id(0) == 0, lambda: pl.program_id(0) == n - 1,
                       name=name, out_shape=out_shape, grid=(n,), in_specs=in_specs,
                       out_specs=out_specs, scratch=scratch, sem=("arbitrary",), args=args)
    return res[:len(out_shape)], res[len(out_shape):]


def _run_comm(name, comm):
    nci, nco = len(comm.ins), len(comm.out_shapes)

    def body(*refs):
        c_in, c_out, c_sem = refs[:nci], refs[nci:nci + nco], refs[nci + nco:]
        comm.start(c_in, c_out, c_sem)
        comm.finish(c_in, c_out, c_sem)

    return _pc(body, name=name, out_shape=tuple(comm.out_shapes), in_specs=[ANY] * nci,
               out_specs=tuple([ANY] * nco), scratch=comm.sems)(*comm.ins)


def _mm_nn_stacked(name, a, w_st, tm, comm=None):
    t, k = a.shape
    _, _, c = w_st.shape
    return _mm(name, a, w_st, NN, (t // tm, N_CHIP, 1),
               pl.BlockSpec((tm, k), lambda i, j, kk: (i, 0)),
               pl.BlockSpec((None, k, c), lambda i, j, kk: (j, 0, 0)),
               jax.ShapeDtypeStruct((t, N_CHIP * c), BF16),
               pl.BlockSpec((tm, c), lambda i, j, kk: (i, j)), comm=comm)


def _mm_tn(name, a, b, tm, tn, stacked=False):
    t, m = a.shape
    _, n = b.shape
    if stacked:
        assert tm == m
        out_shape = jax.ShapeDtypeStruct((n // tn, m, tn), F32)
        o_spec = pl.BlockSpec((None, tm, tn), lambda i, j, kk: (j, 0, 0))
    else:
        out_shape = jax.ShapeDtypeStruct((m, n), F32)
        o_spec = pl.BlockSpec((tm, tn), lambda i, j, kk: (i, j))
    return _mm(name, a, b, TN, (m // tm, n // tn, 1),
               pl.BlockSpec((t, tm), lambda i, j, kk: (0, i)),
               pl.BlockSpec((t, tn), lambda i, j, kk: (0, j)),
               out_shape, o_spec)


def _in_proj_xgrad(dh_parts, win_st, dr1, tm, comm=None):
    t = dr1.shape[0]
    ni = t // tm

    def body(a0, a1, a2, a3, b_ref, add_ref, o_ref, acc):
        j = pl.program_id(1)
        for jj, a_ref in enumerate((a0, a1, a2, a3)):
            @pl.when(j == jj)
            def _(jj=jj, a_ref=a_ref):
                prod = _dot(a_ref[...], b_ref[...], NT)
                if jj == 0:
                    acc[...] = prod + ALPHA * add_ref[...]
                elif jj < N_CHIP - 1:
                    acc[...] += prod
                else:
                    o_ref[...] = acc[...] + prod

    a_spec = pl.BlockSpec((tm, 2 * D_MODEL), lambda i, j: (i, 0))
    tile = pl.BlockSpec((tm, D_MODEL), lambda i, j: (i, 0))
    kw = dict(name="in_proj_xgrad", out_shape=(jax.ShapeDtypeStruct((t, D_MODEL), F32),),
              grid=(ni, N_CHIP),
              in_specs=[a_spec] * 4 + [pl.BlockSpec((None, D_MODEL, 2 * D_MODEL), lambda i, j: (j, 0, 0)),
                                       tile],
              out_specs=(tile,), scratch=[pltpu.VMEM((tm, D_MODEL), F32)],
              sem=("arbitrary", "arbitrary"))
    args = list(dh_parts) + [win_st, dr1]
    if comm is None:
        return _pc(body, **kw)(*args)[0], ()
    res = _hosted_call(body, comm,
                       lambda: (pl.program_id(0) == 0) & (pl.program_id(1) == 0),
                       lambda: (pl.program_id(0) == ni - 1) & (pl.program_id(1) == N_CHIP - 1),
                       args=args, **kw)
    return res[0], res[1:]


def _sgu_mixed(v, wm_ref, bsb_ref, gv, bv):
    gl, dgl = _gelu_and_grad(v)
    vh, rstd = _ln_stats(gl)
    vn = vh * gv + bv
    mixed = []
    for g in range(N_GROUP):
        sl = slice(g * 128, (g + 1) * 128)
        mixed.append(_dot(wm_ref[g], vn[:, sl]) + bsb_ref[g])
    return dgl, vh, rstd, vn, mixed


def _sgu_fwd(h, wm, bsb, gv, bv, comm=None):
    t = h.shape[0]

    def body(u_ref, v_ref, wm_ref, bsb_ref, gv_ref, bv_ref, ya_ref):
        u = u_ref[...].astype(F32)
        _, _, _, _, mixed = _sgu_mixed(v_ref[...].astype(F32), wm_ref, bsb_ref, gv_ref[...], bv_ref[...])
        gu = _gelu(u)
        for g in range(N_GROUP):
            sl = slice(g * 128, (g + 1) * 128)
            ya_ref[:, sl] = (gu[:, sl] * mixed[g]).astype(BF16)

    full3 = pl.BlockSpec((N_GROUP, 128, 128), lambda i: (0, 0, 0))
    vec = pl.BlockSpec((1, D_MODEL), lambda i: (0, 0))
    (ya,), extra = _grid1_call(
        body, comm, t // SGU_BLOCK, name="sgu_fwd",
        out_shape=(jax.ShapeDtypeStruct((t, D_MODEL), BF16),),
        in_specs=[pl.BlockSpec((SGU_BLOCK, D_MODEL), lambda i: (i, 0)),
                  pl.BlockSpec((SGU_BLOCK, D_MODEL), lambda i: (i, 1)),
                  full3, full3, vec, vec],
        out_specs=(pl.BlockSpec((SGU_BLOCK, D_MODEL), lambda i: (i, 0)),),
        scratch=[], args=(h, h, wm, bsb, gv, bv))
    return ya, extra


def _sgu_bwd(h, dya, wm, wmt, bsb, gv, bv, maskf, comm=None):
    t = h.shape[0]
    nb = t // SGU_BLOCK

    def body(u_ref, v_ref, dya_ref, wm_ref, wmt_ref, bsb_ref, gv_ref, bv_ref, mask_ref,
             dh_ref, dws_ref, dbs_ref, dgv_ref, dbv_ref, dmix_acc):
        i = pl.program_id(0)

        @pl.when(i == 0)
        def _():
            dws_ref[...] = jnp.zeros_like(dws_ref)
            dgv_ref[...] = jnp.zeros_like(dgv_ref)
            dbv_ref[...] = jnp.zeros_like(dbv_ref)
            dmix_acc[...] = jnp.zeros_like(dmix_acc)

        u = u_ref[...].astype(F32)
        gvv = gv_ref[...]
        dgl_v, vh, rstd, vn, mixed = _sgu_mixed(v_ref[...].astype(F32), wm_ref, bsb_ref, gvv, bv_ref[...])
        gu, dgl_u = _gelu_and_grad(u)
        dya_v = dya_ref[...]
        dvn_parts = []
        for g in range(N_GROUP):
            sl = slice(g * 128, (g + 1) * 128)
            d_y = dya_v[:, sl]
            dh_ref[:, sl] = (d_y * mixed[g] * dgl_u[:, sl]).astype(BF16)
            d_mixed = d_y * gu[:, sl]
            dmix_acc[g] += d_mixed
            dws_ref[g] += _dot(d_mixed, vn[:, sl], NT) * mask_ref[...]
            dvn_parts.append(_dot(wmt_ref[g], d_mixed))
        dvn = jnp.concatenate(dvn_parts, axis=1)
        dgv_ref[...] += _colsum8(dvn * vh)
        dbv_ref[...] += _colsum8(dvn)
        d_gl = _ln_bwd(dvn * gvv, vh, rstd)
        dh_ref[:, D_MODEL:] = (d_gl * dgl_v).astype(BF16)

        @pl.when(i == nb - 1)
        def _():
            rowid = lax.broadcasted_iota(jnp.int32, (8, 128), 0)
            ones = jnp.ones((8, 128), F32)
            acc = jnp.zeros((8, 128), F32)
            for g in range(N_GROUP):
                rs = _dot32(ones, dmix_acc[g], NT)
                acc = jnp.where(rowid == g, rs, acc)
            dbs_ref[...] = acc

    full3 = pl.BlockSpec((N_GROUP, 128, 128), lambda i: (0, 0, 0))
    vec = pl.BlockSpec((1, D_MODEL), lambda i: (0, 0))
    acc8 = pl.BlockSpec((8, D_MODEL), lambda i: (0, 0))
    return _grid1_call(
        body, comm, nb, name="sgu_bwd",
        out_shape=(jax.ShapeDtypeStruct((t, 2 * D_MODEL), BF16),
                   jax.ShapeDtypeStruct((N_GROUP, 128, 128), F32),
                   jax.ShapeDtypeStruct((8, 128), F32),
                   jax.ShapeDtypeStruct((8, D_MODEL), F32),
                   jax.ShapeDtypeStruct((8, D_MODEL), F32)),
        in_specs=[pl.BlockSpec((SGU_BLOCK, D_MODEL), lambda i: (i, 0)),
                  pl.BlockSpec((SGU_BLOCK, D_MODEL), lambda i: (i, 1)),
                  pl.BlockSpec((SGU_BLOCK, D_MODEL), lambda i: (i, 0)),
                  full3, full3, full3, vec, vec,
                  pl.BlockSpec((128, 128), lambda i: (0, 0))],
        out_specs=(pl.BlockSpec((SGU_BLOCK, 2 * D_MODEL), lambda i: (i, 0)),
                   full3, pl.BlockSpec((8, 128), lambda i: (0, 0)), acc8, acc8),
        scratch=[pltpu.VMEM((N_GROUP, 128, 128), F32)],
        args=(h, h, dya, wm, wmt, bsb, gv, bv, maskf))


def _tri_masks():
    row = lax.broadcasted_iota(jnp.int32, (CHUNK, CHUNK), 0)
    col = lax.broadcasted_iota(jnp.int32, (CHUNK, CHUNK), 1)
    return col <= row, col >= row


def _heads(v):
    return [v[:, hd * HEAD_DIM:(hd + 1) * HEAD_DIM] for hd in range(N_HEAD)]


def _tri_cumsum(tri_bf, v):
    hi = v.astype(BF16)
    r = v - hi.astype(F32)
    mid = r.astype(BF16)
    lo = (r - mid.astype(F32)).astype(BF16)
    return _dot(tri_bf, hi) + _dot(tri_bf, mid) + _dot(tri_bf, lo)


def _hgrn_chunk(q, fp, ii, lb, st_heads, causal):
    sg = _sig(fp)
    f = lb + (1.0 - lb) * sg
    k = 1.0 - f
    c = _tri_cumsum(causal.astype(BF16), jnp.log(f))
    ec = jnp.exp(c)
    en = jnp.exp(-c)
    sq = _sig(q)
    qt = q * sq * ec
    kt = k * en
    ecl = jnp.exp(c[CHUNK - 1:CHUNK, :])
    kk = kt * ecl
    qtb, ktb, iib, kkb = qt.astype(BF16), kt.astype(BF16), ii.astype(BF16), kk.astype(BF16)
    attn, o = [], []
    for hd, (qh, kh, ih) in enumerate(zip(_heads(qtb), _heads(ktb), _heads(iib))):
        a = jnp.where(causal, _dot(qh, kh, NT), 0.0).astype(BF16)
        attn.append(a)
        o.append(_dot(a, ih) + _dot(qh, st_heads[hd], NT))
    return dict(sg=sg, f=f, k=k, ec=ec, en=en, sq=sq, ecl=ecl, kk=kk, qtb=qtb, ktb=ktb, iib=iib,
                kkb=kkb, attn=attn, o=o)


def _rms_heads(o_heads):
    rinv = [lax.rsqrt(jnp.mean(o * o, axis=-1, keepdims=True) + RMS_EPS) for o in o_heads]
    return rinv, jnp.concatenate([o * r for o, r in zip(o_heads, rinv)], axis=1)


HG_CHUNKS = 2
HG_ROWS = HG_CHUNKS * CHUNK


def _hgrn_fwd(h, logits, gn, comm=None):
    t = h.shape[0]
    nb = t // HG_ROWS

    def body(q_ref, f_ref, i_ref, og_ref, lg_ref, gn_ref, yb_ref, st_ref, state):
        @pl.when(pl.program_id(0) == 0)
        def _():
            state[...] = jnp.zeros_like(state)

        causal, _ = _tri_masks()
        lb = _sig(lg_ref[0:1, :] - lg_ref[1:2, :])
        gnv = gn_ref[...]
        st = [state[hd] for hd in range(N_HEAD)]
        for cc in range(HG_CHUNKS):
            rows = slice(cc * CHUNK, (cc + 1) * CHUNK)
            og = og_ref[rows, :].astype(F32)
            r = _hgrn_chunk(q_ref[rows, :].astype(F32), f_ref[rows, :].astype(F32),
                            i_ref[rows, :].astype(F32), lb, [s.astype(BF16) for s in st], causal)
            _, on = _rms_heads(r["o"])
            yb_ref[rows, :] = (on * gnv * (og * _sig(og))).astype(BF16)
            for hd in range(N_HEAD):
                st_ref[cc, hd] = st[hd]
            st = [s * e + _dot(ih, kh, TN)
                  for s, e, ih, kh in zip(st, _heads(r["ecl"]), _heads(r["iib"]), _heads(r["kkb"]))]
        for hd in range(N_HEAD):
            state[hd] = st[hd]

    def col(k):
        return pl.BlockSpec((HG_ROWS, D_MODEL), lambda ci: (ci, k))

    return _grid1_call(body, comm, nb, name="hgrn_fwd",
                       out_shape=(jax.ShapeDtypeStruct((t, D_MODEL), BF16),
                                  jax.ShapeDtypeStruct((t // CHUNK, N_HEAD, HEAD_DIM, HEAD_DIM), F32)),
                       in_specs=[col(2), col(3), col(4), col(5),
                                 pl.BlockSpec((2, D_MODEL), lambda ci: (0, 0)),
                                 pl.BlockSpec((1, D_MODEL), lambda ci: (0, 0))],
                       out_specs=(pl.BlockSpec((HG_ROWS, D_MODEL), lambda ci: (ci, 0)),
                                  pl.BlockSpec((HG_CHUNKS, N_HEAD, HEAD_DIM, HEAD_DIM),
                                               lambda ci: (ci, 0, 0, 0))),
                       scratch=[pltpu.VMEM((N_HEAD, HEAD_DIM, HEAD_DIM), F32)],
                       args=(h, h, h, h, logits, gn))


def _hgrn_chunk_bwd(q, fp, ii, og, dy, gnv, lb, st, dsn, causal, anti):
    stb = [s.astype(BF16) for s in st]
    dsnb = [s.astype(BF16) for s in dsn]
    r = _hgrn_chunk(q, fp, ii, lb, stb, causal)
    rinv, on = _rms_heads(r["o"])
    so = _sig(og)
    sil = og * so
    d_og = dy * on * gnv * (so * (1.0 + og * (1.0 - so)))
    d_on = dy * gnv * sil
    d_ob = jnp.concatenate(
        [ri * (dn - oh * jnp.mean(dn * oh, axis=-1, keepdims=True))
         for ri, dn, oh in zip(rinv, _heads(d_on), _heads(on))], axis=1).astype(BF16)
    d_i, d_qt, d_kt, d_kk, d_st, st_dsn = [], [], [], [], [], []
    ecl = _heads(r["ecl"])
    for hd, (dh, qh, kh, ih, kkh) in enumerate(zip(_heads(d_ob), _heads(r["qtb"]), _heads(r["ktb"]),
                                                   _heads(r["iib"]), _heads(r["kkb"]))):
        d_attn = jnp.where(causal, _dot(dh, ih, NT), 0.0).astype(BF16)
        d_i.append(_dot(r["attn"][hd], dh, TN) + _dot(kkh, dsnb[hd], NT))
        d_qt.append(_dot(d_attn, kh) + _dot(dh, stb[hd]))
        d_kt.append(_dot(d_attn, qh, TN))
        d_kk.append(_dot(ih, dsnb[hd]))
        d_st.append(_dot(dh, qh, TN) + dsn[hd] * ecl[hd])
        st_dsn.append(jnp.sum(st[hd] * dsn[hd], axis=0, keepdims=True))
    d_qt = jnp.concatenate(d_qt, axis=1)
    d_kt = jnp.concatenate(d_kt, axis=1)
    d_kk = jnp.concatenate(d_kk, axis=1)
    kk = r["kk"]
    d_cl = r["ecl"] * jnp.concatenate(st_dsn, axis=1) + jnp.sum(kk * d_kk, axis=0, keepdims=True)
    d_k = (d_kk * r["ecl"] + d_kt) * r["en"]
    d_c = d_qt * r["qtb"].astype(F32) - d_kt * r["ktb"].astype(F32) - d_kk * kk
    rowid = lax.broadcasted_iota(jnp.int32, (CHUNK, D_MODEL), 0)
    d_c = d_c + jnp.where(rowid == CHUNK - 1, d_cl, 0.0)
    d_lf = _tri_cumsum(anti.astype(BF16), d_c)
    d_f = d_lf / r["f"] - d_k
    sg, sq = r["sg"], r["sq"]
    d_q = d_qt * r["ec"] * (sq * (1.0 + q * (1.0 - sq)))
    d_fp = d_f * (1.0 - lb) * sg * (1.0 - sg)
    return (d_q, d_fp, jnp.concatenate(d_i, axis=1), d_og, d_st,
            _colsum8(dy * on * sil), _colsum8(d_f * (1.0 - sg)))


def _hgrn_bwd(h, dyb, st_all, logits, gn, comm=None):
    t = h.shape[0]
    nb = t // HG_ROWS

    def body(q_ref, f_ref, i_ref, og_ref, dyb_ref, st_ref, lg_ref, gn_ref,
             dh1_ref, dh2_ref, dlb_ref, dgn_ref, dstate):
        @pl.when(pl.program_id(0) == 0)
        def _():
            dstate[...] = jnp.zeros_like(dstate)
            dlb_ref[...] = jnp.zeros_like(dlb_ref)
            dgn_ref[...] = jnp.zeros_like(dgn_ref)

        causal, anti = _tri_masks()
        lb = _sig(lg_ref[0:1, :] - lg_ref[1:2, :])
        gnv = gn_ref[...]
        dsn = [dstate[hd] for hd in range(N_HEAD)]
        dgn_acc = jnp.zeros((8, D_MODEL), F32)
        dlb_acc = jnp.zeros((8, D_MODEL), F32)
        for cc in reversed(range(HG_CHUNKS)):
            rows = slice(cc * CHUNK, (cc + 1) * CHUNK)
            d_q, d_fp, d_i, d_og, dsn, dgn_c, dlb_c = _hgrn_chunk_bwd(
                q_ref[rows, :].astype(F32), f_ref[rows, :].astype(F32), i_ref[rows, :].astype(F32),
                og_ref[rows, :].astype(F32), dyb_ref[rows, :], gnv, lb,
                [st_ref[cc, hd] for hd in range(N_HEAD)], dsn, causal, anti)
            dgn_acc = dgn_acc + dgn_c
            dlb_acc = dlb_acc + dlb_c
            dh1_ref[rows, :D_MODEL] = d_q.astype(BF16)
            dh1_ref[rows, D_MODEL:] = d_fp.astype(BF16)
            dh2_ref[rows, :D_MODEL] = d_i.astype(BF16)
            dh2_ref[rows, D_MODEL:] = d_og.astype(BF16)
        dgn_ref[...] += dgn_acc
        dlb_ref[...] += dlb_acc
        for hd in range(N_HEAD):
            dstate[hd] = dsn[hd]

    def col(k):
        return pl.BlockSpec((HG_ROWS, D_MODEL), lambda ci: (nb - 1 - ci, k))

    acc8 = pl.BlockSpec((8, D_MODEL), lambda ci: (0, 0))
    pair = pl.BlockSpec((HG_ROWS, 2 * D_MODEL), lambda ci: (nb - 1 - ci, 0))
    return _grid1_call(body, comm, nb, name="hgrn_bwd",
                       out_shape=(jax.ShapeDtypeStruct((t, 2 * D_MODEL), BF16),
                                  jax.ShapeDtypeStruct((t, 2 * D_MODEL), BF16),
                                  jax.ShapeDtypeStruct((8, D_MODEL), F32),
                                  jax.ShapeDtypeStruct((8, D_MODEL), F32)),
                       in_specs=[col(2), col(3), col(4), col(5),
                                 pl.BlockSpec((HG_ROWS, D_MODEL), lambda ci: (nb - 1 - ci, 0)),
                                 pl.BlockSpec((HG_CHUNKS, N_HEAD, HEAD_DIM, HEAD_DIM),
                                              lambda ci: (nb - 1 - ci, 0, 0, 0)),
                                 pl.BlockSpec((2, D_MODEL), lambda ci: (0, 0)),
                                 pl.BlockSpec((1, D_MODEL), lambda ci: (0, 0))],
                       out_specs=(pair, pair, acc8, acc8),
                       scratch=[pltpu.VMEM((N_HEAD, HEAD_DIM, HEAD_DIM), F32)],
                       args=(h, h, h, h, dyb, st_all, logits, gn))


def _mix_fwd(ya, yb, h, x, wb0, wb1, wo, g1, b1, tm, comm=None):
    t = x.shape[0]

    def body(ya_ref, yb_ref, ga_ref, gb_ref, x_ref, wb0_ref, wb1_ref, wo_ref, g1_ref, b1_ref,
             r1_ref, a_ref, b_ref, m_ref, x1_ref):
        a = _dot(ya_ref[...], wb0_ref[...])
        b = _dot(yb_ref[...], wb1_ref[...])
        m = _sig(ga_ref[...].astype(F32)) * a + _sig(gb_ref[...].astype(F32)) * b
        r1 = ALPHA * x_ref[...] + _dot(m, wo_ref[...])
        xh, _ = _ln_stats(r1)
        r1_ref[...] = r1
        a_ref[...] = a
        b_ref[...] = b
        m_ref[...] = m.astype(BF16)
        x1_ref[...] = (xh * g1_ref[...] + b1_ref[...]).astype(BF16)

    tile = pl.BlockSpec((tm, D_MODEL), lambda i: (i, 0))
    wsp = pl.BlockSpec((D_MODEL, D_MODEL), lambda i: (0, 0))
    vec = pl.BlockSpec((1, D_MODEL), lambda i: (0, 0))
    f32o = jax.ShapeDtypeStruct((t, D_MODEL), F32)
    bfo = jax.ShapeDtypeStruct((t, D_MODEL), BF16)
    return _grid1_call(body, comm, t // tm, name="mix_fwd", out_shape=(f32o, f32o, f32o, bfo, bfo),
                       in_specs=[tile, tile,
                                 pl.BlockSpec((tm, D_MODEL), lambda i: (i, 6)),
                                 pl.BlockSpec((tm, D_MODEL), lambda i: (i, 7)),
                                 tile, wsp, wsp, wsp, vec, vec],
                       out_specs=(tile, tile, tile, tile, tile),
                       scratch=[], args=(ya, yb, h, h, x, wb0, wb1, wo, g1, b1))


def _mix_bwd(dr1, h, a, b, wo, wb0, wb1, tm):
    t = dr1.shape[0]

    def body(dr1_ref, ga_ref, gb_ref, a_ref, b_ref, wo_ref, wb0_ref, wb1_ref,
             da_ref, db_ref, dh3_ref, dya_ref, dyb_ref):
        d_m = _dot(dr1_ref[...], wo_ref[...], NT)
        sa = _sig(ga_ref[...].astype(F32))
        sb = _sig(gb_ref[...].astype(F32))
        d_a = (d_m * sa).astype(BF16)
        d_b = (d_m * sb).astype(BF16)
        da_ref[...] = d_a
        db_ref[...] = d_b
        dh3_ref[:, :D_MODEL] = (d_m * a_ref[...] * sa * (1.0 - sa)).astype(BF16)
        dh3_ref[:, D_MODEL:] = (d_m * b_ref[...] * sb * (1.0 - sb)).astype(BF16)
        dya_ref[...] = _dot(d_a, wb0_ref[...], NT)
        dyb_ref[...] = _dot(d_b, wb1_ref[...], NT)

    tile = pl.BlockSpec((tm, D_MODEL), lambda i: (i, 0))
    wsp = pl.BlockSpec((D_MODEL, D_MODEL), lambda i: (0, 0))
    f32o = jax.ShapeDtypeStruct((t, D_MODEL), F32)
    bfo = jax.ShapeDtypeStruct((t, D_MODEL), BF16)
    return _pc(body, name="mix_bwd",
               out_shape=(bfo, bfo, jax.ShapeDtypeStruct((t, 2 * D_MODEL), BF16), f32o, f32o),
               grid=(t // tm,),
               in_specs=[tile,
                         pl.BlockSpec((tm, D_MODEL), lambda i: (i, 6)),
                         pl.BlockSpec((tm, D_MODEL), lambda i: (i, 7)),
                         tile, tile, wsp, wsp, wsp],
               out_specs=(tile, tile, pl.BlockSpec((tm, 2 * D_MODEL), lambda i: (i, 0)),
                          tile, tile),
               sem=("parallel",))(dr1, h, h, a, b, wo, wb0, wb1)


FF_TILE = 1408
FF_NJ = D_FF // FF_TILE


def _shift_down(v, k):
    return pltpu.roll(v, k, 0)


def _shift_up(v, k):
    return pltpu.roll(v, v.shape[0] - k, 0)


def _conv_gate(ext, cw_ref, cb_ref):
    return (cw_ref[0:1, :] * _shift_down(ext, 2) + cw_ref[1:2, :] * _shift_down(ext, 1)
            + cw_ref[2:3, :] * ext + cb_ref[...])


HALO = 16


def _ffn_act_fwd(h2, convw, convb, tm):
    t = h2.shape[0]
    nth = tm // HALO

    def body(g_ref, gp_ref, v_ref, cw_ref, cb_ref, act_ref):
        i = pl.program_id(1)
        prev = gp_ref[...].astype(F32) * (i > 0).astype(F32)
        ext = jnp.concatenate([prev, g_ref[...].astype(F32)], axis=0)
        gc = _conv_gate(ext, cw_ref, cb_ref)[HALO:, :]
        act_ref[...] = (_gelu(gc) * v_ref[...].astype(F32)).astype(BF16)

    return _pc(body, name="ffn_act_fwd", out_shape=jax.ShapeDtypeStruct((t, D_FF), BF16),
               grid=(FF_NJ, t // tm),
               in_specs=[pl.BlockSpec((tm, FF_TILE), lambda j, i: (i, j)),
                         pl.BlockSpec((HALO, FF_TILE), lambda j, i: (jnp.maximum(i * nth - 1, 0), j)),
                         pl.BlockSpec((tm, FF_TILE), lambda j, i: (i, j + FF_NJ)),
                         pl.BlockSpec((3, FF_TILE), lambda j, i: (0, j)),
                         pl.BlockSpec((1, FF_TILE), lambda j, i: (0, j))],
               out_specs=pl.BlockSpec((tm, FF_TILE), lambda j, i: (i, j)),
               sem=("parallel", "parallel"))(h2, h2, h2, convw, convb)


def _ffn_act_bwd(h2, dact, convw, convb, tm):
    t = h2.shape[0]
    nth = tm // HALO
    ni = t // tm
    last_halo = t // HALO - 1
    main_rows = slice(HALO, HALO + tm)

    def body(g_ref, gp_ref, gn_ref, v_ref, vn_ref, da_ref, dan_ref, cw_ref, cb_ref,
             dh2_ref, dcw_ref, dcb_ref):
        i = pl.program_id(1)

        @pl.when(i == 0)
        def _():
            dcw_ref[...] = jnp.zeros_like(dcw_ref)
            dcb_ref[...] = jnp.zeros_like(dcb_ref)

        zeros = jnp.zeros((HALO, FF_TILE), F32)
        da = da_ref[...].astype(F32)
        prev = gp_ref[...].astype(F32) * (i > 0).astype(F32)
        ext = jnp.concatenate([prev, g_ref[...].astype(F32), gn_ref[...].astype(F32)], axis=0)
        vext = jnp.concatenate([zeros, v_ref[...].astype(F32), vn_ref[...].astype(F32)], axis=0)
        dnext = dan_ref[...].astype(F32) * (i < ni - 1).astype(F32)
        dext = jnp.concatenate([zeros, da, dnext], axis=0)
        g2 = _shift_down(ext, 2)
        g1 = _shift_down(ext, 1)
        gc = cw_ref[0:1, :] * g2 + cw_ref[1:2, :] * g1 + cw_ref[2:3, :] * ext + cb_ref[...]
        gl, dgl = _gelu_and_grad(gc)
        d_gc = dext * vext * dgl
        d_gate = (cw_ref[2:3, :] * d_gc + cw_ref[1:2, :] * _shift_up(d_gc, 1)
                  + cw_ref[0:1, :] * _shift_up(d_gc, 2))
        dh2_ref[0] = d_gate[main_rows, :].astype(BF16)
        dh2_ref[1] = (da * gl[main_rows, :]).astype(BF16)
        dm = d_gc[main_rows, :]
        s0 = jnp.sum(dm * g2[main_rows, :], axis=0, keepdims=True)
        s1 = jnp.sum(dm * g1[main_rows, :], axis=0, keepdims=True)
        s2 = jnp.sum(dm * ext[main_rows, :], axis=0, keepdims=True)
        rowid = lax.broadcasted_iota(jnp.int32, (8, FF_TILE), 0)
        dcw_ref[...] += jnp.where(rowid == 0, s0, jnp.where(rowid == 1, s1,
                                                            jnp.where(rowid == 2, s2, 0.0)))
        dcb_ref[...] += _colsum8(dm)

    def prev8(off):
        return pl.BlockSpec((HALO, FF_TILE), lambda j, i: (jnp.maximum(i * nth - 1, 0), j + off))

    def next8(off):
        return pl.BlockSpec((HALO, FF_TILE), lambda j, i: (jnp.minimum((i + 1) * nth, last_halo), j + off))

    def main(off):
        return pl.BlockSpec((tm, FF_TILE), lambda j, i: (i, j + off))

    acc = pl.BlockSpec((8, FF_TILE), lambda j, i: (0, j))
    return _pc(body, name="ffn_act_bwd",
               out_shape=(jax.ShapeDtypeStruct((2, t, D_FF), BF16),
                          jax.ShapeDtypeStruct((8, D_FF), F32),
                          jax.ShapeDtypeStruct((8, D_FF), F32)),
               grid=(FF_NJ, ni),
               in_specs=[main(0), prev8(0), next8(0), main(FF_NJ), next8(FF_NJ),
                         pl.BlockSpec((tm, FF_TILE), lambda j, i: (i, j)),
                         next8(0),
                         pl.BlockSpec((3, FF_TILE), lambda j, i: (0, j)),
                         pl.BlockSpec((1, FF_TILE), lambda j, i: (0, j))],
               out_specs=(pl.BlockSpec((2, tm, FF_TILE), lambda j, i: (0, i, j)), acc, acc),
               sem=("parallel", "arbitrary"))(h2, h2, h2, h2, h2, dact, dact, convw, convb)


def _out_fwd_bwd(act, x1b, r1, p2, tgt, wd, wpg, wpp, g1, b1, g2, b2, tm):
    t = r1.shape[0]

    def body(act_ref, x1b_ref, r1_ref, p_ref, tgt_ref, wd_ref, wpg_ref, wpp_ref,
             g1_ref, b1_ref, g2_ref, b2_ref,
             dr2_ref, dpg_ref, dpp_ref, loss_ref, dg2_ref, db2_ref):
        i = pl.program_id(0)

        @pl.when(i == 0)
        def _():
            loss_ref[...] = jnp.zeros_like(loss_ref)
            dg2_ref[...] = jnp.zeros_like(dg2_ref)
            db2_ref[...] = jnp.zeros_like(db2_ref)

        ffn = _dot(act_ref[...], wd_ref[...])
        pg = _dot(x1b_ref[...], wpg_ref[...])
        pp = _dot(p_ref[...], wpp_ref[...])
        s = _sig(pg)
        xh1, _ = _ln_stats(r1_ref[...])
        x1 = xh1 * g1_ref[...] + b1_ref[...]
        r2 = ALPHA * x1 + ffn + s * pp
        xh2, rstd2 = _ln_stats(r2)
        g2v = g2_ref[...]
        diff = xh2 * g2v + b2_ref[...] - tgt_ref[...]
        part = jnp.sum(jnp.sum(diff * diff, axis=1, keepdims=True), axis=0, keepdims=True)
        loss_ref[...] += jnp.broadcast_to(part * (0.5 / D_MODEL), loss_ref.shape)
        dy = diff * (1.0 / D_MODEL)
        dg2_ref[...] += _colsum8(dy * xh2)
        db2_ref[...] += _colsum8(dy)
        dr2 = _ln_bwd(dy * g2v, xh2, rstd2)
        dr2_ref[...] = dr2
        dpg_ref[...] = (dr2 * pp * s * (1.0 - s)).astype(BF16)
        dpp_ref[...] = (dr2 * s).astype(BF16)

    tile = pl.BlockSpec((tm, D_MODEL), lambda i: (i, 0))
    vec = pl.BlockSpec((1, D_MODEL), lambda i: (0, 0))
    acc8 = pl.BlockSpec((8, D_MODEL), lambda i: (0, 0))
    acc_shape = jax.ShapeDtypeStruct((8, D_MODEL), F32)
    return _pc(body, name="out_fwd_bwd",
               out_shape=(jax.ShapeDtypeStruct((t, D_MODEL), F32),
                          jax.ShapeDtypeStruct((t, D_MODEL), BF16),
                          jax.ShapeDtypeStruct((t, D_MODEL), BF16),
                          acc_shape, acc_shape, acc_shape),
               grid=(t // tm,),
               in_specs=[pl.BlockSpec((tm, D_FF), lambda i: (i, 0)), tile, tile,
                         pl.BlockSpec((tm, PLE_DIM), lambda i: (i, 0)), tile,
                         pl.BlockSpec((D_FF, D_MODEL), lambda i: (0, 0)),
                         pl.BlockSpec((D_MODEL, D_MODEL), lambda i: (0, 0)),
                         pl.BlockSpec((PLE_DIM, D_MODEL), lambda i: (0, 0)),
                         vec, vec, vec, vec],
               out_specs=(tile, tile, tile, acc8, acc8, acc8),
               sem=("arbitrary",))(act, x1b, r1, p2, tgt, wd, wpg, wpp, g1, b1, g2, b2)


def _ffn_in_bwd(dh2, wup_st, dpg, wpg, dr2, r1, g1, tm):
    t = r1.shape[0]
    ni = t // tm

    def body(dh2_ref, wup_ref, dpg_ref, wpg_ref, dr2_ref, r1_ref, g1_ref,
             dr1_ref, dg1_ref, db1_ref, acc):
        i = pl.program_id(0)
        j = pl.program_id(1)

        @pl.when((i == 0) & (j == 0))
        def _():
            dg1_ref[...] = jnp.zeros_like(dg1_ref)
            db1_ref[...] = jnp.zeros_like(db1_ref)

        @pl.when(j == 0)
        def _():
            acc[...] = _dot(dh2_ref[...], wup_ref[...], NT)

        @pl.when(j > 0)
        def _():
            acc[...] += _dot(dh2_ref[...], wup_ref[...], NT)

        @pl.when(j == N_CHIP - 1)
        def _():
            d_x1 = acc[...] + _dot(dpg_ref[...], wpg_ref[...], NT) + ALPHA * dr2_ref[...]
            xh, rstd = _ln_stats(r1_ref[...])
            dg1_ref[...] += _colsum8(d_x1 * xh)
            db1_ref[...] += _colsum8(d_x1)
            dr1_ref[...] = _ln_bwd(d_x1 * g1_ref[...], xh, rstd)

    tile = pl.BlockSpec((tm, D_MODEL), lambda i, j: (i, 0))
    acc8 = pl.BlockSpec((8, D_MODEL), lambda i, j: (0, 0))
    acc_shape = jax.ShapeDtypeStruct((8, D_MODEL), F32)
    return _pc(body, name="ffn_in_bwd",
               out_shape=(jax.ShapeDtypeStruct((t, D_MODEL), F32), acc_shape, acc_shape),
               grid=(ni, N_CHIP),
               in_specs=[pl.BlockSpec((None, tm, FF_TILE), lambda i, j: (j // FF_NJ, i, j % FF_NJ)),
                         pl.BlockSpec((None, D_MODEL, FF_TILE), lambda i, j: (j, 0, 0)),
                         tile, pl.BlockSpec((D_MODEL, D_MODEL), lambda i, j: (0, 0)),
                         tile, tile, pl.BlockSpec((1, D_MODEL), lambda i, j: (0, 0))],
               out_specs=(tile, acc8, acc8),
               scratch=[pltpu.VMEM((tm, D_MODEL), F32)],
               sem=("arbitrary", "arbitrary"))(dh2, wup_st, dpg, wpg, dr2, r1, g1)


ANY = pl.BlockSpec(memory_space=pl.ANY)


def _chip_peers():
    x, y, c = lax.axis_index("x"), lax.axis_index("y"), lax.axis_index("c")
    return x, y, c, [(1 - x, y), (x, 1 - y), (1 - x, 1 - y)]


def _gather_comm(halved, whole=()):
    n, nw = len(halved), len(whole)

    def copies(ins, outs, sems):
        ici_send, ici_recv, d2d_send, d2d_recv, own_send, own_recv = sems
        x, y, c, peers = _chip_peers()
        me = 2 * x + y
        sibling = (x, y, 1 - c)
        own, ici, ici_wait, fwd, fwd_wait = [], [], [], [], []
        for ti in range(n + nw):
            src, dst = ins[ti], outs[ti]
            own.append(pltpu.make_async_remote_copy(
                src_ref=src, dst_ref=dst.at[me], send_sem=own_send.at[ti], recv_sem=own_recv.at[ti],
                device_id=sibling, device_id_type=MESH))
            for k, (px, py) in enumerate(peers):
                pk = 2 * px + py
                sem = dict(send_sem=ici_send.at[ti * 3 + k], recv_sem=ici_recv.at[ti * 3 + k],
                           device_id=(px, py, c), device_id_type=MESH)
                if ti < n:
                    ici.append(pltpu.make_async_remote_copy(src_ref=src.at[c], dst_ref=dst.at[me, c], **sem))
                    ici_wait.append(pltpu.make_async_remote_copy(src_ref=src.at[c], dst_ref=dst.at[pk, c], **sem))
                    dsem = dict(send_sem=d2d_send.at[ti * 3 + k], recv_sem=d2d_recv.at[ti * 3 + k],
                                device_id=sibling, device_id_type=MESH)
                    fwd.append(pltpu.make_async_remote_copy(src_ref=dst.at[pk, c], dst_ref=dst.at[pk, c], **dsem))
                    fwd_wait.append(pltpu.make_async_remote_copy(
                        src_ref=dst.at[pk, 1 - c], dst_ref=dst.at[pk, 1 - c], **dsem))
                else:
                    ici.append(pltpu.make_async_remote_copy(src_ref=src, dst_ref=dst.at[me], **sem))
                    ici_wait.append(pltpu.make_async_remote_copy(src_ref=src, dst_ref=dst.at[pk], **sem))
        return own, ici, ici_wait, fwd, fwd_wait

    def start(ins, outs, sems):
        own, ici, _, _, _ = copies(ins, outs, sems)
        for cp in own + ici:
            cp.start()

    def finish(ins, outs, sems):
        own, ici, ici_wait, fwd, fwd_wait = copies(ins, outs, sems)
        for i, cp in enumerate(ici_wait):
            cp.wait_recv()
            if i < len(fwd):
                fwd[i].start()
        for cp in fwd_wait + own:
            cp.wait_recv()
        for cp in own + ici + fwd:
            cp.wait_send()

    srcs = list(halved) + list(whole)
    return _Comm(srcs, [jax.ShapeDtypeStruct((N_CHIP,) + s.shape, s.dtype) for s in srcs],
                 [pltpu.SemaphoreType.DMA((3 * (n + nw),)), pltpu.SemaphoreType.DMA((3 * (n + nw),)),
                  pltpu.SemaphoreType.DMA((max(3 * n, 1),)), pltpu.SemaphoreType.DMA((max(3 * n, 1),)),
                  pltpu.SemaphoreType.DMA((n + nw,)), pltpu.SemaphoreType.DMA((n + nw,))],
                 start, finish)


def _sibling_exchange_comm(grads):
    n = len(grads)

    def copies(ins, outs, sems):
        send_sems, recv_sems = sems
        x, y, c = lax.axis_index("x"), lax.axis_index("y"), lax.axis_index("c")
        res = []
        for ti in range(n):
            half = ins[ti].shape[1] // 2
            res.append(pltpu.make_async_remote_copy(
                src_ref=ins[ti].at[:, pl.ds(pl.multiple_of((1 - c) * half, 8), half), :],
                dst_ref=outs[ti],
                send_sem=send_sems.at[ti], recv_sem=recv_sems.at[ti],
                device_id=(x, y, 1 - c), device_id_type=MESH))
        return res

    def start(ins, outs, sems):
        for cp in copies(ins, outs, sems):
            cp.start()

    def finish(ins, outs, sems):
        for cp in copies(ins, outs, sems):
            cp.wait()

    return _Comm(grads, [jax.ShapeDtypeStruct((N_CHIP, g.shape[1] // 2, g.shape[2]), g.dtype) for g in grads],
                 [pltpu.SemaphoreType.DMA((n,)), pltpu.SemaphoreType.DMA((n,))], start, finish)


def _in_proj_gathering(x2b, own, chip, tm):
    t = x2b.shape[0]
    ni = t // tm
    half, cols = own.shape[1], own.shape[2]

    def body(chip_ref, x_ref, own_ref, own_hbm, h_ref, win_out,
             w_scr, ici_send, ici_recv, d2d_send, d2d_recv, own_sems, ld_sems):
        s, i = pl.program_id(0), pl.program_id(1)
        x, y, c, peers = _chip_peers()
        me = 2 * x + y
        sibling = (x, y, 1 - c)

        def ici(k, slot):
            px, py = peers[k]
            return pltpu.make_async_remote_copy(
                src_ref=own_hbm.at[c], dst_ref=win_out.at[slot, c],
                send_sem=ici_send.at[k], recv_sem=ici_recv.at[k],
                device_id=(px, py, c), device_id_type=MESH)

        def forward(k, core):
            pk = 2 * peers[k][0] + peers[k][1]
            return pltpu.make_async_remote_copy(
                src_ref=win_out.at[pk, core], dst_ref=win_out.at[pk, core],
                send_sem=d2d_send.at[k], recv_sem=d2d_recv.at[k],
                device_id=sibling, device_id_type=MESH)

        place_own = pltpu.make_async_remote_copy(
            src_ref=own_hbm, dst_ref=win_out.at[me], send_sem=own_sems.at[0], recv_sem=own_sems.at[1],
            device_id=sibling, device_id_type=MESH)

        @pl.when((s == 0) & (i == 0))
        def _():
            for k in range(3):
                ici(k, me).start()
            place_own.start()

        @pl.when(s == 0)
        def _():
            xv = x_ref[...]
            h_ref[...] = (_dot(xv[:, :half], own_ref[0]) + _dot(xv[:, half:], own_ref[1])).astype(BF16)

        for k in range(3):
            @pl.when((s == k + 1) & (i == 0))
            def _(k=k):
                pk = 2 * peers[k][0] + peers[k][1]
                ici(k, pk).wait_recv()
                forward(k, c).start()
                forward(k, 1 - c).wait_recv()
                loads = [pltpu.make_async_copy(win_out.at[pk, hh], w_scr.at[hh], ld_sems.at[hh])
                         for hh in range(2)]
                for ld in loads:
                    ld.start()
                for ld in loads:
                    ld.wait()

        @pl.when(s > 0)
        def _():
            xv = x_ref[...]
            h_ref[...] = (_dot(xv[:, :half], w_scr[0]) + _dot(xv[:, half:], w_scr[1])).astype(BF16)

        @pl.when((s == N_CHIP - 1) & (i == ni - 1))
        def _():
            place_own.wait()
            for k in range(3):
                ici(k, me).wait_send()
                forward(k, c).wait_send()

    def shard_col(s, me):
        return jnp.where(s == 0, me, me ^ jnp.where(s == 1, 2, jnp.where(s == 2, 1, 3)))

    res = _pc(body, name="in_proj",
              out_shape=(jax.ShapeDtypeStruct((t, N_CHIP * cols), BF16),
                         jax.ShapeDtypeStruct((N_CHIP,) + own.shape, own.dtype)),
              grid=(N_CHIP, ni), nsp=1,
              in_specs=[pl.BlockSpec((tm, 2 * half), lambda s, i, chip_ref: (i, 0)),
                        pl.BlockSpec(own.shape, lambda s, i, chip_ref: (0, 0, 0)),
                        ANY],
              out_specs=(pl.BlockSpec((tm, cols), lambda s, i, chip_ref: (i, shard_col(s, chip_ref[0]))),
                         ANY),
              scratch=[pltpu.VMEM(own.shape, own.dtype),
                       pltpu.SemaphoreType.DMA((3,)), pltpu.SemaphoreType.DMA((3,)),
                       pltpu.SemaphoreType.DMA((3,)), pltpu.SemaphoreType.DMA((3,)),
                       pltpu.SemaphoreType.DMA((2,)), pltpu.SemaphoreType.DMA((2,))],
              sem=("arbitrary", "arbitrary"))(chip, x2b, own, own)
    return res[0], res[1]


def _rs_add_halves(name, grad, recv, core):
    _, r, cdim = grad.shape
    half = r // 2
    tr = _row_tile(half, cdim, mult=16)
    nr = half // tr

    def body(c_ref, g_ref, r_ref, o_ref):
        o_ref[...] = (g_ref[...] + r_ref[...]).astype(BF16)

    return _pc(body, name=name, out_shape=jax.ShapeDtypeStruct((N_CHIP, half, cdim), BF16),
               grid=(N_CHIP, nr), nsp=1,
               in_specs=[pl.BlockSpec((None, tr, cdim), lambda j, i, c_ref: (j, c_ref[0] * nr + i, 0)),
                         pl.BlockSpec((None, tr, cdim), lambda j, i, c_ref: (j, i, 0))],
               out_specs=pl.BlockSpec((None, tr, cdim), lambda j, i, c_ref: (j, i, 0)),
               sem=("parallel", "parallel"))(core, grad, recv)


def _chip_exchange_comm(parts):
    n = len(parts)

    def copies(ins, outs, sems):
        send_sems, recv_sems = sems
        x, y, c, peers = _chip_peers()
        return [pltpu.make_async_remote_copy(
            src_ref=ins[ti].at[2 * px + py], dst_ref=outs[ti].at[k],
            send_sem=send_sems.at[ti * 3 + k], recv_sem=recv_sems.at[ti * 3 + k],
            device_id=(px, py, c), device_id_type=MESH)
            for ti in range(n) for k, (px, py) in enumerate(peers)]

    def start(ins, outs, sems):
        for cp in copies(ins, outs, sems):
            cp.start()

    def finish(ins, outs, sems):
        for cp in copies(ins, outs, sems):
            cp.wait()

    return _Comm(parts, [jax.ShapeDtypeStruct((3,) + p.shape[1:], p.dtype) for p in parts],
                 [pltpu.SemaphoreType.DMA((3 * n,)), pltpu.SemaphoreType.DMA((3 * n,))], start, finish)


def _rs_sum_chips(name, part, recv, chip):
    _, half, cdim = recv.shape
    tr = _row_tile(half, cdim, mult=16)

    def body(chip_ref, p_ref, r_ref, o_ref):
        o_ref[...] = ((p_ref[...].astype(F32) + r_ref[0].astype(F32)) + r_ref[1].astype(F32)
                      ) + r_ref[2].astype(F32)

    return _pc(body, name=name, out_shape=jax.ShapeDtypeStruct((half, cdim), F32),
               grid=(half // tr,), nsp=1,
               in_specs=[pl.BlockSpec((None, tr, cdim), lambda i, chip_ref: (chip_ref[0], i, 0)),
                         pl.BlockSpec((3, tr, cdim), lambda i, chip_ref: (0, i, 0))],
               out_specs=pl.BlockSpec((tr, cdim), lambda i, chip_ref: (i, 0)),
               sem=("parallel",))(chip, part, recv)


def _rs_send_halves(halves):
    n = len(halves)

    def body(*refs):
        ins, outs = refs[:n], refs[n:2 * n]
        send_sems, recv_sems = refs[2 * n:]
        x, y, c = lax.axis_index("x"), lax.axis_index("y"), lax.axis_index("c")
        sends = []
        for ti in range(n):
            cp = pltpu.make_async_remote_copy(
                src_ref=ins[ti], dst_ref=outs[ti],
                send_sem=send_sems.at[ti], recv_sem=recv_sems.at[ti],
                device_id=(x, y, 1 - c), device_id_type=MESH)
            cp.start()
            sends.append(cp)
        for cp in sends:
            cp.wait()

    return _pc(body, name="rs_send_halves",
               out_shape=tuple(jax.ShapeDtypeStruct(hv.shape, hv.dtype) for hv in halves),
               in_specs=[ANY] * n, out_specs=tuple([ANY] * n),
               scratch=[pltpu.SemaphoreType.DMA((n,)), pltpu.SemaphoreType.DMA((n,))])(*halves)


def _adamw_rows(name, mine, theirs, w, m, v, core):
    half, cdim = mine.shape
    tr = _row_tile(half, cdim, budget=1 << 19)
    nrh = half // tr

    def body(c_ref, mine_ref, theirs_ref, w_ref, m_ref, v_ref, g_ref, d_ref, m2_ref, v2_ref):
        is_mine = (pl.program_id(0) // nrh) == c_ref[0]
        g = jnp.where(is_mine, mine_ref[...], theirs_ref[...])
        d, m2, v2 = _adamw(w_ref[...], g, m_ref[...], v_ref[...])
        g_ref[...] = g
        d_ref[...] = d
        m2_ref[...] = m2
        v2_ref[...] = v2

    htile = pl.BlockSpec((tr, cdim), lambda i, c_ref: (i % nrh, 0))
    tile = pl.BlockSpec((tr, cdim), lambda i, c_ref: (i, 0))
    shp = jax.ShapeDtypeStruct((2 * half, cdim), F32)
    return _pc(body, name=name, out_shape=(shp, shp, shp, shp), grid=(2 * nrh,), nsp=1,
               in_specs=[htile, htile, tile, tile, tile], out_specs=(tile, tile, tile, tile),
               sem=("parallel",))(core, mine, theirs, w, m, v)


def _adamw_whole(name, g, w, m, v):
    def body(g_ref, w_ref, m_ref, v_ref, d_ref, m2_ref, v2_ref):
        d, m2, v2 = _adamw(w_ref[...], g_ref[...], m_ref[...], v_ref[...])
        d_ref[...] = d
        m2_ref[...] = m2
        v2_ref[...] = v2

    shp = jax.ShapeDtypeStruct(g.shape, F32)
    return _pc(body, name=name, out_shape=(shp, shp, shp))(g, w, m, v)


SMALL_LAYOUT = (
    ("sgu_w_s", 1024, 1, 0),
    ("sgu_b_s", 8, 1, 1024),
    ("sgu_norm_g", 1, 0, 0),
    ("sgu_norm_b", 1, 0, 1),
    ("hgrn_norm_g", 1, 0, 3),
    ("ln1_g", 1, 0, 4),
    ("ln1_b", 1, 0, 5),
    ("ffn_conv_b", 1, 2, 3),
    ("ln2_g", 1, 0, 6),
    ("ln2_b", 1, 0, 7),
)
LB_ROW = 2
LOSS_ROW = 8
PACK_SHAPES = ((16, D_MODEL), (N_GROUP * 128 + 8, 128), (8, D_FF))


def _small_allreduce_adamw(rows1024, dws, dbs, dcw, dcb, logits, m_logits, v_logits,
                           small_w, small_m, small_v):
    ns = len(SMALL_LAYOUT)
    nr = len(rows1024)
    nb = len(PACK_SHAPES)

    def body(*refs):
        row_refs = refs[:nr]
        dws_ref, dbs_ref, dcw_ref, dcb_ref, lg_ref, mlg_ref, vlg_ref = refs[nr:nr + 7]
        pos = nr + 7
        w_refs = refs[pos:pos + ns]
        m_refs = refs[pos + ns:pos + 2 * ns]
        v_refs = refs[pos + 2 * ns:pos + 3 * ns]
        pos += 3 * ns
        loss_ref, dcw_out = refs[pos:pos + 2]
        lg_outs = refs[pos + 2:pos + 6]
        pos += 6
        outs = refs[pos:pos + 4 * ns]
        pos += 4 * ns
        pack = refs[pos:pos + nb]
        sib = refs[pos + nb:pos + 2 * nb]
        gath = refs[pos + 2 * nb:pos + 3 * nb]
        d2d_send, d2d_recv, ici_send, ici_recv = refs[pos + 3 * nb:]

        x, y, c, peers = _chip_peers()
        me = 2 * x + y
        sibling = (x, y, 1 - c)

        pack[0][...] = jnp.zeros(PACK_SHAPES[0], F32)
        for k in range(nr):
            pack[0][k:k + 1, :] = row_refs[k][0:1, :]
        pack[1][0:N_GROUP * 128, :] = dws_ref[...]
        pack[1][N_GROUP * 128:, :] = dbs_ref[...]
        pack[2][...] = jnp.zeros(PACK_SHAPES[2], F32)
        pack[2][0:3, :] = dcw_ref[0:3, :]
        pack[2][3:4, :] = dcb_ref[0:1, :]

        d2d = [pltpu.make_async_remote_copy(
            src_ref=pack[b], dst_ref=sib[b], send_sem=d2d_send.at[b], recv_sem=d2d_recv.at[b],
            device_id=sibling, device_id_type=MESH) for b in range(nb)]
        for cp in d2d:
            cp.start()
        for cp in d2d:
            cp.wait()
        for b in range(nb):
            gath[b][me] = pack[b][...] + sib[b][...]

        ici, ici_wait = [], []
        for b in range(nb):
            for k, (px, py) in enumerate(peers):
                sem = dict(send_sem=ici_send.at[b * 3 + k], recv_sem=ici_recv.at[b * 3 + k],
                           device_id=(px, py, c), device_id_type=MESH)
                ici.append(pltpu.make_async_remote_copy(src_ref=gath[b].at[me], dst_ref=gath[b].at[me], **sem))
                ici_wait.append(pltpu.make_async_remote_copy(
                    src_ref=gath[b].at[me], dst_ref=gath[b].at[2 * px + py], **sem))
        for cp in ici:
            cp.start()
        for cp in ici_wait:
            cp.wait_recv()
        for cp in ici:
            cp.wait_send()

        tot = pack
        for b in range(nb):
            tot[b][...] = ((gath[b][0] + gath[b][1]) + gath[b][2]) + gath[b][3]

        loss_ref[...] = tot[0][LOSS_ROW:LOSS_ROW + 1, :]
        dcw_out[...] = tot[2][...]
        lb = _sig(lg_ref[0:1, :] - lg_ref[1:2, :])
        d0 = tot[0][LB_ROW:LB_ROW + 1, :] * lb * (1.0 - lb)
        rowid = lax.broadcasted_iota(jnp.int32, (2, D_MODEL), 0)
        g_lg = jnp.where(rowid == 0, d0, -d0)
        dl, ml, vl = _adamw(lg_ref[...], g_lg, mlg_ref[...], vlg_ref[...])
        lg_outs[0][...] = g_lg
        lg_outs[1][...] = dl
        lg_outs[2][...] = ml
        lg_outs[3][...] = vl
        for si, (_, rows, b, r0) in enumerate(SMALL_LAYOUT):
            g = tot[b][r0:r0 + rows, :]
            dl, ml, vl = _adamw(w_refs[si][...], g, m_refs[si][...], v_refs[si][...])
            outs[4 * si][...] = g
            outs[4 * si + 1][...] = dl
            outs[4 * si + 2][...] = ml
            outs[4 * si + 3][...] = vl

    shapes = [jax.ShapeDtypeStruct((1, D_MODEL), F32), jax.ShapeDtypeStruct((8, D_FF), F32)]
    shapes += [jax.ShapeDtypeStruct((2, D_MODEL), F32)] * 4
    for w in small_w:
        shapes += [jax.ShapeDtypeStruct(w.shape, F32)] * 4
    scratch = [pltpu.VMEM(shp, F32) for shp in PACK_SHAPES]
    scratch += [pltpu.VMEM(shp, F32) for shp in PACK_SHAPES]
    scratch += [pltpu.VMEM((N_CHIP,) + shp, F32) for shp in PACK_SHAPES]
    scratch += [pltpu.SemaphoreType.DMA((nb,)), pltpu.SemaphoreType.DMA((nb,)),
                pltpu.SemaphoreType.DMA((3 * nb,)), pltpu.SemaphoreType.DMA((3 * nb,))]
    vm = pl.BlockSpec(memory_space=pltpu.VMEM)
    n_in = nr + 7 + 3 * ns
    res = _pc(body, name="small_allreduce_adamw", out_shape=tuple(shapes),
              in_specs=[vm] * n_in, out_specs=tuple([vm] * len(shapes)),
              scratch=scratch)(*rows1024, dws, dbs, dcw, dcb, logits, m_logits, v_logits,
                               *small_w, *small_m, *small_v)
    return res[0], res[1], res[2:6], res[6:]


def kernel(x, p, w_in, sgu_w_s, sgu_b_s, sgu_norm_g, sgu_norm_b, hgrn_lb_logits, hgrn_norm_g, w_branch, w_out, ln1_g, ln1_b, ffn_w_up, ffn_conv_w, ffn_conv_b, ffn_w_down, ln2_g, ln2_b, ple_w_proj, ple_w_gate, loss_target, m_w_in, m_sgu_w_s, m_sgu_b_s, m_sgu_norm_g, m_sgu_norm_b, m_hgrn_lb_logits, m_hgrn_norm_g, m_w_branch, m_w_out, m_ln1_g, m_ln1_b, m_ffn_w_up, m_ffn_conv_w, m_ffn_conv_b, m_ffn_w_down, m_ln2_g, m_ln2_b, m_ple_w_proj, m_ple_w_gate, v_w_in, v_sgu_w_s, v_sgu_b_s, v_sgu_norm_g, v_sgu_norm_b, v_hgrn_lb_logits, v_hgrn_norm_g, v_w_branch, v_w_out, v_ln1_g, v_ln1_b, v_ffn_w_up, v_ffn_conv_w, v_ffn_conv_b, v_ffn_w_down, v_ln2_g, v_ln2_b, v_ple_w_proj, v_ple_w_gate):
    t = x.shape[1]
    x2 = x.reshape(t, D_MODEL)
    x2b = x2.astype(BF16)
    p2 = p.reshape(t, PLE_DIM)
    tgt = loss_target.reshape(t, D_MODEL)
    core = lax.axis_index("c").astype(jnp.int32).reshape(1)
    chip_id = (2 * lax.axis_index("x") + lax.axis_index("y")).astype(jnp.int32).reshape(1)

    big_w = [w_in[0], w_branch[0, 0], w_branch[0, 1], w_out[0], ffn_w_up[0], ffn_w_down[0],
             ple_w_proj[0], ple_w_gate[0]]
    big_m = [m_w_in[0], m_w_branch[0, 0], m_w_branch[0, 1], m_w_out[0], m_ffn_w_up[0],
             m_ffn_w_down[0], m_ple_w_proj[0], m_ple_w_gate[0]]
    big_v = [v_w_in[0], v_w_branch[0, 0], v_w_branch[0, 1], v_w_out[0], v_ffn_w_up[0],
             v_ffn_w_down[0], v_ple_w_proj[0], v_ple_w_gate[0]]
    def halves_of(i):
        w = big_w[i]
        return w.astype(BF16).reshape(2, w.shape[0] // 2, w.shape[1])

    def stacked(g, i):
        return g.reshape(N_CHIP, big_w[i].shape[0], big_w[i].shape[1])


    cid = jnp.arange(SGU_BLOCK) // CHUNK
    maskf = (cid[:, None] >= cid[None, :]).astype(F32)
    ws_masked = sgu_w_s[0] * maskf[None]
    wm = ws_masked.astype(BF16)
    wmt = jnp.transpose(ws_masked, (0, 2, 1)).astype(BF16)
    bsb = jnp.broadcast_to(sgu_b_s[0][:, :, None], (N_GROUP, SGU_BLOCK, 128))

    h, win_g = _in_proj_gathering(x2b, halves_of(0), chip_id, 512)
    win_st = stacked(win_g, 0)
    ya, _ = _sgu_fwd(h, wm, bsb, sgu_norm_g, sgu_norm_b)
    (yb, st_all), mix_g = _hgrn_fwd(h, hgrn_lb_logits, hgrn_norm_g,
                                     comm=_gather_comm([halves_of(i) for i in (4, 1, 2, 3)]))
    wup_st = stacked(mix_g[0], 4)
    wb0, wb1, wo = [stacked(g, i).reshape(D_MODEL, D_MODEL) for g, i in zip(mix_g[1:], (1, 2, 3))]
    (r1, a_br, b_br, m_bf, x1b), _ = _mix_fwd(ya, yb, h, x2, wb0, wb1, wo, ln1_g, ln1_b, 256)
    h2, out_g = _mm_nn_stacked("ffn_up", x1b, wup_st, 512,
                               comm=_gather_comm([halves_of(i) for i in (5, 6, 7)], [ffn_conv_w[0]]))
    wd = stacked(out_g[0], 5).reshape(D_FF, D_MODEL)
    wpp = jnp.transpose(stacked(out_g[1], 6), (1, 0, 2)).reshape(PLE_DIM, D_MODEL)
    wpg = stacked(out_g[2], 7).reshape(D_MODEL, D_MODEL)
    convw = jnp.transpose(out_g[3], (1, 0, 2)).reshape(3, D_FF)
    act = _ffn_act_fwd(h2, convw, ffn_conv_b, 256)
    dr2, dpg, dpp, loss_acc, dg2, db2 = _out_fwd_bwd(
        act, x1b, r1, p2, tgt, wd, wpg, wpp, ln1_g, ln1_b, ln2_g, ln2_b, 256)

    dact = _mm("ffn_down_bwd", dr2, wd, NT, (t // 512, FF_NJ, 1),
               pl.BlockSpec((512, D_MODEL), lambda i, j, k: (i, 0)),
               pl.BlockSpec((FF_TILE, D_MODEL), lambda i, j, k: (j, 0)),
               jax.ShapeDtypeStruct((t, D_FF), BF16),
               pl.BlockSpec((512, FF_TILE), lambda i, j, k: (i, j)))
    dh2, dcw, dcb = _ffn_act_bwd(h2, dact, convw, ffn_conv_b, 256)
    d_wd = _mm_tn("ffn_down_wgrad", act, dr2, FF_TILE, 512)
    d_wpg = _mm_tn("ple_gate_wgrad", x1b, dpg, 512, D_MODEL)
    d_wpp_st = _mm_tn("ple_proj_wgrad", p2, dpp, PLE_DIM, PLE_DIM, stacked=True)
    d_wup_st = _mm("ffn_up_wgrad", x1b, dh2, TN, (2, N_CHIP, 1),
                   pl.BlockSpec((t, 512), lambda i, j, k: (0, i)),
                   pl.BlockSpec((None, t, FF_TILE), lambda i, j, k: (j // FF_NJ, 0, j % FF_NJ)),
                   jax.ShapeDtypeStruct((N_CHIP, D_MODEL, FF_TILE), F32),
                   pl.BlockSpec((None, 512, FF_TILE), lambda i, j, k: (j, i, 0)))
    dr1, dg1, db1 = _ffn_in_bwd(dh2, wup_st, dpg, wpg, dr2, r1, ln1_g, 512)
    da_bf, db_bf, dh3, dya, dyb = _mix_bwd(dr1, h, a_br, b_br, wo, wb0, wb1, 256)
    d_wo = _mm_tn("out_proj_wgrad", m_bf, dr1, 512, 512)
    d_wb0 = _mm_tn("branch0_wgrad", ya, da_bf, 512, D_MODEL)
    d_wb1 = _mm_tn("branch1_wgrad", yb, db_bf, 512, D_MODEL)
    grads_1 = [d_wb0.reshape(4, 256, D_MODEL), d_wb1.reshape(4, 256, D_MODEL),
               d_wo.reshape(4, 256, D_MODEL), d_wup_st, d_wd.reshape(4, D_FF // 4, D_MODEL),
               d_wpp_st, d_wpg.reshape(4, 256, D_MODEL)]
    (dh0, dws, dbs, dgv, dbv), recv_a1 = _sgu_bwd(h, dya, wm, wmt, bsb, sgu_norm_g, sgu_norm_b, maskf,
                                                  comm=_sibling_exchange_comm(grads_1))
    parts_1 = [_rs_add_halves("rs_add_halves%d" % (i + 1), g, r, core)
               for i, (g, r) in enumerate(zip(grads_1, recv_a1))]
    (dh1, dh2h, dlb, dgn), recv_b1 = _hgrn_bwd(h, dyb, st_all, hgrn_lb_logits, hgrn_norm_g,
                                                comm=_chip_exchange_comm(parts_1))
    dh_parts = [dh0, dh1, dh2h, dh3]
    d_win = [_mm_tn("in_proj_wgrad%d" % j, x2b, dh_parts[j], 512, D_MODEL) for j in range(4)]

    grads_0 = [jnp.stack(d_win)]
    recv_a0 = _run_comm("rs_sibling_exchange0", _sibling_exchange_comm(grads_0))
    parts_0 = [_rs_add_halves("rs_add_halves0", grads_0[0], recv_a0[0], core)]
    gx, recv_b0 = _in_proj_xgrad(dh_parts, win_st, dr1, 512, comm=_chip_exchange_comm(parts_0))
    parts = parts_0 + parts_1
    recv_b = list(recv_b0) + list(recv_b1)
    halves = [_rs_sum_chips("rs_sum_chips%d" % i, pt, r, chip_id)
              for i, (pt, r) in enumerate(zip(parts, recv_b))]
    theirs = _rs_send_halves(halves)
    big_out = [_adamw_rows("adamw_big%d" % i, halves[i], theirs[i], big_w[i], big_m[i], big_v[i], core)
               for i in range(len(halves))]

    small_in = dict(sgu_w_s=(sgu_w_s, m_sgu_w_s, v_sgu_w_s), sgu_b_s=(sgu_b_s, m_sgu_b_s, v_sgu_b_s),
                    sgu_norm_g=(sgu_norm_g, m_sgu_norm_g, v_sgu_norm_g),
                    sgu_norm_b=(sgu_norm_b, m_sgu_norm_b, v_sgu_norm_b),
                    hgrn_norm_g=(hgrn_norm_g, m_hgrn_norm_g, v_hgrn_norm_g),
                    ln1_g=(ln1_g, m_ln1_g, v_ln1_g), ln1_b=(ln1_b, m_ln1_b, v_ln1_b),
                    ffn_conv_b=(ffn_conv_b, m_ffn_conv_b, v_ffn_conv_b),
                    ln2_g=(ln2_g, m_ln2_g, v_ln2_g), ln2_b=(ln2_b, m_ln2_b, v_ln2_b))

    def flat(name, arr):
        rows = dict((n, r) for n, r, _, _ in SMALL_LAYOUT)[name]
        return arr.reshape(rows, arr.size // rows)

    names = [n for n, _, _, _ in SMALL_LAYOUT]
    sw = [flat(n, small_in[n][0]) for n in names]
    sm = [flat(n, small_in[n][1]) for n in names]
    sv = [flat(n, small_in[n][2]) for n in names]
    loss_rows, dcw_tot, lg_out, small_out = _small_allreduce_adamw(
        [dgv, dbv, dlb, dgn, dg1, db1, dg2, db2, loss_acc], dws.reshape(N_GROUP * 128, 128), dbs, dcw, dcb,
        hgrn_lb_logits, m_hgrn_lb_logits, v_hgrn_lb_logits, sw, sm, sv)
    loss = loss_rows[0, 0]

    chip = 2 * lax.axis_index("x") + lax.axis_index("y")
    g_cw = lax.dynamic_slice(dcw_tot, (0, chip * (D_FF // 4)), (3, D_FF // 4))
    cw_out = _adamw_whole("adamw_conv_w", g_cw, ffn_conv_w[0], m_ffn_conv_w[0], v_ffn_conv_w[0])

    res = {}
    for si, n in enumerate(names):
        shp = small_in[n][0].shape
        res[n] = tuple(small_out[4 * si + k].reshape(shp) for k in range(4))
    res["hgrn_lb_logits"] = tuple(lg_out)
    res["ffn_conv_w"] = (g_cw[None],) + tuple(o[None] for o in cw_out)

    def big(i):
        return tuple(big_out[i])

    res["w_in"] = tuple(o[None] for o in big(0))
    res["w_branch"] = tuple(jnp.stack([o0, o1])[None] for o0, o1 in zip(big(1), big(2)))
    res["w_out"] = tuple(o[None] for o in big(3))
    res["ffn_w_up"] = tuple(o[None] for o in big(4))
    res["ffn_w_down"] = tuple(o[None] for o in big(5))
    res["ple_w_proj"] = tuple(o[None] for o in big(6))
    res["ple_w_gate"] = tuple(o[None] for o in big(7))

    order = ["w_in", "sgu_w_s", "sgu_b_s", "sgu_norm_g", "sgu_norm_b", "hgrn_lb_logits",
             "hgrn_norm_g", "w_branch", "w_out", "ln1_g", "ln1_b", "ffn_w_up", "ffn_conv_w",
             "ffn_conv_b", "ffn_w_down", "ln2_g", "ln2_b", "ple_w_proj", "ple_w_gate"]
    outs = [loss, gx.reshape(1, t, D_MODEL)]
    for k in range(4):
        outs += [res[n][k] for n in order]
    return tuple(outs)
```

```python
import functools

import jax
import jax.numpy as jnp
from jax import lax
from jax.experimental import pallas as pl
from jax.experimental.pallas import tpu as pltpu

F32 = jnp.float32
BF16 = jnp.bfloat16
HIGHEST = lax.Precision.HIGHEST
MESH = pl.DeviceIdType.MESH

D_MODEL = 1024
CHUNK = 64
SGU_BLOCK = 128
N_GROUP = 8
N_HEAD = 8
HEAD_DIM = 128
D_FF = 2816
PLE_DIM = 256
IN_COLS = 8192
LN_EPS = 1e-5
RMS_EPS = 1e-6
ALPHA = 2.0 ** 0.25
N_CHIP = 4
N_DEV = 8

ADAM_LR = 0.001
ADAM_B1 = 0.9
ADAM_B2 = 0.999
ADAM_EPS = 1e-08
ADAM_WD = 0.01
ADAM_STEP = 10

VMEM_LIMIT = 56 * 1024 * 1024

NN = (((1,), (0,)), ((), ()))
NT = (((1,), (1,)), ((), ()))
TN = (((0,), (0,)), ((), ()))


def _pc(body, *, name, out_shape, grid=None, in_specs=None, out_specs=None, scratch=(),
        sem=None, nsp=0, vmem=VMEM_LIMIT):
    params = dict(vmem_limit_bytes=vmem)
    if sem is not None:
        params["dimension_semantics"] = sem
    kw = dict(name=name, out_shape=out_shape, compiler_params=pltpu.CompilerParams(**params))
    if nsp:
        kw["grid_spec"] = pltpu.PrefetchScalarGridSpec(
            num_scalar_prefetch=nsp, grid=grid, in_specs=in_specs, out_specs=out_specs,
            scratch_shapes=list(scratch))
    else:
        if grid is not None:
            kw["grid"] = grid
        if in_specs is not None:
            kw["in_specs"] = in_specs
            kw["out_specs"] = out_specs
        kw["scratch_shapes"] = list(scratch)
    return pl.pallas_call(body, **kw)


def _dot(a, b, dims=NN):
    return lax.dot_general(a.astype(BF16), b.astype(BF16), dims, preferred_element_type=F32)


def _dot32(a, b, dims=NN):
    return lax.dot_general(a, b, dims, precision=HIGHEST, preferred_element_type=F32)


def _sig(x):
    return 1.0 / (1.0 + jnp.exp(-x))


_GC = 0.7978845608028654
_GA = 0.044715


def _gelu(x):
    return 0.5 * x * (1.0 + jnp.tanh(_GC * (x + _GA * x * x * x)))


def _gelu_and_grad(x):
    t = jnp.tanh(_GC * (x + _GA * x * x * x))
    g = 0.5 * x * (1.0 + t)
    dg = 0.5 * (1.0 + t) + 0.5 * x * (1.0 - t * t) * _GC * (1.0 + 3.0 * _GA * x * x)
    return g, dg


def _ln_stats(r):
    mu = jnp.mean(r, axis=-1, keepdims=True)
    xc = r - mu
    var = jnp.mean(xc * xc, axis=-1, keepdims=True)
    rstd = lax.rsqrt(var + LN_EPS)
    return xc * rstd, rstd


def _ln_bwd(dxh, xh, rstd):
    m1 = jnp.mean(dxh, axis=-1, keepdims=True)
    m2 = jnp.mean(dxh * xh, axis=-1, keepdims=True)
    return rstd * (dxh - m1 - xh * m2)


def _colsum8(v):
    return jnp.broadcast_to(jnp.sum(v, axis=0, keepdims=True), (8, v.shape[1]))


def _adamw(w, g, m, v):
    m2 = ADAM_B1 * m + (1.0 - ADAM_B1) * g
    v2 = ADAM_B2 * v + (1.0 - ADAM_B2) * (g * g)
    m_hat = m2 / (1.0 - ADAM_B1 ** ADAM_STEP)
    v_hat = v2 / (1.0 - ADAM_B2 ** ADAM_STEP)
    delta = -ADAM_LR * (m_hat / (jnp.sqrt(v_hat) + ADAM_EPS) + ADAM_WD * w)
    return delta, m2, v2


def _row_tile(rows, cols, itemsize=4, budget=1 << 20, mult=8):
    best = mult
    for tr in range(mult, rows + 1, mult):
        if rows % tr == 0 and tr * cols * itemsize <= budget:
            best = tr
    return best


def _mm(name, a, b, dims, grid, a_spec, b_spec, out_shape, o_spec, add=None, add_spec=None,
        add_scale=1.0, comm=None):
    nk = grid[2]
    has_add = add is not None
    out_dtype = out_shape.dtype

    def body(*refs):
        if has_add:
            a_ref, b_ref, add_ref, o_ref = refs[:4]
            rest = refs[4:]
        else:
            a_ref, b_ref, o_ref = refs[:3]
            add_ref = None
            rest = refs[3:]
        prod = _dot(a_ref[...], b_ref[...], dims)

        def finish(acc):
            if has_add:
                acc = acc + add_scale * add_ref[...]
            o_ref[...] = acc.astype(out_dtype)

        if nk == 1:
            finish(prod)
        else:
            acc_ref = rest[0]
            k = pl.program_id(2)

            @pl.when(k == 0)
            def _():
                acc_ref[...] = prod

            @pl.when(k > 0)
            def _():
                acc_ref[...] += prod

            @pl.when(k == nk - 1)
            def _():
                finish(acc_ref[...])

    in_specs = [a_spec, b_spec] + ([add_spec] if has_add else [])
    args = [a, b] + ([add] if has_add else [])
    scratch = []
    if nk > 1:
        blk = [d for d in o_spec.block_shape if d is not None]
        scratch = [pltpu.VMEM(tuple(blk), F32)]
    if comm is None:
        return _pc(body, name=name, out_shape=out_shape, grid=grid, in_specs=in_specs,
                   out_specs=o_spec, scratch=scratch,
                   sem=("parallel", "parallel", "arbitrary"))(*args)

    def first():
        return (pl.program_id(0) == 0) & (pl.program_id(1) == 0) & (pl.program_id(2) == 0)

    def last():
        return ((pl.program_id(0) == grid[0] - 1) & (pl.program_id(1) == grid[1] - 1)
                & (pl.program_id(2) == grid[2] - 1))

    res = _hosted_call(body, comm, first, last, name=name, out_shape=(out_shape,), grid=grid,
                       in_specs=in_specs, out_specs=(o_spec,), scratch=scratch,
                       sem=("arbitrary", "arbitrary", "arbitrary"), args=args)
    return res[0], res[1:]


class _Comm:
    def __init__(self, ins, out_shapes, sems, start, finish):
        self.ins, self.out_shapes, self.sems = list(ins), list(out_shapes), list(sems)
        self.start, self.finish = start, finish


def _hosted_call(body, comm, first, last, *, name, out_shape, grid, in_specs, out_specs, scratch, sem,
                 args):
    n_in, n_out, n_scr = len(in_specs), len(out_shape), len(scratch)
    nci, nco = len(comm.ins), len(comm.out_shapes)

    def wrapped(*refs):
        pos = n_in
        own_in, c_in = refs[:pos], refs[pos:pos + nci]
        pos += nci
        own_out, c_out = refs[pos:pos + n_out], refs[pos + n_out:pos + n_out + nco]
        pos += n_out + nco
        own_scr, c_sem = refs[pos:pos + n_scr], refs[pos + n_scr:]

        @pl.when(first())
        def _():
            comm.start(c_in, c_out, c_sem)

        body(*own_in, *own_out, *own_scr)

        @pl.when(last())
        def _():
            comm.finish(c_in, c_out, c_sem)

    return _pc(wrapped, name=name, out_shape=tuple(out_shape) + tuple(comm.out_shapes), grid=grid,
               in_specs=list(in_specs) + [ANY] * nci, out_specs=tuple(out_specs) + tuple([ANY] * nco),
               scratch=list(scratch) + comm.sems, sem=sem)(*args, *comm.ins)


def _grid1_call(body, comm, n, *, name, out_shape, in_specs, out_specs, scratch, args):
    if comm is None:
        return _pc(body, name=name, out_shape=out_shape, grid=(n,), in_specs=in_specs,
                   out_specs=out_specs, scratch=scratch, sem=("arbitrary",))(*args), ()
    res = _hosted_call(body, comm, lambda: pl.program_id(0) == 0, lambda: pl.program_id(0) == n - 1,
                       name=name, out_shape=out_shape, grid=(n,), in_specs=in_specs,
                       out_specs=out_specs, scratch=scratch, sem=("arbitrary",), args=args)
    return res[:len(out_shape)], res[len(out_shape):]


def _run_comm(name, comm):
    nci, nco = len(comm.ins), len(comm.out_shapes)

    def body(*refs):
        c_in, c_out, c_sem = refs[:nci], refs[nci:nci + nco], refs[nci + nco:]
        comm.start(c_in, c_out, c_sem)
        comm.finish(c_in, c_out, c_sem)

    return _pc(body, name=name, out_shape=tuple(comm.out_shapes), in_specs=[ANY] * nci,
               out_specs=tuple([ANY] * nco), scratch=comm.sems)(*comm.ins)


def _mm_nn_stacked(name, a, w_st, tm, comm=None):
    t, k = a.shape
    _, _, c = w_st.shape
    return _mm(name, a, w_st, NN, (t // tm, N_CHIP, 1),
               pl.BlockSpec((tm, k), lambda i, j, kk: (i, 0)),
               pl.BlockSpec((None, k, c), lambda i, j, kk: (j, 0, 0)),
               jax.ShapeDtypeStruct((t, N_CHIP * c), BF16),
               pl.BlockSpec((tm, c), lambda i, j, kk: (i, j)), comm=comm)


def _mm_tn(name, a, b, tm, tn, stacked=False):
    t, m = a.shape
    _, n = b.shape
    if stacked:
        assert tm == m
        out_shape = jax.ShapeDtypeStruct((n // tn, m, tn), F32)
        o_spec = pl.BlockSpec((None, tm, tn), lambda i, j, kk: (j, 0, 0))
    else:
        out_shape = jax.ShapeDtypeStruct((m, n), F32)
        o_spec = pl.BlockSpec((tm, tn), lambda i, j, kk: (i, j))
    return _mm(name, a, b, TN, (m // tm, n // tn, 1),
               pl.BlockSpec((t, tm), lambda i, j, kk: (0, i)),
               pl.BlockSpec((t, tn), lambda i, j, kk: (0, j)),
               out_shape, o_spec)


def _in_proj_xgrad(dh_parts, win_st, dr1, tm, comm=None):
    t = dr1.shape[0]
    ni = t // tm

    def body(a0, a1, a2, a3, b_ref, add_ref, o_ref, acc):
        j = pl.program_id(1)
        for jj, a_ref in enumerate((a0, a1, a2, a3)):
            @pl.when(j == jj)
            def _(jj=jj, a_ref=a_ref):
                prod = _dot(a_ref[...], b_ref[...], NT)
                if jj == 0:
                    acc[...] = prod + ALPHA * add_ref[...]
                elif jj < N_CHIP - 1:
                    acc[...] += prod
                else:
                    o_ref[...] = acc[...] + prod

    a_spec = pl.BlockSpec((tm, 2 * D_MODEL), lambda i, j: (i, 0))
    tile = pl.BlockSpec((tm, D_MODEL), lambda i, j: (i, 0))
    kw = dict(name="in_proj_xgrad", out_shape=(jax.ShapeDtypeStruct((t, D_MODEL), F32),),
              grid=(ni, N_CHIP),
              in_specs=[a_spec] * 4 + [pl.BlockSpec((None, D_MODEL, 2 * D_MODEL), lambda i, j: (j, 0, 0)),
                                       tile],
              out_specs=(tile,), scratch=[pltpu.VMEM((tm, D_MODEL), F32)],
              sem=("arbitrary", "arbitrary"))
    args = list(dh_parts) + [win_st, dr1]
    if comm is None:
        return _pc(body, **kw)(*args)[0], ()
    res = _hosted_call(body, comm,
                       lambda: (pl.program_id(0) == 0) & (pl.program_id(1) == 0),
                       lambda: (pl.program_id(0) == ni - 1) & (pl.program_id(1) == N_CHIP - 1),
                       args=args, **kw)
    return res[0], res[1:]


def _sgu_mixed(v, wm_ref, bsb_ref, gv, bv):
    gl, dgl = _gelu_and_grad(v)
    vh, rstd = _ln_stats(gl)
    vn = vh * gv + bv
    mixed = []
    for g in range(N_GROUP):
        sl = slice(g * 128, (g + 1) * 128)
        mixed.append(_dot(wm_ref[g], vn[:, sl]) + bsb_ref[g])
    return dgl, vh, rstd, vn, mixed


def _sgu_fwd(h, wm, bsb, gv, bv, comm=None):
    t = h.shape[0]

    def body(u_ref, v_ref, wm_ref, bsb_ref, gv_ref, bv_ref, ya_ref):
        u = u_ref[...].astype(F32)
        _, _, _, _, mixed = _sgu_mixed(v_ref[...].astype(F32), wm_ref, bsb_ref, gv_ref[...], bv_ref[...])
        gu = _gelu(u)
        for g in range(N_GROUP):
            sl = slice(g * 128, (g + 1) * 128)
            ya_ref[:, sl] = (gu[:, sl] * mixed[g]).astype(BF16)

    full3 = pl.BlockSpec((N_GROUP, 128, 128), lambda i: (0, 0, 0))
    vec = pl.BlockSpec((1, D_MODEL), lambda i: (0, 0))
    (ya,), extra = _grid1_call(
        body, comm, t // SGU_BLOCK, name="sgu_fwd",
        out_shape=(jax.ShapeDtypeStruct((t, D_MODEL), BF16),),
        in_specs=[pl.BlockSpec((SGU_BLOCK, D_MODEL), lambda i: (i, 0)),
                  pl.BlockSpec((SGU_BLOCK, D_MODEL), lambda i: (i, 1)),
                  full3, full3, vec, vec],
        out_specs=(pl.BlockSpec((SGU_BLOCK, D_MODEL), lambda i: (i, 0)),),
        scratch=[], args=(h, h, wm, bsb, gv, bv))
    return ya, extra


def _sgu_bwd(h, dya, wm, wmt, bsb, gv, bv, maskf, comm=None):
    t = h.shape[0]
    nb = t // SGU_BLOCK

    def body(u_ref, v_ref, dya_ref, wm_ref, wmt_ref, bsb_ref, gv_ref, bv_ref, mask_ref,
             dh_ref, dws_ref, dbs_ref, dgv_ref, dbv_ref, dmix_acc):
        i = pl.program_id(0)

        @pl.when(i == 0)
        def _():
            dws_ref[...] = jnp.zeros_like(dws_ref)
            dgv_ref[...] = jnp.zeros_like(dgv_ref)
            dbv_ref[...] = jnp.zeros_like(dbv_ref)
            dmix_acc[...] = jnp.zeros_like(dmix_acc)

        u = u_ref[...].astype(F32)
        gvv = gv_ref[...]
        dgl_v, vh, rstd, vn, mixed = _sgu_mixed(v_ref[...].astype(F32), wm_ref, bsb_ref, gvv, bv_ref[...])
        gu, dgl_u = _gelu_and_grad(u)
        dya_v = dya_ref[...]
        dvn_parts = []
        for g in range(N_GROUP):
            sl = slice(g * 128, (g + 1) * 128)
            d_y = dya_v[:, sl]
            dh_ref[:, sl] = (d_y * mixed[g] * dgl_u[:, sl]).astype(BF16)
            d_mixed = d_y * gu[:, sl]
            dmix_acc[g] += d_mixed
            dws_ref[g] += _dot(d_mixed, vn[:, sl], NT) * mask_ref[...]
            dvn_parts.append(_dot(wmt_ref[g], d_mixed))
        dvn = jnp.concatenate(dvn_parts, axis=1)
        dgv_ref[...] += _colsum8(dvn * vh)
        dbv_ref[...] += _colsum8(dvn)
        d_gl = _ln_bwd(dvn * gvv, vh, rstd)
        dh_ref[:, D_MODEL:] = (d_gl * dgl_v).astype(BF16)

        @pl.when(i == nb - 1)
        def _():
            rowid = lax.broadcasted_iota(jnp.int32, (8, 128), 0)
            ones = jnp.ones((8, 128), F32)
            acc = jnp.zeros((8, 128), F32)
            for g in range(N_GROUP):
                rs = _dot32(ones, dmix_acc[g], NT)
                acc = jnp.where(rowid == g, rs, acc)
            dbs_ref[...] = acc

    full3 = pl.BlockSpec((N_GROUP, 128, 128), lambda i: (0, 0, 0))
    vec = pl.BlockSpec((1, D_MODEL), lambda i: (0, 0))
    acc8 = pl.BlockSpec((8, D_MODEL), lambda i: (0, 0))
    return _grid1_call(
        body, comm, nb, name="sgu_bwd",
        out_shape=(jax.ShapeDtypeStruct((t, 2 * D_MODEL), BF16),
                   jax.ShapeDtypeStruct((N_GROUP, 128, 128), F32),
                   jax.ShapeDtypeStruct((8, 128), F32),
                   jax.ShapeDtypeStruct((8, D_MODEL), F32),
                   jax.ShapeDtypeStruct((8, D_MODEL), F32)),
        in_specs=[pl.BlockSpec((SGU_BLOCK, D_MODEL), lambda i: (i, 0)),
                  pl.BlockSpec((SGU_BLOCK, D_MODEL), lambda i: (i, 1)),
                  pl.BlockSpec((SGU_BLOCK, D_MODEL), lambda i: (i, 0)),
                  full3, full3, full3, vec, vec,
                  pl.BlockSpec((128, 128), lambda i: (0, 0))],
        out_specs=(pl.BlockSpec((SGU_BLOCK, 2 * D_MODEL), lambda i: (i, 0)),
                   full3, pl.BlockSpec((8, 128), lambda i: (0, 0)), acc8, acc8),
        scratch=[pltpu.VMEM((N_GROUP, 128, 128), F32)],
        args=(h, h, dya, wm, wmt, bsb, gv, bv, maskf))


def _tri_masks():
    row = lax.broadcasted_iota(jnp.int32, (CHUNK, CHUNK), 0)
    col = lax.broadcasted_iota(jnp.int32, (CHUNK, CHUNK), 1)
    return col <= row, col >= row


def _heads(v):
    return [v[:, hd * HEAD_DIM:(hd + 1) * HEAD_DIM] for hd in range(N_HEAD)]


def _tri_cumsum(tri_bf, v):
    hi = v.astype(BF16)
    r = v - hi.astype(F32)
    mid = r.astype(BF16)
    lo = (r - mid.astype(F32)).astype(BF16)
    return _dot(tri_bf, hi) + _dot(tri_bf, mid) + _dot(tri_bf, lo)


def _hgrn_chunk(q, fp, ii, lb, st_heads, causal):
    sg = _sig(fp)
    f = lb + (1.0 - lb) * sg
    k = 1.0 - f
    c = _tri_cumsum(causal.astype(BF16), jnp.log(f))
    ec = jnp.exp(c)
    en = jnp.exp(-c)
    sq = _sig(q)
    qt = q * sq * ec
    kt = k * en
    ecl = jnp.exp(c[CHUNK - 1:CHUNK, :])
    kk = kt * ecl
    qtb, ktb, iib, kkb = qt.astype(BF16), kt.astype(BF16), ii.astype(BF16), kk.astype(BF16)
    attn, o = [], []
    for hd, (qh, kh, ih) in enumerate(zip(_heads(qtb), _heads(ktb), _heads(iib))):
        a = jnp.where(causal, _dot(qh, kh, NT), 0.0).astype(BF16)
        attn.append(a)
        o.append(_dot(a, ih) + _dot(qh, st_heads[hd], NT))
    return dict(sg=sg, f=f, k=k, ec=ec, en=en, sq=sq, ecl=ecl, kk=kk, qtb=qtb, ktb=ktb, iib=iib,
                kkb=kkb, attn=attn, o=o)


def _rms_heads(o_heads):
    rinv = [lax.rsqrt(jnp.mean(o * o, axis=-1, keepdims=True) + RMS_EPS) for o in o_heads]
    return rinv, jnp.concatenate([o * r for o, r in zip(o_heads, rinv)], axis=1)


HG_CHUNKS = 2
HG_ROWS = HG_CHUNKS * CHUNK


def _hgrn_fwd(h, logits, gn, comm=None):
    t = h.shape[0]
    nb = t // HG_ROWS

    def body(q_ref, f_ref, i_ref, og_ref, lg_ref, gn_ref, yb_ref, st_ref, state):
        @pl.when(pl.program_id(0) == 0)
        def _():
            state[...] = jnp.zeros_like(state)

        causal, _ = _tri_masks()
        lb = _sig(lg_ref[0:1, :] - lg_ref[1:2, :])
        gnv = gn_ref[...]
        st = [state[hd] for hd in range(N_HEAD)]
        for cc in range(HG_CHUNKS):
            rows = slice(cc * CHUNK, (cc + 1) * CHUNK)
            og = og_ref[rows, :].astype(F32)
            r = _hgrn_chunk(q_ref[rows, :].astype(F32), f_ref[rows, :].astype(F32),
                            i_ref[rows, :].astype(F32), lb, [s.astype(BF16) for s in st], causal)
            _, on = _rms_heads(r["o"])
            yb_ref[rows, :] = (on * gnv * (og * _sig(og))).astype(BF16)
            for hd in range(N_HEAD):
                st_ref[cc, hd] = st[hd]
            st = [s * e + _dot(ih, kh, TN)
                  for s, e, ih, kh in zip(st, _heads(r["ecl"]), _heads(r["iib"]), _heads(r["kkb"]))]
        for hd in range(N_HEAD):
            state[hd] = st[hd]

    def col(k):
        return pl.BlockSpec((HG_ROWS, D_MODEL), lambda ci: (ci, k))

    return _grid1_call(body, comm, nb, name="hgrn_fwd",
                       out_shape=(jax.ShapeDtypeStruct((t, D_MODEL), BF16),
                                  jax.ShapeDtypeStruct((t // CHUNK, N_HEAD, HEAD_DIM, HEAD_DIM), F32)),
                       in_specs=[col(2), col(3), col(4), col(5),
                                 pl.BlockSpec((2, D_MODEL), lambda ci: (0, 0)),
                                 pl.BlockSpec((1, D_MODEL), lambda ci: (0, 0))],
                       out_specs=(pl.BlockSpec((HG_ROWS, D_MODEL), lambda ci: (ci, 0)),
                                  pl.BlockSpec((HG_CHUNKS, N_HEAD, HEAD_DIM, HEAD_DIM),
                                               lambda ci: (ci, 0, 0, 0))),
                       scratch=[pltpu.VMEM((N_HEAD, HEAD_DIM, HEAD_DIM), F32)],
                       args=(h, h, h, h, logits, gn))


def _hgrn_chunk_bwd(q, fp, ii, og, dy, gnv, lb, st, dsn, causal, anti):
    stb = [s.astype(BF16) for s in st]
    dsnb = [s.astype(BF16) for s in dsn]
    r = _hgrn_chunk(q, fp, ii, lb, stb, causal)
    rinv, on = _rms_heads(r["o"])
    so = _sig(og)
    sil = og * so
    d_og = dy * on * gnv * (so * (1.0 + og * (1.0 - so)))
    d_on = dy * gnv * sil
    d_ob = jnp.concatenate(
        [ri * (dn - oh * jnp.mean(dn * oh, axis=-1, keepdims=True))
         for ri, dn, oh in zip(rinv, _heads(d_on), _heads(on))], axis=1).astype(BF16)
    d_i, d_qt, d_kt, d_kk, d_st, st_dsn = [], [], [], [], [], []
    ecl = _heads(r["ecl"])
    for hd, (dh, qh, kh, ih, kkh) in enumerate(zip(_heads(d_ob), _heads(r["qtb"]), _heads(r["ktb"]),
                                                   _heads(r["iib"]), _heads(r["kkb"]))):
        d_attn = jnp.where(causal, _dot(dh, ih, NT), 0.0).astype(BF16)
        d_i.append(_dot(r["attn"][hd], dh, TN) + _dot(kkh, dsnb[hd], NT))
        d_qt.append(_dot(d_attn, kh) + _dot(dh, stb[hd]))
        d_kt.append(_dot(d_attn, qh, TN))
        d_kk.append(_dot(ih, dsnb[hd]))
        d_st.append(_dot(dh, qh, TN) + dsn[hd] * ecl[hd])
        st_dsn.append(jnp.sum(st[hd] * dsn[hd], axis=0, keepdims=True))
    d_qt = jnp.concatenate(d_qt, axis=1)
    d_kt = jnp.concatenate(d_kt, axis=1)
    d_kk = jnp.concatenate(d_kk, axis=1)
    kk = r["kk"]
    d_cl = r["ecl"] * jnp.concatenate(st_dsn, axis=1) + jnp.sum(kk * d_kk, axis=0, keepdims=True)
    d_k = (d_kk * r["ecl"] + d_kt) * r["en"]
    d_c = d_qt * r["qtb"].astype(F32) - d_kt * r["ktb"].astype(F32) - d_kk * kk
    rowid = lax.broadcasted_iota(jnp.int32, (CHUNK, D_MODEL), 0)
    d_c = d_c + jnp.where(rowid == CHUNK - 1, d_cl, 0.0)
    d_lf = _tri_cumsum(anti.astype(BF16), d_c)
    d_f = d_lf / r["f"] - d_k
    sg, sq = r["sg"], r["sq"]
    d_q = d_qt * r["ec"] * (sq * (1.0 + q * (1.0 - sq)))
    d_fp = d_f * (1.0 - lb) * sg * (1.0 - sg)
    return (d_q, d_fp, jnp.concatenate(d_i, axis=1), d_og, d_st,
            _colsum8(dy * on * sil), _colsum8(d_f * (1.0 - sg)))


def _hgrn_bwd(h, dyb, st_all, logits, gn, comm=None):
    t = h.shape[0]
    nb = t // HG_ROWS

    def body(q_ref, f_ref, i_ref, og_ref, dyb_ref, st_ref, lg_ref, gn_ref,
             dh1_ref, dh2_ref, dlb_ref, dgn_ref, dstate):
        @pl.when(pl.program_id(0) == 0)
        def _():
            dstate[...] = jnp.zeros_like(dstate)
            dlb_ref[...] = jnp.zeros_like(dlb_ref)
            dgn_ref[...] = jnp.zeros_like(dgn_ref)

        causal, anti = _tri_masks()
        lb = _sig(lg_ref[0:1, :] - lg_ref[1:2, :])
        gnv = gn_ref[...]
        dsn = [dstate[hd] for hd in range(N_HEAD)]
        dgn_acc = jnp.zeros((8, D_MODEL), F32)
        dlb_acc = jnp.zeros((8, D_MODEL), F32)
        for cc in reversed(range(HG_CHUNKS)):
            rows = slice(cc * CHUNK, (cc + 1) * CHUNK)
            d_q, d_fp, d_i, d_og, dsn, dgn_c, dlb_c = _hgrn_chunk_bwd(
                q_ref[rows, :].astype(F32), f_ref[rows, :].astype(F32), i_ref[rows, :].astype(F32),
                og_ref[rows, :].astype(F32), dyb_ref[rows, :], gnv, lb,
                [st_ref[cc, hd] for hd in range(N_HEAD)], dsn, causal, anti)
            dgn_acc = dgn_acc + dgn_c
            dlb_acc = dlb_acc + dlb_c
            dh1_ref[rows, :D_MODEL] = d_q.astype(BF16)
            dh1_ref[rows, D_MODEL:] = d_fp.astype(BF16)
            dh2_ref[rows, :D_MODEL] = d_i.astype(BF16)
            dh2_ref[rows, D_MODEL:] = d_og.astype(BF16)
        dgn_ref[...] += dgn_acc
        dlb_ref[...] += dlb_acc
        for hd in range(N_HEAD):
            dstate[hd] = dsn[hd]

    def col(k):
        return pl.BlockSpec((HG_ROWS, D_MODEL), lambda ci: (nb - 1 - ci, k))

    acc8 = pl.BlockSpec((8, D_MODEL), lambda ci: (0, 0))
    pair = pl.BlockSpec((HG_ROWS, 2 * D_MODEL), lambda ci: (nb - 1 - ci, 0))
    return _grid1_call(body, comm, nb, name="hgrn_bwd",
                       out_shape=(jax.ShapeDtypeStruct((t, 2 * D_MODEL), BF16),
                                  jax.ShapeDtypeStruct((t, 2 * D_MODEL), BF16),
                                  jax.ShapeDtypeStruct((8, D_MODEL), F32),
                                  jax.ShapeDtypeStruct((8, D_MODEL), F32)),
                       in_specs=[col(2), col(3), col(4), col(5),
                                 pl.BlockSpec((HG_ROWS, D_MODEL), lambda ci: (nb - 1 - ci, 0)),
                                 pl.BlockSpec((HG_CHUNKS, N_HEAD, HEAD_DIM, HEAD_DIM),
                                              lambda ci: (nb - 1 - ci, 0, 0, 0)),
                                 pl.BlockSpec((2, D_MODEL), lambda ci: (0, 0)),
                                 pl.BlockSpec((1, D_MODEL), lambda ci: (0, 0))],
                       out_specs=(pair, pair, acc8, acc8),
                       scratch=[pltpu.VMEM((N_HEAD, HEAD_DIM, HEAD_DIM), F32)],
                       args=(h, h, h, h, dyb, st_all, logits, gn))


def _mix_fwd(ya, yb, h, x, wb0, wb1, wo, g1, b1, tm, comm=None):
    t = x.shape[0]

    def body(ya_ref, yb_ref, ga_ref, gb_ref, x_ref, wb0_ref, wb1_ref, wo_ref, g1_ref, b1_ref,
             r1_ref, a_ref, b_ref, m_ref, x1_ref):
        a = _dot(ya_ref[...], wb0_ref[...])
        b = _dot(yb_ref[...], wb1_ref[...])
        m = _sig(ga_ref[...].astype(F32)) * a + _sig(gb_ref[...].astype(F32)) * b
        r1 = ALPHA * x_ref[...] + _dot(m, wo_ref[...])
        xh, _ = _ln_stats(r1)
        r1_ref[...] = r1
        a_ref[...] = a
        b_ref[...] = b
        m_ref[...] = m.astype(BF16)
        x1_ref[...] = (xh * g1_ref[...] + b1_ref[...]).astype(BF16)

    tile = pl.BlockSpec((tm, D_MODEL), lambda i: (i, 0))
    wsp = pl.BlockSpec((D_MODEL, D_MODEL), lambda i: (0, 0))
    vec = pl.BlockSpec((1, D_MODEL), lambda i: (0, 0))
    f32o = jax.ShapeDtypeStruct((t, D_MODEL), F32)
    bfo = jax.ShapeDtypeStruct((t, D_MODEL), BF16)
    return _grid1_call(body, comm, t // tm, name="mix_fwd", out_shape=(f32o, f32o, f32o, bfo, bfo),
                       in_specs=[tile, tile,
                                 pl.BlockSpec((tm, D_MODEL), lambda i: (i, 6)),
                                 pl.BlockSpec((tm, D_MODEL), lambda i: (i, 7)),
                                 tile, wsp, wsp, wsp, vec, vec],
                       out_specs=(tile, tile, tile, tile, tile),
                       scratch=[], args=(ya, yb, h, h, x, wb0, wb1, wo, g1, b1))


def _mix_bwd(dr1, h, a, b, wo, wb0, wb1, tm):
    t = dr1.shape[0]

    def body(dr1_ref, ga_ref, gb_ref, a_ref, b_ref, wo_ref, wb0_ref, wb1_ref,
             da_ref, db_ref, dh3_ref, dya_ref, dyb_ref):
        d_m = _dot(dr1_ref[...], wo_ref[...], NT)
        sa = _sig(ga_ref[...].astype(F32))
        sb = _sig(gb_ref[...].astype(F32))
        d_a = (d_m * sa).astype(BF16)
        d_b = (d_m * sb).astype(BF16)
        da_ref[...] = d_a
        db_ref[...] = d_b
        dh3_ref[:, :D_MODEL] = (d_m * a_ref[...] * sa * (1.0 - sa)).astype(BF16)
        dh3_ref[:, D_MODEL:] = (d_m * b_ref[...] * sb * (1.0 - sb)).astype(BF16)
        dya_ref[...] = _dot(d_a, wb0_ref[...], NT)
        dyb_ref[...] = _dot(d_b, wb1_ref[...], NT)

    tile = pl.BlockSpec((tm, D_MODEL), lambda i: (i, 0))
    wsp = pl.BlockSpec((D_MODEL, D_MODEL), lambda i: (0, 0))
    f32o = jax.ShapeDtypeStruct((t, D_MODEL), F32)
    bfo = jax.ShapeDtypeStruct((t, D_MODEL), BF16)
    return _pc(body, name="mix_bwd",
               out_shape=(bfo, bfo, jax.ShapeDtypeStruct((t, 2 * D_MODEL), BF16), f32o, f32o),
               grid=(t // tm,),
               in_specs=[tile,
                         pl.BlockSpec((tm, D_MODEL), lambda i: (i, 6)),
                         pl.BlockSpec((tm, D_MODEL), lambda i: (i, 7)),
                         tile, tile, wsp, wsp, wsp],
               out_specs=(tile, tile, pl.BlockSpec((tm, 2 * D_MODEL), lambda i: (i, 0)),
                          tile, tile),
               sem=("parallel",))(dr1, h, h, a, b, wo, wb0, wb1)


FF_TILE = 1408
FF_NJ = D_FF // FF_TILE


def _shift_down(v, k):
    return pltpu.roll(v, k, 0)


def _shift_up(v, k):
    return pltpu.roll(v, v.shape[0] - k, 0)


def _conv_gate(ext, cw_ref, cb_ref):
    return (cw_ref[0:1, :] * _shift_down(ext, 2) + cw_ref[1:2, :] * _shift_down(ext, 1)
            + cw_ref[2:3, :] * ext + cb_ref[...])


HALO = 16


def _ffn_act_fwd(h2, convw, convb, tm):
    t = h2.shape[0]
    nth = tm // HALO

    def body(g_ref, gp_ref, v_ref, cw_ref, cb_ref, act_ref):
        i = pl.program_id(1)
        prev = gp_ref[...].astype(F32) * (i > 0).astype(F32)
        ext = jnp.concatenate([prev, g_ref[...].astype(F32)], axis=0)
        gc = _conv_gate(ext, cw_ref, cb_ref)[HALO:, :]
        act_ref[...] = (_gelu(gc) * v_ref[...].astype(F32)).astype(BF16)

    return _pc(body, name="ffn_act_fwd", out_shape=jax.ShapeDtypeStruct((t, D_FF), BF16),
               grid=(FF_NJ, t // tm),
               in_specs=[pl.BlockSpec((tm, FF_TILE), lambda j, i: (i, j)),
                         pl.BlockSpec((HALO, FF_TILE), lambda j, i: (jnp.maximum(i * nth - 1, 0), j)),
                         pl.BlockSpec((tm, FF_TILE), lambda j, i: (i, j + FF_NJ)),
                         pl.BlockSpec((3, FF_TILE), lambda j, i: (0, j)),
                         pl.BlockSpec((1, FF_TILE), lambda j, i: (0, j))],
               out_specs=pl.BlockSpec((tm, FF_TILE), lambda j, i: (i, j)),
               sem=("parallel", "parallel"))(h2, h2, h2, convw, convb)


def _ffn_act_bwd(h2, dact, convw, convb, tm):
    t = h2.shape[0]
    nth = tm // HALO
    ni = t // tm
    last_halo = t // HALO - 1
    main_rows = slice(HALO, HALO + tm)

    def body(g_ref, gp_ref, gn_ref, v_ref, vn_ref, da_ref, dan_ref, cw_ref, cb_ref,
             dh2_ref, dcw_ref, dcb_ref):
        i = pl.program_id(1)

        @pl.when(i == 0)
        def _():
            dcw_ref[...] = jnp.zeros_like(dcw_ref)
            dcb_ref[...] = jnp.zeros_like(dcb_ref)

        zeros = jnp.zeros((HALO, FF_TILE), F32)
        da = da_ref[...].astype(F32)
        prev = gp_ref[...].astype(F32) * (i > 0).astype(F32)
        ext = jnp.concatenate([prev, g_ref[...].astype(F32), gn_ref[...].astype(F32)], axis=0)
        vext = jnp.concatenate([zeros, v_ref[...].astype(F32), vn_ref[...].astype(F32)], axis=0)
        dnext = dan_ref[...].astype(F32) * (i < ni - 1).astype(F32)
        dext = jnp.concatenate([zeros, da, dnext], axis=0)
        g2 = _shift_down(ext, 2)
        g1 = _shift_down(ext, 1)
        gc = cw_ref[0:1, :] * g2 + cw_ref[1:2, :] * g1 + cw_ref[2:3, :] * ext + cb_ref[...]
        gl, dgl = _gelu_and_grad(gc)
        d_gc = dext * vext * dgl
        d_gate = (cw_ref[2:3, :] * d_gc + cw_ref[1:2, :] * _shift_up(d_gc, 1)
                  + cw_ref[0:1, :] * _shift_up(d_gc, 2))
        dh2_ref[0] = d_gate[main_rows, :].astype(BF16)
        dh2_ref[1] = (da * gl[main_rows, :]).astype(BF16)
        dm = d_gc[main_rows, :]
        s0 = jnp.sum(dm * g2[main_rows, :], axis=0, keepdims=True)
        s1 = jnp.sum(dm * g1[main_rows, :], axis=0, keepdims=True)
        s2 = jnp.sum(dm * ext[main_rows, :], axis=0, keepdims=True)
        rowid = lax.broadcasted_iota(jnp.int32, (8, FF_TILE), 0)
        dcw_ref[...] += jnp.where(rowid == 0, s0, jnp.where(rowid == 1, s1,
                                                            jnp.where(rowid == 2, s2, 0.0)))
        dcb_ref[...] += _colsum8(dm)

    def prev8(off):
        return pl.BlockSpec((HALO, FF_TILE), lambda j, i: (jnp.maximum(i * nth - 1, 0), j + off))

    def next8(off):
        return pl.BlockSpec((HALO, FF_TILE), lambda j, i: (jnp.minimum((i + 1) * nth, last_halo), j + off))

    def main(off):
        return pl.BlockSpec((tm, FF_TILE), lambda j, i: (i, j + off))

    acc = pl.BlockSpec((8, FF_TILE), lambda j, i: (0, j))
    return _pc(body, name="ffn_act_bwd",
               out_shape=(jax.ShapeDtypeStruct((2, t, D_FF), BF16),
                          jax.ShapeDtypeStruct((8, D_FF), F32),
                          jax.ShapeDtypeStruct((8, D_FF), F32)),
               grid=(FF_NJ, ni),
               in_specs=[main(0), prev8(0), next8(0), main(FF_NJ), next8(FF_NJ),
                         pl.BlockSpec((tm, FF_TILE), lambda j, i: (i, j)),
                         next8(0),
                         pl.BlockSpec((3, FF_TILE), lambda j, i: (0, j)),
                         pl.BlockSpec((1, FF_TILE), lambda j, i: (0, j))],
               out_specs=(pl.BlockSpec((2, tm, FF_TILE), lambda j, i: (0, i, j)), acc, acc),
               sem=("parallel", "arbitrary"))(h2, h2, h2, h2, h2, dact, dact, convw, convb)


def _out_fwd_bwd(act, x1b, r1, p2, tgt, wd, wpg, wpp, g1, b1, g2, b2, tm):
    t = r1.shape[0]

    def body(act_ref, x1b_ref, r1_ref, p_ref, tgt_ref, wd_ref, wpg_ref, wpp_ref,
             g1_ref, b1_ref, g2_ref, b2_ref,
             dr2_ref, dpg_ref, dpp_ref, loss_ref, dg2_ref, db2_ref):
        i = pl.program_id(0)

        @pl.when(i == 0)
        def _():
            loss_ref[...] = jnp.zeros_like(loss_ref)
            dg2_ref[...] = jnp.zeros_like(dg2_ref)
            db2_ref[...] = jnp.zeros_like(db2_ref)

        ffn = _dot(act_ref[...], wd_ref[...])
        pg = _dot(x1b_ref[...], wpg_ref[...])
        pp = _dot(p_ref[...], wpp_ref[...])
        s = _sig(pg)
        xh1, _ = _ln_stats(r1_ref[...])
        x1 = xh1 * g1_ref[...] + b1_ref[...]
        r2 = ALPHA * x1 + ffn + s * pp
        xh2, rstd2 = _ln_stats(r2)
        g2v = g2_ref[...]
        diff = xh2 * g2v + b2_ref[...] - tgt_ref[...]
        part = jnp.sum(jnp.sum(diff * diff, axis=1, keepdims=True), axis=0, keepdims=True)
        loss_ref[...] += jnp.broadcast_to(part * (0.5 / D_MODEL), loss_ref.shape)
        dy = diff * (1.0 / D_MODEL)
        dg2_ref[...] += _colsum8(dy * xh2)
        db2_ref[...] += _colsum8(dy)
        dr2 = _ln_bwd(dy * g2v, xh2, rstd2)
        dr2_ref[...] = dr2
        dpg_ref[...] = (dr2 * pp * s * (1.0 - s)).astype(BF16)
        dpp_ref[...] = (dr2 * s).astype(BF16)

    tile = pl.BlockSpec((tm, D_MODEL), lambda i: (i, 0))
    vec = pl.BlockSpec((1, D_MODEL), lambda i: (0, 0))
    acc8 = pl.BlockSpec((8, D_MODEL), lambda i: (0, 0))
    acc_shape = jax.ShapeDtypeStruct((8, D_MODEL), F32)
    return _pc(body, name="out_fwd_bwd",
               out_shape=(jax.ShapeDtypeStruct((t, D_MODEL), F32),
                          jax.ShapeDtypeStruct((t, D_MODEL), BF16),
                          jax.ShapeDtypeStruct((t, D_MODEL), BF16),
                          acc_shape, acc_shape, acc_shape),
               grid=(t // tm,),
               in_specs=[pl.BlockSpec((tm, D_FF), lambda i: (i, 0)), tile, tile,
                         pl.BlockSpec((tm, PLE_DIM), lambda i: (i, 0)), tile,
                         pl.BlockSpec((D_FF, D_MODEL), lambda i: (0, 0)),
                         pl.BlockSpec((D_MODEL, D_MODEL), lambda i: (0, 0)),
                         pl.BlockSpec((PLE_DIM, D_MODEL), lambda i: (0, 0)),
                         vec, vec, vec, vec],
               out_specs=(tile, tile, tile, acc8, acc8, acc8),
               sem=("arbitrary",))(act, x1b, r1, p2, tgt, wd, wpg, wpp, g1, b1, g2, b2)


def _ffn_in_bwd(dh2, wup_st, dpg, wpg, dr2, r1, g1, tm):
    t = r1.shape[0]
    ni = t // tm

    def body(dh2_ref, wup_ref, dpg_ref, wpg_ref, dr2_ref, r1_ref, g1_ref,
             dr1_ref, dg1_ref, db1_ref, acc):
        i = pl.program_id(0)
        j = pl.program_id(1)

        @pl.when((i == 0) & (j == 0))
        def _():
            dg1_ref[...] = jnp.zeros_like(dg1_ref)
            db1_ref[...] = jnp.zeros_like(db1_ref)

        @pl.when(j == 0)
        def _():
            acc[...] = _dot(dh2_ref[...], wup_ref[...], NT)

        @pl.when(j > 0)
        def _():
            acc[...] += _dot(dh2_ref[...], wup_ref[...], NT)

        @pl.when(j == N_CHIP - 1)
        def _():
            d_x1 = acc[...] + _dot(dpg_ref[...], wpg_ref[...], NT) + ALPHA * dr2_ref[...]
            xh, rstd = _ln_stats(r1_ref[...])
            dg1_ref[...] += _colsum8(d_x1 * xh)
            db1_ref[...] += _colsum8(d_x1)
            dr1_ref[...] = _ln_bwd(d_x1 * g1_ref[...], xh, rstd)

    tile = pl.BlockSpec((tm, D_MODEL), lambda i, j: (i, 0))
    acc8 = pl.BlockSpec((8, D_MODEL), lambda i, j: (0, 0))
    acc_shape = jax.ShapeDtypeStruct((8, D_MODEL), F32)
    return _pc(body, name="ffn_in_bwd",
               out_shape=(jax.ShapeDtypeStruct((t, D_MODEL), F32), acc_shape, acc_shape),
               grid=(ni, N_CHIP),
               in_specs=[pl.BlockSpec((None, tm, FF_TILE), lambda i, j: (j // FF_NJ, i, j % FF_NJ)),
                         pl.BlockSpec((None, D_MODEL, FF_TILE), lambda i, j: (j, 0, 0)),
                         tile, pl.BlockSpec((D_MODEL, D_MODEL), lambda i, j: (0, 0)),
                         tile, tile, pl.BlockSpec((1, D_MODEL), lambda i, j: (0, 0))],
               out_specs=(tile, acc8, acc8),
               scratch=[pltpu.VMEM((tm, D_MODEL), F32)],
               sem=("arbitrary", "arbitrary"))(dh2, wup_st, dpg, wpg, dr2, r1, g1)


ANY = pl.BlockSpec(memory_space=pl.ANY)


def _chip_peers():
    x, y, c = lax.axis_index("x"), lax.axis_index("y"), lax.axis_index("c")
    return x, y, c, [(1 - x, y), (x, 1 - y), (1 - x, 1 - y)]


def _gather_comm(halved, whole=()):
    n, nw = len(halved), len(whole)

    def copies(ins, outs, sems):
        ici_send, ici_recv, d2d_send, d2d_recv, own_send, own_recv = sems
        x, y, c, peers = _chip_peers()
        me = 2 * x + y
        sibling = (x, y, 1 - c)
        own, ici, ici_wait, fwd, fwd_wait = [], [], [], [], []
        for ti in range(n + nw):
            src, dst = ins[ti], outs[ti]
            own.append(pltpu.make_async_remote_copy(
                src_ref=src, dst_ref=dst.at[me], send_sem=own_send.at[ti], recv_sem=own_recv.at[ti],
                device_id=sibling, device_id_type=MESH))
            for k, (px, py) in enumerate(peers):
                pk = 2 * px + py
                sem = dict(send_sem=ici_send.at[ti * 3 + k], recv_sem=ici_recv.at[ti * 3 + k],
                           device_id=(px, py, c), device_id_type=MESH)
                if ti < n:
                    ici.append(pltpu.make_async_remote_copy(src_ref=src.at[c], dst_ref=dst.at[me, c], **sem))
                    ici_wait.append(pltpu.make_async_remote_copy(src_ref=src.at[c], dst_ref=dst.at[pk, c], **sem))
                    dsem = dict(send_sem=d2d_send.at[ti * 3 + k], recv_sem=d2d_recv.at[ti * 3 + k],
                                device_id=sibling, device_id_type=MESH)
                    fwd.append(pltpu.make_async_remote_copy(src_ref=dst.at[pk, c], dst_ref=dst.at[pk, c], **dsem))
                    fwd_wait.append(pltpu.make_async_remote_copy(
                        src_ref=dst.at[pk, 1 - c], dst_ref=dst.at[pk, 1 - c], **dsem))
                else:
                    ici.append(pltpu.make_async_remote_copy(src_ref=src, dst_ref=dst.at[me], **sem))
                    ici_wait.append(pltpu.make_async_remote_copy(src_ref=src, dst_ref=dst.at[pk], **sem))
        return own, ici, ici_wait, fwd, fwd_wait

    def start(ins, outs, sems):
        own, ici, _, _, _ = copies(ins, outs, sems)
        for cp in own + ici:
            cp.start()

    def finish(ins, outs, sems):
        own, ici, ici_wait, fwd, fwd_wait = copies(ins, outs, sems)
        for i, cp in enumerate(ici_wait):
            cp.wait_recv()
            if i < len(fwd):
                fwd[i].start()
        for cp in fwd_wait + own:
            cp.wait_recv()
        for cp in own + ici + fwd:
            cp.wait_send()

    srcs = list(halved) + list(whole)
    return _Comm(srcs, [jax.ShapeDtypeStruct((N_CHIP,) + s.shape, s.dtype) for s in srcs],
                 [pltpu.SemaphoreType.DMA((3 * (n + nw),)), pltpu.SemaphoreType.DMA((3 * (n + nw),)),
                  pltpu.SemaphoreType.DMA((max(3 * n, 1),)), pltpu.SemaphoreType.DMA((max(3 * n, 1),)),
                  pltpu.SemaphoreType.DMA((n + nw,)), pltpu.SemaphoreType.DMA((n + nw,))],
                 start, finish)


def _sibling_exchange_comm(grads):
    n = len(grads)

    def copies(ins, outs, sems):
        send_sems, recv_sems = sems
        x, y, c = lax.axis_index("x"), lax.axis_index("y"), lax.axis_index("c")
        res = []
        for ti in range(n):
            half = ins[ti].shape[1] // 2
            res.append(pltpu.make_async_remote_copy(
                src_ref=ins[ti].at[:, pl.ds(pl.multiple_of((1 - c) * half, 8), half), :],
                dst_ref=outs[ti],
                send_sem=send_sems.at[ti], recv_sem=recv_sems.at[ti],
                device_id=(x, y, 1 - c), device_id_type=MESH))
        return res

    def start(ins, outs, sems):
        for cp in copies(ins, outs, sems):
            cp.start()

    def finish(ins, outs, sems):
        for cp in copies(ins, outs, sems):
            cp.wait()

    return _Comm(grads, [jax.ShapeDtypeStruct((N_CHIP, g.shape[1] // 2, g.shape[2]), g.dtype) for g in grads],
                 [pltpu.SemaphoreType.DMA((n,)), pltpu.SemaphoreType.DMA((n,))], start, finish)


def _in_proj_gathering(x2b, own, chip, tm, comm):
    t = x2b.shape[0]
    ni = t // tm
    half, cols = own.shape[1], own.shape[2]
    nci, nco = len(comm.ins), len(comm.out_shapes)

    def body(chip_ref, x_ref, own_ref, own_hbm, *rest):
        c_in = rest[:nci]
        h_ref, win_out = rest[nci:nci + 2]
        c_out = rest[nci + 2:nci + 2 + nco]
        w_scr, ici_send, ici_recv, d2d_send, d2d_recv, own_sems, ld_sems = rest[nci + 2 + nco:nci + 9 + nco]
        c_sem = rest[nci + 9 + nco:]
        s, i = pl.program_id(0), pl.program_id(1)
        x, y, c, peers = _chip_peers()
        me = 2 * x + y
        sibling = (x, y, 1 - c)

        def ici(k, slot):
            px, py = peers[k]
            return pltpu.make_async_remote_copy(
                src_ref=own_hbm.at[c], dst_ref=win_out.at[slot, c],
                send_sem=ici_send.at[k], recv_sem=ici_recv.at[k],
                device_id=(px, py, c), device_id_type=MESH)

        def forward(k, core):
            pk = 2 * peers[k][0] + peers[k][1]
            return pltpu.make_async_remote_copy(
                src_ref=win_out.at[pk, core], dst_ref=win_out.at[pk, core],
                send_sem=d2d_send.at[k], recv_sem=d2d_recv.at[k],
                device_id=sibling, device_id_type=MESH)

        place_own = pltpu.make_async_remote_copy(
            src_ref=own_hbm, dst_ref=win_out.at[me], send_sem=own_sems.at[0], recv_sem=own_sems.at[1],
            device_id=sibling, device_id_type=MESH)

        @pl.when((s == 0) & (i == 0))
        def _():
            for k in range(3):
                ici(k, me).start()
            place_own.start()

        @pl.when(s == 0)
        def _():
            xv = x_ref[...]
            h_ref[...] = (_dot(xv[:, :half], own_ref[0]) + _dot(xv[:, half:], own_ref[1])).astype(BF16)

        for k in range(3):
            @pl.when((s == k + 1) & (i == 0))
            def _(k=k):
                pk = 2 * peers[k][0] + peers[k][1]
                ici(k, pk).wait_recv()
                forward(k, c).start()
                forward(k, 1 - c).wait_recv()
                loads = [pltpu.make_async_copy(win_out.at[pk, hh], w_scr.at[hh], ld_sems.at[hh])
                         for hh in range(2)]
                for ld in loads:
                    ld.start()
                for ld in loads:
                    ld.wait()
                if k == 1:
                    comm.start(c_in, c_out, c_sem)

        @pl.when(s > 0)
        def _():
            xv = x_ref[...]
            h_ref[...] = (_dot(xv[:, :half], w_scr[0]) + _dot(xv[:, half:], w_scr[1])).astype(BF16)

        @pl.when((s == N_CHIP - 1) & (i == ni - 1))
        def _():
            place_own.wait()
            for k in range(3):
                ici(k, me).wait_send()
                forward(k, c).wait_send()
            comm.finish(c_in, c_out, c_sem)

    def shard_col(s, me):
        return jnp.where(s == 0, me, me ^ jnp.where(s == 1, 2, jnp.where(s == 2, 1, 3)))

    res = _pc(body, name="in_proj",
              out_shape=(jax.ShapeDtypeStruct((t, N_CHIP * cols), BF16),
                         jax.ShapeDtypeStruct((N_CHIP,) + own.shape, own.dtype)) + tuple(comm.out_shapes),
              grid=(N_CHIP, ni), nsp=1,
              in_specs=[pl.BlockSpec((tm, 2 * half), lambda s, i, chip_ref: (i, 0)),
                        pl.BlockSpec(own.shape, lambda s, i, chip_ref: (0, 0, 0)),
                        ANY] + [ANY] * nci,
              out_specs=(pl.BlockSpec((tm, cols), lambda s, i, chip_ref: (i, shard_col(s, chip_ref[0]))),
                         ANY) + tuple([ANY] * nco),
              scratch=[pltpu.VMEM(own.shape, own.dtype),
                       pltpu.SemaphoreType.DMA((3,)), pltpu.SemaphoreType.DMA((3,)),
                       pltpu.SemaphoreType.DMA((3,)), pltpu.SemaphoreType.DMA((3,)),
                       pltpu.SemaphoreType.DMA((2,)), pltpu.SemaphoreType.DMA((2,))] + comm.sems,
              sem=("arbitrary", "arbitrary"))(chip, x2b, own, own, *comm.ins)
    return res[0], res[1], res[2:]


def _rs_add_halves(name, grad, recv, core):
    _, r, cdim = grad.shape
    half = r // 2
    tr = _row_tile(half, cdim, mult=16)
    nr = half // tr

    def body(c_ref, g_ref, r_ref, o_ref):
        o_ref[...] = (g_ref[...] + r_ref[...]).astype(BF16)

    return _pc(body, name=name, out_shape=jax.ShapeDtypeStruct((N_CHIP, half, cdim), BF16),
               grid=(N_CHIP, nr), nsp=1,
               in_specs=[pl.BlockSpec((None, tr, cdim), lambda j, i, c_ref: (j, c_ref[0] * nr + i, 0)),
                         pl.BlockSpec((None, tr, cdim), lambda j, i, c_ref: (j, i, 0))],
               out_specs=pl.BlockSpec((None, tr, cdim), lambda j, i, c_ref: (j, i, 0)),
               sem=("parallel", "parallel"))(core, grad, recv)


def _chip_exchange_comm(parts):
    n = len(parts)

    def copies(ins, outs, sems):
        send_sems, recv_sems = sems
        x, y, c, peers = _chip_peers()
        return [pltpu.make_async_remote_copy(
            src_ref=ins[ti].at[2 * px + py], dst_ref=outs[ti].at[k],
            send_sem=send_sems.at[ti * 3 + k], recv_sem=recv_sems.at[ti * 3 + k],
            device_id=(px, py, c), device_id_type=MESH)
            for ti in range(n) for k, (px, py) in enumerate(peers)]

    def start(ins, outs, sems):
        for cp in copies(ins, outs, sems):
            cp.start()

    def finish(ins, outs, sems):
        for cp in copies(ins, outs, sems):
            cp.wait()

    return _Comm(parts, [jax.ShapeDtypeStruct((3,) + p.shape[1:], p.dtype) for p in parts],
                 [pltpu.SemaphoreType.DMA((3 * n,)), pltpu.SemaphoreType.DMA((3 * n,))], start, finish)


def _rs_sum_chips(name, part, recv, chip):
    _, half, cdim = recv.shape
    tr = _row_tile(half, cdim, mult=16)

    def body(chip_ref, p_ref, r_ref, o_ref):
        o_ref[...] = ((p_ref[...].astype(F32) + r_ref[0].astype(F32)) + r_ref[1].astype(F32)
                      ) + r_ref[2].astype(F32)

    return _pc(body, name=name, out_shape=jax.ShapeDtypeStruct((half, cdim), F32),
               grid=(half // tr,), nsp=1,
               in_specs=[pl.BlockSpec((None, tr, cdim), lambda i, chip_ref: (chip_ref[0], i, 0)),
                         pl.BlockSpec((3, tr, cdim), lambda i, chip_ref: (0, i, 0))],
               out_specs=pl.BlockSpec((tr, cdim), lambda i, chip_ref: (i, 0)),
               sem=("parallel",))(chip, part, recv)


def _rs_send_halves(halves):
    n = len(halves)

    def body(*refs):
        ins, outs = refs[:n], refs[n:2 * n]
        send_sems, recv_sems = refs[2 * n:]
        x, y, c = lax.axis_index("x"), lax.axis_index("y"), lax.axis_index("c")
        sends = []
        for ti in range(n):
            cp = pltpu.make_async_remote_copy(
                src_ref=ins[ti], dst_ref=outs[ti],
                send_sem=send_sems.at[ti], recv_sem=recv_sems.at[ti],
                device_id=(x, y, 1 - c), device_id_type=MESH)
            cp.start()
            sends.append(cp)
        for cp in sends:
            cp.wait()

    return _pc(body, name="rs_send_halves",
               out_shape=tuple(jax.ShapeDtypeStruct(hv.shape, hv.dtype) for hv in halves),
               in_specs=[ANY] * n, out_specs=tuple([ANY] * n),
               scratch=[pltpu.SemaphoreType.DMA((n,)), pltpu.SemaphoreType.DMA((n,))])(*halves)


def _adamw_rows(name, mine, theirs, w, m, v, core):
    half, cdim = mine.shape
    tr = _row_tile(half, cdim, budget=1 << 19)
    nrh = half // tr

    def body(c_ref, mine_ref, theirs_ref, w_ref, m_ref, v_ref, g_ref, d_ref, m2_ref, v2_ref):
        is_mine = (pl.program_id(0) // nrh) == c_ref[0]
        g = jnp.where(is_mine, mine_ref[...], theirs_ref[...])
        d, m2, v2 = _adamw(w_ref[...], g, m_ref[...], v_ref[...])
        g_ref[...] = g
        d_ref[...] = d
        m2_ref[...] = m2
        v2_ref[...] = v2

    htile = pl.BlockSpec((tr, cdim), lambda i, c_ref: (i % nrh, 0))
    tile = pl.BlockSpec((tr, cdim), lambda i, c_ref: (i, 0))
    shp = jax.ShapeDtypeStruct((2 * half, cdim), F32)
    return _pc(body, name=name, out_shape=(shp, shp, shp, shp), grid=(2 * nrh,), nsp=1,
               in_specs=[htile, htile, tile, tile, tile], out_specs=(tile, tile, tile, tile),
               sem=("parallel",))(core, mine, theirs, w, m, v)


def _adamw_whole(name, g, w, m, v):
    def body(g_ref, w_ref, m_ref, v_ref, d_ref, m2_ref, v2_ref):
        d, m2, v2 = _adamw(w_ref[...], g_ref[...], m_ref[...], v_ref[...])
        d_ref[...] = d
        m2_ref[...] = m2
        v2_ref[...] = v2

    shp = jax.ShapeDtypeStruct(g.shape, F32)
    return _pc(body, name=name, out_shape=(shp, shp, shp))(g, w, m, v)


SMALL_LAYOUT = (
    ("sgu_w_s", 1024, 1, 0),
    ("sgu_b_s", 8, 1, 1024),
    ("sgu_norm_g", 1, 0, 0),
    ("sgu_norm_b", 1, 0, 1),
    ("hgrn_norm_g", 1, 0, 3),
    ("ln1_g", 1, 0, 4),
    ("ln1_b", 1, 0, 5),
    ("ffn_conv_b", 1, 2, 3),
    ("ln2_g", 1, 0, 6),
    ("ln2_b", 1, 0, 7),
)
LB_ROW = 2
LOSS_ROW = 8
PACK_SHAPES = ((16, D_MODEL), (N_GROUP * 128 + 8, 128), (8, D_FF))


def _small_allreduce_adamw(rows1024, dws, dbs, dcw, dcb, logits, m_logits, v_logits,
                           small_w, small_m, small_v):
    ns = len(SMALL_LAYOUT)
    nr = len(rows1024)
    nb = len(PACK_SHAPES)

    def body(*refs):
        row_refs = refs[:nr]
        dws_ref, dbs_ref, dcw_ref, dcb_ref, lg_ref, mlg_ref, vlg_ref = refs[nr:nr + 7]
        pos = nr + 7
        w_refs = refs[pos:pos + ns]
        m_refs = refs[pos + ns:pos + 2 * ns]
        v_refs = refs[pos + 2 * ns:pos + 3 * ns]
        pos += 3 * ns
        loss_ref, dcw_out = refs[pos:pos + 2]
        lg_outs = refs[pos + 2:pos + 6]
        pos += 6
        outs = refs[pos:pos + 4 * ns]
        pos += 4 * ns
        pack = refs[pos:pos + nb]
        sib = refs[pos + nb:pos + 2 * nb]
        gath = refs[pos + 2 * nb:pos + 3 * nb]
        d2d_send, d2d_recv, ici_send, ici_recv = refs[pos + 3 * nb:]

        x, y, c, peers = _chip_peers()
        me = 2 * x + y
        sibling = (x, y, 1 - c)

        pack[0][...] = jnp.zeros(PACK_SHAPES[0], F32)
        for k in range(nr):
            pack[0][k:k + 1, :] = row_refs[k][0:1, :]
        pack[1][0:N_GROUP * 128, :] = dws_ref[...]
        pack[1][N_GROUP * 128:, :] = dbs_ref[...]
        pack[2][...] = jnp.zeros(PACK_SHAPES[2], F32)
        pack[2][0:3, :] = dcw_ref[0:3, :]
        pack[2][3:4, :] = dcb_ref[0:1, :]

        d2d = [pltpu.make_async_remote_copy(
            src_ref=pack[b], dst_ref=sib[b], send_sem=d2d_send.at[b], recv_sem=d2d_recv.at[b],
            device_id=sibling, device_id_type=MESH) for b in range(nb)]
        for cp in d2d:
            cp.start()
        for cp in d2d:
            cp.wait()
        for b in range(nb):
            gath[b][me] = pack[b][...] + sib[b][...]

        ici, ici_wait = [], []
        for b in range(nb):
            for k, (px, py) in enumerate(peers):
                sem = dict(send_sem=ici_send.at[b * 3 + k], recv_sem=ici_recv.at[b * 3 + k],
                           device_id=(px, py, c), device_id_type=MESH)
                ici.append(pltpu.make_async_remote_copy(src_ref=gath[b].at[me], dst_ref=gath[b].at[me], **sem))
                ici_wait.append(pltpu.make_async_remote_copy(
                    src_ref=gath[b].at[me], dst_ref=gath[b].at[2 * px + py], **sem))
        for cp in ici:
            cp.start()
        for cp in ici_wait:
            cp.wait_recv()
        for cp in ici:
            cp.wait_send()

        tot = pack
        for b in range(nb):
            tot[b][...] = ((gath[b][0] + gath[b][1]) + gath[b][2]) + gath[b][3]

        loss_ref[...] = tot[0][LOSS_ROW:LOSS_ROW + 1, :]
        dcw_out[...] = tot[2][...]
        lb = _sig(lg_ref[0:1, :] - lg_ref[1:2, :])
        d0 = tot[0][LB_ROW:LB_ROW + 1, :] * lb * (1.0 - lb)
        rowid = lax.broadcasted_iota(jnp.int32, (2, D_MODEL), 0)
        g_lg = jnp.where(rowid == 0, d0, -d0)
        dl, ml, vl = _adamw(lg_ref[...], g_lg, mlg_ref[...], vlg_ref[...])
        lg_outs[0][...] = g_lg
        lg_outs[1][...] = dl
        lg_outs[2][...] = ml
        lg_outs[3][...] = vl
        for si, (_, rows, b, r0) in enumerate(SMALL_LAYOUT):
            g = tot[b][r0:r0 + rows, :]
            dl, ml, vl = _adamw(w_refs[si][...], g, m_refs[si][...], v_refs[si][...])
            outs[4 * si][...] = g
            outs[4 * si + 1][...] = dl
            outs[4 * si + 2][...] = ml
            outs[4 * si + 3][...] = vl

    shapes = [jax.ShapeDtypeStruct((1, D_MODEL), F32), jax.ShapeDtypeStruct((8, D_FF), F32)]
    shapes += [jax.ShapeDtypeStruct((2, D_MODEL), F32)] * 4
    for w in small_w:
        shapes += [jax.ShapeDtypeStruct(w.shape, F32)] * 4
    scratch = [pltpu.VMEM(shp, F32) for shp in PACK_SHAPES]
    scratch += [pltpu.VMEM(shp, F32) for shp in PACK_SHAPES]
    scratch += [pltpu.VMEM((N_CHIP,) + shp, F32) for shp in PACK_SHAPES]
    scratch += [pltpu.SemaphoreType.DMA((nb,)), pltpu.SemaphoreType.DMA((nb,)),
                pltpu.SemaphoreType.DMA((3 * nb,)), pltpu.SemaphoreType.DMA((3 * nb,))]
    vm = pl.BlockSpec(memory_space=pltpu.VMEM)
    n_in = nr + 7 + 3 * ns
    res = _pc(body, name="small_allreduce_adamw", out_shape=tuple(shapes),
              in_specs=[vm] * n_in, out_specs=tuple([vm] * len(shapes)),
              scratch=scratch)(*rows1024, dws, dbs, dcw, dcb, logits, m_logits, v_logits,
                               *small_w, *small_m, *small_v)
    return res[0], res[1], res[2:6], res[6:]


def kernel(x, p, w_in, sgu_w_s, sgu_b_s, sgu_norm_g, sgu_norm_b, hgrn_lb_logits, hgrn_norm_g, w_branch, w_out, ln1_g, ln1_b, ffn_w_up, ffn_conv_w, ffn_conv_b, ffn_w_down, ln2_g, ln2_b, ple_w_proj, ple_w_gate, loss_target, m_w_in, m_sgu_w_s, m_sgu_b_s, m_sgu_norm_g, m_sgu_norm_b, m_hgrn_lb_logits, m_hgrn_norm_g, m_w_branch, m_w_out, m_ln1_g, m_ln1_b, m_ffn_w_up, m_ffn_conv_w, m_ffn_conv_b, m_ffn_w_down, m_ln2_g, m_ln2_b, m_ple_w_proj, m_ple_w_gate, v_w_in, v_sgu_w_s, v_sgu_b_s, v_sgu_norm_g, v_sgu_norm_b, v_hgrn_lb_logits, v_hgrn_norm_g, v_w_branch, v_w_out, v_ln1_g, v_ln1_b, v_ffn_w_up, v_ffn_conv_w, v_ffn_conv_b, v_ffn_w_down, v_ln2_g, v_ln2_b, v_ple_w_proj, v_ple_w_gate):
    t = x.shape[1]
    x2 = x.reshape(t, D_MODEL)
    x2b = x2.astype(BF16)
    p2 = p.reshape(t, PLE_DIM)
    tgt = loss_target.reshape(t, D_MODEL)
    core = lax.axis_index("c").astype(jnp.int32).reshape(1)
    chip_id = (2 * lax.axis_index("x") + lax.axis_index("y")).astype(jnp.int32).reshape(1)

    big_w = [w_in[0], w_branch[0, 0], w_branch[0, 1], w_out[0], ffn_w_up[0], ffn_w_down[0],
             ple_w_proj[0], ple_w_gate[0]]
    big_m = [m_w_in[0], m_w_branch[0, 0], m_w_branch[0, 1], m_w_out[0], m_ffn_w_up[0],
             m_ffn_w_down[0], m_ple_w_proj[0], m_ple_w_gate[0]]
    big_v = [v_w_in[0], v_w_branch[0, 0], v_w_branch[0, 1], v_w_out[0], v_ffn_w_up[0],
             v_ffn_w_down[0], v_ple_w_proj[0], v_ple_w_gate[0]]
    def halves_of(i):
        w = big_w[i]
        return w.astype(BF16).reshape(2, w.shape[0] // 2, w.shape[1])

    def stacked(g, i):
        return g.reshape(N_CHIP, big_w[i].shape[0], big_w[i].shape[1])


    cid = jnp.arange(SGU_BLOCK) // CHUNK
    maskf = (cid[:, None] >= cid[None, :]).astype(F32)
    ws_masked = sgu_w_s[0] * maskf[None]
    wm = ws_masked.astype(BF16)
    wmt = jnp.transpose(ws_masked, (0, 2, 1)).astype(BF16)
    bsb = jnp.broadcast_to(sgu_b_s[0][:, :, None], (N_GROUP, SGU_BLOCK, 128))

    h, win_g, (wup_g,) = _in_proj_gathering(x2b, halves_of(0), chip_id, 512, _gather_comm([halves_of(4)]))
    win_st = stacked(win_g, 0)
    wup_st = stacked(wup_g, 4)
    ya, _ = _sgu_fwd(h, wm, bsb, sgu_norm_g, sgu_norm_b)
    (yb, st_all), mix_g = _hgrn_fwd(h, hgrn_lb_logits, hgrn_norm_g,
                                     comm=_gather_comm([halves_of(i) for i in (1, 2, 3)]))
    wb0, wb1, wo = [stacked(g, i).reshape(D_MODEL, D_MODEL) for g, i in zip(mix_g, (1, 2, 3))]
    (r1, a_br, b_br, m_bf, x1b), _ = _mix_fwd(ya, yb, h, x2, wb0, wb1, wo, ln1_g, ln1_b, 256)
    h2, out_g = _mm_nn_stacked("ffn_up", x1b, wup_st, 512,
                               comm=_gather_comm([halves_of(i) for i in (5, 6, 7)], [ffn_conv_w[0]]))
    wd = stacked(out_g[0], 5).reshape(D_FF, D_MODEL)
    wpp = jnp.transpose(stacked(out_g[1], 6), (1, 0, 2)).reshape(PLE_DIM, D_MODEL)
    wpg = stacked(out_g[2], 7).reshape(D_MODEL, D_MODEL)
    convw = jnp.transpose(out_g[3], (1, 0, 2)).reshape(3, D_FF)
    act = _ffn_act_fwd(h2, convw, ffn_conv_b, 256)
    dr2, dpg, dpp, loss_acc, dg2, db2 = _out_fwd_bwd(
        act, x1b, r1, p2, tgt, wd, wpg, wpp, ln1_g, ln1_b, ln2_g, ln2_b, 256)

    dact = _mm("ffn_down_bwd", dr2, wd, NT, (t // 512, FF_NJ, 1),
               pl.BlockSpec((512, D_MODEL), lambda i, j, k: (i, 0)),
               pl.BlockSpec((FF_TILE, D_MODEL), lambda i, j, k: (j, 0)),
               jax.ShapeDtypeStruct((t, D_FF), BF16),
               pl.BlockSpec((512, FF_TILE), lambda i, j, k: (i, j)))
    dh2, dcw, dcb = _ffn_act_bwd(h2, dact, convw, ffn_conv_b, 256)
    d_wd = _mm_tn("ffn_down_wgrad", act, dr2, FF_TILE, 512)
    d_wpg = _mm_tn("ple_gate_wgrad", x1b, dpg, 512, D_MODEL)
    d_wpp_st = _mm_tn("ple_proj_wgrad", p2, dpp, PLE_DIM, PLE_DIM, stacked=True)
    d_wup_st = _mm("ffn_up_wgrad", x1b, dh2, TN, (2, N_CHIP, 1),
                   pl.BlockSpec((t, 512), lambda i, j, k: (0, i)),
                   pl.BlockSpec((None, t, FF_TILE), lambda i, j, k: (j // FF_NJ, 0, j % FF_NJ)),
                   jax.ShapeDtypeStruct((N_CHIP, D_MODEL, FF_TILE), F32),
                   pl.BlockSpec((None, 512, FF_TILE), lambda i, j, k: (j, i, 0)))
    dr1, dg1, db1 = _ffn_in_bwd(dh2, wup_st, dpg, wpg, dr2, r1, ln1_g, 512)
    da_bf, db_bf, dh3, dya, dyb = _mix_bwd(dr1, h, a_br, b_br, wo, wb0, wb1, 256)
    d_wo = _mm_tn("out_proj_wgrad", m_bf, dr1, 512, 512)
    d_wb0 = _mm_tn("branch0_wgrad", ya, da_bf, 512, D_MODEL)
    d_wb1 = _mm_tn("branch1_wgrad", yb, db_bf, 512, D_MODEL)
    grads_1 = [d_wb0.reshape(4, 256, D_MODEL), d_wb1.reshape(4, 256, D_MODEL),
               d_wo.reshape(4, 256, D_MODEL), d_wup_st, d_wd.reshape(4, D_FF // 4, D_MODEL),
               d_wpp_st, d_wpg.reshape(4, 256, D_MODEL)]
    (dh0, dws, dbs, dgv, dbv), recv_a1 = _sgu_bwd(h, dya, wm, wmt, bsb, sgu_norm_g, sgu_norm_b, maskf,
                                                  comm=_sibling_exchange_comm(grads_1))
    parts_1 = [_rs_add_halves("rs_add_halves%d" % (i + 1), g, r, core)
               for i, (g, r) in enumerate(zip(grads_1, recv_a1))]
    (dh1, dh2h, dlb, dgn), recv_b1 = _hgrn_bwd(h, dyb, st_all, hgrn_lb_logits, hgrn_norm_g,
                                                comm=_chip_exchange_comm(parts_1))
    dh_parts = [dh0, dh1, dh2h, dh3]
    d_win = [_mm_tn("in_proj_wgrad%d" % j, x2b, dh_parts[j], 512, D_MODEL) for j in range(4)]

    grads_0 = [jnp.stack(d_win)]
    recv_a0 = _run_comm("rs_sibling_exchange0", _sibling_exchange_comm(grads_0))
    parts_0 = [_rs_add_halves("rs_add_halves0", grads_0[0], recv_a0[0], core)]
    gx, recv_b0 = _in_proj_xgrad(dh_parts, win_st, dr1, 512, comm=_chip_exchange_comm(parts_0))
    parts = parts_0 + parts_1
    recv_b = list(recv_b0) + list(recv_b1)
    halves = [_rs_sum_chips("rs_sum_chips%d" % i, pt, r, chip_id)
              for i, (pt, r) in enumerate(zip(parts, recv_b))]
    theirs = _rs_send_halves(halves)
    big_out = [_adamw_rows("adamw_big%d" % i, halves[i], theirs[i], big_w[i], big_m[i], big_v[i], core)
               for i in range(len(halves))]

    small_in = dict(sgu_w_s=(sgu_w_s, m_sgu_w_s, v_sgu_w_s), sgu_b_s=(sgu_b_s, m_sgu_b_s, v_sgu_b_s),
                    sgu_norm_g=(sgu_norm_g, m_sgu_norm_g, v_sgu_norm_g),
                    sgu_norm_b=(sgu_norm_b, m_sgu_norm_b, v_sgu_norm_b),
                    hgrn_norm_g=(hgrn_norm_g, m_hgrn_norm_g, v_hgrn_norm_g),
                    ln1_g=(ln1_g, m_ln1_g, v_ln1_g), ln1_b=(ln1_b, m_ln1_b, v_ln1_b),
                    ffn_conv_b=(ffn_conv_b, m_ffn_conv_b, v_ffn_conv_b),
                    ln2_g=(ln2_g, m_ln2_g, v_ln2_g), ln2_b=(ln2_b, m_ln2_b, v_ln2_b))

    def flat(name, arr):
        rows = dict((n, r) for n, r, _, _ in SMALL_LAYOUT)[name]
        return arr.reshape(rows, arr.size // rows)

    names = [n for n, _, _, _ in SMALL_LAYOUT]
    sw = [flat(n, small_in[n][0]) for n in names]
    sm = [flat(n, small_in[n][1]) for n in names]
    sv = [flat(n, small_in[n][2]) for n in names]
    loss_rows, dcw_tot, lg_out, small_out = _small_allreduce_adamw(
        [dgv, dbv, dlb, dgn, dg1, db1, dg2, db2, loss_acc], dws.reshape(N_GROUP * 128, 128), dbs, dcw, dcb,
        hgrn_lb_logits, m_hgrn_lb_logits, v_hgrn_lb_logits, sw, sm, sv)
    loss = loss_rows[0, 0]

    chip = 2 * lax.axis_index("x") + lax.axis_index("y")
    g_cw = lax.dynamic_slice(dcw_tot, (0, chip * (D_FF // 4)), (3, D_FF // 4))
    cw_out = _adamw_whole("adamw_conv_w", g_cw, ffn_conv_w[0], m_ffn_conv_w[0], v_ffn_conv_w[0])

    res = {}
    for si, n in enumerate(names):
        shp = small_in[n][0].shape
        res[n] = tuple(small_out[4 * si + k].reshape(shp) for k in range(4))
    res["hgrn_lb_logits"] = tuple(lg_out)
    res["ffn_conv_w"] = (g_cw[None],) + tuple(o[None] for o in cw_out)

    def big(i):
        return tuple(big_out[i])

    res["w_in"] = tuple(o[None] for o in big(0))
    res["w_branch"] = tuple(jnp.stack([o0, o1])[None] for o0, o1 in zip(big(1), big(2)))
    res["w_out"] = tuple(o[None] for o in big(3))
    res["ffn_w_up"] = tuple(o[None] for o in big(4))
    res["ffn_w_down"] = tuple(o[None] for o in big(5))
    res["ple_w_proj"] = tuple(o[None] for o in big(6))
    res["ple_w_gate"] = tuple(o[None] for o in big(7))

    order = ["w_in", "sgu_w_s", "sgu_b_s", "sgu_norm_g", "sgu_norm_b", "hgrn_lb_logits",
             "hgrn_norm_g", "w_branch", "w_out", "ln1_g", "ln1_b", "ffn_w_up", "ffn_conv_w",
             "ffn_conv_b", "ffn_w_down", "ln2_g", "ln2_b", "ple_w_proj", "ple_w_gate"]
    outs = [loss, gx.reshape(1, t, D_MODEL)]
    for k in range(4):
        outs += [res[n][k] for n in order]
    return tuple(outs)
```

```python
import functools

import jax
import jax.numpy as jnp
from jax import lax
from jax.experimental import pallas as pl
from jax.experimental.pallas import tpu as pltpu

F32 = jnp.float32
BF16 = jnp.bfloat16
HIGHEST = lax.Precision.HIGHEST
MESH = pl.DeviceIdType.MESH

D_MODEL = 1024
CHUNK = 64
SGU_BLOCK = 128
N_GROUP = 8
N_HEAD = 8
HEAD_DIM = 128
D_FF = 2816
PLE_DIM = 256
IN_COLS = 8192
LN_EPS = 1e-5
RMS_EPS = 1e-6
ALPHA = 2.0 ** 0.25
N_CHIP = 4
N_DEV = 8

ADAM_LR = 0.001
ADAM_B1 = 0.9
ADAM_B2 = 0.999
ADAM_EPS = 1e-08
ADAM_WD = 0.01
ADAM_STEP = 10

VMEM_LIMIT = 56 * 1024 * 1024

NN = (((1,), (0,)), ((), ()))
NT = (((1,), (1,)), ((), ()))
TN = (((0,), (0,)), ((), ()))


def _pc(body, *, name, out_shape, grid=None, in_specs=None, out_specs=None, scratch=(),
        sem=None, nsp=0, vmem=VMEM_LIMIT):
    params = dict(vmem_limit_bytes=vmem)
    if sem is not None:
        params["dimension_semantics"] = sem
    kw = dict(name=name, out_shape=out_shape, compiler_params=pltpu.CompilerParams(**params))
    if nsp:
        kw["grid_spec"] = pltpu.PrefetchScalarGridSpec(
            num_scalar_prefetch=nsp, grid=grid, in_specs=in_specs, out_specs=out_specs,
            scratch_shapes=list(scratch))
    else:
        if grid is not None:
            kw["grid"] = grid
        if in_specs is not None:
            kw["in_specs"] = in_specs
            kw["out_specs"] = out_specs
        kw["scratch_shapes"] = list(scratch)
    return pl.pallas_call(body, **kw)


def _dot(a, b, dims=NN):
    return lax.dot_general(a.astype(BF16), b.astype(BF16), dims, preferred_element_type=F32)


def _dot32(a, b, dims=NN):
    return lax.dot_general(a, b, dims, precision=HIGHEST, preferred_element_type=F32)


def _sig(x):
    return 1.0 / (1.0 + jnp.exp(-x))


_GC = 0.7978845608028654
_GA = 0.044715


def _gelu(x):
    return 0.5 * x * (1.0 + jnp.tanh(_GC * (x + _GA * x * x * x)))


def _gelu_and_grad(x):
    t = jnp.tanh(_GC * (x + _GA * x * x * x))
    g = 0.5 * x * (1.0 + t)
    dg = 0.5 * (1.0 + t) + 0.5 * x * (1.0 - t * t) * _GC * (1.0 + 3.0 * _GA * x * x)
    return g, dg


def _ln_stats(r):
    mu = jnp.mean(r, axis=-1, keepdims=True)
    xc = r - mu
    var = jnp.mean(xc * xc, axis=-1, keepdims=True)
    rstd = lax.rsqrt(var + LN_EPS)
    return xc * rstd, rstd


def _ln_bwd(dxh, xh, rstd):
    m1 = jnp.mean(dxh, axis=-1, keepdims=True)
    m2 = jnp.mean(dxh * xh, axis=-1, keepdims=True)
    return rstd * (dxh - m1 - xh * m2)


def _colsum8(v):
    return jnp.broadcast_to(jnp.sum(v, axis=0, keepdims=True), (8, v.shape[1]))


def _adamw(w, g, m, v):
    m2 = ADAM_B1 * m + (1.0 - ADAM_B1) * g
    v2 = ADAM_B2 * v + (1.0 - ADAM_B2) * (g * g)
    m_hat = m2 / (1.0 - ADAM_B1 ** ADAM_STEP)
    v_hat = v2 / (1.0 - ADAM_B2 ** ADAM_STEP)
    delta = -ADAM_LR * (m_hat / (jnp.sqrt(v_hat) + ADAM_EPS) + ADAM_WD * w)
    return delta, m2, v2


def _row_tile(rows, cols, itemsize=4, budget=1 << 20, mult=8):
    best = mult
    for tr in range(mult, rows + 1, mult):
        if rows % tr == 0 and tr * cols * itemsize <= budget:
            best = tr
    return best


def _mm(name, a, b, dims, grid, a_spec, b_spec, out_shape, o_spec, add=None, add_spec=None,
        add_scale=1.0, comm=None):
    nk = grid[2]
    has_add = add is not None
    out_dtype = out_shape.dtype

    def body(*refs):
        if has_add:
            a_ref, b_ref, add_ref, o_ref = refs[:4]
            rest = refs[4:]
        else:
            a_ref, b_ref, o_ref = refs[:3]
            add_ref = None
            rest = refs[3:]
        prod = _dot(a_ref[...], b_ref[...], dims)

        def finish(acc):
            if has_add:
                acc = acc + add_scale * add_ref[...]
            o_ref[...] = acc.astype(out_dtype)

        if nk == 1:
            finish(prod)
        else:
            acc_ref = rest[0]
            k = pl.program_id(2)

            @pl.when(k == 0)
            def _():
                acc_ref[...] = prod

            @pl.when(k > 0)
            def _():
                acc_ref[...] += prod

            @pl.when(k == nk - 1)
            def _():
                finish(acc_ref[...])

    in_specs = [a_spec, b_spec] + ([add_spec] if has_add else [])
    args = [a, b] + ([add] if has_add else [])
    scratch = []
    if nk > 1:
        blk = [d for d in o_spec.block_shape if d is not None]
        scratch = [pltpu.VMEM(tuple(blk), F32)]
    if comm is None:
        return _pc(body, name=name, out_shape=out_shape, grid=grid, in_specs=in_specs,
                   out_specs=o_spec, scratch=scratch,
                   sem=("parallel", "parallel", "arbitrary"))(*args)

    def first():
        return (pl.program_id(0) == 0) & (pl.program_id(1) == 0) & (pl.program_id(2) == 0)

    def last():
        return ((pl.program_id(0) == grid[0] - 1) & (pl.program_id(1) == grid[1] - 1)
                & (pl.program_id(2) == grid[2] - 1))

    res = _hosted_call(body, comm, first, last, name=name, out_shape=(out_shape,), grid=grid,
                       in_specs=in_specs, out_specs=(o_spec,), scratch=scratch,
                       sem=("arbitrary", "arbitrary", "arbitrary"), args=args)
    return res[0], res[1:]


class _Comm:
    def __init__(self, ins, out_shapes, sems, start, finish):
        self.ins, self.out_shapes, self.sems = list(ins), list(out_shapes), list(sems)
        self.start, self.finish = start, finish


def _hosted_call(body, comm, first, last, *, name, out_shape, grid, in_specs, out_specs, scratch, sem,
                 args):
    n_in, n_out, n_scr = len(in_specs), len(out_shape), len(scratch)
    nci, nco = len(comm.ins), len(comm.out_shapes)

    def wrapped(*refs):
        pos = n_in
        own_in, c_in = refs[:pos], refs[pos:pos + nci]
        pos += nci
        own_out, c_out = refs[pos:pos + n_out], refs[pos + n_out:pos + n_out + nco]
        pos += n_out + nco
        own_scr, c_sem = refs[pos:pos + n_scr], refs[pos + n_scr:]

        @pl.when(first())
        def _():
            comm.start(c_in, c_out, c_sem)

        body(*own_in, *own_out, *own_scr)

        @pl.when(last())
        def _():
            comm.finish(c_in, c_out, c_sem)

    return _pc(wrapped, name=name, out_shape=tuple(out_shape) + tuple(comm.out_shapes), grid=grid,
               in_specs=list(in_specs) + [ANY] * nci, out_specs=tuple(out_specs) + tuple([ANY] * nco),
               scratch=list(scratch) + comm.sems, sem=sem)(*args, *comm.ins)


def _grid1_call(body, comm, n, *, name, out_shape, in_specs, out_specs, scratch, args):
    if comm is None:
        return _pc(body, name=name, out_shape=out_shape, grid=(n,), in_specs=in_specs,
                   out_specs=out_specs, scratch=scratch, sem=("arbitrary",))(*args), ()
    res = _hosted_call(body, comm, lambda: pl.program_id(0) == 0, lambda: pl.program_id(0) == n - 1,
                       name=name, out_shape=out_shape, grid=(n,), in_specs=in_specs,
                       out_specs=out_specs, scratch=scratch, sem=("arbitrary",), args=args)
    return res[:len(out_shape)], res[len(out_shape):]


def _run_comm(name, comm):
    nci, nco = len(comm.ins), len(comm.out_shapes)

    def body(*refs):
        c_in, c_out, c_sem = refs[:nci], refs[nci:nci + nco], refs[nci + nco:]
        comm.start(c_in, c_out, c_sem)
        comm.finish(c_in, c_out, c_sem)

    return _pc(body, name=name, out_shape=tuple(comm.out_shapes), in_specs=[ANY] * nci,
               out_specs=tuple([ANY] * nco), scratch=comm.sems)(*comm.ins)


def _mm_nn_stacked(name, a, w_st, tm, comm=None):
    t, k = a.shape
    _, _, c = w_st.shape
    return _mm(name, a, w_st, NN, (t // tm, N_CHIP, 1),
               pl.BlockSpec((tm, k), lambda i, j, kk: (i, 0)),
               pl.BlockSpec((None, k, c), lambda i, j, kk: (j, 0, 0)),
               jax.ShapeDtypeStruct((t, N_CHIP * c), BF16),
               pl.BlockSpec((tm, c), lambda i, j, kk: (i, j)), comm=comm)


def _mm_tn(name, a, b, tm, tn, stacked=False):
    t, m = a.shape
    _, n = b.shape
    if stacked:
        assert tm == m
        out_shape = jax.ShapeDtypeStruct((n // tn, m, tn), BF16)
        o_spec = pl.BlockSpec((None, tm, tn), lambda i, j, kk: (j, 0, 0))
    else:
        out_shape = jax.ShapeDtypeStruct((m, n), BF16)
        o_spec = pl.BlockSpec((tm, tn), lambda i, j, kk: (i, j))
    return _mm(name, a, b, TN, (m // tm, n // tn, 1),
               pl.BlockSpec((t, tm), lambda i, j, kk: (0, i)),
               pl.BlockSpec((t, tn), lambda i, j, kk: (0, j)),
               out_shape, o_spec)


def _in_proj_xgrad(dh_parts, win_st, dr1, tm, comm=None):
    t = dr1.shape[0]
    ni = t // tm

    def body(a0, a1, a2, a3, b_ref, add_ref, o_ref, acc):
        j = pl.program_id(1)
        for jj, a_ref in enumerate((a0, a1, a2, a3)):
            @pl.when(j == jj)
            def _(jj=jj, a_ref=a_ref):
                prod = _dot(a_ref[...], b_ref[...], NT)
                if jj == 0:
                    acc[...] = prod + ALPHA * add_ref[...]
                elif jj < N_CHIP - 1:
                    acc[...] += prod
                else:
                    o_ref[...] = acc[...] + prod

    a_spec = pl.BlockSpec((tm, 2 * D_MODEL), lambda i, j: (i, 0))
    tile = pl.BlockSpec((tm, D_MODEL), lambda i, j: (i, 0))
    kw = dict(name="in_proj_xgrad", out_shape=(jax.ShapeDtypeStruct((t, D_MODEL), F32),),
              grid=(ni, N_CHIP),
              in_specs=[a_spec] * 4 + [pl.BlockSpec((None, D_MODEL, 2 * D_MODEL), lambda i, j: (j, 0, 0)),
                                       tile],
              out_specs=(tile,), scratch=[pltpu.VMEM((tm, D_MODEL), F32)],
              sem=("arbitrary", "arbitrary"))
    args = list(dh_parts) + [win_st, dr1]
    if comm is None:
        return _pc(body, **kw)(*args)[0], ()
    res = _hosted_call(body, comm,
                       lambda: (pl.program_id(0) == 0) & (pl.program_id(1) == 0),
                       lambda: (pl.program_id(0) == ni - 1) & (pl.program_id(1) == N_CHIP - 1),
                       args=args, **kw)
    return res[0], res[1:]


def _sgu_mixed(v, wm_ref, bsb_ref, gv, bv):
    gl, dgl = _gelu_and_grad(v)
    vh, rstd = _ln_stats(gl)
    vn = vh * gv + bv
    mixed = []
    for g in range(N_GROUP):
        sl = slice(g * 128, (g + 1) * 128)
        mixed.append(_dot(wm_ref[g], vn[:, sl]) + bsb_ref[g])
    return dgl, vh, rstd, vn, mixed


def _sgu_fwd(h, wm, bsb, gv, bv, comm=None):
    t = h.shape[0]

    def body(u_ref, v_ref, wm_ref, bsb_ref, gv_ref, bv_ref, ya_ref):
        u = u_ref[...].astype(F32)
        _, _, _, _, mixed = _sgu_mixed(v_ref[...].astype(F32), wm_ref, bsb_ref, gv_ref[...], bv_ref[...])
        gu = _gelu(u)
        for g in range(N_GROUP):
            sl = slice(g * 128, (g + 1) * 128)
            ya_ref[:, sl] = (gu[:, sl] * mixed[g]).astype(BF16)

    full3 = pl.BlockSpec((N_GROUP, 128, 128), lambda i: (0, 0, 0))
    vec = pl.BlockSpec((1, D_MODEL), lambda i: (0, 0))
    (ya,), extra = _grid1_call(
        body, comm, t // SGU_BLOCK, name="sgu_fwd",
        out_shape=(jax.ShapeDtypeStruct((t, D_MODEL), BF16),),
        in_specs=[pl.BlockSpec((SGU_BLOCK, D_MODEL), lambda i: (i, 0)),
                  pl.BlockSpec((SGU_BLOCK, D_MODEL), lambda i: (i, 1)),
                  full3, full3, vec, vec],
        out_specs=(pl.BlockSpec((SGU_BLOCK, D_MODEL), lambda i: (i, 0)),),
        scratch=[], args=(h, h, wm, bsb, gv, bv))
    return ya, extra


def _sgu_bwd(h, dya, wm, wmt, bsb, gv, bv, maskf, comm=None):
    t = h.shape[0]
    nb = t // SGU_BLOCK

    def body(u_ref, v_ref, dya_ref, wm_ref, wmt_ref, bsb_ref, gv_ref, bv_ref, mask_ref,
             dh_ref, dws_ref, dbs_ref, dgv_ref, dbv_ref, dmix_acc):
        i = pl.program_id(0)

        @pl.when(i == 0)
        def _():
            dws_ref[...] = jnp.zeros_like(dws_ref)
            dgv_ref[...] = jnp.zeros_like(dgv_ref)
            dbv_ref[...] = jnp.zeros_like(dbv_ref)
            dmix_acc[...] = jnp.zeros_like(dmix_acc)

        u = u_ref[...].astype(F32)
        gvv = gv_ref[...]
        dgl_v, vh, rstd, vn, mixed = _sgu_mixed(v_ref[...].astype(F32), wm_ref, bsb_ref, gvv, bv_ref[...])
        gu, dgl_u = _gelu_and_grad(u)
        dya_v = dya_ref[...]
        dvn_parts = []
        for g in range(N_GROUP):
            sl = slice(g * 128, (g + 1) * 128)
            d_y = dya_v[:, sl]
            dh_ref[:, sl] = (d_y * mixed[g] * dgl_u[:, sl]).astype(BF16)
            d_mixed = d_y * gu[:, sl]
            dmix_acc[g] += d_mixed
            dws_ref[g] += _dot(d_mixed, vn[:, sl], NT) * mask_ref[...]
            dvn_parts.append(_dot(wmt_ref[g], d_mixed))
        dvn = jnp.concatenate(dvn_parts, axis=1)
        dgv_ref[...] += _colsum8(dvn * vh)
        dbv_ref[...] += _colsum8(dvn)
        d_gl = _ln_bwd(dvn * gvv, vh, rstd)
        dh_ref[:, D_MODEL:] = (d_gl * dgl_v).astype(BF16)

        @pl.when(i == nb - 1)
        def _():
            rowid = lax.broadcasted_iota(jnp.int32, (8, 128), 0)
            ones = jnp.ones((8, 128), F32)
            acc = jnp.zeros((8, 128), F32)
            for g in range(N_GROUP):
                rs = _dot32(ones, dmix_acc[g], NT)
                acc = jnp.where(rowid == g, rs, acc)
            dbs_ref[...] = acc

    full3 = pl.BlockSpec((N_GROUP, 128, 128), lambda i: (0, 0, 0))
    vec = pl.BlockSpec((1, D_MODEL), lambda i: (0, 0))
    acc8 = pl.BlockSpec((8, D_MODEL), lambda i: (0, 0))
    return _grid1_call(
        body, comm, nb, name="sgu_bwd",
        out_shape=(jax.ShapeDtypeStruct((t, 2 * D_MODEL), BF16),
                   jax.ShapeDtypeStruct((N_GROUP, 128, 128), F32),
                   jax.ShapeDtypeStruct((8, 128), F32),
                   jax.ShapeDtypeStruct((8, D_MODEL), F32),
                   jax.ShapeDtypeStruct((8, D_MODEL), F32)),
        in_specs=[pl.BlockSpec((SGU_BLOCK, D_MODEL), lambda i: (i, 0)),
                  pl.BlockSpec((SGU_BLOCK, D_MODEL), lambda i: (i, 1)),
                  pl.BlockSpec((SGU_BLOCK, D_MODEL), lambda i: (i, 0)),
                  full3, full3, full3, vec, vec,
                  pl.BlockSpec((128, 128), lambda i: (0, 0))],
        out_specs=(pl.BlockSpec((SGU_BLOCK, 2 * D_MODEL), lambda i: (i, 0)),
                   full3, pl.BlockSpec((8, 128), lambda i: (0, 0)), acc8, acc8),
        scratch=[pltpu.VMEM((N_GROUP, 128, 128), F32)],
        args=(h, h, dya, wm, wmt, bsb, gv, bv, maskf))


def _tri_masks():
    row = lax.broadcasted_iota(jnp.int32, (CHUNK, CHUNK), 0)
    col = lax.broadcasted_iota(jnp.int32, (CHUNK, CHUNK), 1)
    return col <= row, col >= row


def _heads(v):
    return [v[:, hd * HEAD_DIM:(hd + 1) * HEAD_DIM] for hd in range(N_HEAD)]


def _tri_cumsum(tri_bf, v):
    hi = v.astype(BF16)
    r = v - hi.astype(F32)
    mid = r.astype(BF16)
    lo = (r - mid.astype(F32)).astype(BF16)
    return _dot(tri_bf, hi) + _dot(tri_bf, mid) + _dot(tri_bf, lo)


def _hgrn_chunk(q, fp, ii, lb, st_heads, causal):
    sg = _sig(fp)
    f = lb + (1.0 - lb) * sg
    k = 1.0 - f
    c = _tri_cumsum(causal.astype(BF16), jnp.log(f))
    ec = jnp.exp(c)
    en = jnp.exp(-c)
    sq = _sig(q)
    qt = q * sq * ec
    kt = k * en
    ecl = jnp.exp(c[CHUNK - 1:CHUNK, :])
    kk = kt * ecl
    qtb, ktb, iib, kkb = qt.astype(BF16), kt.astype(BF16), ii.astype(BF16), kk.astype(BF16)
    attn, o = [], []
    for hd, (qh, kh, ih) in enumerate(zip(_heads(qtb), _heads(ktb), _heads(iib))):
        a = jnp.where(causal, _dot(qh, kh, NT), 0.0).astype(BF16)
        attn.append(a)
        o.append(_dot(a, ih) + _dot(qh, st_heads[hd], NT))
    return dict(sg=sg, f=f, k=k, ec=ec, en=en, sq=sq, ecl=ecl, kk=kk, qtb=qtb, ktb=ktb, iib=iib,
                kkb=kkb, attn=attn, o=o)


def _rms_heads(o_heads):
    rinv = [lax.rsqrt(jnp.mean(o * o, axis=-1, keepdims=True) + RMS_EPS) for o in o_heads]
    return rinv, jnp.concatenate([o * r for o, r in zip(o_heads, rinv)], axis=1)


HG_CHUNKS = 4
HG_ROWS = HG_CHUNKS * CHUNK


def _hgrn_fwd(h, logits, gn, comm=None):
    t = h.shape[0]
    nb = t // HG_ROWS

    def body(q_ref, f_ref, i_ref, og_ref, lg_ref, gn_ref, yb_ref, st_ref, state):
        @pl.when(pl.program_id(0) == 0)
        def _():
            state[...] = jnp.zeros_like(state)

        causal, _ = _tri_masks()
        lb = _sig(lg_ref[0:1, :] - lg_ref[1:2, :])
        gnv = gn_ref[...]
        st = [state[hd] for hd in range(N_HEAD)]
        for cc in range(HG_CHUNKS):
            rows = slice(cc * CHUNK, (cc + 1) * CHUNK)
            og = og_ref[rows, :].astype(F32)
            r = _hgrn_chunk(q_ref[rows, :].astype(F32), f_ref[rows, :].astype(F32),
                            i_ref[rows, :].astype(F32), lb, [s.astype(BF16) for s in st], causal)
            _, on = _rms_heads(r["o"])
            yb_ref[rows, :] = (on * gnv * (og * _sig(og))).astype(BF16)
            for hd in range(N_HEAD):
                st_ref[cc, hd] = st[hd]
            st = [s * e + _dot(ih, kh, TN)
                  for s, e, ih, kh in zip(st, _heads(r["ecl"]), _heads(r["iib"]), _heads(r["kkb"]))]
        for hd in range(N_HEAD):
            state[hd] = st[hd]

    def col(k):
        return pl.BlockSpec((HG_ROWS, D_MODEL), lambda ci: (ci, k))

    return _grid1_call(body, comm, nb, name="hgrn_fwd",
                       out_shape=(jax.ShapeDtypeStruct((t, D_MODEL), BF16),
                                  jax.ShapeDtypeStruct((t // CHUNK, N_HEAD, HEAD_DIM, HEAD_DIM), F32)),
                       in_specs=[col(2), col(3), col(4), col(5),
                                 pl.BlockSpec((2, D_MODEL), lambda ci: (0, 0)),
                                 pl.BlockSpec((1, D_MODEL), lambda ci: (0, 0))],
                       out_specs=(pl.BlockSpec((HG_ROWS, D_MODEL), lambda ci: (ci, 0)),
                                  pl.BlockSpec((HG_CHUNKS, N_HEAD, HEAD_DIM, HEAD_DIM),
                                               lambda ci: (ci, 0, 0, 0))),
                       scratch=[pltpu.VMEM((N_HEAD, HEAD_DIM, HEAD_DIM), F32)],
                       args=(h, h, h, h, logits, gn))


def _hgrn_chunk_bwd(q, fp, ii, og, dy, gnv, lb, st, dsn, causal, anti):
    stb = [s.astype(BF16) for s in st]
    dsnb = [s.astype(BF16) for s in dsn]
    r = _hgrn_chunk(q, fp, ii, lb, stb, causal)
    rinv, on = _rms_heads(r["o"])
    so = _sig(og)
    sil = og * so
    d_og = dy * on * gnv * (so * (1.0 + og * (1.0 - so)))
    d_on = dy * gnv * sil
    d_ob = jnp.concatenate(
        [ri * (dn - oh * jnp.mean(dn * oh, axis=-1, keepdims=True))
         for ri, dn, oh in zip(rinv, _heads(d_on), _heads(on))], axis=1).astype(BF16)
    d_i, d_qt, d_kt, d_kk, d_st, st_dsn = [], [], [], [], [], []
    ecl = _heads(r["ecl"])
    for hd, (dh, qh, kh, ih, kkh) in enumerate(zip(_heads(d_ob), _heads(r["qtb"]), _heads(r["ktb"]),
                                                   _heads(r["iib"]), _heads(r["kkb"]))):
        d_attn = jnp.where(causal, _dot(dh, ih, NT), 0.0).astype(BF16)
        d_i.append(_dot(r["attn"][hd], dh, TN) + _dot(kkh, dsnb[hd], NT))
        d_qt.append(_dot(d_attn, kh) + _dot(dh, stb[hd]))
        d_kt.append(_dot(d_attn, qh, TN))
        d_kk.append(_dot(ih, dsnb[hd]))
        d_st.append(_dot(dh, qh, TN) + dsn[hd] * ecl[hd])
        st_dsn.append(jnp.sum(st[hd] * dsn[hd], axis=0, keepdims=True))
    d_qt = jnp.concatenate(d_qt, axis=1)
    d_kt = jnp.concatenate(d_kt, axis=1)
    d_kk = jnp.concatenate(d_kk, axis=1)
    kk = r["kk"]
    d_cl = r["ecl"] * jnp.concatenate(st_dsn, axis=1) + jnp.sum(kk * d_kk, axis=0, keepdims=True)
    d_k = (d_kk * r["ecl"] + d_kt) * r["en"]
    d_c = d_qt * r["qtb"].astype(F32) - d_kt * r["ktb"].astype(F32) - d_kk * kk
    rowid = lax.broadcasted_iota(jnp.int32, (CHUNK, D_MODEL), 0)
    d_c = d_c + jnp.where(rowid == CHUNK - 1, d_cl, 0.0)
    d_lf = _tri_cumsum(anti.astype(BF16), d_c)
    d_f = d_lf / r["f"] - d_k
    sg, sq = r["sg"], r["sq"]
    d_q = d_qt * r["ec"] * (sq * (1.0 + q * (1.0 - sq)))
    d_fp = d_f * (1.0 - lb) * sg * (1.0 - sg)
    return (d_q, d_fp, jnp.concatenate(d_i, axis=1), d_og, d_st,
            _colsum8(dy * on * sil), _colsum8(d_f * (1.0 - sg)))


def _hgrn_bwd(h, dyb, st_all, logits, gn, comm=None):
    t = h.shape[0]
    nb = t // HG_ROWS

    def body(q_ref, f_ref, i_ref, og_ref, dyb_ref, st_ref, lg_ref, gn_ref,
             dh1_ref, dh2_ref, dlb_ref, dgn_ref, dstate):
        @pl.when(pl.program_id(0) == 0)
        def _():
            dstate[...] = jnp.zeros_like(dstate)
            dlb_ref[...] = jnp.zeros_like(dlb_ref)
            dgn_ref[...] = jnp.zeros_like(dgn_ref)

        causal, anti = _tri_masks()
        lb = _sig(lg_ref[0:1, :] - lg_ref[1:2, :])
        gnv = gn_ref[...]
        dsn = [dstate[hd] for hd in range(N_HEAD)]
        dgn_acc = jnp.zeros((8, D_MODEL), F32)
        dlb_acc = jnp.zeros((8, D_MODEL), F32)
        for cc in reversed(range(HG_CHUNKS)):
            rows = slice(cc * CHUNK, (cc + 1) * CHUNK)
            d_q, d_fp, d_i, d_og, dsn, dgn_c, dlb_c = _hgrn_chunk_bwd(
                q_ref[rows, :].astype(F32), f_ref[rows, :].astype(F32), i_ref[rows, :].astype(F32),
                og_ref[rows, :].astype(F32), dyb_ref[rows, :], gnv, lb,
                [st_ref[cc, hd] for hd in range(N_HEAD)], dsn, causal, anti)
            dgn_acc = dgn_acc + dgn_c
            dlb_acc = dlb_acc + dlb_c
            dh1_ref[rows, :D_MODEL] = d_q.astype(BF16)
            dh1_ref[rows, D_MODEL:] = d_fp.astype(BF16)
            dh2_ref[rows, :D_MODEL] = d_i.astype(BF16)
            dh2_ref[rows, D_MODEL:] = d_og.astype(BF16)
        dgn_ref[...] += dgn_acc
        dlb_ref[...] += dlb_acc
        for hd in range(N_HEAD):
            dstate[hd] = dsn[hd]

    def col(k):
        return pl.BlockSpec((HG_ROWS, D_MODEL), lambda ci: (nb - 1 - ci, k))

    acc8 = pl.BlockSpec((8, D_MODEL), lambda ci: (0, 0))
    pair = pl.BlockSpec((HG_ROWS, 2 * D_MODEL), lambda ci: (nb - 1 - ci, 0))
    return _grid1_call(body, comm, nb, name="hgrn_bwd",
                       out_shape=(jax.ShapeDtypeStruct((t, 2 * D_MODEL), BF16),
                                  jax.ShapeDtypeStruct((t, 2 * D_MODEL), BF16),
                                  jax.ShapeDtypeStruct((8, D_MODEL), F32),
                                  jax.ShapeDtypeStruct((8, D_MODEL), F32)),
                       in_specs=[col(2), col(3), col(4), col(5),
                                 pl.BlockSpec((HG_ROWS, D_MODEL), lambda ci: (nb - 1 - ci, 0)),
                                 pl.BlockSpec((HG_CHUNKS, N_HEAD, HEAD_DIM, HEAD_DIM),
                                              lambda ci: (nb - 1 - ci, 0, 0, 0)),
                                 pl.BlockSpec((2, D_MODEL), lambda ci: (0, 0)),
                                 pl.BlockSpec((1, D_MODEL), lambda ci: (0, 0))],
                       out_specs=(pair, pair, acc8, acc8),
                       scratch=[pltpu.VMEM((N_HEAD, HEAD_DIM, HEAD_DIM), F32)],
                       args=(h, h, h, h, dyb, st_all, logits, gn))


def _mix_fwd(ya, yb, h, x, wb0, wb1, wo, g1, b1, tm, comm=None):
    t = x.shape[0]

    def body(ya_ref, yb_ref, ga_ref, gb_ref, x_ref, wb0_ref, wb1_ref, wo_ref, g1_ref, b1_ref,
             r1_ref, a_ref, b_ref, m_ref, x1_ref):
        a = _dot(ya_ref[...], wb0_ref[...])
        b = _dot(yb_ref[...], wb1_ref[...])
        m = _sig(ga_ref[...].astype(F32)) * a + _sig(gb_ref[...].astype(F32)) * b
        r1 = ALPHA * x_ref[...] + _dot(m, wo_ref[...])
        xh, _ = _ln_stats(r1)
        r1_ref[...] = r1
        a_ref[...] = a
        b_ref[...] = b
        m_ref[...] = m.astype(BF16)
        x1_ref[...] = (xh * g1_ref[...] + b1_ref[...]).astype(BF16)

    tile = pl.BlockSpec((tm, D_MODEL), lambda i: (i, 0))
    wsp = pl.BlockSpec((D_MODEL, D_MODEL), lambda i: (0, 0))
    vec = pl.BlockSpec((1, D_MODEL), lambda i: (0, 0))
    f32o = jax.ShapeDtypeStruct((t, D_MODEL), F32)
    bfo = jax.ShapeDtypeStruct((t, D_MODEL), BF16)
    return _grid1_call(body, comm, t // tm, name="mix_fwd", out_shape=(f32o, f32o, f32o, bfo, bfo),
                       in_specs=[tile, tile,
                                 pl.BlockSpec((tm, D_MODEL), lambda i: (i, 6)),
                                 pl.BlockSpec((tm, D_MODEL), lambda i: (i, 7)),
                                 tile, wsp, wsp, wsp, vec, vec],
                       out_specs=(tile, tile, tile, tile, tile),
                       scratch=[], args=(ya, yb, h, h, x, wb0, wb1, wo, g1, b1))


def _mix_bwd(dr1, h, a, b, wo, wb0, wb1, tm):
    t = dr1.shape[0]

    def body(dr1_ref, ga_ref, gb_ref, a_ref, b_ref, wo_ref, wb0_ref, wb1_ref,
             da_ref, db_ref, dh3_ref, dya_ref, dyb_ref):
        d_m = _dot(dr1_ref[...], wo_ref[...], NT)
        sa = _sig(ga_ref[...].astype(F32))
        sb = _sig(gb_ref[...].astype(F32))
        d_a = (d_m * sa).astype(BF16)
        d_b = (d_m * sb).astype(BF16)
        da_ref[...] = d_a
        db_ref[...] = d_b
        dh3_ref[:, :D_MODEL] = (d_m * a_ref[...] * sa * (1.0 - sa)).astype(BF16)
        dh3_ref[:, D_MODEL:] = (d_m * b_ref[...] * sb * (1.0 - sb)).astype(BF16)
        dya_ref[...] = _dot(d_a, wb0_ref[...], NT)
        dyb_ref[...] = _dot(d_b, wb1_ref[...], NT)

    tile = pl.BlockSpec((tm, D_MODEL), lambda i: (i, 0))
    wsp = pl.BlockSpec((D_MODEL, D_MODEL), lambda i: (0, 0))
    f32o = jax.ShapeDtypeStruct((t, D_MODEL), F32)
    bfo = jax.ShapeDtypeStruct((t, D_MODEL), BF16)
    return _pc(body, name="mix_bwd",
               out_shape=(bfo, bfo, jax.ShapeDtypeStruct((t, 2 * D_MODEL), BF16), f32o, f32o),
               grid=(t // tm,),
               in_specs=[tile,
                         pl.BlockSpec((tm, D_MODEL), lambda i: (i, 6)),
                         pl.BlockSpec((tm, D_MODEL), lambda i: (i, 7)),
                         tile, tile, wsp, wsp, wsp],
               out_specs=(tile, tile, pl.BlockSpec((tm, 2 * D_MODEL), lambda i: (i, 0)),
                          tile, tile),
               sem=("parallel",))(dr1, h, h, a, b, wo, wb0, wb1)


FF_TILE = 1408
FF_NJ = D_FF // FF_TILE


def _shift_down(v, k):
    return pltpu.roll(v, k, 0)


def _shift_up(v, k):
    return pltpu.roll(v, v.shape[0] - k, 0)


def _conv_gate(ext, cw_ref, cb_ref):
    return (cw_ref[0:1, :] * _shift_down(ext, 2) + cw_ref[1:2, :] * _shift_down(ext, 1)
            + cw_ref[2:3, :] * ext + cb_ref[...])


HALO = 16


def _ffn_act_fwd(h2, convw, convb, tm):
    t = h2.shape[0]
    nth = tm // HALO

    def body(g_ref, gp_ref, v_ref, cw_ref, cb_ref, act_ref):
        i = pl.program_id(1)
        prev = gp_ref[...].astype(F32) * (i > 0).astype(F32)
        ext = jnp.concatenate([prev, g_ref[...].astype(F32)], axis=0)
        gc = _conv_gate(ext, cw_ref, cb_ref)[HALO:, :]
        act_ref[...] = (_gelu(gc) * v_ref[...].astype(F32)).astype(BF16)

    return _pc(body, name="ffn_act_fwd", out_shape=jax.ShapeDtypeStruct((t, D_FF), BF16),
               grid=(FF_NJ, t // tm),
               in_specs=[pl.BlockSpec((tm, FF_TILE), lambda j, i: (i, j)),
                         pl.BlockSpec((HALO, FF_TILE), lambda j, i: (jnp.maximum(i * nth - 1, 0), j)),
                         pl.BlockSpec((tm, FF_TILE), lambda j, i: (i, j + FF_NJ)),
                         pl.BlockSpec((3, FF_TILE), lambda j, i: (0, j)),
                         pl.BlockSpec((1, FF_TILE), lambda j, i: (0, j))],
               out_specs=pl.BlockSpec((tm, FF_TILE), lambda j, i: (i, j)),
               sem=("parallel", "parallel"))(h2, h2, h2, convw, convb)


def _ffn_act_bwd(h2, dact, convw, convb, tm):
    t = h2.shape[0]
    nth = tm // HALO
    ni = t // tm
    last_halo = t // HALO - 1
    main_rows = slice(HALO, HALO + tm)

    def body(g_ref, gp_ref, gn_ref, v_ref, vn_ref, da_ref, dan_ref, cw_ref, cb_ref,
             dh2_ref, dcw_ref, dcb_ref):
        i = pl.program_id(1)

        @pl.when(i == 0)
        def _():
            dcw_ref[...] = jnp.zeros_like(dcw_ref)
            dcb_ref[...] = jnp.zeros_like(dcb_ref)

        zeros = jnp.zeros((HALO, FF_TILE), F32)
        da = da_ref[...].astype(F32)
        prev = gp_ref[...].astype(F32) * (i > 0).astype(F32)
        ext = jnp.concatenate([prev, g_ref[...].astype(F32), gn_ref[...].astype(F32)], axis=0)
        vext = jnp.concatenate([zeros, v_ref[...].astype(F32), vn_ref[...].astype(F32)], axis=0)
        dnext = dan_ref[...].astype(F32) * (i < ni - 1).astype(F32)
        dext = jnp.concatenate([zeros, da, dnext], axis=0)
        g2 = _shift_down(ext, 2)
        g1 = _shift_down(ext, 1)
        gc = cw_ref[0:1, :] * g2 + cw_ref[1:2, :] * g1 + cw_ref[2:3, :] * ext + cb_ref[...]
        gl, dgl = _gelu_and_grad(gc)
        d_gc = dext * vext * dgl
        d_gate = (cw_ref[2:3, :] * d_gc + cw_ref[1:2, :] * _shift_up(d_gc, 1)
                  + cw_ref[0:1, :] * _shift_up(d_gc, 2))
        dh2_ref[0] = d_gate[main_rows, :].astype(BF16)
        dh2_ref[1] = (da * gl[main_rows, :]).astype(BF16)
        dm = d_gc[main_rows, :]
        s0 = jnp.sum(dm * g2[main_rows, :], axis=0, keepdims=True)
        s1 = jnp.sum(dm * g1[main_rows, :], axis=0, keepdims=True)
        s2 = jnp.sum(dm * ext[main_rows, :], axis=0, keepdims=True)
        rowid = lax.broadcasted_iota(jnp.int32, (8, FF_TILE), 0)
        dcw_ref[...] += jnp.where(rowid == 0, s0, jnp.where(rowid == 1, s1,
                                                            jnp.where(rowid == 2, s2, 0.0)))
        dcb_ref[...] += _colsum8(dm)

    def prev8(off):
        return pl.BlockSpec((HALO, FF_TILE), lambda j, i: (jnp.maximum(i * nth - 1, 0), j + off))

    def next8(off):
        return pl.BlockSpec((HALO, FF_TILE), lambda j, i: (jnp.minimum((i + 1) * nth, last_halo), j + off))

    def main(off):
        return pl.BlockSpec((tm, FF_TILE), lambda j, i: (i, j + off))

    acc = pl.BlockSpec((8, FF_TILE), lambda j, i: (0, j))
    return _pc(body, name="ffn_act_bwd",
               out_shape=(jax.ShapeDtypeStruct((2, t, D_FF), BF16),
                          jax.ShapeDtypeStruct((8, D_FF), F32),
                          jax.ShapeDtypeStruct((8, D_FF), F32)),
               grid=(FF_NJ, ni),
               in_specs=[main(0), prev8(0), next8(0), main(FF_NJ), next8(FF_NJ),
                         pl.BlockSpec((tm, FF_TILE), lambda j, i: (i, j)),
                         next8(0),
                         pl.BlockSpec((3, FF_TILE), lambda j, i: (0, j)),
                         pl.BlockSpec((1, FF_TILE), lambda j, i: (0, j))],
               out_specs=(pl.BlockSpec((2, tm, FF_TILE), lambda j, i: (0, i, j)), acc, acc),
               sem=("parallel", "arbitrary"))(h2, h2, h2, h2, h2, dact, dact, convw, convb)


def _out_fwd_bwd(act, x1b, r1, p2, tgt, wd, wpg, wpp, g1, b1, g2, b2, tm):
    t = r1.shape[0]

    def body(act_ref, x1b_ref, r1_ref, p_ref, tgt_ref, wd_ref, wpg_ref, wpp_ref,
             g1_ref, b1_ref, g2_ref, b2_ref,
             dr2_ref, dpg_ref, dpp_ref, loss_ref, dg2_ref, db2_ref):
        i = pl.program_id(0)

        @pl.when(i == 0)
        def _():
            loss_ref[...] = jnp.zeros_like(loss_ref)
            dg2_ref[...] = jnp.zeros_like(dg2_ref)
            db2_ref[...] = jnp.zeros_like(db2_ref)

        ffn = _dot(act_ref[...], wd_ref[...])
        pg = _dot(x1b_ref[...], wpg_ref[...])
        pp = _dot(p_ref[...], wpp_ref[...])
        s = _sig(pg)
        xh1, _ = _ln_stats(r1_ref[...])
        x1 = xh1 * g1_ref[...] + b1_ref[...]
        r2 = ALPHA * x1 + ffn + s * pp
        xh2, rstd2 = _ln_stats(r2)
        g2v = g2_ref[...]
        diff = xh2 * g2v + b2_ref[...] - tgt_ref[...]
        part = jnp.sum(jnp.sum(diff * diff, axis=1, keepdims=True), axis=0, keepdims=True)
        loss_ref[...] += jnp.broadcast_to(part * (0.5 / D_MODEL), loss_ref.shape)
        dy = diff * (1.0 / D_MODEL)
        dg2_ref[...] += _colsum8(dy * xh2)
        db2_ref[...] += _colsum8(dy)
        dr2 = _ln_bwd(dy * g2v, xh2, rstd2)
        dr2_ref[...] = dr2
        dpg_ref[...] = (dr2 * pp * s * (1.0 - s)).astype(BF16)
        dpp_ref[...] = (dr2 * s).astype(BF16)

    tile = pl.BlockSpec((tm, D_MODEL), lambda i: (i, 0))
    vec = pl.BlockSpec((1, D_MODEL), lambda i: (0, 0))
    acc8 = pl.BlockSpec((8, D_MODEL), lambda i: (0, 0))
    acc_shape = jax.ShapeDtypeStruct((8, D_MODEL), F32)
    return _pc(body, name="out_fwd_bwd",
               out_shape=(jax.ShapeDtypeStruct((t, D_MODEL), F32),
                          jax.ShapeDtypeStruct((t, D_MODEL), BF16),
                          jax.ShapeDtypeStruct((t, D_MODEL), BF16),
                          acc_shape, acc_shape, acc_shape),
               grid=(t // tm,),
               in_specs=[pl.BlockSpec((tm, D_FF), lambda i: (i, 0)), tile, tile,
                         pl.BlockSpec((tm, PLE_DIM), lambda i: (i, 0)), tile,
                         pl.BlockSpec((D_FF, D_MODEL), lambda i: (0, 0)),
                         pl.BlockSpec((D_MODEL, D_MODEL), lambda i: (0, 0)),
                         pl.BlockSpec((PLE_DIM, D_MODEL), lambda i: (0, 0)),
                         vec, vec, vec, vec],
               out_specs=(tile, tile, tile, acc8, acc8, acc8),
               sem=("arbitrary",))(act, x1b, r1, p2, tgt, wd, wpg, wpp, g1, b1, g2, b2)


def _ffn_in_bwd(dh2, wup_st, dpg, wpg, dr2, r1, g1, tm):
    t = r1.shape[0]
    ni = t // tm

    def body(dh2_ref, wup_ref, dpg_ref, wpg_ref, dr2_ref, r1_ref, g1_ref,
             dr1_ref, dg1_ref, db1_ref, acc):
        i = pl.program_id(0)
        j = pl.program_id(1)

        @pl.when((i == 0) & (j == 0))
        def _():
            dg1_ref[...] = jnp.zeros_like(dg1_ref)
            db1_ref[...] = jnp.zeros_like(db1_ref)

        @pl.when(j == 0)
        def _():
            acc[...] = _dot(dh2_ref[...], wup_ref[...], NT)

        @pl.when(j > 0)
        def _():
            acc[...] += _dot(dh2_ref[...], wup_ref[...], NT)

        @pl.when(j == N_CHIP - 1)
        def _():
            d_x1 = acc[...] + _dot(dpg_ref[...], wpg_ref[...], NT) + ALPHA * dr2_ref[...]
            xh, rstd = _ln_stats(r1_ref[...])
            dg1_ref[...] += _colsum8(d_x1 * xh)
            db1_ref[...] += _colsum8(d_x1)
            dr1_ref[...] = _ln_bwd(d_x1 * g1_ref[...], xh, rstd)

    tile = pl.BlockSpec((tm, D_MODEL), lambda i, j: (i, 0))
    acc8 = pl.BlockSpec((8, D_MODEL), lambda i, j: (0, 0))
    acc_shape = jax.ShapeDtypeStruct((8, D_MODEL), F32)
    return _pc(body, name="ffn_in_bwd",
               out_shape=(jax.ShapeDtypeStruct((t, D_MODEL), F32), acc_shape, acc_shape),
               grid=(ni, N_CHIP),
               in_specs=[pl.BlockSpec((None, tm, FF_TILE), lambda i, j: (j // FF_NJ, i, j % FF_NJ)),
                         pl.BlockSpec((None, D_MODEL, FF_TILE), lambda i, j: (j, 0, 0)),
                         tile, pl.BlockSpec((D_MODEL, D_MODEL), lambda i, j: (0, 0)),
                         tile, tile, pl.BlockSpec((1, D_MODEL), lambda i, j: (0, 0))],
               out_specs=(tile, acc8, acc8),
               scratch=[pltpu.VMEM((tm, D_MODEL), F32)],
               sem=("arbitrary", "arbitrary"))(dh2, wup_st, dpg, wpg, dr2, r1, g1)


ANY = pl.BlockSpec(memory_space=pl.ANY)


def _chip_peers():
    x, y, c = lax.axis_index("x"), lax.axis_index("y"), lax.axis_index("c")
    return x, y, c, [(1 - x, y), (x, 1 - y), (1 - x, 1 - y)]


def _gather_comm(halved, whole=()):
    n, nw = len(halved), len(whole)

    def copies(ins, outs, sems):
        ici_send, ici_recv, d2d_send, d2d_recv, own_send, own_recv = sems
        x, y, c, peers = _chip_peers()
        me = 2 * x + y
        sibling = (x, y, 1 - c)
        own, ici, ici_wait, fwd, fwd_wait = [], [], [], [], []
        for ti in range(n + nw):
            src, dst = ins[ti], outs[ti]
            own.append(pltpu.make_async_remote_copy(
                src_ref=src, dst_ref=dst.at[me], send_sem=own_send.at[ti], recv_sem=own_recv.at[ti],
                device_id=sibling, device_id_type=MESH))
            for k, (px, py) in enumerate(peers):
                pk = 2 * px + py
                sem = dict(send_sem=ici_send.at[ti * 3 + k], recv_sem=ici_recv.at[ti * 3 + k],
                           device_id=(px, py, c), device_id_type=MESH)
                if ti < n:
                    ici.append(pltpu.make_async_remote_copy(src_ref=src.at[c], dst_ref=dst.at[me, c], **sem))
                    ici_wait.append(pltpu.make_async_remote_copy(src_ref=src.at[c], dst_ref=dst.at[pk, c], **sem))
                    dsem = dict(send_sem=d2d_send.at[ti * 3 + k], recv_sem=d2d_recv.at[ti * 3 + k],
                                device_id=sibling, device_id_type=MESH)
                    fwd.append(pltpu.make_async_remote_copy(src_ref=dst.at[pk, c], dst_ref=dst.at[pk, c], **dsem))
                    fwd_wait.append(pltpu.make_async_remote_copy(
                        src_ref=dst.at[pk, 1 - c], dst_ref=dst.at[pk, 1 - c], **dsem))
                else:
                    ici.append(pltpu.make_async_remote_copy(src_ref=src, dst_ref=dst.at[me], **sem))
                    ici_wait.append(pltpu.make_async_remote_copy(src_ref=src, dst_ref=dst.at[pk], **sem))
        return own, ici, ici_wait, fwd, fwd_wait

    def start(ins, outs, sems):
        own, ici, _, _, _ = copies(ins, outs, sems)
        for cp in own + ici:
            cp.start()

    def finish(ins, outs, sems):
        own, ici, ici_wait, fwd, fwd_wait = copies(ins, outs, sems)
        for i, cp in enumerate(ici_wait):
            cp.wait_recv()
            if i < len(fwd):
                fwd[i].start()
        for cp in fwd_wait + own:
            cp.wait_recv()
        for cp in own + ici + fwd:
            cp.wait_send()

    srcs = list(halved) + list(whole)
    return _Comm(srcs, [jax.ShapeDtypeStruct((N_CHIP,) + s.shape, s.dtype) for s in srcs],
                 [pltpu.SemaphoreType.DMA((3 * (n + nw),)), pltpu.SemaphoreType.DMA((3 * (n + nw),)),
                  pltpu.SemaphoreType.DMA((max(3 * n, 1),)), pltpu.SemaphoreType.DMA((max(3 * n, 1),)),
                  pltpu.SemaphoreType.DMA((n + nw,)), pltpu.SemaphoreType.DMA((n + nw,))],
                 start, finish)


def _sibling_exchange_comm(grads):
    n = len(grads)

    def copies(ins, outs, sems):
        send_sems, recv_sems = sems
        x, y, c = lax.axis_index("x"), lax.axis_index("y"), lax.axis_index("c")
        res = []
        for ti in range(n):
            half = ins[ti].shape[1] // 2
            res.append(pltpu.make_async_remote_copy(
                src_ref=ins[ti].at[:, pl.ds(pl.multiple_of((1 - c) * half, 16), half), :],
                dst_ref=outs[ti],
                send_sem=send_sems.at[ti], recv_sem=recv_sems.at[ti],
                device_id=(x, y, 1 - c), device_id_type=MESH))
        return res

    def start(ins, outs, sems):
        for cp in copies(ins, outs, sems):
            cp.start()

    def finish(ins, outs, sems):
        for cp in copies(ins, outs, sems):
            cp.wait()

    return _Comm(grads, [jax.ShapeDtypeStruct((N_CHIP, g.shape[1] // 2, g.shape[2]), g.dtype) for g in grads],
                 [pltpu.SemaphoreType.DMA((n,)), pltpu.SemaphoreType.DMA((n,))], start, finish)


def _in_proj_gathering(x2b, own, chip, tm, comm):
    t = x2b.shape[0]
    ni = t // tm
    half, cols = own.shape[1], own.shape[2]
    nci, nco = len(comm.ins), len(comm.out_shapes)

    def body(chip_ref, x_ref, own_ref, own_hbm, *rest):
        c_in = rest[:nci]
        h_ref, win_out = rest[nci:nci + 2]
        c_out = rest[nci + 2:nci + 2 + nco]
        w_scr, ici_send, ici_recv, d2d_send, d2d_recv, own_sems, ld_sems = rest[nci + 2 + nco:nci + 9 + nco]
        c_sem = rest[nci + 9 + nco:]
        s, i = pl.program_id(0), pl.program_id(1)
        x, y, c, peers = _chip_peers()
        me = 2 * x + y
        sibling = (x, y, 1 - c)

        def ici(k, slot):
            px, py = peers[k]
            return pltpu.make_async_remote_copy(
                src_ref=own_hbm.at[c], dst_ref=win_out.at[slot, c],
                send_sem=ici_send.at[k], recv_sem=ici_recv.at[k],
                device_id=(px, py, c), device_id_type=MESH)

        def forward(k, core):
            pk = 2 * peers[k][0] + peers[k][1]
            return pltpu.make_async_remote_copy(
                src_ref=win_out.at[pk, core], dst_ref=win_out.at[pk, core],
                send_sem=d2d_send.at[k], recv_sem=d2d_recv.at[k],
                device_id=sibling, device_id_type=MESH)

        place_own = pltpu.make_async_remote_copy(
            src_ref=own_hbm, dst_ref=win_out.at[me], send_sem=own_sems.at[0], recv_sem=own_sems.at[1],
            device_id=sibling, device_id_type=MESH)

        @pl.when((s == 0) & (i == 0))
        def _():
            for k in range(3):
                ici(k, me).start()
            place_own.start()

        @pl.when(s == 0)
        def _():
            xv = x_ref[...]
            h_ref[...] = (_dot(xv[:, :half], own_ref[0]) + _dot(xv[:, half:], own_ref[1])).astype(BF16)

        for k in range(3):
            @pl.when((s == k + 1) & (i == 0))
            def _(k=k):
                pk = 2 * peers[k][0] + peers[k][1]
                ici(k, pk).wait_recv()
                forward(k, c).start()
                forward(k, 1 - c).wait_recv()
                loads = [pltpu.make_async_copy(win_out.at[pk, hh], w_scr.at[hh], ld_sems.at[hh])
                         for hh in range(2)]
                for ld in loads:
                    ld.start()
                for ld in loads:
                    ld.wait()
                if k == 1:
                    comm.start(c_in, c_out, c_sem)

        @pl.when(s > 0)
        def _():
            xv = x_ref[...]
            h_ref[...] = (_dot(xv[:, :half], w_scr[0]) + _dot(xv[:, half:], w_scr[1])).astype(BF16)

        @pl.when((s == N_CHIP - 1) & (i == ni - 1))
        def _():
            place_own.wait()
            for k in range(3):
                ici(k, me).wait_send()
                forward(k, c).wait_send()
            comm.finish(c_in, c_out, c_sem)

    def shard_col(s, me):
        return jnp.where(s == 0, me, me ^ jnp.where(s == 1, 2, jnp.where(s == 2, 1, 3)))

    res = _pc(body, name="in_proj",
              out_shape=(jax.ShapeDtypeStruct((t, N_CHIP * cols), BF16),
                         jax.ShapeDtypeStruct((N_CHIP,) + own.shape, own.dtype)) + tuple(comm.out_shapes),
              grid=(N_CHIP, ni), nsp=1,
              in_specs=[pl.BlockSpec((tm, 2 * half), lambda s, i, chip_ref: (i, 0)),
                        pl.BlockSpec(own.shape, lambda s, i, chip_ref: (0, 0, 0)),
                        ANY] + [ANY] * nci,
              out_specs=(pl.BlockSpec((tm, cols), lambda s, i, chip_ref: (i, shard_col(s, chip_ref[0]))),
                         ANY) + tuple([ANY] * nco),
              scratch=[pltpu.VMEM(own.shape, own.dtype),
                       pltpu.SemaphoreType.DMA((3,)), pltpu.SemaphoreType.DMA((3,)),
                       pltpu.SemaphoreType.DMA((3,)), pltpu.SemaphoreType.DMA((3,)),
                       pltpu.SemaphoreType.DMA((2,)), pltpu.SemaphoreType.DMA((2,))] + comm.sems,
              sem=("arbitrary", "arbitrary"))(chip, x2b, own, own, *comm.ins)
    return res[0], res[1], res[2:]


def _rs_add_halves(name, grad, recv, core):
    _, r, cdim = grad.shape
    half = r // 2
    tr = _row_tile(half, cdim, mult=16)
    nr = half // tr

    def body(c_ref, g_ref, r_ref, o_ref):
        o_ref[...] = (g_ref[...].astype(F32) + r_ref[...].astype(F32)).astype(BF16)

    return _pc(body, name=name, out_shape=jax.ShapeDtypeStruct((N_CHIP, half, cdim), BF16),
               grid=(N_CHIP, nr), nsp=1,
               in_specs=[pl.BlockSpec((None, tr, cdim), lambda j, i, c_ref: (j, c_ref[0] * nr + i, 0)),
                         pl.BlockSpec((None, tr, cdim), lambda j, i, c_ref: (j, i, 0))],
               out_specs=pl.BlockSpec((None, tr, cdim), lambda j, i, c_ref: (j, i, 0)),
               sem=("parallel", "parallel"))(core, grad, recv)


def _chip_exchange_comm(parts):
    n = len(parts)

    def copies(ins, outs, sems):
        send_sems, recv_sems = sems
        x, y, c, peers = _chip_peers()
        return [pltpu.make_async_remote_copy(
            src_ref=ins[ti].at[2 * px + py], dst_ref=outs[ti].at[k],
            send_sem=send_sems.at[ti * 3 + k], recv_sem=recv_sems.at[ti * 3 + k],
            device_id=(px, py, c), device_id_type=MESH)
            for ti in range(n) for k, (px, py) in enumerate(peers)]

    def start(ins, outs, sems):
        for cp in copies(ins, outs, sems):
            cp.start()

    def finish(ins, outs, sems):
        for cp in copies(ins, outs, sems):
            cp.wait()

    return _Comm(parts, [jax.ShapeDtypeStruct((3,) + p.shape[1:], p.dtype) for p in parts],
                 [pltpu.SemaphoreType.DMA((3 * n,)), pltpu.SemaphoreType.DMA((3 * n,))], start, finish)


def _rs_sum_chips(name, part, recv, chip):
    _, half, cdim = recv.shape
    tr = _row_tile(half, cdim, mult=16)

    def body(chip_ref, p_ref, r_ref, o_ref):
        o_ref[...] = ((p_ref[...].astype(F32) + r_ref[0].astype(F32)) + r_ref[1].astype(F32)
                      ) + r_ref[2].astype(F32)

    return _pc(body, name=name, out_shape=jax.ShapeDtypeStruct((half, cdim), F32),
               grid=(half // tr,), nsp=1,
               in_specs=[pl.BlockSpec((None, tr, cdim), lambda i, chip_ref: (chip_ref[0], i, 0)),
                         pl.BlockSpec((3, tr, cdim), lambda i, chip_ref: (0, i, 0))],
               out_specs=pl.BlockSpec((tr, cdim), lambda i, chip_ref: (i, 0)),
               sem=("parallel",))(chip, part, recv)


def _rs_send_halves(halves):
    n = len(halves)

    def body(*refs):
        ins, outs = refs[:n], refs[n:2 * n]
        send_sems, recv_sems = refs[2 * n:]
        x, y, c = lax.axis_index("x"), lax.axis_index("y"), lax.axis_index("c")
        sends = []
        for ti in range(n):
            cp = pltpu.make_async_remote_copy(
                src_ref=ins[ti], dst_ref=outs[ti],
                send_sem=send_sems.at[ti], recv_sem=recv_sems.at[ti],
                device_id=(x, y, 1 - c), device_id_type=MESH)
            cp.start()
            sends.append(cp)
        for cp in sends:
            cp.wait()

    return _pc(body, name="rs_send_halves",
               out_shape=tuple(jax.ShapeDtypeStruct(hv.shape, hv.dtype) for hv in halves),
               in_specs=[ANY] * n, out_specs=tuple([ANY] * n),
               scratch=[pltpu.SemaphoreType.DMA((n,)), pltpu.SemaphoreType.DMA((n,))])(*halves)


def _adamw_rows(name, mine, theirs, w, m, v, core):
    half, cdim = mine.shape
    tr = _row_tile(half, cdim, budget=1 << 19)
    nrh = half // tr

    def body(c_ref, mine_ref, theirs_ref, w_ref, m_ref, v_ref, g_ref, d_ref, m2_ref, v2_ref):
        is_mine = (pl.program_id(0) // nrh) == c_ref[0]
        g = jnp.where(is_mine, mine_ref[...], theirs_ref[...])
        d, m2, v2 = _adamw(w_ref[...], g, m_ref[...], v_ref[...])
        g_ref[...] = g
        d_ref[...] = d
        m2_ref[...] = m2
        v2_ref[...] = v2

    htile = pl.BlockSpec((tr, cdim), lambda i, c_ref: (i % nrh, 0))
    tile = pl.BlockSpec((tr, cdim), lambda i, c_ref: (i, 0))
    shp = jax.ShapeDtypeStruct((2 * half, cdim), F32)
    return _pc(body, name=name, out_shape=(shp, shp, shp, shp), grid=(2 * nrh,), nsp=1,
               in_specs=[htile, htile, tile, tile, tile], out_specs=(tile, tile, tile, tile),
               sem=("parallel",))(core, mine, theirs, w, m, v)


def _adamw_whole(name, g, w, m, v):
    def body(g_ref, w_ref, m_ref, v_ref, d_ref, m2_ref, v2_ref):
        d, m2, v2 = _adamw(w_ref[...], g_ref[...], m_ref[...], v_ref[...])
        d_ref[...] = d
        m2_ref[...] = m2
        v2_ref[...] = v2

    shp = jax.ShapeDtypeStruct(g.shape, F32)
    return _pc(body, name=name, out_shape=(shp, shp, shp))(g, w, m, v)


SMALL_LAYOUT = (
    ("sgu_w_s", 1024, 1, 0),
    ("sgu_b_s", 8, 1, 1024),
    ("sgu_norm_g", 1, 0, 0),
    ("sgu_norm_b", 1, 0, 1),
    ("hgrn_norm_g", 1, 0, 3),
    ("ln1_g", 1, 0, 4),
    ("ln1_b", 1, 0, 5),
    ("ffn_conv_b", 1, 2, 3),
    ("ln2_g", 1, 0, 6),
    ("ln2_b", 1, 0, 7),
)
LB_ROW = 2
LOSS_ROW = 8
PACK_SHAPES = ((16, D_MODEL), (N_GROUP * 128 + 8, 128), (8, D_FF))


def _small_allreduce_adamw(rows1024, dws, dbs, dcw, dcb, logits, m_logits, v_logits,
                           small_w, small_m, small_v):
    ns = len(SMALL_LAYOUT)
    nr = len(rows1024)
    nb = len(PACK_SHAPES)

    def body(*refs):
        row_refs = refs[:nr]
        dws_ref, dbs_ref, dcw_ref, dcb_ref, lg_ref, mlg_ref, vlg_ref = refs[nr:nr + 7]
        pos = nr + 7
        w_refs = refs[pos:pos + ns]
        m_refs = refs[pos + ns:pos + 2 * ns]
        v_refs = refs[pos + 2 * ns:pos + 3 * ns]
        pos += 3 * ns
        loss_ref, dcw_out = refs[pos:pos + 2]
        lg_outs = refs[pos + 2:pos + 6]
        pos += 6
        outs = refs[pos:pos + 4 * ns]
        pos += 4 * ns
        pack = refs[pos:pos + nb]
        sib = refs[pos + nb:pos + 2 * nb]
        gath = refs[pos + 2 * nb:pos + 3 * nb]
        d2d_send, d2d_recv, ici_send, ici_recv = refs[pos + 3 * nb:]

        x, y, c, peers = _chip_peers()
        me = 2 * x + y
        sibling = (x, y, 1 - c)

        pack[0][...] = jnp.zeros(PACK_SHAPES[0], F32)
        for k in range(nr):
            pack[0][k:k + 1, :] = row_refs[k][0:1, :]
        pack[1][0:N_GROUP * 128, :] = dws_ref[...]
        pack[1][N_GROUP * 128:, :] = dbs_ref[...]
        pack[2][...] = jnp.zeros(PACK_SHAPES[2], F32)
        pack[2][0:3, :] = dcw_ref[0:3, :]
        pack[2][3:4, :] = dcb_ref[0:1, :]

        d2d = [pltpu.make_async_remote_copy(
            src_ref=pack[b], dst_ref=sib[b], send_sem=d2d_send.at[b], recv_sem=d2d_recv.at[b],
            device_id=sibling, device_id_type=MESH) for b in range(nb)]
        for cp in d2d:
            cp.start()
        for cp in d2d:
            cp.wait()
        for b in range(nb):
            gath[b][me] = pack[b][...] + sib[b][...]

        ici, ici_wait = [], []
        for b in range(nb):
            for k, (px, py) in enumerate(peers):
                sem = dict(send_sem=ici_send.at[b * 3 + k], recv_sem=ici_recv.at[b * 3 + k],
                           device_id=(px, py, c), device_id_type=MESH)
                ici.append(pltpu.make_async_remote_copy(src_ref=gath[b].at[me], dst_ref=gath[b].at[me], **sem))
                ici_wait.append(pltpu.make_async_remote_copy(
                    src_ref=gath[b].at[me], dst_ref=gath[b].at[2 * px + py], **sem))
        for cp in ici:
            cp.start()
        for cp in ici_wait:
            cp.wait_recv()
        for cp in ici:
            cp.wait_send()

        tot = pack
        for b in range(nb):
            tot[b][...] = ((gath[b][0] + gath[b][1]) + gath[b][2]) + gath[b][3]

        loss_ref[...] = tot[0][LOSS_ROW:LOSS_ROW + 1, :]
        dcw_out[...] = tot[2][...]
        lb = _sig(lg_ref[0:1, :] - lg_ref[1:2, :])
        d0 = tot[0][LB_ROW:LB_ROW + 1, :] * lb * (1.0 - lb)
        rowid = lax.broadcasted_iota(jnp.int32, (2, D_MODEL), 0)
        g_lg = jnp.where(rowid == 0, d0, -d0)
        dl, ml, vl = _adamw(lg_ref[...], g_lg, mlg_ref[...], vlg_ref[...])
        lg_outs[0][...] = g_lg
        lg_outs[1][...] = dl
        lg_outs[2][...] = ml
        lg_outs[3][...] = vl
        for si, (_, rows, b, r0) in enumerate(SMALL_LAYOUT):
            g = tot[b][r0:r0 + rows, :]
            dl, ml, vl = _adamw(w_refs[si][...], g, m_refs[si][...], v_refs[si][...])
            outs[4 * si][...] = g
            outs[4 * si + 1][...] = dl
            outs[4 * si + 2][...] = ml
            outs[4 * si + 3][...] = vl

    shapes = [jax.ShapeDtypeStruct((1, D_MODEL), F32), jax.ShapeDtypeStruct((8, D_FF), F32)]
    shapes += [jax.ShapeDtypeStruct((2, D_MODEL), F32)] * 4
    for w in small_w:
        shapes += [jax.ShapeDtypeStruct(w.shape, F32)] * 4
    scratch = [pltpu.VMEM(shp, F32) for shp in PACK_SHAPES]
    scratch += [pltpu.VMEM(shp, F32) for shp in PACK_SHAPES]
    scratch += [pltpu.VMEM((N_CHIP,) + shp, F32) for shp in PACK_SHAPES]
    scratch += [pltpu.SemaphoreType.DMA((nb,)), pltpu.SemaphoreType.DMA((nb,)),
                pltpu.SemaphoreType.DMA((3 * nb,)), pltpu.SemaphoreType.DMA((3 * nb,))]
    vm = pl.BlockSpec(memory_space=pltpu.VMEM)
    n_in = nr + 7 + 3 * ns
    res = _pc(body, name="small_allreduce_adamw", out_shape=tuple(shapes),
              in_specs=[vm] * n_in, out_specs=tuple([vm] * len(shapes)),
              scratch=scratch)(*rows1024, dws, dbs, dcw, dcb, logits, m_logits, v_logits,
                               *small_w, *small_m, *small_v)
    return res[0], res[1], res[2:6], res[6:]


def kernel(x, p, w_in, sgu_w_s, sgu_b_s, sgu_norm_g, sgu_norm_b, hgrn_lb_logits, hgrn_norm_g, w_branch, w_out, ln1_g, ln1_b, ffn_w_up, ffn_conv_w, ffn_conv_b, ffn_w_down, ln2_g, ln2_b, ple_w_proj, ple_w_gate, loss_target, m_w_in, m_sgu_w_s, m_sgu_b_s, m_sgu_norm_g, m_sgu_norm_b, m_hgrn_lb_logits, m_hgrn_norm_g, m_w_branch, m_w_out, m_ln1_g, m_ln1_b, m_ffn_w_up, m_ffn_conv_w, m_ffn_conv_b, m_ffn_w_down, m_ln2_g, m_ln2_b, m_ple_w_proj, m_ple_w_gate, v_w_in, v_sgu_w_s, v_sgu_b_s, v_sgu_norm_g, v_sgu_norm_b, v_hgrn_lb_logits, v_hgrn_norm_g, v_w_branch, v_w_out, v_ln1_g, v_ln1_b, v_ffn_w_up, v_ffn_conv_w, v_ffn_conv_b, v_ffn_w_down, v_ln2_g, v_ln2_b, v_ple_w_proj, v_ple_w_gate):
    t = x.shape[1]
    x2 = x.reshape(t, D_MODEL)
    x2b = x2.astype(BF16)
    p2 = p.reshape(t, PLE_DIM)
    tgt = loss_target.reshape(t, D_MODEL)
    core = lax.axis_index("c").astype(jnp.int32).reshape(1)
    chip_id = (2 * lax.axis_index("x") + lax.axis_index("y")).astype(jnp.int32).reshape(1)

    big_w = [w_in[0], w_branch[0, 0], w_branch[0, 1], w_out[0], ffn_w_up[0], ffn_w_down[0],
             ple_w_proj[0], ple_w_gate[0]]
    big_m = [m_w_in[0], m_w_branch[0, 0], m_w_branch[0, 1], m_w_out[0], m_ffn_w_up[0],
             m_ffn_w_down[0], m_ple_w_proj[0], m_ple_w_gate[0]]
    big_v = [v_w_in[0], v_w_branch[0, 0], v_w_branch[0, 1], v_w_out[0], v_ffn_w_up[0],
             v_ffn_w_down[0], v_ple_w_proj[0], v_ple_w_gate[0]]
    def halves_of(i):
        w = big_w[i]
        return w.astype(BF16).reshape(2, w.shape[0] // 2, w.shape[1])

    def stacked(g, i):
        return g.reshape(N_CHIP, big_w[i].shape[0], big_w[i].shape[1])


    cid = jnp.arange(SGU_BLOCK) // CHUNK
    maskf = (cid[:, None] >= cid[None, :]).astype(F32)
    ws_masked = sgu_w_s[0] * maskf[None]
    wm = ws_masked.astype(BF16)
    wmt = jnp.transpose(ws_masked, (0, 2, 1)).astype(BF16)
    bsb = jnp.broadcast_to(sgu_b_s[0][:, :, None], (N_GROUP, SGU_BLOCK, 128))

    h, win_g, (wup_g,) = _in_proj_gathering(x2b, halves_of(0), chip_id, 512, _gather_comm([halves_of(4)]))
    win_st = stacked(win_g, 0)
    wup_st = stacked(wup_g, 4)
    ya, _ = _sgu_fwd(h, wm, bsb, sgu_norm_g, sgu_norm_b)
    (yb, st_all), mix_g = _hgrn_fwd(h, hgrn_lb_logits, hgrn_norm_g,
                                     comm=_gather_comm([halves_of(i) for i in (1, 2, 3)]))
    wb0, wb1, wo = [stacked(g, i).reshape(D_MODEL, D_MODEL) for g, i in zip(mix_g, (1, 2, 3))]
    (r1, a_br, b_br, m_bf, x1b), _ = _mix_fwd(ya, yb, h, x2, wb0, wb1, wo, ln1_g, ln1_b, 256)
    h2, out_g = _mm_nn_stacked("ffn_up", x1b, wup_st, 512,
                               comm=_gather_comm([halves_of(i) for i in (5, 6, 7)], [ffn_conv_w[0]]))
    wd = stacked(out_g[0], 5).reshape(D_FF, D_MODEL)
    wpp = jnp.transpose(stacked(out_g[1], 6), (1, 0, 2)).reshape(PLE_DIM, D_MODEL)
    wpg = stacked(out_g[2], 7).reshape(D_MODEL, D_MODEL)
    convw = jnp.transpose(out_g[3], (1, 0, 2)).reshape(3, D_FF)
    act = _ffn_act_fwd(h2, convw, ffn_conv_b, 256)
    dr2, dpg, dpp, loss_acc, dg2, db2 = _out_fwd_bwd(
        act, x1b, r1, p2, tgt, wd, wpg, wpp, ln1_g, ln1_b, ln2_g, ln2_b, 256)

    dact = _mm("ffn_down_bwd", dr2, wd, NT, (t // 512, FF_NJ, 1),
               pl.BlockSpec((512, D_MODEL), lambda i, j, k: (i, 0)),
               pl.BlockSpec((FF_TILE, D_MODEL), lambda i, j, k: (j, 0)),
               jax.ShapeDtypeStruct((t, D_FF), BF16),
               pl.BlockSpec((512, FF_TILE), lambda i, j, k: (i, j)))
    dh2, dcw, dcb = _ffn_act_bwd(h2, dact, convw, ffn_conv_b, 256)
    d_wd = _mm_tn("ffn_down_wgrad", act, dr2, FF_TILE, 512)
    d_wpg = _mm_tn("ple_gate_wgrad", x1b, dpg, 512, D_MODEL)
    d_wpp_st = _mm_tn("ple_proj_wgrad", p2, dpp, PLE_DIM, PLE_DIM, stacked=True)
    d_wup_st = _mm("ffn_up_wgrad", x1b, dh2, TN, (2, N_CHIP, 1),
                   pl.BlockSpec((t, 512), lambda i, j, k: (0, i)),
                   pl.BlockSpec((None, t, FF_TILE), lambda i, j, k: (j // FF_NJ, 0, j % FF_NJ)),
                   jax.ShapeDtypeStruct((N_CHIP, D_MODEL, FF_TILE), BF16),
                   pl.BlockSpec((None, 512, FF_TILE), lambda i, j, k: (j, i, 0)))
    dr1, dg1, db1 = _ffn_in_bwd(dh2, wup_st, dpg, wpg, dr2, r1, ln1_g, 512)
    da_bf, db_bf, dh3, dya, dyb = _mix_bwd(dr1, h, a_br, b_br, wo, wb0, wb1, 256)
    d_wo = _mm_tn("out_proj_wgrad", m_bf, dr1, 512, 512)
    d_wb0 = _mm_tn("branch0_wgrad", ya, da_bf, 512, D_MODEL)
    d_wb1 = _mm_tn("branch1_wgrad", yb, db_bf, 512, D_MODEL)
    grads_1 = [d_wb0.reshape(4, 256, D_MODEL), d_wb1.reshape(4, 256, D_MODEL),
               d_wo.reshape(4, 256, D_MODEL), d_wup_st, d_wd.reshape(4, D_FF // 4, D_MODEL),
               d_wpp_st, d_wpg.reshape(4, 256, D_MODEL)]
    (dh0, dws, dbs, dgv, dbv), recv_a1 = _sgu_bwd(h, dya, wm, wmt, bsb, sgu_norm_g, sgu_norm_b, maskf,
                                                  comm=_sibling_exchange_comm(grads_1))
    parts_1 = [_rs_add_halves("rs_add_halves%d" % (i + 1), g, r, core)
               for i, (g, r) in enumerate(zip(grads_1, recv_a1))]
    (dh1, dh2h, dlb, dgn), recv_b1 = _hgrn_bwd(h, dyb, st_all, hgrn_lb_logits, hgrn_norm_g,
                                                comm=_chip_exchange_comm(parts_1))
    dh_parts = [dh0, dh1, dh2h, dh3]
    d_win = [_mm_tn("in_proj_wgrad%d" % j, x2b, dh_parts[j], 512, D_MODEL) for j in range(4)]

    grads_0 = [jnp.stack(d_win)]
    recv_a0 = _run_comm("rs_sibling_exchange0", _sibling_exchange_comm(grads_0))
    parts_0 = [_rs_add_halves("rs_add_halves0", grads_0[0], recv_a0[0], core)]
    gx, recv_b0 = _in_proj_xgrad(dh_parts, win_st, dr1, 512, comm=_chip_exchange_comm(parts_0))
    parts = parts_0 + parts_1
    recv_b = list(recv_b0) + list(recv_b1)
    halves = [_rs_sum_chips("rs_sum_chips%d" % i, pt, r, chip_id)
              for i, (pt, r) in enumerate(zip(parts, recv_b))]
    theirs = _rs_send_halves(halves)
    big_out = [_adamw_rows("adamw_big%d" % i, halves[i], theirs[i], big_w[i], big_m[i], big_v[i], core)
               for i in range(len(halves))]

    small_in = dict(sgu_w_s=(sgu_w_s, m_sgu_w_s, v_sgu_w_s), sgu_b_s=(sgu_b_s, m_sgu_b_s, v_sgu_b_s),
                    sgu_norm_g=(sgu_norm_g, m_sgu_norm_g, v_sgu_norm_g),
                    sgu_norm_b=(sgu_norm_b, m_sgu_norm_b, v_sgu_norm_b),
                    hgrn_norm_g=(hgrn_norm_g, m_hgrn_norm_g, v_hgrn_norm_g),
                    ln1_g=(ln1_g, m_ln1_g, v_ln1_g), ln1_b=(ln1_b, m_ln1_b, v_ln1_b),
                    ffn_conv_b=(ffn_conv_b, m_ffn_conv_b, v_ffn_conv_b),
                    ln2_g=(ln2_g, m_ln2_g, v_ln2_g), ln2_b=(ln2_b, m_ln2_b, v_ln2_b))

    def flat(name, arr):
        rows = dict((n, r) for n, r, _, _ in SMALL_LAYOUT)[name]
        return arr.reshape(rows, arr.size // rows)

    names = [n for n, _, _, _ in SMALL_LAYOUT]
    sw = [flat(n, small_in[n][0]) for n in names]
    sm = [flat(n, small_in[n][1]) for n in names]
    sv = [flat(n, small_in[n][2]) for n in names]
    loss_rows, dcw_tot, lg_out, small_out = _small_allreduce_adamw(
        [dgv, dbv, dlb, dgn, dg1, db1, dg2, db2, loss_acc], dws.reshape(N_GROUP * 128, 128), dbs, dcw, dcb,
        hgrn_lb_logits, m_hgrn_lb_logits, v_hgrn_lb_logits, sw, sm, sv)
    loss = loss_rows[0, 0]

    chip = 2 * lax.axis_index("x") + lax.axis_index("y")
    g_cw = lax.dynamic_slice(dcw_tot, (0, chip * (D_FF // 4)), (3, D_FF // 4))
    cw_out = _adamw_whole("adamw_conv_w", g_cw, ffn_conv_w[0], m_ffn_conv_w[0], v_ffn_conv_w[0])

    res = {}
    for si, n in enumerate(names):
        shp = small_in[n][0].shape
        res[n] = tuple(small_out[4 * si + k].reshape(shp) for k in range(4))
    res["hgrn_lb_logits"] = tuple(lg_out)
    res["ffn_conv_w"] = (g_cw[None],) + tuple(o[None] for o in cw_out)

    def big(i):
        return tuple(big_out[i])

    res["w_in"] = tuple(o[None] for o in big(0))
    res["w_branch"] = tuple(jnp.stack([o0, o1])[None] for o0, o1 in zip(big(1), big(2)))
    res["w_out"] = tuple(o[None] for o in big(3))
    res["ffn_w_up"] = tuple(o[None] for o in big(4))
    res["ffn_w_down"] = tuple(o[None] for o in big(5))
    res["ple_w_proj"] = tuple(o[None] for o in big(6))
    res["ple_w_gate"] = tuple(o[None] for o in big(7))

    order = ["w_in", "sgu_w_s", "sgu_b_s", "sgu_norm_g", "sgu_norm_b", "hgrn_lb_logits",
             "hgrn_norm_g", "w_branch", "w_out", "ln1_g", "ln1_b", "ffn_w_up", "ffn_conv_w",
             "ffn_conv_b", "ffn_w_down", "ln2_g", "ln2_b", "ple_w_proj", "ple_w_gate"]
    outs = [loss, gx.reshape(1, t, D_MODEL)]
    for k in range(4):
        outs += [res[n][k] for n in order]
    return tuple(outs)
```

```python
import functools

import jax
import jax.numpy as jnp
from jax import lax
from jax.experimental import pallas as pl
from jax.experimental.pallas import tpu as pltpu

F32 = jnp.float32
BF16 = jnp.bfloat16
HIGHEST = lax.Precision.HIGHEST
MESH = pl.DeviceIdType.MESH

D_MODEL = 1024
CHUNK = 64
SGU_BLOCK = 128
N_GROUP = 8
N_HEAD = 8
HEAD_DIM = 128
D_FF = 2816
PLE_DIM = 256
IN_COLS = 8192
LN_EPS = 1e-5
RMS_EPS = 1e-6
ALPHA = 2.0 ** 0.25
N_CHIP = 4
N_DEV = 8

ADAM_LR = 0.001
ADAM_B1 = 0.9
ADAM_B2 = 0.999
ADAM_EPS = 1e-08
ADAM_WD = 0.01
ADAM_STEP = 10

VMEM_LIMIT = 56 * 1024 * 1024

NN = (((1,), (0,)), ((), ()))
NT = (((1,), (1,)), ((), ()))
TN = (((0,), (0,)), ((), ()))


def _pc(body, *, name, out_shape, grid=None, in_specs=None, out_specs=None, scratch=(),
        sem=None, nsp=0, vmem=VMEM_LIMIT):
    params = dict(vmem_limit_bytes=vmem)
    if sem is not None:
        params["dimension_semantics"] = sem
    kw = dict(name=name, out_shape=out_shape, compiler_params=pltpu.CompilerParams(**params))
    if nsp:
        kw["grid_spec"] = pltpu.PrefetchScalarGridSpec(
            num_scalar_prefetch=nsp, grid=grid, in_specs=in_specs, out_specs=out_specs,
            scratch_shapes=list(scratch))
    else:
        if grid is not None:
            kw["grid"] = grid
        if in_specs is not None:
            kw["in_specs"] = in_specs
            kw["out_specs"] = out_specs
        kw["scratch_shapes"] = list(scratch)
    return pl.pallas_call(body, **kw)


def _dot(a, b, dims=NN):
    return lax.dot_general(a.astype(BF16), b.astype(BF16), dims, preferred_element_type=F32)


def _dot32(a, b, dims=NN):
    return lax.dot_general(a, b, dims, precision=HIGHEST, preferred_element_type=F32)


def _sig(x):
    return 1.0 / (1.0 + jnp.exp(-x))


_GC = 0.7978845608028654
_GA = 0.044715


def _gelu(x):
    return 0.5 * x * (1.0 + jnp.tanh(_GC * (x + _GA * x * x * x)))


def _gelu_and_grad(x):
    t = jnp.tanh(_GC * (x + _GA * x * x * x))
    g = 0.5 * x * (1.0 + t)
    dg = 0.5 * (1.0 + t) + 0.5 * x * (1.0 - t * t) * _GC * (1.0 + 3.0 * _GA * x * x)
    return g, dg


def _ln_stats(r):
    mu = jnp.mean(r, axis=-1, keepdims=True)
    xc = r - mu
    var = jnp.mean(xc * xc, axis=-1, keepdims=True)
    rstd = lax.rsqrt(var + LN_EPS)
    return xc * rstd, rstd


def _ln_bwd(dxh, xh, rstd):
    m1 = jnp.mean(dxh, axis=-1, keepdims=True)
    m2 = jnp.mean(dxh * xh, axis=-1, keepdims=True)
    return rstd * (dxh - m1 - xh * m2)


def _colsum8(v):
    return jnp.broadcast_to(jnp.sum(v, axis=0, keepdims=True), (8, v.shape[1]))


def _adamw(w, g, m, v):
    m2 = ADAM_B1 * m + (1.0 - ADAM_B1) * g
    v2 = ADAM_B2 * v + (1.0 - ADAM_B2) * (g * g)
    m_hat = m2 / (1.0 - ADAM_B1 ** ADAM_STEP)
    v_hat = v2 / (1.0 - ADAM_B2 ** ADAM_STEP)
    delta = -ADAM_LR * (m_hat / (jnp.sqrt(v_hat) + ADAM_EPS) + ADAM_WD * w)
    return delta, m2, v2


def _row_tile(rows, cols, itemsize=4, budget=1 << 20, mult=8):
    best = mult
    for tr in range(mult, rows + 1, mult):
        if rows % tr == 0 and tr * cols * itemsize <= budget:
            best = tr
    return best


def _mm(name, a, b, dims, grid, a_spec, b_spec, out_shape, o_spec, add=None, add_spec=None,
        add_scale=1.0, comm=None):
    nk = grid[2]
    has_add = add is not None
    out_dtype = out_shape.dtype

    def body(*refs):
        if has_add:
            a_ref, b_ref, add_ref, o_ref = refs[:4]
            rest = refs[4:]
        else:
            a_ref, b_ref, o_ref = refs[:3]
            add_ref = None
            rest = refs[3:]
        prod = _dot(a_ref[...], b_ref[...], dims)

        def finish(acc):
            if has_add:
                acc = acc + add_scale * add_ref[...]
            o_ref[...] = acc.astype(out_dtype)

        if nk == 1:
            finish(prod)
        else:
            acc_ref = rest[0]
            k = pl.program_id(2)

            @pl.when(k == 0)
            def _():
                acc_ref[...] = prod

            @pl.when(k > 0)
            def _():
                acc_ref[...] += prod

            @pl.when(k == nk - 1)
            def _():
                finish(acc_ref[...])

    in_specs = [a_spec, b_spec] + ([add_spec] if has_add else [])
    args = [a, b] + ([add] if has_add else [])
    scratch = []
    if nk > 1:
        blk = [d for d in o_spec.block_shape if d is not None]
        scratch = [pltpu.VMEM(tuple(blk), F32)]
    if comm is None:
        return _pc(body, name=name, out_shape=out_shape, grid=grid, in_specs=in_specs,
                   out_specs=o_spec, scratch=scratch,
                   sem=("parallel", "parallel", "arbitrary"))(*args)

    def first():
        return (pl.program_id(0) == 0) & (pl.program_id(1) == 0) & (pl.program_id(2) == 0)

    def last():
        return ((pl.program_id(0) == grid[0] - 1) & (pl.program_id(1) == grid[1] - 1)
                & (pl.program_id(2) == grid[2] - 1))

    res = _hosted_call(body, comm, first, last, name=name, out_shape=(out_shape,), grid=grid,
                       in_specs=in_specs, out_specs=(o_spec,), scratch=scratch,
                       sem=("arbitrary", "arbitrary", "arbitrary"), args=args)
    return res[0], res[1:]


class _Comm:
    def __init__(self, ins, out_shapes, sems, start, finish):
        self.ins, self.out_shapes, self.sems = list(ins), list(out_shapes), list(sems)
        self.start, self.finish = start, finish


def _hosted_call(body, comm, first, last, *, name, out_shape, grid, in_specs, out_specs, scratch, sem,
                 args):
    n_in, n_out, n_scr = len(in_specs), len(out_shape), len(scratch)
    nci, nco = len(comm.ins), len(comm.out_shapes)

    def wrapped(*refs):
        pos = n_in
        own_in, c_in = refs[:pos], refs[pos:pos + nci]
        pos += nci
        own_out, c_out = refs[pos:pos + n_out], refs[pos + n_out:pos + n_out + nco]
        pos += n_out + nco
        own_scr, c_sem = refs[pos:pos + n_scr], refs[pos + n_scr:]

        @pl.when(first())
        def _():
            comm.start(c_in, c_out, c_sem)

        body(*own_in, *own_out, *own_scr)

        @pl.when(last())
        def _():
            comm.finish(c_in, c_out, c_sem)

    return _pc(wrapped, name=name, out_shape=tuple(out_shape) + tuple(comm.out_shapes), grid=grid,
               in_specs=list(in_specs) + [ANY] * nci, out_specs=tuple(out_specs) + tuple([ANY] * nco),
               scratch=list(scratch) + comm.sems, sem=sem)(*args, *comm.ins)


def _grid1_call(body, comm, n, *, name, out_shape, in_specs, out_specs, scratch, args):
    if comm is None:
        return _pc(body, name=name, out_shape=out_shape, grid=(n,), in_specs=in_specs,
                   out_specs=out_specs, scratch=scratch, sem=("arbitrary",))(*args), ()
    res = _hosted_call(body, comm, lambda: pl.program_id(0) == 0, lambda: pl.program_id(0) == n - 1,
                       name=name, out_shape=out_shape, grid=(n,), in_specs=in_specs,
                       out_specs=out_specs, scratch=scratch, sem=("arbitrary",), args=args)
    return res[:len(out_shape)], res[len(out_shape):]


def _run_comm(name, comm):
    nci, nco = len(comm.ins), len(comm.out_shapes)

    def body(*refs):
        c_in, c_out, c_sem = refs[:nci], refs[nci:nci + nco], refs[nci + nco:]
        comm.start(c_in, c_out, c_sem)
        comm.finish(c_in, c_out, c_sem)

    return _pc(body, name=name, out_shape=tuple(comm.out_shapes), in_specs=[ANY] * nci,
               out_specs=tuple([ANY] * nco), scratch=comm.sems)(*comm.ins)


def _mm_nn_stacked(name, a, w_st, tm, comm=None):
    t, k = a.shape
    _, _, c = w_st.shape
    return _mm(name, a, w_st, NN, (t // tm, N_CHIP, 1),
               pl.BlockSpec((tm, k), lambda i, j, kk: (i, 0)),
               pl.BlockSpec((None, k, c), lambda i, j, kk: (j, 0, 0)),
               jax.ShapeDtypeStruct((t, N_CHIP * c), BF16),
               pl.BlockSpec((tm, c), lambda i, j, kk: (i, j)), comm=comm)


def _mm_tn(name, a, b, tm, tn, stacked=False):
    t, m = a.shape
    _, n = b.shape
    if stacked:
        assert tm == m
        out_shape = jax.ShapeDtypeStruct((n // tn, m, tn), BF16)
        o_spec = pl.BlockSpec((None, tm, tn), lambda i, j, kk: (j, 0, 0))
    else:
        out_shape = jax.ShapeDtypeStruct((m, n), BF16)
        o_spec = pl.BlockSpec((tm, tn), lambda i, j, kk: (i, j))
    return _mm(name, a, b, TN, (m // tm, n // tn, 1),
               pl.BlockSpec((t, tm), lambda i, j, kk: (0, i)),
               pl.BlockSpec((t, tn), lambda i, j, kk: (0, j)),
               out_shape, o_spec)


def _in_proj_xgrad(dh_parts, win_st, dr1, tm, comm=None):
    t = dr1.shape[0]
    ni = t // tm

    def body(a0, a1, a2, a3, b_ref, add_ref, o_ref, acc):
        j = pl.program_id(1)
        for jj, a_ref in enumerate((a0, a1, a2, a3)):
            @pl.when(j == jj)
            def _(jj=jj, a_ref=a_ref):
                prod = _dot(a_ref[...], b_ref[...], NT)
                if jj == 0:
                    acc[...] = prod + ALPHA * add_ref[...]
                elif jj < N_CHIP - 1:
                    acc[...] += prod
                else:
                    o_ref[...] = acc[...] + prod

    a_spec = pl.BlockSpec((tm, 2 * D_MODEL), lambda i, j: (i, 0))
    tile = pl.BlockSpec((tm, D_MODEL), lambda i, j: (i, 0))
    kw = dict(name="in_proj_xgrad", out_shape=(jax.ShapeDtypeStruct((t, D_MODEL), F32),),
              grid=(ni, N_CHIP),
              in_specs=[a_spec] * 4 + [pl.BlockSpec((None, D_MODEL, 2 * D_MODEL), lambda i, j: (j, 0, 0)),
                                       tile],
              out_specs=(tile,), scratch=[pltpu.VMEM((tm, D_MODEL), F32)],
              sem=("arbitrary", "arbitrary"))
    args = list(dh_parts) + [win_st, dr1]
    if comm is None:
        return _pc(body, **kw)(*args)[0], ()
    res = _hosted_call(body, comm,
                       lambda: (pl.program_id(0) == 0) & (pl.program_id(1) == 0),
                       lambda: (pl.program_id(0) == ni - 1) & (pl.program_id(1) == N_CHIP - 1),
                       args=args, **kw)
    return res[0], res[1:]


def _sgu_mixed(v, wm_ref, bsb_ref, gv, bv):
    gl, dgl = _gelu_and_grad(v)
    vh, rstd = _ln_stats(gl)
    vn = vh * gv + bv
    mixed = []
    for g in range(N_GROUP):
        sl = slice(g * 128, (g + 1) * 128)
        mixed.append(_dot(wm_ref[g], vn[:, sl]) + bsb_ref[g])
    return dgl, vh, rstd, vn, mixed


def _sgu_fwd(h, wm, bsb, gv, bv, comm=None):
    t = h.shape[0]

    def body(u_ref, v_ref, wm_ref, bsb_ref, gv_ref, bv_ref, ya_ref):
        u = u_ref[...].astype(F32)
        _, _, _, _, mixed = _sgu_mixed(v_ref[...].astype(F32), wm_ref, bsb_ref, gv_ref[...], bv_ref[...])
        gu = _gelu(u)
        for g in range(N_GROUP):
            sl = slice(g * 128, (g + 1) * 128)
            ya_ref[:, sl] = (gu[:, sl] * mixed[g]).astype(BF16)

    full3 = pl.BlockSpec((N_GROUP, 128, 128), lambda i: (0, 0, 0))
    vec = pl.BlockSpec((1, D_MODEL), lambda i: (0, 0))
    (ya,), extra = _grid1_call(
        body, comm, t // SGU_BLOCK, name="sgu_fwd",
        out_shape=(jax.ShapeDtypeStruct((t, D_MODEL), BF16),),
        in_specs=[pl.BlockSpec((SGU_BLOCK, D_MODEL), lambda i: (i, 0)),
                  pl.BlockSpec((SGU_BLOCK, D_MODEL), lambda i: (i, 1)),
                  full3, full3, vec, vec],
        out_specs=(pl.BlockSpec((SGU_BLOCK, D_MODEL), lambda i: (i, 0)),),
        scratch=[], args=(h, h, wm, bsb, gv, bv))
    return ya, extra


def _sgu_bwd(h, dya, wm, wmt, bsb, gv, bv, maskf, comm=None):
    t = h.shape[0]
    nb = t // SGU_BLOCK

    def body(u_ref, v_ref, dya_ref, wm_ref, wmt_ref, bsb_ref, gv_ref, bv_ref, mask_ref,
             dh_ref, dws_ref, dbs_ref, dgv_ref, dbv_ref, dmix_acc):
        i = pl.program_id(0)

        @pl.when(i == 0)
        def _():
            dws_ref[...] = jnp.zeros_like(dws_ref)
            dgv_ref[...] = jnp.zeros_like(dgv_ref)
            dbv_ref[...] = jnp.zeros_like(dbv_ref)
            dmix_acc[...] = jnp.zeros_like(dmix_acc)

        u = u_ref[...].astype(F32)
        gvv = gv_ref[...]
        dgl_v, vh, rstd, vn, mixed = _sgu_mixed(v_ref[...].astype(F32), wm_ref, bsb_ref, gvv, bv_ref[...])
        gu, dgl_u = _gelu_and_grad(u)
        dya_v = dya_ref[...]
        dvn_parts = []
        for g in range(N_GROUP):
            sl = slice(g * 128, (g + 1) * 128)
            d_y = dya_v[:, sl]
            dh_ref[:, sl] = (d_y * mixed[g] * dgl_u[:, sl]).astype(BF16)
            d_mixed = d_y * gu[:, sl]
            dmix_acc[g] += d_mixed
            dws_ref[g] += _dot(d_mixed, vn[:, sl], NT) * mask_ref[...]
            dvn_parts.append(_dot(wmt_ref[g], d_mixed))
        dvn = jnp.concatenate(dvn_parts, axis=1)
        dgv_ref[...] += _colsum8(dvn * vh)
        dbv_ref[...] += _colsum8(dvn)
        d_gl = _ln_bwd(dvn * gvv, vh, rstd)
        dh_ref[:, D_MODEL:] = (d_gl * dgl_v).astype(BF16)

        @pl.when(i == nb - 1)
        def _():
            rowid = lax.broadcasted_iota(jnp.int32, (8, 128), 0)
            ones = jnp.ones((8, 128), F32)
            acc = jnp.zeros((8, 128), F32)
            for g in range(N_GROUP):
                rs = _dot32(ones, dmix_acc[g], NT)
                acc = jnp.where(rowid == g, rs, acc)
            dbs_ref[...] = acc

    full3 = pl.BlockSpec((N_GROUP, 128, 128), lambda i: (0, 0, 0))
    vec = pl.BlockSpec((1, D_MODEL), lambda i: (0, 0))
    acc8 = pl.BlockSpec((8, D_MODEL), lambda i: (0, 0))
    return _grid1_call(
        body, comm, nb, name="sgu_bwd",
        out_shape=(jax.ShapeDtypeStruct((t, 2 * D_MODEL), BF16),
                   jax.ShapeDtypeStruct((N_GROUP, 128, 128), F32),
                   jax.ShapeDtypeStruct((8, 128), F32),
                   jax.ShapeDtypeStruct((8, D_MODEL), F32),
                   jax.ShapeDtypeStruct((8, D_MODEL), F32)),
        in_specs=[pl.BlockSpec((SGU_BLOCK, D_MODEL), lambda i: (i, 0)),
                  pl.BlockSpec((SGU_BLOCK, D_MODEL), lambda i: (i, 1)),
                  pl.BlockSpec((SGU_BLOCK, D_MODEL), lambda i: (i, 0)),
                  full3, full3, full3, vec, vec,
                  pl.BlockSpec((128, 128), lambda i: (0, 0))],
        out_specs=(pl.BlockSpec((SGU_BLOCK, 2 * D_MODEL), lambda i: (i, 0)),
                   full3, pl.BlockSpec((8, 128), lambda i: (0, 0)), acc8, acc8),
        scratch=[pltpu.VMEM((N_GROUP, 128, 128), F32)],
        args=(h, h, dya, wm, wmt, bsb, gv, bv, maskf))


def _tri_masks():
    row = lax.broadcasted_iota(jnp.int32, (CHUNK, CHUNK), 0)
    col = lax.broadcasted_iota(jnp.int32, (CHUNK, CHUNK), 1)
    return col <= row, col >= row


def _heads(v):
    return [v[:, hd * HEAD_DIM:(hd + 1) * HEAD_DIM] for hd in range(N_HEAD)]


def _tri_cumsum(tri_bf, v):
    hi = v.astype(BF16)
    r = v - hi.astype(F32)
    mid = r.astype(BF16)
    lo = (r - mid.astype(F32)).astype(BF16)
    return _dot(tri_bf, hi) + _dot(tri_bf, mid) + _dot(tri_bf, lo)


def _hgrn_chunk(q, fp, ii, lb, st_heads, causal):
    sg = _sig(fp)
    f = lb + (1.0 - lb) * sg
    k = 1.0 - f
    c = _tri_cumsum(causal.astype(BF16), jnp.log(f))
    ec = jnp.exp(c)
    en = jnp.exp(-c)
    sq = _sig(q)
    qt = q * sq * ec
    kt = k * en
    ecl = jnp.exp(c[CHUNK - 1:CHUNK, :])
    kk = kt * ecl
    qtb, ktb, iib, kkb = qt.astype(BF16), kt.astype(BF16), ii.astype(BF16), kk.astype(BF16)
    attn, o = [], []
    for hd, (qh, kh, ih) in enumerate(zip(_heads(qtb), _heads(ktb), _heads(iib))):
        a = jnp.where(causal, _dot(qh, kh, NT), 0.0).astype(BF16)
        attn.append(a)
        o.append(_dot(a, ih) + _dot(qh, st_heads[hd], NT))
    return dict(sg=sg, f=f, k=k, ec=ec, en=en, sq=sq, ecl=ecl, kk=kk, qtb=qtb, ktb=ktb, iib=iib,
                kkb=kkb, attn=attn, o=o)


def _rms_heads(o_heads):
    rinv = [lax.rsqrt(jnp.mean(o * o, axis=-1, keepdims=True) + RMS_EPS) for o in o_heads]
    return rinv, jnp.concatenate([o * r for o, r in zip(o_heads, rinv)], axis=1)


HG_CHUNKS = 4
HG_ROWS = HG_CHUNKS * CHUNK


def _hgrn_fwd(h, logits, gn, comm=None):
    t = h.shape[0]
    nb = t // HG_ROWS

    def body(q_ref, f_ref, i_ref, og_ref, lg_ref, gn_ref, yb_ref, st_ref, state):
        @pl.when(pl.program_id(0) == 0)
        def _():
            state[...] = jnp.zeros_like(state)

        causal, _ = _tri_masks()
        lb = _sig(lg_ref[0:1, :] - lg_ref[1:2, :])
        gnv = gn_ref[...]
        st = [state[hd] for hd in range(N_HEAD)]
        for cc in range(HG_CHUNKS):
            rows = slice(cc * CHUNK, (cc + 1) * CHUNK)
            og = og_ref[rows, :].astype(F32)
            r = _hgrn_chunk(q_ref[rows, :].astype(F32), f_ref[rows, :].astype(F32),
                            i_ref[rows, :].astype(F32), lb, [s.astype(BF16) for s in st], causal)
            _, on = _rms_heads(r["o"])
            yb_ref[rows, :] = (on * gnv * (og * _sig(og))).astype(BF16)
            for hd in range(N_HEAD):
                st_ref[cc, hd] = st[hd]
            st = [s * e + _dot(ih, kh, TN)
                  for s, e, ih, kh in zip(st, _heads(r["ecl"]), _heads(r["iib"]), _heads(r["kkb"]))]
        for hd in range(N_HEAD):
            state[hd] = st[hd]

    def col(k):
        return pl.BlockSpec((HG_ROWS, D_MODEL), lambda ci: (ci, k))

    return _grid1_call(body, comm, nb, name="hgrn_fwd",
                       out_shape=(jax.ShapeDtypeStruct((t, D_MODEL), BF16),
                                  jax.ShapeDtypeStruct((t // CHUNK, N_HEAD, HEAD_DIM, HEAD_DIM), F32)),
                       in_specs=[col(2), col(3), col(4), col(5),
                                 pl.BlockSpec((2, D_MODEL), lambda ci: (0, 0)),
                                 pl.BlockSpec((1, D_MODEL), lambda ci: (0, 0))],
                       out_specs=(pl.BlockSpec((HG_ROWS, D_MODEL), lambda ci: (ci, 0)),
                                  pl.BlockSpec((HG_CHUNKS, N_HEAD, HEAD_DIM, HEAD_DIM),
                                               lambda ci: (ci, 0, 0, 0))),
                       scratch=[pltpu.VMEM((N_HEAD, HEAD_DIM, HEAD_DIM), F32)],
                       args=(h, h, h, h, logits, gn))


def _hgrn_chunk_bwd(q, fp, ii, og, dy, gnv, lb, st, dsn, causal, anti):
    stb = [s.astype(BF16) for s in st]
    dsnb = [s.astype(BF16) for s in dsn]
    r = _hgrn_chunk(q, fp, ii, lb, stb, causal)
    rinv, on = _rms_heads(r["o"])
    so = _sig(og)
    sil = og * so
    d_og = dy * on * gnv * (so * (1.0 + og * (1.0 - so)))
    d_on = dy * gnv * sil
    d_ob = jnp.concatenate(
        [ri * (dn - oh * jnp.mean(dn * oh, axis=-1, keepdims=True))
         for ri, dn, oh in zip(rinv, _heads(d_on), _heads(on))], axis=1).astype(BF16)
    d_i, d_qt, d_kt, d_kk, d_st, st_dsn = [], [], [], [], [], []
    ecl = _heads(r["ecl"])
    for hd, (dh, qh, kh, ih, kkh) in enumerate(zip(_heads(d_ob), _heads(r["qtb"]), _heads(r["ktb"]),
                                                   _heads(r["iib"]), _heads(r["kkb"]))):
        d_attn = jnp.where(causal, _dot(dh, ih, NT), 0.0).astype(BF16)
        d_i.append(_dot(r["attn"][hd], dh, TN) + _dot(kkh, dsnb[hd], NT))
        d_qt.append(_dot(d_attn, kh) + _dot(dh, stb[hd]))
        d_kt.append(_dot(d_attn, qh, TN))
        d_kk.append(_dot(ih, dsnb[hd]))
        d_st.append(_dot(dh, qh, TN) + dsn[hd] * ecl[hd])
        st_dsn.append(jnp.sum(st[hd] * dsn[hd], axis=0, keepdims=True))
    d_qt = jnp.concatenate(d_qt, axis=1)
    d_kt = jnp.concatenate(d_kt, axis=1)
    d_kk = jnp.concatenate(d_kk, axis=1)
    kk = r["kk"]
    d_cl = r["ecl"] * jnp.concatenate(st_dsn, axis=1) + jnp.sum(kk * d_kk, axis=0, keepdims=True)
    d_k = (d_kk * r["ecl"] + d_kt) * r["en"]
    d_c = d_qt * r["qtb"].astype(F32) - d_kt * r["ktb"].astype(F32) - d_kk * kk
    rowid = lax.broadcasted_iota(jnp.int32, (CHUNK, D_MODEL), 0)
    d_c = d_c + jnp.where(rowid == CHUNK - 1, d_cl, 0.0)
    d_lf = _tri_cumsum(anti.astype(BF16), d_c)
    d_f = d_lf / r["f"] - d_k
    sg, sq = r["sg"], r["sq"]
    d_q = d_qt * r["ec"] * (sq * (1.0 + q * (1.0 - sq)))
    d_fp = d_f * (1.0 - lb) * sg * (1.0 - sg)
    return (d_q, d_fp, jnp.concatenate(d_i, axis=1), d_og, d_st,
            _colsum8(dy * on * sil), _colsum8(d_f * (1.0 - sg)))


def _hgrn_bwd(h, dyb, st_all, logits, gn, comm=None):
    t = h.shape[0]
    nb = t // HG_ROWS

    def body(q_ref, f_ref, i_ref, og_ref, dyb_ref, st_ref, lg_ref, gn_ref,
             dh1_ref, dh2_ref, dlb_ref, dgn_ref, dstate):
        @pl.when(pl.program_id(0) == 0)
        def _():
            dstate[...] = jnp.zeros_like(dstate)
            dlb_ref[...] = jnp.zeros_like(dlb_ref)
            dgn_ref[...] = jnp.zeros_like(dgn_ref)

        causal, anti = _tri_masks()
        lb = _sig(lg_ref[0:1, :] - lg_ref[1:2, :])
        gnv = gn_ref[...]
        dsn = [dstate[hd] for hd in range(N_HEAD)]
        dgn_acc = jnp.zeros((8, D_MODEL), F32)
        dlb_acc = jnp.zeros((8, D_MODEL), F32)
        for cc in reversed(range(HG_CHUNKS)):
            rows = slice(cc * CHUNK, (cc + 1) * CHUNK)
            d_q, d_fp, d_i, d_og, dsn, dgn_c, dlb_c = _hgrn_chunk_bwd(
                q_ref[rows, :].astype(F32), f_ref[rows, :].astype(F32), i_ref[rows, :].astype(F32),
                og_ref[rows, :].astype(F32), dyb_ref[rows, :], gnv, lb,
                [st_ref[cc, hd] for hd in range(N_HEAD)], dsn, causal, anti)
            dgn_acc = dgn_acc + dgn_c
            dlb_acc = dlb_acc + dlb_c
            dh1_ref[rows, :D_MODEL] = d_q.astype(BF16)
            dh1_ref[rows, D_MODEL:] = d_fp.astype(BF16)
            dh2_ref[rows, :D_MODEL] = d_i.astype(BF16)
            dh2_ref[rows, D_MODEL:] = d_og.astype(BF16)
        dgn_ref[...] += dgn_acc
        dlb_ref[...] += dlb_acc
        for hd in range(N_HEAD):
            dstate[hd] = dsn[hd]

    def col(k):
        return pl.BlockSpec((HG_ROWS, D_MODEL), lambda ci: (nb - 1 - ci, k))

    acc8 = pl.BlockSpec((8, D_MODEL), lambda ci: (0, 0))
    pair = pl.BlockSpec((HG_ROWS, 2 * D_MODEL), lambda ci: (nb - 1 - ci, 0))
    return _grid1_call(body, comm, nb, name="hgrn_bwd",
                       out_shape=(jax.ShapeDtypeStruct((t, 2 * D_MODEL), BF16),
                                  jax.ShapeDtypeStruct((t, 2 * D_MODEL), BF16),
                                  jax.ShapeDtypeStruct((8, D_MODEL), F32),
                                  jax.ShapeDtypeStruct((8, D_MODEL), F32)),
                       in_specs=[col(2), col(3), col(4), col(5),
                                 pl.BlockSpec((HG_ROWS, D_MODEL), lambda ci: (nb - 1 - ci, 0)),
                                 pl.BlockSpec((HG_CHUNKS, N_HEAD, HEAD_DIM, HEAD_DIM),
                                              lambda ci: (nb - 1 - ci, 0, 0, 0)),
                                 pl.BlockSpec((2, D_MODEL), lambda ci: (0, 0)),
                                 pl.BlockSpec((1, D_MODEL), lambda ci: (0, 0))],
                       out_specs=(pair, pair, acc8, acc8),
                       scratch=[pltpu.VMEM((N_HEAD, HEAD_DIM, HEAD_DIM), F32)],
                       args=(h, h, h, h, dyb, st_all, logits, gn))


def _mix_fwd(ya, yb, h, x, wb0, wb1, wo, g1, b1, tm, comm=None):
    t = x.shape[0]

    def body(ya_ref, yb_ref, ga_ref, gb_ref, x_ref, wb0_ref, wb1_ref, wo_ref, g1_ref, b1_ref,
             r1_ref, a_ref, b_ref, m_ref, x1_ref):
        a = _dot(ya_ref[...], wb0_ref[...])
        b = _dot(yb_ref[...], wb1_ref[...])
        m = _sig(ga_ref[...].astype(F32)) * a + _sig(gb_ref[...].astype(F32)) * b
        r1 = ALPHA * x_ref[...] + _dot(m, wo_ref[...])
        xh, _ = _ln_stats(r1)
        r1_ref[...] = r1
        a_ref[...] = a
        b_ref[...] = b
        m_ref[...] = m.astype(BF16)
        x1_ref[...] = (xh * g1_ref[...] + b1_ref[...]).astype(BF16)

    tile = pl.BlockSpec((tm, D_MODEL), lambda i: (i, 0))
    wsp = pl.BlockSpec((D_MODEL, D_MODEL), lambda i: (0, 0))
    vec = pl.BlockSpec((1, D_MODEL), lambda i: (0, 0))
    f32o = jax.ShapeDtypeStruct((t, D_MODEL), F32)
    bfo = jax.ShapeDtypeStruct((t, D_MODEL), BF16)
    return _grid1_call(body, comm, t // tm, name="mix_fwd", out_shape=(f32o, f32o, f32o, bfo, bfo),
                       in_specs=[tile, tile,
                                 pl.BlockSpec((tm, D_MODEL), lambda i: (i, 6)),
                                 pl.BlockSpec((tm, D_MODEL), lambda i: (i, 7)),
                                 tile, wsp, wsp, wsp, vec, vec],
                       out_specs=(tile, tile, tile, tile, tile),
                       scratch=[], args=(ya, yb, h, h, x, wb0, wb1, wo, g1, b1))


def _mix_bwd(dr1, h, a, b, wo, wb0, wb1, tm):
    t = dr1.shape[0]

    def body(dr1_ref, ga_ref, gb_ref, a_ref, b_ref, wo_ref, wb0_ref, wb1_ref,
             da_ref, db_ref, dh3_ref, dya_ref, dyb_ref):
        d_m = _dot(dr1_ref[...], wo_ref[...], NT)
        sa = _sig(ga_ref[...].astype(F32))
        sb = _sig(gb_ref[...].astype(F32))
        d_a = (d_m * sa).astype(BF16)
        d_b = (d_m * sb).astype(BF16)
        da_ref[...] = d_a
        db_ref[...] = d_b
        dh3_ref[:, :D_MODEL] = (d_m * a_ref[...] * sa * (1.0 - sa)).astype(BF16)
        dh3_ref[:, D_MODEL:] = (d_m * b_ref[...] * sb * (1.0 - sb)).astype(BF16)
        dya_ref[...] = _dot(d_a, wb0_ref[...], NT)
        dyb_ref[...] = _dot(d_b, wb1_ref[...], NT)

    tile = pl.BlockSpec((tm, D_MODEL), lambda i: (i, 0))
    wsp = pl.BlockSpec((D_MODEL, D_MODEL), lambda i: (0, 0))
    f32o = jax.ShapeDtypeStruct((t, D_MODEL), F32)
    bfo = jax.ShapeDtypeStruct((t, D_MODEL), BF16)
    return _pc(body, name="mix_bwd",
               out_shape=(bfo, bfo, jax.ShapeDtypeStruct((t, 2 * D_MODEL), BF16), f32o, f32o),
               grid=(t // tm,),
               in_specs=[tile,
                         pl.BlockSpec((tm, D_MODEL), lambda i: (i, 6)),
                         pl.BlockSpec((tm, D_MODEL), lambda i: (i, 7)),
                         tile, tile, wsp, wsp, wsp],
               out_specs=(tile, tile, pl.BlockSpec((tm, 2 * D_MODEL), lambda i: (i, 0)),
                          tile, tile),
               sem=("parallel",))(dr1, h, h, a, b, wo, wb0, wb1)


FF_TILE = 1408
FF_NJ = D_FF // FF_TILE


def _shift_down(v, k):
    return pltpu.roll(v, k, 0)


def _shift_up(v, k):
    return pltpu.roll(v, v.shape[0] - k, 0)


def _conv_gate(ext, cw_ref, cb_ref):
    return (cw_ref[0:1, :] * _shift_down(ext, 2) + cw_ref[1:2, :] * _shift_down(ext, 1)
            + cw_ref[2:3, :] * ext + cb_ref[...])


HALO = 16


def _ffn_up_act(x1b, wup_st, convw, convb, tm, comm):
    t = x1b.shape[0]
    ni = t // tm
    nth = tm // HALO

    def body(x_ref, xp_ref, wg_ref, wv_ref, cw_ref, cb_ref, h2_ref, act_ref):
        i = pl.program_id(0)
        wg = wg_ref[...]
        gate = _dot(x_ref[...], wg).astype(BF16)
        val = _dot(x_ref[...], wv_ref[...]).astype(BF16)
        prev = (_dot(xp_ref[...], wg) * (i > 0).astype(F32)).astype(BF16)
        h2_ref[0] = gate
        h2_ref[1] = val
        ext = jnp.concatenate([prev.astype(F32), gate.astype(F32)], axis=0)
        gc = _conv_gate(ext, cw_ref, cb_ref)[HALO:, :]
        act_ref[...] = (_gelu(gc) * val.astype(F32)).astype(BF16)

    res = _hosted_call(
        body, comm,
        lambda: (pl.program_id(0) == 0) & (pl.program_id(1) == 0),
        lambda: (pl.program_id(0) == ni - 1) & (pl.program_id(1) == FF_NJ - 1),
        name="ffn_up",
        out_shape=(jax.ShapeDtypeStruct((2, t, D_FF), BF16), jax.ShapeDtypeStruct((t, D_FF), BF16)),
        grid=(ni, FF_NJ),
        in_specs=[pl.BlockSpec((tm, D_MODEL), lambda i, j: (i, 0)),
                  pl.BlockSpec((HALO, D_MODEL), lambda i, j: (jnp.maximum(i * nth - 1, 0), 0)),
                  pl.BlockSpec((None, D_MODEL, FF_TILE), lambda i, j: (j, 0, 0)),
                  pl.BlockSpec((None, D_MODEL, FF_TILE), lambda i, j: (j + FF_NJ, 0, 0)),
                  pl.BlockSpec((3, FF_TILE), lambda i, j: (0, j)),
                  pl.BlockSpec((1, FF_TILE), lambda i, j: (0, j))],
        out_specs=(pl.BlockSpec((2, tm, FF_TILE), lambda i, j: (0, i, j)),
                   pl.BlockSpec((tm, FF_TILE), lambda i, j: (i, j))),
        scratch=[], sem=("arbitrary", "arbitrary"),
        args=(x1b, x1b, wup_st, wup_st, convw, convb))
    return res[0], res[1], res[2:]


def _ffn_act_bwd(h2, dr2, wd, convw, convb, tm):
    t = h2.shape[1]
    nth = tm // HALO
    ni = t // tm
    last_halo = t // HALO - 1
    main_rows = slice(HALO, HALO + tm)

    def body(g_ref, gp_ref, gn_ref, v_ref, vn_ref, dr2_ref, dr2n_ref, wd_ref, cw_ref, cb_ref,
             dh2_ref, dcw_ref, dcb_ref):
        i = pl.program_id(1)

        @pl.when(i == 0)
        def _():
            dcw_ref[...] = jnp.zeros_like(dcw_ref)
            dcb_ref[...] = jnp.zeros_like(dcb_ref)

        zeros = jnp.zeros((HALO, FF_TILE), F32)
        da = _dot(dr2_ref[...], wd_ref[...], NT)
        prev = gp_ref[...].astype(F32) * (i > 0).astype(F32)
        ext = jnp.concatenate([prev, g_ref[...].astype(F32), gn_ref[...].astype(F32)], axis=0)
        vext = jnp.concatenate([zeros, v_ref[...].astype(F32), vn_ref[...].astype(F32)], axis=0)
        dnext = _dot(dr2n_ref[...], wd_ref[...], NT) * (i < ni - 1).astype(F32)
        dext = jnp.concatenate([zeros, da, dnext], axis=0)
        g2 = _shift_down(ext, 2)
        g1 = _shift_down(ext, 1)
        gc = cw_ref[0:1, :] * g2 + cw_ref[1:2, :] * g1 + cw_ref[2:3, :] * ext + cb_ref[...]
        gl, dgl = _gelu_and_grad(gc)
        d_gc = dext * vext * dgl
        d_gate = (cw_ref[2:3, :] * d_gc + cw_ref[1:2, :] * _shift_up(d_gc, 1)
                  + cw_ref[0:1, :] * _shift_up(d_gc, 2))
        dh2_ref[0] = d_gate[main_rows, :].astype(BF16)
        dh2_ref[1] = (da * gl[main_rows, :]).astype(BF16)
        dm = d_gc[main_rows, :]
        s0 = jnp.sum(dm * g2[main_rows, :], axis=0, keepdims=True)
        s1 = jnp.sum(dm * g1[main_rows, :], axis=0, keepdims=True)
        s2 = jnp.sum(dm * ext[main_rows, :], axis=0, keepdims=True)
        rowid = lax.broadcasted_iota(jnp.int32, (8, FF_TILE), 0)
        dcw_ref[...] += jnp.where(rowid == 0, s0, jnp.where(rowid == 1, s1,
                                                            jnp.where(rowid == 2, s2, 0.0)))
        dcb_ref[...] += _colsum8(dm)

    def prev8(part):
        return pl.BlockSpec((None, HALO, FF_TILE), lambda j, i: (part, jnp.maximum(i * nth - 1, 0), j))

    def next8(part):
        return pl.BlockSpec((None, HALO, FF_TILE),
                            lambda j, i: (part, jnp.minimum((i + 1) * nth, last_halo), j))

    def main(part):
        return pl.BlockSpec((None, tm, FF_TILE), lambda j, i: (part, i, j))

    acc = pl.BlockSpec((8, FF_TILE), lambda j, i: (0, j))
    return _pc(body, name="ffn_act_bwd",
               out_shape=(jax.ShapeDtypeStruct((2, t, D_FF), BF16),
                          jax.ShapeDtypeStruct((8, D_FF), F32),
                          jax.ShapeDtypeStruct((8, D_FF), F32)),
               grid=(FF_NJ, ni),
               in_specs=[main(0), prev8(0), next8(0), main(1), next8(1),
                         pl.BlockSpec((tm, D_MODEL), lambda j, i: (i, 0)),
                         pl.BlockSpec((HALO, D_MODEL),
                                      lambda j, i: (jnp.minimum((i + 1) * nth, last_halo), 0)),
                         pl.BlockSpec((FF_TILE, D_MODEL), lambda j, i: (j, 0)),
                         pl.BlockSpec((3, FF_TILE), lambda j, i: (0, j)),
                         pl.BlockSpec((1, FF_TILE), lambda j, i: (0, j))],
               out_specs=(pl.BlockSpec((2, tm, FF_TILE), lambda j, i: (0, i, j)), acc, acc),
               sem=("parallel", "arbitrary"))(h2, h2, h2, h2, h2, dr2, dr2, wd, convw, convb)


def _out_fwd_bwd(act, x1b, r1, p2, tgt, wd, wpg, wpp, g1, b1, g2, b2, tm):
    t = r1.shape[0]

    def body(act_ref, x1b_ref, r1_ref, p_ref, tgt_ref, wd_ref, wpg_ref, wpp_ref,
             g1_ref, b1_ref, g2_ref, b2_ref,
             dr2_ref, dpg_ref, dpp_ref, loss_ref, dg2_ref, db2_ref):
        i = pl.program_id(0)

        @pl.when(i == 0)
        def _():
            loss_ref[...] = jnp.zeros_like(loss_ref)
            dg2_ref[...] = jnp.zeros_like(dg2_ref)
            db2_ref[...] = jnp.zeros_like(db2_ref)

        ffn = _dot(act_ref[...], wd_ref[...])
        pg = _dot(x1b_ref[...], wpg_ref[...])
        pp = _dot(p_ref[...], wpp_ref[...])
        s = _sig(pg)
        xh1, _ = _ln_stats(r1_ref[...])
        x1 = xh1 * g1_ref[...] + b1_ref[...]
        r2 = ALPHA * x1 + ffn + s * pp
        xh2, rstd2 = _ln_stats(r2)
        g2v = g2_ref[...]
        diff = xh2 * g2v + b2_ref[...] - tgt_ref[...]
        part = jnp.sum(jnp.sum(diff * diff, axis=1, keepdims=True), axis=0, keepdims=True)
        loss_ref[...] += jnp.broadcast_to(part * (0.5 / D_MODEL), loss_ref.shape)
        dy = diff * (1.0 / D_MODEL)
        dg2_ref[...] += _colsum8(dy * xh2)
        db2_ref[...] += _colsum8(dy)
        dr2 = _ln_bwd(dy * g2v, xh2, rstd2)
        dr2_ref[...] = dr2
        dpg_ref[...] = (dr2 * pp * s * (1.0 - s)).astype(BF16)
        dpp_ref[...] = (dr2 * s).astype(BF16)

    tile = pl.BlockSpec((tm, D_MODEL), lambda i: (i, 0))
    vec = pl.BlockSpec((1, D_MODEL), lambda i: (0, 0))
    acc8 = pl.BlockSpec((8, D_MODEL), lambda i: (0, 0))
    acc_shape = jax.ShapeDtypeStruct((8, D_MODEL), F32)
    return _pc(body, name="out_fwd_bwd",
               out_shape=(jax.ShapeDtypeStruct((t, D_MODEL), F32),
                          jax.ShapeDtypeStruct((t, D_MODEL), BF16),
                          jax.ShapeDtypeStruct((t, D_MODEL), BF16),
                          acc_shape, acc_shape, acc_shape),
               grid=(t // tm,),
               in_specs=[pl.BlockSpec((tm, D_FF), lambda i: (i, 0)), tile, tile,
                         pl.BlockSpec((tm, PLE_DIM), lambda i: (i, 0)), tile,
                         pl.BlockSpec((D_FF, D_MODEL), lambda i: (0, 0)),
                         pl.BlockSpec((D_MODEL, D_MODEL), lambda i: (0, 0)),
                         pl.BlockSpec((PLE_DIM, D_MODEL), lambda i: (0, 0)),
                         vec, vec, vec, vec],
               out_specs=(tile, tile, tile, acc8, acc8, acc8),
               sem=("arbitrary",))(act, x1b, r1, p2, tgt, wd, wpg, wpp, g1, b1, g2, b2)


def _ffn_in_bwd(dh2, wup_st, dpg, wpg, dr2, r1, g1, tm):
    t = r1.shape[0]
    ni = t // tm

    def body(dh2_ref, wup_ref, dpg_ref, wpg_ref, dr2_ref, r1_ref, g1_ref,
             dr1_ref, dg1_ref, db1_ref, acc):
        i = pl.program_id(0)
        j = pl.program_id(1)

        @pl.when((i == 0) & (j == 0))
        def _():
            dg1_ref[...] = jnp.zeros_like(dg1_ref)
            db1_ref[...] = jnp.zeros_like(db1_ref)

        @pl.when(j == 0)
        def _():
            acc[...] = _dot(dh2_ref[...], wup_ref[...], NT)

        @pl.when(j > 0)
        def _():
            acc[...] += _dot(dh2_ref[...], wup_ref[...], NT)

        @pl.when(j == N_CHIP - 1)
        def _():
            d_x1 = acc[...] + _dot(dpg_ref[...], wpg_ref[...], NT) + ALPHA * dr2_ref[...]
            xh, rstd = _ln_stats(r1_ref[...])
            dg1_ref[...] += _colsum8(d_x1 * xh)
            db1_ref[...] += _colsum8(d_x1)
            dr1_ref[...] = _ln_bwd(d_x1 * g1_ref[...], xh, rstd)

    tile = pl.BlockSpec((tm, D_MODEL), lambda i, j: (i, 0))
    acc8 = pl.BlockSpec((8, D_MODEL), lambda i, j: (0, 0))
    acc_shape = jax.ShapeDtypeStruct((8, D_MODEL), F32)
    return _pc(body, name="ffn_in_bwd",
               out_shape=(jax.ShapeDtypeStruct((t, D_MODEL), F32), acc_shape, acc_shape),
               grid=(ni, N_CHIP),
               in_specs=[pl.BlockSpec((None, tm, FF_TILE), lambda i, j: (j // FF_NJ, i, j % FF_NJ)),
                         pl.BlockSpec((None, D_MODEL, FF_TILE), lambda i, j: (j, 0, 0)),
                         tile, pl.BlockSpec((D_MODEL, D_MODEL), lambda i, j: (0, 0)),
                         tile, tile, pl.BlockSpec((1, D_MODEL), lambda i, j: (0, 0))],
               out_specs=(tile, acc8, acc8),
               scratch=[pltpu.VMEM((tm, D_MODEL), F32)],
               sem=("arbitrary", "arbitrary"))(dh2, wup_st, dpg, wpg, dr2, r1, g1)


ANY = pl.BlockSpec(memory_space=pl.ANY)


def _chip_peers():
    x, y, c = lax.axis_index("x"), lax.axis_index("y"), lax.axis_index("c")
    return x, y, c, [(1 - x, y), (x, 1 - y), (1 - x, 1 - y)]


def _gather_comm(halved, whole=()):
    n, nw = len(halved), len(whole)

    def copies(ins, outs, sems):
        ici_send, ici_recv, d2d_send, d2d_recv, own_send, own_recv = sems
        x, y, c, peers = _chip_peers()
        me = 2 * x + y
        sibling = (x, y, 1 - c)
        own, ici, ici_wait, fwd, fwd_wait = [], [], [], [], []
        for ti in range(n + nw):
            src, dst = ins[ti], outs[ti]
            own.append(pltpu.make_async_remote_copy(
                src_ref=src, dst_ref=dst.at[me], send_sem=own_send.at[ti], recv_sem=own_recv.at[ti],
                device_id=sibling, device_id_type=MESH))
            for k, (px, py) in enumerate(peers):
                pk = 2 * px + py
                sem = dict(send_sem=ici_send.at[ti * 3 + k], recv_sem=ici_recv.at[ti * 3 + k],
                           device_id=(px, py, c), device_id_type=MESH)
                if ti < n:
                    ici.append(pltpu.make_async_remote_copy(src_ref=src.at[c], dst_ref=dst.at[me, c], **sem))
                    ici_wait.append(pltpu.make_async_remote_copy(src_ref=src.at[c], dst_ref=dst.at[pk, c], **sem))
                    dsem = dict(send_sem=d2d_send.at[ti * 3 + k], recv_sem=d2d_recv.at[ti * 3 + k],
                                device_id=sibling, device_id_type=MESH)
                    fwd.append(pltpu.make_async_remote_copy(src_ref=dst.at[pk, c], dst_ref=dst.at[pk, c], **dsem))
                    fwd_wait.append(pltpu.make_async_remote_copy(
                        src_ref=dst.at[pk, 1 - c], dst_ref=dst.at[pk, 1 - c], **dsem))
                else:
                    ici.append(pltpu.make_async_remote_copy(src_ref=src, dst_ref=dst.at[me], **sem))
                    ici_wait.append(pltpu.make_async_remote_copy(src_ref=src, dst_ref=dst.at[pk], **sem))
        return own, ici, ici_wait, fwd, fwd_wait

    def start(ins, outs, sems):
        own, ici, _, _, _ = copies(ins, outs, sems)
        for cp in own + ici:
            cp.start()

    def finish(ins, outs, sems):
        own, ici, ici_wait, fwd, fwd_wait = copies(ins, outs, sems)
        for i, cp in enumerate(ici_wait):
            cp.wait_recv()
            if i < len(fwd):
                fwd[i].start()
        for cp in fwd_wait + own:
            cp.wait_recv()
        for cp in own + ici + fwd:
            cp.wait_send()

    srcs = list(halved) + list(whole)
    return _Comm(srcs, [jax.ShapeDtypeStruct((N_CHIP,) + s.shape, s.dtype) for s in srcs],
                 [pltpu.SemaphoreType.DMA((3 * (n + nw),)), pltpu.SemaphoreType.DMA((3 * (n + nw),)),
                  pltpu.SemaphoreType.DMA((max(3 * n, 1),)), pltpu.SemaphoreType.DMA((max(3 * n, 1),)),
                  pltpu.SemaphoreType.DMA((n + nw,)), pltpu.SemaphoreType.DMA((n + nw,))],
                 start, finish)


def _sibling_exchange_comm(grads):
    n = len(grads)

    def copies(ins, outs, sems):
        send_sems, recv_sems = sems
        x, y, c = lax.axis_index("x"), lax.axis_index("y"), lax.axis_index("c")
        res = []
        for ti in range(n):
            half = ins[ti].shape[1] // 2
            res.append(pltpu.make_async_remote_copy(
                src_ref=ins[ti].at[:, pl.ds(pl.multiple_of((1 - c) * half, 16), half), :],
                dst_ref=outs[ti],
                send_sem=send_sems.at[ti], recv_sem=recv_sems.at[ti],
                device_id=(x, y, 1 - c), device_id_type=MESH))
        return res

    def start(ins, outs, sems):
        for cp in copies(ins, outs, sems):
            cp.start()

    def finish(ins, outs, sems):
        for cp in copies(ins, outs, sems):
            cp.wait()

    return _Comm(grads, [jax.ShapeDtypeStruct((N_CHIP, g.shape[1] // 2, g.shape[2]), g.dtype) for g in grads],
                 [pltpu.SemaphoreType.DMA((n,)), pltpu.SemaphoreType.DMA((n,))], start, finish)


def _in_proj_gathering(x2b, own, chip, tm, comm):
    t = x2b.shape[0]
    ni = t // tm
    half, cols = own.shape[1], own.shape[2]
    nci, nco = len(comm.ins), len(comm.out_shapes)

    def body(chip_ref, x_ref, own_ref, own_hbm, *rest):
        c_in = rest[:nci]
        h_ref, win_out = rest[nci:nci + 2]
        c_out = rest[nci + 2:nci + 2 + nco]
        w_scr, ici_send, ici_recv, d2d_send, d2d_recv, own_sems, ld_sems = rest[nci + 2 + nco:nci + 9 + nco]
        c_sem = rest[nci + 9 + nco:]
        s, i = pl.program_id(0), pl.program_id(1)
        x, y, c, peers = _chip_peers()
        me = 2 * x + y
        sibling = (x, y, 1 - c)

        def ici(k, slot):
            px, py = peers[k]
            return pltpu.make_async_remote_copy(
                src_ref=own_hbm.at[c], dst_ref=win_out.at[slot, c],
                send_sem=ici_send.at[k], recv_sem=ici_recv.at[k],
                device_id=(px, py, c), device_id_type=MESH)

        def forward(k, core):
            pk = 2 * peers[k][0] + peers[k][1]
            return pltpu.make_async_remote_copy(
                src_ref=win_out.at[pk, core], dst_ref=win_out.at[pk, core],
                send_sem=d2d_send.at[k], recv_sem=d2d_recv.at[k],
                device_id=sibling, device_id_type=MESH)

        place_own = pltpu.make_async_remote_copy(
            src_ref=own_hbm, dst_ref=win_out.at[me], send_sem=own_sems.at[0], recv_sem=own_sems.at[1],
            device_id=sibling, device_id_type=MESH)

        @pl.when((s == 0) & (i == 0))
        def _():
            for k in range(3):
                ici(k, me).start()
            place_own.start()

        @pl.when(s == 0)
        def _():
            xv = x_ref[...]
            h_ref[...] = (_dot(xv[:, :half], own_ref[0]) + _dot(xv[:, half:], own_ref[1])).astype(BF16)

        for k in range(3):
            @pl.when((s == k + 1) & (i == 0))
            def _(k=k):
                pk = 2 * peers[k][0] + peers[k][1]
                ici(k, pk).wait_recv()
                forward(k, c).start()
                forward(k, 1 - c).wait_recv()
                loads = [pltpu.make_async_copy(win_out.at[pk, hh], w_scr.at[hh], ld_sems.at[hh])
                         for hh in range(2)]
                for ld in loads:
                    ld.start()
                for ld in loads:
                    ld.wait()
                if k == 1:
                    comm.start(c_in, c_out, c_sem)

        @pl.when(s > 0)
        def _():
            xv = x_ref[...]
            h_ref[...] = (_dot(xv[:, :half], w_scr[0]) + _dot(xv[:, half:], w_scr[1])).astype(BF16)

        @pl.when((s == N_CHIP - 1) & (i == ni - 1))
        def _():
            place_own.wait()
            for k in range(3):
                ici(k, me).wait_send()
                forward(k, c).wait_send()
            comm.finish(c_in, c_out, c_sem)

    def shard_col(s, me):
        return jnp.where(s == 0, me, me ^ jnp.where(s == 1, 2, jnp.where(s == 2, 1, 3)))

    res = _pc(body, name="in_proj",
              out_shape=(jax.ShapeDtypeStruct((t, N_CHIP * cols), BF16),
                         jax.ShapeDtypeStruct((N_CHIP,) + own.shape, own.dtype)) + tuple(comm.out_shapes),
              grid=(N_CHIP, ni), nsp=1,
              in_specs=[pl.BlockSpec((tm, 2 * half), lambda s, i, chip_ref: (i, 0)),
                        pl.BlockSpec(own.shape, lambda s, i, chip_ref: (0, 0, 0)),
                        ANY] + [ANY] * nci,
              out_specs=(pl.BlockSpec((tm, cols), lambda s, i, chip_ref: (i, shard_col(s, chip_ref[0]))),
                         ANY) + tuple([ANY] * nco),
              scratch=[pltpu.VMEM(own.shape, own.dtype),
                       pltpu.SemaphoreType.DMA((3,)), pltpu.SemaphoreType.DMA((3,)),
                       pltpu.SemaphoreType.DMA((3,)), pltpu.SemaphoreType.DMA((3,)),
                       pltpu.SemaphoreType.DMA((2,)), pltpu.SemaphoreType.DMA((2,))] + comm.sems,
              sem=("arbitrary", "arbitrary"))(chip, x2b, own, own, *comm.ins)
    return res[0], res[1], res[2:]


def _rs_add_halves(name, grad, recv, core):
    _, r, cdim = grad.shape
    half = r // 2
    tr = _row_tile(half, cdim, mult=16)
    nr = half // tr

    def body(c_ref, g_ref, r_ref, o_ref):
        o_ref[...] = (g_ref[...].astype(F32) + r_ref[...].astype(F32)).astype(BF16)

    return _pc(body, name=name, out_shape=jax.ShapeDtypeStruct((N_CHIP, half, cdim), BF16),
               grid=(N_CHIP, nr), nsp=1,
               in_specs=[pl.BlockSpec((None, tr, cdim), lambda j, i, c_ref: (j, c_ref[0] * nr + i, 0)),
                         pl.BlockSpec((None, tr, cdim), lambda j, i, c_ref: (j, i, 0))],
               out_specs=pl.BlockSpec((None, tr, cdim), lambda j, i, c_ref: (j, i, 0)),
               sem=("parallel", "parallel"))(core, grad, recv)


def _chip_exchange_comm(parts):
    n = len(parts)

    def copies(ins, outs, sems):
        send_sems, recv_sems = sems
        x, y, c, peers = _chip_peers()
        return [pltpu.make_async_remote_copy(
            src_ref=ins[ti].at[2 * px + py], dst_ref=outs[ti].at[k],
            send_sem=send_sems.at[ti * 3 + k], recv_sem=recv_sems.at[ti * 3 + k],
            device_id=(px, py, c), device_id_type=MESH)
            for ti in range(n) for k, (px, py) in enumerate(peers)]

    def start(ins, outs, sems):
        for cp in copies(ins, outs, sems):
            cp.start()

    def finish(ins, outs, sems):
        for cp in copies(ins, outs, sems):
            cp.wait()

    return _Comm(parts, [jax.ShapeDtypeStruct((3,) + p.shape[1:], p.dtype) for p in parts],
                 [pltpu.SemaphoreType.DMA((3 * n,)), pltpu.SemaphoreType.DMA((3 * n,))], start, finish)


def _rs_sum_chips(name, part, recv, chip):
    _, half, cdim = recv.shape
    tr = _row_tile(half, cdim, mult=16)

    def body(chip_ref, p_ref, r_ref, o_ref):
        o_ref[...] = ((p_ref[...].astype(F32) + r_ref[0].astype(F32)) + r_ref[1].astype(F32)
                      ) + r_ref[2].astype(F32)

    return _pc(body, name=name, out_shape=jax.ShapeDtypeStruct((half, cdim), F32),
               grid=(half // tr,), nsp=1,
               in_specs=[pl.BlockSpec((None, tr, cdim), lambda i, chip_ref: (chip_ref[0], i, 0)),
                         pl.BlockSpec((3, tr, cdim), lambda i, chip_ref: (0, i, 0))],
               out_specs=pl.BlockSpec((tr, cdim), lambda i, chip_ref: (i, 0)),
               sem=("parallel",))(chip, part, recv)


def _rs_send_halves(halves):
    n = len(halves)

    def body(*refs):
        ins, outs = refs[:n], refs[n:2 * n]
        send_sems, recv_sems = refs[2 * n:]
        x, y, c = lax.axis_index("x"), lax.axis_index("y"), lax.axis_index("c")
        sends = []
        for ti in range(n):
            cp = pltpu.make_async_remote_copy(
                src_ref=ins[ti], dst_ref=outs[ti],
                send_sem=send_sems.at[ti], recv_sem=recv_sems.at[ti],
                device_id=(x, y, 1 - c), device_id_type=MESH)
            cp.start()
            sends.append(cp)
        for cp in sends:
            cp.wait()

    return _pc(body, name="rs_send_halves",
               out_shape=tuple(jax.ShapeDtypeStruct(hv.shape, hv.dtype) for hv in halves),
               in_specs=[ANY] * n, out_specs=tuple([ANY] * n),
               scratch=[pltpu.SemaphoreType.DMA((n,)), pltpu.SemaphoreType.DMA((n,))])(*halves)


def _adamw_rows(name, mine, theirs, w, m, v, core):
    half, cdim = mine.shape
    tr = _row_tile(half, cdim, budget=1 << 19)
    nrh = half // tr

    def body(c_ref, mine_ref, theirs_ref, w_ref, m_ref, v_ref, g_ref, d_ref, m2_ref, v2_ref):
        is_mine = (pl.program_id(0) // nrh) == c_ref[0]
        g = jnp.where(is_mine, mine_ref[...], theirs_ref[...])
        d, m2, v2 = _adamw(w_ref[...], g, m_ref[...], v_ref[...])
        g_ref[...] = g
        d_ref[...] = d
        m2_ref[...] = m2
        v2_ref[...] = v2

    htile = pl.BlockSpec((tr, cdim), lambda i, c_ref: (i % nrh, 0))
    tile = pl.BlockSpec((tr, cdim), lambda i, c_ref: (i, 0))
    shp = jax.ShapeDtypeStruct((2 * half, cdim), F32)
    return _pc(body, name=name, out_shape=(shp, shp, shp, shp), grid=(2 * nrh,), nsp=1,
               in_specs=[htile, htile, tile, tile, tile], out_specs=(tile, tile, tile, tile),
               sem=("parallel",))(core, mine, theirs, w, m, v)


def _adamw_whole(name, g, w, m, v):
    def body(g_ref, w_ref, m_ref, v_ref, d_ref, m2_ref, v2_ref):
        d, m2, v2 = _adamw(w_ref[...], g_ref[...], m_ref[...], v_ref[...])
        d_ref[...] = d
        m2_ref[...] = m2
        v2_ref[...] = v2

    shp = jax.ShapeDtypeStruct(g.shape, F32)
    return _pc(body, name=name, out_shape=(shp, shp, shp))(g, w, m, v)


SMALL_LAYOUT = (
    ("sgu_w_s", 1024, 1, 0),
    ("sgu_b_s", 8, 1, 1024),
    ("sgu_norm_g", 1, 0, 0),
    ("sgu_norm_b", 1, 0, 1),
    ("hgrn_norm_g", 1, 0, 3),
    ("ln1_g", 1, 0, 4),
    ("ln1_b", 1, 0, 5),
    ("ffn_conv_b", 1, 2, 3),
    ("ln2_g", 1, 0, 6),
    ("ln2_b", 1, 0, 7),
)
LB_ROW = 2
LOSS_ROW = 8
PACK_SHAPES = ((16, D_MODEL), (N_GROUP * 128 + 8, 128), (8, D_FF))


def _small_allreduce_adamw(rows1024, dws, dbs, dcw, dcb, logits, m_logits, v_logits,
                           small_w, small_m, small_v):
    ns = len(SMALL_LAYOUT)
    nr = len(rows1024)
    nb = len(PACK_SHAPES)

    def body(*refs):
        row_refs = refs[:nr]
        dws_ref, dbs_ref, dcw_ref, dcb_ref, lg_ref, mlg_ref, vlg_ref = refs[nr:nr + 7]
        pos = nr + 7
        w_refs = refs[pos:pos + ns]
        m_refs = refs[pos + ns:pos + 2 * ns]
        v_refs = refs[pos + 2 * ns:pos + 3 * ns]
        pos += 3 * ns
        loss_ref, dcw_out = refs[pos:pos + 2]
        lg_outs = refs[pos + 2:pos + 6]
        pos += 6
        outs = refs[pos:pos + 4 * ns]
        pos += 4 * ns
        pack = refs[pos:pos + nb]
        sib = refs[pos + nb:pos + 2 * nb]
        gath = refs[pos + 2 * nb:pos + 3 * nb]
        d2d_send, d2d_recv, ici_send, ici_recv = refs[pos + 3 * nb:]

        x, y, c, peers = _chip_peers()
        me = 2 * x + y
        sibling = (x, y, 1 - c)

        pack[0][...] = jnp.zeros(PACK_SHAPES[0], F32)
        for k in range(nr):
            pack[0][k:k + 1, :] = row_refs[k][0:1, :]
        pack[1][0:N_GROUP * 128, :] = dws_ref[...]
        pack[1][N_GROUP * 128:, :] = dbs_ref[...]
        pack[2][...] = jnp.zeros(PACK_SHAPES[2], F32)
        pack[2][0:3, :] = dcw_ref[0:3, :]
        pack[2][3:4, :] = dcb_ref[0:1, :]

        d2d = [pltpu.make_async_remote_copy(
            src_ref=pack[b], dst_ref=sib[b], send_sem=d2d_send.at[b], recv_sem=d2d_recv.at[b],
            device_id=sibling, device_id_type=MESH) for b in range(nb)]
        for cp in d2d:
            cp.start()
        for cp in d2d:
            cp.wait()
        for b in range(nb):
            gath[b][me] = pack[b][...] + sib[b][...]

        ici, ici_wait = [], []
        for b in range(nb):
            for k, (px, py) in enumerate(peers):
                sem = dict(send_sem=ici_send.at[b * 3 + k], recv_sem=ici_recv.at[b * 3 + k],
                           device_id=(px, py, c), device_id_type=MESH)
                ici.append(pltpu.make_async_remote_copy(src_ref=gath[b].at[me], dst_ref=gath[b].at[me], **sem))
                ici_wait.append(pltpu.make_async_remote_copy(
                    src_ref=gath[b].at[me], dst_ref=gath[b].at[2 * px + py], **sem))
        for cp in ici:
            cp.start()
        for cp in ici_wait:
            cp.wait_recv()
        for cp in ici:
            cp.wait_send()

        tot = pack
        for b in range(nb):
            tot[b][...] = ((gath[b][0] + gath[b][1]) + gath[b][2]) + gath[b][3]

        loss_ref[...] = tot[0][LOSS_ROW:LOSS_ROW + 1, :]
        dcw_out[...] = tot[2][...]
        lb = _sig(lg_ref[0:1, :] - lg_ref[1:2, :])
        d0 = tot[0][LB_ROW:LB_ROW + 1, :] * lb * (1.0 - lb)
        rowid = lax.broadcasted_iota(jnp.int32, (2, D_MODEL), 0)
        g_lg = jnp.where(rowid == 0, d0, -d0)
        dl, ml, vl = _adamw(lg_ref[...], g_lg, mlg_ref[...], vlg_ref[...])
        lg_outs[0][...] = g_lg
        lg_outs[1][...] = dl
        lg_outs[2][...] = ml
        lg_outs[3][...] = vl
        for si, (_, rows, b, r0) in enumerate(SMALL_LAYOUT):
            g = tot[b][r0:r0 + rows, :]
            dl, ml, vl = _adamw(w_refs[si][...], g, m_refs[si][...], v_refs[si][...])
            outs[4 * si][...] = g
            outs[4 * si + 1][...] = dl
            outs[4 * si + 2][...] = ml
            outs[4 * si + 3][...] = vl

    shapes = [jax.ShapeDtypeStruct((1, D_MODEL), F32), jax.ShapeDtypeStruct((8, D_FF), F32)]
    shapes += [jax.ShapeDtypeStruct((2, D_MODEL), F32)] * 4
    for w in small_w:
        shapes += [jax.ShapeDtypeStruct(w.shape, F32)] * 4
    scratch = [pltpu.VMEM(shp, F32) for shp in PACK_SHAPES]
    scratch += [pltpu.VMEM(shp, F32) for shp in PACK_SHAPES]
    scratch += [pltpu.VMEM((N_CHIP,) + shp, F32) for shp in PACK_SHAPES]
    scratch += [pltpu.SemaphoreType.DMA((nb,)), pltpu.SemaphoreType.DMA((nb,)),
                pltpu.SemaphoreType.DMA((3 * nb,)), pltpu.SemaphoreType.DMA((3 * nb,))]
    vm = pl.BlockSpec(memory_space=pltpu.VMEM)
    n_in = nr + 7 + 3 * ns
    res = _pc(body, name="small_allreduce_adamw", out_shape=tuple(shapes),
              in_specs=[vm] * n_in, out_specs=tuple([vm] * len(shapes)),
              scratch=scratch)(*rows1024, dws, dbs, dcw, dcb, logits, m_logits, v_logits,
                               *small_w, *small_m, *small_v)
    return res[0], res[1], res[2:6], res[6:]


def kernel(x, p, w_in, sgu_w_s, sgu_b_s, sgu_norm_g, sgu_norm_b, hgrn_lb_logits, hgrn_norm_g, w_branch, w_out, ln1_g, ln1_b, ffn_w_up, ffn_conv_w, ffn_conv_b, ffn_w_down, ln2_g, ln2_b, ple_w_proj, ple_w_gate, loss_target, m_w_in, m_sgu_w_s, m_sgu_b_s, m_sgu_norm_g, m_sgu_norm_b, m_hgrn_lb_logits, m_hgrn_norm_g, m_w_branch, m_w_out, m_ln1_g, m_ln1_b, m_ffn_w_up, m_ffn_conv_w, m_ffn_conv_b, m_ffn_w_down, m_ln2_g, m_ln2_b, m_ple_w_proj, m_ple_w_gate, v_w_in, v_sgu_w_s, v_sgu_b_s, v_sgu_norm_g, v_sgu_norm_b, v_hgrn_lb_logits, v_hgrn_norm_g, v_w_branch, v_w_out, v_ln1_g, v_ln1_b, v_ffn_w_up, v_ffn_conv_w, v_ffn_conv_b, v_ffn_w_down, v_ln2_g, v_ln2_b, v_ple_w_proj, v_ple_w_gate):
    t = x.shape[1]
    x2 = x.reshape(t, D_MODEL)
    x2b = x2.astype(BF16)
    p2 = p.reshape(t, PLE_DIM)
    tgt = loss_target.reshape(t, D_MODEL)
    core = lax.axis_index("c").astype(jnp.int32).reshape(1)
    chip_id = (2 * lax.axis_index("x") + lax.axis_index("y")).astype(jnp.int32).reshape(1)

    big_w = [w_in[0], w_branch[0, 0], w_branch[0, 1], w_out[0], ffn_w_up[0], ffn_w_down[0],
             ple_w_proj[0], ple_w_gate[0]]
    big_m = [m_w_in[0], m_w_branch[0, 0], m_w_branch[0, 1], m_w_out[0], m_ffn_w_up[0],
             m_ffn_w_down[0], m_ple_w_proj[0], m_ple_w_gate[0]]
    big_v = [v_w_in[0], v_w_branch[0, 0], v_w_branch[0, 1], v_w_out[0], v_ffn_w_up[0],
             v_ffn_w_down[0], v_ple_w_proj[0], v_ple_w_gate[0]]
    def halves_of(i):
        w = big_w[i]
        return w.astype(BF16).reshape(2, w.shape[0] // 2, w.shape[1])

    def stacked(g, i):
        return g.reshape(N_CHIP, big_w[i].shape[0], big_w[i].shape[1])


    cid = jnp.arange(SGU_BLOCK) // CHUNK
    maskf = (cid[:, None] >= cid[None, :]).astype(F32)
    ws_masked = sgu_w_s[0] * maskf[None]
    wm = ws_masked.astype(BF16)
    wmt = jnp.transpose(ws_masked, (0, 2, 1)).astype(BF16)
    bsb = jnp.broadcast_to(sgu_b_s[0][:, :, None], (N_GROUP, SGU_BLOCK, 128))

    h, win_g, (wup_g,) = _in_proj_gathering(x2b, halves_of(0), chip_id, 512, _gather_comm([halves_of(4)]))
    win_st = stacked(win_g, 0)
    wup_st = stacked(wup_g, 4)
    ya, _ = _sgu_fwd(h, wm, bsb, sgu_norm_g, sgu_norm_b)
    (yb, st_all), mix_g = _hgrn_fwd(h, hgrn_lb_logits, hgrn_norm_g,
                                     comm=_gather_comm([halves_of(i) for i in (1, 2, 3)], [ffn_conv_w[0]]))
    wb0, wb1, wo = [stacked(g, i).reshape(D_MODEL, D_MODEL) for g, i in zip(mix_g[:3], (1, 2, 3))]
    convw = jnp.transpose(mix_g[3], (1, 0, 2)).reshape(3, D_FF)
    (r1, a_br, b_br, m_bf, x1b), _ = _mix_fwd(ya, yb, h, x2, wb0, wb1, wo, ln1_g, ln1_b, 256)
    h2, act, out_g = _ffn_up_act(x1b, wup_st, convw, ffn_conv_b, 512,
                                 _gather_comm([halves_of(i) for i in (5, 6, 7)]))
    wd = stacked(out_g[0], 5).reshape(D_FF, D_MODEL)
    wpp = jnp.transpose(stacked(out_g[1], 6), (1, 0, 2)).reshape(PLE_DIM, D_MODEL)
    wpg = stacked(out_g[2], 7).reshape(D_MODEL, D_MODEL)
    dr2, dpg, dpp, loss_acc, dg2, db2 = _out_fwd_bwd(
        act, x1b, r1, p2, tgt, wd, wpg, wpp, ln1_g, ln1_b, ln2_g, ln2_b, 256)

    dh2, dcw, dcb = _ffn_act_bwd(h2, dr2, wd, convw, ffn_conv_b, 256)
    d_wd = _mm_tn("ffn_down_wgrad", act, dr2, FF_TILE, 512)
    d_wpg = _mm_tn("ple_gate_wgrad", x1b, dpg, 512, D_MODEL)
    d_wpp_st = _mm_tn("ple_proj_wgrad", p2, dpp, PLE_DIM, PLE_DIM, stacked=True)
    d_wup_st = _mm("ffn_up_wgrad", x1b, dh2, TN, (2, N_CHIP, 1),
                   pl.BlockSpec((t, 512), lambda i, j, k: (0, i)),
                   pl.BlockSpec((None, t, FF_TILE), lambda i, j, k: (j // FF_NJ, 0, j % FF_NJ)),
                   jax.ShapeDtypeStruct((N_CHIP, D_MODEL, FF_TILE), BF16),
                   pl.BlockSpec((None, 512, FF_TILE), lambda i, j, k: (j, i, 0)))
    dr1, dg1, db1 = _ffn_in_bwd(dh2, wup_st, dpg, wpg, dr2, r1, ln1_g, 512)
    da_bf, db_bf, dh3, dya, dyb = _mix_bwd(dr1, h, a_br, b_br, wo, wb0, wb1, 256)
    d_wo = _mm_tn("out_proj_wgrad", m_bf, dr1, 512, 512)
    d_wb0 = _mm_tn("branch0_wgrad", ya, da_bf, 512, D_MODEL)
    d_wb1 = _mm_tn("branch1_wgrad", yb, db_bf, 512, D_MODEL)
    grads_1 = [d_wb0.reshape(4, 256, D_MODEL), d_wb1.reshape(4, 256, D_MODEL),
               d_wo.reshape(4, 256, D_MODEL), d_wup_st, d_wd.reshape(4, D_FF // 4, D_MODEL),
               d_wpp_st, d_wpg.reshape(4, 256, D_MODEL)]
    (dh0, dws, dbs, dgv, dbv), recv_a1 = _sgu_bwd(h, dya, wm, wmt, bsb, sgu_norm_g, sgu_norm_b, maskf,
                                                  comm=_sibling_exchange_comm(grads_1))
    parts_1 = [_rs_add_halves("rs_add_halves%d" % (i + 1), g, r, core)
               for i, (g, r) in enumerate(zip(grads_1, recv_a1))]
    (dh1, dh2h, dlb, dgn), recv_b1 = _hgrn_bwd(h, dyb, st_all, hgrn_lb_logits, hgrn_norm_g,
                                                comm=_chip_exchange_comm(parts_1))
    dh_parts = [dh0, dh1, dh2h, dh3]
    d_win = [_mm_tn("in_proj_wgrad%d" % j, x2b, dh_parts[j], 512, D_MODEL) for j in range(4)]

    grads_0 = [jnp.stack(d_win)]
    recv_a0 = _run_comm("rs_sibling_exchange0", _sibling_exchange_comm(grads_0))
    parts_0 = [_rs_add_halves("rs_add_halves0", grads_0[0], recv_a0[0], core)]
    gx, recv_b0 = _in_proj_xgrad(dh_parts, win_st, dr1, 512, comm=_chip_exchange_comm(parts_0))
    parts = parts_0 + parts_1
    recv_b = list(recv_b0) + list(recv_b1)
    halves = [_rs_sum_chips("rs_sum_chips%d" % i, pt, r, chip_id)
              for i, (pt, r) in enumerate(zip(parts, recv_b))]
    theirs = _rs_send_halves(halves)
    big_out = [_adamw_rows("adamw_big%d" % i, halves[i], theirs[i], big_w[i], big_m[i], big_v[i], core)
               for i in range(len(halves))]

    small_in = dict(sgu_w_s=(sgu_w_s, m_sgu_w_s, v_sgu_w_s), sgu_b_s=(sgu_b_s, m_sgu_b_s, v_sgu_b_s),
                    sgu_norm_g=(sgu_norm_g, m_sgu_norm_g, v_sgu_norm_g),
                    sgu_norm_b=(sgu_norm_b, m_sgu_norm_b, v_sgu_norm_b),
                    hgrn_norm_g=(hgrn_norm_g, m_hgrn_norm_g, v_hgrn_norm_g),
                    ln1_g=(ln1_g, m_ln1_g, v_ln1_g), ln1_b=(ln1_b, m_ln1_b, v_ln1_b),
                    ffn_conv_b=(ffn_conv_b, m_ffn_conv_b, v_ffn_conv_b),
                    ln2_g=(ln2_g, m_ln2_g, v_ln2_g), ln2_b=(ln2_b, m_ln2_b, v_ln2_b))

    def flat(name, arr):
        rows = dict((n, r) for n, r, _, _ in SMALL_LAYOUT)[name]
        return arr.reshape(rows, arr.size // rows)

    names = [n for n, _, _, _ in SMALL_LAYOUT]
    sw = [flat(n, small_in[n][0]) for n in names]
    sm = [flat(n, small_in[n][1]) for n in names]
    sv = [flat(n, small_in[n][2]) for n in names]
    loss_rows, dcw_tot, lg_out, small_out = _small_allreduce_adamw(
        [dgv, dbv, dlb, dgn, dg1, db1, dg2, db2, loss_acc], dws.reshape(N_GROUP * 128, 128), dbs, dcw, dcb,
        hgrn_lb_logits, m_hgrn_lb_logits, v_hgrn_lb_logits, sw, sm, sv)
    loss = loss_rows[0, 0]

    chip = 2 * lax.axis_index("x") + lax.axis_index("y")
    g_cw = lax.dynamic_slice(dcw_tot, (0, chip * (D_FF // 4)), (3, D_FF // 4))
    cw_out = _adamw_whole("adamw_conv_w", g_cw, ffn_conv_w[0], m_ffn_conv_w[0], v_ffn_conv_w[0])

    res = {}
    for si, n in enumerate(names):
        shp = small_in[n][0].shape
        res[n] = tuple(small_out[4 * si + k].reshape(shp) for k in range(4))
    res["hgrn_lb_logits"] = tuple(lg_out)
    res["ffn_conv_w"] = (g_cw[None],) + tuple(o[None] for o in cw_out)

    def big(i):
        return tuple(big_out[i])

    res["w_in"] = tuple(o[None] for o in big(0))
    res["w_branch"] = tuple(jnp.stack([o0, o1])[None] for o0, o1 in zip(big(1), big(2)))
    res["w_out"] = tuple(o[None] for o in big(3))
    res["ffn_w_up"] = tuple(o[None] for o in big(4))
    res["ffn_w_down"] = tuple(o[None] for o in big(5))
    res["ple_w_proj"] = tuple(o[None] for o in big(6))
    res["ple_w_gate"] = tuple(o[None] for o in big(7))

    order = ["w_in", "sgu_w_s", "sgu_b_s", "sgu_norm_g", "sgu_norm_b", "hgrn_lb_logits",
             "hgrn_norm_g", "w_branch", "w_out", "ln1_g", "ln1_b", "ffn_w_up", "ffn_conv_w",
             "ffn_conv_b", "ffn_w_down", "ln2_g", "ln2_b", "ple_w_proj", "ple_w_gate"]
    outs = [loss, gx.reshape(1, t, D_MODEL)]
    for k in range(4):
        outs += [res[n][k] for n in order]
    return tuple(outs)
```

```python
import functools

import jax
import jax.numpy as jnp
from jax import lax
from jax.experimental import pallas as pl
from jax.experimental.pallas import tpu as pltpu

F32 = jnp.float32
BF16 = jnp.bfloat16
HIGHEST = lax.Precision.HIGHEST
MESH = pl.DeviceIdType.MESH

D_MODEL = 1024
CHUNK = 64
SGU_BLOCK = 128
N_GROUP = 8
N_HEAD = 8
HEAD_DIM = 128
D_FF = 2816
PLE_DIM = 256
IN_COLS = 8192
LN_EPS = 1e-5
RMS_EPS = 1e-6
ALPHA = 2.0 ** 0.25
N_CHIP = 4
N_DEV = 8

ADAM_LR = 0.001
ADAM_B1 = 0.9
ADAM_B2 = 0.999
ADAM_EPS = 1e-08
ADAM_WD = 0.01
ADAM_STEP = 10

VMEM_LIMIT = 56 * 1024 * 1024

NN = (((1,), (0,)), ((), ()))
NT = (((1,), (1,)), ((), ()))
TN = (((0,), (0,)), ((), ()))


def _pc(body, *, name, out_shape, grid=None, in_specs=None, out_specs=None, scratch=(),
        sem=None, nsp=0, vmem=VMEM_LIMIT):
    params = dict(vmem_limit_bytes=vmem)
    if sem is not None:
        params["dimension_semantics"] = sem
    kw = dict(name=name, out_shape=out_shape, compiler_params=pltpu.CompilerParams(**params))
    if nsp:
        kw["grid_spec"] = pltpu.PrefetchScalarGridSpec(
            num_scalar_prefetch=nsp, grid=grid, in_specs=in_specs, out_specs=out_specs,
            scratch_shapes=list(scratch))
    else:
        if grid is not None:
            kw["grid"] = grid
        if in_specs is not None:
            kw["in_specs"] = in_specs
            kw["out_specs"] = out_specs
        kw["scratch_shapes"] = list(scratch)
    return pl.pallas_call(body, **kw)


def _dot(a, b, dims=NN):
    return lax.dot_general(a.astype(BF16), b.astype(BF16), dims, preferred_element_type=F32)


def _dot32(a, b, dims=NN):
    return lax.dot_general(a, b, dims, precision=HIGHEST, preferred_element_type=F32)


def _sig(x):
    return 1.0 / (1.0 + jnp.exp(-x))


_GC = 0.7978845608028654
_GA = 0.044715


def _gelu(x):
    return 0.5 * x * (1.0 + jnp.tanh(_GC * (x + _GA * x * x * x)))


def _gelu_and_grad(x):
    t = jnp.tanh(_GC * (x + _GA * x * x * x))
    g = 0.5 * x * (1.0 + t)
    dg = 0.5 * (1.0 + t) + 0.5 * x * (1.0 - t * t) * _GC * (1.0 + 3.0 * _GA * x * x)
    return g, dg


def _ln_stats(r):
    mu = jnp.mean(r, axis=-1, keepdims=True)
    xc = r - mu
    var = jnp.mean(xc * xc, axis=-1, keepdims=True)
    rstd = lax.rsqrt(var + LN_EPS)
    return xc * rstd, rstd


def _ln_bwd(dxh, xh, rstd):
    m1 = jnp.mean(dxh, axis=-1, keepdims=True)
    m2 = jnp.mean(dxh * xh, axis=-1, keepdims=True)
    return rstd * (dxh - m1 - xh * m2)


def _colsum8(v):
    return jnp.broadcast_to(jnp.sum(v, axis=0, keepdims=True), (8, v.shape[1]))


def _adamw(w, g, m, v):
    m2 = ADAM_B1 * m + (1.0 - ADAM_B1) * g
    v2 = ADAM_B2 * v + (1.0 - ADAM_B2) * (g * g)
    m_hat = m2 / (1.0 - ADAM_B1 ** ADAM_STEP)
    v_hat = v2 / (1.0 - ADAM_B2 ** ADAM_STEP)
    delta = -ADAM_LR * (m_hat / (jnp.sqrt(v_hat) + ADAM_EPS) + ADAM_WD * w)
    return delta, m2, v2


def _row_tile(rows, cols, itemsize=4, budget=1 << 20, mult=8):
    best = mult
    for tr in range(mult, rows + 1, mult):
        if rows % tr == 0 and tr * cols * itemsize <= budget:
            best = tr
    return best


def _mm(name, a, b, dims, grid, a_spec, b_spec, out_shape, o_spec, add=None, add_spec=None,
        add_scale=1.0, comm=None):
    nk = grid[2]
    has_add = add is not None
    out_dtype = out_shape.dtype

    def body(*refs):
        if has_add:
            a_ref, b_ref, add_ref, o_ref = refs[:4]
            rest = refs[4:]
        else:
            a_ref, b_ref, o_ref = refs[:3]
            add_ref = None
            rest = refs[3:]
        prod = _dot(a_ref[...], b_ref[...], dims)

        def finish(acc):
            if has_add:
                acc = acc + add_scale * add_ref[...]
            o_ref[...] = acc.astype(out_dtype)

        if nk == 1:
            finish(prod)
        else:
            acc_ref = rest[0]
            k = pl.program_id(2)

            @pl.when(k == 0)
            def _():
                acc_ref[...] = prod

            @pl.when(k > 0)
            def _():
                acc_ref[...] += prod

            @pl.when(k == nk - 1)
            def _():
                finish(acc_ref[...])

    in_specs = [a_spec, b_spec] + ([add_spec] if has_add else [])
    args = [a, b] + ([add] if has_add else [])
    scratch = []
    if nk > 1:
        blk = [d for d in o_spec.block_shape if d is not None]
        scratch = [pltpu.VMEM(tuple(blk), F32)]
    if comm is None:
        return _pc(body, name=name, out_shape=out_shape, grid=grid, in_specs=in_specs,
                   out_specs=o_spec, scratch=scratch,
                   sem=("parallel", "parallel", "arbitrary"))(*args)

    def first():
        return (pl.program_id(0) == 0) & (pl.program_id(1) == 0) & (pl.program_id(2) == 0)

    def last():
        return ((pl.program_id(0) == grid[0] - 1) & (pl.program_id(1) == grid[1] - 1)
                & (pl.program_id(2) == grid[2] - 1))

    res = _hosted_call(body, comm, first, last, name=name, out_shape=(out_shape,), grid=grid,
                       in_specs=in_specs, out_specs=(o_spec,), scratch=scratch,
                       sem=("arbitrary", "arbitrary", "arbitrary"), args=args)
    return res[0], res[1:]


class _Comm:
    def __init__(self, ins, out_shapes, sems, start, finish):
        self.ins, self.out_shapes, self.sems = list(ins), list(out_shapes), list(sems)
        self.start, self.finish = start, finish


def _hosted_call(body, comm, first, last, *, name, out_shape, grid, in_specs, out_specs, scratch, sem,
                 args):
    n_in, n_out, n_scr = len(in_specs), len(out_shape), len(scratch)
    nci, nco = len(comm.ins), len(comm.out_shapes)

    def wrapped(*refs):
        pos = n_in
        own_in, c_in = refs[:pos], refs[pos:pos + nci]
        pos += nci
        own_out, c_out = refs[pos:pos + n_out], refs[pos + n_out:pos + n_out + nco]
        pos += n_out + nco
        own_scr, c_sem = refs[pos:pos + n_scr], refs[pos + n_scr:]

        @pl.when(first())
        def _():
            comm.start(c_in, c_out, c_sem)

        body(*own_in, *own_out, *own_scr)

        @pl.when(last())
        def _():
            comm.finish(c_in, c_out, c_sem)

    return _pc(wrapped, name=name, out_shape=tuple(out_shape) + tuple(comm.out_shapes), grid=grid,
               in_specs=list(in_specs) + [ANY] * nci, out_specs=tuple(out_specs) + tuple([ANY] * nco),
               scratch=list(scratch) + comm.sems, sem=sem)(*args, *comm.ins)


def _grid1_call(body, comm, n, *, name, out_shape, in_specs, out_specs, scratch, args):
    if comm is None:
        return _pc(body, name=name, out_shape=out_shape, grid=(n,), in_specs=in_specs,
                   out_specs=out_specs, scratch=scratch, sem=("arbitrary",))(*args), ()
    res = _hosted_call(body, comm, lambda: pl.program_id(0) == 0, lambda: pl.program_id(0) == n - 1,
                       name=name, out_shape=out_shape, grid=(n,), in_specs=in_specs,
                       out_specs=out_specs, scratch=scratch, sem=("arbitrary",), args=args)
    return res[:len(out_shape)], res[len(out_shape):]


def _run_comm(name, comm):
    nci, nco = len(comm.ins), len(comm.out_shapes)

    def body(*refs):
        c_in, c_out, c_sem = refs[:nci], refs[nci:nci + nco], refs[nci + nco:]
        comm.start(c_in, c_out, c_sem)
        comm.finish(c_in, c_out, c_sem)

    return _pc(body, name=name, out_shape=tuple(comm.out_shapes), in_specs=[ANY] * nci,
               out_specs=tuple([ANY] * nco), scratch=comm.sems)(*comm.ins)


def _mm_nn_stacked(name, a, w_st, tm, comm=None):
    t, k = a.shape
    _, _, c = w_st.shape
    return _mm(name, a, w_st, NN, (t // tm, N_CHIP, 1),
               pl.BlockSpec((tm, k), lambda i, j, kk: (i, 0)),
               pl.BlockSpec((None, k, c), lambda i, j, kk: (j, 0, 0)),
               jax.ShapeDtypeStruct((t, N_CHIP * c), BF16),
               pl.BlockSpec((tm, c), lambda i, j, kk: (i, j)), comm=comm)


def _mm_tn(name, a, b, tm, tn, stacked=False):
    t, m = a.shape
    _, n = b.shape
    if stacked:
        assert tm == m
        out_shape = jax.ShapeDtypeStruct((n // tn, m, tn), BF16)
        o_spec = pl.BlockSpec((None, tm, tn), lambda i, j, kk: (j, 0, 0))
    else:
        out_shape = jax.ShapeDtypeStruct((m, n), BF16)
        o_spec = pl.BlockSpec((tm, tn), lambda i, j, kk: (i, j))
    return _mm(name, a, b, TN, (m // tm, n // tn, 1),
               pl.BlockSpec((t, tm), lambda i, j, kk: (0, i)),
               pl.BlockSpec((t, tn), lambda i, j, kk: (0, j)),
               out_shape, o_spec)


def _in_proj_xgrad(dh_parts, win_st, dr1, tm, comm=None):
    t = dr1.shape[0]
    ni = t // tm

    def body(a0, a1, a2, a3, b_ref, add_ref, o_ref, acc):
        j = pl.program_id(1)
        for jj, a_ref in enumerate((a0, a1, a2, a3)):
            @pl.when(j == jj)
            def _(jj=jj, a_ref=a_ref):
                prod = _dot(a_ref[...], b_ref[...], NT)
                if jj == 0:
                    acc[...] = prod + ALPHA * add_ref[...]
                elif jj < N_CHIP - 1:
                    acc[...] += prod
                else:
                    o_ref[...] = acc[...] + prod

    a_spec = pl.BlockSpec((tm, 2 * D_MODEL), lambda i, j: (i, 0))
    tile = pl.BlockSpec((tm, D_MODEL), lambda i, j: (i, 0))
    kw = dict(name="in_proj_xgrad", out_shape=(jax.ShapeDtypeStruct((t, D_MODEL), F32),),
              grid=(ni, N_CHIP),
              in_specs=[a_spec] * 4 + [pl.BlockSpec((None, D_MODEL, 2 * D_MODEL), lambda i, j: (j, 0, 0)),
                                       tile],
              out_specs=(tile,), scratch=[pltpu.VMEM((tm, D_MODEL), F32)],
              sem=("arbitrary", "arbitrary"))
    args = list(dh_parts) + [win_st, dr1]
    if comm is None:
        return _pc(body, **kw)(*args)[0], ()
    res = _hosted_call(body, comm,
                       lambda: (pl.program_id(0) == 0) & (pl.program_id(1) == 0),
                       lambda: (pl.program_id(0) == ni - 1) & (pl.program_id(1) == N_CHIP - 1),
                       args=args, **kw)
    return res[0], res[1:]


def _sgu_mixed(v, wm_ref, bsb_ref, gv, bv):
    gl, dgl = _gelu_and_grad(v)
    vh, rstd = _ln_stats(gl)
    vn = vh * gv + bv
    mixed = []
    for g in range(N_GROUP):
        sl = slice(g * 128, (g + 1) * 128)
        mixed.append(_dot(wm_ref[g], vn[:, sl]) + bsb_ref[g])
    return dgl, vh, rstd, vn, mixed


def _sgu_fwd(h, wm, bsb, gv, bv, comm=None):
    t = h.shape[0]

    def body(u_ref, v_ref, wm_ref, bsb_ref, gv_ref, bv_ref, ya_ref):
        u = u_ref[...].astype(F32)
        _, _, _, _, mixed = _sgu_mixed(v_ref[...].astype(F32), wm_ref, bsb_ref, gv_ref[...], bv_ref[...])
        gu = _gelu(u)
        for g in range(N_GROUP):
            sl = slice(g * 128, (g + 1) * 128)
            ya_ref[:, sl] = (gu[:, sl] * mixed[g]).astype(BF16)

    full3 = pl.BlockSpec((N_GROUP, 128, 128), lambda i: (0, 0, 0))
    vec = pl.BlockSpec((1, D_MODEL), lambda i: (0, 0))
    (ya,), extra = _grid1_call(
        body, comm, t // SGU_BLOCK, name="sgu_fwd",
        out_shape=(jax.ShapeDtypeStruct((t, D_MODEL), BF16),),
        in_specs=[pl.BlockSpec((SGU_BLOCK, D_MODEL), lambda i: (i, 0)),
                  pl.BlockSpec((SGU_BLOCK, D_MODEL), lambda i: (i, 1)),
                  full3, full3, vec, vec],
        out_specs=(pl.BlockSpec((SGU_BLOCK, D_MODEL), lambda i: (i, 0)),),
        scratch=[], args=(h, h, wm, bsb, gv, bv))
    return ya, extra


def _sgu_bwd(h, dya, wm, wmt, bsb, gv, bv, maskf, comm=None):
    t = h.shape[0]
    nb = t // SGU_BLOCK

    def body(u_ref, v_ref, dya_ref, wm_ref, wmt_ref, bsb_ref, gv_ref, bv_ref, mask_ref,
             dh_ref, dws_ref, dbs_ref, dgv_ref, dbv_ref, dmix_acc):
        i = pl.program_id(0)

        @pl.when(i == 0)
        def _():
            dws_ref[...] = jnp.zeros_like(dws_ref)
            dgv_ref[...] = jnp.zeros_like(dgv_ref)
            dbv_ref[...] = jnp.zeros_like(dbv_ref)
            dmix_acc[...] = jnp.zeros_like(dmix_acc)

        u = u_ref[...].astype(F32)
        gvv = gv_ref[...]
        dgl_v, vh, rstd, vn, mixed = _sgu_mixed(v_ref[...].astype(F32), wm_ref, bsb_ref, gvv, bv_ref[...])
        gu, dgl_u = _gelu_and_grad(u)
        dya_v = dya_ref[...]
        dvn_parts = []
        for g in range(N_GROUP):
            sl = slice(g * 128, (g + 1) * 128)
            d_y = dya_v[:, sl]
            dh_ref[:, sl] = (d_y * mixed[g] * dgl_u[:, sl]).astype(BF16)
            d_mixed = d_y * gu[:, sl]
            dmix_acc[g] += d_mixed
            dws_ref[g] += _dot(d_mixed, vn[:, sl], NT) * mask_ref[...]
            dvn_parts.append(_dot(wmt_ref[g], d_mixed))
        dvn = jnp.concatenate(dvn_parts, axis=1)
        dgv_ref[...] += _colsum8(dvn * vh)
        dbv_ref[...] += _colsum8(dvn)
        d_gl = _ln_bwd(dvn * gvv, vh, rstd)
        dh_ref[:, D_MODEL:] = (d_gl * dgl_v).astype(BF16)

        @pl.when(i == nb - 1)
        def _():
            rowid = lax.broadcasted_iota(jnp.int32, (8, 128), 0)
            ones = jnp.ones((8, 128), F32)
            acc = jnp.zeros((8, 128), F32)
            for g in range(N_GROUP):
                rs = _dot32(ones, dmix_acc[g], NT)
                acc = jnp.where(rowid == g, rs, acc)
            dbs_ref[...] = acc

    full3 = pl.BlockSpec((N_GROUP, 128, 128), lambda i: (0, 0, 0))
    vec = pl.BlockSpec((1, D_MODEL), lambda i: (0, 0))
    acc8 = pl.BlockSpec((8, D_MODEL), lambda i: (0, 0))
    return _grid1_call(
        body, comm, nb, name="sgu_bwd",
        out_shape=(jax.ShapeDtypeStruct((t, 2 * D_MODEL), BF16),
                   jax.ShapeDtypeStruct((N_GROUP, 128, 128), F32),
                   jax.ShapeDtypeStruct((8, 128), F32),
                   jax.ShapeDtypeStruct((8, D_MODEL), F32),
                   jax.ShapeDtypeStruct((8, D_MODEL), F32)),
        in_specs=[pl.BlockSpec((SGU_BLOCK, D_MODEL), lambda i: (i, 0)),
                  pl.BlockSpec((SGU_BLOCK, D_MODEL), lambda i: (i, 1)),
                  pl.BlockSpec((SGU_BLOCK, D_MODEL), lambda i: (i, 0)),
                  full3, full3, full3, vec, vec,
                  pl.BlockSpec((128, 128), lambda i: (0, 0))],
        out_specs=(pl.BlockSpec((SGU_BLOCK, 2 * D_MODEL), lambda i: (i, 0)),
                   full3, pl.BlockSpec((8, 128), lambda i: (0, 0)), acc8, acc8),
        scratch=[pltpu.VMEM((N_GROUP, 128, 128), F32)],
        args=(h, h, dya, wm, wmt, bsb, gv, bv, maskf))


def _tri_masks():
    row = lax.broadcasted_iota(jnp.int32, (CHUNK, CHUNK), 0)
    col = lax.broadcasted_iota(jnp.int32, (CHUNK, CHUNK), 1)
    return col <= row, col >= row


def _heads(v):
    return [v[:, hd * HEAD_DIM:(hd + 1) * HEAD_DIM] for hd in range(N_HEAD)]


def _tri_cumsum(tri_bf, v):
    hi = v.astype(BF16)
    r = v - hi.astype(F32)
    mid = r.astype(BF16)
    lo = (r - mid.astype(F32)).astype(BF16)
    return _dot(tri_bf, hi) + _dot(tri_bf, mid) + _dot(tri_bf, lo)


def _hgrn_chunk(q, fp, ii, lb, st_heads, causal):
    sg = _sig(fp)
    f = lb + (1.0 - lb) * sg
    k = 1.0 - f
    c = _tri_cumsum(causal.astype(BF16), jnp.log(f))
    ec = jnp.exp(c)
    en = jnp.exp(-c)
    sq = _sig(q)
    qt = q * sq * ec
    kt = k * en
    ecl = jnp.exp(c[CHUNK - 1:CHUNK, :])
    kk = kt * ecl
    qtb, ktb, iib, kkb = qt.astype(BF16), kt.astype(BF16), ii.astype(BF16), kk.astype(BF16)
    attn, o = [], []
    for hd, (qh, kh, ih) in enumerate(zip(_heads(qtb), _heads(ktb), _heads(iib))):
        a = jnp.where(causal, _dot(qh, kh, NT), 0.0).astype(BF16)
        attn.append(a)
        o.append(_dot(a, ih) + _dot(qh, st_heads[hd], NT))
    return dict(sg=sg, f=f, k=k, ec=ec, en=en, sq=sq, ecl=ecl, kk=kk, qtb=qtb, ktb=ktb, iib=iib,
                kkb=kkb, attn=attn, o=o)


def _rms_heads(o_heads):
    rinv = [lax.rsqrt(jnp.mean(o * o, axis=-1, keepdims=True) + RMS_EPS) for o in o_heads]
    return rinv, jnp.concatenate([o * r for o, r in zip(o_heads, rinv)], axis=1)


HG_CHUNKS = 4
HG_ROWS = HG_CHUNKS * CHUNK


def _hgrn_fwd(h, logits, gn, comm=None):
    t = h.shape[0]
    nb = t // HG_ROWS

    def body(q_ref, f_ref, i_ref, og_ref, lg_ref, gn_ref, yb_ref, st_ref, state):
        @pl.when(pl.program_id(0) == 0)
        def _():
            state[...] = jnp.zeros_like(state)

        causal, _ = _tri_masks()
        lb = _sig(lg_ref[0:1, :] - lg_ref[1:2, :])
        gnv = gn_ref[...]
        st = [state[hd] for hd in range(N_HEAD)]
        for cc in range(HG_CHUNKS):
            rows = slice(cc * CHUNK, (cc + 1) * CHUNK)
            og = og_ref[rows, :].astype(F32)
            r = _hgrn_chunk(q_ref[rows, :].astype(F32), f_ref[rows, :].astype(F32),
                            i_ref[rows, :].astype(F32), lb, [s.astype(BF16) for s in st], causal)
            _, on = _rms_heads(r["o"])
            yb_ref[rows, :] = (on * gnv * (og * _sig(og))).astype(BF16)
            for hd in range(N_HEAD):
                st_ref[cc, hd] = st[hd]
            st = [s * e + _dot(ih, kh, TN)
                  for s, e, ih, kh in zip(st, _heads(r["ecl"]), _heads(r["iib"]), _heads(r["kkb"]))]
        for hd in range(N_HEAD):
            state[hd] = st[hd]

    def col(k):
        return pl.BlockSpec((HG_ROWS, D_MODEL), lambda ci: (ci, k))

    return _grid1_call(body, comm, nb, name="hgrn_fwd",
                       out_shape=(jax.ShapeDtypeStruct((t, D_MODEL), BF16),
                                  jax.ShapeDtypeStruct((t // CHUNK, N_HEAD, HEAD_DIM, HEAD_DIM), F32)),
                       in_specs=[col(2), col(3), col(4), col(5),
                                 pl.BlockSpec((2, D_MODEL), lambda ci: (0, 0)),
                                 pl.BlockSpec((1, D_MODEL), lambda ci: (0, 0))],
                       out_specs=(pl.BlockSpec((HG_ROWS, D_MODEL), lambda ci: (ci, 0)),
                                  pl.BlockSpec((HG_CHUNKS, N_HEAD, HEAD_DIM, HEAD_DIM),
                                               lambda ci: (ci, 0, 0, 0))),
                       scratch=[pltpu.VMEM((N_HEAD, HEAD_DIM, HEAD_DIM), F32)],
                       args=(h, h, h, h, logits, gn))


def _hgrn_chunk_bwd(q, fp, ii, og, dy, gnv, lb, st, dsn, causal, anti):
    stb = [s.astype(BF16) for s in st]
    dsnb = [s.astype(BF16) for s in dsn]
    r = _hgrn_chunk(q, fp, ii, lb, stb, causal)
    rinv, on = _rms_heads(r["o"])
    so = _sig(og)
    sil = og * so
    d_og = dy * on * gnv * (so * (1.0 + og * (1.0 - so)))
    d_on = dy * gnv * sil
    d_ob = jnp.concatenate(
        [ri * (dn - oh * jnp.mean(dn * oh, axis=-1, keepdims=True))
         for ri, dn, oh in zip(rinv, _heads(d_on), _heads(on))], axis=1).astype(BF16)
    d_i, d_qt, d_kt, d_kk, d_st, st_dsn = [], [], [], [], [], []
    ecl = _heads(r["ecl"])
    for hd, (dh, qh, kh, ih, kkh) in enumerate(zip(_heads(d_ob), _heads(r["qtb"]), _heads(r["ktb"]),
                                                   _heads(r["iib"]), _heads(r["kkb"]))):
        d_attn = jnp.where(causal, _dot(dh, ih, NT), 0.0).astype(BF16)
        d_i.append(_dot(r["attn"][hd], dh, TN) + _dot(kkh, dsnb[hd], NT))
        d_qt.append(_dot(d_attn, kh) + _dot(dh, stb[hd]))
        d_kt.append(_dot(d_attn, qh, TN))
        d_kk.append(_dot(ih, dsnb[hd]))
        d_st.append(_dot(dh, qh, TN) + dsn[hd] * ecl[hd])
        st_dsn.append(jnp.sum(st[hd] * dsn[hd], axis=0, keepdims=True))
    d_qt = jnp.concatenate(d_qt, axis=1)
    d_kt = jnp.concatenate(d_kt, axis=1)
    d_kk = jnp.concatenate(d_kk, axis=1)
    kk = r["kk"]
    d_cl = r["ecl"] * jnp.concatenate(st_dsn, axis=1) + jnp.sum(kk * d_kk, axis=0, keepdims=True)
    d_k = (d_kk * r["ecl"] + d_kt) * r["en"]
    d_c = d_qt * r["qtb"].astype(F32) - d_kt * r["ktb"].astype(F32) - d_kk * kk
    rowid = lax.broadcasted_iota(jnp.int32, (CHUNK, D_MODEL), 0)
    d_c = d_c + jnp.where(rowid == CHUNK - 1, d_cl, 0.0)
    d_lf = _tri_cumsum(anti.astype(BF16), d_c)
    d_f = d_lf / r["f"] - d_k
    sg, sq = r["sg"], r["sq"]
    d_q = d_qt * r["ec"] * (sq * (1.0 + q * (1.0 - sq)))
    d_fp = d_f * (1.0 - lb) * sg * (1.0 - sg)
    return (d_q, d_fp, jnp.concatenate(d_i, axis=1), d_og, d_st,
            _colsum8(dy * on * sil), _colsum8(d_f * (1.0 - sg)))


def _hgrn_bwd(h, dyb, st_all, logits, gn, comm=None):
    t = h.shape[0]
    nb = t // HG_ROWS

    def body(q_ref, f_ref, i_ref, og_ref, dyb_ref, st_ref, lg_ref, gn_ref,
             dh1_ref, dh2_ref, dlb_ref, dgn_ref, dstate):
        @pl.when(pl.program_id(0) == 0)
        def _():
            dstate[...] = jnp.zeros_like(dstate)
            dlb_ref[...] = jnp.zeros_like(dlb_ref)
            dgn_ref[...] = jnp.zeros_like(dgn_ref)

        causal, anti = _tri_masks()
        lb = _sig(lg_ref[0:1, :] - lg_ref[1:2, :])
        gnv = gn_ref[...]
        dsn = [dstate[hd] for hd in range(N_HEAD)]
        dgn_acc = jnp.zeros((8, D_MODEL), F32)
        dlb_acc = jnp.zeros((8, D_MODEL), F32)
        for cc in reversed(range(HG_CHUNKS)):
            rows = slice(cc * CHUNK, (cc + 1) * CHUNK)
            d_q, d_fp, d_i, d_og, dsn, dgn_c, dlb_c = _hgrn_chunk_bwd(
                q_ref[rows, :].astype(F32), f_ref[rows, :].astype(F32), i_ref[rows, :].astype(F32),
                og_ref[rows, :].astype(F32), dyb_ref[rows, :], gnv, lb,
                [st_ref[cc, hd] for hd in range(N_HEAD)], dsn, causal, anti)
            dgn_acc = dgn_acc + dgn_c
            dlb_acc = dlb_acc + dlb_c
            dh1_ref[rows, :D_MODEL] = d_q.astype(BF16)
            dh1_ref[rows, D_MODEL:] = d_fp.astype(BF16)
            dh2_ref[rows, :D_MODEL] = d_i.astype(BF16)
            dh2_ref[rows, D_MODEL:] = d_og.astype(BF16)
        dgn_ref[...] += dgn_acc
        dlb_ref[...] += dlb_acc
        for hd in range(N_HEAD):
            dstate[hd] = dsn[hd]

    def col(k):
        return pl.BlockSpec((HG_ROWS, D_MODEL), lambda ci: (nb - 1 - ci, k))

    acc8 = pl.BlockSpec((8, D_MODEL), lambda ci: (0, 0))
    pair = pl.BlockSpec((HG_ROWS, 2 * D_MODEL), lambda ci: (nb - 1 - ci, 0))
    return _grid1_call(body, comm, nb, name="hgrn_bwd",
                       out_shape=(jax.ShapeDtypeStruct((t, 2 * D_MODEL), BF16),
                                  jax.ShapeDtypeStruct((t, 2 * D_MODEL), BF16),
                                  jax.ShapeDtypeStruct((8, D_MODEL), F32),
                                  jax.ShapeDtypeStruct((8, D_MODEL), F32)),
                       in_specs=[col(2), col(3), col(4), col(5),
                                 pl.BlockSpec((HG_ROWS, D_MODEL), lambda ci: (nb - 1 - ci, 0)),
                                 pl.BlockSpec((HG_CHUNKS, N_HEAD, HEAD_DIM, HEAD_DIM),
                                              lambda ci: (nb - 1 - ci, 0, 0, 0)),
                                 pl.BlockSpec((2, D_MODEL), lambda ci: (0, 0)),
                                 pl.BlockSpec((1, D_MODEL), lambda ci: (0, 0))],
                       out_specs=(pair, pair, acc8, acc8),
                       scratch=[pltpu.VMEM((N_HEAD, HEAD_DIM, HEAD_DIM), F32)],
                       args=(h, h, h, h, dyb, st_all, logits, gn))


def _mix_fwd(ya, yb, h, x, wb0, wb1, wo, g1, b1, tm, comm=None):
    t = x.shape[0]

    def body(ya_ref, yb_ref, ga_ref, gb_ref, x_ref, wb0_ref, wb1_ref, wo_ref, g1_ref, b1_ref,
             r1_ref, a_ref, b_ref, m_ref, x1_ref):
        a = _dot(ya_ref[...], wb0_ref[...])
        b = _dot(yb_ref[...], wb1_ref[...])
        m = _sig(ga_ref[...].astype(F32)) * a + _sig(gb_ref[...].astype(F32)) * b
        r1 = ALPHA * x_ref[...] + _dot(m, wo_ref[...])
        xh, _ = _ln_stats(r1)
        r1_ref[...] = r1
        a_ref[...] = a
        b_ref[...] = b
        m_ref[...] = m.astype(BF16)
        x1_ref[...] = (xh * g1_ref[...] + b1_ref[...]).astype(BF16)

    tile = pl.BlockSpec((tm, D_MODEL), lambda i: (i, 0))
    wsp = pl.BlockSpec((D_MODEL, D_MODEL), lambda i: (0, 0))
    vec = pl.BlockSpec((1, D_MODEL), lambda i: (0, 0))
    f32o = jax.ShapeDtypeStruct((t, D_MODEL), F32)
    bfo = jax.ShapeDtypeStruct((t, D_MODEL), BF16)
    return _grid1_call(body, comm, t // tm, name="mix_fwd", out_shape=(f32o, f32o, f32o, bfo, bfo),
                       in_specs=[tile, tile,
                                 pl.BlockSpec((tm, D_MODEL), lambda i: (i, 6)),
                                 pl.BlockSpec((tm, D_MODEL), lambda i: (i, 7)),
                                 tile, wsp, wsp, wsp, vec, vec],
                       out_specs=(tile, tile, tile, tile, tile),
                       scratch=[], args=(ya, yb, h, h, x, wb0, wb1, wo, g1, b1))


def _mix_bwd(dr1, h, a, b, wo, wb0, wb1, tm):
    t = dr1.shape[0]

    def body(dr1_ref, ga_ref, gb_ref, a_ref, b_ref, wo_ref, wb0_ref, wb1_ref,
             da_ref, db_ref, dh3_ref, dya_ref, dyb_ref):
        d_m = _dot(dr1_ref[...], wo_ref[...], NT)
        sa = _sig(ga_ref[...].astype(F32))
        sb = _sig(gb_ref[...].astype(F32))
        d_a = (d_m * sa).astype(BF16)
        d_b = (d_m * sb).astype(BF16)
        da_ref[...] = d_a
        db_ref[...] = d_b
        dh3_ref[:, :D_MODEL] = (d_m * a_ref[...] * sa * (1.0 - sa)).astype(BF16)
        dh3_ref[:, D_MODEL:] = (d_m * b_ref[...] * sb * (1.0 - sb)).astype(BF16)
        dya_ref[...] = _dot(d_a, wb0_ref[...], NT)
        dyb_ref[...] = _dot(d_b, wb1_ref[...], NT)

    tile = pl.BlockSpec((tm, D_MODEL), lambda i: (i, 0))
    wsp = pl.BlockSpec((D_MODEL, D_MODEL), lambda i: (0, 0))
    f32o = jax.ShapeDtypeStruct((t, D_MODEL), F32)
    bfo = jax.ShapeDtypeStruct((t, D_MODEL), BF16)
    return _pc(body, name="mix_bwd",
               out_shape=(bfo, bfo, jax.ShapeDtypeStruct((t, 2 * D_MODEL), BF16), f32o, f32o),
               grid=(t // tm,),
               in_specs=[tile,
                         pl.BlockSpec((tm, D_MODEL), lambda i: (i, 6)),
                         pl.BlockSpec((tm, D_MODEL), lambda i: (i, 7)),
                         tile, tile, wsp, wsp, wsp],
               out_specs=(tile, tile, pl.BlockSpec((tm, 2 * D_MODEL), lambda i: (i, 0)),
                          tile, tile),
               sem=("parallel",))(dr1, h, h, a, b, wo, wb0, wb1)


FF_TILE = 1408
FF_NJ = D_FF // FF_TILE


def _shift_down(v, k):
    return pltpu.roll(v, k, 0)


def _shift_up(v, k):
    return pltpu.roll(v, v.shape[0] - k, 0)


def _conv_gate(ext, cw_ref, cb_ref):
    return (cw_ref[0:1, :] * _shift_down(ext, 2) + cw_ref[1:2, :] * _shift_down(ext, 1)
            + cw_ref[2:3, :] * ext + cb_ref[...])


HALO = 16


def _ffn_up_act(x1b, wup_st, convw, convb, tm, comm):
    t = x1b.shape[0]
    ni = t // tm
    nth = tm // HALO

    def body(x_ref, xp_ref, wg_ref, wv_ref, cw_ref, cb_ref, h2_ref, act_ref):
        i = pl.program_id(0)
        wg = wg_ref[...]
        gate = _dot(x_ref[...], wg).astype(BF16)
        val = _dot(x_ref[...], wv_ref[...]).astype(BF16)
        prev = (_dot(xp_ref[...], wg) * (i > 0).astype(F32)).astype(BF16)
        h2_ref[0] = gate
        h2_ref[1] = val
        ext = jnp.concatenate([prev.astype(F32), gate.astype(F32)], axis=0)
        gc = _conv_gate(ext, cw_ref, cb_ref)[HALO:, :]
        act_ref[...] = (_gelu(gc) * val.astype(F32)).astype(BF16)

    res = _hosted_call(
        body, comm,
        lambda: (pl.program_id(0) == 0) & (pl.program_id(1) == 0),
        lambda: (pl.program_id(0) == ni - 1) & (pl.program_id(1) == FF_NJ - 1),
        name="ffn_up",
        out_shape=(jax.ShapeDtypeStruct((2, t, D_FF), BF16), jax.ShapeDtypeStruct((t, D_FF), BF16)),
        grid=(ni, FF_NJ),
        in_specs=[pl.BlockSpec((tm, D_MODEL), lambda i, j: (i, 0)),
                  pl.BlockSpec((HALO, D_MODEL), lambda i, j: (jnp.maximum(i * nth - 1, 0), 0)),
                  pl.BlockSpec((None, D_MODEL, FF_TILE), lambda i, j: (j, 0, 0)),
                  pl.BlockSpec((None, D_MODEL, FF_TILE), lambda i, j: (j + FF_NJ, 0, 0)),
                  pl.BlockSpec((3, FF_TILE), lambda i, j: (0, j)),
                  pl.BlockSpec((1, FF_TILE), lambda i, j: (0, j))],
        out_specs=(pl.BlockSpec((2, tm, FF_TILE), lambda i, j: (0, i, j)),
                   pl.BlockSpec((tm, FF_TILE), lambda i, j: (i, j))),
        scratch=[], sem=("arbitrary", "arbitrary"),
        args=(x1b, x1b, wup_st, wup_st, convw, convb))
    return res[0], res[1], res[2:]


def _ffn_act_bwd(h2, dr2, wd, convw, convb, tm):
    t = h2.shape[1]
    nth = tm // HALO
    ni = t // tm
    last_halo = t // HALO - 1
    main_rows = slice(HALO, HALO + tm)

    def body(g_ref, gp_ref, gn_ref, v_ref, vn_ref, dr2_ref, dr2n_ref, wd_ref, cw_ref, cb_ref,
             dh2_ref, dcw_ref, dcb_ref):
        i = pl.program_id(1)

        @pl.when(i == 0)
        def _():
            dcw_ref[...] = jnp.zeros_like(dcw_ref)
            dcb_ref[...] = jnp.zeros_like(dcb_ref)

        zeros = jnp.zeros((HALO, FF_TILE), F32)
        da = _dot(dr2_ref[...], wd_ref[...], NT)
        prev = gp_ref[...].astype(F32) * (i > 0).astype(F32)
        ext = jnp.concatenate([prev, g_ref[...].astype(F32), gn_ref[...].astype(F32)], axis=0)
        vext = jnp.concatenate([zeros, v_ref[...].astype(F32), vn_ref[...].astype(F32)], axis=0)
        dnext = _dot(dr2n_ref[...], wd_ref[...], NT) * (i < ni - 1).astype(F32)
        dext = jnp.concatenate([zeros, da, dnext], axis=0)
        g2 = _shift_down(ext, 2)
        g1 = _shift_down(ext, 1)
        gc = cw_ref[0:1, :] * g2 + cw_ref[1:2, :] * g1 + cw_ref[2:3, :] * ext + cb_ref[...]
        gl, dgl = _gelu_and_grad(gc)
        d_gc = dext * vext * dgl
        d_gate = (cw_ref[2:3, :] * d_gc + cw_ref[1:2, :] * _shift_up(d_gc, 1)
                  + cw_ref[0:1, :] * _shift_up(d_gc, 2))
        dh2_ref[0] = d_gate[main_rows, :].astype(BF16)
        dh2_ref[1] = (da * gl[main_rows, :]).astype(BF16)
        dm = d_gc[main_rows, :]
        s0 = jnp.sum(dm * g2[main_rows, :], axis=0, keepdims=True)
        s1 = jnp.sum(dm * g1[main_rows, :], axis=0, keepdims=True)
        s2 = jnp.sum(dm * ext[main_rows, :], axis=0, keepdims=True)
        rowid = lax.broadcasted_iota(jnp.int32, (8, FF_TILE), 0)
        dcw_ref[...] += jnp.where(rowid == 0, s0, jnp.where(rowid == 1, s1,
                                                            jnp.where(rowid == 2, s2, 0.0)))
        dcb_ref[...] += _colsum8(dm)

    def prev8(part):
        return pl.BlockSpec((None, HALO, FF_TILE), lambda j, i: (part, jnp.maximum(i * nth - 1, 0), j))

    def next8(part):
        return pl.BlockSpec((None, HALO, FF_TILE),
                            lambda j, i: (part, jnp.minimum((i + 1) * nth, last_halo), j))

    def main(part):
        return pl.BlockSpec((None, tm, FF_TILE), lambda j, i: (part, i, j))

    acc = pl.BlockSpec((8, FF_TILE), lambda j, i: (0, j))
    return _pc(body, name="ffn_act_bwd",
               out_shape=(jax.ShapeDtypeStruct((2, t, D_FF), BF16),
                          jax.ShapeDtypeStruct((8, D_FF), F32),
                          jax.ShapeDtypeStruct((8, D_FF), F32)),
               grid=(FF_NJ, ni),
               in_specs=[main(0), prev8(0), next8(0), main(1), next8(1),
                         pl.BlockSpec((tm, D_MODEL), lambda j, i: (i, 0)),
                         pl.BlockSpec((HALO, D_MODEL),
                                      lambda j, i: (jnp.minimum((i + 1) * nth, last_halo), 0)),
                         pl.BlockSpec((FF_TILE, D_MODEL), lambda j, i: (j, 0)),
                         pl.BlockSpec((3, FF_TILE), lambda j, i: (0, j)),
                         pl.BlockSpec((1, FF_TILE), lambda j, i: (0, j))],
               out_specs=(pl.BlockSpec((2, tm, FF_TILE), lambda j, i: (0, i, j)), acc, acc),
               sem=("parallel", "arbitrary"))(h2, h2, h2, h2, h2, dr2, dr2, wd, convw, convb)


def _out_fwd_bwd(act, x1b, r1, p2, tgt, wd, wpg, wpp, g1, b1, g2, b2, tm):
    t = r1.shape[0]

    def body(act_ref, x1b_ref, r1_ref, p_ref, tgt_ref, wd_ref, wpg_ref, wpp_ref,
             g1_ref, b1_ref, g2_ref, b2_ref,
             dr2_ref, dpg_ref, dpp_ref, loss_ref, dg2_ref, db2_ref):
        i = pl.program_id(0)

        @pl.when(i == 0)
        def _():
            loss_ref[...] = jnp.zeros_like(loss_ref)
            dg2_ref[...] = jnp.zeros_like(dg2_ref)
            db2_ref[...] = jnp.zeros_like(db2_ref)

        ffn = _dot(act_ref[...], wd_ref[...])
        pg = _dot(x1b_ref[...], wpg_ref[...])
        pp = _dot(p_ref[...], wpp_ref[...])
        s = _sig(pg)
        xh1, _ = _ln_stats(r1_ref[...])
        x1 = xh1 * g1_ref[...] + b1_ref[...]
        r2 = ALPHA * x1 + ffn + s * pp
        xh2, rstd2 = _ln_stats(r2)
        g2v = g2_ref[...]
        diff = xh2 * g2v + b2_ref[...] - tgt_ref[...]
        part = jnp.sum(jnp.sum(diff * diff, axis=1, keepdims=True), axis=0, keepdims=True)
        loss_ref[...] += jnp.broadcast_to(part * (0.5 / D_MODEL), loss_ref.shape)
        dy = diff * (1.0 / D_MODEL)
        dg2_ref[...] += _colsum8(dy * xh2)
        db2_ref[...] += _colsum8(dy)
        dr2 = _ln_bwd(dy * g2v, xh2, rstd2)
        dr2_ref[...] = dr2
        dpg_ref[...] = (dr2 * pp * s * (1.0 - s)).astype(BF16)
        dpp_ref[...] = (dr2 * s).astype(BF16)

    tile = pl.BlockSpec((tm, D_MODEL), lambda i: (i, 0))
    vec = pl.BlockSpec((1, D_MODEL), lambda i: (0, 0))
    acc8 = pl.BlockSpec((8, D_MODEL), lambda i: (0, 0))
    acc_shape = jax.ShapeDtypeStruct((8, D_MODEL), F32)
    return _pc(body, name="out_fwd_bwd",
               out_shape=(jax.ShapeDtypeStruct((t, D_MODEL), F32),
                          jax.ShapeDtypeStruct((t, D_MODEL), BF16),
                          jax.ShapeDtypeStruct((t, D_MODEL), BF16),
                          acc_shape, acc_shape, acc_shape),
               grid=(t // tm,),
               in_specs=[pl.BlockSpec((tm, D_FF), lambda i: (i, 0)), tile, tile,
                         pl.BlockSpec((tm, PLE_DIM), lambda i: (i, 0)), tile,
                         pl.BlockSpec((D_FF, D_MODEL), lambda i: (0, 0)),
                         pl.BlockSpec((D_MODEL, D_MODEL), lambda i: (0, 0)),
                         pl.BlockSpec((PLE_DIM, D_MODEL), lambda i: (0, 0)),
                         vec, vec, vec, vec],
               out_specs=(tile, tile, tile, acc8, acc8, acc8),
               sem=("arbitrary",))(act, x1b, r1, p2, tgt, wd, wpg, wpp, g1, b1, g2, b2)


def _ffn_bwd(h2, dr2, wd, wup_st, dpg, wpg, r1, g1, convw, convb, tm):
    t = r1.shape[0]
    ni = t // tm
    nth = tm // HALO
    last_halo = t // HALO - 1
    main_rows = slice(HALO, HALO + tm)

    def body(g_ref, gp_ref, gn_ref, v_ref, vn_ref, dr2_ref, dr2n_ref, wd_ref, wug_ref, wuv_ref,
             cw_ref, cb_ref, dpg_ref, wpg_ref, r1_ref, g1_ref,
             dh2_ref, dr1_ref, dcw_ref, dcb_ref, dg1_ref, db1_ref, acc):
        i = pl.program_id(0)
        j = pl.program_id(1)

        @pl.when((i == 0) & (j == 0))
        def _():
            dcw_ref[...] = jnp.zeros_like(dcw_ref)
            dcb_ref[...] = jnp.zeros_like(dcb_ref)
            dg1_ref[...] = jnp.zeros_like(dg1_ref)
            db1_ref[...] = jnp.zeros_like(db1_ref)

        zeros = jnp.zeros((HALO, FF_TILE), F32)
        da = _dot(dr2_ref[...], wd_ref[...], NT)
        prev = gp_ref[...].astype(F32) * (i > 0).astype(F32)
        ext = jnp.concatenate([prev, g_ref[...].astype(F32), gn_ref[...].astype(F32)], axis=0)
        vext = jnp.concatenate([zeros, v_ref[...].astype(F32), vn_ref[...].astype(F32)], axis=0)
        dnext = _dot(dr2n_ref[...], wd_ref[...], NT) * (i < ni - 1).astype(F32)
        dext = jnp.concatenate([zeros, da, dnext], axis=0)
        g2 = _shift_down(ext, 2)
        g1s = _shift_down(ext, 1)
        gc = cw_ref[0:1, :] * g2 + cw_ref[1:2, :] * g1s + cw_ref[2:3, :] * ext + cb_ref[...]
        gl, dgl = _gelu_and_grad(gc)
        d_gc = dext * vext * dgl
        d_gate = (cw_ref[2:3, :] * d_gc + cw_ref[1:2, :] * _shift_up(d_gc, 1)
                  + cw_ref[0:1, :] * _shift_up(d_gc, 2))[main_rows, :].astype(BF16)
        d_val = (da * gl[main_rows, :]).astype(BF16)
        dh2_ref[0] = d_gate
        dh2_ref[1] = d_val
        dm = d_gc[main_rows, :]
        s0 = jnp.sum(dm * g2[main_rows, :], axis=0, keepdims=True)
        s1 = jnp.sum(dm * g1s[main_rows, :], axis=0, keepdims=True)
        s2 = jnp.sum(dm * ext[main_rows, :], axis=0, keepdims=True)
        rowid = lax.broadcasted_iota(jnp.int32, (8, FF_TILE), 0)
        dcw_part = jnp.where(rowid == 0, s0, jnp.where(rowid == 1, s1, jnp.where(rowid == 2, s2, 0.0)))
        dcb_part = _colsum8(dm)
        for jj in range(FF_NJ):
            @pl.when(j == jj)
            def _(jj=jj):
                cols = slice(jj * FF_TILE, (jj + 1) * FF_TILE)
                dcw_ref[:, cols] += dcw_part
                dcb_ref[:, cols] += dcb_part

        prod = _dot(d_gate, wug_ref[...], NT) + _dot(d_val, wuv_ref[...], NT)

        @pl.when(j == 0)
        def _():
            acc[...] = prod

        @pl.when(j > 0)
        def _():
            acc[...] += prod

        @pl.when(j == FF_NJ - 1)
        def _():
            d_x1 = acc[...] + _dot(dpg_ref[...], wpg_ref[...], NT) + ALPHA * dr2_ref[...]
            xh, rstd = _ln_stats(r1_ref[...])
            dg1_ref[...] += _colsum8(d_x1 * xh)
            db1_ref[...] += _colsum8(d_x1)
            dr1_ref[...] = _ln_bwd(d_x1 * g1_ref[...], xh, rstd)

    def h2_main(part):
        return pl.BlockSpec((None, tm, FF_TILE), lambda i, j: (part, i, j))

    def h2_prev(part):
        return pl.BlockSpec((None, HALO, FF_TILE), lambda i, j: (part, jnp.maximum(i * nth - 1, 0), j))

    def h2_next(part):
        return pl.BlockSpec((None, HALO, FF_TILE),
                            lambda i, j: (part, jnp.minimum((i + 1) * nth, last_halo), j))

    tile = pl.BlockSpec((tm, D_MODEL), lambda i, j: (i, 0))
    acc8 = pl.BlockSpec((8, D_MODEL), lambda i, j: (0, 0))
    accff = pl.BlockSpec((8, D_FF), lambda i, j: (0, 0))
    acc_shape = jax.ShapeDtypeStruct((8, D_MODEL), F32)
    accff_shape = jax.ShapeDtypeStruct((8, D_FF), F32)
    return _pc(body, name="ffn_bwd",
               out_shape=(jax.ShapeDtypeStruct((2, t, D_FF), BF16),
                          jax.ShapeDtypeStruct((t, D_MODEL), F32),
                          accff_shape, accff_shape, acc_shape, acc_shape),
               grid=(ni, FF_NJ),
               in_specs=[h2_main(0), h2_prev(0), h2_next(0), h2_main(1), h2_next(1),
                         tile,
                         pl.BlockSpec((HALO, D_MODEL), lambda i, j: (jnp.minimum((i + 1) * nth, last_halo), 0)),
                         pl.BlockSpec((FF_TILE, D_MODEL), lambda i, j: (j, 0)),
                         pl.BlockSpec((None, D_MODEL, FF_TILE), lambda i, j: (j, 0, 0)),
                         pl.BlockSpec((None, D_MODEL, FF_TILE), lambda i, j: (j + FF_NJ, 0, 0)),
                         pl.BlockSpec((3, FF_TILE), lambda i, j: (0, j)),
                         pl.BlockSpec((1, FF_TILE), lambda i, j: (0, j)),
                         tile, pl.BlockSpec((D_MODEL, D_MODEL), lambda i, j: (0, 0)),
                         tile, pl.BlockSpec((1, D_MODEL), lambda i, j: (0, 0))],
               out_specs=(pl.BlockSpec((2, tm, FF_TILE), lambda i, j: (0, i, j)),
                          tile, accff, accff, acc8, acc8),
               scratch=[pltpu.VMEM((tm, D_MODEL), F32)],
               sem=("arbitrary", "arbitrary"))(h2, h2, h2, h2, h2, dr2, dr2, wd, wup_st, wup_st,
                                               convw, convb, dpg, wpg, r1, g1)


ANY = pl.BlockSpec(memory_space=pl.ANY)


def _chip_peers():
    x, y, c = lax.axis_index("x"), lax.axis_index("y"), lax.axis_index("c")
    return x, y, c, [(1 - x, y), (x, 1 - y), (1 - x, 1 - y)]


def _gather_comm(halved, whole=()):
    n, nw = len(halved), len(whole)

    def copies(ins, outs, sems):
        ici_send, ici_recv, d2d_send, d2d_recv, own_send, own_recv = sems
        x, y, c, peers = _chip_peers()
        me = 2 * x + y
        sibling = (x, y, 1 - c)
        own, ici, ici_wait, fwd, fwd_wait = [], [], [], [], []
        for ti in range(n + nw):
            src, dst = ins[ti], outs[ti]
            own.append(pltpu.make_async_remote_copy(
                src_ref=src, dst_ref=dst.at[me], send_sem=own_send.at[ti], recv_sem=own_recv.at[ti],
                device_id=sibling, device_id_type=MESH))
            for k, (px, py) in enumerate(peers):
                pk = 2 * px + py
                sem = dict(send_sem=ici_send.at[ti * 3 + k], recv_sem=ici_recv.at[ti * 3 + k],
                           device_id=(px, py, c), device_id_type=MESH)
                if ti < n:
                    ici.append(pltpu.make_async_remote_copy(src_ref=src.at[c], dst_ref=dst.at[me, c], **sem))
                    ici_wait.append(pltpu.make_async_remote_copy(src_ref=src.at[c], dst_ref=dst.at[pk, c], **sem))
                    dsem = dict(send_sem=d2d_send.at[ti * 3 + k], recv_sem=d2d_recv.at[ti * 3 + k],
                                device_id=sibling, device_id_type=MESH)
                    fwd.append(pltpu.make_async_remote_copy(src_ref=dst.at[pk, c], dst_ref=dst.at[pk, c], **dsem))
                    fwd_wait.append(pltpu.make_async_remote_copy(
                        src_ref=dst.at[pk, 1 - c], dst_ref=dst.at[pk, 1 - c], **dsem))
                else:
                    ici.append(pltpu.make_async_remote_copy(src_ref=src, dst_ref=dst.at[me], **sem))
                    ici_wait.append(pltpu.make_async_remote_copy(src_ref=src, dst_ref=dst.at[pk], **sem))
        return own, ici, ici_wait, fwd, fwd_wait

    def start(ins, outs, sems):
        own, ici, _, _, _ = copies(ins, outs, sems)
        for cp in own + ici:
            cp.start()

    def finish(ins, outs, sems):
        own, ici, ici_wait, fwd, fwd_wait = copies(ins, outs, sems)
        for i, cp in enumerate(ici_wait):
            cp.wait_recv()
            if i < len(fwd):
                fwd[i].start()
        for cp in fwd_wait + own:
            cp.wait_recv()
        for cp in own + ici + fwd:
            cp.wait_send()

    srcs = list(halved) + list(whole)
    return _Comm(srcs, [jax.ShapeDtypeStruct((N_CHIP,) + s.shape, s.dtype) for s in srcs],
                 [pltpu.SemaphoreType.DMA((3 * (n + nw),)), pltpu.SemaphoreType.DMA((3 * (n + nw),)),
                  pltpu.SemaphoreType.DMA((max(3 * n, 1),)), pltpu.SemaphoreType.DMA((max(3 * n, 1),)),
                  pltpu.SemaphoreType.DMA((n + nw,)), pltpu.SemaphoreType.DMA((n + nw,))],
                 start, finish)


def _sibling_exchange_comm(grads):
    n = len(grads)

    def copies(ins, outs, sems):
        send_sems, recv_sems = sems
        x, y, c = lax.axis_index("x"), lax.axis_index("y"), lax.axis_index("c")
        res = []
        for ti in range(n):
            half = ins[ti].shape[1] // 2
            res.append(pltpu.make_async_remote_copy(
                src_ref=ins[ti].at[:, pl.ds(pl.multiple_of((1 - c) * half, 16), half), :],
                dst_ref=outs[ti],
                send_sem=send_sems.at[ti], recv_sem=recv_sems.at[ti],
                device_id=(x, y, 1 - c), device_id_type=MESH))
        return res

    def start(ins, outs, sems):
        for cp in copies(ins, outs, sems):
            cp.start()

    def finish(ins, outs, sems):
        for cp in copies(ins, outs, sems):
            cp.wait()

    return _Comm(grads, [jax.ShapeDtypeStruct((N_CHIP, g.shape[1] // 2, g.shape[2]), g.dtype) for g in grads],
                 [pltpu.SemaphoreType.DMA((n,)), pltpu.SemaphoreType.DMA((n,))], start, finish)


def _in_proj_gathering(x2b, own, chip, tm, comm):
    t = x2b.shape[0]
    ni = t // tm
    half, cols = own.shape[1], own.shape[2]
    nci, nco = len(comm.ins), len(comm.out_shapes)

    def body(chip_ref, x_ref, own_ref, own_hbm, *rest):
        c_in = rest[:nci]
        h_ref, win_out = rest[nci:nci + 2]
        c_out = rest[nci + 2:nci + 2 + nco]
        w_scr, ici_send, ici_recv, d2d_send, d2d_recv, own_sems, ld_sems = rest[nci + 2 + nco:nci + 9 + nco]
        c_sem = rest[nci + 9 + nco:]
        s, i = pl.program_id(0), pl.program_id(1)
        x, y, c, peers = _chip_peers()
        me = 2 * x + y
        sibling = (x, y, 1 - c)

        def ici(k, slot):
            px, py = peers[k]
            return pltpu.make_async_remote_copy(
                src_ref=own_hbm.at[c], dst_ref=win_out.at[slot, c],
                send_sem=ici_send.at[k], recv_sem=ici_recv.at[k],
                device_id=(px, py, c), device_id_type=MESH)

        def forward(k, core):
            pk = 2 * peers[k][0] + peers[k][1]
            return pltpu.make_async_remote_copy(
                src_ref=win_out.at[pk, core], dst_ref=win_out.at[pk, core],
                send_sem=d2d_send.at[k], recv_sem=d2d_recv.at[k],
                device_id=sibling, device_id_type=MESH)

        place_own = pltpu.make_async_remote_copy(
            src_ref=own_hbm, dst_ref=win_out.at[me], send_sem=own_sems.at[0], recv_sem=own_sems.at[1],
            device_id=sibling, device_id_type=MESH)

        @pl.when((s == 0) & (i == 0))
        def _():
            for k in range(3):
                ici(k, me).start()
            place_own.start()

        @pl.when(s == 0)
        def _():
            xv = x_ref[...]
            h_ref[...] = (_dot(xv[:, :half], own_ref[0]) + _dot(xv[:, half:], own_ref[1])).astype(BF16)

        for k in range(3):
            @pl.when((s == k + 1) & (i == 0))
            def _(k=k):
                pk = 2 * peers[k][0] + peers[k][1]
                ici(k, pk).wait_recv()
                forward(k, c).start()
                forward(k, 1 - c).wait_recv()
                loads = [pltpu.make_async_copy(win_out.at[pk, hh], w_scr.at[hh], ld_sems.at[hh])
                         for hh in range(2)]
                for ld in loads:
                    ld.start()
                for ld in loads:
                    ld.wait()
                if k == 1:
                    comm.start(c_in, c_out, c_sem)

        @pl.when(s > 0)
        def _():
            xv = x_ref[...]
            h_ref[...] = (_dot(xv[:, :half], w_scr[0]) + _dot(xv[:, half:], w_scr[1])).astype(BF16)

        @pl.when((s == N_CHIP - 1) & (i == ni - 1))
        def _():
            place_own.wait()
            for k in range(3):
                ici(k, me).wait_send()
                forward(k, c).wait_send()
            comm.finish(c_in, c_out, c_sem)

    def shard_col(s, me):
        return jnp.where(s == 0, me, me ^ jnp.where(s == 1, 2, jnp.where(s == 2, 1, 3)))

    res = _pc(body, name="in_proj",
              out_shape=(jax.ShapeDtypeStruct((t, N_CHIP * cols), BF16),
                         jax.ShapeDtypeStruct((N_CHIP,) + own.shape, own.dtype)) + tuple(comm.out_shapes),
              grid=(N_CHIP, ni), nsp=1,
              in_specs=[pl.BlockSpec((tm, 2 * half), lambda s, i, chip_ref: (i, 0)),
                        pl.BlockSpec(own.shape, lambda s, i, chip_ref: (0, 0, 0)),
                        ANY] + [ANY] * nci,
              out_specs=(pl.BlockSpec((tm, cols), lambda s, i, chip_ref: (i, shard_col(s, chip_ref[0]))),
                         ANY) + tuple([ANY] * nco),
              scratch=[pltpu.VMEM(own.shape, own.dtype),
                       pltpu.SemaphoreType.DMA((3,)), pltpu.SemaphoreType.DMA((3,)),
                       pltpu.SemaphoreType.DMA((3,)), pltpu.SemaphoreType.DMA((3,)),
                       pltpu.SemaphoreType.DMA((2,)), pltpu.SemaphoreType.DMA((2,))] + comm.sems,
              sem=("arbitrary", "arbitrary"))(chip, x2b, own, own, *comm.ins)
    return res[0], res[1], res[2:]


def _rs_add_halves(name, grad, recv, core):
    _, r, cdim = grad.shape
    half = r // 2
    tr = _row_tile(half, cdim, mult=16)
    nr = half // tr

    def body(c_ref, g_ref, r_ref, o_ref):
        o_ref[...] = (g_ref[...].astype(F32) + r_ref[...].astype(F32)).astype(BF16)

    return _pc(body, name=name, out_shape=jax.ShapeDtypeStruct((N_CHIP, half, cdim), BF16),
               grid=(N_CHIP, nr), nsp=1,
               in_specs=[pl.BlockSpec((None, tr, cdim), lambda j, i, c_ref: (j, c_ref[0] * nr + i, 0)),
                         pl.BlockSpec((None, tr, cdim), lambda j, i, c_ref: (j, i, 0))],
               out_specs=pl.BlockSpec((None, tr, cdim), lambda j, i, c_ref: (j, i, 0)),
               sem=("parallel", "parallel"))(core, grad, recv)


def _chip_exchange_comm(parts):
    n = len(parts)

    def copies(ins, outs, sems):
        send_sems, recv_sems = sems
        x, y, c, peers = _chip_peers()
        return [pltpu.make_async_remote_copy(
            src_ref=ins[ti].at[2 * px + py], dst_ref=outs[ti].at[k],
            send_sem=send_sems.at[ti * 3 + k], recv_sem=recv_sems.at[ti * 3 + k],
            device_id=(px, py, c), device_id_type=MESH)
            for ti in range(n) for k, (px, py) in enumerate(peers)]

    def start(ins, outs, sems):
        for cp in copies(ins, outs, sems):
            cp.start()

    def finish(ins, outs, sems):
        for cp in copies(ins, outs, sems):
            cp.wait()

    return _Comm(parts, [jax.ShapeDtypeStruct((3,) + p.shape[1:], p.dtype) for p in parts],
                 [pltpu.SemaphoreType.DMA((3 * n,)), pltpu.SemaphoreType.DMA((3 * n,))], start, finish)


def _rs_sum_chips(name, part, recv, chip):
    _, half, cdim = recv.shape
    tr = _row_tile(half, cdim, mult=16)

    def body(chip_ref, p_ref, r_ref, o_ref):
        o_ref[...] = ((p_ref[...].astype(F32) + r_ref[0].astype(F32)) + r_ref[1].astype(F32)
                      ) + r_ref[2].astype(F32)

    return _pc(body, name=name, out_shape=jax.ShapeDtypeStruct((half, cdim), F32),
               grid=(half // tr,), nsp=1,
               in_specs=[pl.BlockSpec((None, tr, cdim), lambda i, chip_ref: (chip_ref[0], i, 0)),
                         pl.BlockSpec((3, tr, cdim), lambda i, chip_ref: (0, i, 0))],
               out_specs=pl.BlockSpec((tr, cdim), lambda i, chip_ref: (i, 0)),
               sem=("parallel",))(chip, part, recv)


def _rs_send_halves(halves):
    n = len(halves)

    def body(*refs):
        ins, outs = refs[:n], refs[n:2 * n]
        send_sems, recv_sems = refs[2 * n:]
        x, y, c = lax.axis_index("x"), lax.axis_index("y"), lax.axis_index("c")
        sends = []
        for ti in range(n):
            cp = pltpu.make_async_remote_copy(
                src_ref=ins[ti], dst_ref=outs[ti],
                send_sem=send_sems.at[ti], recv_sem=recv_sems.at[ti],
                device_id=(x, y, 1 - c), device_id_type=MESH)
            cp.start()
            sends.append(cp)
        for cp in sends:
            cp.wait()

    return _pc(body, name="rs_send_halves",
               out_shape=tuple(jax.ShapeDtypeStruct(hv.shape, hv.dtype) for hv in halves),
               in_specs=[ANY] * n, out_specs=tuple([ANY] * n),
               scratch=[pltpu.SemaphoreType.DMA((n,)), pltpu.SemaphoreType.DMA((n,))])(*halves)


def _adamw_rows(name, mine, theirs, w, m, v, core):
    half, cdim = mine.shape
    tr = _row_tile(half, cdim, budget=1 << 19)
    nrh = half // tr

    def body(c_ref, mine_ref, theirs_ref, w_ref, m_ref, v_ref, g_ref, d_ref, m2_ref, v2_ref):
        is_mine = (pl.program_id(0) // nrh) == c_ref[0]
        g = jnp.where(is_mine, mine_ref[...], theirs_ref[...])
        d, m2, v2 = _adamw(w_ref[...], g, m_ref[...], v_ref[...])
        g_ref[...] = g
        d_ref[...] = d
        m2_ref[...] = m2
        v2_ref[...] = v2

    htile = pl.BlockSpec((tr, cdim), lambda i, c_ref: (i % nrh, 0))
    tile = pl.BlockSpec((tr, cdim), lambda i, c_ref: (i, 0))
    shp = jax.ShapeDtypeStruct((2 * half, cdim), F32)
    return _pc(body, name=name, out_shape=(shp, shp, shp, shp), grid=(2 * nrh,), nsp=1,
               in_specs=[htile, htile, tile, tile, tile], out_specs=(tile, tile, tile, tile),
               sem=("parallel",))(core, mine, theirs, w, m, v)


def _adamw_whole(name, g, w, m, v):
    def body(g_ref, w_ref, m_ref, v_ref, d_ref, m2_ref, v2_ref):
        d, m2, v2 = _adamw(w_ref[...], g_ref[...], m_ref[...], v_ref[...])
        d_ref[...] = d
        m2_ref[...] = m2
        v2_ref[...] = v2

    shp = jax.ShapeDtypeStruct(g.shape, F32)
    return _pc(body, name=name, out_shape=(shp, shp, shp))(g, w, m, v)


SMALL_LAYOUT = (
    ("sgu_w_s", 1024, 1, 0),
    ("sgu_b_s", 8, 1, 1024),
    ("sgu_norm_g", 1, 0, 0),
    ("sgu_norm_b", 1, 0, 1),
    ("hgrn_norm_g", 1, 0, 3),
    ("ln1_g", 1, 0, 4),
    ("ln1_b", 1, 0, 5),
    ("ffn_conv_b", 1, 2, 3),
    ("ln2_g", 1, 0, 6),
    ("ln2_b", 1, 0, 7),
)
LB_ROW = 2
LOSS_ROW = 8
PACK_SHAPES = ((16, D_MODEL), (N_GROUP * 128 + 8, 128), (8, D_FF))


def _small_allreduce_adamw(rows1024, dws, dbs, dcw, dcb, logits, m_logits, v_logits,
                           small_w, small_m, small_v):
    ns = len(SMALL_LAYOUT)
    nr = len(rows1024)
    nb = len(PACK_SHAPES)

    def body(*refs):
        row_refs = refs[:nr]
        dws_ref, dbs_ref, dcw_ref, dcb_ref, lg_ref, mlg_ref, vlg_ref = refs[nr:nr + 7]
        pos = nr + 7
        w_refs = refs[pos:pos + ns]
        m_refs = refs[pos + ns:pos + 2 * ns]
        v_refs = refs[pos + 2 * ns:pos + 3 * ns]
        pos += 3 * ns
        loss_ref, dcw_out = refs[pos:pos + 2]
        lg_outs = refs[pos + 2:pos + 6]
        pos += 6
        outs = refs[pos:pos + 4 * ns]
        pos += 4 * ns
        pack = refs[pos:pos + nb]
        sib = refs[pos + nb:pos + 2 * nb]
        gath = refs[pos + 2 * nb:pos + 3 * nb]
        d2d_send, d2d_recv, ici_send, ici_recv = refs[pos + 3 * nb:]

        x, y, c, peers = _chip_peers()
        me = 2 * x + y
        sibling = (x, y, 1 - c)

        pack[0][...] = jnp.zeros(PACK_SHAPES[0], F32)
        for k in range(nr):
            pack[0][k:k + 1, :] = row_refs[k][0:1, :]
        pack[1][0:N_GROUP * 128, :] = dws_ref[...]
        pack[1][N_GROUP * 128:, :] = dbs_ref[...]
        pack[2][...] = jnp.zeros(PACK_SHAPES[2], F32)
        pack[2][0:3, :] = dcw_ref[0:3, :]
        pack[2][3:4, :] = dcb_ref[0:1, :]

        d2d = [pltpu.make_async_remote_copy(
            src_ref=pack[b], dst_ref=sib[b], send_sem=d2d_send.at[b], recv_sem=d2d_recv.at[b],
            device_id=sibling, device_id_type=MESH) for b in range(nb)]
        for cp in d2d:
            cp.start()
        for cp in d2d:
            cp.wait()
        for b in range(nb):
            gath[b][me] = pack[b][...] + sib[b][...]

        ici, ici_wait = [], []
        for b in range(nb):
            for k, (px, py) in enumerate(peers):
                sem = dict(send_sem=ici_send.at[b * 3 + k], recv_sem=ici_recv.at[b * 3 + k],
                           device_id=(px, py, c), device_id_type=MESH)
                ici.append(pltpu.make_async_remote_copy(src_ref=gath[b].at[me], dst_ref=gath[b].at[me], **sem))
                ici_wait.append(pltpu.make_async_remote_copy(
                    src_ref=gath[b].at[me], dst_ref=gath[b].at[2 * px + py], **sem))
        for cp in ici:
            cp.start()
        for cp in ici_wait:
            cp.wait_recv()
        for cp in ici:
            cp.wait_send()

        tot = pack
        for b in range(nb):
            tot[b][...] = ((gath[b][0] + gath[b][1]) + gath[b][2]) + gath[b][3]

        loss_ref[...] = tot[0][LOSS_ROW:LOSS_ROW + 1, :]
        dcw_out[...] = tot[2][...]
        lb = _sig(lg_ref[0:1, :] - lg_ref[1:2, :])
        d0 = tot[0][LB_ROW:LB_ROW + 1, :] * lb * (1.0 - lb)
        rowid = lax.broadcasted_iota(jnp.int32, (2, D_MODEL), 0)
        g_lg = jnp.where(rowid == 0, d0, -d0)
        dl, ml, vl = _adamw(lg_ref[...], g_lg, mlg_ref[...], vlg_ref[...])
        lg_outs[0][...] = g_lg
        lg_outs[1][...] = dl
        lg_outs[2][...] = ml
        lg_outs[3][...] = vl
        for si, (_, rows, b, r0) in enumerate(SMALL_LAYOUT):
            g = tot[b][r0:r0 + rows, :]
            dl, ml, vl = _adamw(w_refs[si][...], g, m_refs[si][...], v_refs[si][...])
            outs[4 * si][...] = g
            outs[4 * si + 1][...] = dl
            outs[4 * si + 2][...] = ml
            outs[4 * si + 3][...] = vl

    shapes = [jax.ShapeDtypeStruct((1, D_MODEL), F32), jax.ShapeDtypeStruct((8, D_FF), F32)]
    shapes += [jax.ShapeDtypeStruct((2, D_MODEL), F32)] * 4
    for w in small_w:
        shapes += [jax.ShapeDtypeStruct(w.shape, F32)] * 4
    scratch = [pltpu.VMEM(shp, F32) for shp in PACK_SHAPES]
    scratch += [pltpu.VMEM(shp, F32) for shp in PACK_SHAPES]
    scratch += [pltpu.VMEM((N_CHIP,) + shp, F32) for shp in PACK_SHAPES]
    scratch += [pltpu.SemaphoreType.DMA((nb,)), pltpu.SemaphoreType.DMA((nb,)),
                pltpu.SemaphoreType.DMA((3 * nb,)), pltpu.SemaphoreType.DMA((3 * nb,))]
    vm = pl.BlockSpec(memory_space=pltpu.VMEM)
    n_in = nr + 7 + 3 * ns
    res = _pc(body, name="small_allreduce_adamw", out_shape=tuple(shapes),
              in_specs=[vm] * n_in, out_specs=tuple([vm] * len(shapes)),
              scratch=scratch)(*rows1024, dws, dbs, dcw, dcb, logits, m_logits, v_logits,
                               *small_w, *small_m, *small_v)
    return res[0], res[1], res[2:6], res[6:]


def kernel(x, p, w_in, sgu_w_s, sgu_b_s, sgu_norm_g, sgu_norm_b, hgrn_lb_logits, hgrn_norm_g, w_branch, w_out, ln1_g, ln1_b, ffn_w_up, ffn_conv_w, ffn_conv_b, ffn_w_down, ln2_g, ln2_b, ple_w_proj, ple_w_gate, loss_target, m_w_in, m_sgu_w_s, m_sgu_b_s, m_sgu_norm_g, m_sgu_norm_b, m_hgrn_lb_logits, m_hgrn_norm_g, m_w_branch, m_w_out, m_ln1_g, m_ln1_b, m_ffn_w_up, m_ffn_conv_w, m_ffn_conv_b, m_ffn_w_down, m_ln2_g, m_ln2_b, m_ple_w_proj, m_ple_w_gate, v_w_in, v_sgu_w_s, v_sgu_b_s, v_sgu_norm_g, v_sgu_norm_b, v_hgrn_lb_logits, v_hgrn_norm_g, v_w_branch, v_w_out, v_ln1_g, v_ln1_b, v_ffn_w_up, v_ffn_conv_w, v_ffn_conv_b, v_ffn_w_down, v_ln2_g, v_ln2_b, v_ple_w_proj, v_ple_w_gate):
    t = x.shape[1]
    x2 = x.reshape(t, D_MODEL)
    x2b = x2.astype(BF16)
    p2 = p.reshape(t, PLE_DIM)
    tgt = loss_target.reshape(t, D_MODEL)
    core = lax.axis_index("c").astype(jnp.int32).reshape(1)
    chip_id = (2 * lax.axis_index("x") + lax.axis_index("y")).astype(jnp.int32).reshape(1)

    big_w = [w_in[0], w_branch[0, 0], w_branch[0, 1], w_out[0], ffn_w_up[0], ffn_w_down[0],
             ple_w_proj[0], ple_w_gate[0]]
    big_m = [m_w_in[0], m_w_branch[0, 0], m_w_branch[0, 1], m_w_out[0], m_ffn_w_up[0],
             m_ffn_w_down[0], m_ple_w_proj[0], m_ple_w_gate[0]]
    big_v = [v_w_in[0], v_w_branch[0, 0], v_w_branch[0, 1], v_w_out[0], v_ffn_w_up[0],
             v_ffn_w_down[0], v_ple_w_proj[0], v_ple_w_gate[0]]
    def halves_of(i):
        w = big_w[i]
        return w.astype(BF16).reshape(2, w.shape[0] // 2, w.shape[1])

    def stacked(g, i):
        return g.reshape(N_CHIP, big_w[i].shape[0], big_w[i].shape[1])


    cid = jnp.arange(SGU_BLOCK) // CHUNK
    maskf = (cid[:, None] >= cid[None, :]).astype(F32)
    ws_masked = sgu_w_s[0] * maskf[None]
    wm = ws_masked.astype(BF16)
    wmt = jnp.transpose(ws_masked, (0, 2, 1)).astype(BF16)
    bsb = jnp.broadcast_to(sgu_b_s[0][:, :, None], (N_GROUP, SGU_BLOCK, 128))

    h, win_g, (wup_g,) = _in_proj_gathering(x2b, halves_of(0), chip_id, 512, _gather_comm([halves_of(4)]))
    win_st = stacked(win_g, 0)
    wup_st = stacked(wup_g, 4)
    ya, _ = _sgu_fwd(h, wm, bsb, sgu_norm_g, sgu_norm_b)
    (yb, st_all), mix_g = _hgrn_fwd(h, hgrn_lb_logits, hgrn_norm_g,
                                     comm=_gather_comm([halves_of(i) for i in (1, 2, 3)], [ffn_conv_w[0]]))
    wb0, wb1, wo = [stacked(g, i).reshape(D_MODEL, D_MODEL) for g, i in zip(mix_g[:3], (1, 2, 3))]
    convw = jnp.transpose(mix_g[3], (1, 0, 2)).reshape(3, D_FF)
    (r1, a_br, b_br, m_bf, x1b), _ = _mix_fwd(ya, yb, h, x2, wb0, wb1, wo, ln1_g, ln1_b, 256)
    h2, act, out_g = _ffn_up_act(x1b, wup_st, convw, ffn_conv_b, 512,
                                 _gather_comm([halves_of(i) for i in (5, 6, 7)]))
    wd = stacked(out_g[0], 5).reshape(D_FF, D_MODEL)
    wpp = jnp.transpose(stacked(out_g[1], 6), (1, 0, 2)).reshape(PLE_DIM, D_MODEL)
    wpg = stacked(out_g[2], 7).reshape(D_MODEL, D_MODEL)
    dr2, dpg, dpp, loss_acc, dg2, db2 = _out_fwd_bwd(
        act, x1b, r1, p2, tgt, wd, wpg, wpp, ln1_g, ln1_b, ln2_g, ln2_b, 256)

    dh2, dr1, dcw, dcb, dg1, db1 = _ffn_bwd(h2, dr2, wd, wup_st, dpg, wpg, r1, ln1_g, convw, ffn_conv_b, 256)
    d_wd = _mm_tn("ffn_down_wgrad", act, dr2, FF_TILE, 512)
    d_wpg = _mm_tn("ple_gate_wgrad", x1b, dpg, 512, D_MODEL)
    d_wpp_st = _mm_tn("ple_proj_wgrad", p2, dpp, PLE_DIM, PLE_DIM, stacked=True)
    d_wup_st = _mm("ffn_up_wgrad", x1b, dh2, TN, (2, N_CHIP, 1),
                   pl.BlockSpec((t, 512), lambda i, j, k: (0, i)),
                   pl.BlockSpec((None, t, FF_TILE), lambda i, j, k: (j // FF_NJ, 0, j % FF_NJ)),
                   jax.ShapeDtypeStruct((N_CHIP, D_MODEL, FF_TILE), BF16),
                   pl.BlockSpec((None, 512, FF_TILE), lambda i, j, k: (j, i, 0)))
    da_bf, db_bf, dh3, dya, dyb = _mix_bwd(dr1, h, a_br, b_br, wo, wb0, wb1, 256)
    d_wo = _mm_tn("out_proj_wgrad", m_bf, dr1, 512, 512)
    d_wb0 = _mm_tn("branch0_wgrad", ya, da_bf, 512, D_MODEL)
    d_wb1 = _mm_tn("branch1_wgrad", yb, db_bf, 512, D_MODEL)
    grads_1 = [d_wb0.reshape(4, 256, D_MODEL), d_wb1.reshape(4, 256, D_MODEL),
               d_wo.reshape(4, 256, D_MODEL), d_wup_st, d_wd.reshape(4, D_FF // 4, D_MODEL),
               d_wpp_st, d_wpg.reshape(4, 256, D_MODEL)]
    (dh0, dws, dbs, dgv, dbv), recv_a1 = _sgu_bwd(h, dya, wm, wmt, bsb, sgu_norm_g, sgu_norm_b, maskf,
                                                  comm=_sibling_exchange_comm(grads_1))
    parts_1 = [_rs_add_halves("rs_add_halves%d" % (i + 1), g, r, core)
               for i, (g, r) in enumerate(zip(grads_1, recv_a1))]
    (dh1, dh2h, dlb, dgn), recv_b1 = _hgrn_bwd(h, dyb, st_all, hgrn_lb_logits, hgrn_norm_g,
                                                comm=_chip_exchange_comm(parts_1))
    dh_parts = [dh0, dh1, dh2h, dh3]
    d_win = [_mm_tn("in_proj_wgrad%d" % j, x2b, dh_parts[j], 512, D_MODEL) for j in range(4)]

    grads_0 = [jnp.stack(d_win)]
    recv_a0 = _run_comm("rs_sibling_exchange0", _sibling_exchange_comm(grads_0))
    parts_0 = [_rs_add_halves("rs_add_halves0", grads_0[0], recv_a0[0], core)]
    gx, recv_b0 = _in_proj_xgrad(dh_parts, win_st, dr1, 512, comm=_chip_exchange_comm(parts_0))
    parts = parts_0 + parts_1
    recv_b = list(recv_b0) + list(recv_b1)
    halves = [_rs_sum_chips("rs_sum_chips%d" % i, pt, r, chip_id)
              for i, (pt, r) in enumerate(zip(parts, recv_b))]
    theirs = _rs_send_halves(halves)
    big_out = [_adamw_rows("adamw_big%d" % i, halves[i], theirs[i], big_w[i], big_m[i], big_v[i], core)
               for i in range(len(halves))]

    small_in = dict(sgu_w_s=(sgu_w_s, m_sgu_w_s, v_sgu_w_s), sgu_b_s=(sgu_b_s, m_sgu_b_s, v_sgu_b_s),
                    sgu_norm_g=(sgu_norm_g, m_sgu_norm_g, v_sgu_norm_g),
                    sgu_norm_b=(sgu_norm_b, m_sgu_norm_b, v_sgu_norm_b),
                    hgrn_norm_g=(hgrn_norm_g, m_hgrn_norm_g, v_hgrn_norm_g),
                    ln1_g=(ln1_g, m_ln1_g, v_ln1_g), ln1_b=(ln1_b, m_ln1_b, v_ln1_b),
                    ffn_conv_b=(ffn_conv_b, m_ffn_conv_b, v_ffn_conv_b),
                    ln2_g=(ln2_g, m_ln2_g, v_ln2_g), ln2_b=(ln2_b, m_ln2_b, v_ln2_b))

    def flat(name, arr):
        rows = dict((n, r) for n, r, _, _ in SMALL_LAYOUT)[name]
        return arr.reshape(rows, arr.size // rows)

    names = [n for n, _, _, _ in SMALL_LAYOUT]
    sw = [flat(n, small_in[n][0]) for n in names]
    sm = [flat(n, small_in[n][1]) for n in names]
    sv = [flat(n, small_in[n][2]) for n in names]
    loss_rows, dcw_tot, lg_out, small_out = _small_allreduce_adamw(
        [dgv, dbv, dlb, dgn, dg1, db1, dg2, db2, loss_acc], dws.reshape(N_GROUP * 128, 128), dbs, dcw, dcb,
        hgrn_lb_logits, m_hgrn_lb_logits, v_hgrn_lb_logits, sw, sm, sv)
    loss = loss_rows[0, 0]

    chip = 2 * lax.axis_index("x") + lax.axis_index("y")
    g_cw = lax.dynamic_slice(dcw_tot, (0, chip * (D_FF // 4)), (3, D_FF // 4))
    cw_out = _adamw_whole("adamw_conv_w", g_cw, ffn_conv_w[0], m_ffn_conv_w[0], v_ffn_conv_w[0])

    res = {}
    for si, n in enumerate(names):
        shp = small_in[n][0].shape
        res[n] = tuple(small_out[4 * si + k].reshape(shp) for k in range(4))
    res["hgrn_lb_logits"] = tuple(lg_out)
    res["ffn_conv_w"] = (g_cw[None],) + tuple(o[None] for o in cw_out)

    def big(i):
        return tuple(big_out[i])

    res["w_in"] = tuple(o[None] for o in big(0))
    res["w_branch"] = tuple(jnp.stack([o0, o1])[None] for o0, o1 in zip(big(1), big(2)))
    res["w_out"] = tuple(o[None] for o in big(3))
    res["ffn_w_up"] = tuple(o[None] for o in big(4))
    res["ffn_w_down"] = tuple(o[None] for o in big(5))
    res["ple_w_proj"] = tuple(o[None] for o in big(6))
    res["ple_w_gate"] = tuple(o[None] for o in big(7))

    order = ["w_in", "sgu_w_s", "sgu_b_s", "sgu_norm_g", "sgu_norm_b", "hgrn_lb_logits",
             "hgrn_norm_g", "w_branch", "w_out", "ln1_g", "ln1_b", "ffn_w_up", "ffn_conv_w",
             "ffn_conv_b", "ffn_w_down", "ln2_g", "ln2_b", "ple_w_proj", "ple_w_gate"]
    outs = [loss, gx.reshape(1, t, D_MODEL)]
    for k in range(4):
        outs += [res[n][k] for n in order]
    return tuple(outs)
```

```python
import functools

import jax
import jax.numpy as jnp
from jax import lax
from jax.experimental import pallas as pl
from jax.experimental.pallas import tpu as pltpu

F32 = jnp.float32
BF16 = jnp.bfloat16
HIGHEST = lax.Precision.HIGHEST
MESH = pl.DeviceIdType.MESH

D_MODEL = 1024
CHUNK = 64
SGU_BLOCK = 128
N_GROUP = 8
N_HEAD = 8
HEAD_DIM = 128
D_FF = 2816
PLE_DIM = 256
IN_COLS = 8192
LN_EPS = 1e-5
RMS_EPS = 1e-6
ALPHA = 2.0 ** 0.25
N_CHIP = 4
N_DEV = 8

ADAM_LR = 0.001
ADAM_B1 = 0.9
ADAM_B2 = 0.999
ADAM_EPS = 1e-08
ADAM_WD = 0.01
ADAM_STEP = 10

VMEM_LIMIT = 56 * 1024 * 1024

NN = (((1,), (0,)), ((), ()))
NT = (((1,), (1,)), ((), ()))
TN = (((0,), (0,)), ((), ()))


def _pc(body, *, name, out_shape, grid=None, in_specs=None, out_specs=None, scratch=(),
        sem=None, nsp=0, vmem=VMEM_LIMIT):
    params = dict(vmem_limit_bytes=vmem)
    if sem is not None:
        params["dimension_semantics"] = sem
    kw = dict(name=name, out_shape=out_shape, compiler_params=pltpu.CompilerParams(**params))
    if nsp:
        kw["grid_spec"] = pltpu.PrefetchScalarGridSpec(
            num_scalar_prefetch=nsp, grid=grid, in_specs=in_specs, out_specs=out_specs,
            scratch_shapes=list(scratch))
    else:
        if grid is not None:
            kw["grid"] = grid
        if in_specs is not None:
            kw["in_specs"] = in_specs
            kw["out_specs"] = out_specs
        kw["scratch_shapes"] = list(scratch)
    return pl.pallas_call(body, **kw)


def _dot(a, b, dims=NN):
    return lax.dot_general(a.astype(BF16), b.astype(BF16), dims, preferred_element_type=F32)


def _dot32(a, b, dims=NN):
    return lax.dot_general(a, b, dims, precision=HIGHEST, preferred_element_type=F32)


def _sig(x):
    return 1.0 / (1.0 + jnp.exp(-x))


_GC = 0.7978845608028654
_GA = 0.044715


def _gelu(x):
    return 0.5 * x * (1.0 + jnp.tanh(_GC * (x + _GA * x * x * x)))


def _gelu_and_grad(x):
    t = jnp.tanh(_GC * (x + _GA * x * x * x))
    g = 0.5 * x * (1.0 + t)
    dg = 0.5 * (1.0 + t) + 0.5 * x * (1.0 - t * t) * _GC * (1.0 + 3.0 * _GA * x * x)
    return g, dg


def _ln_stats(r):
    mu = jnp.mean(r, axis=-1, keepdims=True)
    xc = r - mu
    var = jnp.mean(xc * xc, axis=-1, keepdims=True)
    rstd = lax.rsqrt(var + LN_EPS)
    return xc * rstd, rstd


def _ln_bwd(dxh, xh, rstd):
    m1 = jnp.mean(dxh, axis=-1, keepdims=True)
    m2 = jnp.mean(dxh * xh, axis=-1, keepdims=True)
    return rstd * (dxh - m1 - xh * m2)


def _colsum8(v):
    return jnp.broadcast_to(jnp.sum(v, axis=0, keepdims=True), (8, v.shape[1]))


def _adamw(w, g, m, v):
    m2 = ADAM_B1 * m + (1.0 - ADAM_B1) * g
    v2 = ADAM_B2 * v + (1.0 - ADAM_B2) * (g * g)
    m_hat = m2 / (1.0 - ADAM_B1 ** ADAM_STEP)
    v_hat = v2 / (1.0 - ADAM_B2 ** ADAM_STEP)
    delta = -ADAM_LR * (m_hat / (jnp.sqrt(v_hat) + ADAM_EPS) + ADAM_WD * w)
    return delta, m2, v2


def _row_tile(rows, cols, itemsize=4, budget=1 << 20, mult=8):
    best = mult
    for tr in range(mult, rows + 1, mult):
        if rows % tr == 0 and tr * cols * itemsize <= budget:
            best = tr
    return best


def _mm(name, a, b, dims, grid, a_spec, b_spec, out_shape, o_spec, add=None, add_spec=None,
        add_scale=1.0, comm=None):
    nk = grid[2]
    has_add = add is not None
    out_dtype = out_shape.dtype

    def body(*refs):
        if has_add:
            a_ref, b_ref, add_ref, o_ref = refs[:4]
            rest = refs[4:]
        else:
            a_ref, b_ref, o_ref = refs[:3]
            add_ref = None
            rest = refs[3:]
        prod = _dot(a_ref[...], b_ref[...], dims)

        def finish(acc):
            if has_add:
                acc = acc + add_scale * add_ref[...]
            o_ref[...] = acc.astype(out_dtype)

        if nk == 1:
            finish(prod)
        else:
            acc_ref = rest[0]
            k = pl.program_id(2)

            @pl.when(k == 0)
            def _():
                acc_ref[...] = prod

            @pl.when(k > 0)
            def _():
                acc_ref[...] += prod

            @pl.when(k == nk - 1)
            def _():
                finish(acc_ref[...])

    in_specs = [a_spec, b_spec] + ([add_spec] if has_add else [])
    args = [a, b] + ([add] if has_add else [])
    scratch = []
    if nk > 1:
        blk = [d for d in o_spec.block_shape if d is not None]
        scratch = [pltpu.VMEM(tuple(blk), F32)]
    if comm is None:
        return _pc(body, name=name, out_shape=out_shape, grid=grid, in_specs=in_specs,
                   out_specs=o_spec, scratch=scratch,
                   sem=("parallel", "parallel", "arbitrary"))(*args)

    def first():
        return (pl.program_id(0) == 0) & (pl.program_id(1) == 0) & (pl.program_id(2) == 0)

    def last():
        return ((pl.program_id(0) == grid[0] - 1) & (pl.program_id(1) == grid[1] - 1)
                & (pl.program_id(2) == grid[2] - 1))

    res = _hosted_call(body, comm, first, last, name=name, out_shape=(out_shape,), grid=grid,
                       in_specs=in_specs, out_specs=(o_spec,), scratch=scratch,
                       sem=("arbitrary", "arbitrary", "arbitrary"), args=args)
    return res[0], res[1:]


class _Comm:
    def __init__(self, ins, out_shapes, sems, start, finish):
        self.ins, self.out_shapes, self.sems = list(ins), list(out_shapes), list(sems)
        self.start, self.finish = start, finish


def _hosted_call(body, comm, first, last, *, name, out_shape, grid, in_specs, out_specs, scratch, sem,
                 args):
    n_in, n_out, n_scr = len(in_specs), len(out_shape), len(scratch)
    nci, nco = len(comm.ins), len(comm.out_shapes)

    def wrapped(*refs):
        pos = n_in
        own_in, c_in = refs[:pos], refs[pos:pos + nci]
        pos += nci
        own_out, c_out = refs[pos:pos + n_out], refs[pos + n_out:pos + n_out + nco]
        pos += n_out + nco
        own_scr, c_sem = refs[pos:pos + n_scr], refs[pos + n_scr:]

        @pl.when(first())
        def _():
            comm.start(c_in, c_out, c_sem)

        body(*own_in, *own_out, *own_scr)

        @pl.when(last())
        def _():
            comm.finish(c_in, c_out, c_sem)

    return _pc(wrapped, name=name, out_shape=tuple(out_shape) + tuple(comm.out_shapes), grid=grid,
               in_specs=list(in_specs) + [ANY] * nci, out_specs=tuple(out_specs) + tuple([ANY] * nco),
               scratch=list(scratch) + comm.sems, sem=sem)(*args, *comm.ins)


def _grid1_call(body, comm, n, *, name, out_shape, in_specs, out_specs, scratch, args):
    if comm is None:
        return _pc(body, name=name, out_shape=out_shape, grid=(n,), in_specs=in_specs,
                   out_specs=out_specs, scratch=scratch, sem=("arbitrary",))(*args), ()
    res = _hosted_call(body, comm, lambda: pl.program_id(0) == 0, lambda: pl.program_id(0) == n - 1,
                       name=name, out_shape=out_shape, grid=(n,), in_specs=in_specs,
                       out_specs=out_specs, scratch=scratch, sem=("arbitrary",), args=args)
    return res[:len(out_shape)], res[len(out_shape):]


def _run_comm(name, comm):
    nci, nco = len(comm.ins), len(comm.out_shapes)

    def body(*refs):
        c_in, c_out, c_sem = refs[:nci], refs[nci:nci + nco], refs[nci + nco:]
        comm.start(c_in, c_out, c_sem)
        comm.finish(c_in, c_out, c_sem)

    return _pc(body, name=name, out_shape=tuple(comm.out_shapes), in_specs=[ANY] * nci,
               out_specs=tuple([ANY] * nco), scratch=comm.sems)(*comm.ins)


def _mm_nn_stacked(name, a, w_st, tm, comm=None):
    t, k = a.shape
    _, _, c = w_st.shape
    return _mm(name, a, w_st, NN, (t // tm, N_CHIP, 1),
               pl.BlockSpec((tm, k), lambda i, j, kk: (i, 0)),
               pl.BlockSpec((None, k, c), lambda i, j, kk: (j, 0, 0)),
               jax.ShapeDtypeStruct((t, N_CHIP * c), BF16),
               pl.BlockSpec((tm, c), lambda i, j, kk: (i, j)), comm=comm)


def _mm_tn(name, a, b, tm, tn, stacked=False):
    t, m = a.shape
    _, n = b.shape
    if stacked:
        assert tm == m
        out_shape = jax.ShapeDtypeStruct((n // tn, m, tn), BF16)
        o_spec = pl.BlockSpec((None, tm, tn), lambda i, j, kk: (j, 0, 0))
    else:
        out_shape = jax.ShapeDtypeStruct((m, n), BF16)
        o_spec = pl.BlockSpec((tm, tn), lambda i, j, kk: (i, j))
    return _mm(name, a, b, TN, (m // tm, n // tn, 1),
               pl.BlockSpec((t, tm), lambda i, j, kk: (0, i)),
               pl.BlockSpec((t, tn), lambda i, j, kk: (0, j)),
               out_shape, o_spec)


def _in_proj_xgrad(dh_parts, win_st, dr1, tm, comm=None):
    t = dr1.shape[0]
    ni = t // tm

    def body(a0, a1, a2, a3, b_ref, add_ref, o_ref, acc):
        j = pl.program_id(1)
        for jj, a_ref in enumerate((a0, a1, a2, a3)):
            @pl.when(j == jj)
            def _(jj=jj, a_ref=a_ref):
                prod = _dot(a_ref[...], b_ref[...], NT)
                if jj == 0:
                    acc[...] = prod + ALPHA * add_ref[...]
                elif jj < N_CHIP - 1:
                    acc[...] += prod
                else:
                    o_ref[...] = acc[...] + prod

    a_spec = pl.BlockSpec((tm, 2 * D_MODEL), lambda i, j: (i, 0))
    tile = pl.BlockSpec((tm, D_MODEL), lambda i, j: (i, 0))
    kw = dict(name="in_proj_xgrad", out_shape=(jax.ShapeDtypeStruct((t, D_MODEL), F32),),
              grid=(ni, N_CHIP),
              in_specs=[a_spec] * 4 + [pl.BlockSpec((None, D_MODEL, 2 * D_MODEL), lambda i, j: (j, 0, 0)),
                                       tile],
              out_specs=(tile,), scratch=[pltpu.VMEM((tm, D_MODEL), F32)],
              sem=("arbitrary", "arbitrary"))
    args = list(dh_parts) + [win_st, dr1]
    if comm is None:
        return _pc(body, **kw)(*args)[0], ()
    res = _hosted_call(body, comm,
                       lambda: (pl.program_id(0) == 0) & (pl.program_id(1) == 0),
                       lambda: (pl.program_id(0) == ni - 1) & (pl.program_id(1) == N_CHIP - 1),
                       args=args, **kw)
    return res[0], res[1:]


def _sgu_mixed(v, wm_ref, bsb_ref, gv, bv):
    gl, dgl = _gelu_and_grad(v)
    vh, rstd = _ln_stats(gl)
    vn = vh * gv + bv
    mixed = []
    for g in range(N_GROUP):
        sl = slice(g * 128, (g + 1) * 128)
        mixed.append(_dot(wm_ref[g], vn[:, sl]) + bsb_ref[g])
    return dgl, vh, rstd, vn, mixed


def _sgu_fwd(h, wm, bsb, gv, bv, comm=None):
    t = h.shape[0]

    def body(u_ref, v_ref, wm_ref, bsb_ref, gv_ref, bv_ref, ya_ref):
        u = u_ref[...].astype(F32)
        _, _, _, _, mixed = _sgu_mixed(v_ref[...].astype(F32), wm_ref, bsb_ref, gv_ref[...], bv_ref[...])
        gu = _gelu(u)
        for g in range(N_GROUP):
            sl = slice(g * 128, (g + 1) * 128)
            ya_ref[:, sl] = (gu[:, sl] * mixed[g]).astype(BF16)

    full3 = pl.BlockSpec((N_GROUP, 128, 128), lambda i: (0, 0, 0))
    vec = pl.BlockSpec((1, D_MODEL), lambda i: (0, 0))
    (ya,), extra = _grid1_call(
        body, comm, t // SGU_BLOCK, name="sgu_fwd",
        out_shape=(jax.ShapeDtypeStruct((t, D_MODEL), BF16),),
        in_specs=[pl.BlockSpec((SGU_BLOCK, D_MODEL), lambda i: (i, 0)),
                  pl.BlockSpec((SGU_BLOCK, D_MODEL), lambda i: (i, 1)),
                  full3, full3, vec, vec],
        out_specs=(pl.BlockSpec((SGU_BLOCK, D_MODEL), lambda i: (i, 0)),),
        scratch=[], args=(h, h, wm, bsb, gv, bv))
    return ya, extra


def _sgu_bwd(h, dya, wm, wmt, bsb, gv, bv, maskf, comm=None):
    t = h.shape[0]
    nb = t // SGU_BLOCK

    def body(u_ref, v_ref, dya_ref, wm_ref, wmt_ref, bsb_ref, gv_ref, bv_ref, mask_ref,
             dh_ref, dws_ref, dbs_ref, dgv_ref, dbv_ref, dmix_acc):
        i = pl.program_id(0)

        @pl.when(i == 0)
        def _():
            dws_ref[...] = jnp.zeros_like(dws_ref)
            dgv_ref[...] = jnp.zeros_like(dgv_ref)
            dbv_ref[...] = jnp.zeros_like(dbv_ref)
            dmix_acc[...] = jnp.zeros_like(dmix_acc)

        u = u_ref[...].astype(F32)
        gvv = gv_ref[...]
        dgl_v, vh, rstd, vn, mixed = _sgu_mixed(v_ref[...].astype(F32), wm_ref, bsb_ref, gvv, bv_ref[...])
        gu, dgl_u = _gelu_and_grad(u)
        dya_v = dya_ref[...]
        dvn_parts = []
        for g in range(N_GROUP):
            sl = slice(g * 128, (g + 1) * 128)
            d_y = dya_v[:, sl]
            dh_ref[:, sl] = (d_y * mixed[g] * dgl_u[:, sl]).astype(BF16)
            d_mixed = d_y * gu[:, sl]
            dmix_acc[g] += d_mixed
            dws_ref[g] += _dot(d_mixed, vn[:, sl], NT) * mask_ref[...]
            dvn_parts.append(_dot(wmt_ref[g], d_mixed))
        dvn = jnp.concatenate(dvn_parts, axis=1)
        dgv_ref[...] += _colsum8(dvn * vh)
        dbv_ref[...] += _colsum8(dvn)
        d_gl = _ln_bwd(dvn * gvv, vh, rstd)
        dh_ref[:, D_MODEL:] = (d_gl * dgl_v).astype(BF16)

        @pl.when(i == nb - 1)
        def _():
            rowid = lax.broadcasted_iota(jnp.int32, (8, 128), 0)
            ones = jnp.ones((8, 128), F32)
            acc = jnp.zeros((8, 128), F32)
            for g in range(N_GROUP):
                rs = _dot32(ones, dmix_acc[g], NT)
                acc = jnp.where(rowid == g, rs, acc)
            dbs_ref[...] = acc

    full3 = pl.BlockSpec((N_GROUP, 128, 128), lambda i: (0, 0, 0))
    vec = pl.BlockSpec((1, D_MODEL), lambda i: (0, 0))
    acc8 = pl.BlockSpec((8, D_MODEL), lambda i: (0, 0))
    return _grid1_call(
        body, comm, nb, name="sgu_bwd",
        out_shape=(jax.ShapeDtypeStruct((t, 2 * D_MODEL), BF16),
                   jax.ShapeDtypeStruct((N_GROUP, 128, 128), F32),
                   jax.ShapeDtypeStruct((8, 128), F32),
                   jax.ShapeDtypeStruct((8, D_MODEL), F32),
                   jax.ShapeDtypeStruct((8, D_MODEL), F32)),
        in_specs=[pl.BlockSpec((SGU_BLOCK, D_MODEL), lambda i: (i, 0)),
                  pl.BlockSpec((SGU_BLOCK, D_MODEL), lambda i: (i, 1)),
                  pl.BlockSpec((SGU_BLOCK, D_MODEL), lambda i: (i, 0)),
                  full3, full3, full3, vec, vec,
                  pl.BlockSpec((128, 128), lambda i: (0, 0))],
        out_specs=(pl.BlockSpec((SGU_BLOCK, 2 * D_MODEL), lambda i: (i, 0)),
                   full3, pl.BlockSpec((8, 128), lambda i: (0, 0)), acc8, acc8),
        scratch=[pltpu.VMEM((N_GROUP, 128, 128), F32)],
        args=(h, h, dya, wm, wmt, bsb, gv, bv, maskf))


def _tri_masks():
    row = lax.broadcasted_iota(jnp.int32, (CHUNK, CHUNK), 0)
    col = lax.broadcasted_iota(jnp.int32, (CHUNK, CHUNK), 1)
    return col <= row, col >= row


def _heads(v):
    return [v[:, hd * HEAD_DIM:(hd + 1) * HEAD_DIM] for hd in range(N_HEAD)]


def _tri_cumsum(tri_bf, v):
    hi = v.astype(BF16)
    r = v - hi.astype(F32)
    mid = r.astype(BF16)
    lo = (r - mid.astype(F32)).astype(BF16)
    return _dot(tri_bf, hi) + _dot(tri_bf, mid) + _dot(tri_bf, lo)


def _hgrn_chunk(q, fp, ii, lb, st_heads, causal):
    sg = _sig(fp)
    f = lb + (1.0 - lb) * sg
    k = 1.0 - f
    c = _tri_cumsum(causal.astype(BF16), jnp.log(f))
    ec = jnp.exp(c)
    en = jnp.exp(-c)
    sq = _sig(q)
    qt = q * sq * ec
    kt = k * en
    ecl = jnp.exp(c[CHUNK - 1:CHUNK, :])
    kk = kt * ecl
    qtb, ktb, iib, kkb = qt.astype(BF16), kt.astype(BF16), ii.astype(BF16), kk.astype(BF16)
    attn, o = [], []
    for hd, (qh, kh, ih) in enumerate(zip(_heads(qtb), _heads(ktb), _heads(iib))):
        a = jnp.where(causal, _dot(qh, kh, NT), 0.0).astype(BF16)
        attn.append(a)
        o.append(_dot(a, ih) + _dot(qh, st_heads[hd], NT))
    return dict(sg=sg, f=f, k=k, ec=ec, en=en, sq=sq, ecl=ecl, kk=kk, qtb=qtb, ktb=ktb, iib=iib,
                kkb=kkb, attn=attn, o=o)


def _rms_heads(o_heads):
    rinv = [lax.rsqrt(jnp.mean(o * o, axis=-1, keepdims=True) + RMS_EPS) for o in o_heads]
    return rinv, jnp.concatenate([o * r for o, r in zip(o_heads, rinv)], axis=1)


HG_CHUNKS = 4
HG_ROWS = HG_CHUNKS * CHUNK


def _hgrn_fwd(h, logits, gn, comm=None):
    t = h.shape[0]
    nb = t // HG_ROWS

    def body(q_ref, f_ref, i_ref, og_ref, lg_ref, gn_ref, yb_ref, st_ref, state):
        @pl.when(pl.program_id(0) == 0)
        def _():
            state[...] = jnp.zeros_like(state)

        causal, _ = _tri_masks()
        lb = _sig(lg_ref[0:1, :] - lg_ref[1:2, :])
        gnv = gn_ref[...]
        st = [state[hd] for hd in range(N_HEAD)]
        for cc in range(HG_CHUNKS):
            rows = slice(cc * CHUNK, (cc + 1) * CHUNK)
            og = og_ref[rows, :].astype(F32)
            r = _hgrn_chunk(q_ref[rows, :].astype(F32), f_ref[rows, :].astype(F32),
                            i_ref[rows, :].astype(F32), lb, [s.astype(BF16) for s in st], causal)
            _, on = _rms_heads(r["o"])
            yb_ref[rows, :] = (on * gnv * (og * _sig(og))).astype(BF16)
            for hd in range(N_HEAD):
                st_ref[cc, hd] = st[hd]
            st = [s * e + _dot(ih, kh, TN)
                  for s, e, ih, kh in zip(st, _heads(r["ecl"]), _heads(r["iib"]), _heads(r["kkb"]))]
        for hd in range(N_HEAD):
            state[hd] = st[hd]

    def col(k):
        return pl.BlockSpec((HG_ROWS, D_MODEL), lambda ci: (ci, k))

    return _grid1_call(body, comm, nb, name="hgrn_fwd",
                       out_shape=(jax.ShapeDtypeStruct((t, D_MODEL), BF16),
                                  jax.ShapeDtypeStruct((t // CHUNK, N_HEAD, HEAD_DIM, HEAD_DIM), F32)),
                       in_specs=[col(2), col(3), col(4), col(5),
                                 pl.BlockSpec((2, D_MODEL), lambda ci: (0, 0)),
                                 pl.BlockSpec((1, D_MODEL), lambda ci: (0, 0))],
                       out_specs=(pl.BlockSpec((HG_ROWS, D_MODEL), lambda ci: (ci, 0)),
                                  pl.BlockSpec((HG_CHUNKS, N_HEAD, HEAD_DIM, HEAD_DIM),
                                               lambda ci: (ci, 0, 0, 0))),
                       scratch=[pltpu.VMEM((N_HEAD, HEAD_DIM, HEAD_DIM), F32)],
                       args=(h, h, h, h, logits, gn))


def _hgrn_chunk_bwd(q, fp, ii, og, dy, gnv, lb, st, dsn, causal, anti):
    stb = [s.astype(BF16) for s in st]
    dsnb = [s.astype(BF16) for s in dsn]
    r = _hgrn_chunk(q, fp, ii, lb, stb, causal)
    rinv, on = _rms_heads(r["o"])
    so = _sig(og)
    sil = og * so
    d_og = dy * on * gnv * (so * (1.0 + og * (1.0 - so)))
    d_on = dy * gnv * sil
    d_ob = jnp.concatenate(
        [ri * (dn - oh * jnp.mean(dn * oh, axis=-1, keepdims=True))
         for ri, dn, oh in zip(rinv, _heads(d_on), _heads(on))], axis=1).astype(BF16)
    d_i, d_qt, d_kt, d_kk, d_st, st_dsn = [], [], [], [], [], []
    ecl = _heads(r["ecl"])
    for hd, (dh, qh, kh, ih, kkh) in enumerate(zip(_heads(d_ob), _heads(r["qtb"]), _heads(r["ktb"]),
                                                   _heads(r["iib"]), _heads(r["kkb"]))):
        d_attn = jnp.where(causal, _dot(dh, ih, NT), 0.0).astype(BF16)
        d_i.append(_dot(r["attn"][hd], dh, TN) + _dot(kkh, dsnb[hd], NT))
        d_qt.append(_dot(d_attn, kh) + _dot(dh, stb[hd]))
        d_kt.append(_dot(d_attn, qh, TN))
        d_kk.append(_dot(ih, dsnb[hd]))
        d_st.append(_dot(dh, qh, TN) + dsn[hd] * ecl[hd])
        st_dsn.append(jnp.sum(st[hd] * dsn[hd], axis=0, keepdims=True))
    d_qt = jnp.concatenate(d_qt, axis=1)
    d_kt = jnp.concatenate(d_kt, axis=1)
    d_kk = jnp.concatenate(d_kk, axis=1)
    kk = r["kk"]
    d_cl = r["ecl"] * jnp.concatenate(st_dsn, axis=1) + jnp.sum(kk * d_kk, axis=0, keepdims=True)
    d_k = (d_kk * r["ecl"] + d_kt) * r["en"]
    d_c = d_qt * r["qtb"].astype(F32) - d_kt * r["ktb"].astype(F32) - d_kk * kk
    rowid = lax.broadcasted_iota(jnp.int32, (CHUNK, D_MODEL), 0)
    d_c = d_c + jnp.where(rowid == CHUNK - 1, d_cl, 0.0)
    d_lf = _tri_cumsum(anti.astype(BF16), d_c)
    d_f = d_lf / r["f"] - d_k
    sg, sq = r["sg"], r["sq"]
    d_q = d_qt * r["ec"] * (sq * (1.0 + q * (1.0 - sq)))
    d_fp = d_f * (1.0 - lb) * sg * (1.0 - sg)
    return (d_q, d_fp, jnp.concatenate(d_i, axis=1), d_og, d_st,
            _colsum8(dy * on * sil), _colsum8(d_f * (1.0 - sg)))


def _hgrn_bwd(h, dyb, st_all, logits, gn, comm=None):
    t = h.shape[0]
    nb = t // HG_ROWS

    def body(q_ref, f_ref, i_ref, og_ref, dyb_ref, st_ref, lg_ref, gn_ref,
             dh1_ref, dh2_ref, dlb_ref, dgn_ref, dstate):
        @pl.when(pl.program_id(0) == 0)
        def _():
            dstate[...] = jnp.zeros_like(dstate)
            dlb_ref[...] = jnp.zeros_like(dlb_ref)
            dgn_ref[...] = jnp.zeros_like(dgn_ref)

        causal, anti = _tri_masks()
        lb = _sig(lg_ref[0:1, :] - lg_ref[1:2, :])
        gnv = gn_ref[...]
        dsn = [dstate[hd] for hd in range(N_HEAD)]
        dgn_acc = jnp.zeros((8, D_MODEL), F32)
        dlb_acc = jnp.zeros((8, D_MODEL), F32)
        for cc in reversed(range(HG_CHUNKS)):
            rows = slice(cc * CHUNK, (cc + 1) * CHUNK)
            d_q, d_fp, d_i, d_og, dsn, dgn_c, dlb_c = _hgrn_chunk_bwd(
                q_ref[rows, :].astype(F32), f_ref[rows, :].astype(F32), i_ref[rows, :].astype(F32),
                og_ref[rows, :].astype(F32), dyb_ref[rows, :], gnv, lb,
                [st_ref[cc, hd] for hd in range(N_HEAD)], dsn, causal, anti)
            dgn_acc = dgn_acc + dgn_c
            dlb_acc = dlb_acc + dlb_c
            dh1_ref[rows, :D_MODEL] = d_q.astype(BF16)
            dh1_ref[rows, D_MODEL:] = d_fp.astype(BF16)
            dh2_ref[rows, :D_MODEL] = d_i.astype(BF16)
            dh2_ref[rows, D_MODEL:] = d_og.astype(BF16)
        dgn_ref[...] += dgn_acc
        dlb_ref[...] += dlb_acc
        for hd in range(N_HEAD):
            dstate[hd] = dsn[hd]

    def col(k):
        return pl.BlockSpec((HG_ROWS, D_MODEL), lambda ci: (nb - 1 - ci, k))

    acc8 = pl.BlockSpec((8, D_MODEL), lambda ci: (0, 0))
    pair = pl.BlockSpec((HG_ROWS, 2 * D_MODEL), lambda ci: (nb - 1 - ci, 0))
    return _grid1_call(body, comm, nb, name="hgrn_bwd",
                       out_shape=(jax.ShapeDtypeStruct((t, 2 * D_MODEL), BF16),
                                  jax.ShapeDtypeStruct((t, 2 * D_MODEL), BF16),
                                  jax.ShapeDtypeStruct((8, D_MODEL), F32),
                                  jax.ShapeDtypeStruct((8, D_MODEL), F32)),
                       in_specs=[col(2), col(3), col(4), col(5),
                                 pl.BlockSpec((HG_ROWS, D_MODEL), lambda ci: (nb - 1 - ci, 0)),
                                 pl.BlockSpec((HG_CHUNKS, N_HEAD, HEAD_DIM, HEAD_DIM),
                                              lambda ci: (nb - 1 - ci, 0, 0, 0)),
                                 pl.BlockSpec((2, D_MODEL), lambda ci: (0, 0)),
                                 pl.BlockSpec((1, D_MODEL), lambda ci: (0, 0))],
                       out_specs=(pair, pair, acc8, acc8),
                       scratch=[pltpu.VMEM((N_HEAD, HEAD_DIM, HEAD_DIM), F32)],
                       args=(h, h, h, h, dyb, st_all, logits, gn))


def _mix_fwd(ya, yb, h, x, wb0, wb1, wo, g1, b1, tm, comm=None):
    t = x.shape[0]

    def body(ya_ref, yb_ref, ga_ref, gb_ref, x_ref, wb0_ref, wb1_ref, wo_ref, g1_ref, b1_ref,
             r1_ref, a_ref, b_ref, m_ref, x1_ref):
        a = _dot(ya_ref[...], wb0_ref[...])
        b = _dot(yb_ref[...], wb1_ref[...])
        m = _sig(ga_ref[...].astype(F32)) * a + _sig(gb_ref[...].astype(F32)) * b
        r1 = ALPHA * x_ref[...] + _dot(m, wo_ref[...])
        xh, _ = _ln_stats(r1)
        r1_ref[...] = r1
        a_ref[...] = a
        b_ref[...] = b
        m_ref[...] = m.astype(BF16)
        x1_ref[...] = (xh * g1_ref[...] + b1_ref[...]).astype(BF16)

    tile = pl.BlockSpec((tm, D_MODEL), lambda i: (i, 0))
    wsp = pl.BlockSpec((D_MODEL, D_MODEL), lambda i: (0, 0))
    vec = pl.BlockSpec((1, D_MODEL), lambda i: (0, 0))
    f32o = jax.ShapeDtypeStruct((t, D_MODEL), F32)
    bfo = jax.ShapeDtypeStruct((t, D_MODEL), BF16)
    return _grid1_call(body, comm, t // tm, name="mix_fwd", out_shape=(f32o, f32o, f32o, bfo, bfo),
                       in_specs=[tile, tile,
                                 pl.BlockSpec((tm, D_MODEL), lambda i: (i, 6)),
                                 pl.BlockSpec((tm, D_MODEL), lambda i: (i, 7)),
                                 tile, wsp, wsp, wsp, vec, vec],
                       out_specs=(tile, tile, tile, tile, tile),
                       scratch=[], args=(ya, yb, h, h, x, wb0, wb1, wo, g1, b1))


def _mix_bwd(dr1, h, a, b, wo, wb0, wb1, tm):
    t = dr1.shape[0]

    def body(dr1_ref, ga_ref, gb_ref, a_ref, b_ref, wo_ref, wb0_ref, wb1_ref,
             da_ref, db_ref, dh3_ref, dya_ref, dyb_ref):
        d_m = _dot(dr1_ref[...], wo_ref[...], NT)
        sa = _sig(ga_ref[...].astype(F32))
        sb = _sig(gb_ref[...].astype(F32))
        d_a = (d_m * sa).astype(BF16)
        d_b = (d_m * sb).astype(BF16)
        da_ref[...] = d_a
        db_ref[...] = d_b
        dh3_ref[:, :D_MODEL] = (d_m * a_ref[...] * sa * (1.0 - sa)).astype(BF16)
        dh3_ref[:, D_MODEL:] = (d_m * b_ref[...] * sb * (1.0 - sb)).astype(BF16)
        dya_ref[...] = _dot(d_a, wb0_ref[...], NT)
        dyb_ref[...] = _dot(d_b, wb1_ref[...], NT)

    tile = pl.BlockSpec((tm, D_MODEL), lambda i: (i, 0))
    wsp = pl.BlockSpec((D_MODEL, D_MODEL), lambda i: (0, 0))
    f32o = jax.ShapeDtypeStruct((t, D_MODEL), F32)
    bfo = jax.ShapeDtypeStruct((t, D_MODEL), BF16)
    return _pc(body, name="mix_bwd",
               out_shape=(bfo, bfo, jax.ShapeDtypeStruct((t, 2 * D_MODEL), BF16), f32o, f32o),
               grid=(t // tm,),
               in_specs=[tile,
                         pl.BlockSpec((tm, D_MODEL), lambda i: (i, 6)),
                         pl.BlockSpec((tm, D_MODEL), lambda i: (i, 7)),
                         tile, tile, wsp, wsp, wsp],
               out_specs=(tile, tile, pl.BlockSpec((tm, 2 * D_MODEL), lambda i: (i, 0)),
                          tile, tile),
               sem=("parallel",))(dr1, h, h, a, b, wo, wb0, wb1)


FF_TILE = 1408
FF_NJ = D_FF // FF_TILE


def _shift_down(v, k):
    return pltpu.roll(v, k, 0)


def _shift_up(v, k):
    return pltpu.roll(v, v.shape[0] - k, 0)


def _conv_gate(ext, cw_ref, cb_ref):
    return (cw_ref[0:1, :] * _shift_down(ext, 2) + cw_ref[1:2, :] * _shift_down(ext, 1)
            + cw_ref[2:3, :] * ext + cb_ref[...])


HALO = 16
FF_PIECES = ((0, 512), (512, 1024), (1024, FF_TILE))


def _ffn_up_act(x1b, wup_st, convw, convb, tm, comm):
    t = x1b.shape[0]
    ni = t // tm
    nth = tm // HALO

    def body(x_ref, xp_ref, wg_ref, wv_ref, cw_ref, cb_ref, h2_ref, act_ref):
        i = pl.program_id(0)
        wg = wg_ref[...]
        gate = _dot(x_ref[...], wg).astype(BF16)
        val = _dot(x_ref[...], wv_ref[...]).astype(BF16)
        prev = (_dot(xp_ref[...], wg) * (i > 0).astype(F32)).astype(BF16)
        h2_ref[0] = gate
        h2_ref[1] = val
        ext = jnp.concatenate([prev.astype(F32), gate.astype(F32)], axis=0)
        gc = _conv_gate(ext, cw_ref, cb_ref)[HALO:, :]
        act_ref[...] = (_gelu(gc) * val.astype(F32)).astype(BF16)

    res = _hosted_call(
        body, comm,
        lambda: (pl.program_id(0) == 0) & (pl.program_id(1) == 0),
        lambda: (pl.program_id(0) == ni - 1) & (pl.program_id(1) == FF_NJ - 1),
        name="ffn_up",
        out_shape=(jax.ShapeDtypeStruct((2, t, D_FF), BF16), jax.ShapeDtypeStruct((t, D_FF), BF16)),
        grid=(ni, FF_NJ),
        in_specs=[pl.BlockSpec((tm, D_MODEL), lambda i, j: (i, 0)),
                  pl.BlockSpec((HALO, D_MODEL), lambda i, j: (jnp.maximum(i * nth - 1, 0), 0)),
                  pl.BlockSpec((None, D_MODEL, FF_TILE), lambda i, j: (j, 0, 0)),
                  pl.BlockSpec((None, D_MODEL, FF_TILE), lambda i, j: (j + FF_NJ, 0, 0)),
                  pl.BlockSpec((3, FF_TILE), lambda i, j: (0, j)),
                  pl.BlockSpec((1, FF_TILE), lambda i, j: (0, j))],
        out_specs=(pl.BlockSpec((2, tm, FF_TILE), lambda i, j: (0, i, j)),
                   pl.BlockSpec((tm, FF_TILE), lambda i, j: (i, j))),
        scratch=[], sem=("arbitrary", "arbitrary"),
        args=(x1b, x1b, wup_st, wup_st, convw, convb))
    return res[0], res[1], res[2:]


def _ffn_act_bwd(h2, dr2, wd, convw, convb, tm):
    t = h2.shape[1]
    nth = tm // HALO
    ni = t // tm
    last_halo = t // HALO - 1
    main_rows = slice(HALO, HALO + tm)

    def body(g_ref, gp_ref, gn_ref, v_ref, vn_ref, dr2_ref, dr2n_ref, wd_ref, cw_ref, cb_ref,
             dh2_ref, dcw_ref, dcb_ref):
        i = pl.program_id(1)

        @pl.when(i == 0)
        def _():
            dcw_ref[...] = jnp.zeros_like(dcw_ref)
            dcb_ref[...] = jnp.zeros_like(dcb_ref)

        zeros = jnp.zeros((HALO, FF_TILE), F32)
        da = _dot(dr2_ref[...], wd_ref[...], NT)
        prev = gp_ref[...].astype(F32) * (i > 0).astype(F32)
        ext = jnp.concatenate([prev, g_ref[...].astype(F32), gn_ref[...].astype(F32)], axis=0)
        vext = jnp.concatenate([zeros, v_ref[...].astype(F32), vn_ref[...].astype(F32)], axis=0)
        dnext = _dot(dr2n_ref[...], wd_ref[...], NT) * (i < ni - 1).astype(F32)
        dext = jnp.concatenate([zeros, da, dnext], axis=0)
        g2 = _shift_down(ext, 2)
        g1 = _shift_down(ext, 1)
        gc = cw_ref[0:1, :] * g2 + cw_ref[1:2, :] * g1 + cw_ref[2:3, :] * ext + cb_ref[...]
        gl, dgl = _gelu_and_grad(gc)
        d_gc = dext * vext * dgl
        d_gate = (cw_ref[2:3, :] * d_gc + cw_ref[1:2, :] * _shift_up(d_gc, 1)
                  + cw_ref[0:1, :] * _shift_up(d_gc, 2))
        dh2_ref[0] = d_gate[main_rows, :].astype(BF16)
        dh2_ref[1] = (da * gl[main_rows, :]).astype(BF16)
        dm = d_gc[main_rows, :]
        s0 = jnp.sum(dm * g2[main_rows, :], axis=0, keepdims=True)
        s1 = jnp.sum(dm * g1[main_rows, :], axis=0, keepdims=True)
        s2 = jnp.sum(dm * ext[main_rows, :], axis=0, keepdims=True)
        rowid = lax.broadcasted_iota(jnp.int32, (8, FF_TILE), 0)
        dcw_ref[...] += jnp.where(rowid == 0, s0, jnp.where(rowid == 1, s1,
                                                            jnp.where(rowid == 2, s2, 0.0)))
        dcb_ref[...] += _colsum8(dm)

    def prev8(part):
        return pl.BlockSpec((None, HALO, FF_TILE), lambda j, i: (part, jnp.maximum(i * nth - 1, 0), j))

    def next8(part):
        return pl.BlockSpec((None, HALO, FF_TILE),
                            lambda j, i: (part, jnp.minimum((i + 1) * nth, last_halo), j))

    def main(part):
        return pl.BlockSpec((None, tm, FF_TILE), lambda j, i: (part, i, j))

    acc = pl.BlockSpec((8, FF_TILE), lambda j, i: (0, j))
    return _pc(body, name="ffn_act_bwd",
               out_shape=(jax.ShapeDtypeStruct((2, t, D_FF), BF16),
                          jax.ShapeDtypeStruct((8, D_FF), F32),
                          jax.ShapeDtypeStruct((8, D_FF), F32)),
               grid=(FF_NJ, ni),
               in_specs=[main(0), prev8(0), next8(0), main(1), next8(1),
                         pl.BlockSpec((tm, D_MODEL), lambda j, i: (i, 0)),
                         pl.BlockSpec((HALO, D_MODEL),
                                      lambda j, i: (jnp.minimum((i + 1) * nth, last_halo), 0)),
                         pl.BlockSpec((FF_TILE, D_MODEL), lambda j, i: (j, 0)),
                         pl.BlockSpec((3, FF_TILE), lambda j, i: (0, j)),
                         pl.BlockSpec((1, FF_TILE), lambda j, i: (0, j))],
               out_specs=(pl.BlockSpec((2, tm, FF_TILE), lambda j, i: (0, i, j)), acc, acc),
               sem=("parallel", "arbitrary"))(h2, h2, h2, h2, h2, dr2, dr2, wd, convw, convb)


def _out_fwd_bwd(act, x1b, r1, p2, tgt, wd, wpg, wpp, g1, b1, g2, b2, tm):
    t = r1.shape[0]

    def body(act_ref, x1b_ref, r1_ref, p_ref, tgt_ref, wd_ref, wpg_ref, wpp_ref,
             g1_ref, b1_ref, g2_ref, b2_ref,
             dr2_ref, dpg_ref, dpp_ref, loss_ref, dg2_ref, db2_ref):
        i = pl.program_id(0)

        @pl.when(i == 0)
        def _():
            loss_ref[...] = jnp.zeros_like(loss_ref)
            dg2_ref[...] = jnp.zeros_like(dg2_ref)
            db2_ref[...] = jnp.zeros_like(db2_ref)

        ffn = _dot(act_ref[...], wd_ref[...])
        pg = _dot(x1b_ref[...], wpg_ref[...])
        pp = _dot(p_ref[...], wpp_ref[...])
        s = _sig(pg)
        xh1, _ = _ln_stats(r1_ref[...])
        x1 = xh1 * g1_ref[...] + b1_ref[...]
        r2 = ALPHA * x1 + ffn + s * pp
        xh2, rstd2 = _ln_stats(r2)
        g2v = g2_ref[...]
        diff = xh2 * g2v + b2_ref[...] - tgt_ref[...]
        part = jnp.sum(jnp.sum(diff * diff, axis=1, keepdims=True), axis=0, keepdims=True)
        loss_ref[...] += jnp.broadcast_to(part * (0.5 / D_MODEL), loss_ref.shape)
        dy = diff * (1.0 / D_MODEL)
        dg2_ref[...] += _colsum8(dy * xh2)
        db2_ref[...] += _colsum8(dy)
        dr2 = _ln_bwd(dy * g2v, xh2, rstd2)
        dr2_ref[...] = dr2
        dpg_ref[...] = (dr2 * pp * s * (1.0 - s)).astype(BF16)
        dpp_ref[...] = (dr2 * s).astype(BF16)

    tile = pl.BlockSpec((tm, D_MODEL), lambda i: (i, 0))
    vec = pl.BlockSpec((1, D_MODEL), lambda i: (0, 0))
    acc8 = pl.BlockSpec((8, D_MODEL), lambda i: (0, 0))
    acc_shape = jax.ShapeDtypeStruct((8, D_MODEL), F32)
    return _pc(body, name="out_fwd_bwd",
               out_shape=(jax.ShapeDtypeStruct((t, D_MODEL), F32),
                          jax.ShapeDtypeStruct((t, D_MODEL), BF16),
                          jax.ShapeDtypeStruct((t, D_MODEL), BF16),
                          acc_shape, acc_shape, acc_shape),
               grid=(t // tm,),
               in_specs=[pl.BlockSpec((tm, D_FF), lambda i: (i, 0)), tile, tile,
                         pl.BlockSpec((tm, PLE_DIM), lambda i: (i, 0)), tile,
                         pl.BlockSpec((D_FF, D_MODEL), lambda i: (0, 0)),
                         pl.BlockSpec((D_MODEL, D_MODEL), lambda i: (0, 0)),
                         pl.BlockSpec((PLE_DIM, D_MODEL), lambda i: (0, 0)),
                         vec, vec, vec, vec],
               out_specs=(tile, tile, tile, acc8, acc8, acc8),
               sem=("arbitrary",))(act, x1b, r1, p2, tgt, wd, wpg, wpp, g1, b1, g2, b2)


def _ffn_bwd(h2, dr2, wd, wup_st, dpg, wpg, r1, g1, convw, convb, tm):
    t = r1.shape[0]
    ni = t // tm
    nth = tm // HALO
    last_halo = t // HALO - 1
    main_rows = slice(HALO, HALO + tm)

    def body(g_ref, gp_ref, gn_ref, v_ref, vn_ref, dr2_ref, dr2n_ref, wd_ref, wug_ref, wuv_ref,
             cw_ref, cb_ref, dpg_ref, wpg_ref, r1_ref, g1_ref,
             dh2_ref, dr1_ref, dcw_ref, dcb_ref, dg1_ref, db1_ref, acc):
        i = pl.program_id(0)
        j = pl.program_id(1)

        @pl.when((i == 0) & (j == 0))
        def _():
            dcw_ref[...] = jnp.zeros_like(dcw_ref)
            dcb_ref[...] = jnp.zeros_like(dcb_ref)
            dg1_ref[...] = jnp.zeros_like(dg1_ref)
            db1_ref[...] = jnp.zeros_like(db1_ref)

        dr2v = dr2_ref[...].astype(BF16)
        dr2n = dr2n_ref[...].astype(BF16)
        first = (i > 0).astype(F32)
        more = (i < ni - 1).astype(F32)
        prod = None
        dcw_parts, dcb_parts = [], []
        for c0, c1 in FF_PIECES:
            pc = slice(c0, c1)
            zeros = jnp.zeros((HALO, c1 - c0), F32)
            da = _dot(dr2v, wd_ref[pc, :], NT)
            dnext = _dot(dr2n, wd_ref[pc, :], NT) * more
            ext = jnp.concatenate([gp_ref[:, pc].astype(F32) * first, g_ref[:, pc].astype(F32),
                                   gn_ref[:, pc].astype(F32)], axis=0)
            vext = jnp.concatenate([zeros, v_ref[:, pc].astype(F32), vn_ref[:, pc].astype(F32)], axis=0)
            dext = jnp.concatenate([zeros, da, dnext], axis=0)
            g2 = _shift_down(ext, 2)
            g1s = _shift_down(ext, 1)
            gc = cw_ref[0:1, pc] * g2 + cw_ref[1:2, pc] * g1s + cw_ref[2:3, pc] * ext + cb_ref[:, pc]
            gl, dgl = _gelu_and_grad(gc)
            d_gc = dext * vext * dgl
            d_gate = (cw_ref[2:3, pc] * d_gc + cw_ref[1:2, pc] * _shift_up(d_gc, 1)
                      + cw_ref[0:1, pc] * _shift_up(d_gc, 2))[main_rows, :].astype(BF16)
            d_val = (da * gl[main_rows, :]).astype(BF16)
            dh2_ref[0, :, pc] = d_gate
            dh2_ref[1, :, pc] = d_val
            dm = d_gc[main_rows, :]
            s0 = jnp.sum(dm * g2[main_rows, :], axis=0, keepdims=True)
            s1 = jnp.sum(dm * g1s[main_rows, :], axis=0, keepdims=True)
            s2 = jnp.sum(dm * ext[main_rows, :], axis=0, keepdims=True)
            rowid = lax.broadcasted_iota(jnp.int32, (8, c1 - c0), 0)
            dcw_parts.append(jnp.where(rowid == 0, s0, jnp.where(rowid == 1, s1,
                                                                 jnp.where(rowid == 2, s2, 0.0))))
            dcb_parts.append(_colsum8(dm))
            part = _dot(d_gate, wug_ref[:, pc], NT) + _dot(d_val, wuv_ref[:, pc], NT)
            prod = part if prod is None else prod + part
        dcw_part = jnp.concatenate(dcw_parts, axis=1)
        dcb_part = jnp.concatenate(dcb_parts, axis=1)
        for jj in range(FF_NJ):
            @pl.when(j == jj)
            def _(jj=jj):
                cols = slice(jj * FF_TILE, (jj + 1) * FF_TILE)
                dcw_ref[:, cols] += dcw_part
                dcb_ref[:, cols] += dcb_part

        @pl.when(j == 0)
        def _():
            acc[...] = prod

        @pl.when(j > 0)
        def _():
            acc[...] += prod

        @pl.when(j == FF_NJ - 1)
        def _():
            d_x1 = acc[...] + _dot(dpg_ref[...], wpg_ref[...], NT) + ALPHA * dr2_ref[...]
            xh, rstd = _ln_stats(r1_ref[...])
            dg1_ref[...] += _colsum8(d_x1 * xh)
            db1_ref[...] += _colsum8(d_x1)
            dr1_ref[...] = _ln_bwd(d_x1 * g1_ref[...], xh, rstd)

    def h2_main(part):
        return pl.BlockSpec((None, tm, FF_TILE), lambda i, j: (part, i, j))

    def h2_prev(part):
        return pl.BlockSpec((None, HALO, FF_TILE), lambda i, j: (part, jnp.maximum(i * nth - 1, 0), j))

    def h2_next(part):
        return pl.BlockSpec((None, HALO, FF_TILE),
                            lambda i, j: (part, jnp.minimum((i + 1) * nth, last_halo), j))

    tile = pl.BlockSpec((tm, D_MODEL), lambda i, j: (i, 0))
    acc8 = pl.BlockSpec((8, D_MODEL), lambda i, j: (0, 0))
    accff = pl.BlockSpec((8, D_FF), lambda i, j: (0, 0))
    acc_shape = jax.ShapeDtypeStruct((8, D_MODEL), F32)
    accff_shape = jax.ShapeDtypeStruct((8, D_FF), F32)
    return _pc(body, name="ffn_bwd",
               out_shape=(jax.ShapeDtypeStruct((2, t, D_FF), BF16),
                          jax.ShapeDtypeStruct((t, D_MODEL), F32),
                          accff_shape, accff_shape, acc_shape, acc_shape),
               grid=(ni, FF_NJ),
               in_specs=[h2_main(0), h2_prev(0), h2_next(0), h2_main(1), h2_next(1),
                         tile,
                         pl.BlockSpec((HALO, D_MODEL), lambda i, j: (jnp.minimum((i + 1) * nth, last_halo), 0)),
                         pl.BlockSpec((FF_TILE, D_MODEL), lambda i, j: (j, 0)),
                         pl.BlockSpec((None, D_MODEL, FF_TILE), lambda i, j: (j, 0, 0)),
                         pl.BlockSpec((None, D_MODEL, FF_TILE), lambda i, j: (j + FF_NJ, 0, 0)),
                         pl.BlockSpec((3, FF_TILE), lambda i, j: (0, j)),
                         pl.BlockSpec((1, FF_TILE), lambda i, j: (0, j)),
                         tile, pl.BlockSpec((D_MODEL, D_MODEL), lambda i, j: (0, 0)),
                         tile, pl.BlockSpec((1, D_MODEL), lambda i, j: (0, 0))],
               out_specs=(pl.BlockSpec((2, tm, FF_TILE), lambda i, j: (0, i, j)),
                          tile, accff, accff, acc8, acc8),
               scratch=[pltpu.VMEM((tm, D_MODEL), F32)],
               sem=("arbitrary", "arbitrary"))(h2, h2, h2, h2, h2, dr2, dr2, wd, wup_st, wup_st,
                                               convw, convb, dpg, wpg, r1, g1)


ANY = pl.BlockSpec(memory_space=pl.ANY)


def _chip_peers():
    x, y, c = lax.axis_index("x"), lax.axis_index("y"), lax.axis_index("c")
    return x, y, c, [(1 - x, y), (x, 1 - y), (1 - x, 1 - y)]


def _gather_comm(halved, whole=()):
    n, nw = len(halved), len(whole)

    def copies(ins, outs, sems):
        ici_send, ici_recv, d2d_send, d2d_recv, own_send, own_recv = sems
        x, y, c, peers = _chip_peers()
        me = 2 * x + y
        sibling = (x, y, 1 - c)
        own, ici, ici_wait, fwd, fwd_wait = [], [], [], [], []
        for ti in range(n + nw):
            src, dst = ins[ti], outs[ti]
            own.append(pltpu.make_async_remote_copy(
                src_ref=src, dst_ref=dst.at[me], send_sem=own_send.at[ti], recv_sem=own_recv.at[ti],
                device_id=sibling, device_id_type=MESH))
            for k, (px, py) in enumerate(peers):
                pk = 2 * px + py
                sem = dict(send_sem=ici_send.at[ti * 3 + k], recv_sem=ici_recv.at[ti * 3 + k],
                           device_id=(px, py, c), device_id_type=MESH)
                if ti < n:
                    ici.append(pltpu.make_async_remote_copy(src_ref=src.at[c], dst_ref=dst.at[me, c], **sem))
                    ici_wait.append(pltpu.make_async_remote_copy(src_ref=src.at[c], dst_ref=dst.at[pk, c], **sem))
                    dsem = dict(send_sem=d2d_send.at[ti * 3 + k], recv_sem=d2d_recv.at[ti * 3 + k],
                                device_id=sibling, device_id_type=MESH)
                    fwd.append(pltpu.make_async_remote_copy(src_ref=dst.at[pk, c], dst_ref=dst.at[pk, c], **dsem))
                    fwd_wait.append(pltpu.make_async_remote_copy(
                        src_ref=dst.at[pk, 1 - c], dst_ref=dst.at[pk, 1 - c], **dsem))
                else:
                    ici.append(pltpu.make_async_remote_copy(src_ref=src, dst_ref=dst.at[me], **sem))
                    ici_wait.append(pltpu.make_async_remote_copy(src_ref=src, dst_ref=dst.at[pk], **sem))
        return own, ici, ici_wait, fwd, fwd_wait

    def start(ins, outs, sems):
        own, ici, _, _, _ = copies(ins, outs, sems)
        for cp in own + ici:
            cp.start()

    def finish(ins, outs, sems):
        own, ici, ici_wait, fwd, fwd_wait = copies(ins, outs, sems)
        for i, cp in enumerate(ici_wait):
            cp.wait_recv()
            if i < len(fwd):
                fwd[i].start()
        for cp in fwd_wait + own:
            cp.wait_recv()
        for cp in own + ici + fwd:
            cp.wait_send()

    srcs = list(halved) + list(whole)
    return _Comm(srcs, [jax.ShapeDtypeStruct((N_CHIP,) + s.shape, s.dtype) for s in srcs],
                 [pltpu.SemaphoreType.DMA((3 * (n + nw),)), pltpu.SemaphoreType.DMA((3 * (n + nw),)),
                  pltpu.SemaphoreType.DMA((max(3 * n, 1),)), pltpu.SemaphoreType.DMA((max(3 * n, 1),)),
                  pltpu.SemaphoreType.DMA((n + nw,)), pltpu.SemaphoreType.DMA((n + nw,))],
                 start, finish)


def _sibling_exchange_comm(grads):
    n = len(grads)

    def copies(ins, outs, sems):
        send_sems, recv_sems = sems
        x, y, c = lax.axis_index("x"), lax.axis_index("y"), lax.axis_index("c")
        res = []
        for ti in range(n):
            half = ins[ti].shape[1] // 2
            res.append(pltpu.make_async_remote_copy(
                src_ref=ins[ti].at[:, pl.ds(pl.multiple_of((1 - c) * half, 16), half), :],
                dst_ref=outs[ti],
                send_sem=send_sems.at[ti], recv_sem=recv_sems.at[ti],
                device_id=(x, y, 1 - c), device_id_type=MESH))
        return res

    def start(ins, outs, sems):
        for cp in copies(ins, outs, sems):
            cp.start()

    def finish(ins, outs, sems):
        for cp in copies(ins, outs, sems):
            cp.wait()

    return _Comm(grads, [jax.ShapeDtypeStruct((N_CHIP, g.shape[1] // 2, g.shape[2]), g.dtype) for g in grads],
                 [pltpu.SemaphoreType.DMA((n,)), pltpu.SemaphoreType.DMA((n,))], start, finish)


def _in_proj_gathering(x2b, own, chip, tm, comm):
    t = x2b.shape[0]
    ni = t // tm
    half, cols = own.shape[1], own.shape[2]
    nci, nco = len(comm.ins), len(comm.out_shapes)

    def body(chip_ref, x_ref, own_ref, own_hbm, *rest):
        c_in = rest[:nci]
        h_ref, win_out = rest[nci:nci + 2]
        c_out = rest[nci + 2:nci + 2 + nco]
        w_scr, ici_send, ici_recv, d2d_send, d2d_recv, own_sems, ld_sems = rest[nci + 2 + nco:nci + 9 + nco]
        c_sem = rest[nci + 9 + nco:]
        s, i = pl.program_id(0), pl.program_id(1)
        x, y, c, peers = _chip_peers()
        me = 2 * x + y
        sibling = (x, y, 1 - c)

        def ici(k, slot):
            px, py = peers[k]
            return pltpu.make_async_remote_copy(
                src_ref=own_hbm.at[c], dst_ref=win_out.at[slot, c],
                send_sem=ici_send.at[k], recv_sem=ici_recv.at[k],
                device_id=(px, py, c), device_id_type=MESH)

        def forward(k, core):
            pk = 2 * peers[k][0] + peers[k][1]
            return pltpu.make_async_remote_copy(
                src_ref=win_out.at[pk, core], dst_ref=win_out.at[pk, core],
                send_sem=d2d_send.at[k], recv_sem=d2d_recv.at[k],
                device_id=sibling, device_id_type=MESH)

        place_own = pltpu.make_async_remote_copy(
            src_ref=own_hbm, dst_ref=win_out.at[me], send_sem=own_sems.at[0], recv_sem=own_sems.at[1],
            device_id=sibling, device_id_type=MESH)

        @pl.when((s == 0) & (i == 0))
        def _():
            for k in range(3):
                ici(k, me).start()
            place_own.start()

        @pl.when(s == 0)
        def _():
            xv = x_ref[...]
            h_ref[...] = (_dot(xv[:, :half], own_ref[0]) + _dot(xv[:, half:], own_ref[1])).astype(BF16)

        for k in range(3):
            @pl.when((s == k + 1) & (i == 0))
            def _(k=k):
                pk = 2 * peers[k][0] + peers[k][1]
                ici(k, pk).wait_recv()
                forward(k, c).start()
                forward(k, 1 - c).wait_recv()
                loads = [pltpu.make_async_copy(win_out.at[pk, hh], w_scr.at[hh], ld_sems.at[hh])
                         for hh in range(2)]
                for ld in loads:
                    ld.start()
                for ld in loads:
                    ld.wait()
                if k == 1:
                    comm.start(c_in, c_out, c_sem)

        @pl.when(s > 0)
        def _():
            xv = x_ref[...]
            h_ref[...] = (_dot(xv[:, :half], w_scr[0]) + _dot(xv[:, half:], w_scr[1])).astype(BF16)

        @pl.when((s == N_CHIP - 1) & (i == ni - 1))
        def _():
            place_own.wait()
            for k in range(3):
                ici(k, me).wait_send()
                forward(k, c).wait_send()
            comm.finish(c_in, c_out, c_sem)

    def shard_col(s, me):
        return jnp.where(s == 0, me, me ^ jnp.where(s == 1, 2, jnp.where(s == 2, 1, 3)))

    res = _pc(body, name="in_proj",
              out_shape=(jax.ShapeDtypeStruct((t, N_CHIP * cols), BF16),
                         jax.ShapeDtypeStruct((N_CHIP,) + own.shape, own.dtype)) + tuple(comm.out_shapes),
              grid=(N_CHIP, ni), nsp=1,
              in_specs=[pl.BlockSpec((tm, 2 * half), lambda s, i, chip_ref: (i, 0)),
                        pl.BlockSpec(own.shape, lambda s, i, chip_ref: (0, 0, 0)),
                        ANY] + [ANY] * nci,
              out_specs=(pl.BlockSpec((tm, cols), lambda s, i, chip_ref: (i, shard_col(s, chip_ref[0]))),
                         ANY) + tuple([ANY] * nco),
              scratch=[pltpu.VMEM(own.shape, own.dtype),
                       pltpu.SemaphoreType.DMA((3,)), pltpu.SemaphoreType.DMA((3,)),
                       pltpu.SemaphoreType.DMA((3,)), pltpu.SemaphoreType.DMA((3,)),
                       pltpu.SemaphoreType.DMA((2,)), pltpu.SemaphoreType.DMA((2,))] + comm.sems,
              sem=("arbitrary", "arbitrary"))(chip, x2b, own, own, *comm.ins)
    return res[0], res[1], res[2:]


def _rs_add_halves(name, grad, recv, core):
    _, r, cdim = grad.shape
    half = r // 2
    tr = _row_tile(half, cdim, mult=16)
    nr = half // tr

    def body(c_ref, g_ref, r_ref, o_ref):
        o_ref[...] = (g_ref[...].astype(F32) + r_ref[...].astype(F32)).astype(BF16)

    return _pc(body, name=name, out_shape=jax.ShapeDtypeStruct((N_CHIP, half, cdim), BF16),
               grid=(N_CHIP, nr), nsp=1,
               in_specs=[pl.BlockSpec((None, tr, cdim), lambda j, i, c_ref: (j, c_ref[0] * nr + i, 0)),
                         pl.BlockSpec((None, tr, cdim), lambda j, i, c_ref: (j, i, 0))],
               out_specs=pl.BlockSpec((None, tr, cdim), lambda j, i, c_ref: (j, i, 0)),
               sem=("parallel", "parallel"))(core, grad, recv)


def _chip_exchange_comm(parts):
    n = len(parts)

    def copies(ins, outs, sems):
        send_sems, recv_sems = sems
        x, y, c, peers = _chip_peers()
        return [pltpu.make_async_remote_copy(
            src_ref=ins[ti].at[2 * px + py], dst_ref=outs[ti].at[k],
            send_sem=send_sems.at[ti * 3 + k], recv_sem=recv_sems.at[ti * 3 + k],
            device_id=(px, py, c), device_id_type=MESH)
            for ti in range(n) for k, (px, py) in enumerate(peers)]

    def start(ins, outs, sems):
        for cp in copies(ins, outs, sems):
            cp.start()

    def finish(ins, outs, sems):
        for cp in copies(ins, outs, sems):
            cp.wait()

    return _Comm(parts, [jax.ShapeDtypeStruct((3,) + p.shape[1:], p.dtype) for p in parts],
                 [pltpu.SemaphoreType.DMA((3 * n,)), pltpu.SemaphoreType.DMA((3 * n,))], start, finish)


def _rs_sum_chips(name, part, recv, chip):
    _, half, cdim = recv.shape
    tr = _row_tile(half, cdim, mult=16)

    def body(chip_ref, p_ref, r_ref, o_ref):
        o_ref[...] = ((p_ref[...].astype(F32) + r_ref[0].astype(F32)) + r_ref[1].astype(F32)
                      ) + r_ref[2].astype(F32)

    return _pc(body, name=name, out_shape=jax.ShapeDtypeStruct((half, cdim), F32),
               grid=(half // tr,), nsp=1,
               in_specs=[pl.BlockSpec((None, tr, cdim), lambda i, chip_ref: (chip_ref[0], i, 0)),
                         pl.BlockSpec((3, tr, cdim), lambda i, chip_ref: (0, i, 0))],
               out_specs=pl.BlockSpec((tr, cdim), lambda i, chip_ref: (i, 0)),
               sem=("parallel",))(chip, part, recv)


def _rs_send_halves(halves):
    n = len(halves)

    def body(*refs):
        ins, outs = refs[:n], refs[n:2 * n]
        send_sems, recv_sems = refs[2 * n:]
        x, y, c = lax.axis_index("x"), lax.axis_index("y"), lax.axis_index("c")
        sends = []
        for ti in range(n):
            cp = pltpu.make_async_remote_copy(
                src_ref=ins[ti], dst_ref=outs[ti],
                send_sem=send_sems.at[ti], recv_sem=recv_sems.at[ti],
                device_id=(x, y, 1 - c), device_id_type=MESH)
            cp.start()
            sends.append(cp)
        for cp in sends:
            cp.wait()

    return _pc(body, name="rs_send_halves",
               out_shape=tuple(jax.ShapeDtypeStruct(hv.shape, hv.dtype) for hv in halves),
               in_specs=[ANY] * n, out_specs=tuple([ANY] * n),
               scratch=[pltpu.SemaphoreType.DMA((n,)), pltpu.SemaphoreType.DMA((n,))])(*halves)


def _adamw_rows(name, mine, theirs, w, m, v, core):
    half, cdim = mine.shape
    tr = _row_tile(half, cdim, budget=1 << 19)
    nrh = half // tr

    def body(c_ref, mine_ref, theirs_ref, w_ref, m_ref, v_ref, g_ref, d_ref, m2_ref, v2_ref):
        is_mine = (pl.program_id(0) // nrh) == c_ref[0]
        g = jnp.where(is_mine, mine_ref[...], theirs_ref[...])
        d, m2, v2 = _adamw(w_ref[...], g, m_ref[...], v_ref[...])
        g_ref[...] = g
        d_ref[...] = d
        m2_ref[...] = m2
        v2_ref[...] = v2

    htile = pl.BlockSpec((tr, cdim), lambda i, c_ref: (i % nrh, 0))
    tile = pl.BlockSpec((tr, cdim), lambda i, c_ref: (i, 0))
    shp = jax.ShapeDtypeStruct((2 * half, cdim), F32)
    return _pc(body, name=name, out_shape=(shp, shp, shp, shp), grid=(2 * nrh,), nsp=1,
               in_specs=[htile, htile, tile, tile, tile], out_specs=(tile, tile, tile, tile),
               sem=("parallel",))(core, mine, theirs, w, m, v)


def _adamw_whole(name, g, w, m, v):
    def body(g_ref, w_ref, m_ref, v_ref, d_ref, m2_ref, v2_ref):
        d, m2, v2 = _adamw(w_ref[...], g_ref[...], m_ref[...], v_ref[...])
        d_ref[...] = d
        m2_ref[...] = m2
        v2_ref[...] = v2

    shp = jax.ShapeDtypeStruct(g.shape, F32)
    return _pc(body, name=name, out_shape=(shp, shp, shp))(g, w, m, v)


SMALL_LAYOUT = (
    ("sgu_w_s", 1024, 1, 0),
    ("sgu_b_s", 8, 1, 1024),
    ("sgu_norm_g", 1, 0, 0),
    ("sgu_norm_b", 1, 0, 1),
    ("hgrn_norm_g", 1, 0, 3),
    ("ln1_g", 1, 0, 4),
    ("ln1_b", 1, 0, 5),
    ("ffn_conv_b", 1, 2, 3),
    ("ln2_g", 1, 0, 6),
    ("ln2_b", 1, 0, 7),
)
LB_ROW = 2
LOSS_ROW = 8
PACK_SHAPES = ((16, D_MODEL), (N_GROUP * 128 + 8, 128), (8, D_FF))


def _small_allreduce_adamw(rows1024, dws, dbs, dcw, dcb, logits, m_logits, v_logits,
                           small_w, small_m, small_v):
    ns = len(SMALL_LAYOUT)
    nr = len(rows1024)
    nb = len(PACK_SHAPES)

    def body(*refs):
        row_refs = refs[:nr]
        dws_ref, dbs_ref, dcw_ref, dcb_ref, lg_ref, mlg_ref, vlg_ref = refs[nr:nr + 7]
        pos = nr + 7
        w_refs = refs[pos:pos + ns]
        m_refs = refs[pos + ns:pos + 2 * ns]
        v_refs = refs[pos + 2 * ns:pos + 3 * ns]
        pos += 3 * ns
        loss_ref, dcw_out = refs[pos:pos + 2]
        lg_outs = refs[pos + 2:pos + 6]
        pos += 6
        outs = refs[pos:pos + 4 * ns]
        pos += 4 * ns
        pack = refs[pos:pos + nb]
        sib = refs[pos + nb:pos + 2 * nb]
        gath = refs[pos + 2 * nb:pos + 3 * nb]
        d2d_send, d2d_recv, ici_send, ici_recv = refs[pos + 3 * nb:]

        x, y, c, peers = _chip_peers()
        me = 2 * x + y
        sibling = (x, y, 1 - c)

        pack[0][...] = jnp.zeros(PACK_SHAPES[0], F32)
        for k in range(nr):
            pack[0][k:k + 1, :] = row_refs[k][0:1, :]
        pack[1][0:N_GROUP * 128, :] = dws_ref[...]
        pack[1][N_GROUP * 128:, :] = dbs_ref[...]
        pack[2][...] = jnp.zeros(PACK_SHAPES[2], F32)
        pack[2][0:3, :] = dcw_ref[0:3, :]
        pack[2][3:4, :] = dcb_ref[0:1, :]

        d2d = [pltpu.make_async_remote_copy(
            src_ref=pack[b], dst_ref=sib[b], send_sem=d2d_send.at[b], recv_sem=d2d_recv.at[b],
            device_id=sibling, device_id_type=MESH) for b in range(nb)]
        for cp in d2d:
            cp.start()
        for cp in d2d:
            cp.wait()
        for b in range(nb):
            gath[b][me] = pack[b][...] + sib[b][...]

        ici, ici_wait = [], []
        for b in range(nb):
            for k, (px, py) in enumerate(peers):
                sem = dict(send_sem=ici_send.at[b * 3 + k], recv_sem=ici_recv.at[b * 3 + k],
                           device_id=(px, py, c), device_id_type=MESH)
                ici.append(pltpu.make_async_remote_copy(src_ref=gath[b].at[me], dst_ref=gath[b].at[me], **sem))
                ici_wait.append(pltpu.make_async_remote_copy(
                    src_ref=gath[b].at[me], dst_ref=gath[b].at[2 * px + py], **sem))
        for cp in ici:
            cp.start()
        for cp in ici_wait:
            cp.wait_recv()
        for cp in ici:
            cp.wait_send()

        tot = pack
        for b in range(nb):
            tot[b][...] = ((gath[b][0] + gath[b][1]) + gath[b][2]) + gath[b][3]

        loss_ref[...] = tot[0][LOSS_ROW:LOSS_ROW + 1, :]
        dcw_out[...] = tot[2][...]
        lb = _sig(lg_ref[0:1, :] - lg_ref[1:2, :])
        d0 = tot[0][LB_ROW:LB_ROW + 1, :] * lb * (1.0 - lb)
        rowid = lax.broadcasted_iota(jnp.int32, (2, D_MODEL), 0)
        g_lg = jnp.where(rowid == 0, d0, -d0)
        dl, ml, vl = _adamw(lg_ref[...], g_lg, mlg_ref[...], vlg_ref[...])
        lg_outs[0][...] = g_lg
        lg_outs[1][...] = dl
        lg_outs[2][...] = ml
        lg_outs[3][...] = vl
        for si, (_, rows, b, r0) in enumerate(SMALL_LAYOUT):
            g = tot[b][r0:r0 + rows, :]
            dl, ml, vl = _adamw(w_refs[si][...], g, m_refs[si][...], v_refs[si][...])
            outs[4 * si][...] = g
            outs[4 * si + 1][...] = dl
            outs[4 * si + 2][...] = ml
            outs[4 * si + 3][...] = vl

    shapes = [jax.ShapeDtypeStruct((1, D_MODEL), F32), jax.ShapeDtypeStruct((8, D_FF), F32)]
    shapes += [jax.ShapeDtypeStruct((2, D_MODEL), F32)] * 4
    for w in small_w:
        shapes += [jax.ShapeDtypeStruct(w.shape, F32)] * 4
    scratch = [pltpu.VMEM(shp, F32) for shp in PACK_SHAPES]
    scratch += [pltpu.VMEM(shp, F32) for shp in PACK_SHAPES]
    scratch += [pltpu.VMEM((N_CHIP,) + shp, F32) for shp in PACK_SHAPES]
    scratch += [pltpu.SemaphoreType.DMA((nb,)), pltpu.SemaphoreType.DMA((nb,)),
                pltpu.SemaphoreType.DMA((3 * nb,)), pltpu.SemaphoreType.DMA((3 * nb,))]
    vm = pl.BlockSpec(memory_space=pltpu.VMEM)
    n_in = nr + 7 + 3 * ns
    res = _pc(body, name="small_allreduce_adamw", out_shape=tuple(shapes),
              in_specs=[vm] * n_in, out_specs=tuple([vm] * len(shapes)),
              scratch=scratch)(*rows1024, dws, dbs, dcw, dcb, logits, m_logits, v_logits,
                               *small_w, *small_m, *small_v)
    return res[0], res[1], res[2:6], res[6:]


def kernel(x, p, w_in, sgu_w_s, sgu_b_s, sgu_norm_g, sgu_norm_b, hgrn_lb_logits, hgrn_norm_g, w_branch, w_out, ln1_g, ln1_b, ffn_w_up, ffn_conv_w, ffn_conv_b, ffn_w_down, ln2_g, ln2_b, ple_w_proj, ple_w_gate, loss_target, m_w_in, m_sgu_w_s, m_sgu_b_s, m_sgu_norm_g, m_sgu_norm_b, m_hgrn_lb_logits, m_hgrn_norm_g, m_w_branch, m_w_out, m_ln1_g, m_ln1_b, m_ffn_w_up, m_ffn_conv_w, m_ffn_conv_b, m_ffn_w_down, m_ln2_g, m_ln2_b, m_ple_w_proj, m_ple_w_gate, v_w_in, v_sgu_w_s, v_sgu_b_s, v_sgu_norm_g, v_sgu_norm_b, v_hgrn_lb_logits, v_hgrn_norm_g, v_w_branch, v_w_out, v_ln1_g, v_ln1_b, v_ffn_w_up, v_ffn_conv_w, v_ffn_conv_b, v_ffn_w_down, v_ln2_g, v_ln2_b, v_ple_w_proj, v_ple_w_gate):
    t = x.shape[1]
    x2 = x.reshape(t, D_MODEL)
    x2b = x2.astype(BF16)
    p2 = p.reshape(t, PLE_DIM)
    tgt = loss_target.reshape(t, D_MODEL)
    core = lax.axis_index("c").astype(jnp.int32).reshape(1)
    chip_id = (2 * lax.axis_index("x") + lax.axis_index("y")).astype(jnp.int32).reshape(1)

    big_w = [w_in[0], w_branch[0, 0], w_branch[0, 1], w_out[0], ffn_w_up[0], ffn_w_down[0],
             ple_w_proj[0], ple_w_gate[0]]
    big_m = [m_w_in[0], m_w_branch[0, 0], m_w_branch[0, 1], m_w_out[0], m_ffn_w_up[0],
             m_ffn_w_down[0], m_ple_w_proj[0], m_ple_w_gate[0]]
    big_v = [v_w_in[0], v_w_branch[0, 0], v_w_branch[0, 1], v_w_out[0], v_ffn_w_up[0],
             v_ffn_w_down[0], v_ple_w_proj[0], v_ple_w_gate[0]]
    def halves_of(i):
        w = big_w[i]
        return w.astype(BF16).reshape(2, w.shape[0] // 2, w.shape[1])

    def stacked(g, i):
        return g.reshape(N_CHIP, big_w[i].shape[0], big_w[i].shape[1])


    cid = jnp.arange(SGU_BLOCK) // CHUNK
    maskf = (cid[:, None] >= cid[None, :]).astype(F32)
    ws_masked = sgu_w_s[0] * maskf[None]
    wm = ws_masked.astype(BF16)
    wmt = jnp.transpose(ws_masked, (0, 2, 1)).astype(BF16)
    bsb = jnp.broadcast_to(sgu_b_s[0][:, :, None], (N_GROUP, SGU_BLOCK, 128))

    h, win_g, (wup_g,) = _in_proj_gathering(x2b, halves_of(0), chip_id, 512, _gather_comm([halves_of(4)]))
    win_st = stacked(win_g, 0)
    wup_st = stacked(wup_g, 4)
    ya, _ = _sgu_fwd(h, wm, bsb, sgu_norm_g, sgu_norm_b)
    (yb, st_all), mix_g = _hgrn_fwd(h, hgrn_lb_logits, hgrn_norm_g,
                                     comm=_gather_comm([halves_of(i) for i in (1, 2, 3)], [ffn_conv_w[0]]))
    wb0, wb1, wo = [stacked(g, i).reshape(D_MODEL, D_MODEL) for g, i in zip(mix_g[:3], (1, 2, 3))]
    convw = jnp.transpose(mix_g[3], (1, 0, 2)).reshape(3, D_FF)
    (r1, a_br, b_br, m_bf, x1b), _ = _mix_fwd(ya, yb, h, x2, wb0, wb1, wo, ln1_g, ln1_b, 256)
    h2, act, out_g = _ffn_up_act(x1b, wup_st, convw, ffn_conv_b, 512,
                                 _gather_comm([halves_of(i) for i in (5, 6, 7)]))
    wd = stacked(out_g[0], 5).reshape(D_FF, D_MODEL)
    wpp = jnp.transpose(stacked(out_g[1], 6), (1, 0, 2)).reshape(PLE_DIM, D_MODEL)
    wpg = stacked(out_g[2], 7).reshape(D_MODEL, D_MODEL)
    dr2, dpg, dpp, loss_acc, dg2, db2 = _out_fwd_bwd(
        act, x1b, r1, p2, tgt, wd, wpg, wpp, ln1_g, ln1_b, ln2_g, ln2_b, 256)

    dh2, dr1, dcw, dcb, dg1, db1 = _ffn_bwd(h2, dr2, wd, wup_st, dpg, wpg, r1, ln1_g, convw, ffn_conv_b, 256)
    d_wd = _mm_tn("ffn_down_wgrad", act, dr2, FF_TILE, 512)
    d_wpg = _mm_tn("ple_gate_wgrad", x1b, dpg, 512, D_MODEL)
    d_wpp_st = _mm_tn("ple_proj_wgrad", p2, dpp, PLE_DIM, PLE_DIM, stacked=True)
    d_wup_st = _mm("ffn_up_wgrad", x1b, dh2, TN, (2, N_CHIP, 1),
                   pl.BlockSpec((t, 512), lambda i, j, k: (0, i)),
                   pl.BlockSpec((None, t, FF_TILE), lambda i, j, k: (j // FF_NJ, 0, j % FF_NJ)),
                   jax.ShapeDtypeStruct((N_CHIP, D_MODEL, FF_TILE), BF16),
                   pl.BlockSpec((None, 512, FF_TILE), lambda i, j, k: (j, i, 0)))
    da_bf, db_bf, dh3, dya, dyb = _mix_bwd(dr1, h, a_br, b_br, wo, wb0, wb1, 256)
    d_wo = _mm_tn("out_proj_wgrad", m_bf, dr1, 512, 512)
    d_wb0 = _mm_tn("branch0_wgrad", ya, da_bf, 512, D_MODEL)
    d_wb1 = _mm_tn("branch1_wgrad", yb, db_bf, 512, D_MODEL)
    grads_1 = [d_wb0.reshape(4, 256, D_MODEL), d_wb1.reshape(4, 256, D_MODEL),
               d_wo.reshape(4, 256, D_MODEL), d_wup_st, d_wd.reshape(4, D_FF // 4, D_MODEL),
               d_wpp_st, d_wpg.reshape(4, 256, D_MODEL)]
    (dh0, dws, dbs, dgv, dbv), recv_a1 = _sgu_bwd(h, dya, wm, wmt, bsb, sgu_norm_g, sgu_norm_b, maskf,
                                                  comm=_sibling_exchange_comm(grads_1))
    parts_1 = [_rs_add_halves("rs_add_halves%d" % (i + 1), g, r, core)
               for i, (g, r) in enumerate(zip(grads_1, recv_a1))]
    (dh1, dh2h, dlb, dgn), recv_b1 = _hgrn_bwd(h, dyb, st_all, hgrn_lb_logits, hgrn_norm_g,
                                                comm=_chip_exchange_comm(parts_1))
    dh_parts = [dh0, dh1, dh2h, dh3]
    d_win = [_mm_tn("in_proj_wgrad%d" % j, x2b, dh_parts[j], 512, D_MODEL) for j in range(4)]

    grads_0 = [jnp.stack(d_win)]
    recv_a0 = _run_comm("rs_sibling_exchange0", _sibling_exchange_comm(grads_0))
    parts_0 = [_rs_add_halves("rs_add_halves0", grads_0[0], recv_a0[0], core)]
    gx, recv_b0 = _in_proj_xgrad(dh_parts, win_st, dr1, 512, comm=_chip_exchange_comm(parts_0))
    parts = parts_0 + parts_1
    recv_b = list(recv_b0) + list(recv_b1)
    halves = [_rs_sum_chips("rs_sum_chips%d" % i, pt, r, chip_id)
              for i, (pt, r) in enumerate(zip(parts, recv_b))]
    theirs = _rs_send_halves(halves)
    big_out = [_adamw_rows("adamw_big%d" % i, halves[i], theirs[i], big_w[i], big_m[i], big_v[i], core)
               for i in range(len(halves))]

    small_in = dict(sgu_w_s=(sgu_w_s, m_sgu_w_s, v_sgu_w_s), sgu_b_s=(sgu_b_s, m_sgu_b_s, v_sgu_b_s),
                    sgu_norm_g=(sgu_norm_g, m_sgu_norm_g, v_sgu_norm_g),
                    sgu_norm_b=(sgu_norm_b, m_sgu_norm_b, v_sgu_norm_b),
                    hgrn_norm_g=(hgrn_norm_g, m_hgrn_norm_g, v_hgrn_norm_g),
                    ln1_g=(ln1_g, m_ln1_g, v_ln1_g), ln1_b=(ln1_b, m_ln1_b, v_ln1_b),
                    ffn_conv_b=(ffn_conv_b, m_ffn_conv_b, v_ffn_conv_b),
                    ln2_g=(ln2_g, m_ln2_g, v_ln2_g), ln2_b=(ln2_b, m_ln2_b, v_ln2_b))

    def flat(name, arr):
        rows = dict((n, r) for n, r, _, _ in SMALL_LAYOUT)[name]
        return arr.reshape(rows, arr.size // rows)

    names = [n for n, _, _, _ in SMALL_LAYOUT]
    sw = [flat(n, small_in[n][0]) for n in names]
    sm = [flat(n, small_in[n][1]) for n in names]
    sv = [flat(n, small_in[n][2]) for n in names]
    loss_rows, dcw_tot, lg_out, small_out = _small_allreduce_adamw(
        [dgv, dbv, dlb, dgn, dg1, db1, dg2, db2, loss_acc], dws.reshape(N_GROUP * 128, 128), dbs, dcw, dcb,
        hgrn_lb_logits, m_hgrn_lb_logits, v_hgrn_lb_logits, sw, sm, sv)
    loss = loss_rows[0, 0]

    chip = 2 * lax.axis_index("x") + lax.axis_index("y")
    g_cw = lax.dynamic_slice(dcw_tot, (0, chip * (D_FF // 4)), (3, D_FF // 4))
    cw_out = _adamw_whole("adamw_conv_w", g_cw, ffn_conv_w[0], m_ffn_conv_w[0], v_ffn_conv_w[0])

    res = {}
    for si, n in enumerate(names):
        shp = small_in[n][0].shape
        res[n] = tuple(small_out[4 * si + k].reshape(shp) for k in range(4))
    res["hgrn_lb_logits"] = tuple(lg_out)
    res["ffn_conv_w"] = (g_cw[None],) + tuple(o[None] for o in cw_out)

    def big(i):
        return tuple(big_out[i])

    res["w_in"] = tuple(o[None] for o in big(0))
    res["w_branch"] = tuple(jnp.stack([o0, o1])[None] for o0, o1 in zip(big(1), big(2)))
    res["w_out"] = tuple(o[None] for o in big(3))
    res["ffn_w_up"] = tuple(o[None] for o in big(4))
    res["ffn_w_down"] = tuple(o[None] for o in big(5))
    res["ple_w_proj"] = tuple(o[None] for o in big(6))
    res["ple_w_gate"] = tuple(o[None] for o in big(7))

    order = ["w_in", "sgu_w_s", "sgu_b_s", "sgu_norm_g", "sgu_norm_b", "hgrn_lb_logits",
             "hgrn_norm_g", "w_branch", "w_out", "ln1_g", "ln1_b", "ffn_w_up", "ffn_conv_w",
             "ffn_conv_b", "ffn_w_down", "ln2_g", "ln2_b", "ple_w_proj", "ple_w_gate"]
    outs = [loss, gx.reshape(1, t, D_MODEL)]
    for k in range(4):
        outs += [res[n][k] for n in order]
    return tuple(outs)
```

```python
import functools

import jax
import jax.numpy as jnp
from jax import lax
from jax.experimental import pallas as pl
from jax.experimental.pallas import tpu as pltpu

F32 = jnp.float32
BF16 = jnp.bfloat16
HIGHEST = lax.Precision.HIGHEST
MESH = pl.DeviceIdType.MESH

D_MODEL = 1024
CHUNK = 64
SGU_BLOCK = 128
N_GROUP = 8
N_HEAD = 8
HEAD_DIM = 128
D_FF = 2816
PLE_DIM = 256
IN_COLS = 8192
LN_EPS = 1e-5
RMS_EPS = 1e-6
ALPHA = 2.0 ** 0.25
N_CHIP = 4
N_DEV = 8

ADAM_LR = 0.001
ADAM_B1 = 0.9
ADAM_B2 = 0.999
ADAM_EPS = 1e-08
ADAM_WD = 0.01
ADAM_STEP = 10

VMEM_LIMIT = 56 * 1024 * 1024

NN = (((1,), (0,)), ((), ()))
NT = (((1,), (1,)), ((), ()))
TN = (((0,), (0,)), ((), ()))


def _pc(body, *, name, out_shape, grid=None, in_specs=None, out_specs=None, scratch=(),
        sem=None, nsp=0, vmem=VMEM_LIMIT):
    params = dict(vmem_limit_bytes=vmem)
    if sem is not None:
        params["dimension_semantics"] = sem
    kw = dict(name=name, out_shape=out_shape, compiler_params=pltpu.CompilerParams(**params))
    if nsp:
        kw["grid_spec"] = pltpu.PrefetchScalarGridSpec(
            num_scalar_prefetch=nsp, grid=grid, in_specs=in_specs, out_specs=out_specs,
            scratch_shapes=list(scratch))
    else:
        if grid is not None:
            kw["grid"] = grid
        if in_specs is not None:
            kw["in_specs"] = in_specs
            kw["out_specs"] = out_specs
        kw["scratch_shapes"] = list(scratch)
    return pl.pallas_call(body, **kw)


def _dot(a, b, dims=NN):
    return lax.dot_general(a.astype(BF16), b.astype(BF16), dims, preferred_element_type=F32)


def _dot32(a, b, dims=NN):
    return lax.dot_general(a, b, dims, precision=HIGHEST, preferred_element_type=F32)


def _sig(x):
    return 1.0 / (1.0 + jnp.exp(-x))


_GC = 0.7978845608028654
_GA = 0.044715


def _gelu(x):
    return 0.5 * x * (1.0 + jnp.tanh(_GC * (x + _GA * x * x * x)))


def _gelu_and_grad(x):
    t = jnp.tanh(_GC * (x + _GA * x * x * x))
    g = 0.5 * x * (1.0 + t)
    dg = 0.5 * (1.0 + t) + 0.5 * x * (1.0 - t * t) * _GC * (1.0 + 3.0 * _GA * x * x)
    return g, dg


def _ln_stats(r):
    mu = jnp.mean(r, axis=-1, keepdims=True)
    xc = r - mu
    var = jnp.mean(xc * xc, axis=-1, keepdims=True)
    rstd = lax.rsqrt(var + LN_EPS)
    return xc * rstd, rstd


def _ln_bwd(dxh, xh, rstd):
    m1 = jnp.mean(dxh, axis=-1, keepdims=True)
    m2 = jnp.mean(dxh * xh, axis=-1, keepdims=True)
    return rstd * (dxh - m1 - xh * m2)


def _colsum8(v):
    return jnp.broadcast_to(jnp.sum(v, axis=0, keepdims=True), (8, v.shape[1]))


def _adamw(w, g, m, v):
    m2 = ADAM_B1 * m + (1.0 - ADAM_B1) * g
    v2 = ADAM_B2 * v + (1.0 - ADAM_B2) * (g * g)
    m_hat = m2 / (1.0 - ADAM_B1 ** ADAM_STEP)
    v_hat = v2 / (1.0 - ADAM_B2 ** ADAM_STEP)
    delta = -ADAM_LR * (m_hat / (jnp.sqrt(v_hat) + ADAM_EPS) + ADAM_WD * w)
    return delta, m2, v2


def _row_pieces(tm, n=2):
    return [slice(k * (tm // n), (k + 1) * (tm // n)) for k in range(n)]


def _row_tile(rows, cols, itemsize=4, budget=1 << 20, mult=8):
    best = mult
    for tr in range(mult, rows + 1, mult):
        if rows % tr == 0 and tr * cols * itemsize <= budget:
            best = tr
    return best


def _mm(name, a, b, dims, grid, a_spec, b_spec, out_shape, o_spec):
    out_dtype = out_shape.dtype

    def body(a_ref, b_ref, o_ref):
        o_ref[...] = _dot(a_ref[...], b_ref[...], dims).astype(out_dtype)

    return _pc(body, name=name, out_shape=out_shape, grid=grid, in_specs=[a_spec, b_spec],
               out_specs=o_spec, sem=("parallel", "parallel"))(a, b)


class _Comm:
    def __init__(self, ins, out_shapes, sems, start, finish):
        self.ins, self.out_shapes, self.sems = list(ins), list(out_shapes), list(sems)
        self.start, self.finish = start, finish


def _hosted_call(body, comm, first, last, *, name, out_shape, grid, in_specs, out_specs, scratch, sem,
                 args):
    n_in, n_out, n_scr = len(in_specs), len(out_shape), len(scratch)
    nci, nco = len(comm.ins), len(comm.out_shapes)

    def wrapped(*refs):
        pos = n_in
        own_in, c_in = refs[:pos], refs[pos:pos + nci]
        pos += nci
        own_out, c_out = refs[pos:pos + n_out], refs[pos + n_out:pos + n_out + nco]
        pos += n_out + nco
        own_scr, c_sem = refs[pos:pos + n_scr], refs[pos + n_scr:]

        @pl.when(first())
        def _():
            comm.start(c_in, c_out, c_sem)

        body(*own_in, *own_out, *own_scr)

        @pl.when(last())
        def _():
            comm.finish(c_in, c_out, c_sem)

    return _pc(wrapped, name=name, out_shape=tuple(out_shape) + tuple(comm.out_shapes), grid=grid,
               in_specs=list(in_specs) + [ANY] * nci, out_specs=tuple(out_specs) + tuple([ANY] * nco),
               scratch=list(scratch) + comm.sems, sem=sem)(*args, *comm.ins)


def _grid1_call(body, comm, n, *, name, out_shape, in_specs, out_specs, scratch, args):
    if comm is None:
        return _pc(body, name=name, out_shape=out_shape, grid=(n,), in_specs=in_specs,
                   out_specs=out_specs, scratch=scratch, sem=("arbitrary",))(*args), ()
    res = _hosted_call(body, comm, lambda: pl.program_id(0) == 0, lambda: pl.program_id(0) == n - 1,
                       name=name, out_shape=out_shape, grid=(n,), in_specs=in_specs,
                       out_specs=out_specs, scratch=scratch, sem=("arbitrary",), args=args)
    return res[:len(out_shape)], res[len(out_shape):]


def _run_comm(name, comm):
    nci, nco = len(comm.ins), len(comm.out_shapes)

    def body(*refs):
        c_in, c_out, c_sem = refs[:nci], refs[nci:nci + nco], refs[nci + nco:]
        comm.start(c_in, c_out, c_sem)
        comm.finish(c_in, c_out, c_sem)

    return _pc(body, name=name, out_shape=tuple(comm.out_shapes), in_specs=[ANY] * nci,
               out_specs=tuple([ANY] * nco), scratch=comm.sems)(*comm.ins)


def _mm_tn(name, a, b, tm, tn, stacked=False):
    t, m = a.shape
    _, n = b.shape
    if stacked:
        assert tm == m
        out_shape = jax.ShapeDtypeStruct((n // tn, m, tn), BF16)
        o_spec = pl.BlockSpec((None, tm, tn), lambda i, j: (j, 0, 0))
    else:
        out_shape = jax.ShapeDtypeStruct((m, n), BF16)
        o_spec = pl.BlockSpec((tm, tn), lambda i, j: (i, j))
    return _mm(name, a, b, TN, (m // tm, n // tn),
               pl.BlockSpec((t, tm), lambda i, j: (0, i)),
               pl.BlockSpec((t, tn), lambda i, j: (0, j)),
               out_shape, o_spec)


def _in_proj_xgrad(dh_parts, win_st, dr1, tm, comm=None):
    t = dr1.shape[0]
    ni = t // tm

    def body(a0, a1, a2, a3, b_ref, add_ref, o_ref, acc):
        j = pl.program_id(1)
        for jj, a_ref in enumerate((a0, a1, a2, a3)):
            @pl.when(j == jj)
            def _(jj=jj, a_ref=a_ref):
                prod = _dot(a_ref[...], b_ref[...], NT)
                if jj == 0:
                    acc[...] = prod + ALPHA * add_ref[...]
                elif jj < N_CHIP - 1:
                    acc[...] += prod
                else:
                    o_ref[...] = acc[...] + prod

    a_spec = pl.BlockSpec((tm, 2 * D_MODEL), lambda i, j: (i, 0))
    tile = pl.BlockSpec((tm, D_MODEL), lambda i, j: (i, 0))
    kw = dict(name="in_proj_xgrad", out_shape=(jax.ShapeDtypeStruct((t, D_MODEL), F32),),
              grid=(ni, N_CHIP),
              in_specs=[a_spec] * 4 + [pl.BlockSpec((None, D_MODEL, 2 * D_MODEL), lambda i, j: (j, 0, 0)),
                                       tile],
              out_specs=(tile,), scratch=[pltpu.VMEM((tm, D_MODEL), F32)],
              sem=("arbitrary", "arbitrary"))
    args = list(dh_parts) + [win_st, dr1]
    if comm is None:
        return _pc(body, **kw)(*args)[0], ()
    res = _hosted_call(body, comm,
                       lambda: (pl.program_id(0) == 0) & (pl.program_id(1) == 0),
                       lambda: (pl.program_id(0) == ni - 1) & (pl.program_id(1) == N_CHIP - 1),
                       args=args, **kw)
    return res[0], res[1:]


def _sgu_mixed(v, wm_ref, bsb_ref, gv, bv):
    gl, dgl = _gelu_and_grad(v)
    vh, rstd = _ln_stats(gl)
    vn = vh * gv + bv
    mixed = []
    for g in range(N_GROUP):
        sl = slice(g * 128, (g + 1) * 128)
        mixed.append(_dot(wm_ref[g], vn[:, sl]) + bsb_ref[g])
    return dgl, vh, rstd, vn, mixed


def _sgu_fwd(h, wm, bsb, gv, bv, comm=None):
    t = h.shape[0]

    def body(u_ref, v_ref, wm_ref, bsb_ref, gv_ref, bv_ref, ya_ref):
        u = u_ref[...].astype(F32)
        _, _, _, _, mixed = _sgu_mixed(v_ref[...].astype(F32), wm_ref, bsb_ref, gv_ref[...], bv_ref[...])
        gu = _gelu(u)
        for g in range(N_GROUP):
            sl = slice(g * 128, (g + 1) * 128)
            ya_ref[:, sl] = (gu[:, sl] * mixed[g]).astype(BF16)

    full3 = pl.BlockSpec((N_GROUP, 128, 128), lambda i: (0, 0, 0))
    vec = pl.BlockSpec((1, D_MODEL), lambda i: (0, 0))
    (ya,), extra = _grid1_call(
        body, comm, t // SGU_BLOCK, name="sgu_fwd",
        out_shape=(jax.ShapeDtypeStruct((t, D_MODEL), BF16),),
        in_specs=[pl.BlockSpec((SGU_BLOCK, D_MODEL), lambda i: (i, 0)),
                  pl.BlockSpec((SGU_BLOCK, D_MODEL), lambda i: (i, 1)),
                  full3, full3, vec, vec],
        out_specs=(pl.BlockSpec((SGU_BLOCK, D_MODEL), lambda i: (i, 0)),),
        scratch=[], args=(h, h, wm, bsb, gv, bv))
    return ya, extra


def _sgu_bwd(h, dya, wm, wmt, bsb, gv, bv, maskf, comm=None):
    t = h.shape[0]
    nb = t // SGU_BLOCK

    def body(u_ref, v_ref, dya_ref, wm_ref, wmt_ref, bsb_ref, gv_ref, bv_ref, mask_ref,
             dh_ref, dws_ref, dbs_ref, dgv_ref, dbv_ref, dmix_acc):
        i = pl.program_id(0)

        @pl.when(i == 0)
        def _():
            dws_ref[...] = jnp.zeros_like(dws_ref)
            dgv_ref[...] = jnp.zeros_like(dgv_ref)
            dbv_ref[...] = jnp.zeros_like(dbv_ref)
            dmix_acc[...] = jnp.zeros_like(dmix_acc)

        u = u_ref[...].astype(F32)
        gvv = gv_ref[...]
        dgl_v, vh, rstd, vn, mixed = _sgu_mixed(v_ref[...].astype(F32), wm_ref, bsb_ref, gvv, bv_ref[...])
        gu, dgl_u = _gelu_and_grad(u)
        dya_v = dya_ref[...]
        dvn_parts = []
        for g in range(N_GROUP):
            sl = slice(g * 128, (g + 1) * 128)
            d_y = dya_v[:, sl]
            dh_ref[:, sl] = (d_y * mixed[g] * dgl_u[:, sl]).astype(BF16)
            d_mixed = d_y * gu[:, sl]
            dmix_acc[g] += d_mixed
            dws_ref[g] += _dot(d_mixed, vn[:, sl], NT) * mask_ref[...]
            dvn_parts.append(_dot(wmt_ref[g], d_mixed))
        dvn = jnp.concatenate(dvn_parts, axis=1)
        dgv_ref[...] += _colsum8(dvn * vh)
        dbv_ref[...] += _colsum8(dvn)
        d_gl = _ln_bwd(dvn * gvv, vh, rstd)
        dh_ref[:, D_MODEL:] = (d_gl * dgl_v).astype(BF16)

        @pl.when(i == nb - 1)
        def _():
            rowid = lax.broadcasted_iota(jnp.int32, (8, 128), 0)
            ones = jnp.ones((8, 128), F32)
            acc = jnp.zeros((8, 128), F32)
            for g in range(N_GROUP):
                rs = _dot32(ones, dmix_acc[g], NT)
                acc = jnp.where(rowid == g, rs, acc)
            dbs_ref[...] = acc

    full3 = pl.BlockSpec((N_GROUP, 128, 128), lambda i: (0, 0, 0))
    vec = pl.BlockSpec((1, D_MODEL), lambda i: (0, 0))
    acc8 = pl.BlockSpec((8, D_MODEL), lambda i: (0, 0))
    return _grid1_call(
        body, comm, nb, name="sgu_bwd",
        out_shape=(jax.ShapeDtypeStruct((t, 2 * D_MODEL), BF16),
                   jax.ShapeDtypeStruct((N_GROUP, 128, 128), F32),
                   jax.ShapeDtypeStruct((8, 128), F32),
                   jax.ShapeDtypeStruct((8, D_MODEL), F32),
                   jax.ShapeDtypeStruct((8, D_MODEL), F32)),
        in_specs=[pl.BlockSpec((SGU_BLOCK, D_MODEL), lambda i: (i, 0)),
                  pl.BlockSpec((SGU_BLOCK, D_MODEL), lambda i: (i, 1)),
                  pl.BlockSpec((SGU_BLOCK, D_MODEL), lambda i: (i, 0)),
                  full3, full3, full3, vec, vec,
                  pl.BlockSpec((128, 128), lambda i: (0, 0))],
        out_specs=(pl.BlockSpec((SGU_BLOCK, 2 * D_MODEL), lambda i: (i, 0)),
                   full3, pl.BlockSpec((8, 128), lambda i: (0, 0)), acc8, acc8),
        scratch=[pltpu.VMEM((N_GROUP, 128, 128), F32)],
        args=(h, h, dya, wm, wmt, bsb, gv, bv, maskf))


def _tri_masks():
    row = lax.broadcasted_iota(jnp.int32, (CHUNK, CHUNK), 0)
    col = lax.broadcasted_iota(jnp.int32, (CHUNK, CHUNK), 1)
    return col <= row, col >= row


def _heads(v):
    return [v[:, hd * HEAD_DIM:(hd + 1) * HEAD_DIM] for hd in range(N_HEAD)]


def _tri_cumsum(tri_bf, v):
    hi = v.astype(BF16)
    r = v - hi.astype(F32)
    mid = r.astype(BF16)
    lo = (r - mid.astype(F32)).astype(BF16)
    return _dot(tri_bf, hi) + _dot(tri_bf, mid) + _dot(tri_bf, lo)


def _hgrn_chunk(q, fp, ii, lb, st_heads, causal):
    sg = _sig(fp)
    f = lb + (1.0 - lb) * sg
    k = 1.0 - f
    c = _tri_cumsum(causal.astype(BF16), jnp.log(f))
    ec = jnp.exp(c)
    en = jnp.exp(-c)
    sq = _sig(q)
    qt = q * sq * ec
    kt = k * en
    ecl = jnp.exp(c[CHUNK - 1:CHUNK, :])
    kk = kt * ecl
    qtb, ktb, iib, kkb = qt.astype(BF16), kt.astype(BF16), ii.astype(BF16), kk.astype(BF16)
    attn, o = [], []
    for hd, (qh, kh, ih) in enumerate(zip(_heads(qtb), _heads(ktb), _heads(iib))):
        a = jnp.where(causal, _dot(qh, kh, NT), 0.0).astype(BF16)
        attn.append(a)
        o.append(_dot(a, ih) + _dot(qh, st_heads[hd], NT))
    return dict(sg=sg, f=f, k=k, ec=ec, en=en, sq=sq, ecl=ecl, kk=kk, qtb=qtb, ktb=ktb, iib=iib,
                kkb=kkb, attn=attn, o=o)


def _rms_heads(o_heads):
    rinv = [lax.rsqrt(jnp.mean(o * o, axis=-1, keepdims=True) + RMS_EPS) for o in o_heads]
    return rinv, jnp.concatenate([o * r for o, r in zip(o_heads, rinv)], axis=1)


HG_CHUNKS = 4
HG_ROWS = HG_CHUNKS * CHUNK


def _hgrn_fwd(h, logits, gn, comm=None):
    t = h.shape[0]
    nb = t // HG_ROWS

    def body(q_ref, f_ref, i_ref, og_ref, lg_ref, gn_ref, yb_ref, st_ref, state):
        @pl.when(pl.program_id(0) == 0)
        def _():
            state[...] = jnp.zeros_like(state)

        causal, _ = _tri_masks()
        lb = _sig(lg_ref[0:1, :] - lg_ref[1:2, :])
        gnv = gn_ref[...]
        st = [state[hd] for hd in range(N_HEAD)]
        for cc in range(HG_CHUNKS):
            rows = slice(cc * CHUNK, (cc + 1) * CHUNK)
            og = og_ref[rows, :].astype(F32)
            r = _hgrn_chunk(q_ref[rows, :].astype(F32), f_ref[rows, :].astype(F32),
                            i_ref[rows, :].astype(F32), lb, [s.astype(BF16) for s in st], causal)
            _, on = _rms_heads(r["o"])
            yb_ref[rows, :] = (on * gnv * (og * _sig(og))).astype(BF16)
            for hd in range(N_HEAD):
                st_ref[cc, hd] = st[hd]
            st = [s * e + _dot(ih, kh, TN)
                  for s, e, ih, kh in zip(st, _heads(r["ecl"]), _heads(r["iib"]), _heads(r["kkb"]))]
        for hd in range(N_HEAD):
            state[hd] = st[hd]

    def col(k):
        return pl.BlockSpec((HG_ROWS, D_MODEL), lambda ci: (ci, k))

    return _grid1_call(body, comm, nb, name="hgrn_fwd",
                       out_shape=(jax.ShapeDtypeStruct((t, D_MODEL), BF16),
                                  jax.ShapeDtypeStruct((t // CHUNK, N_HEAD, HEAD_DIM, HEAD_DIM), F32)),
                       in_specs=[col(2), col(3), col(4), col(5),
                                 pl.BlockSpec((2, D_MODEL), lambda ci: (0, 0)),
                                 pl.BlockSpec((1, D_MODEL), lambda ci: (0, 0))],
                       out_specs=(pl.BlockSpec((HG_ROWS, D_MODEL), lambda ci: (ci, 0)),
                                  pl.BlockSpec((HG_CHUNKS, N_HEAD, HEAD_DIM, HEAD_DIM),
                                               lambda ci: (ci, 0, 0, 0))),
                       scratch=[pltpu.VMEM((N_HEAD, HEAD_DIM, HEAD_DIM), F32)],
                       args=(h, h, h, h, logits, gn))


def _hgrn_chunk_bwd(q, fp, ii, og, dy, gnv, lb, st, dsn, causal, anti):
    stb = [s.astype(BF16) for s in st]
    dsnb = [s.astype(BF16) for s in dsn]
    r = _hgrn_chunk(q, fp, ii, lb, stb, causal)
    rinv, on = _rms_heads(r["o"])
    so = _sig(og)
    sil = og * so
    d_og = dy * on * gnv * (so * (1.0 + og * (1.0 - so)))
    d_on = dy * gnv * sil
    d_ob = jnp.concatenate(
        [ri * (dn - oh * jnp.mean(dn * oh, axis=-1, keepdims=True))
         for ri, dn, oh in zip(rinv, _heads(d_on), _heads(on))], axis=1).astype(BF16)
    d_i, d_qt, d_kt, d_kk, d_st, st_dsn = [], [], [], [], [], []
    ecl = _heads(r["ecl"])
    for hd, (dh, qh, kh, ih, kkh) in enumerate(zip(_heads(d_ob), _heads(r["qtb"]), _heads(r["ktb"]),
                                                   _heads(r["iib"]), _heads(r["kkb"]))):
        d_attn = jnp.where(causal, _dot(dh, ih, NT), 0.0).astype(BF16)
        d_i.append(_dot(r["attn"][hd], dh, TN) + _dot(kkh, dsnb[hd], NT))
        d_qt.append(_dot(d_attn, kh) + _dot(dh, stb[hd]))
        d_kt.append(_dot(d_attn, qh, TN))
        d_kk.append(_dot(ih, dsnb[hd]))
        d_st.append(_dot(dh, qh, TN) + dsn[hd] * ecl[hd])
        st_dsn.append(jnp.sum(st[hd] * dsn[hd], axis=0, keepdims=True))
    d_qt = jnp.concatenate(d_qt, axis=1)
    d_kt = jnp.concatenate(d_kt, axis=1)
    d_kk = jnp.concatenate(d_kk, axis=1)
    kk = r["kk"]
    d_cl = r["ecl"] * jnp.concatenate(st_dsn, axis=1) + jnp.sum(kk * d_kk, axis=0, keepdims=True)
    d_k = (d_kk * r["ecl"] + d_kt) * r["en"]
    d_c = d_qt * r["qtb"].astype(F32) - d_kt * r["ktb"].astype(F32) - d_kk * kk
    rowid = lax.broadcasted_iota(jnp.int32, (CHUNK, D_MODEL), 0)
    d_c = d_c + jnp.where(rowid == CHUNK - 1, d_cl, 0.0)
    d_lf = _tri_cumsum(anti.astype(BF16), d_c)
    d_f = d_lf / r["f"] - d_k
    sg, sq = r["sg"], r["sq"]
    d_q = d_qt * r["ec"] * (sq * (1.0 + q * (1.0 - sq)))
    d_fp = d_f * (1.0 - lb) * sg * (1.0 - sg)
    return (d_q, d_fp, jnp.concatenate(d_i, axis=1), d_og, d_st,
            _colsum8(dy * on * sil), _colsum8(d_f * (1.0 - sg)))


def _hgrn_bwd(h, dyb, st_all, logits, gn, comm=None):
    t = h.shape[0]
    nb = t // HG_ROWS

    def body(q_ref, f_ref, i_ref, og_ref, dyb_ref, st_ref, lg_ref, gn_ref,
             dh1_ref, dh2_ref, dlb_ref, dgn_ref, dstate):
        @pl.when(pl.program_id(0) == 0)
        def _():
            dstate[...] = jnp.zeros_like(dstate)
            dlb_ref[...] = jnp.zeros_like(dlb_ref)
            dgn_ref[...] = jnp.zeros_like(dgn_ref)

        causal, anti = _tri_masks()
        lb = _sig(lg_ref[0:1, :] - lg_ref[1:2, :])
        gnv = gn_ref[...]
        dsn = [dstate[hd] for hd in range(N_HEAD)]
        dgn_acc = jnp.zeros((8, D_MODEL), F32)
        dlb_acc = jnp.zeros((8, D_MODEL), F32)
        for cc in reversed(range(HG_CHUNKS)):
            rows = slice(cc * CHUNK, (cc + 1) * CHUNK)
            d_q, d_fp, d_i, d_og, dsn, dgn_c, dlb_c = _hgrn_chunk_bwd(
                q_ref[rows, :].astype(F32), f_ref[rows, :].astype(F32), i_ref[rows, :].astype(F32),
                og_ref[rows, :].astype(F32), dyb_ref[rows, :], gnv, lb,
                [st_ref[cc, hd] for hd in range(N_HEAD)], dsn, causal, anti)
            dgn_acc = dgn_acc + dgn_c
            dlb_acc = dlb_acc + dlb_c
            dh1_ref[rows, :D_MODEL] = d_q.astype(BF16)
            dh1_ref[rows, D_MODEL:] = d_fp.astype(BF16)
            dh2_ref[rows, :D_MODEL] = d_i.astype(BF16)
            dh2_ref[rows, D_MODEL:] = d_og.astype(BF16)
        dgn_ref[...] += dgn_acc
        dlb_ref[...] += dlb_acc
        for hd in range(N_HEAD):
            dstate[hd] = dsn[hd]

    def col(k):
        return pl.BlockSpec((HG_ROWS, D_MODEL), lambda ci: (nb - 1 - ci, k))

    acc8 = pl.BlockSpec((8, D_MODEL), lambda ci: (0, 0))
    pair = pl.BlockSpec((HG_ROWS, 2 * D_MODEL), lambda ci: (nb - 1 - ci, 0))
    return _grid1_call(body, comm, nb, name="hgrn_bwd",
                       out_shape=(jax.ShapeDtypeStruct((t, 2 * D_MODEL), BF16),
                                  jax.ShapeDtypeStruct((t, 2 * D_MODEL), BF16),
                                  jax.ShapeDtypeStruct((8, D_MODEL), F32),
                                  jax.ShapeDtypeStruct((8, D_MODEL), F32)),
                       in_specs=[col(2), col(3), col(4), col(5),
                                 pl.BlockSpec((HG_ROWS, D_MODEL), lambda ci: (nb - 1 - ci, 0)),
                                 pl.BlockSpec((HG_CHUNKS, N_HEAD, HEAD_DIM, HEAD_DIM),
                                              lambda ci: (nb - 1 - ci, 0, 0, 0)),
                                 pl.BlockSpec((2, D_MODEL), lambda ci: (0, 0)),
                                 pl.BlockSpec((1, D_MODEL), lambda ci: (0, 0))],
                       out_specs=(pair, pair, acc8, acc8),
                       scratch=[pltpu.VMEM((N_HEAD, HEAD_DIM, HEAD_DIM), F32)],
                       args=(h, h, h, h, dyb, st_all, logits, gn))


def _mix_fwd(ya, yb, h, x, wb0, wb1, wo, g1, b1, tm, comm=None):
    t = x.shape[0]

    def body(ya_ref, yb_ref, ga_ref, gb_ref, x_ref, wb0_ref, wb1_ref, wo_ref, g1_ref, b1_ref,
             r1_ref, a_ref, b_ref, m_ref, x1_ref):
        a = _dot(ya_ref[...], wb0_ref[...])
        b = _dot(yb_ref[...], wb1_ref[...])
        m = _sig(ga_ref[...].astype(F32)) * a + _sig(gb_ref[...].astype(F32)) * b
        r1 = ALPHA * x_ref[...] + _dot(m, wo_ref[...])
        xh, _ = _ln_stats(r1)
        r1_ref[...] = r1
        a_ref[...] = a
        b_ref[...] = b
        m_ref[...] = m.astype(BF16)
        x1_ref[...] = (xh * g1_ref[...] + b1_ref[...]).astype(BF16)

    tile = pl.BlockSpec((tm, D_MODEL), lambda i: (i, 0))
    wsp = pl.BlockSpec((D_MODEL, D_MODEL), lambda i: (0, 0))
    vec = pl.BlockSpec((1, D_MODEL), lambda i: (0, 0))
    f32o = jax.ShapeDtypeStruct((t, D_MODEL), F32)
    bfo = jax.ShapeDtypeStruct((t, D_MODEL), BF16)
    return _grid1_call(body, comm, t // tm, name="mix_fwd", out_shape=(f32o, f32o, f32o, bfo, bfo),
                       in_specs=[tile, tile,
                                 pl.BlockSpec((tm, D_MODEL), lambda i: (i, 6)),
                                 pl.BlockSpec((tm, D_MODEL), lambda i: (i, 7)),
                                 tile, wsp, wsp, wsp, vec, vec],
                       out_specs=(tile, tile, tile, tile, tile),
                       scratch=[], args=(ya, yb, h, h, x, wb0, wb1, wo, g1, b1))


def _mix_bwd(dr1, h, a, b, wo, wb0, wb1, tm):
    t = dr1.shape[0]

    def body(dr1_ref, ga_ref, gb_ref, a_ref, b_ref, wo_ref, wb0_ref, wb1_ref,
             da_ref, db_ref, dh3_ref, dya_ref, dyb_ref):
        d_m = _dot(dr1_ref[...], wo_ref[...], NT)
        sa = _sig(ga_ref[...].astype(F32))
        sb = _sig(gb_ref[...].astype(F32))
        d_a = (d_m * sa).astype(BF16)
        d_b = (d_m * sb).astype(BF16)
        da_ref[...] = d_a
        db_ref[...] = d_b
        dh3_ref[:, :D_MODEL] = (d_m * a_ref[...] * sa * (1.0 - sa)).astype(BF16)
        dh3_ref[:, D_MODEL:] = (d_m * b_ref[...] * sb * (1.0 - sb)).astype(BF16)
        dya_ref[...] = _dot(d_a, wb0_ref[...], NT)
        dyb_ref[...] = _dot(d_b, wb1_ref[...], NT)

    tile = pl.BlockSpec((tm, D_MODEL), lambda i: (i, 0))
    wsp = pl.BlockSpec((D_MODEL, D_MODEL), lambda i: (0, 0))
    f32o = jax.ShapeDtypeStruct((t, D_MODEL), F32)
    bfo = jax.ShapeDtypeStruct((t, D_MODEL), BF16)
    return _pc(body, name="mix_bwd",
               out_shape=(bfo, bfo, jax.ShapeDtypeStruct((t, 2 * D_MODEL), BF16), f32o, f32o),
               grid=(t // tm,),
               in_specs=[tile,
                         pl.BlockSpec((tm, D_MODEL), lambda i: (i, 6)),
                         pl.BlockSpec((tm, D_MODEL), lambda i: (i, 7)),
                         tile, tile, wsp, wsp, wsp],
               out_specs=(tile, tile, pl.BlockSpec((tm, 2 * D_MODEL), lambda i: (i, 0)),
                          tile, tile),
               sem=("parallel",))(dr1, h, h, a, b, wo, wb0, wb1)


FF_TILE = 1408
FF_NJ = D_FF // FF_TILE


def _shift_down(v, k):
    return pltpu.roll(v, k, 0)


def _shift_up(v, k):
    return pltpu.roll(v, v.shape[0] - k, 0)


HALO = 16
FF_PIECES = ((0, 512), (512, 1024), (1024, FF_TILE))


def _ffn_up_act(x1b, wup_st, convw, convb, tm, comm):
    t = x1b.shape[0]
    ni = t // tm
    nth = tm // HALO

    def body(x_ref, xp_ref, wg_ref, wv_ref, cw_ref, cb_ref, h2_ref, act_ref):
        first = (pl.program_id(0) > 0).astype(F32)
        xv, xp = x_ref[...], xp_ref[...]
        for c0, c1 in FF_PIECES:
            pc = slice(c0, c1)
            wg = wg_ref[:, pc]
            gate = _dot(xv, wg).astype(BF16)
            val = _dot(xv, wv_ref[:, pc]).astype(BF16)
            prev = (_dot(xp, wg) * first).astype(BF16)
            h2_ref[0, :, pc] = gate
            h2_ref[1, :, pc] = val
            ext = jnp.concatenate([prev.astype(F32), gate.astype(F32)], axis=0)
            gc = (cw_ref[0:1, pc] * _shift_down(ext, 2) + cw_ref[1:2, pc] * _shift_down(ext, 1)
                  + cw_ref[2:3, pc] * ext + cb_ref[:, pc])[HALO:, :]
            act_ref[:, pc] = (_gelu(gc) * val.astype(F32)).astype(BF16)

    res = _hosted_call(
        body, comm,
        lambda: (pl.program_id(0) == 0) & (pl.program_id(1) == 0),
        lambda: (pl.program_id(0) == ni - 1) & (pl.program_id(1) == FF_NJ - 1),
        name="ffn_up",
        out_shape=(jax.ShapeDtypeStruct((2, t, D_FF), BF16), jax.ShapeDtypeStruct((t, D_FF), BF16)),
        grid=(ni, FF_NJ),
        in_specs=[pl.BlockSpec((tm, D_MODEL), lambda i, j: (i, 0)),
                  pl.BlockSpec((HALO, D_MODEL), lambda i, j: (jnp.maximum(i * nth - 1, 0), 0)),
                  pl.BlockSpec((None, D_MODEL, FF_TILE), lambda i, j: (j, 0, 0)),
                  pl.BlockSpec((None, D_MODEL, FF_TILE), lambda i, j: (j + FF_NJ, 0, 0)),
                  pl.BlockSpec((3, FF_TILE), lambda i, j: (0, j)),
                  pl.BlockSpec((1, FF_TILE), lambda i, j: (0, j))],
        out_specs=(pl.BlockSpec((2, tm, FF_TILE), lambda i, j: (0, i, j)),
                   pl.BlockSpec((tm, FF_TILE), lambda i, j: (i, j))),
        scratch=[], sem=("arbitrary", "arbitrary"),
        args=(x1b, x1b, wup_st, wup_st, convw, convb))
    return res[0], res[1], res[2:]


def _out_fwd_bwd(act, x1b, r1, p2, tgt, wd, wpg, wpp, g1, b1, g2, b2, tm):
    t = r1.shape[0]

    def body(act_ref, x1b_ref, r1_ref, p_ref, tgt_ref, wd_ref, wpg_ref, wpp_ref,
             g1_ref, b1_ref, g2_ref, b2_ref,
             dr2_ref, dpg_ref, dpp_ref, loss_ref, dg2_ref, db2_ref):
        i = pl.program_id(0)

        @pl.when(i == 0)
        def _():
            loss_ref[...] = jnp.zeros_like(loss_ref)
            dg2_ref[...] = jnp.zeros_like(dg2_ref)
            db2_ref[...] = jnp.zeros_like(db2_ref)

        g2v = g2_ref[...]
        loss_acc = jnp.zeros((1, 1), F32)
        dg2_acc = jnp.zeros((8, D_MODEL), F32)
        db2_acc = jnp.zeros((8, D_MODEL), F32)
        for rows in _row_pieces(tm):
            ffn = _dot(act_ref[rows, :], wd_ref[...])
            pg = _dot(x1b_ref[rows, :], wpg_ref[...])
            pp = _dot(p_ref[rows, :], wpp_ref[...])
            s = _sig(pg)
            xh1, _ = _ln_stats(r1_ref[rows, :])
            x1 = xh1 * g1_ref[...] + b1_ref[...]
            r2 = ALPHA * x1 + ffn + s * pp
            xh2, rstd2 = _ln_stats(r2)
            diff = xh2 * g2v + b2_ref[...] - tgt_ref[rows, :]
            loss_acc = loss_acc + jnp.sum(jnp.sum(diff * diff, axis=1, keepdims=True), axis=0, keepdims=True)
            dy = diff * (1.0 / D_MODEL)
            dg2_acc = dg2_acc + _colsum8(dy * xh2)
            db2_acc = db2_acc + _colsum8(dy)
            dr2 = _ln_bwd(dy * g2v, xh2, rstd2)
            dr2_ref[rows, :] = dr2
            dpg_ref[rows, :] = (dr2 * pp * s * (1.0 - s)).astype(BF16)
            dpp_ref[rows, :] = (dr2 * s).astype(BF16)
        loss_ref[...] += jnp.broadcast_to(loss_acc * (0.5 / D_MODEL), loss_ref.shape)
        dg2_ref[...] += dg2_acc
        db2_ref[...] += db2_acc

    tile = pl.BlockSpec((tm, D_MODEL), lambda i: (i, 0))
    vec = pl.BlockSpec((1, D_MODEL), lambda i: (0, 0))
    acc8 = pl.BlockSpec((8, D_MODEL), lambda i: (0, 0))
    acc_shape = jax.ShapeDtypeStruct((8, D_MODEL), F32)
    return _pc(body, name="out_fwd_bwd",
               out_shape=(jax.ShapeDtypeStruct((t, D_MODEL), F32),
                          jax.ShapeDtypeStruct((t, D_MODEL), BF16),
                          jax.ShapeDtypeStruct((t, D_MODEL), BF16),
                          acc_shape, acc_shape, acc_shape),
               grid=(t // tm,),
               in_specs=[pl.BlockSpec((tm, D_FF), lambda i: (i, 0)), tile, tile,
                         pl.BlockSpec((tm, PLE_DIM), lambda i: (i, 0)), tile,
                         pl.BlockSpec((D_FF, D_MODEL), lambda i: (0, 0)),
                         pl.BlockSpec((D_MODEL, D_MODEL), lambda i: (0, 0)),
                         pl.BlockSpec((PLE_DIM, D_MODEL), lambda i: (0, 0)),
                         vec, vec, vec, vec],
               out_specs=(tile, tile, tile, acc8, acc8, acc8),
               sem=("arbitrary",))(act, x1b, r1, p2, tgt, wd, wpg, wpp, g1, b1, g2, b2)


def _ffn_bwd(h2, dr2, wd, wup_st, dpg, wpg, r1, g1, convw, convb, tm):
    t = r1.shape[0]
    ni = t // tm
    nth = tm // HALO
    last_halo = t // HALO - 1
    main_rows = slice(HALO, HALO + tm)

    def body(g_ref, gp_ref, gn_ref, v_ref, vn_ref, dr2_ref, dr2n_ref, wd_ref, wug_ref, wuv_ref,
             cw_ref, cb_ref, dpg_ref, wpg_ref, r1_ref, g1_ref,
             dh2_ref, dr1_ref, dcw_ref, dcb_ref, dg1_ref, db1_ref, acc):
        i = pl.program_id(0)
        j = pl.program_id(1)

        @pl.when((i == 0) & (j == 0))
        def _():
            dcw_ref[...] = jnp.zeros_like(dcw_ref)
            dcb_ref[...] = jnp.zeros_like(dcb_ref)
            dg1_ref[...] = jnp.zeros_like(dg1_ref)
            db1_ref[...] = jnp.zeros_like(db1_ref)

        dr2v = dr2_ref[...].astype(BF16)
        dr2n = dr2n_ref[...].astype(BF16)
        first = (i > 0).astype(F32)
        more = (i < ni - 1).astype(F32)
        prod = None
        dcw_parts, dcb_parts = [], []
        for c0, c1 in FF_PIECES:
            pc = slice(c0, c1)
            zeros = jnp.zeros((HALO, c1 - c0), F32)
            da = _dot(dr2v, wd_ref[pc, :], NT)
            dnext = _dot(dr2n, wd_ref[pc, :], NT) * more
            ext = jnp.concatenate([gp_ref[:, pc].astype(F32) * first, g_ref[:, pc].astype(F32),
                                   gn_ref[:, pc].astype(F32)], axis=0)
            vext = jnp.concatenate([zeros, v_ref[:, pc].astype(F32), vn_ref[:, pc].astype(F32)], axis=0)
            dext = jnp.concatenate([zeros, da, dnext], axis=0)
            g2 = _shift_down(ext, 2)
            g1s = _shift_down(ext, 1)
            gc = cw_ref[0:1, pc] * g2 + cw_ref[1:2, pc] * g1s + cw_ref[2:3, pc] * ext + cb_ref[:, pc]
            gl, dgl = _gelu_and_grad(gc)
            d_gc = dext * vext * dgl
            d_gate = (cw_ref[2:3, pc] * d_gc + cw_ref[1:2, pc] * _shift_up(d_gc, 1)
                      + cw_ref[0:1, pc] * _shift_up(d_gc, 2))[main_rows, :].astype(BF16)
            d_val = (da * gl[main_rows, :]).astype(BF16)
            dh2_ref[0, :, pc] = d_gate
            dh2_ref[1, :, pc] = d_val
            dm = d_gc[main_rows, :]
            s0 = jnp.sum(dm * g2[main_rows, :], axis=0, keepdims=True)
            s1 = jnp.sum(dm * g1s[main_rows, :], axis=0, keepdims=True)
            s2 = jnp.sum(dm * ext[main_rows, :], axis=0, keepdims=True)
            rowid = lax.broadcasted_iota(jnp.int32, (8, c1 - c0), 0)
            dcw_parts.append(jnp.where(rowid == 0, s0, jnp.where(rowid == 1, s1,
                                                                 jnp.where(rowid == 2, s2, 0.0))))
            dcb_parts.append(_colsum8(dm))
            part = _dot(d_gate, wug_ref[:, pc], NT) + _dot(d_val, wuv_ref[:, pc], NT)
            prod = part if prod is None else prod + part
        dcw_part = jnp.concatenate(dcw_parts, axis=1)
        dcb_part = jnp.concatenate(dcb_parts, axis=1)
        for jj in range(FF_NJ):
            @pl.when(j == jj)
            def _(jj=jj):
                cols = slice(jj * FF_TILE, (jj + 1) * FF_TILE)
                dcw_ref[:, cols] += dcw_part
                dcb_ref[:, cols] += dcb_part

        @pl.when(j == 0)
        def _():
            acc[...] = prod

        @pl.when(j > 0)
        def _():
            acc[...] += prod

        @pl.when(j == FF_NJ - 1)
        def _():
            d_x1 = acc[...] + _dot(dpg_ref[...], wpg_ref[...], NT) + ALPHA * dr2_ref[...]
            xh, rstd = _ln_stats(r1_ref[...])
            dg1_ref[...] += _colsum8(d_x1 * xh)
            db1_ref[...] += _colsum8(d_x1)
            dr1_ref[...] = _ln_bwd(d_x1 * g1_ref[...], xh, rstd)

    def h2_main(part):
        return pl.BlockSpec((None, tm, FF_TILE), lambda i, j: (part, i, j))

    def h2_prev(part):
        return pl.BlockSpec((None, HALO, FF_TILE), lambda i, j: (part, jnp.maximum(i * nth - 1, 0), j))

    def h2_next(part):
        return pl.BlockSpec((None, HALO, FF_TILE),
                            lambda i, j: (part, jnp.minimum((i + 1) * nth, last_halo), j))

    tile = pl.BlockSpec((tm, D_MODEL), lambda i, j: (i, 0))
    acc8 = pl.BlockSpec((8, D_MODEL), lambda i, j: (0, 0))
    accff = pl.BlockSpec((8, D_FF), lambda i, j: (0, 0))
    acc_shape = jax.ShapeDtypeStruct((8, D_MODEL), F32)
    accff_shape = jax.ShapeDtypeStruct((8, D_FF), F32)
    return _pc(body, name="ffn_bwd",
               out_shape=(jax.ShapeDtypeStruct((2, t, D_FF), BF16),
                          jax.ShapeDtypeStruct((t, D_MODEL), F32),
                          accff_shape, accff_shape, acc_shape, acc_shape),
               grid=(ni, FF_NJ),
               in_specs=[h2_main(0), h2_prev(0), h2_next(0), h2_main(1), h2_next(1),
                         tile,
                         pl.BlockSpec((HALO, D_MODEL), lambda i, j: (jnp.minimum((i + 1) * nth, last_halo), 0)),
                         pl.BlockSpec((FF_TILE, D_MODEL), lambda i, j: (j, 0)),
                         pl.BlockSpec((None, D_MODEL, FF_TILE), lambda i, j: (j, 0, 0)),
                         pl.BlockSpec((None, D_MODEL, FF_TILE), lambda i, j: (j + FF_NJ, 0, 0)),
                         pl.BlockSpec((3, FF_TILE), lambda i, j: (0, j)),
                         pl.BlockSpec((1, FF_TILE), lambda i, j: (0, j)),
                         tile, pl.BlockSpec((D_MODEL, D_MODEL), lambda i, j: (0, 0)),
                         tile, pl.BlockSpec((1, D_MODEL), lambda i, j: (0, 0))],
               out_specs=(pl.BlockSpec((2, tm, FF_TILE), lambda i, j: (0, i, j)),
                          tile, accff, accff, acc8, acc8),
               scratch=[pltpu.VMEM((tm, D_MODEL), F32)],
               sem=("arbitrary", "arbitrary"))(h2, h2, h2, h2, h2, dr2, dr2, wd, wup_st, wup_st,
                                               convw, convb, dpg, wpg, r1, g1)


ANY = pl.BlockSpec(memory_space=pl.ANY)


def _chip_peers():
    x, y, c = lax.axis_index("x"), lax.axis_index("y"), lax.axis_index("c")
    return x, y, c, [(1 - x, y), (x, 1 - y), (1 - x, 1 - y)]


def _gather_comm(halved, whole=()):
    n, nw = len(halved), len(whole)

    def copies(ins, outs, sems):
        ici_send, ici_recv, d2d_send, d2d_recv, own_send, own_recv = sems
        x, y, c, peers = _chip_peers()
        me = 2 * x + y
        sibling = (x, y, 1 - c)
        own, ici, ici_wait, fwd, fwd_wait = [], [], [], [], []
        for ti in range(n + nw):
            src, dst = ins[ti], outs[ti]
            own.append(pltpu.make_async_remote_copy(
                src_ref=src, dst_ref=dst.at[me], send_sem=own_send.at[ti], recv_sem=own_recv.at[ti],
                device_id=sibling, device_id_type=MESH))
            for k, (px, py) in enumerate(peers):
                pk = 2 * px + py
                sem = dict(send_sem=ici_send.at[ti * 3 + k], recv_sem=ici_recv.at[ti * 3 + k],
                           device_id=(px, py, c), device_id_type=MESH)
                if ti < n:
                    ici.append(pltpu.make_async_remote_copy(src_ref=src.at[c], dst_ref=dst.at[me, c], **sem))
                    ici_wait.append(pltpu.make_async_remote_copy(src_ref=src.at[c], dst_ref=dst.at[pk, c], **sem))
                    dsem = dict(send_sem=d2d_send.at[ti * 3 + k], recv_sem=d2d_recv.at[ti * 3 + k],
                                device_id=sibling, device_id_type=MESH)
                    fwd.append(pltpu.make_async_remote_copy(src_ref=dst.at[pk, c], dst_ref=dst.at[pk, c], **dsem))
                    fwd_wait.append(pltpu.make_async_remote_copy(
                        src_ref=dst.at[pk, 1 - c], dst_ref=dst.at[pk, 1 - c], **dsem))
                else:
                    ici.append(pltpu.make_async_remote_copy(src_ref=src, dst_ref=dst.at[me], **sem))
                    ici_wait.append(pltpu.make_async_remote_copy(src_ref=src, dst_ref=dst.at[pk], **sem))
        return own, ici, ici_wait, fwd, fwd_wait

    def start(ins, outs, sems):
        own, ici, _, _, _ = copies(ins, outs, sems)
        for cp in own + ici:
            cp.start()

    def finish(ins, outs, sems):
        own, ici, ici_wait, fwd, fwd_wait = copies(ins, outs, sems)
        for i, cp in enumerate(ici_wait):
            cp.wait_recv()
            if i < len(fwd):
                fwd[i].start()
        for cp in fwd_wait + own:
            cp.wait_recv()
        for cp in own + ici + fwd:
            cp.wait_send()

    srcs = list(halved) + list(whole)
    return _Comm(srcs, [jax.ShapeDtypeStruct((N_CHIP,) + s.shape, s.dtype) for s in srcs],
                 [pltpu.SemaphoreType.DMA((3 * (n + nw),)), pltpu.SemaphoreType.DMA((3 * (n + nw),)),
                  pltpu.SemaphoreType.DMA((max(3 * n, 1),)), pltpu.SemaphoreType.DMA((max(3 * n, 1),)),
                  pltpu.SemaphoreType.DMA((n + nw,)), pltpu.SemaphoreType.DMA((n + nw,))],
                 start, finish)


def _sibling_exchange_comm(grads):
    n = len(grads)

    def copies(ins, outs, sems):
        send_sems, recv_sems = sems
        x, y, c = lax.axis_index("x"), lax.axis_index("y"), lax.axis_index("c")
        res = []
        for ti in range(n):
            half = ins[ti].shape[1] // 2
            res.append(pltpu.make_async_remote_copy(
                src_ref=ins[ti].at[:, pl.ds(pl.multiple_of((1 - c) * half, 16), half), :],
                dst_ref=outs[ti],
                send_sem=send_sems.at[ti], recv_sem=recv_sems.at[ti],
                device_id=(x, y, 1 - c), device_id_type=MESH))
        return res

    def start(ins, outs, sems):
        for cp in copies(ins, outs, sems):
            cp.start()

    def finish(ins, outs, sems):
        for cp in copies(ins, outs, sems):
            cp.wait()

    return _Comm(grads, [jax.ShapeDtypeStruct((N_CHIP, g.shape[1] // 2, g.shape[2]), g.dtype) for g in grads],
                 [pltpu.SemaphoreType.DMA((n,)), pltpu.SemaphoreType.DMA((n,))], start, finish)


def _in_proj_gathering(x2b, own, chip, tm, comm):
    t = x2b.shape[0]
    ni = t // tm
    half, cols = own.shape[1], own.shape[2]
    nci, nco = len(comm.ins), len(comm.out_shapes)

    def body(chip_ref, x_ref, own_ref, own_hbm, *rest):
        c_in = rest[:nci]
        h_ref, win_out = rest[nci:nci + 2]
        c_out = rest[nci + 2:nci + 2 + nco]
        w_scr, ici_send, ici_recv, d2d_send, d2d_recv, own_sems, ld_sems = rest[nci + 2 + nco:nci + 9 + nco]
        c_sem = rest[nci + 9 + nco:]
        s, i = pl.program_id(0), pl.program_id(1)
        x, y, c, peers = _chip_peers()
        me = 2 * x + y
        sibling = (x, y, 1 - c)

        def ici(k, slot):
            px, py = peers[k]
            return pltpu.make_async_remote_copy(
                src_ref=own_hbm.at[c], dst_ref=win_out.at[slot, c],
                send_sem=ici_send.at[k], recv_sem=ici_recv.at[k],
                device_id=(px, py, c), device_id_type=MESH)

        def forward(k, core):
            pk = 2 * peers[k][0] + peers[k][1]
            return pltpu.make_async_remote_copy(
                src_ref=win_out.at[pk, core], dst_ref=win_out.at[pk, core],
                send_sem=d2d_send.at[k], recv_sem=d2d_recv.at[k],
                device_id=sibling, device_id_type=MESH)

        place_own = pltpu.make_async_remote_copy(
            src_ref=own_hbm, dst_ref=win_out.at[me], send_sem=own_sems.at[0], recv_sem=own_sems.at[1],
            device_id=sibling, device_id_type=MESH)

        @pl.when((s == 0) & (i == 0))
        def _():
            for k in range(3):
                ici(k, me).start()
            place_own.start()

        @pl.when(s == 0)
        def _():
            xv = x_ref[...]
            h_ref[...] = (_dot(xv[:, :half], own_ref[0]) + _dot(xv[:, half:], own_ref[1])).astype(BF16)

        for k in range(3):
            @pl.when((s == k + 1) & (i == 0))
            def _(k=k):
                pk = 2 * peers[k][0] + peers[k][1]
                ici(k, pk).wait_recv()
                forward(k, c).start()
                forward(k, 1 - c).wait_recv()
                loads = [pltpu.make_async_copy(win_out.at[pk, hh], w_scr.at[hh], ld_sems.at[hh])
                         for hh in range(2)]
                for ld in loads:
                    ld.start()
                for ld in loads:
                    ld.wait()
                if k == 1:
                    comm.start(c_in, c_out, c_sem)

        @pl.when(s > 0)
        def _():
            xv = x_ref[...]
            h_ref[...] = (_dot(xv[:, :half], w_scr[0]) + _dot(xv[:, half:], w_scr[1])).astype(BF16)

        @pl.when((s == N_CHIP - 1) & (i == ni - 1))
        def _():
            place_own.wait()
            for k in range(3):
                ici(k, me).wait_send()
                forward(k, c).wait_send()
            comm.finish(c_in, c_out, c_sem)

    def shard_col(s, me):
        return jnp.where(s == 0, me, me ^ jnp.where(s == 1, 2, jnp.where(s == 2, 1, 3)))

    res = _pc(body, name="in_proj",
              out_shape=(jax.ShapeDtypeStruct((t, N_CHIP * cols), BF16),
                         jax.ShapeDtypeStruct((N_CHIP,) + own.shape, own.dtype)) + tuple(comm.out_shapes),
              grid=(N_CHIP, ni), nsp=1,
              in_specs=[pl.BlockSpec((tm, 2 * half), lambda s, i, chip_ref: (i, 0)),
                        pl.BlockSpec(own.shape, lambda s, i, chip_ref: (0, 0, 0)),
                        ANY] + [ANY] * nci,
              out_specs=(pl.BlockSpec((tm, cols), lambda s, i, chip_ref: (i, shard_col(s, chip_ref[0]))),
                         ANY) + tuple([ANY] * nco),
              scratch=[pltpu.VMEM(own.shape, own.dtype),
                       pltpu.SemaphoreType.DMA((3,)), pltpu.SemaphoreType.DMA((3,)),
                       pltpu.SemaphoreType.DMA((3,)), pltpu.SemaphoreType.DMA((3,)),
                       pltpu.SemaphoreType.DMA((2,)), pltpu.SemaphoreType.DMA((2,))] + comm.sems,
              sem=("arbitrary", "arbitrary"))(chip, x2b, own, own, *comm.ins)
    return res[0], res[1], res[2:]


def _rs_add_halves(name, grad, recv, core):
    _, r, cdim = grad.shape
    half = r // 2
    tr = _row_tile(half, cdim, mult=16)
    nr = half // tr

    def body(c_ref, g_ref, r_ref, o_ref):
        o_ref[...] = (g_ref[...].astype(F32) + r_ref[...].astype(F32)).astype(BF16)

    return _pc(body, name=name, out_shape=jax.ShapeDtypeStruct((N_CHIP, half, cdim), BF16),
               grid=(N_CHIP, nr), nsp=1,
               in_specs=[pl.BlockSpec((None, tr, cdim), lambda j, i, c_ref: (j, c_ref[0] * nr + i, 0)),
                         pl.BlockSpec((None, tr, cdim), lambda j, i, c_ref: (j, i, 0))],
               out_specs=pl.BlockSpec((None, tr, cdim), lambda j, i, c_ref: (j, i, 0)),
               sem=("parallel", "parallel"))(core, grad, recv)


def _chip_exchange_comm(parts):
    n = len(parts)

    def copies(ins, outs, sems):
        send_sems, recv_sems = sems
        x, y, c, peers = _chip_peers()
        return [pltpu.make_async_remote_copy(
            src_ref=ins[ti].at[2 * px + py], dst_ref=outs[ti].at[k],
            send_sem=send_sems.at[ti * 3 + k], recv_sem=recv_sems.at[ti * 3 + k],
            device_id=(px, py, c), device_id_type=MESH)
            for ti in range(n) for k, (px, py) in enumerate(peers)]

    def start(ins, outs, sems):
        for cp in copies(ins, outs, sems):
            cp.start()

    def finish(ins, outs, sems):
        for cp in copies(ins, outs, sems):
            cp.wait()

    return _Comm(parts, [jax.ShapeDtypeStruct((3,) + p.shape[1:], p.dtype) for p in parts],
                 [pltpu.SemaphoreType.DMA((3 * n,)), pltpu.SemaphoreType.DMA((3 * n,))], start, finish)


def _rs_sum_chips(name, part, recv, chip):
    _, half, cdim = recv.shape
    tr = _row_tile(half, cdim, mult=16)

    def body(chip_ref, p_ref, r_ref, o_ref):
        o_ref[...] = ((p_ref[...].astype(F32) + r_ref[0].astype(F32)) + r_ref[1].astype(F32)
                      ) + r_ref[2].astype(F32)

    return _pc(body, name=name, out_shape=jax.ShapeDtypeStruct((half, cdim), F32),
               grid=(half // tr,), nsp=1,
               in_specs=[pl.BlockSpec((None, tr, cdim), lambda i, chip_ref: (chip_ref[0], i, 0)),
                         pl.BlockSpec((3, tr, cdim), lambda i, chip_ref: (0, i, 0))],
               out_specs=pl.BlockSpec((tr, cdim), lambda i, chip_ref: (i, 0)),
               sem=("parallel",))(chip, part, recv)


def _rs_send_halves(halves):
    n = len(halves)

    def body(*refs):
        ins, outs = refs[:n], refs[n:2 * n]
        send_sems, recv_sems = refs[2 * n:]
        x, y, c = lax.axis_index("x"), lax.axis_index("y"), lax.axis_index("c")
        sends = []
        for ti in range(n):
            cp = pltpu.make_async_remote_copy(
                src_ref=ins[ti], dst_ref=outs[ti],
                send_sem=send_sems.at[ti], recv_sem=recv_sems.at[ti],
                device_id=(x, y, 1 - c), device_id_type=MESH)
            cp.start()
            sends.append(cp)
        for cp in sends:
            cp.wait()

    return _pc(body, name="rs_send_halves",
               out_shape=tuple(jax.ShapeDtypeStruct(hv.shape, hv.dtype) for hv in halves),
               in_specs=[ANY] * n, out_specs=tuple([ANY] * n),
               scratch=[pltpu.SemaphoreType.DMA((n,)), pltpu.SemaphoreType.DMA((n,))])(*halves)


def _adamw_rows(name, mine, theirs, w, m, v, core):
    half, cdim = mine.shape
    tr = _row_tile(half, cdim, budget=1 << 19)
    nrh = half // tr

    def body(c_ref, mine_ref, theirs_ref, w_ref, m_ref, v_ref, g_ref, d_ref, m2_ref, v2_ref):
        is_mine = (pl.program_id(0) // nrh) == c_ref[0]
        g = jnp.where(is_mine, mine_ref[...], theirs_ref[...])
        d, m2, v2 = _adamw(w_ref[...], g, m_ref[...], v_ref[...])
        g_ref[...] = g
        d_ref[...] = d
        m2_ref[...] = m2
        v2_ref[...] = v2

    htile = pl.BlockSpec((tr, cdim), lambda i, c_ref: (i % nrh, 0))
    tile = pl.BlockSpec((tr, cdim), lambda i, c_ref: (i, 0))
    shp = jax.ShapeDtypeStruct((2 * half, cdim), F32)
    return _pc(body, name=name, out_shape=(shp, shp, shp, shp), grid=(2 * nrh,), nsp=1,
               in_specs=[htile, htile, tile, tile, tile], out_specs=(tile, tile, tile, tile),
               sem=("parallel",))(core, mine, theirs, w, m, v)


def _adamw_whole(name, g, w, m, v):
    def body(g_ref, w_ref, m_ref, v_ref, d_ref, m2_ref, v2_ref):
        d, m2, v2 = _adamw(w_ref[...], g_ref[...], m_ref[...], v_ref[...])
        d_ref[...] = d
        m2_ref[...] = m2
        v2_ref[...] = v2

    shp = jax.ShapeDtypeStruct(g.shape, F32)
    return _pc(body, name=name, out_shape=(shp, shp, shp))(g, w, m, v)


SMALL_LAYOUT = (
    ("sgu_w_s", 1024, 1, 0),
    ("sgu_b_s", 8, 1, 1024),
    ("sgu_norm_g", 1, 0, 0),
    ("sgu_norm_b", 1, 0, 1),
    ("hgrn_norm_g", 1, 0, 3),
    ("ln1_g", 1, 0, 4),
    ("ln1_b", 1, 0, 5),
    ("ffn_conv_b", 1, 2, 3),
    ("ln2_g", 1, 0, 6),
    ("ln2_b", 1, 0, 7),
)
LB_ROW = 2
LOSS_ROW = 8
PACK_SHAPES = ((16, D_MODEL), (N_GROUP * 128 + 8, 128), (8, D_FF))


def _small_allreduce_adamw(rows1024, dws, dbs, dcw, dcb, logits, m_logits, v_logits,
                           small_w, small_m, small_v):
    ns = len(SMALL_LAYOUT)
    nr = len(rows1024)
    nb = len(PACK_SHAPES)

    def body(*refs):
        row_refs = refs[:nr]
        dws_ref, dbs_ref, dcw_ref, dcb_ref, lg_ref, mlg_ref, vlg_ref = refs[nr:nr + 7]
        pos = nr + 7
        w_refs = refs[pos:pos + ns]
        m_refs = refs[pos + ns:pos + 2 * ns]
        v_refs = refs[pos + 2 * ns:pos + 3 * ns]
        pos += 3 * ns
        loss_ref, dcw_out = refs[pos:pos + 2]
        lg_outs = refs[pos + 2:pos + 6]
        pos += 6
        outs = refs[pos:pos + 4 * ns]
        pos += 4 * ns
        pack = refs[pos:pos + nb]
        sib = refs[pos + nb:pos + 2 * nb]
        gath = refs[pos + 2 * nb:pos + 3 * nb]
        d2d_send, d2d_recv, ici_send, ici_recv = refs[pos + 3 * nb:]

        x, y, c, peers = _chip_peers()
        me = 2 * x + y
        sibling = (x, y, 1 - c)

        pack[0][...] = jnp.zeros(PACK_SHAPES[0], F32)
        for k in range(nr):
            pack[0][k:k + 1, :] = row_refs[k][0:1, :]
        pack[1][0:N_GROUP * 128, :] = dws_ref[...]
        pack[1][N_GROUP * 128:, :] = dbs_ref[...]
        pack[2][...] = jnp.zeros(PACK_SHAPES[2], F32)
        pack[2][0:3, :] = dcw_ref[0:3, :]
        pack[2][3:4, :] = dcb_ref[0:1, :]

        d2d = [pltpu.make_async_remote_copy(
            src_ref=pack[b], dst_ref=sib[b], send_sem=d2d_send.at[b], recv_sem=d2d_recv.at[b],
            device_id=sibling, device_id_type=MESH) for b in range(nb)]
        for cp in d2d:
            cp.start()
        for cp in d2d:
            cp.wait()
        for b in range(nb):
            gath[b][me] = pack[b][...] + sib[b][...]

        ici, ici_wait = [], []
        for b in range(nb):
            for k, (px, py) in enumerate(peers):
                sem = dict(send_sem=ici_send.at[b * 3 + k], recv_sem=ici_recv.at[b * 3 + k],
                           device_id=(px, py, c), device_id_type=MESH)
                ici.append(pltpu.make_async_remote_copy(src_ref=gath[b].at[me], dst_ref=gath[b].at[me], **sem))
                ici_wait.append(pltpu.make_async_remote_copy(
                    src_ref=gath[b].at[me], dst_ref=gath[b].at[2 * px + py], **sem))
        for cp in ici:
            cp.start()
        for cp in ici_wait:
            cp.wait_recv()
        for cp in ici:
            cp.wait_send()

        tot = pack
        for b in range(nb):
            tot[b][...] = ((gath[b][0] + gath[b][1]) + gath[b][2]) + gath[b][3]

        loss_ref[...] = tot[0][LOSS_ROW:LOSS_ROW + 1, :]
        dcw_out[...] = tot[2][...]
        lb = _sig(lg_ref[0:1, :] - lg_ref[1:2, :])
        d0 = tot[0][LB_ROW:LB_ROW + 1, :] * lb * (1.0 - lb)
        rowid = lax.broadcasted_iota(jnp.int32, (2, D_MODEL), 0)
        g_lg = jnp.where(rowid == 0, d0, -d0)
        dl, ml, vl = _adamw(lg_ref[...], g_lg, mlg_ref[...], vlg_ref[...])
        lg_outs[0][...] = g_lg
        lg_outs[1][...] = dl
        lg_outs[2][...] = ml
        lg_outs[3][...] = vl
        for si, (_, rows, b, r0) in enumerate(SMALL_LAYOUT):
            g = tot[b][r0:r0 + rows, :]
            dl, ml, vl = _adamw(w_refs[si][...], g, m_refs[si][...], v_refs[si][...])
            outs[4 * si][...] = g
            outs[4 * si + 1][...] = dl
            outs[4 * si + 2][...] = ml
            outs[4 * si + 3][...] = vl

    shapes = [jax.ShapeDtypeStruct((1, D_MODEL), F32), jax.ShapeDtypeStruct((8, D_FF), F32)]
    shapes += [jax.ShapeDtypeStruct((2, D_MODEL), F32)] * 4
    for w in small_w:
        shapes += [jax.ShapeDtypeStruct(w.shape, F32)] * 4
    scratch = [pltpu.VMEM(shp, F32) for shp in PACK_SHAPES]
    scratch += [pltpu.VMEM(shp, F32) for shp in PACK_SHAPES]
    scratch += [pltpu.VMEM((N_CHIP,) + shp, F32) for shp in PACK_SHAPES]
    scratch += [pltpu.SemaphoreType.DMA((nb,)), pltpu.SemaphoreType.DMA((nb,)),
                pltpu.SemaphoreType.DMA((3 * nb,)), pltpu.SemaphoreType.DMA((3 * nb,))]
    vm = pl.BlockSpec(memory_space=pltpu.VMEM)
    n_in = nr + 7 + 3 * ns
    res = _pc(body, name="small_allreduce_adamw", out_shape=tuple(shapes),
              in_specs=[vm] * n_in, out_specs=tuple([vm] * len(shapes)),
              scratch=scratch)(*rows1024, dws, dbs, dcw, dcb, logits, m_logits, v_logits,
                               *small_w, *small_m, *small_v)
    return res[0], res[1], res[2:6], res[6:]


def kernel(x, p, w_in, sgu_w_s, sgu_b_s, sgu_norm_g, sgu_norm_b, hgrn_lb_logits, hgrn_norm_g, w_branch, w_out, ln1_g, ln1_b, ffn_w_up, ffn_conv_w, ffn_conv_b, ffn_w_down, ln2_g, ln2_b, ple_w_proj, ple_w_gate, loss_target, m_w_in, m_sgu_w_s, m_sgu_b_s, m_sgu_norm_g, m_sgu_norm_b, m_hgrn_lb_logits, m_hgrn_norm_g, m_w_branch, m_w_out, m_ln1_g, m_ln1_b, m_ffn_w_up, m_ffn_conv_w, m_ffn_conv_b, m_ffn_w_down, m_ln2_g, m_ln2_b, m_ple_w_proj, m_ple_w_gate, v_w_in, v_sgu_w_s, v_sgu_b_s, v_sgu_norm_g, v_sgu_norm_b, v_hgrn_lb_logits, v_hgrn_norm_g, v_w_branch, v_w_out, v_ln1_g, v_ln1_b, v_ffn_w_up, v_ffn_conv_w, v_ffn_conv_b, v_ffn_w_down, v_ln2_g, v_ln2_b, v_ple_w_proj, v_ple_w_gate):
    t = x.shape[1]
    x2 = x.reshape(t, D_MODEL)
    x2b = x2.astype(BF16)
    p2 = p.reshape(t, PLE_DIM)
    tgt = loss_target.reshape(t, D_MODEL)
    core = lax.axis_index("c").astype(jnp.int32).reshape(1)
    chip_id = (2 * lax.axis_index("x") + lax.axis_index("y")).astype(jnp.int32).reshape(1)

    big_w = [w_in[0], w_branch[0, 0], w_branch[0, 1], w_out[0], ffn_w_up[0], ffn_w_down[0],
             ple_w_proj[0], ple_w_gate[0]]
    big_m = [m_w_in[0], m_w_branch[0, 0], m_w_branch[0, 1], m_w_out[0], m_ffn_w_up[0],
             m_ffn_w_down[0], m_ple_w_proj[0], m_ple_w_gate[0]]
    big_v = [v_w_in[0], v_w_branch[0, 0], v_w_branch[0, 1], v_w_out[0], v_ffn_w_up[0],
             v_ffn_w_down[0], v_ple_w_proj[0], v_ple_w_gate[0]]
    def halves_of(i):
        w = big_w[i]
        return w.astype(BF16).reshape(2, w.shape[0] // 2, w.shape[1])

    def stacked(g, i):
        return g.reshape(N_CHIP, big_w[i].shape[0], big_w[i].shape[1])


    cid = jnp.arange(SGU_BLOCK) // CHUNK
    maskf = (cid[:, None] >= cid[None, :]).astype(F32)
    ws_masked = sgu_w_s[0] * maskf[None]
    wm = ws_masked.astype(BF16)
    wmt = jnp.transpose(ws_masked, (0, 2, 1)).astype(BF16)
    bsb = jnp.broadcast_to(sgu_b_s[0][:, :, None], (N_GROUP, SGU_BLOCK, 128))

    h, win_g, (wup_g,) = _in_proj_gathering(x2b, halves_of(0), chip_id, 512, _gather_comm([halves_of(4)]))
    win_st = stacked(win_g, 0)
    wup_st = stacked(wup_g, 4)
    ya, _ = _sgu_fwd(h, wm, bsb, sgu_norm_g, sgu_norm_b)
    (yb, st_all), mix_g = _hgrn_fwd(h, hgrn_lb_logits, hgrn_norm_g,
                                     comm=_gather_comm([halves_of(i) for i in (1, 2, 3)], [ffn_conv_w[0]]))
    wb0, wb1, wo = [stacked(g, i).reshape(D_MODEL, D_MODEL) for g, i in zip(mix_g[:3], (1, 2, 3))]
    convw = jnp.transpose(mix_g[3], (1, 0, 2)).reshape(3, D_FF)
    (r1, a_br, b_br, m_bf, x1b), _ = _mix_fwd(ya, yb, h, x2, wb0, wb1, wo, ln1_g, ln1_b, 256)
    h2, act, out_g = _ffn_up_act(x1b, wup_st, convw, ffn_conv_b, 512,
                                 _gather_comm([halves_of(i) for i in (5, 6, 7)]))
    wd = stacked(out_g[0], 5).reshape(D_FF, D_MODEL)
    wpp = jnp.transpose(stacked(out_g[1], 6), (1, 0, 2)).reshape(PLE_DIM, D_MODEL)
    wpg = stacked(out_g[2], 7).reshape(D_MODEL, D_MODEL)
    dr2, dpg, dpp, loss_acc, dg2, db2 = _out_fwd_bwd(
        act, x1b, r1, p2, tgt, wd, wpg, wpp, ln1_g, ln1_b, ln2_g, ln2_b, 256)

    dh2, dr1, dcw, dcb, dg1, db1 = _ffn_bwd(h2, dr2, wd, wup_st, dpg, wpg, r1, ln1_g, convw, ffn_conv_b, 256)
    d_wd = _mm_tn("ffn_down_wgrad", act, dr2, FF_TILE, 512)
    d_wpg = _mm_tn("ple_gate_wgrad", x1b, dpg, 512, D_MODEL)
    d_wpp_st = _mm_tn("ple_proj_wgrad", p2, dpp, PLE_DIM, PLE_DIM, stacked=True)
    d_wup_st = _mm("ffn_up_wgrad", x1b, dh2, TN, (2, N_CHIP),
                   pl.BlockSpec((t, 512), lambda i, j: (0, i)),
                   pl.BlockSpec((None, t, FF_TILE), lambda i, j: (j // FF_NJ, 0, j % FF_NJ)),
                   jax.ShapeDtypeStruct((N_CHIP, D_MODEL, FF_TILE), BF16),
                   pl.BlockSpec((None, 512, FF_TILE), lambda i, j: (j, i, 0)))
    da_bf, db_bf, dh3, dya, dyb = _mix_bwd(dr1, h, a_br, b_br, wo, wb0, wb1, 256)
    d_wo = _mm_tn("out_proj_wgrad", m_bf, dr1, 512, 512)
    d_wb0 = _mm_tn("branch0_wgrad", ya, da_bf, 512, D_MODEL)
    d_wb1 = _mm_tn("branch1_wgrad", yb, db_bf, 512, D_MODEL)
    grads_1 = [d_wb0.reshape(4, 256, D_MODEL), d_wb1.reshape(4, 256, D_MODEL),
               d_wo.reshape(4, 256, D_MODEL), d_wup_st, d_wd.reshape(4, D_FF // 4, D_MODEL),
               d_wpp_st, d_wpg.reshape(4, 256, D_MODEL)]
    (dh0, dws, dbs, dgv, dbv), recv_a1 = _sgu_bwd(h, dya, wm, wmt, bsb, sgu_norm_g, sgu_norm_b, maskf,
                                                  comm=_sibling_exchange_comm(grads_1))
    parts_1 = [_rs_add_halves("rs_add_halves%d" % (i + 1), g, r, core)
               for i, (g, r) in enumerate(zip(grads_1, recv_a1))]
    (dh1, dh2h, dlb, dgn), recv_b1 = _hgrn_bwd(h, dyb, st_all, hgrn_lb_logits, hgrn_norm_g,
                                                comm=_chip_exchange_comm(parts_1))
    dh_parts = [dh0, dh1, dh2h, dh3]
    d_win = [_mm_tn("in_proj_wgrad%d" % j, x2b, dh_parts[j], 512, D_MODEL) for j in range(4)]

    grads_0 = [jnp.stack(d_win)]
    recv_a0 = _run_comm("rs_sibling_exchange0", _sibling_exchange_comm(grads_0))
    parts_0 = [_rs_add_halves("rs_add_halves0", grads_0[0], recv_a0[0], core)]
    gx, recv_b0 = _in_proj_xgrad(dh_parts, win_st, dr1, 512, comm=_chip_exchange_comm(parts_0))
    parts = parts_0 + parts_1
    recv_b = list(recv_b0) + list(recv_b1)
    halves = [_rs_sum_chips("rs_sum_chips%d" % i, pt, r, chip_id)
              for i, (pt, r) in enumerate(zip(parts, recv_b))]
    theirs = _rs_send_halves(halves)
    big_out = [_adamw_rows("adamw_big%d" % i, halves[i], theirs[i], big_w[i], big_m[i], big_v[i], core)
               for i in range(len(halves))]

    small_in = dict(sgu_w_s=(sgu_w_s, m_sgu_w_s, v_sgu_w_s), sgu_b_s=(sgu_b_s, m_sgu_b_s, v_sgu_b_s),
                    sgu_norm_g=(sgu_norm_g, m_sgu_norm_g, v_sgu_norm_g),
                    sgu_norm_b=(sgu_norm_b, m_sgu_norm_b, v_sgu_norm_b),
                    hgrn_norm_g=(hgrn_norm_g, m_hgrn_norm_g, v_hgrn_norm_g),
                    ln1_g=(ln1_g, m_ln1_g, v_ln1_g), ln1_b=(ln1_b, m_ln1_b, v_ln1_b),
                    ffn_conv_b=(ffn_conv_b, m_ffn_conv_b, v_ffn_conv_b),
                    ln2_g=(ln2_g, m_ln2_g, v_ln2_g), ln2_b=(ln2_b, m_ln2_b, v_ln2_b))

    def flat(name, arr):
        rows = dict((n, r) for n, r, _, _ in SMALL_LAYOUT)[name]
        return arr.reshape(rows, arr.size // rows)

    names = [n for n, _, _, _ in SMALL_LAYOUT]
    sw = [flat(n, small_in[n][0]) for n in names]
    sm = [flat(n, small_in[n][1]) for n in names]
    sv = [flat(n, small_in[n][2]) for n in names]
    loss_rows, dcw_tot, lg_out, small_out = _small_allreduce_adamw(
        [dgv, dbv, dlb, dgn, dg1, db1, dg2, db2, loss_acc], dws.reshape(N_GROUP * 128, 128), dbs, dcw, dcb,
        hgrn_lb_logits, m_hgrn_lb_logits, v_hgrn_lb_logits, sw, sm, sv)
    loss = loss_rows[0, 0]

    chip = 2 * lax.axis_index("x") + lax.axis_index("y")
    g_cw = lax.dynamic_slice(dcw_tot, (0, chip * (D_FF // 4)), (3, D_FF // 4))
    cw_out = _adamw_whole("adamw_conv_w", g_cw, ffn_conv_w[0], m_ffn_conv_w[0], v_ffn_conv_w[0])

    res = {}
    for si, n in enumerate(names):
        shp = small_in[n][0].shape
        res[n] = tuple(small_out[4 * si + k].reshape(shp) for k in range(4))
    res["hgrn_lb_logits"] = tuple(lg_out)
    res["ffn_conv_w"] = (g_cw[None],) + tuple(o[None] for o in cw_out)

    def big(i):
        return tuple(big_out[i])

    res["w_in"] = tuple(o[None] for o in big(0))
    res["w_branch"] = tuple(jnp.stack([o0, o1])[None] for o0, o1 in zip(big(1), big(2)))
    res["w_out"] = tuple(o[None] for o in big(3))
    res["ffn_w_up"] = tuple(o[None] for o in big(4))
    res["ffn_w_down"] = tuple(o[None] for o in big(5))
    res["ple_w_proj"] = tuple(o[None] for o in big(6))
    res["ple_w_gate"] = tuple(o[None] for o in big(7))

    order = ["w_in", "sgu_w_s", "sgu_b_s", "sgu_norm_g", "sgu_norm_b", "hgrn_lb_logits",
             "hgrn_norm_g", "w_branch", "w_out", "ln1_g", "ln1_b", "ffn_w_up", "ffn_conv_w",
             "ffn_conv_b", "ffn_w_down", "ln2_g", "ln2_b", "ple_w_proj", "ple_w_gate"]
    outs = [loss, gx.reshape(1, t, D_MODEL)]
    for k in range(4):
        outs += [res[n][k] for n in order]
    return tuple(outs)
```

```python
import functools

import jax
import jax.numpy as jnp
from jax import lax
from jax.experimental import pallas as pl
from jax.experimental.pallas import tpu as pltpu

F32 = jnp.float32
BF16 = jnp.bfloat16
HIGHEST = lax.Precision.HIGHEST
MESH = pl.DeviceIdType.MESH

D_MODEL = 1024
CHUNK = 64
SGU_BLOCK = 128
N_GROUP = 8
N_HEAD = 8
HEAD_DIM = 128
D_FF = 2816
PLE_DIM = 256
IN_COLS = 8192
LN_EPS = 1e-5
RMS_EPS = 1e-6
ALPHA = 2.0 ** 0.25
N_CHIP = 4
N_DEV = 8

ADAM_LR = 0.001
ADAM_B1 = 0.9
ADAM_B2 = 0.999
ADAM_EPS = 1e-08
ADAM_WD = 0.01
ADAM_STEP = 10

VMEM_LIMIT = 56 * 1024 * 1024

NN = (((1,), (0,)), ((), ()))
NT = (((1,), (1,)), ((), ()))
TN = (((0,), (0,)), ((), ()))


def _pc(body, *, name, out_shape, grid=None, in_specs=None, out_specs=None, scratch=(),
        sem=None, nsp=0, vmem=VMEM_LIMIT):
    params = dict(vmem_limit_bytes=vmem)
    if sem is not None:
        params["dimension_semantics"] = sem
    kw = dict(name=name, out_shape=out_shape, compiler_params=pltpu.CompilerParams(**params))
    if nsp:
        kw["grid_spec"] = pltpu.PrefetchScalarGridSpec(
            num_scalar_prefetch=nsp, grid=grid, in_specs=in_specs, out_specs=out_specs,
            scratch_shapes=list(scratch))
    else:
        if grid is not None:
            kw["grid"] = grid
        if in_specs is not None:
            kw["in_specs"] = in_specs
            kw["out_specs"] = out_specs
        kw["scratch_shapes"] = list(scratch)
    return pl.pallas_call(body, **kw)


def _dot(a, b, dims=NN):
    return lax.dot_general(a.astype(BF16), b.astype(BF16), dims, preferred_element_type=F32)


def _dot32(a, b, dims=NN):
    return lax.dot_general(a, b, dims, precision=HIGHEST, preferred_element_type=F32)


def _sig(x):
    return 1.0 / (1.0 + jnp.exp(-x))


_GC = 0.7978845608028654
_GA = 0.044715


def _gelu(x):
    return 0.5 * x * (1.0 + jnp.tanh(_GC * (x + _GA * x * x * x)))


def _gelu_and_grad(x):
    t = jnp.tanh(_GC * (x + _GA * x * x * x))
    g = 0.5 * x * (1.0 + t)
    dg = 0.5 * (1.0 + t) + 0.5 * x * (1.0 - t * t) * _GC * (1.0 + 3.0 * _GA * x * x)
    return g, dg


def _ln_stats(r):
    mu = jnp.mean(r, axis=-1, keepdims=True)
    xc = r - mu
    var = jnp.mean(xc * xc, axis=-1, keepdims=True)
    rstd = lax.rsqrt(var + LN_EPS)
    return xc * rstd, rstd


def _ln_bwd(dxh, xh, rstd):
    m1 = jnp.mean(dxh, axis=-1, keepdims=True)
    m2 = jnp.mean(dxh * xh, axis=-1, keepdims=True)
    return rstd * (dxh - m1 - xh * m2)


def _colsum8(v):
    return jnp.broadcast_to(jnp.sum(v, axis=0, keepdims=True), (8, v.shape[1]))


def _adamw(w, g, m, v):
    m2 = ADAM_B1 * m + (1.0 - ADAM_B1) * g
    v2 = ADAM_B2 * v + (1.0 - ADAM_B2) * (g * g)
    m_hat = m2 / (1.0 - ADAM_B1 ** ADAM_STEP)
    v_hat = v2 / (1.0 - ADAM_B2 ** ADAM_STEP)
    delta = -ADAM_LR * (m_hat / (jnp.sqrt(v_hat) + ADAM_EPS) + ADAM_WD * w)
    return delta, m2, v2


def _row_tile(rows, cols, itemsize=4, budget=1 << 20, mult=8):
    best = mult
    for tr in range(mult, rows + 1, mult):
        if rows % tr == 0 and tr * cols * itemsize <= budget:
            best = tr
    return best


def _mm(name, a, b, dims, grid, a_spec, b_spec, out_shape, o_spec):
    out_dtype = out_shape.dtype

    def body(a_ref, b_ref, o_ref):
        o_ref[...] = _dot(a_ref[...], b_ref[...], dims).astype(out_dtype)

    return _pc(body, name=name, out_shape=out_shape, grid=grid, in_specs=[a_spec, b_spec],
               out_specs=o_spec, sem=("parallel", "parallel"))(a, b)


class _Comm:
    def __init__(self, ins, out_shapes, sems, start, finish):
        self.ins, self.out_shapes, self.sems = list(ins), list(out_shapes), list(sems)
        self.start, self.finish = start, finish


def _hosted_call(body, comm, first, last, *, name, out_shape, grid, in_specs, out_specs, scratch, sem,
                 args):
    n_in, n_out, n_scr = len(in_specs), len(out_shape), len(scratch)
    nci, nco = len(comm.ins), len(comm.out_shapes)

    def wrapped(*refs):
        pos = n_in
        own_in, c_in = refs[:pos], refs[pos:pos + nci]
        pos += nci
        own_out, c_out = refs[pos:pos + n_out], refs[pos + n_out:pos + n_out + nco]
        pos += n_out + nco
        own_scr, c_sem = refs[pos:pos + n_scr], refs[pos + n_scr:]

        @pl.when(first())
        def _():
            comm.start(c_in, c_out, c_sem)

        body(*own_in, *own_out, *own_scr)

        @pl.when(last())
        def _():
            comm.finish(c_in, c_out, c_sem)

    return _pc(wrapped, name=name, out_shape=tuple(out_shape) + tuple(comm.out_shapes), grid=grid,
               in_specs=list(in_specs) + [ANY] * nci, out_specs=tuple(out_specs) + tuple([ANY] * nco),
               scratch=list(scratch) + comm.sems, sem=sem)(*args, *comm.ins)


def _grid1_call(body, comm, n, *, name, out_shape, in_specs, out_specs, scratch, args):
    if comm is None:
        return _pc(body, name=name, out_shape=out_shape, grid=(n,), in_specs=in_specs,
                   out_specs=out_specs, scratch=scratch, sem=("arbitrary",))(*args), ()
    res = _hosted_call(body, comm, lambda: pl.program_id(0) == 0, lambda: pl.program_id(0) == n - 1,
                       name=name, out_shape=out_shape, grid=(n,), in_specs=in_specs,
                       out_specs=out_specs, scratch=scratch, sem=("arbitrary",), args=args)
    return res[:len(out_shape)], res[len(out_shape):]


def _run_comm(name, comm):
    nci, nco = len(comm.ins), len(comm.out_shapes)

    def body(*refs):
        c_in, c_out, c_sem = refs[:nci], refs[nci:nci + nco], refs[nci + nco:]
        comm.start(c_in, c_out, c_sem)
        comm.finish(c_in, c_out, c_sem)

    return _pc(body, name=name, out_shape=tuple(comm.out_shapes), in_specs=[ANY] * nci,
               out_specs=tuple([ANY] * nco), scratch=comm.sems)(*comm.ins)


def _mm_tn(name, a, b, tm, tn, stacked=False):
    t, m = a.shape
    _, n = b.shape
    if stacked:
        assert tm == m
        out_shape = jax.ShapeDtypeStruct((n // tn, m, tn), BF16)
        o_spec = pl.BlockSpec((None, tm, tn), lambda i, j: (j, 0, 0))
    else:
        out_shape = jax.ShapeDtypeStruct((m, n), BF16)
        o_spec = pl.BlockSpec((tm, tn), lambda i, j: (i, j))
    return _mm(name, a, b, TN, (m // tm, n // tn),
               pl.BlockSpec((t, tm), lambda i, j: (0, i)),
               pl.BlockSpec((t, tn), lambda i, j: (0, j)),
               out_shape, o_spec)


def _in_proj_xgrad(dh_parts, win_st, dr1, tm, comm=None):
    t = dr1.shape[0]
    ni = t // tm

    def body(a0, a1, a2, a3, b_ref, add_ref, o_ref, acc):
        j = pl.program_id(1)
        for jj, a_ref in enumerate((a0, a1, a2, a3)):
            @pl.when(j == jj)
            def _(jj=jj, a_ref=a_ref):
                prod = _dot(a_ref[...], b_ref[...], NT)
                if jj == 0:
                    acc[...] = prod + ALPHA * add_ref[...]
                elif jj < N_CHIP - 1:
                    acc[...] += prod
                else:
                    o_ref[...] = acc[...] + prod

    a_spec = pl.BlockSpec((tm, 2 * D_MODEL), lambda i, j: (i, 0))
    tile = pl.BlockSpec((tm, D_MODEL), lambda i, j: (i, 0))
    kw = dict(name="in_proj_xgrad", out_shape=(jax.ShapeDtypeStruct((t, D_MODEL), F32),),
              grid=(ni, N_CHIP),
              in_specs=[a_spec] * 4 + [pl.BlockSpec((None, D_MODEL, 2 * D_MODEL), lambda i, j: (j, 0, 0)),
                                       tile],
              out_specs=(tile,), scratch=[pltpu.VMEM((tm, D_MODEL), F32)],
              sem=("arbitrary", "arbitrary"))
    args = list(dh_parts) + [win_st, dr1]
    if comm is None:
        return _pc(body, **kw)(*args)[0], ()
    res = _hosted_call(body, comm,
                       lambda: (pl.program_id(0) == 0) & (pl.program_id(1) == 0),
                       lambda: (pl.program_id(0) == ni - 1) & (pl.program_id(1) == N_CHIP - 1),
                       args=args, **kw)
    return res[0], res[1:]


def _sgu_mixed(v, wm_ref, bsb_ref, gv, bv):
    gl, dgl = _gelu_and_grad(v)
    vh, rstd = _ln_stats(gl)
    vn = vh * gv + bv
    mixed = []
    for g in range(N_GROUP):
        sl = slice(g * 128, (g + 1) * 128)
        mixed.append(_dot(wm_ref[g], vn[:, sl]) + bsb_ref[g])
    return dgl, vh, rstd, vn, mixed


def _sgu_fwd(h, wm, bsb, gv, bv, comm=None):
    t = h.shape[0]

    def body(u_ref, v_ref, wm_ref, bsb_ref, gv_ref, bv_ref, ya_ref):
        u = u_ref[...].astype(F32)
        _, _, _, _, mixed = _sgu_mixed(v_ref[...].astype(F32), wm_ref, bsb_ref, gv_ref[...], bv_ref[...])
        gu = _gelu(u)
        for g in range(N_GROUP):
            sl = slice(g * 128, (g + 1) * 128)
            ya_ref[:, sl] = (gu[:, sl] * mixed[g]).astype(BF16)

    full3 = pl.BlockSpec((N_GROUP, 128, 128), lambda i: (0, 0, 0))
    vec = pl.BlockSpec((1, D_MODEL), lambda i: (0, 0))
    (ya,), extra = _grid1_call(
        body, comm, t // SGU_BLOCK, name="sgu_fwd",
        out_shape=(jax.ShapeDtypeStruct((t, D_MODEL), BF16),),
        in_specs=[pl.BlockSpec((SGU_BLOCK, D_MODEL), lambda i: (i, 0)),
                  pl.BlockSpec((SGU_BLOCK, D_MODEL), lambda i: (i, 1)),
                  full3, full3, vec, vec],
        out_specs=(pl.BlockSpec((SGU_BLOCK, D_MODEL), lambda i: (i, 0)),),
        scratch=[], args=(h, h, wm, bsb, gv, bv))
    return ya, extra


def _sgu_bwd(h, dya, wm, wmt, bsb, gv, bv, maskf, comm=None):
    t = h.shape[0]
    nb = t // SGU_BLOCK

    def body(u_ref, v_ref, dya_ref, wm_ref, wmt_ref, bsb_ref, gv_ref, bv_ref, mask_ref,
             dh_ref, dws_ref, dbs_ref, dgv_ref, dbv_ref, dmix_acc):
        i = pl.program_id(0)

        @pl.when(i == 0)
        def _():
            dws_ref[...] = jnp.zeros_like(dws_ref)
            dgv_ref[...] = jnp.zeros_like(dgv_ref)
            dbv_ref[...] = jnp.zeros_like(dbv_ref)
            dmix_acc[...] = jnp.zeros_like(dmix_acc)

        u = u_ref[...].astype(F32)
        gvv = gv_ref[...]
        dgl_v, vh, rstd, vn, mixed = _sgu_mixed(v_ref[...].astype(F32), wm_ref, bsb_ref, gvv, bv_ref[...])
        gu, dgl_u = _gelu_and_grad(u)
        dya_v = dya_ref[...]
        dvn_parts = []
        for g in range(N_GROUP):
            sl = slice(g * 128, (g + 1) * 128)
            d_y = dya_v[:, sl]
            dh_ref[:, sl] = (d_y * mixed[g] * dgl_u[:, sl]).astype(BF16)
            d_mixed = d_y * gu[:, sl]
            dmix_acc[g] += d_mixed
            dws_ref[g] += _dot(d_mixed, vn[:, sl], NT) * mask_ref[...]
            dvn_parts.append(_dot(wmt_ref[g], d_mixed))
        dvn = jnp.concatenate(dvn_parts, axis=1)
        dgv_ref[...] += _colsum8(dvn * vh)
        dbv_ref[...] += _colsum8(dvn)
        d_gl = _ln_bwd(dvn * gvv, vh, rstd)
        dh_ref[:, D_MODEL:] = (d_gl * dgl_v).astype(BF16)

        @pl.when(i == nb - 1)
        def _():
            rowid = lax.broadcasted_iota(jnp.int32, (8, 128), 0)
            ones = jnp.ones((8, 128), F32)
            acc = jnp.zeros((8, 128), F32)
            for g in range(N_GROUP):
                rs = _dot32(ones, dmix_acc[g], NT)
                acc = jnp.where(rowid == g, rs, acc)
            dbs_ref[...] = acc

    full3 = pl.BlockSpec((N_GROUP, 128, 128), lambda i: (0, 0, 0))
    vec = pl.BlockSpec((1, D_MODEL), lambda i: (0, 0))
    acc8 = pl.BlockSpec((8, D_MODEL), lambda i: (0, 0))
    return _grid1_call(
        body, comm, nb, name="sgu_bwd",
        out_shape=(jax.ShapeDtypeStruct((t, 2 * D_MODEL), BF16),
                   jax.ShapeDtypeStruct((N_GROUP, 128, 128), F32),
                   jax.ShapeDtypeStruct((8, 128), F32),
                   jax.ShapeDtypeStruct((8, D_MODEL), F32),
                   jax.ShapeDtypeStruct((8, D_MODEL), F32)),
        in_specs=[pl.BlockSpec((SGU_BLOCK, D_MODEL), lambda i: (i, 0)),
                  pl.BlockSpec((SGU_BLOCK, D_MODEL), lambda i: (i, 1)),
                  pl.BlockSpec((SGU_BLOCK, D_MODEL), lambda i: (i, 0)),
                  full3, full3, full3, vec, vec,
                  pl.BlockSpec((128, 128), lambda i: (0, 0))],
        out_specs=(pl.BlockSpec((SGU_BLOCK, 2 * D_MODEL), lambda i: (i, 0)),
                   full3, pl.BlockSpec((8, 128), lambda i: (0, 0)), acc8, acc8),
        scratch=[pltpu.VMEM((N_GROUP, 128, 128), F32)],
        args=(h, h, dya, wm, wmt, bsb, gv, bv, maskf))


def _tri_masks():
    row = lax.broadcasted_iota(jnp.int32, (CHUNK, CHUNK), 0)
    col = lax.broadcasted_iota(jnp.int32, (CHUNK, CHUNK), 1)
    return col <= row, col >= row


def _heads(v):
    return [v[:, hd * HEAD_DIM:(hd + 1) * HEAD_DIM] for hd in range(N_HEAD)]


def _tri_cumsum(tri_bf, v):
    hi = v.astype(BF16)
    r = v - hi.astype(F32)
    mid = r.astype(BF16)
    lo = (r - mid.astype(F32)).astype(BF16)
    return _dot(tri_bf, hi) + _dot(tri_bf, mid) + _dot(tri_bf, lo)


def _hgrn_chunk(q, fp, ii, lb, st_heads, causal):
    sg = _sig(fp)
    f = lb + (1.0 - lb) * sg
    k = 1.0 - f
    c = _tri_cumsum(causal.astype(BF16), jnp.log(f))
    ec = jnp.exp(c)
    en = jnp.exp(-c)
    sq = _sig(q)
    qt = q * sq * ec
    kt = k * en
    ecl = jnp.exp(c[CHUNK - 1:CHUNK, :])
    kk = kt * ecl
    qtb, ktb, iib, kkb = qt.astype(BF16), kt.astype(BF16), ii.astype(BF16), kk.astype(BF16)
    attn, o = [], []
    for hd, (qh, kh, ih) in enumerate(zip(_heads(qtb), _heads(ktb), _heads(iib))):
        a = jnp.where(causal, _dot(qh, kh, NT), 0.0).astype(BF16)
        attn.append(a)
        o.append(_dot(a, ih) + _dot(qh, st_heads[hd], NT))
    return dict(sg=sg, f=f, k=k, ec=ec, en=en, sq=sq, ecl=ecl, kk=kk, qtb=qtb, ktb=ktb, iib=iib,
                kkb=kkb, attn=attn, o=o)


def _rms_heads(o_heads):
    rinv = [lax.rsqrt(jnp.mean(o * o, axis=-1, keepdims=True) + RMS_EPS) for o in o_heads]
    return rinv, jnp.concatenate([o * r for o, r in zip(o_heads, rinv)], axis=1)


HG_CHUNKS = 4
HG_ROWS = HG_CHUNKS * CHUNK


def _hgrn_fwd(h, logits, gn, comm=None):
    t = h.shape[0]
    nb = t // HG_ROWS

    def body(q_ref, f_ref, i_ref, og_ref, lg_ref, gn_ref, yb_ref, st_ref, state):
        @pl.when(pl.program_id(0) == 0)
        def _():
            state[...] = jnp.zeros_like(state)

        causal, _ = _tri_masks()
        lb = _sig(lg_ref[0:1, :] - lg_ref[1:2, :])
        gnv = gn_ref[...]
        st = [state[hd] for hd in range(N_HEAD)]
        for cc in range(HG_CHUNKS):
            rows = slice(cc * CHUNK, (cc + 1) * CHUNK)
            og = og_ref[rows, :].astype(F32)
            r = _hgrn_chunk(q_ref[rows, :].astype(F32), f_ref[rows, :].astype(F32),
                            i_ref[rows, :].astype(F32), lb, [s.astype(BF16) for s in st], causal)
            _, on = _rms_heads(r["o"])
            yb_ref[rows, :] = (on * gnv * (og * _sig(og))).astype(BF16)
            for hd in range(N_HEAD):
                st_ref[cc, hd] = st[hd]
            st = [s * e + _dot(ih, kh, TN)
                  for s, e, ih, kh in zip(st, _heads(r["ecl"]), _heads(r["iib"]), _heads(r["kkb"]))]
        for hd in range(N_HEAD):
            state[hd] = st[hd]

    def col(k):
        return pl.BlockSpec((HG_ROWS, D_MODEL), lambda ci: (ci, k))

    return _grid1_call(body, comm, nb, name="hgrn_fwd",
                       out_shape=(jax.ShapeDtypeStruct((t, D_MODEL), BF16),
                                  jax.ShapeDtypeStruct((t // CHUNK, N_HEAD, HEAD_DIM, HEAD_DIM), F32)),
                       in_specs=[col(2), col(3), col(4), col(5),
                                 pl.BlockSpec((2, D_MODEL), lambda ci: (0, 0)),
                                 pl.BlockSpec((1, D_MODEL), lambda ci: (0, 0))],
                       out_specs=(pl.BlockSpec((HG_ROWS, D_MODEL), lambda ci: (ci, 0)),
                                  pl.BlockSpec((HG_CHUNKS, N_HEAD, HEAD_DIM, HEAD_DIM),
                                               lambda ci: (ci, 0, 0, 0))),
                       scratch=[pltpu.VMEM((N_HEAD, HEAD_DIM, HEAD_DIM), F32)],
                       args=(h, h, h, h, logits, gn))


def _hgrn_chunk_bwd(q, fp, ii, og, dy, gnv, lb, st, dsn, causal, anti):
    stb = [s.astype(BF16) for s in st]
    dsnb = [s.astype(BF16) for s in dsn]
    r = _hgrn_chunk(q, fp, ii, lb, stb, causal)
    rinv, on = _rms_heads(r["o"])
    so = _sig(og)
    sil = og * so
    d_og = dy * on * gnv * (so * (1.0 + og * (1.0 - so)))
    d_on = dy * gnv * sil
    d_ob = jnp.concatenate(
        [ri * (dn - oh * jnp.mean(dn * oh, axis=-1, keepdims=True))
         for ri, dn, oh in zip(rinv, _heads(d_on), _heads(on))], axis=1).astype(BF16)
    d_i, d_qt, d_kt, d_kk, d_st, st_dsn = [], [], [], [], [], []
    ecl = _heads(r["ecl"])
    for hd, (dh, qh, kh, ih, kkh) in enumerate(zip(_heads(d_ob), _heads(r["qtb"]), _heads(r["ktb"]),
                                                   _heads(r["iib"]), _heads(r["kkb"]))):
        d_attn = jnp.where(causal, _dot(dh, ih, NT), 0.0).astype(BF16)
        d_i.append(_dot(r["attn"][hd], dh, TN) + _dot(kkh, dsnb[hd], NT))
        d_qt.append(_dot(d_attn, kh) + _dot(dh, stb[hd]))
        d_kt.append(_dot(d_attn, qh, TN))
        d_kk.append(_dot(ih, dsnb[hd]))
        d_st.append(_dot(dh, qh, TN) + dsn[hd] * ecl[hd])
        st_dsn.append(jnp.sum(st[hd] * dsn[hd], axis=0, keepdims=True))
    d_qt = jnp.concatenate(d_qt, axis=1)
    d_kt = jnp.concatenate(d_kt, axis=1)
    d_kk = jnp.concatenate(d_kk, axis=1)
    kk = r["kk"]
    d_cl = r["ecl"] * jnp.concatenate(st_dsn, axis=1) + jnp.sum(kk * d_kk, axis=0, keepdims=True)
    d_k = (d_kk * r["ecl"] + d_kt) * r["en"]
    d_c = d_qt * r["qtb"].astype(F32) - d_kt * r["ktb"].astype(F32) - d_kk * kk
    rowid = lax.broadcasted_iota(jnp.int32, (CHUNK, D_MODEL), 0)
    d_c = d_c + jnp.where(rowid == CHUNK - 1, d_cl, 0.0)
    d_lf = _tri_cumsum(anti.astype(BF16), d_c)
    d_f = d_lf / r["f"] - d_k
    sg, sq = r["sg"], r["sq"]
    d_q = d_qt * r["ec"] * (sq * (1.0 + q * (1.0 - sq)))
    d_fp = d_f * (1.0 - lb) * sg * (1.0 - sg)
    return (d_q, d_fp, jnp.concatenate(d_i, axis=1), d_og, d_st,
            _colsum8(dy * on * sil), _colsum8(d_f * (1.0 - sg)))


def _hgrn_bwd(h, dyb, st_all, logits, gn, comm=None):
    t = h.shape[0]
    nb = t // HG_ROWS

    def body(q_ref, f_ref, i_ref, og_ref, dyb_ref, st_ref, lg_ref, gn_ref,
             dh1_ref, dh2_ref, dlb_ref, dgn_ref, dstate):
        @pl.when(pl.program_id(0) == 0)
        def _():
            dstate[...] = jnp.zeros_like(dstate)
            dlb_ref[...] = jnp.zeros_like(dlb_ref)
            dgn_ref[...] = jnp.zeros_like(dgn_ref)

        causal, anti = _tri_masks()
        lb = _sig(lg_ref[0:1, :] - lg_ref[1:2, :])
        gnv = gn_ref[...]
        dsn = [dstate[hd] for hd in range(N_HEAD)]
        dgn_acc = jnp.zeros((8, D_MODEL), F32)
        dlb_acc = jnp.zeros((8, D_MODEL), F32)
        for cc in reversed(range(HG_CHUNKS)):
            rows = slice(cc * CHUNK, (cc + 1) * CHUNK)
            d_q, d_fp, d_i, d_og, dsn, dgn_c, dlb_c = _hgrn_chunk_bwd(
                q_ref[rows, :].astype(F32), f_ref[rows, :].astype(F32), i_ref[rows, :].astype(F32),
                og_ref[rows, :].astype(F32), dyb_ref[rows, :], gnv, lb,
                [st_ref[cc, hd] for hd in range(N_HEAD)], dsn, causal, anti)
            dgn_acc = dgn_acc + dgn_c
            dlb_acc = dlb_acc + dlb_c
            dh1_ref[rows, :D_MODEL] = d_q.astype(BF16)
            dh1_ref[rows, D_MODEL:] = d_fp.astype(BF16)
            dh2_ref[rows, :D_MODEL] = d_i.astype(BF16)
            dh2_ref[rows, D_MODEL:] = d_og.astype(BF16)
        dgn_ref[...] += dgn_acc
        dlb_ref[...] += dlb_acc
        for hd in range(N_HEAD):
            dstate[hd] = dsn[hd]

    def col(k):
        return pl.BlockSpec((HG_ROWS, D_MODEL), lambda ci: (nb - 1 - ci, k))

    acc8 = pl.BlockSpec((8, D_MODEL), lambda ci: (0, 0))
    pair = pl.BlockSpec((HG_ROWS, 2 * D_MODEL), lambda ci: (nb - 1 - ci, 0))
    return _grid1_call(body, comm, nb, name="hgrn_bwd",
                       out_shape=(jax.ShapeDtypeStruct((t, 2 * D_MODEL), BF16),
                                  jax.ShapeDtypeStruct((t, 2 * D_MODEL), BF16),
                                  jax.ShapeDtypeStruct((8, D_MODEL), F32),
                                  jax.ShapeDtypeStruct((8, D_MODEL), F32)),
                       in_specs=[col(2), col(3), col(4), col(5),
                                 pl.BlockSpec((HG_ROWS, D_MODEL), lambda ci: (nb - 1 - ci, 0)),
                                 pl.BlockSpec((HG_CHUNKS, N_HEAD, HEAD_DIM, HEAD_DIM),
                                              lambda ci: (nb - 1 - ci, 0, 0, 0)),
                                 pl.BlockSpec((2, D_MODEL), lambda ci: (0, 0)),
                                 pl.BlockSpec((1, D_MODEL), lambda ci: (0, 0))],
                       out_specs=(pair, pair, acc8, acc8),
                       scratch=[pltpu.VMEM((N_HEAD, HEAD_DIM, HEAD_DIM), F32)],
                       args=(h, h, h, h, dyb, st_all, logits, gn))


def _mix_fwd(ya, yb, h, x, wb0, wb1, wo, g1, b1, tm, comm=None):
    t = x.shape[0]

    def body(ya_ref, yb_ref, ga_ref, gb_ref, x_ref, wb0_ref, wb1_ref, wo_ref, g1_ref, b1_ref,
             r1_ref, a_ref, b_ref, m_ref, x1_ref):
        a = _dot(ya_ref[...], wb0_ref[...])
        b = _dot(yb_ref[...], wb1_ref[...])
        m = _sig(ga_ref[...].astype(F32)) * a + _sig(gb_ref[...].astype(F32)) * b
        r1 = ALPHA * x_ref[...] + _dot(m, wo_ref[...])
        xh, _ = _ln_stats(r1)
        r1_ref[...] = r1
        a_ref[...] = a
        b_ref[...] = b
        m_ref[...] = m.astype(BF16)
        x1_ref[...] = (xh * g1_ref[...] + b1_ref[...]).astype(BF16)

    tile = pl.BlockSpec((tm, D_MODEL), lambda i: (i, 0))
    wsp = pl.BlockSpec((D_MODEL, D_MODEL), lambda i: (0, 0))
    vec = pl.BlockSpec((1, D_MODEL), lambda i: (0, 0))
    f32o = jax.ShapeDtypeStruct((t, D_MODEL), F32)
    bfo = jax.ShapeDtypeStruct((t, D_MODEL), BF16)
    return _grid1_call(body, comm, t // tm, name="mix_fwd", out_shape=(f32o, f32o, f32o, bfo, bfo),
                       in_specs=[tile, tile,
                                 pl.BlockSpec((tm, D_MODEL), lambda i: (i, 6)),
                                 pl.BlockSpec((tm, D_MODEL), lambda i: (i, 7)),
                                 tile, wsp, wsp, wsp, vec, vec],
                       out_specs=(tile, tile, tile, tile, tile),
                       scratch=[], args=(ya, yb, h, h, x, wb0, wb1, wo, g1, b1))


def _mix_bwd(dr1, h, a, b, wo, wb0, wb1, tm):
    t = dr1.shape[0]

    def body(dr1_ref, ga_ref, gb_ref, a_ref, b_ref, wo_ref, wb0_ref, wb1_ref,
             da_ref, db_ref, dh3_ref, dya_ref, dyb_ref):
        d_m = _dot(dr1_ref[...], wo_ref[...], NT)
        sa = _sig(ga_ref[...].astype(F32))
        sb = _sig(gb_ref[...].astype(F32))
        d_a = (d_m * sa).astype(BF16)
        d_b = (d_m * sb).astype(BF16)
        da_ref[...] = d_a
        db_ref[...] = d_b
        dh3_ref[:, :D_MODEL] = (d_m * a_ref[...] * sa * (1.0 - sa)).astype(BF16)
        dh3_ref[:, D_MODEL:] = (d_m * b_ref[...] * sb * (1.0 - sb)).astype(BF16)
        dya_ref[...] = _dot(d_a, wb0_ref[...], NT)
        dyb_ref[...] = _dot(d_b, wb1_ref[...], NT)

    tile = pl.BlockSpec((tm, D_MODEL), lambda i: (i, 0))
    wsp = pl.BlockSpec((D_MODEL, D_MODEL), lambda i: (0, 0))
    f32o = jax.ShapeDtypeStruct((t, D_MODEL), F32)
    bfo = jax.ShapeDtypeStruct((t, D_MODEL), BF16)
    return _pc(body, name="mix_bwd",
               out_shape=(bfo, bfo, jax.ShapeDtypeStruct((t, 2 * D_MODEL), BF16), f32o, f32o),
               grid=(t // tm,),
               in_specs=[tile,
                         pl.BlockSpec((tm, D_MODEL), lambda i: (i, 6)),
                         pl.BlockSpec((tm, D_MODEL), lambda i: (i, 7)),
                         tile, tile, wsp, wsp, wsp],
               out_specs=(tile, tile, pl.BlockSpec((tm, 2 * D_MODEL), lambda i: (i, 0)),
                          tile, tile),
               sem=("parallel",))(dr1, h, h, a, b, wo, wb0, wb1)


FF_TILE = 1408
FF_NJ = D_FF // FF_TILE


def _shift_down(v, k):
    return pltpu.roll(v, k, 0)


def _shift_up(v, k):
    return pltpu.roll(v, v.shape[0] - k, 0)


HALO = 16
FF_PIECES = ((0, 512), (512, 1024), (1024, FF_TILE))


def _ffn_up_act(x1b, wup_st, convw, convb, tm, comm):
    t = x1b.shape[0]
    ni = t // tm
    nth = tm // HALO

    def body(x_ref, xp_ref, wg_ref, wv_ref, cw_ref, cb_ref, h2_ref, act_ref):
        wg = wg_ref[...]
        gate = _dot(x_ref[...], wg).astype(BF16)
        val = _dot(x_ref[...], wv_ref[...]).astype(BF16)
        prev = (_dot(xp_ref[...], wg) * (pl.program_id(0) > 0).astype(F32)).astype(BF16)
        h2_ref[0] = gate
        h2_ref[1] = val
        ext = jnp.concatenate([prev.astype(F32), gate.astype(F32)], axis=0)
        gc = (cw_ref[0:1, :] * _shift_down(ext, 2) + cw_ref[1:2, :] * _shift_down(ext, 1)
              + cw_ref[2:3, :] * ext + cb_ref[...])[HALO:, :]
        act_ref[...] = (_gelu(gc) * val.astype(F32)).astype(BF16)

    res = _hosted_call(
        body, comm,
        lambda: (pl.program_id(0) == 0) & (pl.program_id(1) == 0),
        lambda: (pl.program_id(0) == ni - 1) & (pl.program_id(1) == FF_NJ - 1),
        name="ffn_up",
        out_shape=(jax.ShapeDtypeStruct((2, t, D_FF), BF16), jax.ShapeDtypeStruct((t, D_FF), BF16)),
        grid=(ni, FF_NJ),
        in_specs=[pl.BlockSpec((tm, D_MODEL), lambda i, j: (i, 0)),
                  pl.BlockSpec((HALO, D_MODEL), lambda i, j: (jnp.maximum(i * nth - 1, 0), 0)),
                  pl.BlockSpec((None, D_MODEL, FF_TILE), lambda i, j: (j, 0, 0)),
                  pl.BlockSpec((None, D_MODEL, FF_TILE), lambda i, j: (j + FF_NJ, 0, 0)),
                  pl.BlockSpec((3, FF_TILE), lambda i, j: (0, j)),
                  pl.BlockSpec((1, FF_TILE), lambda i, j: (0, j))],
        out_specs=(pl.BlockSpec((2, tm, FF_TILE), lambda i, j: (0, i, j)),
                   pl.BlockSpec((tm, FF_TILE), lambda i, j: (i, j))),
        scratch=[], sem=("arbitrary", "arbitrary"),
        args=(x1b, x1b, wup_st, wup_st, convw, convb))
    return res[0], res[1], res[2:]


def _out_fwd_bwd(act, x1b, r1, p2, tgt, wd, wpg, wpp, g1, b1, g2, b2, tm):
    t = r1.shape[0]

    def body(act_ref, x1b_ref, r1_ref, p_ref, tgt_ref, wd_ref, wpg_ref, wpp_ref,
             g1_ref, b1_ref, g2_ref, b2_ref,
             dr2_ref, dpg_ref, dpp_ref, loss_ref, dg2_ref, db2_ref):
        i = pl.program_id(0)

        @pl.when(i == 0)
        def _():
            loss_ref[...] = jnp.zeros_like(loss_ref)
            dg2_ref[...] = jnp.zeros_like(dg2_ref)
            db2_ref[...] = jnp.zeros_like(db2_ref)

        ffn = _dot(act_ref[...], wd_ref[...])
        pg = _dot(x1b_ref[...], wpg_ref[...])
        pp = _dot(p_ref[...], wpp_ref[...])
        s = _sig(pg)
        xh1, _ = _ln_stats(r1_ref[...])
        x1 = xh1 * g1_ref[...] + b1_ref[...]
        r2 = ALPHA * x1 + ffn + s * pp
        xh2, rstd2 = _ln_stats(r2)
        g2v = g2_ref[...]
        diff = xh2 * g2v + b2_ref[...] - tgt_ref[...]
        part = jnp.sum(jnp.sum(diff * diff, axis=1, keepdims=True), axis=0, keepdims=True)
        loss_ref[...] += jnp.broadcast_to(part * (0.5 / D_MODEL), loss_ref.shape)
        dy = diff * (1.0 / D_MODEL)
        dg2_ref[...] += _colsum8(dy * xh2)
        db2_ref[...] += _colsum8(dy)
        dr2 = _ln_bwd(dy * g2v, xh2, rstd2)
        dr2_ref[...] = dr2
        dpg_ref[...] = (dr2 * pp * s * (1.0 - s)).astype(BF16)
        dpp_ref[...] = (dr2 * s).astype(BF16)

    tile = pl.BlockSpec((tm, D_MODEL), lambda i: (i, 0))
    vec = pl.BlockSpec((1, D_MODEL), lambda i: (0, 0))
    acc8 = pl.BlockSpec((8, D_MODEL), lambda i: (0, 0))
    acc_shape = jax.ShapeDtypeStruct((8, D_MODEL), F32)
    return _pc(body, name="out_fwd_bwd",
               out_shape=(jax.ShapeDtypeStruct((t, D_MODEL), F32),
                          jax.ShapeDtypeStruct((t, D_MODEL), BF16),
                          jax.ShapeDtypeStruct((t, D_MODEL), BF16),
                          acc_shape, acc_shape, acc_shape),
               grid=(t // tm,),
               in_specs=[pl.BlockSpec((tm, D_FF), lambda i: (i, 0)), tile, tile,
                         pl.BlockSpec((tm, PLE_DIM), lambda i: (i, 0)), tile,
                         pl.BlockSpec((D_FF, D_MODEL), lambda i: (0, 0)),
                         pl.BlockSpec((D_MODEL, D_MODEL), lambda i: (0, 0)),
                         pl.BlockSpec((PLE_DIM, D_MODEL), lambda i: (0, 0)),
                         vec, vec, vec, vec],
               out_specs=(tile, tile, tile, acc8, acc8, acc8),
               sem=("arbitrary",))(act, x1b, r1, p2, tgt, wd, wpg, wpp, g1, b1, g2, b2)


def _ffn_bwd(h2, dr2, wd, wup_st, dpg, wpg, r1, g1, convw, convb, tm):
    t = r1.shape[0]
    ni = t // tm
    nth = tm // HALO
    last_halo = t // HALO - 1
    main_rows = slice(HALO, HALO + tm)

    def body(g_ref, gp_ref, gn_ref, v_ref, vn_ref, dr2_ref, dr2n_ref, wd_ref, wug_ref, wuv_ref,
             cw_ref, cb_ref, dpg_ref, wpg_ref, r1_ref, g1_ref,
             dh2_ref, dr1_ref, dcw_ref, dcb_ref, dg1_ref, db1_ref, acc):
        i = pl.program_id(0)
        j = pl.program_id(1)

        @pl.when((i == 0) & (j == 0))
        def _():
            dcw_ref[...] = jnp.zeros_like(dcw_ref)
            dcb_ref[...] = jnp.zeros_like(dcb_ref)
            dg1_ref[...] = jnp.zeros_like(dg1_ref)
            db1_ref[...] = jnp.zeros_like(db1_ref)

        dr2v = dr2_ref[...].astype(BF16)
        dr2n = dr2n_ref[...].astype(BF16)
        first = (i > 0).astype(F32)
        more = (i < ni - 1).astype(F32)
        prod = None
        dcw_parts, dcb_parts = [], []
        for c0, c1 in FF_PIECES:
            pc = slice(c0, c1)
            zeros = jnp.zeros((HALO, c1 - c0), F32)
            da = _dot(dr2v, wd_ref[pc, :], NT)
            dnext = _dot(dr2n, wd_ref[pc, :], NT) * more
            ext = jnp.concatenate([gp_ref[:, pc].astype(F32) * first, g_ref[:, pc].astype(F32),
                                   gn_ref[:, pc].astype(F32)], axis=0)
            vext = jnp.concatenate([zeros, v_ref[:, pc].astype(F32), vn_ref[:, pc].astype(F32)], axis=0)
            dext = jnp.concatenate([zeros, da, dnext], axis=0)
            g2 = _shift_down(ext, 2)
            g1s = _shift_down(ext, 1)
            gc = cw_ref[0:1, pc] * g2 + cw_ref[1:2, pc] * g1s + cw_ref[2:3, pc] * ext + cb_ref[:, pc]
            gl, dgl = _gelu_and_grad(gc)
            d_gc = dext * vext * dgl
            d_gate = (cw_ref[2:3, pc] * d_gc + cw_ref[1:2, pc] * _shift_up(d_gc, 1)
                      + cw_ref[0:1, pc] * _shift_up(d_gc, 2))[main_rows, :].astype(BF16)
            d_val = (da * gl[main_rows, :]).astype(BF16)
            dh2_ref[0, :, pc] = d_gate
            dh2_ref[1, :, pc] = d_val
            dm = d_gc[main_rows, :]
            s0 = jnp.sum(dm * g2[main_rows, :], axis=0, keepdims=True)
            s1 = jnp.sum(dm * g1s[main_rows, :], axis=0, keepdims=True)
            s2 = jnp.sum(dm * ext[main_rows, :], axis=0, keepdims=True)
            rowid = lax.broadcasted_iota(jnp.int32, (8, c1 - c0), 0)
            dcw_parts.append(jnp.where(rowid == 0, s0, jnp.where(rowid == 1, s1,
                                                                 jnp.where(rowid == 2, s2, 0.0))))
            dcb_parts.append(_colsum8(dm))
            part = _dot(d_gate, wug_ref[:, pc], NT) + _dot(d_val, wuv_ref[:, pc], NT)
            prod = part if prod is None else prod + part
        dcw_part = jnp.concatenate(dcw_parts, axis=1)
        dcb_part = jnp.concatenate(dcb_parts, axis=1)
        for jj in range(FF_NJ):
            @pl.when(j == jj)
            def _(jj=jj):
                cols = slice(jj * FF_TILE, (jj + 1) * FF_TILE)
                dcw_ref[:, cols] += dcw_part
                dcb_ref[:, cols] += dcb_part

        @pl.when(j == 0)
        def _():
            acc[...] = prod

        @pl.when(j > 0)
        def _():
            acc[...] += prod

        @pl.when(j == FF_NJ - 1)
        def _():
            d_x1 = acc[...] + _dot(dpg_ref[...], wpg_ref[...], NT) + ALPHA * dr2_ref[...]
            xh, rstd = _ln_stats(r1_ref[...])
            dg1_ref[...] += _colsum8(d_x1 * xh)
            db1_ref[...] += _colsum8(d_x1)
            dr1_ref[...] = _ln_bwd(d_x1 * g1_ref[...], xh, rstd)

    def h2_main(part):
        return pl.BlockSpec((None, tm, FF_TILE), lambda i, j: (part, i, j))

    def h2_prev(part):
        return pl.BlockSpec((None, HALO, FF_TILE), lambda i, j: (part, jnp.maximum(i * nth - 1, 0), j))

    def h2_next(part):
        return pl.BlockSpec((None, HALO, FF_TILE),
                            lambda i, j: (part, jnp.minimum((i + 1) * nth, last_halo), j))

    tile = pl.BlockSpec((tm, D_MODEL), lambda i, j: (i, 0))
    acc8 = pl.BlockSpec((8, D_MODEL), lambda i, j: (0, 0))
    accff = pl.BlockSpec((8, D_FF), lambda i, j: (0, 0))
    acc_shape = jax.ShapeDtypeStruct((8, D_MODEL), F32)
    accff_shape = jax.ShapeDtypeStruct((8, D_FF), F32)
    return _pc(body, name="ffn_bwd",
               out_shape=(jax.ShapeDtypeStruct((2, t, D_FF), BF16),
                          jax.ShapeDtypeStruct((t, D_MODEL), F32),
                          accff_shape, accff_shape, acc_shape, acc_shape),
               grid=(ni, FF_NJ),
               in_specs=[h2_main(0), h2_prev(0), h2_next(0), h2_main(1), h2_next(1),
                         tile,
                         pl.BlockSpec((HALO, D_MODEL), lambda i, j: (jnp.minimum((i + 1) * nth, last_halo), 0)),
                         pl.BlockSpec((FF_TILE, D_MODEL), lambda i, j: (j, 0)),
                         pl.BlockSpec((None, D_MODEL, FF_TILE), lambda i, j: (j, 0, 0)),
                         pl.BlockSpec((None, D_MODEL, FF_TILE), lambda i, j: (j + FF_NJ, 0, 0)),
                         pl.BlockSpec((3, FF_TILE), lambda i, j: (0, j)),
                         pl.BlockSpec((1, FF_TILE), lambda i, j: (0, j)),
                         tile, pl.BlockSpec((D_MODEL, D_MODEL), lambda i, j: (0, 0)),
                         tile, pl.BlockSpec((1, D_MODEL), lambda i, j: (0, 0))],
               out_specs=(pl.BlockSpec((2, tm, FF_TILE), lambda i, j: (0, i, j)),
                          tile, accff, accff, acc8, acc8),
               scratch=[pltpu.VMEM((tm, D_MODEL), F32)],
               sem=("arbitrary", "arbitrary"))(h2, h2, h2, h2, h2, dr2, dr2, wd, wup_st, wup_st,
                                               convw, convb, dpg, wpg, r1, g1)


ANY = pl.BlockSpec(memory_space=pl.ANY)


def _chip_peers():
    x, y, c = lax.axis_index("x"), lax.axis_index("y"), lax.axis_index("c")
    return x, y, c, [(1 - x, y), (x, 1 - y), (1 - x, 1 - y)]


def _gather_comm(halved, whole=()):
    n, nw = len(halved), len(whole)

    def copies(ins, outs, sems):
        ici_send, ici_recv, d2d_send, d2d_recv, own_send, own_recv = sems
        x, y, c, peers = _chip_peers()
        me = 2 * x + y
        sibling = (x, y, 1 - c)
        own, ici, ici_wait, fwd, fwd_wait = [], [], [], [], []
        for ti in range(n + nw):
            src, dst = ins[ti], outs[ti]
            own.append(pltpu.make_async_remote_copy(
                src_ref=src, dst_ref=dst.at[me], send_sem=own_send.at[ti], recv_sem=own_recv.at[ti],
                device_id=sibling, device_id_type=MESH))
            for k, (px, py) in enumerate(peers):
                pk = 2 * px + py
                sem = dict(send_sem=ici_send.at[ti * 3 + k], recv_sem=ici_recv.at[ti * 3 + k],
                           device_id=(px, py, c), device_id_type=MESH)
                if ti < n:
                    ici.append(pltpu.make_async_remote_copy(src_ref=src.at[c], dst_ref=dst.at[me, c], **sem))
                    ici_wait.append(pltpu.make_async_remote_copy(src_ref=src.at[c], dst_ref=dst.at[pk, c], **sem))
                    dsem = dict(send_sem=d2d_send.at[ti * 3 + k], recv_sem=d2d_recv.at[ti * 3 + k],
                                device_id=sibling, device_id_type=MESH)
                    fwd.append(pltpu.make_async_remote_copy(src_ref=dst.at[pk, c], dst_ref=dst.at[pk, c], **dsem))
                    fwd_wait.append(pltpu.make_async_remote_copy(
                        src_ref=dst.at[pk, 1 - c], dst_ref=dst.at[pk, 1 - c], **dsem))
                else:
                    ici.append(pltpu.make_async_remote_copy(src_ref=src, dst_ref=dst.at[me], **sem))
                    ici_wait.append(pltpu.make_async_remote_copy(src_ref=src, dst_ref=dst.at[pk], **sem))
        return own, ici, ici_wait, fwd, fwd_wait

    def start(ins, outs, sems):
        own, ici, _, _, _ = copies(ins, outs, sems)
        for cp in own + ici:
            cp.start()

    def finish(ins, outs, sems):
        own, ici, ici_wait, fwd, fwd_wait = copies(ins, outs, sems)
        for i, cp in enumerate(ici_wait):
            cp.wait_recv()
            if i < len(fwd):
                fwd[i].start()
        for cp in fwd_wait + own:
            cp.wait_recv()
        for cp in own + ici + fwd:
            cp.wait_send()

    srcs = list(halved) + list(whole)
    return _Comm(srcs, [jax.ShapeDtypeStruct((N_CHIP,) + s.shape, s.dtype) for s in srcs],
                 [pltpu.SemaphoreType.DMA((3 * (n + nw),)), pltpu.SemaphoreType.DMA((3 * (n + nw),)),
                  pltpu.SemaphoreType.DMA((max(3 * n, 1),)), pltpu.SemaphoreType.DMA((max(3 * n, 1),)),
                  pltpu.SemaphoreType.DMA((n + nw,)), pltpu.SemaphoreType.DMA((n + nw,))],
                 start, finish)


def _sibling_exchange_comm(grads):
    n = len(grads)

    def copies(ins, outs, sems):
        send_sems, recv_sems = sems
        x, y, c = lax.axis_index("x"), lax.axis_index("y"), lax.axis_index("c")
        res = []
        for ti in range(n):
            half = ins[ti].shape[1] // 2
            res.append(pltpu.make_async_remote_copy(
                src_ref=ins[ti].at[:, pl.ds(pl.multiple_of((1 - c) * half, 16), half), :],
                dst_ref=outs[ti],
                send_sem=send_sems.at[ti], recv_sem=recv_sems.at[ti],
                device_id=(x, y, 1 - c), device_id_type=MESH))
        return res

    def start(ins, outs, sems):
        for cp in copies(ins, outs, sems):
            cp.start()

    def finish(ins, outs, sems):
        for cp in copies(ins, outs, sems):
            cp.wait()

    return _Comm(grads, [jax.ShapeDtypeStruct((N_CHIP, g.shape[1] // 2, g.shape[2]), g.dtype) for g in grads],
                 [pltpu.SemaphoreType.DMA((n,)), pltpu.SemaphoreType.DMA((n,))], start, finish)


def _in_proj_gathering(x2b, own, chip, tm, comm):
    t = x2b.shape[0]
    ni = t // tm
    half, cols = own.shape[1], own.shape[2]
    nci, nco = len(comm.ins), len(comm.out_shapes)

    def body(chip_ref, x_ref, own_ref, own_hbm, *rest):
        c_in = rest[:nci]
        h_ref, win_out = rest[nci:nci + 2]
        c_out = rest[nci + 2:nci + 2 + nco]
        w_scr, ici_send, ici_recv, d2d_send, d2d_recv, own_sems, ld_sems = rest[nci + 2 + nco:nci + 9 + nco]
        c_sem = rest[nci + 9 + nco:]
        s, i = pl.program_id(0), pl.program_id(1)
        x, y, c, peers = _chip_peers()
        me = 2 * x + y
        sibling = (x, y, 1 - c)

        def ici(k, slot):
            px, py = peers[k]
            return pltpu.make_async_remote_copy(
                src_ref=own_hbm.at[c], dst_ref=win_out.at[slot, c],
                send_sem=ici_send.at[k], recv_sem=ici_recv.at[k],
                device_id=(px, py, c), device_id_type=MESH)

        def forward(k, core):
            pk = 2 * peers[k][0] + peers[k][1]
            return pltpu.make_async_remote_copy(
                src_ref=win_out.at[pk, core], dst_ref=win_out.at[pk, core],
                send_sem=d2d_send.at[k], recv_sem=d2d_recv.at[k],
                device_id=sibling, device_id_type=MESH)

        place_own = pltpu.make_async_remote_copy(
            src_ref=own_hbm, dst_ref=win_out.at[me], send_sem=own_sems.at[0], recv_sem=own_sems.at[1],
            device_id=sibling, device_id_type=MESH)

        @pl.when((s == 0) & (i == 0))
        def _():
            for k in range(2):
                ici(k, me).start()
            place_own.start()

        @pl.when(s == 0)
        def _():
            xv = x_ref[...]
            h_ref[...] = (_dot(xv[:, :half], own_ref[0]) + _dot(xv[:, half:], own_ref[1])).astype(BF16)

        for k in range(3):
            @pl.when((s == k + 1) & (i == 0))
            def _(k=k):
                pk = 2 * peers[k][0] + peers[k][1]
                ici(k, pk).wait_recv()
                if k == 0:
                    ici(2, me).start()
                forward(k, c).start()
                forward(k, 1 - c).wait_recv()
                loads = [pltpu.make_async_copy(win_out.at[pk, hh], w_scr.at[hh], ld_sems.at[hh])
                         for hh in range(2)]
                for ld in loads:
                    ld.start()
                for ld in loads:
                    ld.wait()
                if k == 1:
                    comm.start(c_in, c_out, c_sem)

        @pl.when(s > 0)
        def _():
            xv = x_ref[...]
            h_ref[...] = (_dot(xv[:, :half], w_scr[0]) + _dot(xv[:, half:], w_scr[1])).astype(BF16)

        @pl.when((s == N_CHIP - 1) & (i == ni - 1))
        def _():
            place_own.wait()
            for k in range(3):
                ici(k, me).wait_send()
                forward(k, c).wait_send()
            comm.finish(c_in, c_out, c_sem)

    def shard_col(s, me):
        return jnp.where(s == 0, me, me ^ jnp.where(s == 1, 2, jnp.where(s == 2, 1, 3)))

    res = _pc(body, name="in_proj",
              out_shape=(jax.ShapeDtypeStruct((t, N_CHIP * cols), BF16),
                         jax.ShapeDtypeStruct((N_CHIP,) + own.shape, own.dtype)) + tuple(comm.out_shapes),
              grid=(N_CHIP, ni), nsp=1,
              in_specs=[pl.BlockSpec((tm, 2 * half), lambda s, i, chip_ref: (i, 0)),
                        pl.BlockSpec(own.shape, lambda s, i, chip_ref: (0, 0, 0)),
                        ANY] + [ANY] * nci,
              out_specs=(pl.BlockSpec((tm, cols), lambda s, i, chip_ref: (i, shard_col(s, chip_ref[0]))),
                         ANY) + tuple([ANY] * nco),
              scratch=[pltpu.VMEM(own.shape, own.dtype),
                       pltpu.SemaphoreType.DMA((3,)), pltpu.SemaphoreType.DMA((3,)),
                       pltpu.SemaphoreType.DMA((3,)), pltpu.SemaphoreType.DMA((3,)),
                       pltpu.SemaphoreType.DMA((2,)), pltpu.SemaphoreType.DMA((2,))] + comm.sems,
              sem=("arbitrary", "arbitrary"))(chip, x2b, own, own, *comm.ins)
    return res[0], res[1], res[2:]


def _rs_add_halves(name, grad, recv, core):
    _, r, cdim = grad.shape
    half = r // 2
    tr = _row_tile(half, cdim, mult=16)
    nr = half // tr

    def body(c_ref, g_ref, r_ref, o_ref):
        o_ref[...] = (g_ref[...].astype(F32) + r_ref[...].astype(F32)).astype(BF16)

    return _pc(body, name=name, out_shape=jax.ShapeDtypeStruct((N_CHIP, half, cdim), BF16),
               grid=(N_CHIP, nr), nsp=1,
               in_specs=[pl.BlockSpec((None, tr, cdim), lambda j, i, c_ref: (j, c_ref[0] * nr + i, 0)),
                         pl.BlockSpec((None, tr, cdim), lambda j, i, c_ref: (j, i, 0))],
               out_specs=pl.BlockSpec((None, tr, cdim), lambda j, i, c_ref: (j, i, 0)),
               sem=("parallel", "parallel"))(core, grad, recv)


def _chip_exchange_comm(parts):
    n = len(parts)

    def copies(ins, outs, sems):
        send_sems, recv_sems = sems
        x, y, c, peers = _chip_peers()
        return [pltpu.make_async_remote_copy(
            src_ref=ins[ti].at[2 * px + py], dst_ref=outs[ti].at[k],
            send_sem=send_sems.at[ti * 3 + k], recv_sem=recv_sems.at[ti * 3 + k],
            device_id=(px, py, c), device_id_type=MESH)
            for ti in range(n) for k, (px, py) in enumerate(peers)]

    def start(ins, outs, sems):
        for cp in copies(ins, outs, sems):
            cp.start()

    def finish(ins, outs, sems):
        for cp in copies(ins, outs, sems):
            cp.wait()

    return _Comm(parts, [jax.ShapeDtypeStruct((3,) + p.shape[1:], p.dtype) for p in parts],
                 [pltpu.SemaphoreType.DMA((3 * n,)), pltpu.SemaphoreType.DMA((3 * n,))], start, finish)


def _rs_sum_chips(name, part, recv, chip):
    _, half, cdim = recv.shape
    tr = _row_tile(half, cdim, mult=16)

    def body(chip_ref, p_ref, r_ref, o_ref):
        o_ref[...] = ((p_ref[...].astype(F32) + r_ref[0].astype(F32)) + r_ref[1].astype(F32)
                      ) + r_ref[2].astype(F32)

    return _pc(body, name=name, out_shape=jax.ShapeDtypeStruct((half, cdim), F32),
               grid=(half // tr,), nsp=1,
               in_specs=[pl.BlockSpec((None, tr, cdim), lambda i, chip_ref: (chip_ref[0], i, 0)),
                         pl.BlockSpec((3, tr, cdim), lambda i, chip_ref: (0, i, 0))],
               out_specs=pl.BlockSpec((tr, cdim), lambda i, chip_ref: (i, 0)),
               sem=("parallel",))(chip, part, recv)


def _rs_send_halves(halves):
    n = len(halves)

    def body(*refs):
        ins, outs = refs[:n], refs[n:2 * n]
        send_sems, recv_sems = refs[2 * n:]
        x, y, c = lax.axis_index("x"), lax.axis_index("y"), lax.axis_index("c")
        sends = []
        for ti in range(n):
            cp = pltpu.make_async_remote_copy(
                src_ref=ins[ti], dst_ref=outs[ti],
                send_sem=send_sems.at[ti], recv_sem=recv_sems.at[ti],
                device_id=(x, y, 1 - c), device_id_type=MESH)
            cp.start()
            sends.append(cp)
        for cp in sends:
            cp.wait()

    return _pc(body, name="rs_send_halves",
               out_shape=tuple(jax.ShapeDtypeStruct(hv.shape, hv.dtype) for hv in halves),
               in_specs=[ANY] * n, out_specs=tuple([ANY] * n),
               scratch=[pltpu.SemaphoreType.DMA((n,)), pltpu.SemaphoreType.DMA((n,))])(*halves)


def _adamw_rows(name, mine, theirs, w, m, v, core):
    half, cdim = mine.shape
    tr = _row_tile(half, cdim, budget=1 << 19)
    nrh = half // tr

    def body(c_ref, mine_ref, theirs_ref, w_ref, m_ref, v_ref, g_ref, d_ref, m2_ref, v2_ref):
        is_mine = (pl.program_id(0) // nrh) == c_ref[0]
        g = jnp.where(is_mine, mine_ref[...], theirs_ref[...])
        d, m2, v2 = _adamw(w_ref[...], g, m_ref[...], v_ref[...])
        g_ref[...] = g
        d_ref[...] = d
        m2_ref[...] = m2
        v2_ref[...] = v2

    htile = pl.BlockSpec((tr, cdim), lambda i, c_ref: (i % nrh, 0))
    tile = pl.BlockSpec((tr, cdim), lambda i, c_ref: (i, 0))
    shp = jax.ShapeDtypeStruct((2 * half, cdim), F32)
    return _pc(body, name=name, out_shape=(shp, shp, shp, shp), grid=(2 * nrh,), nsp=1,
               in_specs=[htile, htile, tile, tile, tile], out_specs=(tile, tile, tile, tile),
               sem=("parallel",))(core, mine, theirs, w, m, v)


def _adamw_whole(name, g, w, m, v):
    def body(g_ref, w_ref, m_ref, v_ref, d_ref, m2_ref, v2_ref):
        d, m2, v2 = _adamw(w_ref[...], g_ref[...], m_ref[...], v_ref[...])
        d_ref[...] = d
        m2_ref[...] = m2
        v2_ref[...] = v2

    shp = jax.ShapeDtypeStruct(g.shape, F32)
    return _pc(body, name=name, out_shape=(shp, shp, shp))(g, w, m, v)


SMALL_LAYOUT = (
    ("sgu_w_s", 1024, 1, 0),
    ("sgu_b_s", 8, 1, 1024),
    ("sgu_norm_g", 1, 0, 0),
    ("sgu_norm_b", 1, 0, 1),
    ("hgrn_norm_g", 1, 0, 3),
    ("ln1_g", 1, 0, 4),
    ("ln1_b", 1, 0, 5),
    ("ffn_conv_b", 1, 2, 3),
    ("ln2_g", 1, 0, 6),
    ("ln2_b", 1, 0, 7),
)
LB_ROW = 2
LOSS_ROW = 8
PACK_SHAPES = ((16, D_MODEL), (N_GROUP * 128 + 8, 128), (8, D_FF))


def _small_allreduce_adamw(rows1024, dws, dbs, dcw, dcb, logits, m_logits, v_logits,
                           small_w, small_m, small_v):
    ns = len(SMALL_LAYOUT)
    nr = len(rows1024)
    nb = len(PACK_SHAPES)

    def body(*refs):
        row_refs = refs[:nr]
        dws_ref, dbs_ref, dcw_ref, dcb_ref, lg_ref, mlg_ref, vlg_ref = refs[nr:nr + 7]
        pos = nr + 7
        w_refs = refs[pos:pos + ns]
        m_refs = refs[pos + ns:pos + 2 * ns]
        v_refs = refs[pos + 2 * ns:pos + 3 * ns]
        pos += 3 * ns
        loss_ref, dcw_out = refs[pos:pos + 2]
        lg_outs = refs[pos + 2:pos + 6]
        pos += 6
        outs = refs[pos:pos + 4 * ns]
        pos += 4 * ns
        pack = refs[pos:pos + nb]
        sib = refs[pos + nb:pos + 2 * nb]
        gath = refs[pos + 2 * nb:pos + 3 * nb]
        d2d_send, d2d_recv, ici_send, ici_recv = refs[pos + 3 * nb:]

        x, y, c, peers = _chip_peers()
        me = 2 * x + y
        sibling = (x, y, 1 - c)

        pack[0][...] = jnp.zeros(PACK_SHAPES[0], F32)
        for k in range(nr):
            pack[0][k:k + 1, :] = row_refs[k][0:1, :]
        pack[1][0:N_GROUP * 128, :] = dws_ref[...]
        pack[1][N_GROUP * 128:, :] = dbs_ref[...]
        pack[2][...] = jnp.zeros(PACK_SHAPES[2], F32)
        pack[2][0:3, :] = dcw_ref[0:3, :]
        pack[2][3:4, :] = dcb_ref[0:1, :]

        d2d = [pltpu.make_async_remote_copy(
            src_ref=pack[b], dst_ref=sib[b], send_sem=d2d_send.at[b], recv_sem=d2d_recv.at[b],
            device_id=sibling, device_id_type=MESH) for b in range(nb)]
        for cp in d2d:
            cp.start()
        for cp in d2d:
            cp.wait()
        for b in range(nb):
            gath[b][me] = pack[b][...] + sib[b][...]

        ici, ici_wait = [], []
        for b in range(nb):
            for k, (px, py) in enumerate(peers):
                sem = dict(send_sem=ici_send.at[b * 3 + k], recv_sem=ici_recv.at[b * 3 + k],
                           device_id=(px, py, c), device_id_type=MESH)
                ici.append(pltpu.make_async_remote_copy(src_ref=gath[b].at[me], dst_ref=gath[b].at[me], **sem))
                ici_wait.append(pltpu.make_async_remote_copy(
                    src_ref=gath[b].at[me], dst_ref=gath[b].at[2 * px + py], **sem))
        for cp in ici:
            cp.start()
        for cp in ici_wait:
            cp.wait_recv()
        for cp in ici:
            cp.wait_send()

        tot = pack
        for b in range(nb):
            tot[b][...] = ((gath[b][0] + gath[b][1]) + gath[b][2]) + gath[b][3]

        loss_ref[...] = tot[0][LOSS_ROW:LOSS_ROW + 1, :]
        dcw_out[...] = tot[2][...]
        lb = _sig(lg_ref[0:1, :] - lg_ref[1:2, :])
        d0 = tot[0][LB_ROW:LB_ROW + 1, :] * lb * (1.0 - lb)
        rowid = lax.broadcasted_iota(jnp.int32, (2, D_MODEL), 0)
        g_lg = jnp.where(rowid == 0, d0, -d0)
        dl, ml, vl = _adamw(lg_ref[...], g_lg, mlg_ref[...], vlg_ref[...])
        lg_outs[0][...] = g_lg
        lg_outs[1][...] = dl
        lg_outs[2][...] = ml
        lg_outs[3][...] = vl
        for si, (_, rows, b, r0) in enumerate(SMALL_LAYOUT):
            g = tot[b][r0:r0 + rows, :]
            dl, ml, vl = _adamw(w_refs[si][...], g, m_refs[si][...], v_refs[si][...])
            outs[4 * si][...] = g
            outs[4 * si + 1][...] = dl
            outs[4 * si + 2][...] = ml
            outs[4 * si + 3][...] = vl

    shapes = [jax.ShapeDtypeStruct((1, D_MODEL), F32), jax.ShapeDtypeStruct((8, D_FF), F32)]
    shapes += [jax.ShapeDtypeStruct((2, D_MODEL), F32)] * 4
    for w in small_w:
        shapes += [jax.ShapeDtypeStruct(w.shape, F32)] * 4
    scratch = [pltpu.VMEM(shp, F32) for shp in PACK_SHAPES]
    scratch += [pltpu.VMEM(shp, F32) for shp in PACK_SHAPES]
    scratch += [pltpu.VMEM((N_CHIP,) + shp, F32) for shp in PACK_SHAPES]
    scratch += [pltpu.SemaphoreType.DMA((nb,)), pltpu.SemaphoreType.DMA((nb,)),
                pltpu.SemaphoreType.DMA((3 * nb,)), pltpu.SemaphoreType.DMA((3 * nb,))]
    vm = pl.BlockSpec(memory_space=pltpu.VMEM)
    n_in = nr + 7 + 3 * ns
    res = _pc(body, name="small_allreduce_adamw", out_shape=tuple(shapes),
              in_specs=[vm] * n_in, out_specs=tuple([vm] * len(shapes)),
              scratch=scratch)(*rows1024, dws, dbs, dcw, dcb, logits, m_logits, v_logits,
                               *small_w, *small_m, *small_v)
    return res[0], res[1], res[2:6], res[6:]


def kernel(x, p, w_in, sgu_w_s, sgu_b_s, sgu_norm_g, sgu_norm_b, hgrn_lb_logits, hgrn_norm_g, w_branch, w_out, ln1_g, ln1_b, ffn_w_up, ffn_conv_w, ffn_conv_b, ffn_w_down, ln2_g, ln2_b, ple_w_proj, ple_w_gate, loss_target, m_w_in, m_sgu_w_s, m_sgu_b_s, m_sgu_norm_g, m_sgu_norm_b, m_hgrn_lb_logits, m_hgrn_norm_g, m_w_branch, m_w_out, m_ln1_g, m_ln1_b, m_ffn_w_up, m_ffn_conv_w, m_ffn_conv_b, m_ffn_w_down, m_ln2_g, m_ln2_b, m_ple_w_proj, m_ple_w_gate, v_w_in, v_sgu_w_s, v_sgu_b_s, v_sgu_norm_g, v_sgu_norm_b, v_hgrn_lb_logits, v_hgrn_norm_g, v_w_branch, v_w_out, v_ln1_g, v_ln1_b, v_ffn_w_up, v_ffn_conv_w, v_ffn_conv_b, v_ffn_w_down, v_ln2_g, v_ln2_b, v_ple_w_proj, v_ple_w_gate):
    t = x.shape[1]
    x2 = x.reshape(t, D_MODEL)
    x2b = x2.astype(BF16)
    p2 = p.reshape(t, PLE_DIM)
    tgt = loss_target.reshape(t, D_MODEL)
    core = lax.axis_index("c").astype(jnp.int32).reshape(1)
    chip_id = (2 * lax.axis_index("x") + lax.axis_index("y")).astype(jnp.int32).reshape(1)

    big_w = [w_in[0], w_branch[0, 0], w_branch[0, 1], w_out[0], ffn_w_up[0], ffn_w_down[0],
             ple_w_proj[0], ple_w_gate[0]]
    big_m = [m_w_in[0], m_w_branch[0, 0], m_w_branch[0, 1], m_w_out[0], m_ffn_w_up[0],
             m_ffn_w_down[0], m_ple_w_proj[0], m_ple_w_gate[0]]
    big_v = [v_w_in[0], v_w_branch[0, 0], v_w_branch[0, 1], v_w_out[0], v_ffn_w_up[0],
             v_ffn_w_down[0], v_ple_w_proj[0], v_ple_w_gate[0]]
    def halves_of(i):
        w = big_w[i]
        return w.astype(BF16).reshape(2, w.shape[0] // 2, w.shape[1])

    def stacked(g, i):
        return g.reshape(N_CHIP, big_w[i].shape[0], big_w[i].shape[1])


    cid = jnp.arange(SGU_BLOCK) // CHUNK
    maskf = (cid[:, None] >= cid[None, :]).astype(F32)
    ws_masked = sgu_w_s[0] * maskf[None]
    wm = ws_masked.astype(BF16)
    wmt = jnp.transpose(ws_masked, (0, 2, 1)).astype(BF16)
    bsb = jnp.broadcast_to(sgu_b_s[0][:, :, None], (N_GROUP, SGU_BLOCK, 128))

    up_rows = big_w[4].shape[0] // 2
    up_blocks = [big_w[4][k * up_rows:(k + 1) * up_rows].astype(BF16).reshape(2, up_rows // 2, -1)
                 for k in range(2)]
    h, win_g, (up0_g,) = _in_proj_gathering(x2b, halves_of(0), chip_id, 512, _gather_comm([up_blocks[0]]))
    win_st = stacked(win_g, 0)
    ya, _ = _sgu_fwd(h, wm, bsb, sgu_norm_g, sgu_norm_b)
    (yb, st_all), mix_g = _hgrn_fwd(
        h, hgrn_lb_logits, hgrn_norm_g,
        comm=_gather_comm([halves_of(i) for i in (1, 2, 3)] + [up_blocks[1]], [ffn_conv_w[0]]))
    wb0, wb1, wo = [stacked(g, i).reshape(D_MODEL, D_MODEL) for g, i in zip(mix_g[:3], (1, 2, 3))]
    wup_st = jnp.concatenate([g.reshape(N_CHIP, up_rows, -1) for g in (up0_g, mix_g[3])], axis=1)
    convw = jnp.transpose(mix_g[4], (1, 0, 2)).reshape(3, D_FF)
    (r1, a_br, b_br, m_bf, x1b), _ = _mix_fwd(ya, yb, h, x2, wb0, wb1, wo, ln1_g, ln1_b, 256)
    h2, act, out_g = _ffn_up_act(x1b, wup_st, convw, ffn_conv_b, 512,
                                 _gather_comm([halves_of(i) for i in (5, 6, 7)]))
    wd = stacked(out_g[0], 5).reshape(D_FF, D_MODEL)
    wpp = jnp.transpose(stacked(out_g[1], 6), (1, 0, 2)).reshape(PLE_DIM, D_MODEL)
    wpg = stacked(out_g[2], 7).reshape(D_MODEL, D_MODEL)
    dr2, dpg, dpp, loss_acc, dg2, db2 = _out_fwd_bwd(
        act, x1b, r1, p2, tgt, wd, wpg, wpp, ln1_g, ln1_b, ln2_g, ln2_b, 256)

    dh2, dr1, dcw, dcb, dg1, db1 = _ffn_bwd(h2, dr2, wd, wup_st, dpg, wpg, r1, ln1_g, convw, ffn_conv_b, 256)
    d_wd = _mm_tn("ffn_down_wgrad", act, dr2, FF_TILE, 512)
    d_wpg = _mm_tn("ple_gate_wgrad", x1b, dpg, 512, D_MODEL)
    d_wpp_st = _mm_tn("ple_proj_wgrad", p2, dpp, PLE_DIM, PLE_DIM, stacked=True)
    d_wup_st = _mm("ffn_up_wgrad", x1b, dh2, TN, (2, N_CHIP),
                   pl.BlockSpec((t, 512), lambda i, j: (0, i)),
                   pl.BlockSpec((None, t, FF_TILE), lambda i, j: (j // FF_NJ, 0, j % FF_NJ)),
                   jax.ShapeDtypeStruct((N_CHIP, D_MODEL, FF_TILE), BF16),
                   pl.BlockSpec((None, 512, FF_TILE), lambda i, j: (j, i, 0)))
    da_bf, db_bf, dh3, dya, dyb = _mix_bwd(dr1, h, a_br, b_br, wo, wb0, wb1, 256)
    d_wo = _mm_tn("out_proj_wgrad", m_bf, dr1, 512, 512)
    d_wb0 = _mm_tn("branch0_wgrad", ya, da_bf, 512, D_MODEL)
    d_wb1 = _mm_tn("branch1_wgrad", yb, db_bf, 512, D_MODEL)
    grads_1 = [d_wb0.reshape(4, 256, D_MODEL), d_wb1.reshape(4, 256, D_MODEL),
               d_wo.reshape(4, 256, D_MODEL), d_wup_st, d_wd.reshape(4, D_FF // 4, D_MODEL),
               d_wpp_st, d_wpg.reshape(4, 256, D_MODEL)]
    (dh0, dws, dbs, dgv, dbv), recv_a1 = _sgu_bwd(h, dya, wm, wmt, bsb, sgu_norm_g, sgu_norm_b, maskf,
                                                  comm=_sibling_exchange_comm(grads_1))
    parts_1 = [_rs_add_halves("rs_add_halves%d" % (i + 1), g, r, core)
               for i, (g, r) in enumerate(zip(grads_1, recv_a1))]
    (dh1, dh2h, dlb, dgn), recv_b1 = _hgrn_bwd(h, dyb, st_all, hgrn_lb_logits, hgrn_norm_g,
                                                comm=_chip_exchange_comm(parts_1))
    dh_parts = [dh0, dh1, dh2h, dh3]
    d_win = [_mm_tn("in_proj_wgrad%d" % j, x2b, dh_parts[j], 512, D_MODEL) for j in range(4)]

    grads_0 = [jnp.stack(d_win)]
    recv_a0 = _run_comm("rs_sibling_exchange0", _sibling_exchange_comm(grads_0))
    parts_0 = [_rs_add_halves("rs_add_halves0", grads_0[0], recv_a0[0], core)]
    gx, recv_b0 = _in_proj_xgrad(dh_parts, win_st, dr1, 512, comm=_chip_exchange_comm(parts_0))
    parts = parts_0 + parts_1
    recv_b = list(recv_b0) + list(recv_b1)
    halves = [_rs_sum_chips("rs_sum_chips%d" % i, pt, r, chip_id)
              for i, (pt, r) in enumerate(zip(parts, recv_b))]
    theirs = _rs_send_halves(halves)
    big_out = [_adamw_rows("adamw_big%d" % i, halves[i], theirs[i], big_w[i], big_m[i], big_v[i], core)
               for i in range(len(halves))]

    small_in = dict(sgu_w_s=(sgu_w_s, m_sgu_w_s, v_sgu_w_s), sgu_b_s=(sgu_b_s, m_sgu_b_s, v_sgu_b_s),
                    sgu_norm_g=(sgu_norm_g, m_sgu_norm_g, v_sgu_norm_g),
                    sgu_norm_b=(sgu_norm_b, m_sgu_norm_b, v_sgu_norm_b),
                    hgrn_norm_g=(hgrn_norm_g, m_hgrn_norm_g, v_hgrn_norm_g),
                    ln1_g=(ln1_g, m_ln1_g, v_ln1_g), ln1_b=(ln1_b, m_ln1_b, v_ln1_b),
                    ffn_conv_b=(ffn_conv_b, m_ffn_conv_b, v_ffn_conv_b),
                    ln2_g=(ln2_g, m_ln2_g, v_ln2_g), ln2_b=(ln2_b, m_ln2_b, v_ln2_b))

    def flat(name, arr):
        rows = dict((n, r) for n, r, _, _ in SMALL_LAYOUT)[name]
        return arr.reshape(rows, arr.size // rows)

    names = [n for n, _, _, _ in SMALL_LAYOUT]
    sw = [flat(n, small_in[n][0]) for n in names]
    sm = [flat(n, small_in[n][1]) for n in names]
    sv = [flat(n, small_in[n][2]) for n in names]
    loss_rows, dcw_tot, lg_out, small_out = _small_allreduce_adamw(
        [dgv, dbv, dlb, dgn, dg1, db1, dg2, db2, loss_acc], dws.reshape(N_GROUP * 128, 128), dbs, dcw, dcb,
        hgrn_lb_logits, m_hgrn_lb_logits, v_hgrn_lb_logits, sw, sm, sv)
    loss = loss_rows[0, 0]

    chip = 2 * lax.axis_index("x") + lax.axis_index("y")
    g_cw = lax.dynamic_slice(dcw_tot, (0, chip * (D_FF // 4)), (3, D_FF // 4))
    cw_out = _adamw_whole("adamw_conv_w", g_cw, ffn_conv_w[0], m_ffn_conv_w[0], v_ffn_conv_w[0])

    res = {}
    for si, n in enumerate(names):
        shp = small_in[n][0].shape
        res[n] = tuple(small_out[4 * si + k].reshape(shp) for k in range(4))
    res["hgrn_lb_logits"] = tuple(lg_out)
    res["ffn_conv_w"] = (g_cw[None],) + tuple(o[None] for o in cw_out)

    def big(i):
        return tuple(big_out[i])

    res["w_in"] = tuple(o[None] for o in big(0))
    res["w_branch"] = tuple(jnp.stack([o0, o1])[None] for o0, o1 in zip(big(1), big(2)))
    res["w_out"] = tuple(o[None] for o in big(3))
    res["ffn_w_up"] = tuple(o[None] for o in big(4))
    res["ffn_w_down"] = tuple(o[None] for o in big(5))
    res["ple_w_proj"] = tuple(o[None] for o in big(6))
    res["ple_w_gate"] = tuple(o[None] for o in big(7))

    order = ["w_in", "sgu_w_s", "sgu_b_s", "sgu_norm_g", "sgu_norm_b", "hgrn_lb_logits",
             "hgrn_norm_g", "w_branch", "w_out", "ln1_g", "ln1_b", "ffn_w_up", "ffn_conv_w",
             "ffn_conv_b", "ffn_w_down", "ln2_g", "ln2_b", "ple_w_proj", "ple_w_gate"]
    outs = [loss, gx.reshape(1, t, D_MODEL)]
    for k in range(4):
        outs += [res[n][k] for n in order]
    return tuple(outs)
```

```python
import functools

import jax
import jax.numpy as jnp
from jax import lax
from jax.experimental import pallas as pl
from jax.experimental.pallas import tpu as pltpu

F32 = jnp.float32
BF16 = jnp.bfloat16
HIGHEST = lax.Precision.HIGHEST
MESH = pl.DeviceIdType.MESH

D_MODEL = 1024
CHUNK = 64
SGU_BLOCK = 128
N_GROUP = 8
N_HEAD = 8
HEAD_DIM = 128
D_FF = 2816
PLE_DIM = 256
IN_COLS = 8192
LN_EPS = 1e-5
RMS_EPS = 1e-6
ALPHA = 2.0 ** 0.25
N_CHIP = 4
N_DEV = 8

ADAM_LR = 0.001
ADAM_B1 = 0.9
ADAM_B2 = 0.999
ADAM_EPS = 1e-08
ADAM_WD = 0.01
ADAM_STEP = 10

VMEM_LIMIT = 56 * 1024 * 1024

NN = (((1,), (0,)), ((), ()))
NT = (((1,), (1,)), ((), ()))
TN = (((0,), (0,)), ((), ()))


def _pc(body, *, name, out_shape, grid=None, in_specs=None, out_specs=None, scratch=(),
        sem=None, nsp=0, vmem=VMEM_LIMIT):
    params = dict(vmem_limit_bytes=vmem)
    if sem is not None:
        params["dimension_semantics"] = sem
    kw = dict(name=name, out_shape=out_shape, compiler_params=pltpu.CompilerParams(**params))
    if nsp:
        kw["grid_spec"] = pltpu.PrefetchScalarGridSpec(
            num_scalar_prefetch=nsp, grid=grid, in_specs=in_specs, out_specs=out_specs,
            scratch_shapes=list(scratch))
    else:
        if grid is not None:
            kw["grid"] = grid
        if in_specs is not None:
            kw["in_specs"] = in_specs
            kw["out_specs"] = out_specs
        kw["scratch_shapes"] = list(scratch)
    return pl.pallas_call(body, **kw)


def _dot(a, b, dims=NN):
    return lax.dot_general(a.astype(BF16), b.astype(BF16), dims, preferred_element_type=F32)


def _dot32(a, b, dims=NN):
    return lax.dot_general(a, b, dims, precision=HIGHEST, preferred_element_type=F32)


def _sig(x):
    return 1.0 / (1.0 + jnp.exp(-x))


_GC = 0.7978845608028654
_GA = 0.044715


def _gelu(x):
    return 0.5 * x * (1.0 + jnp.tanh(_GC * (x + _GA * x * x * x)))


def _gelu_and_grad(x):
    t = jnp.tanh(_GC * (x + _GA * x * x * x))
    g = 0.5 * x * (1.0 + t)
    dg = 0.5 * (1.0 + t) + 0.5 * x * (1.0 - t * t) * _GC * (1.0 + 3.0 * _GA * x * x)
    return g, dg


def _ln_stats(r):
    mu = jnp.mean(r, axis=-1, keepdims=True)
    xc = r - mu
    var = jnp.mean(xc * xc, axis=-1, keepdims=True)
    rstd = lax.rsqrt(var + LN_EPS)
    return xc * rstd, rstd


def _ln_bwd(dxh, xh, rstd):
    m1 = jnp.mean(dxh, axis=-1, keepdims=True)
    m2 = jnp.mean(dxh * xh, axis=-1, keepdims=True)
    return rstd * (dxh - m1 - xh * m2)


def _colsum8(v):
    return jnp.broadcast_to(jnp.sum(v, axis=0, keepdims=True), (8, v.shape[1]))


def _adamw(w, g, m, v):
    m2 = ADAM_B1 * m + (1.0 - ADAM_B1) * g
    v2 = ADAM_B2 * v + (1.0 - ADAM_B2) * (g * g)
    m_hat = m2 / (1.0 - ADAM_B1 ** ADAM_STEP)
    v_hat = v2 / (1.0 - ADAM_B2 ** ADAM_STEP)
    delta = -ADAM_LR * (m_hat / (jnp.sqrt(v_hat) + ADAM_EPS) + ADAM_WD * w)
    return delta, m2, v2


def _row_tile(rows, cols, itemsize=4, budget=1 << 20, mult=8):
    best = mult
    for tr in range(mult, rows + 1, mult):
        if rows % tr == 0 and tr * cols * itemsize <= budget:
            best = tr
    return best


def _mm(name, a, b, dims, grid, a_spec, b_spec, out_shape, o_spec):
    out_dtype = out_shape.dtype

    def body(a_ref, b_ref, o_ref):
        o_ref[...] = _dot(a_ref[...], b_ref[...], dims).astype(out_dtype)

    return _pc(body, name=name, out_shape=out_shape, grid=grid, in_specs=[a_spec, b_spec],
               out_specs=o_spec, sem=("parallel", "parallel"))(a, b)


class _Comm:
    def __init__(self, ins, out_shapes, sems, start, finish):
        self.ins, self.out_shapes, self.sems = list(ins), list(out_shapes), list(sems)
        self.start, self.finish = start, finish


def _hosted_call(body, comm, first, last, *, name, out_shape, grid, in_specs, out_specs, scratch, sem,
                 args):
    n_in, n_out, n_scr = len(in_specs), len(out_shape), len(scratch)
    nci, nco = len(comm.ins), len(comm.out_shapes)

    def wrapped(*refs):
        pos = n_in
        own_in, c_in = refs[:pos], refs[pos:pos + nci]
        pos += nci
        own_out, c_out = refs[pos:pos + n_out], refs[pos + n_out:pos + n_out + nco]
        pos += n_out + nco
        own_scr, c_sem = refs[pos:pos + n_scr], refs[pos + n_scr:]

        @pl.when(first())
        def _():
            comm.start(c_in, c_out, c_sem)

        body(*own_in, *own_out, *own_scr)

        @pl.when(last())
        def _():
            comm.finish(c_in, c_out, c_sem)

    return _pc(wrapped, name=name, out_shape=tuple(out_shape) + tuple(comm.out_shapes), grid=grid,
               in_specs=list(in_specs) + [ANY] * nci, out_specs=tuple(out_specs) + tuple([ANY] * nco),
               scratch=list(scratch) + comm.sems, sem=sem)(*args, *comm.ins)


def _grid1_call(body, comm, n, *, name, out_shape, in_specs, out_specs, scratch, args):
    if comm is None:
        return _pc(body, name=name, out_shape=out_shape, grid=(n,), in_specs=in_specs,
                   out_specs=out_specs, scratch=scratch, sem=("arbitrary",))(*args), ()
    res = _hosted_call(body, comm, lambda: pl.program_id(0) == 0, lambda: pl.program_id(0) == n - 1,
                       name=name, out_shape=out_shape, grid=(n,), in_specs=in_specs,
                       out_specs=out_specs, scratch=scratch, sem=("arbitrary",), args=args)
    return res[:len(out_shape)], res[len(out_shape):]


def _run_comm(name, comm):
    nci, nco = len(comm.ins), len(comm.out_shapes)

    def body(*refs):
        c_in, c_out, c_sem = refs[:nci], refs[nci:nci + nco], refs[nci + nco:]
        comm.start(c_in, c_out, c_sem)
        comm.finish(c_in, c_out, c_sem)

    return _pc(body, name=name, out_shape=tuple(comm.out_shapes), in_specs=[ANY] * nci,
               out_specs=tuple([ANY] * nco), scratch=comm.sems)(*comm.ins)


def _mm_tn(name, a, b, tm, tn, stacked=False):
    t, m = a.shape
    _, n = b.shape
    if stacked:
        assert tm == m
        out_shape = jax.ShapeDtypeStruct((n // tn, m, tn), BF16)
        o_spec = pl.BlockSpec((None, tm, tn), lambda i, j: (j, 0, 0))
    else:
        out_shape = jax.ShapeDtypeStruct((m, n), BF16)
        o_spec = pl.BlockSpec((tm, tn), lambda i, j: (i, j))
    return _mm(name, a, b, TN, (m // tm, n // tn),
               pl.BlockSpec((t, tm), lambda i, j: (0, i)),
               pl.BlockSpec((t, tn), lambda i, j: (0, j)),
               out_shape, o_spec)


def _in_proj_xgrad(dh_parts, win_st, dr1, tm, comm=None):
    t = dr1.shape[0]
    ni = t // tm

    def body(a0, a1, a2, a3, b_ref, add_ref, o_ref, acc):
        j = pl.program_id(1)
        for jj, a_ref in enumerate((a0, a1, a2, a3)):
            @pl.when(j == jj)
            def _(jj=jj, a_ref=a_ref):
                prod = _dot(a_ref[...], b_ref[...], NT)
                if jj == 0:
                    acc[...] = prod + ALPHA * add_ref[...]
                elif jj < N_CHIP - 1:
                    acc[...] += prod
                else:
                    o_ref[...] = acc[...] + prod

    a_spec = pl.BlockSpec((tm, 2 * D_MODEL), lambda i, j: (i, 0))
    tile = pl.BlockSpec((tm, D_MODEL), lambda i, j: (i, 0))
    kw = dict(name="in_proj_xgrad", out_shape=(jax.ShapeDtypeStruct((t, D_MODEL), F32),),
              grid=(ni, N_CHIP),
              in_specs=[a_spec] * 4 + [pl.BlockSpec((None, D_MODEL, 2 * D_MODEL), lambda i, j: (j, 0, 0)),
                                       tile],
              out_specs=(tile,), scratch=[pltpu.VMEM((tm, D_MODEL), F32)],
              sem=("arbitrary", "arbitrary"))
    args = list(dh_parts) + [win_st, dr1]
    if comm is None:
        return _pc(body, **kw)(*args)[0], ()
    res = _hosted_call(body, comm,
                       lambda: (pl.program_id(0) == 0) & (pl.program_id(1) == 0),
                       lambda: (pl.program_id(0) == ni - 1) & (pl.program_id(1) == N_CHIP - 1),
                       args=args, **kw)
    return res[0], res[1:]


def _sgu_mixed(v, wm_ref, bsb_ref, gv, bv):
    gl, dgl = _gelu_and_grad(v)
    vh, rstd = _ln_stats(gl)
    vn = vh * gv + bv
    mixed = []
    for g in range(N_GROUP):
        sl = slice(g * 128, (g + 1) * 128)
        mixed.append(_dot(wm_ref[g], vn[:, sl]) + bsb_ref[g])
    return dgl, vh, rstd, vn, mixed


def _sgu_fwd(h, wm, bsb, gv, bv, comm=None):
    t = h.shape[0]

    def body(u_ref, v_ref, wm_ref, bsb_ref, gv_ref, bv_ref, ya_ref):
        u = u_ref[...].astype(F32)
        _, _, _, _, mixed = _sgu_mixed(v_ref[...].astype(F32), wm_ref, bsb_ref, gv_ref[...], bv_ref[...])
        gu = _gelu(u)
        for g in range(N_GROUP):
            sl = slice(g * 128, (g + 1) * 128)
            ya_ref[:, sl] = (gu[:, sl] * mixed[g]).astype(BF16)

    full3 = pl.BlockSpec((N_GROUP, 128, 128), lambda i: (0, 0, 0))
    vec = pl.BlockSpec((1, D_MODEL), lambda i: (0, 0))
    (ya,), extra = _grid1_call(
        body, comm, t // SGU_BLOCK, name="sgu_fwd",
        out_shape=(jax.ShapeDtypeStruct((t, D_MODEL), BF16),),
        in_specs=[pl.BlockSpec((SGU_BLOCK, D_MODEL), lambda i: (i, 0)),
                  pl.BlockSpec((SGU_BLOCK, D_MODEL), lambda i: (i, 1)),
                  full3, full3, vec, vec],
        out_specs=(pl.BlockSpec((SGU_BLOCK, D_MODEL), lambda i: (i, 0)),),
        scratch=[], args=(h, h, wm, bsb, gv, bv))
    return ya, extra


def _sgu_bwd(h, dya, wm, wmt, bsb, gv, bv, maskf, comm=None):
    t = h.shape[0]
    nb = t // SGU_BLOCK

    def body(u_ref, v_ref, dya_ref, wm_ref, wmt_ref, bsb_ref, gv_ref, bv_ref, mask_ref,
             dh_ref, dws_ref, dbs_ref, dgv_ref, dbv_ref, dmix_acc):
        i = pl.program_id(0)

        @pl.when(i == 0)
        def _():
            dws_ref[...] = jnp.zeros_like(dws_ref)
            dgv_ref[...] = jnp.zeros_like(dgv_ref)
            dbv_ref[...] = jnp.zeros_like(dbv_ref)
            dmix_acc[...] = jnp.zeros_like(dmix_acc)

        u = u_ref[...].astype(F32)
        gvv = gv_ref[...]
        dgl_v, vh, rstd, vn, mixed = _sgu_mixed(v_ref[...].astype(F32), wm_ref, bsb_ref, gvv, bv_ref[...])
        gu, dgl_u = _gelu_and_grad(u)
        dya_v = dya_ref[...].astype(F32)
        dvn_parts = []
        for g in range(N_GROUP):
            sl = slice(g * 128, (g + 1) * 128)
            d_y = dya_v[:, sl]
            dh_ref[:, sl] = (d_y * mixed[g] * dgl_u[:, sl]).astype(BF16)
            d_mixed = d_y * gu[:, sl]
            dmix_acc[g] += d_mixed
            dws_ref[g] += _dot(d_mixed, vn[:, sl], NT) * mask_ref[...]
            dvn_parts.append(_dot(wmt_ref[g], d_mixed))
        dvn = jnp.concatenate(dvn_parts, axis=1)
        dgv_ref[...] += _colsum8(dvn * vh)
        dbv_ref[...] += _colsum8(dvn)
        d_gl = _ln_bwd(dvn * gvv, vh, rstd)
        dh_ref[:, D_MODEL:] = (d_gl * dgl_v).astype(BF16)

        @pl.when(i == nb - 1)
        def _():
            rowid = lax.broadcasted_iota(jnp.int32, (8, 128), 0)
            ones = jnp.ones((8, 128), F32)
            acc = jnp.zeros((8, 128), F32)
            for g in range(N_GROUP):
                rs = _dot32(ones, dmix_acc[g], NT)
                acc = jnp.where(rowid == g, rs, acc)
            dbs_ref[...] = acc

    full3 = pl.BlockSpec((N_GROUP, 128, 128), lambda i: (0, 0, 0))
    vec = pl.BlockSpec((1, D_MODEL), lambda i: (0, 0))
    acc8 = pl.BlockSpec((8, D_MODEL), lambda i: (0, 0))
    return _grid1_call(
        body, comm, nb, name="sgu_bwd",
        out_shape=(jax.ShapeDtypeStruct((t, 2 * D_MODEL), BF16),
                   jax.ShapeDtypeStruct((N_GROUP, 128, 128), F32),
                   jax.ShapeDtypeStruct((8, 128), F32),
                   jax.ShapeDtypeStruct((8, D_MODEL), F32),
                   jax.ShapeDtypeStruct((8, D_MODEL), F32)),
        in_specs=[pl.BlockSpec((SGU_BLOCK, D_MODEL), lambda i: (i, 0)),
                  pl.BlockSpec((SGU_BLOCK, D_MODEL), lambda i: (i, 1)),
                  pl.BlockSpec((SGU_BLOCK, D_MODEL), lambda i: (i, 0)),
                  full3, full3, full3, vec, vec,
                  pl.BlockSpec((128, 128), lambda i: (0, 0))],
        out_specs=(pl.BlockSpec((SGU_BLOCK, 2 * D_MODEL), lambda i: (i, 0)),
                   full3, pl.BlockSpec((8, 128), lambda i: (0, 0)), acc8, acc8),
        scratch=[pltpu.VMEM((N_GROUP, 128, 128), F32)],
        args=(h, h, dya, wm, wmt, bsb, gv, bv, maskf))


def _tri_masks():
    row = lax.broadcasted_iota(jnp.int32, (CHUNK, CHUNK), 0)
    col = lax.broadcasted_iota(jnp.int32, (CHUNK, CHUNK), 1)
    return col <= row, col >= row


def _heads(v):
    return [v[:, hd * HEAD_DIM:(hd + 1) * HEAD_DIM] for hd in range(N_HEAD)]


def _tri_cumsum(tri_bf, v):
    hi = v.astype(BF16)
    r = v - hi.astype(F32)
    mid = r.astype(BF16)
    lo = (r - mid.astype(F32)).astype(BF16)
    return _dot(tri_bf, hi) + _dot(tri_bf, mid) + _dot(tri_bf, lo)


def _hgrn_chunk(q, fp, ii, lb, st_heads, causal):
    sg = _sig(fp)
    f = lb + (1.0 - lb) * sg
    k = 1.0 - f
    c = _tri_cumsum(causal.astype(BF16), jnp.log(f))
    ec = jnp.exp(c)
    en = jnp.exp(-c)
    sq = _sig(q)
    qt = q * sq * ec
    kt = k * en
    ecl = jnp.exp(c[CHUNK - 1:CHUNK, :])
    kk = kt * ecl
    qtb, ktb, iib, kkb = qt.astype(BF16), kt.astype(BF16), ii.astype(BF16), kk.astype(BF16)
    attn, o = [], []
    for hd, (qh, kh, ih) in enumerate(zip(_heads(qtb), _heads(ktb), _heads(iib))):
        a = jnp.where(causal, _dot(qh, kh, NT), 0.0).astype(BF16)
        attn.append(a)
        o.append(_dot(a, ih) + _dot(qh, st_heads[hd], NT))
    return dict(sg=sg, f=f, k=k, ec=ec, en=en, sq=sq, ecl=ecl, kk=kk, qtb=qtb, ktb=ktb, iib=iib,
                kkb=kkb, attn=attn, o=o)


def _rms_heads(o_heads):
    rinv = [lax.rsqrt(jnp.mean(o * o, axis=-1, keepdims=True) + RMS_EPS) for o in o_heads]
    return rinv, jnp.concatenate([o * r for o, r in zip(o_heads, rinv)], axis=1)


HG_CHUNKS = 4
HG_ROWS = HG_CHUNKS * CHUNK


def _hgrn_fwd(h, logits, gn, comm=None):
    t = h.shape[0]
    nb = t // HG_ROWS

    def body(q_ref, f_ref, i_ref, og_ref, lg_ref, gn_ref, yb_ref, st_ref, state):
        @pl.when(pl.program_id(0) == 0)
        def _():
            state[...] = jnp.zeros_like(state)

        causal, _ = _tri_masks()
        lb = _sig(lg_ref[0:1, :] - lg_ref[1:2, :])
        gnv = gn_ref[...]
        st = [state[hd] for hd in range(N_HEAD)]
        for cc in range(HG_CHUNKS):
            rows = slice(cc * CHUNK, (cc + 1) * CHUNK)
            og = og_ref[rows, :].astype(F32)
            r = _hgrn_chunk(q_ref[rows, :].astype(F32), f_ref[rows, :].astype(F32),
                            i_ref[rows, :].astype(F32), lb, [s.astype(BF16) for s in st], causal)
            _, on = _rms_heads(r["o"])
            yb_ref[rows, :] = (on * gnv * (og * _sig(og))).astype(BF16)
            for hd in range(N_HEAD):
                st_ref[cc, hd] = st[hd]
            st = [s * e + _dot(ih, kh, TN)
                  for s, e, ih, kh in zip(st, _heads(r["ecl"]), _heads(r["iib"]), _heads(r["kkb"]))]
        for hd in range(N_HEAD):
            state[hd] = st[hd]

    def col(k):
        return pl.BlockSpec((HG_ROWS, D_MODEL), lambda ci: (ci, k))

    return _grid1_call(body, comm, nb, name="hgrn_fwd",
                       out_shape=(jax.ShapeDtypeStruct((t, D_MODEL), BF16),
                                  jax.ShapeDtypeStruct((t // CHUNK, N_HEAD, HEAD_DIM, HEAD_DIM), F32)),
                       in_specs=[col(2), col(3), col(4), col(5),
                                 pl.BlockSpec((2, D_MODEL), lambda ci: (0, 0)),
                                 pl.BlockSpec((1, D_MODEL), lambda ci: (0, 0))],
                       out_specs=(pl.BlockSpec((HG_ROWS, D_MODEL), lambda ci: (ci, 0)),
                                  pl.BlockSpec((HG_CHUNKS, N_HEAD, HEAD_DIM, HEAD_DIM),
                                               lambda ci: (ci, 0, 0, 0))),
                       scratch=[pltpu.VMEM((N_HEAD, HEAD_DIM, HEAD_DIM), F32)],
                       args=(h, h, h, h, logits, gn))


def _hgrn_chunk_bwd(q, fp, ii, og, dy, gnv, lb, st, dsn, causal, anti):
    stb = [s.astype(BF16) for s in st]
    dsnb = [s.astype(BF16) for s in dsn]
    r = _hgrn_chunk(q, fp, ii, lb, stb, causal)
    rinv, on = _rms_heads(r["o"])
    so = _sig(og)
    sil = og * so
    d_og = dy * on * gnv * (so * (1.0 + og * (1.0 - so)))
    d_on = dy * gnv * sil
    d_ob = jnp.concatenate(
        [ri * (dn - oh * jnp.mean(dn * oh, axis=-1, keepdims=True))
         for ri, dn, oh in zip(rinv, _heads(d_on), _heads(on))], axis=1).astype(BF16)
    d_i, d_qt, d_kt, d_kk, d_st, st_dsn = [], [], [], [], [], []
    ecl = _heads(r["ecl"])
    for hd, (dh, qh, kh, ih, kkh) in enumerate(zip(_heads(d_ob), _heads(r["qtb"]), _heads(r["ktb"]),
                                                   _heads(r["iib"]), _heads(r["kkb"]))):
        d_attn = jnp.where(causal, _dot(dh, ih, NT), 0.0).astype(BF16)
        d_i.append(_dot(r["attn"][hd], dh, TN) + _dot(kkh, dsnb[hd], NT))
        d_qt.append(_dot(d_attn, kh) + _dot(dh, stb[hd]))
        d_kt.append(_dot(d_attn, qh, TN))
        d_kk.append(_dot(ih, dsnb[hd]))
        d_st.append(_dot(dh, qh, TN) + dsn[hd] * ecl[hd])
        st_dsn.append(jnp.sum(st[hd] * dsn[hd], axis=0, keepdims=True))
    d_qt = jnp.concatenate(d_qt, axis=1)
    d_kt = jnp.concatenate(d_kt, axis=1)
    d_kk = jnp.concatenate(d_kk, axis=1)
    kk = r["kk"]
    d_cl = r["ecl"] * jnp.concatenate(st_dsn, axis=1) + jnp.sum(kk * d_kk, axis=0, keepdims=True)
    d_k = (d_kk * r["ecl"] + d_kt) * r["en"]
    d_c = d_qt * r["qtb"].astype(F32) - d_kt * r["ktb"].astype(F32) - d_kk * kk
    rowid = lax.broadcasted_iota(jnp.int32, (CHUNK, D_MODEL), 0)
    d_c = d_c + jnp.where(rowid == CHUNK - 1, d_cl, 0.0)
    d_lf = _tri_cumsum(anti.astype(BF16), d_c)
    d_f = d_lf / r["f"] - d_k
    sg, sq = r["sg"], r["sq"]
    d_q = d_qt * r["ec"] * (sq * (1.0 + q * (1.0 - sq)))
    d_fp = d_f * (1.0 - lb) * sg * (1.0 - sg)
    return (d_q, d_fp, jnp.concatenate(d_i, axis=1), d_og, d_st,
            _colsum8(dy * on * sil), _colsum8(d_f * (1.0 - sg)))


def _hgrn_bwd(h, dyb, st_all, logits, gn, comm=None):
    t = h.shape[0]
    nb = t // HG_ROWS

    def body(q_ref, f_ref, i_ref, og_ref, dyb_ref, st_ref, lg_ref, gn_ref,
             dh1_ref, dh2_ref, dlb_ref, dgn_ref, dstate):
        @pl.when(pl.program_id(0) == 0)
        def _():
            dstate[...] = jnp.zeros_like(dstate)
            dlb_ref[...] = jnp.zeros_like(dlb_ref)
            dgn_ref[...] = jnp.zeros_like(dgn_ref)

        causal, anti = _tri_masks()
        lb = _sig(lg_ref[0:1, :] - lg_ref[1:2, :])
        gnv = gn_ref[...]
        dsn = [dstate[hd] for hd in range(N_HEAD)]
        dgn_acc = jnp.zeros((8, D_MODEL), F32)
        dlb_acc = jnp.zeros((8, D_MODEL), F32)
        for cc in reversed(range(HG_CHUNKS)):
            rows = slice(cc * CHUNK, (cc + 1) * CHUNK)
            d_q, d_fp, d_i, d_og, dsn, dgn_c, dlb_c = _hgrn_chunk_bwd(
                q_ref[rows, :].astype(F32), f_ref[rows, :].astype(F32), i_ref[rows, :].astype(F32),
                og_ref[rows, :].astype(F32), dyb_ref[rows, :].astype(F32), gnv, lb,
                [st_ref[cc, hd] for hd in range(N_HEAD)], dsn, causal, anti)
            dgn_acc = dgn_acc + dgn_c
            dlb_acc = dlb_acc + dlb_c
            dh1_ref[rows, :D_MODEL] = d_q.astype(BF16)
            dh1_ref[rows, D_MODEL:] = d_fp.astype(BF16)
            dh2_ref[rows, :D_MODEL] = d_i.astype(BF16)
            dh2_ref[rows, D_MODEL:] = d_og.astype(BF16)
        dgn_ref[...] += dgn_acc
        dlb_ref[...] += dlb_acc
        for hd in range(N_HEAD):
            dstate[hd] = dsn[hd]

    def col(k):
        return pl.BlockSpec((HG_ROWS, D_MODEL), lambda ci: (nb - 1 - ci, k))

    acc8 = pl.BlockSpec((8, D_MODEL), lambda ci: (0, 0))
    pair = pl.BlockSpec((HG_ROWS, 2 * D_MODEL), lambda ci: (nb - 1 - ci, 0))
    return _grid1_call(body, comm, nb, name="hgrn_bwd",
                       out_shape=(jax.ShapeDtypeStruct((t, 2 * D_MODEL), BF16),
                                  jax.ShapeDtypeStruct((t, 2 * D_MODEL), BF16),
                                  jax.ShapeDtypeStruct((8, D_MODEL), F32),
                                  jax.ShapeDtypeStruct((8, D_MODEL), F32)),
                       in_specs=[col(2), col(3), col(4), col(5),
                                 pl.BlockSpec((HG_ROWS, D_MODEL), lambda ci: (nb - 1 - ci, 0)),
                                 pl.BlockSpec((HG_CHUNKS, N_HEAD, HEAD_DIM, HEAD_DIM),
                                              lambda ci: (nb - 1 - ci, 0, 0, 0)),
                                 pl.BlockSpec((2, D_MODEL), lambda ci: (0, 0)),
                                 pl.BlockSpec((1, D_MODEL), lambda ci: (0, 0))],
                       out_specs=(pair, pair, acc8, acc8),
                       scratch=[pltpu.VMEM((N_HEAD, HEAD_DIM, HEAD_DIM), F32)],
                       args=(h, h, h, h, dyb, st_all, logits, gn))


def _mix_fwd(ya, yb, h, x, wb0, wb1, wo, g1, b1, tm, comm=None):
    t = x.shape[0]

    def body(ya_ref, yb_ref, ga_ref, gb_ref, x_ref, wb0_ref, wb1_ref, wo_ref, g1_ref, b1_ref,
             r1_ref, a_ref, b_ref, m_ref, x1_ref):
        a = _dot(ya_ref[...], wb0_ref[...])
        b = _dot(yb_ref[...], wb1_ref[...])
        m = _sig(ga_ref[...].astype(F32)) * a + _sig(gb_ref[...].astype(F32)) * b
        r1 = ALPHA * x_ref[...] + _dot(m, wo_ref[...])
        xh, _ = _ln_stats(r1)
        r1_ref[...] = r1
        a_ref[...] = a.astype(BF16)
        b_ref[...] = b.astype(BF16)
        m_ref[...] = m.astype(BF16)
        x1_ref[...] = (xh * g1_ref[...] + b1_ref[...]).astype(BF16)

    tile = pl.BlockSpec((tm, D_MODEL), lambda i: (i, 0))
    wsp = pl.BlockSpec((D_MODEL, D_MODEL), lambda i: (0, 0))
    vec = pl.BlockSpec((1, D_MODEL), lambda i: (0, 0))
    f32o = jax.ShapeDtypeStruct((t, D_MODEL), F32)
    bfo = jax.ShapeDtypeStruct((t, D_MODEL), BF16)
    return _grid1_call(body, comm, t // tm, name="mix_fwd", out_shape=(f32o, bfo, bfo, bfo, bfo),
                       in_specs=[tile, tile,
                                 pl.BlockSpec((tm, D_MODEL), lambda i: (i, 6)),
                                 pl.BlockSpec((tm, D_MODEL), lambda i: (i, 7)),
                                 tile, wsp, wsp, wsp, vec, vec],
                       out_specs=(tile, tile, tile, tile, tile),
                       scratch=[], args=(ya, yb, h, h, x, wb0, wb1, wo, g1, b1))


def _mix_bwd(dr1, h, a, b, wo, wb0, wb1, tm):
    t = dr1.shape[0]

    def body(dr1_ref, ga_ref, gb_ref, a_ref, b_ref, wo_ref, wb0_ref, wb1_ref,
             da_ref, db_ref, dh3_ref, dya_ref, dyb_ref):
        d_m = _dot(dr1_ref[...], wo_ref[...], NT)
        sa = _sig(ga_ref[...].astype(F32))
        sb = _sig(gb_ref[...].astype(F32))
        d_a = (d_m * sa).astype(BF16)
        d_b = (d_m * sb).astype(BF16)
        da_ref[...] = d_a
        db_ref[...] = d_b
        dh3_ref[:, :D_MODEL] = (d_m * a_ref[...].astype(F32) * sa * (1.0 - sa)).astype(BF16)
        dh3_ref[:, D_MODEL:] = (d_m * b_ref[...].astype(F32) * sb * (1.0 - sb)).astype(BF16)
        dya_ref[...] = _dot(d_a, wb0_ref[...], NT).astype(BF16)
        dyb_ref[...] = _dot(d_b, wb1_ref[...], NT).astype(BF16)

    tile = pl.BlockSpec((tm, D_MODEL), lambda i: (i, 0))
    wsp = pl.BlockSpec((D_MODEL, D_MODEL), lambda i: (0, 0))
    f32o = jax.ShapeDtypeStruct((t, D_MODEL), F32)
    bfo = jax.ShapeDtypeStruct((t, D_MODEL), BF16)
    return _pc(body, name="mix_bwd",
               out_shape=(bfo, bfo, jax.ShapeDtypeStruct((t, 2 * D_MODEL), BF16), bfo, bfo),
               grid=(t // tm,),
               in_specs=[tile,
                         pl.BlockSpec((tm, D_MODEL), lambda i: (i, 6)),
                         pl.BlockSpec((tm, D_MODEL), lambda i: (i, 7)),
                         tile, tile, wsp, wsp, wsp],
               out_specs=(tile, tile, pl.BlockSpec((tm, 2 * D_MODEL), lambda i: (i, 0)),
                          tile, tile),
               sem=("parallel",))(dr1, h, h, a, b, wo, wb0, wb1)


FF_TILE = 1408
FF_NJ = D_FF // FF_TILE


def _shift_down(v, k):
    return pltpu.roll(v, k, 0)


def _shift_up(v, k):
    return pltpu.roll(v, v.shape[0] - k, 0)


HALO = 16
FF_PIECES = ((0, 768), (768, FF_TILE))


def _ffn_up_act(x1b, wup_st, convw, convb, tm, comm):
    t = x1b.shape[0]
    ni = t // tm
    nth = tm // HALO

    def body(x_ref, xp_ref, wg_ref, wv_ref, cw_ref, cb_ref, h2_ref, act_ref):
        wg = wg_ref[...]
        gate = _dot(x_ref[...], wg).astype(BF16)
        val = _dot(x_ref[...], wv_ref[...]).astype(BF16)
        prev = (_dot(xp_ref[...], wg) * (pl.program_id(0) > 0).astype(F32)).astype(BF16)
        h2_ref[0] = gate
        h2_ref[1] = val
        ext = jnp.concatenate([prev.astype(F32), gate.astype(F32)], axis=0)
        gc = (cw_ref[0:1, :] * _shift_down(ext, 2) + cw_ref[1:2, :] * _shift_down(ext, 1)
              + cw_ref[2:3, :] * ext + cb_ref[...])[HALO:, :]
        act_ref[...] = (_gelu(gc) * val.astype(F32)).astype(BF16)

    res = _hosted_call(
        body, comm,
        lambda: (pl.program_id(0) == 0) & (pl.program_id(1) == 0),
        lambda: (pl.program_id(0) == ni - 1) & (pl.program_id(1) == FF_NJ - 1),
        name="ffn_up",
        out_shape=(jax.ShapeDtypeStruct((2, t, D_FF), BF16), jax.ShapeDtypeStruct((t, D_FF), BF16)),
        grid=(ni, FF_NJ),
        in_specs=[pl.BlockSpec((tm, D_MODEL), lambda i, j: (i, 0)),
                  pl.BlockSpec((HALO, D_MODEL), lambda i, j: (jnp.maximum(i * nth - 1, 0), 0)),
                  pl.BlockSpec((None, D_MODEL, FF_TILE), lambda i, j: (j, 0, 0)),
                  pl.BlockSpec((None, D_MODEL, FF_TILE), lambda i, j: (j + FF_NJ, 0, 0)),
                  pl.BlockSpec((3, FF_TILE), lambda i, j: (0, j)),
                  pl.BlockSpec((1, FF_TILE), lambda i, j: (0, j))],
        out_specs=(pl.BlockSpec((2, tm, FF_TILE), lambda i, j: (0, i, j)),
                   pl.BlockSpec((tm, FF_TILE), lambda i, j: (i, j))),
        scratch=[], sem=("arbitrary", "arbitrary"),
        args=(x1b, x1b, wup_st, wup_st, convw, convb))
    return res[0], res[1], res[2:]


def _out_fwd_bwd(act, x1b, r1, p2, tgt, wd, wpg, wpp, g1, b1, g2, b2, tm):
    t = r1.shape[0]

    def body(act_ref, x1b_ref, r1_ref, p_ref, tgt_ref, wd_ref, wpg_ref, wpp_ref,
             g1_ref, b1_ref, g2_ref, b2_ref,
             dr2_ref, dpg_ref, dpp_ref, loss_ref, dg2_ref, db2_ref):
        i = pl.program_id(0)

        @pl.when(i == 0)
        def _():
            loss_ref[...] = jnp.zeros_like(loss_ref)
            dg2_ref[...] = jnp.zeros_like(dg2_ref)
            db2_ref[...] = jnp.zeros_like(db2_ref)

        ffn = _dot(act_ref[...], wd_ref[...])
        pg = _dot(x1b_ref[...], wpg_ref[...])
        pp = _dot(p_ref[...], wpp_ref[...])
        s = _sig(pg)
        xh1, _ = _ln_stats(r1_ref[...])
        x1 = xh1 * g1_ref[...] + b1_ref[...]
        r2 = ALPHA * x1 + ffn + s * pp
        xh2, rstd2 = _ln_stats(r2)
        g2v = g2_ref[...]
        diff = xh2 * g2v + b2_ref[...] - tgt_ref[...]
        part = jnp.sum(jnp.sum(diff * diff, axis=1, keepdims=True), axis=0, keepdims=True)
        loss_ref[...] += jnp.broadcast_to(part * (0.5 / D_MODEL), loss_ref.shape)
        dy = diff * (1.0 / D_MODEL)
        dg2_ref[...] += _colsum8(dy * xh2)
        db2_ref[...] += _colsum8(dy)
        dr2 = _ln_bwd(dy * g2v, xh2, rstd2)
        dr2_ref[...] = dr2
        dpg_ref[...] = (dr2 * pp * s * (1.0 - s)).astype(BF16)
        dpp_ref[...] = (dr2 * s).astype(BF16)

    tile = pl.BlockSpec((tm, D_MODEL), lambda i: (i, 0))
    vec = pl.BlockSpec((1, D_MODEL), lambda i: (0, 0))
    acc8 = pl.BlockSpec((8, D_MODEL), lambda i: (0, 0))
    acc_shape = jax.ShapeDtypeStruct((8, D_MODEL), F32)
    return _pc(body, name="out_fwd_bwd",
               out_shape=(jax.ShapeDtypeStruct((t, D_MODEL), F32),
                          jax.ShapeDtypeStruct((t, D_MODEL), BF16),
                          jax.ShapeDtypeStruct((t, D_MODEL), BF16),
                          acc_shape, acc_shape, acc_shape),
               grid=(t // tm,),
               in_specs=[pl.BlockSpec((tm, D_FF), lambda i: (i, 0)), tile, tile,
                         pl.BlockSpec((tm, PLE_DIM), lambda i: (i, 0)), tile,
                         pl.BlockSpec((D_FF, D_MODEL), lambda i: (0, 0)),
                         pl.BlockSpec((D_MODEL, D_MODEL), lambda i: (0, 0)),
                         pl.BlockSpec((PLE_DIM, D_MODEL), lambda i: (0, 0)),
                         vec, vec, vec, vec],
               out_specs=(tile, tile, tile, acc8, acc8, acc8),
               sem=("arbitrary",))(act, x1b, r1, p2, tgt, wd, wpg, wpp, g1, b1, g2, b2)


def _ffn_bwd(h2, dr2, wd, wup_st, dpg, wpg, r1, g1, convw, convb, tm):
    t = r1.shape[0]
    ni = t // tm
    nth = tm // HALO
    last_halo = t // HALO - 1
    main_rows = slice(HALO, HALO + tm)

    def body(g_ref, gp_ref, gn_ref, v_ref, vn_ref, dr2_ref, dr2n_ref, wd_ref, wug_ref, wuv_ref,
             cw_ref, cb_ref, dpg_ref, wpg_ref, r1_ref, g1_ref,
             dh2_ref, dr1_ref, dcw_ref, dcb_ref, dg1_ref, db1_ref, acc):
        i = pl.program_id(0)
        j = pl.program_id(1)

        @pl.when((i == 0) & (j == 0))
        def _():
            dcw_ref[...] = jnp.zeros_like(dcw_ref)
            dcb_ref[...] = jnp.zeros_like(dcb_ref)
            dg1_ref[...] = jnp.zeros_like(dg1_ref)
            db1_ref[...] = jnp.zeros_like(db1_ref)

        dr2v = dr2_ref[...].astype(BF16)
        dr2n = dr2n_ref[...].astype(BF16)
        first = (i > 0).astype(F32)
        more = (i < ni - 1).astype(F32)
        prod = None
        dcw_parts, dcb_parts = [], []
        for c0, c1 in FF_PIECES:
            pc = slice(c0, c1)
            zeros = jnp.zeros((HALO, c1 - c0), F32)
            da = _dot(dr2v, wd_ref[pc, :], NT)
            dnext = _dot(dr2n, wd_ref[pc, :], NT) * more
            ext = jnp.concatenate([gp_ref[:, pc].astype(F32) * first, g_ref[:, pc].astype(F32),
                                   gn_ref[:, pc].astype(F32)], axis=0)
            vext = jnp.concatenate([zeros, v_ref[:, pc].astype(F32), vn_ref[:, pc].astype(F32)], axis=0)
            dext = jnp.concatenate([zeros, da, dnext], axis=0)
            g2 = _shift_down(ext, 2)
            g1s = _shift_down(ext, 1)
            gc = cw_ref[0:1, pc] * g2 + cw_ref[1:2, pc] * g1s + cw_ref[2:3, pc] * ext + cb_ref[:, pc]
            gl, dgl = _gelu_and_grad(gc)
            d_gc = dext * vext * dgl
            d_gate = (cw_ref[2:3, pc] * d_gc + cw_ref[1:2, pc] * _shift_up(d_gc, 1)
                      + cw_ref[0:1, pc] * _shift_up(d_gc, 2))[main_rows, :].astype(BF16)
            d_val = (da * gl[main_rows, :]).astype(BF16)
            dh2_ref[0, :, pc] = d_gate
            dh2_ref[1, :, pc] = d_val
            dm = d_gc[main_rows, :]
            s0 = jnp.sum(dm * g2[main_rows, :], axis=0, keepdims=True)
            s1 = jnp.sum(dm * g1s[main_rows, :], axis=0, keepdims=True)
            s2 = jnp.sum(dm * ext[main_rows, :], axis=0, keepdims=True)
            rowid = lax.broadcasted_iota(jnp.int32, (8, c1 - c0), 0)
            dcw_parts.append(jnp.where(rowid == 0, s0, jnp.where(rowid == 1, s1,
                                                                 jnp.where(rowid == 2, s2, 0.0))))
            dcb_parts.append(_colsum8(dm))
            part = _dot(d_gate, wug_ref[:, pc], NT) + _dot(d_val, wuv_ref[:, pc], NT)
            prod = part if prod is None else prod + part
        dcw_part = jnp.concatenate(dcw_parts, axis=1)
        dcb_part = jnp.concatenate(dcb_parts, axis=1)
        for jj in range(FF_NJ):
            @pl.when(j == jj)
            def _(jj=jj):
                cols = slice(jj * FF_TILE, (jj + 1) * FF_TILE)
                dcw_ref[:, cols] += dcw_part
                dcb_ref[:, cols] += dcb_part

        @pl.when(j == 0)
        def _():
            acc[...] = prod

        @pl.when(j > 0)
        def _():
            acc[...] += prod

        @pl.when(j == FF_NJ - 1)
        def _():
            d_x1 = acc[...] + _dot(dpg_ref[...], wpg_ref[...], NT) + ALPHA * dr2_ref[...]
            xh, rstd = _ln_stats(r1_ref[...])
            dg1_ref[...] += _colsum8(d_x1 * xh)
            db1_ref[...] += _colsum8(d_x1)
            dr1_ref[...] = _ln_bwd(d_x1 * g1_ref[...], xh, rstd)

    def h2_main(part):
        return pl.BlockSpec((None, tm, FF_TILE), lambda i, j: (part, i, j))

    def h2_prev(part):
        return pl.BlockSpec((None, HALO, FF_TILE), lambda i, j: (part, jnp.maximum(i * nth - 1, 0), j))

    def h2_next(part):
        return pl.BlockSpec((None, HALO, FF_TILE),
                            lambda i, j: (part, jnp.minimum((i + 1) * nth, last_halo), j))

    tile = pl.BlockSpec((tm, D_MODEL), lambda i, j: (i, 0))
    acc8 = pl.BlockSpec((8, D_MODEL), lambda i, j: (0, 0))
    accff = pl.BlockSpec((8, D_FF), lambda i, j: (0, 0))
    acc_shape = jax.ShapeDtypeStruct((8, D_MODEL), F32)
    accff_shape = jax.ShapeDtypeStruct((8, D_FF), F32)
    return _pc(body, name="ffn_bwd",
               out_shape=(jax.ShapeDtypeStruct((2, t, D_FF), BF16),
                          jax.ShapeDtypeStruct((t, D_MODEL), F32),
                          accff_shape, accff_shape, acc_shape, acc_shape),
               grid=(ni, FF_NJ),
               in_specs=[h2_main(0), h2_prev(0), h2_next(0), h2_main(1), h2_next(1),
                         tile,
                         pl.BlockSpec((HALO, D_MODEL), lambda i, j: (jnp.minimum((i + 1) * nth, last_halo), 0)),
                         pl.BlockSpec((FF_TILE, D_MODEL), lambda i, j: (j, 0)),
                         pl.BlockSpec((None, D_MODEL, FF_TILE), lambda i, j: (j, 0, 0)),
                         pl.BlockSpec((None, D_MODEL, FF_TILE), lambda i, j: (j + FF_NJ, 0, 0)),
                         pl.BlockSpec((3, FF_TILE), lambda i, j: (0, j)),
                         pl.BlockSpec((1, FF_TILE), lambda i, j: (0, j)),
                         tile, pl.BlockSpec((D_MODEL, D_MODEL), lambda i, j: (0, 0)),
                         tile, pl.BlockSpec((1, D_MODEL), lambda i, j: (0, 0))],
               out_specs=(pl.BlockSpec((2, tm, FF_TILE), lambda i, j: (0, i, j)),
                          tile, accff, accff, acc8, acc8),
               scratch=[pltpu.VMEM((tm, D_MODEL), F32)],
               sem=("arbitrary", "arbitrary"))(h2, h2, h2, h2, h2, dr2, dr2, wd, wup_st, wup_st,
                                               convw, convb, dpg, wpg, r1, g1)


ANY = pl.BlockSpec(memory_space=pl.ANY)


def _chip_peers():
    x, y, c = lax.axis_index("x"), lax.axis_index("y"), lax.axis_index("c")
    return x, y, c, [(1 - x, y), (x, 1 - y), (1 - x, 1 - y)]


def _gather_comm(halved, whole=()):
    n, nw = len(halved), len(whole)

    def copies(ins, outs, sems):
        ici_send, ici_recv, d2d_send, d2d_recv, own_send, own_recv = sems
        x, y, c, peers = _chip_peers()
        me = 2 * x + y
        sibling = (x, y, 1 - c)
        own, ici, ici_wait, fwd, fwd_wait = [], [], [], [], []
        for ti in range(n + nw):
            src, dst = ins[ti], outs[ti]
            own.append(pltpu.make_async_remote_copy(
                src_ref=src, dst_ref=dst.at[me], send_sem=own_send.at[ti], recv_sem=own_recv.at[ti],
                device_id=sibling, device_id_type=MESH))
            for k, (px, py) in enumerate(peers):
                pk = 2 * px + py
                sem = dict(send_sem=ici_send.at[ti * 3 + k], recv_sem=ici_recv.at[ti * 3 + k],
                           device_id=(px, py, c), device_id_type=MESH)
                if ti < n:
                    ici.append(pltpu.make_async_remote_copy(src_ref=src.at[c], dst_ref=dst.at[me, c], **sem))
                    ici_wait.append(pltpu.make_async_remote_copy(src_ref=src.at[c], dst_ref=dst.at[pk, c], **sem))
                    dsem = dict(send_sem=d2d_send.at[ti * 3 + k], recv_sem=d2d_recv.at[ti * 3 + k],
                                device_id=sibling, device_id_type=MESH)
                    fwd.append(pltpu.make_async_remote_copy(src_ref=dst.at[pk, c], dst_ref=dst.at[pk, c], **dsem))
                    fwd_wait.append(pltpu.make_async_remote_copy(
                        src_ref=dst.at[pk, 1 - c], dst_ref=dst.at[pk, 1 - c], **dsem))
                else:
                    ici.append(pltpu.make_async_remote_copy(src_ref=src, dst_ref=dst.at[me], **sem))
                    ici_wait.append(pltpu.make_async_remote_copy(src_ref=src, dst_ref=dst.at[pk], **sem))
        return own, ici, ici_wait, fwd, fwd_wait

    def start(ins, outs, sems):
        own, ici, _, _, _ = copies(ins, outs, sems)
        for cp in own + ici:
            cp.start()

    def finish(ins, outs, sems):
        own, ici, ici_wait, fwd, fwd_wait = copies(ins, outs, sems)
        for i, cp in enumerate(ici_wait):
            cp.wait_recv()
            if i < len(fwd):
                fwd[i].start()
        for cp in fwd_wait + own:
            cp.wait_recv()
        for cp in own + ici + fwd:
            cp.wait_send()

    srcs = list(halved) + list(whole)
    return _Comm(srcs, [jax.ShapeDtypeStruct((N_CHIP,) + s.shape, s.dtype) for s in srcs],
                 [pltpu.SemaphoreType.DMA((3 * (n + nw),)), pltpu.SemaphoreType.DMA((3 * (n + nw),)),
                  pltpu.SemaphoreType.DMA((max(3 * n, 1),)), pltpu.SemaphoreType.DMA((max(3 * n, 1),)),
                  pltpu.SemaphoreType.DMA((n + nw,)), pltpu.SemaphoreType.DMA((n + nw,))],
                 start, finish)


def _sibling_exchange_comm(grads):
    n = len(grads)

    def copies(ins, outs, sems):
        send_sems, recv_sems = sems
        x, y, c = lax.axis_index("x"), lax.axis_index("y"), lax.axis_index("c")
        res = []
        for ti in range(n):
            half = ins[ti].shape[1] // 2
            res.append(pltpu.make_async_remote_copy(
                src_ref=ins[ti].at[:, pl.ds(pl.multiple_of((1 - c) * half, 16), half), :],
                dst_ref=outs[ti],
                send_sem=send_sems.at[ti], recv_sem=recv_sems.at[ti],
                device_id=(x, y, 1 - c), device_id_type=MESH))
        return res

    def start(ins, outs, sems):
        for cp in copies(ins, outs, sems):
            cp.start()

    def finish(ins, outs, sems):
        for cp in copies(ins, outs, sems):
            cp.wait()

    return _Comm(grads, [jax.ShapeDtypeStruct((N_CHIP, g.shape[1] // 2, g.shape[2]), g.dtype) for g in grads],
                 [pltpu.SemaphoreType.DMA((n,)), pltpu.SemaphoreType.DMA((n,))], start, finish)


def _in_proj_gathering(x2b, own, chip, tm, comm):
    t = x2b.shape[0]
    ni = t // tm
    half, cols = own.shape[1], own.shape[2]
    nci, nco = len(comm.ins), len(comm.out_shapes)

    def body(chip_ref, x_ref, own_ref, own_hbm, *rest):
        c_in = rest[:nci]
        h_ref, win_out = rest[nci:nci + 2]
        c_out = rest[nci + 2:nci + 2 + nco]
        w_scr, ici_send, ici_recv, d2d_send, d2d_recv, own_sems, ld_sems = rest[nci + 2 + nco:nci + 9 + nco]
        c_sem = rest[nci + 9 + nco:]
        s, i = pl.program_id(0), pl.program_id(1)
        x, y, c, peers = _chip_peers()
        me = 2 * x + y
        sibling = (x, y, 1 - c)

        def ici(k, slot):
            px, py = peers[k]
            return pltpu.make_async_remote_copy(
                src_ref=own_hbm.at[c], dst_ref=win_out.at[slot, c],
                send_sem=ici_send.at[k], recv_sem=ici_recv.at[k],
                device_id=(px, py, c), device_id_type=MESH)

        def forward(k, core):
            pk = 2 * peers[k][0] + peers[k][1]
            return pltpu.make_async_remote_copy(
                src_ref=win_out.at[pk, core], dst_ref=win_out.at[pk, core],
                send_sem=d2d_send.at[k], recv_sem=d2d_recv.at[k],
                device_id=sibling, device_id_type=MESH)

        place_own = pltpu.make_async_remote_copy(
            src_ref=own_hbm, dst_ref=win_out.at[me], send_sem=own_sems.at[0], recv_sem=own_sems.at[1],
            device_id=sibling, device_id_type=MESH)

        @pl.when((s == 0) & (i == 0))
        def _():
            for k in range(2):
                ici(k, me).start()
            place_own.start()

        @pl.when(s == 0)
        def _():
            xv = x_ref[...]
            h_ref[...] = (_dot(xv[:, :half], own_ref[0]) + _dot(xv[:, half:], own_ref[1])).astype(BF16)

        for k in range(3):
            @pl.when((s == k + 1) & (i == 0))
            def _(k=k):
                pk = 2 * peers[k][0] + peers[k][1]
                ici(k, pk).wait_recv()
                if k == 0:
                    ici(2, me).start()
                forward(k, c).start()
                forward(k, 1 - c).wait_recv()
                loads = [pltpu.make_async_copy(win_out.at[pk, hh], w_scr.at[hh], ld_sems.at[hh])
                         for hh in range(2)]
                for ld in loads:
                    ld.start()
                for ld in loads:
                    ld.wait()
                if k == 1:
                    comm.start(c_in, c_out, c_sem)

        @pl.when(s > 0)
        def _():
            xv = x_ref[...]
            h_ref[...] = (_dot(xv[:, :half], w_scr[0]) + _dot(xv[:, half:], w_scr[1])).astype(BF16)

        @pl.when((s == N_CHIP - 1) & (i == ni - 1))
        def _():
            place_own.wait()
            for k in range(3):
                ici(k, me).wait_send()
                forward(k, c).wait_send()
            comm.finish(c_in, c_out, c_sem)

    def shard_col(s, me):
        return jnp.where(s == 0, me, me ^ jnp.where(s == 1, 2, jnp.where(s == 2, 1, 3)))

    res = _pc(body, name="in_proj",
              out_shape=(jax.ShapeDtypeStruct((t, N_CHIP * cols), BF16),
                         jax.ShapeDtypeStruct((N_CHIP,) + own.shape, own.dtype)) + tuple(comm.out_shapes),
              grid=(N_CHIP, ni), nsp=1,
              in_specs=[pl.BlockSpec((tm, 2 * half), lambda s, i, chip_ref: (i, 0)),
                        pl.BlockSpec(own.shape, lambda s, i, chip_ref: (0, 0, 0)),
                        ANY] + [ANY] * nci,
              out_specs=(pl.BlockSpec((tm, cols), lambda s, i, chip_ref: (i, shard_col(s, chip_ref[0]))),
                         ANY) + tuple([ANY] * nco),
              scratch=[pltpu.VMEM(own.shape, own.dtype),
                       pltpu.SemaphoreType.DMA((3,)), pltpu.SemaphoreType.DMA((3,)),
                       pltpu.SemaphoreType.DMA((3,)), pltpu.SemaphoreType.DMA((3,)),
                       pltpu.SemaphoreType.DMA((2,)), pltpu.SemaphoreType.DMA((2,))] + comm.sems,
              sem=("arbitrary", "arbitrary"))(chip, x2b, own, own, *comm.ins)
    return res[0], res[1], res[2:]


def _rs_add_halves(name, grad, recv, core):
    _, r, cdim = grad.shape
    half = r // 2
    tr = _row_tile(half, cdim, mult=16)
    nr = half // tr

    def body(c_ref, g_ref, r_ref, o_ref):
        o_ref[...] = (g_ref[...].astype(F32) + r_ref[...].astype(F32)).astype(BF16)

    return _pc(body, name=name, out_shape=jax.ShapeDtypeStruct((N_CHIP, half, cdim), BF16),
               grid=(N_CHIP, nr), nsp=1,
               in_specs=[pl.BlockSpec((None, tr, cdim), lambda j, i, c_ref: (j, c_ref[0] * nr + i, 0)),
                         pl.BlockSpec((None, tr, cdim), lambda j, i, c_ref: (j, i, 0))],
               out_specs=pl.BlockSpec((None, tr, cdim), lambda j, i, c_ref: (j, i, 0)),
               sem=("parallel", "parallel"))(core, grad, recv)


def _chip_exchange_comm(parts):
    n = len(parts)

    def copies(ins, outs, sems):
        send_sems, recv_sems = sems
        x, y, c, peers = _chip_peers()
        return [pltpu.make_async_remote_copy(
            src_ref=ins[ti].at[2 * px + py], dst_ref=outs[ti].at[k],
            send_sem=send_sems.at[ti * 3 + k], recv_sem=recv_sems.at[ti * 3 + k],
            device_id=(px, py, c), device_id_type=MESH)
            for ti in range(n) for k, (px, py) in enumerate(peers)]

    def start(ins, outs, sems):
        for cp in copies(ins, outs, sems):
            cp.start()

    def finish(ins, outs, sems):
        for cp in copies(ins, outs, sems):
            cp.wait()

    return _Comm(parts, [jax.ShapeDtypeStruct((3,) + p.shape[1:], p.dtype) for p in parts],
                 [pltpu.SemaphoreType.DMA((3 * n,)), pltpu.SemaphoreType.DMA((3 * n,))], start, finish)


def _rs_sum_chips(name, part, recv, chip):
    _, half, cdim = recv.shape
    tr = _row_tile(half, cdim, mult=16)

    def body(chip_ref, p_ref, r_ref, o_ref):
        o_ref[...] = ((p_ref[...].astype(F32) + r_ref[0].astype(F32)) + r_ref[1].astype(F32)
                      ) + r_ref[2].astype(F32)

    return _pc(body, name=name, out_shape=jax.ShapeDtypeStruct((half, cdim), F32),
               grid=(half // tr,), nsp=1,
               in_specs=[pl.BlockSpec((None, tr, cdim), lambda i, chip_ref: (chip_ref[0], i, 0)),
                         pl.BlockSpec((3, tr, cdim), lambda i, chip_ref: (0, i, 0))],
               out_specs=pl.BlockSpec((tr, cdim), lambda i, chip_ref: (i, 0)),
               sem=("parallel",))(chip, part, recv)


def _rs_send_halves(halves):
    n = len(halves)

    def body(*refs):
        ins, outs = refs[:n], refs[n:2 * n]
        send_sems, recv_sems = refs[2 * n:]
        x, y, c = lax.axis_index("x"), lax.axis_index("y"), lax.axis_index("c")
        sends = []
        for ti in range(n):
            cp = pltpu.make_async_remote_copy(
                src_ref=ins[ti], dst_ref=outs[ti],
                send_sem=send_sems.at[ti], recv_sem=recv_sems.at[ti],
                device_id=(x, y, 1 - c), device_id_type=MESH)
            cp.start()
            sends.append(cp)
        for cp in sends:
            cp.wait()

    return _pc(body, name="rs_send_halves",
               out_shape=tuple(jax.ShapeDtypeStruct(hv.shape, hv.dtype) for hv in halves),
               in_specs=[ANY] * n, out_specs=tuple([ANY] * n),
               scratch=[pltpu.SemaphoreType.DMA((n,)), pltpu.SemaphoreType.DMA((n,))])(*halves)


def _adamw_rows(name, mine, theirs, w, m, v, core):
    half, cdim = mine.shape
    tr = _row_tile(half, cdim, budget=1 << 19)
    nrh = half // tr

    def body(c_ref, mine_ref, theirs_ref, w_ref, m_ref, v_ref, g_ref, d_ref, m2_ref, v2_ref):
        is_mine = (pl.program_id(0) // nrh) == c_ref[0]
        g = jnp.where(is_mine, mine_ref[...], theirs_ref[...])
        d, m2, v2 = _adamw(w_ref[...], g, m_ref[...], v_ref[...])
        g_ref[...] = g
        d_ref[...] = d
        m2_ref[...] = m2
        v2_ref[...] = v2

    htile = pl.BlockSpec((tr, cdim), lambda i, c_ref: (i % nrh, 0))
    tile = pl.BlockSpec((tr, cdim), lambda i, c_ref: (i, 0))
    shp = jax.ShapeDtypeStruct((2 * half, cdim), F32)
    return _pc(body, name=name, out_shape=(shp, shp, shp, shp), grid=(2 * nrh,), nsp=1,
               in_specs=[htile, htile, tile, tile, tile], out_specs=(tile, tile, tile, tile),
               sem=("parallel",))(core, mine, theirs, w, m, v)


def _adamw_whole(name, g, w, m, v):
    def body(g_ref, w_ref, m_ref, v_ref, d_ref, m2_ref, v2_ref):
        d, m2, v2 = _adamw(w_ref[...], g_ref[...], m_ref[...], v_ref[...])
        d_ref[...] = d
        m2_ref[...] = m2
        v2_ref[...] = v2

    shp = jax.ShapeDtypeStruct(g.shape, F32)
    return _pc(body, name=name, out_shape=(shp, shp, shp))(g, w, m, v)


SMALL_LAYOUT = (
    ("sgu_w_s", 1024, 1, 0),
    ("sgu_b_s", 8, 1, 1024),
    ("sgu_norm_g", 1, 0, 0),
    ("sgu_norm_b", 1, 0, 1),
    ("hgrn_norm_g", 1, 0, 3),
    ("ln1_g", 1, 0, 4),
    ("ln1_b", 1, 0, 5),
    ("ffn_conv_b", 1, 2, 3),
    ("ln2_g", 1, 0, 6),
    ("ln2_b", 1, 0, 7),
)
LB_ROW = 2
LOSS_ROW = 8
PACK_SHAPES = ((16, D_MODEL), (N_GROUP * 128 + 8, 128), (8, D_FF))


def _small_allreduce_adamw(rows1024, dws, dbs, dcw, dcb, logits, m_logits, v_logits,
                           small_w, small_m, small_v):
    ns = len(SMALL_LAYOUT)
    nr = len(rows1024)
    nb = len(PACK_SHAPES)

    def body(*refs):
        row_refs = refs[:nr]
        dws_ref, dbs_ref, dcw_ref, dcb_ref, lg_ref, mlg_ref, vlg_ref = refs[nr:nr + 7]
        pos = nr + 7
        w_refs = refs[pos:pos + ns]
        m_refs = refs[pos + ns:pos + 2 * ns]
        v_refs = refs[pos + 2 * ns:pos + 3 * ns]
        pos += 3 * ns
        loss_ref, dcw_out = refs[pos:pos + 2]
        lg_outs = refs[pos + 2:pos + 6]
        pos += 6
        outs = refs[pos:pos + 4 * ns]
        pos += 4 * ns
        pack = refs[pos:pos + nb]
        sib = refs[pos + nb:pos + 2 * nb]
        gath = refs[pos + 2 * nb:pos + 3 * nb]
        d2d_send, d2d_recv, ici_send, ici_recv = refs[pos + 3 * nb:]

        x, y, c, peers = _chip_peers()
        me = 2 * x + y
        sibling = (x, y, 1 - c)

        pack[0][...] = jnp.zeros(PACK_SHAPES[0], F32)
        for k in range(nr):
            pack[0][k:k + 1, :] = row_refs[k][0:1, :]
        pack[1][0:N_GROUP * 128, :] = dws_ref[...]
        pack[1][N_GROUP * 128:, :] = dbs_ref[...]
        pack[2][...] = jnp.zeros(PACK_SHAPES[2], F32)
        pack[2][0:3, :] = dcw_ref[0:3, :]
        pack[2][3:4, :] = dcb_ref[0:1, :]

        d2d = [pltpu.make_async_remote_copy(
            src_ref=pack[b], dst_ref=sib[b], send_sem=d2d_send.at[b], recv_sem=d2d_recv.at[b],
            device_id=sibling, device_id_type=MESH) for b in range(nb)]
        for cp in d2d:
            cp.start()
        for cp in d2d:
            cp.wait()
        for b in range(nb):
            gath[b][me] = pack[b][...] + sib[b][...]

        ici, ici_wait = [], []
        for b in range(nb):
            for k, (px, py) in enumerate(peers):
                sem = dict(send_sem=ici_send.at[b * 3 + k], recv_sem=ici_recv.at[b * 3 + k],
                           device_id=(px, py, c), device_id_type=MESH)
                ici.append(pltpu.make_async_remote_copy(src_ref=gath[b].at[me], dst_ref=gath[b].at[me], **sem))
                ici_wait.append(pltpu.make_async_remote_copy(
                    src_ref=gath[b].at[me], dst_ref=gath[b].at[2 * px + py], **sem))
        for cp in ici:
            cp.start()
        for cp in ici_wait:
            cp.wait_recv()
        for cp in ici:
            cp.wait_send()

        tot = pack
        for b in range(nb):
            tot[b][...] = ((gath[b][0] + gath[b][1]) + gath[b][2]) + gath[b][3]

        loss_ref[...] = tot[0][LOSS_ROW:LOSS_ROW + 1, :]
        dcw_out[...] = tot[2][...]
        lb = _sig(lg_ref[0:1, :] - lg_ref[1:2, :])
        d0 = tot[0][LB_ROW:LB_ROW + 1, :] * lb * (1.0 - lb)
        rowid = lax.broadcasted_iota(jnp.int32, (2, D_MODEL), 0)
        g_lg = jnp.where(rowid == 0, d0, -d0)
        dl, ml, vl = _adamw(lg_ref[...], g_lg, mlg_ref[...], vlg_ref[...])
        lg_outs[0][...] = g_lg
        lg_outs[1][...] = dl
        lg_outs[2][...] = ml
        lg_outs[3][...] = vl
        for si, (_, rows, b, r0) in enumerate(SMALL_LAYOUT):
            g = tot[b][r0:r0 + rows, :]
            dl, ml, vl = _adamw(w_refs[si][...], g, m_refs[si][...], v_refs[si][...])
            outs[4 * si][...] = g
            outs[4 * si + 1][...] = dl
            outs[4 * si + 2][...] = ml
            outs[4 * si + 3][...] = vl

    shapes = [jax.ShapeDtypeStruct((1, D_MODEL), F32), jax.ShapeDtypeStruct((8, D_FF), F32)]
    shapes += [jax.ShapeDtypeStruct((2, D_MODEL), F32)] * 4
    for w in small_w:
        shapes += [jax.ShapeDtypeStruct(w.shape, F32)] * 4
    scratch = [pltpu.VMEM(shp, F32) for shp in PACK_SHAPES]
    scratch += [pltpu.VMEM(shp, F32) for shp in PACK_SHAPES]
    scratch += [pltpu.VMEM((N_CHIP,) + shp, F32) for shp in PACK_SHAPES]
    scratch += [pltpu.SemaphoreType.DMA((nb,)), pltpu.SemaphoreType.DMA((nb,)),
                pltpu.SemaphoreType.DMA((3 * nb,)), pltpu.SemaphoreType.DMA((3 * nb,))]
    vm = pl.BlockSpec(memory_space=pltpu.VMEM)
    n_in = nr + 7 + 3 * ns
    res = _pc(body, name="small_allreduce_adamw", out_shape=tuple(shapes),
              in_specs=[vm] * n_in, out_specs=tuple([vm] * len(shapes)),
              scratch=scratch)(*rows1024, dws, dbs, dcw, dcb, logits, m_logits, v_logits,
                               *small_w, *small_m, *small_v)
    return res[0], res[1], res[2:6], res[6:]


def kernel(x, p, w_in, sgu_w_s, sgu_b_s, sgu_norm_g, sgu_norm_b, hgrn_lb_logits, hgrn_norm_g, w_branch, w_out, ln1_g, ln1_b, ffn_w_up, ffn_conv_w, ffn_conv_b, ffn_w_down, ln2_g, ln2_b, ple_w_proj, ple_w_gate, loss_target, m_w_in, m_sgu_w_s, m_sgu_b_s, m_sgu_norm_g, m_sgu_norm_b, m_hgrn_lb_logits, m_hgrn_norm_g, m_w_branch, m_w_out, m_ln1_g, m_ln1_b, m_ffn_w_up, m_ffn_conv_w, m_ffn_conv_b, m_ffn_w_down, m_ln2_g, m_ln2_b, m_ple_w_proj, m_ple_w_gate, v_w_in, v_sgu_w_s, v_sgu_b_s, v_sgu_norm_g, v_sgu_norm_b, v_hgrn_lb_logits, v_hgrn_norm_g, v_w_branch, v_w_out, v_ln1_g, v_ln1_b, v_ffn_w_up, v_ffn_conv_w, v_ffn_conv_b, v_ffn_w_down, v_ln2_g, v_ln2_b, v_ple_w_proj, v_ple_w_gate):
    t = x.shape[1]
    x2 = x.reshape(t, D_MODEL)
    x2b = x2.astype(BF16)
    p2 = p.reshape(t, PLE_DIM)
    tgt = loss_target.reshape(t, D_MODEL)
    core = lax.axis_index("c").astype(jnp.int32).reshape(1)
    chip_id = (2 * lax.axis_index("x") + lax.axis_index("y")).astype(jnp.int32).reshape(1)

    big_w = [w_in[0], w_branch[0, 0], w_branch[0, 1], w_out[0], ffn_w_up[0], ffn_w_down[0],
             ple_w_proj[0], ple_w_gate[0]]
    big_m = [m_w_in[0], m_w_branch[0, 0], m_w_branch[0, 1], m_w_out[0], m_ffn_w_up[0],
             m_ffn_w_down[0], m_ple_w_proj[0], m_ple_w_gate[0]]
    big_v = [v_w_in[0], v_w_branch[0, 0], v_w_branch[0, 1], v_w_out[0], v_ffn_w_up[0],
             v_ffn_w_down[0], v_ple_w_proj[0], v_ple_w_gate[0]]
    def halves_of(i):
        w = big_w[i]
        return w.astype(BF16).reshape(2, w.shape[0] // 2, w.shape[1])

    def stacked(g, i):
        return g.reshape(N_CHIP, big_w[i].shape[0], big_w[i].shape[1])


    cid = jnp.arange(SGU_BLOCK) // CHUNK
    maskf = (cid[:, None] >= cid[None, :]).astype(F32)
    ws_masked = sgu_w_s[0] * maskf[None]
    wm = ws_masked.astype(BF16)
    wmt = jnp.transpose(ws_masked, (0, 2, 1)).astype(BF16)
    bsb = jnp.broadcast_to(sgu_b_s[0][:, :, None], (N_GROUP, SGU_BLOCK, 128))

    up_rows = big_w[4].shape[0] // 2
    up_blocks = [big_w[4][k * up_rows:(k + 1) * up_rows].astype(BF16).reshape(2, up_rows // 2, -1)
                 for k in range(2)]
    h, win_g, (up0_g,) = _in_proj_gathering(x2b, halves_of(0), chip_id, 512, _gather_comm([up_blocks[0]]))
    win_st = stacked(win_g, 0)
    ya, _ = _sgu_fwd(h, wm, bsb, sgu_norm_g, sgu_norm_b)
    (yb, st_all), mix_g = _hgrn_fwd(
        h, hgrn_lb_logits, hgrn_norm_g,
        comm=_gather_comm([halves_of(i) for i in (1, 2, 3)] + [up_blocks[1]], [ffn_conv_w[0]]))
    wb0, wb1, wo = [stacked(g, i).reshape(D_MODEL, D_MODEL) for g, i in zip(mix_g[:3], (1, 2, 3))]
    wup_st = jnp.concatenate([g.reshape(N_CHIP, up_rows, -1) for g in (up0_g, mix_g[3])], axis=1)
    convw = jnp.transpose(mix_g[4], (1, 0, 2)).reshape(3, D_FF)
    (r1, a_br, b_br, m_bf, x1b), _ = _mix_fwd(ya, yb, h, x2, wb0, wb1, wo, ln1_g, ln1_b, 256)
    h2, act, out_g = _ffn_up_act(x1b, wup_st, convw, ffn_conv_b, 512,
                                 _gather_comm([halves_of(i) for i in (5, 6, 7)]))
    wd = stacked(out_g[0], 5).reshape(D_FF, D_MODEL)
    wpp = jnp.transpose(stacked(out_g[1], 6), (1, 0, 2)).reshape(PLE_DIM, D_MODEL)
    wpg = stacked(out_g[2], 7).reshape(D_MODEL, D_MODEL)
    dr2, dpg, dpp, loss_acc, dg2, db2 = _out_fwd_bwd(
        act, x1b, r1, p2, tgt, wd, wpg, wpp, ln1_g, ln1_b, ln2_g, ln2_b, 256)

    dh2, dr1, dcw, dcb, dg1, db1 = _ffn_bwd(h2, dr2, wd, wup_st, dpg, wpg, r1, ln1_g, convw, ffn_conv_b, 256)
    d_wd = _mm_tn("ffn_down_wgrad", act, dr2, FF_TILE, 512)
    d_wpg = _mm_tn("ple_gate_wgrad", x1b, dpg, 512, D_MODEL)
    d_wpp_st = _mm_tn("ple_proj_wgrad", p2, dpp, PLE_DIM, PLE_DIM, stacked=True)
    d_wup_st = _mm("ffn_up_wgrad", x1b, dh2, TN, (2, N_CHIP),
                   pl.BlockSpec((t, 512), lambda i, j: (0, i)),
                   pl.BlockSpec((None, t, FF_TILE), lambda i, j: (j // FF_NJ, 0, j % FF_NJ)),
                   jax.ShapeDtypeStruct((N_CHIP, D_MODEL, FF_TILE), BF16),
                   pl.BlockSpec((None, 512, FF_TILE), lambda i, j: (j, i, 0)))
    da_bf, db_bf, dh3, dya, dyb = _mix_bwd(dr1, h, a_br, b_br, wo, wb0, wb1, 256)
    d_wo = _mm_tn("out_proj_wgrad", m_bf, dr1, 512, 512)
    d_wb0 = _mm_tn("branch0_wgrad", ya, da_bf, 512, D_MODEL)
    d_wb1 = _mm_tn("branch1_wgrad", yb, db_bf, 512, D_MODEL)
    grads_1 = [d_wb0.reshape(4, 256, D_MODEL), d_wb1.reshape(4, 256, D_MODEL),
               d_wo.reshape(4, 256, D_MODEL), d_wup_st, d_wd.reshape(4, D_FF // 4, D_MODEL),
               d_wpp_st, d_wpg.reshape(4, 256, D_MODEL)]
    (dh0, dws, dbs, dgv, dbv), recv_a1 = _sgu_bwd(h, dya, wm, wmt, bsb, sgu_norm_g, sgu_norm_b, maskf,
                                                  comm=_sibling_exchange_comm(grads_1))
    parts_1 = [_rs_add_halves("rs_add_halves%d" % (i + 1), g, r, core)
               for i, (g, r) in enumerate(zip(grads_1, recv_a1))]
    (dh1, dh2h, dlb, dgn), recv_b1 = _hgrn_bwd(h, dyb, st_all, hgrn_lb_logits, hgrn_norm_g,
                                                comm=_chip_exchange_comm(parts_1))
    dh_parts = [dh0, dh1, dh2h, dh3]
    d_win = [_mm_tn("in_proj_wgrad%d" % j, x2b, dh_parts[j], 512, D_MODEL) for j in range(4)]

    grads_0 = [jnp.stack(d_win)]
    recv_a0 = _run_comm("rs_sibling_exchange0", _sibling_exchange_comm(grads_0))
    parts_0 = [_rs_add_halves("rs_add_halves0", grads_0[0], recv_a0[0], core)]
    gx, recv_b0 = _in_proj_xgrad(dh_parts, win_st, dr1, 512, comm=_chip_exchange_comm(parts_0))
    parts = parts_0 + parts_1
    recv_b = list(recv_b0) + list(recv_b1)
    halves = [_rs_sum_chips("rs_sum_chips%d" % i, pt, r, chip_id)
              for i, (pt, r) in enumerate(zip(parts, recv_b))]
    theirs = _rs_send_halves(halves)
    big_out = [_adamw_rows("adamw_big%d" % i, halves[i], theirs[i], big_w[i], big_m[i], big_v[i], core)
               for i in range(len(halves))]

    small_in = dict(sgu_w_s=(sgu_w_s, m_sgu_w_s, v_sgu_w_s), sgu_b_s=(sgu_b_s, m_sgu_b_s, v_sgu_b_s),
                    sgu_norm_g=(sgu_norm_g, m_sgu_norm_g, v_sgu_norm_g),
                    sgu_norm_b=(sgu_norm_b, m_sgu_norm_b, v_sgu_norm_b),
                    hgrn_norm_g=(hgrn_norm_g, m_hgrn_norm_g, v_hgrn_norm_g),
                    ln1_g=(ln1_g, m_ln1_g, v_ln1_g), ln1_b=(ln1_b, m_ln1_b, v_ln1_b),
                    ffn_conv_b=(ffn_conv_b, m_ffn_conv_b, v_ffn_conv_b),
                    ln2_g=(ln2_g, m_ln2_g, v_ln2_g), ln2_b=(ln2_b, m_ln2_b, v_ln2_b))

    def flat(name, arr):
        rows = dict((n, r) for n, r, _, _ in SMALL_LAYOUT)[name]
        return arr.reshape(rows, arr.size // rows)

    names = [n for n, _, _, _ in SMALL_LAYOUT]
    sw = [flat(n, small_in[n][0]) for n in names]
    sm = [flat(n, small_in[n][1]) for n in names]
    sv = [flat(n, small_in[n][2]) for n in names]
    loss_rows, dcw_tot, lg_out, small_out = _small_allreduce_adamw(
        [dgv, dbv, dlb, dgn, dg1, db1, dg2, db2, loss_acc], dws.reshape(N_GROUP * 128, 128), dbs, dcw, dcb,
        hgrn_lb_logits, m_hgrn_lb_logits, v_hgrn_lb_logits, sw, sm, sv)
    loss = loss_rows[0, 0]

    chip = 2 * lax.axis_index("x") + lax.axis_index("y")
    g_cw = lax.dynamic_slice(dcw_tot, (0, chip * (D_FF // 4)), (3, D_FF // 4))
    cw_out = _adamw_whole("adamw_conv_w", g_cw, ffn_conv_w[0], m_ffn_conv_w[0], v_ffn_conv_w[0])

    res = {}
    for si, n in enumerate(names):
        shp = small_in[n][0].shape
        res[n] = tuple(small_out[4 * si + k].reshape(shp) for k in range(4))
    res["hgrn_lb_logits"] = tuple(lg_out)
    res["ffn_conv_w"] = (g_cw[None],) + tuple(o[None] for o in cw_out)

    def big(i):
        return tuple(big_out[i])

    res["w_in"] = tuple(o[None] for o in big(0))
    res["w_branch"] = tuple(jnp.stack([o0, o1])[None] for o0, o1 in zip(big(1), big(2)))
    res["w_out"] = tuple(o[None] for o in big(3))
    res["ffn_w_up"] = tuple(o[None] for o in big(4))
    res["ffn_w_down"] = tuple(o[None] for o in big(5))
    res["ple_w_proj"] = tuple(o[None] for o in big(6))
    res["ple_w_gate"] = tuple(o[None] for o in big(7))

    order = ["w_in", "sgu_w_s", "sgu_b_s", "sgu_norm_g", "sgu_norm_b", "hgrn_lb_logits",
             "hgrn_norm_g", "w_branch", "w_out", "ln1_g", "ln1_b", "ffn_w_up", "ffn_conv_w",
             "ffn_conv_b", "ffn_w_down", "ln2_g", "ln2_b", "ple_w_proj", "ple_w_gate"]
    outs = [loss, gx.reshape(1, t, D_MODEL)]
    for k in range(4):
        outs += [res[n][k] for n in order]
    return tuple(outs)
```

```python
import functools

import jax
import jax.numpy as jnp
from jax import lax
from jax.experimental import pallas as pl
from jax.experimental.pallas import tpu as pltpu

F32 = jnp.float32
BF16 = jnp.bfloat16
HIGHEST = lax.Precision.HIGHEST
MESH = pl.DeviceIdType.MESH

D_MODEL = 1024
CHUNK = 64
SGU_BLOCK = 128
N_GROUP = 8
N_HEAD = 8
HEAD_DIM = 128
D_FF = 2816
PLE_DIM = 256
IN_COLS = 8192
LN_EPS = 1e-5
RMS_EPS = 1e-6
ALPHA = 2.0 ** 0.25
N_CHIP = 4
N_DEV = 8

ADAM_LR = 0.001
ADAM_B1 = 0.9
ADAM_B2 = 0.999
ADAM_EPS = 1e-08
ADAM_WD = 0.01
ADAM_STEP = 10

VMEM_LIMIT = 56 * 1024 * 1024

NN = (((1,), (0,)), ((), ()))
NT = (((1,), (1,)), ((), ()))
TN = (((0,), (0,)), ((), ()))


def _pc(body, *, name, out_shape, grid=None, in_specs=None, out_specs=None, scratch=(),
        sem=None, nsp=0, vmem=VMEM_LIMIT, aliases=None):
    params = dict(vmem_limit_bytes=vmem)
    if sem is not None:
        params["dimension_semantics"] = sem
    kw = dict(name=name, out_shape=out_shape, compiler_params=pltpu.CompilerParams(**params))
    if aliases:
        kw["input_output_aliases"] = aliases
    if nsp:
        kw["grid_spec"] = pltpu.PrefetchScalarGridSpec(
            num_scalar_prefetch=nsp, grid=grid, in_specs=in_specs, out_specs=out_specs,
            scratch_shapes=list(scratch))
    else:
        if grid is not None:
            kw["grid"] = grid
        if in_specs is not None:
            kw["in_specs"] = in_specs
            kw["out_specs"] = out_specs
        kw["scratch_shapes"] = list(scratch)
    return pl.pallas_call(body, **kw)


def _dot(a, b, dims=NN):
    return lax.dot_general(a.astype(BF16), b.astype(BF16), dims, preferred_element_type=F32)


def _dot32(a, b, dims=NN):
    return lax.dot_general(a, b, dims, precision=HIGHEST, preferred_element_type=F32)


def _sig(x):
    return 1.0 / (1.0 + jnp.exp(-x))


_GC = 0.7978845608028654
_GA = 0.044715


def _gelu(x):
    return 0.5 * x * (1.0 + jnp.tanh(_GC * (x + _GA * x * x * x)))


def _gelu_and_grad(x):
    t = jnp.tanh(_GC * (x + _GA * x * x * x))
    g = 0.5 * x * (1.0 + t)
    dg = 0.5 * (1.0 + t) + 0.5 * x * (1.0 - t * t) * _GC * (1.0 + 3.0 * _GA * x * x)
    return g, dg


def _ln_stats(r):
    mu = jnp.mean(r, axis=-1, keepdims=True)
    xc = r - mu
    var = jnp.mean(xc * xc, axis=-1, keepdims=True)
    rstd = lax.rsqrt(var + LN_EPS)
    return xc * rstd, rstd


def _ln_bwd(dxh, xh, rstd):
    m1 = jnp.mean(dxh, axis=-1, keepdims=True)
    m2 = jnp.mean(dxh * xh, axis=-1, keepdims=True)
    return rstd * (dxh - m1 - xh * m2)


def _colsum8(v):
    return jnp.broadcast_to(jnp.sum(v, axis=0, keepdims=True), (8, v.shape[1]))


def _adamw(w, g, m, v):
    m2 = ADAM_B1 * m + (1.0 - ADAM_B1) * g
    v2 = ADAM_B2 * v + (1.0 - ADAM_B2) * (g * g)
    m_hat = m2 / (1.0 - ADAM_B1 ** ADAM_STEP)
    v_hat = v2 / (1.0 - ADAM_B2 ** ADAM_STEP)
    delta = -ADAM_LR * (m_hat / (jnp.sqrt(v_hat) + ADAM_EPS) + ADAM_WD * w)
    return delta, m2, v2


def _row_tile(rows, cols, itemsize=4, budget=1 << 20, mult=8):
    best = mult
    for tr in range(mult, rows + 1, mult):
        if rows % tr == 0 and tr * cols * itemsize <= budget:
            best = tr
    return best


def _mm(name, a, b, dims, grid, a_spec, b_spec, out_shape, o_spec):
    out_dtype = out_shape.dtype

    def body(a_ref, b_ref, o_ref):
        o_ref[...] = _dot(a_ref[...], b_ref[...], dims).astype(out_dtype)

    return _pc(body, name=name, out_shape=out_shape, grid=grid, in_specs=[a_spec, b_spec],
               out_specs=o_spec, sem=("parallel", "parallel"))(a, b)


class _Comm:
    def __init__(self, ins, out_shapes, sems, start, finish):
        self.ins, self.out_shapes, self.sems = list(ins), list(out_shapes), list(sems)
        self.start, self.finish = start, finish


def _hosted_call(body, comm, first, last, *, name, out_shape, grid, in_specs, out_specs, scratch, sem,
                 args, aliases=None):
    n_in, n_out, n_scr = len(in_specs), len(out_shape), len(scratch)
    nci, nco = len(comm.ins), len(comm.out_shapes)

    def wrapped(*refs):
        pos = n_in
        own_in, c_in = refs[:pos], refs[pos:pos + nci]
        pos += nci
        own_out, c_out = refs[pos:pos + n_out], refs[pos + n_out:pos + n_out + nco]
        pos += n_out + nco
        own_scr, c_sem = refs[pos:pos + n_scr], refs[pos + n_scr:]

        @pl.when(first())
        def _():
            comm.start(c_in, c_out, c_sem)

        body(*own_in, *own_out, *own_scr)

        @pl.when(last())
        def _():
            comm.finish(c_in, c_out, c_sem)

    return _pc(wrapped, name=name, out_shape=tuple(out_shape) + tuple(comm.out_shapes), grid=grid,
               in_specs=list(in_specs) + [ANY] * nci, out_specs=tuple(out_specs) + tuple([ANY] * nco),
               scratch=list(scratch) + comm.sems, sem=sem, aliases=aliases)(*args, *comm.ins)


def _grid1_call(body, comm, n, *, name, out_shape, in_specs, out_specs, scratch, args, aliases=None):
    if comm is None:
        return _pc(body, name=name, out_shape=out_shape, grid=(n,), in_specs=in_specs, out_specs=out_specs,
                   scratch=scratch, sem=("arbitrary",), aliases=aliases)(*args), ()
    res = _hosted_call(body, comm, lambda: pl.program_id(0) == 0, lambda: pl.program_id(0) == n - 1,
                       name=name, out_shape=out_shape, grid=(n,), in_specs=in_specs,
                       out_specs=out_specs, scratch=scratch, sem=("arbitrary",), args=args, aliases=aliases)
    return res[:len(out_shape)], res[len(out_shape):]


def _run_comm(name, comm):
    nci, nco = len(comm.ins), len(comm.out_shapes)

    def body(*refs):
        c_in, c_out, c_sem = refs[:nci], refs[nci:nci + nco], refs[nci + nco:]
        comm.start(c_in, c_out, c_sem)
        comm.finish(c_in, c_out, c_sem)

    return _pc(body, name=name, out_shape=tuple(comm.out_shapes), in_specs=[ANY] * nci,
               out_specs=tuple([ANY] * nco), scratch=comm.sems)(*comm.ins)


def _mm_tn(name, a, b, tm, tn, stacked=False):
    t, m = a.shape
    _, n = b.shape
    if stacked:
        assert tm == m
        out_shape = jax.ShapeDtypeStruct((n // tn, m, tn), BF16)
        o_spec = pl.BlockSpec((None, tm, tn), lambda i, j: (j, 0, 0))
    else:
        out_shape = jax.ShapeDtypeStruct((m, n), BF16)
        o_spec = pl.BlockSpec((tm, tn), lambda i, j: (i, j))
    return _mm(name, a, b, TN, (m // tm, n // tn),
               pl.BlockSpec((t, tm), lambda i, j: (0, i)),
               pl.BlockSpec((t, tn), lambda i, j: (0, j)),
               out_shape, o_spec)


DH_SLOT = (2, 0, 1, 3)


def _dh_slot(j):
    return jnp.where(j == 3, 3, (j + 2) % 3)


def _in_proj_wgrad(x2b, dh, tm, tn):
    t, m = x2b.shape
    n = dh.shape[2]

    def body(a_ref, b_ref, o_ref):
        o_ref[...] = _dot(a_ref[...], b_ref[...], TN).astype(BF16)

    return _pc(body, name="in_proj_wgrad", out_shape=jax.ShapeDtypeStruct((N_CHIP, m, n), BF16),
               grid=(N_CHIP, m // tm, n // tn),
               in_specs=[pl.BlockSpec((t, tm), lambda j, i, k: (0, i)),
                         pl.BlockSpec((None, t, tn), lambda j, i, k: (_dh_slot(j), 0, k))],
               out_specs=pl.BlockSpec((None, tm, tn), lambda j, i, k: (j, i, k)),
               sem=("parallel", "parallel", "parallel"))(x2b, dh)


def _in_proj_xgrad(dh, win_st, dr1, tm, comm):
    t = dr1.shape[0]
    ni = t // tm

    def body(a_ref, b_ref, add_ref, o_ref, acc):
        j = pl.program_id(1)
        prod = _dot(a_ref[...], b_ref[...], NT)

        @pl.when(j == 0)
        def _():
            acc[...] = prod + ALPHA * add_ref[...]

        @pl.when((j > 0) & (j < N_CHIP - 1))
        def _():
            acc[...] += prod

        @pl.when(j == N_CHIP - 1)
        def _():
            o_ref[...] = acc[...] + prod

    tile = pl.BlockSpec((tm, D_MODEL), lambda i, j: (i, 0))
    res = _hosted_call(body, comm,
                       lambda: (pl.program_id(0) == 0) & (pl.program_id(1) == 0),
                       lambda: (pl.program_id(0) == ni - 1) & (pl.program_id(1) == N_CHIP - 1),
                       name="in_proj_xgrad", out_shape=(jax.ShapeDtypeStruct((t, D_MODEL), F32),),
                       grid=(ni, N_CHIP),
                       in_specs=[pl.BlockSpec((None, tm, 2 * D_MODEL), lambda i, j: (_dh_slot(j), i, 0)),
                                 pl.BlockSpec((None, D_MODEL, 2 * D_MODEL), lambda i, j: (j, 0, 0)),
                                 tile],
                       out_specs=(tile,), scratch=[pltpu.VMEM((tm, D_MODEL), F32)],
                       sem=("arbitrary", "arbitrary"), args=[dh, win_st, dr1])
    return res[0], res[1:]


def _sgu_mixed(v, wm_ref, bsb_ref, gv, bv):
    gl, dgl = _gelu_and_grad(v)
    vh, rstd = _ln_stats(gl)
    vn = vh * gv + bv
    mixed = []
    for g in range(N_GROUP):
        sl = slice(g * 128, (g + 1) * 128)
        mixed.append(_dot(wm_ref[g], vn[:, sl]) + bsb_ref[g])
    return dgl, vh, rstd, vn, mixed


def _sgu_fwd(h, wm, bsb, gv, bv, comm=None):
    t = h.shape[0]

    def body(u_ref, v_ref, wm_ref, bsb_ref, gv_ref, bv_ref, ya_ref):
        u = u_ref[...].astype(F32)
        _, _, _, _, mixed = _sgu_mixed(v_ref[...].astype(F32), wm_ref, bsb_ref, gv_ref[...], bv_ref[...])
        gu = _gelu(u)
        for g in range(N_GROUP):
            sl = slice(g * 128, (g + 1) * 128)
            ya_ref[:, sl] = (gu[:, sl] * mixed[g]).astype(BF16)

    full3 = pl.BlockSpec((N_GROUP, 128, 128), lambda i: (0, 0, 0))
    vec = pl.BlockSpec((1, D_MODEL), lambda i: (0, 0))
    (ya,), extra = _grid1_call(
        body, comm, t // SGU_BLOCK, name="sgu_fwd",
        out_shape=(jax.ShapeDtypeStruct((t, D_MODEL), BF16),),
        in_specs=[pl.BlockSpec((SGU_BLOCK, D_MODEL), lambda i: (i, 0)),
                  pl.BlockSpec((SGU_BLOCK, D_MODEL), lambda i: (i, 1)),
                  full3, full3, vec, vec],
        out_specs=(pl.BlockSpec((SGU_BLOCK, D_MODEL), lambda i: (i, 0)),),
        scratch=[], args=(h, h, wm, bsb, gv, bv))
    return ya, extra


def _sgu_bwd(h, dya, wm, wmt, bsb, gv, bv, maskf, dh_buf, comm=None):
    t = h.shape[0]
    nb = t // SGU_BLOCK

    def body(u_ref, v_ref, dya_ref, wm_ref, wmt_ref, bsb_ref, gv_ref, bv_ref, mask_ref, dh_buf_ref,
             dh_ref, dws_ref, dbs_ref, dgv_ref, dbv_ref, dmix_acc):
        i = pl.program_id(0)

        @pl.when(i == 0)
        def _():
            dws_ref[...] = jnp.zeros_like(dws_ref)
            dgv_ref[...] = jnp.zeros_like(dgv_ref)
            dbv_ref[...] = jnp.zeros_like(dbv_ref)
            dmix_acc[...] = jnp.zeros_like(dmix_acc)

        u = u_ref[...].astype(F32)
        gvv = gv_ref[...]
        dgl_v, vh, rstd, vn, mixed = _sgu_mixed(v_ref[...].astype(F32), wm_ref, bsb_ref, gvv, bv_ref[...])
        gu, dgl_u = _gelu_and_grad(u)
        dya_v = dya_ref[...].astype(F32)
        dvn_parts = []
        for g in range(N_GROUP):
            sl = slice(g * 128, (g + 1) * 128)
            d_y = dya_v[:, sl]
            dh_ref[:, sl] = (d_y * mixed[g] * dgl_u[:, sl]).astype(BF16)
            d_mixed = d_y * gu[:, sl]
            dmix_acc[g] += d_mixed
            dws_ref[g] += _dot(d_mixed, vn[:, sl], NT) * mask_ref[...]
            dvn_parts.append(_dot(wmt_ref[g], d_mixed))
        dvn = jnp.concatenate(dvn_parts, axis=1)
        dgv_ref[...] += _colsum8(dvn * vh)
        dbv_ref[...] += _colsum8(dvn)
        d_gl = _ln_bwd(dvn * gvv, vh, rstd)
        dh_ref[:, D_MODEL:] = (d_gl * dgl_v).astype(BF16)

        @pl.when(i == nb - 1)
        def _():
            rowid = lax.broadcasted_iota(jnp.int32, (8, 128), 0)
            ones = jnp.ones((8, 128), F32)
            acc = jnp.zeros((8, 128), F32)
            for g in range(N_GROUP):
                rs = _dot32(ones, dmix_acc[g], NT)
                acc = jnp.where(rowid == g, rs, acc)
            dbs_ref[...] = acc

    full3 = pl.BlockSpec((N_GROUP, 128, 128), lambda i: (0, 0, 0))
    vec = pl.BlockSpec((1, D_MODEL), lambda i: (0, 0))
    acc8 = pl.BlockSpec((8, D_MODEL), lambda i: (0, 0))
    return _grid1_call(
        body, comm, nb, name="sgu_bwd",
        out_shape=(jax.ShapeDtypeStruct(dh_buf.shape, BF16),
                   jax.ShapeDtypeStruct((N_GROUP, 128, 128), F32),
                   jax.ShapeDtypeStruct((8, 128), F32),
                   jax.ShapeDtypeStruct((8, D_MODEL), F32),
                   jax.ShapeDtypeStruct((8, D_MODEL), F32)),
        in_specs=[pl.BlockSpec((SGU_BLOCK, D_MODEL), lambda i: (i, 0)),
                  pl.BlockSpec((SGU_BLOCK, D_MODEL), lambda i: (i, 1)),
                  pl.BlockSpec((SGU_BLOCK, D_MODEL), lambda i: (i, 0)),
                  full3, full3, full3, vec, vec,
                  pl.BlockSpec((128, 128), lambda i: (0, 0)), ANY],
        out_specs=(pl.BlockSpec((None, SGU_BLOCK, 2 * D_MODEL), lambda i: (DH_SLOT[0], i, 0)),
                   full3, pl.BlockSpec((8, 128), lambda i: (0, 0)), acc8, acc8),
        scratch=[pltpu.VMEM((N_GROUP, 128, 128), F32)],
        args=(h, h, dya, wm, wmt, bsb, gv, bv, maskf, dh_buf), aliases={9: 0})


def _tri_masks():
    row = lax.broadcasted_iota(jnp.int32, (CHUNK, CHUNK), 0)
    col = lax.broadcasted_iota(jnp.int32, (CHUNK, CHUNK), 1)
    return col <= row, col >= row


def _heads(v):
    return [v[:, hd * HEAD_DIM:(hd + 1) * HEAD_DIM] for hd in range(N_HEAD)]


def _tri_cumsum(tri_bf, v):
    hi = v.astype(BF16)
    r = v - hi.astype(F32)
    mid = r.astype(BF16)
    lo = (r - mid.astype(F32)).astype(BF16)
    return _dot(tri_bf, hi) + _dot(tri_bf, mid) + _dot(tri_bf, lo)


def _hgrn_chunk(q, fp, ii, lb, st_heads, causal):
    sg = _sig(fp)
    f = lb + (1.0 - lb) * sg
    k = 1.0 - f
    c = _tri_cumsum(causal.astype(BF16), jnp.log(f))
    ec = jnp.exp(c)
    en = jnp.exp(-c)
    sq = _sig(q)
    qt = q * sq * ec
    kt = k * en
    ecl = jnp.exp(c[CHUNK - 1:CHUNK, :])
    kk = kt * ecl
    qtb, ktb, iib, kkb = qt.astype(BF16), kt.astype(BF16), ii.astype(BF16), kk.astype(BF16)
    attn, o = [], []
    for hd, (qh, kh, ih) in enumerate(zip(_heads(qtb), _heads(ktb), _heads(iib))):
        a = jnp.where(causal, _dot(qh, kh, NT), 0.0).astype(BF16)
        attn.append(a)
        o.append(_dot(a, ih) + _dot(qh, st_heads[hd], NT))
    return dict(sg=sg, f=f, k=k, ec=ec, en=en, sq=sq, ecl=ecl, kk=kk, qtb=qtb, ktb=ktb, iib=iib,
                kkb=kkb, attn=attn, o=o)


def _rms_heads(o_heads):
    rinv = [lax.rsqrt(jnp.mean(o * o, axis=-1, keepdims=True) + RMS_EPS) for o in o_heads]
    return rinv, jnp.concatenate([o * r for o, r in zip(o_heads, rinv)], axis=1)


HG_CHUNKS = 4
HG_ROWS = HG_CHUNKS * CHUNK


def _hgrn_fwd(h, logits, gn, comm=None):
    t = h.shape[0]
    nb = t // HG_ROWS

    def body(q_ref, f_ref, i_ref, og_ref, lg_ref, gn_ref, yb_ref, st_ref, state):
        @pl.when(pl.program_id(0) == 0)
        def _():
            state[...] = jnp.zeros_like(state)

        causal, _ = _tri_masks()
        lb = _sig(lg_ref[0:1, :] - lg_ref[1:2, :])
        gnv = gn_ref[...]
        st = [state[hd] for hd in range(N_HEAD)]
        for cc in range(HG_CHUNKS):
            rows = slice(cc * CHUNK, (cc + 1) * CHUNK)
            og = og_ref[rows, :].astype(F32)
            r = _hgrn_chunk(q_ref[rows, :].astype(F32), f_ref[rows, :].astype(F32),
                            i_ref[rows, :].astype(F32), lb, [s.astype(BF16) for s in st], causal)
            _, on = _rms_heads(r["o"])
            yb_ref[rows, :] = (on * gnv * (og * _sig(og))).astype(BF16)
            for hd in range(N_HEAD):
                st_ref[cc, hd] = st[hd]
            st = [s * e + _dot(ih, kh, TN)
                  for s, e, ih, kh in zip(st, _heads(r["ecl"]), _heads(r["iib"]), _heads(r["kkb"]))]
        for hd in range(N_HEAD):
            state[hd] = st[hd]

    def col(k):
        return pl.BlockSpec((HG_ROWS, D_MODEL), lambda ci: (ci, k))

    return _grid1_call(body, comm, nb, name="hgrn_fwd",
                       out_shape=(jax.ShapeDtypeStruct((t, D_MODEL), BF16),
                                  jax.ShapeDtypeStruct((t // CHUNK, N_HEAD, HEAD_DIM, HEAD_DIM), F32)),
                       in_specs=[col(2), col(3), col(4), col(5),
                                 pl.BlockSpec((2, D_MODEL), lambda ci: (0, 0)),
                                 pl.BlockSpec((1, D_MODEL), lambda ci: (0, 0))],
                       out_specs=(pl.BlockSpec((HG_ROWS, D_MODEL), lambda ci: (ci, 0)),
                                  pl.BlockSpec((HG_CHUNKS, N_HEAD, HEAD_DIM, HEAD_DIM),
                                               lambda ci: (ci, 0, 0, 0))),
                       scratch=[pltpu.VMEM((N_HEAD, HEAD_DIM, HEAD_DIM), F32)],
                       args=(h, h, h, h, logits, gn))


def _hgrn_chunk_bwd(q, fp, ii, og, dy, gnv, lb, st, dsn, causal, anti):
    stb = [s.astype(BF16) for s in st]
    dsnb = [s.astype(BF16) for s in dsn]
    r = _hgrn_chunk(q, fp, ii, lb, stb, causal)
    rinv, on = _rms_heads(r["o"])
    so = _sig(og)
    sil = og * so
    d_og = dy * on * gnv * (so * (1.0 + og * (1.0 - so)))
    d_on = dy * gnv * sil
    d_ob = jnp.concatenate(
        [ri * (dn - oh * jnp.mean(dn * oh, axis=-1, keepdims=True))
         for ri, dn, oh in zip(rinv, _heads(d_on), _heads(on))], axis=1).astype(BF16)
    d_i, d_qt, d_kt, d_kk, d_st, st_dsn = [], [], [], [], [], []
    ecl = _heads(r["ecl"])
    for hd, (dh, qh, kh, ih, kkh) in enumerate(zip(_heads(d_ob), _heads(r["qtb"]), _heads(r["ktb"]),
                                                   _heads(r["iib"]), _heads(r["kkb"]))):
        d_attn = jnp.where(causal, _dot(dh, ih, NT), 0.0).astype(BF16)
        d_i.append(_dot(r["attn"][hd], dh, TN) + _dot(kkh, dsnb[hd], NT))
        d_qt.append(_dot(d_attn, kh) + _dot(dh, stb[hd]))
        d_kt.append(_dot(d_attn, qh, TN))
        d_kk.append(_dot(ih, dsnb[hd]))
        d_st.append(_dot(dh, qh, TN) + dsn[hd] * ecl[hd])
        st_dsn.append(jnp.sum(st[hd] * dsn[hd], axis=0, keepdims=True))
    d_qt = jnp.concatenate(d_qt, axis=1)
    d_kt = jnp.concatenate(d_kt, axis=1)
    d_kk = jnp.concatenate(d_kk, axis=1)
    kk = r["kk"]
    d_cl = r["ecl"] * jnp.concatenate(st_dsn, axis=1) + jnp.sum(kk * d_kk, axis=0, keepdims=True)
    d_k = (d_kk * r["ecl"] + d_kt) * r["en"]
    d_c = d_qt * r["qtb"].astype(F32) - d_kt * r["ktb"].astype(F32) - d_kk * kk
    rowid = lax.broadcasted_iota(jnp.int32, (CHUNK, D_MODEL), 0)
    d_c = d_c + jnp.where(rowid == CHUNK - 1, d_cl, 0.0)
    d_lf = _tri_cumsum(anti.astype(BF16), d_c)
    d_f = d_lf / r["f"] - d_k
    sg, sq = r["sg"], r["sq"]
    d_q = d_qt * r["ec"] * (sq * (1.0 + q * (1.0 - sq)))
    d_fp = d_f * (1.0 - lb) * sg * (1.0 - sg)
    return (d_q, d_fp, jnp.concatenate(d_i, axis=1), d_og, d_st,
            _colsum8(dy * on * sil), _colsum8(d_f * (1.0 - sg)))


def _hgrn_bwd(h, dyb, st_all, logits, gn, dh_buf, comm=None):
    t = h.shape[0]
    nb = t // HG_ROWS

    def body(q_ref, f_ref, i_ref, og_ref, dyb_ref, st_ref, lg_ref, gn_ref, dh_buf_ref,
             dh_ref, dlb_ref, dgn_ref, dstate):
        @pl.when(pl.program_id(0) == 0)
        def _():
            dstate[...] = jnp.zeros_like(dstate)
            dlb_ref[...] = jnp.zeros_like(dlb_ref)
            dgn_ref[...] = jnp.zeros_like(dgn_ref)

        causal, anti = _tri_masks()
        lb = _sig(lg_ref[0:1, :] - lg_ref[1:2, :])
        gnv = gn_ref[...]
        dsn = [dstate[hd] for hd in range(N_HEAD)]
        dgn_acc = jnp.zeros((8, D_MODEL), F32)
        dlb_acc = jnp.zeros((8, D_MODEL), F32)
        for cc in reversed(range(HG_CHUNKS)):
            rows = slice(cc * CHUNK, (cc + 1) * CHUNK)
            d_q, d_fp, d_i, d_og, dsn, dgn_c, dlb_c = _hgrn_chunk_bwd(
                q_ref[rows, :].astype(F32), f_ref[rows, :].astype(F32), i_ref[rows, :].astype(F32),
                og_ref[rows, :].astype(F32), dyb_ref[rows, :].astype(F32), gnv, lb,
                [st_ref[cc, hd] for hd in range(N_HEAD)], dsn, causal, anti)
            dgn_acc = dgn_acc + dgn_c
            dlb_acc = dlb_acc + dlb_c
            dh_ref[0, rows, :D_MODEL] = d_q.astype(BF16)
            dh_ref[0, rows, D_MODEL:] = d_fp.astype(BF16)
            dh_ref[1, rows, :D_MODEL] = d_i.astype(BF16)
            dh_ref[1, rows, D_MODEL:] = d_og.astype(BF16)
        dgn_ref[...] += dgn_acc
        dlb_ref[...] += dlb_acc
        for hd in range(N_HEAD):
            dstate[hd] = dsn[hd]

    def col(k):
        return pl.BlockSpec((HG_ROWS, D_MODEL), lambda ci: (nb - 1 - ci, k))

    acc8 = pl.BlockSpec((8, D_MODEL), lambda ci: (0, 0))
    pair = pl.BlockSpec((2, HG_ROWS, 2 * D_MODEL), lambda ci: (0, nb - 1 - ci, 0))
    return _grid1_call(body, comm, nb, name="hgrn_bwd",
                       out_shape=(jax.ShapeDtypeStruct(dh_buf.shape, BF16),
                                  jax.ShapeDtypeStruct((8, D_MODEL), F32),
                                  jax.ShapeDtypeStruct((8, D_MODEL), F32)),
                       in_specs=[col(2), col(3), col(4), col(5),
                                 pl.BlockSpec((HG_ROWS, D_MODEL), lambda ci: (nb - 1 - ci, 0)),
                                 pl.BlockSpec((HG_CHUNKS, N_HEAD, HEAD_DIM, HEAD_DIM),
                                              lambda ci: (nb - 1 - ci, 0, 0, 0)),
                                 pl.BlockSpec((2, D_MODEL), lambda ci: (0, 0)),
                                 pl.BlockSpec((1, D_MODEL), lambda ci: (0, 0)), ANY],
                       out_specs=(pair, acc8, acc8),
                       scratch=[pltpu.VMEM((N_HEAD, HEAD_DIM, HEAD_DIM), F32)],
                       args=(h, h, h, h, dyb, st_all, logits, gn, dh_buf), aliases={8: 0})


def _mix_fwd(ya, yb, h, x, wb0, wb1, wo, g1, b1, tm, comm=None):
    t = x.shape[0]

    def body(ya_ref, yb_ref, ga_ref, gb_ref, x_ref, wb0_ref, wb1_ref, wo_ref, g1_ref, b1_ref,
             r1_ref, a_ref, b_ref, m_ref, x1_ref):
        a = _dot(ya_ref[...], wb0_ref[...])
        b = _dot(yb_ref[...], wb1_ref[...])
        m = _sig(ga_ref[...].astype(F32)) * a + _sig(gb_ref[...].astype(F32)) * b
        r1 = ALPHA * x_ref[...] + _dot(m, wo_ref[...])
        xh, _ = _ln_stats(r1)
        r1_ref[...] = r1
        a_ref[...] = a.astype(BF16)
        b_ref[...] = b.astype(BF16)
        m_ref[...] = m.astype(BF16)
        x1_ref[...] = (xh * g1_ref[...] + b1_ref[...]).astype(BF16)

    tile = pl.BlockSpec((tm, D_MODEL), lambda i: (i, 0))
    wsp = pl.BlockSpec((D_MODEL, D_MODEL), lambda i: (0, 0))
    vec = pl.BlockSpec((1, D_MODEL), lambda i: (0, 0))
    f32o = jax.ShapeDtypeStruct((t, D_MODEL), F32)
    bfo = jax.ShapeDtypeStruct((t, D_MODEL), BF16)
    return _grid1_call(body, comm, t // tm, name="mix_fwd", out_shape=(f32o, bfo, bfo, bfo, bfo),
                       in_specs=[tile, tile,
                                 pl.BlockSpec((tm, D_MODEL), lambda i: (i, 6)),
                                 pl.BlockSpec((tm, D_MODEL), lambda i: (i, 7)),
                                 tile, wsp, wsp, wsp, vec, vec],
                       out_specs=(tile, tile, tile, tile, tile),
                       scratch=[], args=(ya, yb, h, h, x, wb0, wb1, wo, g1, b1))


def _mix_bwd(dr1, h, a, b, wo, wb0, wb1, tm):
    t = dr1.shape[0]

    def body(dr1_ref, ga_ref, gb_ref, a_ref, b_ref, wo_ref, wb0_ref, wb1_ref,
             da_ref, db_ref, dh3_ref, dya_ref, dyb_ref):
        d_m = _dot(dr1_ref[...], wo_ref[...], NT)
        sa = _sig(ga_ref[...].astype(F32))
        sb = _sig(gb_ref[...].astype(F32))
        d_a = (d_m * sa).astype(BF16)
        d_b = (d_m * sb).astype(BF16)
        da_ref[...] = d_a
        db_ref[...] = d_b
        dh3_ref[:, :D_MODEL] = (d_m * a_ref[...].astype(F32) * sa * (1.0 - sa)).astype(BF16)
        dh3_ref[:, D_MODEL:] = (d_m * b_ref[...].astype(F32) * sb * (1.0 - sb)).astype(BF16)
        dya_ref[...] = _dot(d_a, wb0_ref[...], NT).astype(BF16)
        dyb_ref[...] = _dot(d_b, wb1_ref[...], NT).astype(BF16)

    tile = pl.BlockSpec((tm, D_MODEL), lambda i: (i, 0))
    wsp = pl.BlockSpec((D_MODEL, D_MODEL), lambda i: (0, 0))
    f32o = jax.ShapeDtypeStruct((t, D_MODEL), F32)
    bfo = jax.ShapeDtypeStruct((t, D_MODEL), BF16)
    return _pc(body, name="mix_bwd",
               out_shape=(bfo, bfo, jax.ShapeDtypeStruct((N_CHIP, t, 2 * D_MODEL), BF16), bfo, bfo),
               grid=(t // tm,),
               in_specs=[tile,
                         pl.BlockSpec((tm, D_MODEL), lambda i: (i, 6)),
                         pl.BlockSpec((tm, D_MODEL), lambda i: (i, 7)),
                         tile, tile, wsp, wsp, wsp],
               out_specs=(tile, tile, pl.BlockSpec((None, tm, 2 * D_MODEL), lambda i: (DH_SLOT[3], i, 0)),
                          tile, tile),
               sem=("parallel",))(dr1, h, h, a, b, wo, wb0, wb1)


FF_TILE = 1408
FF_NJ = D_FF // FF_TILE


def _shift_down(v, k):
    return pltpu.roll(v, k, 0)


def _shift_up(v, k):
    return pltpu.roll(v, v.shape[0] - k, 0)


HALO = 16
FF_PIECES = ((0, 768), (768, FF_TILE))


def _ffn_up_act(x1b, wup_st, convw, convb, tm, comm):
    t = x1b.shape[0]
    ni = t // tm
    nth = tm // HALO

    def body(x_ref, xp_ref, wg_ref, wv_ref, cw_ref, cb_ref, h2_ref, act_ref):
        wg = wg_ref[...]
        gate = _dot(x_ref[...], wg).astype(BF16)
        val = _dot(x_ref[...], wv_ref[...]).astype(BF16)
        prev = (_dot(xp_ref[...], wg) * (pl.program_id(0) > 0).astype(F32)).astype(BF16)
        h2_ref[0] = gate
        h2_ref[1] = val
        ext = jnp.concatenate([prev.astype(F32), gate.astype(F32)], axis=0)
        gc = (cw_ref[0:1, :] * _shift_down(ext, 2) + cw_ref[1:2, :] * _shift_down(ext, 1)
              + cw_ref[2:3, :] * ext + cb_ref[...])[HALO:, :]
        act_ref[...] = (_gelu(gc) * val.astype(F32)).astype(BF16)

    res = _hosted_call(
        body, comm,
        lambda: (pl.program_id(0) == 0) & (pl.program_id(1) == 0),
        lambda: (pl.program_id(0) == ni - 1) & (pl.program_id(1) == FF_NJ - 1),
        name="ffn_up",
        out_shape=(jax.ShapeDtypeStruct((2, t, D_FF), BF16), jax.ShapeDtypeStruct((t, D_FF), BF16)),
        grid=(ni, FF_NJ),
        in_specs=[pl.BlockSpec((tm, D_MODEL), lambda i, j: (i, 0)),
                  pl.BlockSpec((HALO, D_MODEL), lambda i, j: (jnp.maximum(i * nth - 1, 0), 0)),
                  pl.BlockSpec((None, D_MODEL, FF_TILE), lambda i, j: (j, 0, 0)),
                  pl.BlockSpec((None, D_MODEL, FF_TILE), lambda i, j: (j + FF_NJ, 0, 0)),
                  pl.BlockSpec((3, FF_TILE), lambda i, j: (0, j)),
                  pl.BlockSpec((1, FF_TILE), lambda i, j: (0, j))],
        out_specs=(pl.BlockSpec((2, tm, FF_TILE), lambda i, j: (0, i, j)),
                   pl.BlockSpec((tm, FF_TILE), lambda i, j: (i, j))),
        scratch=[], sem=("arbitrary", "arbitrary"),
        args=(x1b, x1b, wup_st, wup_st, convw, convb))
    return res[0], res[1], res[2:]


def _out_fwd_bwd(act, x1b, r1, p2, tgt, wd, wpg, wpp, g1, b1, g2, b2, tm):
    t = r1.shape[0]

    def body(act_ref, x1b_ref, r1_ref, p_ref, tgt_ref, wd_ref, wpg_ref, wpp_ref,
             g1_ref, b1_ref, g2_ref, b2_ref,
             dr2_ref, dpg_ref, dpp_ref, loss_ref, dg2_ref, db2_ref):
        i = pl.program_id(0)

        @pl.when(i == 0)
        def _():
            loss_ref[...] = jnp.zeros_like(loss_ref)
            dg2_ref[...] = jnp.zeros_like(dg2_ref)
            db2_ref[...] = jnp.zeros_like(db2_ref)

        ffn = _dot(act_ref[...], wd_ref[...])
        pg = _dot(x1b_ref[...], wpg_ref[...])
        pp = _dot(p_ref[...], wpp_ref[...])
        s = _sig(pg)
        xh1, _ = _ln_stats(r1_ref[...])
        x1 = xh1 * g1_ref[...] + b1_ref[...]
        r2 = ALPHA * x1 + ffn + s * pp
        xh2, rstd2 = _ln_stats(r2)
        g2v = g2_ref[...]
        diff = xh2 * g2v + b2_ref[...] - tgt_ref[...]
        part = jnp.sum(jnp.sum(diff * diff, axis=1, keepdims=True), axis=0, keepdims=True)
        loss_ref[...] += jnp.broadcast_to(part * (0.5 / D_MODEL), loss_ref.shape)
        dy = diff * (1.0 / D_MODEL)
        dg2_ref[...] += _colsum8(dy * xh2)
        db2_ref[...] += _colsum8(dy)
        dr2 = _ln_bwd(dy * g2v, xh2, rstd2)
        dr2_ref[...] = dr2
        dpg_ref[...] = (dr2 * pp * s * (1.0 - s)).astype(BF16)
        dpp_ref[...] = (dr2 * s).astype(BF16)

    tile = pl.BlockSpec((tm, D_MODEL), lambda i: (i, 0))
    vec = pl.BlockSpec((1, D_MODEL), lambda i: (0, 0))
    acc8 = pl.BlockSpec((8, D_MODEL), lambda i: (0, 0))
    acc_shape = jax.ShapeDtypeStruct((8, D_MODEL), F32)
    return _pc(body, name="out_fwd_bwd",
               out_shape=(jax.ShapeDtypeStruct((t, D_MODEL), F32),
                          jax.ShapeDtypeStruct((t, D_MODEL), BF16),
                          jax.ShapeDtypeStruct((t, D_MODEL), BF16),
                          acc_shape, acc_shape, acc_shape),
               grid=(t // tm,),
               in_specs=[pl.BlockSpec((tm, D_FF), lambda i: (i, 0)), tile, tile,
                         pl.BlockSpec((tm, PLE_DIM), lambda i: (i, 0)), tile,
                         pl.BlockSpec((D_FF, D_MODEL), lambda i: (0, 0)),
                         pl.BlockSpec((D_MODEL, D_MODEL), lambda i: (0, 0)),
                         pl.BlockSpec((PLE_DIM, D_MODEL), lambda i: (0, 0)),
                         vec, vec, vec, vec],
               out_specs=(tile, tile, tile, acc8, acc8, acc8),
               sem=("arbitrary",))(act, x1b, r1, p2, tgt, wd, wpg, wpp, g1, b1, g2, b2)


def _ffn_bwd(h2, dr2, wd, wup_st, dpg, wpg, r1, g1, convw, convb, tm):
    t = r1.shape[0]
    ni = t // tm
    nth = tm // HALO
    last_halo = t // HALO - 1
    main_rows = slice(HALO, HALO + tm)

    def body(g_ref, gp_ref, gn_ref, v_ref, vn_ref, dr2_ref, dr2n_ref, wd_ref, wug_ref, wuv_ref,
             cw_ref, cb_ref, dpg_ref, wpg_ref, r1_ref, g1_ref,
             dh2_ref, dr1_ref, dcw_ref, dcb_ref, dg1_ref, db1_ref, acc):
        i = pl.program_id(0)
        j = pl.program_id(1)

        @pl.when((i == 0) & (j == 0))
        def _():
            dcw_ref[...] = jnp.zeros_like(dcw_ref)
            dcb_ref[...] = jnp.zeros_like(dcb_ref)
            dg1_ref[...] = jnp.zeros_like(dg1_ref)
            db1_ref[...] = jnp.zeros_like(db1_ref)

        dr2v = dr2_ref[...].astype(BF16)
        dr2n = dr2n_ref[...].astype(BF16)
        first = (i > 0).astype(F32)
        more = (i < ni - 1).astype(F32)
        prod = None
        dcw_parts, dcb_parts = [], []
        for c0, c1 in FF_PIECES:
            pc = slice(c0, c1)
            zeros = jnp.zeros((HALO, c1 - c0), F32)
            da = _dot(dr2v, wd_ref[pc, :], NT)
            dnext = _dot(dr2n, wd_ref[pc, :], NT) * more
            ext = jnp.concatenate([gp_ref[:, pc].astype(F32) * first, g_ref[:, pc].astype(F32),
                                   gn_ref[:, pc].astype(F32)], axis=0)
            vext = jnp.concatenate([zeros, v_ref[:, pc].astype(F32), vn_ref[:, pc].astype(F32)], axis=0)
            dext = jnp.concatenate([zeros, da, dnext], axis=0)
            g2 = _shift_down(ext, 2)
            g1s = _shift_down(ext, 1)
            gc = cw_ref[0:1, pc] * g2 + cw_ref[1:2, pc] * g1s + cw_ref[2:3, pc] * ext + cb_ref[:, pc]
            gl, dgl = _gelu_and_grad(gc)
            d_gc = dext * vext * dgl
            d_gate = (cw_ref[2:3, pc] * d_gc + cw_ref[1:2, pc] * _shift_up(d_gc, 1)
                      + cw_ref[0:1, pc] * _shift_up(d_gc, 2))[main_rows, :].astype(BF16)
            d_val = (da * gl[main_rows, :]).astype(BF16)
            dh2_ref[0, :, pc] = d_gate
            dh2_ref[1, :, pc] = d_val
            dm = d_gc[main_rows, :]
            s0 = jnp.sum(dm * g2[main_rows, :], axis=0, keepdims=True)
            s1 = jnp.sum(dm * g1s[main_rows, :], axis=0, keepdims=True)
            s2 = jnp.sum(dm * ext[main_rows, :], axis=0, keepdims=True)
            rowid = lax.broadcasted_iota(jnp.int32, (8, c1 - c0), 0)
            dcw_parts.append(jnp.where(rowid == 0, s0, jnp.where(rowid == 1, s1,
                                                                 jnp.where(rowid == 2, s2, 0.0))))
            dcb_parts.append(_colsum8(dm))
            part = _dot(d_gate, wug_ref[:, pc], NT) + _dot(d_val, wuv_ref[:, pc], NT)
            prod = part if prod is None else prod + part
        dcw_part = jnp.concatenate(dcw_parts, axis=1)
        dcb_part = jnp.concatenate(dcb_parts, axis=1)
        for jj in range(FF_NJ):
            @pl.when(j == jj)
            def _(jj=jj):
                cols = slice(jj * FF_TILE, (jj + 1) * FF_TILE)
                dcw_ref[:, cols] += dcw_part
                dcb_ref[:, cols] += dcb_part

        @pl.when(j == 0)
        def _():
            acc[...] = prod

        @pl.when(j > 0)
        def _():
            acc[...] += prod

        @pl.when(j == FF_NJ - 1)
        def _():
            d_x1 = acc[...] + _dot(dpg_ref[...], wpg_ref[...], NT) + ALPHA * dr2_ref[...]
            xh, rstd = _ln_stats(r1_ref[...])
            dg1_ref[...] += _colsum8(d_x1 * xh)
            db1_ref[...] += _colsum8(d_x1)
            dr1_ref[...] = _ln_bwd(d_x1 * g1_ref[...], xh, rstd)

    def h2_main(part):
        return pl.BlockSpec((None, tm, FF_TILE), lambda i, j: (part, i, j))

    def h2_prev(part):
        return pl.BlockSpec((None, HALO, FF_TILE), lambda i, j: (part, jnp.maximum(i * nth - 1, 0), j))

    def h2_next(part):
        return pl.BlockSpec((None, HALO, FF_TILE),
                            lambda i, j: (part, jnp.minimum((i + 1) * nth, last_halo), j))

    tile = pl.BlockSpec((tm, D_MODEL), lambda i, j: (i, 0))
    acc8 = pl.BlockSpec((8, D_MODEL), lambda i, j: (0, 0))
    accff = pl.BlockSpec((8, D_FF), lambda i, j: (0, 0))
    acc_shape = jax.ShapeDtypeStruct((8, D_MODEL), F32)
    accff_shape = jax.ShapeDtypeStruct((8, D_FF), F32)
    return _pc(body, name="ffn_bwd",
               out_shape=(jax.ShapeDtypeStruct((2, t, D_FF), BF16),
                          jax.ShapeDtypeStruct((t, D_MODEL), F32),
                          accff_shape, accff_shape, acc_shape, acc_shape),
               grid=(ni, FF_NJ),
               in_specs=[h2_main(0), h2_prev(0), h2_next(0), h2_main(1), h2_next(1),
                         tile,
                         pl.BlockSpec((HALO, D_MODEL), lambda i, j: (jnp.minimum((i + 1) * nth, last_halo), 0)),
                         pl.BlockSpec((FF_TILE, D_MODEL), lambda i, j: (j, 0)),
                         pl.BlockSpec((None, D_MODEL, FF_TILE), lambda i, j: (j, 0, 0)),
                         pl.BlockSpec((None, D_MODEL, FF_TILE), lambda i, j: (j + FF_NJ, 0, 0)),
                         pl.BlockSpec((3, FF_TILE), lambda i, j: (0, j)),
                         pl.BlockSpec((1, FF_TILE), lambda i, j: (0, j)),
                         tile, pl.BlockSpec((D_MODEL, D_MODEL), lambda i, j: (0, 0)),
                         tile, pl.BlockSpec((1, D_MODEL), lambda i, j: (0, 0))],
               out_specs=(pl.BlockSpec((2, tm, FF_TILE), lambda i, j: (0, i, j)),
                          tile, accff, accff, acc8, acc8),
               scratch=[pltpu.VMEM((tm, D_MODEL), F32)],
               sem=("arbitrary", "arbitrary"))(h2, h2, h2, h2, h2, dr2, dr2, wd, wup_st, wup_st,
                                               convw, convb, dpg, wpg, r1, g1)


ANY = pl.BlockSpec(memory_space=pl.ANY)


def _chip_peers():
    x, y, c = lax.axis_index("x"), lax.axis_index("y"), lax.axis_index("c")
    return x, y, c, [(1 - x, y), (x, 1 - y), (1 - x, 1 - y)]


def _gather_comm(halved, whole=()):
    n, nw = len(halved), len(whole)

    def copies(ins, outs, sems):
        ici_send, ici_recv, d2d_send, d2d_recv, own_send, own_recv = sems
        x, y, c, peers = _chip_peers()
        me = 2 * x + y
        sibling = (x, y, 1 - c)
        own, ici, ici_wait, fwd, fwd_wait = [], [], [], [], []
        for ti in range(n + nw):
            src, dst = ins[ti], outs[ti]
            own.append(pltpu.make_async_remote_copy(
                src_ref=src, dst_ref=dst.at[me], send_sem=own_send.at[ti], recv_sem=own_recv.at[ti],
                device_id=sibling, device_id_type=MESH))
            for k, (px, py) in enumerate(peers):
                pk = 2 * px + py
                sem = dict(send_sem=ici_send.at[ti * 3 + k], recv_sem=ici_recv.at[ti * 3 + k],
                           device_id=(px, py, c), device_id_type=MESH)
                if ti < n:
                    ici.append(pltpu.make_async_remote_copy(src_ref=src.at[c], dst_ref=dst.at[me, c], **sem))
                    ici_wait.append(pltpu.make_async_remote_copy(src_ref=src.at[c], dst_ref=dst.at[pk, c], **sem))
                    dsem = dict(send_sem=d2d_send.at[ti * 3 + k], recv_sem=d2d_recv.at[ti * 3 + k],
                                device_id=sibling, device_id_type=MESH)
                    fwd.append(pltpu.make_async_remote_copy(src_ref=dst.at[pk, c], dst_ref=dst.at[pk, c], **dsem))
                    fwd_wait.append(pltpu.make_async_remote_copy(
                        src_ref=dst.at[pk, 1 - c], dst_ref=dst.at[pk, 1 - c], **dsem))
                else:
                    ici.append(pltpu.make_async_remote_copy(src_ref=src, dst_ref=dst.at[me], **sem))
                    ici_wait.append(pltpu.make_async_remote_copy(src_ref=src, dst_ref=dst.at[pk], **sem))
        return own, ici, ici_wait, fwd, fwd_wait

    def start(ins, outs, sems):
        own, ici, _, _, _ = copies(ins, outs, sems)
        for cp in own + ici:
            cp.start()

    def finish(ins, outs, sems):
        own, ici, ici_wait, fwd, fwd_wait = copies(ins, outs, sems)
        for i, cp in enumerate(ici_wait):
            cp.wait_recv()
            if i < len(fwd):
                fwd[i].start()
        for cp in fwd_wait + own:
            cp.wait_recv()
        for cp in own + ici + fwd:
            cp.wait_send()

    srcs = list(halved) + list(whole)
    return _Comm(srcs, [jax.ShapeDtypeStruct((N_CHIP,) + s.shape, s.dtype) for s in srcs],
                 [pltpu.SemaphoreType.DMA((3 * (n + nw),)), pltpu.SemaphoreType.DMA((3 * (n + nw),)),
                  pltpu.SemaphoreType.DMA((max(3 * n, 1),)), pltpu.SemaphoreType.DMA((max(3 * n, 1),)),
                  pltpu.SemaphoreType.DMA((n + nw,)), pltpu.SemaphoreType.DMA((n + nw,))],
                 start, finish)


def _sibling_exchange_comm(grads):
    n = len(grads)

    def copies(ins, outs, sems):
        send_sems, recv_sems = sems
        x, y, c = lax.axis_index("x"), lax.axis_index("y"), lax.axis_index("c")
        res = []
        for ti in range(n):
            half = ins[ti].shape[1] // 2
            res.append(pltpu.make_async_remote_copy(
                src_ref=ins[ti].at[:, pl.ds(pl.multiple_of((1 - c) * half, 16), half), :],
                dst_ref=outs[ti],
                send_sem=send_sems.at[ti], recv_sem=recv_sems.at[ti],
                device_id=(x, y, 1 - c), device_id_type=MESH))
        return res

    def start(ins, outs, sems):
        for cp in copies(ins, outs, sems):
            cp.start()

    def finish(ins, outs, sems):
        for cp in copies(ins, outs, sems):
            cp.wait()

    return _Comm(grads, [jax.ShapeDtypeStruct((N_CHIP, g.shape[1] // 2, g.shape[2]), g.dtype) for g in grads],
                 [pltpu.SemaphoreType.DMA((n,)), pltpu.SemaphoreType.DMA((n,))], start, finish)


def _in_proj_gathering(x2b, own, chip, tm, comm):
    t = x2b.shape[0]
    ni = t // tm
    half, cols = own.shape[1], own.shape[2]
    nci, nco = len(comm.ins), len(comm.out_shapes)

    def body(chip_ref, x_ref, own_ref, own_hbm, *rest):
        c_in = rest[:nci]
        h_ref, win_out = rest[nci:nci + 2]
        c_out = rest[nci + 2:nci + 2 + nco]
        w_scr, ici_send, ici_recv, d2d_send, d2d_recv, own_sems, ld_sems = rest[nci + 2 + nco:nci + 9 + nco]
        c_sem = rest[nci + 9 + nco:]
        s, i = pl.program_id(0), pl.program_id(1)
        x, y, c, peers = _chip_peers()
        me = 2 * x + y
        sibling = (x, y, 1 - c)

        def ici(k, slot):
            px, py = peers[k]
            return pltpu.make_async_remote_copy(
                src_ref=own_hbm.at[c], dst_ref=win_out.at[slot, c],
                send_sem=ici_send.at[k], recv_sem=ici_recv.at[k],
                device_id=(px, py, c), device_id_type=MESH)

        def forward(k, core):
            pk = 2 * peers[k][0] + peers[k][1]
            return pltpu.make_async_remote_copy(
                src_ref=win_out.at[pk, core], dst_ref=win_out.at[pk, core],
                send_sem=d2d_send.at[k], recv_sem=d2d_recv.at[k],
                device_id=sibling, device_id_type=MESH)

        place_own = pltpu.make_async_remote_copy(
            src_ref=own_hbm, dst_ref=win_out.at[me], send_sem=own_sems.at[0], recv_sem=own_sems.at[1],
            device_id=sibling, device_id_type=MESH)

        @pl.when((s == 0) & (i == 0))
        def _():
            for k in range(2):
                ici(k, me).start()
            place_own.start()

        @pl.when(s == 0)
        def _():
            xv = x_ref[...]
            h_ref[...] = (_dot(xv[:, :half], own_ref[0]) + _dot(xv[:, half:], own_ref[1])).astype(BF16)

        for k in range(3):
            @pl.when((s == k + 1) & (i == 0))
            def _(k=k):
                pk = 2 * peers[k][0] + peers[k][1]
                ici(k, pk).wait_recv()
                if k == 0:
                    ici(2, me).start()
                forward(k, c).start()
                forward(k, 1 - c).wait_recv()
                loads = [pltpu.make_async_copy(win_out.at[pk, hh], w_scr.at[hh], ld_sems.at[hh])
                         for hh in range(2)]
                for ld in loads:
                    ld.start()
                for ld in loads:
                    ld.wait()
                if k == 1:
                    comm.start(c_in, c_out, c_sem)

        @pl.when(s > 0)
        def _():
            xv = x_ref[...]
            h_ref[...] = (_dot(xv[:, :half], w_scr[0]) + _dot(xv[:, half:], w_scr[1])).astype(BF16)

        @pl.when((s == N_CHIP - 1) & (i == ni - 1))
        def _():
            place_own.wait()
            for k in range(3):
                ici(k, me).wait_send()
                forward(k, c).wait_send()
            comm.finish(c_in, c_out, c_sem)

    def shard_col(s, me):
        return jnp.where(s == 0, me, me ^ jnp.where(s == 1, 2, jnp.where(s == 2, 1, 3)))

    res = _pc(body, name="in_proj",
              out_shape=(jax.ShapeDtypeStruct((t, N_CHIP * cols), BF16),
                         jax.ShapeDtypeStruct((N_CHIP,) + own.shape, own.dtype)) + tuple(comm.out_shapes),
              grid=(N_CHIP, ni), nsp=1,
              in_specs=[pl.BlockSpec((tm, 2 * half), lambda s, i, chip_ref: (i, 0)),
                        pl.BlockSpec(own.shape, lambda s, i, chip_ref: (0, 0, 0)),
                        ANY] + [ANY] * nci,
              out_specs=(pl.BlockSpec((tm, cols), lambda s, i, chip_ref: (i, shard_col(s, chip_ref[0]))),
                         ANY) + tuple([ANY] * nco),
              scratch=[pltpu.VMEM(own.shape, own.dtype),
                       pltpu.SemaphoreType.DMA((3,)), pltpu.SemaphoreType.DMA((3,)),
                       pltpu.SemaphoreType.DMA((3,)), pltpu.SemaphoreType.DMA((3,)),
                       pltpu.SemaphoreType.DMA((2,)), pltpu.SemaphoreType.DMA((2,))] + comm.sems,
              sem=("arbitrary", "arbitrary"))(chip, x2b, own, own, *comm.ins)
    return res[0], res[1], res[2:]


def _rs_add_halves(name, grad, recv, core):
    _, r, cdim = grad.shape
    half = r // 2
    tr = _row_tile(half, cdim, mult=16)
    nr = half // tr

    def body(c_ref, g_ref, r_ref, o_ref):
        o_ref[...] = (g_ref[...].astype(F32) + r_ref[...].astype(F32)).astype(BF16)

    return _pc(body, name=name, out_shape=jax.ShapeDtypeStruct((N_CHIP, half, cdim), BF16),
               grid=(N_CHIP, nr), nsp=1,
               in_specs=[pl.BlockSpec((None, tr, cdim), lambda j, i, c_ref: (j, c_ref[0] * nr + i, 0)),
                         pl.BlockSpec((None, tr, cdim), lambda j, i, c_ref: (j, i, 0))],
               out_specs=pl.BlockSpec((None, tr, cdim), lambda j, i, c_ref: (j, i, 0)),
               sem=("parallel", "parallel"))(core, grad, recv)


def _chip_exchange_comm(parts):
    n = len(parts)

    def copies(ins, outs, sems):
        send_sems, recv_sems = sems
        x, y, c, peers = _chip_peers()
        return [pltpu.make_async_remote_copy(
            src_ref=ins[ti].at[2 * px + py], dst_ref=outs[ti].at[k],
            send_sem=send_sems.at[ti * 3 + k], recv_sem=recv_sems.at[ti * 3 + k],
            device_id=(px, py, c), device_id_type=MESH)
            for ti in range(n) for k, (px, py) in enumerate(peers)]

    def start(ins, outs, sems):
        for cp in copies(ins, outs, sems):
            cp.start()

    def finish(ins, outs, sems):
        for cp in copies(ins, outs, sems):
            cp.wait()

    return _Comm(parts, [jax.ShapeDtypeStruct((3,) + p.shape[1:], p.dtype) for p in parts],
                 [pltpu.SemaphoreType.DMA((3 * n,)), pltpu.SemaphoreType.DMA((3 * n,))], start, finish)


def _rs_sum_chips(name, part, recv, chip):
    _, half, cdim = recv.shape
    tr = _row_tile(half, cdim, mult=16)

    def body(chip_ref, p_ref, r_ref, o_ref):
        o_ref[...] = ((p_ref[...].astype(F32) + r_ref[0].astype(F32)) + r_ref[1].astype(F32)
                      ) + r_ref[2].astype(F32)

    return _pc(body, name=name, out_shape=jax.ShapeDtypeStruct((half, cdim), F32),
               grid=(half // tr,), nsp=1,
               in_specs=[pl.BlockSpec((None, tr, cdim), lambda i, chip_ref: (chip_ref[0], i, 0)),
                         pl.BlockSpec((3, tr, cdim), lambda i, chip_ref: (0, i, 0))],
               out_specs=pl.BlockSpec((tr, cdim), lambda i, chip_ref: (i, 0)),
               sem=("parallel",))(chip, part, recv)


def _rs_send_halves(halves):
    n = len(halves)

    def body(*refs):
        ins, outs = refs[:n], refs[n:2 * n]
        send_sems, recv_sems = refs[2 * n:]
        x, y, c = lax.axis_index("x"), lax.axis_index("y"), lax.axis_index("c")
        sends = []
        for ti in range(n):
            cp = pltpu.make_async_remote_copy(
                src_ref=ins[ti], dst_ref=outs[ti],
                send_sem=send_sems.at[ti], recv_sem=recv_sems.at[ti],
                device_id=(x, y, 1 - c), device_id_type=MESH)
            cp.start()
            sends.append(cp)
        for cp in sends:
            cp.wait()

    return _pc(body, name="rs_send_halves",
               out_shape=tuple(jax.ShapeDtypeStruct(hv.shape, hv.dtype) for hv in halves),
               in_specs=[ANY] * n, out_specs=tuple([ANY] * n),
               scratch=[pltpu.SemaphoreType.DMA((n,)), pltpu.SemaphoreType.DMA((n,))])(*halves)


def _adamw_rows(name, mine, theirs, w, m, v, core):
    half, cdim = mine.shape
    tr = _row_tile(half, cdim, budget=1 << 19)
    nrh = half // tr

    def body(c_ref, mine_ref, theirs_ref, w_ref, m_ref, v_ref, g_ref, d_ref, m2_ref, v2_ref):
        is_mine = (pl.program_id(0) // nrh) == c_ref[0]
        g = jnp.where(is_mine, mine_ref[...], theirs_ref[...])
        d, m2, v2 = _adamw(w_ref[...], g, m_ref[...], v_ref[...])
        g_ref[...] = g
        d_ref[...] = d
        m2_ref[...] = m2
        v2_ref[...] = v2

    htile = pl.BlockSpec((tr, cdim), lambda i, c_ref: (i % nrh, 0))
    tile = pl.BlockSpec((tr, cdim), lambda i, c_ref: (i, 0))
    shp = jax.ShapeDtypeStruct((2 * half, cdim), F32)
    return _pc(body, name=name, out_shape=(shp, shp, shp, shp), grid=(2 * nrh,), nsp=1,
               in_specs=[htile, htile, tile, tile, tile], out_specs=(tile, tile, tile, tile),
               sem=("parallel",))(core, mine, theirs, w, m, v)


def _adamw_whole(name, g, w, m, v):
    def body(g_ref, w_ref, m_ref, v_ref, d_ref, m2_ref, v2_ref):
        d, m2, v2 = _adamw(w_ref[...], g_ref[...], m_ref[...], v_ref[...])
        d_ref[...] = d
        m2_ref[...] = m2
        v2_ref[...] = v2

    shp = jax.ShapeDtypeStruct(g.shape, F32)
    return _pc(body, name=name, out_shape=(shp, shp, shp))(g, w, m, v)


SMALL_LAYOUT = (
    ("sgu_w_s", 1024, 1, 0),
    ("sgu_b_s", 8, 1, 1024),
    ("sgu_norm_g", 1, 0, 0),
    ("sgu_norm_b", 1, 0, 1),
    ("hgrn_norm_g", 1, 0, 3),
    ("ln1_g", 1, 0, 4),
    ("ln1_b", 1, 0, 5),
    ("ffn_conv_b", 1, 2, 3),
    ("ln2_g", 1, 0, 6),
    ("ln2_b", 1, 0, 7),
)
LB_ROW = 2
LOSS_ROW = 8
PACK_SHAPES = ((16, D_MODEL), (N_GROUP * 128 + 8, 128), (8, D_FF))


def _small_allreduce_adamw(rows1024, dws, dbs, dcw, dcb, logits, m_logits, v_logits,
                           small_w, small_m, small_v):
    ns = len(SMALL_LAYOUT)
    nr = len(rows1024)
    nb = len(PACK_SHAPES)

    def body(*refs):
        row_refs = refs[:nr]
        dws_ref, dbs_ref, dcw_ref, dcb_ref, lg_ref, mlg_ref, vlg_ref = refs[nr:nr + 7]
        pos = nr + 7
        w_refs = refs[pos:pos + ns]
        m_refs = refs[pos + ns:pos + 2 * ns]
        v_refs = refs[pos + 2 * ns:pos + 3 * ns]
        pos += 3 * ns
        loss_ref, dcw_out = refs[pos:pos + 2]
        lg_outs = refs[pos + 2:pos + 6]
        pos += 6
        outs = refs[pos:pos + 4 * ns]
        pos += 4 * ns
        pack = refs[pos:pos + nb]
        sib = refs[pos + nb:pos + 2 * nb]
        gath = refs[pos + 2 * nb:pos + 3 * nb]
        d2d_send, d2d_recv, ici_send, ici_recv = refs[pos + 3 * nb:]

        x, y, c, peers = _chip_peers()
        me = 2 * x + y
        sibling = (x, y, 1 - c)

        pack[0][...] = jnp.zeros(PACK_SHAPES[0], F32)
        for k in range(nr):
            pack[0][k:k + 1, :] = row_refs[k][0:1, :]
        pack[1][0:N_GROUP * 128, :] = dws_ref[...]
        pack[1][N_GROUP * 128:, :] = dbs_ref[...]
        pack[2][...] = jnp.zeros(PACK_SHAPES[2], F32)
        pack[2][0:3, :] = dcw_ref[0:3, :]
        pack[2][3:4, :] = dcb_ref[0:1, :]

        d2d = [pltpu.make_async_remote_copy(
            src_ref=pack[b], dst_ref=sib[b], send_sem=d2d_send.at[b], recv_sem=d2d_recv.at[b],
            device_id=sibling, device_id_type=MESH) for b in range(nb)]
        for cp in d2d:
            cp.start()
        for cp in d2d:
            cp.wait()
        for b in range(nb):
            gath[b][me] = pack[b][...] + sib[b][...]

        ici, ici_wait = [], []
        for b in range(nb):
            for k, (px, py) in enumerate(peers):
                sem = dict(send_sem=ici_send.at[b * 3 + k], recv_sem=ici_recv.at[b * 3 + k],
                           device_id=(px, py, c), device_id_type=MESH)
                ici.append(pltpu.make_async_remote_copy(src_ref=gath[b].at[me], dst_ref=gath[b].at[me], **sem))
                ici_wait.append(pltpu.make_async_remote_copy(
                    src_ref=gath[b].at[me], dst_ref=gath[b].at[2 * px + py], **sem))
        for cp in ici:
            cp.start()
        for cp in ici_wait:
            cp.wait_recv()
        for cp in ici:
            cp.wait_send()

        tot = pack
        for b in range(nb):
            tot[b][...] = ((gath[b][0] + gath[b][1]) + gath[b][2]) + gath[b][3]

        loss_ref[...] = tot[0][LOSS_ROW:LOSS_ROW + 1, :]
        dcw_out[...] = tot[2][...]
        lb = _sig(lg_ref[0:1, :] - lg_ref[1:2, :])
        d0 = tot[0][LB_ROW:LB_ROW + 1, :] * lb * (1.0 - lb)
        rowid = lax.broadcasted_iota(jnp.int32, (2, D_MODEL), 0)
        g_lg = jnp.where(rowid == 0, d0, -d0)
        dl, ml, vl = _adamw(lg_ref[...], g_lg, mlg_ref[...], vlg_ref[...])
        lg_outs[0][...] = g_lg
        lg_outs[1][...] = dl
        lg_outs[2][...] = ml
        lg_outs[3][...] = vl
        for si, (_, rows, b, r0) in enumerate(SMALL_LAYOUT):
            g = tot[b][r0:r0 + rows, :]
            dl, ml, vl = _adamw(w_refs[si][...], g, m_refs[si][...], v_refs[si][...])
            outs[4 * si][...] = g
            outs[4 * si + 1][...] = dl
            outs[4 * si + 2][...] = ml
            outs[4 * si + 3][...] = vl

    shapes = [jax.ShapeDtypeStruct((1, D_MODEL), F32), jax.ShapeDtypeStruct((8, D_FF), F32)]
    shapes += [jax.ShapeDtypeStruct((2, D_MODEL), F32)] * 4
    for w in small_w:
        shapes += [jax.ShapeDtypeStruct(w.shape, F32)] * 4
    scratch = [pltpu.VMEM(shp, F32) for shp in PACK_SHAPES]
    scratch += [pltpu.VMEM(shp, F32) for shp in PACK_SHAPES]
    scratch += [pltpu.VMEM((N_CHIP,) + shp, F32) for shp in PACK_SHAPES]
    scratch += [pltpu.SemaphoreType.DMA((nb,)), pltpu.SemaphoreType.DMA((nb,)),
                pltpu.SemaphoreType.DMA((3 * nb,)), pltpu.SemaphoreType.DMA((3 * nb,))]
    vm = pl.BlockSpec(memory_space=pltpu.VMEM)
    n_in = nr + 7 + 3 * ns
    res = _pc(body, name="small_allreduce_adamw", out_shape=tuple(shapes),
              in_specs=[vm] * n_in, out_specs=tuple([vm] * len(shapes)),
              scratch=scratch)(*rows1024, dws, dbs, dcw, dcb, logits, m_logits, v_logits,
                               *small_w, *small_m, *small_v)
    return res[0], res[1], res[2:6], res[6:]


def kernel(x, p, w_in, sgu_w_s, sgu_b_s, sgu_norm_g, sgu_norm_b, hgrn_lb_logits, hgrn_norm_g, w_branch, w_out, ln1_g, ln1_b, ffn_w_up, ffn_conv_w, ffn_conv_b, ffn_w_down, ln2_g, ln2_b, ple_w_proj, ple_w_gate, loss_target, m_w_in, m_sgu_w_s, m_sgu_b_s, m_sgu_norm_g, m_sgu_norm_b, m_hgrn_lb_logits, m_hgrn_norm_g, m_w_branch, m_w_out, m_ln1_g, m_ln1_b, m_ffn_w_up, m_ffn_conv_w, m_ffn_conv_b, m_ffn_w_down, m_ln2_g, m_ln2_b, m_ple_w_proj, m_ple_w_gate, v_w_in, v_sgu_w_s, v_sgu_b_s, v_sgu_norm_g, v_sgu_norm_b, v_hgrn_lb_logits, v_hgrn_norm_g, v_w_branch, v_w_out, v_ln1_g, v_ln1_b, v_ffn_w_up, v_ffn_conv_w, v_ffn_conv_b, v_ffn_w_down, v_ln2_g, v_ln2_b, v_ple_w_proj, v_ple_w_gate):
    t = x.shape[1]
    x2 = x.reshape(t, D_MODEL)
    x2b = x2.astype(BF16)
    p2 = p.reshape(t, PLE_DIM)
    tgt = loss_target.reshape(t, D_MODEL)
    core = lax.axis_index("c").astype(jnp.int32).reshape(1)
    chip_id = (2 * lax.axis_index("x") + lax.axis_index("y")).astype(jnp.int32).reshape(1)

    big_w = [w_in[0], w_branch[0, 0], w_branch[0, 1], w_out[0], ffn_w_up[0], ffn_w_down[0],
             ple_w_proj[0], ple_w_gate[0]]
    big_m = [m_w_in[0], m_w_branch[0, 0], m_w_branch[0, 1], m_w_out[0], m_ffn_w_up[0],
             m_ffn_w_down[0], m_ple_w_proj[0], m_ple_w_gate[0]]
    big_v = [v_w_in[0], v_w_branch[0, 0], v_w_branch[0, 1], v_w_out[0], v_ffn_w_up[0],
             v_ffn_w_down[0], v_ple_w_proj[0], v_ple_w_gate[0]]
    def halves_of(i):
        w = big_w[i]
        return w.astype(BF16).reshape(2, w.shape[0] // 2, w.shape[1])

    def stacked(g, i):
        return g.reshape(N_CHIP, big_w[i].shape[0], big_w[i].shape[1])


    cid = jnp.arange(SGU_BLOCK) // CHUNK
    maskf = (cid[:, None] >= cid[None, :]).astype(F32)
    ws_masked = sgu_w_s[0] * maskf[None]
    wm = ws_masked.astype(BF16)
    wmt = jnp.transpose(ws_masked, (0, 2, 1)).astype(BF16)
    bsb = jnp.broadcast_to(sgu_b_s[0][:, :, None], (N_GROUP, SGU_BLOCK, 128))

    up_rows = big_w[4].shape[0] // 2
    up_blocks = [big_w[4][k * up_rows:(k + 1) * up_rows].astype(BF16).reshape(2, up_rows // 2, -1)
                 for k in range(2)]
    h, win_g, (up0_g,) = _in_proj_gathering(x2b, halves_of(0), chip_id, 512, _gather_comm([up_blocks[0]]))
    win_st = stacked(win_g, 0)
    ya, _ = _sgu_fwd(h, wm, bsb, sgu_norm_g, sgu_norm_b)
    (yb, st_all), mix_g = _hgrn_fwd(
        h, hgrn_lb_logits, hgrn_norm_g,
        comm=_gather_comm([halves_of(i) for i in (1, 2, 3)] + [up_blocks[1]], [ffn_conv_w[0]]))
    wb0, wb1, wo = [stacked(g, i).reshape(D_MODEL, D_MODEL) for g, i in zip(mix_g[:3], (1, 2, 3))]
    wup_st = jnp.concatenate([g.reshape(N_CHIP, up_rows, -1) for g in (up0_g, mix_g[3])], axis=1)
    convw = jnp.transpose(mix_g[4], (1, 0, 2)).reshape(3, D_FF)
    (r1, a_br, b_br, m_bf, x1b), _ = _mix_fwd(ya, yb, h, x2, wb0, wb1, wo, ln1_g, ln1_b, 256)
    h2, act, out_g = _ffn_up_act(x1b, wup_st, convw, ffn_conv_b, 512,
                                 _gather_comm([halves_of(i) for i in (5, 6, 7)]))
    wd = stacked(out_g[0], 5).reshape(D_FF, D_MODEL)
    wpp = jnp.transpose(stacked(out_g[1], 6), (1, 0, 2)).reshape(PLE_DIM, D_MODEL)
    wpg = stacked(out_g[2], 7).reshape(D_MODEL, D_MODEL)
    dr2, dpg, dpp, loss_acc, dg2, db2 = _out_fwd_bwd(
        act, x1b, r1, p2, tgt, wd, wpg, wpp, ln1_g, ln1_b, ln2_g, ln2_b, 256)

    dh2, dr1, dcw, dcb, dg1, db1 = _ffn_bwd(h2, dr2, wd, wup_st, dpg, wpg, r1, ln1_g, convw, ffn_conv_b, 256)
    d_wd = _mm_tn("ffn_down_wgrad", act, dr2, FF_TILE, 512)
    d_wpg = _mm_tn("ple_gate_wgrad", x1b, dpg, 512, D_MODEL)
    d_wpp_st = _mm_tn("ple_proj_wgrad", p2, dpp, PLE_DIM, PLE_DIM, stacked=True)
    d_wup_st = _mm("ffn_up_wgrad", x1b, dh2, TN, (2, N_CHIP),
                   pl.BlockSpec((t, 512), lambda i, j: (0, i)),
                   pl.BlockSpec((None, t, FF_TILE), lambda i, j: (j // FF_NJ, 0, j % FF_NJ)),
                   jax.ShapeDtypeStruct((N_CHIP, D_MODEL, FF_TILE), BF16),
                   pl.BlockSpec((None, 512, FF_TILE), lambda i, j: (j, i, 0)))
    da_bf, db_bf, dh, dya, dyb = _mix_bwd(dr1, h, a_br, b_br, wo, wb0, wb1, 256)
    d_wo = _mm_tn("out_proj_wgrad", m_bf, dr1, 512, 512)
    d_wb0 = _mm_tn("branch0_wgrad", ya, da_bf, 512, D_MODEL)
    d_wb1 = _mm_tn("branch1_wgrad", yb, db_bf, 512, D_MODEL)
    grads_1 = [d_wb0.reshape(4, 256, D_MODEL), d_wb1.reshape(4, 256, D_MODEL),
               d_wo.reshape(4, 256, D_MODEL), d_wup_st, d_wd.reshape(4, D_FF // 4, D_MODEL),
               d_wpp_st, d_wpg.reshape(4, 256, D_MODEL)]
    (dh, dws, dbs, dgv, dbv), recv_a1 = _sgu_bwd(h, dya, wm, wmt, bsb, sgu_norm_g, sgu_norm_b, maskf, dh,
                                                 comm=_sibling_exchange_comm(grads_1))
    parts_1 = [_rs_add_halves("rs_add_halves%d" % (i + 1), g, r, core)
               for i, (g, r) in enumerate(zip(grads_1, recv_a1))]
    (dh, dlb, dgn), recv_b1 = _hgrn_bwd(h, dyb, st_all, hgrn_lb_logits, hgrn_norm_g, dh,
                                         comm=_chip_exchange_comm(parts_1))

    grads_0 = [_in_proj_wgrad(x2b, dh, 512, D_MODEL)]
    recv_a0 = _run_comm("rs_sibling_exchange0", _sibling_exchange_comm(grads_0))
    parts_0 = [_rs_add_halves("rs_add_halves0", grads_0[0], recv_a0[0], core)]
    gx, recv_b0 = _in_proj_xgrad(dh, win_st, dr1, 512, _chip_exchange_comm(parts_0))
    parts = parts_0 + parts_1
    recv_b = list(recv_b0) + list(recv_b1)
    halves = [_rs_sum_chips("rs_sum_chips%d" % i, pt, r, chip_id)
              for i, (pt, r) in enumerate(zip(parts, recv_b))]
    theirs = _rs_send_halves(halves)
    big_out = [_adamw_rows("adamw_big%d" % i, halves[i], theirs[i], big_w[i], big_m[i], big_v[i], core)
               for i in range(len(halves))]

    small_in = dict(sgu_w_s=(sgu_w_s, m_sgu_w_s, v_sgu_w_s), sgu_b_s=(sgu_b_s, m_sgu_b_s, v_sgu_b_s),
                    sgu_norm_g=(sgu_norm_g, m_sgu_norm_g, v_sgu_norm_g),
                    sgu_norm_b=(sgu_norm_b, m_sgu_norm_b, v_sgu_norm_b),
                    hgrn_norm_g=(hgrn_norm_g, m_hgrn_norm_g, v_hgrn_norm_g),
                    ln1_g=(ln1_g, m_ln1_g, v_ln1_g), ln1_b=(ln1_b, m_ln1_b, v_ln1_b),
                    ffn_conv_b=(ffn_conv_b, m_ffn_conv_b, v_ffn_conv_b),
                    ln2_g=(ln2_g, m_ln2_g, v_ln2_g), ln2_b=(ln2_b, m_ln2_b, v_ln2_b))

    def flat(name, arr):
        rows = dict((n, r) for n, r, _, _ in SMALL_LAYOUT)[name]
        return arr.reshape(rows, arr.size // rows)

    names = [n for n, _, _, _ in SMALL_LAYOUT]
    sw = [flat(n, small_in[n][0]) for n in names]
    sm = [flat(n, small_in[n][1]) for n in names]
    sv = [flat(n, small_in[n][2]) for n in names]
    loss_rows, dcw_tot, lg_out, small_out = _small_allreduce_adamw(
        [dgv, dbv, dlb, dgn, dg1, db1, dg2, db2, loss_acc], dws.reshape(N_GROUP * 128, 128), dbs, dcw, dcb,
        hgrn_lb_logits, m_hgrn_lb_logits, v_hgrn_lb_logits, sw, sm, sv)
    loss = loss_rows[0, 0]

    chip = 2 * lax.axis_index("x") + lax.axis_index("y")
    g_cw = lax.dynamic_slice(dcw_tot, (0, chip * (D_FF // 4)), (3, D_FF // 4))
    cw_out = _adamw_whole("adamw_conv_w", g_cw, ffn_conv_w[0], m_ffn_conv_w[0], v_ffn_conv_w[0])

    res = {}
    for si, n in enumerate(names):
        shp = small_in[n][0].shape
        res[n] = tuple(small_out[4 * si + k].reshape(shp) for k in range(4))
    res["hgrn_lb_logits"] = tuple(lg_out)
    res["ffn_conv_w"] = (g_cw[None],) + tuple(o[None] for o in cw_out)

    def big(i):
        return tuple(big_out[i])

    res["w_in"] = tuple(o[None] for o in big(0))
    res["w_branch"] = tuple(jnp.stack([o0, o1])[None] for o0, o1 in zip(big(1), big(2)))
    res["w_out"] = tuple(o[None] for o in big(3))
    res["ffn_w_up"] = tuple(o[None] for o in big(4))
    res["ffn_w_down"] = tuple(o[None] for o in big(5))
    res["ple_w_proj"] = tuple(o[None] for o in big(6))
    res["ple_w_gate"] = tuple(o[None] for o in big(7))

    order = ["w_in", "sgu_w_s", "sgu_b_s", "sgu_norm_g", "sgu_norm_b", "hgrn_lb_logits",
             "hgrn_norm_g", "w_branch", "w_out", "ln1_g", "ln1_b", "ffn_w_up", "ffn_conv_w",
             "ffn_conv_b", "ffn_w_down", "ln2_g", "ln2_b", "ple_w_proj", "ple_w_gate"]
    outs = [loss, gx.reshape(1, t, D_MODEL)]
    for k in range(4):
        outs += [res[n][k] for n in order]
    return tuple(outs)
```

```python
import functools

import jax
import jax.numpy as jnp
from jax import lax
from jax.experimental import pallas as pl
from jax.experimental.pallas import tpu as pltpu

F32 = jnp.float32
BF16 = jnp.bfloat16
HIGHEST = lax.Precision.HIGHEST
MESH = pl.DeviceIdType.MESH

D_MODEL = 1024
CHUNK = 64
SGU_BLOCK = 128
SGU_STEP_BLOCKS = 2
SGU_ROWS = SGU_STEP_BLOCKS * SGU_BLOCK
N_GROUP = 8
N_HEAD = 8
HEAD_DIM = 128
D_FF = 2816
PLE_DIM = 256
IN_COLS = 8192
LN_EPS = 1e-5
RMS_EPS = 1e-6
ALPHA = 2.0 ** 0.25
N_CHIP = 4
N_DEV = 8

ADAM_LR = 0.001
ADAM_B1 = 0.9
ADAM_B2 = 0.999
ADAM_EPS = 1e-08
ADAM_WD = 0.01
ADAM_STEP = 10

VMEM_LIMIT = 56 * 1024 * 1024

NN = (((1,), (0,)), ((), ()))
NT = (((1,), (1,)), ((), ()))
TN = (((0,), (0,)), ((), ()))


def _pc(body, *, name, out_shape, grid=None, in_specs=None, out_specs=None, scratch=(),
        sem=None, nsp=0, vmem=VMEM_LIMIT, aliases=None):
    params = dict(vmem_limit_bytes=vmem)
    if sem is not None:
        params["dimension_semantics"] = sem
    kw = dict(name=name, out_shape=out_shape, compiler_params=pltpu.CompilerParams(**params))
    if aliases:
        kw["input_output_aliases"] = aliases
    if nsp:
        kw["grid_spec"] = pltpu.PrefetchScalarGridSpec(
            num_scalar_prefetch=nsp, grid=grid, in_specs=in_specs, out_specs=out_specs,
            scratch_shapes=list(scratch))
    else:
        if grid is not None:
            kw["grid"] = grid
        if in_specs is not None:
            kw["in_specs"] = in_specs
            kw["out_specs"] = out_specs
        kw["scratch_shapes"] = list(scratch)
    return pl.pallas_call(body, **kw)


def _dot(a, b, dims=NN):
    return lax.dot_general(a.astype(BF16), b.astype(BF16), dims, preferred_element_type=F32)


def _dot32(a, b, dims=NN):
    return lax.dot_general(a, b, dims, precision=HIGHEST, preferred_element_type=F32)


def _sig(x):
    return 1.0 / (1.0 + jnp.exp(-x))


_GC = 0.7978845608028654
_GA = 0.044715


def _gelu(x):
    return 0.5 * x * (1.0 + jnp.tanh(_GC * (x + _GA * x * x * x)))


def _gelu_and_grad(x):
    t = jnp.tanh(_GC * (x + _GA * x * x * x))
    g = 0.5 * x * (1.0 + t)
    dg = 0.5 * (1.0 + t) + 0.5 * x * (1.0 - t * t) * _GC * (1.0 + 3.0 * _GA * x * x)
    return g, dg


def _ln_stats(r):
    mu = jnp.mean(r, axis=-1, keepdims=True)
    xc = r - mu
    var = jnp.mean(xc * xc, axis=-1, keepdims=True)
    rstd = lax.rsqrt(var + LN_EPS)
    return xc * rstd, rstd


def _ln_bwd(dxh, xh, rstd):
    m1 = jnp.mean(dxh, axis=-1, keepdims=True)
    m2 = jnp.mean(dxh * xh, axis=-1, keepdims=True)
    return rstd * (dxh - m1 - xh * m2)


def _colsum8(v):
    return jnp.broadcast_to(jnp.sum(v, axis=0, keepdims=True), (8, v.shape[1]))


def _adamw(w, g, m, v):
    m2 = ADAM_B1 * m + (1.0 - ADAM_B1) * g
    v2 = ADAM_B2 * v + (1.0 - ADAM_B2) * (g * g)
    m_hat = m2 / (1.0 - ADAM_B1 ** ADAM_STEP)
    v_hat = v2 / (1.0 - ADAM_B2 ** ADAM_STEP)
    delta = -ADAM_LR * (m_hat / (jnp.sqrt(v_hat) + ADAM_EPS) + ADAM_WD * w)
    return delta, m2, v2


def _row_tile(rows, cols, itemsize=4, budget=1 << 20, mult=8):
    best = mult
    for tr in range(mult, rows + 1, mult):
        if rows % tr == 0 and tr * cols * itemsize <= budget:
            best = tr
    return best


def _mm(name, a, b, dims, grid, a_spec, b_spec, out_shape, o_spec):
    out_dtype = out_shape.dtype

    def body(a_ref, b_ref, o_ref):
        o_ref[...] = _dot(a_ref[...], b_ref[...], dims).astype(out_dtype)

    return _pc(body, name=name, out_shape=out_shape, grid=grid, in_specs=[a_spec, b_spec],
               out_specs=o_spec, sem=("parallel", "parallel"))(a, b)


class _Comm:
    def __init__(self, ins, out_shapes, sems, start, finish):
        self.ins, self.out_shapes, self.sems = list(ins), list(out_shapes), list(sems)
        self.start, self.finish = start, finish


def _hosted_call(body, comm, first, last, *, name, out_shape, grid, in_specs, out_specs, scratch, sem,
                 args, aliases=None):
    n_in, n_out, n_scr = len(in_specs), len(out_shape), len(scratch)
    nci, nco = len(comm.ins), len(comm.out_shapes)

    def wrapped(*refs):
        pos = n_in
        own_in, c_in = refs[:pos], refs[pos:pos + nci]
        pos += nci
        own_out, c_out = refs[pos:pos + n_out], refs[pos + n_out:pos + n_out + nco]
        pos += n_out + nco
        own_scr, c_sem = refs[pos:pos + n_scr], refs[pos + n_scr:]

        @pl.when(first())
        def _():
            comm.start(c_in, c_out, c_sem)

        body(*own_in, *own_out, *own_scr)

        @pl.when(last())
        def _():
            comm.finish(c_in, c_out, c_sem)

    return _pc(wrapped, name=name, out_shape=tuple(out_shape) + tuple(comm.out_shapes), grid=grid,
               in_specs=list(in_specs) + [ANY] * nci, out_specs=tuple(out_specs) + tuple([ANY] * nco),
               scratch=list(scratch) + comm.sems, sem=sem, aliases=aliases)(*args, *comm.ins)


def _grid1_call(body, comm, n, *, name, out_shape, in_specs, out_specs, scratch, args, aliases=None):
    if comm is None:
        return _pc(body, name=name, out_shape=out_shape, grid=(n,), in_specs=in_specs, out_specs=out_specs,
                   scratch=scratch, sem=("arbitrary",), aliases=aliases)(*args), ()
    res = _hosted_call(body, comm, lambda: pl.program_id(0) == 0, lambda: pl.program_id(0) == n - 1,
                       name=name, out_shape=out_shape, grid=(n,), in_specs=in_specs,
                       out_specs=out_specs, scratch=scratch, sem=("arbitrary",), args=args, aliases=aliases)
    return res[:len(out_shape)], res[len(out_shape):]


def _run_comm(name, comm):
    nci, nco = len(comm.ins), len(comm.out_shapes)

    def body(*refs):
        c_in, c_out, c_sem = refs[:nci], refs[nci:nci + nco], refs[nci + nco:]
        comm.start(c_in, c_out, c_sem)
        comm.finish(c_in, c_out, c_sem)

    return _pc(body, name=name, out_shape=tuple(comm.out_shapes), in_specs=[ANY] * nci,
               out_specs=tuple([ANY] * nco), scratch=comm.sems)(*comm.ins)


def _mm_tn(name, a, b, tm, tn, stacked=False):
    t, m = a.shape
    _, n = b.shape
    if stacked:
        assert tm == m
        out_shape = jax.ShapeDtypeStruct((n // tn, m, tn), BF16)
        o_spec = pl.BlockSpec((None, tm, tn), lambda i, j: (j, 0, 0))
    else:
        out_shape = jax.ShapeDtypeStruct((m, n), BF16)
        o_spec = pl.BlockSpec((tm, tn), lambda i, j: (i, j))
    return _mm(name, a, b, TN, (m // tm, n // tn),
               pl.BlockSpec((t, tm), lambda i, j: (0, i)),
               pl.BlockSpec((t, tn), lambda i, j: (0, j)),
               out_shape, o_spec)


DH_SLOT = (2, 0, 1, 3)


def _dh_slot(j):
    return jnp.where(j == 3, 3, (j + 2) % 3)


def _in_proj_wgrad(x2b, dh, tm, tn):
    t, m = x2b.shape
    n = dh.shape[2]

    def body(a_ref, b_ref, o_ref):
        o_ref[...] = _dot(a_ref[...], b_ref[...], TN).astype(BF16)

    return _pc(body, name="in_proj_wgrad", out_shape=jax.ShapeDtypeStruct((N_CHIP, m, n), BF16),
               grid=(N_CHIP, m // tm, n // tn),
               in_specs=[pl.BlockSpec((t, tm), lambda j, i, k: (0, i)),
                         pl.BlockSpec((None, t, tn), lambda j, i, k: (_dh_slot(j), 0, k))],
               out_specs=pl.BlockSpec((None, tm, tn), lambda j, i, k: (j, i, k)),
               sem=("parallel", "parallel", "parallel"))(x2b, dh)


def _in_proj_xgrad(dh, win_st, dr1, tm, comm):
    t = dr1.shape[0]
    ni = t // tm

    def body(a_ref, b_ref, add_ref, o_ref, acc):
        j = pl.program_id(1)
        prod = _dot(a_ref[...], b_ref[...], NT)

        @pl.when(j == 0)
        def _():
            acc[...] = prod + ALPHA * add_ref[...]

        @pl.when((j > 0) & (j < N_CHIP - 1))
        def _():
            acc[...] += prod

        @pl.when(j == N_CHIP - 1)
        def _():
            o_ref[...] = acc[...] + prod

    tile = pl.BlockSpec((tm, D_MODEL), lambda i, j: (i, 0))
    res = _hosted_call(body, comm,
                       lambda: (pl.program_id(0) == 0) & (pl.program_id(1) == 0),
                       lambda: (pl.program_id(0) == ni - 1) & (pl.program_id(1) == N_CHIP - 1),
                       name="in_proj_xgrad", out_shape=(jax.ShapeDtypeStruct((t, D_MODEL), F32),),
                       grid=(ni, N_CHIP),
                       in_specs=[pl.BlockSpec((None, tm, 2 * D_MODEL), lambda i, j: (_dh_slot(j), i, 0)),
                                 pl.BlockSpec((None, D_MODEL, 2 * D_MODEL), lambda i, j: (j, 0, 0)),
                                 tile],
                       out_specs=(tile,), scratch=[pltpu.VMEM((tm, D_MODEL), F32)],
                       sem=("arbitrary", "arbitrary"), args=[dh, win_st, dr1])
    return res[0], res[1:]


def _sgu_mixed(v, wm_ref, bsb_ref, gv, bv):
    gl, dgl = _gelu_and_grad(v)
    vh, rstd = _ln_stats(gl)
    vn = vh * gv + bv
    mixed = []
    for g in range(N_GROUP):
        sl = slice(g * 128, (g + 1) * 128)
        mixed.append(_dot(wm_ref[g], vn[:, sl]) + bsb_ref[g])
    return dgl, vh, rstd, vn, mixed


def _sgu_fwd(h, wm, bsb, gv, bv, comm=None):
    t = h.shape[0]

    def body(u_ref, v_ref, wm_ref, bsb_ref, gv_ref, bv_ref, ya_ref):
        for bb in range(SGU_STEP_BLOCKS):
            rows = slice(bb * SGU_BLOCK, (bb + 1) * SGU_BLOCK)
            u = u_ref[rows, :].astype(F32)
            _, _, _, _, mixed = _sgu_mixed(v_ref[rows, :].astype(F32), wm_ref, bsb_ref, gv_ref[...],
                                           bv_ref[...])
            gu = _gelu(u)
            for g in range(N_GROUP):
                sl = slice(g * 128, (g + 1) * 128)
                ya_ref[rows, sl] = (gu[:, sl] * mixed[g]).astype(BF16)

    full3 = pl.BlockSpec((N_GROUP, 128, 128), lambda i: (0, 0, 0))
    vec = pl.BlockSpec((1, D_MODEL), lambda i: (0, 0))
    (ya,), extra = _grid1_call(
        body, comm, t // SGU_ROWS, name="sgu_fwd",
        out_shape=(jax.ShapeDtypeStruct((t, D_MODEL), BF16),),
        in_specs=[pl.BlockSpec((SGU_ROWS, D_MODEL), lambda i: (i, 0)),
                  pl.BlockSpec((SGU_ROWS, D_MODEL), lambda i: (i, 1)),
                  full3, full3, vec, vec],
        out_specs=(pl.BlockSpec((SGU_ROWS, D_MODEL), lambda i: (i, 0)),),
        scratch=[], args=(h, h, wm, bsb, gv, bv))
    return ya, extra


def _sgu_bwd(h, dya, wm, wmt, bsb, gv, bv, maskf, dh_buf, comm=None):
    t = h.shape[0]
    nb = t // SGU_ROWS

    def body(u_ref, v_ref, dya_ref, wm_ref, wmt_ref, bsb_ref, gv_ref, bv_ref, mask_ref, dh_buf_ref,
             dh_ref, dws_ref, dbs_ref, dgv_ref, dbv_ref, dmix_acc):
        i = pl.program_id(0)

        @pl.when(i == 0)
        def _():
            dws_ref[...] = jnp.zeros_like(dws_ref)
            dgv_ref[...] = jnp.zeros_like(dgv_ref)
            dbv_ref[...] = jnp.zeros_like(dbv_ref)
            dmix_acc[...] = jnp.zeros_like(dmix_acc)

        gvv = gv_ref[...]
        for bb in range(SGU_STEP_BLOCKS):
            rows = slice(bb * SGU_BLOCK, (bb + 1) * SGU_BLOCK)
            u = u_ref[rows, :].astype(F32)
            dgl_v, vh, rstd, vn, mixed = _sgu_mixed(v_ref[rows, :].astype(F32), wm_ref, bsb_ref, gvv,
                                                    bv_ref[...])
            gu, dgl_u = _gelu_and_grad(u)
            dya_v = dya_ref[rows, :].astype(F32)
            dvn_parts = []
            for g in range(N_GROUP):
                sl = slice(g * 128, (g + 1) * 128)
                d_y = dya_v[:, sl]
                dh_ref[rows, sl] = (d_y * mixed[g] * dgl_u[:, sl]).astype(BF16)
                d_mixed = d_y * gu[:, sl]
                dmix_acc[g] += d_mixed
                dws_ref[g] += _dot(d_mixed, vn[:, sl], NT) * mask_ref[...]
                dvn_parts.append(_dot(wmt_ref[g], d_mixed))
            dvn = jnp.concatenate(dvn_parts, axis=1)
            dgv_ref[...] += _colsum8(dvn * vh)
            dbv_ref[...] += _colsum8(dvn)
            d_gl = _ln_bwd(dvn * gvv, vh, rstd)
            dh_ref[rows, D_MODEL:] = (d_gl * dgl_v).astype(BF16)

        @pl.when(i == nb - 1)
        def _():
            rowid = lax.broadcasted_iota(jnp.int32, (8, 128), 0)
            ones = jnp.ones((8, 128), F32)
            acc = jnp.zeros((8, 128), F32)
            for g in range(N_GROUP):
                rs = _dot32(ones, dmix_acc[g], NT)
                acc = jnp.where(rowid == g, rs, acc)
            dbs_ref[...] = acc

    full3 = pl.BlockSpec((N_GROUP, 128, 128), lambda i: (0, 0, 0))
    vec = pl.BlockSpec((1, D_MODEL), lambda i: (0, 0))
    acc8 = pl.BlockSpec((8, D_MODEL), lambda i: (0, 0))
    return _grid1_call(
        body, comm, nb, name="sgu_bwd",
        out_shape=(jax.ShapeDtypeStruct(dh_buf.shape, BF16),
                   jax.ShapeDtypeStruct((N_GROUP, 128, 128), F32),
                   jax.ShapeDtypeStruct((8, 128), F32),
                   jax.ShapeDtypeStruct((8, D_MODEL), F32),
                   jax.ShapeDtypeStruct((8, D_MODEL), F32)),
        in_specs=[pl.BlockSpec((SGU_ROWS, D_MODEL), lambda i: (i, 0)),
                  pl.BlockSpec((SGU_ROWS, D_MODEL), lambda i: (i, 1)),
                  pl.BlockSpec((SGU_ROWS, D_MODEL), lambda i: (i, 0)),
                  full3, full3, full3, vec, vec,
                  pl.BlockSpec((128, 128), lambda i: (0, 0)), ANY],
        out_specs=(pl.BlockSpec((None, SGU_ROWS, 2 * D_MODEL), lambda i: (DH_SLOT[0], i, 0)),
                   full3, pl.BlockSpec((8, 128), lambda i: (0, 0)), acc8, acc8),
        scratch=[pltpu.VMEM((N_GROUP, 128, 128), F32)],
        args=(h, h, dya, wm, wmt, bsb, gv, bv, maskf, dh_buf), aliases={9: 0})


def _tri_masks():
    row = lax.broadcasted_iota(jnp.int32, (CHUNK, CHUNK), 0)
    col = lax.broadcasted_iota(jnp.int32, (CHUNK, CHUNK), 1)
    return col <= row, col >= row


def _heads(v):
    return [v[:, hd * HEAD_DIM:(hd + 1) * HEAD_DIM] for hd in range(N_HEAD)]


def _tri_cumsum(tri_bf, v):
    hi = v.astype(BF16)
    r = v - hi.astype(F32)
    mid = r.astype(BF16)
    lo = (r - mid.astype(F32)).astype(BF16)
    return _dot(tri_bf, hi) + _dot(tri_bf, mid) + _dot(tri_bf, lo)


def _hgrn_chunk(q, fp, ii, lb, st_heads, causal):
    sg = _sig(fp)
    f = lb + (1.0 - lb) * sg
    k = 1.0 - f
    c = _tri_cumsum(causal.astype(BF16), jnp.log(f))
    ec = jnp.exp(c)
    en = jnp.exp(-c)
    sq = _sig(q)
    qt = q * sq * ec
    kt = k * en
    ecl = jnp.exp(c[CHUNK - 1:CHUNK, :])
    kk = kt * ecl
    qtb, ktb, iib, kkb = qt.astype(BF16), kt.astype(BF16), ii.astype(BF16), kk.astype(BF16)
    attn, o = [], []
    for hd, (qh, kh, ih) in enumerate(zip(_heads(qtb), _heads(ktb), _heads(iib))):
        a = jnp.where(causal, _dot(qh, kh, NT), 0.0).astype(BF16)
        attn.append(a)
        o.append(_dot(a, ih) + _dot(qh, st_heads[hd], NT))
    return dict(sg=sg, f=f, k=k, ec=ec, en=en, sq=sq, ecl=ecl, kk=kk, qtb=qtb, ktb=ktb, iib=iib,
                kkb=kkb, attn=attn, o=o)


def _rms_heads(o_heads):
    rinv = [lax.rsqrt(jnp.mean(o * o, axis=-1, keepdims=True) + RMS_EPS) for o in o_heads]
    return rinv, jnp.concatenate([o * r for o, r in zip(o_heads, rinv)], axis=1)


HG_CHUNKS = 8
HG_ROWS = HG_CHUNKS * CHUNK


def _hgrn_fwd(h, logits, gn, comm=None):
    t = h.shape[0]
    nb = t // HG_ROWS

    def body(q_ref, f_ref, i_ref, og_ref, lg_ref, gn_ref, yb_ref, st_ref, state):
        @pl.when(pl.program_id(0) == 0)
        def _():
            state[...] = jnp.zeros_like(state)

        causal, _ = _tri_masks()
        lb = _sig(lg_ref[0:1, :] - lg_ref[1:2, :])
        gnv = gn_ref[...]
        st = [state[hd] for hd in range(N_HEAD)]
        for cc in range(HG_CHUNKS):
            rows = slice(cc * CHUNK, (cc + 1) * CHUNK)
            og = og_ref[rows, :].astype(F32)
            r = _hgrn_chunk(q_ref[rows, :].astype(F32), f_ref[rows, :].astype(F32),
                            i_ref[rows, :].astype(F32), lb, [s.astype(BF16) for s in st], causal)
            _, on = _rms_heads(r["o"])
            yb_ref[rows, :] = (on * gnv * (og * _sig(og))).astype(BF16)
            for hd in range(N_HEAD):
                st_ref[cc, hd] = st[hd]
            st = [s * e + _dot(ih, kh, TN)
                  for s, e, ih, kh in zip(st, _heads(r["ecl"]), _heads(r["iib"]), _heads(r["kkb"]))]
        for hd in range(N_HEAD):
            state[hd] = st[hd]

    def col(k):
        return pl.BlockSpec((HG_ROWS, D_MODEL), lambda ci: (ci, k))

    return _grid1_call(body, comm, nb, name="hgrn_fwd",
                       out_shape=(jax.ShapeDtypeStruct((t, D_MODEL), BF16),
                                  jax.ShapeDtypeStruct((t // CHUNK, N_HEAD, HEAD_DIM, HEAD_DIM), F32)),
                       in_specs=[col(2), col(3), col(4), col(5),
                                 pl.BlockSpec((2, D_MODEL), lambda ci: (0, 0)),
                                 pl.BlockSpec((1, D_MODEL), lambda ci: (0, 0))],
                       out_specs=(pl.BlockSpec((HG_ROWS, D_MODEL), lambda ci: (ci, 0)),
                                  pl.BlockSpec((HG_CHUNKS, N_HEAD, HEAD_DIM, HEAD_DIM),
                                               lambda ci: (ci, 0, 0, 0))),
                       scratch=[pltpu.VMEM((N_HEAD, HEAD_DIM, HEAD_DIM), F32)],
                       args=(h, h, h, h, logits, gn))


def _hgrn_chunk_bwd(q, fp, ii, og, dy, gnv, lb, st, dsn, causal, anti):
    stb = [s.astype(BF16) for s in st]
    dsnb = [s.astype(BF16) for s in dsn]
    r = _hgrn_chunk(q, fp, ii, lb, stb, causal)
    rinv, on = _rms_heads(r["o"])
    so = _sig(og)
    sil = og * so
    d_og = dy * on * gnv * (so * (1.0 + og * (1.0 - so)))
    d_on = dy * gnv * sil
    d_ob = jnp.concatenate(
        [ri * (dn - oh * jnp.mean(dn * oh, axis=-1, keepdims=True))
         for ri, dn, oh in zip(rinv, _heads(d_on), _heads(on))], axis=1).astype(BF16)
    d_i, d_qt, d_kt, d_kk, d_st, st_dsn = [], [], [], [], [], []
    ecl = _heads(r["ecl"])
    for hd, (dh, qh, kh, ih, kkh) in enumerate(zip(_heads(d_ob), _heads(r["qtb"]), _heads(r["ktb"]),
                                                   _heads(r["iib"]), _heads(r["kkb"]))):
        d_attn = jnp.where(causal, _dot(dh, ih, NT), 0.0).astype(BF16)
        d_i.append(_dot(r["attn"][hd], dh, TN) + _dot(kkh, dsnb[hd], NT))
        d_qt.append(_dot(d_attn, kh) + _dot(dh, stb[hd]))
        d_kt.append(_dot(d_attn, qh, TN))
        d_kk.append(_dot(ih, dsnb[hd]))
        d_st.append(_dot(dh, qh, TN) + dsn[hd] * ecl[hd])
        st_dsn.append(jnp.sum(st[hd] * dsn[hd], axis=0, keepdims=True))
    d_qt = jnp.concatenate(d_qt, axis=1)
    d_kt = jnp.concatenate(d_kt, axis=1)
    d_kk = jnp.concatenate(d_kk, axis=1)
    kk = r["kk"]
    d_cl = r["ecl"] * jnp.concatenate(st_dsn, axis=1) + jnp.sum(kk * d_kk, axis=0, keepdims=True)
    d_k = (d_kk * r["ecl"] + d_kt) * r["en"]
    d_c = d_qt * r["qtb"].astype(F32) - d_kt * r["ktb"].astype(F32) - d_kk * kk
    rowid = lax.broadcasted_iota(jnp.int32, (CHUNK, D_MODEL), 0)
    d_c = d_c + jnp.where(rowid == CHUNK - 1, d_cl, 0.0)
    d_lf = _tri_cumsum(anti.astype(BF16), d_c)
    d_f = d_lf / r["f"] - d_k
    sg, sq = r["sg"], r["sq"]
    d_q = d_qt * r["ec"] * (sq * (1.0 + q * (1.0 - sq)))
    d_fp = d_f * (1.0 - lb) * sg * (1.0 - sg)
    return (d_q, d_fp, jnp.concatenate(d_i, axis=1), d_og, d_st,
            _colsum8(dy * on * sil), _colsum8(d_f * (1.0 - sg)))


def _hgrn_bwd(h, dyb, st_all, logits, gn, dh_buf, comm=None):
    t = h.shape[0]
    nb = t // HG_ROWS

    def body(q_ref, f_ref, i_ref, og_ref, dyb_ref, st_ref, lg_ref, gn_ref, dh_buf_ref,
             dh_ref, dlb_ref, dgn_ref, dstate):
        @pl.when(pl.program_id(0) == 0)
        def _():
            dstate[...] = jnp.zeros_like(dstate)
            dlb_ref[...] = jnp.zeros_like(dlb_ref)
            dgn_ref[...] = jnp.zeros_like(dgn_ref)

        causal, anti = _tri_masks()
        lb = _sig(lg_ref[0:1, :] - lg_ref[1:2, :])
        gnv = gn_ref[...]
        dsn = [dstate[hd] for hd in range(N_HEAD)]
        dgn_acc = jnp.zeros((8, D_MODEL), F32)
        dlb_acc = jnp.zeros((8, D_MODEL), F32)
        for cc in reversed(range(HG_CHUNKS)):
            rows = slice(cc * CHUNK, (cc + 1) * CHUNK)
            d_q, d_fp, d_i, d_og, dsn, dgn_c, dlb_c = _hgrn_chunk_bwd(
                q_ref[rows, :].astype(F32), f_ref[rows, :].astype(F32), i_ref[rows, :].astype(F32),
                og_ref[rows, :].astype(F32), dyb_ref[rows, :].astype(F32), gnv, lb,
                [st_ref[cc, hd] for hd in range(N_HEAD)], dsn, causal, anti)
            dgn_acc = dgn_acc + dgn_c
            dlb_acc = dlb_acc + dlb_c
            dh_ref[0, rows, :D_MODEL] = d_q.astype(BF16)
            dh_ref[0, rows, D_MODEL:] = d_fp.astype(BF16)
            dh_ref[1, rows, :D_MODEL] = d_i.astype(BF16)
            dh_ref[1, rows, D_MODEL:] = d_og.astype(BF16)
        dgn_ref[...] += dgn_acc
        dlb_ref[...] += dlb_acc
        for hd in range(N_HEAD):
            dstate[hd] = dsn[hd]

    def col(k):
        return pl.BlockSpec((HG_ROWS, D_MODEL), lambda ci: (nb - 1 - ci, k))

    acc8 = pl.BlockSpec((8, D_MODEL), lambda ci: (0, 0))
    pair = pl.BlockSpec((2, HG_ROWS, 2 * D_MODEL), lambda ci: (0, nb - 1 - ci, 0))
    return _grid1_call(body, comm, nb, name="hgrn_bwd",
                       out_shape=(jax.ShapeDtypeStruct(dh_buf.shape, BF16),
                                  jax.ShapeDtypeStruct((8, D_MODEL), F32),
                                  jax.ShapeDtypeStruct((8, D_MODEL), F32)),
                       in_specs=[col(2), col(3), col(4), col(5),
                                 pl.BlockSpec((HG_ROWS, D_MODEL), lambda ci: (nb - 1 - ci, 0)),
                                 pl.BlockSpec((HG_CHUNKS, N_HEAD, HEAD_DIM, HEAD_DIM),
                                              lambda ci: (nb - 1 - ci, 0, 0, 0)),
                                 pl.BlockSpec((2, D_MODEL), lambda ci: (0, 0)),
                                 pl.BlockSpec((1, D_MODEL), lambda ci: (0, 0)), ANY],
                       out_specs=(pair, acc8, acc8),
                       scratch=[pltpu.VMEM((N_HEAD, HEAD_DIM, HEAD_DIM), F32)],
                       args=(h, h, h, h, dyb, st_all, logits, gn, dh_buf), aliases={8: 0})


def _mix_fwd(ya, yb, h, x, wb0, wb1, wo, g1, b1, tm, comm=None):
    t = x.shape[0]

    def body(ya_ref, yb_ref, ga_ref, gb_ref, x_ref, wb0_ref, wb1_ref, wo_ref, g1_ref, b1_ref,
             r1_ref, a_ref, b_ref, m_ref, x1_ref):
        a = _dot(ya_ref[...], wb0_ref[...])
        b = _dot(yb_ref[...], wb1_ref[...])
        m = _sig(ga_ref[...].astype(F32)) * a + _sig(gb_ref[...].astype(F32)) * b
        r1 = ALPHA * x_ref[...] + _dot(m, wo_ref[...])
        xh, _ = _ln_stats(r1)
        r1_ref[...] = r1
        a_ref[...] = a.astype(BF16)
        b_ref[...] = b.astype(BF16)
        m_ref[...] = m.astype(BF16)
        x1_ref[...] = (xh * g1_ref[...] + b1_ref[...]).astype(BF16)

    tile = pl.BlockSpec((tm, D_MODEL), lambda i: (i, 0))
    wsp = pl.BlockSpec((D_MODEL, D_MODEL), lambda i: (0, 0))
    vec = pl.BlockSpec((1, D_MODEL), lambda i: (0, 0))
    f32o = jax.ShapeDtypeStruct((t, D_MODEL), F32)
    bfo = jax.ShapeDtypeStruct((t, D_MODEL), BF16)
    return _grid1_call(body, comm, t // tm, name="mix_fwd", out_shape=(f32o, bfo, bfo, bfo, bfo),
                       in_specs=[tile, tile,
                                 pl.BlockSpec((tm, D_MODEL), lambda i: (i, 6)),
                                 pl.BlockSpec((tm, D_MODEL), lambda i: (i, 7)),
                                 tile, wsp, wsp, wsp, vec, vec],
                       out_specs=(tile, tile, tile, tile, tile),
                       scratch=[], args=(ya, yb, h, h, x, wb0, wb1, wo, g1, b1))


def _mix_bwd(dr1, h, a, b, wo, wb0, wb1, tm):
    t = dr1.shape[0]

    def body(dr1_ref, ga_ref, gb_ref, a_ref, b_ref, wo_ref, wb0_ref, wb1_ref,
             da_ref, db_ref, dh3_ref, dya_ref, dyb_ref):
        d_m = _dot(dr1_ref[...], wo_ref[...], NT)
        sa = _sig(ga_ref[...].astype(F32))
        sb = _sig(gb_ref[...].astype(F32))
        d_a = (d_m * sa).astype(BF16)
        d_b = (d_m * sb).astype(BF16)
        da_ref[...] = d_a
        db_ref[...] = d_b
        dh3_ref[:, :D_MODEL] = (d_m * a_ref[...].astype(F32) * sa * (1.0 - sa)).astype(BF16)
        dh3_ref[:, D_MODEL:] = (d_m * b_ref[...].astype(F32) * sb * (1.0 - sb)).astype(BF16)
        dya_ref[...] = _dot(d_a, wb0_ref[...], NT).astype(BF16)
        dyb_ref[...] = _dot(d_b, wb1_ref[...], NT).astype(BF16)

    tile = pl.BlockSpec((tm, D_MODEL), lambda i: (i, 0))
    wsp = pl.BlockSpec((D_MODEL, D_MODEL), lambda i: (0, 0))
    f32o = jax.ShapeDtypeStruct((t, D_MODEL), F32)
    bfo = jax.ShapeDtypeStruct((t, D_MODEL), BF16)
    return _pc(body, name="mix_bwd",
               out_shape=(bfo, bfo, jax.ShapeDtypeStruct((N_CHIP, t, 2 * D_MODEL), BF16), bfo, bfo),
               grid=(t // tm,),
               in_specs=[tile,
                         pl.BlockSpec((tm, D_MODEL), lambda i: (i, 6)),
                         pl.BlockSpec((tm, D_MODEL), lambda i: (i, 7)),
                         tile, tile, wsp, wsp, wsp],
               out_specs=(tile, tile, pl.BlockSpec((None, tm, 2 * D_MODEL), lambda i: (DH_SLOT[3], i, 0)),
                          tile, tile),
               sem=("parallel",))(dr1, h, h, a, b, wo, wb0, wb1)


FF_TILE = 1408
FF_NJ = D_FF // FF_TILE


def _shift_down(v, k):
    return pltpu.roll(v, k, 0)


def _shift_up(v, k):
    return pltpu.roll(v, v.shape[0] - k, 0)


HALO = 16
FF_PIECES = ((0, 768), (768, FF_TILE))


def _ffn_up_act(x1b, wup_st, convw, convb, tm, comm):
    t = x1b.shape[0]
    ni = t // tm
    nth = tm // HALO

    def body(x_ref, xp_ref, wg_ref, wv_ref, cw_ref, cb_ref, h2_ref, act_ref):
        wg = wg_ref[...]
        gate = _dot(x_ref[...], wg).astype(BF16)
        val = _dot(x_ref[...], wv_ref[...]).astype(BF16)
        prev = (_dot(xp_ref[...], wg) * (pl.program_id(0) > 0).astype(F32)).astype(BF16)
        h2_ref[0] = gate
        h2_ref[1] = val
        ext = jnp.concatenate([prev.astype(F32), gate.astype(F32)], axis=0)
        gc = (cw_ref[0:1, :] * _shift_down(ext, 2) + cw_ref[1:2, :] * _shift_down(ext, 1)
              + cw_ref[2:3, :] * ext + cb_ref[...])[HALO:, :]
        act_ref[...] = (_gelu(gc) * val.astype(F32)).astype(BF16)

    res = _hosted_call(
        body, comm,
        lambda: (pl.program_id(0) == 0) & (pl.program_id(1) == 0),
        lambda: (pl.program_id(0) == ni - 1) & (pl.program_id(1) == FF_NJ - 1),
        name="ffn_up",
        out_shape=(jax.ShapeDtypeStruct((2, t, D_FF), BF16), jax.ShapeDtypeStruct((t, D_FF), BF16)),
        grid=(ni, FF_NJ),
        in_specs=[pl.BlockSpec((tm, D_MODEL), lambda i, j: (i, 0)),
                  pl.BlockSpec((HALO, D_MODEL), lambda i, j: (jnp.maximum(i * nth - 1, 0), 0)),
                  pl.BlockSpec((None, D_MODEL, FF_TILE), lambda i, j: (j, 0, 0)),
                  pl.BlockSpec((None, D_MODEL, FF_TILE), lambda i, j: (j + FF_NJ, 0, 0)),
                  pl.BlockSpec((3, FF_TILE), lambda i, j: (0, j)),
                  pl.BlockSpec((1, FF_TILE), lambda i, j: (0, j))],
        out_specs=(pl.BlockSpec((2, tm, FF_TILE), lambda i, j: (0, i, j)),
                   pl.BlockSpec((tm, FF_TILE), lambda i, j: (i, j))),
        scratch=[], sem=("arbitrary", "arbitrary"),
        args=(x1b, x1b, wup_st, wup_st, convw, convb))
    return res[0], res[1], res[2:]


def _out_fwd_bwd(act, x1b, r1, p2, tgt, wd, wpg, wpp, g1, b1, g2, b2, tm):
    t = r1.shape[0]

    def body(act_ref, x1b_ref, r1_ref, p_ref, tgt_ref, wd_ref, wpg_ref, wpp_ref,
             g1_ref, b1_ref, g2_ref, b2_ref,
             dr2_ref, dpg_ref, dpp_ref, loss_ref, dg2_ref, db2_ref):
        i = pl.program_id(0)

        @pl.when(i == 0)
        def _():
            loss_ref[...] = jnp.zeros_like(loss_ref)
            dg2_ref[...] = jnp.zeros_like(dg2_ref)
            db2_ref[...] = jnp.zeros_like(db2_ref)

        ffn = _dot(act_ref[...], wd_ref[...])
        pg = _dot(x1b_ref[...], wpg_ref[...])
        pp = _dot(p_ref[...], wpp_ref[...])
        s = _sig(pg)
        xh1, _ = _ln_stats(r1_ref[...])
        x1 = xh1 * g1_ref[...] + b1_ref[...]
        r2 = ALPHA * x1 + ffn + s * pp
        xh2, rstd2 = _ln_stats(r2)
        g2v = g2_ref[...]
        diff = xh2 * g2v + b2_ref[...] - tgt_ref[...]
        part = jnp.sum(jnp.sum(diff * diff, axis=1, keepdims=True), axis=0, keepdims=True)
        loss_ref[...] += jnp.broadcast_to(part * (0.5 / D_MODEL), loss_ref.shape)
        dy = diff * (1.0 / D_MODEL)
        dg2_ref[...] += _colsum8(dy * xh2)
        db2_ref[...] += _colsum8(dy)
        dr2 = _ln_bwd(dy * g2v, xh2, rstd2)
        dr2_ref[...] = dr2
        dpg_ref[...] = (dr2 * pp * s * (1.0 - s)).astype(BF16)
        dpp_ref[...] = (dr2 * s).astype(BF16)

    tile = pl.BlockSpec((tm, D_MODEL), lambda i: (i, 0))
    vec = pl.BlockSpec((1, D_MODEL), lambda i: (0, 0))
    acc8 = pl.BlockSpec((8, D_MODEL), lambda i: (0, 0))
    acc_shape = jax.ShapeDtypeStruct((8, D_MODEL), F32)
    return _pc(body, name="out_fwd_bwd",
               out_shape=(jax.ShapeDtypeStruct((t, D_MODEL), F32),
                          jax.ShapeDtypeStruct((t, D_MODEL), BF16),
                          jax.ShapeDtypeStruct((t, D_MODEL), BF16),
                          acc_shape, acc_shape, acc_shape),
               grid=(t // tm,),
               in_specs=[pl.BlockSpec((tm, D_FF), lambda i: (i, 0)), tile, tile,
                         pl.BlockSpec((tm, PLE_DIM), lambda i: (i, 0)), tile,
                         pl.BlockSpec((D_FF, D_MODEL), lambda i: (0, 0)),
                         pl.BlockSpec((D_MODEL, D_MODEL), lambda i: (0, 0)),
                         pl.BlockSpec((PLE_DIM, D_MODEL), lambda i: (0, 0)),
                         vec, vec, vec, vec],
               out_specs=(tile, tile, tile, acc8, acc8, acc8),
               sem=("arbitrary",))(act, x1b, r1, p2, tgt, wd, wpg, wpp, g1, b1, g2, b2)


def _ffn_bwd(h2, dr2, wd, wup_st, dpg, wpg, r1, g1, convw, convb, tm):
    t = r1.shape[0]
    ni = t // tm
    nth = tm // HALO
    last_halo = t // HALO - 1
    main_rows = slice(HALO, HALO + tm)

    def body(g_ref, gp_ref, gn_ref, v_ref, vn_ref, dr2_ref, dr2n_ref, wd_ref, wug_ref, wuv_ref,
             cw_ref, cb_ref, dpg_ref, wpg_ref, r1_ref, g1_ref,
             dh2_ref, dr1_ref, dcw_ref, dcb_ref, dg1_ref, db1_ref, acc):
        i = pl.program_id(0)
        j = pl.program_id(1)

        @pl.when((i == 0) & (j == 0))
        def _():
            dcw_ref[...] = jnp.zeros_like(dcw_ref)
            dcb_ref[...] = jnp.zeros_like(dcb_ref)
            dg1_ref[...] = jnp.zeros_like(dg1_ref)
            db1_ref[...] = jnp.zeros_like(db1_ref)

        dr2v = dr2_ref[...].astype(BF16)
        dr2n = dr2n_ref[...].astype(BF16)
        first = (i > 0).astype(F32)
        more = (i < ni - 1).astype(F32)
        prod = None
        dcw_parts, dcb_parts = [], []
        for c0, c1 in FF_PIECES:
            pc = slice(c0, c1)
            zeros = jnp.zeros((HALO, c1 - c0), F32)
            da = _dot(dr2v, wd_ref[pc, :], NT)
            dnext = _dot(dr2n, wd_ref[pc, :], NT) * more
            ext = jnp.concatenate([gp_ref[:, pc].astype(F32) * first, g_ref[:, pc].astype(F32),
                                   gn_ref[:, pc].astype(F32)], axis=0)
            vext = jnp.concatenate([zeros, v_ref[:, pc].astype(F32), vn_ref[:, pc].astype(F32)], axis=0)
            dext = jnp.concatenate([zeros, da, dnext], axis=0)
            g2 = _shift_down(ext, 2)
            g1s = _shift_down(ext, 1)
            gc = cw_ref[0:1, pc] * g2 + cw_ref[1:2, pc] * g1s + cw_ref[2:3, pc] * ext + cb_ref[:, pc]
            gl, dgl = _gelu_and_grad(gc)
            d_gc = dext * vext * dgl
            d_gate = (cw_ref[2:3, pc] * d_gc + cw_ref[1:2, pc] * _shift_up(d_gc, 1)
                      + cw_ref[0:1, pc] * _shift_up(d_gc, 2))[main_rows, :].astype(BF16)
            d_val = (da * gl[main_rows, :]).astype(BF16)
            dh2_ref[0, :, pc] = d_gate
            dh2_ref[1, :, pc] = d_val
            dm = d_gc[main_rows, :]
            s0 = jnp.sum(dm * g2[main_rows, :], axis=0, keepdims=True)
            s1 = jnp.sum(dm * g1s[main_rows, :], axis=0, keepdims=True)
            s2 = jnp.sum(dm * ext[main_rows, :], axis=0, keepdims=True)
            rowid = lax.broadcasted_iota(jnp.int32, (8, c1 - c0), 0)
            dcw_parts.append(jnp.where(rowid == 0, s0, jnp.where(rowid == 1, s1,
                                                                 jnp.where(rowid == 2, s2, 0.0))))
            dcb_parts.append(_colsum8(dm))
            part = _dot(d_gate, wug_ref[:, pc], NT) + _dot(d_val, wuv_ref[:, pc], NT)
            prod = part if prod is None else prod + part
        dcw_part = jnp.concatenate(dcw_parts, axis=1)
        dcb_part = jnp.concatenate(dcb_parts, axis=1)
        for jj in range(FF_NJ):
            @pl.when(j == jj)
            def _(jj=jj):
                cols = slice(jj * FF_TILE, (jj + 1) * FF_TILE)
                dcw_ref[:, cols] += dcw_part
                dcb_ref[:, cols] += dcb_part

        @pl.when(j == 0)
        def _():
            acc[...] = prod

        @pl.when(j > 0)
        def _():
            acc[...] += prod

        @pl.when(j == FF_NJ - 1)
        def _():
            d_x1 = acc[...] + _dot(dpg_ref[...], wpg_ref[...], NT) + ALPHA * dr2_ref[...]
            xh, rstd = _ln_stats(r1_ref[...])
            dg1_ref[...] += _colsum8(d_x1 * xh)
            db1_ref[...] += _colsum8(d_x1)
            dr1_ref[...] = _ln_bwd(d_x1 * g1_ref[...], xh, rstd)

    def h2_main(part):
        return pl.BlockSpec((None, tm, FF_TILE), lambda i, j: (part, i, j))

    def h2_prev(part):
        return pl.BlockSpec((None, HALO, FF_TILE), lambda i, j: (part, jnp.maximum(i * nth - 1, 0), j))

    def h2_next(part):
        return pl.BlockSpec((None, HALO, FF_TILE),
                            lambda i, j: (part, jnp.minimum((i + 1) * nth, last_halo), j))

    tile = pl.BlockSpec((tm, D_MODEL), lambda i, j: (i, 0))
    acc8 = pl.BlockSpec((8, D_MODEL), lambda i, j: (0, 0))
    accff = pl.BlockSpec((8, D_FF), lambda i, j: (0, 0))
    acc_shape = jax.ShapeDtypeStruct((8, D_MODEL), F32)
    accff_shape = jax.ShapeDtypeStruct((8, D_FF), F32)
    return _pc(body, name="ffn_bwd",
               out_shape=(jax.ShapeDtypeStruct((2, t, D_FF), BF16),
                          jax.ShapeDtypeStruct((t, D_MODEL), F32),
                          accff_shape, accff_shape, acc_shape, acc_shape),
               grid=(ni, FF_NJ),
               in_specs=[h2_main(0), h2_prev(0), h2_next(0), h2_main(1), h2_next(1),
                         tile,
                         pl.BlockSpec((HALO, D_MODEL), lambda i, j: (jnp.minimum((i + 1) * nth, last_halo), 0)),
                         pl.BlockSpec((FF_TILE, D_MODEL), lambda i, j: (j, 0)),
                         pl.BlockSpec((None, D_MODEL, FF_TILE), lambda i, j: (j, 0, 0)),
                         pl.BlockSpec((None, D_MODEL, FF_TILE), lambda i, j: (j + FF_NJ, 0, 0)),
                         pl.BlockSpec((3, FF_TILE), lambda i, j: (0, j)),
                         pl.BlockSpec((1, FF_TILE), lambda i, j: (0, j)),
                         tile, pl.BlockSpec((D_MODEL, D_MODEL), lambda i, j: (0, 0)),
                         tile, pl.BlockSpec((1, D_MODEL), lambda i, j: (0, 0))],
               out_specs=(pl.BlockSpec((2, tm, FF_TILE), lambda i, j: (0, i, j)),
                          tile, accff, accff, acc8, acc8),
               scratch=[pltpu.VMEM((tm, D_MODEL), F32)],
               sem=("arbitrary", "arbitrary"))(h2, h2, h2, h2, h2, dr2, dr2, wd, wup_st, wup_st,
                                               convw, convb, dpg, wpg, r1, g1)


ANY = pl.BlockSpec(memory_space=pl.ANY)


def _chip_peers():
    x, y, c = lax.axis_index("x"), lax.axis_index("y"), lax.axis_index("c")
    return x, y, c, [(1 - x, y), (x, 1 - y), (1 - x, 1 - y)]


def _gather_comm(halved, whole=()):
    n, nw = len(halved), len(whole)

    def copies(ins, outs, sems):
        ici_send, ici_recv, d2d_send, d2d_recv, own_send, own_recv = sems
        x, y, c, peers = _chip_peers()
        me = 2 * x + y
        sibling = (x, y, 1 - c)
        own, ici, ici_wait, fwd, fwd_wait = [], [], [], [], []
        for ti in range(n + nw):
            src, dst = ins[ti], outs[ti]
            own.append(pltpu.make_async_remote_copy(
                src_ref=src, dst_ref=dst.at[me], send_sem=own_send.at[ti], recv_sem=own_recv.at[ti],
                device_id=sibling, device_id_type=MESH))
            for k, (px, py) in enumerate(peers):
                pk = 2 * px + py
                sem = dict(send_sem=ici_send.at[ti * 3 + k], recv_sem=ici_recv.at[ti * 3 + k],
                           device_id=(px, py, c), device_id_type=MESH)
                if ti < n:
                    ici.append(pltpu.make_async_remote_copy(src_ref=src.at[c], dst_ref=dst.at[me, c], **sem))
                    ici_wait.append(pltpu.make_async_remote_copy(src_ref=src.at[c], dst_ref=dst.at[pk, c], **sem))
                    dsem = dict(send_sem=d2d_send.at[ti * 3 + k], recv_sem=d2d_recv.at[ti * 3 + k],
                                device_id=sibling, device_id_type=MESH)
                    fwd.append(pltpu.make_async_remote_copy(src_ref=dst.at[pk, c], dst_ref=dst.at[pk, c], **dsem))
                    fwd_wait.append(pltpu.make_async_remote_copy(
                        src_ref=dst.at[pk, 1 - c], dst_ref=dst.at[pk, 1 - c], **dsem))
                else:
                    ici.append(pltpu.make_async_remote_copy(src_ref=src, dst_ref=dst.at[me], **sem))
                    ici_wait.append(pltpu.make_async_remote_copy(src_ref=src, dst_ref=dst.at[pk], **sem))
        return own, ici, ici_wait, fwd, fwd_wait

    def start(ins, outs, sems):
        own, ici, _, _, _ = copies(ins, outs, sems)
        for cp in own + ici:
            cp.start()

    def finish(ins, outs, sems):
        own, ici, ici_wait, fwd, fwd_wait = copies(ins, outs, sems)
        for i, cp in enumerate(ici_wait):
            cp.wait_recv()
            if i < len(fwd):
                fwd[i].start()
        for cp in fwd_wait + own:
            cp.wait_recv()
        for cp in own + ici + fwd:
            cp.wait_send()

    srcs = list(halved) + list(whole)
    return _Comm(srcs, [jax.ShapeDtypeStruct((N_CHIP,) + s.shape, s.dtype) for s in srcs],
                 [pltpu.SemaphoreType.DMA((3 * (n + nw),)), pltpu.SemaphoreType.DMA((3 * (n + nw),)),
                  pltpu.SemaphoreType.DMA((max(3 * n, 1),)), pltpu.SemaphoreType.DMA((max(3 * n, 1),)),
                  pltpu.SemaphoreType.DMA((n + nw,)), pltpu.SemaphoreType.DMA((n + nw,))],
                 start, finish)


def _sibling_exchange_comm(grads):
    n = len(grads)

    def copies(ins, outs, sems):
        send_sems, recv_sems = sems
        x, y, c = lax.axis_index("x"), lax.axis_index("y"), lax.axis_index("c")
        res = []
        for ti in range(n):
            half = ins[ti].shape[1] // 2
            res.append(pltpu.make_async_remote_copy(
                src_ref=ins[ti].at[:, pl.ds(pl.multiple_of((1 - c) * half, 16), half), :],
                dst_ref=outs[ti],
                send_sem=send_sems.at[ti], recv_sem=recv_sems.at[ti],
                device_id=(x, y, 1 - c), device_id_type=MESH))
        return res

    def start(ins, outs, sems):
        for cp in copies(ins, outs, sems):
            cp.start()

    def finish(ins, outs, sems):
        for cp in copies(ins, outs, sems):
            cp.wait()

    return _Comm(grads, [jax.ShapeDtypeStruct((N_CHIP, g.shape[1] // 2, g.shape[2]), g.dtype) for g in grads],
                 [pltpu.SemaphoreType.DMA((n,)), pltpu.SemaphoreType.DMA((n,))], start, finish)


def _in_proj_gathering(x2b, own, chip, tm, comm):
    t = x2b.shape[0]
    ni = t // tm
    half, cols = own.shape[1], own.shape[2]
    nci, nco = len(comm.ins), len(comm.out_shapes)

    def body(chip_ref, x_ref, own_ref, own_hbm, *rest):
        c_in = rest[:nci]
        h_ref, win_out = rest[nci:nci + 2]
        c_out = rest[nci + 2:nci + 2 + nco]
        w_scr, ici_send, ici_recv, d2d_send, d2d_recv, own_sems, ld_sems = rest[nci + 2 + nco:nci + 9 + nco]
        c_sem = rest[nci + 9 + nco:]
        s, i = pl.program_id(0), pl.program_id(1)
        x, y, c, peers = _chip_peers()
        me = 2 * x + y
        sibling = (x, y, 1 - c)

        def ici(k, slot):
            px, py = peers[k]
            return pltpu.make_async_remote_copy(
                src_ref=own_hbm.at[c], dst_ref=win_out.at[slot, c],
                send_sem=ici_send.at[k], recv_sem=ici_recv.at[k],
                device_id=(px, py, c), device_id_type=MESH)

        def forward(k, core):
            pk = 2 * peers[k][0] + peers[k][1]
            return pltpu.make_async_remote_copy(
                src_ref=win_out.at[pk, core], dst_ref=win_out.at[pk, core],
                send_sem=d2d_send.at[k], recv_sem=d2d_recv.at[k],
                device_id=sibling, device_id_type=MESH)

        place_own = pltpu.make_async_remote_copy(
            src_ref=own_hbm, dst_ref=win_out.at[me], send_sem=own_sems.at[0], recv_sem=own_sems.at[1],
            device_id=sibling, device_id_type=MESH)

        @pl.when((s == 0) & (i == 0))
        def _():
            for k in range(2):
                ici(k, me).start()
            place_own.start()

        @pl.when(s == 0)
        def _():
            xv = x_ref[...]
            h_ref[...] = (_dot(xv[:, :half], own_ref[0]) + _dot(xv[:, half:], own_ref[1])).astype(BF16)

        for k in range(3):
            @pl.when((s == k + 1) & (i == 0))
            def _(k=k):
                pk = 2 * peers[k][0] + peers[k][1]
                ici(k, pk).wait_recv()
                if k == 0:
                    ici(2, me).start()
                forward(k, c).start()
                forward(k, 1 - c).wait_recv()
                loads = [pltpu.make_async_copy(win_out.at[pk, hh], w_scr.at[hh], ld_sems.at[hh])
                         for hh in range(2)]
                for ld in loads:
                    ld.start()
                for ld in loads:
                    ld.wait()
                if k == 1:
                    comm.start(c_in, c_out, c_sem)

        @pl.when(s > 0)
        def _():
            xv = x_ref[...]
            h_ref[...] = (_dot(xv[:, :half], w_scr[0]) + _dot(xv[:, half:], w_scr[1])).astype(BF16)

        @pl.when((s == N_CHIP - 1) & (i == ni - 1))
        def _():
            place_own.wait()
            for k in range(3):
                ici(k, me).wait_send()
                forward(k, c).wait_send()
            comm.finish(c_in, c_out, c_sem)

    def shard_col(s, me):
        return jnp.where(s == 0, me, me ^ jnp.where(s == 1, 2, jnp.where(s == 2, 1, 3)))

    res = _pc(body, name="in_proj",
              out_shape=(jax.ShapeDtypeStruct((t, N_CHIP * cols), BF16),
                         jax.ShapeDtypeStruct((N_CHIP,) + own.shape, own.dtype)) + tuple(comm.out_shapes),
              grid=(N_CHIP, ni), nsp=1,
              in_specs=[pl.BlockSpec((tm, 2 * half), lambda s, i, chip_ref: (i, 0)),
                        pl.BlockSpec(own.shape, lambda s, i, chip_ref: (0, 0, 0)),
                        ANY] + [ANY] * nci,
              out_specs=(pl.BlockSpec((tm, cols), lambda s, i, chip_ref: (i, shard_col(s, chip_ref[0]))),
                         ANY) + tuple([ANY] * nco),
              scratch=[pltpu.VMEM(own.shape, own.dtype),
                       pltpu.SemaphoreType.DMA((3,)), pltpu.SemaphoreType.DMA((3,)),
                       pltpu.SemaphoreType.DMA((3,)), pltpu.SemaphoreType.DMA((3,)),
                       pltpu.SemaphoreType.DMA((2,)), pltpu.SemaphoreType.DMA((2,))] + comm.sems,
              sem=("arbitrary", "arbitrary"))(chip, x2b, own, own, *comm.ins)
    return res[0], res[1], res[2:]


def _rs_add_halves(name, grad, recv, core):
    _, r, cdim = grad.shape
    half = r // 2
    tr = _row_tile(half, cdim, mult=16)
    nr = half // tr

    def body(c_ref, g_ref, r_ref, o_ref):
        o_ref[...] = (g_ref[...].astype(F32) + r_ref[...].astype(F32)).astype(BF16)

    return _pc(body, name=name, out_shape=jax.ShapeDtypeStruct((N_CHIP, half, cdim), BF16),
               grid=(N_CHIP, nr), nsp=1,
               in_specs=[pl.BlockSpec((None, tr, cdim), lambda j, i, c_ref: (j, c_ref[0] * nr + i, 0)),
                         pl.BlockSpec((None, tr, cdim), lambda j, i, c_ref: (j, i, 0))],
               out_specs=pl.BlockSpec((None, tr, cdim), lambda j, i, c_ref: (j, i, 0)),
               sem=("parallel", "parallel"))(core, grad, recv)


def _chip_exchange_comm(parts):
    n = len(parts)

    def copies(ins, outs, sems):
        send_sems, recv_sems = sems
        x, y, c, peers = _chip_peers()
        return [pltpu.make_async_remote_copy(
            src_ref=ins[ti].at[2 * px + py], dst_ref=outs[ti].at[k],
            send_sem=send_sems.at[ti * 3 + k], recv_sem=recv_sems.at[ti * 3 + k],
            device_id=(px, py, c), device_id_type=MESH)
            for ti in range(n) for k, (px, py) in enumerate(peers)]

    def start(ins, outs, sems):
        for cp in copies(ins, outs, sems):
            cp.start()

    def finish(ins, outs, sems):
        for cp in copies(ins, outs, sems):
            cp.wait()

    return _Comm(parts, [jax.ShapeDtypeStruct((3,) + p.shape[1:], p.dtype) for p in parts],
                 [pltpu.SemaphoreType.DMA((3 * n,)), pltpu.SemaphoreType.DMA((3 * n,))], start, finish)


def _rs_sum_chips(name, part, recv, chip):
    _, half, cdim = recv.shape
    tr = _row_tile(half, cdim, mult=16)

    def body(chip_ref, p_ref, r_ref, o_ref):
        o_ref[...] = ((p_ref[...].astype(F32) + r_ref[0].astype(F32)) + r_ref[1].astype(F32)
                      ) + r_ref[2].astype(F32)

    return _pc(body, name=name, out_shape=jax.ShapeDtypeStruct((half, cdim), F32),
               grid=(half // tr,), nsp=1,
               in_specs=[pl.BlockSpec((None, tr, cdim), lambda i, chip_ref: (chip_ref[0], i, 0)),
                         pl.BlockSpec((3, tr, cdim), lambda i, chip_ref: (0, i, 0))],
               out_specs=pl.BlockSpec((tr, cdim), lambda i, chip_ref: (i, 0)),
               sem=("parallel",))(chip, part, recv)


def _rs_send_halves(halves):
    n = len(halves)

    def body(*refs):
        ins, outs = refs[:n], refs[n:2 * n]
        send_sems, recv_sems = refs[2 * n:]
        x, y, c = lax.axis_index("x"), lax.axis_index("y"), lax.axis_index("c")
        sends = []
        for ti in range(n):
            cp = pltpu.make_async_remote_copy(
                src_ref=ins[ti], dst_ref=outs[ti],
                send_sem=send_sems.at[ti], recv_sem=recv_sems.at[ti],
                device_id=(x, y, 1 - c), device_id_type=MESH)
            cp.start()
            sends.append(cp)
        for cp in sends:
            cp.wait()

    return _pc(body, name="rs_send_halves",
               out_shape=tuple(jax.ShapeDtypeStruct(hv.shape, hv.dtype) for hv in halves),
               in_specs=[ANY] * n, out_specs=tuple([ANY] * n),
               scratch=[pltpu.SemaphoreType.DMA((n,)), pltpu.SemaphoreType.DMA((n,))])(*halves)


def _adamw_rows(name, mine, theirs, w, m, v, core):
    half, cdim = mine.shape
    tr = _row_tile(half, cdim, budget=1 << 19)
    nrh = half // tr

    def body(c_ref, mine_ref, theirs_ref, w_ref, m_ref, v_ref, g_ref, d_ref, m2_ref, v2_ref):
        is_mine = (pl.program_id(0) // nrh) == c_ref[0]
        g = jnp.where(is_mine, mine_ref[...], theirs_ref[...])
        d, m2, v2 = _adamw(w_ref[...], g, m_ref[...], v_ref[...])
        g_ref[...] = g
        d_ref[...] = d
        m2_ref[...] = m2
        v2_ref[...] = v2

    htile = pl.BlockSpec((tr, cdim), lambda i, c_ref: (i % nrh, 0))
    tile = pl.BlockSpec((tr, cdim), lambda i, c_ref: (i, 0))
    shp = jax.ShapeDtypeStruct((2 * half, cdim), F32)
    return _pc(body, name=name, out_shape=(shp, shp, shp, shp), grid=(2 * nrh,), nsp=1,
               in_specs=[htile, htile, tile, tile, tile], out_specs=(tile, tile, tile, tile),
               sem=("parallel",))(core, mine, theirs, w, m, v)


def _adamw_whole(name, g, w, m, v):
    def body(g_ref, w_ref, m_ref, v_ref, d_ref, m2_ref, v2_ref):
        d, m2, v2 = _adamw(w_ref[...], g_ref[...], m_ref[...], v_ref[...])
        d_ref[...] = d
        m2_ref[...] = m2
        v2_ref[...] = v2

    shp = jax.ShapeDtypeStruct(g.shape, F32)
    return _pc(body, name=name, out_shape=(shp, shp, shp))(g, w, m, v)


SMALL_LAYOUT = (
    ("sgu_w_s", 1024, 1, 0),
    ("sgu_b_s", 8, 1, 1024),
    ("sgu_norm_g", 1, 0, 0),
    ("sgu_norm_b", 1, 0, 1),
    ("hgrn_norm_g", 1, 0, 3),
    ("ln1_g", 1, 0, 4),
    ("ln1_b", 1, 0, 5),
    ("ffn_conv_b", 1, 2, 3),
    ("ln2_g", 1, 0, 6),
    ("ln2_b", 1, 0, 7),
)
LB_ROW = 2
LOSS_ROW = 8
PACK_SHAPES = ((16, D_MODEL), (N_GROUP * 128 + 8, 128), (8, D_FF))


def _small_allreduce_adamw(rows1024, dws, dbs, dcw, dcb, logits, m_logits, v_logits,
                           small_w, small_m, small_v):
    ns = len(SMALL_LAYOUT)
    nr = len(rows1024)
    nb = len(PACK_SHAPES)

    def body(*refs):
        row_refs = refs[:nr]
        dws_ref, dbs_ref, dcw_ref, dcb_ref, lg_ref, mlg_ref, vlg_ref = refs[nr:nr + 7]
        pos = nr + 7
        w_refs = refs[pos:pos + ns]
        m_refs = refs[pos + ns:pos + 2 * ns]
        v_refs = refs[pos + 2 * ns:pos + 3 * ns]
        pos += 3 * ns
        loss_ref, dcw_out = refs[pos:pos + 2]
        lg_outs = refs[pos + 2:pos + 6]
        pos += 6
        outs = refs[pos:pos + 4 * ns]
        pos += 4 * ns
        pack = refs[pos:pos + nb]
        sib = refs[pos + nb:pos + 2 * nb]
        gath = refs[pos + 2 * nb:pos + 3 * nb]
        d2d_send, d2d_recv, ici_send, ici_recv = refs[pos + 3 * nb:]

        x, y, c, peers = _chip_peers()
        me = 2 * x + y
        sibling = (x, y, 1 - c)

        pack[0][...] = jnp.zeros(PACK_SHAPES[0], F32)
        for k in range(nr):
            pack[0][k:k + 1, :] = row_refs[k][0:1, :]
        pack[1][0:N_GROUP * 128, :] = dws_ref[...]
        pack[1][N_GROUP * 128:, :] = dbs_ref[...]
        pack[2][...] = jnp.zeros(PACK_SHAPES[2], F32)
        pack[2][0:3, :] = dcw_ref[0:3, :]
        pack[2][3:4, :] = dcb_ref[0:1, :]

        d2d = [pltpu.make_async_remote_copy(
            src_ref=pack[b], dst_ref=sib[b], send_sem=d2d_send.at[b], recv_sem=d2d_recv.at[b],
            device_id=sibling, device_id_type=MESH) for b in range(nb)]
        for cp in d2d:
            cp.start()
        for cp in d2d:
            cp.wait()
        for b in range(nb):
            gath[b][me] = pack[b][...] + sib[b][...]

        ici, ici_wait = [], []
        for b in range(nb):
            for k, (px, py) in enumerate(peers):
                sem = dict(send_sem=ici_send.at[b * 3 + k], recv_sem=ici_recv.at[b * 3 + k],
                           device_id=(px, py, c), device_id_type=MESH)
                ici.append(pltpu.make_async_remote_copy(src_ref=gath[b].at[me], dst_ref=gath[b].at[me], **sem))
                ici_wait.append(pltpu.make_async_remote_copy(
                    src_ref=gath[b].at[me], dst_ref=gath[b].at[2 * px + py], **sem))
        for cp in ici:
            cp.start()
        for cp in ici_wait:
            cp.wait_recv()
        for cp in ici:
            cp.wait_send()

        tot = pack
        for b in range(nb):
            tot[b][...] = ((gath[b][0] + gath[b][1]) + gath[b][2]) + gath[b][3]

        loss_ref[...] = tot[0][LOSS_ROW:LOSS_ROW + 1, :]
        dcw_out[...] = tot[2][...]
        lb = _sig(lg_ref[0:1, :] - lg_ref[1:2, :])
        d0 = tot[0][LB_ROW:LB_ROW + 1, :] * lb * (1.0 - lb)
        rowid = lax.broadcasted_iota(jnp.int32, (2, D_MODEL), 0)
        g_lg = jnp.where(rowid == 0, d0, -d0)
        dl, ml, vl = _adamw(lg_ref[...], g_lg, mlg_ref[...], vlg_ref[...])
        lg_outs[0][...] = g_lg
        lg_outs[1][...] = dl
        lg_outs[2][...] = ml
        lg_outs[3][...] = vl
        for si, (_, rows, b, r0) in enumerate(SMALL_LAYOUT):
            g = tot[b][r0:r0 + rows, :]
            dl, ml, vl = _adamw(w_refs[si][...], g, m_refs[si][...], v_refs[si][...])
            outs[4 * si][...] = g
            outs[4 * si + 1][...] = dl
            outs[4 * si + 2][...] = ml
            outs[4 * si + 3][...] = vl

    shapes = [jax.ShapeDtypeStruct((1, D_MODEL), F32), jax.ShapeDtypeStruct((8, D_FF), F32)]
    shapes += [jax.ShapeDtypeStruct((2, D_MODEL), F32)] * 4
    for w in small_w:
        shapes += [jax.ShapeDtypeStruct(w.shape, F32)] * 4
    scratch = [pltpu.VMEM(shp, F32) for shp in PACK_SHAPES]
    scratch += [pltpu.VMEM(shp, F32) for shp in PACK_SHAPES]
    scratch += [pltpu.VMEM((N_CHIP,) + shp, F32) for shp in PACK_SHAPES]
    scratch += [pltpu.SemaphoreType.DMA((nb,)), pltpu.SemaphoreType.DMA((nb,)),
                pltpu.SemaphoreType.DMA((3 * nb,)), pltpu.SemaphoreType.DMA((3 * nb,))]
    vm = pl.BlockSpec(memory_space=pltpu.VMEM)
    n_in = nr + 7 + 3 * ns
    res = _pc(body, name="small_allreduce_adamw", out_shape=tuple(shapes),
              in_specs=[vm] * n_in, out_specs=tuple([vm] * len(shapes)),
              scratch=scratch)(*rows1024, dws, dbs, dcw, dcb, logits, m_logits, v_logits,
                               *small_w, *small_m, *small_v)
    return res[0], res[1], res[2:6], res[6:]


def kernel(x, p, w_in, sgu_w_s, sgu_b_s, sgu_norm_g, sgu_norm_b, hgrn_lb_logits, hgrn_norm_g, w_branch, w_out, ln1_g, ln1_b, ffn_w_up, ffn_conv_w, ffn_conv_b, ffn_w_down, ln2_g, ln2_b, ple_w_proj, ple_w_gate, loss_target, m_w_in, m_sgu_w_s, m_sgu_b_s, m_sgu_norm_g, m_sgu_norm_b, m_hgrn_lb_logits, m_hgrn_norm_g, m_w_branch, m_w_out, m_ln1_g, m_ln1_b, m_ffn_w_up, m_ffn_conv_w, m_ffn_conv_b, m_ffn_w_down, m_ln2_g, m_ln2_b, m_ple_w_proj, m_ple_w_gate, v_w_in, v_sgu_w_s, v_sgu_b_s, v_sgu_norm_g, v_sgu_norm_b, v_hgrn_lb_logits, v_hgrn_norm_g, v_w_branch, v_w_out, v_ln1_g, v_ln1_b, v_ffn_w_up, v_ffn_conv_w, v_ffn_conv_b, v_ffn_w_down, v_ln2_g, v_ln2_b, v_ple_w_proj, v_ple_w_gate):
    t = x.shape[1]
    x2 = x.reshape(t, D_MODEL)
    x2b = x2.astype(BF16)
    p2 = p.reshape(t, PLE_DIM)
    tgt = loss_target.reshape(t, D_MODEL)
    core = lax.axis_index("c").astype(jnp.int32).reshape(1)
    chip_id = (2 * lax.axis_index("x") + lax.axis_index("y")).astype(jnp.int32).reshape(1)

    big_w = [w_in[0], w_branch[0, 0], w_branch[0, 1], w_out[0], ffn_w_up[0], ffn_w_down[0],
             ple_w_proj[0], ple_w_gate[0]]
    big_m = [m_w_in[0], m_w_branch[0, 0], m_w_branch[0, 1], m_w_out[0], m_ffn_w_up[0],
             m_ffn_w_down[0], m_ple_w_proj[0], m_ple_w_gate[0]]
    big_v = [v_w_in[0], v_w_branch[0, 0], v_w_branch[0, 1], v_w_out[0], v_ffn_w_up[0],
             v_ffn_w_down[0], v_ple_w_proj[0], v_ple_w_gate[0]]
    def halves_of(i):
        w = big_w[i]
        return w.astype(BF16).reshape(2, w.shape[0] // 2, w.shape[1])

    def stacked(g, i):
        return g.reshape(N_CHIP, big_w[i].shape[0], big_w[i].shape[1])


    cid = jnp.arange(SGU_BLOCK) // CHUNK
    maskf = (cid[:, None] >= cid[None, :]).astype(F32)
    ws_masked = sgu_w_s[0] * maskf[None]
    wm = ws_masked.astype(BF16)
    wmt = jnp.transpose(ws_masked, (0, 2, 1)).astype(BF16)
    bsb = jnp.broadcast_to(sgu_b_s[0][:, :, None], (N_GROUP, SGU_BLOCK, 128))

    up_rows = big_w[4].shape[0] // 2
    up_blocks = [big_w[4][k * up_rows:(k + 1) * up_rows].astype(BF16).reshape(2, up_rows // 2, -1)
                 for k in range(2)]
    h, win_g, (up0_g,) = _in_proj_gathering(x2b, halves_of(0), chip_id, 512, _gather_comm([up_blocks[0]]))
    win_st = stacked(win_g, 0)
    ya, _ = _sgu_fwd(h, wm, bsb, sgu_norm_g, sgu_norm_b)
    (yb, st_all), mix_g = _hgrn_fwd(
        h, hgrn_lb_logits, hgrn_norm_g,
        comm=_gather_comm([halves_of(i) for i in (1, 2, 3)] + [up_blocks[1]], [ffn_conv_w[0]]))
    wb0, wb1, wo = [stacked(g, i).reshape(D_MODEL, D_MODEL) for g, i in zip(mix_g[:3], (1, 2, 3))]
    wup_st = jnp.concatenate([g.reshape(N_CHIP, up_rows, -1) for g in (up0_g, mix_g[3])], axis=1)
    convw = jnp.transpose(mix_g[4], (1, 0, 2)).reshape(3, D_FF)
    (r1, a_br, b_br, m_bf, x1b), _ = _mix_fwd(ya, yb, h, x2, wb0, wb1, wo, ln1_g, ln1_b, 256)
    h2, act, out_g = _ffn_up_act(x1b, wup_st, convw, ffn_conv_b, 512,
                                 _gather_comm([halves_of(i) for i in (5, 6, 7)]))
    wd = stacked(out_g[0], 5).reshape(D_FF, D_MODEL)
    wpp = jnp.transpose(stacked(out_g[1], 6), (1, 0, 2)).reshape(PLE_DIM, D_MODEL)
    wpg = stacked(out_g[2], 7).reshape(D_MODEL, D_MODEL)
    dr2, dpg, dpp, loss_acc, dg2, db2 = _out_fwd_bwd(
        act, x1b, r1, p2, tgt, wd, wpg, wpp, ln1_g, ln1_b, ln2_g, ln2_b, 256)

    dh2, dr1, dcw, dcb, dg1, db1 = _ffn_bwd(h2, dr2, wd, wup_st, dpg, wpg, r1, ln1_g, convw, ffn_conv_b, 256)
    d_wd = _mm_tn("ffn_down_wgrad", act, dr2, FF_TILE, 512)
    d_wpg = _mm_tn("ple_gate_wgrad", x1b, dpg, 512, D_MODEL)
    d_wpp_st = _mm_tn("ple_proj_wgrad", p2, dpp, PLE_DIM, PLE_DIM, stacked=True)
    d_wup_st = _mm("ffn_up_wgrad", x1b, dh2, TN, (2, N_CHIP),
                   pl.BlockSpec((t, 512), lambda i, j: (0, i)),
                   pl.BlockSpec((None, t, FF_TILE), lambda i, j: (j // FF_NJ, 0, j % FF_NJ)),
                   jax.ShapeDtypeStruct((N_CHIP, D_MODEL, FF_TILE), BF16),
                   pl.BlockSpec((None, 512, FF_TILE), lambda i, j: (j, i, 0)))
    da_bf, db_bf, dh, dya, dyb = _mix_bwd(dr1, h, a_br, b_br, wo, wb0, wb1, 256)
    d_wo = _mm_tn("out_proj_wgrad", m_bf, dr1, 512, 512)
    d_wb0 = _mm_tn("branch0_wgrad", ya, da_bf, 512, D_MODEL)
    d_wb1 = _mm_tn("branch1_wgrad", yb, db_bf, 512, D_MODEL)
    grads_1 = [d_wb0.reshape(4, 256, D_MODEL), d_wb1.reshape(4, 256, D_MODEL),
               d_wo.reshape(4, 256, D_MODEL), d_wup_st, d_wd.reshape(4, D_FF // 4, D_MODEL),
               d_wpp_st, d_wpg.reshape(4, 256, D_MODEL)]
    (dh, dws, dbs, dgv, dbv), recv_a1 = _sgu_bwd(h, dya, wm, wmt, bsb, sgu_norm_g, sgu_norm_b, maskf, dh,
                                                 comm=_sibling_exchange_comm(grads_1))
    parts_1 = [_rs_add_halves("rs_add_halves%d" % (i + 1), g, r, core)
               for i, (g, r) in enumerate(zip(grads_1, recv_a1))]
    (dh, dlb, dgn), recv_b1 = _hgrn_bwd(h, dyb, st_all, hgrn_lb_logits, hgrn_norm_g, dh,
                                         comm=_chip_exchange_comm(parts_1))

    grads_0 = [_in_proj_wgrad(x2b, dh, 512, D_MODEL)]
    recv_a0 = _run_comm("rs_sibling_exchange0", _sibling_exchange_comm(grads_0))
    parts_0 = [_rs_add_halves("rs_add_halves0", grads_0[0], recv_a0[0], core)]
    gx, recv_b0 = _in_proj_xgrad(dh, win_st, dr1, 512, _chip_exchange_comm(parts_0))
    parts = parts_0 + parts_1
    recv_b = list(recv_b0) + list(recv_b1)
    halves = [_rs_sum_chips("rs_sum_chips%d" % i, pt, r, chip_id)
              for i, (pt, r) in enumerate(zip(parts, recv_b))]
    theirs = _rs_send_halves(halves)
    big_out = [_adamw_rows("adamw_big%d" % i, halves[i], theirs[i], big_w[i], big_m[i], big_v[i], core)
               for i in range(len(halves))]

    small_in = dict(sgu_w_s=(sgu_w_s, m_sgu_w_s, v_sgu_w_s), sgu_b_s=(sgu_b_s, m_sgu_b_s, v_sgu_b_s),
                    sgu_norm_g=(sgu_norm_g, m_sgu_norm_g, v_sgu_norm_g),
                    sgu_norm_b=(sgu_norm_b, m_sgu_norm_b, v_sgu_norm_b),
                    hgrn_norm_g=(hgrn_norm_g, m_hgrn_norm_g, v_hgrn_norm_g),
                    ln1_g=(ln1_g, m_ln1_g, v_ln1_g), ln1_b=(ln1_b, m_ln1_b, v_ln1_b),
                    ffn_conv_b=(ffn_conv_b, m_ffn_conv_b, v_ffn_conv_b),
                    ln2_g=(ln2_g, m_ln2_g, v_ln2_g), ln2_b=(ln2_b, m_ln2_b, v_ln2_b))

    def flat(name, arr):
        rows = dict((n, r) for n, r, _, _ in SMALL_LAYOUT)[name]
        return arr.reshape(rows, arr.size // rows)

    names = [n for n, _, _, _ in SMALL_LAYOUT]
    sw = [flat(n, small_in[n][0]) for n in names]
    sm = [flat(n, small_in[n][1]) for n in names]
    sv = [flat(n, small_in[n][2]) for n in names]
    loss_rows, dcw_tot, lg_out, small_out = _small_allreduce_adamw(
        [dgv, dbv, dlb, dgn, dg1, db1, dg2, db2, loss_acc], dws.reshape(N_GROUP * 128, 128), dbs, dcw, dcb,
        hgrn_lb_logits, m_hgrn_lb_logits, v_hgrn_lb_logits, sw, sm, sv)
    loss = loss_rows[0, 0]

    chip = 2 * lax.axis_index("x") + lax.axis_index("y")
    g_cw = lax.dynamic_slice(dcw_tot, (0, chip * (D_FF // 4)), (3, D_FF // 4))
    cw_out = _adamw_whole("adamw_conv_w", g_cw, ffn_conv_w[0], m_ffn_conv_w[0], v_ffn_conv_w[0])

    res = {}
    for si, n in enumerate(names):
        shp = small_in[n][0].shape
        res[n] = tuple(small_out[4 * si + k].reshape(shp) for k in range(4))
    res["hgrn_lb_logits"] = tuple(lg_out)
    res["ffn_conv_w"] = (g_cw[None],) + tuple(o[None] for o in cw_out)

    def big(i):
        return tuple(big_out[i])

    res["w_in"] = tuple(o[None] for o in big(0))
    res["w_branch"] = tuple(jnp.stack([o0, o1])[None] for o0, o1 in zip(big(1), big(2)))
    res["w_out"] = tuple(o[None] for o in big(3))
    res["ffn_w_up"] = tuple(o[None] for o in big(4))
    res["ffn_w_down"] = tuple(o[None] for o in big(5))
    res["ple_w_proj"] = tuple(o[None] for o in big(6))
    res["ple_w_gate"] = tuple(o[None] for o in big(7))

    order = ["w_in", "sgu_w_s", "sgu_b_s", "sgu_norm_g", "sgu_norm_b", "hgrn_lb_logits",
             "hgrn_norm_g", "w_branch", "w_out", "ln1_g", "ln1_b", "ffn_w_up", "ffn_conv_w",
             "ffn_conv_b", "ffn_w_down", "ln2_g", "ln2_b", "ple_w_proj", "ple_w_gate"]
    outs = [loss, gx.reshape(1, t, D_MODEL)]
    for k in range(4):
        outs += [res[n][k] for n in order]
    return tuple(outs)
```

```python
import functools

import jax
import jax.numpy as jnp
from jax import lax
from jax.experimental import pallas as pl
from jax.experimental.pallas import tpu as pltpu

F32 = jnp.float32
BF16 = jnp.bfloat16
HIGHEST = lax.Precision.HIGHEST
MESH = pl.DeviceIdType.MESH

D_MODEL = 1024
CHUNK = 64
SGU_BLOCK = 128
SGU_STEP_BLOCKS = 2
SGU_ROWS = SGU_STEP_BLOCKS * SGU_BLOCK
N_GROUP = 8
N_HEAD = 8
HEAD_DIM = 128
D_FF = 2816
PLE_DIM = 256
IN_COLS = 8192
LN_EPS = 1e-5
RMS_EPS = 1e-6
ALPHA = 2.0 ** 0.25
N_CHIP = 4
N_DEV = 8

ADAM_LR = 0.001
ADAM_B1 = 0.9
ADAM_B2 = 0.999
ADAM_EPS = 1e-08
ADAM_WD = 0.01
ADAM_STEP = 10

VMEM_LIMIT = 56 * 1024 * 1024

NN = (((1,), (0,)), ((), ()))
NT = (((1,), (1,)), ((), ()))
TN = (((0,), (0,)), ((), ()))


def _pc(body, *, name, out_shape, grid=None, in_specs=None, out_specs=None, scratch=(),
        sem=None, nsp=0, vmem=VMEM_LIMIT, aliases=None):
    params = dict(vmem_limit_bytes=vmem)
    if sem is not None:
        params["dimension_semantics"] = sem
    kw = dict(name=name, out_shape=out_shape, compiler_params=pltpu.CompilerParams(**params))
    if aliases:
        kw["input_output_aliases"] = aliases
    if nsp:
        kw["grid_spec"] = pltpu.PrefetchScalarGridSpec(
            num_scalar_prefetch=nsp, grid=grid, in_specs=in_specs, out_specs=out_specs,
            scratch_shapes=list(scratch))
    else:
        if grid is not None:
            kw["grid"] = grid
        if in_specs is not None:
            kw["in_specs"] = in_specs
            kw["out_specs"] = out_specs
        kw["scratch_shapes"] = list(scratch)
    return pl.pallas_call(body, **kw)


def _dot(a, b, dims=NN):
    return lax.dot_general(a.astype(BF16), b.astype(BF16), dims, preferred_element_type=F32)


def _dot32(a, b, dims=NN):
    return lax.dot_general(a, b, dims, precision=HIGHEST, preferred_element_type=F32)


def _sig(x):
    return 1.0 / (1.0 + jnp.exp(-x))


_GC = 0.7978845608028654
_GA = 0.044715


def _gelu(x):
    return 0.5 * x * (1.0 + jnp.tanh(_GC * (x + _GA * x * x * x)))


def _gelu_and_grad(x):
    t = jnp.tanh(_GC * (x + _GA * x * x * x))
    g = 0.5 * x * (1.0 + t)
    dg = 0.5 * (1.0 + t) + 0.5 * x * (1.0 - t * t) * _GC * (1.0 + 3.0 * _GA * x * x)
    return g, dg


def _ln_stats(r):
    mu = jnp.mean(r, axis=-1, keepdims=True)
    xc = r - mu
    var = jnp.mean(xc * xc, axis=-1, keepdims=True)
    rstd = lax.rsqrt(var + LN_EPS)
    return xc * rstd, rstd


def _ln_bwd(dxh, xh, rstd):
    m1 = jnp.mean(dxh, axis=-1, keepdims=True)
    m2 = jnp.mean(dxh * xh, axis=-1, keepdims=True)
    return rstd * (dxh - m1 - xh * m2)


def _colsum8(v):
    return jnp.broadcast_to(jnp.sum(v, axis=0, keepdims=True), (8, v.shape[1]))


def _adamw(w, g, m, v):
    m2 = ADAM_B1 * m + (1.0 - ADAM_B1) * g
    v2 = ADAM_B2 * v + (1.0 - ADAM_B2) * (g * g)
    m_hat = m2 / (1.0 - ADAM_B1 ** ADAM_STEP)
    v_hat = v2 / (1.0 - ADAM_B2 ** ADAM_STEP)
    delta = -ADAM_LR * (m_hat / (jnp.sqrt(v_hat) + ADAM_EPS) + ADAM_WD * w)
    return delta, m2, v2


def _row_tile(rows, cols, itemsize=4, budget=1 << 20, mult=8):
    best = mult
    for tr in range(mult, rows + 1, mult):
        if rows % tr == 0 and tr * cols * itemsize <= budget:
            best = tr
    return best


def _mm(name, a, b, dims, grid, a_spec, b_spec, out_shape, o_spec):
    out_dtype = out_shape.dtype

    def body(a_ref, b_ref, o_ref):
        o_ref[...] = _dot(a_ref[...], b_ref[...], dims).astype(out_dtype)

    return _pc(body, name=name, out_shape=out_shape, grid=grid, in_specs=[a_spec, b_spec],
               out_specs=o_spec, sem=("parallel", "parallel"))(a, b)


class _Comm:
    def __init__(self, ins, out_shapes, sems, start, finish):
        self.ins, self.out_shapes, self.sems = list(ins), list(out_shapes), list(sems)
        self.start, self.finish = start, finish


def _hosted_call(body, comm, first, last, *, name, out_shape, grid, in_specs, out_specs, scratch, sem,
                 args, aliases=None):
    n_in, n_out, n_scr = len(in_specs), len(out_shape), len(scratch)
    nci, nco = len(comm.ins), len(comm.out_shapes)

    def wrapped(*refs):
        pos = n_in
        own_in, c_in = refs[:pos], refs[pos:pos + nci]
        pos += nci
        own_out, c_out = refs[pos:pos + n_out], refs[pos + n_out:pos + n_out + nco]
        pos += n_out + nco
        own_scr, c_sem = refs[pos:pos + n_scr], refs[pos + n_scr:]

        @pl.when(first())
        def _():
            comm.start(c_in, c_out, c_sem)

        body(*own_in, *own_out, *own_scr)

        @pl.when(last())
        def _():
            comm.finish(c_in, c_out, c_sem)

    return _pc(wrapped, name=name, out_shape=tuple(out_shape) + tuple(comm.out_shapes), grid=grid,
               in_specs=list(in_specs) + [ANY] * nci, out_specs=tuple(out_specs) + tuple([ANY] * nco),
               scratch=list(scratch) + comm.sems, sem=sem, aliases=aliases)(*args, *comm.ins)


def _grid1_call(body, comm, n, *, name, out_shape, in_specs, out_specs, scratch, args, aliases=None):
    if comm is None:
        return _pc(body, name=name, out_shape=out_shape, grid=(n,), in_specs=in_specs, out_specs=out_specs,
                   scratch=scratch, sem=("arbitrary",), aliases=aliases)(*args), ()
    res = _hosted_call(body, comm, lambda: pl.program_id(0) == 0, lambda: pl.program_id(0) == n - 1,
                       name=name, out_shape=out_shape, grid=(n,), in_specs=in_specs,
                       out_specs=out_specs, scratch=scratch, sem=("arbitrary",), args=args, aliases=aliases)
    return res[:len(out_shape)], res[len(out_shape):]


def _run_comm(name, comm):
    nci, nco = len(comm.ins), len(comm.out_shapes)

    def body(*refs):
        c_in, c_out, c_sem = refs[:nci], refs[nci:nci + nco], refs[nci + nco:]
        comm.start(c_in, c_out, c_sem)
        comm.finish(c_in, c_out, c_sem)

    return _pc(body, name=name, out_shape=tuple(comm.out_shapes), in_specs=[ANY] * nci,
               out_specs=tuple([ANY] * nco), scratch=comm.sems)(*comm.ins)


def _mm_tn(name, a, b, tm, tn, stacked=False):
    t, m = a.shape
    _, n = b.shape
    if stacked:
        assert tm == m
        out_shape = jax.ShapeDtypeStruct((n // tn, m, tn), BF16)
        o_spec = pl.BlockSpec((None, tm, tn), lambda i, j: (j, 0, 0))
    else:
        out_shape = jax.ShapeDtypeStruct((m, n), BF16)
        o_spec = pl.BlockSpec((tm, tn), lambda i, j: (i, j))
    return _mm(name, a, b, TN, (m // tm, n // tn),
               pl.BlockSpec((t, tm), lambda i, j: (0, i)),
               pl.BlockSpec((t, tn), lambda i, j: (0, j)),
               out_shape, o_spec)


DH_SLOT = (2, 0, 1, 3)


def _dh_slot(j):
    return jnp.where(j == 3, 3, (j + 2) % 3)


def _in_proj_wgrad(x2b, dh, tm, tn):
    t, m = x2b.shape
    n = dh.shape[2]

    def body(a_ref, b_ref, o_ref):
        o_ref[...] = _dot(a_ref[...], b_ref[...], TN).astype(BF16)

    return _pc(body, name="in_proj_wgrad", out_shape=jax.ShapeDtypeStruct((N_CHIP, m, n), BF16),
               grid=(N_CHIP, m // tm, n // tn),
               in_specs=[pl.BlockSpec((t, tm), lambda j, i, k: (0, i)),
                         pl.BlockSpec((None, t, tn), lambda j, i, k: (_dh_slot(j), 0, k))],
               out_specs=pl.BlockSpec((None, tm, tn), lambda j, i, k: (j, i, k)),
               sem=("parallel", "parallel", "parallel"))(x2b, dh)


def _in_proj_xgrad(dh, win_st, dr1, tm, comm):
    t = dr1.shape[0]
    ni = t // tm

    def body(a_ref, b_ref, add_ref, o_ref, acc):
        j = pl.program_id(1)
        prod = _dot(a_ref[...], b_ref[...], NT)

        @pl.when(j == 0)
        def _():
            acc[...] = prod + ALPHA * add_ref[...]

        @pl.when((j > 0) & (j < N_CHIP - 1))
        def _():
            acc[...] += prod

        @pl.when(j == N_CHIP - 1)
        def _():
            o_ref[...] = acc[...] + prod

    tile = pl.BlockSpec((tm, D_MODEL), lambda i, j: (i, 0))
    res = _hosted_call(body, comm,
                       lambda: (pl.program_id(0) == 0) & (pl.program_id(1) == 0),
                       lambda: (pl.program_id(0) == ni - 1) & (pl.program_id(1) == N_CHIP - 1),
                       name="in_proj_xgrad", out_shape=(jax.ShapeDtypeStruct((t, D_MODEL), F32),),
                       grid=(ni, N_CHIP),
                       in_specs=[pl.BlockSpec((None, tm, 2 * D_MODEL), lambda i, j: (_dh_slot(j), i, 0)),
                                 pl.BlockSpec((None, D_MODEL, 2 * D_MODEL), lambda i, j: (j, 0, 0)),
                                 tile],
                       out_specs=(tile,), scratch=[pltpu.VMEM((tm, D_MODEL), F32)],
                       sem=("arbitrary", "arbitrary"), args=[dh, win_st, dr1])
    return res[0], res[1:]


def _sgu_mixed(v, wm_ref, bsb_ref, gv, bv):
    gl, dgl = _gelu_and_grad(v)
    vh, rstd = _ln_stats(gl)
    vn = vh * gv + bv
    mixed = []
    for g in range(N_GROUP):
        sl = slice(g * 128, (g + 1) * 128)
        mixed.append(_dot(wm_ref[g], vn[:, sl]) + bsb_ref[g])
    return dgl, vh, rstd, vn, mixed


def _sgu_fwd(h, wm, bsb, gv, bv, comm=None):
    t = h.shape[0]

    def body(u_ref, v_ref, wm_ref, bsb_ref, gv_ref, bv_ref, ya_ref):
        for bb in range(SGU_STEP_BLOCKS):
            rows = slice(bb * SGU_BLOCK, (bb + 1) * SGU_BLOCK)
            u = u_ref[rows, :].astype(F32)
            _, _, _, _, mixed = _sgu_mixed(v_ref[rows, :].astype(F32), wm_ref, bsb_ref, gv_ref[...],
                                           bv_ref[...])
            gu = _gelu(u)
            for g in range(N_GROUP):
                sl = slice(g * 128, (g + 1) * 128)
                ya_ref[rows, sl] = (gu[:, sl] * mixed[g]).astype(BF16)

    full3 = pl.BlockSpec((N_GROUP, 128, 128), lambda i: (0, 0, 0))
    vec = pl.BlockSpec((1, D_MODEL), lambda i: (0, 0))
    (ya,), extra = _grid1_call(
        body, comm, t // SGU_ROWS, name="sgu_fwd",
        out_shape=(jax.ShapeDtypeStruct((t, D_MODEL), BF16),),
        in_specs=[pl.BlockSpec((SGU_ROWS, D_MODEL), lambda i: (i, 0)),
                  pl.BlockSpec((SGU_ROWS, D_MODEL), lambda i: (i, 1)),
                  full3, full3, vec, vec],
        out_specs=(pl.BlockSpec((SGU_ROWS, D_MODEL), lambda i: (i, 0)),),
        scratch=[], args=(h, h, wm, bsb, gv, bv))
    return ya, extra


def _sgu_bwd(h, dya, wm, wmt, bsb, gv, bv, maskf, dh_buf, comm=None):
    t = h.shape[0]
    nb = t // SGU_ROWS

    def body(u_ref, v_ref, dya_ref, wm_ref, wmt_ref, bsb_ref, gv_ref, bv_ref, mask_ref, dh_buf_ref,
             dh_ref, dws_ref, dbs_ref, dgv_ref, dbv_ref, dmix_acc):
        i = pl.program_id(0)

        @pl.when(i == 0)
        def _():
            dws_ref[...] = jnp.zeros_like(dws_ref)
            dgv_ref[...] = jnp.zeros_like(dgv_ref)
            dbv_ref[...] = jnp.zeros_like(dbv_ref)
            dmix_acc[...] = jnp.zeros_like(dmix_acc)

        gvv = gv_ref[...]
        for bb in range(SGU_STEP_BLOCKS):
            rows = slice(bb * SGU_BLOCK, (bb + 1) * SGU_BLOCK)
            u = u_ref[rows, :].astype(F32)
            dgl_v, vh, rstd, vn, mixed = _sgu_mixed(v_ref[rows, :].astype(F32), wm_ref, bsb_ref, gvv,
                                                    bv_ref[...])
            gu, dgl_u = _gelu_and_grad(u)
            dya_v = dya_ref[rows, :].astype(F32)
            dvn_parts = []
            for g in range(N_GROUP):
                sl = slice(g * 128, (g + 1) * 128)
                d_y = dya_v[:, sl]
                dh_ref[rows, sl] = (d_y * mixed[g] * dgl_u[:, sl]).astype(BF16)
                d_mixed = d_y * gu[:, sl]
                dmix_acc[g] += d_mixed
                dws_ref[g] += _dot(d_mixed, vn[:, sl], NT) * mask_ref[...]
                dvn_parts.append(_dot(wmt_ref[g], d_mixed))
            dvn = jnp.concatenate(dvn_parts, axis=1)
            dgv_ref[...] += _colsum8(dvn * vh)
            dbv_ref[...] += _colsum8(dvn)
            d_gl = _ln_bwd(dvn * gvv, vh, rstd)
            dh_ref[rows, D_MODEL:] = (d_gl * dgl_v).astype(BF16)

        @pl.when(i == nb - 1)
        def _():
            rowid = lax.broadcasted_iota(jnp.int32, (8, 128), 0)
            ones = jnp.ones((8, 128), F32)
            acc = jnp.zeros((8, 128), F32)
            for g in range(N_GROUP):
                rs = _dot32(ones, dmix_acc[g], NT)
                acc = jnp.where(rowid == g, rs, acc)
            dbs_ref[...] = acc

    full3 = pl.BlockSpec((N_GROUP, 128, 128), lambda i: (0, 0, 0))
    vec = pl.BlockSpec((1, D_MODEL), lambda i: (0, 0))
    acc8 = pl.BlockSpec((8, D_MODEL), lambda i: (0, 0))
    return _grid1_call(
        body, comm, nb, name="sgu_bwd",
        out_shape=(jax.ShapeDtypeStruct(dh_buf.shape, BF16),
                   jax.ShapeDtypeStruct((N_GROUP, 128, 128), F32),
                   jax.ShapeDtypeStruct((8, 128), F32),
                   jax.ShapeDtypeStruct((8, D_MODEL), F32),
                   jax.ShapeDtypeStruct((8, D_MODEL), F32)),
        in_specs=[pl.BlockSpec((SGU_ROWS, D_MODEL), lambda i: (i, 0)),
                  pl.BlockSpec((SGU_ROWS, D_MODEL), lambda i: (i, 1)),
                  pl.BlockSpec((SGU_ROWS, D_MODEL), lambda i: (i, 0)),
                  full3, full3, full3, vec, vec,
                  pl.BlockSpec((128, 128), lambda i: (0, 0)), ANY],
        out_specs=(pl.BlockSpec((None, SGU_ROWS, 2 * D_MODEL), lambda i: (DH_SLOT[0], i, 0)),
                   full3, pl.BlockSpec((8, 128), lambda i: (0, 0)), acc8, acc8),
        scratch=[pltpu.VMEM((N_GROUP, 128, 128), F32)],
        args=(h, h, dya, wm, wmt, bsb, gv, bv, maskf, dh_buf), aliases={9: 0})


def _tri_masks():
    row = lax.broadcasted_iota(jnp.int32, (CHUNK, CHUNK), 0)
    col = lax.broadcasted_iota(jnp.int32, (CHUNK, CHUNK), 1)
    return col <= row, col >= row


def _heads(v):
    return [v[:, hd * HEAD_DIM:(hd + 1) * HEAD_DIM] for hd in range(N_HEAD)]


def _tri_cumsum(tri_bf, v):
    hi = v.astype(BF16)
    r = v - hi.astype(F32)
    mid = r.astype(BF16)
    lo = (r - mid.astype(F32)).astype(BF16)
    return _dot(tri_bf, hi) + _dot(tri_bf, mid) + _dot(tri_bf, lo)


def _hgrn_chunk(q, fp, ii, lb, st_heads, causal, with_o=True):
    sg = _sig(fp)
    f = lb + (1.0 - lb) * sg
    k = 1.0 - f
    c = _tri_cumsum(causal.astype(BF16), jnp.log(f))
    ec = jnp.exp(c)
    en = jnp.exp(-c)
    sq = _sig(q)
    qt = q * sq * ec
    kt = k * en
    ecl = jnp.exp(c[CHUNK - 1:CHUNK, :])
    kk = kt * ecl
    qtb, ktb, iib, kkb = qt.astype(BF16), kt.astype(BF16), ii.astype(BF16), kk.astype(BF16)
    attn, o = [], []
    for hd, (qh, kh, ih) in enumerate(zip(_heads(qtb), _heads(ktb), _heads(iib))):
        a = jnp.where(causal, _dot(qh, kh, NT), 0.0).astype(BF16)
        attn.append(a)
        if with_o:
            o.append(_dot(a, ih) + _dot(qh, st_heads[hd], NT))
    return dict(sg=sg, f=f, k=k, ec=ec, en=en, sq=sq, ecl=ecl, kk=kk, qtb=qtb, ktb=ktb, iib=iib,
                kkb=kkb, attn=attn, o=o)


def _rms_heads(o_heads):
    rinv = [lax.rsqrt(jnp.mean(o * o, axis=-1, keepdims=True) + RMS_EPS) for o in o_heads]
    return rinv, jnp.concatenate([o * r for o, r in zip(o_heads, rinv)], axis=1)


HG_CHUNKS = 8
HG_ROWS = HG_CHUNKS * CHUNK


def _hgrn_fwd(h, logits, gn, comm=None):
    t = h.shape[0]
    nb = t // HG_ROWS

    def body(q_ref, f_ref, i_ref, og_ref, lg_ref, gn_ref, yb_ref, o_ref, st_ref, state):
        @pl.when(pl.program_id(0) == 0)
        def _():
            state[...] = jnp.zeros_like(state)

        causal, _ = _tri_masks()
        lb = _sig(lg_ref[0:1, :] - lg_ref[1:2, :])
        gnv = gn_ref[...]
        st = [state[hd] for hd in range(N_HEAD)]
        for cc in range(HG_CHUNKS):
            rows = slice(cc * CHUNK, (cc + 1) * CHUNK)
            og = og_ref[rows, :].astype(F32)
            r = _hgrn_chunk(q_ref[rows, :].astype(F32), f_ref[rows, :].astype(F32),
                            i_ref[rows, :].astype(F32), lb, [s.astype(BF16) for s in st], causal)
            o_bf = jnp.concatenate(r["o"], axis=1).astype(BF16)
            o_ref[rows, :] = o_bf
            _, on = _rms_heads(_heads(o_bf.astype(F32)))
            yb_ref[rows, :] = (on * gnv * (og * _sig(og))).astype(BF16)
            for hd in range(N_HEAD):
                st_ref[cc, hd] = st[hd]
            st = [s * e + _dot(ih, kh, TN)
                  for s, e, ih, kh in zip(st, _heads(r["ecl"]), _heads(r["iib"]), _heads(r["kkb"]))]
        for hd in range(N_HEAD):
            state[hd] = st[hd]

    def col(k):
        return pl.BlockSpec((HG_ROWS, D_MODEL), lambda ci: (ci, k))

    return _grid1_call(body, comm, nb, name="hgrn_fwd",
                       out_shape=(jax.ShapeDtypeStruct((t, D_MODEL), BF16),
                                  jax.ShapeDtypeStruct((t, D_MODEL), BF16),
                                  jax.ShapeDtypeStruct((t // CHUNK, N_HEAD, HEAD_DIM, HEAD_DIM), F32)),
                       in_specs=[col(2), col(3), col(4), col(5),
                                 pl.BlockSpec((2, D_MODEL), lambda ci: (0, 0)),
                                 pl.BlockSpec((1, D_MODEL), lambda ci: (0, 0))],
                       out_specs=(pl.BlockSpec((HG_ROWS, D_MODEL), lambda ci: (ci, 0)),
                                  pl.BlockSpec((HG_ROWS, D_MODEL), lambda ci: (ci, 0)),
                                  pl.BlockSpec((HG_CHUNKS, N_HEAD, HEAD_DIM, HEAD_DIM),
                                               lambda ci: (ci, 0, 0, 0))),
                       scratch=[pltpu.VMEM((N_HEAD, HEAD_DIM, HEAD_DIM), F32)],
                       args=(h, h, h, h, logits, gn))


def _hgrn_chunk_bwd(q, fp, ii, og, o_saved, dy, gnv, lb, st, dsn, causal, anti):
    stb = [s.astype(BF16) for s in st]
    dsnb = [s.astype(BF16) for s in dsn]
    r = _hgrn_chunk(q, fp, ii, lb, stb, causal, with_o=False)
    rinv, on = _rms_heads(_heads(o_saved))
    so = _sig(og)
    sil = og * so
    d_og = dy * on * gnv * (so * (1.0 + og * (1.0 - so)))
    d_on = dy * gnv * sil
    d_ob = jnp.concatenate(
        [ri * (dn - oh * jnp.mean(dn * oh, axis=-1, keepdims=True))
         for ri, dn, oh in zip(rinv, _heads(d_on), _heads(on))], axis=1).astype(BF16)
    d_i, d_qt, d_kt, d_kk, d_st, st_dsn = [], [], [], [], [], []
    ecl = _heads(r["ecl"])
    for hd, (dh, qh, kh, ih, kkh) in enumerate(zip(_heads(d_ob), _heads(r["qtb"]), _heads(r["ktb"]),
                                                   _heads(r["iib"]), _heads(r["kkb"]))):
        d_attn = jnp.where(causal, _dot(dh, ih, NT), 0.0).astype(BF16)
        d_i.append(_dot(r["attn"][hd], dh, TN) + _dot(kkh, dsnb[hd], NT))
        d_qt.append(_dot(d_attn, kh) + _dot(dh, stb[hd]))
        d_kt.append(_dot(d_attn, qh, TN))
        d_kk.append(_dot(ih, dsnb[hd]))
        d_st.append(_dot(dh, qh, TN) + dsn[hd] * ecl[hd])
        st_dsn.append(jnp.sum(st[hd] * dsn[hd], axis=0, keepdims=True))
    d_qt = jnp.concatenate(d_qt, axis=1)
    d_kt = jnp.concatenate(d_kt, axis=1)
    d_kk = jnp.concatenate(d_kk, axis=1)
    kk = r["kk"]
    d_cl = r["ecl"] * jnp.concatenate(st_dsn, axis=1) + jnp.sum(kk * d_kk, axis=0, keepdims=True)
    d_k = (d_kk * r["ecl"] + d_kt) * r["en"]
    d_c = d_qt * r["qtb"].astype(F32) - d_kt * r["ktb"].astype(F32) - d_kk * kk
    rowid = lax.broadcasted_iota(jnp.int32, (CHUNK, D_MODEL), 0)
    d_c = d_c + jnp.where(rowid == CHUNK - 1, d_cl, 0.0)
    d_lf = _tri_cumsum(anti.astype(BF16), d_c)
    d_f = d_lf / r["f"] - d_k
    sg, sq = r["sg"], r["sq"]
    d_q = d_qt * r["ec"] * (sq * (1.0 + q * (1.0 - sq)))
    d_fp = d_f * (1.0 - lb) * sg * (1.0 - sg)
    return (d_q, d_fp, jnp.concatenate(d_i, axis=1), d_og, d_st,
            _colsum8(dy * on * sil), _colsum8(d_f * (1.0 - sg)))


def _hgrn_bwd(h, o_all, dyb, st_all, logits, gn, dh_buf, comm=None):
    t = h.shape[0]
    nb = t // HG_ROWS

    def body(q_ref, f_ref, i_ref, og_ref, o_ref, dyb_ref, st_ref, lg_ref, gn_ref, dh_buf_ref,
             dh_ref, dlb_ref, dgn_ref, dstate):
        @pl.when(pl.program_id(0) == 0)
        def _():
            dstate[...] = jnp.zeros_like(dstate)
            dlb_ref[...] = jnp.zeros_like(dlb_ref)
            dgn_ref[...] = jnp.zeros_like(dgn_ref)

        causal, anti = _tri_masks()
        lb = _sig(lg_ref[0:1, :] - lg_ref[1:2, :])
        gnv = gn_ref[...]
        dsn = [dstate[hd] for hd in range(N_HEAD)]
        dgn_acc = jnp.zeros((8, D_MODEL), F32)
        dlb_acc = jnp.zeros((8, D_MODEL), F32)
        for cc in reversed(range(HG_CHUNKS)):
            rows = slice(cc * CHUNK, (cc + 1) * CHUNK)
            d_q, d_fp, d_i, d_og, dsn, dgn_c, dlb_c = _hgrn_chunk_bwd(
                q_ref[rows, :].astype(F32), f_ref[rows, :].astype(F32), i_ref[rows, :].astype(F32),
                og_ref[rows, :].astype(F32), o_ref[rows, :].astype(F32), dyb_ref[rows, :].astype(F32), gnv, lb,
                [st_ref[cc, hd] for hd in range(N_HEAD)], dsn, causal, anti)
            dgn_acc = dgn_acc + dgn_c
            dlb_acc = dlb_acc + dlb_c
            dh_ref[0, rows, :D_MODEL] = d_q.astype(BF16)
            dh_ref[0, rows, D_MODEL:] = d_fp.astype(BF16)
            dh_ref[1, rows, :D_MODEL] = d_i.astype(BF16)
            dh_ref[1, rows, D_MODEL:] = d_og.astype(BF16)
        dgn_ref[...] += dgn_acc
        dlb_ref[...] += dlb_acc
        for hd in range(N_HEAD):
            dstate[hd] = dsn[hd]

    def col(k):
        return pl.BlockSpec((HG_ROWS, D_MODEL), lambda ci: (nb - 1 - ci, k))

    acc8 = pl.BlockSpec((8, D_MODEL), lambda ci: (0, 0))
    pair = pl.BlockSpec((2, HG_ROWS, 2 * D_MODEL), lambda ci: (0, nb - 1 - ci, 0))
    return _grid1_call(body, comm, nb, name="hgrn_bwd",
                       out_shape=(jax.ShapeDtypeStruct(dh_buf.shape, BF16),
                                  jax.ShapeDtypeStruct((8, D_MODEL), F32),
                                  jax.ShapeDtypeStruct((8, D_MODEL), F32)),
                       in_specs=[col(2), col(3), col(4), col(5), col(0),
                                 pl.BlockSpec((HG_ROWS, D_MODEL), lambda ci: (nb - 1 - ci, 0)),
                                 pl.BlockSpec((HG_CHUNKS, N_HEAD, HEAD_DIM, HEAD_DIM),
                                              lambda ci: (nb - 1 - ci, 0, 0, 0)),
                                 pl.BlockSpec((2, D_MODEL), lambda ci: (0, 0)),
                                 pl.BlockSpec((1, D_MODEL), lambda ci: (0, 0)), ANY],
                       out_specs=(pair, acc8, acc8),
                       scratch=[pltpu.VMEM((N_HEAD, HEAD_DIM, HEAD_DIM), F32)],
                       args=(h, h, h, h, o_all, dyb, st_all, logits, gn, dh_buf), aliases={9: 0})


def _mix_fwd(ya, yb, h, x, wb0, wb1, wo, g1, b1, tm, comm=None):
    t = x.shape[0]

    def body(ya_ref, yb_ref, ga_ref, gb_ref, x_ref, wb0_ref, wb1_ref, wo_ref, g1_ref, b1_ref,
             r1_ref, a_ref, b_ref, m_ref, x1_ref):
        a = _dot(ya_ref[...], wb0_ref[...])
        b = _dot(yb_ref[...], wb1_ref[...])
        m = _sig(ga_ref[...].astype(F32)) * a + _sig(gb_ref[...].astype(F32)) * b
        r1 = ALPHA * x_ref[...] + _dot(m, wo_ref[...])
        xh, _ = _ln_stats(r1)
        r1_ref[...] = r1
        a_ref[...] = a.astype(BF16)
        b_ref[...] = b.astype(BF16)
        m_ref[...] = m.astype(BF16)
        x1_ref[...] = (xh * g1_ref[...] + b1_ref[...]).astype(BF16)

    tile = pl.BlockSpec((tm, D_MODEL), lambda i: (i, 0))
    wsp = pl.BlockSpec((D_MODEL, D_MODEL), lambda i: (0, 0))
    vec = pl.BlockSpec((1, D_MODEL), lambda i: (0, 0))
    f32o = jax.ShapeDtypeStruct((t, D_MODEL), F32)
    bfo = jax.ShapeDtypeStruct((t, D_MODEL), BF16)
    return _grid1_call(body, comm, t // tm, name="mix_fwd", out_shape=(f32o, bfo, bfo, bfo, bfo),
                       in_specs=[tile, tile,
                                 pl.BlockSpec((tm, D_MODEL), lambda i: (i, 6)),
                                 pl.BlockSpec((tm, D_MODEL), lambda i: (i, 7)),
                                 tile, wsp, wsp, wsp, vec, vec],
                       out_specs=(tile, tile, tile, tile, tile),
                       scratch=[], args=(ya, yb, h, h, x, wb0, wb1, wo, g1, b1))


def _mix_bwd(dr1, h, a, b, wo, wb0, wb1, tm):
    t = dr1.shape[0]

    def body(dr1_ref, ga_ref, gb_ref, a_ref, b_ref, wo_ref, wb0_ref, wb1_ref,
             da_ref, db_ref, dh3_ref, dya_ref, dyb_ref):
        d_m = _dot(dr1_ref[...], wo_ref[...], NT)
        sa = _sig(ga_ref[...].astype(F32))
        sb = _sig(gb_ref[...].astype(F32))
        d_a = (d_m * sa).astype(BF16)
        d_b = (d_m * sb).astype(BF16)
        da_ref[...] = d_a
        db_ref[...] = d_b
        dh3_ref[:, :D_MODEL] = (d_m * a_ref[...].astype(F32) * sa * (1.0 - sa)).astype(BF16)
        dh3_ref[:, D_MODEL:] = (d_m * b_ref[...].astype(F32) * sb * (1.0 - sb)).astype(BF16)
        dya_ref[...] = _dot(d_a, wb0_ref[...], NT).astype(BF16)
        dyb_ref[...] = _dot(d_b, wb1_ref[...], NT).astype(BF16)

    tile = pl.BlockSpec((tm, D_MODEL), lambda i: (i, 0))
    wsp = pl.BlockSpec((D_MODEL, D_MODEL), lambda i: (0, 0))
    f32o = jax.ShapeDtypeStruct((t, D_MODEL), F32)
    bfo = jax.ShapeDtypeStruct((t, D_MODEL), BF16)
    return _pc(body, name="mix_bwd",
               out_shape=(bfo, bfo, jax.ShapeDtypeStruct((N_CHIP, t, 2 * D_MODEL), BF16), bfo, bfo),
               grid=(t // tm,),
               in_specs=[tile,
                         pl.BlockSpec((tm, D_MODEL), lambda i: (i, 6)),
                         pl.BlockSpec((tm, D_MODEL), lambda i: (i, 7)),
                         tile, tile, wsp, wsp, wsp],
               out_specs=(tile, tile, pl.BlockSpec((None, tm, 2 * D_MODEL), lambda i: (DH_SLOT[3], i, 0)),
                          tile, tile),
               sem=("parallel",))(dr1, h, h, a, b, wo, wb0, wb1)


FF_TILE = 1408
FF_NJ = D_FF // FF_TILE


def _shift_down(v, k):
    return pltpu.roll(v, k, 0)


def _shift_up(v, k):
    return pltpu.roll(v, v.shape[0] - k, 0)


HALO = 16
FF_PIECES = ((0, 768), (768, FF_TILE))


def _ffn_up_act(x1b, wup_st, convw, convb, tm, comm):
    t = x1b.shape[0]
    ni = t // tm
    nth = tm // HALO

    def body(x_ref, xp_ref, wg_ref, wv_ref, cw_ref, cb_ref, h2_ref, act_ref):
        wg = wg_ref[...]
        gate = _dot(x_ref[...], wg).astype(BF16)
        val = _dot(x_ref[...], wv_ref[...]).astype(BF16)
        prev = (_dot(xp_ref[...], wg) * (pl.program_id(0) > 0).astype(F32)).astype(BF16)
        h2_ref[0] = gate
        h2_ref[1] = val
        ext = jnp.concatenate([prev.astype(F32), gate.astype(F32)], axis=0)
        gc = (cw_ref[0:1, :] * _shift_down(ext, 2) + cw_ref[1:2, :] * _shift_down(ext, 1)
              + cw_ref[2:3, :] * ext + cb_ref[...])[HALO:, :]
        act_ref[...] = (_gelu(gc) * val.astype(F32)).astype(BF16)

    res = _hosted_call(
        body, comm,
        lambda: (pl.program_id(0) == 0) & (pl.program_id(1) == 0),
        lambda: (pl.program_id(0) == ni - 1) & (pl.program_id(1) == FF_NJ - 1),
        name="ffn_up",
        out_shape=(jax.ShapeDtypeStruct((2, t, D_FF), BF16), jax.ShapeDtypeStruct((t, D_FF), BF16)),
        grid=(ni, FF_NJ),
        in_specs=[pl.BlockSpec((tm, D_MODEL), lambda i, j: (i, 0)),
                  pl.BlockSpec((HALO, D_MODEL), lambda i, j: (jnp.maximum(i * nth - 1, 0), 0)),
                  pl.BlockSpec((None, D_MODEL, FF_TILE), lambda i, j: (j, 0, 0)),
                  pl.BlockSpec((None, D_MODEL, FF_TILE), lambda i, j: (j + FF_NJ, 0, 0)),
                  pl.BlockSpec((3, FF_TILE), lambda i, j: (0, j)),
                  pl.BlockSpec((1, FF_TILE), lambda i, j: (0, j))],
        out_specs=(pl.BlockSpec((2, tm, FF_TILE), lambda i, j: (0, i, j)),
                   pl.BlockSpec((tm, FF_TILE), lambda i, j: (i, j))),
        scratch=[], sem=("arbitrary", "arbitrary"),
        args=(x1b, x1b, wup_st, wup_st, convw, convb))
    return res[0], res[1], res[2:]


def _out_fwd_bwd(act, x1b, r1, p2, tgt, wd, wpg, wpp, g1, b1, g2, b2, tm):
    t = r1.shape[0]

    def body(act_ref, x1b_ref, r1_ref, p_ref, tgt_ref, wd_ref, wpg_ref, wpp_ref,
             g1_ref, b1_ref, g2_ref, b2_ref,
             dr2_ref, dpg_ref, dpp_ref, loss_ref, dg2_ref, db2_ref):
        i = pl.program_id(0)

        @pl.when(i == 0)
        def _():
            loss_ref[...] = jnp.zeros_like(loss_ref)
            dg2_ref[...] = jnp.zeros_like(dg2_ref)
            db2_ref[...] = jnp.zeros_like(db2_ref)

        ffn = _dot(act_ref[...], wd_ref[...])
        pg = _dot(x1b_ref[...], wpg_ref[...])
        pp = _dot(p_ref[...], wpp_ref[...])
        s = _sig(pg)
        xh1, _ = _ln_stats(r1_ref[...])
        x1 = xh1 * g1_ref[...] + b1_ref[...]
        r2 = ALPHA * x1 + ffn + s * pp
        xh2, rstd2 = _ln_stats(r2)
        g2v = g2_ref[...]
        diff = xh2 * g2v + b2_ref[...] - tgt_ref[...]
        part = jnp.sum(jnp.sum(diff * diff, axis=1, keepdims=True), axis=0, keepdims=True)
        loss_ref[...] += jnp.broadcast_to(part * (0.5 / D_MODEL), loss_ref.shape)
        dy = diff * (1.0 / D_MODEL)
        dg2_ref[...] += _colsum8(dy * xh2)
        db2_ref[...] += _colsum8(dy)
        dr2 = _ln_bwd(dy * g2v, xh2, rstd2)
        dr2_ref[...] = dr2
        dpg_ref[...] = (dr2 * pp * s * (1.0 - s)).astype(BF16)
        dpp_ref[...] = (dr2 * s).astype(BF16)

    tile = pl.BlockSpec((tm, D_MODEL), lambda i: (i, 0))
    vec = pl.BlockSpec((1, D_MODEL), lambda i: (0, 0))
    acc8 = pl.BlockSpec((8, D_MODEL), lambda i: (0, 0))
    acc_shape = jax.ShapeDtypeStruct((8, D_MODEL), F32)
    return _pc(body, name="out_fwd_bwd",
               out_shape=(jax.ShapeDtypeStruct((t, D_MODEL), F32),
                          jax.ShapeDtypeStruct((t, D_MODEL), BF16),
                          jax.ShapeDtypeStruct((t, D_MODEL), BF16),
                          acc_shape, acc_shape, acc_shape),
               grid=(t // tm,),
               in_specs=[pl.BlockSpec((tm, D_FF), lambda i: (i, 0)), tile, tile,
                         pl.BlockSpec((tm, PLE_DIM), lambda i: (i, 0)), tile,
                         pl.BlockSpec((D_FF, D_MODEL), lambda i: (0, 0)),
                         pl.BlockSpec((D_MODEL, D_MODEL), lambda i: (0, 0)),
                         pl.BlockSpec((PLE_DIM, D_MODEL), lambda i: (0, 0)),
                         vec, vec, vec, vec],
               out_specs=(tile, tile, tile, acc8, acc8, acc8),
               sem=("arbitrary",))(act, x1b, r1, p2, tgt, wd, wpg, wpp, g1, b1, g2, b2)


def _ffn_bwd(h2, dr2, wd, wup_st, dpg, wpg, r1, g1, convw, convb, tm):
    t = r1.shape[0]
    ni = t // tm
    nth = tm // HALO
    last_halo = t // HALO - 1
    main_rows = slice(HALO, HALO + tm)

    def body(g_ref, gp_ref, gn_ref, v_ref, vn_ref, dr2_ref, dr2n_ref, wd_ref, wug_ref, wuv_ref,
             cw_ref, cb_ref, dpg_ref, wpg_ref, r1_ref, g1_ref,
             dh2_ref, dr1_ref, dcw_ref, dcb_ref, dg1_ref, db1_ref, acc):
        i = pl.program_id(0)
        j = pl.program_id(1)

        @pl.when((i == 0) & (j == 0))
        def _():
            dcw_ref[...] = jnp.zeros_like(dcw_ref)
            dcb_ref[...] = jnp.zeros_like(dcb_ref)
            dg1_ref[...] = jnp.zeros_like(dg1_ref)
            db1_ref[...] = jnp.zeros_like(db1_ref)

        dr2v = dr2_ref[...].astype(BF16)
        dr2n = dr2n_ref[...].astype(BF16)
        first = (i > 0).astype(F32)
        more = (i < ni - 1).astype(F32)
        prod = None
        dcw_parts, dcb_parts = [], []
        for c0, c1 in FF_PIECES:
            pc = slice(c0, c1)
            zeros = jnp.zeros((HALO, c1 - c0), F32)
            da = _dot(dr2v, wd_ref[pc, :], NT)
            dnext = _dot(dr2n, wd_ref[pc, :], NT) * more
            ext = jnp.concatenate([gp_ref[:, pc].astype(F32) * first, g_ref[:, pc].astype(F32),
                                   gn_ref[:, pc].astype(F32)], axis=0)
            vext = jnp.concatenate([zeros, v_ref[:, pc].astype(F32), vn_ref[:, pc].astype(F32)], axis=0)
            dext = jnp.concatenate([zeros, da, dnext], axis=0)
            g2 = _shift_down(ext, 2)
            g1s = _shift_down(ext, 1)
            gc = cw_ref[0:1, pc] * g2 + cw_ref[1:2, pc] * g1s + cw_ref[2:3, pc] * ext + cb_ref[:, pc]
            gl, dgl = _gelu_and_grad(gc)
            d_gc = dext * vext * dgl
            d_gate = (cw_ref[2:3, pc] * d_gc + cw_ref[1:2, pc] * _shift_up(d_gc, 1)
                      + cw_ref[0:1, pc] * _shift_up(d_gc, 2))[main_rows, :].astype(BF16)
            d_val = (da * gl[main_rows, :]).astype(BF16)
            dh2_ref[0, :, pc] = d_gate
            dh2_ref[1, :, pc] = d_val
            dm = d_gc[main_rows, :]
            s0 = jnp.sum(dm * g2[main_rows, :], axis=0, keepdims=True)
            s1 = jnp.sum(dm * g1s[main_rows, :], axis=0, keepdims=True)
            s2 = jnp.sum(dm * ext[main_rows, :], axis=0, keepdims=True)
            rowid = lax.broadcasted_iota(jnp.int32, (8, c1 - c0), 0)
            dcw_parts.append(jnp.where(rowid == 0, s0, jnp.where(rowid == 1, s1,
                                                                 jnp.where(rowid == 2, s2, 0.0))))
            dcb_parts.append(_colsum8(dm))
            part = _dot(d_gate, wug_ref[:, pc], NT) + _dot(d_val, wuv_ref[:, pc], NT)
            prod = part if prod is None else prod + part
        dcw_part = jnp.concatenate(dcw_parts, axis=1)
        dcb_part = jnp.concatenate(dcb_parts, axis=1)
        for jj in range(FF_NJ):
            @pl.when(j == jj)
            def _(jj=jj):
                cols = slice(jj * FF_TILE, (jj + 1) * FF_TILE)
                dcw_ref[:, cols] += dcw_part
                dcb_ref[:, cols] += dcb_part

        @pl.when(j == 0)
        def _():
            acc[...] = prod

        @pl.when(j > 0)
        def _():
            acc[...] += prod

        @pl.when(j == FF_NJ - 1)
        def _():
            d_x1 = acc[...] + _dot(dpg_ref[...], wpg_ref[...], NT) + ALPHA * dr2_ref[...]
            xh, rstd = _ln_stats(r1_ref[...])
            dg1_ref[...] += _colsum8(d_x1 * xh)
            db1_ref[...] += _colsum8(d_x1)
            dr1_ref[...] = _ln_bwd(d_x1 * g1_ref[...], xh, rstd)

    def h2_main(part):
        return pl.BlockSpec((None, tm, FF_TILE), lambda i, j: (part, i, j))

    def h2_prev(part):
        return pl.BlockSpec((None, HALO, FF_TILE), lambda i, j: (part, jnp.maximum(i * nth - 1, 0), j))

    def h2_next(part):
        return pl.BlockSpec((None, HALO, FF_TILE),
                            lambda i, j: (part, jnp.minimum((i + 1) * nth, last_halo), j))

    tile = pl.BlockSpec((tm, D_MODEL), lambda i, j: (i, 0))
    acc8 = pl.BlockSpec((8, D_MODEL), lambda i, j: (0, 0))
    accff = pl.BlockSpec((8, D_FF), lambda i, j: (0, 0))
    acc_shape = jax.ShapeDtypeStruct((8, D_MODEL), F32)
    accff_shape = jax.ShapeDtypeStruct((8, D_FF), F32)
    return _pc(body, name="ffn_bwd",
               out_shape=(jax.ShapeDtypeStruct((2, t, D_FF), BF16),
                          jax.ShapeDtypeStruct((t, D_MODEL), F32),
                          accff_shape, accff_shape, acc_shape, acc_shape),
               grid=(ni, FF_NJ),
               in_specs=[h2_main(0), h2_prev(0), h2_next(0), h2_main(1), h2_next(1),
                         tile,
                         pl.BlockSpec((HALO, D_MODEL), lambda i, j: (jnp.minimum((i + 1) * nth, last_halo), 0)),
                         pl.BlockSpec((FF_TILE, D_MODEL), lambda i, j: (j, 0)),
                         pl.BlockSpec((None, D_MODEL, FF_TILE), lambda i, j: (j, 0, 0)),
                         pl.BlockSpec((None, D_MODEL, FF_TILE), lambda i, j: (j + FF_NJ, 0, 0)),
                         pl.BlockSpec((3, FF_TILE), lambda i, j: (0, j)),
                         pl.BlockSpec((1, FF_TILE), lambda i, j: (0, j)),
                         tile, pl.BlockSpec((D_MODEL, D_MODEL), lambda i, j: (0, 0)),
                         tile, pl.BlockSpec((1, D_MODEL), lambda i, j: (0, 0))],
               out_specs=(pl.BlockSpec((2, tm, FF_TILE), lambda i, j: (0, i, j)),
                          tile, accff, accff, acc8, acc8),
               scratch=[pltpu.VMEM((tm, D_MODEL), F32)],
               sem=("arbitrary", "arbitrary"))(h2, h2, h2, h2, h2, dr2, dr2, wd, wup_st, wup_st,
                                               convw, convb, dpg, wpg, r1, g1)


ANY = pl.BlockSpec(memory_space=pl.ANY)


def _chip_peers():
    x, y, c = lax.axis_index("x"), lax.axis_index("y"), lax.axis_index("c")
    return x, y, c, [(1 - x, y), (x, 1 - y), (1 - x, 1 - y)]


def _gather_comm(halved, whole=()):
    n, nw = len(halved), len(whole)

    def copies(ins, outs, sems):
        ici_send, ici_recv, d2d_send, d2d_recv, own_send, own_recv = sems
        x, y, c, peers = _chip_peers()
        me = 2 * x + y
        sibling = (x, y, 1 - c)
        own, ici, ici_wait, fwd, fwd_wait = [], [], [], [], []
        for ti in range(n + nw):
            src, dst = ins[ti], outs[ti]
            own.append(pltpu.make_async_remote_copy(
                src_ref=src, dst_ref=dst.at[me], send_sem=own_send.at[ti], recv_sem=own_recv.at[ti],
                device_id=sibling, device_id_type=MESH))
            for k, (px, py) in enumerate(peers):
                pk = 2 * px + py
                sem = dict(send_sem=ici_send.at[ti * 3 + k], recv_sem=ici_recv.at[ti * 3 + k],
                           device_id=(px, py, c), device_id_type=MESH)
                if ti < n:
                    ici.append(pltpu.make_async_remote_copy(src_ref=src.at[c], dst_ref=dst.at[me, c], **sem))
                    ici_wait.append(pltpu.make_async_remote_copy(src_ref=src.at[c], dst_ref=dst.at[pk, c], **sem))
                    dsem = dict(send_sem=d2d_send.at[ti * 3 + k], recv_sem=d2d_recv.at[ti * 3 + k],
                                device_id=sibling, device_id_type=MESH)
                    fwd.append(pltpu.make_async_remote_copy(src_ref=dst.at[pk, c], dst_ref=dst.at[pk, c], **dsem))
                    fwd_wait.append(pltpu.make_async_remote_copy(
                        src_ref=dst.at[pk, 1 - c], dst_ref=dst.at[pk, 1 - c], **dsem))
                else:
                    ici.append(pltpu.make_async_remote_copy(src_ref=src, dst_ref=dst.at[me], **sem))
                    ici_wait.append(pltpu.make_async_remote_copy(src_ref=src, dst_ref=dst.at[pk], **sem))
        return own, ici, ici_wait, fwd, fwd_wait

    def start(ins, outs, sems):
        own, ici, _, _, _ = copies(ins, outs, sems)
        for cp in own + ici:
            cp.start()

    def finish(ins, outs, sems):
        own, ici, ici_wait, fwd, fwd_wait = copies(ins, outs, sems)
        for i, cp in enumerate(ici_wait):
            cp.wait_recv()
            if i < len(fwd):
                fwd[i].start()
        for cp in fwd_wait + own:
            cp.wait_recv()
        for cp in own + ici + fwd:
            cp.wait_send()

    srcs = list(halved) + list(whole)
    return _Comm(srcs, [jax.ShapeDtypeStruct((N_CHIP,) + s.shape, s.dtype) for s in srcs],
                 [pltpu.SemaphoreType.DMA((3 * (n + nw),)), pltpu.SemaphoreType.DMA((3 * (n + nw),)),
                  pltpu.SemaphoreType.DMA((max(3 * n, 1),)), pltpu.SemaphoreType.DMA((max(3 * n, 1),)),
                  pltpu.SemaphoreType.DMA((n + nw,)), pltpu.SemaphoreType.DMA((n + nw,))],
                 start, finish)


def _sibling_exchange_comm(grads):
    n = len(grads)

    def copies(ins, outs, sems):
        send_sems, recv_sems = sems
        x, y, c = lax.axis_index("x"), lax.axis_index("y"), lax.axis_index("c")
        res = []
        for ti in range(n):
            half = ins[ti].shape[1] // 2
            res.append(pltpu.make_async_remote_copy(
                src_ref=ins[ti].at[:, pl.ds(pl.multiple_of((1 - c) * half, 16), half), :],
                dst_ref=outs[ti],
                send_sem=send_sems.at[ti], recv_sem=recv_sems.at[ti],
                device_id=(x, y, 1 - c), device_id_type=MESH))
        return res

    def start(ins, outs, sems):
        for cp in copies(ins, outs, sems):
            cp.start()

    def finish(ins, outs, sems):
        for cp in copies(ins, outs, sems):
            cp.wait()

    return _Comm(grads, [jax.ShapeDtypeStruct((N_CHIP, g.shape[1] // 2, g.shape[2]), g.dtype) for g in grads],
                 [pltpu.SemaphoreType.DMA((n,)), pltpu.SemaphoreType.DMA((n,))], start, finish)


def _in_proj_gathering(x2b, own, chip, tm, comm):
    t = x2b.shape[0]
    ni = t // tm
    half, cols = own.shape[1], own.shape[2]
    nci, nco = len(comm.ins), len(comm.out_shapes)

    def body(chip_ref, x_ref, own_ref, own_hbm, *rest):
        c_in = rest[:nci]
        h_ref, win_out = rest[nci:nci + 2]
        c_out = rest[nci + 2:nci + 2 + nco]
        w_scr, ici_send, ici_recv, d2d_send, d2d_recv, own_sems, ld_sems = rest[nci + 2 + nco:nci + 9 + nco]
        c_sem = rest[nci + 9 + nco:]
        s, i = pl.program_id(0), pl.program_id(1)
        x, y, c, peers = _chip_peers()
        me = 2 * x + y
        sibling = (x, y, 1 - c)

        def ici(k, slot):
            px, py = peers[k]
            return pltpu.make_async_remote_copy(
                src_ref=own_hbm.at[c], dst_ref=win_out.at[slot, c],
                send_sem=ici_send.at[k], recv_sem=ici_recv.at[k],
                device_id=(px, py, c), device_id_type=MESH)

        def forward(k, core):
            pk = 2 * peers[k][0] + peers[k][1]
            return pltpu.make_async_remote_copy(
                src_ref=win_out.at[pk, core], dst_ref=win_out.at[pk, core],
                send_sem=d2d_send.at[k], recv_sem=d2d_recv.at[k],
                device_id=sibling, device_id_type=MESH)

        place_own = pltpu.make_async_remote_copy(
            src_ref=own_hbm, dst_ref=win_out.at[me], send_sem=own_sems.at[0], recv_sem=own_sems.at[1],
            device_id=sibling, device_id_type=MESH)

        @pl.when((s == 0) & (i == 0))
        def _():
            for k in range(2):
                ici(k, me).start()
            place_own.start()

        @pl.when(s == 0)
        def _():
            xv = x_ref[...]
            h_ref[...] = (_dot(xv[:, :half], own_ref[0]) + _dot(xv[:, half:], own_ref[1])).astype(BF16)

        for k in range(3):
            @pl.when((s == k + 1) & (i == 0))
            def _(k=k):
                pk = 2 * peers[k][0] + peers[k][1]
                ici(k, pk).wait_recv()
                if k == 0:
                    ici(2, me).start()
                forward(k, c).start()
                forward(k, 1 - c).wait_recv()
                loads = [pltpu.make_async_copy(win_out.at[pk, hh], w_scr.at[hh], ld_sems.at[hh])
                         for hh in range(2)]
                for ld in loads:
                    ld.start()
                for ld in loads:
                    ld.wait()
                if k == 1:
                    comm.start(c_in, c_out, c_sem)

        @pl.when(s > 0)
        def _():
            xv = x_ref[...]
            h_ref[...] = (_dot(xv[:, :half], w_scr[0]) + _dot(xv[:, half:], w_scr[1])).astype(BF16)

        @pl.when((s == N_CHIP - 1) & (i == ni - 1))
        def _():
            place_own.wait()
            for k in range(3):
                ici(k, me).wait_send()
                forward(k, c).wait_send()
            comm.finish(c_in, c_out, c_sem)

    def shard_col(s, me):
        return jnp.where(s == 0, me, me ^ jnp.where(s == 1, 2, jnp.where(s == 2, 1, 3)))

    res = _pc(body, name="in_proj",
              out_shape=(jax.ShapeDtypeStruct((t, N_CHIP * cols), BF16),
                         jax.ShapeDtypeStruct((N_CHIP,) + own.shape, own.dtype)) + tuple(comm.out_shapes),
              grid=(N_CHIP, ni), nsp=1,
              in_specs=[pl.BlockSpec((tm, 2 * half), lambda s, i, chip_ref: (i, 0)),
                        pl.BlockSpec(own.shape, lambda s, i, chip_ref: (0, 0, 0)),
                        ANY] + [ANY] * nci,
              out_specs=(pl.BlockSpec((tm, cols), lambda s, i, chip_ref: (i, shard_col(s, chip_ref[0]))),
                         ANY) + tuple([ANY] * nco),
              scratch=[pltpu.VMEM(own.shape, own.dtype),
                       pltpu.SemaphoreType.DMA((3,)), pltpu.SemaphoreType.DMA((3,)),
                       pltpu.SemaphoreType.DMA((3,)), pltpu.SemaphoreType.DMA((3,)),
                       pltpu.SemaphoreType.DMA((2,)), pltpu.SemaphoreType.DMA((2,))] + comm.sems,
              sem=("arbitrary", "arbitrary"))(chip, x2b, own, own, *comm.ins)
    return res[0], res[1], res[2:]


def _rs_add_halves(name, grad, recv, core):
    _, r, cdim = grad.shape
    half = r // 2
    tr = _row_tile(half, cdim, mult=16)
    nr = half // tr

    def body(c_ref, g_ref, r_ref, o_ref):
        o_ref[...] = (g_ref[...].astype(F32) + r_ref[...].astype(F32)).astype(BF16)

    return _pc(body, name=name, out_shape=jax.ShapeDtypeStruct((N_CHIP, half, cdim), BF16),
               grid=(N_CHIP, nr), nsp=1,
               in_specs=[pl.BlockSpec((None, tr, cdim), lambda j, i, c_ref: (j, c_ref[0] * nr + i, 0)),
                         pl.BlockSpec((None, tr, cdim), lambda j, i, c_ref: (j, i, 0))],
               out_specs=pl.BlockSpec((None, tr, cdim), lambda j, i, c_ref: (j, i, 0)),
               sem=("parallel", "parallel"))(core, grad, recv)


def _chip_exchange_comm(parts):
    n = len(parts)

    def copies(ins, outs, sems):
        send_sems, recv_sems = sems
        x, y, c, peers = _chip_peers()
        return [pltpu.make_async_remote_copy(
            src_ref=ins[ti].at[2 * px + py], dst_ref=outs[ti].at[k],
            send_sem=send_sems.at[ti * 3 + k], recv_sem=recv_sems.at[ti * 3 + k],
            device_id=(px, py, c), device_id_type=MESH)
            for ti in range(n) for k, (px, py) in enumerate(peers)]

    def start(ins, outs, sems):
        for cp in copies(ins, outs, sems):
            cp.start()

    def finish(ins, outs, sems):
        for cp in copies(ins, outs, sems):
            cp.wait()

    return _Comm(parts, [jax.ShapeDtypeStruct((3,) + p.shape[1:], p.dtype) for p in parts],
                 [pltpu.SemaphoreType.DMA((3 * n,)), pltpu.SemaphoreType.DMA((3 * n,))], start, finish)


def _rs_sum_chips(name, part, recv, chip):
    _, half, cdim = recv.shape
    tr = _row_tile(half, cdim, mult=16)

    def body(chip_ref, p_ref, r_ref, o_ref):
        o_ref[...] = ((p_ref[...].astype(F32) + r_ref[0].astype(F32)) + r_ref[1].astype(F32)
                      ) + r_ref[2].astype(F32)

    return _pc(body, name=name, out_shape=jax.ShapeDtypeStruct((half, cdim), F32),
               grid=(half // tr,), nsp=1,
               in_specs=[pl.BlockSpec((None, tr, cdim), lambda i, chip_ref: (chip_ref[0], i, 0)),
                         pl.BlockSpec((3, tr, cdim), lambda i, chip_ref: (0, i, 0))],
               out_specs=pl.BlockSpec((tr, cdim), lambda i, chip_ref: (i, 0)),
               sem=("parallel",))(chip, part, recv)


def _rs_send_halves(halves):
    n = len(halves)

    def body(*refs):
        ins, outs = refs[:n], refs[n:2 * n]
        send_sems, recv_sems = refs[2 * n:]
        x, y, c = lax.axis_index("x"), lax.axis_index("y"), lax.axis_index("c")
        sends = []
        for ti in range(n):
            cp = pltpu.make_async_remote_copy(
                src_ref=ins[ti], dst_ref=outs[ti],
                send_sem=send_sems.at[ti], recv_sem=recv_sems.at[ti],
                device_id=(x, y, 1 - c), device_id_type=MESH)
            cp.start()
            sends.append(cp)
        for cp in sends:
            cp.wait()

    return _pc(body, name="rs_send_halves",
               out_shape=tuple(jax.ShapeDtypeStruct(hv.shape, hv.dtype) for hv in halves),
               in_specs=[ANY] * n, out_specs=tuple([ANY] * n),
               scratch=[pltpu.SemaphoreType.DMA((n,)), pltpu.SemaphoreType.DMA((n,))])(*halves)


def _adamw_rows(name, mine, theirs, w, m, v, core):
    half, cdim = mine.shape
    tr = _row_tile(half, cdim, budget=1 << 19)
    nrh = half // tr

    def body(c_ref, mine_ref, theirs_ref, w_ref, m_ref, v_ref, g_ref, d_ref, m2_ref, v2_ref):
        is_mine = (pl.program_id(0) // nrh) == c_ref[0]
        g = jnp.where(is_mine, mine_ref[...], theirs_ref[...])
        d, m2, v2 = _adamw(w_ref[...], g, m_ref[...], v_ref[...])
        g_ref[...] = g
        d_ref[...] = d
        m2_ref[...] = m2
        v2_ref[...] = v2

    htile = pl.BlockSpec((tr, cdim), lambda i, c_ref: (i % nrh, 0))
    tile = pl.BlockSpec((tr, cdim), lambda i, c_ref: (i, 0))
    shp = jax.ShapeDtypeStruct((2 * half, cdim), F32)
    return _pc(body, name=name, out_shape=(shp, shp, shp, shp), grid=(2 * nrh,), nsp=1,
               in_specs=[htile, htile, tile, tile, tile], out_specs=(tile, tile, tile, tile),
               sem=("parallel",))(core, mine, theirs, w, m, v)


def _adamw_whole(name, g, w, m, v):
    def body(g_ref, w_ref, m_ref, v_ref, d_ref, m2_ref, v2_ref):
        d, m2, v2 = _adamw(w_ref[...], g_ref[...], m_ref[...], v_ref[...])
        d_ref[...] = d
        m2_ref[...] = m2
        v2_ref[...] = v2

    shp = jax.ShapeDtypeStruct(g.shape, F32)
    return _pc(body, name=name, out_shape=(shp, shp, shp))(g, w, m, v)


SMALL_LAYOUT = (
    ("sgu_w_s", 1024, 1, 0),
    ("sgu_b_s", 8, 1, 1024),
    ("sgu_norm_g", 1, 0, 0),
    ("sgu_norm_b", 1, 0, 1),
    ("hgrn_norm_g", 1, 0, 3),
    ("ln1_g", 1, 0, 4),
    ("ln1_b", 1, 0, 5),
    ("ffn_conv_b", 1, 2, 3),
    ("ln2_g", 1, 0, 6),
    ("ln2_b", 1, 0, 7),
)
LB_ROW = 2
LOSS_ROW = 8
PACK_SHAPES = ((16, D_MODEL), (N_GROUP * 128 + 8, 128), (8, D_FF))


def _small_allreduce_adamw(rows1024, dws, dbs, dcw, dcb, logits, m_logits, v_logits,
                           small_w, small_m, small_v):
    ns = len(SMALL_LAYOUT)
    nr = len(rows1024)
    nb = len(PACK_SHAPES)

    def body(*refs):
        row_refs = refs[:nr]
        dws_ref, dbs_ref, dcw_ref, dcb_ref, lg_ref, mlg_ref, vlg_ref = refs[nr:nr + 7]
        pos = nr + 7
        w_refs = refs[pos:pos + ns]
        m_refs = refs[pos + ns:pos + 2 * ns]
        v_refs = refs[pos + 2 * ns:pos + 3 * ns]
        pos += 3 * ns
        loss_ref, dcw_out = refs[pos:pos + 2]
        lg_outs = refs[pos + 2:pos + 6]
        pos += 6
        outs = refs[pos:pos + 4 * ns]
        pos += 4 * ns
        pack = refs[pos:pos + nb]
        sib = refs[pos + nb:pos + 2 * nb]
        gath = refs[pos + 2 * nb:pos + 3 * nb]
        d2d_send, d2d_recv, ici_send, ici_recv = refs[pos + 3 * nb:]

        x, y, c, peers = _chip_peers()
        me = 2 * x + y
        sibling = (x, y, 1 - c)

        pack[0][...] = jnp.zeros(PACK_SHAPES[0], F32)
        for k in range(nr):
            pack[0][k:k + 1, :] = row_refs[k][0:1, :]
        pack[1][0:N_GROUP * 128, :] = dws_ref[...]
        pack[1][N_GROUP * 128:, :] = dbs_ref[...]
        pack[2][...] = jnp.zeros(PACK_SHAPES[2], F32)
        pack[2][0:3, :] = dcw_ref[0:3, :]
        pack[2][3:4, :] = dcb_ref[0:1, :]

        d2d = [pltpu.make_async_remote_copy(
            src_ref=pack[b], dst_ref=sib[b], send_sem=d2d_send.at[b], recv_sem=d2d_recv.at[b],
            device_id=sibling, device_id_type=MESH) for b in range(nb)]
        for cp in d2d:
            cp.start()
        for cp in d2d:
            cp.wait()
        for b in range(nb):
            gath[b][me] = pack[b][...] + sib[b][...]

        ici, ici_wait = [], []
        for b in range(nb):
            for k, (px, py) in enumerate(peers):
                sem = dict(send_sem=ici_send.at[b * 3 + k], recv_sem=ici_recv.at[b * 3 + k],
                           device_id=(px, py, c), device_id_type=MESH)
                ici.append(pltpu.make_async_remote_copy(src_ref=gath[b].at[me], dst_ref=gath[b].at[me], **sem))
                ici_wait.append(pltpu.make_async_remote_copy(
                    src_ref=gath[b].at[me], dst_ref=gath[b].at[2 * px + py], **sem))
        for cp in ici:
            cp.start()
        for cp in ici_wait:
            cp.wait_recv()
        for cp in ici:
            cp.wait_send()

        tot = pack
        for b in range(nb):
            tot[b][...] = ((gath[b][0] + gath[b][1]) + gath[b][2]) + gath[b][3]

        loss_ref[...] = tot[0][LOSS_ROW:LOSS_ROW + 1, :]
        dcw_out[...] = tot[2][...]
        lb = _sig(lg_ref[0:1, :] - lg_ref[1:2, :])
        d0 = tot[0][LB_ROW:LB_ROW + 1, :] * lb * (1.0 - lb)
        rowid = lax.broadcasted_iota(jnp.int32, (2, D_MODEL), 0)
        g_lg = jnp.where(rowid == 0, d0, -d0)
        dl, ml, vl = _adamw(lg_ref[...], g_lg, mlg_ref[...], vlg_ref[...])
        lg_outs[0][...] = g_lg
        lg_outs[1][...] = dl
        lg_outs[2][...] = ml
        lg_outs[3][...] = vl
        for si, (_, rows, b, r0) in enumerate(SMALL_LAYOUT):
            g = tot[b][r0:r0 + rows, :]
            dl, ml, vl = _adamw(w_refs[si][...], g, m_refs[si][...], v_refs[si][...])
            outs[4 * si][...] = g
            outs[4 * si + 1][...] = dl
            outs[4 * si + 2][...] = ml
            outs[4 * si + 3][...] = vl

    shapes = [jax.ShapeDtypeStruct((1, D_MODEL), F32), jax.ShapeDtypeStruct((8, D_FF), F32)]
    shapes += [jax.ShapeDtypeStruct((2, D_MODEL), F32)] * 4
    for w in small_w:
        shapes += [jax.ShapeDtypeStruct(w.shape, F32)] * 4
    scratch = [pltpu.VMEM(shp, F32) for shp in PACK_SHAPES]
    scratch += [pltpu.VMEM(shp, F32) for shp in PACK_SHAPES]
    scratch += [pltpu.VMEM((N_CHIP,) + shp, F32) for shp in PACK_SHAPES]
    scratch += [pltpu.SemaphoreType.DMA((nb,)), pltpu.SemaphoreType.DMA((nb,)),
                pltpu.SemaphoreType.DMA((3 * nb,)), pltpu.SemaphoreType.DMA((3 * nb,))]
    vm = pl.BlockSpec(memory_space=pltpu.VMEM)
    n_in = nr + 7 + 3 * ns
    res = _pc(body, name="small_allreduce_adamw", out_shape=tuple(shapes),
              in_specs=[vm] * n_in, out_specs=tuple([vm] * len(shapes)),
              scratch=scratch)(*rows1024, dws, dbs, dcw, dcb, logits, m_logits, v_logits,
                               *small_w, *small_m, *small_v)
    return res[0], res[1], res[2:6], res[6:]


def kernel(x, p, w_in, sgu_w_s, sgu_b_s, sgu_norm_g, sgu_norm_b, hgrn_lb_logits, hgrn_norm_g, w_branch, w_out, ln1_g, ln1_b, ffn_w_up, ffn_conv_w, ffn_conv_b, ffn_w_down, ln2_g, ln2_b, ple_w_proj, ple_w_gate, loss_target, m_w_in, m_sgu_w_s, m_sgu_b_s, m_sgu_norm_g, m_sgu_norm_b, m_hgrn_lb_logits, m_hgrn_norm_g, m_w_branch, m_w_out, m_ln1_g, m_ln1_b, m_ffn_w_up, m_ffn_conv_w, m_ffn_conv_b, m_ffn_w_down, m_ln2_g, m_ln2_b, m_ple_w_proj, m_ple_w_gate, v_w_in, v_sgu_w_s, v_sgu_b_s, v_sgu_norm_g, v_sgu_norm_b, v_hgrn_lb_logits, v_hgrn_norm_g, v_w_branch, v_w_out, v_ln1_g, v_ln1_b, v_ffn_w_up, v_ffn_conv_w, v_ffn_conv_b, v_ffn_w_down, v_ln2_g, v_ln2_b, v_ple_w_proj, v_ple_w_gate):
    t = x.shape[1]
    x2 = x.reshape(t, D_MODEL)
    x2b = x2.astype(BF16)
    p2 = p.reshape(t, PLE_DIM)
    tgt = loss_target.reshape(t, D_MODEL)
    core = lax.axis_index("c").astype(jnp.int32).reshape(1)
    chip_id = (2 * lax.axis_index("x") + lax.axis_index("y")).astype(jnp.int32).reshape(1)

    big_w = [w_in[0], w_branch[0, 0], w_branch[0, 1], w_out[0], ffn_w_up[0], ffn_w_down[0],
             ple_w_proj[0], ple_w_gate[0]]
    big_m = [m_w_in[0], m_w_branch[0, 0], m_w_branch[0, 1], m_w_out[0], m_ffn_w_up[0],
             m_ffn_w_down[0], m_ple_w_proj[0], m_ple_w_gate[0]]
    big_v = [v_w_in[0], v_w_branch[0, 0], v_w_branch[0, 1], v_w_out[0], v_ffn_w_up[0],
             v_ffn_w_down[0], v_ple_w_proj[0], v_ple_w_gate[0]]
    def halves_of(i):
        w = big_w[i]
        return w.astype(BF16).reshape(2, w.shape[0] // 2, w.shape[1])

    def stacked(g, i):
        return g.reshape(N_CHIP, big_w[i].shape[0], big_w[i].shape[1])


    cid = jnp.arange(SGU_BLOCK) // CHUNK
    maskf = (cid[:, None] >= cid[None, :]).astype(F32)
    ws_masked = sgu_w_s[0] * maskf[None]
    wm = ws_masked.astype(BF16)
    wmt = jnp.transpose(ws_masked, (0, 2, 1)).astype(BF16)
    bsb = jnp.broadcast_to(sgu_b_s[0][:, :, None], (N_GROUP, SGU_BLOCK, 128))

    up_rows = big_w[4].shape[0] // 2
    up_blocks = [big_w[4][k * up_rows:(k + 1) * up_rows].astype(BF16).reshape(2, up_rows // 2, -1)
                 for k in range(2)]
    h, win_g, (up0_g,) = _in_proj_gathering(x2b, halves_of(0), chip_id, 512, _gather_comm([up_blocks[0]]))
    win_st = stacked(win_g, 0)
    ya, _ = _sgu_fwd(h, wm, bsb, sgu_norm_g, sgu_norm_b)
    (yb, o_all, st_all), mix_g = _hgrn_fwd(
        h, hgrn_lb_logits, hgrn_norm_g,
        comm=_gather_comm([halves_of(i) for i in (1, 2, 3)] + [up_blocks[1]], [ffn_conv_w[0]]))
    wb0, wb1, wo = [stacked(g, i).reshape(D_MODEL, D_MODEL) for g, i in zip(mix_g[:3], (1, 2, 3))]
    wup_st = jnp.concatenate([g.reshape(N_CHIP, up_rows, -1) for g in (up0_g, mix_g[3])], axis=1)
    convw = jnp.transpose(mix_g[4], (1, 0, 2)).reshape(3, D_FF)
    (r1, a_br, b_br, m_bf, x1b), _ = _mix_fwd(ya, yb, h, x2, wb0, wb1, wo, ln1_g, ln1_b, 256)
    h2, act, out_g = _ffn_up_act(x1b, wup_st, convw, ffn_conv_b, 512,
                                 _gather_comm([halves_of(i) for i in (5, 6, 7)]))
    wd = stacked(out_g[0], 5).reshape(D_FF, D_MODEL)
    wpp = jnp.transpose(stacked(out_g[1], 6), (1, 0, 2)).reshape(PLE_DIM, D_MODEL)
    wpg = stacked(out_g[2], 7).reshape(D_MODEL, D_MODEL)
    dr2, dpg, dpp, loss_acc, dg2, db2 = _out_fwd_bwd(
        act, x1b, r1, p2, tgt, wd, wpg, wpp, ln1_g, ln1_b, ln2_g, ln2_b, 256)

    dh2, dr1, dcw, dcb, dg1, db1 = _ffn_bwd(h2, dr2, wd, wup_st, dpg, wpg, r1, ln1_g, convw, ffn_conv_b, 256)
    d_wd = _mm_tn("ffn_down_wgrad", act, dr2, FF_TILE, 512)
    d_wpg = _mm_tn("ple_gate_wgrad", x1b, dpg, 512, D_MODEL)
    d_wpp_st = _mm_tn("ple_proj_wgrad", p2, dpp, PLE_DIM, PLE_DIM, stacked=True)
    d_wup_st = _mm("ffn_up_wgrad", x1b, dh2, TN, (2, N_CHIP),
                   pl.BlockSpec((t, 512), lambda i, j: (0, i)),
                   pl.BlockSpec((None, t, FF_TILE), lambda i, j: (j // FF_NJ, 0, j % FF_NJ)),
                   jax.ShapeDtypeStruct((N_CHIP, D_MODEL, FF_TILE), BF16),
                   pl.BlockSpec((None, 512, FF_TILE), lambda i, j: (j, i, 0)))
    da_bf, db_bf, dh, dya, dyb = _mix_bwd(dr1, h, a_br, b_br, wo, wb0, wb1, 256)
    d_wo = _mm_tn("out_proj_wgrad", m_bf, dr1, 512, 512)
    d_wb0 = _mm_tn("branch0_wgrad", ya, da_bf, 512, D_MODEL)
    d_wb1 = _mm_tn("branch1_wgrad", yb, db_bf, 512, D_MODEL)
    grads_1 = [d_wb0.reshape(4, 256, D_MODEL), d_wb1.reshape(4, 256, D_MODEL),
               d_wo.reshape(4, 256, D_MODEL), d_wup_st, d_wd.reshape(4, D_FF // 4, D_MODEL),
               d_wpp_st, d_wpg.reshape(4, 256, D_MODEL)]
    (dh, dws, dbs, dgv, dbv), recv_a1 = _sgu_bwd(h, dya, wm, wmt, bsb, sgu_norm_g, sgu_norm_b, maskf, dh,
                                                 comm=_sibling_exchange_comm(grads_1))
    parts_1 = [_rs_add_halves("rs_add_halves%d" % (i + 1), g, r, core)
               for i, (g, r) in enumerate(zip(grads_1, recv_a1))]
    (dh, dlb, dgn), recv_b1 = _hgrn_bwd(h, o_all, dyb, st_all, hgrn_lb_logits, hgrn_norm_g, dh,
                                         comm=_chip_exchange_comm(parts_1))

    grads_0 = [_in_proj_wgrad(x2b, dh, 512, D_MODEL)]
    recv_a0 = _run_comm("rs_sibling_exchange0", _sibling_exchange_comm(grads_0))
    parts_0 = [_rs_add_halves("rs_add_halves0", grads_0[0], recv_a0[0], core)]
    gx, recv_b0 = _in_proj_xgrad(dh, win_st, dr1, 512, _chip_exchange_comm(parts_0))
    parts = parts_0 + parts_1
    recv_b = list(recv_b0) + list(recv_b1)
    halves = [_rs_sum_chips("rs_sum_chips%d" % i, pt, r, chip_id)
              for i, (pt, r) in enumerate(zip(parts, recv_b))]
    theirs = _rs_send_halves(halves)
    big_out = [_adamw_rows("adamw_big%d" % i, halves[i], theirs[i], big_w[i], big_m[i], big_v[i], core)
               for i in range(len(halves))]

    small_in = dict(sgu_w_s=(sgu_w_s, m_sgu_w_s, v_sgu_w_s), sgu_b_s=(sgu_b_s, m_sgu_b_s, v_sgu_b_s),
                    sgu_norm_g=(sgu_norm_g, m_sgu_norm_g, v_sgu_norm_g),
                    sgu_norm_b=(sgu_norm_b, m_sgu_norm_b, v_sgu_norm_b),
                    hgrn_norm_g=(hgrn_norm_g, m_hgrn_norm_g, v_hgrn_norm_g),
                    ln1_g=(ln1_g, m_ln1_g, v_ln1_g), ln1_b=(ln1_b, m_ln1_b, v_ln1_b),
                    ffn_conv_b=(ffn_conv_b, m_ffn_conv_b, v_ffn_conv_b),
                    ln2_g=(ln2_g, m_ln2_g, v_ln2_g), ln2_b=(ln2_b, m_ln2_b, v_ln2_b))

    def flat(name, arr):
        rows = dict((n, r) for n, r, _, _ in SMALL_LAYOUT)[name]
        return arr.reshape(rows, arr.size // rows)

    names = [n for n, _, _, _ in SMALL_LAYOUT]
    sw = [flat(n, small_in[n][0]) for n in names]
    sm = [flat(n, small_in[n][1]) for n in names]
    sv = [flat(n, small_in[n][2]) for n in names]
    loss_rows, dcw_tot, lg_out, small_out = _small_allreduce_adamw(
        [dgv, dbv, dlb, dgn, dg1, db1, dg2, db2, loss_acc], dws.reshape(N_GROUP * 128, 128), dbs, dcw, dcb,
        hgrn_lb_logits, m_hgrn_lb_logits, v_hgrn_lb_logits, sw, sm, sv)
    loss = loss_rows[0, 0]

    chip = 2 * lax.axis_index("x") + lax.axis_index("y")
    g_cw = lax.dynamic_slice(dcw_tot, (0, chip * (D_FF // 4)), (3, D_FF // 4))
    cw_out = _adamw_whole("adamw_conv_w", g_cw, ffn_conv_w[0], m_ffn_conv_w[0], v_ffn_conv_w[0])

    res = {}
    for si, n in enumerate(names):
        shp = small_in[n][0].shape
        res[n] = tuple(small_out[4 * si + k].reshape(shp) for k in range(4))
    res["hgrn_lb_logits"] = tuple(lg_out)
    res["ffn_conv_w"] = (g_cw[None],) + tuple(o[None] for o in cw_out)

    def big(i):
        return tuple(big_out[i])

    res["w_in"] = tuple(o[None] for o in big(0))
    res["w_branch"] = tuple(jnp.stack([o0, o1])[None] for o0, o1 in zip(big(1), big(2)))
    res["w_out"] = tuple(o[None] for o in big(3))
    res["ffn_w_up"] = tuple(o[None] for o in big(4))
    res["ffn_w_down"] = tuple(o[None] for o in big(5))
    res["ple_w_proj"] = tuple(o[None] for o in big(6))
    res["ple_w_gate"] = tuple(o[None] for o in big(7))

    order = ["w_in", "sgu_w_s", "sgu_b_s", "sgu_norm_g", "sgu_norm_b", "hgrn_lb_logits",
             "hgrn_norm_g", "w_branch", "w_out", "ln1_g", "ln1_b", "ffn_w_up", "ffn_conv_w",
             "ffn_conv_b", "ffn_w_down", "ln2_g", "ln2_b", "ple_w_proj", "ple_w_gate"]
    outs = [loss, gx.reshape(1, t, D_MODEL)]
    for k in range(4):
        outs += [res[n][k] for n in order]
    return tuple(outs)
```

```python
import functools

import jax
import jax.numpy as jnp
from jax import lax
from jax.experimental import pallas as pl
from jax.experimental.pallas import tpu as pltpu

F32 = jnp.float32
BF16 = jnp.bfloat16
HIGHEST = lax.Precision.HIGHEST
MESH = pl.DeviceIdType.MESH

D_MODEL = 1024
CHUNK = 64
SGU_BLOCK = 128
SGU_STEP_BLOCKS = 2
SGU_ROWS = SGU_STEP_BLOCKS * SGU_BLOCK
N_GROUP = 8
N_HEAD = 8
HEAD_DIM = 128
D_FF = 2816
PLE_DIM = 256
IN_COLS = 8192
LN_EPS = 1e-5
RMS_EPS = 1e-6
ALPHA = 2.0 ** 0.25
N_CHIP = 4
N_DEV = 8

ADAM_LR = 0.001
ADAM_B1 = 0.9
ADAM_B2 = 0.999
ADAM_EPS = 1e-08
ADAM_WD = 0.01
ADAM_STEP = 10

VMEM_LIMIT = 56 * 1024 * 1024

NN = (((1,), (0,)), ((), ()))
NT = (((1,), (1,)), ((), ()))
TN = (((0,), (0,)), ((), ()))


def _pc(body, *, name, out_shape, grid=None, in_specs=None, out_specs=None, scratch=(),
        sem=None, nsp=0, vmem=VMEM_LIMIT, aliases=None):
    params = dict(vmem_limit_bytes=vmem)
    if sem is not None:
        params["dimension_semantics"] = sem
    kw = dict(name=name, out_shape=out_shape, compiler_params=pltpu.CompilerParams(**params))
    if aliases:
        kw["input_output_aliases"] = aliases
    if nsp:
        kw["grid_spec"] = pltpu.PrefetchScalarGridSpec(
            num_scalar_prefetch=nsp, grid=grid, in_specs=in_specs, out_specs=out_specs,
            scratch_shapes=list(scratch))
    else:
        if grid is not None:
            kw["grid"] = grid
        if in_specs is not None:
            kw["in_specs"] = in_specs
            kw["out_specs"] = out_specs
        kw["scratch_shapes"] = list(scratch)
    return pl.pallas_call(body, **kw)


def _dot(a, b, dims=NN):
    return lax.dot_general(a.astype(BF16), b.astype(BF16), dims, preferred_element_type=F32)


def _dot32(a, b, dims=NN):
    return lax.dot_general(a, b, dims, precision=HIGHEST, preferred_element_type=F32)


def _sig(x):
    return 1.0 / (1.0 + jnp.exp(-x))


_GC = 0.7978845608028654
_GA = 0.044715


def _gelu(x):
    return 0.5 * x * (1.0 + jnp.tanh(_GC * (x + _GA * x * x * x)))


def _gelu_and_grad(x):
    t = jnp.tanh(_GC * (x + _GA * x * x * x))
    g = 0.5 * x * (1.0 + t)
    dg = 0.5 * (1.0 + t) + 0.5 * x * (1.0 - t * t) * _GC * (1.0 + 3.0 * _GA * x * x)
    return g, dg


def _ln_stats(r):
    mu = jnp.mean(r, axis=-1, keepdims=True)
    xc = r - mu
    var = jnp.mean(xc * xc, axis=-1, keepdims=True)
    rstd = lax.rsqrt(var + LN_EPS)
    return xc * rstd, rstd


def _ln_bwd(dxh, xh, rstd):
    m1 = jnp.mean(dxh, axis=-1, keepdims=True)
    m2 = jnp.mean(dxh * xh, axis=-1, keepdims=True)
    return rstd * (dxh - m1 - xh * m2)


def _colsum8(v):
    return jnp.broadcast_to(jnp.sum(v, axis=0, keepdims=True), (8, v.shape[1]))


def _adamw(w, g, m, v):
    m2 = ADAM_B1 * m + (1.0 - ADAM_B1) * g
    v2 = ADAM_B2 * v + (1.0 - ADAM_B2) * (g * g)
    m_hat = m2 / (1.0 - ADAM_B1 ** ADAM_STEP)
    v_hat = v2 / (1.0 - ADAM_B2 ** ADAM_STEP)
    delta = -ADAM_LR * (m_hat / (jnp.sqrt(v_hat) + ADAM_EPS) + ADAM_WD * w)
    return delta, m2, v2


def _row_tile(rows, cols, itemsize=4, budget=1 << 20, mult=8):
    best = mult
    for tr in range(mult, rows + 1, mult):
        if rows % tr == 0 and tr * cols * itemsize <= budget:
            best = tr
    return best


def _mm(name, a, b, dims, grid, a_spec, b_spec, out_shape, o_spec):
    out_dtype = out_shape.dtype

    def body(a_ref, b_ref, o_ref):
        o_ref[...] = _dot(a_ref[...], b_ref[...], dims).astype(out_dtype)

    return _pc(body, name=name, out_shape=out_shape, grid=grid, in_specs=[a_spec, b_spec],
               out_specs=o_spec, sem=("parallel", "parallel"))(a, b)


class _Comm:
    def __init__(self, ins, out_shapes, sems, start, finish):
        self.ins, self.out_shapes, self.sems = list(ins), list(out_shapes), list(sems)
        self.start, self.finish = start, finish


def _hosted_call(body, comm, first, last, *, name, out_shape, grid, in_specs, out_specs, scratch, sem,
                 args, aliases=None):
    n_in, n_out, n_scr = len(in_specs), len(out_shape), len(scratch)
    nci, nco = len(comm.ins), len(comm.out_shapes)

    def wrapped(*refs):
        pos = n_in
        own_in, c_in = refs[:pos], refs[pos:pos + nci]
        pos += nci
        own_out, c_out = refs[pos:pos + n_out], refs[pos + n_out:pos + n_out + nco]
        pos += n_out + nco
        own_scr, c_sem = refs[pos:pos + n_scr], refs[pos + n_scr:]

        @pl.when(first())
        def _():
            comm.start(c_in, c_out, c_sem)

        body(*own_in, *own_out, *own_scr)

        @pl.when(last())
        def _():
            comm.finish(c_in, c_out, c_sem)

    return _pc(wrapped, name=name, out_shape=tuple(out_shape) + tuple(comm.out_shapes), grid=grid,
               in_specs=list(in_specs) + [ANY] * nci, out_specs=tuple(out_specs) + tuple([ANY] * nco),
               scratch=list(scratch) + comm.sems, sem=sem, aliases=aliases)(*args, *comm.ins)


def _grid1_call(body, comm, n, *, name, out_shape, in_specs, out_specs, scratch, args, aliases=None):
    if comm is None:
        return _pc(body, name=name, out_shape=out_shape, grid=(n,), in_specs=in_specs, out_specs=out_specs,
                   scratch=scratch, sem=("arbitrary",), aliases=aliases)(*args), ()
    res = _hosted_call(body, comm, lambda: pl.program_id(0) == 0, lambda: pl.program_id(0) == n - 1,
                       name=name, out_shape=out_shape, grid=(n,), in_specs=in_specs,
                       out_specs=out_specs, scratch=scratch, sem=("arbitrary",), args=args, aliases=aliases)
    return res[:len(out_shape)], res[len(out_shape):]


def _run_comm(name, comm):
    nci, nco = len(comm.ins), len(comm.out_shapes)

    def body(*refs):
        c_in, c_out, c_sem = refs[:nci], refs[nci:nci + nco], refs[nci + nco:]
        comm.start(c_in, c_out, c_sem)
        comm.finish(c_in, c_out, c_sem)

    return _pc(body, name=name, out_shape=tuple(comm.out_shapes), in_specs=[ANY] * nci,
               out_specs=tuple([ANY] * nco), scratch=comm.sems)(*comm.ins)


def _mm_tn(name, a, b, tm, tn, stacked=False):
    t, m = a.shape
    _, n = b.shape
    if stacked:
        assert tm == m
        out_shape = jax.ShapeDtypeStruct((n // tn, m, tn), BF16)
        o_spec = pl.BlockSpec((None, tm, tn), lambda i, j: (j, 0, 0))
    else:
        out_shape = jax.ShapeDtypeStruct((m, n), BF16)
        o_spec = pl.BlockSpec((tm, tn), lambda i, j: (i, j))
    return _mm(name, a, b, TN, (m // tm, n // tn),
               pl.BlockSpec((t, tm), lambda i, j: (0, i)),
               pl.BlockSpec((t, tn), lambda i, j: (0, j)),
               out_shape, o_spec)


DH_SLOT = (2, 0, 1, 3)


def _dh_slot(j):
    return jnp.where(j == 3, 3, (j + 2) % 3)


def _in_proj_wgrad(x2b, dh, tm, tn):
    t, m = x2b.shape
    n = dh.shape[2]

    def body(a_ref, b_ref, o_ref):
        o_ref[...] = _dot(a_ref[...], b_ref[...], TN).astype(BF16)

    return _pc(body, name="in_proj_wgrad", out_shape=jax.ShapeDtypeStruct((N_CHIP, m, n), BF16),
               grid=(N_CHIP, m // tm, n // tn),
               in_specs=[pl.BlockSpec((t, tm), lambda j, i, k: (0, i)),
                         pl.BlockSpec((None, t, tn), lambda j, i, k: (_dh_slot(j), 0, k))],
               out_specs=pl.BlockSpec((None, tm, tn), lambda j, i, k: (j, i, k)),
               sem=("parallel", "parallel", "parallel"))(x2b, dh)


def _in_proj_xgrad(dh, win_st, dr1, tm, comm):
    t = dr1.shape[0]
    ni = t // tm

    def body(a_ref, b_ref, add_ref, o_ref, acc):
        j = pl.program_id(1)
        prod = _dot(a_ref[...], b_ref[...], NT)

        @pl.when(j == 0)
        def _():
            acc[...] = prod + ALPHA * add_ref[...]

        @pl.when((j > 0) & (j < N_CHIP - 1))
        def _():
            acc[...] += prod

        @pl.when(j == N_CHIP - 1)
        def _():
            o_ref[...] = acc[...] + prod

    tile = pl.BlockSpec((tm, D_MODEL), lambda i, j: (i, 0))
    res = _hosted_call(body, comm,
                       lambda: (pl.program_id(0) == 0) & (pl.program_id(1) == 0),
                       lambda: (pl.program_id(0) == ni - 1) & (pl.program_id(1) == N_CHIP - 1),
                       name="in_proj_xgrad", out_shape=(jax.ShapeDtypeStruct((t, D_MODEL), F32),),
                       grid=(ni, N_CHIP),
                       in_specs=[pl.BlockSpec((None, tm, 2 * D_MODEL), lambda i, j: (_dh_slot(j), i, 0)),
                                 pl.BlockSpec((None, D_MODEL, 2 * D_MODEL), lambda i, j: (j, 0, 0)),
                                 tile],
                       out_specs=(tile,), scratch=[pltpu.VMEM((tm, D_MODEL), F32)],
                       sem=("arbitrary", "arbitrary"), args=[dh, win_st, dr1])
    return res[0], res[1:]


def _sgu_mixed(v, wm_ref, bsb_ref, gv, bv):
    gl, dgl = _gelu_and_grad(v)
    vh, rstd = _ln_stats(gl)
    vn = vh * gv + bv
    mixed = []
    for g in range(N_GROUP):
        sl = slice(g * 128, (g + 1) * 128)
        mixed.append(_dot(wm_ref[g], vn[:, sl]) + bsb_ref[g])
    return dgl, vh, rstd, vn, mixed


def _sgu_fwd(h, wm, bsb, gv, bv, comm=None):
    t = h.shape[0]

    def body(u_ref, v_ref, wm_ref, bsb_ref, gv_ref, bv_ref, ya_ref):
        for bb in range(SGU_STEP_BLOCKS):
            rows = slice(bb * SGU_BLOCK, (bb + 1) * SGU_BLOCK)
            u = u_ref[rows, :].astype(F32)
            _, _, _, _, mixed = _sgu_mixed(v_ref[rows, :].astype(F32), wm_ref, bsb_ref, gv_ref[...],
                                           bv_ref[...])
            gu = _gelu(u)
            for g in range(N_GROUP):
                sl = slice(g * 128, (g + 1) * 128)
                ya_ref[rows, sl] = (gu[:, sl] * mixed[g]).astype(BF16)

    full3 = pl.BlockSpec((N_GROUP, 128, 128), lambda i: (0, 0, 0))
    vec = pl.BlockSpec((1, D_MODEL), lambda i: (0, 0))
    (ya,), extra = _grid1_call(
        body, comm, t // SGU_ROWS, name="sgu_fwd",
        out_shape=(jax.ShapeDtypeStruct((t, D_MODEL), BF16),),
        in_specs=[pl.BlockSpec((SGU_ROWS, D_MODEL), lambda i: (i, 0)),
                  pl.BlockSpec((SGU_ROWS, D_MODEL), lambda i: (i, 1)),
                  full3, full3, vec, vec],
        out_specs=(pl.BlockSpec((SGU_ROWS, D_MODEL), lambda i: (i, 0)),),
        scratch=[], args=(h, h, wm, bsb, gv, bv))
    return ya, extra


def _sgu_bwd(h, dya, wm, wmt, bsb, gv, bv, maskf, dh_buf, comm=None):
    t = h.shape[0]
    nb = t // SGU_ROWS

    def body(u_ref, v_ref, dya_ref, wm_ref, wmt_ref, bsb_ref, gv_ref, bv_ref, mask_ref, dh_buf_ref,
             dh_ref, dws_ref, dbs_ref, dgv_ref, dbv_ref, dmix_acc):
        i = pl.program_id(0)

        @pl.when(i == 0)
        def _():
            dws_ref[...] = jnp.zeros_like(dws_ref)
            dgv_ref[...] = jnp.zeros_like(dgv_ref)
            dbv_ref[...] = jnp.zeros_like(dbv_ref)
            dmix_acc[...] = jnp.zeros_like(dmix_acc)

        gvv = gv_ref[...]
        for bb in range(SGU_STEP_BLOCKS):
            rows = slice(bb * SGU_BLOCK, (bb + 1) * SGU_BLOCK)
            u = u_ref[rows, :].astype(F32)
            dgl_v, vh, rstd, vn, mixed = _sgu_mixed(v_ref[rows, :].astype(F32), wm_ref, bsb_ref, gvv,
                                                    bv_ref[...])
            gu, dgl_u = _gelu_and_grad(u)
            dya_v = dya_ref[rows, :].astype(F32)
            dvn_parts = []
            for g in range(N_GROUP):
                sl = slice(g * 128, (g + 1) * 128)
                d_y = dya_v[:, sl]
                dh_ref[rows, sl] = (d_y * mixed[g] * dgl_u[:, sl]).astype(BF16)
                d_mixed = d_y * gu[:, sl]
                dmix_acc[g] += d_mixed
                dws_ref[g] += _dot(d_mixed, vn[:, sl], NT) * mask_ref[...]
                dvn_parts.append(_dot(wmt_ref[g], d_mixed))
            dvn = jnp.concatenate(dvn_parts, axis=1)
            dgv_ref[...] += _colsum8(dvn * vh)
            dbv_ref[...] += _colsum8(dvn)
            d_gl = _ln_bwd(dvn * gvv, vh, rstd)
            dh_ref[rows, D_MODEL:] = (d_gl * dgl_v).astype(BF16)

        @pl.when(i == nb - 1)
        def _():
            rowid = lax.broadcasted_iota(jnp.int32, (8, 128), 0)
            ones = jnp.ones((8, 128), F32)
            acc = jnp.zeros((8, 128), F32)
            for g in range(N_GROUP):
                rs = _dot32(ones, dmix_acc[g], NT)
                acc = jnp.where(rowid == g, rs, acc)
            dbs_ref[...] = acc

    full3 = pl.BlockSpec((N_GROUP, 128, 128), lambda i: (0, 0, 0))
    vec = pl.BlockSpec((1, D_MODEL), lambda i: (0, 0))
    acc8 = pl.BlockSpec((8, D_MODEL), lambda i: (0, 0))
    return _grid1_call(
        body, comm, nb, name="sgu_bwd",
        out_shape=(jax.ShapeDtypeStruct(dh_buf.shape, BF16),
                   jax.ShapeDtypeStruct((N_GROUP, 128, 128), F32),
                   jax.ShapeDtypeStruct((8, 128), F32),
                   jax.ShapeDtypeStruct((8, D_MODEL), F32),
                   jax.ShapeDtypeStruct((8, D_MODEL), F32)),
        in_specs=[pl.BlockSpec((SGU_ROWS, D_MODEL), lambda i: (i, 0)),
                  pl.BlockSpec((SGU_ROWS, D_MODEL), lambda i: (i, 1)),
                  pl.BlockSpec((SGU_ROWS, D_MODEL), lambda i: (i, 0)),
                  full3, full3, full3, vec, vec,
                  pl.BlockSpec((128, 128), lambda i: (0, 0)), ANY],
        out_specs=(pl.BlockSpec((None, SGU_ROWS, 2 * D_MODEL), lambda i: (DH_SLOT[0], i, 0)),
                   full3, pl.BlockSpec((8, 128), lambda i: (0, 0)), acc8, acc8),
        scratch=[pltpu.VMEM((N_GROUP, 128, 128), F32)],
        args=(h, h, dya, wm, wmt, bsb, gv, bv, maskf, dh_buf), aliases={9: 0})


def _tri_masks():
    row = lax.broadcasted_iota(jnp.int32, (CHUNK, CHUNK), 0)
    col = lax.broadcasted_iota(jnp.int32, (CHUNK, CHUNK), 1)
    return col <= row, col >= row


def _heads(v):
    return [v[:, hd * HEAD_DIM:(hd + 1) * HEAD_DIM] for hd in range(N_HEAD)]


def _tri_cumsum(tri_bf, v):
    hi = v.astype(BF16)
    r = v - hi.astype(F32)
    mid = r.astype(BF16)
    lo = (r - mid.astype(F32)).astype(BF16)
    return _dot(tri_bf, hi) + _dot(tri_bf, mid) + _dot(tri_bf, lo)


def _hgrn_chunk(q, fp, ii, lb, st_heads, causal, with_o=True):
    sg = _sig(fp)
    f = lb + (1.0 - lb) * sg
    k = 1.0 - f
    c = _tri_cumsum(causal.astype(BF16), jnp.log(f))
    ec = jnp.exp(c)
    en = jnp.exp(-c)
    sq = _sig(q)
    qt = q * sq * ec
    kt = k * en
    ecl = jnp.exp(c[CHUNK - 1:CHUNK, :])
    kk = kt * ecl
    qtb, ktb, iib, kkb = qt.astype(BF16), kt.astype(BF16), ii.astype(BF16), kk.astype(BF16)
    attn, o = [], []
    for hd, (qh, kh, ih) in enumerate(zip(_heads(qtb), _heads(ktb), _heads(iib))):
        a = jnp.where(causal, _dot(qh, kh, NT), 0.0).astype(BF16)
        attn.append(a)
        if with_o:
            o.append(_dot(a, ih) + _dot(qh, st_heads[hd], NT))
    return dict(sg=sg, f=f, k=k, ec=ec, en=en, sq=sq, ecl=ecl, kk=kk, qtb=qtb, ktb=ktb, iib=iib,
                kkb=kkb, attn=attn, o=o)


def _rms_heads(o_heads):
    rinv = [lax.rsqrt(jnp.mean(o * o, axis=-1, keepdims=True) + RMS_EPS) for o in o_heads]
    return rinv, jnp.concatenate([o * r for o, r in zip(o_heads, rinv)], axis=1)


HG_CHUNKS = 8
HG_ROWS = HG_CHUNKS * CHUNK


def _hgrn_fwd(h, logits, gn, comm=None):
    t = h.shape[0]
    nb = t // HG_ROWS

    def body(q_ref, f_ref, i_ref, og_ref, lg_ref, gn_ref, yb_ref, o_ref, st_ref, state):
        @pl.when(pl.program_id(0) == 0)
        def _():
            state[...] = jnp.zeros_like(state)

        causal, _ = _tri_masks()
        lb = _sig(lg_ref[0:1, :] - lg_ref[1:2, :])
        gnv = gn_ref[...]
        st = [state[hd] for hd in range(N_HEAD)]
        for cc in range(HG_CHUNKS):
            rows = slice(cc * CHUNK, (cc + 1) * CHUNK)
            og = og_ref[rows, :].astype(F32)
            r = _hgrn_chunk(q_ref[rows, :].astype(F32), f_ref[rows, :].astype(F32),
                            i_ref[rows, :].astype(F32), lb, [s.astype(BF16) for s in st], causal)
            o_bf = jnp.concatenate(r["o"], axis=1).astype(BF16)
            o_ref[rows, :] = o_bf
            _, on = _rms_heads(_heads(o_bf.astype(F32)))
            yb_ref[rows, :] = (on * gnv * (og * _sig(og))).astype(BF16)
            for hd in range(N_HEAD):
                st_ref[cc, hd] = st[hd]
            st = [s * e + _dot(ih, kh, TN)
                  for s, e, ih, kh in zip(st, _heads(r["ecl"]), _heads(r["iib"]), _heads(r["kkb"]))]
        for hd in range(N_HEAD):
            state[hd] = st[hd]

    def col(k):
        return pl.BlockSpec((HG_ROWS, D_MODEL), lambda ci: (ci, k))

    return _grid1_call(body, comm, nb, name="hgrn_fwd",
                       out_shape=(jax.ShapeDtypeStruct((t, D_MODEL), BF16),
                                  jax.ShapeDtypeStruct((t, D_MODEL), BF16),
                                  jax.ShapeDtypeStruct((t // CHUNK, N_HEAD, HEAD_DIM, HEAD_DIM), F32)),
                       in_specs=[col(2), col(3), col(4), col(5),
                                 pl.BlockSpec((2, D_MODEL), lambda ci: (0, 0)),
                                 pl.BlockSpec((1, D_MODEL), lambda ci: (0, 0))],
                       out_specs=(pl.BlockSpec((HG_ROWS, D_MODEL), lambda ci: (ci, 0)),
                                  pl.BlockSpec((HG_ROWS, D_MODEL), lambda ci: (ci, 0)),
                                  pl.BlockSpec((HG_CHUNKS, N_HEAD, HEAD_DIM, HEAD_DIM),
                                               lambda ci: (ci, 0, 0, 0))),
                       scratch=[pltpu.VMEM((N_HEAD, HEAD_DIM, HEAD_DIM), F32)],
                       args=(h, h, h, h, logits, gn))


def _hgrn_chunk_bwd(q, fp, ii, og, o_saved, dy, gnv, lb, st, dsn, causal, anti):
    stb = [s.astype(BF16) for s in st]
    dsnb = [s.astype(BF16) for s in dsn]
    r = _hgrn_chunk(q, fp, ii, lb, stb, causal, with_o=False)
    rinv, on = _rms_heads(_heads(o_saved))
    so = _sig(og)
    sil = og * so
    d_og = dy * on * gnv * (so * (1.0 + og * (1.0 - so)))
    d_on = dy * gnv * sil
    d_ob = jnp.concatenate(
        [ri * (dn - oh * jnp.mean(dn * oh, axis=-1, keepdims=True))
         for ri, dn, oh in zip(rinv, _heads(d_on), _heads(on))], axis=1).astype(BF16)
    d_i, d_qt, d_kt, d_kk, d_st, st_dsn = [], [], [], [], [], []
    ecl = _heads(r["ecl"])
    for hd, (dh, qh, kh, ih, kkh) in enumerate(zip(_heads(d_ob), _heads(r["qtb"]), _heads(r["ktb"]),
                                                   _heads(r["iib"]), _heads(r["kkb"]))):
        d_attn = jnp.where(causal, _dot(dh, ih, NT), 0.0).astype(BF16)
        d_i.append(_dot(r["attn"][hd], dh, TN) + _dot(kkh, dsnb[hd], NT))
        d_qt.append(_dot(d_attn, kh) + _dot(dh, stb[hd]))
        d_kt.append(_dot(d_attn, qh, TN))
        d_kk.append(_dot(ih, dsnb[hd]))
        d_st.append(_dot(dh, qh, TN) + dsn[hd] * ecl[hd])
        st_dsn.append(jnp.sum(st[hd] * dsn[hd], axis=0, keepdims=True))
    d_qt = jnp.concatenate(d_qt, axis=1)
    d_kt = jnp.concatenate(d_kt, axis=1)
    d_kk = jnp.concatenate(d_kk, axis=1)
    kk = r["kk"]
    d_cl = r["ecl"] * jnp.concatenate(st_dsn, axis=1) + jnp.sum(kk * d_kk, axis=0, keepdims=True)
    d_k = (d_kk * r["ecl"] + d_kt) * r["en"]
    d_c = d_qt * r["qtb"].astype(F32) - d_kt * r["ktb"].astype(F32) - d_kk * kk
    rowid = lax.broadcasted_iota(jnp.int32, (CHUNK, D_MODEL), 0)
    d_c = d_c + jnp.where(rowid == CHUNK - 1, d_cl, 0.0)
    d_lf = _tri_cumsum(anti.astype(BF16), d_c)
    d_f = d_lf / r["f"] - d_k
    sg, sq = r["sg"], r["sq"]
    d_q = d_qt * r["ec"] * (sq * (1.0 + q * (1.0 - sq)))
    d_fp = d_f * (1.0 - lb) * sg * (1.0 - sg)
    return (d_q, d_fp, jnp.concatenate(d_i, axis=1), d_og, d_st,
            _colsum8(dy * on * sil), _colsum8(d_f * (1.0 - sg)))


def _hgrn_bwd(h, o_all, dyb, st_all, logits, gn, dh_buf, comm=None):
    t = h.shape[0]
    nb = t // HG_ROWS

    def body(q_ref, f_ref, i_ref, og_ref, o_ref, dyb_ref, st_ref, lg_ref, gn_ref, dh_buf_ref,
             dh_ref, dlb_ref, dgn_ref, dstate):
        @pl.when(pl.program_id(0) == 0)
        def _():
            dstate[...] = jnp.zeros_like(dstate)
            dlb_ref[...] = jnp.zeros_like(dlb_ref)
            dgn_ref[...] = jnp.zeros_like(dgn_ref)

        causal, anti = _tri_masks()
        lb = _sig(lg_ref[0:1, :] - lg_ref[1:2, :])
        gnv = gn_ref[...]
        dsn = [dstate[hd] for hd in range(N_HEAD)]
        dgn_acc = jnp.zeros((8, D_MODEL), F32)
        dlb_acc = jnp.zeros((8, D_MODEL), F32)
        for cc in reversed(range(HG_CHUNKS)):
            rows = slice(cc * CHUNK, (cc + 1) * CHUNK)
            d_q, d_fp, d_i, d_og, dsn, dgn_c, dlb_c = _hgrn_chunk_bwd(
                q_ref[rows, :].astype(F32), f_ref[rows, :].astype(F32), i_ref[rows, :].astype(F32),
                og_ref[rows, :].astype(F32), o_ref[rows, :].astype(F32), dyb_ref[rows, :].astype(F32), gnv, lb,
                [st_ref[cc, hd] for hd in range(N_HEAD)], dsn, causal, anti)
            dgn_acc = dgn_acc + dgn_c
            dlb_acc = dlb_acc + dlb_c
            dh_ref[0, rows, :D_MODEL] = d_q.astype(BF16)
            dh_ref[0, rows, D_MODEL:] = d_fp.astype(BF16)
            dh_ref[1, rows, :D_MODEL] = d_i.astype(BF16)
            dh_ref[1, rows, D_MODEL:] = d_og.astype(BF16)
        dgn_ref[...] += dgn_acc
        dlb_ref[...] += dlb_acc
        for hd in range(N_HEAD):
            dstate[hd] = dsn[hd]

    def col(k):
        return pl.BlockSpec((HG_ROWS, D_MODEL), lambda ci: (nb - 1 - ci, k))

    acc8 = pl.BlockSpec((8, D_MODEL), lambda ci: (0, 0))
    pair = pl.BlockSpec((2, HG_ROWS, 2 * D_MODEL), lambda ci: (0, nb - 1 - ci, 0))
    return _grid1_call(body, comm, nb, name="hgrn_bwd",
                       out_shape=(jax.ShapeDtypeStruct(dh_buf.shape, BF16),
                                  jax.ShapeDtypeStruct((8, D_MODEL), F32),
                                  jax.ShapeDtypeStruct((8, D_MODEL), F32)),
                       in_specs=[col(2), col(3), col(4), col(5), col(0),
                                 pl.BlockSpec((HG_ROWS, D_MODEL), lambda ci: (nb - 1 - ci, 0)),
                                 pl.BlockSpec((HG_CHUNKS, N_HEAD, HEAD_DIM, HEAD_DIM),
                                              lambda ci: (nb - 1 - ci, 0, 0, 0)),
                                 pl.BlockSpec((2, D_MODEL), lambda ci: (0, 0)),
                                 pl.BlockSpec((1, D_MODEL), lambda ci: (0, 0)), ANY],
                       out_specs=(pair, acc8, acc8),
                       scratch=[pltpu.VMEM((N_HEAD, HEAD_DIM, HEAD_DIM), F32)],
                       args=(h, h, h, h, o_all, dyb, st_all, logits, gn, dh_buf), aliases={9: 0})


def _mix_fwd(ya, yb, h, x, wb0, wb1, wo, g1, b1, tm, comm=None):
    t = x.shape[0]

    def body(ya_ref, yb_ref, ga_ref, gb_ref, x_ref, wb0_ref, wb1_ref, wo_ref, g1_ref, b1_ref,
             r1_ref, a_ref, b_ref, m_ref, x1_ref):
        a = _dot(ya_ref[...], wb0_ref[...])
        b = _dot(yb_ref[...], wb1_ref[...])
        m = _sig(ga_ref[...].astype(F32)) * a + _sig(gb_ref[...].astype(F32)) * b
        r1 = ALPHA * x_ref[...] + _dot(m, wo_ref[...])
        xh, _ = _ln_stats(r1)
        r1_ref[...] = r1
        a_ref[...] = a.astype(BF16)
        b_ref[...] = b.astype(BF16)
        m_ref[...] = m.astype(BF16)
        x1_ref[...] = (xh * g1_ref[...] + b1_ref[...]).astype(BF16)

    tile = pl.BlockSpec((tm, D_MODEL), lambda i: (i, 0))
    wsp = pl.BlockSpec((D_MODEL, D_MODEL), lambda i: (0, 0))
    vec = pl.BlockSpec((1, D_MODEL), lambda i: (0, 0))
    f32o = jax.ShapeDtypeStruct((t, D_MODEL), F32)
    bfo = jax.ShapeDtypeStruct((t, D_MODEL), BF16)
    return _grid1_call(body, comm, t // tm, name="mix_fwd", out_shape=(f32o, bfo, bfo, bfo, bfo),
                       in_specs=[tile, tile,
                                 pl.BlockSpec((tm, D_MODEL), lambda i: (i, 6)),
                                 pl.BlockSpec((tm, D_MODEL), lambda i: (i, 7)),
                                 tile, wsp, wsp, wsp, vec, vec],
                       out_specs=(tile, tile, tile, tile, tile),
                       scratch=[], args=(ya, yb, h, h, x, wb0, wb1, wo, g1, b1))


def _mix_bwd(dr1, h, a, b, wo, wb0, wb1, tm):
    t = dr1.shape[0]

    def body(dr1_ref, ga_ref, gb_ref, a_ref, b_ref, wo_ref, wb0_ref, wb1_ref,
             da_ref, db_ref, dh3_ref, dya_ref, dyb_ref):
        d_m = _dot(dr1_ref[...], wo_ref[...], NT)
        sa = _sig(ga_ref[...].astype(F32))
        sb = _sig(gb_ref[...].astype(F32))
        d_a = (d_m * sa).astype(BF16)
        d_b = (d_m * sb).astype(BF16)
        da_ref[...] = d_a
        db_ref[...] = d_b
        dh3_ref[:, :D_MODEL] = (d_m * a_ref[...].astype(F32) * sa * (1.0 - sa)).astype(BF16)
        dh3_ref[:, D_MODEL:] = (d_m * b_ref[...].astype(F32) * sb * (1.0 - sb)).astype(BF16)
        dya_ref[...] = _dot(d_a, wb0_ref[...], NT).astype(BF16)
        dyb_ref[...] = _dot(d_b, wb1_ref[...], NT).astype(BF16)

    tile = pl.BlockSpec((tm, D_MODEL), lambda i: (i, 0))
    wsp = pl.BlockSpec((D_MODEL, D_MODEL), lambda i: (0, 0))
    f32o = jax.ShapeDtypeStruct((t, D_MODEL), F32)
    bfo = jax.ShapeDtypeStruct((t, D_MODEL), BF16)
    return _pc(body, name="mix_bwd",
               out_shape=(bfo, bfo, jax.ShapeDtypeStruct((N_CHIP, t, 2 * D_MODEL), BF16), bfo, bfo),
               grid=(t // tm,),
               in_specs=[tile,
                         pl.BlockSpec((tm, D_MODEL), lambda i: (i, 6)),
                         pl.BlockSpec((tm, D_MODEL), lambda i: (i, 7)),
                         tile, tile, wsp, wsp, wsp],
               out_specs=(tile, tile, pl.BlockSpec((None, tm, 2 * D_MODEL), lambda i: (DH_SLOT[3], i, 0)),
                          tile, tile),
               sem=("parallel",))(dr1, h, h, a, b, wo, wb0, wb1)


FF_TILE = 1408
FF_NJ = D_FF // FF_TILE


def _shift_down(v, k):
    return pltpu.roll(v, k, 0)


def _shift_up(v, k):
    return pltpu.roll(v, v.shape[0] - k, 0)


HALO = 16
FF_PIECES = ((0, 768), (768, FF_TILE))


def _ffn_up_act(x1b, wup_st, convw, convb, tm, comm):
    t = x1b.shape[0]
    ni = t // tm
    nth = tm // HALO

    def body(x_ref, xp_ref, wg_ref, wv_ref, cw_ref, cb_ref, h2_ref, act_ref):
        wg = wg_ref[...]
        gate = _dot(x_ref[...], wg).astype(BF16)
        val = _dot(x_ref[...], wv_ref[...]).astype(BF16)
        prev = (_dot(xp_ref[...], wg) * (pl.program_id(0) > 0).astype(F32)).astype(BF16)
        h2_ref[0] = gate
        h2_ref[1] = val
        ext = jnp.concatenate([prev.astype(F32), gate.astype(F32)], axis=0)
        gc = (cw_ref[0:1, :] * _shift_down(ext, 2) + cw_ref[1:2, :] * _shift_down(ext, 1)
              + cw_ref[2:3, :] * ext + cb_ref[...])[HALO:, :].astype(BF16)
        h2_ref[2] = gc
        act_ref[...] = (_gelu(gc.astype(F32)) * val.astype(F32)).astype(BF16)

    res = _hosted_call(
        body, comm,
        lambda: (pl.program_id(0) == 0) & (pl.program_id(1) == 0),
        lambda: (pl.program_id(0) == ni - 1) & (pl.program_id(1) == FF_NJ - 1),
        name="ffn_up",
        out_shape=(jax.ShapeDtypeStruct((3, t, D_FF), BF16), jax.ShapeDtypeStruct((t, D_FF), BF16)),
        grid=(ni, FF_NJ),
        in_specs=[pl.BlockSpec((tm, D_MODEL), lambda i, j: (i, 0)),
                  pl.BlockSpec((HALO, D_MODEL), lambda i, j: (jnp.maximum(i * nth - 1, 0), 0)),
                  pl.BlockSpec((None, D_MODEL, FF_TILE), lambda i, j: (j, 0, 0)),
                  pl.BlockSpec((None, D_MODEL, FF_TILE), lambda i, j: (j + FF_NJ, 0, 0)),
                  pl.BlockSpec((3, FF_TILE), lambda i, j: (0, j)),
                  pl.BlockSpec((1, FF_TILE), lambda i, j: (0, j))],
        out_specs=(pl.BlockSpec((3, tm, FF_TILE), lambda i, j: (0, i, j)),
                   pl.BlockSpec((tm, FF_TILE), lambda i, j: (i, j))),
        scratch=[], sem=("arbitrary", "arbitrary"),
        args=(x1b, x1b, wup_st, wup_st, convw, convb))
    return res[0], res[1], res[2:]


def _out_fwd_bwd(act, x1b, r1, p2, tgt, wd, wpg, wpp, g1, b1, g2, b2, tm):
    t = r1.shape[0]

    def body(act_ref, x1b_ref, r1_ref, p_ref, tgt_ref, wd_ref, wpg_ref, wpp_ref,
             g1_ref, b1_ref, g2_ref, b2_ref,
             dr2_ref, dpg_ref, dpp_ref, loss_ref, dg2_ref, db2_ref):
        i = pl.program_id(0)

        @pl.when(i == 0)
        def _():
            loss_ref[...] = jnp.zeros_like(loss_ref)
            dg2_ref[...] = jnp.zeros_like(dg2_ref)
            db2_ref[...] = jnp.zeros_like(db2_ref)

        ffn = _dot(act_ref[...], wd_ref[...])
        pg = _dot(x1b_ref[...], wpg_ref[...])
        pp = _dot(p_ref[...], wpp_ref[...])
        s = _sig(pg)
        xh1, _ = _ln_stats(r1_ref[...])
        x1 = xh1 * g1_ref[...] + b1_ref[...]
        r2 = ALPHA * x1 + ffn + s * pp
        xh2, rstd2 = _ln_stats(r2)
        g2v = g2_ref[...]
        diff = xh2 * g2v + b2_ref[...] - tgt_ref[...]
        part = jnp.sum(jnp.sum(diff * diff, axis=1, keepdims=True), axis=0, keepdims=True)
        loss_ref[...] += jnp.broadcast_to(part * (0.5 / D_MODEL), loss_ref.shape)
        dy = diff * (1.0 / D_MODEL)
        dg2_ref[...] += _colsum8(dy * xh2)
        db2_ref[...] += _colsum8(dy)
        dr2 = _ln_bwd(dy * g2v, xh2, rstd2)
        dr2_ref[...] = dr2
        dpg_ref[...] = (dr2 * pp * s * (1.0 - s)).astype(BF16)
        dpp_ref[...] = (dr2 * s).astype(BF16)

    tile = pl.BlockSpec((tm, D_MODEL), lambda i: (i, 0))
    vec = pl.BlockSpec((1, D_MODEL), lambda i: (0, 0))
    acc8 = pl.BlockSpec((8, D_MODEL), lambda i: (0, 0))
    acc_shape = jax.ShapeDtypeStruct((8, D_MODEL), F32)
    return _pc(body, name="out_fwd_bwd",
               out_shape=(jax.ShapeDtypeStruct((t, D_MODEL), F32),
                          jax.ShapeDtypeStruct((t, D_MODEL), BF16),
                          jax.ShapeDtypeStruct((t, D_MODEL), BF16),
                          acc_shape, acc_shape, acc_shape),
               grid=(t // tm,),
               in_specs=[pl.BlockSpec((tm, D_FF), lambda i: (i, 0)), tile, tile,
                         pl.BlockSpec((tm, PLE_DIM), lambda i: (i, 0)), tile,
                         pl.BlockSpec((D_FF, D_MODEL), lambda i: (0, 0)),
                         pl.BlockSpec((D_MODEL, D_MODEL), lambda i: (0, 0)),
                         pl.BlockSpec((PLE_DIM, D_MODEL), lambda i: (0, 0)),
                         vec, vec, vec, vec],
               out_specs=(tile, tile, tile, acc8, acc8, acc8),
               sem=("arbitrary",))(act, x1b, r1, p2, tgt, wd, wpg, wpp, g1, b1, g2, b2)


def _ffn_bwd(h2, dr2, wd, wup_st, dpg, wpg, r1, g1, convw, tm):
    t = r1.shape[0]
    ni = t // tm
    nth = tm // HALO
    last_halo = t // HALO - 1
    main_rows = slice(0, tm)

    def body(g_ref, gc_ref, gcn_ref, v_ref, vn_ref, dr2_ref, dr2n_ref, wd_ref, wug_ref, wuv_ref,
             cw_ref, dpg_ref, wpg_ref, r1_ref, g1_ref,
             dh2_ref, dr1_ref, dcw_ref, dcb_ref, dg1_ref, db1_ref, acc):
        i = pl.program_id(0)
        j = pl.program_id(1)

        @pl.when((i == 0) & (j == 0))
        def _():
            dcw_ref[...] = jnp.zeros_like(dcw_ref)
            dcb_ref[...] = jnp.zeros_like(dcb_ref)
            dg1_ref[...] = jnp.zeros_like(dg1_ref)
            db1_ref[...] = jnp.zeros_like(db1_ref)

        dr2v = dr2_ref[...].astype(BF16)
        dr2n = dr2n_ref[...].astype(BF16)
        more = (i < ni - 1).astype(F32)
        prod = None
        dcw_parts, dcb_parts = [], []
        for c0, c1 in FF_PIECES:
            pc = slice(c0, c1)
            da = _dot(dr2v, wd_ref[pc, :], NT)
            dnext = _dot(dr2n, wd_ref[pc, :], NT) * more
            gc = jnp.concatenate([gc_ref[:, pc].astype(F32), gcn_ref[:, pc].astype(F32)], axis=0)
            vext = jnp.concatenate([v_ref[:, pc].astype(F32), vn_ref[:, pc].astype(F32)], axis=0)
            dext = jnp.concatenate([da, dnext], axis=0)
            gl, dgl = _gelu_and_grad(gc)
            d_gc = dext * vext * dgl
            up1 = _shift_up(d_gc, 1)[main_rows, :]
            up2 = _shift_up(d_gc, 2)[main_rows, :]
            dm = d_gc[main_rows, :]
            d_gate = (cw_ref[2:3, pc] * dm + cw_ref[1:2, pc] * up1 + cw_ref[0:1, pc] * up2).astype(BF16)
            d_val = (da * gl[main_rows, :]).astype(BF16)
            dh2_ref[0, :, pc] = d_gate
            dh2_ref[1, :, pc] = d_val
            g = g_ref[:, pc].astype(F32)
            s0 = jnp.sum(g * up2, axis=0, keepdims=True)
            s1 = jnp.sum(g * up1, axis=0, keepdims=True)
            s2 = jnp.sum(g * dm, axis=0, keepdims=True)
            rowid = lax.broadcasted_iota(jnp.int32, (8, c1 - c0), 0)
            dcw_parts.append(jnp.where(rowid == 0, s0, jnp.where(rowid == 1, s1,
                                                                 jnp.where(rowid == 2, s2, 0.0))))
            dcb_parts.append(_colsum8(dm))
            part = _dot(d_gate, wug_ref[:, pc], NT) + _dot(d_val, wuv_ref[:, pc], NT)
            prod = part if prod is None else prod + part
        dcw_part = jnp.concatenate(dcw_parts, axis=1)
        dcb_part = jnp.concatenate(dcb_parts, axis=1)
        for jj in range(FF_NJ):
            @pl.when(j == jj)
            def _(jj=jj):
                cols = slice(jj * FF_TILE, (jj + 1) * FF_TILE)
                dcw_ref[:, cols] += dcw_part
                dcb_ref[:, cols] += dcb_part

        @pl.when(j == 0)
        def _():
            acc[...] = prod

        @pl.when(j > 0)
        def _():
            acc[...] += prod

        @pl.when(j == FF_NJ - 1)
        def _():
            d_x1 = acc[...] + _dot(dpg_ref[...], wpg_ref[...], NT) + ALPHA * dr2_ref[...]
            xh, rstd = _ln_stats(r1_ref[...])
            dg1_ref[...] += _colsum8(d_x1 * xh)
            db1_ref[...] += _colsum8(d_x1)
            dr1_ref[...] = _ln_bwd(d_x1 * g1_ref[...], xh, rstd)

    def h2_main(part):
        return pl.BlockSpec((None, tm, FF_TILE), lambda i, j: (part, i, j))

    def h2_next(part):
        return pl.BlockSpec((None, HALO, FF_TILE),
                            lambda i, j: (part, jnp.minimum((i + 1) * nth, last_halo), j))

    tile = pl.BlockSpec((tm, D_MODEL), lambda i, j: (i, 0))
    acc8 = pl.BlockSpec((8, D_MODEL), lambda i, j: (0, 0))
    accff = pl.BlockSpec((8, D_FF), lambda i, j: (0, 0))
    acc_shape = jax.ShapeDtypeStruct((8, D_MODEL), F32)
    accff_shape = jax.ShapeDtypeStruct((8, D_FF), F32)
    return _pc(body, name="ffn_bwd",
               out_shape=(jax.ShapeDtypeStruct((2, t, D_FF), BF16),
                          jax.ShapeDtypeStruct((t, D_MODEL), F32),
                          accff_shape, accff_shape, acc_shape, acc_shape),
               grid=(ni, FF_NJ),
               in_specs=[h2_main(0), h2_main(2), h2_next(2), h2_main(1), h2_next(1),
                         tile,
                         pl.BlockSpec((HALO, D_MODEL), lambda i, j: (jnp.minimum((i + 1) * nth, last_halo), 0)),
                         pl.BlockSpec((FF_TILE, D_MODEL), lambda i, j: (j, 0)),
                         pl.BlockSpec((None, D_MODEL, FF_TILE), lambda i, j: (j, 0, 0)),
                         pl.BlockSpec((None, D_MODEL, FF_TILE), lambda i, j: (j + FF_NJ, 0, 0)),
                         pl.BlockSpec((3, FF_TILE), lambda i, j: (0, j)),
                         tile, pl.BlockSpec((D_MODEL, D_MODEL), lambda i, j: (0, 0)),
                         tile, pl.BlockSpec((1, D_MODEL), lambda i, j: (0, 0))],
               out_specs=(pl.BlockSpec((2, tm, FF_TILE), lambda i, j: (0, i, j)),
                          tile, accff, accff, acc8, acc8),
               scratch=[pltpu.VMEM((tm, D_MODEL), F32)],
               sem=("arbitrary", "arbitrary"))(h2, h2, h2, h2, h2, dr2, dr2, wd, wup_st, wup_st,
                                               convw, dpg, wpg, r1, g1)


ANY = pl.BlockSpec(memory_space=pl.ANY)


def _chip_peers():
    x, y, c = lax.axis_index("x"), lax.axis_index("y"), lax.axis_index("c")
    return x, y, c, [(1 - x, y), (x, 1 - y), (1 - x, 1 - y)]


def _gather_comm(halved, whole=()):
    n, nw = len(halved), len(whole)

    def copies(ins, outs, sems):
        ici_send, ici_recv, d2d_send, d2d_recv, own_send, own_recv = sems
        x, y, c, peers = _chip_peers()
        me = 2 * x + y
        sibling = (x, y, 1 - c)
        own, ici, ici_wait, fwd, fwd_wait = [], [], [], [], []
        for ti in range(n + nw):
            src, dst = ins[ti], outs[ti]
            own.append(pltpu.make_async_remote_copy(
                src_ref=src, dst_ref=dst.at[me], send_sem=own_send.at[ti], recv_sem=own_recv.at[ti],
                device_id=sibling, device_id_type=MESH))
            for k, (px, py) in enumerate(peers):
                pk = 2 * px + py
                sem = dict(send_sem=ici_send.at[ti * 3 + k], recv_sem=ici_recv.at[ti * 3 + k],
                           device_id=(px, py, c), device_id_type=MESH)
                if ti < n:
                    ici.append(pltpu.make_async_remote_copy(src_ref=src.at[c], dst_ref=dst.at[me, c], **sem))
                    ici_wait.append(pltpu.make_async_remote_copy(src_ref=src.at[c], dst_ref=dst.at[pk, c], **sem))
                    dsem = dict(send_sem=d2d_send.at[ti * 3 + k], recv_sem=d2d_recv.at[ti * 3 + k],
                                device_id=sibling, device_id_type=MESH)
                    fwd.append(pltpu.make_async_remote_copy(src_ref=dst.at[pk, c], dst_ref=dst.at[pk, c], **dsem))
                    fwd_wait.append(pltpu.make_async_remote_copy(
                        src_ref=dst.at[pk, 1 - c], dst_ref=dst.at[pk, 1 - c], **dsem))
                else:
                    ici.append(pltpu.make_async_remote_copy(src_ref=src, dst_ref=dst.at[me], **sem))
                    ici_wait.append(pltpu.make_async_remote_copy(src_ref=src, dst_ref=dst.at[pk], **sem))
        return own, ici, ici_wait, fwd, fwd_wait

    def start(ins, outs, sems):
        own, ici, _, _, _ = copies(ins, outs, sems)
        for cp in own + ici:
            cp.start()

    def finish(ins, outs, sems):
        own, ici, ici_wait, fwd, fwd_wait = copies(ins, outs, sems)
        for i, cp in enumerate(ici_wait):
            cp.wait_recv()
            if i < len(fwd):
                fwd[i].start()
        for cp in fwd_wait + own:
            cp.wait_recv()
        for cp in own + ici + fwd:
            cp.wait_send()

    srcs = list(halved) + list(whole)
    return _Comm(srcs, [jax.ShapeDtypeStruct((N_CHIP,) + s.shape, s.dtype) for s in srcs],
                 [pltpu.SemaphoreType.DMA((3 * (n + nw),)), pltpu.SemaphoreType.DMA((3 * (n + nw),)),
                  pltpu.SemaphoreType.DMA((max(3 * n, 1),)), pltpu.SemaphoreType.DMA((max(3 * n, 1),)),
                  pltpu.SemaphoreType.DMA((n + nw,)), pltpu.SemaphoreType.DMA((n + nw,))],
                 start, finish)


def _sibling_exchange_comm(grads):
    n = len(grads)

    def copies(ins, outs, sems):
        send_sems, recv_sems = sems
        x, y, c = lax.axis_index("x"), lax.axis_index("y"), lax.axis_index("c")
        res = []
        for ti in range(n):
            half = ins[ti].shape[1] // 2
            res.append(pltpu.make_async_remote_copy(
                src_ref=ins[ti].at[:, pl.ds(pl.multiple_of((1 - c) * half, 16), half), :],
                dst_ref=outs[ti],
                send_sem=send_sems.at[ti], recv_sem=recv_sems.at[ti],
                device_id=(x, y, 1 - c), device_id_type=MESH))
        return res

    def start(ins, outs, sems):
        for cp in copies(ins, outs, sems):
            cp.start()

    def finish(ins, outs, sems):
        for cp in copies(ins, outs, sems):
            cp.wait()

    return _Comm(grads, [jax.ShapeDtypeStruct((N_CHIP, g.shape[1] // 2, g.shape[2]), g.dtype) for g in grads],
                 [pltpu.SemaphoreType.DMA((n,)), pltpu.SemaphoreType.DMA((n,))], start, finish)


def _in_proj_gathering(x2b, own, chip, tm, comm):
    t = x2b.shape[0]
    ni = t // tm
    half, cols = own.shape[1], own.shape[2]
    nci, nco = len(comm.ins), len(comm.out_shapes)

    def body(chip_ref, x_ref, own_ref, own_hbm, *rest):
        c_in = rest[:nci]
        h_ref, win_out = rest[nci:nci + 2]
        c_out = rest[nci + 2:nci + 2 + nco]
        w_scr, ici_send, ici_recv, d2d_send, d2d_recv, own_sems, ld_sems = rest[nci + 2 + nco:nci + 9 + nco]
        c_sem = rest[nci + 9 + nco:]
        s, i = pl.program_id(0), pl.program_id(1)
        x, y, c, peers = _chip_peers()
        me = 2 * x + y
        sibling = (x, y, 1 - c)

        def ici(k, slot):
            px, py = peers[k]
            return pltpu.make_async_remote_copy(
                src_ref=own_hbm.at[c], dst_ref=win_out.at[slot, c],
                send_sem=ici_send.at[k], recv_sem=ici_recv.at[k],
                device_id=(px, py, c), device_id_type=MESH)

        def forward(k, core):
            pk = 2 * peers[k][0] + peers[k][1]
            return pltpu.make_async_remote_copy(
                src_ref=win_out.at[pk, core], dst_ref=win_out.at[pk, core],
                send_sem=d2d_send.at[k], recv_sem=d2d_recv.at[k],
                device_id=sibling, device_id_type=MESH)

        place_own = pltpu.make_async_remote_copy(
            src_ref=own_hbm, dst_ref=win_out.at[me], send_sem=own_sems.at[0], recv_sem=own_sems.at[1],
            device_id=sibling, device_id_type=MESH)

        @pl.when((s == 0) & (i == 0))
        def _():
            for k in range(2):
                ici(k, me).start()
            place_own.start()

        @pl.when(s == 0)
        def _():
            xv = x_ref[...]
            h_ref[...] = (_dot(xv[:, :half], own_ref[0]) + _dot(xv[:, half:], own_ref[1])).astype(BF16)

        for k in range(3):
            @pl.when((s == k + 1) & (i == 0))
            def _(k=k):
                pk = 2 * peers[k][0] + peers[k][1]
                ici(k, pk).wait_recv()
                if k == 0:
                    ici(2, me).start()
                forward(k, c).start()
                forward(k, 1 - c).wait_recv()
                loads = [pltpu.make_async_copy(win_out.at[pk, hh], w_scr.at[hh], ld_sems.at[hh])
                         for hh in range(2)]
                for ld in loads:
                    ld.start()
                for ld in loads:
                    ld.wait()
                if k == 1:
                    comm.start(c_in, c_out, c_sem)

        @pl.when(s > 0)
        def _():
            xv = x_ref[...]
            h_ref[...] = (_dot(xv[:, :half], w_scr[0]) + _dot(xv[:, half:], w_scr[1])).astype(BF16)

        @pl.when((s == N_CHIP - 1) & (i == ni - 1))
        def _():
            place_own.wait()
            for k in range(3):
                ici(k, me).wait_send()
                forward(k, c).wait_send()
            comm.finish(c_in, c_out, c_sem)

    def shard_col(s, me):
        return jnp.where(s == 0, me, me ^ jnp.where(s == 1, 2, jnp.where(s == 2, 1, 3)))

    res = _pc(body, name="in_proj",
              out_shape=(jax.ShapeDtypeStruct((t, N_CHIP * cols), BF16),
                         jax.ShapeDtypeStruct((N_CHIP,) + own.shape, own.dtype)) + tuple(comm.out_shapes),
              grid=(N_CHIP, ni), nsp=1,
              in_specs=[pl.BlockSpec((tm, 2 * half), lambda s, i, chip_ref: (i, 0)),
                        pl.BlockSpec(own.shape, lambda s, i, chip_ref: (0, 0, 0)),
                        ANY] + [ANY] * nci,
              out_specs=(pl.BlockSpec((tm, cols), lambda s, i, chip_ref: (i, shard_col(s, chip_ref[0]))),
                         ANY) + tuple([ANY] * nco),
              scratch=[pltpu.VMEM(own.shape, own.dtype),
                       pltpu.SemaphoreType.DMA((3,)), pltpu.SemaphoreType.DMA((3,)),
                       pltpu.SemaphoreType.DMA((3,)), pltpu.SemaphoreType.DMA((3,)),
                       pltpu.SemaphoreType.DMA((2,)), pltpu.SemaphoreType.DMA((2,))] + comm.sems,
              sem=("arbitrary", "arbitrary"))(chip, x2b, own, own, *comm.ins)
    return res[0], res[1], res[2:]


def _rs_add_halves(name, grad, recv, core):
    _, r, cdim = grad.shape
    half = r // 2
    tr = _row_tile(half, cdim, mult=16)
    nr = half // tr

    def body(c_ref, g_ref, r_ref, o_ref):
        o_ref[...] = (g_ref[...].astype(F32) + r_ref[...].astype(F32)).astype(BF16)

    return _pc(body, name=name, out_shape=jax.ShapeDtypeStruct((N_CHIP, half, cdim), BF16),
               grid=(N_CHIP, nr), nsp=1,
               in_specs=[pl.BlockSpec((None, tr, cdim), lambda j, i, c_ref: (j, c_ref[0] * nr + i, 0)),
                         pl.BlockSpec((None, tr, cdim), lambda j, i, c_ref: (j, i, 0))],
               out_specs=pl.BlockSpec((None, tr, cdim), lambda j, i, c_ref: (j, i, 0)),
               sem=("parallel", "parallel"))(core, grad, recv)


def _chip_exchange_comm(parts):
    n = len(parts)

    def copies(ins, outs, sems):
        send_sems, recv_sems = sems
        x, y, c, peers = _chip_peers()
        return [pltpu.make_async_remote_copy(
            src_ref=ins[ti].at[2 * px + py], dst_ref=outs[ti].at[k],
            send_sem=send_sems.at[ti * 3 + k], recv_sem=recv_sems.at[ti * 3 + k],
            device_id=(px, py, c), device_id_type=MESH)
            for ti in range(n) for k, (px, py) in enumerate(peers)]

    def start(ins, outs, sems):
        for cp in copies(ins, outs, sems):
            cp.start()

    def finish(ins, outs, sems):
        for cp in copies(ins, outs, sems):
            cp.wait()

    return _Comm(parts, [jax.ShapeDtypeStruct((3,) + p.shape[1:], p.dtype) for p in parts],
                 [pltpu.SemaphoreType.DMA((3 * n,)), pltpu.SemaphoreType.DMA((3 * n,))], start, finish)


def _rs_sum_chips(name, part, recv, chip):
    _, half, cdim = recv.shape
    tr = _row_tile(half, cdim, mult=16)

    def body(chip_ref, p_ref, r_ref, o_ref):
        o_ref[...] = ((p_ref[...].astype(F32) + r_ref[0].astype(F32)) + r_ref[1].astype(F32)
                      ) + r_ref[2].astype(F32)

    return _pc(body, name=name, out_shape=jax.ShapeDtypeStruct((half, cdim), F32),
               grid=(half // tr,), nsp=1,
               in_specs=[pl.BlockSpec((None, tr, cdim), lambda i, chip_ref: (chip_ref[0], i, 0)),
                         pl.BlockSpec((3, tr, cdim), lambda i, chip_ref: (0, i, 0))],
               out_specs=pl.BlockSpec((tr, cdim), lambda i, chip_ref: (i, 0)),
               sem=("parallel",))(chip, part, recv)


def _rs_send_halves(halves):
    n = len(halves)

    def body(*refs):
        ins, outs = refs[:n], refs[n:2 * n]
        send_sems, recv_sems = refs[2 * n:]
        x, y, c = lax.axis_index("x"), lax.axis_index("y"), lax.axis_index("c")
        sends = []
        for ti in range(n):
            cp = pltpu.make_async_remote_copy(
                src_ref=ins[ti], dst_ref=outs[ti],
                send_sem=send_sems.at[ti], recv_sem=recv_sems.at[ti],
                device_id=(x, y, 1 - c), device_id_type=MESH)
            cp.start()
            sends.append(cp)
        for cp in sends:
            cp.wait()

    return _pc(body, name="rs_send_halves",
               out_shape=tuple(jax.ShapeDtypeStruct(hv.shape, hv.dtype) for hv in halves),
               in_specs=[ANY] * n, out_specs=tuple([ANY] * n),
               scratch=[pltpu.SemaphoreType.DMA((n,)), pltpu.SemaphoreType.DMA((n,))])(*halves)


def _adamw_rows(name, mine, theirs, w, m, v, core):
    half, cdim = mine.shape
    tr = _row_tile(half, cdim, budget=1 << 19)
    nrh = half // tr

    def body(c_ref, mine_ref, theirs_ref, w_ref, m_ref, v_ref, g_ref, d_ref, m2_ref, v2_ref):
        is_mine = (pl.program_id(0) // nrh) == c_ref[0]
        g = jnp.where(is_mine, mine_ref[...], theirs_ref[...])
        d, m2, v2 = _adamw(w_ref[...], g, m_ref[...], v_ref[...])
        g_ref[...] = g
        d_ref[...] = d
        m2_ref[...] = m2
        v2_ref[...] = v2

    htile = pl.BlockSpec((tr, cdim), lambda i, c_ref: (i % nrh, 0))
    tile = pl.BlockSpec((tr, cdim), lambda i, c_ref: (i, 0))
    shp = jax.ShapeDtypeStruct((2 * half, cdim), F32)
    return _pc(body, name=name, out_shape=(shp, shp, shp, shp), grid=(2 * nrh,), nsp=1,
               in_specs=[htile, htile, tile, tile, tile], out_specs=(tile, tile, tile, tile),
               sem=("parallel",))(core, mine, theirs, w, m, v)


def _adamw_whole(name, g, w, m, v):
    def body(g_ref, w_ref, m_ref, v_ref, d_ref, m2_ref, v2_ref):
        d, m2, v2 = _adamw(w_ref[...], g_ref[...], m_ref[...], v_ref[...])
        d_ref[...] = d
        m2_ref[...] = m2
        v2_ref[...] = v2

    shp = jax.ShapeDtypeStruct(g.shape, F32)
    return _pc(body, name=name, out_shape=(shp, shp, shp))(g, w, m, v)


SMALL_LAYOUT = (
    ("sgu_w_s", 1024, 1, 0),
    ("sgu_b_s", 8, 1, 1024),
    ("sgu_norm_g", 1, 0, 0),
    ("sgu_norm_b", 1, 0, 1),
    ("hgrn_norm_g", 1, 0, 3),
    ("ln1_g", 1, 0, 4),
    ("ln1_b", 1, 0, 5),
    ("ffn_conv_b", 1, 2, 3),
    ("ln2_g", 1, 0, 6),
    ("ln2_b", 1, 0, 7),
)
LB_ROW = 2
LOSS_ROW = 8
PACK_SHAPES = ((16, D_MODEL), (N_GROUP * 128 + 8, 128), (8, D_FF))


def _small_allreduce_adamw(rows1024, dws, dbs, dcw, dcb, logits, m_logits, v_logits,
                           small_w, small_m, small_v):
    ns = len(SMALL_LAYOUT)
    nr = len(rows1024)
    nb = len(PACK_SHAPES)

    def body(*refs):
        row_refs = refs[:nr]
        dws_ref, dbs_ref, dcw_ref, dcb_ref, lg_ref, mlg_ref, vlg_ref = refs[nr:nr + 7]
        pos = nr + 7
        w_refs = refs[pos:pos + ns]
        m_refs = refs[pos + ns:pos + 2 * ns]
        v_refs = refs[pos + 2 * ns:pos + 3 * ns]
        pos += 3 * ns
        loss_ref, dcw_out = refs[pos:pos + 2]
        lg_outs = refs[pos + 2:pos + 6]
        pos += 6
        outs = refs[pos:pos + 4 * ns]
        pos += 4 * ns
        pack = refs[pos:pos + nb]
        sib = refs[pos + nb:pos + 2 * nb]
        gath = refs[pos + 2 * nb:pos + 3 * nb]
        d2d_send, d2d_recv, ici_send, ici_recv = refs[pos + 3 * nb:]

        x, y, c, peers = _chip_peers()
        me = 2 * x + y
        sibling = (x, y, 1 - c)

        pack[0][...] = jnp.zeros(PACK_SHAPES[0], F32)
        for k in range(nr):
            pack[0][k:k + 1, :] = row_refs[k][0:1, :]
        pack[1][0:N_GROUP * 128, :] = dws_ref[...]
        pack[1][N_GROUP * 128:, :] = dbs_ref[...]
        pack[2][...] = jnp.zeros(PACK_SHAPES[2], F32)
        pack[2][0:3, :] = dcw_ref[0:3, :]
        pack[2][3:4, :] = dcb_ref[0:1, :]

        d2d = [pltpu.make_async_remote_copy(
            src_ref=pack[b], dst_ref=sib[b], send_sem=d2d_send.at[b], recv_sem=d2d_recv.at[b],
            device_id=sibling, device_id_type=MESH) for b in range(nb)]
        for cp in d2d:
            cp.start()
        for cp in d2d:
            cp.wait()
        for b in range(nb):
            gath[b][me] = pack[b][...] + sib[b][...]

        ici, ici_wait = [], []
        for b in range(nb):
            for k, (px, py) in enumerate(peers):
                sem = dict(send_sem=ici_send.at[b * 3 + k], recv_sem=ici_recv.at[b * 3 + k],
                           device_id=(px, py, c), device_id_type=MESH)
                ici.append(pltpu.make_async_remote_copy(src_ref=gath[b].at[me], dst_ref=gath[b].at[me], **sem))
                ici_wait.append(pltpu.make_async_remote_copy(
                    src_ref=gath[b].at[me], dst_ref=gath[b].at[2 * px + py], **sem))
        for cp in ici:
            cp.start()
        for cp in ici_wait:
            cp.wait_recv()
        for cp in ici:
            cp.wait_send()

        tot = pack
        for b in range(nb):
            tot[b][...] = ((gath[b][0] + gath[b][1]) + gath[b][2]) + gath[b][3]

        loss_ref[...] = tot[0][LOSS_ROW:LOSS_ROW + 1, :]
        dcw_out[...] = tot[2][...]
        lb = _sig(lg_ref[0:1, :] - lg_ref[1:2, :])
        d0 = tot[0][LB_ROW:LB_ROW + 1, :] * lb * (1.0 - lb)
        rowid = lax.broadcasted_iota(jnp.int32, (2, D_MODEL), 0)
        g_lg = jnp.where(rowid == 0, d0, -d0)
        dl, ml, vl = _adamw(lg_ref[...], g_lg, mlg_ref[...], vlg_ref[...])
        lg_outs[0][...] = g_lg
        lg_outs[1][...] = dl
        lg_outs[2][...] = ml
        lg_outs[3][...] = vl
        for si, (_, rows, b, r0) in enumerate(SMALL_LAYOUT):
            g = tot[b][r0:r0 + rows, :]
            dl, ml, vl = _adamw(w_refs[si][...], g, m_refs[si][...], v_refs[si][...])
            outs[4 * si][...] = g
            outs[4 * si + 1][...] = dl
            outs[4 * si + 2][...] = ml
            outs[4 * si + 3][...] = vl

    shapes = [jax.ShapeDtypeStruct((1, D_MODEL), F32), jax.ShapeDtypeStruct((8, D_FF), F32)]
    shapes += [jax.ShapeDtypeStruct((2, D_MODEL), F32)] * 4
    for w in small_w:
        shapes += [jax.ShapeDtypeStruct(w.shape, F32)] * 4
    scratch = [pltpu.VMEM(shp, F32) for shp in PACK_SHAPES]
    scratch += [pltpu.VMEM(shp, F32) for shp in PACK_SHAPES]
    scratch += [pltpu.VMEM((N_CHIP,) + shp, F32) for shp in PACK_SHAPES]
    scratch += [pltpu.SemaphoreType.DMA((nb,)), pltpu.SemaphoreType.DMA((nb,)),
                pltpu.SemaphoreType.DMA((3 * nb,)), pltpu.SemaphoreType.DMA((3 * nb,))]
    vm = pl.BlockSpec(memory_space=pltpu.VMEM)
    n_in = nr + 7 + 3 * ns
    res = _pc(body, name="small_allreduce_adamw", out_shape=tuple(shapes),
              in_specs=[vm] * n_in, out_specs=tuple([vm] * len(shapes)),
              scratch=scratch)(*rows1024, dws, dbs, dcw, dcb, logits, m_logits, v_logits,
                               *small_w, *small_m, *small_v)
    return res[0], res[1], res[2:6], res[6:]


def kernel(x, p, w_in, sgu_w_s, sgu_b_s, sgu_norm_g, sgu_norm_b, hgrn_lb_logits, hgrn_norm_g, w_branch, w_out, ln1_g, ln1_b, ffn_w_up, ffn_conv_w, ffn_conv_b, ffn_w_down, ln2_g, ln2_b, ple_w_proj, ple_w_gate, loss_target, m_w_in, m_sgu_w_s, m_sgu_b_s, m_sgu_norm_g, m_sgu_norm_b, m_hgrn_lb_logits, m_hgrn_norm_g, m_w_branch, m_w_out, m_ln1_g, m_ln1_b, m_ffn_w_up, m_ffn_conv_w, m_ffn_conv_b, m_ffn_w_down, m_ln2_g, m_ln2_b, m_ple_w_proj, m_ple_w_gate, v_w_in, v_sgu_w_s, v_sgu_b_s, v_sgu_norm_g, v_sgu_norm_b, v_hgrn_lb_logits, v_hgrn_norm_g, v_w_branch, v_w_out, v_ln1_g, v_ln1_b, v_ffn_w_up, v_ffn_conv_w, v_ffn_conv_b, v_ffn_w_down, v_ln2_g, v_ln2_b, v_ple_w_proj, v_ple_w_gate):
    t = x.shape[1]
    x2 = x.reshape(t, D_MODEL)
    x2b = x2.astype(BF16)
    p2 = p.reshape(t, PLE_DIM)
    tgt = loss_target.reshape(t, D_MODEL)
    core = lax.axis_index("c").astype(jnp.int32).reshape(1)
    chip_id = (2 * lax.axis_index("x") + lax.axis_index("y")).astype(jnp.int32).reshape(1)

    big_w = [w_in[0], w_branch[0, 0], w_branch[0, 1], w_out[0], ffn_w_up[0], ffn_w_down[0],
             ple_w_proj[0], ple_w_gate[0]]
    big_m = [m_w_in[0], m_w_branch[0, 0], m_w_branch[0, 1], m_w_out[0], m_ffn_w_up[0],
             m_ffn_w_down[0], m_ple_w_proj[0], m_ple_w_gate[0]]
    big_v = [v_w_in[0], v_w_branch[0, 0], v_w_branch[0, 1], v_w_out[0], v_ffn_w_up[0],
             v_ffn_w_down[0], v_ple_w_proj[0], v_ple_w_gate[0]]
    def halves_of(i):
        w = big_w[i]
        return w.astype(BF16).reshape(2, w.shape[0] // 2, w.shape[1])

    def stacked(g, i):
        return g.reshape(N_CHIP, big_w[i].shape[0], big_w[i].shape[1])


    cid = jnp.arange(SGU_BLOCK) // CHUNK
    maskf = (cid[:, None] >= cid[None, :]).astype(F32)
    ws_masked = sgu_w_s[0] * maskf[None]
    wm = ws_masked.astype(BF16)
    wmt = jnp.transpose(ws_masked, (0, 2, 1)).astype(BF16)
    bsb = jnp.broadcast_to(sgu_b_s[0][:, :, None], (N_GROUP, SGU_BLOCK, 128))

    up_rows = big_w[4].shape[0] // 2
    up_blocks = [big_w[4][k * up_rows:(k + 1) * up_rows].astype(BF16).reshape(2, up_rows // 2, -1)
                 for k in range(2)]
    h, win_g, (up0_g,) = _in_proj_gathering(x2b, halves_of(0), chip_id, 512, _gather_comm([up_blocks[0]]))
    win_st = stacked(win_g, 0)
    ya, _ = _sgu_fwd(h, wm, bsb, sgu_norm_g, sgu_norm_b)
    (yb, o_all, st_all), mix_g = _hgrn_fwd(
        h, hgrn_lb_logits, hgrn_norm_g,
        comm=_gather_comm([halves_of(i) for i in (1, 2, 3)] + [up_blocks[1]], [ffn_conv_w[0]]))
    wb0, wb1, wo = [stacked(g, i).reshape(D_MODEL, D_MODEL) for g, i in zip(mix_g[:3], (1, 2, 3))]
    wup_st = jnp.concatenate([g.reshape(N_CHIP, up_rows, -1) for g in (up0_g, mix_g[3])], axis=1)
    convw = jnp.transpose(mix_g[4], (1, 0, 2)).reshape(3, D_FF)
    (r1, a_br, b_br, m_bf, x1b), _ = _mix_fwd(ya, yb, h, x2, wb0, wb1, wo, ln1_g, ln1_b, 256)
    h2, act, out_g = _ffn_up_act(x1b, wup_st, convw, ffn_conv_b, 512,
                                 _gather_comm([halves_of(i) for i in (5, 6, 7)]))
    wd = stacked(out_g[0], 5).reshape(D_FF, D_MODEL)
    wpp = jnp.transpose(stacked(out_g[1], 6), (1, 0, 2)).reshape(PLE_DIM, D_MODEL)
    wpg = stacked(out_g[2], 7).reshape(D_MODEL, D_MODEL)
    dr2, dpg, dpp, loss_acc, dg2, db2 = _out_fwd_bwd(
        act, x1b, r1, p2, tgt, wd, wpg, wpp, ln1_g, ln1_b, ln2_g, ln2_b, 256)

    dh2, dr1, dcw, dcb, dg1, db1 = _ffn_bwd(h2, dr2, wd, wup_st, dpg, wpg, r1, ln1_g, convw, 256)
    d_wd = _mm_tn("ffn_down_wgrad", act, dr2, FF_TILE, 512)
    d_wpg = _mm_tn("ple_gate_wgrad", x1b, dpg, 512, D_MODEL)
    d_wpp_st = _mm_tn("ple_proj_wgrad", p2, dpp, PLE_DIM, PLE_DIM, stacked=True)
    d_wup_st = _mm("ffn_up_wgrad", x1b, dh2, TN, (2, N_CHIP),
                   pl.BlockSpec((t, 512), lambda i, j: (0, i)),
                   pl.BlockSpec((None, t, FF_TILE), lambda i, j: (j // FF_NJ, 0, j % FF_NJ)),
                   jax.ShapeDtypeStruct((N_CHIP, D_MODEL, FF_TILE), BF16),
                   pl.BlockSpec((None, 512, FF_TILE), lambda i, j: (j, i, 0)))
    da_bf, db_bf, dh, dya, dyb = _mix_bwd(dr1, h, a_br, b_br, wo, wb0, wb1, 256)
    d_wo = _mm_tn("out_proj_wgrad", m_bf, dr1, 512, 512)
    d_wb0 = _mm_tn("branch0_wgrad", ya, da_bf, 512, D_MODEL)
    d_wb1 = _mm_tn("branch1_wgrad", yb, db_bf, 512, D_MODEL)
    grads_1 = [d_wb0.reshape(4, 256, D_MODEL), d_wb1.reshape(4, 256, D_MODEL),
               d_wo.reshape(4, 256, D_MODEL), d_wup_st, d_wd.reshape(4, D_FF // 4, D_MODEL),
               d_wpp_st, d_wpg.reshape(4, 256, D_MODEL)]
    (dh, dws, dbs, dgv, dbv), recv_a1 = _sgu_bwd(h, dya, wm, wmt, bsb, sgu_norm_g, sgu_norm_b, maskf, dh,
                                                 comm=_sibling_exchange_comm(grads_1))
    parts_1 = [_rs_add_halves("rs_add_halves%d" % (i + 1), g, r, core)
               for i, (g, r) in enumerate(zip(grads_1, recv_a1))]
    (dh, dlb, dgn), recv_b1 = _hgrn_bwd(h, o_all, dyb, st_all, hgrn_lb_logits, hgrn_norm_g, dh,
                                         comm=_chip_exchange_comm(parts_1))

    grads_0 = [_in_proj_wgrad(x2b, dh, 512, D_MODEL)]
    recv_a0 = _run_comm("rs_sibling_exchange0", _sibling_exchange_comm(grads_0))
    parts_0 = [_rs_add_halves("rs_add_halves0", grads_0[0], recv_a0[0], core)]
    gx, recv_b0 = _in_proj_xgrad(dh, win_st, dr1, 512, _chip_exchange_comm(parts_0))
    parts = parts_0 + parts_1
    recv_b = list(recv_b0) + list(recv_b1)
    halves = [_rs_sum_chips("rs_sum_chips%d" % i, pt, r, chip_id)
              for i, (pt, r) in enumerate(zip(parts, recv_b))]
    theirs = _rs_send_halves(halves)
    big_out = [_adamw_rows("adamw_big%d" % i, halves[i], theirs[i], big_w[i], big_m[i], big_v[i], core)
               for i in range(len(halves))]

    small_in = dict(sgu_w_s=(sgu_w_s, m_sgu_w_s, v_sgu_w_s), sgu_b_s=(sgu_b_s, m_sgu_b_s, v_sgu_b_s),
                    sgu_norm_g=(sgu_norm_g, m_sgu_norm_g, v_sgu_norm_g),
                    sgu_norm_b=(sgu_norm_b, m_sgu_norm_b, v_sgu_norm_b),
                    hgrn_norm_g=(hgrn_norm_g, m_hgrn_norm_g, v_hgrn_norm_g),
                    ln1_g=(ln1_g, m_ln1_g, v_ln1_g), ln1_b=(ln1_b, m_ln1_b, v_ln1_b),
                    ffn_conv_b=(ffn_conv_b, m_ffn_conv_b, v_ffn_conv_b),
                    ln2_g=(ln2_g, m_ln2_g, v_ln2_g), ln2_b=(ln2_b, m_ln2_b, v_ln2_b))

    def flat(name, arr):
        rows = dict((n, r) for n, r, _, _ in SMALL_LAYOUT)[name]
        return arr.reshape(rows, arr.size // rows)

    names = [n for n, _, _, _ in SMALL_LAYOUT]
    sw = [flat(n, small_in[n][0]) for n in names]
    sm = [flat(n, small_in[n][1]) for n in names]
    sv = [flat(n, small_in[n][2]) for n in names]
    loss_rows, dcw_tot, lg_out, small_out = _small_allreduce_adamw(
        [dgv, dbv, dlb, dgn, dg1, db1, dg2, db2, loss_acc], dws.reshape(N_GROUP * 128, 128), dbs, dcw, dcb,
        hgrn_lb_logits, m_hgrn_lb_logits, v_hgrn_lb_logits, sw, sm, sv)
    loss = loss_rows[0, 0]

    chip = 2 * lax.axis_index("x") + lax.axis_index("y")
    g_cw = lax.dynamic_slice(dcw_tot, (0, chip * (D_FF // 4)), (3, D_FF // 4))
    cw_out = _adamw_whole("adamw_conv_w", g_cw, ffn_conv_w[0], m_ffn_conv_w[0], v_ffn_conv_w[0])

    res = {}
    for si, n in enumerate(names):
        shp = small_in[n][0].shape
        res[n] = tuple(small_out[4 * si + k].reshape(shp) for k in range(4))
    res["hgrn_lb_logits"] = tuple(lg_out)
    res["ffn_conv_w"] = (g_cw[None],) + tuple(o[None] for o in cw_out)

    def big(i):
        return tuple(big_out[i])

    res["w_in"] = tuple(o[None] for o in big(0))
    res["w_branch"] = tuple(jnp.stack([o0, o1])[None] for o0, o1 in zip(big(1), big(2)))
    res["w_out"] = tuple(o[None] for o in big(3))
    res["ffn_w_up"] = tuple(o[None] for o in big(4))
    res["ffn_w_down"] = tuple(o[None] for o in big(5))
    res["ple_w_proj"] = tuple(o[None] for o in big(6))
    res["ple_w_gate"] = tuple(o[None] for o in big(7))

    order = ["w_in", "sgu_w_s", "sgu_b_s", "sgu_norm_g", "sgu_norm_b", "hgrn_lb_logits",
             "hgrn_norm_g", "w_branch", "w_out", "ln1_g", "ln1_b", "ffn_w_up", "ffn_conv_w",
             "ffn_conv_b", "ffn_w_down", "ln2_g", "ln2_b", "ple_w_proj", "ple_w_gate"]
    outs = [loss, gx.reshape(1, t, D_MODEL)]
    for k in range(4):
        outs += [res[n][k] for n in order]
    return tuple(outs)
```

```python
import jax
import jax.numpy as jnp
from jax import lax
from jax.experimental import pallas as pl
from jax.experimental.pallas import tpu as pltpu

F32 = jnp.float32
BF16 = jnp.bfloat16
HIGHEST = lax.Precision.HIGHEST
MESH = pl.DeviceIdType.MESH

D_MODEL = 1024
CHUNK = 64
SGU_BLOCK = 128
SGU_STEP_BLOCKS = 4
SGU_ROWS = SGU_STEP_BLOCKS * SGU_BLOCK
N_GROUP = 8
N_HEAD = 8
HEAD_DIM = 128
D_FF = 2816
PLE_DIM = 256
LN_EPS = 1e-5
RMS_EPS = 1e-6
ALPHA = 2.0 ** 0.25
N_CHIP = 4

ADAM_LR = 0.001
ADAM_B1 = 0.9
ADAM_B2 = 0.999
ADAM_EPS = 1e-08
ADAM_WD = 0.01
ADAM_STEP = 10

VMEM_LIMIT = 56 * 1024 * 1024

NN = (((1,), (0,)), ((), ()))
NT = (((1,), (1,)), ((), ()))
TN = (((0,), (0,)), ((), ()))


def _pc(body, *, name, out_shape, grid=None, in_specs=None, out_specs=None, scratch=(),
        sem=None, nsp=0, vmem=VMEM_LIMIT, aliases=None):
    params = dict(vmem_limit_bytes=vmem)
    if sem is not None:
        params["dimension_semantics"] = sem
    kw = dict(name=name, out_shape=out_shape, compiler_params=pltpu.CompilerParams(**params))
    if aliases:
        kw["input_output_aliases"] = aliases
    if nsp:
        kw["grid_spec"] = pltpu.PrefetchScalarGridSpec(
            num_scalar_prefetch=nsp, grid=grid, in_specs=in_specs, out_specs=out_specs,
            scratch_shapes=list(scratch))
    else:
        if grid is not None:
            kw["grid"] = grid
        if in_specs is not None:
            kw["in_specs"] = in_specs
            kw["out_specs"] = out_specs
        kw["scratch_shapes"] = list(scratch)
    return pl.pallas_call(body, **kw)


def _dot(a, b, dims=NN):
    return lax.dot_general(a.astype(BF16), b.astype(BF16), dims, preferred_element_type=F32)


def _dot32(a, b, dims=NN):
    return lax.dot_general(a, b, dims, precision=HIGHEST, preferred_element_type=F32)


def _sig(x):
    return 1.0 / (1.0 + jnp.exp(-x))


_GC = 0.7978845608028654
_GA = 0.044715


def _gelu(x):
    return 0.5 * x * (1.0 + jnp.tanh(_GC * (x + _GA * x * x * x)))


def _gelu_and_grad(x):
    t = jnp.tanh(_GC * (x + _GA * x * x * x))
    g = 0.5 * x * (1.0 + t)
    dg = 0.5 * (1.0 + t) + 0.5 * x * (1.0 - t * t) * _GC * (1.0 + 3.0 * _GA * x * x)
    return g, dg


def _ln_stats(r):
    mu = jnp.mean(r, axis=-1, keepdims=True)
    xc = r - mu
    var = jnp.mean(xc * xc, axis=-1, keepdims=True)
    rstd = lax.rsqrt(var + LN_EPS)
    return xc * rstd, rstd


def _ln_bwd(dxh, xh, rstd):
    m1 = jnp.mean(dxh, axis=-1, keepdims=True)
    m2 = jnp.mean(dxh * xh, axis=-1, keepdims=True)
    return rstd * (dxh - m1 - xh * m2)


def _colsum8(v):
    return jnp.broadcast_to(jnp.sum(v, axis=0, keepdims=True), (8, v.shape[1]))


def _adamw(w, g, m, v):
    m2 = ADAM_B1 * m + (1.0 - ADAM_B1) * g
    v2 = ADAM_B2 * v + (1.0 - ADAM_B2) * (g * g)
    m_hat = m2 / (1.0 - ADAM_B1 ** ADAM_STEP)
    v_hat = v2 / (1.0 - ADAM_B2 ** ADAM_STEP)
    delta = -ADAM_LR * (m_hat / (jnp.sqrt(v_hat) + ADAM_EPS) + ADAM_WD * w)
    return delta, m2, v2


def _row_tile(rows, cols, itemsize=4, budget=1 << 20, mult=8):
    best = mult
    for tr in range(mult, rows + 1, mult):
        if rows % tr == 0 and tr * cols * itemsize <= budget:
            best = tr
    return best


def _mm(name, a, b, dims, grid, a_spec, b_spec, out_shape, o_spec):
    out_dtype = out_shape.dtype

    def body(a_ref, b_ref, o_ref):
        o_ref[...] = _dot(a_ref[...], b_ref[...], dims).astype(out_dtype)

    return _pc(body, name=name, out_shape=out_shape, grid=grid, in_specs=[a_spec, b_spec],
               out_specs=o_spec, sem=("parallel", "parallel"))(a, b)


class _Comm:
    def __init__(self, ins, out_shapes, sems, start, finish):
        self.ins, self.out_shapes, self.sems = list(ins), list(out_shapes), list(sems)
        self.start, self.finish = start, finish


def _hosted_call(body, comm, first, last, *, name, out_shape, grid, in_specs, out_specs, scratch, sem,
                 args, aliases=None):
    n_in, n_out, n_scr = len(in_specs), len(out_shape), len(scratch)
    nci, nco = len(comm.ins), len(comm.out_shapes)

    def wrapped(*refs):
        pos = n_in
        own_in, c_in = refs[:pos], refs[pos:pos + nci]
        pos += nci
        own_out, c_out = refs[pos:pos + n_out], refs[pos + n_out:pos + n_out + nco]
        pos += n_out + nco
        own_scr, c_sem = refs[pos:pos + n_scr], refs[pos + n_scr:]

        @pl.when(first())
        def _():
            comm.start(c_in, c_out, c_sem)

        body(*own_in, *own_out, *own_scr)

        @pl.when(last())
        def _():
            comm.finish(c_in, c_out, c_sem)

    return _pc(wrapped, name=name, out_shape=tuple(out_shape) + tuple(comm.out_shapes), grid=grid,
               in_specs=list(in_specs) + [ANY] * nci, out_specs=tuple(out_specs) + tuple([ANY] * nco),
               scratch=list(scratch) + comm.sems, sem=sem, aliases=aliases)(*args, *comm.ins)


def _grid1_call(body, comm, n, *, name, out_shape, in_specs, out_specs, scratch, args, aliases=None):
    if comm is None:
        return _pc(body, name=name, out_shape=out_shape, grid=(n,), in_specs=in_specs, out_specs=out_specs,
                   scratch=scratch, sem=("arbitrary",), aliases=aliases)(*args), ()
    res = _hosted_call(body, comm, lambda: pl.program_id(0) == 0, lambda: pl.program_id(0) == n - 1,
                       name=name, out_shape=out_shape, grid=(n,), in_specs=in_specs,
                       out_specs=out_specs, scratch=scratch, sem=("arbitrary",), args=args, aliases=aliases)
    return res[:len(out_shape)], res[len(out_shape):]


def _run_comm(name, comm):
    nci, nco = len(comm.ins), len(comm.out_shapes)

    def body(*refs):
        c_in, c_out, c_sem = refs[:nci], refs[nci:nci + nco], refs[nci + nco:]
        comm.start(c_in, c_out, c_sem)
        comm.finish(c_in, c_out, c_sem)

    return _pc(body, name=name, out_shape=tuple(comm.out_shapes), in_specs=[ANY] * nci,
               out_specs=tuple([ANY] * nco), scratch=comm.sems)(*comm.ins)


def _mm_tn(name, a, b, tm, tn, stacked=False):
    t, m = a.shape
    _, n = b.shape
    if stacked:
        assert tm == m
        out_shape = jax.ShapeDtypeStruct((n // tn, m, tn), BF16)
        o_spec = pl.BlockSpec((None, tm, tn), lambda i, j: (j, 0, 0))
    else:
        out_shape = jax.ShapeDtypeStruct((m, n), BF16)
        o_spec = pl.BlockSpec((tm, tn), lambda i, j: (i, j))
    return _mm(name, a, b, TN, (m // tm, n // tn),
               pl.BlockSpec((t, tm), lambda i, j: (0, i)),
               pl.BlockSpec((t, tn), lambda i, j: (0, j)),
               out_shape, o_spec)


DH_SLOT = (2, 0, 1, 3)


def _dh_slot(j):
    return jnp.where(j == 3, 3, (j + 2) % 3)


def _in_proj_wgrad(x2b, dh, tm, tn):
    t, m = x2b.shape
    n = dh.shape[2]

    def body(a_ref, b_ref, o_ref):
        o_ref[...] = _dot(a_ref[...], b_ref[...], TN).astype(BF16)

    return _pc(body, name="in_proj_wgrad", out_shape=jax.ShapeDtypeStruct((N_CHIP, m, n), BF16),
               grid=(N_CHIP, m // tm, n // tn),
               in_specs=[pl.BlockSpec((t, tm), lambda j, i, k: (0, i)),
                         pl.BlockSpec((None, t, tn), lambda j, i, k: (_dh_slot(j), 0, k))],
               out_specs=pl.BlockSpec((None, tm, tn), lambda j, i, k: (j, i, k)),
               sem=("parallel", "parallel", "parallel"))(x2b, dh)


def _in_proj_xgrad(dh, win_st, dr1, tm, comm):
    t = dr1.shape[0]
    ni = t // tm

    def body(a_ref, b_ref, add_ref, o_ref, acc):
        j = pl.program_id(1)
        prod = _dot(a_ref[...], b_ref[...], NT)

        @pl.when(j == 0)
        def _():
            acc[...] = prod + ALPHA * add_ref[...]

        @pl.when((j > 0) & (j < N_CHIP - 1))
        def _():
            acc[...] += prod

        @pl.when(j == N_CHIP - 1)
        def _():
            o_ref[...] = acc[...] + prod

    tile = pl.BlockSpec((tm, D_MODEL), lambda i, j: (i, 0))
    res = _hosted_call(body, comm,
                       lambda: (pl.program_id(0) == 0) & (pl.program_id(1) == 0),
                       lambda: (pl.program_id(0) == ni - 1) & (pl.program_id(1) == N_CHIP - 1),
                       name="in_proj_xgrad", out_shape=(jax.ShapeDtypeStruct((t, D_MODEL), F32),),
                       grid=(ni, N_CHIP),
                       in_specs=[pl.BlockSpec((None, tm, 2 * D_MODEL), lambda i, j: (_dh_slot(j), i, 0)),
                                 pl.BlockSpec((None, D_MODEL, 2 * D_MODEL), lambda i, j: (j, 0, 0)),
                                 tile],
                       out_specs=(tile,), scratch=[pltpu.VMEM((tm, D_MODEL), F32)],
                       sem=("arbitrary", "arbitrary"), args=[dh, win_st, dr1])
    return res[0], res[1:]


def _sgu_mixed(v, wm_ref, bsb_ref, gv, bv):
    gl, dgl = _gelu_and_grad(v)
    vh, rstd = _ln_stats(gl)
    vn = vh * gv + bv
    mixed = []
    for g in range(N_GROUP):
        sl = slice(g * 128, (g + 1) * 128)
        mixed.append(_dot(wm_ref[g], vn[:, sl]) + bsb_ref[g])
    return dgl, vh, rstd, vn, mixed


def _sgu_fwd(h, wm, bsb, gv, bv, comm=None):
    t = h.shape[0]

    def body(u_ref, v_ref, wm_ref, bsb_ref, gv_ref, bv_ref, ya_ref):
        for bb in range(SGU_STEP_BLOCKS):
            rows = slice(bb * SGU_BLOCK, (bb + 1) * SGU_BLOCK)
            u = u_ref[rows, :].astype(F32)
            _, _, _, _, mixed = _sgu_mixed(v_ref[rows, :].astype(F32), wm_ref, bsb_ref, gv_ref[...],
                                           bv_ref[...])
            gu = _gelu(u)
            for g in range(N_GROUP):
                sl = slice(g * 128, (g + 1) * 128)
                ya_ref[rows, sl] = (gu[:, sl] * mixed[g]).astype(BF16)

    full3 = pl.BlockSpec((N_GROUP, 128, 128), lambda i: (0, 0, 0))
    vec = pl.BlockSpec((1, D_MODEL), lambda i: (0, 0))
    (ya,), extra = _grid1_call(
        body, comm, t // SGU_ROWS, name="sgu_fwd",
        out_shape=(jax.ShapeDtypeStruct((t, D_MODEL), BF16),),
        in_specs=[pl.BlockSpec((SGU_ROWS, D_MODEL), lambda i: (i, 0)),
                  pl.BlockSpec((SGU_ROWS, D_MODEL), lambda i: (i, 1)),
                  full3, full3, vec, vec],
        out_specs=(pl.BlockSpec((SGU_ROWS, D_MODEL), lambda i: (i, 0)),),
        scratch=[], args=(h, h, wm, bsb, gv, bv))
    return ya, extra


def _sgu_bwd(h, dya, wm, wmt, bsb, gv, bv, maskf, dh_buf, comm=None):
    t = h.shape[0]
    nb = t // SGU_ROWS

    def body(u_ref, v_ref, dya_ref, wm_ref, wmt_ref, bsb_ref, gv_ref, bv_ref, mask_ref, dh_buf_ref,
             dh_ref, dws_ref, dbs_ref, dgv_ref, dbv_ref, dmix_acc):
        i = pl.program_id(0)

        @pl.when(i == 0)
        def _():
            dws_ref[...] = jnp.zeros_like(dws_ref)
            dgv_ref[...] = jnp.zeros_like(dgv_ref)
            dbv_ref[...] = jnp.zeros_like(dbv_ref)
            dmix_acc[...] = jnp.zeros_like(dmix_acc)

        gvv = gv_ref[...]
        for bb in range(SGU_STEP_BLOCKS):
            rows = slice(bb * SGU_BLOCK, (bb + 1) * SGU_BLOCK)
            u = u_ref[rows, :].astype(F32)
            dgl_v, vh, rstd, vn, mixed = _sgu_mixed(v_ref[rows, :].astype(F32), wm_ref, bsb_ref, gvv,
                                                    bv_ref[...])
            gu, dgl_u = _gelu_and_grad(u)
            dya_v = dya_ref[rows, :].astype(F32)
            dvn_parts = []
            for g in range(N_GROUP):
                sl = slice(g * 128, (g + 1) * 128)
                d_y = dya_v[:, sl]
                dh_ref[rows, sl] = (d_y * mixed[g] * dgl_u[:, sl]).astype(BF16)
                d_mixed = d_y * gu[:, sl]
                dmix_acc[g] += d_mixed
                dws_ref[g] += _dot(d_mixed, vn[:, sl], NT) * mask_ref[...]
                dvn_parts.append(_dot(wmt_ref[g], d_mixed))
            dvn = jnp.concatenate(dvn_parts, axis=1)
            dgv_ref[...] += _colsum8(dvn * vh)
            dbv_ref[...] += _colsum8(dvn)
            d_gl = _ln_bwd(dvn * gvv, vh, rstd)
            dh_ref[rows, D_MODEL:] = (d_gl * dgl_v).astype(BF16)

        @pl.when(i == nb - 1)
        def _():
            rowid = lax.broadcasted_iota(jnp.int32, (8, 128), 0)
            ones = jnp.ones((8, 128), F32)
            acc = jnp.zeros((8, 128), F32)
            for g in range(N_GROUP):
                rs = _dot32(ones, dmix_acc[g], NT)
                acc = jnp.where(rowid == g, rs, acc)
            dbs_ref[...] = acc

    full3 = pl.BlockSpec((N_GROUP, 128, 128), lambda i: (0, 0, 0))
    vec = pl.BlockSpec((1, D_MODEL), lambda i: (0, 0))
    acc8 = pl.BlockSpec((8, D_MODEL), lambda i: (0, 0))
    return _grid1_call(
        body, comm, nb, name="sgu_bwd",
        out_shape=(jax.ShapeDtypeStruct(dh_buf.shape, BF16),
                   jax.ShapeDtypeStruct((N_GROUP, 128, 128), F32),
                   jax.ShapeDtypeStruct((8, 128), F32),
                   jax.ShapeDtypeStruct((8, D_MODEL), F32),
                   jax.ShapeDtypeStruct((8, D_MODEL), F32)),
        in_specs=[pl.BlockSpec((SGU_ROWS, D_MODEL), lambda i: (i, 0)),
                  pl.BlockSpec((SGU_ROWS, D_MODEL), lambda i: (i, 1)),
                  pl.BlockSpec((SGU_ROWS, D_MODEL), lambda i: (i, 0)),
                  full3, full3, full3, vec, vec,
                  pl.BlockSpec((128, 128), lambda i: (0, 0)), ANY],
        out_specs=(pl.BlockSpec((None, SGU_ROWS, 2 * D_MODEL), lambda i: (DH_SLOT[0], i, 0)),
                   full3, pl.BlockSpec((8, 128), lambda i: (0, 0)), acc8, acc8),
        scratch=[pltpu.VMEM((N_GROUP, 128, 128), F32)],
        args=(h, h, dya, wm, wmt, bsb, gv, bv, maskf, dh_buf), aliases={9: 0})


def _tri_masks():
    row = lax.broadcasted_iota(jnp.int32, (CHUNK, CHUNK), 0)
    col = lax.broadcasted_iota(jnp.int32, (CHUNK, CHUNK), 1)
    return col <= row, col >= row


def _heads(v):
    return [v[:, hd * HEAD_DIM:(hd + 1) * HEAD_DIM] for hd in range(N_HEAD)]


def _tri_cumsum(tri_bf, v):
    hi = v.astype(BF16)
    r = v - hi.astype(F32)
    mid = r.astype(BF16)
    lo = (r - mid.astype(F32)).astype(BF16)
    return _dot(tri_bf, hi) + _dot(tri_bf, mid) + _dot(tri_bf, lo)


def _hgrn_chunk(q, fp, ii, lb, st_heads, causal, with_o=True):
    sg = _sig(fp)
    f = lb + (1.0 - lb) * sg
    k = 1.0 - f
    c = _tri_cumsum(causal.astype(BF16), jnp.log(f))
    ec = jnp.exp(c)
    en = jnp.exp(-c)
    sq = _sig(q)
    qt = q * sq * ec
    kt = k * en
    ecl = jnp.exp(c[CHUNK - 1:CHUNK, :])
    kk = kt * ecl
    qtb, ktb, iib, kkb = qt.astype(BF16), kt.astype(BF16), ii.astype(BF16), kk.astype(BF16)
    attn, o = [], []
    for hd, (qh, kh, ih) in enumerate(zip(_heads(qtb), _heads(ktb), _heads(iib))):
        a = jnp.where(causal, _dot(qh, kh, NT), 0.0).astype(BF16)
        attn.append(a)
        if with_o:
            o.append(_dot(a, ih) + _dot(qh, st_heads[hd], NT))
    return dict(sg=sg, f=f, k=k, ec=ec, en=en, sq=sq, ecl=ecl, kk=kk, qtb=qtb, ktb=ktb, iib=iib,
                kkb=kkb, attn=attn, o=o)


def _rms_heads(o_heads):
    rinv = [lax.rsqrt(jnp.mean(o * o, axis=-1, keepdims=True) + RMS_EPS) for o in o_heads]
    return rinv, jnp.concatenate([o * r for o, r in zip(o_heads, rinv)], axis=1)


HG_CHUNKS = 8
HG_ROWS = HG_CHUNKS * CHUNK


def _hgrn_fwd(h, logits, gn, comm=None):
    t = h.shape[0]
    nb = t // HG_ROWS

    def body(q_ref, f_ref, i_ref, og_ref, lg_ref, gn_ref, yb_ref, o_ref, st_ref, state):
        @pl.when(pl.program_id(0) == 0)
        def _():
            state[...] = jnp.zeros_like(state)

        causal, _ = _tri_masks()
        lb = _sig(lg_ref[0:1, :] - lg_ref[1:2, :])
        gnv = gn_ref[...]
        st = [state[hd] for hd in range(N_HEAD)]
        for cc in range(HG_CHUNKS):
            rows = slice(cc * CHUNK, (cc + 1) * CHUNK)
            og = og_ref[rows, :].astype(F32)
            r = _hgrn_chunk(q_ref[rows, :].astype(F32), f_ref[rows, :].astype(F32),
                            i_ref[rows, :].astype(F32), lb, [s.astype(BF16) for s in st], causal)
            o_bf = jnp.concatenate(r["o"], axis=1).astype(BF16)
            o_ref[rows, :] = o_bf
            _, on = _rms_heads(_heads(o_bf.astype(F32)))
            yb_ref[rows, :] = (on * gnv * (og * _sig(og))).astype(BF16)
            for hd in range(N_HEAD):
                st_ref[cc, hd] = st[hd]
            st = [s * e + _dot(ih, kh, TN)
                  for s, e, ih, kh in zip(st, _heads(r["ecl"]), _heads(r["iib"]), _heads(r["kkb"]))]
        for hd in range(N_HEAD):
            state[hd] = st[hd]

    def col(k):
        return pl.BlockSpec((HG_ROWS, D_MODEL), lambda ci: (ci, k))

    return _grid1_call(body, comm, nb, name="hgrn_fwd",
                       out_shape=(jax.ShapeDtypeStruct((t, D_MODEL), BF16),
                                  jax.ShapeDtypeStruct((t, D_MODEL), BF16),
                                  jax.ShapeDtypeStruct((t // CHUNK, N_HEAD, HEAD_DIM, HEAD_DIM), F32)),
                       in_specs=[col(2), col(3), col(4), col(5),
                                 pl.BlockSpec((2, D_MODEL), lambda ci: (0, 0)),
                                 pl.BlockSpec((1, D_MODEL), lambda ci: (0, 0))],
                       out_specs=(pl.BlockSpec((HG_ROWS, D_MODEL), lambda ci: (ci, 0)),
                                  pl.BlockSpec((HG_ROWS, D_MODEL), lambda ci: (ci, 0)),
                                  pl.BlockSpec((HG_CHUNKS, N_HEAD, HEAD_DIM, HEAD_DIM),
                                               lambda ci: (ci, 0, 0, 0))),
                       scratch=[pltpu.VMEM((N_HEAD, HEAD_DIM, HEAD_DIM), F32)],
                       args=(h, h, h, h, logits, gn))


def _hgrn_chunk_bwd(q, fp, ii, og, o_saved, dy, gnv, lb, st, dsn, causal, anti):
    stb = [s.astype(BF16) for s in st]
    dsnb = [s.astype(BF16) for s in dsn]
    r = _hgrn_chunk(q, fp, ii, lb, stb, causal, with_o=False)
    rinv, on = _rms_heads(_heads(o_saved))
    so = _sig(og)
    sil = og * so
    d_og = dy * on * gnv * (so * (1.0 + og * (1.0 - so)))
    d_on = dy * gnv * sil
    d_ob = jnp.concatenate(
        [ri * (dn - oh * jnp.mean(dn * oh, axis=-1, keepdims=True))
         for ri, dn, oh in zip(rinv, _heads(d_on), _heads(on))], axis=1).astype(BF16)
    d_i, d_qt, d_kt, d_kk, d_st, st_dsn = [], [], [], [], [], []
    ecl = _heads(r["ecl"])
    for hd, (dh, qh, kh, ih, kkh) in enumerate(zip(_heads(d_ob), _heads(r["qtb"]), _heads(r["ktb"]),
                                                   _heads(r["iib"]), _heads(r["kkb"]))):
        d_attn = jnp.where(causal, _dot(dh, ih, NT), 0.0).astype(BF16)
        d_i.append(_dot(r["attn"][hd], dh, TN) + _dot(kkh, dsnb[hd], NT))
        d_qt.append(_dot(d_attn, kh) + _dot(dh, stb[hd]))
        d_kt.append(_dot(d_attn, qh, TN))
        d_kk.append(_dot(ih, dsnb[hd]))
        d_st.append(_dot(dh, qh, TN) + dsn[hd] * ecl[hd])
        st_dsn.append(jnp.sum(st[hd] * dsn[hd], axis=0, keepdims=True))
    d_qt = jnp.concatenate(d_qt, axis=1)
    d_kt = jnp.concatenate(d_kt, axis=1)
    d_kk = jnp.concatenate(d_kk, axis=1)
    kk = r["kk"]
    d_cl = r["ecl"] * jnp.concatenate(st_dsn, axis=1) + jnp.sum(kk * d_kk, axis=0, keepdims=True)
    d_k = (d_kk * r["ecl"] + d_kt) * r["en"]
    d_c = d_qt * r["qtb"].astype(F32) - d_kt * r["ktb"].astype(F32) - d_kk * kk
    rowid = lax.broadcasted_iota(jnp.int32, (CHUNK, D_MODEL), 0)
    d_c = d_c + jnp.where(rowid == CHUNK - 1, d_cl, 0.0)
    d_lf = _tri_cumsum(anti.astype(BF16), d_c)
    d_f = d_lf / r["f"] - d_k
    sg, sq = r["sg"], r["sq"]
    d_q = d_qt * r["ec"] * (sq * (1.0 + q * (1.0 - sq)))
    d_fp = d_f * (1.0 - lb) * sg * (1.0 - sg)
    return (d_q, d_fp, jnp.concatenate(d_i, axis=1), d_og, d_st,
            _colsum8(dy * on * sil), _colsum8(d_f * (1.0 - sg)))


def _hgrn_bwd(h, o_all, dyb, st_all, logits, gn, dh_buf, comm=None):
    t = h.shape[0]
    nb = t // HG_ROWS

    def body(q_ref, f_ref, i_ref, og_ref, o_ref, dyb_ref, st_ref, lg_ref, gn_ref, dh_buf_ref,
             dh_ref, dlb_ref, dgn_ref, dstate):
        @pl.when(pl.program_id(0) == 0)
        def _():
            dstate[...] = jnp.zeros_like(dstate)
            dlb_ref[...] = jnp.zeros_like(dlb_ref)
            dgn_ref[...] = jnp.zeros_like(dgn_ref)

        causal, anti = _tri_masks()
        lb = _sig(lg_ref[0:1, :] - lg_ref[1:2, :])
        gnv = gn_ref[...]
        dsn = [dstate[hd] for hd in range(N_HEAD)]
        dgn_acc = jnp.zeros((8, D_MODEL), F32)
        dlb_acc = jnp.zeros((8, D_MODEL), F32)
        for cc in reversed(range(HG_CHUNKS)):
            rows = slice(cc * CHUNK, (cc + 1) * CHUNK)
            d_q, d_fp, d_i, d_og, dsn, dgn_c, dlb_c = _hgrn_chunk_bwd(
                q_ref[rows, :].astype(F32), f_ref[rows, :].astype(F32), i_ref[rows, :].astype(F32),
                og_ref[rows, :].astype(F32), o_ref[rows, :].astype(F32), dyb_ref[rows, :].astype(F32), gnv, lb,
                [st_ref[cc, hd] for hd in range(N_HEAD)], dsn, causal, anti)
            dgn_acc = dgn_acc + dgn_c
            dlb_acc = dlb_acc + dlb_c
            dh_ref[0, rows, :D_MODEL] = d_q.astype(BF16)
            dh_ref[0, rows, D_MODEL:] = d_fp.astype(BF16)
            dh_ref[1, rows, :D_MODEL] = d_i.astype(BF16)
            dh_ref[1, rows, D_MODEL:] = d_og.astype(BF16)
        dgn_ref[...] += dgn_acc
        dlb_ref[...] += dlb_acc
        for hd in range(N_HEAD):
            dstate[hd] = dsn[hd]

    def col(k):
        return pl.BlockSpec((HG_ROWS, D_MODEL), lambda ci: (nb - 1 - ci, k))

    acc8 = pl.BlockSpec((8, D_MODEL), lambda ci: (0, 0))
    pair = pl.BlockSpec((2, HG_ROWS, 2 * D_MODEL), lambda ci: (0, nb - 1 - ci, 0))
    return _grid1_call(body, comm, nb, name="hgrn_bwd",
                       out_shape=(jax.ShapeDtypeStruct(dh_buf.shape, BF16),
                                  jax.ShapeDtypeStruct((8, D_MODEL), F32),
                                  jax.ShapeDtypeStruct((8, D_MODEL), F32)),
                       in_specs=[col(2), col(3), col(4), col(5), col(0),
                                 pl.BlockSpec((HG_ROWS, D_MODEL), lambda ci: (nb - 1 - ci, 0)),
                                 pl.BlockSpec((HG_CHUNKS, N_HEAD, HEAD_DIM, HEAD_DIM),
                                              lambda ci: (nb - 1 - ci, 0, 0, 0)),
                                 pl.BlockSpec((2, D_MODEL), lambda ci: (0, 0)),
                                 pl.BlockSpec((1, D_MODEL), lambda ci: (0, 0)), ANY],
                       out_specs=(pair, acc8, acc8),
                       scratch=[pltpu.VMEM((N_HEAD, HEAD_DIM, HEAD_DIM), F32)],
                       args=(h, h, h, h, o_all, dyb, st_all, logits, gn, dh_buf), aliases={9: 0})


def _mix_fwd(ya, yb, h, x, wb0, wb1, wo, g1, b1, tm, comm=None):
    t = x.shape[0]

    def body(ya_ref, yb_ref, ga_ref, gb_ref, x_ref, wb0_ref, wb1_ref, wo_ref, g1_ref, b1_ref,
             r1_ref, a_ref, b_ref, m_ref, x1_ref):
        a = _dot(ya_ref[...], wb0_ref[...])
        b = _dot(yb_ref[...], wb1_ref[...])
        m = _sig(ga_ref[...].astype(F32)) * a + _sig(gb_ref[...].astype(F32)) * b
        r1 = ALPHA * x_ref[...] + _dot(m, wo_ref[...])
        xh, _ = _ln_stats(r1)
        r1_ref[...] = r1
        a_ref[...] = a.astype(BF16)
        b_ref[...] = b.astype(BF16)
        m_ref[...] = m.astype(BF16)
        x1_ref[...] = (xh * g1_ref[...] + b1_ref[...]).astype(BF16)

    tile = pl.BlockSpec((tm, D_MODEL), lambda i: (i, 0))
    wsp = pl.BlockSpec((D_MODEL, D_MODEL), lambda i: (0, 0))
    vec = pl.BlockSpec((1, D_MODEL), lambda i: (0, 0))
    f32o = jax.ShapeDtypeStruct((t, D_MODEL), F32)
    bfo = jax.ShapeDtypeStruct((t, D_MODEL), BF16)
    return _grid1_call(body, comm, t // tm, name="mix_fwd", out_shape=(f32o, bfo, bfo, bfo, bfo),
                       in_specs=[tile, tile,
                                 pl.BlockSpec((tm, D_MODEL), lambda i: (i, 6)),
                                 pl.BlockSpec((tm, D_MODEL), lambda i: (i, 7)),
                                 tile, wsp, wsp, wsp, vec, vec],
                       out_specs=(tile, tile, tile, tile, tile),
                       scratch=[], args=(ya, yb, h, h, x, wb0, wb1, wo, g1, b1))


def _mix_bwd(dr1, h, a, b, wo, wb0, wb1, tm):
    t = dr1.shape[0]

    def body(dr1_ref, ga_ref, gb_ref, a_ref, b_ref, wo_ref, wb0_ref, wb1_ref,
             da_ref, db_ref, dh3_ref, dya_ref, dyb_ref):
        d_m = _dot(dr1_ref[...], wo_ref[...], NT)
        sa = _sig(ga_ref[...].astype(F32))
        sb = _sig(gb_ref[...].astype(F32))
        d_a = (d_m * sa).astype(BF16)
        d_b = (d_m * sb).astype(BF16)
        da_ref[...] = d_a
        db_ref[...] = d_b
        dh3_ref[:, :D_MODEL] = (d_m * a_ref[...].astype(F32) * sa * (1.0 - sa)).astype(BF16)
        dh3_ref[:, D_MODEL:] = (d_m * b_ref[...].astype(F32) * sb * (1.0 - sb)).astype(BF16)
        dya_ref[...] = _dot(d_a, wb0_ref[...], NT).astype(BF16)
        dyb_ref[...] = _dot(d_b, wb1_ref[...], NT).astype(BF16)

    tile = pl.BlockSpec((tm, D_MODEL), lambda i: (i, 0))
    wsp = pl.BlockSpec((D_MODEL, D_MODEL), lambda i: (0, 0))
    f32o = jax.ShapeDtypeStruct((t, D_MODEL), F32)
    bfo = jax.ShapeDtypeStruct((t, D_MODEL), BF16)
    return _pc(body, name="mix_bwd",
               out_shape=(bfo, bfo, jax.ShapeDtypeStruct((N_CHIP, t, 2 * D_MODEL), BF16), bfo, bfo),
               grid=(t // tm,),
               in_specs=[tile,
                         pl.BlockSpec((tm, D_MODEL), lambda i: (i, 6)),
                         pl.BlockSpec((tm, D_MODEL), lambda i: (i, 7)),
                         tile, tile, wsp, wsp, wsp],
               out_specs=(tile, tile, pl.BlockSpec((None, tm, 2 * D_MODEL), lambda i: (DH_SLOT[3], i, 0)),
                          tile, tile),
               sem=("parallel",))(dr1, h, h, a, b, wo, wb0, wb1)


FF_TILE = 1408
FF_NJ = D_FF // FF_TILE


def _shift_down(v, k):
    return pltpu.roll(v, k, 0)


def _shift_up(v, k):
    return pltpu.roll(v, v.shape[0] - k, 0)


HALO = 16
FF_PIECES = ((0, 768), (768, FF_TILE))


def _ffn_up_act(x1b, wup_st, convw, convb, tm, comm):
    t = x1b.shape[0]
    ni = t // tm
    nth = tm // HALO

    def body(x_ref, xp_ref, wg_ref, wv_ref, cw_ref, cb_ref, h2_ref, act_ref):
        wg = wg_ref[...]
        gate = _dot(x_ref[...], wg).astype(BF16)
        val = _dot(x_ref[...], wv_ref[...]).astype(BF16)
        prev = (_dot(xp_ref[...], wg) * (pl.program_id(0) > 0).astype(F32)).astype(BF16)
        h2_ref[0] = gate
        h2_ref[1] = val
        ext = jnp.concatenate([prev.astype(F32), gate.astype(F32)], axis=0)
        gc = (cw_ref[0:1, :] * _shift_down(ext, 2) + cw_ref[1:2, :] * _shift_down(ext, 1)
              + cw_ref[2:3, :] * ext + cb_ref[...])[HALO:, :].astype(BF16)
        h2_ref[2] = gc
        act_ref[...] = (_gelu(gc.astype(F32)) * val.astype(F32)).astype(BF16)

    res = _hosted_call(
        body, comm,
        lambda: (pl.program_id(0) == 0) & (pl.program_id(1) == 0),
        lambda: (pl.program_id(0) == ni - 1) & (pl.program_id(1) == FF_NJ - 1),
        name="ffn_up",
        out_shape=(jax.ShapeDtypeStruct((3, t, D_FF), BF16), jax.ShapeDtypeStruct((t, D_FF), BF16)),
        grid=(ni, FF_NJ),
        in_specs=[pl.BlockSpec((tm, D_MODEL), lambda i, j: (i, 0)),
                  pl.BlockSpec((HALO, D_MODEL), lambda i, j: (jnp.maximum(i * nth - 1, 0), 0)),
                  pl.BlockSpec((None, D_MODEL, FF_TILE), lambda i, j: (j, 0, 0)),
                  pl.BlockSpec((None, D_MODEL, FF_TILE), lambda i, j: (j + FF_NJ, 0, 0)),
                  pl.BlockSpec((3, FF_TILE), lambda i, j: (0, j)),
                  pl.BlockSpec((1, FF_TILE), lambda i, j: (0, j))],
        out_specs=(pl.BlockSpec((3, tm, FF_TILE), lambda i, j: (0, i, j)),
                   pl.BlockSpec((tm, FF_TILE), lambda i, j: (i, j))),
        scratch=[], sem=("arbitrary", "arbitrary"),
        args=(x1b, x1b, wup_st, wup_st, convw, convb))
    return res[0], res[1], res[2:]


def _out_fwd_bwd(act, x1b, r1, p2, tgt, wd, wpg, wpp, g1, b1, g2, b2, tm):
    t = r1.shape[0]

    def body(act_ref, x1b_ref, r1_ref, p_ref, tgt_ref, wd_ref, wpg_ref, wpp_ref,
             g1_ref, b1_ref, g2_ref, b2_ref,
             dr2_ref, dpg_ref, dpp_ref, loss_ref, dg2_ref, db2_ref):
        i = pl.program_id(0)

        @pl.when(i == 0)
        def _():
            loss_ref[...] = jnp.zeros_like(loss_ref)
            dg2_ref[...] = jnp.zeros_like(dg2_ref)
            db2_ref[...] = jnp.zeros_like(db2_ref)

        ffn = _dot(act_ref[...], wd_ref[...])
        pg = _dot(x1b_ref[...], wpg_ref[...])
        pp = _dot(p_ref[...], wpp_ref[...])
        s = _sig(pg)
        xh1, _ = _ln_stats(r1_ref[...])
        x1 = xh1 * g1_ref[...] + b1_ref[...]
        r2 = ALPHA * x1 + ffn + s * pp
        xh2, rstd2 = _ln_stats(r2)
        g2v = g2_ref[...]
        diff = xh2 * g2v + b2_ref[...] - tgt_ref[...]
        part = jnp.sum(jnp.sum(diff * diff, axis=1, keepdims=True), axis=0, keepdims=True)
        loss_ref[...] += jnp.broadcast_to(part * (0.5 / D_MODEL), loss_ref.shape)
        dy = diff * (1.0 / D_MODEL)
        dg2_ref[...] += _colsum8(dy * xh2)
        db2_ref[...] += _colsum8(dy)
        dr2 = _ln_bwd(dy * g2v, xh2, rstd2)
        dr2_ref[...] = dr2
        dpg_ref[...] = (dr2 * pp * s * (1.0 - s)).astype(BF16)
        dpp_ref[...] = (dr2 * s).astype(BF16)

    tile = pl.BlockSpec((tm, D_MODEL), lambda i: (i, 0))
    vec = pl.BlockSpec((1, D_MODEL), lambda i: (0, 0))
    acc8 = pl.BlockSpec((8, D_MODEL), lambda i: (0, 0))
    acc_shape = jax.ShapeDtypeStruct((8, D_MODEL), F32)
    return _pc(body, name="out_fwd_bwd",
               out_shape=(jax.ShapeDtypeStruct((t, D_MODEL), F32),
                          jax.ShapeDtypeStruct((t, D_MODEL), BF16),
                          jax.ShapeDtypeStruct((t, D_MODEL), BF16),
                          acc_shape, acc_shape, acc_shape),
               grid=(t // tm,),
               in_specs=[pl.BlockSpec((tm, D_FF), lambda i: (i, 0)), tile, tile,
                         pl.BlockSpec((tm, PLE_DIM), lambda i: (i, 0)), tile,
                         pl.BlockSpec((D_FF, D_MODEL), lambda i: (0, 0)),
                         pl.BlockSpec((D_MODEL, D_MODEL), lambda i: (0, 0)),
                         pl.BlockSpec((PLE_DIM, D_MODEL), lambda i: (0, 0)),
                         vec, vec, vec, vec],
               out_specs=(tile, tile, tile, acc8, acc8, acc8),
               sem=("arbitrary",))(act, x1b, r1, p2, tgt, wd, wpg, wpp, g1, b1, g2, b2)


def _ffn_bwd(h2, dr2, wd, wup_st, dpg, wpg, r1, g1, convw, tm):
    t = r1.shape[0]
    ni = t // tm
    nth = tm // HALO
    last_halo = t // HALO - 1
    main_rows = slice(0, tm)

    def body(g_ref, gc_ref, gcn_ref, v_ref, vn_ref, dr2_ref, dr2n_ref, wd_ref, wug_ref, wuv_ref,
             cw_ref, dpg_ref, wpg_ref, r1_ref, g1_ref,
             dh2_ref, dr1_ref, dcw_ref, dcb_ref, dg1_ref, db1_ref, acc):
        i = pl.program_id(0)
        j = pl.program_id(1)

        @pl.when((i == 0) & (j == 0))
        def _():
            dcw_ref[...] = jnp.zeros_like(dcw_ref)
            dcb_ref[...] = jnp.zeros_like(dcb_ref)
            dg1_ref[...] = jnp.zeros_like(dg1_ref)
            db1_ref[...] = jnp.zeros_like(db1_ref)

        dr2v = dr2_ref[...].astype(BF16)
        dr2n = dr2n_ref[...].astype(BF16)
        more = (i < ni - 1).astype(F32)
        prod = None
        dcw_parts, dcb_parts = [], []
        for c0, c1 in FF_PIECES:
            pc = slice(c0, c1)
            da = _dot(dr2v, wd_ref[pc, :], NT)
            dnext = _dot(dr2n, wd_ref[pc, :], NT) * more
            gc = jnp.concatenate([gc_ref[:, pc].astype(F32), gcn_ref[:, pc].astype(F32)], axis=0)
            vext = jnp.concatenate([v_ref[:, pc].astype(F32), vn_ref[:, pc].astype(F32)], axis=0)
            dext = jnp.concatenate([da, dnext], axis=0)
            gl, dgl = _gelu_and_grad(gc)
            d_gc = dext * vext * dgl
            up1 = _shift_up(d_gc, 1)[main_rows, :]
            up2 = _shift_up(d_gc, 2)[main_rows, :]
            dm = d_gc[main_rows, :]
            d_gate = (cw_ref[2:3, pc] * dm + cw_ref[1:2, pc] * up1 + cw_ref[0:1, pc] * up2).astype(BF16)
            d_val = (da * gl[main_rows, :]).astype(BF16)
            dh2_ref[0, :, pc] = d_gate
            dh2_ref[1, :, pc] = d_val
            g = g_ref[:, pc].astype(F32)
            s0 = jnp.sum(g * up2, axis=0, keepdims=True)
            s1 = jnp.sum(g * up1, axis=0, keepdims=True)
            s2 = jnp.sum(g * dm, axis=0, keepdims=True)
            rowid = lax.broadcasted_iota(jnp.int32, (8, c1 - c0), 0)
            dcw_parts.append(jnp.where(rowid == 0, s0, jnp.where(rowid == 1, s1,
                                                                 jnp.where(rowid == 2, s2, 0.0))))
            dcb_parts.append(_colsum8(dm))
            part = _dot(d_gate, wug_ref[:, pc], NT) + _dot(d_val, wuv_ref[:, pc], NT)
            prod = part if prod is None else prod + part
        dcw_part = jnp.concatenate(dcw_parts, axis=1)
        dcb_part = jnp.concatenate(dcb_parts, axis=1)
        for jj in range(FF_NJ):
            @pl.when(j == jj)
            def _(jj=jj):
                cols = slice(jj * FF_TILE, (jj + 1) * FF_TILE)
                dcw_ref[:, cols] += dcw_part
                dcb_ref[:, cols] += dcb_part

        @pl.when(j == 0)
        def _():
            acc[...] = prod

        @pl.when(j > 0)
        def _():
            acc[...] += prod

        @pl.when(j == FF_NJ - 1)
        def _():
            d_x1 = acc[...] + _dot(dpg_ref[...], wpg_ref[...], NT) + ALPHA * dr2_ref[...]
            xh, rstd = _ln_stats(r1_ref[...])
            dg1_ref[...] += _colsum8(d_x1 * xh)
            db1_ref[...] += _colsum8(d_x1)
            dr1_ref[...] = _ln_bwd(d_x1 * g1_ref[...], xh, rstd)

    def h2_main(part):
        return pl.BlockSpec((None, tm, FF_TILE), lambda i, j: (part, i, j))

    def h2_next(part):
        return pl.BlockSpec((None, HALO, FF_TILE),
                            lambda i, j: (part, jnp.minimum((i + 1) * nth, last_halo), j))

    tile = pl.BlockSpec((tm, D_MODEL), lambda i, j: (i, 0))
    acc8 = pl.BlockSpec((8, D_MODEL), lambda i, j: (0, 0))
    accff = pl.BlockSpec((8, D_FF), lambda i, j: (0, 0))
    acc_shape = jax.ShapeDtypeStruct((8, D_MODEL), F32)
    accff_shape = jax.ShapeDtypeStruct((8, D_FF), F32)
    return _pc(body, name="ffn_bwd",
               out_shape=(jax.ShapeDtypeStruct((2, t, D_FF), BF16),
                          jax.ShapeDtypeStruct((t, D_MODEL), F32),
                          accff_shape, accff_shape, acc_shape, acc_shape),
               grid=(ni, FF_NJ),
               in_specs=[h2_main(0), h2_main(2), h2_next(2), h2_main(1), h2_next(1),
                         tile,
                         pl.BlockSpec((HALO, D_MODEL), lambda i, j: (jnp.minimum((i + 1) * nth, last_halo), 0)),
                         pl.BlockSpec((FF_TILE, D_MODEL), lambda i, j: (j, 0)),
                         pl.BlockSpec((None, D_MODEL, FF_TILE), lambda i, j: (j, 0, 0)),
                         pl.BlockSpec((None, D_MODEL, FF_TILE), lambda i, j: (j + FF_NJ, 0, 0)),
                         pl.BlockSpec((3, FF_TILE), lambda i, j: (0, j)),
                         tile, pl.BlockSpec((D_MODEL, D_MODEL), lambda i, j: (0, 0)),
                         tile, pl.BlockSpec((1, D_MODEL), lambda i, j: (0, 0))],
               out_specs=(pl.BlockSpec((2, tm, FF_TILE), lambda i, j: (0, i, j)),
                          tile, accff, accff, acc8, acc8),
               scratch=[pltpu.VMEM((tm, D_MODEL), F32)],
               sem=("arbitrary", "arbitrary"))(h2, h2, h2, h2, h2, dr2, dr2, wd, wup_st, wup_st,
                                               convw, dpg, wpg, r1, g1)


ANY = pl.BlockSpec(memory_space=pl.ANY)


def _chip_peers():
    x, y, c = lax.axis_index("x"), lax.axis_index("y"), lax.axis_index("c")
    return x, y, c, [(1 - x, y), (x, 1 - y), (1 - x, 1 - y)]


def _gather_comm(halved, whole=()):
    n, nw = len(halved), len(whole)

    def copies(ins, outs, sems):
        ici_send, ici_recv, d2d_send, d2d_recv, own_send, own_recv = sems
        x, y, c, peers = _chip_peers()
        me = 2 * x + y
        sibling = (x, y, 1 - c)
        own, ici, ici_wait, fwd, fwd_wait = [], [], [], [], []
        for ti in range(n + nw):
            src, dst = ins[ti], outs[ti]
            own.append(pltpu.make_async_remote_copy(
                src_ref=src, dst_ref=dst.at[me], send_sem=own_send.at[ti], recv_sem=own_recv.at[ti],
                device_id=sibling, device_id_type=MESH))
            for k, (px, py) in enumerate(peers):
                pk = 2 * px + py
                sem = dict(send_sem=ici_send.at[ti * 3 + k], recv_sem=ici_recv.at[ti * 3 + k],
                           device_id=(px, py, c), device_id_type=MESH)
                if ti < n:
                    ici.append(pltpu.make_async_remote_copy(src_ref=src.at[c], dst_ref=dst.at[me, c], **sem))
                    ici_wait.append(pltpu.make_async_remote_copy(src_ref=src.at[c], dst_ref=dst.at[pk, c], **sem))
                    dsem = dict(send_sem=d2d_send.at[ti * 3 + k], recv_sem=d2d_recv.at[ti * 3 + k],
                                device_id=sibling, device_id_type=MESH)
                    fwd.append(pltpu.make_async_remote_copy(src_ref=dst.at[pk, c], dst_ref=dst.at[pk, c], **dsem))
                    fwd_wait.append(pltpu.make_async_remote_copy(
                        src_ref=dst.at[pk, 1 - c], dst_ref=dst.at[pk, 1 - c], **dsem))
                else:
                    ici.append(pltpu.make_async_remote_copy(src_ref=src, dst_ref=dst.at[me], **sem))
                    ici_wait.append(pltpu.make_async_remote_copy(src_ref=src, dst_ref=dst.at[pk], **sem))
        return own, ici, ici_wait, fwd, fwd_wait

    def start(ins, outs, sems):
        own, ici, _, _, _ = copies(ins, outs, sems)
        for cp in own + ici:
            cp.start()

    def finish(ins, outs, sems):
        own, ici, ici_wait, fwd, fwd_wait = copies(ins, outs, sems)
        for i, cp in enumerate(ici_wait):
            cp.wait_recv()
            if i < len(fwd):
                fwd[i].start()
        for cp in fwd_wait + own:
            cp.wait_recv()
        for cp in own + ici + fwd:
            cp.wait_send()

    srcs = list(halved) + list(whole)
    return _Comm(srcs, [jax.ShapeDtypeStruct((N_CHIP,) + s.shape, s.dtype) for s in srcs],
                 [pltpu.SemaphoreType.DMA((3 * (n + nw),)), pltpu.SemaphoreType.DMA((3 * (n + nw),)),
                  pltpu.SemaphoreType.DMA((max(3 * n, 1),)), pltpu.SemaphoreType.DMA((max(3 * n, 1),)),
                  pltpu.SemaphoreType.DMA((n + nw,)), pltpu.SemaphoreType.DMA((n + nw,))],
                 start, finish)


def _sibling_exchange_comm(grads):
    n = len(grads)

    def copies(ins, outs, sems):
        send_sems, recv_sems = sems
        x, y, c = lax.axis_index("x"), lax.axis_index("y"), lax.axis_index("c")
        res = []
        for ti in range(n):
            half = ins[ti].shape[1] // 2
            res.append(pltpu.make_async_remote_copy(
                src_ref=ins[ti].at[:, pl.ds(pl.multiple_of((1 - c) * half, 16), half), :],
                dst_ref=outs[ti],
                send_sem=send_sems.at[ti], recv_sem=recv_sems.at[ti],
                device_id=(x, y, 1 - c), device_id_type=MESH))
        return res

    def start(ins, outs, sems):
        for cp in copies(ins, outs, sems):
            cp.start()

    def finish(ins, outs, sems):
        for cp in copies(ins, outs, sems):
            cp.wait()

    return _Comm(grads, [jax.ShapeDtypeStruct((N_CHIP, g.shape[1] // 2, g.shape[2]), g.dtype) for g in grads],
                 [pltpu.SemaphoreType.DMA((n,)), pltpu.SemaphoreType.DMA((n,))], start, finish)


def _in_proj_gathering(x2b, own, chip, tm, comm):
    t = x2b.shape[0]
    ni = t // tm
    half, cols = own.shape[1], own.shape[2]
    nci, nco = len(comm.ins), len(comm.out_shapes)

    def body(chip_ref, x_ref, own_ref, own_hbm, *rest):
        c_in = rest[:nci]
        h_ref, win_out = rest[nci:nci + 2]
        c_out = rest[nci + 2:nci + 2 + nco]
        w_scr, ici_send, ici_recv, d2d_send, d2d_recv, own_sems, ld_sems = rest[nci + 2 + nco:nci + 9 + nco]
        c_sem = rest[nci + 9 + nco:]
        s, i = pl.program_id(0), pl.program_id(1)
        x, y, c, peers = _chip_peers()
        me = 2 * x + y
        sibling = (x, y, 1 - c)

        def ici(k, slot):
            px, py = peers[k]
            return pltpu.make_async_remote_copy(
                src_ref=own_hbm.at[c], dst_ref=win_out.at[slot, c],
                send_sem=ici_send.at[k], recv_sem=ici_recv.at[k],
                device_id=(px, py, c), device_id_type=MESH)

        def forward(k, core):
            pk = 2 * peers[k][0] + peers[k][1]
            return pltpu.make_async_remote_copy(
                src_ref=win_out.at[pk, core], dst_ref=win_out.at[pk, core],
                send_sem=d2d_send.at[k], recv_sem=d2d_recv.at[k],
                device_id=sibling, device_id_type=MESH)

        place_own = pltpu.make_async_remote_copy(
            src_ref=own_hbm, dst_ref=win_out.at[me], send_sem=own_sems.at[0], recv_sem=own_sems.at[1],
            device_id=sibling, device_id_type=MESH)

        @pl.when((s == 0) & (i == 0))
        def _():
            for k in range(2):
                ici(k, me).start()
            place_own.start()

        @pl.when(s == 0)
        def _():
            xv = x_ref[...]
            h_ref[...] = (_dot(xv[:, :half], own_ref[0]) + _dot(xv[:, half:], own_ref[1])).astype(BF16)

        for k in range(3):
            @pl.when((s == k + 1) & (i == 0))
            def _(k=k):
                pk = 2 * peers[k][0] + peers[k][1]
                ici(k, pk).wait_recv()
                if k == 0:
                    ici(2, me).start()
                forward(k, c).start()
                forward(k, 1 - c).wait_recv()
                loads = [pltpu.make_async_copy(win_out.at[pk, hh], w_scr.at[hh], ld_sems.at[hh])
                         for hh in range(2)]
                for ld in loads:
                    ld.start()
                for ld in loads:
                    ld.wait()
                if k == 1:
                    comm.start(c_in, c_out, c_sem)

        @pl.when(s > 0)
        def _():
            xv = x_ref[...]
            h_ref[...] = (_dot(xv[:, :half], w_scr[0]) + _dot(xv[:, half:], w_scr[1])).astype(BF16)

        @pl.when((s == N_CHIP - 1) & (i == ni - 1))
        def _():
            place_own.wait()
            for k in range(3):
                ici(k, me).wait_send()
                forward(k, c).wait_send()
            comm.finish(c_in, c_out, c_sem)

    def shard_col(s, me):
        return jnp.where(s == 0, me, me ^ jnp.where(s == 1, 2, jnp.where(s == 2, 1, 3)))

    res = _pc(body, name="in_proj",
              out_shape=(jax.ShapeDtypeStruct((t, N_CHIP * cols), BF16),
                         jax.ShapeDtypeStruct((N_CHIP,) + own.shape, own.dtype)) + tuple(comm.out_shapes),
              grid=(N_CHIP, ni), nsp=1,
              in_specs=[pl.BlockSpec((tm, 2 * half), lambda s, i, chip_ref: (i, 0)),
                        pl.BlockSpec(own.shape, lambda s, i, chip_ref: (0, 0, 0)),
                        ANY] + [ANY] * nci,
              out_specs=(pl.BlockSpec((tm, cols), lambda s, i, chip_ref: (i, shard_col(s, chip_ref[0]))),
                         ANY) + tuple([ANY] * nco),
              scratch=[pltpu.VMEM(own.shape, own.dtype),
                       pltpu.SemaphoreType.DMA((3,)), pltpu.SemaphoreType.DMA((3,)),
                       pltpu.SemaphoreType.DMA((3,)), pltpu.SemaphoreType.DMA((3,)),
                       pltpu.SemaphoreType.DMA((2,)), pltpu.SemaphoreType.DMA((2,))] + comm.sems,
              sem=("arbitrary", "arbitrary"))(chip, x2b, own, own, *comm.ins)
    return res[0], res[1], res[2:]


def _rs_add_halves(name, grad, recv, core):
    _, r, cdim = grad.shape
    half = r // 2
    tr = _row_tile(half, cdim, mult=16)
    nr = half // tr

    def body(c_ref, g_ref, r_ref, o_ref):
        o_ref[...] = (g_ref[...].astype(F32) + r_ref[...].astype(F32)).astype(BF16)

    return _pc(body, name=name, out_shape=jax.ShapeDtypeStruct((N_CHIP, half, cdim), BF16),
               grid=(N_CHIP, nr), nsp=1,
               in_specs=[pl.BlockSpec((None, tr, cdim), lambda j, i, c_ref: (j, c_ref[0] * nr + i, 0)),
                         pl.BlockSpec((None, tr, cdim), lambda j, i, c_ref: (j, i, 0))],
               out_specs=pl.BlockSpec((None, tr, cdim), lambda j, i, c_ref: (j, i, 0)),
               sem=("parallel", "parallel"))(core, grad, recv)


def _chip_exchange_comm(parts):
    n = len(parts)

    def copies(ins, outs, sems):
        send_sems, recv_sems = sems
        x, y, c, peers = _chip_peers()
        return [pltpu.make_async_remote_copy(
            src_ref=ins[ti].at[2 * px + py], dst_ref=outs[ti].at[k],
            send_sem=send_sems.at[ti * 3 + k], recv_sem=recv_sems.at[ti * 3 + k],
            device_id=(px, py, c), device_id_type=MESH)
            for ti in range(n) for k, (px, py) in enumerate(peers)]

    def start(ins, outs, sems):
        for cp in copies(ins, outs, sems):
            cp.start()

    def finish(ins, outs, sems):
        for cp in copies(ins, outs, sems):
            cp.wait()

    return _Comm(parts, [jax.ShapeDtypeStruct((3,) + p.shape[1:], p.dtype) for p in parts],
                 [pltpu.SemaphoreType.DMA((3 * n,)), pltpu.SemaphoreType.DMA((3 * n,))], start, finish)


def _rs_sum_chips(name, part, recv, chip):
    _, half, cdim = recv.shape
    tr = _row_tile(half, cdim, mult=16)

    def body(chip_ref, p_ref, r_ref, o_ref):
        o_ref[...] = ((p_ref[...].astype(F32) + r_ref[0].astype(F32)) + r_ref[1].astype(F32)
                      ) + r_ref[2].astype(F32)

    return _pc(body, name=name, out_shape=jax.ShapeDtypeStruct((half, cdim), F32),
               grid=(half // tr,), nsp=1,
               in_specs=[pl.BlockSpec((None, tr, cdim), lambda i, chip_ref: (chip_ref[0], i, 0)),
                         pl.BlockSpec((3, tr, cdim), lambda i, chip_ref: (0, i, 0))],
               out_specs=pl.BlockSpec((tr, cdim), lambda i, chip_ref: (i, 0)),
               sem=("parallel",))(chip, part, recv)


def _rs_send_halves(halves):
    n = len(halves)

    def body(*refs):
        ins, outs = refs[:n], refs[n:2 * n]
        send_sems, recv_sems = refs[2 * n:]
        x, y, c = lax.axis_index("x"), lax.axis_index("y"), lax.axis_index("c")
        sends = []
        for ti in range(n):
            cp = pltpu.make_async_remote_copy(
                src_ref=ins[ti], dst_ref=outs[ti],
                send_sem=send_sems.at[ti], recv_sem=recv_sems.at[ti],
                device_id=(x, y, 1 - c), device_id_type=MESH)
            cp.start()
            sends.append(cp)
        for cp in sends:
            cp.wait()

    return _pc(body, name="rs_send_halves",
               out_shape=tuple(jax.ShapeDtypeStruct(hv.shape, hv.dtype) for hv in halves),
               in_specs=[ANY] * n, out_specs=tuple([ANY] * n),
               scratch=[pltpu.SemaphoreType.DMA((n,)), pltpu.SemaphoreType.DMA((n,))])(*halves)


def _adamw_rows(name, mine, theirs, w, m, v, core):
    half, cdim = mine.shape
    tr = _row_tile(half, cdim, budget=1 << 19)
    nrh = half // tr

    def body(c_ref, mine_ref, theirs_ref, w_ref, m_ref, v_ref, g_ref, d_ref, m2_ref, v2_ref):
        is_mine = (pl.program_id(0) // nrh) == c_ref[0]
        g = jnp.where(is_mine, mine_ref[...], theirs_ref[...])
        d, m2, v2 = _adamw(w_ref[...], g, m_ref[...], v_ref[...])
        g_ref[...] = g
        d_ref[...] = d
        m2_ref[...] = m2
        v2_ref[...] = v2

    htile = pl.BlockSpec((tr, cdim), lambda i, c_ref: (i % nrh, 0))
    tile = pl.BlockSpec((tr, cdim), lambda i, c_ref: (i, 0))
    shp = jax.ShapeDtypeStruct((2 * half, cdim), F32)
    return _pc(body, name=name, out_shape=(shp, shp, shp, shp), grid=(2 * nrh,), nsp=1,
               in_specs=[htile, htile, tile, tile, tile], out_specs=(tile, tile, tile, tile),
               sem=("parallel",))(core, mine, theirs, w, m, v)


def _adamw_whole(name, g, w, m, v):
    def body(g_ref, w_ref, m_ref, v_ref, d_ref, m2_ref, v2_ref):
        d, m2, v2 = _adamw(w_ref[...], g_ref[...], m_ref[...], v_ref[...])
        d_ref[...] = d
        m2_ref[...] = m2
        v2_ref[...] = v2

    shp = jax.ShapeDtypeStruct(g.shape, F32)
    return _pc(body, name=name, out_shape=(shp, shp, shp))(g, w, m, v)


SMALL_LAYOUT = (
    ("sgu_w_s", 1024, 1, 0),
    ("sgu_b_s", 8, 1, 1024),
    ("sgu_norm_g", 1, 0, 0),
    ("sgu_norm_b", 1, 0, 1),
    ("hgrn_norm_g", 1, 0, 3),
    ("ln1_g", 1, 0, 4),
    ("ln1_b", 1, 0, 5),
    ("ffn_conv_b", 1, 2, 3),
    ("ln2_g", 1, 0, 6),
    ("ln2_b", 1, 0, 7),
)
LB_ROW = 2
LOSS_ROW = 8
PACK_SHAPES = ((16, D_MODEL), (N_GROUP * 128 + 8, 128), (8, D_FF))


def _small_allreduce_adamw(rows1024, dws, dbs, dcw, dcb, logits, m_logits, v_logits,
                           small_w, small_m, small_v):
    ns = len(SMALL_LAYOUT)
    nr = len(rows1024)
    nb = len(PACK_SHAPES)

    def body(*refs):
        row_refs = refs[:nr]
        dws_ref, dbs_ref, dcw_ref, dcb_ref, lg_ref, mlg_ref, vlg_ref = refs[nr:nr + 7]
        pos = nr + 7
        w_refs = refs[pos:pos + ns]
        m_refs = refs[pos + ns:pos + 2 * ns]
        v_refs = refs[pos + 2 * ns:pos + 3 * ns]
        pos += 3 * ns
        loss_ref, dcw_out = refs[pos:pos + 2]
        lg_outs = refs[pos + 2:pos + 6]
        pos += 6
        outs = refs[pos:pos + 4 * ns]
        pos += 4 * ns
        pack = refs[pos:pos + nb]
        sib = refs[pos + nb:pos + 2 * nb]
        gath = refs[pos + 2 * nb:pos + 3 * nb]
        d2d_send, d2d_recv, ici_send, ici_recv = refs[pos + 3 * nb:]

        x, y, c, peers = _chip_peers()
        me = 2 * x + y
        sibling = (x, y, 1 - c)

        pack[0][...] = jnp.zeros(PACK_SHAPES[0], F32)
        for k in range(nr):
            pack[0][k:k + 1, :] = row_refs[k][0:1, :]
        pack[1][0:N_GROUP * 128, :] = dws_ref[...]
        pack[1][N_GROUP * 128:, :] = dbs_ref[...]
        pack[2][...] = jnp.zeros(PACK_SHAPES[2], F32)
        pack[2][0:3, :] = dcw_ref[0:3, :]
        pack[2][3:4, :] = dcb_ref[0:1, :]

        d2d = [pltpu.make_async_remote_copy(
            src_ref=pack[b], dst_ref=sib[b], send_sem=d2d_send.at[b], recv_sem=d2d_recv.at[b],
            device_id=sibling, device_id_type=MESH) for b in range(nb)]
        for cp in d2d:
            cp.start()
        for cp in d2d:
            cp.wait()
        for b in range(nb):
            gath[b][me] = pack[b][...] + sib[b][...]

        ici, ici_wait = [], []
        for b in range(nb):
            for k, (px, py) in enumerate(peers):
                sem = dict(send_sem=ici_send.at[b * 3 + k], recv_sem=ici_recv.at[b * 3 + k],
                           device_id=(px, py, c), device_id_type=MESH)
                ici.append(pltpu.make_async_remote_copy(src_ref=gath[b].at[me], dst_ref=gath[b].at[me], **sem))
                ici_wait.append(pltpu.make_async_remote_copy(
                    src_ref=gath[b].at[me], dst_ref=gath[b].at[2 * px + py], **sem))
        for cp in ici:
            cp.start()
        for cp in ici_wait:
            cp.wait_recv()
        for cp in ici:
            cp.wait_send()

        tot = pack
        for b in range(nb):
            tot[b][...] = ((gath[b][0] + gath[b][1]) + gath[b][2]) + gath[b][3]

        loss_ref[...] = tot[0][LOSS_ROW:LOSS_ROW + 1, :]
        dcw_out[...] = tot[2][...]
        lb = _sig(lg_ref[0:1, :] - lg_ref[1:2, :])
        d0 = tot[0][LB_ROW:LB_ROW + 1, :] * lb * (1.0 - lb)
        rowid = lax.broadcasted_iota(jnp.int32, (2, D_MODEL), 0)
        g_lg = jnp.where(rowid == 0, d0, -d0)
        dl, ml, vl = _adamw(lg_ref[...], g_lg, mlg_ref[...], vlg_ref[...])
        lg_outs[0][...] = g_lg
        lg_outs[1][...] = dl
        lg_outs[2][...] = ml
        lg_outs[3][...] = vl
        for si, (_, rows, b, r0) in enumerate(SMALL_LAYOUT):
            g = tot[b][r0:r0 + rows, :]
            dl, ml, vl = _adamw(w_refs[si][...], g, m_refs[si][...], v_refs[si][...])
            outs[4 * si][...] = g
            outs[4 * si + 1][...] = dl
            outs[4 * si + 2][...] = ml
            outs[4 * si + 3][...] = vl

    shapes = [jax.ShapeDtypeStruct((1, D_MODEL), F32), jax.ShapeDtypeStruct((8, D_FF), F32)]
    shapes += [jax.ShapeDtypeStruct((2, D_MODEL), F32)] * 4
    for w in small_w:
        shapes += [jax.ShapeDtypeStruct(w.shape, F32)] * 4
    scratch = [pltpu.VMEM(shp, F32) for shp in PACK_SHAPES]
    scratch += [pltpu.VMEM(shp, F32) for shp in PACK_SHAPES]
    scratch += [pltpu.VMEM((N_CHIP,) + shp, F32) for shp in PACK_SHAPES]
    scratch += [pltpu.SemaphoreType.DMA((nb,)), pltpu.SemaphoreType.DMA((nb,)),
                pltpu.SemaphoreType.DMA((3 * nb,)), pltpu.SemaphoreType.DMA((3 * nb,))]
    vm = pl.BlockSpec(memory_space=pltpu.VMEM)
    n_in = nr + 7 + 3 * ns
    res = _pc(body, name="small_allreduce_adamw", out_shape=tuple(shapes),
              in_specs=[vm] * n_in, out_specs=tuple([vm] * len(shapes)),
              scratch=scratch)(*rows1024, dws, dbs, dcw, dcb, logits, m_logits, v_logits,
                               *small_w, *small_m, *small_v)
    return res[0], res[1], res[2:6], res[6:]


def kernel(x, p, w_in, sgu_w_s, sgu_b_s, sgu_norm_g, sgu_norm_b, hgrn_lb_logits, hgrn_norm_g, w_branch, w_out, ln1_g, ln1_b, ffn_w_up, ffn_conv_w, ffn_conv_b, ffn_w_down, ln2_g, ln2_b, ple_w_proj, ple_w_gate, loss_target, m_w_in, m_sgu_w_s, m_sgu_b_s, m_sgu_norm_g, m_sgu_norm_b, m_hgrn_lb_logits, m_hgrn_norm_g, m_w_branch, m_w_out, m_ln1_g, m_ln1_b, m_ffn_w_up, m_ffn_conv_w, m_ffn_conv_b, m_ffn_w_down, m_ln2_g, m_ln2_b, m_ple_w_proj, m_ple_w_gate, v_w_in, v_sgu_w_s, v_sgu_b_s, v_sgu_norm_g, v_sgu_norm_b, v_hgrn_lb_logits, v_hgrn_norm_g, v_w_branch, v_w_out, v_ln1_g, v_ln1_b, v_ffn_w_up, v_ffn_conv_w, v_ffn_conv_b, v_ffn_w_down, v_ln2_g, v_ln2_b, v_ple_w_proj, v_ple_w_gate):
    t = x.shape[1]
    x2 = x.reshape(t, D_MODEL)
    x2b = x2.astype(BF16)
    p2 = p.reshape(t, PLE_DIM)
    tgt = loss_target.reshape(t, D_MODEL)
    core = lax.axis_index("c").astype(jnp.int32).reshape(1)
    chip_id = (2 * lax.axis_index("x") + lax.axis_index("y")).astype(jnp.int32).reshape(1)

    big_w = [w_in[0], w_branch[0, 0], w_branch[0, 1], w_out[0], ffn_w_up[0], ffn_w_down[0],
             ple_w_proj[0], ple_w_gate[0]]
    big_m = [m_w_in[0], m_w_branch[0, 0], m_w_branch[0, 1], m_w_out[0], m_ffn_w_up[0],
             m_ffn_w_down[0], m_ple_w_proj[0], m_ple_w_gate[0]]
    big_v = [v_w_in[0], v_w_branch[0, 0], v_w_branch[0, 1], v_w_out[0], v_ffn_w_up[0],
             v_ffn_w_down[0], v_ple_w_proj[0], v_ple_w_gate[0]]
    def halves_of(i):
        w = big_w[i]
        return w.astype(BF16).reshape(2, w.shape[0] // 2, w.shape[1])

    def stacked(g, i):
        return g.reshape(N_CHIP, big_w[i].shape[0], big_w[i].shape[1])


    cid = jnp.arange(SGU_BLOCK) // CHUNK
    maskf = (cid[:, None] >= cid[None, :]).astype(F32)
    ws_masked = sgu_w_s[0] * maskf[None]
    wm = ws_masked.astype(BF16)
    wmt = jnp.transpose(ws_masked, (0, 2, 1)).astype(BF16)
    bsb = jnp.broadcast_to(sgu_b_s[0][:, :, None], (N_GROUP, SGU_BLOCK, 128))

    up_rows = big_w[4].shape[0] // 2
    up_blocks = [big_w[4][k * up_rows:(k + 1) * up_rows].astype(BF16).reshape(2, up_rows // 2, -1)
                 for k in range(2)]
    h, win_g, (up0_g,) = _in_proj_gathering(x2b, halves_of(0), chip_id, 512, _gather_comm([up_blocks[0]]))
    win_st = stacked(win_g, 0)
    ya, _ = _sgu_fwd(h, wm, bsb, sgu_norm_g, sgu_norm_b)
    (yb, o_all, st_all), mix_g = _hgrn_fwd(
        h, hgrn_lb_logits, hgrn_norm_g,
        comm=_gather_comm([halves_of(i) for i in (1, 2, 3)] + [up_blocks[1]], [ffn_conv_w[0]]))
    wb0, wb1, wo = [stacked(g, i).reshape(D_MODEL, D_MODEL) for g, i in zip(mix_g[:3], (1, 2, 3))]
    wup_st = jnp.concatenate([g.reshape(N_CHIP, up_rows, -1) for g in (up0_g, mix_g[3])], axis=1)
    convw = jnp.transpose(mix_g[4], (1, 0, 2)).reshape(3, D_FF)
    (r1, a_br, b_br, m_bf, x1b), _ = _mix_fwd(ya, yb, h, x2, wb0, wb1, wo, ln1_g, ln1_b, 256)
    h2, act, out_g = _ffn_up_act(x1b, wup_st, convw, ffn_conv_b, 512,
                                 _gather_comm([halves_of(i) for i in (5, 6, 7)]))
    wd = stacked(out_g[0], 5).reshape(D_FF, D_MODEL)
    wpp = jnp.transpose(stacked(out_g[1], 6), (1, 0, 2)).reshape(PLE_DIM, D_MODEL)
    wpg = stacked(out_g[2], 7).reshape(D_MODEL, D_MODEL)
    dr2, dpg, dpp, loss_acc, dg2, db2 = _out_fwd_bwd(
        act, x1b, r1, p2, tgt, wd, wpg, wpp, ln1_g, ln1_b, ln2_g, ln2_b, 256)

    dh2, dr1, dcw, dcb, dg1, db1 = _ffn_bwd(h2, dr2, wd, wup_st, dpg, wpg, r1, ln1_g, convw, 256)
    d_wd = _mm_tn("ffn_down_wgrad", act, dr2, FF_TILE, 512)
    d_wpg = _mm_tn("ple_gate_wgrad", x1b, dpg, 512, D_MODEL)
    d_wpp_st = _mm_tn("ple_proj_wgrad", p2, dpp, PLE_DIM, PLE_DIM, stacked=True)
    d_wup_st = _mm("ffn_up_wgrad", x1b, dh2, TN, (2, N_CHIP),
                   pl.BlockSpec((t, 512), lambda i, j: (0, i)),
                   pl.BlockSpec((None, t, FF_TILE), lambda i, j: (j // FF_NJ, 0, j % FF_NJ)),
                   jax.ShapeDtypeStruct((N_CHIP, D_MODEL, FF_TILE), BF16),
                   pl.BlockSpec((None, 512, FF_TILE), lambda i, j: (j, i, 0)))
    da_bf, db_bf, dh, dya, dyb = _mix_bwd(dr1, h, a_br, b_br, wo, wb0, wb1, 256)
    d_wo = _mm_tn("out_proj_wgrad", m_bf, dr1, 512, 512)
    d_wb0 = _mm_tn("branch0_wgrad", ya, da_bf, 512, D_MODEL)
    d_wb1 = _mm_tn("branch1_wgrad", yb, db_bf, 512, D_MODEL)
    grads_1 = [d_wb0.reshape(4, 256, D_MODEL), d_wb1.reshape(4, 256, D_MODEL),
               d_wo.reshape(4, 256, D_MODEL), d_wup_st, d_wd.reshape(4, D_FF // 4, D_MODEL),
               d_wpp_st, d_wpg.reshape(4, 256, D_MODEL)]
    (dh, dws, dbs, dgv, dbv), recv_a1 = _sgu_bwd(h, dya, wm, wmt, bsb, sgu_norm_g, sgu_norm_b, maskf, dh,
                                                 comm=_sibling_exchange_comm(grads_1))
    parts_1 = [_rs_add_halves("rs_add_halves%d" % (i + 1), g, r, core)
               for i, (g, r) in enumerate(zip(grads_1, recv_a1))]
    (dh, dlb, dgn), recv_b1 = _hgrn_bwd(h, o_all, dyb, st_all, hgrn_lb_logits, hgrn_norm_g, dh,
                                         comm=_chip_exchange_comm(parts_1))

    grads_0 = [_in_proj_wgrad(x2b, dh, 512, D_MODEL)]
    recv_a0 = _run_comm("rs_sibling_exchange0", _sibling_exchange_comm(grads_0))
    parts_0 = [_rs_add_halves("rs_add_halves0", grads_0[0], recv_a0[0], core)]
    gx, recv_b0 = _in_proj_xgrad(dh, win_st, dr1, 512, _chip_exchange_comm(parts_0))
    parts = parts_0 + parts_1
    recv_b = list(recv_b0) + list(recv_b1)
    halves = [_rs_sum_chips("rs_sum_chips%d" % i, pt, r, chip_id)
              for i, (pt, r) in enumerate(zip(parts, recv_b))]
    theirs = _rs_send_halves(halves)
    big_out = [_adamw_rows("adamw_big%d" % i, halves[i], theirs[i], big_w[i], big_m[i], big_v[i], core)
               for i in range(len(halves))]

    small_in = dict(sgu_w_s=(sgu_w_s, m_sgu_w_s, v_sgu_w_s), sgu_b_s=(sgu_b_s, m_sgu_b_s, v_sgu_b_s),
                    sgu_norm_g=(sgu_norm_g, m_sgu_norm_g, v_sgu_norm_g),
                    sgu_norm_b=(sgu_norm_b, m_sgu_norm_b, v_sgu_norm_b),
                    hgrn_norm_g=(hgrn_norm_g, m_hgrn_norm_g, v_hgrn_norm_g),
                    ln1_g=(ln1_g, m_ln1_g, v_ln1_g), ln1_b=(ln1_b, m_ln1_b, v_ln1_b),
                    ffn_conv_b=(ffn_conv_b, m_ffn_conv_b, v_ffn_conv_b),
                    ln2_g=(ln2_g, m_ln2_g, v_ln2_g), ln2_b=(ln2_b, m_ln2_b, v_ln2_b))

    def flat(name, arr):
        rows = dict((n, r) for n, r, _, _ in SMALL_LAYOUT)[name]
        return arr.reshape(rows, arr.size // rows)

    names = [n for n, _, _, _ in SMALL_LAYOUT]
    sw = [flat(n, small_in[n][0]) for n in names]
    sm = [flat(n, small_in[n][1]) for n in names]
    sv = [flat(n, small_in[n][2]) for n in names]
    loss_rows, dcw_tot, lg_out, small_out = _small_allreduce_adamw(
        [dgv, dbv, dlb, dgn, dg1, db1, dg2, db2, loss_acc], dws.reshape(N_GROUP * 128, 128), dbs, dcw, dcb,
        hgrn_lb_logits, m_hgrn_lb_logits, v_hgrn_lb_logits, sw, sm, sv)
    loss = loss_rows[0, 0]

    chip = 2 * lax.axis_index("x") + lax.axis_index("y")
    g_cw = lax.dynamic_slice(dcw_tot, (0, chip * (D_FF // 4)), (3, D_FF // 4))
    cw_out = _adamw_whole("adamw_conv_w", g_cw, ffn_conv_w[0], m_ffn_conv_w[0], v_ffn_conv_w[0])

    res = {}
    for si, n in enumerate(names):
        shp = small_in[n][0].shape
        res[n] = tuple(small_out[4 * si + k].reshape(shp) for k in range(4))
    res["hgrn_lb_logits"] = tuple(lg_out)
    res["ffn_conv_w"] = (g_cw[None],) + tuple(o[None] for o in cw_out)

    def big(i):
        return tuple(big_out[i])

    res["w_in"] = tuple(o[None] for o in big(0))
    res["w_branch"] = tuple(jnp.stack([o0, o1])[None] for o0, o1 in zip(big(1), big(2)))
    res["w_out"] = tuple(o[None] for o in big(3))
    res["ffn_w_up"] = tuple(o[None] for o in big(4))
    res["ffn_w_down"] = tuple(o[None] for o in big(5))
    res["ple_w_proj"] = tuple(o[None] for o in big(6))
    res["ple_w_gate"] = tuple(o[None] for o in big(7))

    order = ["w_in", "sgu_w_s", "sgu_b_s", "sgu_norm_g", "sgu_norm_b", "hgrn_lb_logits",
             "hgrn_norm_g", "w_branch", "w_out", "ln1_g", "ln1_b", "ffn_w_up", "ffn_conv_w",
             "ffn_conv_b", "ffn_w_down", "ln2_g", "ln2_b", "ple_w_proj", "ple_w_gate"]
    outs = [loss, gx.reshape(1, t, D_MODEL)]
    for k in range(4):
        outs += [res[n][k] for n in order]
    return tuple(outs)
```

```python
import jax
import jax.numpy as jnp
from jax import lax
from jax.experimental import pallas as pl
from jax.experimental.pallas import tpu as pltpu

F32 = jnp.float32
BF16 = jnp.bfloat16
HIGHEST = lax.Precision.HIGHEST
MESH = pl.DeviceIdType.MESH

D_MODEL = 1024
CHUNK = 64
SGU_BLOCK = 128
SGU_STEP_BLOCKS = 4
SGU_ROWS = SGU_STEP_BLOCKS * SGU_BLOCK
N_GROUP = 8
N_HEAD = 8
HEAD_DIM = 128
D_FF = 2816
PLE_DIM = 256
LN_EPS = 1e-5
RMS_EPS = 1e-6
ALPHA = 2.0 ** 0.25
N_CHIP = 4

ADAM_LR = 0.001
ADAM_B1 = 0.9
ADAM_B2 = 0.999
ADAM_EPS = 1e-08
ADAM_WD = 0.01
ADAM_STEP = 10

VMEM_LIMIT = 56 * 1024 * 1024

NN = (((1,), (0,)), ((), ()))
NT = (((1,), (1,)), ((), ()))
TN = (((0,), (0,)), ((), ()))


def _pc(body, *, name, out_shape, grid=None, in_specs=None, out_specs=None, scratch=(),
        sem=None, nsp=0, vmem=VMEM_LIMIT, aliases=None):
    params = dict(vmem_limit_bytes=vmem)
    if sem is not None:
        params["dimension_semantics"] = sem
    kw = dict(name=name, out_shape=out_shape, compiler_params=pltpu.CompilerParams(**params))
    if aliases:
        kw["input_output_aliases"] = aliases
    if nsp:
        kw["grid_spec"] = pltpu.PrefetchScalarGridSpec(
            num_scalar_prefetch=nsp, grid=grid, in_specs=in_specs, out_specs=out_specs,
            scratch_shapes=list(scratch))
    else:
        if grid is not None:
            kw["grid"] = grid
        if in_specs is not None:
            kw["in_specs"] = in_specs
            kw["out_specs"] = out_specs
        kw["scratch_shapes"] = list(scratch)
    return pl.pallas_call(body, **kw)


def _dot(a, b, dims=NN):
    return lax.dot_general(a.astype(BF16), b.astype(BF16), dims, preferred_element_type=F32)


def _dot32(a, b, dims=NN):
    return lax.dot_general(a, b, dims, precision=HIGHEST, preferred_element_type=F32)


def _sig(x):
    return 1.0 / (1.0 + jnp.exp(-x))


_GC = 0.7978845608028654
_GA = 0.044715


def _gelu(x):
    return 0.5 * x * (1.0 + jnp.tanh(_GC * (x + _GA * x * x * x)))


def _gelu_and_grad(x):
    t = jnp.tanh(_GC * (x + _GA * x * x * x))
    g = 0.5 * x * (1.0 + t)
    dg = 0.5 * (1.0 + t) + 0.5 * x * (1.0 - t * t) * _GC * (1.0 + 3.0 * _GA * x * x)
    return g, dg


def _ln_stats(r):
    mu = jnp.mean(r, axis=-1, keepdims=True)
    xc = r - mu
    var = jnp.mean(xc * xc, axis=-1, keepdims=True)
    rstd = lax.rsqrt(var + LN_EPS)
    return xc * rstd, rstd


def _ln_bwd(dxh, xh, rstd):
    m1 = jnp.mean(dxh, axis=-1, keepdims=True)
    m2 = jnp.mean(dxh * xh, axis=-1, keepdims=True)
    return rstd * (dxh - m1 - xh * m2)


def _colsum8(v):
    return jnp.broadcast_to(jnp.sum(v, axis=0, keepdims=True), (8, v.shape[1]))


def _adamw(w, g, m, v):
    m2 = ADAM_B1 * m + (1.0 - ADAM_B1) * g
    v2 = ADAM_B2 * v + (1.0 - ADAM_B2) * (g * g)
    m_hat = m2 / (1.0 - ADAM_B1 ** ADAM_STEP)
    v_hat = v2 / (1.0 - ADAM_B2 ** ADAM_STEP)
    delta = -ADAM_LR * (m_hat / (jnp.sqrt(v_hat) + ADAM_EPS) + ADAM_WD * w)
    return delta, m2, v2


def _row_tile(rows, cols, itemsize=4, budget=1 << 20, mult=8):
    best = mult
    for tr in range(mult, rows + 1, mult):
        if rows % tr == 0 and tr * cols * itemsize <= budget:
            best = tr
    return best


def _mm(name, a, b, dims, grid, a_spec, b_spec, out_shape, o_spec):
    out_dtype = out_shape.dtype

    def body(a_ref, b_ref, o_ref):
        o_ref[...] = _dot(a_ref[...], b_ref[...], dims).astype(out_dtype)

    return _pc(body, name=name, out_shape=out_shape, grid=grid, in_specs=[a_spec, b_spec],
               out_specs=o_spec, sem=("parallel", "parallel"))(a, b)


class _Comm:
    def __init__(self, ins, out_shapes, sems, start, finish, middle=None, finish_late=None):
        self.ins, self.out_shapes, self.sems = list(ins), list(out_shapes), list(sems)
        self.start, self.finish = start, finish
        self.middle, self.finish_late = middle, finish_late


def _hosted_call(body, comm, first, last, *, name, out_shape, grid, in_specs, out_specs, scratch, sem,
                 args, aliases=None, mid=None):
    n_in, n_out, n_scr = len(in_specs), len(out_shape), len(scratch)
    nci, nco = len(comm.ins), len(comm.out_shapes)

    def wrapped(*refs):
        pos = n_in
        own_in, c_in = refs[:pos], refs[pos:pos + nci]
        pos += nci
        own_out, c_out = refs[pos:pos + n_out], refs[pos + n_out:pos + n_out + nco]
        pos += n_out + nco
        own_scr, c_sem = refs[pos:pos + n_scr], refs[pos + n_scr:]

        @pl.when(first())
        def _():
            comm.start(c_in, c_out, c_sem)

        body(*own_in, *own_out, *own_scr)

        if mid is not None:
            @pl.when(mid())
            def _():
                comm.middle(c_in, c_out, c_sem)

        @pl.when(last())
        def _():
            (comm.finish if mid is None else comm.finish_late)(c_in, c_out, c_sem)

    return _pc(wrapped, name=name, out_shape=tuple(out_shape) + tuple(comm.out_shapes), grid=grid,
               in_specs=list(in_specs) + [ANY] * nci, out_specs=tuple(out_specs) + tuple([ANY] * nco),
               scratch=list(scratch) + comm.sems, sem=sem, aliases=aliases)(*args, *comm.ins)


def _grid1_call(body, comm, n, *, name, out_shape, in_specs, out_specs, scratch, args, aliases=None):
    if comm is None:
        return _pc(body, name=name, out_shape=out_shape, grid=(n,), in_specs=in_specs, out_specs=out_specs,
                   scratch=scratch, sem=("arbitrary",), aliases=aliases)(*args), ()
    res = _hosted_call(body, comm, lambda: pl.program_id(0) == 0, lambda: pl.program_id(0) == n - 1,
                       name=name, out_shape=out_shape, grid=(n,), in_specs=in_specs,
                       out_specs=out_specs, scratch=scratch, sem=("arbitrary",), args=args, aliases=aliases)
    return res[:len(out_shape)], res[len(out_shape):]


def _run_comm(name, comm):
    nci, nco = len(comm.ins), len(comm.out_shapes)

    def body(*refs):
        c_in, c_out, c_sem = refs[:nci], refs[nci:nci + nco], refs[nci + nco:]
        comm.start(c_in, c_out, c_sem)
        comm.finish(c_in, c_out, c_sem)

    return _pc(body, name=name, out_shape=tuple(comm.out_shapes), in_specs=[ANY] * nci,
               out_specs=tuple([ANY] * nco), scratch=comm.sems)(*comm.ins)


def _mm_tn(name, a, b, tm, tn, stacked=False):
    t, m = a.shape
    _, n = b.shape
    if stacked:
        assert tm == m
        out_shape = jax.ShapeDtypeStruct((n // tn, m, tn), BF16)
        o_spec = pl.BlockSpec((None, tm, tn), lambda i, j: (j, 0, 0))
    else:
        out_shape = jax.ShapeDtypeStruct((m, n), BF16)
        o_spec = pl.BlockSpec((tm, tn), lambda i, j: (i, j))
    return _mm(name, a, b, TN, (m // tm, n // tn),
               pl.BlockSpec((t, tm), lambda i, j: (0, i)),
               pl.BlockSpec((t, tn), lambda i, j: (0, j)),
               out_shape, o_spec)


DH_SLOT = (2, 0, 1, 3)


def _dh_slot(j):
    return jnp.where(j == 3, 3, (j + 2) % 3)


def _in_proj_wgrad(x2b, dh, tm, tn):
    t, m = x2b.shape
    n = dh.shape[2]

    def body(a_ref, b_ref, o_ref):
        o_ref[...] = _dot(a_ref[...], b_ref[...], TN).astype(BF16)

    return _pc(body, name="in_proj_wgrad", out_shape=jax.ShapeDtypeStruct((N_CHIP, m, n), BF16),
               grid=(N_CHIP, m // tm, n // tn),
               in_specs=[pl.BlockSpec((t, tm), lambda j, i, k: (0, i)),
                         pl.BlockSpec((None, t, tn), lambda j, i, k: (_dh_slot(j), 0, k))],
               out_specs=pl.BlockSpec((None, tm, tn), lambda j, i, k: (j, i, k)),
               sem=("parallel", "parallel", "parallel"))(x2b, dh)


def _in_proj_xgrad(dh, win_st, dr1, tm, comm):
    t = dr1.shape[0]
    ni = t // tm

    def body(a_ref, b_ref, add_ref, o_ref, acc):
        j = pl.program_id(1)
        prod = _dot(a_ref[...], b_ref[...], NT)

        @pl.when(j == 0)
        def _():
            acc[...] = prod + ALPHA * add_ref[...]

        @pl.when((j > 0) & (j < N_CHIP - 1))
        def _():
            acc[...] += prod

        @pl.when(j == N_CHIP - 1)
        def _():
            o_ref[...] = acc[...] + prod

    tile = pl.BlockSpec((tm, D_MODEL), lambda i, j: (i, 0))
    res = _hosted_call(body, comm,
                       lambda: (pl.program_id(0) == 0) & (pl.program_id(1) == 0),
                       lambda: (pl.program_id(0) == ni - 1) & (pl.program_id(1) == N_CHIP - 1),
                       name="in_proj_xgrad", out_shape=(jax.ShapeDtypeStruct((t, D_MODEL), F32),),
                       grid=(ni, N_CHIP),
                       in_specs=[pl.BlockSpec((None, tm, 2 * D_MODEL), lambda i, j: (_dh_slot(j), i, 0)),
                                 pl.BlockSpec((None, D_MODEL, 2 * D_MODEL), lambda i, j: (j, 0, 0)),
                                 tile],
                       out_specs=(tile,), scratch=[pltpu.VMEM((tm, D_MODEL), F32)],
                       sem=("arbitrary", "arbitrary"), args=[dh, win_st, dr1])
    return res[0], res[1:]


def _sgu_mixed(v, wm_ref, bsb_ref, gv, bv):
    gl, dgl = _gelu_and_grad(v)
    vh, rstd = _ln_stats(gl)
    vn = vh * gv + bv
    mixed = []
    for g in range(N_GROUP):
        sl = slice(g * 128, (g + 1) * 128)
        mixed.append(_dot(wm_ref[g], vn[:, sl]) + bsb_ref[g])
    return dgl, vh, rstd, vn, mixed


def _sgu_fwd(h, wm, bsb, gv, bv, comm=None):
    t = h.shape[0]

    def body(u_ref, v_ref, wm_ref, bsb_ref, gv_ref, bv_ref, ya_ref):
        for bb in range(SGU_STEP_BLOCKS):
            rows = slice(bb * SGU_BLOCK, (bb + 1) * SGU_BLOCK)
            u = u_ref[rows, :].astype(F32)
            _, _, _, _, mixed = _sgu_mixed(v_ref[rows, :].astype(F32), wm_ref, bsb_ref, gv_ref[...],
                                           bv_ref[...])
            gu = _gelu(u)
            for g in range(N_GROUP):
                sl = slice(g * 128, (g + 1) * 128)
                ya_ref[rows, sl] = (gu[:, sl] * mixed[g]).astype(BF16)

    full3 = pl.BlockSpec((N_GROUP, 128, 128), lambda i: (0, 0, 0))
    vec = pl.BlockSpec((1, D_MODEL), lambda i: (0, 0))
    (ya,), extra = _grid1_call(
        body, comm, t // SGU_ROWS, name="sgu_fwd",
        out_shape=(jax.ShapeDtypeStruct((t, D_MODEL), BF16),),
        in_specs=[pl.BlockSpec((SGU_ROWS, D_MODEL), lambda i: (i, 0)),
                  pl.BlockSpec((SGU_ROWS, D_MODEL), lambda i: (i, 1)),
                  full3, full3, vec, vec],
        out_specs=(pl.BlockSpec((SGU_ROWS, D_MODEL), lambda i: (i, 0)),),
        scratch=[], args=(h, h, wm, bsb, gv, bv))
    return ya, extra


def _sgu_bwd(h, dya, wm, wmt, bsb, gv, bv, maskf, dh_buf, comm=None):
    t = h.shape[0]
    nb = t // SGU_ROWS

    def body(u_ref, v_ref, dya_ref, wm_ref, wmt_ref, bsb_ref, gv_ref, bv_ref, mask_ref, dh_buf_ref,
             dh_ref, dws_ref, dbs_ref, dgv_ref, dbv_ref, dmix_acc):
        i = pl.program_id(0)

        @pl.when(i == 0)
        def _():
            dws_ref[...] = jnp.zeros_like(dws_ref)
            dgv_ref[...] = jnp.zeros_like(dgv_ref)
            dbv_ref[...] = jnp.zeros_like(dbv_ref)
            dmix_acc[...] = jnp.zeros_like(dmix_acc)

        gvv = gv_ref[...]
        for bb in range(SGU_STEP_BLOCKS):
            rows = slice(bb * SGU_BLOCK, (bb + 1) * SGU_BLOCK)
            u = u_ref[rows, :].astype(F32)
            dgl_v, vh, rstd, vn, mixed = _sgu_mixed(v_ref[rows, :].astype(F32), wm_ref, bsb_ref, gvv,
                                                    bv_ref[...])
            gu, dgl_u = _gelu_and_grad(u)
            dya_v = dya_ref[rows, :].astype(F32)
            dvn_parts = []
            for g in range(N_GROUP):
                sl = slice(g * 128, (g + 1) * 128)
                d_y = dya_v[:, sl]
                dh_ref[rows, sl] = (d_y * mixed[g] * dgl_u[:, sl]).astype(BF16)
                d_mixed = d_y * gu[:, sl]
                dmix_acc[g] += d_mixed
                dws_ref[g] += _dot(d_mixed, vn[:, sl], NT) * mask_ref[...]
                dvn_parts.append(_dot(wmt_ref[g], d_mixed))
            dvn = jnp.concatenate(dvn_parts, axis=1)
            dgv_ref[...] += _colsum8(dvn * vh)
            dbv_ref[...] += _colsum8(dvn)
            d_gl = _ln_bwd(dvn * gvv, vh, rstd)
            dh_ref[rows, D_MODEL:] = (d_gl * dgl_v).astype(BF16)

        @pl.when(i == nb - 1)
        def _():
            rowid = lax.broadcasted_iota(jnp.int32, (8, 128), 0)
            ones = jnp.ones((8, 128), F32)
            acc = jnp.zeros((8, 128), F32)
            for g in range(N_GROUP):
                rs = _dot32(ones, dmix_acc[g], NT)
                acc = jnp.where(rowid == g, rs, acc)
            dbs_ref[...] = acc

    full3 = pl.BlockSpec((N_GROUP, 128, 128), lambda i: (0, 0, 0))
    vec = pl.BlockSpec((1, D_MODEL), lambda i: (0, 0))
    acc8 = pl.BlockSpec((8, D_MODEL), lambda i: (0, 0))
    return _grid1_call(
        body, comm, nb, name="sgu_bwd",
        out_shape=(jax.ShapeDtypeStruct(dh_buf.shape, BF16),
                   jax.ShapeDtypeStruct((N_GROUP, 128, 128), F32),
                   jax.ShapeDtypeStruct((8, 128), F32),
                   jax.ShapeDtypeStruct((8, D_MODEL), F32),
                   jax.ShapeDtypeStruct((8, D_MODEL), F32)),
        in_specs=[pl.BlockSpec((SGU_ROWS, D_MODEL), lambda i: (i, 0)),
                  pl.BlockSpec((SGU_ROWS, D_MODEL), lambda i: (i, 1)),
                  pl.BlockSpec((SGU_ROWS, D_MODEL), lambda i: (i, 0)),
                  full3, full3, full3, vec, vec,
                  pl.BlockSpec((128, 128), lambda i: (0, 0)), ANY],
        out_specs=(pl.BlockSpec((None, SGU_ROWS, 2 * D_MODEL), lambda i: (DH_SLOT[0], i, 0)),
                   full3, pl.BlockSpec((8, 128), lambda i: (0, 0)), acc8, acc8),
        scratch=[pltpu.VMEM((N_GROUP, 128, 128), F32)],
        args=(h, h, dya, wm, wmt, bsb, gv, bv, maskf, dh_buf), aliases={9: 0})


def _tri_masks():
    row = lax.broadcasted_iota(jnp.int32, (CHUNK, CHUNK), 0)
    col = lax.broadcasted_iota(jnp.int32, (CHUNK, CHUNK), 1)
    return col <= row, col >= row


def _heads(v):
    return [v[:, hd * HEAD_DIM:(hd + 1) * HEAD_DIM] for hd in range(N_HEAD)]


def _tri_cumsum(tri_bf, v):
    hi = v.astype(BF16)
    r = v - hi.astype(F32)
    mid = r.astype(BF16)
    lo = (r - mid.astype(F32)).astype(BF16)
    return _dot(tri_bf, hi) + _dot(tri_bf, mid) + _dot(tri_bf, lo)


def _hgrn_chunk(q, fp, ii, lb, st_heads, causal, with_o=True):
    sg = _sig(fp)
    f = lb + (1.0 - lb) * sg
    k = 1.0 - f
    c = _tri_cumsum(causal.astype(BF16), jnp.log(f))
    ec = jnp.exp(c)
    en = jnp.exp(-c)
    sq = _sig(q)
    qt = q * sq * ec
    kt = k * en
    ecl = jnp.exp(c[CHUNK - 1:CHUNK, :])
    kk = kt * ecl
    qtb, ktb, iib, kkb = qt.astype(BF16), kt.astype(BF16), ii.astype(BF16), kk.astype(BF16)
    attn, o = [], []
    for hd, (qh, kh, ih) in enumerate(zip(_heads(qtb), _heads(ktb), _heads(iib))):
        a = jnp.where(causal, _dot(qh, kh, NT), 0.0).astype(BF16)
        attn.append(a)
        if with_o:
            o.append(_dot(a, ih) + _dot(qh, st_heads[hd], NT))
    return dict(sg=sg, f=f, k=k, ec=ec, en=en, sq=sq, ecl=ecl, kk=kk, qtb=qtb, ktb=ktb, iib=iib,
                kkb=kkb, attn=attn, o=o)


def _rms_heads(o_heads):
    rinv = [lax.rsqrt(jnp.mean(o * o, axis=-1, keepdims=True) + RMS_EPS) for o in o_heads]
    return rinv, jnp.concatenate([o * r for o, r in zip(o_heads, rinv)], axis=1)


HG_CHUNKS = 8
HG_ROWS = HG_CHUNKS * CHUNK


def _hgrn_fwd(h, logits, gn, comm=None):
    t = h.shape[0]
    nb = t // HG_ROWS

    def body(q_ref, f_ref, i_ref, og_ref, lg_ref, gn_ref, yb_ref, o_ref, st_ref, state):
        @pl.when(pl.program_id(0) == 0)
        def _():
            state[...] = jnp.zeros_like(state)

        causal, _ = _tri_masks()
        lb = _sig(lg_ref[0:1, :] - lg_ref[1:2, :])
        gnv = gn_ref[...]
        st = [state[hd] for hd in range(N_HEAD)]
        for cc in range(HG_CHUNKS):
            rows = slice(cc * CHUNK, (cc + 1) * CHUNK)
            og = og_ref[rows, :].astype(F32)
            r = _hgrn_chunk(q_ref[rows, :].astype(F32), f_ref[rows, :].astype(F32),
                            i_ref[rows, :].astype(F32), lb, [s.astype(BF16) for s in st], causal)
            o_bf = jnp.concatenate(r["o"], axis=1).astype(BF16)
            o_ref[rows, :] = o_bf
            _, on = _rms_heads(_heads(o_bf.astype(F32)))
            yb_ref[rows, :] = (on * gnv * (og * _sig(og))).astype(BF16)
            for hd in range(N_HEAD):
                st_ref[cc, hd] = st[hd]
            st = [s * e + _dot(ih, kh, TN)
                  for s, e, ih, kh in zip(st, _heads(r["ecl"]), _heads(r["iib"]), _heads(r["kkb"]))]
        for hd in range(N_HEAD):
            state[hd] = st[hd]

    def col(k):
        return pl.BlockSpec((HG_ROWS, D_MODEL), lambda ci: (ci, k))

    return _grid1_call(body, comm, nb, name="hgrn_fwd",
                       out_shape=(jax.ShapeDtypeStruct((t, D_MODEL), BF16),
                                  jax.ShapeDtypeStruct((t, D_MODEL), BF16),
                                  jax.ShapeDtypeStruct((t // CHUNK, N_HEAD, HEAD_DIM, HEAD_DIM), F32)),
                       in_specs=[col(2), col(3), col(4), col(5),
                                 pl.BlockSpec((2, D_MODEL), lambda ci: (0, 0)),
                                 pl.BlockSpec((1, D_MODEL), lambda ci: (0, 0))],
                       out_specs=(pl.BlockSpec((HG_ROWS, D_MODEL), lambda ci: (ci, 0)),
                                  pl.BlockSpec((HG_ROWS, D_MODEL), lambda ci: (ci, 0)),
                                  pl.BlockSpec((HG_CHUNKS, N_HEAD, HEAD_DIM, HEAD_DIM),
                                               lambda ci: (ci, 0, 0, 0))),
                       scratch=[pltpu.VMEM((N_HEAD, HEAD_DIM, HEAD_DIM), F32)],
                       args=(h, h, h, h, logits, gn))


def _hgrn_chunk_bwd(q, fp, ii, og, o_saved, dy, gnv, lb, st, dsn, causal, anti):
    stb = [s.astype(BF16) for s in st]
    dsnb = [s.astype(BF16) for s in dsn]
    r = _hgrn_chunk(q, fp, ii, lb, stb, causal, with_o=False)
    rinv, on = _rms_heads(_heads(o_saved))
    so = _sig(og)
    sil = og * so
    d_og = dy * on * gnv * (so * (1.0 + og * (1.0 - so)))
    d_on = dy * gnv * sil
    d_ob = jnp.concatenate(
        [ri * (dn - oh * jnp.mean(dn * oh, axis=-1, keepdims=True))
         for ri, dn, oh in zip(rinv, _heads(d_on), _heads(on))], axis=1).astype(BF16)
    d_i, d_qt, d_kt, d_kk, d_st, st_dsn = [], [], [], [], [], []
    ecl = _heads(r["ecl"])
    for hd, (dh, qh, kh, ih, kkh) in enumerate(zip(_heads(d_ob), _heads(r["qtb"]), _heads(r["ktb"]),
                                                   _heads(r["iib"]), _heads(r["kkb"]))):
        d_attn = jnp.where(causal, _dot(dh, ih, NT), 0.0).astype(BF16)
        d_i.append(_dot(r["attn"][hd], dh, TN) + _dot(kkh, dsnb[hd], NT))
        d_qt.append(_dot(d_attn, kh) + _dot(dh, stb[hd]))
        d_kt.append(_dot(d_attn, qh, TN))
        d_kk.append(_dot(ih, dsnb[hd]))
        d_st.append(_dot(dh, qh, TN) + dsn[hd] * ecl[hd])
        st_dsn.append(jnp.sum(st[hd] * dsn[hd], axis=0, keepdims=True))
    d_qt = jnp.concatenate(d_qt, axis=1)
    d_kt = jnp.concatenate(d_kt, axis=1)
    d_kk = jnp.concatenate(d_kk, axis=1)
    kk = r["kk"]
    d_cl = r["ecl"] * jnp.concatenate(st_dsn, axis=1) + jnp.sum(kk * d_kk, axis=0, keepdims=True)
    d_k = (d_kk * r["ecl"] + d_kt) * r["en"]
    d_c = d_qt * r["qtb"].astype(F32) - d_kt * r["ktb"].astype(F32) - d_kk * kk
    rowid = lax.broadcasted_iota(jnp.int32, (CHUNK, D_MODEL), 0)
    d_c = d_c + jnp.where(rowid == CHUNK - 1, d_cl, 0.0)
    d_lf = _tri_cumsum(anti.astype(BF16), d_c)
    d_f = d_lf / r["f"] - d_k
    sg, sq = r["sg"], r["sq"]
    d_q = d_qt * r["ec"] * (sq * (1.0 + q * (1.0 - sq)))
    d_fp = d_f * (1.0 - lb) * sg * (1.0 - sg)
    return (d_q, d_fp, jnp.concatenate(d_i, axis=1), d_og, d_st,
            _colsum8(dy * on * sil), _colsum8(d_f * (1.0 - sg)))


def _hgrn_bwd(h, o_all, dyb, st_all, logits, gn, dh_buf, comm=None):
    t = h.shape[0]
    nb = t // HG_ROWS

    def body(q_ref, f_ref, i_ref, og_ref, o_ref, dyb_ref, st_ref, lg_ref, gn_ref, dh_buf_ref,
             dh_ref, dlb_ref, dgn_ref, dstate):
        @pl.when(pl.program_id(0) == 0)
        def _():
            dstate[...] = jnp.zeros_like(dstate)
            dlb_ref[...] = jnp.zeros_like(dlb_ref)
            dgn_ref[...] = jnp.zeros_like(dgn_ref)

        causal, anti = _tri_masks()
        lb = _sig(lg_ref[0:1, :] - lg_ref[1:2, :])
        gnv = gn_ref[...]
        dsn = [dstate[hd] for hd in range(N_HEAD)]
        dgn_acc = jnp.zeros((8, D_MODEL), F32)
        dlb_acc = jnp.zeros((8, D_MODEL), F32)
        for cc in reversed(range(HG_CHUNKS)):
            rows = slice(cc * CHUNK, (cc + 1) * CHUNK)
            d_q, d_fp, d_i, d_og, dsn, dgn_c, dlb_c = _hgrn_chunk_bwd(
                q_ref[rows, :].astype(F32), f_ref[rows, :].astype(F32), i_ref[rows, :].astype(F32),
                og_ref[rows, :].astype(F32), o_ref[rows, :].astype(F32), dyb_ref[rows, :].astype(F32), gnv, lb,
                [st_ref[cc, hd] for hd in range(N_HEAD)], dsn, causal, anti)
            dgn_acc = dgn_acc + dgn_c
            dlb_acc = dlb_acc + dlb_c
            dh_ref[0, rows, :D_MODEL] = d_q.astype(BF16)
            dh_ref[0, rows, D_MODEL:] = d_fp.astype(BF16)
            dh_ref[1, rows, :D_MODEL] = d_i.astype(BF16)
            dh_ref[1, rows, D_MODEL:] = d_og.astype(BF16)
        dgn_ref[...] += dgn_acc
        dlb_ref[...] += dlb_acc
        for hd in range(N_HEAD):
            dstate[hd] = dsn[hd]

    def col(k):
        return pl.BlockSpec((HG_ROWS, D_MODEL), lambda ci: (nb - 1 - ci, k))

    acc8 = pl.BlockSpec((8, D_MODEL), lambda ci: (0, 0))
    pair = pl.BlockSpec((2, HG_ROWS, 2 * D_MODEL), lambda ci: (0, nb - 1 - ci, 0))
    return _grid1_call(body, comm, nb, name="hgrn_bwd",
                       out_shape=(jax.ShapeDtypeStruct(dh_buf.shape, BF16),
                                  jax.ShapeDtypeStruct((8, D_MODEL), F32),
                                  jax.ShapeDtypeStruct((8, D_MODEL), F32)),
                       in_specs=[col(2), col(3), col(4), col(5), col(0),
                                 pl.BlockSpec((HG_ROWS, D_MODEL), lambda ci: (nb - 1 - ci, 0)),
                                 pl.BlockSpec((HG_CHUNKS, N_HEAD, HEAD_DIM, HEAD_DIM),
                                              lambda ci: (nb - 1 - ci, 0, 0, 0)),
                                 pl.BlockSpec((2, D_MODEL), lambda ci: (0, 0)),
                                 pl.BlockSpec((1, D_MODEL), lambda ci: (0, 0)), ANY],
                       out_specs=(pair, acc8, acc8),
                       scratch=[pltpu.VMEM((N_HEAD, HEAD_DIM, HEAD_DIM), F32)],
                       args=(h, h, h, h, o_all, dyb, st_all, logits, gn, dh_buf), aliases={9: 0})


def _mix_fwd(ya, yb, h, x, wb0, wb1, wo, g1, b1, tm, comm=None):
    t = x.shape[0]

    def body(ya_ref, yb_ref, ga_ref, gb_ref, x_ref, wb0_ref, wb1_ref, wo_ref, g1_ref, b1_ref,
             r1_ref, a_ref, b_ref, m_ref, x1_ref):
        a = _dot(ya_ref[...], wb0_ref[...])
        b = _dot(yb_ref[...], wb1_ref[...])
        m = _sig(ga_ref[...].astype(F32)) * a + _sig(gb_ref[...].astype(F32)) * b
        r1 = ALPHA * x_ref[...] + _dot(m, wo_ref[...])
        xh, _ = _ln_stats(r1)
        r1_ref[...] = r1
        a_ref[...] = a.astype(BF16)
        b_ref[...] = b.astype(BF16)
        m_ref[...] = m.astype(BF16)
        x1_ref[...] = (xh * g1_ref[...] + b1_ref[...]).astype(BF16)

    tile = pl.BlockSpec((tm, D_MODEL), lambda i: (i, 0))
    wsp = pl.BlockSpec((D_MODEL, D_MODEL), lambda i: (0, 0))
    vec = pl.BlockSpec((1, D_MODEL), lambda i: (0, 0))
    f32o = jax.ShapeDtypeStruct((t, D_MODEL), F32)
    bfo = jax.ShapeDtypeStruct((t, D_MODEL), BF16)
    return _grid1_call(body, comm, t // tm, name="mix_fwd", out_shape=(f32o, bfo, bfo, bfo, bfo),
                       in_specs=[tile, tile,
                                 pl.BlockSpec((tm, D_MODEL), lambda i: (i, 6)),
                                 pl.BlockSpec((tm, D_MODEL), lambda i: (i, 7)),
                                 tile, wsp, wsp, wsp, vec, vec],
                       out_specs=(tile, tile, tile, tile, tile),
                       scratch=[], args=(ya, yb, h, h, x, wb0, wb1, wo, g1, b1))


def _mix_bwd(dr1, h, a, b, wo, wb0, wb1, tm):
    t = dr1.shape[0]

    def body(dr1_ref, ga_ref, gb_ref, a_ref, b_ref, wo_ref, wb0_ref, wb1_ref,
             da_ref, db_ref, dh3_ref, dya_ref, dyb_ref):
        d_m = _dot(dr1_ref[...], wo_ref[...], NT)
        sa = _sig(ga_ref[...].astype(F32))
        sb = _sig(gb_ref[...].astype(F32))
        d_a = (d_m * sa).astype(BF16)
        d_b = (d_m * sb).astype(BF16)
        da_ref[...] = d_a
        db_ref[...] = d_b
        dh3_ref[:, :D_MODEL] = (d_m * a_ref[...].astype(F32) * sa * (1.0 - sa)).astype(BF16)
        dh3_ref[:, D_MODEL:] = (d_m * b_ref[...].astype(F32) * sb * (1.0 - sb)).astype(BF16)
        dya_ref[...] = _dot(d_a, wb0_ref[...], NT).astype(BF16)
        dyb_ref[...] = _dot(d_b, wb1_ref[...], NT).astype(BF16)

    tile = pl.BlockSpec((tm, D_MODEL), lambda i: (i, 0))
    wsp = pl.BlockSpec((D_MODEL, D_MODEL), lambda i: (0, 0))
    f32o = jax.ShapeDtypeStruct((t, D_MODEL), F32)
    bfo = jax.ShapeDtypeStruct((t, D_MODEL), BF16)
    return _pc(body, name="mix_bwd",
               out_shape=(bfo, bfo, jax.ShapeDtypeStruct((N_CHIP, t, 2 * D_MODEL), BF16), bfo, bfo),
               grid=(t // tm,),
               in_specs=[tile,
                         pl.BlockSpec((tm, D_MODEL), lambda i: (i, 6)),
                         pl.BlockSpec((tm, D_MODEL), lambda i: (i, 7)),
                         tile, tile, wsp, wsp, wsp],
               out_specs=(tile, tile, pl.BlockSpec((None, tm, 2 * D_MODEL), lambda i: (DH_SLOT[3], i, 0)),
                          tile, tile),
               sem=("parallel",))(dr1, h, h, a, b, wo, wb0, wb1)


FF_TILE = 1408
FF_NJ = D_FF // FF_TILE


def _shift_down(v, k):
    return pltpu.roll(v, k, 0)


def _shift_up(v, k):
    return pltpu.roll(v, v.shape[0] - k, 0)


HALO = 16
FF_PIECES = ((0, 768), (768, FF_TILE))


def _ffn_up_act(x1b, wup_st, convw, convb, tm, comm):
    t = x1b.shape[0]
    ni = t // tm
    nth = tm // HALO

    def body(x_ref, xp_ref, wg_ref, wv_ref, cw_ref, cb_ref, h2_ref, act_ref):
        wg = wg_ref[...]
        gate = _dot(x_ref[...], wg).astype(BF16)
        val = _dot(x_ref[...], wv_ref[...]).astype(BF16)
        prev = (_dot(xp_ref[...], wg) * (pl.program_id(0) > 0).astype(F32)).astype(BF16)
        h2_ref[0] = gate
        h2_ref[1] = val
        ext = jnp.concatenate([prev.astype(F32), gate.astype(F32)], axis=0)
        gc = (cw_ref[0:1, :] * _shift_down(ext, 2) + cw_ref[1:2, :] * _shift_down(ext, 1)
              + cw_ref[2:3, :] * ext + cb_ref[...])[HALO:, :].astype(BF16)
        h2_ref[2] = gc
        act_ref[...] = (_gelu(gc.astype(F32)) * val.astype(F32)).astype(BF16)

    res = _hosted_call(
        body, comm,
        lambda: (pl.program_id(0) == 0) & (pl.program_id(1) == 0),
        lambda: (pl.program_id(0) == ni - 1) & (pl.program_id(1) == FF_NJ - 1),
        mid=lambda: (pl.program_id(0) == max(ni - 2, 0)) & (pl.program_id(1) == 0),
        name="ffn_up",
        out_shape=(jax.ShapeDtypeStruct((3, t, D_FF), BF16), jax.ShapeDtypeStruct((t, D_FF), BF16)),
        grid=(ni, FF_NJ),
        in_specs=[pl.BlockSpec((tm, D_MODEL), lambda i, j: (i, 0)),
                  pl.BlockSpec((HALO, D_MODEL), lambda i, j: (jnp.maximum(i * nth - 1, 0), 0)),
                  pl.BlockSpec((None, D_MODEL, FF_TILE), lambda i, j: (j, 0, 0)),
                  pl.BlockSpec((None, D_MODEL, FF_TILE), lambda i, j: (j + FF_NJ, 0, 0)),
                  pl.BlockSpec((3, FF_TILE), lambda i, j: (0, j)),
                  pl.BlockSpec((1, FF_TILE), lambda i, j: (0, j))],
        out_specs=(pl.BlockSpec((3, tm, FF_TILE), lambda i, j: (0, i, j)),
                   pl.BlockSpec((tm, FF_TILE), lambda i, j: (i, j))),
        scratch=[], sem=("arbitrary", "arbitrary"),
        args=(x1b, x1b, wup_st, wup_st, convw, convb))
    return res[0], res[1], res[2:]


def _out_fwd_bwd(act, x1b, r1, p2, tgt, wd, wpg, wpp, g1, b1, g2, b2, tm):
    t = r1.shape[0]

    def body(act_ref, x1b_ref, r1_ref, p_ref, tgt_ref, wd_ref, wpg_ref, wpp_ref,
             g1_ref, b1_ref, g2_ref, b2_ref,
             dr2_ref, dpg_ref, dpp_ref, loss_ref, dg2_ref, db2_ref):
        i = pl.program_id(0)

        @pl.when(i == 0)
        def _():
            loss_ref[...] = jnp.zeros_like(loss_ref)
            dg2_ref[...] = jnp.zeros_like(dg2_ref)
            db2_ref[...] = jnp.zeros_like(db2_ref)

        ffn = _dot(act_ref[...], wd_ref[...])
        pg = _dot(x1b_ref[...], wpg_ref[...])
        pp = _dot(p_ref[...], wpp_ref[...])
        s = _sig(pg)
        xh1, _ = _ln_stats(r1_ref[...])
        x1 = xh1 * g1_ref[...] + b1_ref[...]
        r2 = ALPHA * x1 + ffn + s * pp
        xh2, rstd2 = _ln_stats(r2)
        g2v = g2_ref[...]
        diff = xh2 * g2v + b2_ref[...] - tgt_ref[...]
        part = jnp.sum(jnp.sum(diff * diff, axis=1, keepdims=True), axis=0, keepdims=True)
        loss_ref[...] += jnp.broadcast_to(part * (0.5 / D_MODEL), loss_ref.shape)
        dy = diff * (1.0 / D_MODEL)
        dg2_ref[...] += _colsum8(dy * xh2)
        db2_ref[...] += _colsum8(dy)
        dr2 = _ln_bwd(dy * g2v, xh2, rstd2)
        dr2_ref[...] = dr2
        dpg_ref[...] = (dr2 * pp * s * (1.0 - s)).astype(BF16)
        dpp_ref[...] = (dr2 * s).astype(BF16)

    tile = pl.BlockSpec((tm, D_MODEL), lambda i: (i, 0))
    vec = pl.BlockSpec((1, D_MODEL), lambda i: (0, 0))
    acc8 = pl.BlockSpec((8, D_MODEL), lambda i: (0, 0))
    acc_shape = jax.ShapeDtypeStruct((8, D_MODEL), F32)
    return _pc(body, name="out_fwd_bwd",
               out_shape=(jax.ShapeDtypeStruct((t, D_MODEL), F32),
                          jax.ShapeDtypeStruct((t, D_MODEL), BF16),
                          jax.ShapeDtypeStruct((t, D_MODEL), BF16),
                          acc_shape, acc_shape, acc_shape),
               grid=(t // tm,),
               in_specs=[pl.BlockSpec((tm, D_FF), lambda i: (i, 0)), tile, tile,
                         pl.BlockSpec((tm, PLE_DIM), lambda i: (i, 0)), tile,
                         pl.BlockSpec((D_FF, D_MODEL), lambda i: (0, 0)),
                         pl.BlockSpec((D_MODEL, D_MODEL), lambda i: (0, 0)),
                         pl.BlockSpec((PLE_DIM, D_MODEL), lambda i: (0, 0)),
                         vec, vec, vec, vec],
               out_specs=(tile, tile, tile, acc8, acc8, acc8),
               sem=("arbitrary",))(act, x1b, r1, p2, tgt, wd, wpg, wpp, g1, b1, g2, b2)


def _ffn_bwd(h2, dr2, wd, wup_st, dpg, wpg, r1, g1, convw, tm):
    t = r1.shape[0]
    ni = t // tm
    nth = tm // HALO
    last_halo = t // HALO - 1
    main_rows = slice(0, tm)

    def body(g_ref, gc_ref, gcn_ref, v_ref, vn_ref, dr2_ref, dr2n_ref, wd_ref, wug_ref, wuv_ref,
             cw_ref, dpg_ref, wpg_ref, r1_ref, g1_ref,
             dh2_ref, dr1_ref, dcw_ref, dcb_ref, dg1_ref, db1_ref, acc):
        i = pl.program_id(0)
        j = pl.program_id(1)

        @pl.when((i == 0) & (j == 0))
        def _():
            dcw_ref[...] = jnp.zeros_like(dcw_ref)
            dcb_ref[...] = jnp.zeros_like(dcb_ref)
            dg1_ref[...] = jnp.zeros_like(dg1_ref)
            db1_ref[...] = jnp.zeros_like(db1_ref)

        dr2v = dr2_ref[...].astype(BF16)
        dr2n = dr2n_ref[...].astype(BF16)
        more = (i < ni - 1).astype(F32)
        prod = None
        dcw_parts, dcb_parts = [], []
        for c0, c1 in FF_PIECES:
            pc = slice(c0, c1)
            da = _dot(dr2v, wd_ref[pc, :], NT)
            dnext = _dot(dr2n, wd_ref[pc, :], NT) * more
            gc = jnp.concatenate([gc_ref[:, pc].astype(F32), gcn_ref[:, pc].astype(F32)], axis=0)
            vext = jnp.concatenate([v_ref[:, pc].astype(F32), vn_ref[:, pc].astype(F32)], axis=0)
            dext = jnp.concatenate([da, dnext], axis=0)
            gl, dgl = _gelu_and_grad(gc)
            d_gc = dext * vext * dgl
            up1 = _shift_up(d_gc, 1)[main_rows, :]
            up2 = _shift_up(d_gc, 2)[main_rows, :]
            dm = d_gc[main_rows, :]
            d_gate = (cw_ref[2:3, pc] * dm + cw_ref[1:2, pc] * up1 + cw_ref[0:1, pc] * up2).astype(BF16)
            d_val = (da * gl[main_rows, :]).astype(BF16)
            dh2_ref[0, :, pc] = d_gate
            dh2_ref[1, :, pc] = d_val
            g = g_ref[:, pc].astype(F32)
            s0 = jnp.sum(g * up2, axis=0, keepdims=True)
            s1 = jnp.sum(g * up1, axis=0, keepdims=True)
            s2 = jnp.sum(g * dm, axis=0, keepdims=True)
            rowid = lax.broadcasted_iota(jnp.int32, (8, c1 - c0), 0)
            dcw_parts.append(jnp.where(rowid == 0, s0, jnp.where(rowid == 1, s1,
                                                                 jnp.where(rowid == 2, s2, 0.0))))
            dcb_parts.append(_colsum8(dm))
            part = _dot(d_gate, wug_ref[:, pc], NT) + _dot(d_val, wuv_ref[:, pc], NT)
            prod = part if prod is None else prod + part
        dcw_part = jnp.concatenate(dcw_parts, axis=1)
        dcb_part = jnp.concatenate(dcb_parts, axis=1)
        for jj in range(FF_NJ):
            @pl.when(j == jj)
            def _(jj=jj):
                cols = slice(jj * FF_TILE, (jj + 1) * FF_TILE)
                dcw_ref[:, cols] += dcw_part
                dcb_ref[:, cols] += dcb_part

        @pl.when(j == 0)
        def _():
            acc[...] = prod

        @pl.when(j > 0)
        def _():
            acc[...] += prod

        @pl.when(j == FF_NJ - 1)
        def _():
            d_x1 = acc[...] + _dot(dpg_ref[...], wpg_ref[...], NT) + ALPHA * dr2_ref[...]
            xh, rstd = _ln_stats(r1_ref[...])
            dg1_ref[...] += _colsum8(d_x1 * xh)
            db1_ref[...] += _colsum8(d_x1)
            dr1_ref[...] = _ln_bwd(d_x1 * g1_ref[...], xh, rstd)

    def h2_main(part):
        return pl.BlockSpec((None, tm, FF_TILE), lambda i, j: (part, i, j))

    def h2_next(part):
        return pl.BlockSpec((None, HALO, FF_TILE),
                            lambda i, j: (part, jnp.minimum((i + 1) * nth, last_halo), j))

    tile = pl.BlockSpec((tm, D_MODEL), lambda i, j: (i, 0))
    acc8 = pl.BlockSpec((8, D_MODEL), lambda i, j: (0, 0))
    accff = pl.BlockSpec((8, D_FF), lambda i, j: (0, 0))
    acc_shape = jax.ShapeDtypeStruct((8, D_MODEL), F32)
    accff_shape = jax.ShapeDtypeStruct((8, D_FF), F32)
    return _pc(body, name="ffn_bwd",
               out_shape=(jax.ShapeDtypeStruct((2, t, D_FF), BF16),
                          jax.ShapeDtypeStruct((t, D_MODEL), F32),
                          accff_shape, accff_shape, acc_shape, acc_shape),
               grid=(ni, FF_NJ),
               in_specs=[h2_main(0), h2_main(2), h2_next(2), h2_main(1), h2_next(1),
                         tile,
                         pl.BlockSpec((HALO, D_MODEL), lambda i, j: (jnp.minimum((i + 1) * nth, last_halo), 0)),
                         pl.BlockSpec((FF_TILE, D_MODEL), lambda i, j: (j, 0)),
                         pl.BlockSpec((None, D_MODEL, FF_TILE), lambda i, j: (j, 0, 0)),
                         pl.BlockSpec((None, D_MODEL, FF_TILE), lambda i, j: (j + FF_NJ, 0, 0)),
                         pl.BlockSpec((3, FF_TILE), lambda i, j: (0, j)),
                         tile, pl.BlockSpec((D_MODEL, D_MODEL), lambda i, j: (0, 0)),
                         tile, pl.BlockSpec((1, D_MODEL), lambda i, j: (0, 0))],
               out_specs=(pl.BlockSpec((2, tm, FF_TILE), lambda i, j: (0, i, j)),
                          tile, accff, accff, acc8, acc8),
               scratch=[pltpu.VMEM((tm, D_MODEL), F32)],
               sem=("arbitrary", "arbitrary"))(h2, h2, h2, h2, h2, dr2, dr2, wd, wup_st, wup_st,
                                               convw, dpg, wpg, r1, g1)


ANY = pl.BlockSpec(memory_space=pl.ANY)


def _chip_peers():
    x, y, c = lax.axis_index("x"), lax.axis_index("y"), lax.axis_index("c")
    return x, y, c, [(1 - x, y), (x, 1 - y), (1 - x, 1 - y)]


def _gather_comm(halved, whole=()):
    n, nw = len(halved), len(whole)

    def copies(ins, outs, sems):
        ici_send, ici_recv, d2d_send, d2d_recv, own_send, own_recv = sems
        x, y, c, peers = _chip_peers()
        me = 2 * x + y
        sibling = (x, y, 1 - c)
        own, ici, ici_wait, fwd, fwd_wait = [], [], [], [], []
        for ti in range(n + nw):
            src, dst = ins[ti], outs[ti]
            own.append(pltpu.make_async_remote_copy(
                src_ref=src, dst_ref=dst.at[me], send_sem=own_send.at[ti], recv_sem=own_recv.at[ti],
                device_id=sibling, device_id_type=MESH))
            for k, (px, py) in enumerate(peers):
                pk = 2 * px + py
                sem = dict(send_sem=ici_send.at[ti * 3 + k], recv_sem=ici_recv.at[ti * 3 + k],
                           device_id=(px, py, c), device_id_type=MESH)
                if ti < n:
                    ici.append(pltpu.make_async_remote_copy(src_ref=src.at[c], dst_ref=dst.at[me, c], **sem))
                    ici_wait.append(pltpu.make_async_remote_copy(src_ref=src.at[c], dst_ref=dst.at[pk, c], **sem))
                    dsem = dict(send_sem=d2d_send.at[ti * 3 + k], recv_sem=d2d_recv.at[ti * 3 + k],
                                device_id=sibling, device_id_type=MESH)
                    fwd.append(pltpu.make_async_remote_copy(src_ref=dst.at[pk, c], dst_ref=dst.at[pk, c], **dsem))
                    fwd_wait.append(pltpu.make_async_remote_copy(
                        src_ref=dst.at[pk, 1 - c], dst_ref=dst.at[pk, 1 - c], **dsem))
                else:
                    ici.append(pltpu.make_async_remote_copy(src_ref=src, dst_ref=dst.at[me], **sem))
                    ici_wait.append(pltpu.make_async_remote_copy(src_ref=src, dst_ref=dst.at[pk], **sem))
        return own, ici, ici_wait, fwd, fwd_wait

    def start(ins, outs, sems):
        own, ici, _, _, _ = copies(ins, outs, sems)
        for cp in own + ici:
            cp.start()

    def finish(ins, outs, sems):
        own, ici, ici_wait, fwd, fwd_wait = copies(ins, outs, sems)
        for i, cp in enumerate(ici_wait):
            cp.wait_recv()
            if i < len(fwd):
                fwd[i].start()
        for cp in fwd_wait + own:
            cp.wait_recv()
        for cp in own + ici + fwd:
            cp.wait_send()

    def middle(ins, outs, sems):
        _, _, ici_wait, fwd, _ = copies(ins, outs, sems)
        for i, cp in enumerate(ici_wait):
            cp.wait_recv()
            if i < len(fwd):
                fwd[i].start()

    def finish_late(ins, outs, sems):
        own, ici, _, fwd, fwd_wait = copies(ins, outs, sems)
        for cp in fwd_wait + own:
            cp.wait_recv()
        for cp in own + ici + fwd:
            cp.wait_send()

    srcs = list(halved) + list(whole)
    return _Comm(srcs, [jax.ShapeDtypeStruct((N_CHIP,) + s.shape, s.dtype) for s in srcs],
                 [pltpu.SemaphoreType.DMA((3 * (n + nw),)), pltpu.SemaphoreType.DMA((3 * (n + nw),)),
                  pltpu.SemaphoreType.DMA((max(3 * n, 1),)), pltpu.SemaphoreType.DMA((max(3 * n, 1),)),
                  pltpu.SemaphoreType.DMA((n + nw,)), pltpu.SemaphoreType.DMA((n + nw,))],
                 start, finish, middle, finish_late)


def _sibling_exchange_comm(grads):
    n = len(grads)

    def copies(ins, outs, sems):
        send_sems, recv_sems = sems
        x, y, c = lax.axis_index("x"), lax.axis_index("y"), lax.axis_index("c")
        res = []
        for ti in range(n):
            half = ins[ti].shape[1] // 2
            res.append(pltpu.make_async_remote_copy(
                src_ref=ins[ti].at[:, pl.ds(pl.multiple_of((1 - c) * half, 16), half), :],
                dst_ref=outs[ti],
                send_sem=send_sems.at[ti], recv_sem=recv_sems.at[ti],
                device_id=(x, y, 1 - c), device_id_type=MESH))
        return res

    def start(ins, outs, sems):
        for cp in copies(ins, outs, sems):
            cp.start()

    def finish(ins, outs, sems):
        for cp in copies(ins, outs, sems):
            cp.wait()

    return _Comm(grads, [jax.ShapeDtypeStruct((N_CHIP, g.shape[1] // 2, g.shape[2]), g.dtype) for g in grads],
                 [pltpu.SemaphoreType.DMA((n,)), pltpu.SemaphoreType.DMA((n,))], start, finish)


def _in_proj_gathering(x2b, own, chip, tm, comm):
    t = x2b.shape[0]
    ni = t // tm
    half, cols = own.shape[1], own.shape[2]
    nci, nco = len(comm.ins), len(comm.out_shapes)

    def body(chip_ref, x_ref, own_ref, own_hbm, *rest):
        c_in = rest[:nci]
        h_ref, win_out = rest[nci:nci + 2]
        c_out = rest[nci + 2:nci + 2 + nco]
        w_scr, ici_send, ici_recv, d2d_send, d2d_recv, own_sems, ld_sems = rest[nci + 2 + nco:nci + 9 + nco]
        c_sem = rest[nci + 9 + nco:]
        s, i = pl.program_id(0), pl.program_id(1)
        x, y, c, peers = _chip_peers()
        me = 2 * x + y
        sibling = (x, y, 1 - c)

        def ici(k, slot):
            px, py = peers[k]
            return pltpu.make_async_remote_copy(
                src_ref=own_hbm.at[c], dst_ref=win_out.at[slot, c],
                send_sem=ici_send.at[k], recv_sem=ici_recv.at[k],
                device_id=(px, py, c), device_id_type=MESH)

        def forward(k, core):
            pk = 2 * peers[k][0] + peers[k][1]
            return pltpu.make_async_remote_copy(
                src_ref=win_out.at[pk, core], dst_ref=win_out.at[pk, core],
                send_sem=d2d_send.at[k], recv_sem=d2d_recv.at[k],
                device_id=sibling, device_id_type=MESH)

        place_own = pltpu.make_async_remote_copy(
            src_ref=own_hbm, dst_ref=win_out.at[me], send_sem=own_sems.at[0], recv_sem=own_sems.at[1],
            device_id=sibling, device_id_type=MESH)

        @pl.when((s == 0) & (i == 0))
        def _():
            for k in range(2):
                ici(k, me).start()
            place_own.start()

        @pl.when(s == 0)
        def _():
            xv = x_ref[...]
            h_ref[...] = (_dot(xv[:, :half], own_ref[0]) + _dot(xv[:, half:], own_ref[1])).astype(BF16)

        for k in range(3):
            @pl.when((s == k + 1) & (i == 0))
            def _(k=k):
                pk = 2 * peers[k][0] + peers[k][1]
                ici(k, pk).wait_recv()
                if k == 0:
                    ici(2, me).start()
                forward(k, c).start()
                forward(k, 1 - c).wait_recv()
                loads = [pltpu.make_async_copy(win_out.at[pk, hh], w_scr.at[hh], ld_sems.at[hh])
                         for hh in range(2)]
                for ld in loads:
                    ld.start()
                for ld in loads:
                    ld.wait()
                if k == 1:
                    comm.start(c_in, c_out, c_sem)

        @pl.when(s > 0)
        def _():
            xv = x_ref[...]
            h_ref[...] = (_dot(xv[:, :half], w_scr[0]) + _dot(xv[:, half:], w_scr[1])).astype(BF16)

        @pl.when((s == N_CHIP - 1) & (i == ni - 1))
        def _():
            place_own.wait()
            for k in range(3):
                ici(k, me).wait_send()
                forward(k, c).wait_send()
            comm.finish(c_in, c_out, c_sem)

    def shard_col(s, me):
        return jnp.where(s == 0, me, me ^ jnp.where(s == 1, 2, jnp.where(s == 2, 1, 3)))

    res = _pc(body, name="in_proj",
              out_shape=(jax.ShapeDtypeStruct((t, N_CHIP * cols), BF16),
                         jax.ShapeDtypeStruct((N_CHIP,) + own.shape, own.dtype)) + tuple(comm.out_shapes),
              grid=(N_CHIP, ni), nsp=1,
              in_specs=[pl.BlockSpec((tm, 2 * half), lambda s, i, chip_ref: (i, 0)),
                        pl.BlockSpec(own.shape, lambda s, i, chip_ref: (0, 0, 0)),
                        ANY] + [ANY] * nci,
              out_specs=(pl.BlockSpec((tm, cols), lambda s, i, chip_ref: (i, shard_col(s, chip_ref[0]))),
                         ANY) + tuple([ANY] * nco),
              scratch=[pltpu.VMEM(own.shape, own.dtype),
                       pltpu.SemaphoreType.DMA((3,)), pltpu.SemaphoreType.DMA((3,)),
                       pltpu.SemaphoreType.DMA((3,)), pltpu.SemaphoreType.DMA((3,)),
                       pltpu.SemaphoreType.DMA((2,)), pltpu.SemaphoreType.DMA((2,))] + comm.sems,
              sem=("arbitrary", "arbitrary"))(chip, x2b, own, own, *comm.ins)
    return res[0], res[1], res[2:]


def _rs_add_halves(name, grad, recv, core):
    _, r, cdim = grad.shape
    half = r // 2
    tr = _row_tile(half, cdim, mult=16)
    nr = half // tr

    def body(c_ref, g_ref, r_ref, o_ref):
        o_ref[...] = (g_ref[...].astype(F32) + r_ref[...].astype(F32)).astype(BF16)

    return _pc(body, name=name, out_shape=jax.ShapeDtypeStruct((N_CHIP, half, cdim), BF16),
               grid=(N_CHIP, nr), nsp=1,
               in_specs=[pl.BlockSpec((None, tr, cdim), lambda j, i, c_ref: (j, c_ref[0] * nr + i, 0)),
                         pl.BlockSpec((None, tr, cdim), lambda j, i, c_ref: (j, i, 0))],
               out_specs=pl.BlockSpec((None, tr, cdim), lambda j, i, c_ref: (j, i, 0)),
               sem=("parallel", "parallel"))(core, grad, recv)


def _chip_exchange_comm(parts):
    n = len(parts)

    def copies(ins, outs, sems):
        send_sems, recv_sems = sems
        x, y, c, peers = _chip_peers()
        return [pltpu.make_async_remote_copy(
            src_ref=ins[ti].at[2 * px + py], dst_ref=outs[ti].at[k],
            send_sem=send_sems.at[ti * 3 + k], recv_sem=recv_sems.at[ti * 3 + k],
            device_id=(px, py, c), device_id_type=MESH)
            for ti in range(n) for k, (px, py) in enumerate(peers)]

    def start(ins, outs, sems):
        for cp in copies(ins, outs, sems):
            cp.start()

    def finish(ins, outs, sems):
        for cp in copies(ins, outs, sems):
            cp.wait()

    return _Comm(parts, [jax.ShapeDtypeStruct((3,) + p.shape[1:], p.dtype) for p in parts],
                 [pltpu.SemaphoreType.DMA((3 * n,)), pltpu.SemaphoreType.DMA((3 * n,))], start, finish)


def _rs_sum_chips(name, part, recv, chip):
    _, half, cdim = recv.shape
    tr = _row_tile(half, cdim, mult=16)

    def body(chip_ref, p_ref, r_ref, o_ref):
        o_ref[...] = ((p_ref[...].astype(F32) + r_ref[0].astype(F32)) + r_ref[1].astype(F32)
                      ) + r_ref[2].astype(F32)

    return _pc(body, name=name, out_shape=jax.ShapeDtypeStruct((half, cdim), F32),
               grid=(half // tr,), nsp=1,
               in_specs=[pl.BlockSpec((None, tr, cdim), lambda i, chip_ref: (chip_ref[0], i, 0)),
                         pl.BlockSpec((3, tr, cdim), lambda i, chip_ref: (0, i, 0))],
               out_specs=pl.BlockSpec((tr, cdim), lambda i, chip_ref: (i, 0)),
               sem=("parallel",))(chip, part, recv)


def _rs_send_halves(halves):
    n = len(halves)

    def body(*refs):
        ins, outs = refs[:n], refs[n:2 * n]
        send_sems, recv_sems = refs[2 * n:]
        x, y, c = lax.axis_index("x"), lax.axis_index("y"), lax.axis_index("c")
        sends = []
        for ti in range(n):
            cp = pltpu.make_async_remote_copy(
                src_ref=ins[ti], dst_ref=outs[ti],
                send_sem=send_sems.at[ti], recv_sem=recv_sems.at[ti],
                device_id=(x, y, 1 - c), device_id_type=MESH)
            cp.start()
            sends.append(cp)
        for cp in sends:
            cp.wait()

    return _pc(body, name="rs_send_halves",
               out_shape=tuple(jax.ShapeDtypeStruct(hv.shape, hv.dtype) for hv in halves),
               in_specs=[ANY] * n, out_specs=tuple([ANY] * n),
               scratch=[pltpu.SemaphoreType.DMA((n,)), pltpu.SemaphoreType.DMA((n,))])(*halves)


def _adamw_rows(name, mine, theirs, w, m, v, core):
    half, cdim = mine.shape
    tr = _row_tile(half, cdim, budget=1 << 19)
    nrh = half // tr

    def body(c_ref, mine_ref, theirs_ref, w_ref, m_ref, v_ref, g_ref, d_ref, m2_ref, v2_ref):
        is_mine = (pl.program_id(0) // nrh) == c_ref[0]
        g = jnp.where(is_mine, mine_ref[...], theirs_ref[...])
        d, m2, v2 = _adamw(w_ref[...], g, m_ref[...], v_ref[...])
        g_ref[...] = g
        d_ref[...] = d
        m2_ref[...] = m2
        v2_ref[...] = v2

    htile = pl.BlockSpec((tr, cdim), lambda i, c_ref: (i % nrh, 0))
    tile = pl.BlockSpec((tr, cdim), lambda i, c_ref: (i, 0))
    shp = jax.ShapeDtypeStruct((2 * half, cdim), F32)
    return _pc(body, name=name, out_shape=(shp, shp, shp, shp), grid=(2 * nrh,), nsp=1,
               in_specs=[htile, htile, tile, tile, tile], out_specs=(tile, tile, tile, tile),
               sem=("parallel",))(core, mine, theirs, w, m, v)


def _adamw_whole(name, g, w, m, v):
    def body(g_ref, w_ref, m_ref, v_ref, d_ref, m2_ref, v2_ref):
        d, m2, v2 = _adamw(w_ref[...], g_ref[...], m_ref[...], v_ref[...])
        d_ref[...] = d
        m2_ref[...] = m2
        v2_ref[...] = v2

    shp = jax.ShapeDtypeStruct(g.shape, F32)
    return _pc(body, name=name, out_shape=(shp, shp, shp))(g, w, m, v)


SMALL_LAYOUT = (
    ("sgu_w_s", 1024, 1, 0),
    ("sgu_b_s", 8, 1, 1024),
    ("sgu_norm_g", 1, 0, 0),
    ("sgu_norm_b", 1, 0, 1),
    ("hgrn_norm_g", 1, 0, 3),
    ("ln1_g", 1, 0, 4),
    ("ln1_b", 1, 0, 5),
    ("ffn_conv_b", 1, 2, 3),
    ("ln2_g", 1, 0, 6),
    ("ln2_b", 1, 0, 7),
)
LB_ROW = 2
LOSS_ROW = 8
PACK_SHAPES = ((16, D_MODEL), (N_GROUP * 128 + 8, 128), (8, D_FF))


def _small_allreduce_adamw(rows1024, dws, dbs, dcw, dcb, logits, m_logits, v_logits,
                           small_w, small_m, small_v):
    ns = len(SMALL_LAYOUT)
    nr = len(rows1024)
    nb = len(PACK_SHAPES)

    def body(*refs):
        row_refs = refs[:nr]
        dws_ref, dbs_ref, dcw_ref, dcb_ref, lg_ref, mlg_ref, vlg_ref = refs[nr:nr + 7]
        pos = nr + 7
        w_refs = refs[pos:pos + ns]
        m_refs = refs[pos + ns:pos + 2 * ns]
        v_refs = refs[pos + 2 * ns:pos + 3 * ns]
        pos += 3 * ns
        loss_ref, dcw_out = refs[pos:pos + 2]
        lg_outs = refs[pos + 2:pos + 6]
        pos += 6
        outs = refs[pos:pos + 4 * ns]
        pos += 4 * ns
        pack = refs[pos:pos + nb]
        sib = refs[pos + nb:pos + 2 * nb]
        gath = refs[pos + 2 * nb:pos + 3 * nb]
        d2d_send, d2d_recv, ici_send, ici_recv = refs[pos + 3 * nb:]

        x, y, c, peers = _chip_peers()
        me = 2 * x + y
        sibling = (x, y, 1 - c)

        pack[0][...] = jnp.zeros(PACK_SHAPES[0], F32)
        for k in range(nr):
            pack[0][k:k + 1, :] = row_refs[k][0:1, :]
        pack[1][0:N_GROUP * 128, :] = dws_ref[...]
        pack[1][N_GROUP * 128:, :] = dbs_ref[...]
        pack[2][...] = jnp.zeros(PACK_SHAPES[2], F32)
        pack[2][0:3, :] = dcw_ref[0:3, :]
        pack[2][3:4, :] = dcb_ref[0:1, :]

        d2d = [pltpu.make_async_remote_copy(
            src_ref=pack[b], dst_ref=sib[b], send_sem=d2d_send.at[b], recv_sem=d2d_recv.at[b],
            device_id=sibling, device_id_type=MESH) for b in range(nb)]
        for cp in d2d:
            cp.start()
        for cp in d2d:
            cp.wait()
        for b in range(nb):
            gath[b][me] = pack[b][...] + sib[b][...]

        ici, ici_wait = [], []
        for b in range(nb):
            for k, (px, py) in enumerate(peers):
                sem = dict(send_sem=ici_send.at[b * 3 + k], recv_sem=ici_recv.at[b * 3 + k],
                           device_id=(px, py, c), device_id_type=MESH)
                ici.append(pltpu.make_async_remote_copy(src_ref=gath[b].at[me], dst_ref=gath[b].at[me], **sem))
                ici_wait.append(pltpu.make_async_remote_copy(
                    src_ref=gath[b].at[me], dst_ref=gath[b].at[2 * px + py], **sem))
        for cp in ici:
            cp.start()
        for cp in ici_wait:
            cp.wait_recv()
        for cp in ici:
            cp.wait_send()

        tot = pack
        for b in range(nb):
            tot[b][...] = ((gath[b][0] + gath[b][1]) + gath[b][2]) + gath[b][3]

        loss_ref[...] = tot[0][LOSS_ROW:LOSS_ROW + 1, :]
        dcw_out[...] = tot[2][...]
        lb = _sig(lg_ref[0:1, :] - lg_ref[1:2, :])
        d0 = tot[0][LB_ROW:LB_ROW + 1, :] * lb * (1.0 - lb)
        rowid = lax.broadcasted_iota(jnp.int32, (2, D_MODEL), 0)
        g_lg = jnp.where(rowid == 0, d0, -d0)
        dl, ml, vl = _adamw(lg_ref[...], g_lg, mlg_ref[...], vlg_ref[...])
        lg_outs[0][...] = g_lg
        lg_outs[1][...] = dl
        lg_outs[2][...] = ml
        lg_outs[3][...] = vl
        for si, (_, rows, b, r0) in enumerate(SMALL_LAYOUT):
            g = tot[b][r0:r0 + rows, :]
            dl, ml, vl = _adamw(w_refs[si][...], g, m_refs[si][...], v_refs[si][...])
            outs[4 * si][...] = g
            outs[4 * si + 1][...] = dl
            outs[4 * si + 2][...] = ml
            outs[4 * si + 3][...] = vl

    shapes = [jax.ShapeDtypeStruct((1, D_MODEL), F32), jax.ShapeDtypeStruct((8, D_FF), F32)]
    shapes += [jax.ShapeDtypeStruct((2, D_MODEL), F32)] * 4
    for w in small_w:
        shapes += [jax.ShapeDtypeStruct(w.shape, F32)] * 4
    scratch = [pltpu.VMEM(shp, F32) for shp in PACK_SHAPES]
    scratch += [pltpu.VMEM(shp, F32) for shp in PACK_SHAPES]
    scratch += [pltpu.VMEM((N_CHIP,) + shp, F32) for shp in PACK_SHAPES]
    scratch += [pltpu.SemaphoreType.DMA((nb,)), pltpu.SemaphoreType.DMA((nb,)),
                pltpu.SemaphoreType.DMA((3 * nb,)), pltpu.SemaphoreType.DMA((3 * nb,))]
    vm = pl.BlockSpec(memory_space=pltpu.VMEM)
    n_in = nr + 7 + 3 * ns
    res = _pc(body, name="small_allreduce_adamw", out_shape=tuple(shapes),
              in_specs=[vm] * n_in, out_specs=tuple([vm] * len(shapes)),
              scratch=scratch)(*rows1024, dws, dbs, dcw, dcb, logits, m_logits, v_logits,
                               *small_w, *small_m, *small_v)
    return res[0], res[1], res[2:6], res[6:]


def kernel(x, p, w_in, sgu_w_s, sgu_b_s, sgu_norm_g, sgu_norm_b, hgrn_lb_logits, hgrn_norm_g, w_branch, w_out, ln1_g, ln1_b, ffn_w_up, ffn_conv_w, ffn_conv_b, ffn_w_down, ln2_g, ln2_b, ple_w_proj, ple_w_gate, loss_target, m_w_in, m_sgu_w_s, m_sgu_b_s, m_sgu_norm_g, m_sgu_norm_b, m_hgrn_lb_logits, m_hgrn_norm_g, m_w_branch, m_w_out, m_ln1_g, m_ln1_b, m_ffn_w_up, m_ffn_conv_w, m_ffn_conv_b, m_ffn_w_down, m_ln2_g, m_ln2_b, m_ple_w_proj, m_ple_w_gate, v_w_in, v_sgu_w_s, v_sgu_b_s, v_sgu_norm_g, v_sgu_norm_b, v_hgrn_lb_logits, v_hgrn_norm_g, v_w_branch, v_w_out, v_ln1_g, v_ln1_b, v_ffn_w_up, v_ffn_conv_w, v_ffn_conv_b, v_ffn_w_down, v_ln2_g, v_ln2_b, v_ple_w_proj, v_ple_w_gate):
    t = x.shape[1]
    x2 = x.reshape(t, D_MODEL)
    x2b = x2.astype(BF16)
    p2 = p.reshape(t, PLE_DIM)
    tgt = loss_target.reshape(t, D_MODEL)
    core = lax.axis_index("c").astype(jnp.int32).reshape(1)
    chip_id = (2 * lax.axis_index("x") + lax.axis_index("y")).astype(jnp.int32).reshape(1)

    big_w = [w_in[0], w_branch[0, 0], w_branch[0, 1], w_out[0], ffn_w_up[0], ffn_w_down[0],
             ple_w_proj[0], ple_w_gate[0]]
    big_m = [m_w_in[0], m_w_branch[0, 0], m_w_branch[0, 1], m_w_out[0], m_ffn_w_up[0],
             m_ffn_w_down[0], m_ple_w_proj[0], m_ple_w_gate[0]]
    big_v = [v_w_in[0], v_w_branch[0, 0], v_w_branch[0, 1], v_w_out[0], v_ffn_w_up[0],
             v_ffn_w_down[0], v_ple_w_proj[0], v_ple_w_gate[0]]
    def halves_of(i):
        w = big_w[i]
        return w.astype(BF16).reshape(2, w.shape[0] // 2, w.shape[1])

    def stacked(g, i):
        return g.reshape(N_CHIP, big_w[i].shape[0], big_w[i].shape[1])


    cid = jnp.arange(SGU_BLOCK) // CHUNK
    maskf = (cid[:, None] >= cid[None, :]).astype(F32)
    ws_masked = sgu_w_s[0] * maskf[None]
    wm = ws_masked.astype(BF16)
    wmt = jnp.transpose(ws_masked, (0, 2, 1)).astype(BF16)
    bsb = jnp.broadcast_to(sgu_b_s[0][:, :, None], (N_GROUP, SGU_BLOCK, 128))

    up_rows = big_w[4].shape[0] // 2
    up_blocks = [big_w[4][k * up_rows:(k + 1) * up_rows].astype(BF16).reshape(2, up_rows // 2, -1)
                 for k in range(2)]
    h, win_g, (up0_g,) = _in_proj_gathering(x2b, halves_of(0), chip_id, 512, _gather_comm([up_blocks[0]]))
    win_st = stacked(win_g, 0)
    ya, _ = _sgu_fwd(h, wm, bsb, sgu_norm_g, sgu_norm_b)
    (yb, o_all, st_all), mix_g = _hgrn_fwd(
        h, hgrn_lb_logits, hgrn_norm_g,
        comm=_gather_comm([halves_of(i) for i in (1, 2, 3)] + [up_blocks[1]], [ffn_conv_w[0]]))
    wb0, wb1, wo = [stacked(g, i).reshape(D_MODEL, D_MODEL) for g, i in zip(mix_g[:3], (1, 2, 3))]
    wup_st = jnp.concatenate([g.reshape(N_CHIP, up_rows, -1) for g in (up0_g, mix_g[3])], axis=1)
    convw = jnp.transpose(mix_g[4], (1, 0, 2)).reshape(3, D_FF)
    (r1, a_br, b_br, m_bf, x1b), _ = _mix_fwd(ya, yb, h, x2, wb0, wb1, wo, ln1_g, ln1_b, 256)
    h2, act, out_g = _ffn_up_act(x1b, wup_st, convw, ffn_conv_b, 512,
                                 _gather_comm([halves_of(i) for i in (5, 6, 7)]))
    wd = stacked(out_g[0], 5).reshape(D_FF, D_MODEL)
    wpp = jnp.transpose(stacked(out_g[1], 6), (1, 0, 2)).reshape(PLE_DIM, D_MODEL)
    wpg = stacked(out_g[2], 7).reshape(D_MODEL, D_MODEL)
    dr2, dpg, dpp, loss_acc, dg2, db2 = _out_fwd_bwd(
        act, x1b, r1, p2, tgt, wd, wpg, wpp, ln1_g, ln1_b, ln2_g, ln2_b, 256)

    dh2, dr1, dcw, dcb, dg1, db1 = _ffn_bwd(h2, dr2, wd, wup_st, dpg, wpg, r1, ln1_g, convw, 256)
    d_wd = _mm_tn("ffn_down_wgrad", act, dr2, FF_TILE, 512)
    d_wpg = _mm_tn("ple_gate_wgrad", x1b, dpg, 512, D_MODEL)
    d_wpp_st = _mm_tn("ple_proj_wgrad", p2, dpp, PLE_DIM, PLE_DIM, stacked=True)
    d_wup_st = _mm("ffn_up_wgrad", x1b, dh2, TN, (2, N_CHIP),
                   pl.BlockSpec((t, 512), lambda i, j: (0, i)),
                   pl.BlockSpec((None, t, FF_TILE), lambda i, j: (j // FF_NJ, 0, j % FF_NJ)),
                   jax.ShapeDtypeStruct((N_CHIP, D_MODEL, FF_TILE), BF16),
                   pl.BlockSpec((None, 512, FF_TILE), lambda i, j: (j, i, 0)))
    da_bf, db_bf, dh, dya, dyb = _mix_bwd(dr1, h, a_br, b_br, wo, wb0, wb1, 256)
    d_wo = _mm_tn("out_proj_wgrad", m_bf, dr1, 512, 512)
    d_wb0 = _mm_tn("branch0_wgrad", ya, da_bf, 512, D_MODEL)
    d_wb1 = _mm_tn("branch1_wgrad", yb, db_bf, 512, D_MODEL)
    grads_1 = [d_wb0.reshape(4, 256, D_MODEL), d_wb1.reshape(4, 256, D_MODEL),
               d_wo.reshape(4, 256, D_MODEL), d_wup_st, d_wd.reshape(4, D_FF // 4, D_MODEL),
               d_wpp_st, d_wpg.reshape(4, 256, D_MODEL)]
    (dh, dws, dbs, dgv, dbv), recv_a1 = _sgu_bwd(h, dya, wm, wmt, bsb, sgu_norm_g, sgu_norm_b, maskf, dh,
                                                 comm=_sibling_exchange_comm(grads_1))
    parts_1 = [_rs_add_halves("rs_add_halves%d" % (i + 1), g, r, core)
               for i, (g, r) in enumerate(zip(grads_1, recv_a1))]
    (dh, dlb, dgn), recv_b1 = _hgrn_bwd(h, o_all, dyb, st_all, hgrn_lb_logits, hgrn_norm_g, dh,
                                         comm=_chip_exchange_comm(parts_1))

    grads_0 = [_in_proj_wgrad(x2b, dh, 512, D_MODEL)]
    recv_a0 = _run_comm("rs_sibling_exchange0", _sibling_exchange_comm(grads_0))
    parts_0 = [_rs_add_halves("rs_add_halves0", grads_0[0], recv_a0[0], core)]
    gx, recv_b0 = _in_proj_xgrad(dh, win_st, dr1, 512, _chip_exchange_comm(parts_0))
    parts = parts_0 + parts_1
    recv_b = list(recv_b0) + list(recv_b1)
    halves = [_rs_sum_chips("rs_sum_chips%d" % i, pt, r, chip_id)
              for i, (pt, r) in enumerate(zip(parts, recv_b))]
    theirs = _rs_send_halves(halves)
    big_out = [_adamw_rows("adamw_big%d" % i, halves[i], theirs[i], big_w[i], big_m[i], big_v[i], core)
               for i in range(len(halves))]

    small_in = dict(sgu_w_s=(sgu_w_s, m_sgu_w_s, v_sgu_w_s), sgu_b_s=(sgu_b_s, m_sgu_b_s, v_sgu_b_s),
                    sgu_norm_g=(sgu_norm_g, m_sgu_norm_g, v_sgu_norm_g),
                    sgu_norm_b=(sgu_norm_b, m_sgu_norm_b, v_sgu_norm_b),
                    hgrn_norm_g=(hgrn_norm_g, m_hgrn_norm_g, v_hgrn_norm_g),
                    ln1_g=(ln1_g, m_ln1_g, v_ln1_g), ln1_b=(ln1_b, m_ln1_b, v_ln1_b),
                    ffn_conv_b=(ffn_conv_b, m_ffn_conv_b, v_ffn_conv_b),
                    ln2_g=(ln2_g, m_ln2_g, v_ln2_g), ln2_b=(ln2_b, m_ln2_b, v_ln2_b))

    def flat(name, arr):
        rows = dict((n, r) for n, r, _, _ in SMALL_LAYOUT)[name]
        return arr.reshape(rows, arr.size // rows)

    names = [n for n, _, _, _ in SMALL_LAYOUT]
    sw = [flat(n, small_in[n][0]) for n in names]
    sm = [flat(n, small_in[n][1]) for n in names]
    sv = [flat(n, small_in[n][2]) for n in names]
    loss_rows, dcw_tot, lg_out, small_out = _small_allreduce_adamw(
        [dgv, dbv, dlb, dgn, dg1, db1, dg2, db2, loss_acc], dws.reshape(N_GROUP * 128, 128), dbs, dcw, dcb,
        hgrn_lb_logits, m_hgrn_lb_logits, v_hgrn_lb_logits, sw, sm, sv)
    loss = loss_rows[0, 0]

    chip = 2 * lax.axis_index("x") + lax.axis_index("y")
    g_cw = lax.dynamic_slice(dcw_tot, (0, chip * (D_FF // 4)), (3, D_FF // 4))
    cw_out = _adamw_whole("adamw_conv_w", g_cw, ffn_conv_w[0], m_ffn_conv_w[0], v_ffn_conv_w[0])

    res = {}
    for si, n in enumerate(names):
        shp = small_in[n][0].shape
        res[n] = tuple(small_out[4 * si + k].reshape(shp) for k in range(4))
    res["hgrn_lb_logits"] = tuple(lg_out)
    res["ffn_conv_w"] = (g_cw[None],) + tuple(o[None] for o in cw_out)

    def big(i):
        return tuple(big_out[i])

    res["w_in"] = tuple(o[None] for o in big(0))
    res["w_branch"] = tuple(jnp.stack([o0, o1])[None] for o0, o1 in zip(big(1), big(2)))
    res["w_out"] = tuple(o[None] for o in big(3))
    res["ffn_w_up"] = tuple(o[None] for o in big(4))
    res["ffn_w_down"] = tuple(o[None] for o in big(5))
    res["ple_w_proj"] = tuple(o[None] for o in big(6))
    res["ple_w_gate"] = tuple(o[None] for o in big(7))

    order = ["w_in", "sgu_w_s", "sgu_b_s", "sgu_norm_g", "sgu_norm_b", "hgrn_lb_logits",
             "hgrn_norm_g", "w_branch", "w_out", "ln1_g", "ln1_b", "ffn_w_up", "ffn_conv_w",
             "ffn_conv_b", "ffn_w_down", "ln2_g", "ln2_b", "ple_w_proj", "ple_w_gate"]
    outs = [loss, gx.reshape(1, t, D_MODEL)]
    for k in range(4):
        outs += [res[n][k] for n in order]
    return tuple(outs)
```

```python
import jax
import jax.numpy as jnp
from jax import lax
from jax.experimental import pallas as pl
from jax.experimental.pallas import tpu as pltpu

F32 = jnp.float32
BF16 = jnp.bfloat16
HIGHEST = lax.Precision.HIGHEST
MESH = pl.DeviceIdType.MESH

D_MODEL = 1024
CHUNK = 64
SGU_BLOCK = 128
SGU_STEP_BLOCKS = 4
SGU_ROWS = SGU_STEP_BLOCKS * SGU_BLOCK
N_GROUP = 8
N_HEAD = 8
HEAD_DIM = 128
D_FF = 2816
PLE_DIM = 256
LN_EPS = 1e-5
RMS_EPS = 1e-6
ALPHA = 2.0 ** 0.25
N_CHIP = 4

ADAM_LR = 0.001
ADAM_B1 = 0.9
ADAM_B2 = 0.999
ADAM_EPS = 1e-08
ADAM_WD = 0.01
ADAM_STEP = 10

VMEM_LIMIT = 56 * 1024 * 1024

NN = (((1,), (0,)), ((), ()))
NT = (((1,), (1,)), ((), ()))
TN = (((0,), (0,)), ((), ()))


def _pc(body, *, name, out_shape, grid=None, in_specs=None, out_specs=None, scratch=(),
        sem=None, nsp=0, vmem=VMEM_LIMIT, aliases=None):
    params = dict(vmem_limit_bytes=vmem)
    if sem is not None:
        params["dimension_semantics"] = sem
    kw = dict(name=name, out_shape=out_shape, compiler_params=pltpu.CompilerParams(**params))
    if aliases:
        kw["input_output_aliases"] = aliases
    if nsp:
        kw["grid_spec"] = pltpu.PrefetchScalarGridSpec(
            num_scalar_prefetch=nsp, grid=grid, in_specs=in_specs, out_specs=out_specs,
            scratch_shapes=list(scratch))
    else:
        if grid is not None:
            kw["grid"] = grid
        if in_specs is not None:
            kw["in_specs"] = in_specs
            kw["out_specs"] = out_specs
        kw["scratch_shapes"] = list(scratch)
    return pl.pallas_call(body, **kw)


def _dot(a, b, dims=NN):
    return lax.dot_general(a.astype(BF16), b.astype(BF16), dims, preferred_element_type=F32)


def _dot32(a, b, dims=NN):
    return lax.dot_general(a, b, dims, precision=HIGHEST, preferred_element_type=F32)


def _sig(x):
    return 1.0 / (1.0 + jnp.exp(-x))


_GC = 0.7978845608028654
_GA = 0.044715


def _gelu(x):
    return 0.5 * x * (1.0 + jnp.tanh(_GC * (x + _GA * x * x * x)))


def _gelu_and_grad(x):
    t = jnp.tanh(_GC * (x + _GA * x * x * x))
    g = 0.5 * x * (1.0 + t)
    dg = 0.5 * (1.0 + t) + 0.5 * x * (1.0 - t * t) * _GC * (1.0 + 3.0 * _GA * x * x)
    return g, dg


def _ln_stats(r):
    mu = jnp.mean(r, axis=-1, keepdims=True)
    xc = r - mu
    var = jnp.mean(xc * xc, axis=-1, keepdims=True)
    rstd = lax.rsqrt(var + LN_EPS)
    return xc * rstd, rstd


def _ln_bwd(dxh, xh, rstd):
    m1 = jnp.mean(dxh, axis=-1, keepdims=True)
    m2 = jnp.mean(dxh * xh, axis=-1, keepdims=True)
    return rstd * (dxh - m1 - xh * m2)


def _colsum8(v):
    return jnp.broadcast_to(jnp.sum(v, axis=0, keepdims=True), (8, v.shape[1]))


def _adamw(w, g, m, v):
    m2 = ADAM_B1 * m + (1.0 - ADAM_B1) * g
    v2 = ADAM_B2 * v + (1.0 - ADAM_B2) * (g * g)
    m_hat = m2 / (1.0 - ADAM_B1 ** ADAM_STEP)
    v_hat = v2 / (1.0 - ADAM_B2 ** ADAM_STEP)
    delta = -ADAM_LR * (m_hat / (jnp.sqrt(v_hat) + ADAM_EPS) + ADAM_WD * w)
    return delta, m2, v2


def _row_tile(rows, cols, itemsize=4, budget=1 << 20, mult=8):
    best = mult
    for tr in range(mult, rows + 1, mult):
        if rows % tr == 0 and tr * cols * itemsize <= budget:
            best = tr
    return best


def _mm(name, a, b, dims, grid, a_spec, b_spec, out_shape, o_spec):
    out_dtype = out_shape.dtype

    def body(a_ref, b_ref, o_ref):
        o_ref[...] = _dot(a_ref[...], b_ref[...], dims).astype(out_dtype)

    return _pc(body, name=name, out_shape=out_shape, grid=grid, in_specs=[a_spec, b_spec],
               out_specs=o_spec, sem=("parallel", "parallel"))(a, b)


class _Comm:
    def __init__(self, ins, out_shapes, sems, start, finish, middle=None, finish_late=None):
        self.ins, self.out_shapes, self.sems = list(ins), list(out_shapes), list(sems)
        self.start, self.finish = start, finish
        self.middle, self.finish_late = middle, finish_late


def _hosted_call(body, comm, first, last, *, name, out_shape, grid, in_specs, out_specs, scratch, sem,
                 args, aliases=None, mid=None):
    n_in, n_out, n_scr = len(in_specs), len(out_shape), len(scratch)
    nci, nco = len(comm.ins), len(comm.out_shapes)

    def wrapped(*refs):
        pos = n_in
        own_in, c_in = refs[:pos], refs[pos:pos + nci]
        pos += nci
        own_out, c_out = refs[pos:pos + n_out], refs[pos + n_out:pos + n_out + nco]
        pos += n_out + nco
        own_scr, c_sem = refs[pos:pos + n_scr], refs[pos + n_scr:]

        @pl.when(first())
        def _():
            comm.start(c_in, c_out, c_sem)

        body(*own_in, *own_out, *own_scr)

        if mid is not None:
            @pl.when(mid())
            def _():
                comm.middle(c_in, c_out, c_sem)

        @pl.when(last())
        def _():
            (comm.finish if mid is None else comm.finish_late)(c_in, c_out, c_sem)

    return _pc(wrapped, name=name, out_shape=tuple(out_shape) + tuple(comm.out_shapes), grid=grid,
               in_specs=list(in_specs) + [ANY] * nci, out_specs=tuple(out_specs) + tuple([ANY] * nco),
               scratch=list(scratch) + comm.sems, sem=sem, aliases=aliases)(*args, *comm.ins)


def _grid1_call(body, comm, n, *, name, out_shape, in_specs, out_specs, scratch, args, aliases=None):
    if comm is None:
        return _pc(body, name=name, out_shape=out_shape, grid=(n,), in_specs=in_specs, out_specs=out_specs,
                   scratch=scratch, sem=("arbitrary",), aliases=aliases)(*args), ()
    res = _hosted_call(body, comm, lambda: pl.program_id(0) == 0, lambda: pl.program_id(0) == n - 1,
                       name=name, out_shape=out_shape, grid=(n,), in_specs=in_specs,
                       out_specs=out_specs, scratch=scratch, sem=("arbitrary",), args=args, aliases=aliases)
    return res[:len(out_shape)], res[len(out_shape):]


def _run_comm(name, comm):
    nci, nco = len(comm.ins), len(comm.out_shapes)

    def body(*refs):
        c_in, c_out, c_sem = refs[:nci], refs[nci:nci + nco], refs[nci + nco:]
        comm.start(c_in, c_out, c_sem)
        comm.finish(c_in, c_out, c_sem)

    return _pc(body, name=name, out_shape=tuple(comm.out_shapes), in_specs=[ANY] * nci,
               out_specs=tuple([ANY] * nco), scratch=comm.sems)(*comm.ins)


def _mm_tn(name, a, b, tm, tn, stacked=False):
    t, m = a.shape
    _, n = b.shape
    if stacked:
        assert tm == m
        out_shape = jax.ShapeDtypeStruct((n // tn, m, tn), BF16)
        o_spec = pl.BlockSpec((None, tm, tn), lambda i, j: (j, 0, 0))
    else:
        out_shape = jax.ShapeDtypeStruct((m, n), BF16)
        o_spec = pl.BlockSpec((tm, tn), lambda i, j: (i, j))
    return _mm(name, a, b, TN, (m // tm, n // tn),
               pl.BlockSpec((t, tm), lambda i, j: (0, i)),
               pl.BlockSpec((t, tn), lambda i, j: (0, j)),
               out_shape, o_spec)


DH_SLOT = (2, 0, 1, 3)


def _dh_slot(j):
    return jnp.where(j == 3, 3, (j + 2) % 3)


def _in_proj_wgrad(x2b, dh, tm, tn):
    t, m = x2b.shape
    n = dh.shape[2]

    def body(a_ref, b_ref, o_ref):
        o_ref[...] = _dot(a_ref[...], b_ref[...], TN).astype(BF16)

    return _pc(body, name="in_proj_wgrad", out_shape=jax.ShapeDtypeStruct((N_CHIP, m, n), BF16),
               grid=(N_CHIP, m // tm, n // tn),
               in_specs=[pl.BlockSpec((t, tm), lambda j, i, k: (0, i)),
                         pl.BlockSpec((None, t, tn), lambda j, i, k: (_dh_slot(j), 0, k))],
               out_specs=pl.BlockSpec((None, tm, tn), lambda j, i, k: (j, i, k)),
               sem=("parallel", "parallel", "parallel"))(x2b, dh)


def _in_proj_xgrad(dh, win_st, dr1, tm, comm):
    t = dr1.shape[0]
    ni = t // tm

    def body(a_ref, b_ref, add_ref, o_ref, acc):
        j = pl.program_id(1)
        prod = _dot(a_ref[...], b_ref[...], NT)

        @pl.when(j == 0)
        def _():
            acc[...] = prod + ALPHA * add_ref[...]

        @pl.when((j > 0) & (j < N_CHIP - 1))
        def _():
            acc[...] += prod

        @pl.when(j == N_CHIP - 1)
        def _():
            o_ref[...] = acc[...] + prod

    tile = pl.BlockSpec((tm, D_MODEL), lambda i, j: (i, 0))
    res = _hosted_call(body, comm,
                       lambda: (pl.program_id(0) == 0) & (pl.program_id(1) == 0),
                       lambda: (pl.program_id(0) == ni - 1) & (pl.program_id(1) == N_CHIP - 1),
                       name="in_proj_xgrad", out_shape=(jax.ShapeDtypeStruct((t, D_MODEL), F32),),
                       grid=(ni, N_CHIP),
                       in_specs=[pl.BlockSpec((None, tm, 2 * D_MODEL), lambda i, j: (_dh_slot(j), i, 0)),
                                 pl.BlockSpec((None, D_MODEL, 2 * D_MODEL), lambda i, j: (j, 0, 0)),
                                 tile],
                       out_specs=(tile,), scratch=[pltpu.VMEM((tm, D_MODEL), F32)],
                       sem=("arbitrary", "arbitrary"), args=[dh, win_st, dr1])
    return res[0], res[1:]


def _sgu_mixed(v, wm_ref, bsb_ref, gv, bv):
    gl, dgl = _gelu_and_grad(v)
    vh, rstd = _ln_stats(gl)
    vn = vh * gv + bv
    mixed = []
    for g in range(N_GROUP):
        sl = slice(g * 128, (g + 1) * 128)
        mixed.append(_dot(wm_ref[g], vn[:, sl]) + bsb_ref[g])
    return dgl, vh, rstd, vn, mixed


def _sgu_fwd(h, wm, bsb, gv, bv, comm=None):
    t = h.shape[0]

    def body(u_ref, v_ref, wm_ref, bsb_ref, gv_ref, bv_ref, ya_ref):
        for bb in range(SGU_STEP_BLOCKS):
            rows = slice(bb * SGU_BLOCK, (bb + 1) * SGU_BLOCK)
            u = u_ref[rows, :].astype(F32)
            _, _, _, _, mixed = _sgu_mixed(v_ref[rows, :].astype(F32), wm_ref, bsb_ref, gv_ref[...],
                                           bv_ref[...])
            gu = _gelu(u)
            for g in range(N_GROUP):
                sl = slice(g * 128, (g + 1) * 128)
                ya_ref[rows, sl] = (gu[:, sl] * mixed[g]).astype(BF16)

    full3 = pl.BlockSpec((N_GROUP, 128, 128), lambda i: (0, 0, 0))
    vec = pl.BlockSpec((1, D_MODEL), lambda i: (0, 0))
    (ya,), extra = _grid1_call(
        body, comm, t // SGU_ROWS, name="sgu_fwd",
        out_shape=(jax.ShapeDtypeStruct((t, D_MODEL), BF16),),
        in_specs=[pl.BlockSpec((SGU_ROWS, D_MODEL), lambda i: (i, 0)),
                  pl.BlockSpec((SGU_ROWS, D_MODEL), lambda i: (i, 1)),
                  full3, full3, vec, vec],
        out_specs=(pl.BlockSpec((SGU_ROWS, D_MODEL), lambda i: (i, 0)),),
        scratch=[], args=(h, h, wm, bsb, gv, bv))
    return ya, extra


def _sgu_bwd(h, dya, wm, wmt, bsb, gv, bv, maskf, dh_buf, comm=None):
    t = h.shape[0]
    nb = t // SGU_ROWS

    def body(u_ref, v_ref, dya_ref, wm_ref, wmt_ref, bsb_ref, gv_ref, bv_ref, mask_ref, dh_buf_ref,
             dh_ref, dws_ref, dbs_ref, dgv_ref, dbv_ref, dmix_acc):
        i = pl.program_id(0)

        @pl.when(i == 0)
        def _():
            dws_ref[...] = jnp.zeros_like(dws_ref)
            dgv_ref[...] = jnp.zeros_like(dgv_ref)
            dbv_ref[...] = jnp.zeros_like(dbv_ref)
            dmix_acc[...] = jnp.zeros_like(dmix_acc)

        gvv = gv_ref[...]
        for bb in range(SGU_STEP_BLOCKS):
            rows = slice(bb * SGU_BLOCK, (bb + 1) * SGU_BLOCK)
            u = u_ref[rows, :].astype(F32)
            dgl_v, vh, rstd, vn, mixed = _sgu_mixed(v_ref[rows, :].astype(F32), wm_ref, bsb_ref, gvv,
                                                    bv_ref[...])
            gu, dgl_u = _gelu_and_grad(u)
            dya_v = dya_ref[rows, :].astype(F32)
            dvn_parts = []
            for g in range(N_GROUP):
                sl = slice(g * 128, (g + 1) * 128)
                d_y = dya_v[:, sl]
                dh_ref[rows, sl] = (d_y * mixed[g] * dgl_u[:, sl]).astype(BF16)
                d_mixed = d_y * gu[:, sl]
                dmix_acc[g] += d_mixed
                dws_ref[g] += _dot(d_mixed, vn[:, sl], NT) * mask_ref[...]
                dvn_parts.append(_dot(wmt_ref[g], d_mixed))
            dvn = jnp.concatenate(dvn_parts, axis=1)
            dgv_ref[...] += _colsum8(dvn * vh)
            dbv_ref[...] += _colsum8(dvn)
            d_gl = _ln_bwd(dvn * gvv, vh, rstd)
            dh_ref[rows, D_MODEL:] = (d_gl * dgl_v).astype(BF16)

        @pl.when(i == nb - 1)
        def _():
            rowid = lax.broadcasted_iota(jnp.int32, (8, 128), 0)
            ones = jnp.ones((8, 128), F32)
            acc = jnp.zeros((8, 128), F32)
            for g in range(N_GROUP):
                rs = _dot32(ones, dmix_acc[g], NT)
                acc = jnp.where(rowid == g, rs, acc)
            dbs_ref[...] = acc

    full3 = pl.BlockSpec((N_GROUP, 128, 128), lambda i: (0, 0, 0))
    vec = pl.BlockSpec((1, D_MODEL), lambda i: (0, 0))
    acc8 = pl.BlockSpec((8, D_MODEL), lambda i: (0, 0))
    return _grid1_call(
        body, comm, nb, name="sgu_bwd",
        out_shape=(jax.ShapeDtypeStruct(dh_buf.shape, BF16),
                   jax.ShapeDtypeStruct((N_GROUP, 128, 128), F32),
                   jax.ShapeDtypeStruct((8, 128), F32),
                   jax.ShapeDtypeStruct((8, D_MODEL), F32),
                   jax.ShapeDtypeStruct((8, D_MODEL), F32)),
        in_specs=[pl.BlockSpec((SGU_ROWS, D_MODEL), lambda i: (i, 0)),
                  pl.BlockSpec((SGU_ROWS, D_MODEL), lambda i: (i, 1)),
                  pl.BlockSpec((SGU_ROWS, D_MODEL), lambda i: (i, 0)),
                  full3, full3, full3, vec, vec,
                  pl.BlockSpec((128, 128), lambda i: (0, 0)), ANY],
        out_specs=(pl.BlockSpec((None, SGU_ROWS, 2 * D_MODEL), lambda i: (DH_SLOT[0], i, 0)),
                   full3, pl.BlockSpec((8, 128), lambda i: (0, 0)), acc8, acc8),
        scratch=[pltpu.VMEM((N_GROUP, 128, 128), F32)],
        args=(h, h, dya, wm, wmt, bsb, gv, bv, maskf, dh_buf), aliases={9: 0})


def _tri_masks():
    row = lax.broadcasted_iota(jnp.int32, (CHUNK, CHUNK), 0)
    col = lax.broadcasted_iota(jnp.int32, (CHUNK, CHUNK), 1)
    return col <= row, col >= row


def _heads(v):
    return [v[:, hd * HEAD_DIM:(hd + 1) * HEAD_DIM] for hd in range(N_HEAD)]


def _tri_cumsum(tri_bf, v):
    hi = v.astype(BF16)
    r = v - hi.astype(F32)
    mid = r.astype(BF16)
    lo = (r - mid.astype(F32)).astype(BF16)
    return _dot(tri_bf, hi) + _dot(tri_bf, mid) + _dot(tri_bf, lo)


def _hgrn_chunk(q, fp, ii, lb, st_heads, causal, with_o=True):
    sg = _sig(fp)
    f = lb + (1.0 - lb) * sg
    k = 1.0 - f
    c = _tri_cumsum(causal.astype(BF16), jnp.log(f))
    ec = jnp.exp(c)
    en = jnp.exp(-c)
    sq = _sig(q)
    qt = q * sq * ec
    kt = k * en
    ecl = jnp.exp(c[CHUNK - 1:CHUNK, :])
    kk = kt * ecl
    qtb, ktb, iib, kkb = qt.astype(BF16), kt.astype(BF16), ii.astype(BF16), kk.astype(BF16)
    attn, o = [], []
    for hd, (qh, kh, ih) in enumerate(zip(_heads(qtb), _heads(ktb), _heads(iib))):
        a = jnp.where(causal, _dot(qh, kh, NT), 0.0).astype(BF16)
        attn.append(a)
        if with_o:
            o.append(_dot(a, ih) + _dot(qh, st_heads[hd], NT))
    return dict(sg=sg, f=f, k=k, ec=ec, en=en, sq=sq, ecl=ecl, kk=kk, qtb=qtb, ktb=ktb, iib=iib,
                kkb=kkb, attn=attn, o=o)


def _rms_heads(o_heads):
    rinv = [lax.rsqrt(jnp.mean(o * o, axis=-1, keepdims=True) + RMS_EPS) for o in o_heads]
    return rinv, jnp.concatenate([o * r for o, r in zip(o_heads, rinv)], axis=1)


HG_CHUNKS = 8
HG_ROWS = HG_CHUNKS * CHUNK


def _hgrn_fwd(h, logits, gn, comm=None):
    t = h.shape[0]
    nb = t // HG_ROWS

    def body(q_ref, f_ref, i_ref, og_ref, lg_ref, gn_ref, yb_ref, o_ref, st_ref, state):
        @pl.when(pl.program_id(0) == 0)
        def _():
            state[...] = jnp.zeros_like(state)

        causal, _ = _tri_masks()
        lb = _sig(lg_ref[0:1, :] - lg_ref[1:2, :])
        gnv = gn_ref[...]
        st = [state[hd] for hd in range(N_HEAD)]
        for cc in range(HG_CHUNKS):
            rows = slice(cc * CHUNK, (cc + 1) * CHUNK)
            og = og_ref[rows, :].astype(F32)
            r = _hgrn_chunk(q_ref[rows, :].astype(F32), f_ref[rows, :].astype(F32),
                            i_ref[rows, :].astype(F32), lb, [s.astype(BF16) for s in st], causal)
            o_bf = jnp.concatenate(r["o"], axis=1).astype(BF16)
            o_ref[rows, :] = o_bf
            _, on = _rms_heads(_heads(o_bf.astype(F32)))
            yb_ref[rows, :] = (on * gnv * (og * _sig(og))).astype(BF16)
            for hd in range(N_HEAD):
                st_ref[cc, hd] = st[hd]
            st = [s * e + _dot(ih, kh, TN)
                  for s, e, ih, kh in zip(st, _heads(r["ecl"]), _heads(r["iib"]), _heads(r["kkb"]))]
        for hd in range(N_HEAD):
            state[hd] = st[hd]

    def col(k):
        return pl.BlockSpec((HG_ROWS, D_MODEL), lambda ci: (ci, k))

    return _grid1_call(body, comm, nb, name="hgrn_fwd",
                       out_shape=(jax.ShapeDtypeStruct((t, D_MODEL), BF16),
                                  jax.ShapeDtypeStruct((t, D_MODEL), BF16),
                                  jax.ShapeDtypeStruct((t // CHUNK, N_HEAD, HEAD_DIM, HEAD_DIM), F32)),
                       in_specs=[col(2), col(3), col(4), col(5),
                                 pl.BlockSpec((2, D_MODEL), lambda ci: (0, 0)),
                                 pl.BlockSpec((1, D_MODEL), lambda ci: (0, 0))],
                       out_specs=(pl.BlockSpec((HG_ROWS, D_MODEL), lambda ci: (ci, 0)),
                                  pl.BlockSpec((HG_ROWS, D_MODEL), lambda ci: (ci, 0)),
                                  pl.BlockSpec((HG_CHUNKS, N_HEAD, HEAD_DIM, HEAD_DIM),
                                               lambda ci: (ci, 0, 0, 0))),
                       scratch=[pltpu.VMEM((N_HEAD, HEAD_DIM, HEAD_DIM), F32)],
                       args=(h, h, h, h, logits, gn))


def _hgrn_chunk_bwd(q, fp, ii, og, o_saved, dy, gnv, lb, st, dsn, causal, anti):
    stb = [s.astype(BF16) for s in st]
    dsnb = [s.astype(BF16) for s in dsn]
    r = _hgrn_chunk(q, fp, ii, lb, stb, causal, with_o=False)
    rinv, on = _rms_heads(_heads(o_saved))
    so = _sig(og)
    sil = og * so
    d_og = dy * on * gnv * (so * (1.0 + og * (1.0 - so)))
    d_on = dy * gnv * sil
    d_ob = jnp.concatenate(
        [ri * (dn - oh * jnp.mean(dn * oh, axis=-1, keepdims=True))
         for ri, dn, oh in zip(rinv, _heads(d_on), _heads(on))], axis=1).astype(BF16)
    d_i, d_qt, d_kt, d_kk, d_st, st_dsn = [], [], [], [], [], []
    ecl = _heads(r["ecl"])
    for hd, (dh, qh, kh, ih, kkh) in enumerate(zip(_heads(d_ob), _heads(r["qtb"]), _heads(r["ktb"]),
                                                   _heads(r["iib"]), _heads(r["kkb"]))):
        d_attn = jnp.where(causal, _dot(dh, ih, NT), 0.0).astype(BF16)
        d_i.append(_dot(r["attn"][hd], dh, TN) + _dot(kkh, dsnb[hd], NT))
        d_qt.append(_dot(d_attn, kh) + _dot(dh, stb[hd]))
        d_kt.append(_dot(d_attn, qh, TN))
        d_kk.append(_dot(ih, dsnb[hd]))
        d_st.append(_dot(dh, qh, TN) + dsn[hd] * ecl[hd])
        st_dsn.append(jnp.sum(st[hd] * dsn[hd], axis=0, keepdims=True))
    d_qt = jnp.concatenate(d_qt, axis=1)
    d_kt = jnp.concatenate(d_kt, axis=1)
    d_kk = jnp.concatenate(d_kk, axis=1)
    kk = r["kk"]
    d_cl = r["ecl"] * jnp.concatenate(st_dsn, axis=1) + jnp.sum(kk * d_kk, axis=0, keepdims=True)
    d_k = (d_kk * r["ecl"] + d_kt) * r["en"]
    d_c = d_qt * r["qtb"].astype(F32) - d_kt * r["ktb"].astype(F32) - d_kk * kk
    rowid = lax.broadcasted_iota(jnp.int32, (CHUNK, D_MODEL), 0)
    d_c = d_c + jnp.where(rowid == CHUNK - 1, d_cl, 0.0)
    d_lf = _tri_cumsum(anti.astype(BF16), d_c)
    d_f = d_lf / r["f"] - d_k
    sg, sq = r["sg"], r["sq"]
    d_q = d_qt * r["ec"] * (sq * (1.0 + q * (1.0 - sq)))
    d_fp = d_f * (1.0 - lb) * sg * (1.0 - sg)
    return (d_q, d_fp, jnp.concatenate(d_i, axis=1), d_og, d_st,
            _colsum8(dy * on * sil), _colsum8(d_f * (1.0 - sg)))


def _hgrn_bwd(h, o_all, dyb, st_all, logits, gn, dh_buf, comm=None):
    t = h.shape[0]
    nb = t // HG_ROWS

    def body(q_ref, f_ref, i_ref, og_ref, o_ref, dyb_ref, st_ref, lg_ref, gn_ref, dh_buf_ref,
             dh_ref, dlb_ref, dgn_ref, dstate):
        @pl.when(pl.program_id(0) == 0)
        def _():
            dstate[...] = jnp.zeros_like(dstate)
            dlb_ref[...] = jnp.zeros_like(dlb_ref)
            dgn_ref[...] = jnp.zeros_like(dgn_ref)

        causal, anti = _tri_masks()
        lb = _sig(lg_ref[0:1, :] - lg_ref[1:2, :])
        gnv = gn_ref[...]
        dsn = [dstate[hd] for hd in range(N_HEAD)]
        dgn_acc = jnp.zeros((8, D_MODEL), F32)
        dlb_acc = jnp.zeros((8, D_MODEL), F32)
        for cc in reversed(range(HG_CHUNKS)):
            rows = slice(cc * CHUNK, (cc + 1) * CHUNK)
            d_q, d_fp, d_i, d_og, dsn, dgn_c, dlb_c = _hgrn_chunk_bwd(
                q_ref[rows, :].astype(F32), f_ref[rows, :].astype(F32), i_ref[rows, :].astype(F32),
                og_ref[rows, :].astype(F32), o_ref[rows, :].astype(F32), dyb_ref[rows, :].astype(F32), gnv, lb,
                [st_ref[cc, hd] for hd in range(N_HEAD)], dsn, causal, anti)
            dgn_acc = dgn_acc + dgn_c
            dlb_acc = dlb_acc + dlb_c
            dh_ref[0, rows, :D_MODEL] = d_q.astype(BF16)
            dh_ref[0, rows, D_MODEL:] = d_fp.astype(BF16)
            dh_ref[1, rows, :D_MODEL] = d_i.astype(BF16)
            dh_ref[1, rows, D_MODEL:] = d_og.astype(BF16)
        dgn_ref[...] += dgn_acc
        dlb_ref[...] += dlb_acc
        for hd in range(N_HEAD):
            dstate[hd] = dsn[hd]

    def col(k):
        return pl.BlockSpec((HG_ROWS, D_MODEL), lambda ci: (nb - 1 - ci, k))

    acc8 = pl.BlockSpec((8, D_MODEL), lambda ci: (0, 0))
    pair = pl.BlockSpec((2, HG_ROWS, 2 * D_MODEL), lambda ci: (0, nb - 1 - ci, 0))
    return _grid1_call(body, comm, nb, name="hgrn_bwd",
                       out_shape=(jax.ShapeDtypeStruct(dh_buf.shape, BF16),
                                  jax.ShapeDtypeStruct((8, D_MODEL), F32),
                                  jax.ShapeDtypeStruct((8, D_MODEL), F32)),
                       in_specs=[col(2), col(3), col(4), col(5), col(0),
                                 pl.BlockSpec((HG_ROWS, D_MODEL), lambda ci: (nb - 1 - ci, 0)),
                                 pl.BlockSpec((HG_CHUNKS, N_HEAD, HEAD_DIM, HEAD_DIM),
                                              lambda ci: (nb - 1 - ci, 0, 0, 0)),
                                 pl.BlockSpec((2, D_MODEL), lambda ci: (0, 0)),
                                 pl.BlockSpec((1, D_MODEL), lambda ci: (0, 0)), ANY],
                       out_specs=(pair, acc8, acc8),
                       scratch=[pltpu.VMEM((N_HEAD, HEAD_DIM, HEAD_DIM), F32)],
                       args=(h, h, h, h, o_all, dyb, st_all, logits, gn, dh_buf), aliases={9: 0})


def _mix_fwd(ya, yb, h, x, wb0, wb1, wo, g1, b1, tm, comm=None):
    t = x.shape[0]

    def body(ya_ref, yb_ref, ga_ref, gb_ref, x_ref, wb0_ref, wb1_ref, wo_ref, g1_ref, b1_ref,
             r1_ref, a_ref, b_ref, m_ref, x1_ref):
        a = _dot(ya_ref[...], wb0_ref[...])
        b = _dot(yb_ref[...], wb1_ref[...])
        m = _sig(ga_ref[...].astype(F32)) * a + _sig(gb_ref[...].astype(F32)) * b
        r1 = ALPHA * x_ref[...] + _dot(m, wo_ref[...])
        xh, _ = _ln_stats(r1)
        r1_ref[...] = r1
        a_ref[...] = a.astype(BF16)
        b_ref[...] = b.astype(BF16)
        m_ref[...] = m.astype(BF16)
        x1_ref[...] = (xh * g1_ref[...] + b1_ref[...]).astype(BF16)

    tile = pl.BlockSpec((tm, D_MODEL), lambda i: (i, 0))
    wsp = pl.BlockSpec((D_MODEL, D_MODEL), lambda i: (0, 0))
    vec = pl.BlockSpec((1, D_MODEL), lambda i: (0, 0))
    f32o = jax.ShapeDtypeStruct((t, D_MODEL), F32)
    bfo = jax.ShapeDtypeStruct((t, D_MODEL), BF16)
    return _grid1_call(body, comm, t // tm, name="mix_fwd", out_shape=(f32o, bfo, bfo, bfo, bfo),
                       in_specs=[tile, tile,
                                 pl.BlockSpec((tm, D_MODEL), lambda i: (i, 6)),
                                 pl.BlockSpec((tm, D_MODEL), lambda i: (i, 7)),
                                 tile, wsp, wsp, wsp, vec, vec],
                       out_specs=(tile, tile, tile, tile, tile),
                       scratch=[], args=(ya, yb, h, h, x, wb0, wb1, wo, g1, b1))


def _mix_bwd(dr1, h, a, b, wo, wb0, wb1, tm):
    t = dr1.shape[0]

    def body(dr1_ref, ga_ref, gb_ref, a_ref, b_ref, wo_ref, wb0_ref, wb1_ref,
             da_ref, db_ref, dh3_ref, dya_ref, dyb_ref):
        d_m = _dot(dr1_ref[...], wo_ref[...], NT)
        sa = _sig(ga_ref[...].astype(F32))
        sb = _sig(gb_ref[...].astype(F32))
        d_a = (d_m * sa).astype(BF16)
        d_b = (d_m * sb).astype(BF16)
        da_ref[...] = d_a
        db_ref[...] = d_b
        dh3_ref[:, :D_MODEL] = (d_m * a_ref[...].astype(F32) * sa * (1.0 - sa)).astype(BF16)
        dh3_ref[:, D_MODEL:] = (d_m * b_ref[...].astype(F32) * sb * (1.0 - sb)).astype(BF16)
        dya_ref[...] = _dot(d_a, wb0_ref[...], NT).astype(BF16)
        dyb_ref[...] = _dot(d_b, wb1_ref[...], NT).astype(BF16)

    tile = pl.BlockSpec((tm, D_MODEL), lambda i: (i, 0))
    wsp = pl.BlockSpec((D_MODEL, D_MODEL), lambda i: (0, 0))
    f32o = jax.ShapeDtypeStruct((t, D_MODEL), F32)
    bfo = jax.ShapeDtypeStruct((t, D_MODEL), BF16)
    return _pc(body, name="mix_bwd",
               out_shape=(bfo, bfo, jax.ShapeDtypeStruct((N_CHIP, t, 2 * D_MODEL), BF16), bfo, bfo),
               grid=(t // tm,),
               in_specs=[tile,
                         pl.BlockSpec((tm, D_MODEL), lambda i: (i, 6)),
                         pl.BlockSpec((tm, D_MODEL), lambda i: (i, 7)),
                         tile, tile, wsp, wsp, wsp],
               out_specs=(tile, tile, pl.BlockSpec((None, tm, 2 * D_MODEL), lambda i: (DH_SLOT[3], i, 0)),
                          tile, tile),
               sem=("parallel",))(dr1, h, h, a, b, wo, wb0, wb1)


FF_TILE = 1408
FF_NJ = D_FF // FF_TILE


def _shift_down(v, k):
    return pltpu.roll(v, k, 0)


def _shift_up(v, k):
    return pltpu.roll(v, v.shape[0] - k, 0)


HALO = 16
FF_PIECES = ((0, 768), (768, FF_TILE))


def _ffn_up_act(x1b, wup_st, convw, convb, tm, comm):
    t = x1b.shape[0]
    ni = t // tm
    nth = tm // HALO

    def body(x_ref, xp_ref, wg_ref, wv_ref, cw_ref, cb_ref, h2_ref, act_ref):
        wg = wg_ref[...]
        gate = _dot(x_ref[...], wg).astype(BF16)
        val = _dot(x_ref[...], wv_ref[...]).astype(BF16)
        prev = (_dot(xp_ref[...], wg) * (pl.program_id(0) > 0).astype(F32)).astype(BF16)
        h2_ref[0] = gate
        h2_ref[1] = val
        ext = jnp.concatenate([prev.astype(F32), gate.astype(F32)], axis=0)
        gc = (cw_ref[0:1, :] * _shift_down(ext, 2) + cw_ref[1:2, :] * _shift_down(ext, 1)
              + cw_ref[2:3, :] * ext + cb_ref[...])[HALO:, :].astype(BF16)
        h2_ref[2] = gc
        act_ref[...] = (_gelu(gc.astype(F32)) * val.astype(F32)).astype(BF16)

    res = _hosted_call(
        body, comm,
        lambda: (pl.program_id(0) == 0) & (pl.program_id(1) == 0),
        lambda: (pl.program_id(0) == ni - 1) & (pl.program_id(1) == FF_NJ - 1),
        mid=lambda: (pl.program_id(0) == max(ni - 2, 0)) & (pl.program_id(1) == 0),
        name="ffn_up",
        out_shape=(jax.ShapeDtypeStruct((3, t, D_FF), BF16), jax.ShapeDtypeStruct((t, D_FF), BF16)),
        grid=(ni, FF_NJ),
        in_specs=[pl.BlockSpec((tm, D_MODEL), lambda i, j: (i, 0)),
                  pl.BlockSpec((HALO, D_MODEL), lambda i, j: (jnp.maximum(i * nth - 1, 0), 0)),
                  pl.BlockSpec((None, D_MODEL, FF_TILE), lambda i, j: (j, 0, 0)),
                  pl.BlockSpec((None, D_MODEL, FF_TILE), lambda i, j: (j + FF_NJ, 0, 0)),
                  pl.BlockSpec((3, FF_TILE), lambda i, j: (0, j)),
                  pl.BlockSpec((1, FF_TILE), lambda i, j: (0, j))],
        out_specs=(pl.BlockSpec((3, tm, FF_TILE), lambda i, j: (0, i, j)),
                   pl.BlockSpec((tm, FF_TILE), lambda i, j: (i, j))),
        scratch=[], sem=("arbitrary", "arbitrary"),
        args=(x1b, x1b, wup_st, wup_st, convw, convb))
    return res[0], res[1], res[2:]


def _out_fwd_bwd(act, x1b, r1, p2, tgt, wd, wpg, wpp, g1, b1, g2, b2, tm):
    t = r1.shape[0]

    def body(act_ref, x1b_ref, r1_ref, p_ref, tgt_ref, wd_ref, wpg_ref, wpp_ref,
             g1_ref, b1_ref, g2_ref, b2_ref,
             dr2_ref, dpg_ref, dpp_ref, loss_ref, dg2_ref, db2_ref):
        i = pl.program_id(0)

        @pl.when(i == 0)
        def _():
            loss_ref[...] = jnp.zeros_like(loss_ref)
            dg2_ref[...] = jnp.zeros_like(dg2_ref)
            db2_ref[...] = jnp.zeros_like(db2_ref)

        ffn = _dot(act_ref[...], wd_ref[...])
        pg = _dot(x1b_ref[...], wpg_ref[...])
        pp = _dot(p_ref[...], wpp_ref[...])
        s = _sig(pg)
        xh1, _ = _ln_stats(r1_ref[...])
        x1 = xh1 * g1_ref[...] + b1_ref[...]
        r2 = ALPHA * x1 + ffn + s * pp
        xh2, rstd2 = _ln_stats(r2)
        g2v = g2_ref[...]
        diff = xh2 * g2v + b2_ref[...] - tgt_ref[...]
        part = jnp.sum(jnp.sum(diff * diff, axis=1, keepdims=True), axis=0, keepdims=True)
        loss_ref[...] += jnp.broadcast_to(part * (0.5 / D_MODEL), loss_ref.shape)
        dy = diff * (1.0 / D_MODEL)
        dg2_ref[...] += _colsum8(dy * xh2)
        db2_ref[...] += _colsum8(dy)
        dr2 = _ln_bwd(dy * g2v, xh2, rstd2)
        dr2_ref[...] = dr2
        dpg_ref[...] = (dr2 * pp * s * (1.0 - s)).astype(BF16)
        dpp_ref[...] = (dr2 * s).astype(BF16)

    tile = pl.BlockSpec((tm, D_MODEL), lambda i: (i, 0))
    vec = pl.BlockSpec((1, D_MODEL), lambda i: (0, 0))
    acc8 = pl.BlockSpec((8, D_MODEL), lambda i: (0, 0))
    acc_shape = jax.ShapeDtypeStruct((8, D_MODEL), F32)
    return _pc(body, name="out_fwd_bwd",
               out_shape=(jax.ShapeDtypeStruct((t, D_MODEL), F32),
                          jax.ShapeDtypeStruct((t, D_MODEL), BF16),
                          jax.ShapeDtypeStruct((t, D_MODEL), BF16),
                          acc_shape, acc_shape, acc_shape),
               grid=(t // tm,),
               in_specs=[pl.BlockSpec((tm, D_FF), lambda i: (i, 0)), tile, tile,
                         pl.BlockSpec((tm, PLE_DIM), lambda i: (i, 0)), tile,
                         pl.BlockSpec((D_FF, D_MODEL), lambda i: (0, 0)),
                         pl.BlockSpec((D_MODEL, D_MODEL), lambda i: (0, 0)),
                         pl.BlockSpec((PLE_DIM, D_MODEL), lambda i: (0, 0)),
                         vec, vec, vec, vec],
               out_specs=(tile, tile, tile, acc8, acc8, acc8),
               sem=("arbitrary",))(act, x1b, r1, p2, tgt, wd, wpg, wpp, g1, b1, g2, b2)


def _ffn_bwd(h2, dr2, wd, wup_st, dpg, wpg, r1, g1, convw, tm):
    t = r1.shape[0]
    ni = t // tm
    nth = tm // HALO
    last_halo = t // HALO - 1
    main_rows = slice(0, tm)

    def body(g_ref, gc_ref, gcn_ref, v_ref, vn_ref, dr2_ref, dr2n_ref, wd_ref, wug_ref, wuv_ref,
             cw_ref, dpg_ref, wpg_ref, r1_ref, g1_ref,
             dh2_ref, dr1_ref, dcw_ref, dcb_ref, dg1_ref, db1_ref, acc):
        i = pl.program_id(0)
        j = pl.program_id(1)

        @pl.when((i == 0) & (j == 0))
        def _():
            dcw_ref[...] = jnp.zeros_like(dcw_ref)
            dcb_ref[...] = jnp.zeros_like(dcb_ref)
            dg1_ref[...] = jnp.zeros_like(dg1_ref)
            db1_ref[...] = jnp.zeros_like(db1_ref)

        dr2v = dr2_ref[...].astype(BF16)
        dr2n = dr2n_ref[...].astype(BF16)
        more = (i < ni - 1).astype(F32)
        prod = None
        dcw_parts, dcb_parts = [], []
        for c0, c1 in FF_PIECES:
            pc = slice(c0, c1)
            da = _dot(dr2v, wd_ref[pc, :], NT)
            dnext = _dot(dr2n, wd_ref[pc, :], NT) * more
            gc = jnp.concatenate([gc_ref[:, pc].astype(F32), gcn_ref[:, pc].astype(F32)], axis=0)
            vext = jnp.concatenate([v_ref[:, pc].astype(F32), vn_ref[:, pc].astype(F32)], axis=0)
            dext = jnp.concatenate([da, dnext], axis=0)
            gl, dgl = _gelu_and_grad(gc)
            d_gc = dext * vext * dgl
            up1 = _shift_up(d_gc, 1)[main_rows, :]
            up2 = _shift_up(d_gc, 2)[main_rows, :]
            dm = d_gc[main_rows, :]
            d_gate = (cw_ref[2:3, pc] * dm + cw_ref[1:2, pc] * up1 + cw_ref[0:1, pc] * up2).astype(BF16)
            d_val = (da * gl[main_rows, :]).astype(BF16)
            dh2_ref[0, :, pc] = d_gate
            dh2_ref[1, :, pc] = d_val
            g = g_ref[:, pc].astype(F32)
            s0 = jnp.sum(g * up2, axis=0, keepdims=True)
            s1 = jnp.sum(g * up1, axis=0, keepdims=True)
            s2 = jnp.sum(g * dm, axis=0, keepdims=True)
            rowid = lax.broadcasted_iota(jnp.int32, (8, c1 - c0), 0)
            dcw_parts.append(jnp.where(rowid == 0, s0, jnp.where(rowid == 1, s1,
                                                                 jnp.where(rowid == 2, s2, 0.0))))
            dcb_parts.append(_colsum8(dm))
            part = _dot(d_gate, wug_ref[:, pc], NT) + _dot(d_val, wuv_ref[:, pc], NT)
            prod = part if prod is None else prod + part
        dcw_part = jnp.concatenate(dcw_parts, axis=1)
        dcb_part = jnp.concatenate(dcb_parts, axis=1)
        for jj in range(FF_NJ):
            @pl.when(j == jj)
            def _(jj=jj):
                cols = slice(jj * FF_TILE, (jj + 1) * FF_TILE)
                dcw_ref[:, cols] += dcw_part
                dcb_ref[:, cols] += dcb_part

        @pl.when(j == 0)
        def _():
            acc[...] = prod

        @pl.when(j > 0)
        def _():
            acc[...] += prod

        @pl.when(j == FF_NJ - 1)
        def _():
            d_x1 = acc[...] + _dot(dpg_ref[...], wpg_ref[...], NT) + ALPHA * dr2_ref[...]
            xh, rstd = _ln_stats(r1_ref[...])
            dg1_ref[...] += _colsum8(d_x1 * xh)
            db1_ref[...] += _colsum8(d_x1)
            dr1_ref[...] = _ln_bwd(d_x1 * g1_ref[...], xh, rstd)

    def h2_main(part):
        return pl.BlockSpec((None, tm, FF_TILE), lambda i, j: (part, i, j))

    def h2_next(part):
        return pl.BlockSpec((None, HALO, FF_TILE),
                            lambda i, j: (part, jnp.minimum((i + 1) * nth, last_halo), j))

    tile = pl.BlockSpec((tm, D_MODEL), lambda i, j: (i, 0))
    acc8 = pl.BlockSpec((8, D_MODEL), lambda i, j: (0, 0))
    accff = pl.BlockSpec((8, D_FF), lambda i, j: (0, 0))
    acc_shape = jax.ShapeDtypeStruct((8, D_MODEL), F32)
    accff_shape = jax.ShapeDtypeStruct((8, D_FF), F32)
    return _pc(body, name="ffn_bwd",
               out_shape=(jax.ShapeDtypeStruct((2, t, D_FF), BF16),
                          jax.ShapeDtypeStruct((t, D_MODEL), F32),
                          accff_shape, accff_shape, acc_shape, acc_shape),
               grid=(ni, FF_NJ),
               in_specs=[h2_main(0), h2_main(2), h2_next(2), h2_main(1), h2_next(1),
                         tile,
                         pl.BlockSpec((HALO, D_MODEL), lambda i, j: (jnp.minimum((i + 1) * nth, last_halo), 0)),
                         pl.BlockSpec((FF_TILE, D_MODEL), lambda i, j: (j, 0)),
                         pl.BlockSpec((None, D_MODEL, FF_TILE), lambda i, j: (j, 0, 0)),
                         pl.BlockSpec((None, D_MODEL, FF_TILE), lambda i, j: (j + FF_NJ, 0, 0)),
                         pl.BlockSpec((3, FF_TILE), lambda i, j: (0, j)),
                         tile, pl.BlockSpec((D_MODEL, D_MODEL), lambda i, j: (0, 0)),
                         tile, pl.BlockSpec((1, D_MODEL), lambda i, j: (0, 0))],
               out_specs=(pl.BlockSpec((2, tm, FF_TILE), lambda i, j: (0, i, j)),
                          tile, accff, accff, acc8, acc8),
               scratch=[pltpu.VMEM((tm, D_MODEL), F32)],
               sem=("arbitrary", "arbitrary"))(h2, h2, h2, h2, h2, dr2, dr2, wd, wup_st, wup_st,
                                               convw, dpg, wpg, r1, g1)


ANY = pl.BlockSpec(memory_space=pl.ANY)


def _chip_peers():
    x, y, c = lax.axis_index("x"), lax.axis_index("y"), lax.axis_index("c")
    return x, y, c, [(1 - x, y), (x, 1 - y), (1 - x, 1 - y)]


def _gather_comm(halved, whole=()):
    n, nw = len(halved), len(whole)

    def copies(ins, outs, sems):
        ici_send, ici_recv, d2d_send, d2d_recv, own_send, own_recv = sems
        x, y, c, peers = _chip_peers()
        me = 2 * x + y
        sibling = (x, y, 1 - c)
        own, ici, ici_wait, fwd, fwd_wait = [], [], [], [], []
        for ti in range(n + nw):
            src, dst = ins[ti], outs[ti]
            own.append(pltpu.make_async_remote_copy(
                src_ref=src, dst_ref=dst.at[me], send_sem=own_send.at[ti], recv_sem=own_recv.at[ti],
                device_id=sibling, device_id_type=MESH))
            for k, (px, py) in enumerate(peers):
                pk = 2 * px + py
                sem = dict(send_sem=ici_send.at[ti * 3 + k], recv_sem=ici_recv.at[ti * 3 + k],
                           device_id=(px, py, c), device_id_type=MESH)
                if ti < n:
                    ici.append(pltpu.make_async_remote_copy(src_ref=src.at[c], dst_ref=dst.at[me, c], **sem))
                    ici_wait.append(pltpu.make_async_remote_copy(src_ref=src.at[c], dst_ref=dst.at[pk, c], **sem))
                    dsem = dict(send_sem=d2d_send.at[ti * 3 + k], recv_sem=d2d_recv.at[ti * 3 + k],
                                device_id=sibling, device_id_type=MESH)
                    fwd.append(pltpu.make_async_remote_copy(src_ref=dst.at[pk, c], dst_ref=dst.at[pk, c], **dsem))
                    fwd_wait.append(pltpu.make_async_remote_copy(
                        src_ref=dst.at[pk, 1 - c], dst_ref=dst.at[pk, 1 - c], **dsem))
                else:
                    ici.append(pltpu.make_async_remote_copy(src_ref=src, dst_ref=dst.at[me], **sem))
                    ici_wait.append(pltpu.make_async_remote_copy(src_ref=src, dst_ref=dst.at[pk], **sem))
        return own, ici, ici_wait, fwd, fwd_wait

    def start(ins, outs, sems):
        own, ici, _, _, _ = copies(ins, outs, sems)
        for cp in own + ici:
            cp.start()

    def finish(ins, outs, sems):
        own, ici, ici_wait, fwd, fwd_wait = copies(ins, outs, sems)
        for i, cp in enumerate(ici_wait):
            cp.wait_recv()
            if i < len(fwd):
                fwd[i].start()
        for cp in fwd_wait + own:
            cp.wait_recv()
        for cp in own + ici + fwd:
            cp.wait_send()

    def middle(ins, outs, sems):
        _, _, ici_wait, fwd, _ = copies(ins, outs, sems)
        for i, cp in enumerate(ici_wait):
            cp.wait_recv()
            if i < len(fwd):
                fwd[i].start()

    def finish_late(ins, outs, sems):
        own, ici, _, fwd, fwd_wait = copies(ins, outs, sems)
        for cp in fwd_wait + own:
            cp.wait_recv()
        for cp in own + ici + fwd:
            cp.wait_send()

    srcs = list(halved) + list(whole)
    return _Comm(srcs, [jax.ShapeDtypeStruct((N_CHIP,) + s.shape, s.dtype) for s in srcs],
                 [pltpu.SemaphoreType.DMA((3 * (n + nw),)), pltpu.SemaphoreType.DMA((3 * (n + nw),)),
                  pltpu.SemaphoreType.DMA((max(3 * n, 1),)), pltpu.SemaphoreType.DMA((max(3 * n, 1),)),
                  pltpu.SemaphoreType.DMA((n + nw,)), pltpu.SemaphoreType.DMA((n + nw,))],
                 start, finish, middle, finish_late)


def _sibling_exchange_comm(grads):
    n = len(grads)

    def copies(ins, outs, sems):
        send_sems, recv_sems = sems
        x, y, c = lax.axis_index("x"), lax.axis_index("y"), lax.axis_index("c")
        res = []
        for ti in range(n):
            half = ins[ti].shape[1] // 2
            res.append(pltpu.make_async_remote_copy(
                src_ref=ins[ti].at[:, pl.ds(pl.multiple_of((1 - c) * half, 16), half), :],
                dst_ref=outs[ti],
                send_sem=send_sems.at[ti], recv_sem=recv_sems.at[ti],
                device_id=(x, y, 1 - c), device_id_type=MESH))
        return res

    def start(ins, outs, sems):
        for cp in copies(ins, outs, sems):
            cp.start()

    def finish(ins, outs, sems):
        for cp in copies(ins, outs, sems):
            cp.wait()

    return _Comm(grads, [jax.ShapeDtypeStruct((N_CHIP, g.shape[1] // 2, g.shape[2]), g.dtype) for g in grads],
                 [pltpu.SemaphoreType.DMA((n,)), pltpu.SemaphoreType.DMA((n,))], start, finish)


def _in_proj_gathering(x2b, own, chip, tm, comm):
    t = x2b.shape[0]
    ni = t // tm
    half, cols = own.shape[1], own.shape[2]
    nci, nco = len(comm.ins), len(comm.out_shapes)

    def body(chip_ref, x_ref, own_ref, own_hbm, *rest):
        c_in = rest[:nci]
        h_ref, win_out = rest[nci:nci + 2]
        c_out = rest[nci + 2:nci + 2 + nco]
        w_scr, ici_send, ici_recv, d2d_send, d2d_recv, own_sems, ld_sems = rest[nci + 2 + nco:nci + 9 + nco]
        c_sem = rest[nci + 9 + nco:]
        s, i = pl.program_id(0), pl.program_id(1)
        x, y, c, peers = _chip_peers()
        me = 2 * x + y
        sibling = (x, y, 1 - c)

        def ici(k, slot):
            px, py = peers[k]
            return pltpu.make_async_remote_copy(
                src_ref=own_hbm.at[c], dst_ref=win_out.at[slot, c],
                send_sem=ici_send.at[k], recv_sem=ici_recv.at[k],
                device_id=(px, py, c), device_id_type=MESH)

        def forward(k, core):
            pk = 2 * peers[k][0] + peers[k][1]
            return pltpu.make_async_remote_copy(
                src_ref=win_out.at[pk, core], dst_ref=win_out.at[pk, core],
                send_sem=d2d_send.at[k], recv_sem=d2d_recv.at[k],
                device_id=sibling, device_id_type=MESH)

        place_own = pltpu.make_async_remote_copy(
            src_ref=own_hbm, dst_ref=win_out.at[me], send_sem=own_sems.at[0], recv_sem=own_sems.at[1],
            device_id=sibling, device_id_type=MESH)

        @pl.when((s == 0) & (i == 0))
        def _():
            for k in range(2):
                ici(k, me).start()
            place_own.start()

        @pl.when(s == 0)
        def _():
            xv = x_ref[...]
            h_ref[...] = (_dot(xv[:, :half], own_ref[0]) + _dot(xv[:, half:], own_ref[1])).astype(BF16)

        for k in range(3):
            @pl.when((s == k + 1) & (i == 0))
            def _(k=k):
                pk = 2 * peers[k][0] + peers[k][1]
                ici(k, pk).wait_recv()
                if k == 0:
                    ici(2, me).start()
                forward(k, c).start()
                forward(k, 1 - c).wait_recv()
                loads = [pltpu.make_async_copy(win_out.at[pk, hh], w_scr.at[hh], ld_sems.at[hh])
                         for hh in range(2)]
                for ld in loads:
                    ld.start()
                for ld in loads:
                    ld.wait()
                if k == 1:
                    comm.start(c_in, c_out, c_sem)

        @pl.when(s > 0)
        def _():
            xv = x_ref[...]
            h_ref[...] = (_dot(xv[:, :half], w_scr[0]) + _dot(xv[:, half:], w_scr[1])).astype(BF16)

        @pl.when((s == N_CHIP - 1) & (i == ni - 1))
        def _():
            place_own.wait()
            for k in range(3):
                ici(k, me).wait_send()
                forward(k, c).wait_send()
            comm.finish(c_in, c_out, c_sem)

    def shard_col(s, me):
        return jnp.where(s == 0, me, me ^ jnp.where(s == 1, 2, jnp.where(s == 2, 1, 3)))

    res = _pc(body, name="in_proj",
              out_shape=(jax.ShapeDtypeStruct((t, N_CHIP * cols), BF16),
                         jax.ShapeDtypeStruct((N_CHIP,) + own.shape, own.dtype)) + tuple(comm.out_shapes),
              grid=(N_CHIP, ni), nsp=1,
              in_specs=[pl.BlockSpec((tm, 2 * half), lambda s, i, chip_ref: (i, 0)),
                        pl.BlockSpec(own.shape, lambda s, i, chip_ref: (0, 0, 0)),
                        ANY] + [ANY] * nci,
              out_specs=(pl.BlockSpec((tm, cols), lambda s, i, chip_ref: (i, shard_col(s, chip_ref[0]))),
                         ANY) + tuple([ANY] * nco),
              scratch=[pltpu.VMEM(own.shape, own.dtype),
                       pltpu.SemaphoreType.DMA((3,)), pltpu.SemaphoreType.DMA((3,)),
                       pltpu.SemaphoreType.DMA((3,)), pltpu.SemaphoreType.DMA((3,)),
                       pltpu.SemaphoreType.DMA((2,)), pltpu.SemaphoreType.DMA((2,))] + comm.sems,
              sem=("arbitrary", "arbitrary"))(chip, x2b, own, own, *comm.ins)
    return res[0], res[1], res[2:]


def _rs_add_halves(name, grad, recv, core):
    _, r, cdim = grad.shape
    half = r // 2
    tr = _row_tile(half, cdim, mult=16)
    nr = half // tr

    def body(c_ref, g_ref, r_ref, o_ref):
        o_ref[...] = (g_ref[...].astype(F32) + r_ref[...].astype(F32)).astype(BF16)

    return _pc(body, name=name, out_shape=jax.ShapeDtypeStruct((N_CHIP, half, cdim), BF16),
               grid=(N_CHIP, nr), nsp=1,
               in_specs=[pl.BlockSpec((None, tr, cdim), lambda j, i, c_ref: (j, c_ref[0] * nr + i, 0)),
                         pl.BlockSpec((None, tr, cdim), lambda j, i, c_ref: (j, i, 0))],
               out_specs=pl.BlockSpec((None, tr, cdim), lambda j, i, c_ref: (j, i, 0)),
               sem=("parallel", "parallel"))(core, grad, recv)


def _chip_exchange_comm(parts):
    n = len(parts)

    def copies(ins, outs, sems):
        send_sems, recv_sems = sems
        x, y, c, peers = _chip_peers()
        return [pltpu.make_async_remote_copy(
            src_ref=ins[ti].at[2 * px + py], dst_ref=outs[ti].at[k],
            send_sem=send_sems.at[ti * 3 + k], recv_sem=recv_sems.at[ti * 3 + k],
            device_id=(px, py, c), device_id_type=MESH)
            for ti in range(n) for k, (px, py) in enumerate(peers)]

    def start(ins, outs, sems):
        for cp in copies(ins, outs, sems):
            cp.start()

    def finish(ins, outs, sems):
        for cp in copies(ins, outs, sems):
            cp.wait()

    return _Comm(parts, [jax.ShapeDtypeStruct((3,) + p.shape[1:], p.dtype) for p in parts],
                 [pltpu.SemaphoreType.DMA((3 * n,)), pltpu.SemaphoreType.DMA((3 * n,))], start, finish)


def _rs_sum_chips(name, part, recv, chip):
    _, half, cdim = recv.shape
    tr = _row_tile(half, cdim, mult=16)

    def body(chip_ref, p_ref, r_ref, o_ref):
        o_ref[...] = ((p_ref[...].astype(F32) + r_ref[0].astype(F32)) + r_ref[1].astype(F32)
                      ) + r_ref[2].astype(F32)

    return _pc(body, name=name, out_shape=jax.ShapeDtypeStruct((half, cdim), F32),
               grid=(half // tr,), nsp=1,
               in_specs=[pl.BlockSpec((None, tr, cdim), lambda i, chip_ref: (chip_ref[0], i, 0)),
                         pl.BlockSpec((3, tr, cdim), lambda i, chip_ref: (0, i, 0))],
               out_specs=pl.BlockSpec((tr, cdim), lambda i, chip_ref: (i, 0)),
               sem=("parallel",))(chip, part, recv)


def _rs_send_halves(halves):
    n = len(halves)

    def body(*refs):
        ins, outs = refs[:n], refs[n:2 * n]
        send_sems, recv_sems = refs[2 * n:]
        x, y, c = lax.axis_index("x"), lax.axis_index("y"), lax.axis_index("c")
        sends = []
        for ti in range(n):
            cp = pltpu.make_async_remote_copy(
                src_ref=ins[ti], dst_ref=outs[ti],
                send_sem=send_sems.at[ti], recv_sem=recv_sems.at[ti],
                device_id=(x, y, 1 - c), device_id_type=MESH)
            cp.start()
            sends.append(cp)
        for cp in sends:
            cp.wait()

    return _pc(body, name="rs_send_halves",
               out_shape=tuple(jax.ShapeDtypeStruct(hv.shape, hv.dtype) for hv in halves),
               in_specs=[ANY] * n, out_specs=tuple([ANY] * n),
               scratch=[pltpu.SemaphoreType.DMA((n,)), pltpu.SemaphoreType.DMA((n,))])(*halves)


def _adamw_rows(name, mine, theirs, w, m, v, core):
    half, cdim = mine.shape
    tr = _row_tile(half, cdim, budget=1 << 19)
    nrh = half // tr

    def body(c_ref, mine_ref, theirs_ref, w_ref, m_ref, v_ref, g_ref, d_ref, m2_ref, v2_ref):
        is_mine = (pl.program_id(0) // nrh) == c_ref[0]
        g = jnp.where(is_mine, mine_ref[...], theirs_ref[...])
        d, m2, v2 = _adamw(w_ref[...], g, m_ref[...], v_ref[...])
        g_ref[...] = g
        d_ref[...] = d
        m2_ref[...] = m2
        v2_ref[...] = v2

    htile = pl.BlockSpec((tr, cdim), lambda i, c_ref: (i % nrh, 0))
    tile = pl.BlockSpec((tr, cdim), lambda i, c_ref: (i, 0))
    shp = jax.ShapeDtypeStruct((2 * half, cdim), F32)
    return _pc(body, name=name, out_shape=(shp, shp, shp, shp), grid=(2 * nrh,), nsp=1,
               in_specs=[htile, htile, tile, tile, tile], out_specs=(tile, tile, tile, tile),
               sem=("parallel",))(core, mine, theirs, w, m, v)


def _adamw_whole(name, g, w, m, v):
    def body(g_ref, w_ref, m_ref, v_ref, d_ref, m2_ref, v2_ref):
        d, m2, v2 = _adamw(w_ref[...], g_ref[...], m_ref[...], v_ref[...])
        d_ref[...] = d
        m2_ref[...] = m2
        v2_ref[...] = v2

    shp = jax.ShapeDtypeStruct(g.shape, F32)
    return _pc(body, name=name, out_shape=(shp, shp, shp))(g, w, m, v)


SMALL_LAYOUT = (
    ("sgu_w_s", 1024, 1, 0),
    ("sgu_b_s", 8, 1, 1024),
    ("sgu_norm_g", 1, 0, 0),
    ("sgu_norm_b", 1, 0, 1),
    ("hgrn_norm_g", 1, 0, 3),
    ("ln1_g", 1, 0, 4),
    ("ln1_b", 1, 0, 5),
    ("ffn_conv_b", 1, 2, 3),
    ("ln2_g", 1, 0, 6),
    ("ln2_b", 1, 0, 7),
)
LB_ROW = 2
LOSS_ROW = 8
PACK_SHAPES = ((16, D_MODEL), (N_GROUP * 128 + 16, 128), (8, D_FF))
GATH_DTYPES = (F32, BF16, F32)


def _small_allreduce_adamw(rows1024, dws, dbs, dcw, dcb, logits, m_logits, v_logits,
                           small_w, small_m, small_v):
    ns = len(SMALL_LAYOUT)
    nr = len(rows1024)
    nb = len(PACK_SHAPES)

    def body(*refs):
        row_refs = refs[:nr]
        dws_ref, dbs_ref, dcw_ref, dcb_ref, lg_ref, mlg_ref, vlg_ref = refs[nr:nr + 7]
        pos = nr + 7
        w_refs = refs[pos:pos + ns]
        m_refs = refs[pos + ns:pos + 2 * ns]
        v_refs = refs[pos + 2 * ns:pos + 3 * ns]
        pos += 3 * ns
        loss_ref, dcw_out = refs[pos:pos + 2]
        lg_outs = refs[pos + 2:pos + 6]
        pos += 6
        outs = refs[pos:pos + 4 * ns]
        pos += 4 * ns
        pack = refs[pos:pos + nb]
        sib = refs[pos + nb:pos + 2 * nb]
        gath = refs[pos + 2 * nb:pos + 3 * nb]
        d2d_send, d2d_recv, ici_send, ici_recv = refs[pos + 3 * nb:]

        x, y, c, peers = _chip_peers()
        me = 2 * x + y
        sibling = (x, y, 1 - c)

        pack[0][...] = jnp.zeros(PACK_SHAPES[0], F32)
        for k in range(nr):
            pack[0][k:k + 1, :] = row_refs[k][0:1, :]
        pack[1][0:N_GROUP * 128, :] = dws_ref[...]
        pack[1][N_GROUP * 128:N_GROUP * 128 + 8, :] = dbs_ref[...]
        pack[1][N_GROUP * 128 + 8:, :] = jnp.zeros((8, 128), F32)
        pack[2][...] = jnp.zeros(PACK_SHAPES[2], F32)
        pack[2][0:3, :] = dcw_ref[0:3, :]
        pack[2][3:4, :] = dcb_ref[0:1, :]

        d2d = [pltpu.make_async_remote_copy(
            src_ref=pack[b], dst_ref=sib[b], send_sem=d2d_send.at[b], recv_sem=d2d_recv.at[b],
            device_id=sibling, device_id_type=MESH) for b in range(nb)]
        for cp in d2d:
            cp.start()
        for cp in d2d:
            cp.wait()
        for b in range(nb):
            gath[b][me] = (pack[b][...] + sib[b][...]).astype(GATH_DTYPES[b])

        ici, ici_wait = [], []
        for b in range(nb):
            for k, (px, py) in enumerate(peers):
                sem = dict(send_sem=ici_send.at[b * 3 + k], recv_sem=ici_recv.at[b * 3 + k],
                           device_id=(px, py, c), device_id_type=MESH)
                ici.append(pltpu.make_async_remote_copy(src_ref=gath[b].at[me], dst_ref=gath[b].at[me], **sem))
                ici_wait.append(pltpu.make_async_remote_copy(
                    src_ref=gath[b].at[me], dst_ref=gath[b].at[2 * px + py], **sem))
        for cp in ici:
            cp.start()
        for cp in ici_wait:
            cp.wait_recv()
        for cp in ici:
            cp.wait_send()

        tot = pack
        for b in range(nb):
            tot[b][...] = ((gath[b][0].astype(F32) + gath[b][1].astype(F32)) + gath[b][2].astype(F32)
                           ) + gath[b][3].astype(F32)

        loss_ref[...] = tot[0][LOSS_ROW:LOSS_ROW + 1, :]
        dcw_out[...] = tot[2][...]
        lb = _sig(lg_ref[0:1, :] - lg_ref[1:2, :])
        d0 = tot[0][LB_ROW:LB_ROW + 1, :] * lb * (1.0 - lb)
        rowid = lax.broadcasted_iota(jnp.int32, (2, D_MODEL), 0)
        g_lg = jnp.where(rowid == 0, d0, -d0)
        dl, ml, vl = _adamw(lg_ref[...], g_lg, mlg_ref[...], vlg_ref[...])
        lg_outs[0][...] = g_lg
        lg_outs[1][...] = dl
        lg_outs[2][...] = ml
        lg_outs[3][...] = vl
        for si, (_, rows, b, r0) in enumerate(SMALL_LAYOUT):
            g = tot[b][r0:r0 + rows, :]
            dl, ml, vl = _adamw(w_refs[si][...], g, m_refs[si][...], v_refs[si][...])
            outs[4 * si][...] = g
            outs[4 * si + 1][...] = dl
            outs[4 * si + 2][...] = ml
            outs[4 * si + 3][...] = vl

    shapes = [jax.ShapeDtypeStruct((1, D_MODEL), F32), jax.ShapeDtypeStruct((8, D_FF), F32)]
    shapes += [jax.ShapeDtypeStruct((2, D_MODEL), F32)] * 4
    for w in small_w:
        shapes += [jax.ShapeDtypeStruct(w.shape, F32)] * 4
    scratch = [pltpu.VMEM(shp, F32) for shp in PACK_SHAPES]
    scratch += [pltpu.VMEM(shp, F32) for shp in PACK_SHAPES]
    scratch += [pltpu.VMEM((N_CHIP,) + shp, dt) for shp, dt in zip(PACK_SHAPES, GATH_DTYPES)]
    scratch += [pltpu.SemaphoreType.DMA((nb,)), pltpu.SemaphoreType.DMA((nb,)),
                pltpu.SemaphoreType.DMA((3 * nb,)), pltpu.SemaphoreType.DMA((3 * nb,))]
    vm = pl.BlockSpec(memory_space=pltpu.VMEM)
    n_in = nr + 7 + 3 * ns
    res = _pc(body, name="small_allreduce_adamw", out_shape=tuple(shapes),
              in_specs=[vm] * n_in, out_specs=tuple([vm] * len(shapes)),
              scratch=scratch)(*rows1024, dws, dbs, dcw, dcb, logits, m_logits, v_logits,
                               *small_w, *small_m, *small_v)
    return res[0], res[1], res[2:6], res[6:]


def kernel(x, p, w_in, sgu_w_s, sgu_b_s, sgu_norm_g, sgu_norm_b, hgrn_lb_logits, hgrn_norm_g, w_branch, w_out, ln1_g, ln1_b, ffn_w_up, ffn_conv_w, ffn_conv_b, ffn_w_down, ln2_g, ln2_b, ple_w_proj, ple_w_gate, loss_target, m_w_in, m_sgu_w_s, m_sgu_b_s, m_sgu_norm_g, m_sgu_norm_b, m_hgrn_lb_logits, m_hgrn_norm_g, m_w_branch, m_w_out, m_ln1_g, m_ln1_b, m_ffn_w_up, m_ffn_conv_w, m_ffn_conv_b, m_ffn_w_down, m_ln2_g, m_ln2_b, m_ple_w_proj, m_ple_w_gate, v_w_in, v_sgu_w_s, v_sgu_b_s, v_sgu_norm_g, v_sgu_norm_b, v_hgrn_lb_logits, v_hgrn_norm_g, v_w_branch, v_w_out, v_ln1_g, v_ln1_b, v_ffn_w_up, v_ffn_conv_w, v_ffn_conv_b, v_ffn_w_down, v_ln2_g, v_ln2_b, v_ple_w_proj, v_ple_w_gate):
    t = x.shape[1]
    x2 = x.reshape(t, D_MODEL)
    x2b = x2.astype(BF16)
    p2 = p.reshape(t, PLE_DIM)
    tgt = loss_target.reshape(t, D_MODEL)
    core = lax.axis_index("c").astype(jnp.int32).reshape(1)
    chip_id = (2 * lax.axis_index("x") + lax.axis_index("y")).astype(jnp.int32).reshape(1)

    big_w = [w_in[0], w_branch[0, 0], w_branch[0, 1], w_out[0], ffn_w_up[0], ffn_w_down[0],
             ple_w_proj[0], ple_w_gate[0]]
    big_m = [m_w_in[0], m_w_branch[0, 0], m_w_branch[0, 1], m_w_out[0], m_ffn_w_up[0],
             m_ffn_w_down[0], m_ple_w_proj[0], m_ple_w_gate[0]]
    big_v = [v_w_in[0], v_w_branch[0, 0], v_w_branch[0, 1], v_w_out[0], v_ffn_w_up[0],
             v_ffn_w_down[0], v_ple_w_proj[0], v_ple_w_gate[0]]
    def halves_of(i):
        w = big_w[i]
        return w.astype(BF16).reshape(2, w.shape[0] // 2, w.shape[1])

    def stacked(g, i):
        return g.reshape(N_CHIP, big_w[i].shape[0], big_w[i].shape[1])


    cid = jnp.arange(SGU_BLOCK) // CHUNK
    maskf = (cid[:, None] >= cid[None, :]).astype(F32)
    ws_masked = sgu_w_s[0] * maskf[None]
    wm = ws_masked.astype(BF16)
    wmt = jnp.transpose(ws_masked, (0, 2, 1)).astype(BF16)
    bsb = jnp.broadcast_to(sgu_b_s[0][:, :, None], (N_GROUP, SGU_BLOCK, 128))

    up_rows = big_w[4].shape[0] // 2
    up_blocks = [big_w[4][k * up_rows:(k + 1) * up_rows].astype(BF16).reshape(2, up_rows // 2, -1)
                 for k in range(2)]
    h, win_g, (up0_g,) = _in_proj_gathering(x2b, halves_of(0), chip_id, 512, _gather_comm([up_blocks[0]]))
    win_st = stacked(win_g, 0)
    ya, _ = _sgu_fwd(h, wm, bsb, sgu_norm_g, sgu_norm_b)
    (yb, o_all, st_all), mix_g = _hgrn_fwd(
        h, hgrn_lb_logits, hgrn_norm_g,
        comm=_gather_comm([halves_of(i) for i in (1, 2, 3)] + [up_blocks[1]], [ffn_conv_w[0]]))
    wb0, wb1, wo = [stacked(g, i).reshape(D_MODEL, D_MODEL) for g, i in zip(mix_g[:3], (1, 2, 3))]
    wup_st = jnp.concatenate([g.reshape(N_CHIP, up_rows, -1) for g in (up0_g, mix_g[3])], axis=1)
    convw = jnp.transpose(mix_g[4], (1, 0, 2)).reshape(3, D_FF)
    (r1, a_br, b_br, m_bf, x1b), _ = _mix_fwd(ya, yb, h, x2, wb0, wb1, wo, ln1_g, ln1_b, 256)
    h2, act, out_g = _ffn_up_act(x1b, wup_st, convw, ffn_conv_b, 512,
                                 _gather_comm([halves_of(i) for i in (5, 6, 7)]))
    wd = stacked(out_g[0], 5).reshape(D_FF, D_MODEL)
    wpp = jnp.transpose(stacked(out_g[1], 6), (1, 0, 2)).reshape(PLE_DIM, D_MODEL)
    wpg = stacked(out_g[2], 7).reshape(D_MODEL, D_MODEL)
    dr2, dpg, dpp, loss_acc, dg2, db2 = _out_fwd_bwd(
        act, x1b, r1, p2, tgt, wd, wpg, wpp, ln1_g, ln1_b, ln2_g, ln2_b, 256)

    dh2, dr1, dcw, dcb, dg1, db1 = _ffn_bwd(h2, dr2, wd, wup_st, dpg, wpg, r1, ln1_g, convw, 256)
    d_wd = _mm_tn("ffn_down_wgrad", act, dr2, FF_TILE, 512)
    d_wpg = _mm_tn("ple_gate_wgrad", x1b, dpg, 512, D_MODEL)
    d_wpp_st = _mm_tn("ple_proj_wgrad", p2, dpp, PLE_DIM, PLE_DIM, stacked=True)
    d_wup_st = _mm("ffn_up_wgrad", x1b, dh2, TN, (2, N_CHIP),
                   pl.BlockSpec((t, 512), lambda i, j: (0, i)),
                   pl.BlockSpec((None, t, FF_TILE), lambda i, j: (j // FF_NJ, 0, j % FF_NJ)),
                   jax.ShapeDtypeStruct((N_CHIP, D_MODEL, FF_TILE), BF16),
                   pl.BlockSpec((None, 512, FF_TILE), lambda i, j: (j, i, 0)))
    da_bf, db_bf, dh, dya, dyb = _mix_bwd(dr1, h, a_br, b_br, wo, wb0, wb1, 256)
    d_wo = _mm_tn("out_proj_wgrad", m_bf, dr1, 512, 512)
    d_wb0 = _mm_tn("branch0_wgrad", ya, da_bf, 512, D_MODEL)
    d_wb1 = _mm_tn("branch1_wgrad", yb, db_bf, 512, D_MODEL)
    grads_1 = [d_wb0.reshape(4, 256, D_MODEL), d_wb1.reshape(4, 256, D_MODEL),
               d_wo.reshape(4, 256, D_MODEL), d_wup_st, d_wd.reshape(4, D_FF // 4, D_MODEL),
               d_wpp_st, d_wpg.reshape(4, 256, D_MODEL)]
    (dh, dws, dbs, dgv, dbv), recv_a1 = _sgu_bwd(h, dya, wm, wmt, bsb, sgu_norm_g, sgu_norm_b, maskf, dh,
                                                 comm=_sibling_exchange_comm(grads_1))
    parts_1 = [_rs_add_halves("rs_add_halves%d" % (i + 1), g, r, core)
               for i, (g, r) in enumerate(zip(grads_1, recv_a1))]
    (dh, dlb, dgn), recv_b1 = _hgrn_bwd(h, o_all, dyb, st_all, hgrn_lb_logits, hgrn_norm_g, dh,
                                         comm=_chip_exchange_comm(parts_1))

    grads_0 = [_in_proj_wgrad(x2b, dh, 512, D_MODEL)]
    recv_a0 = _run_comm("rs_sibling_exchange0", _sibling_exchange_comm(grads_0))
    parts_0 = [_rs_add_halves("rs_add_halves0", grads_0[0], recv_a0[0], core)]
    gx, recv_b0 = _in_proj_xgrad(dh, win_st, dr1, 512, _chip_exchange_comm(parts_0))
    parts = parts_0 + parts_1
    recv_b = list(recv_b0) + list(recv_b1)
    halves = [_rs_sum_chips("rs_sum_chips%d" % i, pt, r, chip_id)
              for i, (pt, r) in enumerate(zip(parts, recv_b))]
    theirs = _rs_send_halves(halves)
    big_out = [_adamw_rows("adamw_big%d" % i, halves[i], theirs[i], big_w[i], big_m[i], big_v[i], core)
               for i in range(len(halves))]

    small_in = dict(sgu_w_s=(sgu_w_s, m_sgu_w_s, v_sgu_w_s), sgu_b_s=(sgu_b_s, m_sgu_b_s, v_sgu_b_s),
                    sgu_norm_g=(sgu_norm_g, m_sgu_norm_g, v_sgu_norm_g),
                    sgu_norm_b=(sgu_norm_b, m_sgu_norm_b, v_sgu_norm_b),
                    hgrn_norm_g=(hgrn_norm_g, m_hgrn_norm_g, v_hgrn_norm_g),
                    ln1_g=(ln1_g, m_ln1_g, v_ln1_g), ln1_b=(ln1_b, m_ln1_b, v_ln1_b),
                    ffn_conv_b=(ffn_conv_b, m_ffn_conv_b, v_ffn_conv_b),
                    ln2_g=(ln2_g, m_ln2_g, v_ln2_g), ln2_b=(ln2_b, m_ln2_b, v_ln2_b))

    def flat(name, arr):
        rows = dict((n, r) for n, r, _, _ in SMALL_LAYOUT)[name]
        return arr.reshape(rows, arr.size // rows)

    names = [n for n, _, _, _ in SMALL_LAYOUT]
    sw = [flat(n, small_in[n][0]) for n in names]
    sm = [flat(n, small_in[n][1]) for n in names]
    sv = [flat(n, small_in[n][2]) for n in names]
    loss_rows, dcw_tot, lg_out, small_out = _small_allreduce_adamw(
        [dgv, dbv, dlb, dgn, dg1, db1, dg2, db2, loss_acc], dws.reshape(N_GROUP * 128, 128), dbs, dcw, dcb,
        hgrn_lb_logits, m_hgrn_lb_logits, v_hgrn_lb_logits, sw, sm, sv)
    loss = loss_rows[0, 0]

    chip = 2 * lax.axis_index("x") + lax.axis_index("y")
    g_cw = lax.dynamic_slice(dcw_tot, (0, chip * (D_FF // 4)), (3, D_FF // 4))
    cw_out = _adamw_whole("adamw_conv_w", g_cw, ffn_conv_w[0], m_ffn_conv_w[0], v_ffn_conv_w[0])

    res = {}
    for si, n in enumerate(names):
        shp = small_in[n][0].shape
        res[n] = tuple(small_out[4 * si + k].reshape(shp) for k in range(4))
    res["hgrn_lb_logits"] = tuple(lg_out)
    res["ffn_conv_w"] = (g_cw[None],) + tuple(o[None] for o in cw_out)

    def big(i):
        return tuple(big_out[i])

    res["w_in"] = tuple(o[None] for o in big(0))
    res["w_branch"] = tuple(jnp.stack([o0, o1])[None] for o0, o1 in zip(big(1), big(2)))
    res["w_out"] = tuple(o[None] for o in big(3))
    res["ffn_w_up"] = tuple(o[None] for o in big(4))
    res["ffn_w_down"] = tuple(o[None] for o in big(5))
    res["ple_w_proj"] = tuple(o[None] for o in big(6))
    res["ple_w_gate"] = tuple(o[None] for o in big(7))

    order = ["w_in", "sgu_w_s", "sgu_b_s", "sgu_norm_g", "sgu_norm_b", "hgrn_lb_logits",
             "hgrn_norm_g", "w_branch", "w_out", "ln1_g", "ln1_b", "ffn_w_up", "ffn_conv_w",
             "ffn_conv_b", "ffn_w_down", "ln2_g", "ln2_b", "ple_w_proj", "ple_w_gate"]
    outs = [loss, gx.reshape(1, t, D_MODEL)]
    for k in range(4):
        outs += [res[n][k] for n in order]
    return tuple(outs)
```

```python
import jax
import jax.numpy as jnp
from jax import lax
from jax.experimental import pallas as pl
from jax.experimental.pallas import tpu as pltpu

F32 = jnp.float32
BF16 = jnp.bfloat16
HIGHEST = lax.Precision.HIGHEST
MESH = pl.DeviceIdType.MESH

D_MODEL = 1024
CHUNK = 64
SGU_BLOCK = 128
SGU_STEP_BLOCKS = 4
SGU_ROWS = SGU_STEP_BLOCKS * SGU_BLOCK
N_GROUP = 8
N_HEAD = 8
HEAD_DIM = 128
D_FF = 2816
PLE_DIM = 256
LN_EPS = 1e-5
RMS_EPS = 1e-6
ALPHA = 2.0 ** 0.25
N_CHIP = 4

ADAM_LR = 0.001
ADAM_B1 = 0.9
ADAM_B2 = 0.999
ADAM_EPS = 1e-08
ADAM_WD = 0.01
ADAM_STEP = 10

VMEM_LIMIT = 56 * 1024 * 1024

NN = (((1,), (0,)), ((), ()))
NT = (((1,), (1,)), ((), ()))
TN = (((0,), (0,)), ((), ()))


def _pc(body, *, name, out_shape, grid=None, in_specs=None, out_specs=None, scratch=(),
        sem=None, nsp=0, vmem=VMEM_LIMIT, aliases=None):
    params = dict(vmem_limit_bytes=vmem)
    if sem is not None:
        params["dimension_semantics"] = sem
    kw = dict(name=name, out_shape=out_shape, compiler_params=pltpu.CompilerParams(**params))
    if aliases:
        kw["input_output_aliases"] = aliases
    if nsp:
        kw["grid_spec"] = pltpu.PrefetchScalarGridSpec(
            num_scalar_prefetch=nsp, grid=grid, in_specs=in_specs, out_specs=out_specs,
            scratch_shapes=list(scratch))
    else:
        if grid is not None:
            kw["grid"] = grid
        if in_specs is not None:
            kw["in_specs"] = in_specs
            kw["out_specs"] = out_specs
        kw["scratch_shapes"] = list(scratch)
    return pl.pallas_call(body, **kw)


def _dot(a, b, dims=NN):
    return lax.dot_general(a.astype(BF16), b.astype(BF16), dims, preferred_element_type=F32)


def _dot32(a, b, dims=NN):
    return lax.dot_general(a, b, dims, precision=HIGHEST, preferred_element_type=F32)


def _sig(x):
    return 1.0 / (1.0 + jnp.exp(-x))


_GC = 0.7978845608028654
_GA = 0.044715


def _gelu(x):
    return 0.5 * x * (1.0 + jnp.tanh(_GC * (x + _GA * x * x * x)))


def _gelu_and_grad(x):
    t = jnp.tanh(_GC * (x + _GA * x * x * x))
    g = 0.5 * x * (1.0 + t)
    dg = 0.5 * (1.0 + t) + 0.5 * x * (1.0 - t * t) * _GC * (1.0 + 3.0 * _GA * x * x)
    return g, dg


def _ln_stats(r):
    mu = jnp.mean(r, axis=-1, keepdims=True)
    xc = r - mu
    var = jnp.mean(xc * xc, axis=-1, keepdims=True)
    rstd = lax.rsqrt(var + LN_EPS)
    return xc * rstd, rstd


def _ln_bwd(dxh, xh, rstd):
    m1 = jnp.mean(dxh, axis=-1, keepdims=True)
    m2 = jnp.mean(dxh * xh, axis=-1, keepdims=True)
    return rstd * (dxh - m1 - xh * m2)


def _colsum8(v):
    return jnp.broadcast_to(jnp.sum(v, axis=0, keepdims=True), (8, v.shape[1]))


def _adamw(w, g, m, v):
    m2 = ADAM_B1 * m + (1.0 - ADAM_B1) * g
    v2 = ADAM_B2 * v + (1.0 - ADAM_B2) * (g * g)
    m_hat = m2 / (1.0 - ADAM_B1 ** ADAM_STEP)
    v_hat = v2 / (1.0 - ADAM_B2 ** ADAM_STEP)
    delta = -ADAM_LR * (m_hat / (jnp.sqrt(v_hat) + ADAM_EPS) + ADAM_WD * w)
    return delta, m2, v2


def _row_tile(rows, cols, itemsize=4, budget=1 << 20, mult=8):
    best = mult
    for tr in range(mult, rows + 1, mult):
        if rows % tr == 0 and tr * cols * itemsize <= budget:
            best = tr
    return best


def _mm(name, a, b, dims, grid, a_spec, b_spec, out_shape, o_spec):
    out_dtype = out_shape.dtype

    def body(a_ref, b_ref, o_ref):
        o_ref[...] = _dot(a_ref[...], b_ref[...], dims).astype(out_dtype)

    return _pc(body, name=name, out_shape=out_shape, grid=grid, in_specs=[a_spec, b_spec],
               out_specs=o_spec, sem=("parallel", "parallel"))(a, b)


class _Comm:
    def __init__(self, ins, out_shapes, sems, start, finish, middle=None, finish_late=None, aliases=None):
        self.ins, self.out_shapes, self.sems = list(ins), list(out_shapes), list(sems)
        self.start, self.finish = start, finish
        self.middle, self.finish_late = middle, finish_late
        self.aliases = aliases or {}


def _hosted_call(body, comm, first, last, *, name, out_shape, grid, in_specs, out_specs, scratch, sem,
                 args, aliases=None, mid=None):
    n_in, n_out, n_scr = len(in_specs), len(out_shape), len(scratch)
    nci, nco = len(comm.ins), len(comm.out_shapes)

    def wrapped(*refs):
        pos = n_in
        own_in, c_in = refs[:pos], refs[pos:pos + nci]
        pos += nci
        own_out, c_out = refs[pos:pos + n_out], refs[pos + n_out:pos + n_out + nco]
        pos += n_out + nco
        own_scr, c_sem = refs[pos:pos + n_scr], refs[pos + n_scr:]

        @pl.when(first())
        def _():
            comm.start(c_in, c_out, c_sem)

        body(*own_in, *own_out, *own_scr)

        if mid is not None:
            @pl.when(mid())
            def _():
                comm.middle(c_in, c_out, c_sem)

        @pl.when(last())
        def _():
            (comm.finish if mid is None else comm.finish_late)(c_in, c_out, c_sem)

    return _pc(wrapped, name=name, out_shape=tuple(out_shape) + tuple(comm.out_shapes), grid=grid,
               in_specs=list(in_specs) + [ANY] * nci, out_specs=tuple(out_specs) + tuple([ANY] * nco),
               scratch=list(scratch) + comm.sems, sem=sem,
               aliases={**(aliases or {}), **{n_in + ci: n_out + co for ci, co in comm.aliases.items()}},
               )(*args, *comm.ins)


def _grid1_call(body, comm, n, *, name, out_shape, in_specs, out_specs, scratch, args, aliases=None):
    if comm is None:
        return _pc(body, name=name, out_shape=out_shape, grid=(n,), in_specs=in_specs, out_specs=out_specs,
                   scratch=scratch, sem=("arbitrary",), aliases=aliases)(*args), ()
    res = _hosted_call(body, comm, lambda: pl.program_id(0) == 0, lambda: pl.program_id(0) == n - 1,
                       name=name, out_shape=out_shape, grid=(n,), in_specs=in_specs,
                       out_specs=out_specs, scratch=scratch, sem=("arbitrary",), args=args, aliases=aliases)
    return res[:len(out_shape)], res[len(out_shape):]


def _run_comm(name, comm):
    nci, nco = len(comm.ins), len(comm.out_shapes)

    def body(*refs):
        c_in, c_out, c_sem = refs[:nci], refs[nci:nci + nco], refs[nci + nco:]
        comm.start(c_in, c_out, c_sem)
        comm.finish(c_in, c_out, c_sem)

    return _pc(body, name=name, out_shape=tuple(comm.out_shapes), in_specs=[ANY] * nci,
               out_specs=tuple([ANY] * nco), scratch=comm.sems)(*comm.ins)


def _mm_tn(name, a, b, tm, tn, stacked=False):
    t, m = a.shape
    _, n = b.shape
    if stacked:
        assert tm == m
        out_shape = jax.ShapeDtypeStruct((n // tn, m, tn), BF16)
        o_spec = pl.BlockSpec((None, tm, tn), lambda i, j: (j, 0, 0))
    else:
        out_shape = jax.ShapeDtypeStruct((m, n), BF16)
        o_spec = pl.BlockSpec((tm, tn), lambda i, j: (i, j))
    return _mm(name, a, b, TN, (m // tm, n // tn),
               pl.BlockSpec((t, tm), lambda i, j: (0, i)),
               pl.BlockSpec((t, tn), lambda i, j: (0, j)),
               out_shape, o_spec)


DH_SLOT = (2, 0, 1, 3)


def _dh_slot(j):
    return jnp.where(j == 3, 3, (j + 2) % 3)


def _in_proj_wgrad(x2b, dh, tm, tn):
    t, m = x2b.shape
    n = dh.shape[2]

    def body(a_ref, b_ref, o_ref):
        o_ref[...] = _dot(a_ref[...], b_ref[...], TN).astype(BF16)

    return _pc(body, name="in_proj_wgrad", out_shape=jax.ShapeDtypeStruct((N_CHIP, m, n), BF16),
               grid=(N_CHIP, m // tm, n // tn),
               in_specs=[pl.BlockSpec((t, tm), lambda j, i, k: (0, i)),
                         pl.BlockSpec((None, t, tn), lambda j, i, k: (_dh_slot(j), 0, k))],
               out_specs=pl.BlockSpec((None, tm, tn), lambda j, i, k: (j, i, k)),
               sem=("parallel", "parallel", "parallel"))(x2b, dh)


def _in_proj_xgrad(dh, win_st, dr1, tm, comm):
    t = dr1.shape[0]
    ni = t // tm

    def body(a_ref, b_ref, add_ref, o_ref, acc):
        j = pl.program_id(1)
        prod = _dot(a_ref[...], b_ref[...], NT)

        @pl.when(j == 0)
        def _():
            acc[...] = prod + ALPHA * add_ref[...]

        @pl.when((j > 0) & (j < N_CHIP - 1))
        def _():
            acc[...] += prod

        @pl.when(j == N_CHIP - 1)
        def _():
            o_ref[...] = acc[...] + prod

    tile = pl.BlockSpec((tm, D_MODEL), lambda i, j: (i, 0))
    res = _hosted_call(body, comm,
                       lambda: (pl.program_id(0) == 0) & (pl.program_id(1) == 0),
                       lambda: (pl.program_id(0) == ni - 1) & (pl.program_id(1) == N_CHIP - 1),
                       name="in_proj_xgrad", out_shape=(jax.ShapeDtypeStruct((t, D_MODEL), F32),),
                       grid=(ni, N_CHIP),
                       in_specs=[pl.BlockSpec((None, tm, 2 * D_MODEL), lambda i, j: (_dh_slot(j), i, 0)),
                                 pl.BlockSpec((None, D_MODEL, 2 * D_MODEL), lambda i, j: (j, 0, 0)),
                                 tile],
                       out_specs=(tile,), scratch=[pltpu.VMEM((tm, D_MODEL), F32)],
                       sem=("arbitrary", "arbitrary"), args=[dh, win_st, dr1])
    return res[0], res[1:]


def _sgu_mixed(v, wm_ref, bsb_ref, gv, bv):
    gl, dgl = _gelu_and_grad(v)
    vh, rstd = _ln_stats(gl)
    vn = vh * gv + bv
    mixed = []
    for g in range(N_GROUP):
        sl = slice(g * 128, (g + 1) * 128)
        mixed.append(_dot(wm_ref[g], vn[:, sl]) + bsb_ref[g])
    return dgl, vh, rstd, vn, mixed


def _sgu_fwd(h, wm, bsb, gv, bv, comm=None):
    t = h.shape[0]

    def body(u_ref, v_ref, wm_ref, bsb_ref, gv_ref, bv_ref, ya_ref):
        for bb in range(SGU_STEP_BLOCKS):
            rows = slice(bb * SGU_BLOCK, (bb + 1) * SGU_BLOCK)
            u = u_ref[rows, :].astype(F32)
            _, _, _, _, mixed = _sgu_mixed(v_ref[rows, :].astype(F32), wm_ref, bsb_ref, gv_ref[...],
                                           bv_ref[...])
            gu = _gelu(u)
            for g in range(N_GROUP):
                sl = slice(g * 128, (g + 1) * 128)
                ya_ref[rows, sl] = (gu[:, sl] * mixed[g]).astype(BF16)

    full3 = pl.BlockSpec((N_GROUP, 128, 128), lambda i: (0, 0, 0))
    vec = pl.BlockSpec((1, D_MODEL), lambda i: (0, 0))
    (ya,), extra = _grid1_call(
        body, comm, t // SGU_ROWS, name="sgu_fwd",
        out_shape=(jax.ShapeDtypeStruct((t, D_MODEL), BF16),),
        in_specs=[pl.BlockSpec((SGU_ROWS, D_MODEL), lambda i: (i, 0)),
                  pl.BlockSpec((SGU_ROWS, D_MODEL), lambda i: (i, 1)),
                  full3, full3, vec, vec],
        out_specs=(pl.BlockSpec((SGU_ROWS, D_MODEL), lambda i: (i, 0)),),
        scratch=[], args=(h, h, wm, bsb, gv, bv))
    return ya, extra


def _sgu_bwd(h, dya, wm, wmt, bsb, gv, bv, maskf, dh_buf, comm=None):
    t = h.shape[0]
    nb = t // SGU_ROWS

    def body(u_ref, v_ref, dya_ref, wm_ref, wmt_ref, bsb_ref, gv_ref, bv_ref, mask_ref, dh_buf_ref,
             dh_ref, dws_ref, dbs_ref, dgv_ref, dbv_ref, dmix_acc):
        i = pl.program_id(0)

        @pl.when(i == 0)
        def _():
            dws_ref[...] = jnp.zeros_like(dws_ref)
            dgv_ref[...] = jnp.zeros_like(dgv_ref)
            dbv_ref[...] = jnp.zeros_like(dbv_ref)
            dmix_acc[...] = jnp.zeros_like(dmix_acc)

        gvv = gv_ref[...]
        for bb in range(SGU_STEP_BLOCKS):
            rows = slice(bb * SGU_BLOCK, (bb + 1) * SGU_BLOCK)
            u = u_ref[rows, :].astype(F32)
            dgl_v, vh, rstd, vn, mixed = _sgu_mixed(v_ref[rows, :].astype(F32), wm_ref, bsb_ref, gvv,
                                                    bv_ref[...])
            gu, dgl_u = _gelu_and_grad(u)
            dya_v = dya_ref[rows, :].astype(F32)
            dvn_parts = []
            for g in range(N_GROUP):
                sl = slice(g * 128, (g + 1) * 128)
                d_y = dya_v[:, sl]
                dh_ref[rows, sl] = (d_y * mixed[g] * dgl_u[:, sl]).astype(BF16)
                d_mixed = d_y * gu[:, sl]
                dmix_acc[g] += d_mixed
                dws_ref[g] += _dot(d_mixed, vn[:, sl], NT) * mask_ref[...]
                dvn_parts.append(_dot(wmt_ref[g], d_mixed))
            dvn = jnp.concatenate(dvn_parts, axis=1)
            dgv_ref[...] += _colsum8(dvn * vh)
            dbv_ref[...] += _colsum8(dvn)
            d_gl = _ln_bwd(dvn * gvv, vh, rstd)
            dh_ref[rows, D_MODEL:] = (d_gl * dgl_v).astype(BF16)

        @pl.when(i == nb - 1)
        def _():
            rowid = lax.broadcasted_iota(jnp.int32, (8, 128), 0)
            ones = jnp.ones((8, 128), F32)
            acc = jnp.zeros((8, 128), F32)
            for g in range(N_GROUP):
                rs = _dot32(ones, dmix_acc[g], NT)
                acc = jnp.where(rowid == g, rs, acc)
            dbs_ref[...] = acc

    full3 = pl.BlockSpec((N_GROUP, 128, 128), lambda i: (0, 0, 0))
    vec = pl.BlockSpec((1, D_MODEL), lambda i: (0, 0))
    acc8 = pl.BlockSpec((8, D_MODEL), lambda i: (0, 0))
    return _grid1_call(
        body, comm, nb, name="sgu_bwd",
        out_shape=(jax.ShapeDtypeStruct(dh_buf.shape, BF16),
                   jax.ShapeDtypeStruct((N_GROUP, 128, 128), F32),
                   jax.ShapeDtypeStruct((8, 128), F32),
                   jax.ShapeDtypeStruct((8, D_MODEL), F32),
                   jax.ShapeDtypeStruct((8, D_MODEL), F32)),
        in_specs=[pl.BlockSpec((SGU_ROWS, D_MODEL), lambda i: (i, 0)),
                  pl.BlockSpec((SGU_ROWS, D_MODEL), lambda i: (i, 1)),
                  pl.BlockSpec((SGU_ROWS, D_MODEL), lambda i: (i, 0)),
                  full3, full3, full3, vec, vec,
                  pl.BlockSpec((128, 128), lambda i: (0, 0)), ANY],
        out_specs=(pl.BlockSpec((None, SGU_ROWS, 2 * D_MODEL), lambda i: (DH_SLOT[0], i, 0)),
                   full3, pl.BlockSpec((8, 128), lambda i: (0, 0)), acc8, acc8),
        scratch=[pltpu.VMEM((N_GROUP, 128, 128), F32)],
        args=(h, h, dya, wm, wmt, bsb, gv, bv, maskf, dh_buf), aliases={9: 0})


def _tri_masks():
    row = lax.broadcasted_iota(jnp.int32, (CHUNK, CHUNK), 0)
    col = lax.broadcasted_iota(jnp.int32, (CHUNK, CHUNK), 1)
    return col <= row, col >= row


def _heads(v):
    return [v[:, hd * HEAD_DIM:(hd + 1) * HEAD_DIM] for hd in range(N_HEAD)]


def _tri_cumsum(tri_bf, v):
    hi = v.astype(BF16)
    r = v - hi.astype(F32)
    mid = r.astype(BF16)
    lo = (r - mid.astype(F32)).astype(BF16)
    return _dot(tri_bf, hi) + _dot(tri_bf, mid) + _dot(tri_bf, lo)


def _hgrn_chunk(q, fp, ii, lb, st_heads, causal, with_o=True):
    sg = _sig(fp)
    f = lb + (1.0 - lb) * sg
    k = 1.0 - f
    c = _tri_cumsum(causal.astype(BF16), jnp.log(f))
    ec = jnp.exp(c)
    en = jnp.exp(-c)
    sq = _sig(q)
    qt = q * sq * ec
    kt = k * en
    ecl = jnp.exp(c[CHUNK - 1:CHUNK, :])
    kk = kt * ecl
    qtb, ktb, iib, kkb = qt.astype(BF16), kt.astype(BF16), ii.astype(BF16), kk.astype(BF16)
    attn, o = [], []
    for hd, (qh, kh, ih) in enumerate(zip(_heads(qtb), _heads(ktb), _heads(iib))):
        a = jnp.where(causal, _dot(qh, kh, NT), 0.0).astype(BF16)
        attn.append(a)
        if with_o:
            o.append(_dot(a, ih) + _dot(qh, st_heads[hd], NT))
    return dict(sg=sg, f=f, k=k, ec=ec, en=en, sq=sq, ecl=ecl, kk=kk, qtb=qtb, ktb=ktb, iib=iib,
                kkb=kkb, attn=attn, o=o)


def _rms_heads(o_heads):
    rinv = [lax.rsqrt(jnp.mean(o * o, axis=-1, keepdims=True) + RMS_EPS) for o in o_heads]
    return rinv, jnp.concatenate([o * r for o, r in zip(o_heads, rinv)], axis=1)


HG_CHUNKS = 8
HG_ROWS = HG_CHUNKS * CHUNK


def _hgrn_fwd(h, logits, gn, comm=None):
    t = h.shape[0]
    nb = t // HG_ROWS

    def body(q_ref, f_ref, i_ref, og_ref, lg_ref, gn_ref, yb_ref, o_ref, st_ref, state):
        @pl.when(pl.program_id(0) == 0)
        def _():
            state[...] = jnp.zeros_like(state)

        causal, _ = _tri_masks()
        lb = _sig(lg_ref[0:1, :] - lg_ref[1:2, :])
        gnv = gn_ref[...]
        st = [state[hd] for hd in range(N_HEAD)]
        for cc in range(HG_CHUNKS):
            rows = slice(cc * CHUNK, (cc + 1) * CHUNK)
            og = og_ref[rows, :].astype(F32)
            r = _hgrn_chunk(q_ref[rows, :].astype(F32), f_ref[rows, :].astype(F32),
                            i_ref[rows, :].astype(F32), lb, [s.astype(BF16) for s in st], causal)
            o_bf = jnp.concatenate(r["o"], axis=1).astype(BF16)
            o_ref[rows, :] = o_bf
            _, on = _rms_heads(_heads(o_bf.astype(F32)))
            yb_ref[rows, :] = (on * gnv * (og * _sig(og))).astype(BF16)
            for hd in range(N_HEAD):
                st_ref[cc, hd] = st[hd]
            st = [s * e + _dot(ih, kh, TN)
                  for s, e, ih, kh in zip(st, _heads(r["ecl"]), _heads(r["iib"]), _heads(r["kkb"]))]
        for hd in range(N_HEAD):
            state[hd] = st[hd]

    def col(k):
        return pl.BlockSpec((HG_ROWS, D_MODEL), lambda ci: (ci, k))

    return _grid1_call(body, comm, nb, name="hgrn_fwd",
                       out_shape=(jax.ShapeDtypeStruct((t, D_MODEL), BF16),
                                  jax.ShapeDtypeStruct((t, D_MODEL), BF16),
                                  jax.ShapeDtypeStruct((t // CHUNK, N_HEAD, HEAD_DIM, HEAD_DIM), F32)),
                       in_specs=[col(2), col(3), col(4), col(5),
                                 pl.BlockSpec((2, D_MODEL), lambda ci: (0, 0)),
                                 pl.BlockSpec((1, D_MODEL), lambda ci: (0, 0))],
                       out_specs=(pl.BlockSpec((HG_ROWS, D_MODEL), lambda ci: (ci, 0)),
                                  pl.BlockSpec((HG_ROWS, D_MODEL), lambda ci: (ci, 0)),
                                  pl.BlockSpec((HG_CHUNKS, N_HEAD, HEAD_DIM, HEAD_DIM),
                                               lambda ci: (ci, 0, 0, 0))),
                       scratch=[pltpu.VMEM((N_HEAD, HEAD_DIM, HEAD_DIM), F32)],
                       args=(h, h, h, h, logits, gn))


def _hgrn_chunk_bwd(q, fp, ii, og, o_saved, dy, gnv, lb, st, dsn, causal, anti):
    stb = [s.astype(BF16) for s in st]
    dsnb = [s.astype(BF16) for s in dsn]
    r = _hgrn_chunk(q, fp, ii, lb, stb, causal, with_o=False)
    rinv, on = _rms_heads(_heads(o_saved))
    so = _sig(og)
    sil = og * so
    d_og = dy * on * gnv * (so * (1.0 + og * (1.0 - so)))
    d_on = dy * gnv * sil
    d_ob = jnp.concatenate(
        [ri * (dn - oh * jnp.mean(dn * oh, axis=-1, keepdims=True))
         for ri, dn, oh in zip(rinv, _heads(d_on), _heads(on))], axis=1).astype(BF16)
    d_i, d_qt, d_kt, d_kk, d_st, st_dsn = [], [], [], [], [], []
    ecl = _heads(r["ecl"])
    for hd, (dh, qh, kh, ih, kkh) in enumerate(zip(_heads(d_ob), _heads(r["qtb"]), _heads(r["ktb"]),
                                                   _heads(r["iib"]), _heads(r["kkb"]))):
        d_attn = jnp.where(causal, _dot(dh, ih, NT), 0.0).astype(BF16)
        d_i.append(_dot(r["attn"][hd], dh, TN) + _dot(kkh, dsnb[hd], NT))
        d_qt.append(_dot(d_attn, kh) + _dot(dh, stb[hd]))
        d_kt.append(_dot(d_attn, qh, TN))
        d_kk.append(_dot(ih, dsnb[hd]))
        d_st.append(_dot(dh, qh, TN) + dsn[hd] * ecl[hd])
        st_dsn.append(jnp.sum(st[hd] * dsn[hd], axis=0, keepdims=True))
    d_qt = jnp.concatenate(d_qt, axis=1)
    d_kt = jnp.concatenate(d_kt, axis=1)
    d_kk = jnp.concatenate(d_kk, axis=1)
    kk = r["kk"]
    d_cl = r["ecl"] * jnp.concatenate(st_dsn, axis=1) + jnp.sum(kk * d_kk, axis=0, keepdims=True)
    d_k = (d_kk * r["ecl"] + d_kt) * r["en"]
    d_c = d_qt * r["qtb"].astype(F32) - d_kt * r["ktb"].astype(F32) - d_kk * kk
    rowid = lax.broadcasted_iota(jnp.int32, (CHUNK, D_MODEL), 0)
    d_c = d_c + jnp.where(rowid == CHUNK - 1, d_cl, 0.0)
    d_lf = _tri_cumsum(anti.astype(BF16), d_c)
    d_f = d_lf / r["f"] - d_k
    sg, sq = r["sg"], r["sq"]
    d_q = d_qt * r["ec"] * (sq * (1.0 + q * (1.0 - sq)))
    d_fp = d_f * (1.0 - lb) * sg * (1.0 - sg)
    return (d_q, d_fp, jnp.concatenate(d_i, axis=1), d_og, d_st,
            _colsum8(dy * on * sil), _colsum8(d_f * (1.0 - sg)))


def _hgrn_bwd(h, o_all, dyb, st_all, logits, gn, dh_buf, comm=None):
    t = h.shape[0]
    nb = t // HG_ROWS

    def body(q_ref, f_ref, i_ref, og_ref, o_ref, dyb_ref, st_ref, lg_ref, gn_ref, dh_buf_ref,
             dh_ref, dlb_ref, dgn_ref, dstate):
        @pl.when(pl.program_id(0) == 0)
        def _():
            dstate[...] = jnp.zeros_like(dstate)
            dlb_ref[...] = jnp.zeros_like(dlb_ref)
            dgn_ref[...] = jnp.zeros_like(dgn_ref)

        causal, anti = _tri_masks()
        lb = _sig(lg_ref[0:1, :] - lg_ref[1:2, :])
        gnv = gn_ref[...]
        dsn = [dstate[hd] for hd in range(N_HEAD)]
        dgn_acc = jnp.zeros((8, D_MODEL), F32)
        dlb_acc = jnp.zeros((8, D_MODEL), F32)
        for cc in reversed(range(HG_CHUNKS)):
            rows = slice(cc * CHUNK, (cc + 1) * CHUNK)
            d_q, d_fp, d_i, d_og, dsn, dgn_c, dlb_c = _hgrn_chunk_bwd(
                q_ref[rows, :].astype(F32), f_ref[rows, :].astype(F32), i_ref[rows, :].astype(F32),
                og_ref[rows, :].astype(F32), o_ref[rows, :].astype(F32), dyb_ref[rows, :].astype(F32), gnv, lb,
                [st_ref[cc, hd] for hd in range(N_HEAD)], dsn, causal, anti)
            dgn_acc = dgn_acc + dgn_c
            dlb_acc = dlb_acc + dlb_c
            dh_ref[0, rows, :D_MODEL] = d_q.astype(BF16)
            dh_ref[0, rows, D_MODEL:] = d_fp.astype(BF16)
            dh_ref[1, rows, :D_MODEL] = d_i.astype(BF16)
            dh_ref[1, rows, D_MODEL:] = d_og.astype(BF16)
        dgn_ref[...] += dgn_acc
        dlb_ref[...] += dlb_acc
        for hd in range(N_HEAD):
            dstate[hd] = dsn[hd]

    def col(k):
        return pl.BlockSpec((HG_ROWS, D_MODEL), lambda ci: (nb - 1 - ci, k))

    acc8 = pl.BlockSpec((8, D_MODEL), lambda ci: (0, 0))
    pair = pl.BlockSpec((2, HG_ROWS, 2 * D_MODEL), lambda ci: (0, nb - 1 - ci, 0))
    return _grid1_call(body, comm, nb, name="hgrn_bwd",
                       out_shape=(jax.ShapeDtypeStruct(dh_buf.shape, BF16),
                                  jax.ShapeDtypeStruct((8, D_MODEL), F32),
                                  jax.ShapeDtypeStruct((8, D_MODEL), F32)),
                       in_specs=[col(2), col(3), col(4), col(5), col(0),
                                 pl.BlockSpec((HG_ROWS, D_MODEL), lambda ci: (nb - 1 - ci, 0)),
                                 pl.BlockSpec((HG_CHUNKS, N_HEAD, HEAD_DIM, HEAD_DIM),
                                              lambda ci: (nb - 1 - ci, 0, 0, 0)),
                                 pl.BlockSpec((2, D_MODEL), lambda ci: (0, 0)),
                                 pl.BlockSpec((1, D_MODEL), lambda ci: (0, 0)), ANY],
                       out_specs=(pair, acc8, acc8),
                       scratch=[pltpu.VMEM((N_HEAD, HEAD_DIM, HEAD_DIM), F32)],
                       args=(h, h, h, h, o_all, dyb, st_all, logits, gn, dh_buf), aliases={9: 0})


def _mix_fwd(ya, yb, h, x, wb0, wb1, wo, g1, b1, tm, comm=None):
    t = x.shape[0]

    def body(ya_ref, yb_ref, ga_ref, gb_ref, x_ref, wb0_ref, wb1_ref, wo_ref, g1_ref, b1_ref,
             r1_ref, a_ref, b_ref, m_ref, x1_ref):
        a = _dot(ya_ref[...], wb0_ref[...])
        b = _dot(yb_ref[...], wb1_ref[...])
        m = _sig(ga_ref[...].astype(F32)) * a + _sig(gb_ref[...].astype(F32)) * b
        r1 = ALPHA * x_ref[...] + _dot(m, wo_ref[...])
        xh, _ = _ln_stats(r1)
        r1_ref[...] = r1
        a_ref[...] = a.astype(BF16)
        b_ref[...] = b.astype(BF16)
        m_ref[...] = m.astype(BF16)
        x1_ref[...] = (xh * g1_ref[...] + b1_ref[...]).astype(BF16)

    tile = pl.BlockSpec((tm, D_MODEL), lambda i: (i, 0))
    wsp = pl.BlockSpec((D_MODEL, D_MODEL), lambda i: (0, 0))
    vec = pl.BlockSpec((1, D_MODEL), lambda i: (0, 0))
    f32o = jax.ShapeDtypeStruct((t, D_MODEL), F32)
    bfo = jax.ShapeDtypeStruct((t, D_MODEL), BF16)
    return _grid1_call(body, comm, t // tm, name="mix_fwd", out_shape=(f32o, bfo, bfo, bfo, bfo),
                       in_specs=[tile, tile,
                                 pl.BlockSpec((tm, D_MODEL), lambda i: (i, 6)),
                                 pl.BlockSpec((tm, D_MODEL), lambda i: (i, 7)),
                                 tile, wsp, wsp, wsp, vec, vec],
                       out_specs=(tile, tile, tile, tile, tile),
                       scratch=[], args=(ya, yb, h, h, x, wb0, wb1, wo, g1, b1))


def _mix_bwd(dr1, h, a, b, wo, wb0, wb1, tm):
    t = dr1.shape[0]

    def body(dr1_ref, ga_ref, gb_ref, a_ref, b_ref, wo_ref, wb0_ref, wb1_ref,
             da_ref, db_ref, dh3_ref, dya_ref, dyb_ref):
        d_m = _dot(dr1_ref[...], wo_ref[...], NT)
        sa = _sig(ga_ref[...].astype(F32))
        sb = _sig(gb_ref[...].astype(F32))
        d_a = (d_m * sa).astype(BF16)
        d_b = (d_m * sb).astype(BF16)
        da_ref[...] = d_a
        db_ref[...] = d_b
        dh3_ref[:, :D_MODEL] = (d_m * a_ref[...].astype(F32) * sa * (1.0 - sa)).astype(BF16)
        dh3_ref[:, D_MODEL:] = (d_m * b_ref[...].astype(F32) * sb * (1.0 - sb)).astype(BF16)
        dya_ref[...] = _dot(d_a, wb0_ref[...], NT).astype(BF16)
        dyb_ref[...] = _dot(d_b, wb1_ref[...], NT).astype(BF16)

    tile = pl.BlockSpec((tm, D_MODEL), lambda i: (i, 0))
    wsp = pl.BlockSpec((D_MODEL, D_MODEL), lambda i: (0, 0))
    f32o = jax.ShapeDtypeStruct((t, D_MODEL), F32)
    bfo = jax.ShapeDtypeStruct((t, D_MODEL), BF16)
    return _pc(body, name="mix_bwd",
               out_shape=(bfo, bfo, jax.ShapeDtypeStruct((N_CHIP, t, 2 * D_MODEL), BF16), bfo, bfo),
               grid=(t // tm,),
               in_specs=[tile,
                         pl.BlockSpec((tm, D_MODEL), lambda i: (i, 6)),
                         pl.BlockSpec((tm, D_MODEL), lambda i: (i, 7)),
                         tile, tile, wsp, wsp, wsp],
               out_specs=(tile, tile, pl.BlockSpec((None, tm, 2 * D_MODEL), lambda i: (DH_SLOT[3], i, 0)),
                          tile, tile),
               sem=("parallel",))(dr1, h, h, a, b, wo, wb0, wb1)


FF_TILE = 1408
FF_NJ = D_FF // FF_TILE


def _shift_down(v, k):
    return pltpu.roll(v, k, 0)


def _shift_up(v, k):
    return pltpu.roll(v, v.shape[0] - k, 0)


HALO = 16
FF_PIECES = ((0, 768), (768, FF_TILE))


def _ffn_up_act(x1b, wup_st, convw, convb, tm, comm):
    t = x1b.shape[0]
    ni = t // tm
    nth = tm // HALO

    def body(x_ref, xp_ref, wg_ref, wv_ref, cw_ref, cb_ref, h2_ref, act_ref):
        wg = wg_ref[...]
        gate = _dot(x_ref[...], wg).astype(BF16)
        val = _dot(x_ref[...], wv_ref[...]).astype(BF16)
        prev = (_dot(xp_ref[...], wg) * (pl.program_id(0) > 0).astype(F32)).astype(BF16)
        h2_ref[0] = gate
        h2_ref[1] = val
        ext = jnp.concatenate([prev.astype(F32), gate.astype(F32)], axis=0)
        gc = (cw_ref[0:1, :] * _shift_down(ext, 2) + cw_ref[1:2, :] * _shift_down(ext, 1)
              + cw_ref[2:3, :] * ext + cb_ref[...])[HALO:, :].astype(BF16)
        h2_ref[2] = gc
        act_ref[...] = (_gelu(gc.astype(F32)) * val.astype(F32)).astype(BF16)

    res = _hosted_call(
        body, comm,
        lambda: (pl.program_id(0) == 0) & (pl.program_id(1) == 0),
        lambda: (pl.program_id(0) == ni - 1) & (pl.program_id(1) == FF_NJ - 1),
        mid=lambda: (pl.program_id(0) == max(ni - 2, 0)) & (pl.program_id(1) == 0),
        name="ffn_up",
        out_shape=(jax.ShapeDtypeStruct((3, t, D_FF), BF16), jax.ShapeDtypeStruct((t, D_FF), BF16)),
        grid=(ni, FF_NJ),
        in_specs=[pl.BlockSpec((tm, D_MODEL), lambda i, j: (i, 0)),
                  pl.BlockSpec((HALO, D_MODEL), lambda i, j: (jnp.maximum(i * nth - 1, 0), 0)),
                  pl.BlockSpec((None, D_MODEL, FF_TILE), lambda i, j: (j, 0, 0)),
                  pl.BlockSpec((None, D_MODEL, FF_TILE), lambda i, j: (j + FF_NJ, 0, 0)),
                  pl.BlockSpec((3, FF_TILE), lambda i, j: (0, j)),
                  pl.BlockSpec((1, FF_TILE), lambda i, j: (0, j))],
        out_specs=(pl.BlockSpec((3, tm, FF_TILE), lambda i, j: (0, i, j)),
                   pl.BlockSpec((tm, FF_TILE), lambda i, j: (i, j))),
        scratch=[], sem=("arbitrary", "arbitrary"),
        args=(x1b, x1b, wup_st, wup_st, convw, convb))
    return res[0], res[1], res[2:]


def _out_fwd_bwd(act, x1b, r1, p2, tgt, wd, wpg, wpp, g1, b1, g2, b2, tm):
    t = r1.shape[0]

    def body(act_ref, x1b_ref, r1_ref, p_ref, tgt_ref, wd_ref, wpg_ref, wpp_ref,
             g1_ref, b1_ref, g2_ref, b2_ref,
             dr2_ref, dpg_ref, dpp_ref, loss_ref, dg2_ref, db2_ref):
        i = pl.program_id(0)

        @pl.when(i == 0)
        def _():
            loss_ref[...] = jnp.zeros_like(loss_ref)
            dg2_ref[...] = jnp.zeros_like(dg2_ref)
            db2_ref[...] = jnp.zeros_like(db2_ref)

        ffn = _dot(act_ref[...], wd_ref[...])
        pg = _dot(x1b_ref[...], wpg_ref[...])
        pp = _dot(p_ref[...], wpp_ref[...])
        s = _sig(pg)
        xh1, _ = _ln_stats(r1_ref[...])
        x1 = xh1 * g1_ref[...] + b1_ref[...]
        r2 = ALPHA * x1 + ffn + s * pp
        xh2, rstd2 = _ln_stats(r2)
        g2v = g2_ref[...]
        diff = xh2 * g2v + b2_ref[...] - tgt_ref[...]
        part = jnp.sum(jnp.sum(diff * diff, axis=1, keepdims=True), axis=0, keepdims=True)
        loss_ref[...] += jnp.broadcast_to(part * (0.5 / D_MODEL), loss_ref.shape)
        dy = diff * (1.0 / D_MODEL)
        dg2_ref[...] += _colsum8(dy * xh2)
        db2_ref[...] += _colsum8(dy)
        dr2 = _ln_bwd(dy * g2v, xh2, rstd2)
        dr2_ref[...] = dr2
        dpg_ref[...] = (dr2 * pp * s * (1.0 - s)).astype(BF16)
        dpp_ref[...] = (dr2 * s).astype(BF16)

    tile = pl.BlockSpec((tm, D_MODEL), lambda i: (i, 0))
    vec = pl.BlockSpec((1, D_MODEL), lambda i: (0, 0))
    acc8 = pl.BlockSpec((8, D_MODEL), lambda i: (0, 0))
    acc_shape = jax.ShapeDtypeStruct((8, D_MODEL), F32)
    return _pc(body, name="out_fwd_bwd",
               out_shape=(jax.ShapeDtypeStruct((t, D_MODEL), F32),
                          jax.ShapeDtypeStruct((t, D_MODEL), BF16),
                          jax.ShapeDtypeStruct((t, D_MODEL), BF16),
                          acc_shape, acc_shape, acc_shape),
               grid=(t // tm,),
               in_specs=[pl.BlockSpec((tm, D_FF), lambda i: (i, 0)), tile, tile,
                         pl.BlockSpec((tm, PLE_DIM), lambda i: (i, 0)), tile,
                         pl.BlockSpec((D_FF, D_MODEL), lambda i: (0, 0)),
                         pl.BlockSpec((D_MODEL, D_MODEL), lambda i: (0, 0)),
                         pl.BlockSpec((PLE_DIM, D_MODEL), lambda i: (0, 0)),
                         vec, vec, vec, vec],
               out_specs=(tile, tile, tile, acc8, acc8, acc8),
               sem=("arbitrary",))(act, x1b, r1, p2, tgt, wd, wpg, wpp, g1, b1, g2, b2)


def _ffn_bwd(h2, dr2, wd, wup_st, dpg, wpg, r1, g1, convw, tm):
    t = r1.shape[0]
    ni = t // tm
    nth = tm // HALO
    last_halo = t // HALO - 1
    main_rows = slice(0, tm)

    def body(g_ref, gc_ref, gcn_ref, v_ref, vn_ref, dr2_ref, dr2n_ref, wd_ref, wug_ref, wuv_ref,
             cw_ref, dpg_ref, wpg_ref, r1_ref, g1_ref,
             dh2_ref, dr1_ref, dcw_ref, dcb_ref, dg1_ref, db1_ref, acc):
        i = pl.program_id(0)
        j = pl.program_id(1)

        @pl.when((i == 0) & (j == 0))
        def _():
            dcw_ref[...] = jnp.zeros_like(dcw_ref)
            dcb_ref[...] = jnp.zeros_like(dcb_ref)
            dg1_ref[...] = jnp.zeros_like(dg1_ref)
            db1_ref[...] = jnp.zeros_like(db1_ref)

        dr2v = dr2_ref[...].astype(BF16)
        dr2n = dr2n_ref[...].astype(BF16)
        more = (i < ni - 1).astype(F32)
        prod = None
        dcw_parts, dcb_parts = [], []
        for c0, c1 in FF_PIECES:
            pc = slice(c0, c1)
            da = _dot(dr2v, wd_ref[pc, :], NT)
            dnext = _dot(dr2n, wd_ref[pc, :], NT) * more
            gc = jnp.concatenate([gc_ref[:, pc].astype(F32), gcn_ref[:, pc].astype(F32)], axis=0)
            vext = jnp.concatenate([v_ref[:, pc].astype(F32), vn_ref[:, pc].astype(F32)], axis=0)
            dext = jnp.concatenate([da, dnext], axis=0)
            gl, dgl = _gelu_and_grad(gc)
            d_gc = dext * vext * dgl
            up1 = _shift_up(d_gc, 1)[main_rows, :]
            up2 = _shift_up(d_gc, 2)[main_rows, :]
            dm = d_gc[main_rows, :]
            d_gate = (cw_ref[2:3, pc] * dm + cw_ref[1:2, pc] * up1 + cw_ref[0:1, pc] * up2).astype(BF16)
            d_val = (da * gl[main_rows, :]).astype(BF16)
            dh2_ref[0, :, pc] = d_gate
            dh2_ref[1, :, pc] = d_val
            g = g_ref[:, pc].astype(F32)
            s0 = jnp.sum(g * up2, axis=0, keepdims=True)
            s1 = jnp.sum(g * up1, axis=0, keepdims=True)
            s2 = jnp.sum(g * dm, axis=0, keepdims=True)
            rowid = lax.broadcasted_iota(jnp.int32, (8, c1 - c0), 0)
            dcw_parts.append(jnp.where(rowid == 0, s0, jnp.where(rowid == 1, s1,
                                                                 jnp.where(rowid == 2, s2, 0.0))))
            dcb_parts.append(_colsum8(dm))
            part = _dot(d_gate, wug_ref[:, pc], NT) + _dot(d_val, wuv_ref[:, pc], NT)
            prod = part if prod is None else prod + part
        dcw_part = jnp.concatenate(dcw_parts, axis=1)
        dcb_part = jnp.concatenate(dcb_parts, axis=1)
        for jj in range(FF_NJ):
            @pl.when(j == jj)
            def _(jj=jj):
                cols = slice(jj * FF_TILE, (jj + 1) * FF_TILE)
                dcw_ref[:, cols] += dcw_part
                dcb_ref[:, cols] += dcb_part

        @pl.when(j == 0)
        def _():
            acc[...] = prod

        @pl.when(j > 0)
        def _():
            acc[...] += prod

        @pl.when(j == FF_NJ - 1)
        def _():
            d_x1 = acc[...] + _dot(dpg_ref[...], wpg_ref[...], NT) + ALPHA * dr2_ref[...]
            xh, rstd = _ln_stats(r1_ref[...])
            dg1_ref[...] += _colsum8(d_x1 * xh)
            db1_ref[...] += _colsum8(d_x1)
            dr1_ref[...] = _ln_bwd(d_x1 * g1_ref[...], xh, rstd)

    def h2_main(part):
        return pl.BlockSpec((None, tm, FF_TILE), lambda i, j: (part, i, j))

    def h2_next(part):
        return pl.BlockSpec((None, HALO, FF_TILE),
                            lambda i, j: (part, jnp.minimum((i + 1) * nth, last_halo), j))

    tile = pl.BlockSpec((tm, D_MODEL), lambda i, j: (i, 0))
    acc8 = pl.BlockSpec((8, D_MODEL), lambda i, j: (0, 0))
    accff = pl.BlockSpec((8, D_FF), lambda i, j: (0, 0))
    acc_shape = jax.ShapeDtypeStruct((8, D_MODEL), F32)
    accff_shape = jax.ShapeDtypeStruct((8, D_FF), F32)
    return _pc(body, name="ffn_bwd",
               out_shape=(jax.ShapeDtypeStruct((2, t, D_FF), BF16),
                          jax.ShapeDtypeStruct((t, D_MODEL), F32),
                          accff_shape, accff_shape, acc_shape, acc_shape),
               grid=(ni, FF_NJ),
               in_specs=[h2_main(0), h2_main(2), h2_next(2), h2_main(1), h2_next(1),
                         tile,
                         pl.BlockSpec((HALO, D_MODEL), lambda i, j: (jnp.minimum((i + 1) * nth, last_halo), 0)),
                         pl.BlockSpec((FF_TILE, D_MODEL), lambda i, j: (j, 0)),
                         pl.BlockSpec((None, D_MODEL, FF_TILE), lambda i, j: (j, 0, 0)),
                         pl.BlockSpec((None, D_MODEL, FF_TILE), lambda i, j: (j + FF_NJ, 0, 0)),
                         pl.BlockSpec((3, FF_TILE), lambda i, j: (0, j)),
                         tile, pl.BlockSpec((D_MODEL, D_MODEL), lambda i, j: (0, 0)),
                         tile, pl.BlockSpec((1, D_MODEL), lambda i, j: (0, 0))],
               out_specs=(pl.BlockSpec((2, tm, FF_TILE), lambda i, j: (0, i, j)),
                          tile, accff, accff, acc8, acc8),
               scratch=[pltpu.VMEM((tm, D_MODEL), F32)],
               sem=("arbitrary", "arbitrary"))(h2, h2, h2, h2, h2, dr2, dr2, wd, wup_st, wup_st,
                                               convw, dpg, wpg, r1, g1)


ANY = pl.BlockSpec(memory_space=pl.ANY)


def _chip_peers():
    x, y, c = lax.axis_index("x"), lax.axis_index("y"), lax.axis_index("c")
    return x, y, c, [(1 - x, y), (x, 1 - y), (1 - x, 1 - y)]


def _gather_comm(halved, whole=(), blocks=None):
    n, nw = len(halved), len(whole)
    blocks = blocks or {}

    def at(ti, ref, chip, *rest):
        return ref.at[(chip, blocks[ti][1]) + rest] if ti in blocks else ref.at[(chip,) + rest]

    def copies(ins, outs, sems):
        ici_send, ici_recv, d2d_send, d2d_recv, own_send, own_recv = sems
        x, y, c, peers = _chip_peers()
        me = 2 * x + y
        sibling = (x, y, 1 - c)
        own, ici, ici_wait, fwd, fwd_wait = [], [], [], [], []
        for ti in range(n + nw):
            src, dst = ins[ti], outs[ti]
            own.append(pltpu.make_async_remote_copy(
                src_ref=src, dst_ref=at(ti, dst, me), send_sem=own_send.at[ti], recv_sem=own_recv.at[ti],
                device_id=sibling, device_id_type=MESH))
            for k, (px, py) in enumerate(peers):
                pk = 2 * px + py
                sem = dict(send_sem=ici_send.at[ti * 3 + k], recv_sem=ici_recv.at[ti * 3 + k],
                           device_id=(px, py, c), device_id_type=MESH)
                if ti < n:
                    ici.append(pltpu.make_async_remote_copy(src_ref=src.at[c], dst_ref=at(ti, dst, me, c), **sem))
                    ici_wait.append(pltpu.make_async_remote_copy(src_ref=src.at[c], dst_ref=at(ti, dst, pk, c),
                                                                 **sem))
                    dsem = dict(send_sem=d2d_send.at[ti * 3 + k], recv_sem=d2d_recv.at[ti * 3 + k],
                                device_id=sibling, device_id_type=MESH)
                    fwd.append(pltpu.make_async_remote_copy(src_ref=at(ti, dst, pk, c), dst_ref=at(ti, dst, pk, c),
                                                            **dsem))
                    fwd_wait.append(pltpu.make_async_remote_copy(
                        src_ref=at(ti, dst, pk, 1 - c), dst_ref=at(ti, dst, pk, 1 - c), **dsem))
                else:
                    ici.append(pltpu.make_async_remote_copy(src_ref=src, dst_ref=dst.at[me], **sem))
                    ici_wait.append(pltpu.make_async_remote_copy(src_ref=src, dst_ref=dst.at[pk], **sem))
        return own, ici, ici_wait, fwd, fwd_wait

    def start(ins, outs, sems):
        own, ici, _, _, _ = copies(ins, outs, sems)
        for cp in own + ici:
            cp.start()

    def finish(ins, outs, sems):
        own, ici, ici_wait, fwd, fwd_wait = copies(ins, outs, sems)
        for i, cp in enumerate(ici_wait):
            cp.wait_recv()
            if i < len(fwd):
                fwd[i].start()
        for cp in fwd_wait + own:
            cp.wait_recv()
        for cp in own + ici + fwd:
            cp.wait_send()

    def middle(ins, outs, sems):
        _, _, ici_wait, fwd, _ = copies(ins, outs, sems)
        for i, cp in enumerate(ici_wait):
            cp.wait_recv()
            if i < len(fwd):
                fwd[i].start()

    def finish_late(ins, outs, sems):
        own, ici, _, fwd, fwd_wait = copies(ins, outs, sems)
        for cp in fwd_wait + own:
            cp.wait_recv()
        for cp in own + ici + fwd:
            cp.wait_send()

    srcs = list(halved) + list(whole)
    shapes = [jax.ShapeDtypeStruct((N_CHIP,) + ((blocks[ti][0],) if ti in blocks else ()) + s.shape, s.dtype)
              for ti, s in enumerate(srcs)]
    buffers = [(ti, blk[2]) for ti, blk in sorted(blocks.items()) if blk[2] is not None]
    aliases = {len(srcs) + bi: ti for bi, (ti, _) in enumerate(buffers)}
    return _Comm(srcs + [buf for _, buf in buffers], shapes,
                 [pltpu.SemaphoreType.DMA((3 * (n + nw),)), pltpu.SemaphoreType.DMA((3 * (n + nw),)),
                  pltpu.SemaphoreType.DMA((max(3 * n, 1),)), pltpu.SemaphoreType.DMA((max(3 * n, 1),)),
                  pltpu.SemaphoreType.DMA((n + nw,)), pltpu.SemaphoreType.DMA((n + nw,))],
                 start, finish, middle, finish_late, aliases)


def _sibling_exchange_comm(grads):
    n = len(grads)

    def copies(ins, outs, sems):
        send_sems, recv_sems = sems
        x, y, c = lax.axis_index("x"), lax.axis_index("y"), lax.axis_index("c")
        res = []
        for ti in range(n):
            half = ins[ti].shape[1] // 2
            res.append(pltpu.make_async_remote_copy(
                src_ref=ins[ti].at[:, pl.ds(pl.multiple_of((1 - c) * half, 16), half), :],
                dst_ref=outs[ti],
                send_sem=send_sems.at[ti], recv_sem=recv_sems.at[ti],
                device_id=(x, y, 1 - c), device_id_type=MESH))
        return res

    def start(ins, outs, sems):
        for cp in copies(ins, outs, sems):
            cp.start()

    def finish(ins, outs, sems):
        for cp in copies(ins, outs, sems):
            cp.wait()

    return _Comm(grads, [jax.ShapeDtypeStruct((N_CHIP, g.shape[1] // 2, g.shape[2]), g.dtype) for g in grads],
                 [pltpu.SemaphoreType.DMA((n,)), pltpu.SemaphoreType.DMA((n,))], start, finish)


def _in_proj_gathering(x2b, own, chip, tm, comm):
    t = x2b.shape[0]
    ni = t // tm
    half, cols = own.shape[1], own.shape[2]
    nci, nco = len(comm.ins), len(comm.out_shapes)

    def body(chip_ref, x_ref, own_ref, own_hbm, *rest):
        c_in = rest[:nci]
        h_ref, win_out = rest[nci:nci + 2]
        c_out = rest[nci + 2:nci + 2 + nco]
        w_scr, ici_send, ici_recv, d2d_send, d2d_recv, own_sems, ld_sems = rest[nci + 2 + nco:nci + 9 + nco]
        c_sem = rest[nci + 9 + nco:]
        s, i = pl.program_id(0), pl.program_id(1)
        x, y, c, peers = _chip_peers()
        me = 2 * x + y
        sibling = (x, y, 1 - c)

        def ici(k, slot):
            px, py = peers[k]
            return pltpu.make_async_remote_copy(
                src_ref=own_hbm.at[c], dst_ref=win_out.at[slot, c],
                send_sem=ici_send.at[k], recv_sem=ici_recv.at[k],
                device_id=(px, py, c), device_id_type=MESH)

        def forward(k, core):
            pk = 2 * peers[k][0] + peers[k][1]
            return pltpu.make_async_remote_copy(
                src_ref=win_out.at[pk, core], dst_ref=win_out.at[pk, core],
                send_sem=d2d_send.at[k], recv_sem=d2d_recv.at[k],
                device_id=sibling, device_id_type=MESH)

        place_own = pltpu.make_async_remote_copy(
            src_ref=own_hbm, dst_ref=win_out.at[me], send_sem=own_sems.at[0], recv_sem=own_sems.at[1],
            device_id=sibling, device_id_type=MESH)

        @pl.when((s == 0) & (i == 0))
        def _():
            for k in range(2):
                ici(k, me).start()
            place_own.start()

        @pl.when(s == 0)
        def _():
            xv = x_ref[...]
            h_ref[...] = (_dot(xv[:, :half], own_ref[0]) + _dot(xv[:, half:], own_ref[1])).astype(BF16)

        for k in range(3):
            @pl.when((s == k + 1) & (i == 0))
            def _(k=k):
                pk = 2 * peers[k][0] + peers[k][1]
                ici(k, pk).wait_recv()
                if k == 0:
                    ici(2, me).start()
                forward(k, c).start()
                forward(k, 1 - c).wait_recv()
                loads = [pltpu.make_async_copy(win_out.at[pk, hh], w_scr.at[hh], ld_sems.at[hh])
                         for hh in range(2)]
                for ld in loads:
                    ld.start()
                for ld in loads:
                    ld.wait()
                if k == 1:
                    comm.start(c_in, c_out, c_sem)

        @pl.when(s > 0)
        def _():
            xv = x_ref[...]
            h_ref[...] = (_dot(xv[:, :half], w_scr[0]) + _dot(xv[:, half:], w_scr[1])).astype(BF16)

        @pl.when((s == N_CHIP - 1) & (i == ni - 1))
        def _():
            place_own.wait()
            for k in range(3):
                ici(k, me).wait_send()
                forward(k, c).wait_send()
            comm.finish(c_in, c_out, c_sem)

    def shard_col(s, me):
        return jnp.where(s == 0, me, me ^ jnp.where(s == 1, 2, jnp.where(s == 2, 1, 3)))

    res = _pc(body, name="in_proj",
              out_shape=(jax.ShapeDtypeStruct((t, N_CHIP * cols), BF16),
                         jax.ShapeDtypeStruct((N_CHIP,) + own.shape, own.dtype)) + tuple(comm.out_shapes),
              grid=(N_CHIP, ni), nsp=1,
              in_specs=[pl.BlockSpec((tm, 2 * half), lambda s, i, chip_ref: (i, 0)),
                        pl.BlockSpec(own.shape, lambda s, i, chip_ref: (0, 0, 0)),
                        ANY] + [ANY] * nci,
              out_specs=(pl.BlockSpec((tm, cols), lambda s, i, chip_ref: (i, shard_col(s, chip_ref[0]))),
                         ANY) + tuple([ANY] * nco),
              scratch=[pltpu.VMEM(own.shape, own.dtype),
                       pltpu.SemaphoreType.DMA((3,)), pltpu.SemaphoreType.DMA((3,)),
                       pltpu.SemaphoreType.DMA((3,)), pltpu.SemaphoreType.DMA((3,)),
                       pltpu.SemaphoreType.DMA((2,)), pltpu.SemaphoreType.DMA((2,))] + comm.sems,
              sem=("arbitrary", "arbitrary"))(chip, x2b, own, own, *comm.ins)
    return res[0], res[1], res[2:]


def _rs_add_halves(name, grad, recv, core):
    _, r, cdim = grad.shape
    half = r // 2
    tr = _row_tile(half, cdim, mult=16)
    nr = half // tr

    def body(c_ref, g_ref, r_ref, o_ref):
        o_ref[...] = (g_ref[...].astype(F32) + r_ref[...].astype(F32)).astype(BF16)

    return _pc(body, name=name, out_shape=jax.ShapeDtypeStruct((N_CHIP, half, cdim), BF16),
               grid=(N_CHIP, nr), nsp=1,
               in_specs=[pl.BlockSpec((None, tr, cdim), lambda j, i, c_ref: (j, c_ref[0] * nr + i, 0)),
                         pl.BlockSpec((None, tr, cdim), lambda j, i, c_ref: (j, i, 0))],
               out_specs=pl.BlockSpec((None, tr, cdim), lambda j, i, c_ref: (j, i, 0)),
               sem=("parallel", "parallel"))(core, grad, recv)


def _chip_exchange_comm(parts):
    n = len(parts)

    def copies(ins, outs, sems):
        send_sems, recv_sems = sems
        x, y, c, peers = _chip_peers()
        return [pltpu.make_async_remote_copy(
            src_ref=ins[ti].at[2 * px + py], dst_ref=outs[ti].at[k],
            send_sem=send_sems.at[ti * 3 + k], recv_sem=recv_sems.at[ti * 3 + k],
            device_id=(px, py, c), device_id_type=MESH)
            for ti in range(n) for k, (px, py) in enumerate(peers)]

    def start(ins, outs, sems):
        for cp in copies(ins, outs, sems):
            cp.start()

    def finish(ins, outs, sems):
        for cp in copies(ins, outs, sems):
            cp.wait()

    return _Comm(parts, [jax.ShapeDtypeStruct((3,) + p.shape[1:], p.dtype) for p in parts],
                 [pltpu.SemaphoreType.DMA((3 * n,)), pltpu.SemaphoreType.DMA((3 * n,))], start, finish)


def _rs_sum_chips(name, part, recv, chip):
    _, half, cdim = recv.shape
    tr = _row_tile(half, cdim, mult=16)

    def body(chip_ref, p_ref, r_ref, o_ref):
        o_ref[...] = ((p_ref[...].astype(F32) + r_ref[0].astype(F32)) + r_ref[1].astype(F32)
                      ) + r_ref[2].astype(F32)

    return _pc(body, name=name, out_shape=jax.ShapeDtypeStruct((half, cdim), F32),
               grid=(half // tr,), nsp=1,
               in_specs=[pl.BlockSpec((None, tr, cdim), lambda i, chip_ref: (chip_ref[0], i, 0)),
                         pl.BlockSpec((3, tr, cdim), lambda i, chip_ref: (0, i, 0))],
               out_specs=pl.BlockSpec((tr, cdim), lambda i, chip_ref: (i, 0)),
               sem=("parallel",))(chip, part, recv)


def _rs_send_halves(halves):
    n = len(halves)

    def body(*refs):
        ins, outs = refs[:n], refs[n:2 * n]
        send_sems, recv_sems = refs[2 * n:]
        x, y, c = lax.axis_index("x"), lax.axis_index("y"), lax.axis_index("c")
        sends = []
        for ti in range(n):
            cp = pltpu.make_async_remote_copy(
                src_ref=ins[ti], dst_ref=outs[ti],
                send_sem=send_sems.at[ti], recv_sem=recv_sems.at[ti],
                device_id=(x, y, 1 - c), device_id_type=MESH)
            cp.start()
            sends.append(cp)
        for cp in sends:
            cp.wait()

    return _pc(body, name="rs_send_halves",
               out_shape=tuple(jax.ShapeDtypeStruct(hv.shape, hv.dtype) for hv in halves),
               in_specs=[ANY] * n, out_specs=tuple([ANY] * n),
               scratch=[pltpu.SemaphoreType.DMA((n,)), pltpu.SemaphoreType.DMA((n,))])(*halves)


def _adamw_rows(name, mine, theirs, w, m, v, core):
    half, cdim = mine.shape
    tr = _row_tile(half, cdim, budget=1 << 19)
    nrh = half // tr

    def body(c_ref, mine_ref, theirs_ref, w_ref, m_ref, v_ref, g_ref, d_ref, m2_ref, v2_ref):
        is_mine = (pl.program_id(0) // nrh) == c_ref[0]
        g = jnp.where(is_mine, mine_ref[...], theirs_ref[...])
        d, m2, v2 = _adamw(w_ref[...], g, m_ref[...], v_ref[...])
        g_ref[...] = g
        d_ref[...] = d
        m2_ref[...] = m2
        v2_ref[...] = v2

    htile = pl.BlockSpec((tr, cdim), lambda i, c_ref: (i % nrh, 0))
    tile = pl.BlockSpec((tr, cdim), lambda i, c_ref: (i, 0))
    shp = jax.ShapeDtypeStruct((2 * half, cdim), F32)
    return _pc(body, name=name, out_shape=(shp, shp, shp, shp), grid=(2 * nrh,), nsp=1,
               in_specs=[htile, htile, tile, tile, tile], out_specs=(tile, tile, tile, tile),
               sem=("parallel",))(core, mine, theirs, w, m, v)


def _adamw_whole(name, g, w, m, v):
    def body(g_ref, w_ref, m_ref, v_ref, d_ref, m2_ref, v2_ref):
        d, m2, v2 = _adamw(w_ref[...], g_ref[...], m_ref[...], v_ref[...])
        d_ref[...] = d
        m2_ref[...] = m2
        v2_ref[...] = v2

    shp = jax.ShapeDtypeStruct(g.shape, F32)
    return _pc(body, name=name, out_shape=(shp, shp, shp))(g, w, m, v)


SMALL_LAYOUT = (
    ("sgu_w_s", 1024, 1, 0),
    ("sgu_b_s", 8, 1, 1024),
    ("sgu_norm_g", 1, 0, 0),
    ("sgu_norm_b", 1, 0, 1),
    ("hgrn_norm_g", 1, 0, 3),
    ("ln1_g", 1, 0, 4),
    ("ln1_b", 1, 0, 5),
    ("ffn_conv_b", 1, 2, 3),
    ("ln2_g", 1, 0, 6),
    ("ln2_b", 1, 0, 7),
)
LB_ROW = 2
LOSS_ROW = 8
PACK_SHAPES = ((16, D_MODEL), (N_GROUP * 128 + 16, 128), (8, D_FF))
GATH_DTYPES = (F32, BF16, F32)


def _small_allreduce_adamw(rows1024, dws, dbs, dcw, dcb, logits, m_logits, v_logits,
                           small_w, small_m, small_v):
    ns = len(SMALL_LAYOUT)
    nr = len(rows1024)
    nb = len(PACK_SHAPES)

    def body(*refs):
        row_refs = refs[:nr]
        dws_ref, dbs_ref, dcw_ref, dcb_ref, lg_ref, mlg_ref, vlg_ref = refs[nr:nr + 7]
        pos = nr + 7
        w_refs = refs[pos:pos + ns]
        m_refs = refs[pos + ns:pos + 2 * ns]
        v_refs = refs[pos + 2 * ns:pos + 3 * ns]
        pos += 3 * ns
        loss_ref, dcw_out = refs[pos:pos + 2]
        lg_outs = refs[pos + 2:pos + 6]
        pos += 6
        outs = refs[pos:pos + 4 * ns]
        pos += 4 * ns
        pack = refs[pos:pos + nb]
        sib = refs[pos + nb:pos + 2 * nb]
        gath = refs[pos + 2 * nb:pos + 3 * nb]
        d2d_send, d2d_recv, ici_send, ici_recv = refs[pos + 3 * nb:]

        x, y, c, peers = _chip_peers()
        me = 2 * x + y
        sibling = (x, y, 1 - c)

        pack[0][...] = jnp.zeros(PACK_SHAPES[0], F32)
        for k in range(nr):
            pack[0][k:k + 1, :] = row_refs[k][0:1, :]
        pack[1][0:N_GROUP * 128, :] = dws_ref[...]
        pack[1][N_GROUP * 128:N_GROUP * 128 + 8, :] = dbs_ref[...]
        pack[1][N_GROUP * 128 + 8:, :] = jnp.zeros((8, 128), F32)
        pack[2][...] = jnp.zeros(PACK_SHAPES[2], F32)
        pack[2][0:3, :] = dcw_ref[0:3, :]
        pack[2][3:4, :] = dcb_ref[0:1, :]

        d2d = [pltpu.make_async_remote_copy(
            src_ref=pack[b], dst_ref=sib[b], send_sem=d2d_send.at[b], recv_sem=d2d_recv.at[b],
            device_id=sibling, device_id_type=MESH) for b in range(nb)]
        for cp in d2d:
            cp.start()
        for cp in d2d:
            cp.wait()
        for b in range(nb):
            gath[b][me] = (pack[b][...] + sib[b][...]).astype(GATH_DTYPES[b])

        ici, ici_wait = [], []
        for b in range(nb):
            for k, (px, py) in enumerate(peers):
                sem = dict(send_sem=ici_send.at[b * 3 + k], recv_sem=ici_recv.at[b * 3 + k],
                           device_id=(px, py, c), device_id_type=MESH)
                ici.append(pltpu.make_async_remote_copy(src_ref=gath[b].at[me], dst_ref=gath[b].at[me], **sem))
                ici_wait.append(pltpu.make_async_remote_copy(
                    src_ref=gath[b].at[me], dst_ref=gath[b].at[2 * px + py], **sem))
        for cp in ici:
            cp.start()
        for cp in ici_wait:
            cp.wait_recv()
        for cp in ici:
            cp.wait_send()

        tot = pack
        for b in range(nb):
            tot[b][...] = ((gath[b][0].astype(F32) + gath[b][1].astype(F32)) + gath[b][2].astype(F32)
                           ) + gath[b][3].astype(F32)

        loss_ref[...] = tot[0][LOSS_ROW:LOSS_ROW + 1, :]
        dcw_out[...] = tot[2][...]
        lb = _sig(lg_ref[0:1, :] - lg_ref[1:2, :])
        d0 = tot[0][LB_ROW:LB_ROW + 1, :] * lb * (1.0 - lb)
        rowid = lax.broadcasted_iota(jnp.int32, (2, D_MODEL), 0)
        g_lg = jnp.where(rowid == 0, d0, -d0)
        dl, ml, vl = _adamw(lg_ref[...], g_lg, mlg_ref[...], vlg_ref[...])
        lg_outs[0][...] = g_lg
        lg_outs[1][...] = dl
        lg_outs[2][...] = ml
        lg_outs[3][...] = vl
        for si, (_, rows, b, r0) in enumerate(SMALL_LAYOUT):
            g = tot[b][r0:r0 + rows, :]
            dl, ml, vl = _adamw(w_refs[si][...], g, m_refs[si][...], v_refs[si][...])
            outs[4 * si][...] = g
            outs[4 * si + 1][...] = dl
            outs[4 * si + 2][...] = ml
            outs[4 * si + 3][...] = vl

    shapes = [jax.ShapeDtypeStruct((1, D_MODEL), F32), jax.ShapeDtypeStruct((8, D_FF), F32)]
    shapes += [jax.ShapeDtypeStruct((2, D_MODEL), F32)] * 4
    for w in small_w:
        shapes += [jax.ShapeDtypeStruct(w.shape, F32)] * 4
    scratch = [pltpu.VMEM(shp, F32) for shp in PACK_SHAPES]
    scratch += [pltpu.VMEM(shp, F32) for shp in PACK_SHAPES]
    scratch += [pltpu.VMEM((N_CHIP,) + shp, dt) for shp, dt in zip(PACK_SHAPES, GATH_DTYPES)]
    scratch += [pltpu.SemaphoreType.DMA((nb,)), pltpu.SemaphoreType.DMA((nb,)),
                pltpu.SemaphoreType.DMA((3 * nb,)), pltpu.SemaphoreType.DMA((3 * nb,))]
    vm = pl.BlockSpec(memory_space=pltpu.VMEM)
    n_in = nr + 7 + 3 * ns
    res = _pc(body, name="small_allreduce_adamw", out_shape=tuple(shapes),
              in_specs=[vm] * n_in, out_specs=tuple([vm] * len(shapes)),
              scratch=scratch)(*rows1024, dws, dbs, dcw, dcb, logits, m_logits, v_logits,
                               *small_w, *small_m, *small_v)
    return res[0], res[1], res[2:6], res[6:]


def kernel(x, p, w_in, sgu_w_s, sgu_b_s, sgu_norm_g, sgu_norm_b, hgrn_lb_logits, hgrn_norm_g, w_branch, w_out, ln1_g, ln1_b, ffn_w_up, ffn_conv_w, ffn_conv_b, ffn_w_down, ln2_g, ln2_b, ple_w_proj, ple_w_gate, loss_target, m_w_in, m_sgu_w_s, m_sgu_b_s, m_sgu_norm_g, m_sgu_norm_b, m_hgrn_lb_logits, m_hgrn_norm_g, m_w_branch, m_w_out, m_ln1_g, m_ln1_b, m_ffn_w_up, m_ffn_conv_w, m_ffn_conv_b, m_ffn_w_down, m_ln2_g, m_ln2_b, m_ple_w_proj, m_ple_w_gate, v_w_in, v_sgu_w_s, v_sgu_b_s, v_sgu_norm_g, v_sgu_norm_b, v_hgrn_lb_logits, v_hgrn_norm_g, v_w_branch, v_w_out, v_ln1_g, v_ln1_b, v_ffn_w_up, v_ffn_conv_w, v_ffn_conv_b, v_ffn_w_down, v_ln2_g, v_ln2_b, v_ple_w_proj, v_ple_w_gate):
    t = x.shape[1]
    x2 = x.reshape(t, D_MODEL)
    x2b = x2.astype(BF16)
    p2 = p.reshape(t, PLE_DIM)
    tgt = loss_target.reshape(t, D_MODEL)
    core = lax.axis_index("c").astype(jnp.int32).reshape(1)
    chip_id = (2 * lax.axis_index("x") + lax.axis_index("y")).astype(jnp.int32).reshape(1)

    big_w = [w_in[0], w_branch[0, 0], w_branch[0, 1], w_out[0], ffn_w_up[0], ffn_w_down[0],
             ple_w_proj[0], ple_w_gate[0]]
    big_m = [m_w_in[0], m_w_branch[0, 0], m_w_branch[0, 1], m_w_out[0], m_ffn_w_up[0],
             m_ffn_w_down[0], m_ple_w_proj[0], m_ple_w_gate[0]]
    big_v = [v_w_in[0], v_w_branch[0, 0], v_w_branch[0, 1], v_w_out[0], v_ffn_w_up[0],
             v_ffn_w_down[0], v_ple_w_proj[0], v_ple_w_gate[0]]
    def halves_of(i):
        w = big_w[i]
        return w.astype(BF16).reshape(2, w.shape[0] // 2, w.shape[1])

    def stacked(g, i):
        return g.reshape(N_CHIP, big_w[i].shape[0], big_w[i].shape[1])


    cid = jnp.arange(SGU_BLOCK) // CHUNK
    maskf = (cid[:, None] >= cid[None, :]).astype(F32)
    ws_masked = sgu_w_s[0] * maskf[None]
    wm = ws_masked.astype(BF16)
    wmt = jnp.transpose(ws_masked, (0, 2, 1)).astype(BF16)
    bsb = jnp.broadcast_to(sgu_b_s[0][:, :, None], (N_GROUP, SGU_BLOCK, 128))

    up_rows = big_w[4].shape[0] // 2
    up_blocks = [big_w[4][k * up_rows:(k + 1) * up_rows].astype(BF16).reshape(2, up_rows // 2, -1)
                 for k in range(2)]
    h, win_g, (up_g,) = _in_proj_gathering(x2b, halves_of(0), chip_id, 512,
                                           _gather_comm([up_blocks[0]], blocks={0: (2, 0, None)}))
    win_st = stacked(win_g, 0)
    ya, _ = _sgu_fwd(h, wm, bsb, sgu_norm_g, sgu_norm_b)
    (yb, o_all, st_all), mix_g = _hgrn_fwd(
        h, hgrn_lb_logits, hgrn_norm_g,
        comm=_gather_comm([halves_of(i) for i in (1, 2, 3)] + [up_blocks[1]], [ffn_conv_w[0]],
                          blocks={3: (2, 1, up_g)}))
    wb0, wb1, wo = [stacked(g, i).reshape(D_MODEL, D_MODEL) for g, i in zip(mix_g[:3], (1, 2, 3))]
    wup_st = stacked(mix_g[3], 4)
    convw = jnp.transpose(mix_g[4], (1, 0, 2)).reshape(3, D_FF)
    (r1, a_br, b_br, m_bf, x1b), _ = _mix_fwd(ya, yb, h, x2, wb0, wb1, wo, ln1_g, ln1_b, 256)
    h2, act, out_g = _ffn_up_act(x1b, wup_st, convw, ffn_conv_b, 512,
                                 _gather_comm([halves_of(i) for i in (5, 6, 7)]))
    wd = stacked(out_g[0], 5).reshape(D_FF, D_MODEL)
    wpp = jnp.transpose(stacked(out_g[1], 6), (1, 0, 2)).reshape(PLE_DIM, D_MODEL)
    wpg = stacked(out_g[2], 7).reshape(D_MODEL, D_MODEL)
    dr2, dpg, dpp, loss_acc, dg2, db2 = _out_fwd_bwd(
        act, x1b, r1, p2, tgt, wd, wpg, wpp, ln1_g, ln1_b, ln2_g, ln2_b, 256)

    dh2, dr1, dcw, dcb, dg1, db1 = _ffn_bwd(h2, dr2, wd, wup_st, dpg, wpg, r1, ln1_g, convw, 256)
    d_wd = _mm_tn("ffn_down_wgrad", act, dr2, FF_TILE, 512)
    d_wpg = _mm_tn("ple_gate_wgrad", x1b, dpg, 512, D_MODEL)
    d_wpp_st = _mm_tn("ple_proj_wgrad", p2, dpp, PLE_DIM, PLE_DIM, stacked=True)
    d_wup_st = _mm("ffn_up_wgrad", x1b, dh2, TN, (2, N_CHIP),
                   pl.BlockSpec((t, 512), lambda i, j: (0, i)),
                   pl.BlockSpec((None, t, FF_TILE), lambda i, j: (j // FF_NJ, 0, j % FF_NJ)),
                   jax.ShapeDtypeStruct((N_CHIP, D_MODEL, FF_TILE), BF16),
                   pl.BlockSpec((None, 512, FF_TILE), lambda i, j: (j, i, 0)))
    da_bf, db_bf, dh, dya, dyb = _mix_bwd(dr1, h, a_br, b_br, wo, wb0, wb1, 256)
    d_wo = _mm_tn("out_proj_wgrad", m_bf, dr1, 512, 512)
    d_wb0 = _mm_tn("branch0_wgrad", ya, da_bf, 512, D_MODEL)
    d_wb1 = _mm_tn("branch1_wgrad", yb, db_bf, 512, D_MODEL)
    grads_1 = [d_wb0.reshape(4, 256, D_MODEL), d_wb1.reshape(4, 256, D_MODEL),
               d_wo.reshape(4, 256, D_MODEL), d_wup_st, d_wd.reshape(4, D_FF // 4, D_MODEL),
               d_wpp_st, d_wpg.reshape(4, 256, D_MODEL)]
    (dh, dws, dbs, dgv, dbv), recv_a1 = _sgu_bwd(h, dya, wm, wmt, bsb, sgu_norm_g, sgu_norm_b, maskf, dh,
                                                 comm=_sibling_exchange_comm(grads_1))
    parts_1 = [_rs_add_halves("rs_add_halves%d" % (i + 1), g, r, core)
               for i, (g, r) in enumerate(zip(grads_1, recv_a1))]
    (dh, dlb, dgn), recv_b1 = _hgrn_bwd(h, o_all, dyb, st_all, hgrn_lb_logits, hgrn_norm_g, dh,
                                         comm=_chip_exchange_comm(parts_1))

    grads_0 = [_in_proj_wgrad(x2b, dh, 512, D_MODEL)]
    recv_a0 = _run_comm("rs_sibling_exchange0", _sibling_exchange_comm(grads_0))
    parts_0 = [_rs_add_halves("rs_add_halves0", grads_0[0], recv_a0[0], core)]
    gx, recv_b0 = _in_proj_xgrad(dh, win_st, dr1, 512, _chip_exchange_comm(parts_0))
    parts = parts_0 + parts_1
    recv_b = list(recv_b0) + list(recv_b1)
    halves = [_rs_sum_chips("rs_sum_chips%d" % i, pt, r, chip_id)
              for i, (pt, r) in enumerate(zip(parts, recv_b))]
    theirs = _rs_send_halves(halves)
    big_out = [_adamw_rows("adamw_big%d" % i, halves[i], theirs[i], big_w[i], big_m[i], big_v[i], core)
               for i in range(len(halves))]

    small_in = dict(sgu_w_s=(sgu_w_s, m_sgu_w_s, v_sgu_w_s), sgu_b_s=(sgu_b_s, m_sgu_b_s, v_sgu_b_s),
                    sgu_norm_g=(sgu_norm_g, m_sgu_norm_g, v_sgu_norm_g),
                    sgu_norm_b=(sgu_norm_b, m_sgu_norm_b, v_sgu_norm_b),
                    hgrn_norm_g=(hgrn_norm_g, m_hgrn_norm_g, v_hgrn_norm_g),
                    ln1_g=(ln1_g, m_ln1_g, v_ln1_g), ln1_b=(ln1_b, m_ln1_b, v_ln1_b),
                    ffn_conv_b=(ffn_conv_b, m_ffn_conv_b, v_ffn_conv_b),
                    ln2_g=(ln2_g, m_ln2_g, v_ln2_g), ln2_b=(ln2_b, m_ln2_b, v_ln2_b))

    def flat(name, arr):
        rows = dict((n, r) for n, r, _, _ in SMALL_LAYOUT)[name]
        return arr.reshape(rows, arr.size // rows)

    names = [n for n, _, _, _ in SMALL_LAYOUT]
    sw = [flat(n, small_in[n][0]) for n in names]
    sm = [flat(n, small_in[n][1]) for n in names]
    sv = [flat(n, small_in[n][2]) for n in names]
    loss_rows, dcw_tot, lg_out, small_out = _small_allreduce_adamw(
        [dgv, dbv, dlb, dgn, dg1, db1, dg2, db2, loss_acc], dws.reshape(N_GROUP * 128, 128), dbs, dcw, dcb,
        hgrn_lb_logits, m_hgrn_lb_logits, v_hgrn_lb_logits, sw, sm, sv)
    loss = loss_rows[0, 0]

    chip = 2 * lax.axis_index("x") + lax.axis_index("y")
    g_cw = lax.dynamic_slice(dcw_tot, (0, chip * (D_FF // 4)), (3, D_FF // 4))
    cw_out = _adamw_whole("adamw_conv_w", g_cw, ffn_conv_w[0], m_ffn_conv_w[0], v_ffn_conv_w[0])

    res = {}
    for si, n in enumerate(names):
        shp = small_in[n][0].shape
        res[n] = tuple(small_out[4 * si + k].reshape(shp) for k in range(4))
    res["hgrn_lb_logits"] = tuple(lg_out)
    res["ffn_conv_w"] = (g_cw[None],) + tuple(o[None] for o in cw_out)

    def big(i):
        return tuple(big_out[i])

    res["w_in"] = tuple(o[None] for o in big(0))
    res["w_branch"] = tuple(jnp.stack([o0, o1])[None] for o0, o1 in zip(big(1), big(2)))
    res["w_out"] = tuple(o[None] for o in big(3))
    res["ffn_w_up"] = tuple(o[None] for o in big(4))
    res["ffn_w_down"] = tuple(o[None] for o in big(5))
    res["ple_w_proj"] = tuple(o[None] for o in big(6))
    res["ple_w_gate"] = tuple(o[None] for o in big(7))

    order = ["w_in", "sgu_w_s", "sgu_b_s", "sgu_norm_g", "sgu_norm_b", "hgrn_lb_logits",
             "hgrn_norm_g", "w_branch", "w_out", "ln1_g", "ln1_b", "ffn_w_up", "ffn_conv_w",
             "ffn_conv_b", "ffn_w_down", "ln2_g", "ln2_b", "ple_w_proj", "ple_w_gate"]
    outs = [loss, gx.reshape(1, t, D_MODEL)]
    for k in range(4):
        outs += [res[n][k] for n in order]
    return tuple(outs)
```

```python
import jax
import jax.numpy as jnp
from jax import lax
from jax.experimental import pallas as pl
from jax.experimental.pallas import tpu as pltpu

F32 = jnp.float32
BF16 = jnp.bfloat16
HIGHEST = lax.Precision.HIGHEST
MESH = pl.DeviceIdType.MESH

D_MODEL = 1024
CHUNK = 64
SGU_BLOCK = 128
SGU_STEP_BLOCKS = 4
SGU_ROWS = SGU_STEP_BLOCKS * SGU_BLOCK
N_GROUP = 8
N_HEAD = 8
HEAD_DIM = 128
D_FF = 2816
PLE_DIM = 256
LN_EPS = 1e-5
RMS_EPS = 1e-6
ALPHA = 2.0 ** 0.25
N_CHIP = 4

ADAM_LR = 0.001
ADAM_B1 = 0.9
ADAM_B2 = 0.999
ADAM_EPS = 1e-08
ADAM_WD = 0.01
ADAM_STEP = 10

VMEM_LIMIT = 56 * 1024 * 1024

NN = (((1,), (0,)), ((), ()))
NT = (((1,), (1,)), ((), ()))
TN = (((0,), (0,)), ((), ()))


def _pc(body, *, name, out_shape, grid=None, in_specs=None, out_specs=None, scratch=(),
        sem=None, nsp=0, vmem=VMEM_LIMIT, aliases=None):
    params = dict(vmem_limit_bytes=vmem)
    if sem is not None:
        params["dimension_semantics"] = sem
    kw = dict(name=name, out_shape=out_shape, compiler_params=pltpu.CompilerParams(**params))
    if aliases:
        kw["input_output_aliases"] = aliases
    if nsp:
        kw["grid_spec"] = pltpu.PrefetchScalarGridSpec(
            num_scalar_prefetch=nsp, grid=grid, in_specs=in_specs, out_specs=out_specs,
            scratch_shapes=list(scratch))
    else:
        if grid is not None:
            kw["grid"] = grid
        if in_specs is not None:
            kw["in_specs"] = in_specs
            kw["out_specs"] = out_specs
        kw["scratch_shapes"] = list(scratch)
    return pl.pallas_call(body, **kw)


def _dot(a, b, dims=NN):
    return lax.dot_general(a.astype(BF16), b.astype(BF16), dims, preferred_element_type=F32)


def _dot32(a, b, dims=NN):
    return lax.dot_general(a, b, dims, precision=HIGHEST, preferred_element_type=F32)


def _sig(x):
    return 1.0 / (1.0 + jnp.exp(-x))


_GC = 0.7978845608028654
_GA = 0.044715


def _gelu(x):
    return 0.5 * x * (1.0 + jnp.tanh(_GC * (x + _GA * x * x * x)))


def _gelu_and_grad(x):
    t = jnp.tanh(_GC * (x + _GA * x * x * x))
    g = 0.5 * x * (1.0 + t)
    dg = 0.5 * (1.0 + t) + 0.5 * x * (1.0 - t * t) * _GC * (1.0 + 3.0 * _GA * x * x)
    return g, dg


def _ln_stats(r):
    mu = jnp.mean(r, axis=-1, keepdims=True)
    xc = r - mu
    var = jnp.mean(xc * xc, axis=-1, keepdims=True)
    rstd = lax.rsqrt(var + LN_EPS)
    return xc * rstd, rstd


def _ln_bwd(dxh, xh, rstd):
    m1 = jnp.mean(dxh, axis=-1, keepdims=True)
    m2 = jnp.mean(dxh * xh, axis=-1, keepdims=True)
    return rstd * (dxh - m1 - xh * m2)


def _colsum8(v):
    return jnp.broadcast_to(jnp.sum(v, axis=0, keepdims=True), (8, v.shape[1]))


def _adamw(w, g, m, v):
    m2 = ADAM_B1 * m + (1.0 - ADAM_B1) * g
    v2 = ADAM_B2 * v + (1.0 - ADAM_B2) * (g * g)
    m_hat = m2 / (1.0 - ADAM_B1 ** ADAM_STEP)
    v_hat = v2 / (1.0 - ADAM_B2 ** ADAM_STEP)
    delta = -ADAM_LR * (m_hat / (jnp.sqrt(v_hat) + ADAM_EPS) + ADAM_WD * w)
    return delta, m2, v2


def _row_tile(rows, cols, itemsize=4, budget=1 << 20, mult=8):
    best = mult
    for tr in range(mult, rows + 1, mult):
        if rows % tr == 0 and tr * cols * itemsize <= budget:
            best = tr
    return best


def _mm(name, a, b, dims, grid, a_spec, b_spec, out_shape, o_spec):
    out_dtype = out_shape.dtype

    def body(a_ref, b_ref, o_ref):
        o_ref[...] = _dot(a_ref[...], b_ref[...], dims).astype(out_dtype)

    return _pc(body, name=name, out_shape=out_shape, grid=grid, in_specs=[a_spec, b_spec],
               out_specs=o_spec, sem=("parallel", "parallel"))(a, b)


class _Comm:
    def __init__(self, ins, out_shapes, sems, start, finish, middle=None, finish_late=None, aliases=None):
        self.ins, self.out_shapes, self.sems = list(ins), list(out_shapes), list(sems)
        self.start, self.finish = start, finish
        self.middle, self.finish_late = middle, finish_late
        self.aliases = aliases or {}


def _hosted_call(body, comm, first, last, *, name, out_shape, grid, in_specs, out_specs, scratch, sem,
                 args, aliases=None, mid=None):
    n_in, n_out, n_scr = len(in_specs), len(out_shape), len(scratch)
    nci, nco = len(comm.ins), len(comm.out_shapes)

    def wrapped(*refs):
        pos = n_in
        own_in, c_in = refs[:pos], refs[pos:pos + nci]
        pos += nci
        own_out, c_out = refs[pos:pos + n_out], refs[pos + n_out:pos + n_out + nco]
        pos += n_out + nco
        own_scr, c_sem = refs[pos:pos + n_scr], refs[pos + n_scr:]

        @pl.when(first())
        def _():
            comm.start(c_in, c_out, c_sem)

        body(*own_in, *own_out, *own_scr)

        if mid is not None:
            @pl.when(mid())
            def _():
                comm.middle(c_in, c_out, c_sem)

        @pl.when(last())
        def _():
            (comm.finish if mid is None else comm.finish_late)(c_in, c_out, c_sem)

    return _pc(wrapped, name=name, out_shape=tuple(out_shape) + tuple(comm.out_shapes), grid=grid,
               in_specs=list(in_specs) + [ANY] * nci, out_specs=tuple(out_specs) + tuple([ANY] * nco),
               scratch=list(scratch) + comm.sems, sem=sem,
               aliases={**(aliases or {}), **{n_in + ci: n_out + co for ci, co in comm.aliases.items()}},
               )(*args, *comm.ins)


def _grid1_call(body, comm, n, *, name, out_shape, in_specs, out_specs, scratch, args, aliases=None,
                mid_step=None):
    if comm is None:
        return _pc(body, name=name, out_shape=out_shape, grid=(n,), in_specs=in_specs, out_specs=out_specs,
                   scratch=scratch, sem=("arbitrary",), aliases=aliases)(*args), ()
    mid = None if mid_step is None else (lambda: pl.program_id(0) == mid_step)
    res = _hosted_call(body, comm, lambda: pl.program_id(0) == 0, lambda: pl.program_id(0) == n - 1,
                       name=name, out_shape=out_shape, grid=(n,), in_specs=in_specs, mid=mid,
                       out_specs=out_specs, scratch=scratch, sem=("arbitrary",), args=args, aliases=aliases)
    return res[:len(out_shape)], res[len(out_shape):]


def _run_comm(name, comm):
    nci, nco = len(comm.ins), len(comm.out_shapes)

    def body(*refs):
        c_in, c_out, c_sem = refs[:nci], refs[nci:nci + nco], refs[nci + nco:]
        comm.start(c_in, c_out, c_sem)
        comm.finish(c_in, c_out, c_sem)

    return _pc(body, name=name, out_shape=tuple(comm.out_shapes), in_specs=[ANY] * nci,
               out_specs=tuple([ANY] * nco), scratch=comm.sems)(*comm.ins)


def _mm_tn(name, a, b, tm, tn, stacked=False):
    t, m = a.shape
    _, n = b.shape
    if stacked:
        assert tm == m
        out_shape = jax.ShapeDtypeStruct((n // tn, m, tn), BF16)
        o_spec = pl.BlockSpec((None, tm, tn), lambda i, j: (j, 0, 0))
    else:
        out_shape = jax.ShapeDtypeStruct((m, n), BF16)
        o_spec = pl.BlockSpec((tm, tn), lambda i, j: (i, j))
    return _mm(name, a, b, TN, (m // tm, n // tn),
               pl.BlockSpec((t, tm), lambda i, j: (0, i)),
               pl.BlockSpec((t, tn), lambda i, j: (0, j)),
               out_shape, o_spec)


DH_SLOT = (2, 0, 1, 3)


def _dh_slot(j):
    return jnp.where(j == 3, 3, (j + 2) % 3)


def _in_proj_wgrad(x2b, dh, tm, tn):
    t, m = x2b.shape
    n = dh.shape[2]

    def body(a_ref, b_ref, o_ref):
        o_ref[...] = _dot(a_ref[...], b_ref[...], TN).astype(BF16)

    return _pc(body, name="in_proj_wgrad", out_shape=jax.ShapeDtypeStruct((N_CHIP, m, n), BF16),
               grid=(N_CHIP, m // tm, n // tn),
               in_specs=[pl.BlockSpec((t, tm), lambda j, i, k: (0, i)),
                         pl.BlockSpec((None, t, tn), lambda j, i, k: (_dh_slot(j), 0, k))],
               out_specs=pl.BlockSpec((None, tm, tn), lambda j, i, k: (j, i, k)),
               sem=("parallel", "parallel", "parallel"))(x2b, dh)


def _in_proj_xgrad(dh, win_st, dr1, tm, comm):
    t = dr1.shape[0]
    ni = t // tm

    def body(a_ref, b_ref, add_ref, o_ref, acc):
        j = pl.program_id(1)
        prod = _dot(a_ref[...], b_ref[...], NT)

        @pl.when(j == 0)
        def _():
            acc[...] = prod + ALPHA * add_ref[...]

        @pl.when((j > 0) & (j < N_CHIP - 1))
        def _():
            acc[...] += prod

        @pl.when(j == N_CHIP - 1)
        def _():
            o_ref[...] = acc[...] + prod

    tile = pl.BlockSpec((tm, D_MODEL), lambda i, j: (i, 0))
    res = _hosted_call(body, comm,
                       lambda: (pl.program_id(0) == 0) & (pl.program_id(1) == 0),
                       lambda: (pl.program_id(0) == ni - 1) & (pl.program_id(1) == N_CHIP - 1),
                       name="in_proj_xgrad", out_shape=(jax.ShapeDtypeStruct((t, D_MODEL), F32),),
                       grid=(ni, N_CHIP),
                       in_specs=[pl.BlockSpec((None, tm, 2 * D_MODEL), lambda i, j: (_dh_slot(j), i, 0)),
                                 pl.BlockSpec((None, D_MODEL, 2 * D_MODEL), lambda i, j: (j, 0, 0)),
                                 tile],
                       out_specs=(tile,), scratch=[pltpu.VMEM((tm, D_MODEL), F32)],
                       sem=("arbitrary", "arbitrary"), args=[dh, win_st, dr1])
    return res[0], res[1:]


def _sgu_mixed(v, wm_ref, bsb_ref, gv, bv):
    gl, dgl = _gelu_and_grad(v)
    vh, rstd = _ln_stats(gl)
    vn = vh * gv + bv
    mixed = []
    for g in range(N_GROUP):
        sl = slice(g * 128, (g + 1) * 128)
        mixed.append(_dot(wm_ref[g], vn[:, sl]) + bsb_ref[g])
    return dgl, vh, rstd, vn, mixed


def _sgu_fwd(h, wm, bsb, gv, bv, comm=None):
    t = h.shape[0]

    def body(u_ref, v_ref, wm_ref, bsb_ref, gv_ref, bv_ref, ya_ref):
        for bb in range(SGU_STEP_BLOCKS):
            rows = slice(bb * SGU_BLOCK, (bb + 1) * SGU_BLOCK)
            u = u_ref[rows, :].astype(F32)
            _, _, _, _, mixed = _sgu_mixed(v_ref[rows, :].astype(F32), wm_ref, bsb_ref, gv_ref[...],
                                           bv_ref[...])
            gu = _gelu(u)
            for g in range(N_GROUP):
                sl = slice(g * 128, (g + 1) * 128)
                ya_ref[rows, sl] = (gu[:, sl] * mixed[g]).astype(BF16)

    full3 = pl.BlockSpec((N_GROUP, 128, 128), lambda i: (0, 0, 0))
    vec = pl.BlockSpec((1, D_MODEL), lambda i: (0, 0))
    (ya,), extra = _grid1_call(
        body, comm, t // SGU_ROWS, name="sgu_fwd",
        out_shape=(jax.ShapeDtypeStruct((t, D_MODEL), BF16),),
        in_specs=[pl.BlockSpec((SGU_ROWS, D_MODEL), lambda i: (i, 0)),
                  pl.BlockSpec((SGU_ROWS, D_MODEL), lambda i: (i, 1)),
                  full3, full3, vec, vec],
        out_specs=(pl.BlockSpec((SGU_ROWS, D_MODEL), lambda i: (i, 0)),),
        scratch=[], args=(h, h, wm, bsb, gv, bv))
    return ya, extra


def _sgu_bwd(h, dya, wm, wmt, bsb, gv, bv, maskf, dh_buf, comm=None):
    t = h.shape[0]
    nb = t // SGU_ROWS

    def body(u_ref, v_ref, dya_ref, wm_ref, wmt_ref, bsb_ref, gv_ref, bv_ref, mask_ref, dh_buf_ref,
             dh_ref, dws_ref, dbs_ref, dgv_ref, dbv_ref, dmix_acc):
        i = pl.program_id(0)

        @pl.when(i == 0)
        def _():
            dws_ref[...] = jnp.zeros_like(dws_ref)
            dgv_ref[...] = jnp.zeros_like(dgv_ref)
            dbv_ref[...] = jnp.zeros_like(dbv_ref)
            dmix_acc[...] = jnp.zeros_like(dmix_acc)

        gvv = gv_ref[...]
        for bb in range(SGU_STEP_BLOCKS):
            rows = slice(bb * SGU_BLOCK, (bb + 1) * SGU_BLOCK)
            u = u_ref[rows, :].astype(F32)
            dgl_v, vh, rstd, vn, mixed = _sgu_mixed(v_ref[rows, :].astype(F32), wm_ref, bsb_ref, gvv,
                                                    bv_ref[...])
            gu, dgl_u = _gelu_and_grad(u)
            dya_v = dya_ref[rows, :].astype(F32)
            dvn_parts = []
            for g in range(N_GROUP):
                sl = slice(g * 128, (g + 1) * 128)
                d_y = dya_v[:, sl]
                dh_ref[rows, sl] = (d_y * mixed[g] * dgl_u[:, sl]).astype(BF16)
                d_mixed = d_y * gu[:, sl]
                dmix_acc[g] += d_mixed
                dws_ref[g] += _dot(d_mixed, vn[:, sl], NT) * mask_ref[...]
                dvn_parts.append(_dot(wmt_ref[g], d_mixed))
            dvn = jnp.concatenate(dvn_parts, axis=1)
            dgv_ref[...] += _colsum8(dvn * vh)
            dbv_ref[...] += _colsum8(dvn)
            d_gl = _ln_bwd(dvn * gvv, vh, rstd)
            dh_ref[rows, D_MODEL:] = (d_gl * dgl_v).astype(BF16)

        @pl.when(i == nb - 1)
        def _():
            rowid = lax.broadcasted_iota(jnp.int32, (8, 128), 0)
            ones = jnp.ones((8, 128), F32)
            acc = jnp.zeros((8, 128), F32)
            for g in range(N_GROUP):
                rs = _dot32(ones, dmix_acc[g], NT)
                acc = jnp.where(rowid == g, rs, acc)
            dbs_ref[...] = acc

    full3 = pl.BlockSpec((N_GROUP, 128, 128), lambda i: (0, 0, 0))
    vec = pl.BlockSpec((1, D_MODEL), lambda i: (0, 0))
    acc8 = pl.BlockSpec((8, D_MODEL), lambda i: (0, 0))
    return _grid1_call(
        body, comm, nb, name="sgu_bwd",
        out_shape=(jax.ShapeDtypeStruct(dh_buf.shape, BF16),
                   jax.ShapeDtypeStruct((N_GROUP, 128, 128), F32),
                   jax.ShapeDtypeStruct((8, 128), F32),
                   jax.ShapeDtypeStruct((8, D_MODEL), F32),
                   jax.ShapeDtypeStruct((8, D_MODEL), F32)),
        in_specs=[pl.BlockSpec((SGU_ROWS, D_MODEL), lambda i: (i, 0)),
                  pl.BlockSpec((SGU_ROWS, D_MODEL), lambda i: (i, 1)),
                  pl.BlockSpec((SGU_ROWS, D_MODEL), lambda i: (i, 0)),
                  full3, full3, full3, vec, vec,
                  pl.BlockSpec((128, 128), lambda i: (0, 0)), ANY],
        out_specs=(pl.BlockSpec((None, SGU_ROWS, 2 * D_MODEL), lambda i: (DH_SLOT[0], i, 0)),
                   full3, pl.BlockSpec((8, 128), lambda i: (0, 0)), acc8, acc8),
        scratch=[pltpu.VMEM((N_GROUP, 128, 128), F32)],
        args=(h, h, dya, wm, wmt, bsb, gv, bv, maskf, dh_buf), aliases={9: 0})


def _tri_masks():
    row = lax.broadcasted_iota(jnp.int32, (CHUNK, CHUNK), 0)
    col = lax.broadcasted_iota(jnp.int32, (CHUNK, CHUNK), 1)
    return col <= row, col >= row


def _heads(v):
    return [v[:, hd * HEAD_DIM:(hd + 1) * HEAD_DIM] for hd in range(N_HEAD)]


def _tri_cumsum(tri_bf, v):
    hi = v.astype(BF16)
    r = v - hi.astype(F32)
    mid = r.astype(BF16)
    lo = (r - mid.astype(F32)).astype(BF16)
    return _dot(tri_bf, hi) + _dot(tri_bf, mid) + _dot(tri_bf, lo)


def _hgrn_chunk(q, fp, ii, lb, st_heads, causal, with_o=True):
    sg = _sig(fp)
    f = lb + (1.0 - lb) * sg
    k = 1.0 - f
    c = _tri_cumsum(causal.astype(BF16), jnp.log(f))
    ec = jnp.exp(c)
    en = jnp.exp(-c)
    sq = _sig(q)
    qt = q * sq * ec
    kt = k * en
    ecl = jnp.exp(c[CHUNK - 1:CHUNK, :])
    kk = kt * ecl
    qtb, ktb, iib, kkb = qt.astype(BF16), kt.astype(BF16), ii.astype(BF16), kk.astype(BF16)
    attn, o = [], []
    for hd, (qh, kh, ih) in enumerate(zip(_heads(qtb), _heads(ktb), _heads(iib))):
        a = jnp.where(causal, _dot(qh, kh, NT), 0.0).astype(BF16)
        attn.append(a)
        if with_o:
            o.append(_dot(a, ih) + _dot(qh, st_heads[hd], NT))
    return dict(sg=sg, f=f, k=k, ec=ec, en=en, sq=sq, ecl=ecl, kk=kk, qtb=qtb, ktb=ktb, iib=iib,
                kkb=kkb, attn=attn, o=o)


def _rms_heads(o_heads):
    rinv = [lax.rsqrt(jnp.mean(o * o, axis=-1, keepdims=True) + RMS_EPS) for o in o_heads]
    return rinv, jnp.concatenate([o * r for o, r in zip(o_heads, rinv)], axis=1)


HG_CHUNKS = 8
HG_ROWS = HG_CHUNKS * CHUNK


def _hgrn_fwd(h, logits, gn, comm=None):
    t = h.shape[0]
    nb = t // HG_ROWS

    def body(q_ref, f_ref, i_ref, og_ref, lg_ref, gn_ref, yb_ref, o_ref, st_ref, state):
        @pl.when(pl.program_id(0) == 0)
        def _():
            state[...] = jnp.zeros_like(state)

        causal, _ = _tri_masks()
        lb = _sig(lg_ref[0:1, :] - lg_ref[1:2, :])
        gnv = gn_ref[...]
        st = [state[hd] for hd in range(N_HEAD)]
        for cc in range(HG_CHUNKS):
            rows = slice(cc * CHUNK, (cc + 1) * CHUNK)
            og = og_ref[rows, :].astype(F32)
            r = _hgrn_chunk(q_ref[rows, :].astype(F32), f_ref[rows, :].astype(F32),
                            i_ref[rows, :].astype(F32), lb, [s.astype(BF16) for s in st], causal)
            o_bf = jnp.concatenate(r["o"], axis=1).astype(BF16)
            o_ref[rows, :] = o_bf
            _, on = _rms_heads(_heads(o_bf.astype(F32)))
            yb_ref[rows, :] = (on * gnv * (og * _sig(og))).astype(BF16)
            for hd in range(N_HEAD):
                st_ref[cc, hd] = st[hd]
            st = [s * e + _dot(ih, kh, TN)
                  for s, e, ih, kh in zip(st, _heads(r["ecl"]), _heads(r["iib"]), _heads(r["kkb"]))]
        for hd in range(N_HEAD):
            state[hd] = st[hd]

    def col(k):
        return pl.BlockSpec((HG_ROWS, D_MODEL), lambda ci: (ci, k))

    return _grid1_call(body, comm, nb, name="hgrn_fwd",
                       out_shape=(jax.ShapeDtypeStruct((t, D_MODEL), BF16),
                                  jax.ShapeDtypeStruct((t, D_MODEL), BF16),
                                  jax.ShapeDtypeStruct((t // CHUNK, N_HEAD, HEAD_DIM, HEAD_DIM), F32)),
                       in_specs=[col(2), col(3), col(4), col(5),
                                 pl.BlockSpec((2, D_MODEL), lambda ci: (0, 0)),
                                 pl.BlockSpec((1, D_MODEL), lambda ci: (0, 0))],
                       out_specs=(pl.BlockSpec((HG_ROWS, D_MODEL), lambda ci: (ci, 0)),
                                  pl.BlockSpec((HG_ROWS, D_MODEL), lambda ci: (ci, 0)),
                                  pl.BlockSpec((HG_CHUNKS, N_HEAD, HEAD_DIM, HEAD_DIM),
                                               lambda ci: (ci, 0, 0, 0))),
                       scratch=[pltpu.VMEM((N_HEAD, HEAD_DIM, HEAD_DIM), F32)],
                       args=(h, h, h, h, logits, gn),
                       mid_step=None if comm is None or comm.middle is None else max(nb - 2, 0))


def _hgrn_chunk_bwd(q, fp, ii, og, o_saved, dy, gnv, lb, st, dsn, causal, anti):
    stb = [s.astype(BF16) for s in st]
    dsnb = [s.astype(BF16) for s in dsn]
    r = _hgrn_chunk(q, fp, ii, lb, stb, causal, with_o=False)
    rinv, on = _rms_heads(_heads(o_saved))
    so = _sig(og)
    sil = og * so
    d_og = dy * on * gnv * (so * (1.0 + og * (1.0 - so)))
    d_on = dy * gnv * sil
    d_ob = jnp.concatenate(
        [ri * (dn - oh * jnp.mean(dn * oh, axis=-1, keepdims=True))
         for ri, dn, oh in zip(rinv, _heads(d_on), _heads(on))], axis=1).astype(BF16)
    d_i, d_qt, d_kt, d_kk, d_st, st_dsn = [], [], [], [], [], []
    ecl = _heads(r["ecl"])
    for hd, (dh, qh, kh, ih, kkh) in enumerate(zip(_heads(d_ob), _heads(r["qtb"]), _heads(r["ktb"]),
                                                   _heads(r["iib"]), _heads(r["kkb"]))):
        d_attn = jnp.where(causal, _dot(dh, ih, NT), 0.0).astype(BF16)
        d_i.append(_dot(r["attn"][hd], dh, TN) + _dot(kkh, dsnb[hd], NT))
        d_qt.append(_dot(d_attn, kh) + _dot(dh, stb[hd]))
        d_kt.append(_dot(d_attn, qh, TN))
        d_kk.append(_dot(ih, dsnb[hd]))
        d_st.append(_dot(dh, qh, TN) + dsn[hd] * ecl[hd])
        st_dsn.append(jnp.sum(st[hd] * dsn[hd], axis=0, keepdims=True))
    d_qt = jnp.concatenate(d_qt, axis=1)
    d_kt = jnp.concatenate(d_kt, axis=1)
    d_kk = jnp.concatenate(d_kk, axis=1)
    kk = r["kk"]
    d_cl = r["ecl"] * jnp.concatenate(st_dsn, axis=1) + jnp.sum(kk * d_kk, axis=0, keepdims=True)
    d_k = (d_kk * r["ecl"] + d_kt) * r["en"]
    d_c = d_qt * r["qtb"].astype(F32) - d_kt * r["ktb"].astype(F32) - d_kk * kk
    rowid = lax.broadcasted_iota(jnp.int32, (CHUNK, D_MODEL), 0)
    d_c = d_c + jnp.where(rowid == CHUNK - 1, d_cl, 0.0)
    d_lf = _tri_cumsum(anti.astype(BF16), d_c)
    d_f = d_lf / r["f"] - d_k
    sg, sq = r["sg"], r["sq"]
    d_q = d_qt * r["ec"] * (sq * (1.0 + q * (1.0 - sq)))
    d_fp = d_f * (1.0 - lb) * sg * (1.0 - sg)
    return (d_q, d_fp, jnp.concatenate(d_i, axis=1), d_og, d_st,
            _colsum8(dy * on * sil), _colsum8(d_f * (1.0 - sg)))


def _hgrn_bwd(h, o_all, dyb, st_all, logits, gn, dh_buf, comm=None):
    t = h.shape[0]
    nb = t // HG_ROWS

    def body(q_ref, f_ref, i_ref, og_ref, o_ref, dyb_ref, st_ref, lg_ref, gn_ref, dh_buf_ref,
             dh_ref, dlb_ref, dgn_ref, dstate):
        @pl.when(pl.program_id(0) == 0)
        def _():
            dstate[...] = jnp.zeros_like(dstate)
            dlb_ref[...] = jnp.zeros_like(dlb_ref)
            dgn_ref[...] = jnp.zeros_like(dgn_ref)

        causal, anti = _tri_masks()
        lb = _sig(lg_ref[0:1, :] - lg_ref[1:2, :])
        gnv = gn_ref[...]
        dsn = [dstate[hd] for hd in range(N_HEAD)]
        dgn_acc = jnp.zeros((8, D_MODEL), F32)
        dlb_acc = jnp.zeros((8, D_MODEL), F32)
        for cc in reversed(range(HG_CHUNKS)):
            rows = slice(cc * CHUNK, (cc + 1) * CHUNK)
            d_q, d_fp, d_i, d_og, dsn, dgn_c, dlb_c = _hgrn_chunk_bwd(
                q_ref[rows, :].astype(F32), f_ref[rows, :].astype(F32), i_ref[rows, :].astype(F32),
                og_ref[rows, :].astype(F32), o_ref[rows, :].astype(F32), dyb_ref[rows, :].astype(F32), gnv, lb,
                [st_ref[cc, hd] for hd in range(N_HEAD)], dsn, causal, anti)
            dgn_acc = dgn_acc + dgn_c
            dlb_acc = dlb_acc + dlb_c
            dh_ref[0, rows, :D_MODEL] = d_q.astype(BF16)
            dh_ref[0, rows, D_MODEL:] = d_fp.astype(BF16)
            dh_ref[1, rows, :D_MODEL] = d_i.astype(BF16)
            dh_ref[1, rows, D_MODEL:] = d_og.astype(BF16)
        dgn_ref[...] += dgn_acc
        dlb_ref[...] += dlb_acc
        for hd in range(N_HEAD):
            dstate[hd] = dsn[hd]

    def col(k):
        return pl.BlockSpec((HG_ROWS, D_MODEL), lambda ci: (nb - 1 - ci, k))

    acc8 = pl.BlockSpec((8, D_MODEL), lambda ci: (0, 0))
    pair = pl.BlockSpec((2, HG_ROWS, 2 * D_MODEL), lambda ci: (0, nb - 1 - ci, 0))
    return _grid1_call(body, comm, nb, name="hgrn_bwd",
                       out_shape=(jax.ShapeDtypeStruct(dh_buf.shape, BF16),
                                  jax.ShapeDtypeStruct((8, D_MODEL), F32),
                                  jax.ShapeDtypeStruct((8, D_MODEL), F32)),
                       in_specs=[col(2), col(3), col(4), col(5), col(0),
                                 pl.BlockSpec((HG_ROWS, D_MODEL), lambda ci: (nb - 1 - ci, 0)),
                                 pl.BlockSpec((HG_CHUNKS, N_HEAD, HEAD_DIM, HEAD_DIM),
                                              lambda ci: (nb - 1 - ci, 0, 0, 0)),
                                 pl.BlockSpec((2, D_MODEL), lambda ci: (0, 0)),
                                 pl.BlockSpec((1, D_MODEL), lambda ci: (0, 0)), ANY],
                       out_specs=(pair, acc8, acc8),
                       scratch=[pltpu.VMEM((N_HEAD, HEAD_DIM, HEAD_DIM), F32)],
                       args=(h, h, h, h, o_all, dyb, st_all, logits, gn, dh_buf), aliases={9: 0})


def _mix_fwd(ya, yb, h, x, wb0, wb1, wo, g1, b1, tm, comm=None):
    t = x.shape[0]

    def body(ya_ref, yb_ref, ga_ref, gb_ref, x_ref, wb0_ref, wb1_ref, wo_ref, g1_ref, b1_ref,
             r1_ref, a_ref, b_ref, m_ref, x1_ref):
        a = _dot(ya_ref[...], wb0_ref[...])
        b = _dot(yb_ref[...], wb1_ref[...])
        m = _sig(ga_ref[...].astype(F32)) * a + _sig(gb_ref[...].astype(F32)) * b
        r1 = ALPHA * x_ref[...] + _dot(m, wo_ref[...])
        xh, _ = _ln_stats(r1)
        r1_ref[...] = r1
        a_ref[...] = a.astype(BF16)
        b_ref[...] = b.astype(BF16)
        m_ref[...] = m.astype(BF16)
        x1_ref[...] = (xh * g1_ref[...] + b1_ref[...]).astype(BF16)

    tile = pl.BlockSpec((tm, D_MODEL), lambda i: (i, 0))
    wsp = pl.BlockSpec((D_MODEL, D_MODEL), lambda i: (0, 0))
    vec = pl.BlockSpec((1, D_MODEL), lambda i: (0, 0))
    f32o = jax.ShapeDtypeStruct((t, D_MODEL), F32)
    bfo = jax.ShapeDtypeStruct((t, D_MODEL), BF16)
    return _grid1_call(body, comm, t // tm, name="mix_fwd", out_shape=(f32o, bfo, bfo, bfo, bfo),
                       in_specs=[tile, tile,
                                 pl.BlockSpec((tm, D_MODEL), lambda i: (i, 6)),
                                 pl.BlockSpec((tm, D_MODEL), lambda i: (i, 7)),
                                 tile, wsp, wsp, wsp, vec, vec],
                       out_specs=(tile, tile, tile, tile, tile),
                       scratch=[], args=(ya, yb, h, h, x, wb0, wb1, wo, g1, b1))


def _mix_bwd(dr1, h, a, b, wo, wb0, wb1, tm):
    t = dr1.shape[0]

    def body(dr1_ref, ga_ref, gb_ref, a_ref, b_ref, wo_ref, wb0_ref, wb1_ref,
             da_ref, db_ref, dh3_ref, dya_ref, dyb_ref):
        d_m = _dot(dr1_ref[...], wo_ref[...], NT)
        sa = _sig(ga_ref[...].astype(F32))
        sb = _sig(gb_ref[...].astype(F32))
        d_a = (d_m * sa).astype(BF16)
        d_b = (d_m * sb).astype(BF16)
        da_ref[...] = d_a
        db_ref[...] = d_b
        dh3_ref[:, :D_MODEL] = (d_m * a_ref[...].astype(F32) * sa * (1.0 - sa)).astype(BF16)
        dh3_ref[:, D_MODEL:] = (d_m * b_ref[...].astype(F32) * sb * (1.0 - sb)).astype(BF16)
        dya_ref[...] = _dot(d_a, wb0_ref[...], NT).astype(BF16)
        dyb_ref[...] = _dot(d_b, wb1_ref[...], NT).astype(BF16)

    tile = pl.BlockSpec((tm, D_MODEL), lambda i: (i, 0))
    wsp = pl.BlockSpec((D_MODEL, D_MODEL), lambda i: (0, 0))
    f32o = jax.ShapeDtypeStruct((t, D_MODEL), F32)
    bfo = jax.ShapeDtypeStruct((t, D_MODEL), BF16)
    return _pc(body, name="mix_bwd",
               out_shape=(bfo, bfo, jax.ShapeDtypeStruct((N_CHIP, t, 2 * D_MODEL), BF16), bfo, bfo),
               grid=(t // tm,),
               in_specs=[tile,
                         pl.BlockSpec((tm, D_MODEL), lambda i: (i, 6)),
                         pl.BlockSpec((tm, D_MODEL), lambda i: (i, 7)),
                         tile, tile, wsp, wsp, wsp],
               out_specs=(tile, tile, pl.BlockSpec((None, tm, 2 * D_MODEL), lambda i: (DH_SLOT[3], i, 0)),
                          tile, tile),
               sem=("parallel",))(dr1, h, h, a, b, wo, wb0, wb1)


FF_TILE = 1408
FF_NJ = D_FF // FF_TILE


def _shift_down(v, k):
    return pltpu.roll(v, k, 0)


def _shift_up(v, k):
    return pltpu.roll(v, v.shape[0] - k, 0)


HALO = 16
FF_PIECES = ((0, 768), (768, FF_TILE))


def _ffn_up_act(x1b, wup_st, convw, convb, tm, comm):
    t = x1b.shape[0]
    ni = t // tm
    nth = tm // HALO

    def body(x_ref, xp_ref, wg_ref, wv_ref, cw_ref, cb_ref, h2_ref, act_ref):
        wg = wg_ref[...]
        gate = _dot(x_ref[...], wg).astype(BF16)
        val = _dot(x_ref[...], wv_ref[...]).astype(BF16)
        prev = (_dot(xp_ref[...], wg) * (pl.program_id(0) > 0).astype(F32)).astype(BF16)
        h2_ref[0] = gate
        h2_ref[1] = val
        ext = jnp.concatenate([prev.astype(F32), gate.astype(F32)], axis=0)
        gc = (cw_ref[0:1, :] * _shift_down(ext, 2) + cw_ref[1:2, :] * _shift_down(ext, 1)
              + cw_ref[2:3, :] * ext + cb_ref[...])[HALO:, :].astype(BF16)
        h2_ref[2] = gc
        act_ref[...] = (_gelu(gc.astype(F32)) * val.astype(F32)).astype(BF16)

    res = _hosted_call(
        body, comm,
        lambda: (pl.program_id(0) == 0) & (pl.program_id(1) == 0),
        lambda: (pl.program_id(0) == ni - 1) & (pl.program_id(1) == FF_NJ - 1),
        mid=lambda: (pl.program_id(0) == max(ni - 2, 0)) & (pl.program_id(1) == 0),
        name="ffn_up",
        out_shape=(jax.ShapeDtypeStruct((3, t, D_FF), BF16), jax.ShapeDtypeStruct((t, D_FF), BF16)),
        grid=(ni, FF_NJ),
        in_specs=[pl.BlockSpec((tm, D_MODEL), lambda i, j: (i, 0)),
                  pl.BlockSpec((HALO, D_MODEL), lambda i, j: (jnp.maximum(i * nth - 1, 0), 0)),
                  pl.BlockSpec((None, D_MODEL, FF_TILE), lambda i, j: (j, 0, 0)),
                  pl.BlockSpec((None, D_MODEL, FF_TILE), lambda i, j: (j + FF_NJ, 0, 0)),
                  pl.BlockSpec((3, FF_TILE), lambda i, j: (0, j)),
                  pl.BlockSpec((1, FF_TILE), lambda i, j: (0, j))],
        out_specs=(pl.BlockSpec((3, tm, FF_TILE), lambda i, j: (0, i, j)),
                   pl.BlockSpec((tm, FF_TILE), lambda i, j: (i, j))),
        scratch=[], sem=("arbitrary", "arbitrary"),
        args=(x1b, x1b, wup_st, wup_st, convw, convb))
    return res[0], res[1], res[2:]


def _out_fwd_bwd(act, x1b, r1, p2, tgt, wd, wpg, wpp, g1, b1, g2, b2, tm):
    t = r1.shape[0]

    def body(act_ref, x1b_ref, r1_ref, p_ref, tgt_ref, wd_ref, wpg_ref, wpp_ref,
             g1_ref, b1_ref, g2_ref, b2_ref,
             dr2_ref, dpg_ref, dpp_ref, loss_ref, dg2_ref, db2_ref):
        i = pl.program_id(0)

        @pl.when(i == 0)
        def _():
            loss_ref[...] = jnp.zeros_like(loss_ref)
            dg2_ref[...] = jnp.zeros_like(dg2_ref)
            db2_ref[...] = jnp.zeros_like(db2_ref)

        ffn = _dot(act_ref[...], wd_ref[...])
        pg = _dot(x1b_ref[...], wpg_ref[...])
        pp = _dot(p_ref[...], wpp_ref[...])
        s = _sig(pg)
        xh1, _ = _ln_stats(r1_ref[...])
        x1 = xh1 * g1_ref[...] + b1_ref[...]
        r2 = ALPHA * x1 + ffn + s * pp
        xh2, rstd2 = _ln_stats(r2)
        g2v = g2_ref[...]
        diff = xh2 * g2v + b2_ref[...] - tgt_ref[...]
        part = jnp.sum(jnp.sum(diff * diff, axis=1, keepdims=True), axis=0, keepdims=True)
        loss_ref[...] += jnp.broadcast_to(part * (0.5 / D_MODEL), loss_ref.shape)
        dy = diff * (1.0 / D_MODEL)
        dg2_ref[...] += _colsum8(dy * xh2)
        db2_ref[...] += _colsum8(dy)
        dr2 = _ln_bwd(dy * g2v, xh2, rstd2)
        dr2_ref[...] = dr2
        dpg_ref[...] = (dr2 * pp * s * (1.0 - s)).astype(BF16)
        dpp_ref[...] = (dr2 * s).astype(BF16)

    tile = pl.BlockSpec((tm, D_MODEL), lambda i: (i, 0))
    vec = pl.BlockSpec((1, D_MODEL), lambda i: (0, 0))
    acc8 = pl.BlockSpec((8, D_MODEL), lambda i: (0, 0))
    acc_shape = jax.ShapeDtypeStruct((8, D_MODEL), F32)
    return _pc(body, name="out_fwd_bwd",
               out_shape=(jax.ShapeDtypeStruct((t, D_MODEL), F32),
                          jax.ShapeDtypeStruct((t, D_MODEL), BF16),
                          jax.ShapeDtypeStruct((t, D_MODEL), BF16),
                          acc_shape, acc_shape, acc_shape),
               grid=(t // tm,),
               in_specs=[pl.BlockSpec((tm, D_FF), lambda i: (i, 0)), tile, tile,
                         pl.BlockSpec((tm, PLE_DIM), lambda i: (i, 0)), tile,
                         pl.BlockSpec((D_FF, D_MODEL), lambda i: (0, 0)),
                         pl.BlockSpec((D_MODEL, D_MODEL), lambda i: (0, 0)),
                         pl.BlockSpec((PLE_DIM, D_MODEL), lambda i: (0, 0)),
                         vec, vec, vec, vec],
               out_specs=(tile, tile, tile, acc8, acc8, acc8),
               sem=("arbitrary",))(act, x1b, r1, p2, tgt, wd, wpg, wpp, g1, b1, g2, b2)


def _ffn_bwd(h2, dr2, wd, wup_st, dpg, wpg, r1, g1, convw, tm):
    t = r1.shape[0]
    ni = t // tm
    nth = tm // HALO
    last_halo = t // HALO - 1
    main_rows = slice(0, tm)

    def body(g_ref, gc_ref, gcn_ref, v_ref, vn_ref, dr2_ref, dr2n_ref, wd_ref, wug_ref, wuv_ref,
             cw_ref, dpg_ref, wpg_ref, r1_ref, g1_ref,
             dh2_ref, dr1_ref, dcw_ref, dcb_ref, dg1_ref, db1_ref, acc):
        i = pl.program_id(0)
        j = pl.program_id(1)

        @pl.when((i == 0) & (j == 0))
        def _():
            dcw_ref[...] = jnp.zeros_like(dcw_ref)
            dcb_ref[...] = jnp.zeros_like(dcb_ref)
            dg1_ref[...] = jnp.zeros_like(dg1_ref)
            db1_ref[...] = jnp.zeros_like(db1_ref)

        dr2v = dr2_ref[...].astype(BF16)
        dr2n = dr2n_ref[...].astype(BF16)
        more = (i < ni - 1).astype(F32)
        prod = None
        dcw_parts, dcb_parts = [], []
        for c0, c1 in FF_PIECES:
            pc = slice(c0, c1)
            da = _dot(dr2v, wd_ref[pc, :], NT)
            dnext = _dot(dr2n, wd_ref[pc, :], NT) * more
            gc = jnp.concatenate([gc_ref[:, pc].astype(F32), gcn_ref[:, pc].astype(F32)], axis=0)
            vext = jnp.concatenate([v_ref[:, pc].astype(F32), vn_ref[:, pc].astype(F32)], axis=0)
            dext = jnp.concatenate([da, dnext], axis=0)
            gl, dgl = _gelu_and_grad(gc)
            d_gc = dext * vext * dgl
            up1 = _shift_up(d_gc, 1)[main_rows, :]
            up2 = _shift_up(d_gc, 2)[main_rows, :]
            dm = d_gc[main_rows, :]
            d_gate = (cw_ref[2:3, pc] * dm + cw_ref[1:2, pc] * up1 + cw_ref[0:1, pc] * up2).astype(BF16)
            d_val = (da * gl[main_rows, :]).astype(BF16)
            dh2_ref[0, :, pc] = d_gate
            dh2_ref[1, :, pc] = d_val
            g = g_ref[:, pc].astype(F32)
            s0 = jnp.sum(g * up2, axis=0, keepdims=True)
            s1 = jnp.sum(g * up1, axis=0, keepdims=True)
            s2 = jnp.sum(g * dm, axis=0, keepdims=True)
            rowid = lax.broadcasted_iota(jnp.int32, (8, c1 - c0), 0)
            dcw_parts.append(jnp.where(rowid == 0, s0, jnp.where(rowid == 1, s1,
                                                                 jnp.where(rowid == 2, s2, 0.0))))
            dcb_parts.append(_colsum8(dm))
            part = _dot(d_gate, wug_ref[:, pc], NT) + _dot(d_val, wuv_ref[:, pc], NT)
            prod = part if prod is None else prod + part
        dcw_part = jnp.concatenate(dcw_parts, axis=1)
        dcb_part = jnp.concatenate(dcb_parts, axis=1)
        for jj in range(FF_NJ):
            @pl.when(j == jj)
            def _(jj=jj):
                cols = slice(jj * FF_TILE, (jj + 1) * FF_TILE)
                dcw_ref[:, cols] += dcw_part
                dcb_ref[:, cols] += dcb_part

        @pl.when(j == 0)
        def _():
            acc[...] = prod

        @pl.when(j > 0)
        def _():
            acc[...] += prod

        @pl.when(j == FF_NJ - 1)
        def _():
            d_x1 = acc[...] + _dot(dpg_ref[...], wpg_ref[...], NT) + ALPHA * dr2_ref[...]
            xh, rstd = _ln_stats(r1_ref[...])
            dg1_ref[...] += _colsum8(d_x1 * xh)
            db1_ref[...] += _colsum8(d_x1)
            dr1_ref[...] = _ln_bwd(d_x1 * g1_ref[...], xh, rstd)

    def h2_main(part):
        return pl.BlockSpec((None, tm, FF_TILE), lambda i, j: (part, i, j))

    def h2_next(part):
        return pl.BlockSpec((None, HALO, FF_TILE),
                            lambda i, j: (part, jnp.minimum((i + 1) * nth, last_halo), j))

    tile = pl.BlockSpec((tm, D_MODEL), lambda i, j: (i, 0))
    acc8 = pl.BlockSpec((8, D_MODEL), lambda i, j: (0, 0))
    accff = pl.BlockSpec((8, D_FF), lambda i, j: (0, 0))
    acc_shape = jax.ShapeDtypeStruct((8, D_MODEL), F32)
    accff_shape = jax.ShapeDtypeStruct((8, D_FF), F32)
    return _pc(body, name="ffn_bwd",
               out_shape=(jax.ShapeDtypeStruct((2, t, D_FF), BF16),
                          jax.ShapeDtypeStruct((t, D_MODEL), F32),
                          accff_shape, accff_shape, acc_shape, acc_shape),
               grid=(ni, FF_NJ),
               in_specs=[h2_main(0), h2_main(2), h2_next(2), h2_main(1), h2_next(1),
                         tile,
                         pl.BlockSpec((HALO, D_MODEL), lambda i, j: (jnp.minimum((i + 1) * nth, last_halo), 0)),
                         pl.BlockSpec((FF_TILE, D_MODEL), lambda i, j: (j, 0)),
                         pl.BlockSpec((None, D_MODEL, FF_TILE), lambda i, j: (j, 0, 0)),
                         pl.BlockSpec((None, D_MODEL, FF_TILE), lambda i, j: (j + FF_NJ, 0, 0)),
                         pl.BlockSpec((3, FF_TILE), lambda i, j: (0, j)),
                         tile, pl.BlockSpec((D_MODEL, D_MODEL), lambda i, j: (0, 0)),
                         tile, pl.BlockSpec((1, D_MODEL), lambda i, j: (0, 0))],
               out_specs=(pl.BlockSpec((2, tm, FF_TILE), lambda i, j: (0, i, j)),
                          tile, accff, accff, acc8, acc8),
               scratch=[pltpu.VMEM((tm, D_MODEL), F32)],
               sem=("arbitrary", "arbitrary"))(h2, h2, h2, h2, h2, dr2, dr2, wd, wup_st, wup_st,
                                               convw, dpg, wpg, r1, g1)


ANY = pl.BlockSpec(memory_space=pl.ANY)


def _chip_peers():
    x, y, c = lax.axis_index("x"), lax.axis_index("y"), lax.axis_index("c")
    return x, y, c, [(1 - x, y), (x, 1 - y), (1 - x, 1 - y)]


def _gather_comm(halved, whole=(), blocks=None):
    n, nw = len(halved), len(whole)
    blocks = blocks or {}

    def at(ti, ref, chip, *rest):
        return ref.at[(chip, blocks[ti][1]) + rest] if ti in blocks else ref.at[(chip,) + rest]

    def copies(ins, outs, sems):
        ici_send, ici_recv, d2d_send, d2d_recv, own_send, own_recv = sems
        x, y, c, peers = _chip_peers()
        me = 2 * x + y
        sibling = (x, y, 1 - c)
        own, ici, ici_wait, fwd, fwd_wait = [], [], [], [], []
        for ti in range(n + nw):
            src, dst = ins[ti], outs[ti]
            own.append(pltpu.make_async_remote_copy(
                src_ref=src, dst_ref=at(ti, dst, me), send_sem=own_send.at[ti], recv_sem=own_recv.at[ti],
                device_id=sibling, device_id_type=MESH))
            for k, (px, py) in enumerate(peers):
                pk = 2 * px + py
                sem = dict(send_sem=ici_send.at[ti * 3 + k], recv_sem=ici_recv.at[ti * 3 + k],
                           device_id=(px, py, c), device_id_type=MESH)
                if ti < n:
                    ici.append(pltpu.make_async_remote_copy(src_ref=src.at[c], dst_ref=at(ti, dst, me, c), **sem))
                    ici_wait.append(pltpu.make_async_remote_copy(src_ref=src.at[c], dst_ref=at(ti, dst, pk, c),
                                                                 **sem))
                    dsem = dict(send_sem=d2d_send.at[ti * 3 + k], recv_sem=d2d_recv.at[ti * 3 + k],
                                device_id=sibling, device_id_type=MESH)
                    fwd.append(pltpu.make_async_remote_copy(src_ref=at(ti, dst, pk, c), dst_ref=at(ti, dst, pk, c),
                                                            **dsem))
                    fwd_wait.append(pltpu.make_async_remote_copy(
                        src_ref=at(ti, dst, pk, 1 - c), dst_ref=at(ti, dst, pk, 1 - c), **dsem))
                else:
                    ici.append(pltpu.make_async_remote_copy(src_ref=src, dst_ref=dst.at[me], **sem))
                    ici_wait.append(pltpu.make_async_remote_copy(src_ref=src, dst_ref=dst.at[pk], **sem))
        return own, ici, ici_wait, fwd, fwd_wait

    def start(ins, outs, sems):
        own, ici, _, _, _ = copies(ins, outs, sems)
        for cp in own + ici:
            cp.start()

    def finish(ins, outs, sems):
        own, ici, ici_wait, fwd, fwd_wait = copies(ins, outs, sems)
        for i, cp in enumerate(ici_wait):
            cp.wait_recv()
            if i < len(fwd):
                fwd[i].start()
        for cp in fwd_wait + own:
            cp.wait_recv()
        for cp in own + ici + fwd:
            cp.wait_send()

    def middle(ins, outs, sems):
        _, _, ici_wait, fwd, _ = copies(ins, outs, sems)
        for i, cp in enumerate(ici_wait):
            cp.wait_recv()
            if i < len(fwd):
                fwd[i].start()

    def finish_late(ins, outs, sems):
        own, ici, _, fwd, fwd_wait = copies(ins, outs, sems)
        for cp in fwd_wait + own:
            cp.wait_recv()
        for cp in own + ici + fwd:
            cp.wait_send()

    srcs = list(halved) + list(whole)
    shapes = [jax.ShapeDtypeStruct((N_CHIP,) + ((blocks[ti][0],) if ti in blocks else ()) + s.shape, s.dtype)
              for ti, s in enumerate(srcs)]
    buffers = [(ti, blk[2]) for ti, blk in sorted(blocks.items()) if blk[2] is not None]
    aliases = {len(srcs) + bi: ti for bi, (ti, _) in enumerate(buffers)}
    return _Comm(srcs + [buf for _, buf in buffers], shapes,
                 [pltpu.SemaphoreType.DMA((3 * (n + nw),)), pltpu.SemaphoreType.DMA((3 * (n + nw),)),
                  pltpu.SemaphoreType.DMA((max(3 * n, 1),)), pltpu.SemaphoreType.DMA((max(3 * n, 1),)),
                  pltpu.SemaphoreType.DMA((n + nw,)), pltpu.SemaphoreType.DMA((n + nw,))],
                 start, finish, middle, finish_late, aliases)


def _sibling_exchange_comm(grads):
    n = len(grads)

    def copies(ins, outs, sems):
        send_sems, recv_sems = sems
        x, y, c = lax.axis_index("x"), lax.axis_index("y"), lax.axis_index("c")
        res = []
        for ti in range(n):
            half = ins[ti].shape[1] // 2
            res.append(pltpu.make_async_remote_copy(
                src_ref=ins[ti].at[:, pl.ds(pl.multiple_of((1 - c) * half, 16), half), :],
                dst_ref=outs[ti],
                send_sem=send_sems.at[ti], recv_sem=recv_sems.at[ti],
                device_id=(x, y, 1 - c), device_id_type=MESH))
        return res

    def start(ins, outs, sems):
        for cp in copies(ins, outs, sems):
            cp.start()

    def finish(ins, outs, sems):
        for cp in copies(ins, outs, sems):
            cp.wait()

    return _Comm(grads, [jax.ShapeDtypeStruct((N_CHIP, g.shape[1] // 2, g.shape[2]), g.dtype) for g in grads],
                 [pltpu.SemaphoreType.DMA((n,)), pltpu.SemaphoreType.DMA((n,))], start, finish)


def _in_proj_gathering(x2b, own, chip, tm, comm):
    t = x2b.shape[0]
    ni = t // tm
    half, cols = own.shape[1], own.shape[2]
    nci, nco = len(comm.ins), len(comm.out_shapes)

    def body(chip_ref, x_ref, own_ref, own_hbm, *rest):
        c_in = rest[:nci]
        h_ref, win_out = rest[nci:nci + 2]
        c_out = rest[nci + 2:nci + 2 + nco]
        w_scr, ici_send, ici_recv, d2d_send, d2d_recv, own_sems, ld_sems = rest[nci + 2 + nco:nci + 9 + nco]
        c_sem = rest[nci + 9 + nco:]
        s, i = pl.program_id(0), pl.program_id(1)
        x, y, c, peers = _chip_peers()
        me = 2 * x + y
        sibling = (x, y, 1 - c)

        def ici(k, slot):
            px, py = peers[k]
            return pltpu.make_async_remote_copy(
                src_ref=own_hbm.at[c], dst_ref=win_out.at[slot, c],
                send_sem=ici_send.at[k], recv_sem=ici_recv.at[k],
                device_id=(px, py, c), device_id_type=MESH)

        def forward(k, core):
            pk = 2 * peers[k][0] + peers[k][1]
            return pltpu.make_async_remote_copy(
                src_ref=win_out.at[pk, core], dst_ref=win_out.at[pk, core],
                send_sem=d2d_send.at[k], recv_sem=d2d_recv.at[k],
                device_id=sibling, device_id_type=MESH)

        place_own = pltpu.make_async_remote_copy(
            src_ref=own_hbm, dst_ref=win_out.at[me], send_sem=own_sems.at[0], recv_sem=own_sems.at[1],
            device_id=sibling, device_id_type=MESH)

        @pl.when((s == 0) & (i == 0))
        def _():
            for k in range(2):
                ici(k, me).start()
            place_own.start()

        @pl.when(s == 0)
        def _():
            xv = x_ref[...]
            h_ref[...] = (_dot(xv[:, :half], own_ref[0]) + _dot(xv[:, half:], own_ref[1])).astype(BF16)

        for k in range(3):
            @pl.when((s == k + 1) & (i == 0))
            def _(k=k):
                pk = 2 * peers[k][0] + peers[k][1]
                ici(k, pk).wait_recv()
                if k == 0:
                    ici(2, me).start()
                forward(k, c).start()
                forward(k, 1 - c).wait_recv()
                loads = [pltpu.make_async_copy(win_out.at[pk, hh], w_scr.at[hh], ld_sems.at[hh])
                         for hh in range(2)]
                for ld in loads:
                    ld.start()
                for ld in loads:
                    ld.wait()
                if k == 1:
                    comm.start(c_in, c_out, c_sem)

        @pl.when(s > 0)
        def _():
            xv = x_ref[...]
            h_ref[...] = (_dot(xv[:, :half], w_scr[0]) + _dot(xv[:, half:], w_scr[1])).astype(BF16)

        @pl.when((s == N_CHIP - 1) & (i == ni - 1))
        def _():
            place_own.wait()
            for k in range(3):
                ici(k, me).wait_send()
                forward(k, c).wait_send()
            comm.finish(c_in, c_out, c_sem)

    def shard_col(s, me):
        return jnp.where(s == 0, me, me ^ jnp.where(s == 1, 2, jnp.where(s == 2, 1, 3)))

    res = _pc(body, name="in_proj",
              out_shape=(jax.ShapeDtypeStruct((t, N_CHIP * cols), BF16),
                         jax.ShapeDtypeStruct((N_CHIP,) + own.shape, own.dtype)) + tuple(comm.out_shapes),
              grid=(N_CHIP, ni), nsp=1,
              in_specs=[pl.BlockSpec((tm, 2 * half), lambda s, i, chip_ref: (i, 0)),
                        pl.BlockSpec(own.shape, lambda s, i, chip_ref: (0, 0, 0)),
                        ANY] + [ANY] * nci,
              out_specs=(pl.BlockSpec((tm, cols), lambda s, i, chip_ref: (i, shard_col(s, chip_ref[0]))),
                         ANY) + tuple([ANY] * nco),
              scratch=[pltpu.VMEM(own.shape, own.dtype),
                       pltpu.SemaphoreType.DMA((3,)), pltpu.SemaphoreType.DMA((3,)),
                       pltpu.SemaphoreType.DMA((3,)), pltpu.SemaphoreType.DMA((3,)),
                       pltpu.SemaphoreType.DMA((2,)), pltpu.SemaphoreType.DMA((2,))] + comm.sems,
              sem=("arbitrary", "arbitrary"))(chip, x2b, own, own, *comm.ins)
    return res[0], res[1], res[2:]


def _rs_add_halves(name, grad, recv, core):
    _, r, cdim = grad.shape
    half = r // 2
    tr = _row_tile(half, cdim, mult=16)
    nr = half // tr

    def body(c_ref, g_ref, r_ref, o_ref):
        o_ref[...] = (g_ref[...].astype(F32) + r_ref[...].astype(F32)).astype(BF16)

    return _pc(body, name=name, out_shape=jax.ShapeDtypeStruct((N_CHIP, half, cdim), BF16),
               grid=(N_CHIP, nr), nsp=1,
               in_specs=[pl.BlockSpec((None, tr, cdim), lambda j, i, c_ref: (j, c_ref[0] * nr + i, 0)),
                         pl.BlockSpec((None, tr, cdim), lambda j, i, c_ref: (j, i, 0))],
               out_specs=pl.BlockSpec((None, tr, cdim), lambda j, i, c_ref: (j, i, 0)),
               sem=("parallel", "parallel"))(core, grad, recv)


def _chip_exchange_comm(parts):
    n = len(parts)

    def copies(ins, outs, sems):
        send_sems, recv_sems = sems
        x, y, c, peers = _chip_peers()
        return [pltpu.make_async_remote_copy(
            src_ref=ins[ti].at[2 * px + py], dst_ref=outs[ti].at[k],
            send_sem=send_sems.at[ti * 3 + k], recv_sem=recv_sems.at[ti * 3 + k],
            device_id=(px, py, c), device_id_type=MESH)
            for ti in range(n) for k, (px, py) in enumerate(peers)]

    def start(ins, outs, sems):
        for cp in copies(ins, outs, sems):
            cp.start()

    def finish(ins, outs, sems):
        for cp in copies(ins, outs, sems):
            cp.wait()

    return _Comm(parts, [jax.ShapeDtypeStruct((3,) + p.shape[1:], p.dtype) for p in parts],
                 [pltpu.SemaphoreType.DMA((3 * n,)), pltpu.SemaphoreType.DMA((3 * n,))], start, finish)


def _rs_sum_chips(name, part, recv, chip):
    _, half, cdim = recv.shape
    tr = _row_tile(half, cdim, mult=16)

    def body(chip_ref, p_ref, r_ref, o_ref):
        o_ref[...] = ((p_ref[...].astype(F32) + r_ref[0].astype(F32)) + r_ref[1].astype(F32)
                      ) + r_ref[2].astype(F32)

    return _pc(body, name=name, out_shape=jax.ShapeDtypeStruct((half, cdim), F32),
               grid=(half // tr,), nsp=1,
               in_specs=[pl.BlockSpec((None, tr, cdim), lambda i, chip_ref: (chip_ref[0], i, 0)),
                         pl.BlockSpec((3, tr, cdim), lambda i, chip_ref: (0, i, 0))],
               out_specs=pl.BlockSpec((tr, cdim), lambda i, chip_ref: (i, 0)),
               sem=("parallel",))(chip, part, recv)


def _rs_send_halves(halves):
    n = len(halves)

    def body(*refs):
        ins, outs = refs[:n], refs[n:2 * n]
        send_sems, recv_sems = refs[2 * n:]
        x, y, c = lax.axis_index("x"), lax.axis_index("y"), lax.axis_index("c")
        sends = []
        for ti in range(n):
            cp = pltpu.make_async_remote_copy(
                src_ref=ins[ti], dst_ref=outs[ti],
                send_sem=send_sems.at[ti], recv_sem=recv_sems.at[ti],
                device_id=(x, y, 1 - c), device_id_type=MESH)
            cp.start()
            sends.append(cp)
        for cp in sends:
            cp.wait()

    return _pc(body, name="rs_send_halves",
               out_shape=tuple(jax.ShapeDtypeStruct(hv.shape, hv.dtype) for hv in halves),
               in_specs=[ANY] * n, out_specs=tuple([ANY] * n),
               scratch=[pltpu.SemaphoreType.DMA((n,)), pltpu.SemaphoreType.DMA((n,))])(*halves)


def _adamw_rows(name, mine, theirs, w, m, v, core):
    half, cdim = mine.shape
    tr = _row_tile(half, cdim, budget=1 << 19)
    nrh = half // tr

    def body(c_ref, mine_ref, theirs_ref, w_ref, m_ref, v_ref, g_ref, d_ref, m2_ref, v2_ref):
        is_mine = (pl.program_id(0) // nrh) == c_ref[0]
        g = jnp.where(is_mine, mine_ref[...], theirs_ref[...])
        d, m2, v2 = _adamw(w_ref[...], g, m_ref[...], v_ref[...])
        g_ref[...] = g
        d_ref[...] = d
        m2_ref[...] = m2
        v2_ref[...] = v2

    htile = pl.BlockSpec((tr, cdim), lambda i, c_ref: (i % nrh, 0))
    tile = pl.BlockSpec((tr, cdim), lambda i, c_ref: (i, 0))
    shp = jax.ShapeDtypeStruct((2 * half, cdim), F32)
    return _pc(body, name=name, out_shape=(shp, shp, shp, shp), grid=(2 * nrh,), nsp=1,
               in_specs=[htile, htile, tile, tile, tile], out_specs=(tile, tile, tile, tile),
               sem=("parallel",))(core, mine, theirs, w, m, v)


def _adamw_whole(name, g, w, m, v):
    def body(g_ref, w_ref, m_ref, v_ref, d_ref, m2_ref, v2_ref):
        d, m2, v2 = _adamw(w_ref[...], g_ref[...], m_ref[...], v_ref[...])
        d_ref[...] = d
        m2_ref[...] = m2
        v2_ref[...] = v2

    shp = jax.ShapeDtypeStruct(g.shape, F32)
    return _pc(body, name=name, out_shape=(shp, shp, shp))(g, w, m, v)


SMALL_LAYOUT = (
    ("sgu_w_s", 1024, 1, 0),
    ("sgu_b_s", 8, 1, 1024),
    ("sgu_norm_g", 1, 0, 0),
    ("sgu_norm_b", 1, 0, 1),
    ("hgrn_norm_g", 1, 0, 3),
    ("ln1_g", 1, 0, 4),
    ("ln1_b", 1, 0, 5),
    ("ffn_conv_b", 1, 2, 3),
    ("ln2_g", 1, 0, 6),
    ("ln2_b", 1, 0, 7),
)
LB_ROW = 2
LOSS_ROW = 8
PACK_SHAPES = ((16, D_MODEL), (N_GROUP * 128 + 16, 128), (8, D_FF))
GATH_DTYPES = (F32, BF16, F32)


def _small_allreduce_adamw(rows1024, dws, dbs, dcw, dcb, logits, m_logits, v_logits,
                           small_w, small_m, small_v):
    ns = len(SMALL_LAYOUT)
    nr = len(rows1024)
    nb = len(PACK_SHAPES)

    def body(*refs):
        row_refs = refs[:nr]
        dws_ref, dbs_ref, dcw_ref, dcb_ref, lg_ref, mlg_ref, vlg_ref = refs[nr:nr + 7]
        pos = nr + 7
        w_refs = refs[pos:pos + ns]
        m_refs = refs[pos + ns:pos + 2 * ns]
        v_refs = refs[pos + 2 * ns:pos + 3 * ns]
        pos += 3 * ns
        loss_ref, dcw_out = refs[pos:pos + 2]
        lg_outs = refs[pos + 2:pos + 6]
        pos += 6
        outs = refs[pos:pos + 4 * ns]
        pos += 4 * ns
        pack = refs[pos:pos + nb]
        sib = refs[pos + nb:pos + 2 * nb]
        gath = refs[pos + 2 * nb:pos + 3 * nb]
        d2d_send, d2d_recv, ici_send, ici_recv = refs[pos + 3 * nb:]

        x, y, c, peers = _chip_peers()
        me = 2 * x + y
        sibling = (x, y, 1 - c)

        pack[0][...] = jnp.zeros(PACK_SHAPES[0], F32)
        for k in range(nr):
            pack[0][k:k + 1, :] = row_refs[k][0:1, :]
        pack[1][0:N_GROUP * 128, :] = dws_ref[...]
        pack[1][N_GROUP * 128:N_GROUP * 128 + 8, :] = dbs_ref[...]
        pack[1][N_GROUP * 128 + 8:, :] = jnp.zeros((8, 128), F32)
        pack[2][...] = jnp.zeros(PACK_SHAPES[2], F32)
        pack[2][0:3, :] = dcw_ref[0:3, :]
        pack[2][3:4, :] = dcb_ref[0:1, :]

        d2d = [pltpu.make_async_remote_copy(
            src_ref=pack[b], dst_ref=sib[b], send_sem=d2d_send.at[b], recv_sem=d2d_recv.at[b],
            device_id=sibling, device_id_type=MESH) for b in range(nb)]
        for cp in d2d:
            cp.start()
        for cp in d2d:
            cp.wait()
        for b in range(nb):
            gath[b][me] = (pack[b][...] + sib[b][...]).astype(GATH_DTYPES[b])

        ici, ici_wait = [], []
        for b in range(nb):
            for k, (px, py) in enumerate(peers):
                sem = dict(send_sem=ici_send.at[b * 3 + k], recv_sem=ici_recv.at[b * 3 + k],
                           device_id=(px, py, c), device_id_type=MESH)
                ici.append(pltpu.make_async_remote_copy(src_ref=gath[b].at[me], dst_ref=gath[b].at[me], **sem))
                ici_wait.append(pltpu.make_async_remote_copy(
                    src_ref=gath[b].at[me], dst_ref=gath[b].at[2 * px + py], **sem))
        for cp in ici:
            cp.start()
        for cp in ici_wait:
            cp.wait_recv()
        for cp in ici:
            cp.wait_send()

        tot = pack
        for b in range(nb):
            tot[b][...] = ((gath[b][0].astype(F32) + gath[b][1].astype(F32)) + gath[b][2].astype(F32)
                           ) + gath[b][3].astype(F32)

        loss_ref[...] = tot[0][LOSS_ROW:LOSS_ROW + 1, :]
        dcw_out[...] = tot[2][...]
        lb = _sig(lg_ref[0:1, :] - lg_ref[1:2, :])
        d0 = tot[0][LB_ROW:LB_ROW + 1, :] * lb * (1.0 - lb)
        rowid = lax.broadcasted_iota(jnp.int32, (2, D_MODEL), 0)
        g_lg = jnp.where(rowid == 0, d0, -d0)
        dl, ml, vl = _adamw(lg_ref[...], g_lg, mlg_ref[...], vlg_ref[...])
        lg_outs[0][...] = g_lg
        lg_outs[1][...] = dl
        lg_outs[2][...] = ml
        lg_outs[3][...] = vl
        for si, (_, rows, b, r0) in enumerate(SMALL_LAYOUT):
            g = tot[b][r0:r0 + rows, :]
            dl, ml, vl = _adamw(w_refs[si][...], g, m_refs[si][...], v_refs[si][...])
            outs[4 * si][...] = g
            outs[4 * si + 1][...] = dl
            outs[4 * si + 2][...] = ml
            outs[4 * si + 3][...] = vl

    shapes = [jax.ShapeDtypeStruct((1, D_MODEL), F32), jax.ShapeDtypeStruct((8, D_FF), F32)]
    shapes += [jax.ShapeDtypeStruct((2, D_MODEL), F32)] * 4
    for w in small_w:
        shapes += [jax.ShapeDtypeStruct(w.shape, F32)] * 4
    scratch = [pltpu.VMEM(shp, F32) for shp in PACK_SHAPES]
    scratch += [pltpu.VMEM(shp, F32) for shp in PACK_SHAPES]
    scratch += [pltpu.VMEM((N_CHIP,) + shp, dt) for shp, dt in zip(PACK_SHAPES, GATH_DTYPES)]
    scratch += [pltpu.SemaphoreType.DMA((nb,)), pltpu.SemaphoreType.DMA((nb,)),
                pltpu.SemaphoreType.DMA((3 * nb,)), pltpu.SemaphoreType.DMA((3 * nb,))]
    vm = pl.BlockSpec(memory_space=pltpu.VMEM)
    n_in = nr + 7 + 3 * ns
    res = _pc(body, name="small_allreduce_adamw", out_shape=tuple(shapes),
              in_specs=[vm] * n_in, out_specs=tuple([vm] * len(shapes)),
              scratch=scratch)(*rows1024, dws, dbs, dcw, dcb, logits, m_logits, v_logits,
                               *small_w, *small_m, *small_v)
    return res[0], res[1], res[2:6], res[6:]


def kernel(x, p, w_in, sgu_w_s, sgu_b_s, sgu_norm_g, sgu_norm_b, hgrn_lb_logits, hgrn_norm_g, w_branch, w_out, ln1_g, ln1_b, ffn_w_up, ffn_conv_w, ffn_conv_b, ffn_w_down, ln2_g, ln2_b, ple_w_proj, ple_w_gate, loss_target, m_w_in, m_sgu_w_s, m_sgu_b_s, m_sgu_norm_g, m_sgu_norm_b, m_hgrn_lb_logits, m_hgrn_norm_g, m_w_branch, m_w_out, m_ln1_g, m_ln1_b, m_ffn_w_up, m_ffn_conv_w, m_ffn_conv_b, m_ffn_w_down, m_ln2_g, m_ln2_b, m_ple_w_proj, m_ple_w_gate, v_w_in, v_sgu_w_s, v_sgu_b_s, v_sgu_norm_g, v_sgu_norm_b, v_hgrn_lb_logits, v_hgrn_norm_g, v_w_branch, v_w_out, v_ln1_g, v_ln1_b, v_ffn_w_up, v_ffn_conv_w, v_ffn_conv_b, v_ffn_w_down, v_ln2_g, v_ln2_b, v_ple_w_proj, v_ple_w_gate):
    t = x.shape[1]
    x2 = x.reshape(t, D_MODEL)
    x2b = x2.astype(BF16)
    p2 = p.reshape(t, PLE_DIM)
    tgt = loss_target.reshape(t, D_MODEL)
    core = lax.axis_index("c").astype(jnp.int32).reshape(1)
    chip_id = (2 * lax.axis_index("x") + lax.axis_index("y")).astype(jnp.int32).reshape(1)

    big_w = [w_in[0], w_branch[0, 0], w_branch[0, 1], w_out[0], ffn_w_up[0], ffn_w_down[0],
             ple_w_proj[0], ple_w_gate[0]]
    big_m = [m_w_in[0], m_w_branch[0, 0], m_w_branch[0, 1], m_w_out[0], m_ffn_w_up[0],
             m_ffn_w_down[0], m_ple_w_proj[0], m_ple_w_gate[0]]
    big_v = [v_w_in[0], v_w_branch[0, 0], v_w_branch[0, 1], v_w_out[0], v_ffn_w_up[0],
             v_ffn_w_down[0], v_ple_w_proj[0], v_ple_w_gate[0]]
    def halves_of(i):
        w = big_w[i]
        return w.astype(BF16).reshape(2, w.shape[0] // 2, w.shape[1])

    def stacked(g, i):
        return g.reshape(N_CHIP, big_w[i].shape[0], big_w[i].shape[1])


    cid = jnp.arange(SGU_BLOCK) // CHUNK
    maskf = (cid[:, None] >= cid[None, :]).astype(F32)
    ws_masked = sgu_w_s[0] * maskf[None]
    wm = ws_masked.astype(BF16)
    wmt = jnp.transpose(ws_masked, (0, 2, 1)).astype(BF16)
    bsb = jnp.broadcast_to(sgu_b_s[0][:, :, None], (N_GROUP, SGU_BLOCK, 128))

    up_rows = big_w[4].shape[0] // 2
    up_blocks = [big_w[4][k * up_rows:(k + 1) * up_rows].astype(BF16).reshape(2, up_rows // 2, -1)
                 for k in range(2)]
    h, win_g, (up_g,) = _in_proj_gathering(x2b, halves_of(0), chip_id, 512,
                                           _gather_comm([up_blocks[0]], blocks={0: (2, 0, None)}))
    win_st = stacked(win_g, 0)
    ya, _ = _sgu_fwd(h, wm, bsb, sgu_norm_g, sgu_norm_b)
    (yb, o_all, st_all), mix_g = _hgrn_fwd(
        h, hgrn_lb_logits, hgrn_norm_g,
        comm=_gather_comm([halves_of(i) for i in (1, 2, 3)] + [up_blocks[1]], [ffn_conv_w[0]],
                          blocks={3: (2, 1, up_g)}))
    wb0, wb1, wo = [stacked(g, i).reshape(D_MODEL, D_MODEL) for g, i in zip(mix_g[:3], (1, 2, 3))]
    wup_st = stacked(mix_g[3], 4)
    convw = jnp.transpose(mix_g[4], (1, 0, 2)).reshape(3, D_FF)
    (r1, a_br, b_br, m_bf, x1b), _ = _mix_fwd(ya, yb, h, x2, wb0, wb1, wo, ln1_g, ln1_b, 256)
    h2, act, out_g = _ffn_up_act(x1b, wup_st, convw, ffn_conv_b, 512,
                                 _gather_comm([halves_of(i) for i in (5, 6, 7)]))
    wd = stacked(out_g[0], 5).reshape(D_FF, D_MODEL)
    wpp = jnp.transpose(stacked(out_g[1], 6), (1, 0, 2)).reshape(PLE_DIM, D_MODEL)
    wpg = stacked(out_g[2], 7).reshape(D_MODEL, D_MODEL)
    dr2, dpg, dpp, loss_acc, dg2, db2 = _out_fwd_bwd(
        act, x1b, r1, p2, tgt, wd, wpg, wpp, ln1_g, ln1_b, ln2_g, ln2_b, 256)

    dh2, dr1, dcw, dcb, dg1, db1 = _ffn_bwd(h2, dr2, wd, wup_st, dpg, wpg, r1, ln1_g, convw, 256)
    d_wd = _mm_tn("ffn_down_wgrad", act, dr2, FF_TILE, 512)
    d_wpg = _mm_tn("ple_gate_wgrad", x1b, dpg, 512, D_MODEL)
    d_wpp_st = _mm_tn("ple_proj_wgrad", p2, dpp, PLE_DIM, PLE_DIM, stacked=True)
    d_wup_st = _mm("ffn_up_wgrad", x1b, dh2, TN, (2, N_CHIP),
                   pl.BlockSpec((t, 512), lambda i, j: (0, i)),
                   pl.BlockSpec((None, t, FF_TILE), lambda i, j: (j // FF_NJ, 0, j % FF_NJ)),
                   jax.ShapeDtypeStruct((N_CHIP, D_MODEL, FF_TILE), BF16),
                   pl.BlockSpec((None, 512, FF_TILE), lambda i, j: (j, i, 0)))
    da_bf, db_bf, dh, dya, dyb = _mix_bwd(dr1, h, a_br, b_br, wo, wb0, wb1, 256)
    d_wo = _mm_tn("out_proj_wgrad", m_bf, dr1, 512, 512)
    d_wb0 = _mm_tn("branch0_wgrad", ya, da_bf, 512, D_MODEL)
    d_wb1 = _mm_tn("branch1_wgrad", yb, db_bf, 512, D_MODEL)
    grads_1 = [d_wb0.reshape(4, 256, D_MODEL), d_wb1.reshape(4, 256, D_MODEL),
               d_wo.reshape(4, 256, D_MODEL), d_wup_st, d_wd.reshape(4, D_FF // 4, D_MODEL),
               d_wpp_st, d_wpg.reshape(4, 256, D_MODEL)]
    (dh, dws, dbs, dgv, dbv), recv_a1 = _sgu_bwd(h, dya, wm, wmt, bsb, sgu_norm_g, sgu_norm_b, maskf, dh,
                                                 comm=_sibling_exchange_comm(grads_1))
    parts_1 = [_rs_add_halves("rs_add_halves%d" % (i + 1), g, r, core)
               for i, (g, r) in enumerate(zip(grads_1, recv_a1))]
    (dh, dlb, dgn), recv_b1 = _hgrn_bwd(h, o_all, dyb, st_all, hgrn_lb_logits, hgrn_norm_g, dh,
                                         comm=_chip_exchange_comm(parts_1))

    grads_0 = [_in_proj_wgrad(x2b, dh, 512, D_MODEL)]
    recv_a0 = _run_comm("rs_sibling_exchange0", _sibling_exchange_comm(grads_0))
    parts_0 = [_rs_add_halves("rs_add_halves0", grads_0[0], recv_a0[0], core)]
    gx, recv_b0 = _in_proj_xgrad(dh, win_st, dr1, 512, _chip_exchange_comm(parts_0))
    parts = parts_0 + parts_1
    recv_b = list(recv_b0) + list(recv_b1)
    halves = [_rs_sum_chips("rs_sum_chips%d" % i, pt, r, chip_id)
              for i, (pt, r) in enumerate(zip(parts, recv_b))]
    theirs = _rs_send_halves(halves)
    big_out = [_adamw_rows("adamw_big%d" % i, halves[i], theirs[i], big_w[i], big_m[i], big_v[i], core)
               for i in range(len(halves))]

    small_in = dict(sgu_w_s=(sgu_w_s, m_sgu_w_s, v_sgu_w_s), sgu_b_s=(sgu_b_s, m_sgu_b_s, v_sgu_b_s),
                    sgu_norm_g=(sgu_norm_g, m_sgu_norm_g, v_sgu_norm_g),
                    sgu_norm_b=(sgu_norm_b, m_sgu_norm_b, v_sgu_norm_b),
                    hgrn_norm_g=(hgrn_norm_g, m_hgrn_norm_g, v_hgrn_norm_g),
                    ln1_g=(ln1_g, m_ln1_g, v_ln1_g), ln1_b=(ln1_b, m_ln1_b, v_ln1_b),
                    ffn_conv_b=(ffn_conv_b, m_ffn_conv_b, v_ffn_conv_b),
                    ln2_g=(ln2_g, m_ln2_g, v_ln2_g), ln2_b=(ln2_b, m_ln2_b, v_ln2_b))

    def flat(name, arr):
        rows = dict((n, r) for n, r, _, _ in SMALL_LAYOUT)[name]
        return arr.reshape(rows, arr.size // rows)

    names = [n for n, _, _, _ in SMALL_LAYOUT]
    sw = [flat(n, small_in[n][0]) for n in names]
    sm = [flat(n, small_in[n][1]) for n in names]
    sv = [flat(n, small_in[n][2]) for n in names]
    loss_rows, dcw_tot, lg_out, small_out = _small_allreduce_adamw(
        [dgv, dbv, dlb, dgn, dg1, db1, dg2, db2, loss_acc], dws.reshape(N_GROUP * 128, 128), dbs, dcw, dcb,
        hgrn_lb_logits, m_hgrn_lb_logits, v_hgrn_lb_logits, sw, sm, sv)
    loss = loss_rows[0, 0]

    chip = 2 * lax.axis_index("x") + lax.axis_index("y")
    g_cw = lax.dynamic_slice(dcw_tot, (0, chip * (D_FF // 4)), (3, D_FF // 4))
    cw_out = _adamw_whole("adamw_conv_w", g_cw, ffn_conv_w[0], m_ffn_conv_w[0], v_ffn_conv_w[0])

    res = {}
    for si, n in enumerate(names):
        shp = small_in[n][0].shape
        res[n] = tuple(small_out[4 * si + k].reshape(shp) for k in range(4))
    res["hgrn_lb_logits"] = tuple(lg_out)
    res["ffn_conv_w"] = (g_cw[None],) + tuple(o[None] for o in cw_out)

    def big(i):
        return tuple(big_out[i])

    res["w_in"] = tuple(o[None] for o in big(0))
    res["w_branch"] = tuple(jnp.stack([o0, o1])[None] for o0, o1 in zip(big(1), big(2)))
    res["w_out"] = tuple(o[None] for o in big(3))
    res["ffn_w_up"] = tuple(o[None] for o in big(4))
    res["ffn_w_down"] = tuple(o[None] for o in big(5))
    res["ple_w_proj"] = tuple(o[None] for o in big(6))
    res["ple_w_gate"] = tuple(o[None] for o in big(7))

    order = ["w_in", "sgu_w_s", "sgu_b_s", "sgu_norm_g", "sgu_norm_b", "hgrn_lb_logits",
             "hgrn_norm_g", "w_branch", "w_out", "ln1_g", "ln1_b", "ffn_w_up", "ffn_conv_w",
             "ffn_conv_b", "ffn_w_down", "ln2_g", "ln2_b", "ple_w_proj", "ple_w_gate"]
    outs = [loss, gx.reshape(1, t, D_MODEL)]
    for k in range(4):
        outs += [res[n][k] for n in order]
    return tuple(outs)
```

```python
import jax
import jax.numpy as jnp
from jax import lax
from jax.experimental import pallas as pl
from jax.experimental.pallas import tpu as pltpu

F32 = jnp.float32
BF16 = jnp.bfloat16
HIGHEST = lax.Precision.HIGHEST
MESH = pl.DeviceIdType.MESH

D_MODEL = 1024
CHUNK = 64
SGU_BLOCK = 128
SGU_STEP_BLOCKS = 4
SGU_ROWS = SGU_STEP_BLOCKS * SGU_BLOCK
N_GROUP = 8
N_HEAD = 8
HEAD_DIM = 128
D_FF = 2816
PLE_DIM = 256
LN_EPS = 1e-5
RMS_EPS = 1e-6
ALPHA = 2.0 ** 0.25
N_CHIP = 4

ADAM_LR = 0.001
ADAM_B1 = 0.9
ADAM_B2 = 0.999
ADAM_EPS = 1e-08
ADAM_WD = 0.01
ADAM_STEP = 10

VMEM_LIMIT = 56 * 1024 * 1024

NN = (((1,), (0,)), ((), ()))
NT = (((1,), (1,)), ((), ()))
TN = (((0,), (0,)), ((), ()))


def _pc(body, *, name, out_shape, grid=None, in_specs=None, out_specs=None, scratch=(),
        sem=None, nsp=0, vmem=VMEM_LIMIT, aliases=None):
    params = dict(vmem_limit_bytes=vmem)
    if sem is not None:
        params["dimension_semantics"] = sem
    kw = dict(name=name, out_shape=out_shape, compiler_params=pltpu.CompilerParams(**params))
    if aliases:
        kw["input_output_aliases"] = aliases
    if nsp:
        kw["grid_spec"] = pltpu.PrefetchScalarGridSpec(
            num_scalar_prefetch=nsp, grid=grid, in_specs=in_specs, out_specs=out_specs,
            scratch_shapes=list(scratch))
    else:
        if grid is not None:
            kw["grid"] = grid
        if in_specs is not None:
            kw["in_specs"] = in_specs
            kw["out_specs"] = out_specs
        kw["scratch_shapes"] = list(scratch)
    return pl.pallas_call(body, **kw)


def _dot(a, b, dims=NN):
    return lax.dot_general(a.astype(BF16), b.astype(BF16), dims, preferred_element_type=F32)


def _dot32(a, b, dims=NN):
    return lax.dot_general(a, b, dims, precision=HIGHEST, preferred_element_type=F32)


def _sig(x):
    return 1.0 / (1.0 + jnp.exp(-x))


_GC = 0.7978845608028654
_GA = 0.044715


def _gelu(x):
    return 0.5 * x * (1.0 + jnp.tanh(_GC * (x + _GA * x * x * x)))


def _gelu_and_grad(x):
    t = jnp.tanh(_GC * (x + _GA * x * x * x))
    g = 0.5 * x * (1.0 + t)
    dg = 0.5 * (1.0 + t) + 0.5 * x * (1.0 - t * t) * _GC * (1.0 + 3.0 * _GA * x * x)
    return g, dg


def _ln_stats(r):
    mu = jnp.mean(r, axis=-1, keepdims=True)
    xc = r - mu
    var = jnp.mean(xc * xc, axis=-1, keepdims=True)
    rstd = lax.rsqrt(var + LN_EPS)
    return xc * rstd, rstd


def _ln_bwd(dxh, xh, rstd):
    m1 = jnp.mean(dxh, axis=-1, keepdims=True)
    m2 = jnp.mean(dxh * xh, axis=-1, keepdims=True)
    return rstd * (dxh - m1 - xh * m2)


def _colsum8(v):
    return jnp.broadcast_to(jnp.sum(v, axis=0, keepdims=True), (8, v.shape[1]))


def _adamw(w, g, m, v):
    m2 = ADAM_B1 * m + (1.0 - ADAM_B1) * g
    v2 = ADAM_B2 * v + (1.0 - ADAM_B2) * (g * g)
    m_hat = m2 / (1.0 - ADAM_B1 ** ADAM_STEP)
    v_hat = v2 / (1.0 - ADAM_B2 ** ADAM_STEP)
    delta = -ADAM_LR * (m_hat / (jnp.sqrt(v_hat) + ADAM_EPS) + ADAM_WD * w)
    return delta, m2, v2


def _row_tile(rows, cols, itemsize=4, budget=1 << 20, mult=8):
    best = mult
    for tr in range(mult, rows + 1, mult):
        if rows % tr == 0 and tr * cols * itemsize <= budget:
            best = tr
    return best


def _mm(name, a, b, dims, grid, a_spec, b_spec, out_shape, o_spec):
    out_dtype = out_shape.dtype

    def body(a_ref, b_ref, o_ref):
        o_ref[...] = _dot(a_ref[...], b_ref[...], dims).astype(out_dtype)

    return _pc(body, name=name, out_shape=out_shape, grid=grid, in_specs=[a_spec, b_spec],
               out_specs=o_spec, sem=("parallel", "parallel"))(a, b)


class _Comm:
    def __init__(self, ins, out_shapes, sems, start, finish, middle=None, finish_late=None, aliases=None):
        self.ins, self.out_shapes, self.sems = list(ins), list(out_shapes), list(sems)
        self.start, self.finish = start, finish
        self.middle, self.finish_late = middle, finish_late
        self.aliases = aliases or {}


def _hosted_call(body, comm, first, last, *, name, out_shape, grid, in_specs, out_specs, scratch, sem,
                 args, aliases=None, mid=None):
    n_in, n_out, n_scr = len(in_specs), len(out_shape), len(scratch)
    nci, nco = len(comm.ins), len(comm.out_shapes)

    def wrapped(*refs):
        pos = n_in
        own_in, c_in = refs[:pos], refs[pos:pos + nci]
        pos += nci
        own_out, c_out = refs[pos:pos + n_out], refs[pos + n_out:pos + n_out + nco]
        pos += n_out + nco
        own_scr, c_sem = refs[pos:pos + n_scr], refs[pos + n_scr:]

        @pl.when(first())
        def _():
            comm.start(c_in, c_out, c_sem)

        body(*own_in, *own_out, *own_scr)

        if mid is not None:
            @pl.when(mid())
            def _():
                comm.middle(c_in, c_out, c_sem)

        @pl.when(last())
        def _():
            (comm.finish if mid is None else comm.finish_late)(c_in, c_out, c_sem)

    return _pc(wrapped, name=name, out_shape=tuple(out_shape) + tuple(comm.out_shapes), grid=grid,
               in_specs=list(in_specs) + [ANY] * nci, out_specs=tuple(out_specs) + tuple([ANY] * nco),
               scratch=list(scratch) + comm.sems, sem=sem,
               aliases={**(aliases or {}), **{n_in + ci: n_out + co for ci, co in comm.aliases.items()}},
               )(*args, *comm.ins)


def _grid1_call(body, comm, n, *, name, out_shape, in_specs, out_specs, scratch, args, aliases=None):
    if comm is None:
        return _pc(body, name=name, out_shape=out_shape, grid=(n,), in_specs=in_specs, out_specs=out_specs,
                   scratch=scratch, sem=("arbitrary",), aliases=aliases)(*args), ()
    res = _hosted_call(body, comm, lambda: pl.program_id(0) == 0, lambda: pl.program_id(0) == n - 1,
                       name=name, out_shape=out_shape, grid=(n,), in_specs=in_specs,
                       out_specs=out_specs, scratch=scratch, sem=("arbitrary",), args=args, aliases=aliases)
    return res[:len(out_shape)], res[len(out_shape):]


def _run_comm(name, comm):
    nci, nco = len(comm.ins), len(comm.out_shapes)

    def body(*refs):
        c_in, c_out, c_sem = refs[:nci], refs[nci:nci + nco], refs[nci + nco:]
        comm.start(c_in, c_out, c_sem)
        comm.finish(c_in, c_out, c_sem)

    return _pc(body, name=name, out_shape=tuple(comm.out_shapes), in_specs=[ANY] * nci,
               out_specs=tuple([ANY] * nco), scratch=comm.sems)(*comm.ins)


def _mm_tn(name, a, b, tm, tn, stacked=False):
    t, m = a.shape
    _, n = b.shape
    if stacked:
        assert tm == m
        out_shape = jax.ShapeDtypeStruct((n // tn, m, tn), BF16)
        o_spec = pl.BlockSpec((None, tm, tn), lambda i, j: (j, 0, 0))
    else:
        out_shape = jax.ShapeDtypeStruct((m, n), BF16)
        o_spec = pl.BlockSpec((tm, tn), lambda i, j: (i, j))
    return _mm(name, a, b, TN, (m // tm, n // tn),
               pl.BlockSpec((t, tm), lambda i, j: (0, i)),
               pl.BlockSpec((t, tn), lambda i, j: (0, j)),
               out_shape, o_spec)


DH_SLOT = (2, 0, 1, 3)


def _dh_slot(j):
    return jnp.where(j == 3, 3, (j + 2) % 3)


def _in_proj_wgrad(x2b, dh, tm, tn, comm):
    t, m = x2b.shape
    n = dh.shape[2]
    grid = (N_CHIP, m // tm, n // tn)

    def body(a_ref, b_ref, o_ref):
        o_ref[...] = _dot(a_ref[...], b_ref[...], TN).astype(BF16)

    def at(corner):
        return lambda: ((pl.program_id(0) == corner[0]) & (pl.program_id(1) == corner[1])
                        & (pl.program_id(2) == corner[2]))

    res = _hosted_call(body, comm, at((0, 0, 0)), at(tuple(g - 1 for g in grid)),
                       name="in_proj_wgrad", out_shape=(jax.ShapeDtypeStruct((N_CHIP, m, n), BF16),),
                       grid=grid,
                       in_specs=[pl.BlockSpec((t, tm), lambda j, i, k: (0, i)),
                                 pl.BlockSpec((None, t, tn), lambda j, i, k: (_dh_slot(j), 0, k))],
                       out_specs=(pl.BlockSpec((None, tm, tn), lambda j, i, k: (j, i, k)),),
                       scratch=[], sem=("arbitrary", "arbitrary", "arbitrary"), args=(x2b, dh))
    return res[0], res[1:]


def _in_proj_xgrad(dh, win_st, dr1, tm, comm):
    t = dr1.shape[0]
    ni = t // tm

    def body(a_ref, b_ref, add_ref, o_ref, acc):
        j = pl.program_id(1)
        prod = _dot(a_ref[...], b_ref[...], NT)

        @pl.when(j == 0)
        def _():
            acc[...] = prod + ALPHA * add_ref[...]

        @pl.when((j > 0) & (j < N_CHIP - 1))
        def _():
            acc[...] += prod

        @pl.when(j == N_CHIP - 1)
        def _():
            o_ref[...] = acc[...] + prod

    tile = pl.BlockSpec((tm, D_MODEL), lambda i, j: (i, 0))
    res = _hosted_call(body, comm,
                       lambda: (pl.program_id(0) == 0) & (pl.program_id(1) == 0),
                       lambda: (pl.program_id(0) == ni - 1) & (pl.program_id(1) == N_CHIP - 1),
                       name="in_proj_xgrad", out_shape=(jax.ShapeDtypeStruct((t, D_MODEL), F32),),
                       grid=(ni, N_CHIP),
                       in_specs=[pl.BlockSpec((None, tm, 2 * D_MODEL), lambda i, j: (_dh_slot(j), i, 0)),
                                 pl.BlockSpec((None, D_MODEL, 2 * D_MODEL), lambda i, j: (j, 0, 0)),
                                 tile],
                       out_specs=(tile,), scratch=[pltpu.VMEM((tm, D_MODEL), F32)],
                       sem=("arbitrary", "arbitrary"), args=[dh, win_st, dr1])
    return res[0], res[1:]


def _sgu_mixed(v, wm_ref, bsb_ref, gv, bv):
    gl, dgl = _gelu_and_grad(v)
    vh, rstd = _ln_stats(gl)
    vn = vh * gv + bv
    mixed = []
    for g in range(N_GROUP):
        sl = slice(g * 128, (g + 1) * 128)
        mixed.append(_dot(wm_ref[g], vn[:, sl]) + bsb_ref[g])
    return dgl, vh, rstd, vn, mixed


def _sgu_fwd(h, wm, bsb, gv, bv, comm=None):
    t = h.shape[0]

    def body(u_ref, v_ref, wm_ref, bsb_ref, gv_ref, bv_ref, ya_ref):
        for bb in range(SGU_STEP_BLOCKS):
            rows = slice(bb * SGU_BLOCK, (bb + 1) * SGU_BLOCK)
            u = u_ref[rows, :].astype(F32)
            _, _, _, _, mixed = _sgu_mixed(v_ref[rows, :].astype(F32), wm_ref, bsb_ref, gv_ref[...],
                                           bv_ref[...])
            gu = _gelu(u)
            for g in range(N_GROUP):
                sl = slice(g * 128, (g + 1) * 128)
                ya_ref[rows, sl] = (gu[:, sl] * mixed[g]).astype(BF16)

    full3 = pl.BlockSpec((N_GROUP, 128, 128), lambda i: (0, 0, 0))
    vec = pl.BlockSpec((1, D_MODEL), lambda i: (0, 0))
    (ya,), extra = _grid1_call(
        body, comm, t // SGU_ROWS, name="sgu_fwd",
        out_shape=(jax.ShapeDtypeStruct((t, D_MODEL), BF16),),
        in_specs=[pl.BlockSpec((SGU_ROWS, D_MODEL), lambda i: (i, 0)),
                  pl.BlockSpec((SGU_ROWS, D_MODEL), lambda i: (i, 1)),
                  full3, full3, vec, vec],
        out_specs=(pl.BlockSpec((SGU_ROWS, D_MODEL), lambda i: (i, 0)),),
        scratch=[], args=(h, h, wm, bsb, gv, bv))
    return ya, extra


def _sgu_bwd(h, dya, wm, wmt, bsb, gv, bv, maskf, dh_buf, comm=None):
    t = h.shape[0]
    nb = t // SGU_ROWS

    def body(u_ref, v_ref, dya_ref, wm_ref, wmt_ref, bsb_ref, gv_ref, bv_ref, mask_ref, dh_buf_ref,
             dh_ref, dws_ref, dbs_ref, dgv_ref, dbv_ref, dmix_acc):
        i = pl.program_id(0)

        @pl.when(i == 0)
        def _():
            dws_ref[...] = jnp.zeros_like(dws_ref)
            dgv_ref[...] = jnp.zeros_like(dgv_ref)
            dbv_ref[...] = jnp.zeros_like(dbv_ref)
            dmix_acc[...] = jnp.zeros_like(dmix_acc)

        gvv = gv_ref[...]
        for bb in range(SGU_STEP_BLOCKS):
            rows = slice(bb * SGU_BLOCK, (bb + 1) * SGU_BLOCK)
            u = u_ref[rows, :].astype(F32)
            dgl_v, vh, rstd, vn, mixed = _sgu_mixed(v_ref[rows, :].astype(F32), wm_ref, bsb_ref, gvv,
                                                    bv_ref[...])
            gu, dgl_u = _gelu_and_grad(u)
            dya_v = dya_ref[rows, :].astype(F32)
            dvn_parts = []
            for g in range(N_GROUP):
                sl = slice(g * 128, (g + 1) * 128)
                d_y = dya_v[:, sl]
                dh_ref[rows, sl] = (d_y * mixed[g] * dgl_u[:, sl]).astype(BF16)
                d_mixed = d_y * gu[:, sl]
                dmix_acc[g] += d_mixed
                dws_ref[g] += _dot(d_mixed, vn[:, sl], NT) * mask_ref[...]
                dvn_parts.append(_dot(wmt_ref[g], d_mixed))
            dvn = jnp.concatenate(dvn_parts, axis=1)
            dgv_ref[...] += _colsum8(dvn * vh)
            dbv_ref[...] += _colsum8(dvn)
            d_gl = _ln_bwd(dvn * gvv, vh, rstd)
            dh_ref[rows, D_MODEL:] = (d_gl * dgl_v).astype(BF16)

        @pl.when(i == nb - 1)
        def _():
            rowid = lax.broadcasted_iota(jnp.int32, (8, 128), 0)
            ones = jnp.ones((8, 128), F32)
            acc = jnp.zeros((8, 128), F32)
            for g in range(N_GROUP):
                rs = _dot32(ones, dmix_acc[g], NT)
                acc = jnp.where(rowid == g, rs, acc)
            dbs_ref[...] = acc

    full3 = pl.BlockSpec((N_GROUP, 128, 128), lambda i: (0, 0, 0))
    vec = pl.BlockSpec((1, D_MODEL), lambda i: (0, 0))
    acc8 = pl.BlockSpec((8, D_MODEL), lambda i: (0, 0))
    return _grid1_call(
        body, comm, nb, name="sgu_bwd",
        out_shape=(jax.ShapeDtypeStruct(dh_buf.shape, BF16),
                   jax.ShapeDtypeStruct((N_GROUP, 128, 128), F32),
                   jax.ShapeDtypeStruct((8, 128), F32),
                   jax.ShapeDtypeStruct((8, D_MODEL), F32),
                   jax.ShapeDtypeStruct((8, D_MODEL), F32)),
        in_specs=[pl.BlockSpec((SGU_ROWS, D_MODEL), lambda i: (i, 0)),
                  pl.BlockSpec((SGU_ROWS, D_MODEL), lambda i: (i, 1)),
                  pl.BlockSpec((SGU_ROWS, D_MODEL), lambda i: (i, 0)),
                  full3, full3, full3, vec, vec,
                  pl.BlockSpec((128, 128), lambda i: (0, 0)), ANY],
        out_specs=(pl.BlockSpec((None, SGU_ROWS, 2 * D_MODEL), lambda i: (DH_SLOT[0], i, 0)),
                   full3, pl.BlockSpec((8, 128), lambda i: (0, 0)), acc8, acc8),
        scratch=[pltpu.VMEM((N_GROUP, 128, 128), F32)],
        args=(h, h, dya, wm, wmt, bsb, gv, bv, maskf, dh_buf), aliases={9: 0})


def _tri_masks():
    row = lax.broadcasted_iota(jnp.int32, (CHUNK, CHUNK), 0)
    col = lax.broadcasted_iota(jnp.int32, (CHUNK, CHUNK), 1)
    return col <= row, col >= row


def _heads(v):
    return [v[:, hd * HEAD_DIM:(hd + 1) * HEAD_DIM] for hd in range(N_HEAD)]


def _tri_cumsum(tri_bf, v):
    hi = v.astype(BF16)
    r = v - hi.astype(F32)
    mid = r.astype(BF16)
    lo = (r - mid.astype(F32)).astype(BF16)
    return _dot(tri_bf, hi) + _dot(tri_bf, mid) + _dot(tri_bf, lo)


def _hgrn_chunk(q, fp, ii, lb, st_heads, causal, with_o=True):
    sg = _sig(fp)
    f = lb + (1.0 - lb) * sg
    k = 1.0 - f
    c = _tri_cumsum(causal.astype(BF16), jnp.log(f))
    ec = jnp.exp(c)
    en = jnp.exp(-c)
    sq = _sig(q)
    qt = q * sq * ec
    kt = k * en
    ecl = jnp.exp(c[CHUNK - 1:CHUNK, :])
    kk = kt * ecl
    qtb, ktb, iib, kkb = qt.astype(BF16), kt.astype(BF16), ii.astype(BF16), kk.astype(BF16)
    attn, o = [], []
    for hd, (qh, kh, ih) in enumerate(zip(_heads(qtb), _heads(ktb), _heads(iib))):
        a = jnp.where(causal, _dot(qh, kh, NT), 0.0).astype(BF16)
        attn.append(a)
        if with_o:
            o.append(_dot(a, ih) + _dot(qh, st_heads[hd], NT))
    return dict(sg=sg, f=f, k=k, ec=ec, en=en, sq=sq, ecl=ecl, kk=kk, qtb=qtb, ktb=ktb, iib=iib,
                kkb=kkb, attn=attn, o=o)


def _rms_heads(o_heads):
    rinv = [lax.rsqrt(jnp.mean(o * o, axis=-1, keepdims=True) + RMS_EPS) for o in o_heads]
    return rinv, jnp.concatenate([o * r for o, r in zip(o_heads, rinv)], axis=1)


HG_CHUNKS = 8
HG_ROWS = HG_CHUNKS * CHUNK


def _hgrn_fwd(h, logits, gn, comm=None):
    t = h.shape[0]
    nb = t // HG_ROWS

    def body(q_ref, f_ref, i_ref, og_ref, lg_ref, gn_ref, yb_ref, o_ref, st_ref, state):
        @pl.when(pl.program_id(0) == 0)
        def _():
            state[...] = jnp.zeros_like(state)

        causal, _ = _tri_masks()
        lb = _sig(lg_ref[0:1, :] - lg_ref[1:2, :])
        gnv = gn_ref[...]
        st = [state[hd] for hd in range(N_HEAD)]
        for cc in range(HG_CHUNKS):
            rows = slice(cc * CHUNK, (cc + 1) * CHUNK)
            og = og_ref[rows, :].astype(F32)
            r = _hgrn_chunk(q_ref[rows, :].astype(F32), f_ref[rows, :].astype(F32),
                            i_ref[rows, :].astype(F32), lb, [s.astype(BF16) for s in st], causal)
            o_bf = jnp.concatenate(r["o"], axis=1).astype(BF16)
            o_ref[rows, :] = o_bf
            _, on = _rms_heads(_heads(o_bf.astype(F32)))
            yb_ref[rows, :] = (on * gnv * (og * _sig(og))).astype(BF16)
            for hd in range(N_HEAD):
                st_ref[cc, hd] = st[hd]
            st = [s * e + _dot(ih, kh, TN)
                  for s, e, ih, kh in zip(st, _heads(r["ecl"]), _heads(r["iib"]), _heads(r["kkb"]))]
        for hd in range(N_HEAD):
            state[hd] = st[hd]

    def col(k):
        return pl.BlockSpec((HG_ROWS, D_MODEL), lambda ci: (ci, k))

    return _grid1_call(body, comm, nb, name="hgrn_fwd",
                       out_shape=(jax.ShapeDtypeStruct((t, D_MODEL), BF16),
                                  jax.ShapeDtypeStruct((t, D_MODEL), BF16),
                                  jax.ShapeDtypeStruct((t // CHUNK, N_HEAD, HEAD_DIM, HEAD_DIM), F32)),
                       in_specs=[col(2), col(3), col(4), col(5),
                                 pl.BlockSpec((2, D_MODEL), lambda ci: (0, 0)),
                                 pl.BlockSpec((1, D_MODEL), lambda ci: (0, 0))],
                       out_specs=(pl.BlockSpec((HG_ROWS, D_MODEL), lambda ci: (ci, 0)),
                                  pl.BlockSpec((HG_ROWS, D_MODEL), lambda ci: (ci, 0)),
                                  pl.BlockSpec((HG_CHUNKS, N_HEAD, HEAD_DIM, HEAD_DIM),
                                               lambda ci: (ci, 0, 0, 0))),
                       scratch=[pltpu.VMEM((N_HEAD, HEAD_DIM, HEAD_DIM), F32)],
                       args=(h, h, h, h, logits, gn))


def _hgrn_chunk_bwd(q, fp, ii, og, o_saved, dy, gnv, lb, st, dsn, causal, anti):
    stb = [s.astype(BF16) for s in st]
    dsnb = [s.astype(BF16) for s in dsn]
    r = _hgrn_chunk(q, fp, ii, lb, stb, causal, with_o=False)
    rinv, on = _rms_heads(_heads(o_saved))
    so = _sig(og)
    sil = og * so
    d_og = dy * on * gnv * (so * (1.0 + og * (1.0 - so)))
    d_on = dy * gnv * sil
    d_ob = jnp.concatenate(
        [ri * (dn - oh * jnp.mean(dn * oh, axis=-1, keepdims=True))
         for ri, dn, oh in zip(rinv, _heads(d_on), _heads(on))], axis=1).astype(BF16)
    d_i, d_qt, d_kt, d_kk, d_st, st_dsn = [], [], [], [], [], []
    ecl = _heads(r["ecl"])
    for hd, (dh, qh, kh, ih, kkh) in enumerate(zip(_heads(d_ob), _heads(r["qtb"]), _heads(r["ktb"]),
                                                   _heads(r["iib"]), _heads(r["kkb"]))):
        d_attn = jnp.where(causal, _dot(dh, ih, NT), 0.0).astype(BF16)
        d_i.append(_dot(r["attn"][hd], dh, TN) + _dot(kkh, dsnb[hd], NT))
        d_qt.append(_dot(d_attn, kh) + _dot(dh, stb[hd]))
        d_kt.append(_dot(d_attn, qh, TN))
        d_kk.append(_dot(ih, dsnb[hd]))
        d_st.append(_dot(dh, qh, TN) + dsn[hd] * ecl[hd])
        st_dsn.append(jnp.sum(st[hd] * dsn[hd], axis=0, keepdims=True))
    d_qt = jnp.concatenate(d_qt, axis=1)
    d_kt = jnp.concatenate(d_kt, axis=1)
    d_kk = jnp.concatenate(d_kk, axis=1)
    kk = r["kk"]
    d_cl = r["ecl"] * jnp.concatenate(st_dsn, axis=1) + jnp.sum(kk * d_kk, axis=0, keepdims=True)
    d_k = (d_kk * r["ecl"] + d_kt) * r["en"]
    d_c = d_qt * r["qtb"].astype(F32) - d_kt * r["ktb"].astype(F32) - d_kk * kk
    rowid = lax.broadcasted_iota(jnp.int32, (CHUNK, D_MODEL), 0)
    d_c = d_c + jnp.where(rowid == CHUNK - 1, d_cl, 0.0)
    d_lf = _tri_cumsum(anti.astype(BF16), d_c)
    d_f = d_lf / r["f"] - d_k
    sg, sq = r["sg"], r["sq"]
    d_q = d_qt * r["ec"] * (sq * (1.0 + q * (1.0 - sq)))
    d_fp = d_f * (1.0 - lb) * sg * (1.0 - sg)
    return (d_q, d_fp, jnp.concatenate(d_i, axis=1), d_og, d_st,
            _colsum8(dy * on * sil), _colsum8(d_f * (1.0 - sg)))


def _hgrn_bwd(h, o_all, dyb, st_all, logits, gn, dh_buf, comm=None):
    t = h.shape[0]
    nb = t // HG_ROWS

    def body(q_ref, f_ref, i_ref, og_ref, o_ref, dyb_ref, st_ref, lg_ref, gn_ref, dh_buf_ref,
             dh_ref, dlb_ref, dgn_ref, dstate):
        @pl.when(pl.program_id(0) == 0)
        def _():
            dstate[...] = jnp.zeros_like(dstate)
            dlb_ref[...] = jnp.zeros_like(dlb_ref)
            dgn_ref[...] = jnp.zeros_like(dgn_ref)

        causal, anti = _tri_masks()
        lb = _sig(lg_ref[0:1, :] - lg_ref[1:2, :])
        gnv = gn_ref[...]
        dsn = [dstate[hd] for hd in range(N_HEAD)]
        dgn_acc = jnp.zeros((8, D_MODEL), F32)
        dlb_acc = jnp.zeros((8, D_MODEL), F32)
        for cc in reversed(range(HG_CHUNKS)):
            rows = slice(cc * CHUNK, (cc + 1) * CHUNK)
            d_q, d_fp, d_i, d_og, dsn, dgn_c, dlb_c = _hgrn_chunk_bwd(
                q_ref[rows, :].astype(F32), f_ref[rows, :].astype(F32), i_ref[rows, :].astype(F32),
                og_ref[rows, :].astype(F32), o_ref[rows, :].astype(F32), dyb_ref[rows, :].astype(F32), gnv, lb,
                [st_ref[cc, hd] for hd in range(N_HEAD)], dsn, causal, anti)
            dgn_acc = dgn_acc + dgn_c
            dlb_acc = dlb_acc + dlb_c
            dh_ref[0, rows, :D_MODEL] = d_q.astype(BF16)
            dh_ref[0, rows, D_MODEL:] = d_fp.astype(BF16)
            dh_ref[1, rows, :D_MODEL] = d_i.astype(BF16)
            dh_ref[1, rows, D_MODEL:] = d_og.astype(BF16)
        dgn_ref[...] += dgn_acc
        dlb_ref[...] += dlb_acc
        for hd in range(N_HEAD):
            dstate[hd] = dsn[hd]

    def col(k):
        return pl.BlockSpec((HG_ROWS, D_MODEL), lambda ci: (nb - 1 - ci, k))

    acc8 = pl.BlockSpec((8, D_MODEL), lambda ci: (0, 0))
    pair = pl.BlockSpec((2, HG_ROWS, 2 * D_MODEL), lambda ci: (0, nb - 1 - ci, 0))
    return _grid1_call(body, comm, nb, name="hgrn_bwd",
                       out_shape=(jax.ShapeDtypeStruct(dh_buf.shape, BF16),
                                  jax.ShapeDtypeStruct((8, D_MODEL), F32),
                                  jax.ShapeDtypeStruct((8, D_MODEL), F32)),
                       in_specs=[col(2), col(3), col(4), col(5), col(0),
                                 pl.BlockSpec((HG_ROWS, D_MODEL), lambda ci: (nb - 1 - ci, 0)),
                                 pl.BlockSpec((HG_CHUNKS, N_HEAD, HEAD_DIM, HEAD_DIM),
                                              lambda ci: (nb - 1 - ci, 0, 0, 0)),
                                 pl.BlockSpec((2, D_MODEL), lambda ci: (0, 0)),
                                 pl.BlockSpec((1, D_MODEL), lambda ci: (0, 0)), ANY],
                       out_specs=(pair, acc8, acc8),
                       scratch=[pltpu.VMEM((N_HEAD, HEAD_DIM, HEAD_DIM), F32)],
                       args=(h, h, h, h, o_all, dyb, st_all, logits, gn, dh_buf), aliases={9: 0})


def _mix_fwd(ya, yb, h, x, wb0, wb1, wo, g1, b1, tm, comm=None):
    t = x.shape[0]

    def body(ya_ref, yb_ref, ga_ref, gb_ref, x_ref, wb0_ref, wb1_ref, wo_ref, g1_ref, b1_ref,
             r1_ref, a_ref, b_ref, m_ref, x1_ref):
        a = _dot(ya_ref[...], wb0_ref[...])
        b = _dot(yb_ref[...], wb1_ref[...])
        m = _sig(ga_ref[...].astype(F32)) * a + _sig(gb_ref[...].astype(F32)) * b
        r1 = ALPHA * x_ref[...] + _dot(m, wo_ref[...])
        xh, _ = _ln_stats(r1)
        r1_ref[...] = r1
        a_ref[...] = a.astype(BF16)
        b_ref[...] = b.astype(BF16)
        m_ref[...] = m.astype(BF16)
        x1_ref[...] = (xh * g1_ref[...] + b1_ref[...]).astype(BF16)

    tile = pl.BlockSpec((tm, D_MODEL), lambda i: (i, 0))
    wsp = pl.BlockSpec((D_MODEL, D_MODEL), lambda i: (0, 0))
    vec = pl.BlockSpec((1, D_MODEL), lambda i: (0, 0))
    f32o = jax.ShapeDtypeStruct((t, D_MODEL), F32)
    bfo = jax.ShapeDtypeStruct((t, D_MODEL), BF16)
    return _grid1_call(body, comm, t // tm, name="mix_fwd", out_shape=(f32o, bfo, bfo, bfo, bfo),
                       in_specs=[tile, tile,
                                 pl.BlockSpec((tm, D_MODEL), lambda i: (i, 6)),
                                 pl.BlockSpec((tm, D_MODEL), lambda i: (i, 7)),
                                 tile, wsp, wsp, wsp, vec, vec],
                       out_specs=(tile, tile, tile, tile, tile),
                       scratch=[], args=(ya, yb, h, h, x, wb0, wb1, wo, g1, b1))


def _mix_bwd(dr1, h, a, b, wo, wb0, wb1, tm):
    t = dr1.shape[0]

    def body(dr1_ref, ga_ref, gb_ref, a_ref, b_ref, wo_ref, wb0_ref, wb1_ref,
             da_ref, db_ref, dh3_ref, dya_ref, dyb_ref):
        d_m = _dot(dr1_ref[...], wo_ref[...], NT)
        sa = _sig(ga_ref[...].astype(F32))
        sb = _sig(gb_ref[...].astype(F32))
        d_a = (d_m * sa).astype(BF16)
        d_b = (d_m * sb).astype(BF16)
        da_ref[...] = d_a
        db_ref[...] = d_b
        dh3_ref[:, :D_MODEL] = (d_m * a_ref[...].astype(F32) * sa * (1.0 - sa)).astype(BF16)
        dh3_ref[:, D_MODEL:] = (d_m * b_ref[...].astype(F32) * sb * (1.0 - sb)).astype(BF16)
        dya_ref[...] = _dot(d_a, wb0_ref[...], NT).astype(BF16)
        dyb_ref[...] = _dot(d_b, wb1_ref[...], NT).astype(BF16)

    tile = pl.BlockSpec((tm, D_MODEL), lambda i: (i, 0))
    wsp = pl.BlockSpec((D_MODEL, D_MODEL), lambda i: (0, 0))
    f32o = jax.ShapeDtypeStruct((t, D_MODEL), F32)
    bfo = jax.ShapeDtypeStruct((t, D_MODEL), BF16)
    return _pc(body, name="mix_bwd",
               out_shape=(bfo, bfo, jax.ShapeDtypeStruct((N_CHIP, t, 2 * D_MODEL), BF16), bfo, bfo),
               grid=(t // tm,),
               in_specs=[tile,
                         pl.BlockSpec((tm, D_MODEL), lambda i: (i, 6)),
                         pl.BlockSpec((tm, D_MODEL), lambda i: (i, 7)),
                         tile, tile, wsp, wsp, wsp],
               out_specs=(tile, tile, pl.BlockSpec((None, tm, 2 * D_MODEL), lambda i: (DH_SLOT[3], i, 0)),
                          tile, tile),
               sem=("parallel",))(dr1, h, h, a, b, wo, wb0, wb1)


FF_TILE = 1408
FF_NJ = D_FF // FF_TILE


def _shift_down(v, k):
    return pltpu.roll(v, k, 0)


def _shift_up(v, k):
    return pltpu.roll(v, v.shape[0] - k, 0)


HALO = 16
FF_PIECES = ((0, 768), (768, FF_TILE))


def _ffn_up_act(x1b, wup_st, convw, convb, tm, comm):
    t = x1b.shape[0]
    ni = t // tm
    nth = tm // HALO

    def body(x_ref, xp_ref, wg_ref, wv_ref, cw_ref, cb_ref, h2_ref, act_ref):
        wg = wg_ref[...]
        gate = _dot(x_ref[...], wg).astype(BF16)
        val = _dot(x_ref[...], wv_ref[...]).astype(BF16)
        prev = (_dot(xp_ref[...], wg) * (pl.program_id(0) > 0).astype(F32)).astype(BF16)
        h2_ref[0] = gate
        h2_ref[1] = val
        ext = jnp.concatenate([prev.astype(F32), gate.astype(F32)], axis=0)
        gc = (cw_ref[0:1, :] * _shift_down(ext, 2) + cw_ref[1:2, :] * _shift_down(ext, 1)
              + cw_ref[2:3, :] * ext + cb_ref[...])[HALO:, :].astype(BF16)
        h2_ref[2] = gc
        act_ref[...] = (_gelu(gc.astype(F32)) * val.astype(F32)).astype(BF16)

    res = _hosted_call(
        body, comm,
        lambda: (pl.program_id(0) == 0) & (pl.program_id(1) == 0),
        lambda: (pl.program_id(0) == ni - 1) & (pl.program_id(1) == FF_NJ - 1),
        mid=lambda: (pl.program_id(0) == max(ni - 2, 0)) & (pl.program_id(1) == 0),
        name="ffn_up",
        out_shape=(jax.ShapeDtypeStruct((3, t, D_FF), BF16), jax.ShapeDtypeStruct((t, D_FF), BF16)),
        grid=(ni, FF_NJ),
        in_specs=[pl.BlockSpec((tm, D_MODEL), lambda i, j: (i, 0)),
                  pl.BlockSpec((HALO, D_MODEL), lambda i, j: (jnp.maximum(i * nth - 1, 0), 0)),
                  pl.BlockSpec((None, D_MODEL, FF_TILE), lambda i, j: (j, 0, 0)),
                  pl.BlockSpec((None, D_MODEL, FF_TILE), lambda i, j: (j + FF_NJ, 0, 0)),
                  pl.BlockSpec((3, FF_TILE), lambda i, j: (0, j)),
                  pl.BlockSpec((1, FF_TILE), lambda i, j: (0, j))],
        out_specs=(pl.BlockSpec((3, tm, FF_TILE), lambda i, j: (0, i, j)),
                   pl.BlockSpec((tm, FF_TILE), lambda i, j: (i, j))),
        scratch=[], sem=("arbitrary", "arbitrary"),
        args=(x1b, x1b, wup_st, wup_st, convw, convb))
    return res[0], res[1], res[2:]


def _out_fwd_bwd(act, x1b, r1, p2, tgt, wd, wpg, wpp, g1, b1, g2, b2, tm):
    t = r1.shape[0]

    def body(act_ref, x1b_ref, r1_ref, p_ref, tgt_ref, wd_ref, wpg_ref, wpp_ref,
             g1_ref, b1_ref, g2_ref, b2_ref,
             dr2_ref, dpg_ref, dpp_ref, loss_ref, dg2_ref, db2_ref):
        i = pl.program_id(0)

        @pl.when(i == 0)
        def _():
            loss_ref[...] = jnp.zeros_like(loss_ref)
            dg2_ref[...] = jnp.zeros_like(dg2_ref)
            db2_ref[...] = jnp.zeros_like(db2_ref)

        ffn = _dot(act_ref[...], wd_ref[...])
        pg = _dot(x1b_ref[...], wpg_ref[...])
        pp = _dot(p_ref[...], wpp_ref[...])
        s = _sig(pg)
        xh1, _ = _ln_stats(r1_ref[...])
        x1 = xh1 * g1_ref[...] + b1_ref[...]
        r2 = ALPHA * x1 + ffn + s * pp
        xh2, rstd2 = _ln_stats(r2)
        g2v = g2_ref[...]
        diff = xh2 * g2v + b2_ref[...] - tgt_ref[...]
        part = jnp.sum(jnp.sum(diff * diff, axis=1, keepdims=True), axis=0, keepdims=True)
        loss_ref[...] += jnp.broadcast_to(part * (0.5 / D_MODEL), loss_ref.shape)
        dy = diff * (1.0 / D_MODEL)
        dg2_ref[...] += _colsum8(dy * xh2)
        db2_ref[...] += _colsum8(dy)
        dr2 = _ln_bwd(dy * g2v, xh2, rstd2)
        dr2_ref[...] = dr2
        dpg_ref[...] = (dr2 * pp * s * (1.0 - s)).astype(BF16)
        dpp_ref[...] = (dr2 * s).astype(BF16)

    tile = pl.BlockSpec((tm, D_MODEL), lambda i: (i, 0))
    vec = pl.BlockSpec((1, D_MODEL), lambda i: (0, 0))
    acc8 = pl.BlockSpec((8, D_MODEL), lambda i: (0, 0))
    acc_shape = jax.ShapeDtypeStruct((8, D_MODEL), F32)
    return _pc(body, name="out_fwd_bwd",
               out_shape=(jax.ShapeDtypeStruct((t, D_MODEL), F32),
                          jax.ShapeDtypeStruct((t, D_MODEL), BF16),
                          jax.ShapeDtypeStruct((t, D_MODEL), BF16),
                          acc_shape, acc_shape, acc_shape),
               grid=(t // tm,),
               in_specs=[pl.BlockSpec((tm, D_FF), lambda i: (i, 0)), tile, tile,
                         pl.BlockSpec((tm, PLE_DIM), lambda i: (i, 0)), tile,
                         pl.BlockSpec((D_FF, D_MODEL), lambda i: (0, 0)),
                         pl.BlockSpec((D_MODEL, D_MODEL), lambda i: (0, 0)),
                         pl.BlockSpec((PLE_DIM, D_MODEL), lambda i: (0, 0)),
                         vec, vec, vec, vec],
               out_specs=(tile, tile, tile, acc8, acc8, acc8),
               sem=("arbitrary",))(act, x1b, r1, p2, tgt, wd, wpg, wpp, g1, b1, g2, b2)


def _ffn_bwd(h2, dr2, wd, wup_st, dpg, wpg, r1, g1, convw, tm):
    t = r1.shape[0]
    ni = t // tm
    nth = tm // HALO
    last_halo = t // HALO - 1
    main_rows = slice(0, tm)

    def body(g_ref, gc_ref, gcn_ref, v_ref, vn_ref, dr2_ref, dr2n_ref, wd_ref, wug_ref, wuv_ref,
             cw_ref, dpg_ref, wpg_ref, r1_ref, g1_ref,
             dh2_ref, dr1_ref, dcw_ref, dcb_ref, dg1_ref, db1_ref, acc):
        i = pl.program_id(0)
        j = pl.program_id(1)

        @pl.when((i == 0) & (j == 0))
        def _():
            dcw_ref[...] = jnp.zeros_like(dcw_ref)
            dcb_ref[...] = jnp.zeros_like(dcb_ref)
            dg1_ref[...] = jnp.zeros_like(dg1_ref)
            db1_ref[...] = jnp.zeros_like(db1_ref)

        dr2v = dr2_ref[...].astype(BF16)
        dr2n = dr2n_ref[...].astype(BF16)
        more = (i < ni - 1).astype(F32)
        prod = None
        dcw_parts, dcb_parts = [], []
        for c0, c1 in FF_PIECES:
            pc = slice(c0, c1)
            da = _dot(dr2v, wd_ref[pc, :], NT)
            dnext = _dot(dr2n, wd_ref[pc, :], NT) * more
            gc = jnp.concatenate([gc_ref[:, pc].astype(F32), gcn_ref[:, pc].astype(F32)], axis=0)
            vext = jnp.concatenate([v_ref[:, pc].astype(F32), vn_ref[:, pc].astype(F32)], axis=0)
            dext = jnp.concatenate([da, dnext], axis=0)
            gl, dgl = _gelu_and_grad(gc)
            d_gc = dext * vext * dgl
            up1 = _shift_up(d_gc, 1)[main_rows, :]
            up2 = _shift_up(d_gc, 2)[main_rows, :]
            dm = d_gc[main_rows, :]
            d_gate = (cw_ref[2:3, pc] * dm + cw_ref[1:2, pc] * up1 + cw_ref[0:1, pc] * up2).astype(BF16)
            d_val = (da * gl[main_rows, :]).astype(BF16)
            dh2_ref[0, :, pc] = d_gate
            dh2_ref[1, :, pc] = d_val
            g = g_ref[:, pc].astype(F32)
            s0 = jnp.sum(g * up2, axis=0, keepdims=True)
            s1 = jnp.sum(g * up1, axis=0, keepdims=True)
            s2 = jnp.sum(g * dm, axis=0, keepdims=True)
            rowid = lax.broadcasted_iota(jnp.int32, (8, c1 - c0), 0)
            dcw_parts.append(jnp.where(rowid == 0, s0, jnp.where(rowid == 1, s1,
                                                                 jnp.where(rowid == 2, s2, 0.0))))
            dcb_parts.append(_colsum8(dm))
            part = _dot(d_gate, wug_ref[:, pc], NT) + _dot(d_val, wuv_ref[:, pc], NT)
            prod = part if prod is None else prod + part
        dcw_part = jnp.concatenate(dcw_parts, axis=1)
        dcb_part = jnp.concatenate(dcb_parts, axis=1)
        for jj in range(FF_NJ):
            @pl.when(j == jj)
            def _(jj=jj):
                cols = slice(jj * FF_TILE, (jj + 1) * FF_TILE)
                dcw_ref[:, cols] += dcw_part
                dcb_ref[:, cols] += dcb_part

        @pl.when(j == 0)
        def _():
            acc[...] = prod

        @pl.when(j > 0)
        def _():
            acc[...] += prod

        @pl.when(j == FF_NJ - 1)
        def _():
            d_x1 = acc[...] + _dot(dpg_ref[...], wpg_ref[...], NT) + ALPHA * dr2_ref[...]
            xh, rstd = _ln_stats(r1_ref[...])
            dg1_ref[...] += _colsum8(d_x1 * xh)
            db1_ref[...] += _colsum8(d_x1)
            dr1_ref[...] = _ln_bwd(d_x1 * g1_ref[...], xh, rstd)

    def h2_main(part):
        return pl.BlockSpec((None, tm, FF_TILE), lambda i, j: (part, i, j))

    def h2_next(part):
        return pl.BlockSpec((None, HALO, FF_TILE),
                            lambda i, j: (part, jnp.minimum((i + 1) * nth, last_halo), j))

    tile = pl.BlockSpec((tm, D_MODEL), lambda i, j: (i, 0))
    acc8 = pl.BlockSpec((8, D_MODEL), lambda i, j: (0, 0))
    accff = pl.BlockSpec((8, D_FF), lambda i, j: (0, 0))
    acc_shape = jax.ShapeDtypeStruct((8, D_MODEL), F32)
    accff_shape = jax.ShapeDtypeStruct((8, D_FF), F32)
    return _pc(body, name="ffn_bwd",
               out_shape=(jax.ShapeDtypeStruct((2, t, D_FF), BF16),
                          jax.ShapeDtypeStruct((t, D_MODEL), F32),
                          accff_shape, accff_shape, acc_shape, acc_shape),
               grid=(ni, FF_NJ),
               in_specs=[h2_main(0), h2_main(2), h2_next(2), h2_main(1), h2_next(1),
                         tile,
                         pl.BlockSpec((HALO, D_MODEL), lambda i, j: (jnp.minimum((i + 1) * nth, last_halo), 0)),
                         pl.BlockSpec((FF_TILE, D_MODEL), lambda i, j: (j, 0)),
                         pl.BlockSpec((None, D_MODEL, FF_TILE), lambda i, j: (j, 0, 0)),
                         pl.BlockSpec((None, D_MODEL, FF_TILE), lambda i, j: (j + FF_NJ, 0, 0)),
                         pl.BlockSpec((3, FF_TILE), lambda i, j: (0, j)),
                         tile, pl.BlockSpec((D_MODEL, D_MODEL), lambda i, j: (0, 0)),
                         tile, pl.BlockSpec((1, D_MODEL), lambda i, j: (0, 0))],
               out_specs=(pl.BlockSpec((2, tm, FF_TILE), lambda i, j: (0, i, j)),
                          tile, accff, accff, acc8, acc8),
               scratch=[pltpu.VMEM((tm, D_MODEL), F32)],
               sem=("arbitrary", "arbitrary"))(h2, h2, h2, h2, h2, dr2, dr2, wd, wup_st, wup_st,
                                               convw, dpg, wpg, r1, g1)


ANY = pl.BlockSpec(memory_space=pl.ANY)


def _chip_peers():
    x, y, c = lax.axis_index("x"), lax.axis_index("y"), lax.axis_index("c")
    return x, y, c, [(1 - x, y), (x, 1 - y), (1 - x, 1 - y)]


def _gather_comm(halved, whole=(), blocks=None):
    n, nw = len(halved), len(whole)
    blocks = blocks or {}

    def at(ti, ref, chip, *rest):
        return ref.at[(chip, blocks[ti][1]) + rest] if ti in blocks else ref.at[(chip,) + rest]

    def copies(ins, outs, sems):
        ici_send, ici_recv, d2d_send, d2d_recv, own_send, own_recv = sems
        x, y, c, peers = _chip_peers()
        me = 2 * x + y
        sibling = (x, y, 1 - c)
        own, ici, ici_wait, fwd, fwd_wait = [], [], [], [], []
        for ti in range(n + nw):
            src, dst = ins[ti], outs[ti]
            own.append(pltpu.make_async_remote_copy(
                src_ref=src, dst_ref=at(ti, dst, me), send_sem=own_send.at[ti], recv_sem=own_recv.at[ti],
                device_id=sibling, device_id_type=MESH))
            for k, (px, py) in enumerate(peers):
                pk = 2 * px + py
                sem = dict(send_sem=ici_send.at[ti * 3 + k], recv_sem=ici_recv.at[ti * 3 + k],
                           device_id=(px, py, c), device_id_type=MESH)
                if ti < n:
                    ici.append(pltpu.make_async_remote_copy(src_ref=src.at[c], dst_ref=at(ti, dst, me, c), **sem))
                    ici_wait.append(pltpu.make_async_remote_copy(src_ref=src.at[c], dst_ref=at(ti, dst, pk, c),
                                                                 **sem))
                    dsem = dict(send_sem=d2d_send.at[ti * 3 + k], recv_sem=d2d_recv.at[ti * 3 + k],
                                device_id=sibling, device_id_type=MESH)
                    fwd.append(pltpu.make_async_remote_copy(src_ref=at(ti, dst, pk, c), dst_ref=at(ti, dst, pk, c),
                                                            **dsem))
                    fwd_wait.append(pltpu.make_async_remote_copy(
                        src_ref=at(ti, dst, pk, 1 - c), dst_ref=at(ti, dst, pk, 1 - c), **dsem))
                else:
                    ici.append(pltpu.make_async_remote_copy(src_ref=src, dst_ref=dst.at[me], **sem))
                    ici_wait.append(pltpu.make_async_remote_copy(src_ref=src, dst_ref=dst.at[pk], **sem))
        return own, ici, ici_wait, fwd, fwd_wait

    def start(ins, outs, sems):
        own, ici, _, _, _ = copies(ins, outs, sems)
        for cp in own + ici:
            cp.start()

    def finish(ins, outs, sems):
        own, ici, ici_wait, fwd, fwd_wait = copies(ins, outs, sems)
        for i, cp in enumerate(ici_wait):
            cp.wait_recv()
            if i < len(fwd):
                fwd[i].start()
        for cp in fwd_wait + own:
            cp.wait_recv()
        for cp in own + ici + fwd:
            cp.wait_send()

    def middle(ins, outs, sems):
        _, _, ici_wait, fwd, _ = copies(ins, outs, sems)
        for i, cp in enumerate(ici_wait):
            cp.wait_recv()
            if i < len(fwd):
                fwd[i].start()

    def finish_late(ins, outs, sems):
        own, ici, _, fwd, fwd_wait = copies(ins, outs, sems)
        for cp in fwd_wait + own:
            cp.wait_recv()
        for cp in own + ici + fwd:
            cp.wait_send()

    srcs = list(halved) + list(whole)
    shapes = [jax.ShapeDtypeStruct((N_CHIP,) + ((blocks[ti][0],) if ti in blocks else ()) + s.shape, s.dtype)
              for ti, s in enumerate(srcs)]
    buffers = [(ti, blk[2]) for ti, blk in sorted(blocks.items()) if blk[2] is not None]
    aliases = {len(srcs) + bi: ti for bi, (ti, _) in enumerate(buffers)}
    return _Comm(srcs + [buf for _, buf in buffers], shapes,
                 [pltpu.SemaphoreType.DMA((3 * (n + nw),)), pltpu.SemaphoreType.DMA((3 * (n + nw),)),
                  pltpu.SemaphoreType.DMA((max(3 * n, 1),)), pltpu.SemaphoreType.DMA((max(3 * n, 1),)),
                  pltpu.SemaphoreType.DMA((n + nw,)), pltpu.SemaphoreType.DMA((n + nw,))],
                 start, finish, middle, finish_late, aliases)


def _sibling_exchange_comm(grads):
    n = len(grads)

    def copies(ins, outs, sems):
        send_sems, recv_sems = sems
        x, y, c = lax.axis_index("x"), lax.axis_index("y"), lax.axis_index("c")
        res = []
        for ti in range(n):
            half = ins[ti].shape[1] // 2
            res.append(pltpu.make_async_remote_copy(
                src_ref=ins[ti].at[:, pl.ds(pl.multiple_of((1 - c) * half, 16), half), :],
                dst_ref=outs[ti],
                send_sem=send_sems.at[ti], recv_sem=recv_sems.at[ti],
                device_id=(x, y, 1 - c), device_id_type=MESH))
        return res

    def start(ins, outs, sems):
        for cp in copies(ins, outs, sems):
            cp.start()

    def finish(ins, outs, sems):
        for cp in copies(ins, outs, sems):
            cp.wait()

    return _Comm(grads, [jax.ShapeDtypeStruct((N_CHIP, g.shape[1] // 2, g.shape[2]), g.dtype) for g in grads],
                 [pltpu.SemaphoreType.DMA((n,)), pltpu.SemaphoreType.DMA((n,))], start, finish)


def _in_proj_gathering(x2b, own, chip, tm, comm):
    t = x2b.shape[0]
    ni = t // tm
    half, cols = own.shape[1], own.shape[2]
    nci, nco = len(comm.ins), len(comm.out_shapes)

    def body(chip_ref, x_ref, own_ref, own_hbm, *rest):
        c_in = rest[:nci]
        h_ref, win_out = rest[nci:nci + 2]
        c_out = rest[nci + 2:nci + 2 + nco]
        w_scr, ici_send, ici_recv, d2d_send, d2d_recv, own_sems, ld_sems = rest[nci + 2 + nco:nci + 9 + nco]
        c_sem = rest[nci + 9 + nco:]
        s, i = pl.program_id(0), pl.program_id(1)
        x, y, c, peers = _chip_peers()
        me = 2 * x + y
        sibling = (x, y, 1 - c)

        def ici(k, slot):
            px, py = peers[k]
            return pltpu.make_async_remote_copy(
                src_ref=own_hbm.at[c], dst_ref=win_out.at[slot, c],
                send_sem=ici_send.at[k], recv_sem=ici_recv.at[k],
                device_id=(px, py, c), device_id_type=MESH)

        def forward(k, core):
            pk = 2 * peers[k][0] + peers[k][1]
            return pltpu.make_async_remote_copy(
                src_ref=win_out.at[pk, core], dst_ref=win_out.at[pk, core],
                send_sem=d2d_send.at[k], recv_sem=d2d_recv.at[k],
                device_id=sibling, device_id_type=MESH)

        place_own = pltpu.make_async_remote_copy(
            src_ref=own_hbm, dst_ref=win_out.at[me], send_sem=own_sems.at[0], recv_sem=own_sems.at[1],
            device_id=sibling, device_id_type=MESH)

        @pl.when((s == 0) & (i == 0))
        def _():
            for k in range(2):
                ici(k, me).start()
            place_own.start()

        @pl.when(s == 0)
        def _():
            xv = x_ref[...]
            h_ref[...] = (_dot(xv[:, :half], own_ref[0]) + _dot(xv[:, half:], own_ref[1])).astype(BF16)

        for k in range(3):
            @pl.when((s == k + 1) & (i == 0))
            def _(k=k):
                pk = 2 * peers[k][0] + peers[k][1]
                ici(k, pk).wait_recv()
                if k == 0:
                    ici(2, me).start()
                forward(k, c).start()
                forward(k, 1 - c).wait_recv()
                loads = [pltpu.make_async_copy(win_out.at[pk, hh], w_scr.at[hh], ld_sems.at[hh])
                         for hh in range(2)]
                for ld in loads:
                    ld.start()
                for ld in loads:
                    ld.wait()
                if k == 1:
                    comm.start(c_in, c_out, c_sem)

        @pl.when(s > 0)
        def _():
            xv = x_ref[...]
            h_ref[...] = (_dot(xv[:, :half], w_scr[0]) + _dot(xv[:, half:], w_scr[1])).astype(BF16)

        @pl.when((s == N_CHIP - 1) & (i == ni - 1))
        def _():
            place_own.wait()
            for k in range(3):
                ici(k, me).wait_send()
                forward(k, c).wait_send()
            comm.finish(c_in, c_out, c_sem)

    def shard_col(s, me):
        return jnp.where(s == 0, me, me ^ jnp.where(s == 1, 2, jnp.where(s == 2, 1, 3)))

    res = _pc(body, name="in_proj",
              out_shape=(jax.ShapeDtypeStruct((t, N_CHIP * cols), BF16),
                         jax.ShapeDtypeStruct((N_CHIP,) + own.shape, own.dtype)) + tuple(comm.out_shapes),
              grid=(N_CHIP, ni), nsp=1,
              in_specs=[pl.BlockSpec((tm, 2 * half), lambda s, i, chip_ref: (i, 0)),
                        pl.BlockSpec(own.shape, lambda s, i, chip_ref: (0, 0, 0)),
                        ANY] + [ANY] * nci,
              out_specs=(pl.BlockSpec((tm, cols), lambda s, i, chip_ref: (i, shard_col(s, chip_ref[0]))),
                         ANY) + tuple([ANY] * nco),
              scratch=[pltpu.VMEM(own.shape, own.dtype),
                       pltpu.SemaphoreType.DMA((3,)), pltpu.SemaphoreType.DMA((3,)),
                       pltpu.SemaphoreType.DMA((3,)), pltpu.SemaphoreType.DMA((3,)),
                       pltpu.SemaphoreType.DMA((2,)), pltpu.SemaphoreType.DMA((2,))] + comm.sems,
              sem=("arbitrary", "arbitrary"))(chip, x2b, own, own, *comm.ins)
    return res[0], res[1], res[2:]


def _rs_add_halves(name, grad, recv, core):
    _, r, cdim = grad.shape
    half = r // 2
    tr = _row_tile(half, cdim, mult=16)
    nr = half // tr

    def body(c_ref, g_ref, r_ref, o_ref):
        o_ref[...] = (g_ref[...].astype(F32) + r_ref[...].astype(F32)).astype(BF16)

    return _pc(body, name=name, out_shape=jax.ShapeDtypeStruct((N_CHIP, half, cdim), BF16),
               grid=(N_CHIP, nr), nsp=1,
               in_specs=[pl.BlockSpec((None, tr, cdim), lambda j, i, c_ref: (j, c_ref[0] * nr + i, 0)),
                         pl.BlockSpec((None, tr, cdim), lambda j, i, c_ref: (j, i, 0))],
               out_specs=pl.BlockSpec((None, tr, cdim), lambda j, i, c_ref: (j, i, 0)),
               sem=("parallel", "parallel"))(core, grad, recv)


def _chip_exchange_comm(parts):
    n = len(parts)

    def copies(ins, outs, sems):
        send_sems, recv_sems = sems
        x, y, c, peers = _chip_peers()
        return [pltpu.make_async_remote_copy(
            src_ref=ins[ti].at[2 * px + py], dst_ref=outs[ti].at[k],
            send_sem=send_sems.at[ti * 3 + k], recv_sem=recv_sems.at[ti * 3 + k],
            device_id=(px, py, c), device_id_type=MESH)
            for ti in range(n) for k, (px, py) in enumerate(peers)]

    def start(ins, outs, sems):
        for cp in copies(ins, outs, sems):
            cp.start()

    def finish(ins, outs, sems):
        for cp in copies(ins, outs, sems):
            cp.wait()

    return _Comm(parts, [jax.ShapeDtypeStruct((3,) + p.shape[1:], p.dtype) for p in parts],
                 [pltpu.SemaphoreType.DMA((3 * n,)), pltpu.SemaphoreType.DMA((3 * n,))], start, finish)


def _rs_sum_chips(name, part, recv, chip):
    _, half, cdim = recv.shape
    tr = _row_tile(half, cdim, mult=16)

    def body(chip_ref, p_ref, r_ref, o_ref):
        o_ref[...] = ((p_ref[...].astype(F32) + r_ref[0].astype(F32)) + r_ref[1].astype(F32)
                      ) + r_ref[2].astype(F32)

    return _pc(body, name=name, out_shape=jax.ShapeDtypeStruct((half, cdim), F32),
               grid=(half // tr,), nsp=1,
               in_specs=[pl.BlockSpec((None, tr, cdim), lambda i, chip_ref: (chip_ref[0], i, 0)),
                         pl.BlockSpec((3, tr, cdim), lambda i, chip_ref: (0, i, 0))],
               out_specs=pl.BlockSpec((tr, cdim), lambda i, chip_ref: (i, 0)),
               sem=("parallel",))(chip, part, recv)


def _send_halves_comm(halves):
    n = len(halves)

    def copies(ins, outs, sems):
        send_sems, recv_sems = sems
        x, y, c = lax.axis_index("x"), lax.axis_index("y"), lax.axis_index("c")
        return [pltpu.make_async_remote_copy(
            src_ref=ins[ti], dst_ref=outs[ti], send_sem=send_sems.at[ti], recv_sem=recv_sems.at[ti],
            device_id=(x, y, 1 - c), device_id_type=MESH) for ti in range(n)]

    def start(ins, outs, sems):
        for cp in copies(ins, outs, sems):
            cp.start()

    def finish(ins, outs, sems):
        for cp in copies(ins, outs, sems):
            cp.wait()

    return _Comm(halves, [jax.ShapeDtypeStruct(hv.shape, hv.dtype) for hv in halves],
                 [pltpu.SemaphoreType.DMA((n,)), pltpu.SemaphoreType.DMA((n,))], start, finish)


def _adamw_rows(name, mine, theirs, w, m, v, core):
    half, cdim = mine.shape
    tr = _row_tile(half, cdim, budget=1 << 19)
    nrh = half // tr

    def body(c_ref, mine_ref, theirs_ref, w_ref, m_ref, v_ref, g_ref, d_ref, m2_ref, v2_ref):
        is_mine = (pl.program_id(0) // nrh) == c_ref[0]
        g = jnp.where(is_mine, mine_ref[...], theirs_ref[...])
        d, m2, v2 = _adamw(w_ref[...], g, m_ref[...], v_ref[...])
        g_ref[...] = g
        d_ref[...] = d
        m2_ref[...] = m2
        v2_ref[...] = v2

    htile = pl.BlockSpec((tr, cdim), lambda i, c_ref: (i % nrh, 0))
    tile = pl.BlockSpec((tr, cdim), lambda i, c_ref: (i, 0))
    shp = jax.ShapeDtypeStruct((2 * half, cdim), F32)
    return _pc(body, name=name, out_shape=(shp, shp, shp, shp), grid=(2 * nrh,), nsp=1,
               in_specs=[htile, htile, tile, tile, tile], out_specs=(tile, tile, tile, tile),
               sem=("parallel",))(core, mine, theirs, w, m, v)


def _adamw_whole(name, g, w, m, v):
    def body(g_ref, w_ref, m_ref, v_ref, d_ref, m2_ref, v2_ref):
        d, m2, v2 = _adamw(w_ref[...], g_ref[...], m_ref[...], v_ref[...])
        d_ref[...] = d
        m2_ref[...] = m2
        v2_ref[...] = v2

    shp = jax.ShapeDtypeStruct(g.shape, F32)
    return _pc(body, name=name, out_shape=(shp, shp, shp))(g, w, m, v)


SMALL_LAYOUT = (
    ("sgu_w_s", 1024, 1, 0),
    ("sgu_b_s", 8, 1, 1024),
    ("sgu_norm_g", 1, 0, 0),
    ("sgu_norm_b", 1, 0, 1),
    ("hgrn_norm_g", 1, 0, 3),
    ("ln1_g", 1, 0, 4),
    ("ln1_b", 1, 0, 5),
    ("ffn_conv_b", 1, 2, 3),
    ("ln2_g", 1, 0, 6),
    ("ln2_b", 1, 0, 7),
)
LB_ROW = 2
LOSS_ROW = 8
PACK_SHAPES = ((16, D_MODEL), (N_GROUP * 128 + 16, 128), (8, D_FF))
GATH_DTYPES = (F32, BF16, F32)


def _small_allreduce_adamw(rows1024, dws, dbs, dcw, dcb, logits, m_logits, v_logits,
                           small_w, small_m, small_v):
    ns = len(SMALL_LAYOUT)
    nr = len(rows1024)
    nb = len(PACK_SHAPES)

    def body(*refs):
        row_refs = refs[:nr]
        dws_ref, dbs_ref, dcw_ref, dcb_ref, lg_ref, mlg_ref, vlg_ref = refs[nr:nr + 7]
        pos = nr + 7
        w_refs = refs[pos:pos + ns]
        m_refs = refs[pos + ns:pos + 2 * ns]
        v_refs = refs[pos + 2 * ns:pos + 3 * ns]
        pos += 3 * ns
        loss_ref, dcw_out = refs[pos:pos + 2]
        lg_outs = refs[pos + 2:pos + 6]
        pos += 6
        outs = refs[pos:pos + 4 * ns]
        pos += 4 * ns
        pack = refs[pos:pos + nb]
        sib = refs[pos + nb:pos + 2 * nb]
        gath = refs[pos + 2 * nb:pos + 3 * nb]
        d2d_send, d2d_recv, ici_send, ici_recv = refs[pos + 3 * nb:]

        x, y, c, peers = _chip_peers()
        me = 2 * x + y
        sibling = (x, y, 1 - c)

        pack[0][...] = jnp.zeros(PACK_SHAPES[0], F32)
        for k in range(nr):
            pack[0][k:k + 1, :] = row_refs[k][0:1, :]
        pack[1][0:N_GROUP * 128, :] = dws_ref[...]
        pack[1][N_GROUP * 128:N_GROUP * 128 + 8, :] = dbs_ref[...]
        pack[1][N_GROUP * 128 + 8:, :] = jnp.zeros((8, 128), F32)
        pack[2][...] = jnp.zeros(PACK_SHAPES[2], F32)
        pack[2][0:3, :] = dcw_ref[0:3, :]
        pack[2][3:4, :] = dcb_ref[0:1, :]

        d2d = [pltpu.make_async_remote_copy(
            src_ref=pack[b], dst_ref=sib[b], send_sem=d2d_send.at[b], recv_sem=d2d_recv.at[b],
            device_id=sibling, device_id_type=MESH) for b in range(nb)]
        for cp in d2d:
            cp.start()
        for cp in d2d:
            cp.wait()
        for b in range(nb):
            gath[b][me] = (pack[b][...] + sib[b][...]).astype(GATH_DTYPES[b])

        ici, ici_wait = [], []
        for b in range(nb):
            for k, (px, py) in enumerate(peers):
                sem = dict(send_sem=ici_send.at[b * 3 + k], recv_sem=ici_recv.at[b * 3 + k],
                           device_id=(px, py, c), device_id_type=MESH)
                ici.append(pltpu.make_async_remote_copy(src_ref=gath[b].at[me], dst_ref=gath[b].at[me], **sem))
                ici_wait.append(pltpu.make_async_remote_copy(
                    src_ref=gath[b].at[me], dst_ref=gath[b].at[2 * px + py], **sem))
        for cp in ici:
            cp.start()
        for cp in ici_wait:
            cp.wait_recv()
        for cp in ici:
            cp.wait_send()

        tot = pack
        for b in range(nb):
            tot[b][...] = ((gath[b][0].astype(F32) + gath[b][1].astype(F32)) + gath[b][2].astype(F32)
                           ) + gath[b][3].astype(F32)

        loss_ref[...] = tot[0][LOSS_ROW:LOSS_ROW + 1, :]
        dcw_out[...] = tot[2][...]
        lb = _sig(lg_ref[0:1, :] - lg_ref[1:2, :])
        d0 = tot[0][LB_ROW:LB_ROW + 1, :] * lb * (1.0 - lb)
        rowid = lax.broadcasted_iota(jnp.int32, (2, D_MODEL), 0)
        g_lg = jnp.where(rowid == 0, d0, -d0)
        dl, ml, vl = _adamw(lg_ref[...], g_lg, mlg_ref[...], vlg_ref[...])
        lg_outs[0][...] = g_lg
        lg_outs[1][...] = dl
        lg_outs[2][...] = ml
        lg_outs[3][...] = vl
        for si, (_, rows, b, r0) in enumerate(SMALL_LAYOUT):
            g = tot[b][r0:r0 + rows, :]
            dl, ml, vl = _adamw(w_refs[si][...], g, m_refs[si][...], v_refs[si][...])
            outs[4 * si][...] = g
            outs[4 * si + 1][...] = dl
            outs[4 * si + 2][...] = ml
            outs[4 * si + 3][...] = vl

    shapes = [jax.ShapeDtypeStruct((1, D_MODEL), F32), jax.ShapeDtypeStruct((8, D_FF), F32)]
    shapes += [jax.ShapeDtypeStruct((2, D_MODEL), F32)] * 4
    for w in small_w:
        shapes += [jax.ShapeDtypeStruct(w.shape, F32)] * 4
    scratch = [pltpu.VMEM(shp, F32) for shp in PACK_SHAPES]
    scratch += [pltpu.VMEM(shp, F32) for shp in PACK_SHAPES]
    scratch += [pltpu.VMEM((N_CHIP,) + shp, dt) for shp, dt in zip(PACK_SHAPES, GATH_DTYPES)]
    scratch += [pltpu.SemaphoreType.DMA((nb,)), pltpu.SemaphoreType.DMA((nb,)),
                pltpu.SemaphoreType.DMA((3 * nb,)), pltpu.SemaphoreType.DMA((3 * nb,))]
    vm = pl.BlockSpec(memory_space=pltpu.VMEM)
    n_in = nr + 7 + 3 * ns
    res = _pc(body, name="small_allreduce_adamw", out_shape=tuple(shapes),
              in_specs=[vm] * n_in, out_specs=tuple([vm] * len(shapes)),
              scratch=scratch)(*rows1024, dws, dbs, dcw, dcb, logits, m_logits, v_logits,
                               *small_w, *small_m, *small_v)
    return res[0], res[1], res[2:6], res[6:]


def kernel(x, p, w_in, sgu_w_s, sgu_b_s, sgu_norm_g, sgu_norm_b, hgrn_lb_logits, hgrn_norm_g, w_branch, w_out, ln1_g, ln1_b, ffn_w_up, ffn_conv_w, ffn_conv_b, ffn_w_down, ln2_g, ln2_b, ple_w_proj, ple_w_gate, loss_target, m_w_in, m_sgu_w_s, m_sgu_b_s, m_sgu_norm_g, m_sgu_norm_b, m_hgrn_lb_logits, m_hgrn_norm_g, m_w_branch, m_w_out, m_ln1_g, m_ln1_b, m_ffn_w_up, m_ffn_conv_w, m_ffn_conv_b, m_ffn_w_down, m_ln2_g, m_ln2_b, m_ple_w_proj, m_ple_w_gate, v_w_in, v_sgu_w_s, v_sgu_b_s, v_sgu_norm_g, v_sgu_norm_b, v_hgrn_lb_logits, v_hgrn_norm_g, v_w_branch, v_w_out, v_ln1_g, v_ln1_b, v_ffn_w_up, v_ffn_conv_w, v_ffn_conv_b, v_ffn_w_down, v_ln2_g, v_ln2_b, v_ple_w_proj, v_ple_w_gate):
    t = x.shape[1]
    x2 = x.reshape(t, D_MODEL)
    x2b = x2.astype(BF16)
    p2 = p.reshape(t, PLE_DIM)
    tgt = loss_target.reshape(t, D_MODEL)
    core = lax.axis_index("c").astype(jnp.int32).reshape(1)
    chip_id = (2 * lax.axis_index("x") + lax.axis_index("y")).astype(jnp.int32).reshape(1)

    big_w = [w_in[0], w_branch[0, 0], w_branch[0, 1], w_out[0], ffn_w_up[0], ffn_w_down[0],
             ple_w_proj[0], ple_w_gate[0]]
    big_m = [m_w_in[0], m_w_branch[0, 0], m_w_branch[0, 1], m_w_out[0], m_ffn_w_up[0],
             m_ffn_w_down[0], m_ple_w_proj[0], m_ple_w_gate[0]]
    big_v = [v_w_in[0], v_w_branch[0, 0], v_w_branch[0, 1], v_w_out[0], v_ffn_w_up[0],
             v_ffn_w_down[0], v_ple_w_proj[0], v_ple_w_gate[0]]
    def halves_of(i):
        w = big_w[i]
        return w.astype(BF16).reshape(2, w.shape[0] // 2, w.shape[1])

    def stacked(g, i):
        return g.reshape(N_CHIP, big_w[i].shape[0], big_w[i].shape[1])


    cid = jnp.arange(SGU_BLOCK) // CHUNK
    maskf = (cid[:, None] >= cid[None, :]).astype(F32)
    ws_masked = sgu_w_s[0] * maskf[None]
    wm = ws_masked.astype(BF16)
    wmt = jnp.transpose(ws_masked, (0, 2, 1)).astype(BF16)
    bsb = jnp.broadcast_to(sgu_b_s[0][:, :, None], (N_GROUP, SGU_BLOCK, 128))

    up_rows = big_w[4].shape[0] // 2
    up_blocks = [big_w[4][k * up_rows:(k + 1) * up_rows].astype(BF16).reshape(2, up_rows // 2, -1)
                 for k in range(2)]
    h, win_g, (up_g,) = _in_proj_gathering(x2b, halves_of(0), chip_id, 512,
                                           _gather_comm([up_blocks[0]], blocks={0: (2, 0, None)}))
    win_st = stacked(win_g, 0)
    ya, _ = _sgu_fwd(h, wm, bsb, sgu_norm_g, sgu_norm_b)
    (yb, o_all, st_all), mix_g = _hgrn_fwd(
        h, hgrn_lb_logits, hgrn_norm_g,
        comm=_gather_comm([halves_of(i) for i in (1, 2, 3)] + [up_blocks[1]], [ffn_conv_w[0]],
                          blocks={3: (2, 1, up_g)}))
    wb0, wb1, wo = [stacked(g, i).reshape(D_MODEL, D_MODEL) for g, i in zip(mix_g[:3], (1, 2, 3))]
    wup_st = stacked(mix_g[3], 4)
    convw = jnp.transpose(mix_g[4], (1, 0, 2)).reshape(3, D_FF)
    (r1, a_br, b_br, m_bf, x1b), _ = _mix_fwd(ya, yb, h, x2, wb0, wb1, wo, ln1_g, ln1_b, 256)
    h2, act, out_g = _ffn_up_act(x1b, wup_st, convw, ffn_conv_b, 512,
                                 _gather_comm([halves_of(i) for i in (5, 6, 7)]))
    wd = stacked(out_g[0], 5).reshape(D_FF, D_MODEL)
    wpp = jnp.transpose(stacked(out_g[1], 6), (1, 0, 2)).reshape(PLE_DIM, D_MODEL)
    wpg = stacked(out_g[2], 7).reshape(D_MODEL, D_MODEL)
    dr2, dpg, dpp, loss_acc, dg2, db2 = _out_fwd_bwd(
        act, x1b, r1, p2, tgt, wd, wpg, wpp, ln1_g, ln1_b, ln2_g, ln2_b, 256)

    dh2, dr1, dcw, dcb, dg1, db1 = _ffn_bwd(h2, dr2, wd, wup_st, dpg, wpg, r1, ln1_g, convw, 256)
    d_wd = _mm_tn("ffn_down_wgrad", act, dr2, FF_TILE, 512)
    d_wpg = _mm_tn("ple_gate_wgrad", x1b, dpg, 512, D_MODEL)
    d_wpp_st = _mm_tn("ple_proj_wgrad", p2, dpp, PLE_DIM, PLE_DIM, stacked=True)
    d_wup_st = _mm("ffn_up_wgrad", x1b, dh2, TN, (2, N_CHIP),
                   pl.BlockSpec((t, 512), lambda i, j: (0, i)),
                   pl.BlockSpec((None, t, FF_TILE), lambda i, j: (j // FF_NJ, 0, j % FF_NJ)),
                   jax.ShapeDtypeStruct((N_CHIP, D_MODEL, FF_TILE), BF16),
                   pl.BlockSpec((None, 512, FF_TILE), lambda i, j: (j, i, 0)))
    da_bf, db_bf, dh, dya, dyb = _mix_bwd(dr1, h, a_br, b_br, wo, wb0, wb1, 256)
    d_wo = _mm_tn("out_proj_wgrad", m_bf, dr1, 512, 512)
    d_wb0 = _mm_tn("branch0_wgrad", ya, da_bf, 512, D_MODEL)
    d_wb1 = _mm_tn("branch1_wgrad", yb, db_bf, 512, D_MODEL)
    grads_1 = [d_wb0.reshape(4, 256, D_MODEL), d_wb1.reshape(4, 256, D_MODEL),
               d_wo.reshape(4, 256, D_MODEL), d_wup_st, d_wd.reshape(4, D_FF // 4, D_MODEL),
               d_wpp_st, d_wpg.reshape(4, 256, D_MODEL)]
    (dh, dws, dbs, dgv, dbv), recv_a1 = _sgu_bwd(h, dya, wm, wmt, bsb, sgu_norm_g, sgu_norm_b, maskf, dh,
                                                 comm=_sibling_exchange_comm(grads_1))
    parts_1 = [_rs_add_halves("rs_add_halves%d" % (i + 1), g, r, core)
               for i, (g, r) in enumerate(zip(grads_1, recv_a1))]
    (dh, dlb, dgn), recv_b1 = _hgrn_bwd(h, o_all, dyb, st_all, hgrn_lb_logits, hgrn_norm_g, dh,
                                         comm=_chip_exchange_comm(parts_1))

    halves_1 = [_rs_sum_chips("rs_sum_chips%d" % (i + 1), pt, r, chip_id)
                for i, (pt, r) in enumerate(zip(parts_1, recv_b1))]
    d_win, theirs_1 = _in_proj_wgrad(x2b, dh, 512, D_MODEL, _send_halves_comm(halves_1))
    grads_0 = [d_win]
    recv_a0 = _run_comm("rs_sibling_exchange0", _sibling_exchange_comm(grads_0))
    parts_0 = [_rs_add_halves("rs_add_halves0", grads_0[0], recv_a0[0], core)]
    gx, recv_b0 = _in_proj_xgrad(dh, win_st, dr1, 512, _chip_exchange_comm(parts_0))
    halves_0 = [_rs_sum_chips("rs_sum_chips0", parts_0[0], recv_b0[0], chip_id)]
    theirs_0 = _run_comm("rs_send_halves0", _send_halves_comm(halves_0))
    halves = halves_0 + halves_1
    theirs = list(theirs_0) + list(theirs_1)
    big_out = [_adamw_rows("adamw_big%d" % i, halves[i], theirs[i], big_w[i], big_m[i], big_v[i], core)
               for i in range(len(halves))]

    small_in = dict(sgu_w_s=(sgu_w_s, m_sgu_w_s, v_sgu_w_s), sgu_b_s=(sgu_b_s, m_sgu_b_s, v_sgu_b_s),
                    sgu_norm_g=(sgu_norm_g, m_sgu_norm_g, v_sgu_norm_g),
                    sgu_norm_b=(sgu_norm_b, m_sgu_norm_b, v_sgu_norm_b),
                    hgrn_norm_g=(hgrn_norm_g, m_hgrn_norm_g, v_hgrn_norm_g),
                    ln1_g=(ln1_g, m_ln1_g, v_ln1_g), ln1_b=(ln1_b, m_ln1_b, v_ln1_b),
                    ffn_conv_b=(ffn_conv_b, m_ffn_conv_b, v_ffn_conv_b),
                    ln2_g=(ln2_g, m_ln2_g, v_ln2_g), ln2_b=(ln2_b, m_ln2_b, v_ln2_b))

    def flat(name, arr):
        rows = dict((n, r) for n, r, _, _ in SMALL_LAYOUT)[name]
        return arr.reshape(rows, arr.size // rows)

    names = [n for n, _, _, _ in SMALL_LAYOUT]
    sw = [flat(n, small_in[n][0]) for n in names]
    sm = [flat(n, small_in[n][1]) for n in names]
    sv = [flat(n, small_in[n][2]) for n in names]
    loss_rows, dcw_tot, lg_out, small_out = _small_allreduce_adamw(
        [dgv, dbv, dlb, dgn, dg1, db1, dg2, db2, loss_acc], dws.reshape(N_GROUP * 128, 128), dbs, dcw, dcb,
        hgrn_lb_logits, m_hgrn_lb_logits, v_hgrn_lb_logits, sw, sm, sv)
    loss = loss_rows[0, 0]

    chip = 2 * lax.axis_index("x") + lax.axis_index("y")
    g_cw = lax.dynamic_slice(dcw_tot, (0, chip * (D_FF // 4)), (3, D_FF // 4))
    cw_out = _adamw_whole("adamw_conv_w", g_cw, ffn_conv_w[0], m_ffn_conv_w[0], v_ffn_conv_w[0])

    res = {}
    for si, n in enumerate(names):
        shp = small_in[n][0].shape
        res[n] = tuple(small_out[4 * si + k].reshape(shp) for k in range(4))
    res["hgrn_lb_logits"] = tuple(lg_out)
    res["ffn_conv_w"] = (g_cw[None],) + tuple(o[None] for o in cw_out)

    def big(i):
        return tuple(big_out[i])

    res["w_in"] = tuple(o[None] for o in big(0))
    res["w_branch"] = tuple(jnp.stack([o0, o1])[None] for o0, o1 in zip(big(1), big(2)))
    res["w_out"] = tuple(o[None] for o in big(3))
    res["ffn_w_up"] = tuple(o[None] for o in big(4))
    res["ffn_w_down"] = tuple(o[None] for o in big(5))
    res["ple_w_proj"] = tuple(o[None] for o in big(6))
    res["ple_w_gate"] = tuple(o[None] for o in big(7))

    order = ["w_in", "sgu_w_s", "sgu_b_s", "sgu_norm_g", "sgu_norm_b", "hgrn_lb_logits",
             "hgrn_norm_g", "w_branch", "w_out", "ln1_g", "ln1_b", "ffn_w_up", "ffn_conv_w",
             "ffn_conv_b", "ffn_w_down", "ln2_g", "ln2_b", "ple_w_proj", "ple_w_gate"]
    outs = [loss, gx.reshape(1, t, D_MODEL)]
    for k in range(4):
        outs += [res[n][k] for n in order]
    return tuple(outs)
```

```python
import jax
import jax.numpy as jnp
from jax import lax
from jax.experimental import pallas as pl
from jax.experimental.pallas import tpu as pltpu

F32 = jnp.float32
BF16 = jnp.bfloat16
HIGHEST = lax.Precision.HIGHEST
MESH = pl.DeviceIdType.MESH

D_MODEL = 1024
CHUNK = 64
SGU_BLOCK = 128
SGU_STEP_BLOCKS = 4
SGU_ROWS = SGU_STEP_BLOCKS * SGU_BLOCK
N_GROUP = 8
N_HEAD = 8
HEAD_DIM = 128
D_FF = 2816
PLE_DIM = 256
LN_EPS = 1e-5
RMS_EPS = 1e-6
ALPHA = 2.0 ** 0.25
N_CHIP = 4

ADAM_LR = 0.001
ADAM_B1 = 0.9
ADAM_B2 = 0.999
ADAM_EPS = 1e-08
ADAM_WD = 0.01
ADAM_STEP = 10

VMEM_LIMIT = 56 * 1024 * 1024

NN = (((1,), (0,)), ((), ()))
NT = (((1,), (1,)), ((), ()))
TN = (((0,), (0,)), ((), ()))


def _pc(body, *, name, out_shape, grid=None, in_specs=None, out_specs=None, scratch=(),
        sem=None, nsp=0, vmem=VMEM_LIMIT, aliases=None):
    params = dict(vmem_limit_bytes=vmem)
    if sem is not None:
        params["dimension_semantics"] = sem
    kw = dict(name=name, out_shape=out_shape, compiler_params=pltpu.CompilerParams(**params))
    if aliases:
        kw["input_output_aliases"] = aliases
    if nsp:
        kw["grid_spec"] = pltpu.PrefetchScalarGridSpec(
            num_scalar_prefetch=nsp, grid=grid, in_specs=in_specs, out_specs=out_specs,
            scratch_shapes=list(scratch))
    else:
        if grid is not None:
            kw["grid"] = grid
        if in_specs is not None:
            kw["in_specs"] = in_specs
            kw["out_specs"] = out_specs
        kw["scratch_shapes"] = list(scratch)
    return pl.pallas_call(body, **kw)


def _dot(a, b, dims=NN):
    return lax.dot_general(a.astype(BF16), b.astype(BF16), dims, preferred_element_type=F32)


def _dot32(a, b, dims=NN):
    return lax.dot_general(a, b, dims, precision=HIGHEST, preferred_element_type=F32)


def _sig(x):
    return 1.0 / (1.0 + jnp.exp(-x))


_GC = 0.7978845608028654
_GA = 0.044715


def _gelu(x):
    return 0.5 * x * (1.0 + jnp.tanh(_GC * (x + _GA * x * x * x)))


def _gelu_and_grad(x):
    t = jnp.tanh(_GC * (x + _GA * x * x * x))
    g = 0.5 * x * (1.0 + t)
    dg = 0.5 * (1.0 + t) + 0.5 * x * (1.0 - t * t) * _GC * (1.0 + 3.0 * _GA * x * x)
    return g, dg


def _ln_stats(r):
    mu = jnp.mean(r, axis=-1, keepdims=True)
    xc = r - mu
    var = jnp.mean(xc * xc, axis=-1, keepdims=True)
    rstd = lax.rsqrt(var + LN_EPS)
    return xc * rstd, rstd


def _ln_bwd(dxh, xh, rstd):
    m1 = jnp.mean(dxh, axis=-1, keepdims=True)
    m2 = jnp.mean(dxh * xh, axis=-1, keepdims=True)
    return rstd * (dxh - m1 - xh * m2)


def _colsum8(v):
    return jnp.broadcast_to(jnp.sum(v, axis=0, keepdims=True), (8, v.shape[1]))


def _adamw(w, g, m, v):
    m2 = ADAM_B1 * m + (1.0 - ADAM_B1) * g
    v2 = ADAM_B2 * v + (1.0 - ADAM_B2) * (g * g)
    m_hat = m2 / (1.0 - ADAM_B1 ** ADAM_STEP)
    v_hat = v2 / (1.0 - ADAM_B2 ** ADAM_STEP)
    delta = -ADAM_LR * (m_hat / (jnp.sqrt(v_hat) + ADAM_EPS) + ADAM_WD * w)
    return delta, m2, v2


def _row_tile(rows, cols, itemsize=4, budget=1 << 20, mult=8):
    best = mult
    for tr in range(mult, rows + 1, mult):
        if rows % tr == 0 and tr * cols * itemsize <= budget:
            best = tr
    return best


def _mm(name, a, b, dims, grid, a_spec, b_spec, out_shape, o_spec):
    out_dtype = out_shape.dtype

    def body(a_ref, b_ref, o_ref):
        o_ref[...] = _dot(a_ref[...], b_ref[...], dims).astype(out_dtype)

    return _pc(body, name=name, out_shape=out_shape, grid=grid, in_specs=[a_spec, b_spec],
               out_specs=o_spec, sem=("parallel", "parallel"))(a, b)


class _Comm:
    def __init__(self, ins, out_shapes, sems, start, finish, middle=None, finish_late=None, aliases=None):
        self.ins, self.out_shapes, self.sems = list(ins), list(out_shapes), list(sems)
        self.start, self.finish = start, finish
        self.middle, self.finish_late = middle, finish_late
        self.aliases = aliases or {}


def _hosted_call(body, comm, first, last, *, name, out_shape, grid, in_specs, out_specs, scratch, sem,
                 args, aliases=None, mid=None):
    n_in, n_out, n_scr = len(in_specs), len(out_shape), len(scratch)
    nci, nco = len(comm.ins), len(comm.out_shapes)

    def wrapped(*refs):
        pos = n_in
        own_in, c_in = refs[:pos], refs[pos:pos + nci]
        pos += nci
        own_out, c_out = refs[pos:pos + n_out], refs[pos + n_out:pos + n_out + nco]
        pos += n_out + nco
        own_scr, c_sem = refs[pos:pos + n_scr], refs[pos + n_scr:]

        @pl.when(first())
        def _():
            comm.start(c_in, c_out, c_sem)

        body(*own_in, *own_out, *own_scr)

        if mid is not None:
            @pl.when(mid())
            def _():
                comm.middle(c_in, c_out, c_sem)

        @pl.when(last())
        def _():
            (comm.finish if mid is None else comm.finish_late)(c_in, c_out, c_sem)

    return _pc(wrapped, name=name, out_shape=tuple(out_shape) + tuple(comm.out_shapes), grid=grid,
               in_specs=list(in_specs) + [ANY] * nci, out_specs=tuple(out_specs) + tuple([ANY] * nco),
               scratch=list(scratch) + comm.sems, sem=sem,
               aliases={**(aliases or {}), **{n_in + ci: n_out + co for ci, co in comm.aliases.items()}},
               )(*args, *comm.ins)


def _grid1_call(body, comm, n, *, name, out_shape, in_specs, out_specs, scratch, args, aliases=None):
    if comm is None:
        return _pc(body, name=name, out_shape=out_shape, grid=(n,), in_specs=in_specs, out_specs=out_specs,
                   scratch=scratch, sem=("arbitrary",), aliases=aliases)(*args), ()
    res = _hosted_call(body, comm, lambda: pl.program_id(0) == 0, lambda: pl.program_id(0) == n - 1,
                       name=name, out_shape=out_shape, grid=(n,), in_specs=in_specs,
                       out_specs=out_specs, scratch=scratch, sem=("arbitrary",), args=args, aliases=aliases)
    return res[:len(out_shape)], res[len(out_shape):]


def _run_comm(name, comm):
    nci, nco = len(comm.ins), len(comm.out_shapes)

    def body(*refs):
        c_in, c_out, c_sem = refs[:nci], refs[nci:nci + nco], refs[nci + nco:]
        comm.start(c_in, c_out, c_sem)
        comm.finish(c_in, c_out, c_sem)

    return _pc(body, name=name, out_shape=tuple(comm.out_shapes), in_specs=[ANY] * nci,
               out_specs=tuple([ANY] * nco), scratch=comm.sems)(*comm.ins)


def _mm_tn(name, a, b, tm, tn, stacked=False):
    t, m = a.shape
    _, n = b.shape
    if stacked:
        assert tm == m
        out_shape = jax.ShapeDtypeStruct((n // tn, m, tn), BF16)
        o_spec = pl.BlockSpec((None, tm, tn), lambda i, j: (j, 0, 0))
    else:
        out_shape = jax.ShapeDtypeStruct((m, n), BF16)
        o_spec = pl.BlockSpec((tm, tn), lambda i, j: (i, j))
    return _mm(name, a, b, TN, (m // tm, n // tn),
               pl.BlockSpec((t, tm), lambda i, j: (0, i)),
               pl.BlockSpec((t, tn), lambda i, j: (0, j)),
               out_shape, o_spec)


DH_SLOT = (2, 0, 1, 3)


def _dh_slot(j):
    return jnp.where(j == 3, 3, (j + 2) % 3)


def _in_proj_wgrad(x2, dh, tm, tn):
    t, m = x2.shape
    n = dh.shape[2]

    def body(a_ref, b_ref, o_ref):
        o_ref[...] = _dot(a_ref[...], b_ref[...], TN).astype(BF16)

    return _pc(body, name="in_proj_wgrad", out_shape=jax.ShapeDtypeStruct((N_CHIP, m, n), BF16),
               grid=(N_CHIP, m // tm, n // tn),
               in_specs=[pl.BlockSpec((t, tm), lambda j, i, k: (0, i)),
                         pl.BlockSpec((None, t, tn), lambda j, i, k: (_dh_slot(j), 0, k))],
               out_specs=pl.BlockSpec((None, tm, tn), lambda j, i, k: (j, i, k)),
               sem=("parallel", "parallel", "parallel"))(x2, dh)


def _in_proj_xgrad(dh, win_st, dr1, tm, comm):
    t = dr1.shape[0]
    ni = t // tm

    def body(a_ref, b_ref, add_ref, o_ref, acc):
        j = pl.program_id(1)
        prod = _dot(a_ref[...], b_ref[...], NT)

        @pl.when(j == 0)
        def _():
            acc[...] = prod + ALPHA * add_ref[...]

        @pl.when((j > 0) & (j < N_CHIP - 1))
        def _():
            acc[...] += prod

        @pl.when(j == N_CHIP - 1)
        def _():
            o_ref[...] = acc[...] + prod

    tile = pl.BlockSpec((tm, D_MODEL), lambda i, j: (i, 0))
    res = _hosted_call(body, comm,
                       lambda: (pl.program_id(0) == 0) & (pl.program_id(1) == 0),
                       lambda: (pl.program_id(0) == ni - 1) & (pl.program_id(1) == N_CHIP - 1),
                       name="in_proj_xgrad", out_shape=(jax.ShapeDtypeStruct((t, D_MODEL), F32),),
                       grid=(ni, N_CHIP),
                       in_specs=[pl.BlockSpec((None, tm, 2 * D_MODEL), lambda i, j: (_dh_slot(j), i, 0)),
                                 pl.BlockSpec((None, D_MODEL, 2 * D_MODEL), lambda i, j: (j, 0, 0)),
                                 tile],
                       out_specs=(tile,), scratch=[pltpu.VMEM((tm, D_MODEL), F32)],
                       sem=("arbitrary", "arbitrary"), args=[dh, win_st, dr1])
    return res[0], res[1:]


def _sgu_mixed(v, wm_ref, bsb_ref, gv, bv):
    gl, dgl = _gelu_and_grad(v)
    vh, rstd = _ln_stats(gl)
    vn = vh * gv + bv
    mixed = []
    for g in range(N_GROUP):
        sl = slice(g * 128, (g + 1) * 128)
        mixed.append(_dot(wm_ref[g], vn[:, sl]) + bsb_ref[g])
    return dgl, vh, rstd, vn, mixed


def _sgu_fwd(h, wm, bsb, gv, bv, comm=None):
    t = h.shape[0]

    def body(u_ref, v_ref, wm_ref, bsb_ref, gv_ref, bv_ref, ya_ref):
        for bb in range(SGU_STEP_BLOCKS):
            rows = slice(bb * SGU_BLOCK, (bb + 1) * SGU_BLOCK)
            u = u_ref[rows, :].astype(F32)
            _, _, _, _, mixed = _sgu_mixed(v_ref[rows, :].astype(F32), wm_ref, bsb_ref, gv_ref[...],
                                           bv_ref[...])
            gu = _gelu(u)
            for g in range(N_GROUP):
                sl = slice(g * 128, (g + 1) * 128)
                ya_ref[rows, sl] = (gu[:, sl] * mixed[g]).astype(BF16)

    full3 = pl.BlockSpec((N_GROUP, 128, 128), lambda i: (0, 0, 0))
    vec = pl.BlockSpec((1, D_MODEL), lambda i: (0, 0))
    (ya,), extra = _grid1_call(
        body, comm, t // SGU_ROWS, name="sgu_fwd",
        out_shape=(jax.ShapeDtypeStruct((t, D_MODEL), BF16),),
        in_specs=[pl.BlockSpec((SGU_ROWS, D_MODEL), lambda i: (i, 0)),
                  pl.BlockSpec((SGU_ROWS, D_MODEL), lambda i: (i, 1)),
                  full3, full3, vec, vec],
        out_specs=(pl.BlockSpec((SGU_ROWS, D_MODEL), lambda i: (i, 0)),),
        scratch=[], args=(h, h, wm, bsb, gv, bv))
    return ya, extra


def _sgu_bwd(h, dya, wm, wmt, bsb, gv, bv, maskf, dh_buf, comm=None):
    t = h.shape[0]
    nb = t // SGU_ROWS

    def body(u_ref, v_ref, dya_ref, wm_ref, wmt_ref, bsb_ref, gv_ref, bv_ref, mask_ref, dh_buf_ref,
             dh_ref, dws_ref, dbs_ref, dgv_ref, dbv_ref, dmix_acc):
        i = pl.program_id(0)

        @pl.when(i == 0)
        def _():
            dws_ref[...] = jnp.zeros_like(dws_ref)
            dgv_ref[...] = jnp.zeros_like(dgv_ref)
            dbv_ref[...] = jnp.zeros_like(dbv_ref)
            dmix_acc[...] = jnp.zeros_like(dmix_acc)

        gvv = gv_ref[...]
        for bb in range(SGU_STEP_BLOCKS):
            rows = slice(bb * SGU_BLOCK, (bb + 1) * SGU_BLOCK)
            u = u_ref[rows, :].astype(F32)
            dgl_v, vh, rstd, vn, mixed = _sgu_mixed(v_ref[rows, :].astype(F32), wm_ref, bsb_ref, gvv,
                                                    bv_ref[...])
            gu, dgl_u = _gelu_and_grad(u)
            dya_v = dya_ref[rows, :].astype(F32)
            dvn_parts = []
            for g in range(N_GROUP):
                sl = slice(g * 128, (g + 1) * 128)
                d_y = dya_v[:, sl]
                dh_ref[rows, sl] = (d_y * mixed[g] * dgl_u[:, sl]).astype(BF16)
                d_mixed = d_y * gu[:, sl]
                dmix_acc[g] += d_mixed
                dws_ref[g] += _dot(d_mixed, vn[:, sl], NT) * mask_ref[...]
                dvn_parts.append(_dot(wmt_ref[g], d_mixed))
            dvn = jnp.concatenate(dvn_parts, axis=1)
            dgv_ref[...] += _colsum8(dvn * vh)
            dbv_ref[...] += _colsum8(dvn)
            d_gl = _ln_bwd(dvn * gvv, vh, rstd)
            dh_ref[rows, D_MODEL:] = (d_gl * dgl_v).astype(BF16)

        @pl.when(i == nb - 1)
        def _():
            rowid = lax.broadcasted_iota(jnp.int32, (8, 128), 0)
            ones = jnp.ones((8, 128), F32)
            acc = jnp.zeros((8, 128), F32)
            for g in range(N_GROUP):
                rs = _dot32(ones, dmix_acc[g], NT)
                acc = jnp.where(rowid == g, rs, acc)
            dbs_ref[...] = acc

    full3 = pl.BlockSpec((N_GROUP, 128, 128), lambda i: (0, 0, 0))
    vec = pl.BlockSpec((1, D_MODEL), lambda i: (0, 0))
    acc8 = pl.BlockSpec((8, D_MODEL), lambda i: (0, 0))
    return _grid1_call(
        body, comm, nb, name="sgu_bwd",
        out_shape=(jax.ShapeDtypeStruct(dh_buf.shape, BF16),
                   jax.ShapeDtypeStruct((N_GROUP, 128, 128), F32),
                   jax.ShapeDtypeStruct((8, 128), F32),
                   jax.ShapeDtypeStruct((8, D_MODEL), F32),
                   jax.ShapeDtypeStruct((8, D_MODEL), F32)),
        in_specs=[pl.BlockSpec((SGU_ROWS, D_MODEL), lambda i: (i, 0)),
                  pl.BlockSpec((SGU_ROWS, D_MODEL), lambda i: (i, 1)),
                  pl.BlockSpec((SGU_ROWS, D_MODEL), lambda i: (i, 0)),
                  full3, full3, full3, vec, vec,
                  pl.BlockSpec((128, 128), lambda i: (0, 0)), ANY],
        out_specs=(pl.BlockSpec((None, SGU_ROWS, 2 * D_MODEL), lambda i: (DH_SLOT[0], i, 0)),
                   full3, pl.BlockSpec((8, 128), lambda i: (0, 0)), acc8, acc8),
        scratch=[pltpu.VMEM((N_GROUP, 128, 128), F32)],
        args=(h, h, dya, wm, wmt, bsb, gv, bv, maskf, dh_buf), aliases={9: 0})


def _tri_masks():
    row = lax.broadcasted_iota(jnp.int32, (CHUNK, CHUNK), 0)
    col = lax.broadcasted_iota(jnp.int32, (CHUNK, CHUNK), 1)
    return col <= row, col >= row


def _heads(v):
    return [v[:, hd * HEAD_DIM:(hd + 1) * HEAD_DIM] for hd in range(N_HEAD)]


def _tri_cumsum(tri_bf, v):
    hi = v.astype(BF16)
    r = v - hi.astype(F32)
    mid = r.astype(BF16)
    lo = (r - mid.astype(F32)).astype(BF16)
    return _dot(tri_bf, hi) + _dot(tri_bf, mid) + _dot(tri_bf, lo)


def _hgrn_chunk(q, fp, ii, lb, st_heads, causal, with_o=True):
    sg = _sig(fp)
    f = lb + (1.0 - lb) * sg
    k = 1.0 - f
    c = _tri_cumsum(causal.astype(BF16), jnp.log(f))
    ec = jnp.exp(c)
    en = jnp.exp(-c)
    sq = _sig(q)
    qt = q * sq * ec
    kt = k * en
    ecl = jnp.exp(c[CHUNK - 1:CHUNK, :])
    kk = kt * ecl
    qtb, ktb, iib, kkb = qt.astype(BF16), kt.astype(BF16), ii.astype(BF16), kk.astype(BF16)
    attn, o = [], []
    for hd, (qh, kh, ih) in enumerate(zip(_heads(qtb), _heads(ktb), _heads(iib))):
        a = jnp.where(causal, _dot(qh, kh, NT), 0.0).astype(BF16)
        attn.append(a)
        if with_o:
            o.append(_dot(a, ih) + _dot(qh, st_heads[hd], NT))
    return dict(sg=sg, f=f, k=k, ec=ec, en=en, sq=sq, ecl=ecl, kk=kk, qtb=qtb, ktb=ktb, iib=iib,
                kkb=kkb, attn=attn, o=o)


def _rms_heads(o_heads):
    rinv = [lax.rsqrt(jnp.mean(o * o, axis=-1, keepdims=True) + RMS_EPS) for o in o_heads]
    return rinv, jnp.concatenate([o * r for o, r in zip(o_heads, rinv)], axis=1)


HG_CHUNKS = 8
HG_ROWS = HG_CHUNKS * CHUNK


def _hgrn_fwd(h, logits, gn, comm=None):
    t = h.shape[0]
    nb = t // HG_ROWS

    def body(q_ref, f_ref, i_ref, og_ref, lg_ref, gn_ref, yb_ref, o_ref, st_ref, state):
        @pl.when(pl.program_id(0) == 0)
        def _():
            state[...] = jnp.zeros_like(state)

        causal, _ = _tri_masks()
        lb = _sig(lg_ref[0:1, :] - lg_ref[1:2, :])
        gnv = gn_ref[...]
        st = [state[hd] for hd in range(N_HEAD)]
        for cc in range(HG_CHUNKS):
            rows = slice(cc * CHUNK, (cc + 1) * CHUNK)
            og = og_ref[rows, :].astype(F32)
            r = _hgrn_chunk(q_ref[rows, :].astype(F32), f_ref[rows, :].astype(F32),
                            i_ref[rows, :].astype(F32), lb, [s.astype(BF16) for s in st], causal)
            o_bf = jnp.concatenate(r["o"], axis=1).astype(BF16)
            o_ref[rows, :] = o_bf
            _, on = _rms_heads(_heads(o_bf.astype(F32)))
            yb_ref[rows, :] = (on * gnv * (og * _sig(og))).astype(BF16)
            for hd in range(N_HEAD):
                st_ref[cc, hd] = st[hd]
            st = [s * e + _dot(ih, kh, TN)
                  for s, e, ih, kh in zip(st, _heads(r["ecl"]), _heads(r["iib"]), _heads(r["kkb"]))]
        for hd in range(N_HEAD):
            state[hd] = st[hd]

    def col(k):
        return pl.BlockSpec((HG_ROWS, D_MODEL), lambda ci: (ci, k))

    return _grid1_call(body, comm, nb, name="hgrn_fwd",
                       out_shape=(jax.ShapeDtypeStruct((t, D_MODEL), BF16),
                                  jax.ShapeDtypeStruct((t, D_MODEL), BF16),
                                  jax.ShapeDtypeStruct((t // CHUNK, N_HEAD, HEAD_DIM, HEAD_DIM), F32)),
                       in_specs=[col(2), col(3), col(4), col(5),
                                 pl.BlockSpec((2, D_MODEL), lambda ci: (0, 0)),
                                 pl.BlockSpec((1, D_MODEL), lambda ci: (0, 0))],
                       out_specs=(pl.BlockSpec((HG_ROWS, D_MODEL), lambda ci: (ci, 0)),
                                  pl.BlockSpec((HG_ROWS, D_MODEL), lambda ci: (ci, 0)),
                                  pl.BlockSpec((HG_CHUNKS, N_HEAD, HEAD_DIM, HEAD_DIM),
                                               lambda ci: (ci, 0, 0, 0))),
                       scratch=[pltpu.VMEM((N_HEAD, HEAD_DIM, HEAD_DIM), F32)],
                       args=(h, h, h, h, logits, gn))


def _hgrn_chunk_bwd(q, fp, ii, og, o_saved, dy, gnv, lb, st, dsn, causal, anti):
    stb = [s.astype(BF16) for s in st]
    dsnb = [s.astype(BF16) for s in dsn]
    r = _hgrn_chunk(q, fp, ii, lb, stb, causal, with_o=False)
    rinv, on = _rms_heads(_heads(o_saved))
    so = _sig(og)
    sil = og * so
    d_og = dy * on * gnv * (so * (1.0 + og * (1.0 - so)))
    d_on = dy * gnv * sil
    d_ob = jnp.concatenate(
        [ri * (dn - oh * jnp.mean(dn * oh, axis=-1, keepdims=True))
         for ri, dn, oh in zip(rinv, _heads(d_on), _heads(on))], axis=1).astype(BF16)
    d_i, d_qt, d_kt, d_kk, d_st, st_dsn = [], [], [], [], [], []
    ecl = _heads(r["ecl"])
    for hd, (dh, qh, kh, ih, kkh) in enumerate(zip(_heads(d_ob), _heads(r["qtb"]), _heads(r["ktb"]),
                                                   _heads(r["iib"]), _heads(r["kkb"]))):
        d_attn = jnp.where(causal, _dot(dh, ih, NT), 0.0).astype(BF16)
        d_i.append(_dot(r["attn"][hd], dh, TN) + _dot(kkh, dsnb[hd], NT))
        d_qt.append(_dot(d_attn, kh) + _dot(dh, stb[hd]))
        d_kt.append(_dot(d_attn, qh, TN))
        d_kk.append(_dot(ih, dsnb[hd]))
        d_st.append(_dot(dh, qh, TN) + dsn[hd] * ecl[hd])
        st_dsn.append(jnp.sum(st[hd] * dsn[hd], axis=0, keepdims=True))
    d_qt = jnp.concatenate(d_qt, axis=1)
    d_kt = jnp.concatenate(d_kt, axis=1)
    d_kk = jnp.concatenate(d_kk, axis=1)
    kk = r["kk"]
    d_cl = r["ecl"] * jnp.concatenate(st_dsn, axis=1) + jnp.sum(kk * d_kk, axis=0, keepdims=True)
    d_k = (d_kk * r["ecl"] + d_kt) * r["en"]
    d_c = d_qt * r["qtb"].astype(F32) - d_kt * r["ktb"].astype(F32) - d_kk * kk
    rowid = lax.broadcasted_iota(jnp.int32, (CHUNK, D_MODEL), 0)
    d_c = d_c + jnp.where(rowid == CHUNK - 1, d_cl, 0.0)
    d_lf = _tri_cumsum(anti.astype(BF16), d_c)
    d_f = d_lf / r["f"] - d_k
    sg, sq = r["sg"], r["sq"]
    d_q = d_qt * r["ec"] * (sq * (1.0 + q * (1.0 - sq)))
    d_fp = d_f * (1.0 - lb) * sg * (1.0 - sg)
    return (d_q, d_fp, jnp.concatenate(d_i, axis=1), d_og, d_st,
            _colsum8(dy * on * sil), _colsum8(d_f * (1.0 - sg)))


def _hgrn_bwd(h, o_all, dyb, st_all, logits, gn, dh_buf, comm=None):
    t = h.shape[0]
    nb = t // HG_ROWS

    def body(q_ref, f_ref, i_ref, og_ref, o_ref, dyb_ref, st_ref, lg_ref, gn_ref, dh_buf_ref,
             dh_ref, dlb_ref, dgn_ref, dstate):
        @pl.when(pl.program_id(0) == 0)
        def _():
            dstate[...] = jnp.zeros_like(dstate)
            dlb_ref[...] = jnp.zeros_like(dlb_ref)
            dgn_ref[...] = jnp.zeros_like(dgn_ref)

        causal, anti = _tri_masks()
        lb = _sig(lg_ref[0:1, :] - lg_ref[1:2, :])
        gnv = gn_ref[...]
        dsn = [dstate[hd] for hd in range(N_HEAD)]
        dgn_acc = jnp.zeros((8, D_MODEL), F32)
        dlb_acc = jnp.zeros((8, D_MODEL), F32)
        for cc in reversed(range(HG_CHUNKS)):
            rows = slice(cc * CHUNK, (cc + 1) * CHUNK)
            d_q, d_fp, d_i, d_og, dsn, dgn_c, dlb_c = _hgrn_chunk_bwd(
                q_ref[rows, :].astype(F32), f_ref[rows, :].astype(F32), i_ref[rows, :].astype(F32),
                og_ref[rows, :].astype(F32), o_ref[rows, :].astype(F32), dyb_ref[rows, :].astype(F32), gnv, lb,
                [st_ref[cc, hd] for hd in range(N_HEAD)], dsn, causal, anti)
            dgn_acc = dgn_acc + dgn_c
            dlb_acc = dlb_acc + dlb_c
            dh_ref[0, rows, :D_MODEL] = d_q.astype(BF16)
            dh_ref[0, rows, D_MODEL:] = d_fp.astype(BF16)
            dh_ref[1, rows, :D_MODEL] = d_i.astype(BF16)
            dh_ref[1, rows, D_MODEL:] = d_og.astype(BF16)
        dgn_ref[...] += dgn_acc
        dlb_ref[...] += dlb_acc
        for hd in range(N_HEAD):
            dstate[hd] = dsn[hd]

    def col(k):
        return pl.BlockSpec((HG_ROWS, D_MODEL), lambda ci: (nb - 1 - ci, k))

    acc8 = pl.BlockSpec((8, D_MODEL), lambda ci: (0, 0))
    pair = pl.BlockSpec((2, HG_ROWS, 2 * D_MODEL), lambda ci: (0, nb - 1 - ci, 0))
    return _grid1_call(body, comm, nb, name="hgrn_bwd",
                       out_shape=(jax.ShapeDtypeStruct(dh_buf.shape, BF16),
                                  jax.ShapeDtypeStruct((8, D_MODEL), F32),
                                  jax.ShapeDtypeStruct((8, D_MODEL), F32)),
                       in_specs=[col(2), col(3), col(4), col(5), col(0),
                                 pl.BlockSpec((HG_ROWS, D_MODEL), lambda ci: (nb - 1 - ci, 0)),
                                 pl.BlockSpec((HG_CHUNKS, N_HEAD, HEAD_DIM, HEAD_DIM),
                                              lambda ci: (nb - 1 - ci, 0, 0, 0)),
                                 pl.BlockSpec((2, D_MODEL), lambda ci: (0, 0)),
                                 pl.BlockSpec((1, D_MODEL), lambda ci: (0, 0)), ANY],
                       out_specs=(pair, acc8, acc8),
                       scratch=[pltpu.VMEM((N_HEAD, HEAD_DIM, HEAD_DIM), F32)],
                       args=(h, h, h, h, o_all, dyb, st_all, logits, gn, dh_buf), aliases={9: 0})


def _mix_fwd(ya, yb, h, x, wb0, wb1, wo, g1, b1, tm, comm=None):
    t = x.shape[0]

    def body(ya_ref, yb_ref, ga_ref, gb_ref, x_ref, wb0_ref, wb1_ref, wo_ref, g1_ref, b1_ref,
             r1_ref, a_ref, b_ref, m_ref, x1_ref):
        a = _dot(ya_ref[...], wb0_ref[...])
        b = _dot(yb_ref[...], wb1_ref[...])
        m = _sig(ga_ref[...].astype(F32)) * a + _sig(gb_ref[...].astype(F32)) * b
        r1 = ALPHA * x_ref[...] + _dot(m, wo_ref[...])
        xh, _ = _ln_stats(r1)
        r1_ref[...] = r1
        a_ref[...] = a.astype(BF16)
        b_ref[...] = b.astype(BF16)
        m_ref[...] = m.astype(BF16)
        x1_ref[...] = (xh * g1_ref[...] + b1_ref[...]).astype(BF16)

    tile = pl.BlockSpec((tm, D_MODEL), lambda i: (i, 0))
    wsp = pl.BlockSpec((D_MODEL, D_MODEL), lambda i: (0, 0))
    vec = pl.BlockSpec((1, D_MODEL), lambda i: (0, 0))
    f32o = jax.ShapeDtypeStruct((t, D_MODEL), F32)
    bfo = jax.ShapeDtypeStruct((t, D_MODEL), BF16)
    return _grid1_call(body, comm, t // tm, name="mix_fwd", out_shape=(f32o, bfo, bfo, bfo, bfo),
                       in_specs=[tile, tile,
                                 pl.BlockSpec((tm, D_MODEL), lambda i: (i, 6)),
                                 pl.BlockSpec((tm, D_MODEL), lambda i: (i, 7)),
                                 tile, wsp, wsp, wsp, vec, vec],
                       out_specs=(tile, tile, tile, tile, tile),
                       scratch=[], args=(ya, yb, h, h, x, wb0, wb1, wo, g1, b1))


def _mix_bwd(dr1, h, a, b, wo, wb0, wb1, tm):
    t = dr1.shape[0]

    def body(dr1_ref, ga_ref, gb_ref, a_ref, b_ref, wo_ref, wb0_ref, wb1_ref,
             da_ref, db_ref, dh3_ref, dya_ref, dyb_ref):
        d_m = _dot(dr1_ref[...], wo_ref[...], NT)
        sa = _sig(ga_ref[...].astype(F32))
        sb = _sig(gb_ref[...].astype(F32))
        d_a = (d_m * sa).astype(BF16)
        d_b = (d_m * sb).astype(BF16)
        da_ref[...] = d_a
        db_ref[...] = d_b
        dh3_ref[:, :D_MODEL] = (d_m * a_ref[...].astype(F32) * sa * (1.0 - sa)).astype(BF16)
        dh3_ref[:, D_MODEL:] = (d_m * b_ref[...].astype(F32) * sb * (1.0 - sb)).astype(BF16)
        dya_ref[...] = _dot(d_a, wb0_ref[...], NT).astype(BF16)
        dyb_ref[...] = _dot(d_b, wb1_ref[...], NT).astype(BF16)

    tile = pl.BlockSpec((tm, D_MODEL), lambda i: (i, 0))
    wsp = pl.BlockSpec((D_MODEL, D_MODEL), lambda i: (0, 0))
    f32o = jax.ShapeDtypeStruct((t, D_MODEL), F32)
    bfo = jax.ShapeDtypeStruct((t, D_MODEL), BF16)
    return _pc(body, name="mix_bwd",
               out_shape=(bfo, bfo, jax.ShapeDtypeStruct((N_CHIP, t, 2 * D_MODEL), BF16), bfo, bfo),
               grid=(t // tm,),
               in_specs=[tile,
                         pl.BlockSpec((tm, D_MODEL), lambda i: (i, 6)),
                         pl.BlockSpec((tm, D_MODEL), lambda i: (i, 7)),
                         tile, tile, wsp, wsp, wsp],
               out_specs=(tile, tile, pl.BlockSpec((None, tm, 2 * D_MODEL), lambda i: (DH_SLOT[3], i, 0)),
                          tile, tile),
               sem=("parallel",))(dr1, h, h, a, b, wo, wb0, wb1)


FF_TILE = 1408
FF_NJ = D_FF // FF_TILE


def _shift_down(v, k):
    return pltpu.roll(v, k, 0)


def _shift_up(v, k):
    return pltpu.roll(v, v.shape[0] - k, 0)


HALO = 16
FF_PIECES = ((0, 768), (768, FF_TILE))


def _ffn_up_act(x1b, wup_st, convw, convb, tm, comm):
    t = x1b.shape[0]
    ni = t // tm
    nth = tm // HALO

    def body(x_ref, xp_ref, wg_ref, wv_ref, cw_ref, cb_ref, h2_ref, act_ref):
        wg = wg_ref[...]
        gate = _dot(x_ref[...], wg).astype(BF16)
        val = _dot(x_ref[...], wv_ref[...]).astype(BF16)
        prev = (_dot(xp_ref[...], wg) * (pl.program_id(0) > 0).astype(F32)).astype(BF16)
        h2_ref[0] = gate
        h2_ref[1] = val
        ext = jnp.concatenate([prev.astype(F32), gate.astype(F32)], axis=0)
        gc = (cw_ref[0:1, :] * _shift_down(ext, 2) + cw_ref[1:2, :] * _shift_down(ext, 1)
              + cw_ref[2:3, :] * ext + cb_ref[...])[HALO:, :].astype(BF16)
        h2_ref[2] = gc
        act_ref[...] = (_gelu(gc.astype(F32)) * val.astype(F32)).astype(BF16)

    res = _hosted_call(
        body, comm,
        lambda: (pl.program_id(0) == 0) & (pl.program_id(1) == 0),
        lambda: (pl.program_id(0) == ni - 1) & (pl.program_id(1) == FF_NJ - 1),
        mid=lambda: (pl.program_id(0) == max(ni - 2, 0)) & (pl.program_id(1) == 0),
        name="ffn_up",
        out_shape=(jax.ShapeDtypeStruct((3, t, D_FF), BF16), jax.ShapeDtypeStruct((t, D_FF), BF16)),
        grid=(ni, FF_NJ),
        in_specs=[pl.BlockSpec((tm, D_MODEL), lambda i, j: (i, 0)),
                  pl.BlockSpec((HALO, D_MODEL), lambda i, j: (jnp.maximum(i * nth - 1, 0), 0)),
                  pl.BlockSpec((None, D_MODEL, FF_TILE), lambda i, j: (j, 0, 0)),
                  pl.BlockSpec((None, D_MODEL, FF_TILE), lambda i, j: (j + FF_NJ, 0, 0)),
                  pl.BlockSpec((3, FF_TILE), lambda i, j: (0, j)),
                  pl.BlockSpec((1, FF_TILE), lambda i, j: (0, j))],
        out_specs=(pl.BlockSpec((3, tm, FF_TILE), lambda i, j: (0, i, j)),
                   pl.BlockSpec((tm, FF_TILE), lambda i, j: (i, j))),
        scratch=[], sem=("arbitrary", "arbitrary"),
        args=(x1b, x1b, wup_st, wup_st, convw, convb))
    return res[0], res[1], res[2:]


def _out_fwd_bwd(act, x1b, r1, p2, tgt, wd, wpg, wpp, g1, b1, g2, b2, tm):
    t = r1.shape[0]

    def body(act_ref, x1b_ref, r1_ref, p_ref, tgt_ref, wd_ref, wpg_ref, wpp_ref,
             g1_ref, b1_ref, g2_ref, b2_ref,
             dr2_ref, dpg_ref, dpp_ref, loss_ref, dg2_ref, db2_ref):
        i = pl.program_id(0)

        @pl.when(i == 0)
        def _():
            loss_ref[...] = jnp.zeros_like(loss_ref)
            dg2_ref[...] = jnp.zeros_like(dg2_ref)
            db2_ref[...] = jnp.zeros_like(db2_ref)

        ffn = _dot(act_ref[...], wd_ref[...])
        pg = _dot(x1b_ref[...], wpg_ref[...])
        pp = _dot(p_ref[...], wpp_ref[...])
        s = _sig(pg)
        xh1, _ = _ln_stats(r1_ref[...])
        x1 = xh1 * g1_ref[...] + b1_ref[...]
        r2 = ALPHA * x1 + ffn + s * pp
        xh2, rstd2 = _ln_stats(r2)
        g2v = g2_ref[...]
        diff = xh2 * g2v + b2_ref[...] - tgt_ref[...]
        part = jnp.sum(jnp.sum(diff * diff, axis=1, keepdims=True), axis=0, keepdims=True)
        loss_ref[...] += jnp.broadcast_to(part * (0.5 / D_MODEL), loss_ref.shape)
        dy = diff * (1.0 / D_MODEL)
        dg2_ref[...] += _colsum8(dy * xh2)
        db2_ref[...] += _colsum8(dy)
        dr2 = _ln_bwd(dy * g2v, xh2, rstd2)
        dr2_ref[...] = dr2
        dpg_ref[...] = (dr2 * pp * s * (1.0 - s)).astype(BF16)
        dpp_ref[...] = (dr2 * s).astype(BF16)

    tile = pl.BlockSpec((tm, D_MODEL), lambda i: (i, 0))
    vec = pl.BlockSpec((1, D_MODEL), lambda i: (0, 0))
    acc8 = pl.BlockSpec((8, D_MODEL), lambda i: (0, 0))
    acc_shape = jax.ShapeDtypeStruct((8, D_MODEL), F32)
    return _pc(body, name="out_fwd_bwd",
               out_shape=(jax.ShapeDtypeStruct((t, D_MODEL), F32),
                          jax.ShapeDtypeStruct((t, D_MODEL), BF16),
                          jax.ShapeDtypeStruct((t, D_MODEL), BF16),
                          acc_shape, acc_shape, acc_shape),
               grid=(t // tm,),
               in_specs=[pl.BlockSpec((tm, D_FF), lambda i: (i, 0)), tile, tile,
                         pl.BlockSpec((tm, PLE_DIM), lambda i: (i, 0)), tile,
                         pl.BlockSpec((D_FF, D_MODEL), lambda i: (0, 0)),
                         pl.BlockSpec((D_MODEL, D_MODEL), lambda i: (0, 0)),
                         pl.BlockSpec((PLE_DIM, D_MODEL), lambda i: (0, 0)),
                         vec, vec, vec, vec],
               out_specs=(tile, tile, tile, acc8, acc8, acc8),
               sem=("arbitrary",))(act, x1b, r1, p2, tgt, wd, wpg, wpp, g1, b1, g2, b2)


def _ffn_bwd(h2, dr2, wd, wup_st, dpg, wpg, r1, g1, convw, tm):
    t = r1.shape[0]
    ni = t // tm
    nth = tm // HALO
    last_halo = t // HALO - 1
    main_rows = slice(0, tm)

    def body(g_ref, gc_ref, gcn_ref, v_ref, vn_ref, dr2_ref, dr2n_ref, wd_ref, wug_ref, wuv_ref,
             cw_ref, dpg_ref, wpg_ref, r1_ref, g1_ref,
             dh2_ref, dr1_ref, dcw_ref, dcb_ref, dg1_ref, db1_ref, acc):
        i = pl.program_id(0)
        j = pl.program_id(1)

        @pl.when((i == 0) & (j == 0))
        def _():
            dcw_ref[...] = jnp.zeros_like(dcw_ref)
            dcb_ref[...] = jnp.zeros_like(dcb_ref)
            dg1_ref[...] = jnp.zeros_like(dg1_ref)
            db1_ref[...] = jnp.zeros_like(db1_ref)

        dr2v = dr2_ref[...].astype(BF16)
        dr2n = dr2n_ref[...].astype(BF16)
        more = (i < ni - 1).astype(F32)
        prod = None
        dcw_parts, dcb_parts = [], []
        for c0, c1 in FF_PIECES:
            pc = slice(c0, c1)
            da = _dot(dr2v, wd_ref[pc, :], NT)
            dnext = _dot(dr2n, wd_ref[pc, :], NT) * more
            gc = jnp.concatenate([gc_ref[:, pc].astype(F32), gcn_ref[:, pc].astype(F32)], axis=0)
            vext = jnp.concatenate([v_ref[:, pc].astype(F32), vn_ref[:, pc].astype(F32)], axis=0)
            dext = jnp.concatenate([da, dnext], axis=0)
            gl, dgl = _gelu_and_grad(gc)
            d_gc = dext * vext * dgl
            up1 = _shift_up(d_gc, 1)[main_rows, :]
            up2 = _shift_up(d_gc, 2)[main_rows, :]
            dm = d_gc[main_rows, :]
            d_gate = (cw_ref[2:3, pc] * dm + cw_ref[1:2, pc] * up1 + cw_ref[0:1, pc] * up2).astype(BF16)
            d_val = (da * gl[main_rows, :]).astype(BF16)
            dh2_ref[0, :, pc] = d_gate
            dh2_ref[1, :, pc] = d_val
            g = g_ref[:, pc].astype(F32)
            s0 = jnp.sum(g * up2, axis=0, keepdims=True)
            s1 = jnp.sum(g * up1, axis=0, keepdims=True)
            s2 = jnp.sum(g * dm, axis=0, keepdims=True)
            rowid = lax.broadcasted_iota(jnp.int32, (8, c1 - c0), 0)
            dcw_parts.append(jnp.where(rowid == 0, s0, jnp.where(rowid == 1, s1,
                                                                 jnp.where(rowid == 2, s2, 0.0))))
            dcb_parts.append(_colsum8(dm))
            part = _dot(d_gate, wug_ref[:, pc], NT) + _dot(d_val, wuv_ref[:, pc], NT)
            prod = part if prod is None else prod + part
        dcw_part = jnp.concatenate(dcw_parts, axis=1)
        dcb_part = jnp.concatenate(dcb_parts, axis=1)
        for jj in range(FF_NJ):
            @pl.when(j == jj)
            def _(jj=jj):
                cols = slice(jj * FF_TILE, (jj + 1) * FF_TILE)
                dcw_ref[:, cols] += dcw_part
                dcb_ref[:, cols] += dcb_part

        @pl.when(j == 0)
        def _():
            acc[...] = prod

        @pl.when(j > 0)
        def _():
            acc[...] += prod

        @pl.when(j == FF_NJ - 1)
        def _():
            d_x1 = acc[...] + _dot(dpg_ref[...], wpg_ref[...], NT) + ALPHA * dr2_ref[...]
            xh, rstd = _ln_stats(r1_ref[...])
            dg1_ref[...] += _colsum8(d_x1 * xh)
            db1_ref[...] += _colsum8(d_x1)
            dr1_ref[...] = _ln_bwd(d_x1 * g1_ref[...], xh, rstd)

    def h2_main(part):
        return pl.BlockSpec((None, tm, FF_TILE), lambda i, j: (part, i, j))

    def h2_next(part):
        return pl.BlockSpec((None, HALO, FF_TILE),
                            lambda i, j: (part, jnp.minimum((i + 1) * nth, last_halo), j))

    tile = pl.BlockSpec((tm, D_MODEL), lambda i, j: (i, 0))
    acc8 = pl.BlockSpec((8, D_MODEL), lambda i, j: (0, 0))
    accff = pl.BlockSpec((8, D_FF), lambda i, j: (0, 0))
    acc_shape = jax.ShapeDtypeStruct((8, D_MODEL), F32)
    accff_shape = jax.ShapeDtypeStruct((8, D_FF), F32)
    return _pc(body, name="ffn_bwd",
               out_shape=(jax.ShapeDtypeStruct((2, t, D_FF), BF16),
                          jax.ShapeDtypeStruct((t, D_MODEL), F32),
                          accff_shape, accff_shape, acc_shape, acc_shape),
               grid=(ni, FF_NJ),
               in_specs=[h2_main(0), h2_main(2), h2_next(2), h2_main(1), h2_next(1),
                         tile,
                         pl.BlockSpec((HALO, D_MODEL), lambda i, j: (jnp.minimum((i + 1) * nth, last_halo), 0)),
                         pl.BlockSpec((FF_TILE, D_MODEL), lambda i, j: (j, 0)),
                         pl.BlockSpec((None, D_MODEL, FF_TILE), lambda i, j: (j, 0, 0)),
                         pl.BlockSpec((None, D_MODEL, FF_TILE), lambda i, j: (j + FF_NJ, 0, 0)),
                         pl.BlockSpec((3, FF_TILE), lambda i, j: (0, j)),
                         tile, pl.BlockSpec((D_MODEL, D_MODEL), lambda i, j: (0, 0)),
                         tile, pl.BlockSpec((1, D_MODEL), lambda i, j: (0, 0))],
               out_specs=(pl.BlockSpec((2, tm, FF_TILE), lambda i, j: (0, i, j)),
                          tile, accff, accff, acc8, acc8),
               scratch=[pltpu.VMEM((tm, D_MODEL), F32)],
               sem=("arbitrary", "arbitrary"))(h2, h2, h2, h2, h2, dr2, dr2, wd, wup_st, wup_st,
                                               convw, dpg, wpg, r1, g1)


ANY = pl.BlockSpec(memory_space=pl.ANY)


def _chip_peers():
    x, y, c = lax.axis_index("x"), lax.axis_index("y"), lax.axis_index("c")
    return x, y, c, [(1 - x, y), (x, 1 - y), (1 - x, 1 - y)]


def _gather_comm(halved, whole=(), blocks=None):
    n, nw = len(halved), len(whole)
    blocks = blocks or {}

    def at(ti, ref, chip, *rest):
        return ref.at[(chip, blocks[ti][1]) + rest] if ti in blocks else ref.at[(chip,) + rest]

    def copies(ins, outs, sems):
        ici_send, ici_recv, d2d_send, d2d_recv, own_send, own_recv = sems
        x, y, c, peers = _chip_peers()
        me = 2 * x + y
        sibling = (x, y, 1 - c)
        own, ici, ici_wait, fwd, fwd_wait = [], [], [], [], []
        for ti in range(n + nw):
            src, dst = ins[ti], outs[ti]
            own.append(pltpu.make_async_remote_copy(
                src_ref=src, dst_ref=at(ti, dst, me), send_sem=own_send.at[ti], recv_sem=own_recv.at[ti],
                device_id=sibling, device_id_type=MESH))
            for k, (px, py) in enumerate(peers):
                pk = 2 * px + py
                sem = dict(send_sem=ici_send.at[ti * 3 + k], recv_sem=ici_recv.at[ti * 3 + k],
                           device_id=(px, py, c), device_id_type=MESH)
                if ti < n:
                    ici.append(pltpu.make_async_remote_copy(src_ref=src.at[c], dst_ref=at(ti, dst, me, c), **sem))
                    ici_wait.append(pltpu.make_async_remote_copy(src_ref=src.at[c], dst_ref=at(ti, dst, pk, c),
                                                                 **sem))
                    dsem = dict(send_sem=d2d_send.at[ti * 3 + k], recv_sem=d2d_recv.at[ti * 3 + k],
                                device_id=sibling, device_id_type=MESH)
                    fwd.append(pltpu.make_async_remote_copy(src_ref=at(ti, dst, pk, c), dst_ref=at(ti, dst, pk, c),
                                                            **dsem))
                    fwd_wait.append(pltpu.make_async_remote_copy(
                        src_ref=at(ti, dst, pk, 1 - c), dst_ref=at(ti, dst, pk, 1 - c), **dsem))
                else:
                    ici.append(pltpu.make_async_remote_copy(src_ref=src, dst_ref=dst.at[me], **sem))
                    ici_wait.append(pltpu.make_async_remote_copy(src_ref=src, dst_ref=dst.at[pk], **sem))
        return own, ici, ici_wait, fwd, fwd_wait

    def start(ins, outs, sems):
        own, ici, _, _, _ = copies(ins, outs, sems)
        for cp in own + ici:
            cp.start()

    def finish(ins, outs, sems):
        own, ici, ici_wait, fwd, fwd_wait = copies(ins, outs, sems)
        for i, cp in enumerate(ici_wait):
            cp.wait_recv()
            if i < len(fwd):
                fwd[i].start()
        for cp in fwd_wait + own:
            cp.wait_recv()
        for cp in own + ici + fwd:
            cp.wait_send()

    def middle(ins, outs, sems):
        _, _, ici_wait, fwd, _ = copies(ins, outs, sems)
        for i, cp in enumerate(ici_wait):
            cp.wait_recv()
            if i < len(fwd):
                fwd[i].start()

    def finish_late(ins, outs, sems):
        own, ici, _, fwd, fwd_wait = copies(ins, outs, sems)
        for cp in fwd_wait + own:
            cp.wait_recv()
        for cp in own + ici + fwd:
            cp.wait_send()

    srcs = list(halved) + list(whole)
    shapes = [jax.ShapeDtypeStruct((N_CHIP,) + ((blocks[ti][0],) if ti in blocks else ()) + s.shape, s.dtype)
              for ti, s in enumerate(srcs)]
    buffers = [(ti, blk[2]) for ti, blk in sorted(blocks.items()) if blk[2] is not None]
    aliases = {len(srcs) + bi: ti for bi, (ti, _) in enumerate(buffers)}
    return _Comm(srcs + [buf for _, buf in buffers], shapes,
                 [pltpu.SemaphoreType.DMA((3 * (n + nw),)), pltpu.SemaphoreType.DMA((3 * (n + nw),)),
                  pltpu.SemaphoreType.DMA((max(3 * n, 1),)), pltpu.SemaphoreType.DMA((max(3 * n, 1),)),
                  pltpu.SemaphoreType.DMA((n + nw,)), pltpu.SemaphoreType.DMA((n + nw,))],
                 start, finish, middle, finish_late, aliases)


def _sibling_exchange_comm(grads):
    n = len(grads)

    def copies(ins, outs, sems):
        send_sems, recv_sems = sems
        x, y, c = lax.axis_index("x"), lax.axis_index("y"), lax.axis_index("c")
        res = []
        for ti in range(n):
            half = ins[ti].shape[1] // 2
            res.append(pltpu.make_async_remote_copy(
                src_ref=ins[ti].at[:, pl.ds(pl.multiple_of((1 - c) * half, 16), half), :],
                dst_ref=outs[ti],
                send_sem=send_sems.at[ti], recv_sem=recv_sems.at[ti],
                device_id=(x, y, 1 - c), device_id_type=MESH))
        return res

    def start(ins, outs, sems):
        for cp in copies(ins, outs, sems):
            cp.start()

    def finish(ins, outs, sems):
        for cp in copies(ins, outs, sems):
            cp.wait()

    return _Comm(grads, [jax.ShapeDtypeStruct((N_CHIP, g.shape[1] // 2, g.shape[2]), g.dtype) for g in grads],
                 [pltpu.SemaphoreType.DMA((n,)), pltpu.SemaphoreType.DMA((n,))], start, finish)


def _in_proj_gathering(x2, own, chip, tm, comm):
    t = x2.shape[0]
    ni = t // tm
    half, cols = own.shape[1], own.shape[2]
    nci, nco = len(comm.ins), len(comm.out_shapes)

    def body(chip_ref, x_ref, own_ref, own_hbm, *rest):
        c_in = rest[:nci]
        h_ref, win_out = rest[nci:nci + 2]
        c_out = rest[nci + 2:nci + 2 + nco]
        w_scr, ici_send, ici_recv, d2d_send, d2d_recv, own_sems, ld_sems = rest[nci + 2 + nco:nci + 9 + nco]
        c_sem = rest[nci + 9 + nco:]
        s, i = pl.program_id(0), pl.program_id(1)
        x, y, c, peers = _chip_peers()
        me = 2 * x + y
        sibling = (x, y, 1 - c)

        def ici(k, slot):
            px, py = peers[k]
            return pltpu.make_async_remote_copy(
                src_ref=own_hbm.at[c], dst_ref=win_out.at[slot, c],
                send_sem=ici_send.at[k], recv_sem=ici_recv.at[k],
                device_id=(px, py, c), device_id_type=MESH)

        def forward(k, core):
            pk = 2 * peers[k][0] + peers[k][1]
            return pltpu.make_async_remote_copy(
                src_ref=win_out.at[pk, core], dst_ref=win_out.at[pk, core],
                send_sem=d2d_send.at[k], recv_sem=d2d_recv.at[k],
                device_id=sibling, device_id_type=MESH)

        place_own = pltpu.make_async_remote_copy(
            src_ref=own_hbm, dst_ref=win_out.at[me], send_sem=own_sems.at[0], recv_sem=own_sems.at[1],
            device_id=sibling, device_id_type=MESH)

        @pl.when((s == 0) & (i == 0))
        def _():
            for k in range(2):
                ici(k, me).start()
            place_own.start()

        @pl.when(s == 0)
        def _():
            xv = x_ref[...]
            h_ref[...] = (_dot(xv[:, :half], own_ref[0]) + _dot(xv[:, half:], own_ref[1])).astype(BF16)

        for k in range(3):
            @pl.when((s == k + 1) & (i == 0))
            def _(k=k):
                pk = 2 * peers[k][0] + peers[k][1]
                ici(k, pk).wait_recv()
                if k == 0:
                    ici(2, me).start()
                forward(k, c).start()
                forward(k, 1 - c).wait_recv()
                loads = [pltpu.make_async_copy(win_out.at[pk, hh], w_scr.at[hh], ld_sems.at[hh])
                         for hh in range(2)]
                for ld in loads:
                    ld.start()
                for ld in loads:
                    ld.wait()
                if k == 1:
                    comm.start(c_in, c_out, c_sem)

        @pl.when(s > 0)
        def _():
            xv = x_ref[...]
            h_ref[...] = (_dot(xv[:, :half], w_scr[0]) + _dot(xv[:, half:], w_scr[1])).astype(BF16)

        @pl.when((s == N_CHIP - 1) & (i == ni - 1))
        def _():
            place_own.wait()
            for k in range(3):
                ici(k, me).wait_send()
                forward(k, c).wait_send()
            comm.finish(c_in, c_out, c_sem)

    def shard_col(s, me):
        return jnp.where(s == 0, me, me ^ jnp.where(s == 1, 2, jnp.where(s == 2, 1, 3)))

    res = _pc(body, name="in_proj",
              out_shape=(jax.ShapeDtypeStruct((t, N_CHIP * cols), BF16),
                         jax.ShapeDtypeStruct((N_CHIP,) + own.shape, own.dtype)) + tuple(comm.out_shapes),
              grid=(N_CHIP, ni), nsp=1,
              in_specs=[pl.BlockSpec((tm, 2 * half), lambda s, i, chip_ref: (i, 0)),
                        pl.BlockSpec(own.shape, lambda s, i, chip_ref: (0, 0, 0)),
                        ANY] + [ANY] * nci,
              out_specs=(pl.BlockSpec((tm, cols), lambda s, i, chip_ref: (i, shard_col(s, chip_ref[0]))),
                         ANY) + tuple([ANY] * nco),
              scratch=[pltpu.VMEM(own.shape, own.dtype),
                       pltpu.SemaphoreType.DMA((3,)), pltpu.SemaphoreType.DMA((3,)),
                       pltpu.SemaphoreType.DMA((3,)), pltpu.SemaphoreType.DMA((3,)),
                       pltpu.SemaphoreType.DMA((2,)), pltpu.SemaphoreType.DMA((2,))] + comm.sems,
              sem=("arbitrary", "arbitrary"))(chip, x2, own, own, *comm.ins)
    return res[0], res[1], res[2:]


def _rs_add_halves(name, grad, recv, core):
    _, r, cdim = grad.shape
    half = r // 2
    tr = _row_tile(half, cdim, mult=16)
    nr = half // tr

    def body(c_ref, g_ref, r_ref, o_ref):
        o_ref[...] = (g_ref[...].astype(F32) + r_ref[...].astype(F32)).astype(BF16)

    return _pc(body, name=name, out_shape=jax.ShapeDtypeStruct((N_CHIP, half, cdim), BF16),
               grid=(N_CHIP, nr), nsp=1,
               in_specs=[pl.BlockSpec((None, tr, cdim), lambda j, i, c_ref: (j, c_ref[0] * nr + i, 0)),
                         pl.BlockSpec((None, tr, cdim), lambda j, i, c_ref: (j, i, 0))],
               out_specs=pl.BlockSpec((None, tr, cdim), lambda j, i, c_ref: (j, i, 0)),
               sem=("parallel", "parallel"))(core, grad, recv)


def _chip_exchange_comm(parts):
    n = len(parts)

    def copies(ins, outs, sems):
        send_sems, recv_sems = sems
        x, y, c, peers = _chip_peers()
        return [pltpu.make_async_remote_copy(
            src_ref=ins[ti].at[2 * px + py], dst_ref=outs[ti].at[k],
            send_sem=send_sems.at[ti * 3 + k], recv_sem=recv_sems.at[ti * 3 + k],
            device_id=(px, py, c), device_id_type=MESH)
            for ti in range(n) for k, (px, py) in enumerate(peers)]

    def start(ins, outs, sems):
        for cp in copies(ins, outs, sems):
            cp.start()

    def finish(ins, outs, sems):
        for cp in copies(ins, outs, sems):
            cp.wait()

    return _Comm(parts, [jax.ShapeDtypeStruct((3,) + p.shape[1:], p.dtype) for p in parts],
                 [pltpu.SemaphoreType.DMA((3 * n,)), pltpu.SemaphoreType.DMA((3 * n,))], start, finish)


def _rs_sum_chips(name, part, recv, chip):
    _, half, cdim = recv.shape
    tr = _row_tile(half, cdim, mult=16)

    def body(chip_ref, p_ref, r_ref, o_ref):
        o_ref[...] = ((p_ref[...].astype(F32) + r_ref[0].astype(F32)) + r_ref[1].astype(F32)
                      ) + r_ref[2].astype(F32)

    return _pc(body, name=name, out_shape=jax.ShapeDtypeStruct((half, cdim), F32),
               grid=(half // tr,), nsp=1,
               in_specs=[pl.BlockSpec((None, tr, cdim), lambda i, chip_ref: (chip_ref[0], i, 0)),
                         pl.BlockSpec((3, tr, cdim), lambda i, chip_ref: (0, i, 0))],
               out_specs=pl.BlockSpec((tr, cdim), lambda i, chip_ref: (i, 0)),
               sem=("parallel",))(chip, part, recv)


def _rs_send_halves(halves):
    n = len(halves)

    def body(*refs):
        ins, outs = refs[:n], refs[n:2 * n]
        send_sems, recv_sems = refs[2 * n:]
        x, y, c = lax.axis_index("x"), lax.axis_index("y"), lax.axis_index("c")
        sends = []
        for ti in range(n):
            cp = pltpu.make_async_remote_copy(
                src_ref=ins[ti], dst_ref=outs[ti],
                send_sem=send_sems.at[ti], recv_sem=recv_sems.at[ti],
                device_id=(x, y, 1 - c), device_id_type=MESH)
            cp.start()
            sends.append(cp)
        for cp in sends:
            cp.wait()

    return _pc(body, name="rs_send_halves",
               out_shape=tuple(jax.ShapeDtypeStruct(hv.shape, hv.dtype) for hv in halves),
               in_specs=[ANY] * n, out_specs=tuple([ANY] * n),
               scratch=[pltpu.SemaphoreType.DMA((n,)), pltpu.SemaphoreType.DMA((n,))])(*halves)


def _adamw_rows(name, mine, theirs, w, m, v, core):
    half, cdim = mine.shape
    tr = _row_tile(half, cdim, budget=1 << 19)
    nrh = half // tr

    def body(c_ref, mine_ref, theirs_ref, w_ref, m_ref, v_ref, g_ref, d_ref, m2_ref, v2_ref):
        is_mine = (pl.program_id(0) // nrh) == c_ref[0]
        g = jnp.where(is_mine, mine_ref[...], theirs_ref[...])
        d, m2, v2 = _adamw(w_ref[...], g, m_ref[...], v_ref[...])
        g_ref[...] = g
        d_ref[...] = d
        m2_ref[...] = m2
        v2_ref[...] = v2

    htile = pl.BlockSpec((tr, cdim), lambda i, c_ref: (i % nrh, 0))
    tile = pl.BlockSpec((tr, cdim), lambda i, c_ref: (i, 0))
    shp = jax.ShapeDtypeStruct((2 * half, cdim), F32)
    return _pc(body, name=name, out_shape=(shp, shp, shp, shp), grid=(2 * nrh,), nsp=1,
               in_specs=[htile, htile, tile, tile, tile], out_specs=(tile, tile, tile, tile),
               sem=("parallel",))(core, mine, theirs, w, m, v)


def _adamw_whole(name, g, w, m, v):
    def body(g_ref, w_ref, m_ref, v_ref, d_ref, m2_ref, v2_ref):
        d, m2, v2 = _adamw(w_ref[...], g_ref[...], m_ref[...], v_ref[...])
        d_ref[...] = d
        m2_ref[...] = m2
        v2_ref[...] = v2

    shp = jax.ShapeDtypeStruct(g.shape, F32)
    return _pc(body, name=name, out_shape=(shp, shp, shp))(g, w, m, v)


SMALL_LAYOUT = (
    ("sgu_w_s", 1024, 1, 0),
    ("sgu_b_s", 8, 1, 1024),
    ("sgu_norm_g", 1, 0, 0),
    ("sgu_norm_b", 1, 0, 1),
    ("hgrn_norm_g", 1, 0, 3),
    ("ln1_g", 1, 0, 4),
    ("ln1_b", 1, 0, 5),
    ("ffn_conv_b", 1, 2, 3),
    ("ln2_g", 1, 0, 6),
    ("ln2_b", 1, 0, 7),
)
LB_ROW = 2
LOSS_ROW = 8
PACK_SHAPES = ((16, D_MODEL), (N_GROUP * 128 + 16, 128), (8, D_FF))
GATH_DTYPES = (F32, BF16, F32)


def _small_allreduce_adamw(rows1024, dws, dbs, dcw, dcb, logits, m_logits, v_logits,
                           small_w, small_m, small_v):
    ns = len(SMALL_LAYOUT)
    nr = len(rows1024)
    nb = len(PACK_SHAPES)

    def body(*refs):
        row_refs = refs[:nr]
        dws_ref, dbs_ref, dcw_ref, dcb_ref, lg_ref, mlg_ref, vlg_ref = refs[nr:nr + 7]
        pos = nr + 7
        w_refs = refs[pos:pos + ns]
        m_refs = refs[pos + ns:pos + 2 * ns]
        v_refs = refs[pos + 2 * ns:pos + 3 * ns]
        pos += 3 * ns
        loss_ref, dcw_out = refs[pos:pos + 2]
        lg_outs = refs[pos + 2:pos + 6]
        pos += 6
        outs = refs[pos:pos + 4 * ns]
        pos += 4 * ns
        pack = refs[pos:pos + nb]
        sib = refs[pos + nb:pos + 2 * nb]
        gath = refs[pos + 2 * nb:pos + 3 * nb]
        d2d_send, d2d_recv, ici_send, ici_recv = refs[pos + 3 * nb:]

        x, y, c, peers = _chip_peers()
        me = 2 * x + y
        sibling = (x, y, 1 - c)

        pack[0][...] = jnp.zeros(PACK_SHAPES[0], F32)
        for k in range(nr):
            pack[0][k:k + 1, :] = row_refs[k][0:1, :]
        pack[1][0:N_GROUP * 128, :] = dws_ref[...]
        pack[1][N_GROUP * 128:N_GROUP * 128 + 8, :] = dbs_ref[...]
        pack[1][N_GROUP * 128 + 8:, :] = jnp.zeros((8, 128), F32)
        pack[2][...] = jnp.zeros(PACK_SHAPES[2], F32)
        pack[2][0:3, :] = dcw_ref[0:3, :]
        pack[2][3:4, :] = dcb_ref[0:1, :]

        d2d = [pltpu.make_async_remote_copy(
            src_ref=pack[b], dst_ref=sib[b], send_sem=d2d_send.at[b], recv_sem=d2d_recv.at[b],
            device_id=sibling, device_id_type=MESH) for b in range(nb)]
        for cp in d2d:
            cp.start()
        for cp in d2d:
            cp.wait()
        for b in range(nb):
            gath[b][me] = (pack[b][...] + sib[b][...]).astype(GATH_DTYPES[b])

        ici, ici_wait = [], []
        for b in range(nb):
            for k, (px, py) in enumerate(peers):
                sem = dict(send_sem=ici_send.at[b * 3 + k], recv_sem=ici_recv.at[b * 3 + k],
                           device_id=(px, py, c), device_id_type=MESH)
                ici.append(pltpu.make_async_remote_copy(src_ref=gath[b].at[me], dst_ref=gath[b].at[me], **sem))
                ici_wait.append(pltpu.make_async_remote_copy(
                    src_ref=gath[b].at[me], dst_ref=gath[b].at[2 * px + py], **sem))
        for cp in ici:
            cp.start()
        for cp in ici_wait:
            cp.wait_recv()
        for cp in ici:
            cp.wait_send()

        tot = pack
        for b in range(nb):
            tot[b][...] = ((gath[b][0].astype(F32) + gath[b][1].astype(F32)) + gath[b][2].astype(F32)
                           ) + gath[b][3].astype(F32)

        loss_ref[...] = tot[0][LOSS_ROW:LOSS_ROW + 1, :]
        dcw_out[...] = tot[2][...]
        lb = _sig(lg_ref[0:1, :] - lg_ref[1:2, :])
        d0 = tot[0][LB_ROW:LB_ROW + 1, :] * lb * (1.0 - lb)
        rowid = lax.broadcasted_iota(jnp.int32, (2, D_MODEL), 0)
        g_lg = jnp.where(rowid == 0, d0, -d0)
        dl, ml, vl = _adamw(lg_ref[...], g_lg, mlg_ref[...], vlg_ref[...])
        lg_outs[0][...] = g_lg
        lg_outs[1][...] = dl
        lg_outs[2][...] = ml
        lg_outs[3][...] = vl
        for si, (_, rows, b, r0) in enumerate(SMALL_LAYOUT):
            g = tot[b][r0:r0 + rows, :]
            dl, ml, vl = _adamw(w_refs[si][...], g, m_refs[si][...], v_refs[si][...])
            outs[4 * si][...] = g
            outs[4 * si + 1][...] = dl
            outs[4 * si + 2][...] = ml
            outs[4 * si + 3][...] = vl

    shapes = [jax.ShapeDtypeStruct((1, D_MODEL), F32), jax.ShapeDtypeStruct((8, D_FF), F32)]
    shapes += [jax.ShapeDtypeStruct((2, D_MODEL), F32)] * 4
    for w in small_w:
        shapes += [jax.ShapeDtypeStruct(w.shape, F32)] * 4
    scratch = [pltpu.VMEM(shp, F32) for shp in PACK_SHAPES]
    scratch += [pltpu.VMEM(shp, F32) for shp in PACK_SHAPES]
    scratch += [pltpu.VMEM((N_CHIP,) + shp, dt) for shp, dt in zip(PACK_SHAPES, GATH_DTYPES)]
    scratch += [pltpu.SemaphoreType.DMA((nb,)), pltpu.SemaphoreType.DMA((nb,)),
                pltpu.SemaphoreType.DMA((3 * nb,)), pltpu.SemaphoreType.DMA((3 * nb,))]
    vm = pl.BlockSpec(memory_space=pltpu.VMEM)
    n_in = nr + 7 + 3 * ns
    res = _pc(body, name="small_allreduce_adamw", out_shape=tuple(shapes),
              in_specs=[vm] * n_in, out_specs=tuple([vm] * len(shapes)),
              scratch=scratch)(*rows1024, dws, dbs, dcw, dcb, logits, m_logits, v_logits,
                               *small_w, *small_m, *small_v)
    return res[0], res[1], res[2:6], res[6:]


def kernel(x, p, w_in, sgu_w_s, sgu_b_s, sgu_norm_g, sgu_norm_b, hgrn_lb_logits, hgrn_norm_g, w_branch, w_out, ln1_g, ln1_b, ffn_w_up, ffn_conv_w, ffn_conv_b, ffn_w_down, ln2_g, ln2_b, ple_w_proj, ple_w_gate, loss_target, m_w_in, m_sgu_w_s, m_sgu_b_s, m_sgu_norm_g, m_sgu_norm_b, m_hgrn_lb_logits, m_hgrn_norm_g, m_w_branch, m_w_out, m_ln1_g, m_ln1_b, m_ffn_w_up, m_ffn_conv_w, m_ffn_conv_b, m_ffn_w_down, m_ln2_g, m_ln2_b, m_ple_w_proj, m_ple_w_gate, v_w_in, v_sgu_w_s, v_sgu_b_s, v_sgu_norm_g, v_sgu_norm_b, v_hgrn_lb_logits, v_hgrn_norm_g, v_w_branch, v_w_out, v_ln1_g, v_ln1_b, v_ffn_w_up, v_ffn_conv_w, v_ffn_conv_b, v_ffn_w_down, v_ln2_g, v_ln2_b, v_ple_w_proj, v_ple_w_gate):
    t = x.shape[1]
    x2 = x.reshape(t, D_MODEL)
    p2 = p.reshape(t, PLE_DIM)
    tgt = loss_target.reshape(t, D_MODEL)
    core = lax.axis_index("c").astype(jnp.int32).reshape(1)
    chip_id = (2 * lax.axis_index("x") + lax.axis_index("y")).astype(jnp.int32).reshape(1)

    big_w = [w_in[0], w_branch[0, 0], w_branch[0, 1], w_out[0], ffn_w_up[0], ffn_w_down[0],
             ple_w_proj[0], ple_w_gate[0]]
    big_m = [m_w_in[0], m_w_branch[0, 0], m_w_branch[0, 1], m_w_out[0], m_ffn_w_up[0],
             m_ffn_w_down[0], m_ple_w_proj[0], m_ple_w_gate[0]]
    big_v = [v_w_in[0], v_w_branch[0, 0], v_w_branch[0, 1], v_w_out[0], v_ffn_w_up[0],
             v_ffn_w_down[0], v_ple_w_proj[0], v_ple_w_gate[0]]
    def halves_of(i):
        w = big_w[i]
        return w.astype(BF16).reshape(2, w.shape[0] // 2, w.shape[1])

    def stacked(g, i):
        return g.reshape(N_CHIP, big_w[i].shape[0], big_w[i].shape[1])


    cid = jnp.arange(SGU_BLOCK) // CHUNK
    maskf = (cid[:, None] >= cid[None, :]).astype(F32)
    ws_masked = sgu_w_s[0] * maskf[None]
    wm = ws_masked.astype(BF16)
    wmt = jnp.transpose(ws_masked, (0, 2, 1)).astype(BF16)
    bsb = jnp.broadcast_to(sgu_b_s[0][:, :, None], (N_GROUP, SGU_BLOCK, 128))

    up_rows = big_w[4].shape[0] // 2
    up_blocks = [big_w[4][k * up_rows:(k + 1) * up_rows].astype(BF16).reshape(2, up_rows // 2, -1)
                 for k in range(2)]
    h, win_g, (up_g,) = _in_proj_gathering(x2, halves_of(0), chip_id, 512,
                                           _gather_comm([up_blocks[0]], blocks={0: (2, 0, None)}))
    win_st = stacked(win_g, 0)
    ya, _ = _sgu_fwd(h, wm, bsb, sgu_norm_g, sgu_norm_b)
    (yb, o_all, st_all), mix_g = _hgrn_fwd(
        h, hgrn_lb_logits, hgrn_norm_g,
        comm=_gather_comm([halves_of(i) for i in (1, 2, 3)] + [up_blocks[1]], [ffn_conv_w[0]],
                          blocks={3: (2, 1, up_g)}))
    wb0, wb1, wo = [stacked(g, i).reshape(D_MODEL, D_MODEL) for g, i in zip(mix_g[:3], (1, 2, 3))]
    wup_st = stacked(mix_g[3], 4)
    convw = jnp.transpose(mix_g[4], (1, 0, 2)).reshape(3, D_FF)
    (r1, a_br, b_br, m_bf, x1b), _ = _mix_fwd(ya, yb, h, x2, wb0, wb1, wo, ln1_g, ln1_b, 256)
    h2, act, out_g = _ffn_up_act(x1b, wup_st, convw, ffn_conv_b, 512,
                                 _gather_comm([halves_of(i) for i in (5, 6, 7)]))
    wd = stacked(out_g[0], 5).reshape(D_FF, D_MODEL)
    wpp = jnp.transpose(stacked(out_g[1], 6), (1, 0, 2)).reshape(PLE_DIM, D_MODEL)
    wpg = stacked(out_g[2], 7).reshape(D_MODEL, D_MODEL)
    dr2, dpg, dpp, loss_acc, dg2, db2 = _out_fwd_bwd(
        act, x1b, r1, p2, tgt, wd, wpg, wpp, ln1_g, ln1_b, ln2_g, ln2_b, 256)

    dh2, dr1, dcw, dcb, dg1, db1 = _ffn_bwd(h2, dr2, wd, wup_st, dpg, wpg, r1, ln1_g, convw, 256)
    d_wd = _mm_tn("ffn_down_wgrad", act, dr2, FF_TILE, 512)
    d_wpg = _mm_tn("ple_gate_wgrad", x1b, dpg, 512, D_MODEL)
    d_wpp_st = _mm_tn("ple_proj_wgrad", p2, dpp, PLE_DIM, PLE_DIM, stacked=True)
    d_wup_st = _mm("ffn_up_wgrad", x1b, dh2, TN, (2, N_CHIP),
                   pl.BlockSpec((t, 512), lambda i, j: (0, i)),
                   pl.BlockSpec((None, t, FF_TILE), lambda i, j: (j // FF_NJ, 0, j % FF_NJ)),
                   jax.ShapeDtypeStruct((N_CHIP, D_MODEL, FF_TILE), BF16),
                   pl.BlockSpec((None, 512, FF_TILE), lambda i, j: (j, i, 0)))
    da_bf, db_bf, dh, dya, dyb = _mix_bwd(dr1, h, a_br, b_br, wo, wb0, wb1, 256)
    d_wo = _mm_tn("out_proj_wgrad", m_bf, dr1, 512, 512)
    d_wb0 = _mm_tn("branch0_wgrad", ya, da_bf, 512, D_MODEL)
    d_wb1 = _mm_tn("branch1_wgrad", yb, db_bf, 512, D_MODEL)
    grads_1 = [d_wb0.reshape(4, 256, D_MODEL), d_wb1.reshape(4, 256, D_MODEL),
               d_wo.reshape(4, 256, D_MODEL), d_wup_st, d_wd.reshape(4, D_FF // 4, D_MODEL),
               d_wpp_st, d_wpg.reshape(4, 256, D_MODEL)]
    (dh, dws, dbs, dgv, dbv), recv_a1 = _sgu_bwd(h, dya, wm, wmt, bsb, sgu_norm_g, sgu_norm_b, maskf, dh,
                                                 comm=_sibling_exchange_comm(grads_1))
    parts_1 = [_rs_add_halves("rs_add_halves%d" % (i + 1), g, r, core)
               for i, (g, r) in enumerate(zip(grads_1, recv_a1))]
    (dh, dlb, dgn), recv_b1 = _hgrn_bwd(h, o_all, dyb, st_all, hgrn_lb_logits, hgrn_norm_g, dh,
                                         comm=_chip_exchange_comm(parts_1))

    grads_0 = [_in_proj_wgrad(x2, dh, 512, D_MODEL)]
    recv_a0 = _run_comm("rs_sibling_exchange0", _sibling_exchange_comm(grads_0))
    parts_0 = [_rs_add_halves("rs_add_halves0", grads_0[0], recv_a0[0], core)]
    gx, recv_b0 = _in_proj_xgrad(dh, win_st, dr1, 512, _chip_exchange_comm(parts_0))
    parts = parts_0 + parts_1
    recv_b = list(recv_b0) + list(recv_b1)
    halves = [_rs_sum_chips("rs_sum_chips%d" % i, pt, r, chip_id)
              for i, (pt, r) in enumerate(zip(parts, recv_b))]
    theirs = _rs_send_halves(halves)
    big_out = [_adamw_rows("adamw_big%d" % i, halves[i], theirs[i], big_w[i], big_m[i], big_v[i], core)
               for i in range(len(halves))]

    small_in = dict(sgu_w_s=(sgu_w_s, m_sgu_w_s, v_sgu_w_s), sgu_b_s=(sgu_b_s, m_sgu_b_s, v_sgu_b_s),
                    sgu_norm_g=(sgu_norm_g, m_sgu_norm_g, v_sgu_norm_g),
                    sgu_norm_b=(sgu_norm_b, m_sgu_norm_b, v_sgu_norm_b),
                    hgrn_norm_g=(hgrn_norm_g, m_hgrn_norm_g, v_hgrn_norm_g),
                    ln1_g=(ln1_g, m_ln1_g, v_ln1_g), ln1_b=(ln1_b, m_ln1_b, v_ln1_b),
                    ffn_conv_b=(ffn_conv_b, m_ffn_conv_b, v_ffn_conv_b),
                    ln2_g=(ln2_g, m_ln2_g, v_ln2_g), ln2_b=(ln2_b, m_ln2_b, v_ln2_b))

    def flat(name, arr):
        rows = dict((n, r) for n, r, _, _ in SMALL_LAYOUT)[name]
        return arr.reshape(rows, arr.size // rows)

    names = [n for n, _, _, _ in SMALL_LAYOUT]
    sw = [flat(n, small_in[n][0]) for n in names]
    sm = [flat(n, small_in[n][1]) for n in names]
    sv = [flat(n, small_in[n][2]) for n in names]
    loss_rows, dcw_tot, lg_out, small_out = _small_allreduce_adamw(
        [dgv, dbv, dlb, dgn, dg1, db1, dg2, db2, loss_acc], dws.reshape(N_GROUP * 128, 128), dbs, dcw, dcb,
        hgrn_lb_logits, m_hgrn_lb_logits, v_hgrn_lb_logits, sw, sm, sv)
    loss = loss_rows[0, 0]

    chip = 2 * lax.axis_index("x") + lax.axis_index("y")
    g_cw = lax.dynamic_slice(dcw_tot, (0, chip * (D_FF // 4)), (3, D_FF // 4))
    cw_out = _adamw_whole("adamw_conv_w", g_cw, ffn_conv_w[0], m_ffn_conv_w[0], v_ffn_conv_w[0])

    res = {}
    for si, n in enumerate(names):
        shp = small_in[n][0].shape
        res[n] = tuple(small_out[4 * si + k].reshape(shp) for k in range(4))
    res["hgrn_lb_logits"] = tuple(lg_out)
    res["ffn_conv_w"] = (g_cw[None],) + tuple(o[None] for o in cw_out)

    def big(i):
        return tuple(big_out[i])

    res["w_in"] = tuple(o[None] for o in big(0))
    res["w_branch"] = tuple(jnp.stack([o0, o1])[None] for o0, o1 in zip(big(1), big(2)))
    res["w_out"] = tuple(o[None] for o in big(3))
    res["ffn_w_up"] = tuple(o[None] for o in big(4))
    res["ffn_w_down"] = tuple(o[None] for o in big(5))
    res["ple_w_proj"] = tuple(o[None] for o in big(6))
    res["ple_w_gate"] = tuple(o[None] for o in big(7))

    order = ["w_in", "sgu_w_s", "sgu_b_s", "sgu_norm_g", "sgu_norm_b", "hgrn_lb_logits",
             "hgrn_norm_g", "w_branch", "w_out", "ln1_g", "ln1_b", "ffn_w_up", "ffn_conv_w",
             "ffn_conv_b", "ffn_w_down", "ln2_g", "ln2_b", "ple_w_proj", "ple_w_gate"]
    outs = [loss, gx.reshape(1, t, D_MODEL)]
    for k in range(4):
        outs += [res[n][k] for n in order]
    return tuple(outs)
```

```python
import jax
import jax.numpy as jnp
from jax import lax
from jax.experimental import pallas as pl
from jax.experimental.pallas import tpu as pltpu

F32 = jnp.float32
BF16 = jnp.bfloat16
HIGHEST = lax.Precision.HIGHEST
MESH = pl.DeviceIdType.MESH

D_MODEL = 1024
CHUNK = 64
SGU_BLOCK = 128
SGU_STEP_BLOCKS = 4
SGU_ROWS = SGU_STEP_BLOCKS * SGU_BLOCK
N_GROUP = 8
N_HEAD = 8
HEAD_DIM = 128
D_FF = 2816
PLE_DIM = 256
LN_EPS = 1e-5
RMS_EPS = 1e-6
ALPHA = 2.0 ** 0.25
N_CHIP = 4

ADAM_LR = 0.001
ADAM_B1 = 0.9
ADAM_B2 = 0.999
ADAM_EPS = 1e-08
ADAM_WD = 0.01
ADAM_STEP = 10

VMEM_LIMIT = 56 * 1024 * 1024

NN = (((1,), (0,)), ((), ()))
NT = (((1,), (1,)), ((), ()))
TN = (((0,), (0,)), ((), ()))


def _pc(body, *, name, out_shape, grid=None, in_specs=None, out_specs=None, scratch=(),
        sem=None, nsp=0, vmem=VMEM_LIMIT, aliases=None):
    params = dict(vmem_limit_bytes=vmem)
    if sem is not None:
        params["dimension_semantics"] = sem
    kw = dict(name=name, out_shape=out_shape, compiler_params=pltpu.CompilerParams(**params))
    if aliases:
        kw["input_output_aliases"] = aliases
    if nsp:
        kw["grid_spec"] = pltpu.PrefetchScalarGridSpec(
            num_scalar_prefetch=nsp, grid=grid, in_specs=in_specs, out_specs=out_specs,
            scratch_shapes=list(scratch))
    else:
        if grid is not None:
            kw["grid"] = grid
        if in_specs is not None:
            kw["in_specs"] = in_specs
            kw["out_specs"] = out_specs
        kw["scratch_shapes"] = list(scratch)
    return pl.pallas_call(body, **kw)


def _dot(a, b, dims=NN):
    return lax.dot_general(a.astype(BF16), b.astype(BF16), dims, preferred_element_type=F32)


def _dot32(a, b, dims=NN):
    return lax.dot_general(a, b, dims, precision=HIGHEST, preferred_element_type=F32)


def _sig(x):
    return 1.0 / (1.0 + jnp.exp(-x))


_GC = 0.7978845608028654
_GA = 0.044715


def _gelu(x):
    return 0.5 * x * (1.0 + jnp.tanh(_GC * (x + _GA * x * x * x)))


def _gelu_and_grad(x):
    t = jnp.tanh(_GC * (x + _GA * x * x * x))
    g = 0.5 * x * (1.0 + t)
    dg = 0.5 * (1.0 + t) + 0.5 * x * (1.0 - t * t) * _GC * (1.0 + 3.0 * _GA * x * x)
    return g, dg


def _ln_stats(r):
    mu = jnp.mean(r, axis=-1, keepdims=True)
    xc = r - mu
    var = jnp.mean(xc * xc, axis=-1, keepdims=True)
    rstd = lax.rsqrt(var + LN_EPS)
    return xc * rstd, rstd


def _ln_bwd(dxh, xh, rstd):
    m1 = jnp.mean(dxh, axis=-1, keepdims=True)
    m2 = jnp.mean(dxh * xh, axis=-1, keepdims=True)
    return rstd * (dxh - m1 - xh * m2)


def _colsum8(v):
    return jnp.broadcast_to(jnp.sum(v, axis=0, keepdims=True), (8, v.shape[1]))


def _adamw(w, g, m, v):
    m2 = ADAM_B1 * m + (1.0 - ADAM_B1) * g
    v2 = ADAM_B2 * v + (1.0 - ADAM_B2) * (g * g)
    m_hat = m2 / (1.0 - ADAM_B1 ** ADAM_STEP)
    v_hat = v2 / (1.0 - ADAM_B2 ** ADAM_STEP)
    delta = -ADAM_LR * (m_hat / (jnp.sqrt(v_hat) + ADAM_EPS) + ADAM_WD * w)
    return delta, m2, v2


def _row_tile(rows, cols, itemsize=4, budget=1 << 20, mult=8):
    best = mult
    for tr in range(mult, rows + 1, mult):
        if rows % tr == 0 and tr * cols * itemsize <= budget:
            best = tr
    return best


def _mm(name, a, b, dims, grid, a_spec, b_spec, out_shape, o_spec):
    out_dtype = out_shape.dtype

    def body(a_ref, b_ref, o_ref):
        o_ref[...] = _dot(a_ref[...], b_ref[...], dims).astype(out_dtype)

    return _pc(body, name=name, out_shape=out_shape, grid=grid, in_specs=[a_spec, b_spec],
               out_specs=o_spec, sem=("parallel", "parallel"))(a, b)


class _Comm:
    def __init__(self, ins, out_shapes, sems, start, finish, middle=None, finish_late=None, aliases=None):
        self.ins, self.out_shapes, self.sems = list(ins), list(out_shapes), list(sems)
        self.start, self.finish = start, finish
        self.middle, self.finish_late = middle, finish_late
        self.aliases = aliases or {}


def _hosted_call(body, comm, first, last, *, name, out_shape, grid, in_specs, out_specs, scratch, sem,
                 args, aliases=None, mid=None):
    n_in, n_out, n_scr = len(in_specs), len(out_shape), len(scratch)
    nci, nco = len(comm.ins), len(comm.out_shapes)

    def wrapped(*refs):
        pos = n_in
        own_in, c_in = refs[:pos], refs[pos:pos + nci]
        pos += nci
        own_out, c_out = refs[pos:pos + n_out], refs[pos + n_out:pos + n_out + nco]
        pos += n_out + nco
        own_scr, c_sem = refs[pos:pos + n_scr], refs[pos + n_scr:]

        @pl.when(first())
        def _():
            comm.start(c_in, c_out, c_sem)

        body(*own_in, *own_out, *own_scr)

        if mid is not None:
            @pl.when(mid())
            def _():
                comm.middle(c_in, c_out, c_sem)

        @pl.when(last())
        def _():
            (comm.finish if mid is None else comm.finish_late)(c_in, c_out, c_sem)

    return _pc(wrapped, name=name, out_shape=tuple(out_shape) + tuple(comm.out_shapes), grid=grid,
               in_specs=list(in_specs) + [ANY] * nci, out_specs=tuple(out_specs) + tuple([ANY] * nco),
               scratch=list(scratch) + comm.sems, sem=sem,
               aliases={**(aliases or {}), **{n_in + ci: n_out + co for ci, co in comm.aliases.items()}},
               )(*args, *comm.ins)


def _grid1_call(body, comm, n, *, name, out_shape, in_specs, out_specs, scratch, args, aliases=None):
    if comm is None:
        return _pc(body, name=name, out_shape=out_shape, grid=(n,), in_specs=in_specs, out_specs=out_specs,
                   scratch=scratch, sem=("arbitrary",), aliases=aliases)(*args), ()
    res = _hosted_call(body, comm, lambda: pl.program_id(0) == 0, lambda: pl.program_id(0) == n - 1,
                       name=name, out_shape=out_shape, grid=(n,), in_specs=in_specs,
                       out_specs=out_specs, scratch=scratch, sem=("arbitrary",), args=args, aliases=aliases)
    return res[:len(out_shape)], res[len(out_shape):]


def _run_comm(name, comm):
    nci, nco = len(comm.ins), len(comm.out_shapes)

    def body(*refs):
        c_in, c_out, c_sem = refs[:nci], refs[nci:nci + nco], refs[nci + nco:]
        comm.start(c_in, c_out, c_sem)
        comm.finish(c_in, c_out, c_sem)

    return _pc(body, name=name, out_shape=tuple(comm.out_shapes), in_specs=[ANY] * nci,
               out_specs=tuple([ANY] * nco), scratch=comm.sems)(*comm.ins)


def _mm_tn(name, a, b, tm, tn, stacked=False):
    t, m = a.shape
    _, n = b.shape
    if stacked:
        assert tm == m
        out_shape = jax.ShapeDtypeStruct((n // tn, m, tn), BF16)
        o_spec = pl.BlockSpec((None, tm, tn), lambda i, j: (j, 0, 0))
    else:
        out_shape = jax.ShapeDtypeStruct((m, n), BF16)
        o_spec = pl.BlockSpec((tm, tn), lambda i, j: (i, j))
    return _mm(name, a, b, TN, (m // tm, n // tn),
               pl.BlockSpec((t, tm), lambda i, j: (0, i)),
               pl.BlockSpec((t, tn), lambda i, j: (0, j)),
               out_shape, o_spec)


DH_SLOT = (2, 0, 1, 3)


def _dh_slot(j):
    return jnp.where(j == 3, 3, (j + 2) % 3)


def _in_proj_wgrad(x2b, dh, tm, tn):
    t, m = x2b.shape
    n = dh.shape[2]

    def body(a_ref, b_ref, o_ref):
        o_ref[...] = _dot(a_ref[...], b_ref[...], TN).astype(BF16)

    return _pc(body, name="in_proj_wgrad", out_shape=jax.ShapeDtypeStruct((N_CHIP, m, n), BF16),
               grid=(N_CHIP, m // tm, n // tn),
               in_specs=[pl.BlockSpec((t, tm), lambda j, i, k: (0, i)),
                         pl.BlockSpec((None, t, tn), lambda j, i, k: (_dh_slot(j), 0, k))],
               out_specs=pl.BlockSpec((None, tm, tn), lambda j, i, k: (j, i, k)),
               sem=("parallel", "parallel", "parallel"))(x2b, dh)


def _in_proj_xgrad(dh, win_st, dr1, tm, comm):
    t = dr1.shape[0]
    ni = t // tm

    def body(a_ref, b_ref, add_ref, o_ref, acc):
        j = pl.program_id(1)
        prod = _dot(a_ref[...], b_ref[...], NT)

        @pl.when(j == 0)
        def _():
            acc[...] = prod + ALPHA * add_ref[...].astype(F32)

        @pl.when((j > 0) & (j < N_CHIP - 1))
        def _():
            acc[...] += prod

        @pl.when(j == N_CHIP - 1)
        def _():
            o_ref[...] = acc[...] + prod

    tile = pl.BlockSpec((tm, D_MODEL), lambda i, j: (i, 0))
    res = _hosted_call(body, comm,
                       lambda: (pl.program_id(0) == 0) & (pl.program_id(1) == 0),
                       lambda: (pl.program_id(0) == ni - 1) & (pl.program_id(1) == N_CHIP - 1),
                       name="in_proj_xgrad", out_shape=(jax.ShapeDtypeStruct((t, D_MODEL), F32),),
                       grid=(ni, N_CHIP),
                       in_specs=[pl.BlockSpec((None, tm, 2 * D_MODEL), lambda i, j: (_dh_slot(j), i, 0)),
                                 pl.BlockSpec((None, D_MODEL, 2 * D_MODEL), lambda i, j: (j, 0, 0)),
                                 tile],
                       out_specs=(tile,), scratch=[pltpu.VMEM((tm, D_MODEL), F32)],
                       sem=("arbitrary", "arbitrary"), args=[dh, win_st, dr1])
    return res[0], res[1:]


def _sgu_mixed(v, wm_ref, bsb_ref, gv, bv):
    gl, dgl = _gelu_and_grad(v)
    vh, rstd = _ln_stats(gl)
    vn = vh * gv + bv
    mixed = []
    for g in range(N_GROUP):
        sl = slice(g * 128, (g + 1) * 128)
        mixed.append(_dot(wm_ref[g], vn[:, sl]) + bsb_ref[g])
    return dgl, vh, rstd, vn, mixed


def _sgu_fwd(h, wm, bsb, gv, bv, comm=None):
    t = h.shape[0]

    def body(u_ref, v_ref, wm_ref, bsb_ref, gv_ref, bv_ref, ya_ref):
        for bb in range(SGU_STEP_BLOCKS):
            rows = slice(bb * SGU_BLOCK, (bb + 1) * SGU_BLOCK)
            u = u_ref[rows, :].astype(F32)
            _, _, _, _, mixed = _sgu_mixed(v_ref[rows, :].astype(F32), wm_ref, bsb_ref, gv_ref[...],
                                           bv_ref[...])
            gu = _gelu(u)
            for g in range(N_GROUP):
                sl = slice(g * 128, (g + 1) * 128)
                ya_ref[rows, sl] = (gu[:, sl] * mixed[g]).astype(BF16)

    full3 = pl.BlockSpec((N_GROUP, 128, 128), lambda i: (0, 0, 0))
    vec = pl.BlockSpec((1, D_MODEL), lambda i: (0, 0))
    (ya,), extra = _grid1_call(
        body, comm, t // SGU_ROWS, name="sgu_fwd",
        out_shape=(jax.ShapeDtypeStruct((t, D_MODEL), BF16),),
        in_specs=[pl.BlockSpec((SGU_ROWS, D_MODEL), lambda i: (i, 0)),
                  pl.BlockSpec((SGU_ROWS, D_MODEL), lambda i: (i, 1)),
                  full3, full3, vec, vec],
        out_specs=(pl.BlockSpec((SGU_ROWS, D_MODEL), lambda i: (i, 0)),),
        scratch=[], args=(h, h, wm, bsb, gv, bv))
    return ya, extra


def _sgu_bwd(h, dya, wm, wmt, bsb, gv, bv, maskf, dh_buf, comm=None):
    t = h.shape[0]
    nb = t // SGU_ROWS

    def body(u_ref, v_ref, dya_ref, wm_ref, wmt_ref, bsb_ref, gv_ref, bv_ref, mask_ref, dh_buf_ref,
             dh_ref, dws_ref, dbs_ref, dgv_ref, dbv_ref, dmix_acc):
        i = pl.program_id(0)

        @pl.when(i == 0)
        def _():
            dws_ref[...] = jnp.zeros_like(dws_ref)
            dgv_ref[...] = jnp.zeros_like(dgv_ref)
            dbv_ref[...] = jnp.zeros_like(dbv_ref)
            dmix_acc[...] = jnp.zeros_like(dmix_acc)

        gvv = gv_ref[...]
        for bb in range(SGU_STEP_BLOCKS):
            rows = slice(bb * SGU_BLOCK, (bb + 1) * SGU_BLOCK)
            u = u_ref[rows, :].astype(F32)
            dgl_v, vh, rstd, vn, mixed = _sgu_mixed(v_ref[rows, :].astype(F32), wm_ref, bsb_ref, gvv,
                                                    bv_ref[...])
            gu, dgl_u = _gelu_and_grad(u)
            dya_v = dya_ref[rows, :].astype(F32)
            dvn_parts = []
            for g in range(N_GROUP):
                sl = slice(g * 128, (g + 1) * 128)
                d_y = dya_v[:, sl]
                dh_ref[rows, sl] = (d_y * mixed[g] * dgl_u[:, sl]).astype(BF16)
                d_mixed = d_y * gu[:, sl]
                dmix_acc[g] += d_mixed
                dws_ref[g] += _dot(d_mixed, vn[:, sl], NT) * mask_ref[...]
                dvn_parts.append(_dot(wmt_ref[g], d_mixed))
            dvn = jnp.concatenate(dvn_parts, axis=1)
            dgv_ref[...] += _colsum8(dvn * vh)
            dbv_ref[...] += _colsum8(dvn)
            d_gl = _ln_bwd(dvn * gvv, vh, rstd)
            dh_ref[rows, D_MODEL:] = (d_gl * dgl_v).astype(BF16)

        @pl.when(i == nb - 1)
        def _():
            rowid = lax.broadcasted_iota(jnp.int32, (8, 128), 0)
            ones = jnp.ones((8, 128), F32)
            acc = jnp.zeros((8, 128), F32)
            for g in range(N_GROUP):
                rs = _dot32(ones, dmix_acc[g], NT)
                acc = jnp.where(rowid == g, rs, acc)
            dbs_ref[...] = acc

    full3 = pl.BlockSpec((N_GROUP, 128, 128), lambda i: (0, 0, 0))
    vec = pl.BlockSpec((1, D_MODEL), lambda i: (0, 0))
    acc8 = pl.BlockSpec((8, D_MODEL), lambda i: (0, 0))
    return _grid1_call(
        body, comm, nb, name="sgu_bwd",
        out_shape=(jax.ShapeDtypeStruct(dh_buf.shape, BF16),
                   jax.ShapeDtypeStruct((N_GROUP, 128, 128), F32),
                   jax.ShapeDtypeStruct((8, 128), F32),
                   jax.ShapeDtypeStruct((8, D_MODEL), F32),
                   jax.ShapeDtypeStruct((8, D_MODEL), F32)),
        in_specs=[pl.BlockSpec((SGU_ROWS, D_MODEL), lambda i: (i, 0)),
                  pl.BlockSpec((SGU_ROWS, D_MODEL), lambda i: (i, 1)),
                  pl.BlockSpec((SGU_ROWS, D_MODEL), lambda i: (i, 0)),
                  full3, full3, full3, vec, vec,
                  pl.BlockSpec((128, 128), lambda i: (0, 0)), ANY],
        out_specs=(pl.BlockSpec((None, SGU_ROWS, 2 * D_MODEL), lambda i: (DH_SLOT[0], i, 0)),
                   full3, pl.BlockSpec((8, 128), lambda i: (0, 0)), acc8, acc8),
        scratch=[pltpu.VMEM((N_GROUP, 128, 128), F32)],
        args=(h, h, dya, wm, wmt, bsb, gv, bv, maskf, dh_buf), aliases={9: 0})


def _tri_masks():
    row = lax.broadcasted_iota(jnp.int32, (CHUNK, CHUNK), 0)
    col = lax.broadcasted_iota(jnp.int32, (CHUNK, CHUNK), 1)
    return col <= row, col >= row


def _heads(v):
    return [v[:, hd * HEAD_DIM:(hd + 1) * HEAD_DIM] for hd in range(N_HEAD)]


def _tri_cumsum(tri_bf, v):
    hi = v.astype(BF16)
    r = v - hi.astype(F32)
    mid = r.astype(BF16)
    lo = (r - mid.astype(F32)).astype(BF16)
    return _dot(tri_bf, hi) + _dot(tri_bf, mid) + _dot(tri_bf, lo)


def _hgrn_chunk(q, fp, ii, lb, st_heads, causal, with_o=True):
    sg = _sig(fp)
    f = lb + (1.0 - lb) * sg
    k = 1.0 - f
    c = _tri_cumsum(causal.astype(BF16), jnp.log(f))
    ec = jnp.exp(c)
    en = jnp.exp(-c)
    sq = _sig(q)
    qt = q * sq * ec
    kt = k * en
    ecl = jnp.exp(c[CHUNK - 1:CHUNK, :])
    kk = kt * ecl
    qtb, ktb, iib, kkb = qt.astype(BF16), kt.astype(BF16), ii.astype(BF16), kk.astype(BF16)
    attn, o = [], []
    for hd, (qh, kh, ih) in enumerate(zip(_heads(qtb), _heads(ktb), _heads(iib))):
        a = jnp.where(causal, _dot(qh, kh, NT), 0.0).astype(BF16)
        attn.append(a)
        if with_o:
            o.append(_dot(a, ih) + _dot(qh, st_heads[hd], NT))
    return dict(sg=sg, f=f, k=k, ec=ec, en=en, sq=sq, ecl=ecl, kk=kk, qtb=qtb, ktb=ktb, iib=iib,
                kkb=kkb, attn=attn, o=o)


def _rms_heads(o_heads):
    rinv = [lax.rsqrt(jnp.mean(o * o, axis=-1, keepdims=True) + RMS_EPS) for o in o_heads]
    return rinv, jnp.concatenate([o * r for o, r in zip(o_heads, rinv)], axis=1)


HG_CHUNKS = 8
HG_ROWS = HG_CHUNKS * CHUNK


def _hgrn_fwd(h, logits, gn, comm=None):
    t = h.shape[0]
    nb = t // HG_ROWS

    def body(q_ref, f_ref, i_ref, og_ref, lg_ref, gn_ref, yb_ref, o_ref, st_ref, state):
        @pl.when(pl.program_id(0) == 0)
        def _():
            state[...] = jnp.zeros_like(state)

        causal, _ = _tri_masks()
        lb = _sig(lg_ref[0:1, :] - lg_ref[1:2, :])
        gnv = gn_ref[...]
        st = [state[hd] for hd in range(N_HEAD)]
        for cc in range(HG_CHUNKS):
            rows = slice(cc * CHUNK, (cc + 1) * CHUNK)
            og = og_ref[rows, :].astype(F32)
            r = _hgrn_chunk(q_ref[rows, :].astype(F32), f_ref[rows, :].astype(F32),
                            i_ref[rows, :].astype(F32), lb, [s.astype(BF16) for s in st], causal)
            o_bf = jnp.concatenate(r["o"], axis=1).astype(BF16)
            o_ref[rows, :] = o_bf
            _, on = _rms_heads(_heads(o_bf.astype(F32)))
            yb_ref[rows, :] = (on * gnv * (og * _sig(og))).astype(BF16)
            for hd in range(N_HEAD):
                st_ref[cc, hd] = st[hd]
            st = [s * e + _dot(ih, kh, TN)
                  for s, e, ih, kh in zip(st, _heads(r["ecl"]), _heads(r["iib"]), _heads(r["kkb"]))]
        for hd in range(N_HEAD):
            state[hd] = st[hd]

    def col(k):
        return pl.BlockSpec((HG_ROWS, D_MODEL), lambda ci: (ci, k))

    return _grid1_call(body, comm, nb, name="hgrn_fwd",
                       out_shape=(jax.ShapeDtypeStruct((t, D_MODEL), BF16),
                                  jax.ShapeDtypeStruct((t, D_MODEL), BF16),
                                  jax.ShapeDtypeStruct((t // CHUNK, N_HEAD, HEAD_DIM, HEAD_DIM), F32)),
                       in_specs=[col(2), col(3), col(4), col(5),
                                 pl.BlockSpec((2, D_MODEL), lambda ci: (0, 0)),
                                 pl.BlockSpec((1, D_MODEL), lambda ci: (0, 0))],
                       out_specs=(pl.BlockSpec((HG_ROWS, D_MODEL), lambda ci: (ci, 0)),
                                  pl.BlockSpec((HG_ROWS, D_MODEL), lambda ci: (ci, 0)),
                                  pl.BlockSpec((HG_CHUNKS, N_HEAD, HEAD_DIM, HEAD_DIM),
                                               lambda ci: (ci, 0, 0, 0))),
                       scratch=[pltpu.VMEM((N_HEAD, HEAD_DIM, HEAD_DIM), F32)],
                       args=(h, h, h, h, logits, gn))


def _hgrn_chunk_bwd(q, fp, ii, og, o_saved, dy, gnv, lb, st, dsn, causal, anti):
    stb = [s.astype(BF16) for s in st]
    dsnb = [s.astype(BF16) for s in dsn]
    r = _hgrn_chunk(q, fp, ii, lb, stb, causal, with_o=False)
    rinv, on = _rms_heads(_heads(o_saved))
    so = _sig(og)
    sil = og * so
    d_og = dy * on * gnv * (so * (1.0 + og * (1.0 - so)))
    d_on = dy * gnv * sil
    d_ob = jnp.concatenate(
        [ri * (dn - oh * jnp.mean(dn * oh, axis=-1, keepdims=True))
         for ri, dn, oh in zip(rinv, _heads(d_on), _heads(on))], axis=1).astype(BF16)
    d_i, d_qt, d_kt, d_kk, d_st, st_dsn = [], [], [], [], [], []
    ecl = _heads(r["ecl"])
    for hd, (dh, qh, kh, ih, kkh) in enumerate(zip(_heads(d_ob), _heads(r["qtb"]), _heads(r["ktb"]),
                                                   _heads(r["iib"]), _heads(r["kkb"]))):
        d_attn = jnp.where(causal, _dot(dh, ih, NT), 0.0).astype(BF16)
        d_i.append(_dot(r["attn"][hd], dh, TN) + _dot(kkh, dsnb[hd], NT))
        d_qt.append(_dot(d_attn, kh) + _dot(dh, stb[hd]))
        d_kt.append(_dot(d_attn, qh, TN))
        d_kk.append(_dot(ih, dsnb[hd]))
        d_st.append(_dot(dh, qh, TN) + dsn[hd] * ecl[hd])
        st_dsn.append(jnp.sum(st[hd] * dsn[hd], axis=0, keepdims=True))
    d_qt = jnp.concatenate(d_qt, axis=1)
    d_kt = jnp.concatenate(d_kt, axis=1)
    d_kk = jnp.concatenate(d_kk, axis=1)
    kk = r["kk"]
    d_cl = r["ecl"] * jnp.concatenate(st_dsn, axis=1) + jnp.sum(kk * d_kk, axis=0, keepdims=True)
    d_k = (d_kk * r["ecl"] + d_kt) * r["en"]
    d_c = d_qt * r["qtb"].astype(F32) - d_kt * r["ktb"].astype(F32) - d_kk * kk
    rowid = lax.broadcasted_iota(jnp.int32, (CHUNK, D_MODEL), 0)
    d_c = d_c + jnp.where(rowid == CHUNK - 1, d_cl, 0.0)
    d_lf = _tri_cumsum(anti.astype(BF16), d_c)
    d_f = d_lf / r["f"] - d_k
    sg, sq = r["sg"], r["sq"]
    d_q = d_qt * r["ec"] * (sq * (1.0 + q * (1.0 - sq)))
    d_fp = d_f * (1.0 - lb) * sg * (1.0 - sg)
    return (d_q, d_fp, jnp.concatenate(d_i, axis=1), d_og, d_st,
            _colsum8(dy * on * sil), _colsum8(d_f * (1.0 - sg)))


def _hgrn_bwd(h, o_all, dyb, st_all, logits, gn, dh_buf, comm=None):
    t = h.shape[0]
    nb = t // HG_ROWS

    def body(q_ref, f_ref, i_ref, og_ref, o_ref, dyb_ref, st_ref, lg_ref, gn_ref, dh_buf_ref,
             dh_ref, dlb_ref, dgn_ref, dstate):
        @pl.when(pl.program_id(0) == 0)
        def _():
            dstate[...] = jnp.zeros_like(dstate)
            dlb_ref[...] = jnp.zeros_like(dlb_ref)
            dgn_ref[...] = jnp.zeros_like(dgn_ref)

        causal, anti = _tri_masks()
        lb = _sig(lg_ref[0:1, :] - lg_ref[1:2, :])
        gnv = gn_ref[...]
        dsn = [dstate[hd] for hd in range(N_HEAD)]
        dgn_acc = jnp.zeros((8, D_MODEL), F32)
        dlb_acc = jnp.zeros((8, D_MODEL), F32)
        for cc in reversed(range(HG_CHUNKS)):
            rows = slice(cc * CHUNK, (cc + 1) * CHUNK)
            d_q, d_fp, d_i, d_og, dsn, dgn_c, dlb_c = _hgrn_chunk_bwd(
                q_ref[rows, :].astype(F32), f_ref[rows, :].astype(F32), i_ref[rows, :].astype(F32),
                og_ref[rows, :].astype(F32), o_ref[rows, :].astype(F32), dyb_ref[rows, :].astype(F32), gnv, lb,
                [st_ref[cc, hd] for hd in range(N_HEAD)], dsn, causal, anti)
            dgn_acc = dgn_acc + dgn_c
            dlb_acc = dlb_acc + dlb_c
            dh_ref[0, rows, :D_MODEL] = d_q.astype(BF16)
            dh_ref[0, rows, D_MODEL:] = d_fp.astype(BF16)
            dh_ref[1, rows, :D_MODEL] = d_i.astype(BF16)
            dh_ref[1, rows, D_MODEL:] = d_og.astype(BF16)
        dgn_ref[...] += dgn_acc
        dlb_ref[...] += dlb_acc
        for hd in range(N_HEAD):
            dstate[hd] = dsn[hd]

    def col(k):
        return pl.BlockSpec((HG_ROWS, D_MODEL), lambda ci: (nb - 1 - ci, k))

    acc8 = pl.BlockSpec((8, D_MODEL), lambda ci: (0, 0))
    pair = pl.BlockSpec((2, HG_ROWS, 2 * D_MODEL), lambda ci: (0, nb - 1 - ci, 0))
    return _grid1_call(body, comm, nb, name="hgrn_bwd",
                       out_shape=(jax.ShapeDtypeStruct(dh_buf.shape, BF16),
                                  jax.ShapeDtypeStruct((8, D_MODEL), F32),
                                  jax.ShapeDtypeStruct((8, D_MODEL), F32)),
                       in_specs=[col(2), col(3), col(4), col(5), col(0),
                                 pl.BlockSpec((HG_ROWS, D_MODEL), lambda ci: (nb - 1 - ci, 0)),
                                 pl.BlockSpec((HG_CHUNKS, N_HEAD, HEAD_DIM, HEAD_DIM),
                                              lambda ci: (nb - 1 - ci, 0, 0, 0)),
                                 pl.BlockSpec((2, D_MODEL), lambda ci: (0, 0)),
                                 pl.BlockSpec((1, D_MODEL), lambda ci: (0, 0)), ANY],
                       out_specs=(pair, acc8, acc8),
                       scratch=[pltpu.VMEM((N_HEAD, HEAD_DIM, HEAD_DIM), F32)],
                       args=(h, h, h, h, o_all, dyb, st_all, logits, gn, dh_buf), aliases={9: 0})


def _mix_fwd(ya, yb, h, x, wb0, wb1, wo, g1, b1, tm, comm=None):
    t = x.shape[0]

    def body(ya_ref, yb_ref, ga_ref, gb_ref, x_ref, wb0_ref, wb1_ref, wo_ref, g1_ref, b1_ref,
             r1_ref, a_ref, b_ref, m_ref, x1_ref):
        a = _dot(ya_ref[...], wb0_ref[...])
        b = _dot(yb_ref[...], wb1_ref[...])
        m = _sig(ga_ref[...].astype(F32)) * a + _sig(gb_ref[...].astype(F32)) * b
        r1 = ALPHA * x_ref[...] + _dot(m, wo_ref[...])
        xh, _ = _ln_stats(r1)
        r1_ref[...] = r1
        a_ref[...] = a.astype(BF16)
        b_ref[...] = b.astype(BF16)
        m_ref[...] = m.astype(BF16)
        x1_ref[...] = (xh * g1_ref[...] + b1_ref[...]).astype(BF16)

    tile = pl.BlockSpec((tm, D_MODEL), lambda i: (i, 0))
    wsp = pl.BlockSpec((D_MODEL, D_MODEL), lambda i: (0, 0))
    vec = pl.BlockSpec((1, D_MODEL), lambda i: (0, 0))
    f32o = jax.ShapeDtypeStruct((t, D_MODEL), F32)
    bfo = jax.ShapeDtypeStruct((t, D_MODEL), BF16)
    return _grid1_call(body, comm, t // tm, name="mix_fwd", out_shape=(f32o, bfo, bfo, bfo, bfo),
                       in_specs=[tile, tile,
                                 pl.BlockSpec((tm, D_MODEL), lambda i: (i, 6)),
                                 pl.BlockSpec((tm, D_MODEL), lambda i: (i, 7)),
                                 tile, wsp, wsp, wsp, vec, vec],
                       out_specs=(tile, tile, tile, tile, tile),
                       scratch=[], args=(ya, yb, h, h, x, wb0, wb1, wo, g1, b1))


def _mix_bwd(dr1, h, a, b, wo, wb0, wb1, tm):
    t = dr1.shape[0]

    def body(dr1_ref, ga_ref, gb_ref, a_ref, b_ref, wo_ref, wb0_ref, wb1_ref,
             da_ref, db_ref, dh3_ref, dya_ref, dyb_ref):
        d_m = _dot(dr1_ref[...], wo_ref[...], NT)
        sa = _sig(ga_ref[...].astype(F32))
        sb = _sig(gb_ref[...].astype(F32))
        d_a = (d_m * sa).astype(BF16)
        d_b = (d_m * sb).astype(BF16)
        da_ref[...] = d_a
        db_ref[...] = d_b
        dh3_ref[:, :D_MODEL] = (d_m * a_ref[...].astype(F32) * sa * (1.0 - sa)).astype(BF16)
        dh3_ref[:, D_MODEL:] = (d_m * b_ref[...].astype(F32) * sb * (1.0 - sb)).astype(BF16)
        dya_ref[...] = _dot(d_a, wb0_ref[...], NT).astype(BF16)
        dyb_ref[...] = _dot(d_b, wb1_ref[...], NT).astype(BF16)

    tile = pl.BlockSpec((tm, D_MODEL), lambda i: (i, 0))
    wsp = pl.BlockSpec((D_MODEL, D_MODEL), lambda i: (0, 0))
    f32o = jax.ShapeDtypeStruct((t, D_MODEL), F32)
    bfo = jax.ShapeDtypeStruct((t, D_MODEL), BF16)
    return _pc(body, name="mix_bwd",
               out_shape=(bfo, bfo, jax.ShapeDtypeStruct((N_CHIP, t, 2 * D_MODEL), BF16), bfo, bfo),
               grid=(t // tm,),
               in_specs=[tile,
                         pl.BlockSpec((tm, D_MODEL), lambda i: (i, 6)),
                         pl.BlockSpec((tm, D_MODEL), lambda i: (i, 7)),
                         tile, tile, wsp, wsp, wsp],
               out_specs=(tile, tile, pl.BlockSpec((None, tm, 2 * D_MODEL), lambda i: (DH_SLOT[3], i, 0)),
                          tile, tile),
               sem=("parallel",))(dr1, h, h, a, b, wo, wb0, wb1)


FF_TILE = 1408
FF_NJ = D_FF // FF_TILE


def _shift_down(v, k):
    return pltpu.roll(v, k, 0)


def _shift_up(v, k):
    return pltpu.roll(v, v.shape[0] - k, 0)


HALO = 16
FF_PIECES = ((0, 768), (768, FF_TILE))


def _ffn_up_act(x1b, wup_st, convw, convb, tm, comm):
    t = x1b.shape[0]
    ni = t // tm
    nth = tm // HALO

    def body(x_ref, xp_ref, wg_ref, wv_ref, cw_ref, cb_ref, h2_ref, act_ref):
        wg = wg_ref[...]
        gate = _dot(x_ref[...], wg).astype(BF16)
        val = _dot(x_ref[...], wv_ref[...]).astype(BF16)
        prev = (_dot(xp_ref[...], wg) * (pl.program_id(0) > 0).astype(F32)).astype(BF16)
        h2_ref[0] = gate
        h2_ref[1] = val
        ext = jnp.concatenate([prev.astype(F32), gate.astype(F32)], axis=0)
        gc = (cw_ref[0:1, :] * _shift_down(ext, 2) + cw_ref[1:2, :] * _shift_down(ext, 1)
              + cw_ref[2:3, :] * ext + cb_ref[...])[HALO:, :].astype(BF16)
        h2_ref[2] = gc
        act_ref[...] = (_gelu(gc.astype(F32)) * val.astype(F32)).astype(BF16)

    res = _hosted_call(
        body, comm,
        lambda: (pl.program_id(0) == 0) & (pl.program_id(1) == 0),
        lambda: (pl.program_id(0) == ni - 1) & (pl.program_id(1) == FF_NJ - 1),
        mid=lambda: (pl.program_id(0) == max(ni - 2, 0)) & (pl.program_id(1) == 0),
        name="ffn_up",
        out_shape=(jax.ShapeDtypeStruct((3, t, D_FF), BF16), jax.ShapeDtypeStruct((t, D_FF), BF16)),
        grid=(ni, FF_NJ),
        in_specs=[pl.BlockSpec((tm, D_MODEL), lambda i, j: (i, 0)),
                  pl.BlockSpec((HALO, D_MODEL), lambda i, j: (jnp.maximum(i * nth - 1, 0), 0)),
                  pl.BlockSpec((None, D_MODEL, FF_TILE), lambda i, j: (j, 0, 0)),
                  pl.BlockSpec((None, D_MODEL, FF_TILE), lambda i, j: (j + FF_NJ, 0, 0)),
                  pl.BlockSpec((3, FF_TILE), lambda i, j: (0, j)),
                  pl.BlockSpec((1, FF_TILE), lambda i, j: (0, j))],
        out_specs=(pl.BlockSpec((3, tm, FF_TILE), lambda i, j: (0, i, j)),
                   pl.BlockSpec((tm, FF_TILE), lambda i, j: (i, j))),
        scratch=[], sem=("arbitrary", "arbitrary"),
        args=(x1b, x1b, wup_st, wup_st, convw, convb))
    return res[0], res[1], res[2:]


def _out_fwd_bwd(act, x1b, r1, p2, tgt, wd, wpg, wpp, g1, b1, g2, b2, tm):
    t = r1.shape[0]

    def body(act_ref, x1b_ref, r1_ref, p_ref, tgt_ref, wd_ref, wpg_ref, wpp_ref,
             g1_ref, b1_ref, g2_ref, b2_ref,
             dr2_ref, dpg_ref, dpp_ref, loss_ref, dg2_ref, db2_ref):
        i = pl.program_id(0)

        @pl.when(i == 0)
        def _():
            loss_ref[...] = jnp.zeros_like(loss_ref)
            dg2_ref[...] = jnp.zeros_like(dg2_ref)
            db2_ref[...] = jnp.zeros_like(db2_ref)

        ffn = _dot(act_ref[...], wd_ref[...])
        pg = _dot(x1b_ref[...], wpg_ref[...])
        pp = _dot(p_ref[...], wpp_ref[...])
        s = _sig(pg)
        xh1, _ = _ln_stats(r1_ref[...])
        x1 = xh1 * g1_ref[...] + b1_ref[...]
        r2 = ALPHA * x1 + ffn + s * pp
        xh2, rstd2 = _ln_stats(r2)
        g2v = g2_ref[...]
        diff = xh2 * g2v + b2_ref[...] - tgt_ref[...]
        part = jnp.sum(jnp.sum(diff * diff, axis=1, keepdims=True), axis=0, keepdims=True)
        loss_ref[...] += jnp.broadcast_to(part * (0.5 / D_MODEL), loss_ref.shape)
        dy = diff * (1.0 / D_MODEL)
        dg2_ref[...] += _colsum8(dy * xh2)
        db2_ref[...] += _colsum8(dy)
        dr2 = _ln_bwd(dy * g2v, xh2, rstd2)
        dr2_ref[...] = dr2.astype(BF16)
        dpg_ref[...] = (dr2 * pp * s * (1.0 - s)).astype(BF16)
        dpp_ref[...] = (dr2 * s).astype(BF16)

    tile = pl.BlockSpec((tm, D_MODEL), lambda i: (i, 0))
    vec = pl.BlockSpec((1, D_MODEL), lambda i: (0, 0))
    acc8 = pl.BlockSpec((8, D_MODEL), lambda i: (0, 0))
    acc_shape = jax.ShapeDtypeStruct((8, D_MODEL), F32)
    return _pc(body, name="out_fwd_bwd",
               out_shape=(jax.ShapeDtypeStruct((t, D_MODEL), BF16),
                          jax.ShapeDtypeStruct((t, D_MODEL), BF16),
                          jax.ShapeDtypeStruct((t, D_MODEL), BF16),
                          acc_shape, acc_shape, acc_shape),
               grid=(t // tm,),
               in_specs=[pl.BlockSpec((tm, D_FF), lambda i: (i, 0)), tile, tile,
                         pl.BlockSpec((tm, PLE_DIM), lambda i: (i, 0)), tile,
                         pl.BlockSpec((D_FF, D_MODEL), lambda i: (0, 0)),
                         pl.BlockSpec((D_MODEL, D_MODEL), lambda i: (0, 0)),
                         pl.BlockSpec((PLE_DIM, D_MODEL), lambda i: (0, 0)),
                         vec, vec, vec, vec],
               out_specs=(tile, tile, tile, acc8, acc8, acc8),
               sem=("arbitrary",))(act, x1b, r1, p2, tgt, wd, wpg, wpp, g1, b1, g2, b2)


def _ffn_bwd(h2, dr2, wd, wup_st, dpg, wpg, r1, g1, convw, tm):
    t = r1.shape[0]
    ni = t // tm
    nth = tm // HALO
    last_halo = t // HALO - 1
    main_rows = slice(0, tm)

    def body(g_ref, gc_ref, gcn_ref, v_ref, vn_ref, dr2_ref, dr2n_ref, wd_ref, wug_ref, wuv_ref,
             cw_ref, dpg_ref, wpg_ref, r1_ref, g1_ref,
             dh2_ref, dr1_ref, dcw_ref, dcb_ref, dg1_ref, db1_ref, acc):
        i = pl.program_id(0)
        j = pl.program_id(1)

        @pl.when((i == 0) & (j == 0))
        def _():
            dcw_ref[...] = jnp.zeros_like(dcw_ref)
            dcb_ref[...] = jnp.zeros_like(dcb_ref)
            dg1_ref[...] = jnp.zeros_like(dg1_ref)
            db1_ref[...] = jnp.zeros_like(db1_ref)

        dr2v = dr2_ref[...].astype(BF16)
        dr2n = dr2n_ref[...].astype(BF16)
        more = (i < ni - 1).astype(F32)
        prod = None
        dcw_parts, dcb_parts = [], []
        for c0, c1 in FF_PIECES:
            pc = slice(c0, c1)
            da = _dot(dr2v, wd_ref[pc, :], NT)
            dnext = _dot(dr2n, wd_ref[pc, :], NT) * more
            gc = jnp.concatenate([gc_ref[:, pc].astype(F32), gcn_ref[:, pc].astype(F32)], axis=0)
            vext = jnp.concatenate([v_ref[:, pc].astype(F32), vn_ref[:, pc].astype(F32)], axis=0)
            dext = jnp.concatenate([da, dnext], axis=0)
            gl, dgl = _gelu_and_grad(gc)
            d_gc = dext * vext * dgl
            up1 = _shift_up(d_gc, 1)[main_rows, :]
            up2 = _shift_up(d_gc, 2)[main_rows, :]
            dm = d_gc[main_rows, :]
            d_gate = (cw_ref[2:3, pc] * dm + cw_ref[1:2, pc] * up1 + cw_ref[0:1, pc] * up2).astype(BF16)
            d_val = (da * gl[main_rows, :]).astype(BF16)
            dh2_ref[0, :, pc] = d_gate
            dh2_ref[1, :, pc] = d_val
            g = g_ref[:, pc].astype(F32)
            s0 = jnp.sum(g * up2, axis=0, keepdims=True)
            s1 = jnp.sum(g * up1, axis=0, keepdims=True)
            s2 = jnp.sum(g * dm, axis=0, keepdims=True)
            rowid = lax.broadcasted_iota(jnp.int32, (8, c1 - c0), 0)
            dcw_parts.append(jnp.where(rowid == 0, s0, jnp.where(rowid == 1, s1,
                                                                 jnp.where(rowid == 2, s2, 0.0))))
            dcb_parts.append(_colsum8(dm))
            part = _dot(d_gate, wug_ref[:, pc], NT) + _dot(d_val, wuv_ref[:, pc], NT)
            prod = part if prod is None else prod + part
        dcw_part = jnp.concatenate(dcw_parts, axis=1)
        dcb_part = jnp.concatenate(dcb_parts, axis=1)
        for jj in range(FF_NJ):
            @pl.when(j == jj)
            def _(jj=jj):
                cols = slice(jj * FF_TILE, (jj + 1) * FF_TILE)
                dcw_ref[:, cols] += dcw_part
                dcb_ref[:, cols] += dcb_part

        @pl.when(j == 0)
        def _():
            acc[...] = prod

        @pl.when(j > 0)
        def _():
            acc[...] += prod

        @pl.when(j == FF_NJ - 1)
        def _():
            d_x1 = acc[...] + _dot(dpg_ref[...], wpg_ref[...], NT) + ALPHA * dr2_ref[...].astype(F32)
            xh, rstd = _ln_stats(r1_ref[...])
            dg1_ref[...] += _colsum8(d_x1 * xh)
            db1_ref[...] += _colsum8(d_x1)
            dr1_ref[...] = _ln_bwd(d_x1 * g1_ref[...], xh, rstd).astype(BF16)

    def h2_main(part):
        return pl.BlockSpec((None, tm, FF_TILE), lambda i, j: (part, i, j))

    def h2_next(part):
        return pl.BlockSpec((None, HALO, FF_TILE),
                            lambda i, j: (part, jnp.minimum((i + 1) * nth, last_halo), j))

    tile = pl.BlockSpec((tm, D_MODEL), lambda i, j: (i, 0))
    acc8 = pl.BlockSpec((8, D_MODEL), lambda i, j: (0, 0))
    accff = pl.BlockSpec((8, D_FF), lambda i, j: (0, 0))
    acc_shape = jax.ShapeDtypeStruct((8, D_MODEL), F32)
    accff_shape = jax.ShapeDtypeStruct((8, D_FF), F32)
    return _pc(body, name="ffn_bwd",
               out_shape=(jax.ShapeDtypeStruct((2, t, D_FF), BF16),
                          jax.ShapeDtypeStruct((t, D_MODEL), BF16),
                          accff_shape, accff_shape, acc_shape, acc_shape),
               grid=(ni, FF_NJ),
               in_specs=[h2_main(0), h2_main(2), h2_next(2), h2_main(1), h2_next(1),
                         tile,
                         pl.BlockSpec((HALO, D_MODEL), lambda i, j: (jnp.minimum((i + 1) * nth, last_halo), 0)),
                         pl.BlockSpec((FF_TILE, D_MODEL), lambda i, j: (j, 0)),
                         pl.BlockSpec((None, D_MODEL, FF_TILE), lambda i, j: (j, 0, 0)),
                         pl.BlockSpec((None, D_MODEL, FF_TILE), lambda i, j: (j + FF_NJ, 0, 0)),
                         pl.BlockSpec((3, FF_TILE), lambda i, j: (0, j)),
                         tile, pl.BlockSpec((D_MODEL, D_MODEL), lambda i, j: (0, 0)),
                         tile, pl.BlockSpec((1, D_MODEL), lambda i, j: (0, 0))],
               out_specs=(pl.BlockSpec((2, tm, FF_TILE), lambda i, j: (0, i, j)),
                          tile, accff, accff, acc8, acc8),
               scratch=[pltpu.VMEM((tm, D_MODEL), F32)],
               sem=("arbitrary", "arbitrary"))(h2, h2, h2, h2, h2, dr2, dr2, wd, wup_st, wup_st,
                                               convw, dpg, wpg, r1, g1)


ANY = pl.BlockSpec(memory_space=pl.ANY)


def _chip_peers():
    x, y, c = lax.axis_index("x"), lax.axis_index("y"), lax.axis_index("c")
    return x, y, c, [(1 - x, y), (x, 1 - y), (1 - x, 1 - y)]


def _gather_comm(halved, whole=(), blocks=None):
    n, nw = len(halved), len(whole)
    blocks = blocks or {}

    def at(ti, ref, chip, *rest):
        return ref.at[(chip, blocks[ti][1]) + rest] if ti in blocks else ref.at[(chip,) + rest]

    def copies(ins, outs, sems):
        ici_send, ici_recv, d2d_send, d2d_recv, own_send, own_recv = sems
        x, y, c, peers = _chip_peers()
        me = 2 * x + y
        sibling = (x, y, 1 - c)
        own, ici, ici_wait, fwd, fwd_wait = [], [], [], [], []
        for ti in range(n + nw):
            src, dst = ins[ti], outs[ti]
            own.append(pltpu.make_async_remote_copy(
                src_ref=src, dst_ref=at(ti, dst, me), send_sem=own_send.at[ti], recv_sem=own_recv.at[ti],
                device_id=sibling, device_id_type=MESH))
            for k, (px, py) in enumerate(peers):
                pk = 2 * px + py
                sem = dict(send_sem=ici_send.at[ti * 3 + k], recv_sem=ici_recv.at[ti * 3 + k],
                           device_id=(px, py, c), device_id_type=MESH)
                if ti < n:
                    ici.append(pltpu.make_async_remote_copy(src_ref=src.at[c], dst_ref=at(ti, dst, me, c), **sem))
                    ici_wait.append(pltpu.make_async_remote_copy(src_ref=src.at[c], dst_ref=at(ti, dst, pk, c),
                                                                 **sem))
                    dsem = dict(send_sem=d2d_send.at[ti * 3 + k], recv_sem=d2d_recv.at[ti * 3 + k],
                                device_id=sibling, device_id_type=MESH)
                    fwd.append(pltpu.make_async_remote_copy(src_ref=at(ti, dst, pk, c), dst_ref=at(ti, dst, pk, c),
                                                            **dsem))
                    fwd_wait.append(pltpu.make_async_remote_copy(
                        src_ref=at(ti, dst, pk, 1 - c), dst_ref=at(ti, dst, pk, 1 - c), **dsem))
                else:
                    ici.append(pltpu.make_async_remote_copy(src_ref=src, dst_ref=dst.at[me], **sem))
                    ici_wait.append(pltpu.make_async_remote_copy(src_ref=src, dst_ref=dst.at[pk], **sem))
        return own, ici, ici_wait, fwd, fwd_wait

    def start(ins, outs, sems):
        own, ici, _, _, _ = copies(ins, outs, sems)
        for cp in own + ici:
            cp.start()

    def finish(ins, outs, sems):
        own, ici, ici_wait, fwd, fwd_wait = copies(ins, outs, sems)
        for i, cp in enumerate(ici_wait):
            cp.wait_recv()
            if i < len(fwd):
                fwd[i].start()
        for cp in fwd_wait + own:
            cp.wait_recv()
        for cp in own + ici + fwd:
            cp.wait_send()

    def middle(ins, outs, sems):
        _, _, ici_wait, fwd, _ = copies(ins, outs, sems)
        for i, cp in enumerate(ici_wait):
            cp.wait_recv()
            if i < len(fwd):
                fwd[i].start()

    def finish_late(ins, outs, sems):
        own, ici, _, fwd, fwd_wait = copies(ins, outs, sems)
        for cp in fwd_wait + own:
            cp.wait_recv()
        for cp in own + ici + fwd:
            cp.wait_send()

    srcs = list(halved) + list(whole)
    shapes = [jax.ShapeDtypeStruct((N_CHIP,) + ((blocks[ti][0],) if ti in blocks else ()) + s.shape, s.dtype)
              for ti, s in enumerate(srcs)]
    buffers = [(ti, blk[2]) for ti, blk in sorted(blocks.items()) if blk[2] is not None]
    aliases = {len(srcs) + bi: ti for bi, (ti, _) in enumerate(buffers)}
    return _Comm(srcs + [buf for _, buf in buffers], shapes,
                 [pltpu.SemaphoreType.DMA((3 * (n + nw),)), pltpu.SemaphoreType.DMA((3 * (n + nw),)),
                  pltpu.SemaphoreType.DMA((max(3 * n, 1),)), pltpu.SemaphoreType.DMA((max(3 * n, 1),)),
                  pltpu.SemaphoreType.DMA((n + nw,)), pltpu.SemaphoreType.DMA((n + nw,))],
                 start, finish, middle, finish_late, aliases)


def _sibling_exchange_comm(grads):
    n = len(grads)

    def copies(ins, outs, sems):
        send_sems, recv_sems = sems
        x, y, c = lax.axis_index("x"), lax.axis_index("y"), lax.axis_index("c")
        res = []
        for ti in range(n):
            half = ins[ti].shape[1] // 2
            res.append(pltpu.make_async_remote_copy(
                src_ref=ins[ti].at[:, pl.ds(pl.multiple_of((1 - c) * half, 16), half), :],
                dst_ref=outs[ti],
                send_sem=send_sems.at[ti], recv_sem=recv_sems.at[ti],
                device_id=(x, y, 1 - c), device_id_type=MESH))
        return res

    def start(ins, outs, sems):
        for cp in copies(ins, outs, sems):
            cp.start()

    def finish(ins, outs, sems):
        for cp in copies(ins, outs, sems):
            cp.wait()

    return _Comm(grads, [jax.ShapeDtypeStruct((N_CHIP, g.shape[1] // 2, g.shape[2]), g.dtype) for g in grads],
                 [pltpu.SemaphoreType.DMA((n,)), pltpu.SemaphoreType.DMA((n,))], start, finish)


def _in_proj_gathering(x2b, own, chip, tm, comm):
    t = x2b.shape[0]
    ni = t // tm
    half, cols = own.shape[1], own.shape[2]
    nci, nco = len(comm.ins), len(comm.out_shapes)

    def body(chip_ref, x_ref, own_ref, own_hbm, *rest):
        c_in = rest[:nci]
        h_ref, win_out = rest[nci:nci + 2]
        c_out = rest[nci + 2:nci + 2 + nco]
        w_scr, ici_send, ici_recv, d2d_send, d2d_recv, own_sems, ld_sems = rest[nci + 2 + nco:nci + 9 + nco]
        c_sem = rest[nci + 9 + nco:]
        s, i = pl.program_id(0), pl.program_id(1)
        x, y, c, peers = _chip_peers()
        me = 2 * x + y
        sibling = (x, y, 1 - c)

        def ici(k, slot):
            px, py = peers[k]
            return pltpu.make_async_remote_copy(
                src_ref=own_hbm.at[c], dst_ref=win_out.at[slot, c],
                send_sem=ici_send.at[k], recv_sem=ici_recv.at[k],
                device_id=(px, py, c), device_id_type=MESH)

        def forward(k, core):
            pk = 2 * peers[k][0] + peers[k][1]
            return pltpu.make_async_remote_copy(
                src_ref=win_out.at[pk, core], dst_ref=win_out.at[pk, core],
                send_sem=d2d_send.at[k], recv_sem=d2d_recv.at[k],
                device_id=sibling, device_id_type=MESH)

        place_own = pltpu.make_async_remote_copy(
            src_ref=own_hbm, dst_ref=win_out.at[me], send_sem=own_sems.at[0], recv_sem=own_sems.at[1],
            device_id=sibling, device_id_type=MESH)

        @pl.when((s == 0) & (i == 0))
        def _():
            for k in range(2):
                ici(k, me).start()
            place_own.start()

        @pl.when(s == 0)
        def _():
            xv = x_ref[...]
            h_ref[...] = (_dot(xv[:, :half], own_ref[0]) + _dot(xv[:, half:], own_ref[1])).astype(BF16)

        for k in range(3):
            @pl.when((s == k + 1) & (i == 0))
            def _(k=k):
                pk = 2 * peers[k][0] + peers[k][1]
                ici(k, pk).wait_recv()
                if k == 0:
                    ici(2, me).start()
                forward(k, c).start()
                forward(k, 1 - c).wait_recv()
                loads = [pltpu.make_async_copy(win_out.at[pk, hh], w_scr.at[hh], ld_sems.at[hh])
                         for hh in range(2)]
                for ld in loads:
                    ld.start()
                for ld in loads:
                    ld.wait()
                if k == 1:
                    comm.start(c_in, c_out, c_sem)

        @pl.when(s > 0)
        def _():
            xv = x_ref[...]
            h_ref[...] = (_dot(xv[:, :half], w_scr[0]) + _dot(xv[:, half:], w_scr[1])).astype(BF16)

        @pl.when((s == N_CHIP - 1) & (i == ni - 1))
        def _():
            place_own.wait()
            for k in range(3):
                ici(k, me).wait_send()
                forward(k, c).wait_send()
            comm.finish(c_in, c_out, c_sem)

    def shard_col(s, me):
        return jnp.where(s == 0, me, me ^ jnp.where(s == 1, 2, jnp.where(s == 2, 1, 3)))

    res = _pc(body, name="in_proj",
              out_shape=(jax.ShapeDtypeStruct((t, N_CHIP * cols), BF16),
                         jax.ShapeDtypeStruct((N_CHIP,) + own.shape, own.dtype)) + tuple(comm.out_shapes),
              grid=(N_CHIP, ni), nsp=1,
              in_specs=[pl.BlockSpec((tm, 2 * half), lambda s, i, chip_ref: (i, 0)),
                        pl.BlockSpec(own.shape, lambda s, i, chip_ref: (0, 0, 0)),
                        ANY] + [ANY] * nci,
              out_specs=(pl.BlockSpec((tm, cols), lambda s, i, chip_ref: (i, shard_col(s, chip_ref[0]))),
                         ANY) + tuple([ANY] * nco),
              scratch=[pltpu.VMEM(own.shape, own.dtype),
                       pltpu.SemaphoreType.DMA((3,)), pltpu.SemaphoreType.DMA((3,)),
                       pltpu.SemaphoreType.DMA((3,)), pltpu.SemaphoreType.DMA((3,)),
                       pltpu.SemaphoreType.DMA((2,)), pltpu.SemaphoreType.DMA((2,))] + comm.sems,
              sem=("arbitrary", "arbitrary"))(chip, x2b, own, own, *comm.ins)
    return res[0], res[1], res[2:]


def _rs_add_halves(name, grad, recv, core):
    _, r, cdim = grad.shape
    half = r // 2
    tr = _row_tile(half, cdim, mult=16)
    nr = half // tr

    def body(c_ref, g_ref, r_ref, o_ref):
        o_ref[...] = (g_ref[...].astype(F32) + r_ref[...].astype(F32)).astype(BF16)

    return _pc(body, name=name, out_shape=jax.ShapeDtypeStruct((N_CHIP, half, cdim), BF16),
               grid=(N_CHIP, nr), nsp=1,
               in_specs=[pl.BlockSpec((None, tr, cdim), lambda j, i, c_ref: (j, c_ref[0] * nr + i, 0)),
                         pl.BlockSpec((None, tr, cdim), lambda j, i, c_ref: (j, i, 0))],
               out_specs=pl.BlockSpec((None, tr, cdim), lambda j, i, c_ref: (j, i, 0)),
               sem=("parallel", "parallel"))(core, grad, recv)


def _chip_exchange_comm(parts):
    n = len(parts)

    def copies(ins, outs, sems):
        send_sems, recv_sems = sems
        x, y, c, peers = _chip_peers()
        return [pltpu.make_async_remote_copy(
            src_ref=ins[ti].at[2 * px + py], dst_ref=outs[ti].at[k],
            send_sem=send_sems.at[ti * 3 + k], recv_sem=recv_sems.at[ti * 3 + k],
            device_id=(px, py, c), device_id_type=MESH)
            for ti in range(n) for k, (px, py) in enumerate(peers)]

    def start(ins, outs, sems):
        for cp in copies(ins, outs, sems):
            cp.start()

    def finish(ins, outs, sems):
        for cp in copies(ins, outs, sems):
            cp.wait()

    return _Comm(parts, [jax.ShapeDtypeStruct((3,) + p.shape[1:], p.dtype) for p in parts],
                 [pltpu.SemaphoreType.DMA((3 * n,)), pltpu.SemaphoreType.DMA((3 * n,))], start, finish)


def _rs_sum_chips(name, part, recv, chip):
    _, half, cdim = recv.shape
    tr = _row_tile(half, cdim, mult=16)

    def body(chip_ref, p_ref, r_ref, o_ref):
        o_ref[...] = ((p_ref[...].astype(F32) + r_ref[0].astype(F32)) + r_ref[1].astype(F32)
                      ) + r_ref[2].astype(F32)

    return _pc(body, name=name, out_shape=jax.ShapeDtypeStruct((half, cdim), F32),
               grid=(half // tr,), nsp=1,
               in_specs=[pl.BlockSpec((None, tr, cdim), lambda i, chip_ref: (chip_ref[0], i, 0)),
                         pl.BlockSpec((3, tr, cdim), lambda i, chip_ref: (0, i, 0))],
               out_specs=pl.BlockSpec((tr, cdim), lambda i, chip_ref: (i, 0)),
               sem=("parallel",))(chip, part, recv)


def _rs_send_halves(halves):
    n = len(halves)

    def body(*refs):
        ins, outs = refs[:n], refs[n:2 * n]
        send_sems, recv_sems = refs[2 * n:]
        x, y, c = lax.axis_index("x"), lax.axis_index("y"), lax.axis_index("c")
        sends = []
        for ti in range(n):
            cp = pltpu.make_async_remote_copy(
                src_ref=ins[ti], dst_ref=outs[ti],
                send_sem=send_sems.at[ti], recv_sem=recv_sems.at[ti],
                device_id=(x, y, 1 - c), device_id_type=MESH)
            cp.start()
            sends.append(cp)
        for cp in sends:
            cp.wait()

    return _pc(body, name="rs_send_halves",
               out_shape=tuple(jax.ShapeDtypeStruct(hv.shape, hv.dtype) for hv in halves),
               in_specs=[ANY] * n, out_specs=tuple([ANY] * n),
               scratch=[pltpu.SemaphoreType.DMA((n,)), pltpu.SemaphoreType.DMA((n,))])(*halves)


def _adamw_rows(name, mine, theirs, w, m, v, core):
    half, cdim = mine.shape
    tr = _row_tile(half, cdim, budget=1 << 19)
    nrh = half // tr

    def body(c_ref, mine_ref, theirs_ref, w_ref, m_ref, v_ref, g_ref, d_ref, m2_ref, v2_ref):
        is_mine = (pl.program_id(0) // nrh) == c_ref[0]
        g = jnp.where(is_mine, mine_ref[...], theirs_ref[...])
        d, m2, v2 = _adamw(w_ref[...], g, m_ref[...], v_ref[...])
        g_ref[...] = g
        d_ref[...] = d
        m2_ref[...] = m2
        v2_ref[...] = v2

    htile = pl.BlockSpec((tr, cdim), lambda i, c_ref: (i % nrh, 0))
    tile = pl.BlockSpec((tr, cdim), lambda i, c_ref: (i, 0))
    shp = jax.ShapeDtypeStruct((2 * half, cdim), F32)
    return _pc(body, name=name, out_shape=(shp, shp, shp, shp), grid=(2 * nrh,), nsp=1,
               in_specs=[htile, htile, tile, tile, tile], out_specs=(tile, tile, tile, tile),
               sem=("parallel",))(core, mine, theirs, w, m, v)


def _adamw_whole(name, g, w, m, v):
    def body(g_ref, w_ref, m_ref, v_ref, d_ref, m2_ref, v2_ref):
        d, m2, v2 = _adamw(w_ref[...], g_ref[...], m_ref[...], v_ref[...])
        d_ref[...] = d
        m2_ref[...] = m2
        v2_ref[...] = v2

    shp = jax.ShapeDtypeStruct(g.shape, F32)
    return _pc(body, name=name, out_shape=(shp, shp, shp))(g, w, m, v)


SMALL_LAYOUT = (
    ("sgu_w_s", 1024, 1, 0),
    ("sgu_b_s", 8, 1, 1024),
    ("sgu_norm_g", 1, 0, 0),
    ("sgu_norm_b", 1, 0, 1),
    ("hgrn_norm_g", 1, 0, 3),
    ("ln1_g", 1, 0, 4),
    ("ln1_b", 1, 0, 5),
    ("ffn_conv_b", 1, 2, 3),
    ("ln2_g", 1, 0, 6),
    ("ln2_b", 1, 0, 7),
)
LB_ROW = 2
LOSS_ROW = 8
PACK_SHAPES = ((16, D_MODEL), (N_GROUP * 128 + 16, 128), (8, D_FF))
GATH_DTYPES = (F32, BF16, F32)


def _small_allreduce_adamw(rows1024, dws, dbs, dcw, dcb, logits, m_logits, v_logits,
                           small_w, small_m, small_v):
    ns = len(SMALL_LAYOUT)
    nr = len(rows1024)
    nb = len(PACK_SHAPES)

    def body(*refs):
        row_refs = refs[:nr]
        dws_ref, dbs_ref, dcw_ref, dcb_ref, lg_ref, mlg_ref, vlg_ref = refs[nr:nr + 7]
        pos = nr + 7
        w_refs = refs[pos:pos + ns]
        m_refs = refs[pos + ns:pos + 2 * ns]
        v_refs = refs[pos + 2 * ns:pos + 3 * ns]
        pos += 3 * ns
        loss_ref, dcw_out = refs[pos:pos + 2]
        lg_outs = refs[pos + 2:pos + 6]
        pos += 6
        outs = refs[pos:pos + 4 * ns]
        pos += 4 * ns
        pack = refs[pos:pos + nb]
        sib = refs[pos + nb:pos + 2 * nb]
        gath = refs[pos + 2 * nb:pos + 3 * nb]
        d2d_send, d2d_recv, ici_send, ici_recv = refs[pos + 3 * nb:]

        x, y, c, peers = _chip_peers()
        me = 2 * x + y
        sibling = (x, y, 1 - c)

        pack[0][...] = jnp.zeros(PACK_SHAPES[0], F32)
        for k in range(nr):
            pack[0][k:k + 1, :] = row_refs[k][0:1, :]
        pack[1][0:N_GROUP * 128, :] = dws_ref[...]
        pack[1][N_GROUP * 128:N_GROUP * 128 + 8, :] = dbs_ref[...]
        pack[1][N_GROUP * 128 + 8:, :] = jnp.zeros((8, 128), F32)
        pack[2][...] = jnp.zeros(PACK_SHAPES[2], F32)
        pack[2][0:3, :] = dcw_ref[0:3, :]
        pack[2][3:4, :] = dcb_ref[0:1, :]

        d2d = [pltpu.make_async_remote_copy(
            src_ref=pack[b], dst_ref=sib[b], send_sem=d2d_send.at[b], recv_sem=d2d_recv.at[b],
            device_id=sibling, device_id_type=MESH) for b in range(nb)]
        for cp in d2d:
            cp.start()
        for cp in d2d:
            cp.wait()
        for b in range(nb):
            gath[b][me] = (pack[b][...] + sib[b][...]).astype(GATH_DTYPES[b])

        ici, ici_wait = [], []
        for b in range(nb):
            for k, (px, py) in enumerate(peers):
                sem = dict(send_sem=ici_send.at[b * 3 + k], recv_sem=ici_recv.at[b * 3 + k],
                           device_id=(px, py, c), device_id_type=MESH)
                ici.append(pltpu.make_async_remote_copy(src_ref=gath[b].at[me], dst_ref=gath[b].at[me], **sem))
                ici_wait.append(pltpu.make_async_remote_copy(
                    src_ref=gath[b].at[me], dst_ref=gath[b].at[2 * px + py], **sem))
        for cp in ici:
            cp.start()
        for cp in ici_wait:
            cp.wait_recv()
        for cp in ici:
            cp.wait_send()

        tot = pack
        for b in range(nb):
            tot[b][...] = ((gath[b][0].astype(F32) + gath[b][1].astype(F32)) + gath[b][2].astype(F32)
                           ) + gath[b][3].astype(F32)

        loss_ref[...] = tot[0][LOSS_ROW:LOSS_ROW + 1, :]
        dcw_out[...] = tot[2][...]
        lb = _sig(lg_ref[0:1, :] - lg_ref[1:2, :])
        d0 = tot[0][LB_ROW:LB_ROW + 1, :] * lb * (1.0 - lb)
        rowid = lax.broadcasted_iota(jnp.int32, (2, D_MODEL), 0)
        g_lg = jnp.where(rowid == 0, d0, -d0)
        dl, ml, vl = _adamw(lg_ref[...], g_lg, mlg_ref[...], vlg_ref[...])
        lg_outs[0][...] = g_lg
        lg_outs[1][...] = dl
        lg_outs[2][...] = ml
        lg_outs[3][...] = vl
        for si, (_, rows, b, r0) in enumerate(SMALL_LAYOUT):
            g = tot[b][r0:r0 + rows, :]
            dl, ml, vl = _adamw(w_refs[si][...], g, m_refs[si][...], v_refs[si][...])
            outs[4 * si][...] = g
            outs[4 * si + 1][...] = dl
            outs[4 * si + 2][...] = ml
            outs[4 * si + 3][...] = vl

    shapes = [jax.ShapeDtypeStruct((1, D_MODEL), F32), jax.ShapeDtypeStruct((8, D_FF), F32)]
    shapes += [jax.ShapeDtypeStruct((2, D_MODEL), F32)] * 4
    for w in small_w:
        shapes += [jax.ShapeDtypeStruct(w.shape, F32)] * 4
    scratch = [pltpu.VMEM(shp, F32) for shp in PACK_SHAPES]
    scratch += [pltpu.VMEM(shp, F32) for shp in PACK_SHAPES]
    scratch += [pltpu.VMEM((N_CHIP,) + shp, dt) for shp, dt in zip(PACK_SHAPES, GATH_DTYPES)]
    scratch += [pltpu.SemaphoreType.DMA((nb,)), pltpu.SemaphoreType.DMA((nb,)),
                pltpu.SemaphoreType.DMA((3 * nb,)), pltpu.SemaphoreType.DMA((3 * nb,))]
    vm = pl.BlockSpec(memory_space=pltpu.VMEM)
    n_in = nr + 7 + 3 * ns
    res = _pc(body, name="small_allreduce_adamw", out_shape=tuple(shapes),
              in_specs=[vm] * n_in, out_specs=tuple([vm] * len(shapes)),
              scratch=scratch)(*rows1024, dws, dbs, dcw, dcb, logits, m_logits, v_logits,
                               *small_w, *small_m, *small_v)
    return res[0], res[1], res[2:6], res[6:]


def kernel(x, p, w_in, sgu_w_s, sgu_b_s, sgu_norm_g, sgu_norm_b, hgrn_lb_logits, hgrn_norm_g, w_branch, w_out, ln1_g, ln1_b, ffn_w_up, ffn_conv_w, ffn_conv_b, ffn_w_down, ln2_g, ln2_b, ple_w_proj, ple_w_gate, loss_target, m_w_in, m_sgu_w_s, m_sgu_b_s, m_sgu_norm_g, m_sgu_norm_b, m_hgrn_lb_logits, m_hgrn_norm_g, m_w_branch, m_w_out, m_ln1_g, m_ln1_b, m_ffn_w_up, m_ffn_conv_w, m_ffn_conv_b, m_ffn_w_down, m_ln2_g, m_ln2_b, m_ple_w_proj, m_ple_w_gate, v_w_in, v_sgu_w_s, v_sgu_b_s, v_sgu_norm_g, v_sgu_norm_b, v_hgrn_lb_logits, v_hgrn_norm_g, v_w_branch, v_w_out, v_ln1_g, v_ln1_b, v_ffn_w_up, v_ffn_conv_w, v_ffn_conv_b, v_ffn_w_down, v_ln2_g, v_ln2_b, v_ple_w_proj, v_ple_w_gate):
    t = x.shape[1]
    x2 = x.reshape(t, D_MODEL)
    x2b = x2.astype(BF16)
    p2 = p.reshape(t, PLE_DIM)
    tgt = loss_target.reshape(t, D_MODEL)
    core = lax.axis_index("c").astype(jnp.int32).reshape(1)
    chip_id = (2 * lax.axis_index("x") + lax.axis_index("y")).astype(jnp.int32).reshape(1)

    big_w = [w_in[0], w_branch[0, 0], w_branch[0, 1], w_out[0], ffn_w_up[0], ffn_w_down[0],
             ple_w_proj[0], ple_w_gate[0]]
    big_m = [m_w_in[0], m_w_branch[0, 0], m_w_branch[0, 1], m_w_out[0], m_ffn_w_up[0],
             m_ffn_w_down[0], m_ple_w_proj[0], m_ple_w_gate[0]]
    big_v = [v_w_in[0], v_w_branch[0, 0], v_w_branch[0, 1], v_w_out[0], v_ffn_w_up[0],
             v_ffn_w_down[0], v_ple_w_proj[0], v_ple_w_gate[0]]
    def halves_of(i):
        w = big_w[i]
        return w.astype(BF16).reshape(2, w.shape[0] // 2, w.shape[1])

    def stacked(g, i):
        return g.reshape(N_CHIP, big_w[i].shape[0], big_w[i].shape[1])


    cid = jnp.arange(SGU_BLOCK) // CHUNK
    maskf = (cid[:, None] >= cid[None, :]).astype(F32)
    ws_masked = sgu_w_s[0] * maskf[None]
    wm = ws_masked.astype(BF16)
    wmt = jnp.transpose(ws_masked, (0, 2, 1)).astype(BF16)
    bsb = jnp.broadcast_to(sgu_b_s[0][:, :, None], (N_GROUP, SGU_BLOCK, 128))

    up_rows = big_w[4].shape[0] // 2
    up_blocks = [big_w[4][k * up_rows:(k + 1) * up_rows].astype(BF16).reshape(2, up_rows // 2, -1)
                 for k in range(2)]
    h, win_g, (up_g,) = _in_proj_gathering(x2b, halves_of(0), chip_id, 512,
                                           _gather_comm([up_blocks[0]], blocks={0: (2, 0, None)}))
    win_st = stacked(win_g, 0)
    ya, _ = _sgu_fwd(h, wm, bsb, sgu_norm_g, sgu_norm_b)
    (yb, o_all, st_all), mix_g = _hgrn_fwd(
        h, hgrn_lb_logits, hgrn_norm_g,
        comm=_gather_comm([halves_of(i) for i in (1, 2, 3)] + [up_blocks[1]], [ffn_conv_w[0]],
                          blocks={3: (2, 1, up_g)}))
    wb0, wb1, wo = [stacked(g, i).reshape(D_MODEL, D_MODEL) for g, i in zip(mix_g[:3], (1, 2, 3))]
    wup_st = stacked(mix_g[3], 4)
    convw = jnp.transpose(mix_g[4], (1, 0, 2)).reshape(3, D_FF)
    (r1, a_br, b_br, m_bf, x1b), _ = _mix_fwd(ya, yb, h, x2, wb0, wb1, wo, ln1_g, ln1_b, 256)
    h2, act, out_g = _ffn_up_act(x1b, wup_st, convw, ffn_conv_b, 512,
                                 _gather_comm([halves_of(i) for i in (5, 6, 7)]))
    wd = stacked(out_g[0], 5).reshape(D_FF, D_MODEL)
    wpp = jnp.transpose(stacked(out_g[1], 6), (1, 0, 2)).reshape(PLE_DIM, D_MODEL)
    wpg = stacked(out_g[2], 7).reshape(D_MODEL, D_MODEL)
    dr2, dpg, dpp, loss_acc, dg2, db2 = _out_fwd_bwd(
        act, x1b, r1, p2, tgt, wd, wpg, wpp, ln1_g, ln1_b, ln2_g, ln2_b, 256)

    dh2, dr1, dcw, dcb, dg1, db1 = _ffn_bwd(h2, dr2, wd, wup_st, dpg, wpg, r1, ln1_g, convw, 256)
    d_wd = _mm_tn("ffn_down_wgrad", act, dr2, FF_TILE, 512)
    d_wpg = _mm_tn("ple_gate_wgrad", x1b, dpg, 512, D_MODEL)
    d_wpp_st = _mm_tn("ple_proj_wgrad", p2, dpp, PLE_DIM, PLE_DIM, stacked=True)
    d_wup_st = _mm("ffn_up_wgrad", x1b, dh2, TN, (2, N_CHIP),
                   pl.BlockSpec((t, 512), lambda i, j: (0, i)),
                   pl.BlockSpec((None, t, FF_TILE), lambda i, j: (j // FF_NJ, 0, j % FF_NJ)),
                   jax.ShapeDtypeStruct((N_CHIP, D_MODEL, FF_TILE), BF16),
                   pl.BlockSpec((None, 512, FF_TILE), lambda i, j: (j, i, 0)))
    da_bf, db_bf, dh, dya, dyb = _mix_bwd(dr1, h, a_br, b_br, wo, wb0, wb1, 256)
    d_wo = _mm_tn("out_proj_wgrad", m_bf, dr1, 512, 512)
    d_wb0 = _mm_tn("branch0_wgrad", ya, da_bf, 512, D_MODEL)
    d_wb1 = _mm_tn("branch1_wgrad", yb, db_bf, 512, D_MODEL)
    grads_1 = [d_wb0.reshape(4, 256, D_MODEL), d_wb1.reshape(4, 256, D_MODEL),
               d_wo.reshape(4, 256, D_MODEL), d_wup_st, d_wd.reshape(4, D_FF // 4, D_MODEL),
               d_wpp_st, d_wpg.reshape(4, 256, D_MODEL)]
    (dh, dws, dbs, dgv, dbv), recv_a1 = _sgu_bwd(h, dya, wm, wmt, bsb, sgu_norm_g, sgu_norm_b, maskf, dh,
                                                 comm=_sibling_exchange_comm(grads_1))
    parts_1 = [_rs_add_halves("rs_add_halves%d" % (i + 1), g, r, core)
               for i, (g, r) in enumerate(zip(grads_1, recv_a1))]
    (dh, dlb, dgn), recv_b1 = _hgrn_bwd(h, o_all, dyb, st_all, hgrn_lb_logits, hgrn_norm_g, dh,
                                         comm=_chip_exchange_comm(parts_1))

    grads_0 = [_in_proj_wgrad(x2b, dh, 512, D_MODEL)]
    recv_a0 = _run_comm("rs_sibling_exchange0", _sibling_exchange_comm(grads_0))
    parts_0 = [_rs_add_halves("rs_add_halves0", grads_0[0], recv_a0[0], core)]
    gx, recv_b0 = _in_proj_xgrad(dh, win_st, dr1, 512, _chip_exchange_comm(parts_0))
    parts = parts_0 + parts_1
    recv_b = list(recv_b0) + list(recv_b1)
    halves = [_rs_sum_chips("rs_sum_chips%d" % i, pt, r, chip_id)
              for i, (pt, r) in enumerate(zip(parts, recv_b))]
    theirs = _rs_send_halves(halves)
    big_out = [_adamw_rows("adamw_big%d" % i, halves[i], theirs[i], big_w[i], big_m[i], big_v[i], core)
               for i in range(len(halves))]

    small_in = dict(sgu_w_s=(sgu_w_s, m_sgu_w_s, v_sgu_w_s), sgu_b_s=(sgu_b_s, m_sgu_b_s, v_sgu_b_s),
                    sgu_norm_g=(sgu_norm_g, m_sgu_norm_g, v_sgu_norm_g),
                    sgu_norm_b=(sgu_norm_b, m_sgu_norm_b, v_sgu_norm_b),
                    hgrn_norm_g=(hgrn_norm_g, m_hgrn_norm_g, v_hgrn_norm_g),
                    ln1_g=(ln1_g, m_ln1_g, v_ln1_g), ln1_b=(ln1_b, m_ln1_b, v_ln1_b),
                    ffn_conv_b=(ffn_conv_b, m_ffn_conv_b, v_ffn_conv_b),
                    ln2_g=(ln2_g, m_ln2_g, v_ln2_g), ln2_b=(ln2_b, m_ln2_b, v_ln2_b))

    def flat(name, arr):
        rows = dict((n, r) for n, r, _, _ in SMALL_LAYOUT)[name]
        return arr.reshape(rows, arr.size // rows)

    names = [n for n, _, _, _ in SMALL_LAYOUT]
    sw = [flat(n, small_in[n][0]) for n in names]
    sm = [flat(n, small_in[n][1]) for n in names]
    sv = [flat(n, small_in[n][2]) for n in names]
    loss_rows, dcw_tot, lg_out, small_out = _small_allreduce_adamw(
        [dgv, dbv, dlb, dgn, dg1, db1, dg2, db2, loss_acc], dws.reshape(N_GROUP * 128, 128), dbs, dcw, dcb,
        hgrn_lb_logits, m_hgrn_lb_logits, v_hgrn_lb_logits, sw, sm, sv)
    loss = loss_rows[0, 0]

    chip = 2 * lax.axis_index("x") + lax.axis_index("y")
    g_cw = lax.dynamic_slice(dcw_tot, (0, chip * (D_FF // 4)), (3, D_FF // 4))
    cw_out = _adamw_whole("adamw_conv_w", g_cw, ffn_conv_w[0], m_ffn_conv_w[0], v_ffn_conv_w[0])

    res = {}
    for si, n in enumerate(names):
        shp = small_in[n][0].shape
        res[n] = tuple(small_out[4 * si + k].reshape(shp) for k in range(4))
    res["hgrn_lb_logits"] = tuple(lg_out)
    res["ffn_conv_w"] = (g_cw[None],) + tuple(o[None] for o in cw_out)

    def big(i):
        return tuple(big_out[i])

    res["w_in"] = tuple(o[None] for o in big(0))
    res["w_branch"] = tuple(jnp.stack([o0, o1])[None] for o0, o1 in zip(big(1), big(2)))
    res["w_out"] = tuple(o[None] for o in big(3))
    res["ffn_w_up"] = tuple(o[None] for o in big(4))
    res["ffn_w_down"] = tuple(o[None] for o in big(5))
    res["ple_w_proj"] = tuple(o[None] for o in big(6))
    res["ple_w_gate"] = tuple(o[None] for o in big(7))

    order = ["w_in", "sgu_w_s", "sgu_b_s", "sgu_norm_g", "sgu_norm_b", "hgrn_lb_logits",
             "hgrn_norm_g", "w_branch", "w_out", "ln1_g", "ln1_b", "ffn_w_up", "ffn_conv_w",
             "ffn_conv_b", "ffn_w_down", "ln2_g", "ln2_b", "ple_w_proj", "ple_w_gate"]
    outs = [loss, gx.reshape(1, t, D_MODEL)]
    for k in range(4):
        outs += [res[n][k] for n in order]
    return tuple(outs)
```

```python
import jax
import jax.numpy as jnp
from jax import lax
from jax.experimental import pallas as pl
from jax.experimental.pallas import tpu as pltpu

F32 = jnp.float32
BF16 = jnp.bfloat16
HIGHEST = lax.Precision.HIGHEST
MESH = pl.DeviceIdType.MESH

D_MODEL = 1024
CHUNK = 64
SGU_BLOCK = 128
SGU_STEP_BLOCKS = 4
SGU_ROWS = SGU_STEP_BLOCKS * SGU_BLOCK
N_GROUP = 8
N_HEAD = 8
HEAD_DIM = 128
D_FF = 2816
PLE_DIM = 256
LN_EPS = 1e-5
RMS_EPS = 1e-6
ALPHA = 2.0 ** 0.25
N_CHIP = 4

ADAM_LR = 0.001
ADAM_B1 = 0.9
ADAM_B2 = 0.999
ADAM_EPS = 1e-08
ADAM_WD = 0.01
ADAM_STEP = 10

VMEM_LIMIT = 56 * 1024 * 1024

NN = (((1,), (0,)), ((), ()))
NT = (((1,), (1,)), ((), ()))
TN = (((0,), (0,)), ((), ()))


def _pc(body, *, name, out_shape, grid=None, in_specs=None, out_specs=None, scratch=(),
        sem=None, nsp=0, vmem=VMEM_LIMIT, aliases=None):
    params = dict(vmem_limit_bytes=vmem)
    if sem is not None:
        params["dimension_semantics"] = sem
    kw = dict(name=name, out_shape=out_shape, compiler_params=pltpu.CompilerParams(**params))
    if aliases:
        kw["input_output_aliases"] = aliases
    if nsp:
        kw["grid_spec"] = pltpu.PrefetchScalarGridSpec(
            num_scalar_prefetch=nsp, grid=grid, in_specs=in_specs, out_specs=out_specs,
            scratch_shapes=list(scratch))
    else:
        if grid is not None:
            kw["grid"] = grid
        if in_specs is not None:
            kw["in_specs"] = in_specs
            kw["out_specs"] = out_specs
        kw["scratch_shapes"] = list(scratch)
    return pl.pallas_call(body, **kw)


def _dot(a, b, dims=NN):
    return lax.dot_general(a.astype(BF16), b.astype(BF16), dims, preferred_element_type=F32)


def _dot32(a, b, dims=NN):
    return lax.dot_general(a, b, dims, precision=HIGHEST, preferred_element_type=F32)


def _sig(x):
    return 1.0 / (1.0 + jnp.exp(-x))


_GC = 0.7978845608028654
_GA = 0.044715


def _gelu(x):
    return 0.5 * x * (1.0 + jnp.tanh(_GC * (x + _GA * x * x * x)))


def _gelu_and_grad(x):
    t = jnp.tanh(_GC * (x + _GA * x * x * x))
    g = 0.5 * x * (1.0 + t)
    dg = 0.5 * (1.0 + t) + 0.5 * x * (1.0 - t * t) * _GC * (1.0 + 3.0 * _GA * x * x)
    return g, dg


def _ln_stats(r):
    mu = jnp.mean(r, axis=-1, keepdims=True)
    xc = r - mu
    var = jnp.mean(xc * xc, axis=-1, keepdims=True)
    rstd = lax.rsqrt(var + LN_EPS)
    return xc * rstd, rstd


def _ln_bwd(dxh, xh, rstd):
    m1 = jnp.mean(dxh, axis=-1, keepdims=True)
    m2 = jnp.mean(dxh * xh, axis=-1, keepdims=True)
    return rstd * (dxh - m1 - xh * m2)


def _colsum8(v):
    return jnp.broadcast_to(jnp.sum(v, axis=0, keepdims=True), (8, v.shape[1]))


def _adamw(w, g, m, v):
    m2 = ADAM_B1 * m + (1.0 - ADAM_B1) * g
    v2 = ADAM_B2 * v + (1.0 - ADAM_B2) * (g * g)
    m_hat = m2 / (1.0 - ADAM_B1 ** ADAM_STEP)
    v_hat = v2 / (1.0 - ADAM_B2 ** ADAM_STEP)
    delta = -ADAM_LR * (m_hat / (jnp.sqrt(v_hat) + ADAM_EPS) + ADAM_WD * w)
    return delta, m2, v2


def _row_tile(rows, cols, itemsize=4, budget=1 << 20, mult=8):
    best = mult
    for tr in range(mult, rows + 1, mult):
        if rows % tr == 0 and tr * cols * itemsize <= budget:
            best = tr
    return best


def _mm(name, a, b, dims, grid, a_spec, b_spec, out_shape, o_spec):
    out_dtype = out_shape.dtype

    def body(a_ref, b_ref, o_ref):
        o_ref[...] = _dot(a_ref[...], b_ref[...], dims).astype(out_dtype)

    return _pc(body, name=name, out_shape=out_shape, grid=grid, in_specs=[a_spec, b_spec],
               out_specs=o_spec, sem=("parallel", "parallel"))(a, b)


class _Comm:
    def __init__(self, ins, out_shapes, sems, start, finish, middle=None, finish_late=None, aliases=None):
        self.ins, self.out_shapes, self.sems = list(ins), list(out_shapes), list(sems)
        self.start, self.finish = start, finish
        self.middle, self.finish_late = middle, finish_late
        self.aliases = aliases or {}


def _hosted_call(body, comm, first, last, *, name, out_shape, grid, in_specs, out_specs, scratch, sem,
                 args, aliases=None, mid=None):
    n_in, n_out, n_scr = len(in_specs), len(out_shape), len(scratch)
    nci, nco = len(comm.ins), len(comm.out_shapes)

    def wrapped(*refs):
        pos = n_in
        own_in, c_in = refs[:pos], refs[pos:pos + nci]
        pos += nci
        own_out, c_out = refs[pos:pos + n_out], refs[pos + n_out:pos + n_out + nco]
        pos += n_out + nco
        own_scr, c_sem = refs[pos:pos + n_scr], refs[pos + n_scr:]

        @pl.when(first())
        def _():
            comm.start(c_in, c_out, c_sem)

        body(*own_in, *own_out, *own_scr)

        if mid is not None:
            @pl.when(mid())
            def _():
                comm.middle(c_in, c_out, c_sem)

        @pl.when(last())
        def _():
            (comm.finish if mid is None else comm.finish_late)(c_in, c_out, c_sem)

    return _pc(wrapped, name=name, out_shape=tuple(out_shape) + tuple(comm.out_shapes), grid=grid,
               in_specs=list(in_specs) + [ANY] * nci, out_specs=tuple(out_specs) + tuple([ANY] * nco),
               scratch=list(scratch) + comm.sems, sem=sem,
               aliases={**(aliases or {}), **{n_in + ci: n_out + co for ci, co in comm.aliases.items()}},
               )(*args, *comm.ins)


def _grid1_call(body, comm, n, *, name, out_shape, in_specs, out_specs, scratch, args, aliases=None):
    if comm is None:
        return _pc(body, name=name, out_shape=out_shape, grid=(n,), in_specs=in_specs, out_specs=out_specs,
                   scratch=scratch, sem=("arbitrary",), aliases=aliases)(*args), ()
    res = _hosted_call(body, comm, lambda: pl.program_id(0) == 0, lambda: pl.program_id(0) == n - 1,
                       name=name, out_shape=out_shape, grid=(n,), in_specs=in_specs,
                       out_specs=out_specs, scratch=scratch, sem=("arbitrary",), args=args, aliases=aliases)
    return res[:len(out_shape)], res[len(out_shape):]


def _run_comm(name, comm):
    nci, nco = len(comm.ins), len(comm.out_shapes)

    def body(*refs):
        c_in, c_out, c_sem = refs[:nci], refs[nci:nci + nco], refs[nci + nco:]
        comm.start(c_in, c_out, c_sem)
        comm.finish(c_in, c_out, c_sem)

    return _pc(body, name=name, out_shape=tuple(comm.out_shapes), in_specs=[ANY] * nci,
               out_specs=tuple([ANY] * nco), scratch=comm.sems)(*comm.ins)


def _mm_tn(name, a, b, tm, tn, stacked=False):
    t, m = a.shape
    _, n = b.shape
    if stacked:
        assert tm == m
        out_shape = jax.ShapeDtypeStruct((n // tn, m, tn), BF16)
        o_spec = pl.BlockSpec((None, tm, tn), lambda i, j: (j, 0, 0))
    else:
        out_shape = jax.ShapeDtypeStruct((m, n), BF16)
        o_spec = pl.BlockSpec((tm, tn), lambda i, j: (i, j))
    return _mm(name, a, b, TN, (m // tm, n // tn),
               pl.BlockSpec((t, tm), lambda i, j: (0, i)),
               pl.BlockSpec((t, tn), lambda i, j: (0, j)),
               out_shape, o_spec)


DH_SLOT = (2, 0, 1, 3)


def _dh_slot(j):
    return jnp.where(j == 3, 3, (j + 2) % 3)


def _in_proj_wgrad(x2b, dh, tm, tn):
    t, m = x2b.shape
    n = dh.shape[2]

    def body(a_ref, b_ref, o_ref):
        o_ref[...] = _dot(a_ref[...], b_ref[...], TN).astype(BF16)

    return _pc(body, name="in_proj_wgrad", out_shape=jax.ShapeDtypeStruct((N_CHIP, m, n), BF16),
               grid=(N_CHIP, m // tm, n // tn),
               in_specs=[pl.BlockSpec((t, tm), lambda j, i, k: (0, i)),
                         pl.BlockSpec((None, t, tn), lambda j, i, k: (_dh_slot(j), 0, k))],
               out_specs=pl.BlockSpec((None, tm, tn), lambda j, i, k: (j, i, k)),
               sem=("parallel", "parallel", "parallel"))(x2b, dh)


def _in_proj_xgrad(dh, win_st, dr1, tm, comm):
    t = dr1.shape[0]
    ni = t // tm

    def body(a_ref, b_ref, add_ref, o_ref, acc):
        j = pl.program_id(1)
        prod = _dot(a_ref[...], b_ref[...], NT)

        @pl.when(j == 0)
        def _():
            acc[...] = prod + ALPHA * add_ref[...].astype(F32)

        @pl.when((j > 0) & (j < N_CHIP - 1))
        def _():
            acc[...] += prod

        @pl.when(j == N_CHIP - 1)
        def _():
            o_ref[...] = acc[...] + prod

    tile = pl.BlockSpec((tm, D_MODEL), lambda i, j: (i, 0))
    res = _hosted_call(body, comm,
                       lambda: (pl.program_id(0) == 0) & (pl.program_id(1) == 0),
                       lambda: (pl.program_id(0) == ni - 1) & (pl.program_id(1) == N_CHIP - 1),
                       name="in_proj_xgrad", out_shape=(jax.ShapeDtypeStruct((1, t, D_MODEL), F32),),
                       grid=(ni, N_CHIP),
                       in_specs=[pl.BlockSpec((None, tm, 2 * D_MODEL), lambda i, j: (_dh_slot(j), i, 0)),
                                 pl.BlockSpec((None, D_MODEL, 2 * D_MODEL), lambda i, j: (j, 0, 0)),
                                 tile],
                       out_specs=(pl.BlockSpec((None, tm, D_MODEL), lambda i, j: (0, i, 0)),),
                       scratch=[pltpu.VMEM((tm, D_MODEL), F32)],
                       sem=("arbitrary", "arbitrary"), args=[dh, win_st, dr1])
    return res[0], res[1:]


def _sgu_mixed(v, wm_ref, bsb_ref, gv, bv):
    gl, dgl = _gelu_and_grad(v)
    vh, rstd = _ln_stats(gl)
    vn = vh * gv + bv
    mixed = []
    for g in range(N_GROUP):
        sl = slice(g * 128, (g + 1) * 128)
        mixed.append(_dot(wm_ref[g], vn[:, sl]) + bsb_ref[g])
    return dgl, vh, rstd, vn, mixed


def _sgu_fwd(h, wm, bsb, gv, bv, comm=None):
    t = h.shape[0]

    def body(u_ref, v_ref, wm_ref, bsb_ref, gv_ref, bv_ref, ya_ref):
        for bb in range(SGU_STEP_BLOCKS):
            rows = slice(bb * SGU_BLOCK, (bb + 1) * SGU_BLOCK)
            u = u_ref[rows, :].astype(F32)
            _, _, _, _, mixed = _sgu_mixed(v_ref[rows, :].astype(F32), wm_ref, bsb_ref, gv_ref[...],
                                           bv_ref[...])
            gu = _gelu(u)
            for g in range(N_GROUP):
                sl = slice(g * 128, (g + 1) * 128)
                ya_ref[rows, sl] = (gu[:, sl] * mixed[g]).astype(BF16)

    full3 = pl.BlockSpec((N_GROUP, 128, 128), lambda i: (0, 0, 0))
    vec = pl.BlockSpec((1, D_MODEL), lambda i: (0, 0))
    (ya,), extra = _grid1_call(
        body, comm, t // SGU_ROWS, name="sgu_fwd",
        out_shape=(jax.ShapeDtypeStruct((t, D_MODEL), BF16),),
        in_specs=[pl.BlockSpec((SGU_ROWS, D_MODEL), lambda i: (i, 0)),
                  pl.BlockSpec((SGU_ROWS, D_MODEL), lambda i: (i, 1)),
                  full3, full3, vec, vec],
        out_specs=(pl.BlockSpec((SGU_ROWS, D_MODEL), lambda i: (i, 0)),),
        scratch=[], args=(h, h, wm, bsb, gv, bv))
    return ya, extra


def _sgu_bwd(h, dya, wm, wmt, bsb, gv, bv, maskf, dh_buf, comm=None):
    t = h.shape[0]
    nb = t // SGU_ROWS

    def body(u_ref, v_ref, dya_ref, wm_ref, wmt_ref, bsb_ref, gv_ref, bv_ref, mask_ref, dh_buf_ref,
             dh_ref, dws_ref, dbs_ref, dgv_ref, dbv_ref, dmix_acc):
        i = pl.program_id(0)

        @pl.when(i == 0)
        def _():
            dws_ref[...] = jnp.zeros_like(dws_ref)
            dgv_ref[...] = jnp.zeros_like(dgv_ref)
            dbv_ref[...] = jnp.zeros_like(dbv_ref)
            dmix_acc[...] = jnp.zeros_like(dmix_acc)

        gvv = gv_ref[...]
        for bb in range(SGU_STEP_BLOCKS):
            rows = slice(bb * SGU_BLOCK, (bb + 1) * SGU_BLOCK)
            u = u_ref[rows, :].astype(F32)
            dgl_v, vh, rstd, vn, mixed = _sgu_mixed(v_ref[rows, :].astype(F32), wm_ref, bsb_ref, gvv,
                                                    bv_ref[...])
            gu, dgl_u = _gelu_and_grad(u)
            dya_v = dya_ref[rows, :].astype(F32)
            dvn_parts = []
            for g in range(N_GROUP):
                sl = slice(g * 128, (g + 1) * 128)
                d_y = dya_v[:, sl]
                dh_ref[rows, sl] = (d_y * mixed[g] * dgl_u[:, sl]).astype(BF16)
                d_mixed = d_y * gu[:, sl]
                dmix_acc[g] += d_mixed
                dws_ref[g] += _dot(d_mixed, vn[:, sl], NT) * mask_ref[...]
                dvn_parts.append(_dot(wmt_ref[g], d_mixed))
            dvn = jnp.concatenate(dvn_parts, axis=1)
            dgv_ref[...] += _colsum8(dvn * vh)
            dbv_ref[...] += _colsum8(dvn)
            d_gl = _ln_bwd(dvn * gvv, vh, rstd)
            dh_ref[rows, D_MODEL:] = (d_gl * dgl_v).astype(BF16)

        @pl.when(i == nb - 1)
        def _():
            rowid = lax.broadcasted_iota(jnp.int32, (8, 128), 0)
            ones = jnp.ones((8, 128), F32)
            acc = jnp.zeros((8, 128), F32)
            for g in range(N_GROUP):
                rs = _dot32(ones, dmix_acc[g], NT)
                acc = jnp.where(rowid == g, rs, acc)
            dbs_ref[...] = acc

    full3 = pl.BlockSpec((N_GROUP, 128, 128), lambda i: (0, 0, 0))
    vec = pl.BlockSpec((1, D_MODEL), lambda i: (0, 0))
    acc8 = pl.BlockSpec((8, D_MODEL), lambda i: (0, 0))
    return _grid1_call(
        body, comm, nb, name="sgu_bwd",
        out_shape=(jax.ShapeDtypeStruct(dh_buf.shape, BF16),
                   jax.ShapeDtypeStruct((N_GROUP, 128, 128), F32),
                   jax.ShapeDtypeStruct((8, 128), F32),
                   jax.ShapeDtypeStruct((8, D_MODEL), F32),
                   jax.ShapeDtypeStruct((8, D_MODEL), F32)),
        in_specs=[pl.BlockSpec((SGU_ROWS, D_MODEL), lambda i: (i, 0)),
                  pl.BlockSpec((SGU_ROWS, D_MODEL), lambda i: (i, 1)),
                  pl.BlockSpec((SGU_ROWS, D_MODEL), lambda i: (i, 0)),
                  full3, full3, full3, vec, vec,
                  pl.BlockSpec((128, 128), lambda i: (0, 0)), ANY],
        out_specs=(pl.BlockSpec((None, SGU_ROWS, 2 * D_MODEL), lambda i: (DH_SLOT[0], i, 0)),
                   full3, pl.BlockSpec((8, 128), lambda i: (0, 0)), acc8, acc8),
        scratch=[pltpu.VMEM((N_GROUP, 128, 128), F32)],
        args=(h, h, dya, wm, wmt, bsb, gv, bv, maskf, dh_buf), aliases={9: 0})


def _tri_masks():
    row = lax.broadcasted_iota(jnp.int32, (CHUNK, CHUNK), 0)
    col = lax.broadcasted_iota(jnp.int32, (CHUNK, CHUNK), 1)
    return col <= row, col >= row


def _heads(v):
    return [v[:, hd * HEAD_DIM:(hd + 1) * HEAD_DIM] for hd in range(N_HEAD)]


def _tri_cumsum(tri_bf, v):
    hi = v.astype(BF16)
    r = v - hi.astype(F32)
    mid = r.astype(BF16)
    lo = (r - mid.astype(F32)).astype(BF16)
    return _dot(tri_bf, hi) + _dot(tri_bf, mid) + _dot(tri_bf, lo)


def _hgrn_chunk(q, fp, ii, lb, st_heads, causal, with_o=True):
    sg = _sig(fp)
    f = lb + (1.0 - lb) * sg
    k = 1.0 - f
    c = _tri_cumsum(causal.astype(BF16), jnp.log(f))
    ec = jnp.exp(c)
    en = jnp.exp(-c)
    sq = _sig(q)
    qt = q * sq * ec
    kt = k * en
    ecl = jnp.exp(c[CHUNK - 1:CHUNK, :])
    kk = kt * ecl
    qtb, ktb, iib, kkb = qt.astype(BF16), kt.astype(BF16), ii.astype(BF16), kk.astype(BF16)
    attn, o = [], []
    for hd, (qh, kh, ih) in enumerate(zip(_heads(qtb), _heads(ktb), _heads(iib))):
        a = jnp.where(causal, _dot(qh, kh, NT), 0.0).astype(BF16)
        attn.append(a)
        if with_o:
            o.append(_dot(a, ih) + _dot(qh, st_heads[hd], NT))
    return dict(sg=sg, f=f, k=k, ec=ec, en=en, sq=sq, ecl=ecl, kk=kk, qtb=qtb, ktb=ktb, iib=iib,
                kkb=kkb, attn=attn, o=o)


def _rms_heads(o_heads):
    rinv = [lax.rsqrt(jnp.mean(o * o, axis=-1, keepdims=True) + RMS_EPS) for o in o_heads]
    return rinv, jnp.concatenate([o * r for o, r in zip(o_heads, rinv)], axis=1)


HG_CHUNKS = 8
HG_ROWS = HG_CHUNKS * CHUNK


def _hgrn_fwd(h, logits, gn, comm=None):
    t = h.shape[0]
    nb = t // HG_ROWS

    def body(q_ref, f_ref, i_ref, og_ref, lg_ref, gn_ref, yb_ref, o_ref, st_ref, state):
        @pl.when(pl.program_id(0) == 0)
        def _():
            state[...] = jnp.zeros_like(state)

        causal, _ = _tri_masks()
        lb = _sig(lg_ref[0:1, :] - lg_ref[1:2, :])
        gnv = gn_ref[...]
        st = [state[hd] for hd in range(N_HEAD)]
        for cc in range(HG_CHUNKS):
            rows = slice(cc * CHUNK, (cc + 1) * CHUNK)
            og = og_ref[rows, :].astype(F32)
            r = _hgrn_chunk(q_ref[rows, :].astype(F32), f_ref[rows, :].astype(F32),
                            i_ref[rows, :].astype(F32), lb, [s.astype(BF16) for s in st], causal)
            o_bf = jnp.concatenate(r["o"], axis=1).astype(BF16)
            o_ref[rows, :] = o_bf
            _, on = _rms_heads(_heads(o_bf.astype(F32)))
            yb_ref[rows, :] = (on * gnv * (og * _sig(og))).astype(BF16)
            for hd in range(N_HEAD):
                st_ref[cc, hd] = st[hd]
            st = [s * e + _dot(ih, kh, TN)
                  for s, e, ih, kh in zip(st, _heads(r["ecl"]), _heads(r["iib"]), _heads(r["kkb"]))]
        for hd in range(N_HEAD):
            state[hd] = st[hd]

    def col(k):
        return pl.BlockSpec((HG_ROWS, D_MODEL), lambda ci: (ci, k))

    return _grid1_call(body, comm, nb, name="hgrn_fwd",
                       out_shape=(jax.ShapeDtypeStruct((t, D_MODEL), BF16),
                                  jax.ShapeDtypeStruct((t, D_MODEL), BF16),
                                  jax.ShapeDtypeStruct((t // CHUNK, N_HEAD, HEAD_DIM, HEAD_DIM), F32)),
                       in_specs=[col(2), col(3), col(4), col(5),
                                 pl.BlockSpec((2, D_MODEL), lambda ci: (0, 0)),
                                 pl.BlockSpec((1, D_MODEL), lambda ci: (0, 0))],
                       out_specs=(pl.BlockSpec((HG_ROWS, D_MODEL), lambda ci: (ci, 0)),
                                  pl.BlockSpec((HG_ROWS, D_MODEL), lambda ci: (ci, 0)),
                                  pl.BlockSpec((HG_CHUNKS, N_HEAD, HEAD_DIM, HEAD_DIM),
                                               lambda ci: (ci, 0, 0, 0))),
                       scratch=[pltpu.VMEM((N_HEAD, HEAD_DIM, HEAD_DIM), F32)],
                       args=(h, h, h, h, logits, gn))


def _hgrn_chunk_bwd(q, fp, ii, og, o_saved, dy, gnv, lb, st, dsn, causal, anti):
    stb = [s.astype(BF16) for s in st]
    dsnb = [s.astype(BF16) for s in dsn]
    r = _hgrn_chunk(q, fp, ii, lb, stb, causal, with_o=False)
    rinv, on = _rms_heads(_heads(o_saved))
    so = _sig(og)
    sil = og * so
    d_og = dy * on * gnv * (so * (1.0 + og * (1.0 - so)))
    d_on = dy * gnv * sil
    d_ob = jnp.concatenate(
        [ri * (dn - oh * jnp.mean(dn * oh, axis=-1, keepdims=True))
         for ri, dn, oh in zip(rinv, _heads(d_on), _heads(on))], axis=1).astype(BF16)
    d_i, d_qt, d_kt, d_kk, d_st, st_dsn = [], [], [], [], [], []
    ecl = _heads(r["ecl"])
    for hd, (dh, qh, kh, ih, kkh) in enumerate(zip(_heads(d_ob), _heads(r["qtb"]), _heads(r["ktb"]),
                                                   _heads(r["iib"]), _heads(r["kkb"]))):
        d_attn = jnp.where(causal, _dot(dh, ih, NT), 0.0).astype(BF16)
        d_i.append(_dot(r["attn"][hd], dh, TN) + _dot(kkh, dsnb[hd], NT))
        d_qt.append(_dot(d_attn, kh) + _dot(dh, stb[hd]))
        d_kt.append(_dot(d_attn, qh, TN))
        d_kk.append(_dot(ih, dsnb[hd]))
        d_st.append(_dot(dh, qh, TN) + dsn[hd] * ecl[hd])
        st_dsn.append(jnp.sum(st[hd] * dsn[hd], axis=0, keepdims=True))
    d_qt = jnp.concatenate(d_qt, axis=1)
    d_kt = jnp.concatenate(d_kt, axis=1)
    d_kk = jnp.concatenate(d_kk, axis=1)
    kk = r["kk"]
    d_cl = r["ecl"] * jnp.concatenate(st_dsn, axis=1) + jnp.sum(kk * d_kk, axis=0, keepdims=True)
    d_k = (d_kk * r["ecl"] + d_kt) * r["en"]
    d_c = d_qt * r["qtb"].astype(F32) - d_kt * r["ktb"].astype(F32) - d_kk * kk
    rowid = lax.broadcasted_iota(jnp.int32, (CHUNK, D_MODEL), 0)
    d_c = d_c + jnp.where(rowid == CHUNK - 1, d_cl, 0.0)
    d_lf = _tri_cumsum(anti.astype(BF16), d_c)
    d_f = d_lf / r["f"] - d_k
    sg, sq = r["sg"], r["sq"]
    d_q = d_qt * r["ec"] * (sq * (1.0 + q * (1.0 - sq)))
    d_fp = d_f * (1.0 - lb) * sg * (1.0 - sg)
    return (d_q, d_fp, jnp.concatenate(d_i, axis=1), d_og, d_st,
            _colsum8(dy * on * sil), _colsum8(d_f * (1.0 - sg)))


def _hgrn_bwd(h, o_all, dyb, st_all, logits, gn, dh_buf, comm=None):
    t = h.shape[0]
    nb = t // HG_ROWS

    def body(q_ref, f_ref, i_ref, og_ref, o_ref, dyb_ref, st_ref, lg_ref, gn_ref, dh_buf_ref,
             dh_ref, dlb_ref, dgn_ref, dstate):
        @pl.when(pl.program_id(0) == 0)
        def _():
            dstate[...] = jnp.zeros_like(dstate)
            dlb_ref[...] = jnp.zeros_like(dlb_ref)
            dgn_ref[...] = jnp.zeros_like(dgn_ref)

        causal, anti = _tri_masks()
        lb = _sig(lg_ref[0:1, :] - lg_ref[1:2, :])
        gnv = gn_ref[...]
        dsn = [dstate[hd] for hd in range(N_HEAD)]
        dgn_acc = jnp.zeros((8, D_MODEL), F32)
        dlb_acc = jnp.zeros((8, D_MODEL), F32)
        for cc in reversed(range(HG_CHUNKS)):
            rows = slice(cc * CHUNK, (cc + 1) * CHUNK)
            d_q, d_fp, d_i, d_og, dsn, dgn_c, dlb_c = _hgrn_chunk_bwd(
                q_ref[rows, :].astype(F32), f_ref[rows, :].astype(F32), i_ref[rows, :].astype(F32),
                og_ref[rows, :].astype(F32), o_ref[rows, :].astype(F32), dyb_ref[rows, :].astype(F32), gnv, lb,
                [st_ref[cc, hd] for hd in range(N_HEAD)], dsn, causal, anti)
            dgn_acc = dgn_acc + dgn_c
            dlb_acc = dlb_acc + dlb_c
            dh_ref[0, rows, :D_MODEL] = d_q.astype(BF16)
            dh_ref[0, rows, D_MODEL:] = d_fp.astype(BF16)
            dh_ref[1, rows, :D_MODEL] = d_i.astype(BF16)
            dh_ref[1, rows, D_MODEL:] = d_og.astype(BF16)
        dgn_ref[...] += dgn_acc
        dlb_ref[...] += dlb_acc
        for hd in range(N_HEAD):
            dstate[hd] = dsn[hd]

    def col(k):
        return pl.BlockSpec((HG_ROWS, D_MODEL), lambda ci: (nb - 1 - ci, k))

    acc8 = pl.BlockSpec((8, D_MODEL), lambda ci: (0, 0))
    pair = pl.BlockSpec((2, HG_ROWS, 2 * D_MODEL), lambda ci: (0, nb - 1 - ci, 0))
    return _grid1_call(body, comm, nb, name="hgrn_bwd",
                       out_shape=(jax.ShapeDtypeStruct(dh_buf.shape, BF16),
                                  jax.ShapeDtypeStruct((8, D_MODEL), F32),
                                  jax.ShapeDtypeStruct((8, D_MODEL), F32)),
                       in_specs=[col(2), col(3), col(4), col(5), col(0),
                                 pl.BlockSpec((HG_ROWS, D_MODEL), lambda ci: (nb - 1 - ci, 0)),
                                 pl.BlockSpec((HG_CHUNKS, N_HEAD, HEAD_DIM, HEAD_DIM),
                                              lambda ci: (nb - 1 - ci, 0, 0, 0)),
                                 pl.BlockSpec((2, D_MODEL), lambda ci: (0, 0)),
                                 pl.BlockSpec((1, D_MODEL), lambda ci: (0, 0)), ANY],
                       out_specs=(pair, acc8, acc8),
                       scratch=[pltpu.VMEM((N_HEAD, HEAD_DIM, HEAD_DIM), F32)],
                       args=(h, h, h, h, o_all, dyb, st_all, logits, gn, dh_buf), aliases={9: 0})


def _mix_fwd(ya, yb, h, x, wb0, wb1, wo, g1, b1, tm, comm=None):
    t = x.shape[0]

    def body(ya_ref, yb_ref, ga_ref, gb_ref, x_ref, wb0_ref, wb1_ref, wo_ref, g1_ref, b1_ref,
             r1_ref, a_ref, b_ref, m_ref, x1_ref):
        a = _dot(ya_ref[...], wb0_ref[...])
        b = _dot(yb_ref[...], wb1_ref[...])
        m = _sig(ga_ref[...].astype(F32)) * a + _sig(gb_ref[...].astype(F32)) * b
        r1 = ALPHA * x_ref[...] + _dot(m, wo_ref[...])
        xh, _ = _ln_stats(r1)
        r1_ref[...] = r1
        a_ref[...] = a.astype(BF16)
        b_ref[...] = b.astype(BF16)
        m_ref[...] = m.astype(BF16)
        x1_ref[...] = (xh * g1_ref[...] + b1_ref[...]).astype(BF16)

    tile = pl.BlockSpec((tm, D_MODEL), lambda i: (i, 0))
    wsp = pl.BlockSpec((D_MODEL, D_MODEL), lambda i: (0, 0))
    vec = pl.BlockSpec((1, D_MODEL), lambda i: (0, 0))
    f32o = jax.ShapeDtypeStruct((t, D_MODEL), F32)
    bfo = jax.ShapeDtypeStruct((t, D_MODEL), BF16)
    return _grid1_call(body, comm, t // tm, name="mix_fwd", out_shape=(f32o, bfo, bfo, bfo, bfo),
                       in_specs=[tile, tile,
                                 pl.BlockSpec((tm, D_MODEL), lambda i: (i, 6)),
                                 pl.BlockSpec((tm, D_MODEL), lambda i: (i, 7)),
                                 tile, wsp, wsp, wsp, vec, vec],
                       out_specs=(tile, tile, tile, tile, tile),
                       scratch=[], args=(ya, yb, h, h, x, wb0, wb1, wo, g1, b1))


def _mix_bwd(dr1, h, a, b, wo, wb0, wb1, tm):
    t = dr1.shape[0]

    def body(dr1_ref, ga_ref, gb_ref, a_ref, b_ref, wo_ref, wb0_ref, wb1_ref,
             da_ref, db_ref, dh3_ref, dya_ref, dyb_ref):
        d_m = _dot(dr1_ref[...], wo_ref[...], NT)
        sa = _sig(ga_ref[...].astype(F32))
        sb = _sig(gb_ref[...].astype(F32))
        d_a = (d_m * sa).astype(BF16)
        d_b = (d_m * sb).astype(BF16)
        da_ref[...] = d_a
        db_ref[...] = d_b
        dh3_ref[:, :D_MODEL] = (d_m * a_ref[...].astype(F32) * sa * (1.0 - sa)).astype(BF16)
        dh3_ref[:, D_MODEL:] = (d_m * b_ref[...].astype(F32) * sb * (1.0 - sb)).astype(BF16)
        dya_ref[...] = _dot(d_a, wb0_ref[...], NT).astype(BF16)
        dyb_ref[...] = _dot(d_b, wb1_ref[...], NT).astype(BF16)

    tile = pl.BlockSpec((tm, D_MODEL), lambda i: (i, 0))
    wsp = pl.BlockSpec((D_MODEL, D_MODEL), lambda i: (0, 0))
    f32o = jax.ShapeDtypeStruct((t, D_MODEL), F32)
    bfo = jax.ShapeDtypeStruct((t, D_MODEL), BF16)
    return _pc(body, name="mix_bwd",
               out_shape=(bfo, bfo, jax.ShapeDtypeStruct((N_CHIP, t, 2 * D_MODEL), BF16), bfo, bfo),
               grid=(t // tm,),
               in_specs=[tile,
                         pl.BlockSpec((tm, D_MODEL), lambda i: (i, 6)),
                         pl.BlockSpec((tm, D_MODEL), lambda i: (i, 7)),
                         tile, tile, wsp, wsp, wsp],
               out_specs=(tile, tile, pl.BlockSpec((None, tm, 2 * D_MODEL), lambda i: (DH_SLOT[3], i, 0)),
                          tile, tile),
               sem=("parallel",))(dr1, h, h, a, b, wo, wb0, wb1)


FF_TILE = 1408
FF_NJ = D_FF // FF_TILE


def _shift_down(v, k):
    return pltpu.roll(v, k, 0)


def _shift_up(v, k):
    return pltpu.roll(v, v.shape[0] - k, 0)


HALO = 16
FF_PIECES = ((0, 768), (768, FF_TILE))


def _ffn_up_act(x1b, wup_st, convw, convb, tm, comm):
    t = x1b.shape[0]
    ni = t // tm
    nth = tm // HALO

    def body(x_ref, xp_ref, wg_ref, wv_ref, cw_ref, cb_ref, h2_ref, act_ref):
        wg = wg_ref[...]
        gate = _dot(x_ref[...], wg).astype(BF16)
        val = _dot(x_ref[...], wv_ref[...]).astype(BF16)
        prev = (_dot(xp_ref[...], wg) * (pl.program_id(0) > 0).astype(F32)).astype(BF16)
        h2_ref[0] = gate
        h2_ref[1] = val
        ext = jnp.concatenate([prev.astype(F32), gate.astype(F32)], axis=0)
        gc = (cw_ref[0:1, :] * _shift_down(ext, 2) + cw_ref[1:2, :] * _shift_down(ext, 1)
              + cw_ref[2:3, :] * ext + cb_ref[...])[HALO:, :].astype(BF16)
        h2_ref[2] = gc
        act_ref[...] = (_gelu(gc.astype(F32)) * val.astype(F32)).astype(BF16)

    res = _hosted_call(
        body, comm,
        lambda: (pl.program_id(0) == 0) & (pl.program_id(1) == 0),
        lambda: (pl.program_id(0) == ni - 1) & (pl.program_id(1) == FF_NJ - 1),
        mid=lambda: (pl.program_id(0) == max(ni - 2, 0)) & (pl.program_id(1) == 0),
        name="ffn_up",
        out_shape=(jax.ShapeDtypeStruct((3, t, D_FF), BF16), jax.ShapeDtypeStruct((t, D_FF), BF16)),
        grid=(ni, FF_NJ),
        in_specs=[pl.BlockSpec((tm, D_MODEL), lambda i, j: (i, 0)),
                  pl.BlockSpec((HALO, D_MODEL), lambda i, j: (jnp.maximum(i * nth - 1, 0), 0)),
                  pl.BlockSpec((None, D_MODEL, FF_TILE), lambda i, j: (j, 0, 0)),
                  pl.BlockSpec((None, D_MODEL, FF_TILE), lambda i, j: (j + FF_NJ, 0, 0)),
                  pl.BlockSpec((3, FF_TILE), lambda i, j: (0, j)),
                  pl.BlockSpec((1, FF_TILE), lambda i, j: (0, j))],
        out_specs=(pl.BlockSpec((3, tm, FF_TILE), lambda i, j: (0, i, j)),
                   pl.BlockSpec((tm, FF_TILE), lambda i, j: (i, j))),
        scratch=[], sem=("arbitrary", "arbitrary"),
        args=(x1b, x1b, wup_st, wup_st, convw, convb))
    return res[0], res[1], res[2:]


def _out_fwd_bwd(act, x1b, r1, p2, tgt, wd, wpg, wpp, g1, b1, g2, b2, tm):
    t = r1.shape[0]

    def body(act_ref, x1b_ref, r1_ref, p_ref, tgt_ref, wd_ref, wpg_ref, wpp_ref,
             g1_ref, b1_ref, g2_ref, b2_ref,
             dr2_ref, dpg_ref, dpp_ref, loss_ref, dg2_ref, db2_ref):
        i = pl.program_id(0)

        @pl.when(i == 0)
        def _():
            loss_ref[...] = jnp.zeros_like(loss_ref)
            dg2_ref[...] = jnp.zeros_like(dg2_ref)
            db2_ref[...] = jnp.zeros_like(db2_ref)

        ffn = _dot(act_ref[...], wd_ref[...])
        pg = _dot(x1b_ref[...], wpg_ref[...])
        pp = _dot(p_ref[...], wpp_ref[...])
        s = _sig(pg)
        xh1, _ = _ln_stats(r1_ref[...])
        x1 = xh1 * g1_ref[...] + b1_ref[...]
        r2 = ALPHA * x1 + ffn + s * pp
        xh2, rstd2 = _ln_stats(r2)
        g2v = g2_ref[...]
        diff = xh2 * g2v + b2_ref[...] - tgt_ref[...]
        part = jnp.sum(jnp.sum(diff * diff, axis=1, keepdims=True), axis=0, keepdims=True)
        loss_ref[...] += jnp.broadcast_to(part * (0.5 / D_MODEL), loss_ref.shape)
        dy = diff * (1.0 / D_MODEL)
        dg2_ref[...] += _colsum8(dy * xh2)
        db2_ref[...] += _colsum8(dy)
        dr2 = _ln_bwd(dy * g2v, xh2, rstd2)
        dr2_ref[...] = dr2.astype(BF16)
        dpg_ref[...] = (dr2 * pp * s * (1.0 - s)).astype(BF16)
        dpp_ref[...] = (dr2 * s).astype(BF16)

    tile = pl.BlockSpec((tm, D_MODEL), lambda i: (i, 0))
    vec = pl.BlockSpec((1, D_MODEL), lambda i: (0, 0))
    acc8 = pl.BlockSpec((8, D_MODEL), lambda i: (0, 0))
    acc_shape = jax.ShapeDtypeStruct((8, D_MODEL), F32)
    return _pc(body, name="out_fwd_bwd",
               out_shape=(jax.ShapeDtypeStruct((t, D_MODEL), BF16),
                          jax.ShapeDtypeStruct((t, D_MODEL), BF16),
                          jax.ShapeDtypeStruct((t, D_MODEL), BF16),
                          acc_shape, acc_shape, acc_shape),
               grid=(t // tm,),
               in_specs=[pl.BlockSpec((tm, D_FF), lambda i: (i, 0)), tile, tile,
                         pl.BlockSpec((tm, PLE_DIM), lambda i: (i, 0)), tile,
                         pl.BlockSpec((D_FF, D_MODEL), lambda i: (0, 0)),
                         pl.BlockSpec((D_MODEL, D_MODEL), lambda i: (0, 0)),
                         pl.BlockSpec((PLE_DIM, D_MODEL), lambda i: (0, 0)),
                         vec, vec, vec, vec],
               out_specs=(tile, tile, tile, acc8, acc8, acc8),
               sem=("arbitrary",))(act, x1b, r1, p2, tgt, wd, wpg, wpp, g1, b1, g2, b2)


def _ffn_bwd(h2, dr2, wd, wup_st, dpg, wpg, r1, g1, convw, tm):
    t = r1.shape[0]
    ni = t // tm
    nth = tm // HALO
    last_halo = t // HALO - 1
    main_rows = slice(0, tm)

    def body(g_ref, gc_ref, gcn_ref, v_ref, vn_ref, dr2_ref, dr2n_ref, wd_ref, wug_ref, wuv_ref,
             cw_ref, dpg_ref, wpg_ref, r1_ref, g1_ref,
             dh2_ref, dr1_ref, dcw_ref, dcb_ref, dg1_ref, db1_ref, acc):
        i = pl.program_id(0)
        j = pl.program_id(1)

        @pl.when((i == 0) & (j == 0))
        def _():
            dcw_ref[...] = jnp.zeros_like(dcw_ref)
            dcb_ref[...] = jnp.zeros_like(dcb_ref)
            dg1_ref[...] = jnp.zeros_like(dg1_ref)
            db1_ref[...] = jnp.zeros_like(db1_ref)

        dr2v = dr2_ref[...].astype(BF16)
        dr2n = dr2n_ref[...].astype(BF16)
        more = (i < ni - 1).astype(F32)
        prod = None
        dcw_parts, dcb_parts = [], []
        for c0, c1 in FF_PIECES:
            pc = slice(c0, c1)
            da = _dot(dr2v, wd_ref[pc, :], NT)
            dnext = _dot(dr2n, wd_ref[pc, :], NT) * more
            gc = jnp.concatenate([gc_ref[:, pc].astype(F32), gcn_ref[:, pc].astype(F32)], axis=0)
            vext = jnp.concatenate([v_ref[:, pc].astype(F32), vn_ref[:, pc].astype(F32)], axis=0)
            dext = jnp.concatenate([da, dnext], axis=0)
            gl, dgl = _gelu_and_grad(gc)
            d_gc = dext * vext * dgl
            up1 = _shift_up(d_gc, 1)[main_rows, :]
            up2 = _shift_up(d_gc, 2)[main_rows, :]
            dm = d_gc[main_rows, :]
            d_gate = (cw_ref[2:3, pc] * dm + cw_ref[1:2, pc] * up1 + cw_ref[0:1, pc] * up2).astype(BF16)
            d_val = (da * gl[main_rows, :]).astype(BF16)
            dh2_ref[0, :, pc] = d_gate
            dh2_ref[1, :, pc] = d_val
            g = g_ref[:, pc].astype(F32)
            s0 = jnp.sum(g * up2, axis=0, keepdims=True)
            s1 = jnp.sum(g * up1, axis=0, keepdims=True)
            s2 = jnp.sum(g * dm, axis=0, keepdims=True)
            rowid = lax.broadcasted_iota(jnp.int32, (8, c1 - c0), 0)
            dcw_parts.append(jnp.where(rowid == 0, s0, jnp.where(rowid == 1, s1,
                                                                 jnp.where(rowid == 2, s2, 0.0))))
            dcb_parts.append(_colsum8(dm))
            part = _dot(d_gate, wug_ref[:, pc], NT) + _dot(d_val, wuv_ref[:, pc], NT)
            prod = part if prod is None else prod + part
        dcw_part = jnp.concatenate(dcw_parts, axis=1)
        dcb_part = jnp.concatenate(dcb_parts, axis=1)
        for jj in range(FF_NJ):
            @pl.when(j == jj)
            def _(jj=jj):
                cols = slice(jj * FF_TILE, (jj + 1) * FF_TILE)
                dcw_ref[:, cols] += dcw_part
                dcb_ref[:, cols] += dcb_part

        @pl.when(j == 0)
        def _():
            acc[...] = prod

        @pl.when(j > 0)
        def _():
            acc[...] += prod

        @pl.when(j == FF_NJ - 1)
        def _():
            d_x1 = acc[...] + _dot(dpg_ref[...], wpg_ref[...], NT) + ALPHA * dr2_ref[...].astype(F32)
            xh, rstd = _ln_stats(r1_ref[...])
            dg1_ref[...] += _colsum8(d_x1 * xh)
            db1_ref[...] += _colsum8(d_x1)
            dr1_ref[...] = _ln_bwd(d_x1 * g1_ref[...], xh, rstd).astype(BF16)

    def h2_main(part):
        return pl.BlockSpec((None, tm, FF_TILE), lambda i, j: (part, i, j))

    def h2_next(part):
        return pl.BlockSpec((None, HALO, FF_TILE),
                            lambda i, j: (part, jnp.minimum((i + 1) * nth, last_halo), j))

    tile = pl.BlockSpec((tm, D_MODEL), lambda i, j: (i, 0))
    acc8 = pl.BlockSpec((8, D_MODEL), lambda i, j: (0, 0))
    accff = pl.BlockSpec((8, D_FF), lambda i, j: (0, 0))
    acc_shape = jax.ShapeDtypeStruct((8, D_MODEL), F32)
    accff_shape = jax.ShapeDtypeStruct((8, D_FF), F32)
    return _pc(body, name="ffn_bwd",
               out_shape=(jax.ShapeDtypeStruct((2, t, D_FF), BF16),
                          jax.ShapeDtypeStruct((t, D_MODEL), BF16),
                          accff_shape, accff_shape, acc_shape, acc_shape),
               grid=(ni, FF_NJ),
               in_specs=[h2_main(0), h2_main(2), h2_next(2), h2_main(1), h2_next(1),
                         tile,
                         pl.BlockSpec((HALO, D_MODEL), lambda i, j: (jnp.minimum((i + 1) * nth, last_halo), 0)),
                         pl.BlockSpec((FF_TILE, D_MODEL), lambda i, j: (j, 0)),
                         pl.BlockSpec((None, D_MODEL, FF_TILE), lambda i, j: (j, 0, 0)),
                         pl.BlockSpec((None, D_MODEL, FF_TILE), lambda i, j: (j + FF_NJ, 0, 0)),
                         pl.BlockSpec((3, FF_TILE), lambda i, j: (0, j)),
                         tile, pl.BlockSpec((D_MODEL, D_MODEL), lambda i, j: (0, 0)),
                         tile, pl.BlockSpec((1, D_MODEL), lambda i, j: (0, 0))],
               out_specs=(pl.BlockSpec((2, tm, FF_TILE), lambda i, j: (0, i, j)),
                          tile, accff, accff, acc8, acc8),
               scratch=[pltpu.VMEM((tm, D_MODEL), F32)],
               sem=("arbitrary", "arbitrary"))(h2, h2, h2, h2, h2, dr2, dr2, wd, wup_st, wup_st,
                                               convw, dpg, wpg, r1, g1)


ANY = pl.BlockSpec(memory_space=pl.ANY)


def _chip_peers():
    x, y, c = lax.axis_index("x"), lax.axis_index("y"), lax.axis_index("c")
    return x, y, c, [(1 - x, y), (x, 1 - y), (1 - x, 1 - y)]


def _gather_comm(halved, whole=(), blocks=None):
    n, nw = len(halved), len(whole)
    blocks = blocks or {}

    def at(ti, ref, chip, *rest):
        return ref.at[(chip, blocks[ti][1]) + rest] if ti in blocks else ref.at[(chip,) + rest]

    def copies(ins, outs, sems):
        ici_send, ici_recv, d2d_send, d2d_recv, own_send, own_recv = sems
        x, y, c, peers = _chip_peers()
        me = 2 * x + y
        sibling = (x, y, 1 - c)
        own, ici, ici_wait, fwd, fwd_wait = [], [], [], [], []
        for ti in range(n + nw):
            src, dst = ins[ti], outs[ti]
            own.append(pltpu.make_async_remote_copy(
                src_ref=src, dst_ref=at(ti, dst, me), send_sem=own_send.at[ti], recv_sem=own_recv.at[ti],
                device_id=sibling, device_id_type=MESH))
            for k, (px, py) in enumerate(peers):
                pk = 2 * px + py
                sem = dict(send_sem=ici_send.at[ti * 3 + k], recv_sem=ici_recv.at[ti * 3 + k],
                           device_id=(px, py, c), device_id_type=MESH)
                if ti < n:
                    ici.append(pltpu.make_async_remote_copy(src_ref=src.at[c], dst_ref=at(ti, dst, me, c), **sem))
                    ici_wait.append(pltpu.make_async_remote_copy(src_ref=src.at[c], dst_ref=at(ti, dst, pk, c),
                                                                 **sem))
                    dsem = dict(send_sem=d2d_send.at[ti * 3 + k], recv_sem=d2d_recv.at[ti * 3 + k],
                                device_id=sibling, device_id_type=MESH)
                    fwd.append(pltpu.make_async_remote_copy(src_ref=at(ti, dst, pk, c), dst_ref=at(ti, dst, pk, c),
                                                            **dsem))
                    fwd_wait.append(pltpu.make_async_remote_copy(
                        src_ref=at(ti, dst, pk, 1 - c), dst_ref=at(ti, dst, pk, 1 - c), **dsem))
                else:
                    ici.append(pltpu.make_async_remote_copy(src_ref=src, dst_ref=dst.at[me], **sem))
                    ici_wait.append(pltpu.make_async_remote_copy(src_ref=src, dst_ref=dst.at[pk], **sem))
        return own, ici, ici_wait, fwd, fwd_wait

    def start(ins, outs, sems):
        own, ici, _, _, _ = copies(ins, outs, sems)
        for cp in own + ici:
            cp.start()

    def finish(ins, outs, sems):
        own, ici, ici_wait, fwd, fwd_wait = copies(ins, outs, sems)
        for i, cp in enumerate(ici_wait):
            cp.wait_recv()
            if i < len(fwd):
                fwd[i].start()
        for cp in fwd_wait + own:
            cp.wait_recv()
        for cp in own + ici + fwd:
            cp.wait_send()

    def middle(ins, outs, sems):
        _, _, ici_wait, fwd, _ = copies(ins, outs, sems)
        for i, cp in enumerate(ici_wait):
            cp.wait_recv()
            if i < len(fwd):
                fwd[i].start()

    def finish_late(ins, outs, sems):
        own, ici, _, fwd, fwd_wait = copies(ins, outs, sems)
        for cp in fwd_wait + own:
            cp.wait_recv()
        for cp in own + ici + fwd:
            cp.wait_send()

    srcs = list(halved) + list(whole)
    shapes = [jax.ShapeDtypeStruct((N_CHIP,) + ((blocks[ti][0],) if ti in blocks else ()) + s.shape, s.dtype)
              for ti, s in enumerate(srcs)]
    buffers = [(ti, blk[2]) for ti, blk in sorted(blocks.items()) if blk[2] is not None]
    aliases = {len(srcs) + bi: ti for bi, (ti, _) in enumerate(buffers)}
    return _Comm(srcs + [buf for _, buf in buffers], shapes,
                 [pltpu.SemaphoreType.DMA((3 * (n + nw),)), pltpu.SemaphoreType.DMA((3 * (n + nw),)),
                  pltpu.SemaphoreType.DMA((max(3 * n, 1),)), pltpu.SemaphoreType.DMA((max(3 * n, 1),)),
                  pltpu.SemaphoreType.DMA((n + nw,)), pltpu.SemaphoreType.DMA((n + nw,))],
                 start, finish, middle, finish_late, aliases)


def _sibling_exchange_comm(grads):
    n = len(grads)

    def copies(ins, outs, sems):
        send_sems, recv_sems = sems
        x, y, c = lax.axis_index("x"), lax.axis_index("y"), lax.axis_index("c")
        res = []
        for ti in range(n):
            half = ins[ti].shape[1] // 2
            res.append(pltpu.make_async_remote_copy(
                src_ref=ins[ti].at[:, pl.ds(pl.multiple_of((1 - c) * half, 16), half), :],
                dst_ref=outs[ti],
                send_sem=send_sems.at[ti], recv_sem=recv_sems.at[ti],
                device_id=(x, y, 1 - c), device_id_type=MESH))
        return res

    def start(ins, outs, sems):
        for cp in copies(ins, outs, sems):
            cp.start()

    def finish(ins, outs, sems):
        for cp in copies(ins, outs, sems):
            cp.wait()

    return _Comm(grads, [jax.ShapeDtypeStruct((N_CHIP, g.shape[1] // 2, g.shape[2]), g.dtype) for g in grads],
                 [pltpu.SemaphoreType.DMA((n,)), pltpu.SemaphoreType.DMA((n,))], start, finish)


def _in_proj_gathering(x2b, own, chip, tm, comm):
    t = x2b.shape[0]
    ni = t // tm
    half, cols = own.shape[1], own.shape[2]
    nci, nco = len(comm.ins), len(comm.out_shapes)

    def body(chip_ref, x_ref, own_ref, own_hbm, *rest):
        c_in = rest[:nci]
        h_ref, win_out = rest[nci:nci + 2]
        c_out = rest[nci + 2:nci + 2 + nco]
        w_scr, ici_send, ici_recv, d2d_send, d2d_recv, own_sems, ld_sems = rest[nci + 2 + nco:nci + 9 + nco]
        c_sem = rest[nci + 9 + nco:]
        s, i = pl.program_id(0), pl.program_id(1)
        x, y, c, peers = _chip_peers()
        me = 2 * x + y
        sibling = (x, y, 1 - c)

        def ici(k, slot):
            px, py = peers[k]
            return pltpu.make_async_remote_copy(
                src_ref=own_hbm.at[c], dst_ref=win_out.at[slot, c],
                send_sem=ici_send.at[k], recv_sem=ici_recv.at[k],
                device_id=(px, py, c), device_id_type=MESH)

        def forward(k, core):
            pk = 2 * peers[k][0] + peers[k][1]
            return pltpu.make_async_remote_copy(
                src_ref=win_out.at[pk, core], dst_ref=win_out.at[pk, core],
                send_sem=d2d_send.at[k], recv_sem=d2d_recv.at[k],
                device_id=sibling, device_id_type=MESH)

        place_own = pltpu.make_async_remote_copy(
            src_ref=own_hbm, dst_ref=win_out.at[me], send_sem=own_sems.at[0], recv_sem=own_sems.at[1],
            device_id=sibling, device_id_type=MESH)

        @pl.when((s == 0) & (i == 0))
        def _():
            for k in range(2):
                ici(k, me).start()
            place_own.start()

        @pl.when(s == 0)
        def _():
            xv = x_ref[...]
            h_ref[...] = (_dot(xv[:, :half], own_ref[0]) + _dot(xv[:, half:], own_ref[1])).astype(BF16)

        for k in range(3):
            @pl.when((s == k + 1) & (i == 0))
            def _(k=k):
                pk = 2 * peers[k][0] + peers[k][1]
                ici(k, pk).wait_recv()
                if k == 0:
                    ici(2, me).start()
                forward(k, c).start()
                forward(k, 1 - c).wait_recv()
                loads = [pltpu.make_async_copy(win_out.at[pk, hh], w_scr.at[hh], ld_sems.at[hh])
                         for hh in range(2)]
                for ld in loads:
                    ld.start()
                for ld in loads:
                    ld.wait()
                if k == 1:
                    comm.start(c_in, c_out, c_sem)

        @pl.when(s > 0)
        def _():
            xv = x_ref[...]
            h_ref[...] = (_dot(xv[:, :half], w_scr[0]) + _dot(xv[:, half:], w_scr[1])).astype(BF16)

        @pl.when((s == N_CHIP - 1) & (i == ni - 1))
        def _():
            place_own.wait()
            for k in range(3):
                ici(k, me).wait_send()
                forward(k, c).wait_send()
            comm.finish(c_in, c_out, c_sem)

    def shard_col(s, me):
        return jnp.where(s == 0, me, me ^ jnp.where(s == 1, 2, jnp.where(s == 2, 1, 3)))

    res = _pc(body, name="in_proj",
              out_shape=(jax.ShapeDtypeStruct((t, N_CHIP * cols), BF16),
                         jax.ShapeDtypeStruct((N_CHIP,) + own.shape, own.dtype)) + tuple(comm.out_shapes),
              grid=(N_CHIP, ni), nsp=1,
              in_specs=[pl.BlockSpec((tm, 2 * half), lambda s, i, chip_ref: (i, 0)),
                        pl.BlockSpec(own.shape, lambda s, i, chip_ref: (0, 0, 0)),
                        ANY] + [ANY] * nci,
              out_specs=(pl.BlockSpec((tm, cols), lambda s, i, chip_ref: (i, shard_col(s, chip_ref[0]))),
                         ANY) + tuple([ANY] * nco),
              scratch=[pltpu.VMEM(own.shape, own.dtype),
                       pltpu.SemaphoreType.DMA((3,)), pltpu.SemaphoreType.DMA((3,)),
                       pltpu.SemaphoreType.DMA((3,)), pltpu.SemaphoreType.DMA((3,)),
                       pltpu.SemaphoreType.DMA((2,)), pltpu.SemaphoreType.DMA((2,))] + comm.sems,
              sem=("arbitrary", "arbitrary"))(chip, x2b, own, own, *comm.ins)
    return res[0], res[1], res[2:]


def _rs_add_halves(name, grad, recv, core):
    _, r, cdim = grad.shape
    half = r // 2
    tr = _row_tile(half, cdim, mult=16)
    nr = half // tr

    def body(c_ref, g_ref, r_ref, o_ref):
        o_ref[...] = (g_ref[...].astype(F32) + r_ref[...].astype(F32)).astype(BF16)

    return _pc(body, name=name, out_shape=jax.ShapeDtypeStruct((N_CHIP, half, cdim), BF16),
               grid=(N_CHIP, nr), nsp=1,
               in_specs=[pl.BlockSpec((None, tr, cdim), lambda j, i, c_ref: (j, c_ref[0] * nr + i, 0)),
                         pl.BlockSpec((None, tr, cdim), lambda j, i, c_ref: (j, i, 0))],
               out_specs=pl.BlockSpec((None, tr, cdim), lambda j, i, c_ref: (j, i, 0)),
               sem=("parallel", "parallel"))(core, grad, recv)


def _chip_exchange_comm(parts):
    n = len(parts)

    def copies(ins, outs, sems):
        send_sems, recv_sems = sems
        x, y, c, peers = _chip_peers()
        return [pltpu.make_async_remote_copy(
            src_ref=ins[ti].at[2 * px + py], dst_ref=outs[ti].at[k],
            send_sem=send_sems.at[ti * 3 + k], recv_sem=recv_sems.at[ti * 3 + k],
            device_id=(px, py, c), device_id_type=MESH)
            for ti in range(n) for k, (px, py) in enumerate(peers)]

    def start(ins, outs, sems):
        for cp in copies(ins, outs, sems):
            cp.start()

    def finish(ins, outs, sems):
        for cp in copies(ins, outs, sems):
            cp.wait()

    return _Comm(parts, [jax.ShapeDtypeStruct((3,) + p.shape[1:], p.dtype) for p in parts],
                 [pltpu.SemaphoreType.DMA((3 * n,)), pltpu.SemaphoreType.DMA((3 * n,))], start, finish)


def _rs_sum_chips(name, part, recv, chip):
    _, half, cdim = recv.shape
    tr = _row_tile(half, cdim, mult=16)

    def body(chip_ref, p_ref, r_ref, o_ref):
        o_ref[...] = ((p_ref[...].astype(F32) + r_ref[0].astype(F32)) + r_ref[1].astype(F32)
                      ) + r_ref[2].astype(F32)

    return _pc(body, name=name, out_shape=jax.ShapeDtypeStruct((half, cdim), F32),
               grid=(half // tr,), nsp=1,
               in_specs=[pl.BlockSpec((None, tr, cdim), lambda i, chip_ref: (chip_ref[0], i, 0)),
                         pl.BlockSpec((3, tr, cdim), lambda i, chip_ref: (0, i, 0))],
               out_specs=pl.BlockSpec((tr, cdim), lambda i, chip_ref: (i, 0)),
               sem=("parallel",))(chip, part, recv)


def _rs_send_halves(halves):
    n = len(halves)

    def body(*refs):
        ins, outs = refs[:n], refs[n:2 * n]
        send_sems, recv_sems = refs[2 * n:]
        x, y, c = lax.axis_index("x"), lax.axis_index("y"), lax.axis_index("c")
        sends = []
        for ti in range(n):
            cp = pltpu.make_async_remote_copy(
                src_ref=ins[ti], dst_ref=outs[ti],
                send_sem=send_sems.at[ti], recv_sem=recv_sems.at[ti],
                device_id=(x, y, 1 - c), device_id_type=MESH)
            cp.start()
            sends.append(cp)
        for cp in sends:
            cp.wait()

    return _pc(body, name="rs_send_halves",
               out_shape=tuple(jax.ShapeDtypeStruct(hv.shape, hv.dtype) for hv in halves),
               in_specs=[ANY] * n, out_specs=tuple([ANY] * n),
               scratch=[pltpu.SemaphoreType.DMA((n,)), pltpu.SemaphoreType.DMA((n,))])(*halves)


def _adamw_rows(name, mine, theirs, w, m, v, core):
    half, cdim = mine.shape
    tr = _row_tile(half, cdim, budget=1 << 19)
    nrh = half // tr

    def body(c_ref, mine_ref, theirs_ref, w_ref, m_ref, v_ref, g_ref, d_ref, m2_ref, v2_ref):
        is_mine = (pl.program_id(0) // nrh) == c_ref[0]
        g = jnp.where(is_mine, mine_ref[...], theirs_ref[...])
        d, m2, v2 = _adamw(w_ref[...], g, m_ref[...], v_ref[...])
        g_ref[...] = g
        d_ref[...] = d
        m2_ref[...] = m2
        v2_ref[...] = v2

    htile = pl.BlockSpec((tr, cdim), lambda i, c_ref: (i % nrh, 0))
    tile = pl.BlockSpec((tr, cdim), lambda i, c_ref: (i, 0))
    shp = jax.ShapeDtypeStruct((2 * half, cdim), F32)
    return _pc(body, name=name, out_shape=(shp, shp, shp, shp), grid=(2 * nrh,), nsp=1,
               in_specs=[htile, htile, tile, tile, tile], out_specs=(tile, tile, tile, tile),
               sem=("parallel",))(core, mine, theirs, w, m, v)


def _adamw_whole(name, g, w, m, v):
    def body(g_ref, w_ref, m_ref, v_ref, d_ref, m2_ref, v2_ref):
        d, m2, v2 = _adamw(w_ref[...], g_ref[...], m_ref[...], v_ref[...])
        d_ref[...] = d
        m2_ref[...] = m2
        v2_ref[...] = v2

    shp = jax.ShapeDtypeStruct(g.shape, F32)
    return _pc(body, name=name, out_shape=(shp, shp, shp))(g, w, m, v)


SMALL_LAYOUT = (
    ("sgu_w_s", 1024, 1, 0),
    ("sgu_b_s", 8, 1, 1024),
    ("sgu_norm_g", 1, 0, 0),
    ("sgu_norm_b", 1, 0, 1),
    ("hgrn_norm_g", 1, 0, 3),
    ("ln1_g", 1, 0, 4),
    ("ln1_b", 1, 0, 5),
    ("ffn_conv_b", 1, 2, 3),
    ("ln2_g", 1, 0, 6),
    ("ln2_b", 1, 0, 7),
)
LB_ROW = 2
LOSS_ROW = 8
PACK_SHAPES = ((16, D_MODEL), (N_GROUP * 128 + 16, 128), (8, D_FF))
GATH_DTYPES = (F32, BF16, F32)


def _small_allreduce_adamw(rows1024, dws, dbs, dcw, dcb, logits, m_logits, v_logits,
                           small_w, small_m, small_v):
    ns = len(SMALL_LAYOUT)
    nr = len(rows1024)
    nb = len(PACK_SHAPES)

    def body(*refs):
        row_refs = refs[:nr]
        dws_ref, dbs_ref, dcw_ref, dcb_ref, lg_ref, mlg_ref, vlg_ref = refs[nr:nr + 7]
        pos = nr + 7
        w_refs = refs[pos:pos + ns]
        m_refs = refs[pos + ns:pos + 2 * ns]
        v_refs = refs[pos + 2 * ns:pos + 3 * ns]
        pos += 3 * ns
        loss_ref, dcw_out = refs[pos:pos + 2]
        lg_outs = refs[pos + 2:pos + 6]
        pos += 6
        outs = refs[pos:pos + 4 * ns]
        pos += 4 * ns
        pack = refs[pos:pos + nb]
        sib = refs[pos + nb:pos + 2 * nb]
        gath = refs[pos + 2 * nb:pos + 3 * nb]
        d2d_send, d2d_recv, ici_send, ici_recv = refs[pos + 3 * nb:]

        x, y, c, peers = _chip_peers()
        me = 2 * x + y
        sibling = (x, y, 1 - c)

        pack[0][...] = jnp.zeros(PACK_SHAPES[0], F32)
        for k in range(nr):
            pack[0][k:k + 1, :] = row_refs[k][0:1, :]
        pack[1][0:N_GROUP * 128, :] = dws_ref[...]
        pack[1][N_GROUP * 128:N_GROUP * 128 + 8, :] = dbs_ref[...]
        pack[1][N_GROUP * 128 + 8:, :] = jnp.zeros((8, 128), F32)
        pack[2][...] = jnp.zeros(PACK_SHAPES[2], F32)
        pack[2][0:3, :] = dcw_ref[0:3, :]
        pack[2][3:4, :] = dcb_ref[0:1, :]

        d2d = [pltpu.make_async_remote_copy(
            src_ref=pack[b], dst_ref=sib[b], send_sem=d2d_send.at[b], recv_sem=d2d_recv.at[b],
            device_id=sibling, device_id_type=MESH) for b in range(nb)]
        for cp in d2d:
            cp.start()
        for cp in d2d:
            cp.wait()
        for b in range(nb):
            gath[b][me] = (pack[b][...] + sib[b][...]).astype(GATH_DTYPES[b])

        ici, ici_wait = [], []
        for b in range(nb):
            for k, (px, py) in enumerate(peers):
                sem = dict(send_sem=ici_send.at[b * 3 + k], recv_sem=ici_recv.at[b * 3 + k],
                           device_id=(px, py, c), device_id_type=MESH)
                ici.append(pltpu.make_async_remote_copy(src_ref=gath[b].at[me], dst_ref=gath[b].at[me], **sem))
                ici_wait.append(pltpu.make_async_remote_copy(
                    src_ref=gath[b].at[me], dst_ref=gath[b].at[2 * px + py], **sem))
        for cp in ici:
            cp.start()
        for cp in ici_wait:
            cp.wait_recv()
        for cp in ici:
            cp.wait_send()

        tot = pack
        for b in range(nb):
            tot[b][...] = ((gath[b][0].astype(F32) + gath[b][1].astype(F32)) + gath[b][2].astype(F32)
                           ) + gath[b][3].astype(F32)

        loss_ref[...] = tot[0][LOSS_ROW:LOSS_ROW + 1, :]
        dcw_out[...] = tot[2][...]
        lb = _sig(lg_ref[0:1, :] - lg_ref[1:2, :])
        d0 = tot[0][LB_ROW:LB_ROW + 1, :] * lb * (1.0 - lb)
        rowid = lax.broadcasted_iota(jnp.int32, (2, D_MODEL), 0)
        g_lg = jnp.where(rowid == 0, d0, -d0)
        dl, ml, vl = _adamw(lg_ref[...], g_lg, mlg_ref[...], vlg_ref[...])
        lg_outs[0][...] = g_lg
        lg_outs[1][...] = dl
        lg_outs[2][...] = ml
        lg_outs[3][...] = vl
        for si, (_, rows, b, r0) in enumerate(SMALL_LAYOUT):
            g = tot[b][r0:r0 + rows, :]
            dl, ml, vl = _adamw(w_refs[si][...], g, m_refs[si][...], v_refs[si][...])
            outs[4 * si][...] = g
            outs[4 * si + 1][...] = dl
            outs[4 * si + 2][...] = ml
            outs[4 * si + 3][...] = vl

    shapes = [jax.ShapeDtypeStruct((1, D_MODEL), F32), jax.ShapeDtypeStruct((8, D_FF), F32)]
    shapes += [jax.ShapeDtypeStruct((2, D_MODEL), F32)] * 4
    for w in small_w:
        shapes += [jax.ShapeDtypeStruct(w.shape, F32)] * 4
    scratch = [pltpu.VMEM(shp, F32) for shp in PACK_SHAPES]
    scratch += [pltpu.VMEM(shp, F32) for shp in PACK_SHAPES]
    scratch += [pltpu.VMEM((N_CHIP,) + shp, dt) for shp, dt in zip(PACK_SHAPES, GATH_DTYPES)]
    scratch += [pltpu.SemaphoreType.DMA((nb,)), pltpu.SemaphoreType.DMA((nb,)),
                pltpu.SemaphoreType.DMA((3 * nb,)), pltpu.SemaphoreType.DMA((3 * nb,))]
    vm = pl.BlockSpec(memory_space=pltpu.VMEM)
    n_in = nr + 7 + 3 * ns
    res = _pc(body, name="small_allreduce_adamw", out_shape=tuple(shapes),
              in_specs=[vm] * n_in, out_specs=tuple([vm] * len(shapes)),
              scratch=scratch)(*rows1024, dws, dbs, dcw, dcb, logits, m_logits, v_logits,
                               *small_w, *small_m, *small_v)
    return res[0], res[1], res[2:6], res[6:]


def kernel(x, p, w_in, sgu_w_s, sgu_b_s, sgu_norm_g, sgu_norm_b, hgrn_lb_logits, hgrn_norm_g, w_branch, w_out, ln1_g, ln1_b, ffn_w_up, ffn_conv_w, ffn_conv_b, ffn_w_down, ln2_g, ln2_b, ple_w_proj, ple_w_gate, loss_target, m_w_in, m_sgu_w_s, m_sgu_b_s, m_sgu_norm_g, m_sgu_norm_b, m_hgrn_lb_logits, m_hgrn_norm_g, m_w_branch, m_w_out, m_ln1_g, m_ln1_b, m_ffn_w_up, m_ffn_conv_w, m_ffn_conv_b, m_ffn_w_down, m_ln2_g, m_ln2_b, m_ple_w_proj, m_ple_w_gate, v_w_in, v_sgu_w_s, v_sgu_b_s, v_sgu_norm_g, v_sgu_norm_b, v_hgrn_lb_logits, v_hgrn_norm_g, v_w_branch, v_w_out, v_ln1_g, v_ln1_b, v_ffn_w_up, v_ffn_conv_w, v_ffn_conv_b, v_ffn_w_down, v_ln2_g, v_ln2_b, v_ple_w_proj, v_ple_w_gate):
    t = x.shape[1]
    x2 = x.reshape(t, D_MODEL)
    x2b = x2.astype(BF16)
    p2 = p.reshape(t, PLE_DIM)
    tgt = loss_target.reshape(t, D_MODEL)
    core = lax.axis_index("c").astype(jnp.int32).reshape(1)
    chip_id = (2 * lax.axis_index("x") + lax.axis_index("y")).astype(jnp.int32).reshape(1)

    big_w = [w_in[0], w_branch[0, 0], w_branch[0, 1], w_out[0], ffn_w_up[0], ffn_w_down[0],
             ple_w_proj[0], ple_w_gate[0]]
    big_m = [m_w_in[0], m_w_branch[0, 0], m_w_branch[0, 1], m_w_out[0], m_ffn_w_up[0],
             m_ffn_w_down[0], m_ple_w_proj[0], m_ple_w_gate[0]]
    big_v = [v_w_in[0], v_w_branch[0, 0], v_w_branch[0, 1], v_w_out[0], v_ffn_w_up[0],
             v_ffn_w_down[0], v_ple_w_proj[0], v_ple_w_gate[0]]
    def halves_of(i):
        w = big_w[i]
        return w.astype(BF16).reshape(2, w.shape[0] // 2, w.shape[1])

    def stacked(g, i):
        return g.reshape(N_CHIP, big_w[i].shape[0], big_w[i].shape[1])


    cid = jnp.arange(SGU_BLOCK) // CHUNK
    maskf = (cid[:, None] >= cid[None, :]).astype(F32)
    ws_masked = sgu_w_s[0] * maskf[None]
    wm = ws_masked.astype(BF16)
    wmt = jnp.transpose(ws_masked, (0, 2, 1)).astype(BF16)
    bsb = jnp.broadcast_to(sgu_b_s[0][:, :, None], (N_GROUP, SGU_BLOCK, 128))

    up_rows = big_w[4].shape[0] // 2
    up_blocks = [big_w[4][k * up_rows:(k + 1) * up_rows].astype(BF16).reshape(2, up_rows // 2, -1)
                 for k in range(2)]
    h, win_g, (up_g,) = _in_proj_gathering(x2b, halves_of(0), chip_id, 512,
                                           _gather_comm([up_blocks[0]], blocks={0: (2, 0, None)}))
    win_st = stacked(win_g, 0)
    ya, _ = _sgu_fwd(h, wm, bsb, sgu_norm_g, sgu_norm_b)
    (yb, o_all, st_all), mix_g = _hgrn_fwd(
        h, hgrn_lb_logits, hgrn_norm_g,
        comm=_gather_comm([halves_of(i) for i in (1, 2, 3)] + [up_blocks[1]], [ffn_conv_w[0]],
                          blocks={3: (2, 1, up_g)}))
    wb0, wb1, wo = [stacked(g, i).reshape(D_MODEL, D_MODEL) for g, i in zip(mix_g[:3], (1, 2, 3))]
    wup_st = stacked(mix_g[3], 4)
    convw = jnp.transpose(mix_g[4], (1, 0, 2)).reshape(3, D_FF)
    (r1, a_br, b_br, m_bf, x1b), _ = _mix_fwd(ya, yb, h, x2, wb0, wb1, wo, ln1_g, ln1_b, 256)
    h2, act, out_g = _ffn_up_act(x1b, wup_st, convw, ffn_conv_b, 512,
                                 _gather_comm([halves_of(i) for i in (5, 6, 7)]))
    wd = stacked(out_g[0], 5).reshape(D_FF, D_MODEL)
    wpp = jnp.transpose(stacked(out_g[1], 6), (1, 0, 2)).reshape(PLE_DIM, D_MODEL)
    wpg = stacked(out_g[2], 7).reshape(D_MODEL, D_MODEL)
    dr2, dpg, dpp, loss_acc, dg2, db2 = _out_fwd_bwd(
        act, x1b, r1, p2, tgt, wd, wpg, wpp, ln1_g, ln1_b, ln2_g, ln2_b, 256)

    dh2, dr1, dcw, dcb, dg1, db1 = _ffn_bwd(h2, dr2, wd, wup_st, dpg, wpg, r1, ln1_g, convw, 256)
    d_wd = _mm_tn("ffn_down_wgrad", act, dr2, FF_TILE, 512)
    d_wpg = _mm_tn("ple_gate_wgrad", x1b, dpg, 512, D_MODEL)
    d_wpp_st = _mm_tn("ple_proj_wgrad", p2, dpp, PLE_DIM, PLE_DIM, stacked=True)
    d_wup_st = _mm("ffn_up_wgrad", x1b, dh2, TN, (2, N_CHIP),
                   pl.BlockSpec((t, 512), lambda i, j: (0, i)),
                   pl.BlockSpec((None, t, FF_TILE), lambda i, j: (j // FF_NJ, 0, j % FF_NJ)),
                   jax.ShapeDtypeStruct((N_CHIP, D_MODEL, FF_TILE), BF16),
                   pl.BlockSpec((None, 512, FF_TILE), lambda i, j: (j, i, 0)))
    da_bf, db_bf, dh, dya, dyb = _mix_bwd(dr1, h, a_br, b_br, wo, wb0, wb1, 256)
    d_wo = _mm_tn("out_proj_wgrad", m_bf, dr1, 512, 512)
    d_wb0 = _mm_tn("branch0_wgrad", ya, da_bf, 512, D_MODEL)
    d_wb1 = _mm_tn("branch1_wgrad", yb, db_bf, 512, D_MODEL)
    grads_1 = [d_wb0.reshape(4, 256, D_MODEL), d_wb1.reshape(4, 256, D_MODEL),
               d_wo.reshape(4, 256, D_MODEL), d_wup_st, d_wd.reshape(4, D_FF // 4, D_MODEL),
               d_wpp_st, d_wpg.reshape(4, 256, D_MODEL)]
    (dh, dws, dbs, dgv, dbv), recv_a1 = _sgu_bwd(h, dya, wm, wmt, bsb, sgu_norm_g, sgu_norm_b, maskf, dh,
                                                 comm=_sibling_exchange_comm(grads_1))
    parts_1 = [_rs_add_halves("rs_add_halves%d" % (i + 1), g, r, core)
               for i, (g, r) in enumerate(zip(grads_1, recv_a1))]
    (dh, dlb, dgn), recv_b1 = _hgrn_bwd(h, o_all, dyb, st_all, hgrn_lb_logits, hgrn_norm_g, dh,
                                         comm=_chip_exchange_comm(parts_1))

    grads_0 = [_in_proj_wgrad(x2b, dh, 512, D_MODEL)]
    recv_a0 = _run_comm("rs_sibling_exchange0", _sibling_exchange_comm(grads_0))
    parts_0 = [_rs_add_halves("rs_add_halves0", grads_0[0], recv_a0[0], core)]
    gx, recv_b0 = _in_proj_xgrad(dh, win_st, dr1, 512, _chip_exchange_comm(parts_0))
    parts = parts_0 + parts_1
    recv_b = list(recv_b0) + list(recv_b1)
    halves = [_rs_sum_chips("rs_sum_chips%d" % i, pt, r, chip_id)
              for i, (pt, r) in enumerate(zip(parts, recv_b))]
    theirs = _rs_send_halves(halves)
    big_out = [_adamw_rows("adamw_big%d" % i, halves[i], theirs[i], big_w[i], big_m[i], big_v[i], core)
               for i in range(len(halves))]

    small_in = dict(sgu_w_s=(sgu_w_s, m_sgu_w_s, v_sgu_w_s), sgu_b_s=(sgu_b_s, m_sgu_b_s, v_sgu_b_s),
                    sgu_norm_g=(sgu_norm_g, m_sgu_norm_g, v_sgu_norm_g),
                    sgu_norm_b=(sgu_norm_b, m_sgu_norm_b, v_sgu_norm_b),
                    hgrn_norm_g=(hgrn_norm_g, m_hgrn_norm_g, v_hgrn_norm_g),
                    ln1_g=(ln1_g, m_ln1_g, v_ln1_g), ln1_b=(ln1_b, m_ln1_b, v_ln1_b),
                    ffn_conv_b=(ffn_conv_b, m_ffn_conv_b, v_ffn_conv_b),
                    ln2_g=(ln2_g, m_ln2_g, v_ln2_g), ln2_b=(ln2_b, m_ln2_b, v_ln2_b))

    def flat(name, arr):
        rows = dict((n, r) for n, r, _, _ in SMALL_LAYOUT)[name]
        return arr.reshape(rows, arr.size // rows)

    names = [n for n, _, _, _ in SMALL_LAYOUT]
    sw = [flat(n, small_in[n][0]) for n in names]
    sm = [flat(n, small_in[n][1]) for n in names]
    sv = [flat(n, small_in[n][2]) for n in names]
    loss_rows, dcw_tot, lg_out, small_out = _small_allreduce_adamw(
        [dgv, dbv, dlb, dgn, dg1, db1, dg2, db2, loss_acc], dws.reshape(N_GROUP * 128, 128), dbs, dcw, dcb,
        hgrn_lb_logits, m_hgrn_lb_logits, v_hgrn_lb_logits, sw, sm, sv)
    loss = loss_rows[0, 0]

    chip = 2 * lax.axis_index("x") + lax.axis_index("y")
    g_cw = lax.dynamic_slice(dcw_tot, (0, chip * (D_FF // 4)), (3, D_FF // 4))
    cw_out = _adamw_whole("adamw_conv_w", g_cw, ffn_conv_w[0], m_ffn_conv_w[0], v_ffn_conv_w[0])

    res = {}
    for si, n in enumerate(names):
        shp = small_in[n][0].shape
        res[n] = tuple(small_out[4 * si + k].reshape(shp) for k in range(4))
    res["hgrn_lb_logits"] = tuple(lg_out)
    res["ffn_conv_w"] = (g_cw[None],) + tuple(o[None] for o in cw_out)

    def big(i):
        return tuple(big_out[i])

    res["w_in"] = tuple(o[None] for o in big(0))
    res["w_branch"] = tuple(jnp.stack([o0, o1])[None] for o0, o1 in zip(big(1), big(2)))
    res["w_out"] = tuple(o[None] for o in big(3))
    res["ffn_w_up"] = tuple(o[None] for o in big(4))
    res["ffn_w_down"] = tuple(o[None] for o in big(5))
    res["ple_w_proj"] = tuple(o[None] for o in big(6))
    res["ple_w_gate"] = tuple(o[None] for o in big(7))

    order = ["w_in", "sgu_w_s", "sgu_b_s", "sgu_norm_g", "sgu_norm_b", "hgrn_lb_logits",
             "hgrn_norm_g", "w_branch", "w_out", "ln1_g", "ln1_b", "ffn_w_up", "ffn_conv_w",
             "ffn_conv_b", "ffn_w_down", "ln2_g", "ln2_b", "ple_w_proj", "ple_w_gate"]
    outs = [loss, gx]
    for k in range(4):
        outs += [res[n][k] for n in order]
    return tuple(outs)
```

```python
import jax
import jax.numpy as jnp
from jax import lax
from jax.experimental import pallas as pl
from jax.experimental.pallas import tpu as pltpu

F32 = jnp.float32
BF16 = jnp.bfloat16
HIGHEST = lax.Precision.HIGHEST
MESH = pl.DeviceIdType.MESH

D_MODEL = 1024
CHUNK = 64
SGU_BLOCK = 128
SGU_STEP_BLOCKS = 4
SGU_ROWS = SGU_STEP_BLOCKS * SGU_BLOCK
N_GROUP = 8
N_HEAD = 8
HEAD_DIM = 128
D_FF = 2816
PLE_DIM = 256
LN_EPS = 1e-5
RMS_EPS = 1e-6
ALPHA = 2.0 ** 0.25
N_CHIP = 4

ADAM_LR = 0.001
ADAM_B1 = 0.9
ADAM_B2 = 0.999
ADAM_EPS = 1e-08
ADAM_WD = 0.01
ADAM_STEP = 10

VMEM_LIMIT = 56 * 1024 * 1024

NN = (((1,), (0,)), ((), ()))
NT = (((1,), (1,)), ((), ()))
TN = (((0,), (0,)), ((), ()))


def _pc(body, *, name, out_shape, grid=None, in_specs=None, out_specs=None, scratch=(),
        sem=None, nsp=0, vmem=VMEM_LIMIT, aliases=None):
    params = dict(vmem_limit_bytes=vmem)
    if sem is not None:
        params["dimension_semantics"] = sem
    kw = dict(name=name, out_shape=out_shape, compiler_params=pltpu.CompilerParams(**params))
    if aliases:
        kw["input_output_aliases"] = aliases
    if nsp:
        kw["grid_spec"] = pltpu.PrefetchScalarGridSpec(
            num_scalar_prefetch=nsp, grid=grid, in_specs=in_specs, out_specs=out_specs,
            scratch_shapes=list(scratch))
    else:
        if grid is not None:
            kw["grid"] = grid
        if in_specs is not None:
            kw["in_specs"] = in_specs
            kw["out_specs"] = out_specs
        kw["scratch_shapes"] = list(scratch)
    return pl.pallas_call(body, **kw)


def _dot(a, b, dims=NN):
    return lax.dot_general(a.astype(BF16), b.astype(BF16), dims, preferred_element_type=F32)


def _dot32(a, b, dims=NN):
    return lax.dot_general(a, b, dims, precision=HIGHEST, preferred_element_type=F32)


def _sig(x):
    return 1.0 / (1.0 + jnp.exp(-x))


_GC = 0.7978845608028654
_GA = 0.044715


def _gelu(x):
    return 0.5 * x * (1.0 + jnp.tanh(_GC * (x + _GA * x * x * x)))


def _gelu_and_grad(x):
    t = jnp.tanh(_GC * (x + _GA * x * x * x))
    g = 0.5 * x * (1.0 + t)
    dg = 0.5 * (1.0 + t) + 0.5 * x * (1.0 - t * t) * _GC * (1.0 + 3.0 * _GA * x * x)
    return g, dg


def _ln_stats(r):
    mu = jnp.mean(r, axis=-1, keepdims=True)
    xc = r - mu
    var = jnp.mean(xc * xc, axis=-1, keepdims=True)
    rstd = lax.rsqrt(var + LN_EPS)
    return xc * rstd, rstd


def _ln_bwd(dxh, xh, rstd):
    m1 = jnp.mean(dxh, axis=-1, keepdims=True)
    m2 = jnp.mean(dxh * xh, axis=-1, keepdims=True)
    return rstd * (dxh - m1 - xh * m2)


def _colsum8(v):
    return jnp.broadcast_to(jnp.sum(v, axis=0, keepdims=True), (8, v.shape[1]))


def _adamw(w, g, m, v):
    m2 = ADAM_B1 * m + (1.0 - ADAM_B1) * g
    v2 = ADAM_B2 * v + (1.0 - ADAM_B2) * (g * g)
    m_hat = m2 / (1.0 - ADAM_B1 ** ADAM_STEP)
    v_hat = v2 / (1.0 - ADAM_B2 ** ADAM_STEP)
    delta = -ADAM_LR * (m_hat / (jnp.sqrt(v_hat) + ADAM_EPS) + ADAM_WD * w)
    return delta, m2, v2


def _row_tile(rows, cols, itemsize=4, budget=1 << 20, mult=8):
    best = mult
    for tr in range(mult, rows + 1, mult):
        if rows % tr == 0 and tr * cols * itemsize <= budget:
            best = tr
    return best


def _mm(name, a, b, dims, grid, a_spec, b_spec, out_shape, o_spec):
    out_dtype = out_shape.dtype

    def body(a_ref, b_ref, o_ref):
        o_ref[...] = _dot(a_ref[...], b_ref[...], dims).astype(out_dtype)

    return _pc(body, name=name, out_shape=out_shape, grid=grid, in_specs=[a_spec, b_spec],
               out_specs=o_spec, sem=("parallel", "parallel"))(a, b)


class _Comm:
    def __init__(self, ins, out_shapes, sems, start, finish, middle=None, finish_late=None, aliases=None):
        self.ins, self.out_shapes, self.sems = list(ins), list(out_shapes), list(sems)
        self.start, self.finish = start, finish
        self.middle, self.finish_late = middle, finish_late
        self.aliases = aliases or {}


def _hosted_call(body, comm, first, last, *, name, out_shape, grid, in_specs, out_specs, scratch, sem,
                 args, aliases=None, mid=None):
    n_in, n_out, n_scr = len(in_specs), len(out_shape), len(scratch)
    nci, nco = len(comm.ins), len(comm.out_shapes)

    def wrapped(*refs):
        pos = n_in
        own_in, c_in = refs[:pos], refs[pos:pos + nci]
        pos += nci
        own_out, c_out = refs[pos:pos + n_out], refs[pos + n_out:pos + n_out + nco]
        pos += n_out + nco
        own_scr, c_sem = refs[pos:pos + n_scr], refs[pos + n_scr:]

        @pl.when(first())
        def _():
            comm.start(c_in, c_out, c_sem)

        body(*own_in, *own_out, *own_scr)

        if mid is not None:
            @pl.when(mid())
            def _():
                comm.middle(c_in, c_out, c_sem)

        @pl.when(last())
        def _():
            (comm.finish if mid is None else comm.finish_late)(c_in, c_out, c_sem)

    return _pc(wrapped, name=name, out_shape=tuple(out_shape) + tuple(comm.out_shapes), grid=grid,
               in_specs=list(in_specs) + [ANY] * nci, out_specs=tuple(out_specs) + tuple([ANY] * nco),
               scratch=list(scratch) + comm.sems, sem=sem,
               aliases={**(aliases or {}), **{n_in + ci: n_out + co for ci, co in comm.aliases.items()}},
               )(*args, *comm.ins)


def _grid1_call(body, comm, n, *, name, out_shape, in_specs, out_specs, scratch, args, aliases=None):
    if comm is None:
        return _pc(body, name=name, out_shape=out_shape, grid=(n,), in_specs=in_specs, out_specs=out_specs,
                   scratch=scratch, sem=("arbitrary",), aliases=aliases)(*args), ()
    res = _hosted_call(body, comm, lambda: pl.program_id(0) == 0, lambda: pl.program_id(0) == n - 1,
                       name=name, out_shape=out_shape, grid=(n,), in_specs=in_specs,
                       out_specs=out_specs, scratch=scratch, sem=("arbitrary",), args=args, aliases=aliases)
    return res[:len(out_shape)], res[len(out_shape):]


def _run_comm(name, comm):
    nci, nco = len(comm.ins), len(comm.out_shapes)

    def body(*refs):
        c_in, c_out, c_sem = refs[:nci], refs[nci:nci + nco], refs[nci + nco:]
        comm.start(c_in, c_out, c_sem)
        comm.finish(c_in, c_out, c_sem)

    return _pc(body, name=name, out_shape=tuple(comm.out_shapes), in_specs=[ANY] * nci,
               out_specs=tuple([ANY] * nco), scratch=comm.sems)(*comm.ins)


def _mm_tn(name, a, b, tm, tn, stacked=False):
    t, m = a.shape
    _, n = b.shape
    if stacked:
        assert tm == m
        out_shape = jax.ShapeDtypeStruct((n // tn, m, tn), BF16)
        o_spec = pl.BlockSpec((None, tm, tn), lambda i, j: (j, 0, 0))
    else:
        out_shape = jax.ShapeDtypeStruct((m, n), BF16)
        o_spec = pl.BlockSpec((tm, tn), lambda i, j: (i, j))
    return _mm(name, a, b, TN, (m // tm, n // tn),
               pl.BlockSpec((t, tm), lambda i, j: (0, i)),
               pl.BlockSpec((t, tn), lambda i, j: (0, j)),
               out_shape, o_spec)


DH_SLOT = (2, 0, 1, 3)


def _dh_slot(j):
    return jnp.where(j == 3, 3, (j + 2) % 3)


def _in_proj_wgrad(x2b, dh, tm, tn):
    t, m = x2b.shape
    n = dh.shape[2]

    def body(a_ref, b_ref, o_ref):
        o_ref[...] = _dot(a_ref[...], b_ref[...], TN).astype(BF16)

    return _pc(body, name="in_proj_wgrad", out_shape=jax.ShapeDtypeStruct((N_CHIP, m, n), BF16),
               grid=(N_CHIP, m // tm, n // tn),
               in_specs=[pl.BlockSpec((t, tm), lambda j, i, k: (0, i)),
                         pl.BlockSpec((None, t, tn), lambda j, i, k: (_dh_slot(j), 0, k))],
               out_specs=pl.BlockSpec((None, tm, tn), lambda j, i, k: (j, i, k)),
               sem=("parallel", "parallel", "parallel"))(x2b, dh)


def _in_proj_xgrad(dh, win_st, dr1, tm, comm):
    t = dr1.shape[0]
    ni = t // tm

    def body(a_ref, b_ref, add_ref, o_ref, acc):
        j = pl.program_id(1)
        prod = _dot(a_ref[...], b_ref[...], NT)

        @pl.when(j == 0)
        def _():
            acc[...] = prod + ALPHA * add_ref[...].astype(F32)

        @pl.when((j > 0) & (j < N_CHIP - 1))
        def _():
            acc[...] += prod

        @pl.when(j == N_CHIP - 1)
        def _():
            o_ref[...] = acc[...] + prod

    tile = pl.BlockSpec((tm, D_MODEL), lambda i, j: (i, 0))
    res = _hosted_call(body, comm,
                       lambda: (pl.program_id(0) == 0) & (pl.program_id(1) == 0),
                       lambda: (pl.program_id(0) == ni - 1) & (pl.program_id(1) == N_CHIP - 1),
                       name="in_proj_xgrad", out_shape=(jax.ShapeDtypeStruct((1, t, D_MODEL), F32),),
                       grid=(ni, N_CHIP),
                       in_specs=[pl.BlockSpec((None, tm, 2 * D_MODEL), lambda i, j: (_dh_slot(j), i, 0)),
                                 pl.BlockSpec((None, D_MODEL, 2 * D_MODEL), lambda i, j: (j, 0, 0)),
                                 tile],
                       out_specs=(pl.BlockSpec((None, tm, D_MODEL), lambda i, j: (0, i, 0)),),
                       scratch=[pltpu.VMEM((tm, D_MODEL), F32)],
                       sem=("arbitrary", "arbitrary"), args=[dh, win_st, dr1])
    return res[0], res[1:]


def _sgu_mixed(v, wm_ref, bsb_ref, gv, bv):
    gl, dgl = _gelu_and_grad(v)
    vh, rstd = _ln_stats(gl)
    vn = vh * gv + bv
    mixed = []
    for g in range(N_GROUP):
        sl = slice(g * 128, (g + 1) * 128)
        mixed.append(_dot(wm_ref[g], vn[:, sl]) + bsb_ref[g])
    return dgl, vh, rstd, vn, mixed


def _sgu_fwd(h, wm, bsb, gv, bv, comm=None):
    t = h.shape[0]

    def body(u_ref, v_ref, wm_ref, bsb_ref, gv_ref, bv_ref, ya_ref):
        for bb in range(SGU_STEP_BLOCKS):
            rows = slice(bb * SGU_BLOCK, (bb + 1) * SGU_BLOCK)
            u = u_ref[rows, :].astype(F32)
            _, _, _, _, mixed = _sgu_mixed(v_ref[rows, :].astype(F32), wm_ref, bsb_ref, gv_ref[...],
                                           bv_ref[...])
            gu = _gelu(u)
            for g in range(N_GROUP):
                sl = slice(g * 128, (g + 1) * 128)
                ya_ref[rows, sl] = (gu[:, sl] * mixed[g]).astype(BF16)

    full3 = pl.BlockSpec((N_GROUP, 128, 128), lambda i: (0, 0, 0))
    vec = pl.BlockSpec((1, D_MODEL), lambda i: (0, 0))
    (ya,), extra = _grid1_call(
        body, comm, t // SGU_ROWS, name="sgu_fwd",
        out_shape=(jax.ShapeDtypeStruct((t, D_MODEL), BF16),),
        in_specs=[pl.BlockSpec((SGU_ROWS, D_MODEL), lambda i: (i, 0)),
                  pl.BlockSpec((SGU_ROWS, D_MODEL), lambda i: (i, 1)),
                  full3, full3, vec, vec],
        out_specs=(pl.BlockSpec((SGU_ROWS, D_MODEL), lambda i: (i, 0)),),
        scratch=[], args=(h, h, wm, bsb, gv, bv))
    return ya, extra


def _sgu_bwd(h, dya, wm, wmt, bsb, gv, bv, maskf, dh_buf, comm=None):
    t = h.shape[0]
    nb = t // SGU_ROWS

    def body(u_ref, v_ref, dya_ref, wm_ref, wmt_ref, bsb_ref, gv_ref, bv_ref, mask_ref, dh_buf_ref,
             dh_ref, dws_ref, dbs_ref, dgv_ref, dbv_ref, dmix_acc):
        i = pl.program_id(0)

        @pl.when(i == 0)
        def _():
            dws_ref[...] = jnp.zeros_like(dws_ref)
            dgv_ref[...] = jnp.zeros_like(dgv_ref)
            dbv_ref[...] = jnp.zeros_like(dbv_ref)
            dmix_acc[...] = jnp.zeros_like(dmix_acc)

        gvv = gv_ref[...]
        for bb in range(SGU_STEP_BLOCKS):
            rows = slice(bb * SGU_BLOCK, (bb + 1) * SGU_BLOCK)
            u = u_ref[rows, :].astype(F32)
            dgl_v, vh, rstd, vn, mixed = _sgu_mixed(v_ref[rows, :].astype(F32), wm_ref, bsb_ref, gvv,
                                                    bv_ref[...])
            gu, dgl_u = _gelu_and_grad(u)
            dya_v = dya_ref[rows, :].astype(F32)
            dvn_parts = []
            for g in range(N_GROUP):
                sl = slice(g * 128, (g + 1) * 128)
                d_y = dya_v[:, sl]
                dh_ref[rows, sl] = (d_y * mixed[g] * dgl_u[:, sl]).astype(BF16)
                d_mixed = d_y * gu[:, sl]
                dmix_acc[g] += d_mixed
                dws_ref[g] += _dot(d_mixed, vn[:, sl], NT) * mask_ref[...]
                dvn_parts.append(_dot(wmt_ref[g], d_mixed))
            dvn = jnp.concatenate(dvn_parts, axis=1)
            dgv_ref[...] += _colsum8(dvn * vh)
            dbv_ref[...] += _colsum8(dvn)
            d_gl = _ln_bwd(dvn * gvv, vh, rstd)
            dh_ref[rows, D_MODEL:] = (d_gl * dgl_v).astype(BF16)

        @pl.when(i == nb - 1)
        def _():
            rowid = lax.broadcasted_iota(jnp.int32, (8, 128), 0)
            ones = jnp.ones((8, 128), F32)
            acc = jnp.zeros((8, 128), F32)
            for g in range(N_GROUP):
                rs = _dot32(ones, dmix_acc[g], NT)
                acc = jnp.where(rowid == g, rs, acc)
            dbs_ref[...] = acc

    full3 = pl.BlockSpec((N_GROUP, 128, 128), lambda i: (0, 0, 0))
    vec = pl.BlockSpec((1, D_MODEL), lambda i: (0, 0))
    acc8 = pl.BlockSpec((8, D_MODEL), lambda i: (0, 0))
    return _grid1_call(
        body, comm, nb, name="sgu_bwd",
        out_shape=(jax.ShapeDtypeStruct(dh_buf.shape, BF16),
                   jax.ShapeDtypeStruct((N_GROUP, 128, 128), F32),
                   jax.ShapeDtypeStruct((8, 128), F32),
                   jax.ShapeDtypeStruct((8, D_MODEL), F32),
                   jax.ShapeDtypeStruct((8, D_MODEL), F32)),
        in_specs=[pl.BlockSpec((SGU_ROWS, D_MODEL), lambda i: (i, 0)),
                  pl.BlockSpec((SGU_ROWS, D_MODEL), lambda i: (i, 1)),
                  pl.BlockSpec((SGU_ROWS, D_MODEL), lambda i: (i, 0)),
                  full3, full3, full3, vec, vec,
                  pl.BlockSpec((128, 128), lambda i: (0, 0)), ANY],
        out_specs=(pl.BlockSpec((None, SGU_ROWS, 2 * D_MODEL), lambda i: (DH_SLOT[0], i, 0)),
                   full3, pl.BlockSpec((8, 128), lambda i: (0, 0)), acc8, acc8),
        scratch=[pltpu.VMEM((N_GROUP, 128, 128), F32)],
        args=(h, h, dya, wm, wmt, bsb, gv, bv, maskf, dh_buf), aliases={9: 0})


def _tri_masks():
    row = lax.broadcasted_iota(jnp.int32, (CHUNK, CHUNK), 0)
    col = lax.broadcasted_iota(jnp.int32, (CHUNK, CHUNK), 1)
    return col <= row, col >= row


def _heads(v):
    return [v[:, hd * HEAD_DIM:(hd + 1) * HEAD_DIM] for hd in range(N_HEAD)]


def _tri_cumsum(tri_bf, v):
    hi = v.astype(BF16)
    r = v - hi.astype(F32)
    mid = r.astype(BF16)
    lo = (r - mid.astype(F32)).astype(BF16)
    return _dot(tri_bf, hi) + _dot(tri_bf, mid) + _dot(tri_bf, lo)


def _hgrn_chunk(q, fp, ii, lb, st_heads, causal, with_o=True):
    sg = _sig(fp)
    f = lb + (1.0 - lb) * sg
    k = 1.0 - f
    c = _tri_cumsum(causal.astype(BF16), jnp.log(f))
    ec = jnp.exp(c)
    en = jnp.exp(-c)
    sq = _sig(q)
    qt = q * sq * ec
    kt = k * en
    ecl = jnp.exp(c[CHUNK - 1:CHUNK, :])
    kk = kt * ecl
    qtb, ktb, iib, kkb = qt.astype(BF16), kt.astype(BF16), ii.astype(BF16), kk.astype(BF16)
    attn, o = [], []
    for hd, (qh, kh, ih) in enumerate(zip(_heads(qtb), _heads(ktb), _heads(iib))):
        a = jnp.where(causal, _dot(qh, kh, NT), 0.0).astype(BF16)
        attn.append(a)
        if with_o:
            o.append(_dot(a, ih) + _dot(qh, st_heads[hd], NT))
    return dict(sg=sg, f=f, k=k, ec=ec, en=en, sq=sq, ecl=ecl, kk=kk, qtb=qtb, ktb=ktb, iib=iib,
                kkb=kkb, attn=attn, o=o)


def _rms_heads(o_heads):
    rinv = [lax.rsqrt(jnp.mean(o * o, axis=-1, keepdims=True) + RMS_EPS) for o in o_heads]
    return rinv, jnp.concatenate([o * r for o, r in zip(o_heads, rinv)], axis=1)


HG_CHUNKS = 8
HG_ROWS = HG_CHUNKS * CHUNK


def _hgrn_fwd(h, logits, gn, comm=None):
    t = h.shape[0]
    nb = t // HG_ROWS

    def body(q_ref, f_ref, i_ref, og_ref, lg_ref, gn_ref, yb_ref, o_ref, st_ref, state):
        @pl.when(pl.program_id(0) == 0)
        def _():
            state[...] = jnp.zeros_like(state)

        causal, _ = _tri_masks()
        lb = _sig(lg_ref[0:1, :] - lg_ref[1:2, :])
        gnv = gn_ref[...]
        st = [state[hd] for hd in range(N_HEAD)]
        for cc in range(HG_CHUNKS):
            rows = slice(cc * CHUNK, (cc + 1) * CHUNK)
            og = og_ref[rows, :].astype(F32)
            r = _hgrn_chunk(q_ref[rows, :].astype(F32), f_ref[rows, :].astype(F32),
                            i_ref[rows, :].astype(F32), lb, [s.astype(BF16) for s in st], causal)
            o_bf = jnp.concatenate(r["o"], axis=1).astype(BF16)
            o_ref[rows, :] = o_bf
            _, on = _rms_heads(_heads(o_bf.astype(F32)))
            yb_ref[rows, :] = (on * gnv * (og * _sig(og))).astype(BF16)
            for hd in range(N_HEAD):
                st_ref[cc, hd] = st[hd]
            st = [s * e + _dot(ih, kh, TN)
                  for s, e, ih, kh in zip(st, _heads(r["ecl"]), _heads(r["iib"]), _heads(r["kkb"]))]
        for hd in range(N_HEAD):
            state[hd] = st[hd]

    def col(k):
        return pl.BlockSpec((HG_ROWS, D_MODEL), lambda ci: (ci, k))

    return _grid1_call(body, comm, nb, name="hgrn_fwd",
                       out_shape=(jax.ShapeDtypeStruct((t, D_MODEL), BF16),
                                  jax.ShapeDtypeStruct((t, D_MODEL), BF16),
                                  jax.ShapeDtypeStruct((t // CHUNK, N_HEAD, HEAD_DIM, HEAD_DIM), F32)),
                       in_specs=[col(2), col(3), col(4), col(5),
                                 pl.BlockSpec((2, D_MODEL), lambda ci: (0, 0)),
                                 pl.BlockSpec((1, D_MODEL), lambda ci: (0, 0))],
                       out_specs=(pl.BlockSpec((HG_ROWS, D_MODEL), lambda ci: (ci, 0)),
                                  pl.BlockSpec((HG_ROWS, D_MODEL), lambda ci: (ci, 0)),
                                  pl.BlockSpec((HG_CHUNKS, N_HEAD, HEAD_DIM, HEAD_DIM),
                                               lambda ci: (ci, 0, 0, 0))),
                       scratch=[pltpu.VMEM((N_HEAD, HEAD_DIM, HEAD_DIM), F32)],
                       args=(h, h, h, h, logits, gn))


def _hgrn_chunk_bwd(q, fp, ii, og, o_saved, dy, gnv, lb, st, dsn, causal, anti):
    stb = [s.astype(BF16) for s in st]
    dsnb = [s.astype(BF16) for s in dsn]
    r = _hgrn_chunk(q, fp, ii, lb, stb, causal, with_o=False)
    rinv, on = _rms_heads(_heads(o_saved))
    so = _sig(og)
    sil = og * so
    d_og = dy * on * gnv * (so * (1.0 + og * (1.0 - so)))
    d_on = dy * gnv * sil
    d_ob = jnp.concatenate(
        [ri * (dn - oh * jnp.mean(dn * oh, axis=-1, keepdims=True))
         for ri, dn, oh in zip(rinv, _heads(d_on), _heads(on))], axis=1).astype(BF16)
    d_i, d_qt, d_kt, d_kk, d_st, st_dsn = [], [], [], [], [], []
    ecl = _heads(r["ecl"])
    for hd, (dh, qh, kh, ih, kkh) in enumerate(zip(_heads(d_ob), _heads(r["qtb"]), _heads(r["ktb"]),
                                                   _heads(r["iib"]), _heads(r["kkb"]))):
        d_attn = jnp.where(causal, _dot(dh, ih, NT), 0.0).astype(BF16)
        d_i.append(_dot(r["attn"][hd], dh, TN) + _dot(kkh, dsnb[hd], NT))
        d_qt.append(_dot(d_attn, kh) + _dot(dh, stb[hd]))
        d_kt.append(_dot(d_attn, qh, TN))
        d_kk.append(_dot(ih, dsnb[hd]))
        d_st.append(_dot(dh, qh, TN) + dsn[hd] * ecl[hd])
        st_dsn.append(jnp.sum(st[hd] * dsn[hd], axis=0, keepdims=True))
    d_qt = jnp.concatenate(d_qt, axis=1)
    d_kt = jnp.concatenate(d_kt, axis=1)
    d_kk = jnp.concatenate(d_kk, axis=1)
    kk = r["kk"]
    d_cl = r["ecl"] * jnp.concatenate(st_dsn, axis=1) + jnp.sum(kk * d_kk, axis=0, keepdims=True)
    d_k = (d_kk * r["ecl"] + d_kt) * r["en"]
    d_c = d_qt * r["qtb"].astype(F32) - d_kt * r["ktb"].astype(F32) - d_kk * kk
    rowid = lax.broadcasted_iota(jnp.int32, (CHUNK, D_MODEL), 0)
    d_c = d_c + jnp.where(rowid == CHUNK - 1, d_cl, 0.0)
    d_lf = _tri_cumsum(anti.astype(BF16), d_c)
    d_f = d_lf / r["f"] - d_k
    sg, sq = r["sg"], r["sq"]
    d_q = d_qt * r["ec"] * (sq * (1.0 + q * (1.0 - sq)))
    d_fp = d_f * (1.0 - lb) * sg * (1.0 - sg)
    return (d_q, d_fp, jnp.concatenate(d_i, axis=1), d_og, d_st,
            _colsum8(dy * on * sil), _colsum8(d_f * (1.0 - sg)))


def _hgrn_bwd(h, o_all, dyb, st_all, logits, gn, dh_buf, comm=None):
    t = h.shape[0]
    nb = t // HG_ROWS

    def body(q_ref, f_ref, i_ref, og_ref, o_ref, dyb_ref, st_ref, lg_ref, gn_ref, dh_buf_ref,
             dh_ref, dlb_ref, dgn_ref, dstate):
        @pl.when(pl.program_id(0) == 0)
        def _():
            dstate[...] = jnp.zeros_like(dstate)
            dlb_ref[...] = jnp.zeros_like(dlb_ref)
            dgn_ref[...] = jnp.zeros_like(dgn_ref)

        causal, anti = _tri_masks()
        lb = _sig(lg_ref[0:1, :] - lg_ref[1:2, :])
        gnv = gn_ref[...]
        dsn = [dstate[hd] for hd in range(N_HEAD)]
        dgn_acc = jnp.zeros((8, D_MODEL), F32)
        dlb_acc = jnp.zeros((8, D_MODEL), F32)
        for cc in reversed(range(HG_CHUNKS)):
            rows = slice(cc * CHUNK, (cc + 1) * CHUNK)
            d_q, d_fp, d_i, d_og, dsn, dgn_c, dlb_c = _hgrn_chunk_bwd(
                q_ref[rows, :].astype(F32), f_ref[rows, :].astype(F32), i_ref[rows, :].astype(F32),
                og_ref[rows, :].astype(F32), o_ref[rows, :].astype(F32), dyb_ref[rows, :].astype(F32), gnv, lb,
                [st_ref[cc, hd] for hd in range(N_HEAD)], dsn, causal, anti)
            dgn_acc = dgn_acc + dgn_c
            dlb_acc = dlb_acc + dlb_c
            dh_ref[0, rows, :D_MODEL] = d_q.astype(BF16)
            dh_ref[0, rows, D_MODEL:] = d_fp.astype(BF16)
            dh_ref[1, rows, :D_MODEL] = d_i.astype(BF16)
            dh_ref[1, rows, D_MODEL:] = d_og.astype(BF16)
        dgn_ref[...] += dgn_acc
        dlb_ref[...] += dlb_acc
        for hd in range(N_HEAD):
            dstate[hd] = dsn[hd]

    def col(k):
        return pl.BlockSpec((HG_ROWS, D_MODEL), lambda ci: (nb - 1 - ci, k))

    acc8 = pl.BlockSpec((8, D_MODEL), lambda ci: (0, 0))
    pair = pl.BlockSpec((2, HG_ROWS, 2 * D_MODEL), lambda ci: (0, nb - 1 - ci, 0))
    return _grid1_call(body, comm, nb, name="hgrn_bwd",
                       out_shape=(jax.ShapeDtypeStruct(dh_buf.shape, BF16),
                                  jax.ShapeDtypeStruct((8, D_MODEL), F32),
                                  jax.ShapeDtypeStruct((8, D_MODEL), F32)),
                       in_specs=[col(2), col(3), col(4), col(5), col(0),
                                 pl.BlockSpec((HG_ROWS, D_MODEL), lambda ci: (nb - 1 - ci, 0)),
                                 pl.BlockSpec((HG_CHUNKS, N_HEAD, HEAD_DIM, HEAD_DIM),
                                              lambda ci: (nb - 1 - ci, 0, 0, 0)),
                                 pl.BlockSpec((2, D_MODEL), lambda ci: (0, 0)),
                                 pl.BlockSpec((1, D_MODEL), lambda ci: (0, 0)), ANY],
                       out_specs=(pair, acc8, acc8),
                       scratch=[pltpu.VMEM((N_HEAD, HEAD_DIM, HEAD_DIM), F32)],
                       args=(h, h, h, h, o_all, dyb, st_all, logits, gn, dh_buf), aliases={9: 0})


def _mix_fwd(ya, yb, h, x, wb0, wb1, wo, g1, b1, tm, comm=None):
    t = x.shape[0]

    def body(ya_ref, yb_ref, ga_ref, gb_ref, x_ref, wb0_ref, wb1_ref, wo_ref, g1_ref, b1_ref,
             r1_ref, a_ref, b_ref, m_ref, x1_ref):
        a = _dot(ya_ref[...], wb0_ref[...])
        b = _dot(yb_ref[...], wb1_ref[...])
        m = _sig(ga_ref[...].astype(F32)) * a + _sig(gb_ref[...].astype(F32)) * b
        r1 = ALPHA * x_ref[...] + _dot(m, wo_ref[...])
        xh, _ = _ln_stats(r1)
        r1_ref[...] = r1
        a_ref[...] = a.astype(BF16)
        b_ref[...] = b.astype(BF16)
        m_ref[...] = m.astype(BF16)
        x1_ref[...] = (xh * g1_ref[...] + b1_ref[...]).astype(BF16)

    tile = pl.BlockSpec((tm, D_MODEL), lambda i: (i, 0))
    wsp = pl.BlockSpec((D_MODEL, D_MODEL), lambda i: (0, 0))
    vec = pl.BlockSpec((1, D_MODEL), lambda i: (0, 0))
    f32o = jax.ShapeDtypeStruct((t, D_MODEL), F32)
    bfo = jax.ShapeDtypeStruct((t, D_MODEL), BF16)
    return _grid1_call(body, comm, t // tm, name="mix_fwd", out_shape=(f32o, bfo, bfo, bfo, bfo),
                       in_specs=[tile, tile,
                                 pl.BlockSpec((tm, D_MODEL), lambda i: (i, 6)),
                                 pl.BlockSpec((tm, D_MODEL), lambda i: (i, 7)),
                                 tile, wsp, wsp, wsp, vec, vec],
                       out_specs=(tile, tile, tile, tile, tile),
                       scratch=[], args=(ya, yb, h, h, x, wb0, wb1, wo, g1, b1))


def _mix_bwd(dr1, h, a, b, wo, wb0, wb1, tm):
    t = dr1.shape[0]

    def body(dr1_ref, ga_ref, gb_ref, a_ref, b_ref, wo_ref, wb0_ref, wb1_ref,
             da_ref, db_ref, dh3_ref, dya_ref, dyb_ref):
        d_m = _dot(dr1_ref[...], wo_ref[...], NT)
        sa = _sig(ga_ref[...].astype(F32))
        sb = _sig(gb_ref[...].astype(F32))
        d_a = (d_m * sa).astype(BF16)
        d_b = (d_m * sb).astype(BF16)
        da_ref[...] = d_a
        db_ref[...] = d_b
        dh3_ref[:, :D_MODEL] = (d_m * a_ref[...].astype(F32) * sa * (1.0 - sa)).astype(BF16)
        dh3_ref[:, D_MODEL:] = (d_m * b_ref[...].astype(F32) * sb * (1.0 - sb)).astype(BF16)
        dya_ref[...] = _dot(d_a, wb0_ref[...], NT).astype(BF16)
        dyb_ref[...] = _dot(d_b, wb1_ref[...], NT).astype(BF16)

    tile = pl.BlockSpec((tm, D_MODEL), lambda i: (i, 0))
    wsp = pl.BlockSpec((D_MODEL, D_MODEL), lambda i: (0, 0))
    f32o = jax.ShapeDtypeStruct((t, D_MODEL), F32)
    bfo = jax.ShapeDtypeStruct((t, D_MODEL), BF16)
    return _pc(body, name="mix_bwd",
               out_shape=(bfo, bfo, jax.ShapeDtypeStruct((N_CHIP, t, 2 * D_MODEL), BF16), bfo, bfo),
               grid=(t // tm,),
               in_specs=[tile,
                         pl.BlockSpec((tm, D_MODEL), lambda i: (i, 6)),
                         pl.BlockSpec((tm, D_MODEL), lambda i: (i, 7)),
                         tile, tile, wsp, wsp, wsp],
               out_specs=(tile, tile, pl.BlockSpec((None, tm, 2 * D_MODEL), lambda i: (DH_SLOT[3], i, 0)),
                          tile, tile),
               sem=("parallel",))(dr1, h, h, a, b, wo, wb0, wb1)


FF_TILE = 1408
FF_NJ = D_FF // FF_TILE


def _shift_down(v, k):
    return pltpu.roll(v, k, 0)


def _shift_up(v, k):
    return pltpu.roll(v, v.shape[0] - k, 0)


HALO = 16
FF_PIECES = ((0, 768), (768, FF_TILE))


def _ffn_up_act(x1b, wup_st, convw, convb, tm, comm):
    t = x1b.shape[0]
    ni = t // tm
    nth = tm // HALO

    def body(x_ref, xp_ref, wg_ref, wv_ref, cw_ref, cb_ref, h2_ref, act_ref):
        wg = wg_ref[...]
        gate = _dot(x_ref[...], wg).astype(BF16)
        val = _dot(x_ref[...], wv_ref[...]).astype(BF16)
        prev = (_dot(xp_ref[...], wg) * (pl.program_id(0) > 0).astype(F32)).astype(BF16)
        h2_ref[0] = gate
        h2_ref[1] = val
        ext = jnp.concatenate([prev.astype(F32), gate.astype(F32)], axis=0)
        gc = (cw_ref[0:1, :] * _shift_down(ext, 2) + cw_ref[1:2, :] * _shift_down(ext, 1)
              + cw_ref[2:3, :] * ext + cb_ref[...])[HALO:, :].astype(BF16)
        h2_ref[2] = gc
        act_ref[...] = (_gelu(gc.astype(F32)) * val.astype(F32)).astype(BF16)

    res = _hosted_call(
        body, comm,
        lambda: (pl.program_id(0) == 0) & (pl.program_id(1) == 0),
        lambda: (pl.program_id(0) == ni - 1) & (pl.program_id(1) == FF_NJ - 1),
        mid=lambda: (pl.program_id(0) == max(ni - 2, 0)) & (pl.program_id(1) == 0),
        name="ffn_up",
        out_shape=(jax.ShapeDtypeStruct((3, t, D_FF), BF16), jax.ShapeDtypeStruct((t, D_FF), BF16)),
        grid=(ni, FF_NJ),
        in_specs=[pl.BlockSpec((tm, D_MODEL), lambda i, j: (i, 0)),
                  pl.BlockSpec((HALO, D_MODEL), lambda i, j: (jnp.maximum(i * nth - 1, 0), 0)),
                  pl.BlockSpec((None, D_MODEL, FF_TILE), lambda i, j: (j, 0, 0)),
                  pl.BlockSpec((None, D_MODEL, FF_TILE), lambda i, j: (j + FF_NJ, 0, 0)),
                  pl.BlockSpec((3, FF_TILE), lambda i, j: (0, j)),
                  pl.BlockSpec((1, FF_TILE), lambda i, j: (0, j))],
        out_specs=(pl.BlockSpec((3, tm, FF_TILE), lambda i, j: (0, i, j)),
                   pl.BlockSpec((tm, FF_TILE), lambda i, j: (i, j))),
        scratch=[], sem=("arbitrary", "arbitrary"),
        args=(x1b, x1b, wup_st, wup_st, convw, convb))
    return res[0], res[1], res[2:]


def _out_fwd_bwd(act, x1b, r1, p2, tgt, wd, wpg, wpp, g1, b1, g2, b2, tm):
    t = r1.shape[0]

    def body(act_ref, x1b_ref, r1_ref, p_ref, tgt_ref, wd_ref, wpg_ref, wpp_ref,
             g1_ref, b1_ref, g2_ref, b2_ref,
             dr2_ref, dpg_ref, dpp_ref, loss_ref, dg2_ref, db2_ref):
        i = pl.program_id(0)

        @pl.when(i == 0)
        def _():
            loss_ref[...] = jnp.zeros_like(loss_ref)
            dg2_ref[...] = jnp.zeros_like(dg2_ref)
            db2_ref[...] = jnp.zeros_like(db2_ref)

        ffn = _dot(act_ref[...], wd_ref[...])
        pg = _dot(x1b_ref[...], wpg_ref[...])
        pp = _dot(p_ref[...], wpp_ref[...])
        s = _sig(pg)
        xh1, _ = _ln_stats(r1_ref[...])
        x1 = xh1 * g1_ref[...] + b1_ref[...]
        r2 = ALPHA * x1 + ffn + s * pp
        xh2, rstd2 = _ln_stats(r2)
        g2v = g2_ref[...]
        diff = xh2 * g2v + b2_ref[...] - tgt_ref[...]
        part = jnp.sum(jnp.sum(diff * diff, axis=1, keepdims=True), axis=0, keepdims=True)
        loss_ref[...] += jnp.broadcast_to(part * (0.5 / D_MODEL), loss_ref.shape)
        dy = diff * (1.0 / D_MODEL)
        dg2_ref[...] += _colsum8(dy * xh2)
        db2_ref[...] += _colsum8(dy)
        dr2 = _ln_bwd(dy * g2v, xh2, rstd2)
        dr2_ref[...] = dr2.astype(BF16)
        dpg_ref[...] = (dr2 * pp * s * (1.0 - s)).astype(BF16)
        dpp_ref[...] = (dr2 * s).astype(BF16)

    tile = pl.BlockSpec((tm, D_MODEL), lambda i: (i, 0))
    vec = pl.BlockSpec((1, D_MODEL), lambda i: (0, 0))
    acc8 = pl.BlockSpec((8, D_MODEL), lambda i: (0, 0))
    acc_shape = jax.ShapeDtypeStruct((8, D_MODEL), F32)
    return _pc(body, name="out_fwd_bwd",
               out_shape=(jax.ShapeDtypeStruct((t, D_MODEL), BF16),
                          jax.ShapeDtypeStruct((t, D_MODEL), BF16),
                          jax.ShapeDtypeStruct((t, D_MODEL), BF16),
                          acc_shape, acc_shape, acc_shape),
               grid=(t // tm,),
               in_specs=[pl.BlockSpec((tm, D_FF), lambda i: (i, 0)), tile, tile,
                         pl.BlockSpec((tm, PLE_DIM), lambda i: (i, 0)), tile,
                         pl.BlockSpec((D_FF, D_MODEL), lambda i: (0, 0)),
                         pl.BlockSpec((D_MODEL, D_MODEL), lambda i: (0, 0)),
                         pl.BlockSpec((PLE_DIM, D_MODEL), lambda i: (0, 0)),
                         vec, vec, vec, vec],
               out_specs=(tile, tile, tile, acc8, acc8, acc8),
               sem=("arbitrary",))(act, x1b, r1, p2, tgt, wd, wpg, wpp, g1, b1, g2, b2)


def _ffn_bwd(h2, dr2, wd, wup_st, dpg, wpg, r1, g1, convw, tm):
    t = r1.shape[0]
    ni = t // tm
    nth = tm // HALO
    last_halo = t // HALO - 1
    main_rows = slice(0, tm)

    def body(g_ref, gc_ref, gcn_ref, v_ref, vn_ref, dr2_ref, dr2n_ref, wd_ref, wug_ref, wuv_ref,
             cw_ref, dpg_ref, wpg_ref, r1_ref, g1_ref,
             dh2_ref, dr1_ref, dcw_ref, dcb_ref, dg1_ref, db1_ref, acc):
        i = pl.program_id(0)
        j = pl.program_id(1)

        @pl.when((i == 0) & (j == 0))
        def _():
            dcw_ref[...] = jnp.zeros_like(dcw_ref)
            dcb_ref[...] = jnp.zeros_like(dcb_ref)
            dg1_ref[...] = jnp.zeros_like(dg1_ref)
            db1_ref[...] = jnp.zeros_like(db1_ref)

        dr2v = dr2_ref[...].astype(BF16)
        dr2n = dr2n_ref[...].astype(BF16)
        more = (i < ni - 1).astype(F32)
        prod = None
        dcw_parts, dcb_parts = [], []
        for c0, c1 in FF_PIECES:
            pc = slice(c0, c1)
            da = _dot(dr2v, wd_ref[pc, :], NT)
            dnext = _dot(dr2n, wd_ref[pc, :], NT) * more
            gc = jnp.concatenate([gc_ref[:, pc].astype(F32), gcn_ref[:, pc].astype(F32)], axis=0)
            vext = jnp.concatenate([v_ref[:, pc].astype(F32), vn_ref[:, pc].astype(F32)], axis=0)
            dext = jnp.concatenate([da, dnext], axis=0)
            gl, dgl = _gelu_and_grad(gc)
            d_gc = dext * vext * dgl
            up1 = _shift_up(d_gc, 1)[main_rows, :]
            up2 = _shift_up(d_gc, 2)[main_rows, :]
            dm = d_gc[main_rows, :]
            d_gate = (cw_ref[2:3, pc] * dm + cw_ref[1:2, pc] * up1 + cw_ref[0:1, pc] * up2).astype(BF16)
            d_val = (da * gl[main_rows, :]).astype(BF16)
            dh2_ref[0, :, pc] = d_gate
            dh2_ref[1, :, pc] = d_val
            g = g_ref[:, pc].astype(F32)
            s0 = jnp.sum(g * up2, axis=0, keepdims=True)
            s1 = jnp.sum(g * up1, axis=0, keepdims=True)
            s2 = jnp.sum(g * dm, axis=0, keepdims=True)
            rowid = lax.broadcasted_iota(jnp.int32, (8, c1 - c0), 0)
            dcw_parts.append(jnp.where(rowid == 0, s0, jnp.where(rowid == 1, s1,
                                                                 jnp.where(rowid == 2, s2, 0.0))))
            dcb_parts.append(_colsum8(dm))
            part = _dot(d_gate, wug_ref[:, pc], NT) + _dot(d_val, wuv_ref[:, pc], NT)
            prod = part if prod is None else prod + part
        dcw_part = jnp.concatenate(dcw_parts, axis=1)
        dcb_part = jnp.concatenate(dcb_parts, axis=1)
        for jj in range(FF_NJ):
            @pl.when(j == jj)
            def _(jj=jj):
                cols = slice(jj * FF_TILE, (jj + 1) * FF_TILE)
                dcw_ref[:, cols] += dcw_part
                dcb_ref[:, cols] += dcb_part

        @pl.when(j == 0)
        def _():
            acc[...] = prod

        @pl.when(j > 0)
        def _():
            acc[...] += prod

        @pl.when(j == FF_NJ - 1)
        def _():
            d_x1 = acc[...] + _dot(dpg_ref[...], wpg_ref[...], NT) + ALPHA * dr2_ref[...].astype(F32)
            xh, rstd = _ln_stats(r1_ref[...])
            dg1_ref[...] += _colsum8(d_x1 * xh)
            db1_ref[...] += _colsum8(d_x1)
            dr1_ref[...] = _ln_bwd(d_x1 * g1_ref[...], xh, rstd).astype(BF16)

    def h2_main(part):
        return pl.BlockSpec((None, tm, FF_TILE), lambda i, j: (part, i, j))

    def h2_next(part):
        return pl.BlockSpec((None, HALO, FF_TILE),
                            lambda i, j: (part, jnp.minimum((i + 1) * nth, last_halo), j))

    tile = pl.BlockSpec((tm, D_MODEL), lambda i, j: (i, 0))
    acc8 = pl.BlockSpec((8, D_MODEL), lambda i, j: (0, 0))
    accff = pl.BlockSpec((8, D_FF), lambda i, j: (0, 0))
    acc_shape = jax.ShapeDtypeStruct((8, D_MODEL), F32)
    accff_shape = jax.ShapeDtypeStruct((8, D_FF), F32)
    return _pc(body, name="ffn_bwd",
               out_shape=(jax.ShapeDtypeStruct((2, t, D_FF), BF16),
                          jax.ShapeDtypeStruct((t, D_MODEL), BF16),
                          accff_shape, accff_shape, acc_shape, acc_shape),
               grid=(ni, FF_NJ),
               in_specs=[h2_main(0), h2_main(2), h2_next(2), h2_main(1), h2_next(1),
                         tile,
                         pl.BlockSpec((HALO, D_MODEL), lambda i, j: (jnp.minimum((i + 1) * nth, last_halo), 0)),
                         pl.BlockSpec((FF_TILE, D_MODEL), lambda i, j: (j, 0)),
                         pl.BlockSpec((None, D_MODEL, FF_TILE), lambda i, j: (j, 0, 0)),
                         pl.BlockSpec((None, D_MODEL, FF_TILE), lambda i, j: (j + FF_NJ, 0, 0)),
                         pl.BlockSpec((3, FF_TILE), lambda i, j: (0, j)),
                         tile, pl.BlockSpec((D_MODEL, D_MODEL), lambda i, j: (0, 0)),
                         tile, pl.BlockSpec((1, D_MODEL), lambda i, j: (0, 0))],
               out_specs=(pl.BlockSpec((2, tm, FF_TILE), lambda i, j: (0, i, j)),
                          tile, accff, accff, acc8, acc8),
               scratch=[pltpu.VMEM((tm, D_MODEL), F32)],
               sem=("arbitrary", "arbitrary"))(h2, h2, h2, h2, h2, dr2, dr2, wd, wup_st, wup_st,
                                               convw, dpg, wpg, r1, g1)


ANY = pl.BlockSpec(memory_space=pl.ANY)


def _chip_peers():
    x, y, c = lax.axis_index("x"), lax.axis_index("y"), lax.axis_index("c")
    return x, y, c, [(1 - x, y), (x, 1 - y), (1 - x, 1 - y)]


def _gather_comm(halved, whole=(), blocks=None):
    n, nw = len(halved), len(whole)
    blocks = blocks or {}

    def at(ti, ref, chip, *rest):
        return ref.at[(chip, blocks[ti][1]) + rest] if ti in blocks else ref.at[(chip,) + rest]

    def copies(ins, outs, sems):
        ici_send, ici_recv, d2d_send, d2d_recv, own_send, own_recv = sems
        x, y, c, peers = _chip_peers()
        me = 2 * x + y
        sibling = (x, y, 1 - c)
        own, ici, ici_wait, fwd, fwd_wait = [], [], [], [], []
        for ti in range(n + nw):
            src, dst = ins[ti], outs[ti]
            own.append(pltpu.make_async_remote_copy(
                src_ref=src, dst_ref=at(ti, dst, me), send_sem=own_send.at[ti], recv_sem=own_recv.at[ti],
                device_id=sibling, device_id_type=MESH))
            for k, (px, py) in enumerate(peers):
                pk = 2 * px + py
                sem = dict(send_sem=ici_send.at[ti * 3 + k], recv_sem=ici_recv.at[ti * 3 + k],
                           device_id=(px, py, c), device_id_type=MESH)
                if ti < n:
                    ici.append(pltpu.make_async_remote_copy(src_ref=src.at[c], dst_ref=at(ti, dst, me, c), **sem))
                    ici_wait.append(pltpu.make_async_remote_copy(src_ref=src.at[c], dst_ref=at(ti, dst, pk, c),
                                                                 **sem))
                    dsem = dict(send_sem=d2d_send.at[ti * 3 + k], recv_sem=d2d_recv.at[ti * 3 + k],
                                device_id=sibling, device_id_type=MESH)
                    fwd.append(pltpu.make_async_remote_copy(src_ref=at(ti, dst, pk, c), dst_ref=at(ti, dst, pk, c),
                                                            **dsem))
                    fwd_wait.append(pltpu.make_async_remote_copy(
                        src_ref=at(ti, dst, pk, 1 - c), dst_ref=at(ti, dst, pk, 1 - c), **dsem))
                else:
                    ici.append(pltpu.make_async_remote_copy(src_ref=src, dst_ref=dst.at[me], **sem))
                    ici_wait.append(pltpu.make_async_remote_copy(src_ref=src, dst_ref=dst.at[pk], **sem))
        return own, ici, ici_wait, fwd, fwd_wait

    def start(ins, outs, sems):
        own, ici, _, _, _ = copies(ins, outs, sems)
        for cp in own + ici:
            cp.start()

    def finish(ins, outs, sems):
        own, ici, ici_wait, fwd, fwd_wait = copies(ins, outs, sems)
        for i, cp in enumerate(ici_wait):
            cp.wait_recv()
            if i < len(fwd):
                fwd[i].start()
        for cp in fwd_wait + own:
            cp.wait_recv()
        for cp in own + ici + fwd:
            cp.wait_send()

    def middle(ins, outs, sems):
        _, _, ici_wait, fwd, _ = copies(ins, outs, sems)
        for i, cp in enumerate(ici_wait):
            cp.wait_recv()
            if i < len(fwd):
                fwd[i].start()

    def finish_late(ins, outs, sems):
        own, ici, _, fwd, fwd_wait = copies(ins, outs, sems)
        for cp in fwd_wait + own:
            cp.wait_recv()
        for cp in own + ici + fwd:
            cp.wait_send()

    srcs = list(halved) + list(whole)
    shapes = [jax.ShapeDtypeStruct((N_CHIP,) + ((blocks[ti][0],) if ti in blocks else ()) + s.shape, s.dtype)
              for ti, s in enumerate(srcs)]
    buffers = [(ti, blk[2]) for ti, blk in sorted(blocks.items()) if blk[2] is not None]
    aliases = {len(srcs) + bi: ti for bi, (ti, _) in enumerate(buffers)}
    return _Comm(srcs + [buf for _, buf in buffers], shapes,
                 [pltpu.SemaphoreType.DMA((3 * (n + nw),)), pltpu.SemaphoreType.DMA((3 * (n + nw),)),
                  pltpu.SemaphoreType.DMA((max(3 * n, 1),)), pltpu.SemaphoreType.DMA((max(3 * n, 1),)),
                  pltpu.SemaphoreType.DMA((n + nw,)), pltpu.SemaphoreType.DMA((n + nw,))],
                 start, finish, middle, finish_late, aliases)


def _sibling_exchange_comm(grads):
    n = len(grads)

    def copies(ins, outs, sems):
        send_sems, recv_sems = sems
        x, y, c = lax.axis_index("x"), lax.axis_index("y"), lax.axis_index("c")
        res = []
        for ti in range(n):
            half = ins[ti].shape[1] // 2
            res.append(pltpu.make_async_remote_copy(
                src_ref=ins[ti].at[:, pl.ds(pl.multiple_of((1 - c) * half, 16), half), :],
                dst_ref=outs[ti],
                send_sem=send_sems.at[ti], recv_sem=recv_sems.at[ti],
                device_id=(x, y, 1 - c), device_id_type=MESH))
        return res

    def start(ins, outs, sems):
        for cp in copies(ins, outs, sems):
            cp.start()

    def finish(ins, outs, sems):
        for cp in copies(ins, outs, sems):
            cp.wait()

    return _Comm(grads, [jax.ShapeDtypeStruct((N_CHIP, g.shape[1] // 2, g.shape[2]), g.dtype) for g in grads],
                 [pltpu.SemaphoreType.DMA((n,)), pltpu.SemaphoreType.DMA((n,))], start, finish)


def _in_proj_gathering(x2b, own, chip, tm, comm):
    t = x2b.shape[0]
    ni = t // tm
    half, cols = own.shape[1], own.shape[2]
    nci, nco = len(comm.ins), len(comm.out_shapes)

    def body(chip_ref, x_ref, own_ref, own_hbm, *rest):
        c_in = rest[:nci]
        h_ref, win_out = rest[nci:nci + 2]
        c_out = rest[nci + 2:nci + 2 + nco]
        w_scr, ici_send, ici_recv, d2d_send, d2d_recv, own_sems, ld_sems = rest[nci + 2 + nco:nci + 9 + nco]
        c_sem = rest[nci + 9 + nco:]
        s, i = pl.program_id(0), pl.program_id(1)
        x, y, c, peers = _chip_peers()
        me = 2 * x + y
        sibling = (x, y, 1 - c)

        def ici(k, slot):
            px, py = peers[k]
            return pltpu.make_async_remote_copy(
                src_ref=own_hbm.at[c], dst_ref=win_out.at[slot, c],
                send_sem=ici_send.at[k], recv_sem=ici_recv.at[k],
                device_id=(px, py, c), device_id_type=MESH)

        def forward(k, core):
            pk = 2 * peers[k][0] + peers[k][1]
            return pltpu.make_async_remote_copy(
                src_ref=win_out.at[pk, core], dst_ref=win_out.at[pk, core],
                send_sem=d2d_send.at[k], recv_sem=d2d_recv.at[k],
                device_id=sibling, device_id_type=MESH)

        place_own = pltpu.make_async_remote_copy(
            src_ref=own_hbm, dst_ref=win_out.at[me], send_sem=own_sems.at[0], recv_sem=own_sems.at[1],
            device_id=sibling, device_id_type=MESH)

        @pl.when((s == 0) & (i == 0))
        def _():
            for k in range(2):
                ici(k, me).start()
            place_own.start()

        @pl.when(s == 0)
        def _():
            xv = x_ref[...]
            h_ref[...] = (_dot(xv[:, :half], own_ref[0]) + _dot(xv[:, half:], own_ref[1])).astype(BF16)

        for k in range(3):
            @pl.when((s == k + 1) & (i == 0))
            def _(k=k):
                pk = 2 * peers[k][0] + peers[k][1]
                ici(k, pk).wait_recv()
                if k == 0:
                    ici(2, me).start()
                forward(k, c).start()
                forward(k, 1 - c).wait_recv()
                loads = [pltpu.make_async_copy(win_out.at[pk, hh], w_scr.at[hh], ld_sems.at[hh])
                         for hh in range(2)]
                for ld in loads:
                    ld.start()
                for ld in loads:
                    ld.wait()
                if k == 1:
                    comm.start(c_in, c_out, c_sem)

        @pl.when(s > 0)
        def _():
            xv = x_ref[...]
            h_ref[...] = (_dot(xv[:, :half], w_scr[0]) + _dot(xv[:, half:], w_scr[1])).astype(BF16)

        @pl.when((s == N_CHIP - 1) & (i == ni - 1))
        def _():
            place_own.wait()
            for k in range(3):
                ici(k, me).wait_send()
                forward(k, c).wait_send()
            comm.finish(c_in, c_out, c_sem)

    def shard_col(s, me):
        return jnp.where(s == 0, me, me ^ jnp.where(s == 1, 2, jnp.where(s == 2, 1, 3)))

    res = _pc(body, name="in_proj",
              out_shape=(jax.ShapeDtypeStruct((t, N_CHIP * cols), BF16),
                         jax.ShapeDtypeStruct((N_CHIP,) + own.shape, own.dtype)) + tuple(comm.out_shapes),
              grid=(N_CHIP, ni), nsp=1,
              in_specs=[pl.BlockSpec((tm, 2 * half), lambda s, i, chip_ref: (i, 0)),
                        pl.BlockSpec(own.shape, lambda s, i, chip_ref: (0, 0, 0)),
                        ANY] + [ANY] * nci,
              out_specs=(pl.BlockSpec((tm, cols), lambda s, i, chip_ref: (i, shard_col(s, chip_ref[0]))),
                         ANY) + tuple([ANY] * nco),
              scratch=[pltpu.VMEM(own.shape, own.dtype),
                       pltpu.SemaphoreType.DMA((3,)), pltpu.SemaphoreType.DMA((3,)),
                       pltpu.SemaphoreType.DMA((3,)), pltpu.SemaphoreType.DMA((3,)),
                       pltpu.SemaphoreType.DMA((2,)), pltpu.SemaphoreType.DMA((2,))] + comm.sems,
              sem=("arbitrary", "arbitrary"))(chip, x2b, own, own, *comm.ins)
    return res[0], res[1], res[2:]


def _rs_add_halves(name, grad, recv, core):
    _, r, cdim = grad.shape
    half = r // 2
    tr = _row_tile(half, cdim, mult=16)
    nr = half // tr

    def body(c_ref, g_ref, r_ref, o_ref):
        o_ref[...] = (g_ref[...].astype(F32) + r_ref[...].astype(F32)).astype(BF16)

    return _pc(body, name=name, out_shape=jax.ShapeDtypeStruct((N_CHIP, half, cdim), BF16),
               grid=(N_CHIP, nr), nsp=1,
               in_specs=[pl.BlockSpec((None, tr, cdim), lambda j, i, c_ref: (j, c_ref[0] * nr + i, 0)),
                         pl.BlockSpec((None, tr, cdim), lambda j, i, c_ref: (j, i, 0))],
               out_specs=pl.BlockSpec((None, tr, cdim), lambda j, i, c_ref: (j, i, 0)),
               sem=("parallel", "parallel"))(core, grad, recv)


def _chip_exchange_comm(parts):
    n = len(parts)

    def copies(ins, outs, sems):
        send_sems, recv_sems = sems
        x, y, c, peers = _chip_peers()
        return [pltpu.make_async_remote_copy(
            src_ref=ins[ti].at[2 * px + py], dst_ref=outs[ti].at[k],
            send_sem=send_sems.at[ti * 3 + k], recv_sem=recv_sems.at[ti * 3 + k],
            device_id=(px, py, c), device_id_type=MESH)
            for ti in range(n) for k, (px, py) in enumerate(peers)]

    def start(ins, outs, sems):
        for cp in copies(ins, outs, sems):
            cp.start()

    def finish(ins, outs, sems):
        for cp in copies(ins, outs, sems):
            cp.wait()

    return _Comm(parts, [jax.ShapeDtypeStruct((3,) + p.shape[1:], p.dtype) for p in parts],
                 [pltpu.SemaphoreType.DMA((3 * n,)), pltpu.SemaphoreType.DMA((3 * n,))], start, finish)


def _rs_sum_chips(name, part, recv, chip):
    _, half, cdim = recv.shape
    tr = _row_tile(half, cdim, mult=16)

    def body(chip_ref, p_ref, r_ref, o_ref):
        o_ref[...] = ((p_ref[...].astype(F32) + r_ref[0].astype(F32)) + r_ref[1].astype(F32)
                      ) + r_ref[2].astype(F32)

    return _pc(body, name=name, out_shape=jax.ShapeDtypeStruct((half, cdim), F32),
               grid=(half // tr,), nsp=1,
               in_specs=[pl.BlockSpec((None, tr, cdim), lambda i, chip_ref: (chip_ref[0], i, 0)),
                         pl.BlockSpec((3, tr, cdim), lambda i, chip_ref: (0, i, 0))],
               out_specs=pl.BlockSpec((tr, cdim), lambda i, chip_ref: (i, 0)),
               sem=("parallel",))(chip, part, recv)


def _rs_send_halves(halves):
    n = len(halves)

    def body(*refs):
        ins, outs = refs[:n], refs[n:2 * n]
        send_sems, recv_sems = refs[2 * n:]
        x, y, c = lax.axis_index("x"), lax.axis_index("y"), lax.axis_index("c")
        sends = []
        for ti in range(n):
            cp = pltpu.make_async_remote_copy(
                src_ref=ins[ti], dst_ref=outs[ti],
                send_sem=send_sems.at[ti], recv_sem=recv_sems.at[ti],
                device_id=(x, y, 1 - c), device_id_type=MESH)
            cp.start()
            sends.append(cp)
        for cp in sends:
            cp.wait()

    return _pc(body, name="rs_send_halves",
               out_shape=tuple(jax.ShapeDtypeStruct(hv.shape, hv.dtype) for hv in halves),
               in_specs=[ANY] * n, out_specs=tuple([ANY] * n),
               scratch=[pltpu.SemaphoreType.DMA((n,)), pltpu.SemaphoreType.DMA((n,))])(*halves)


def _adamw_rows(name, mine, theirs, w, m, v, core):
    half, cdim = mine.shape
    tr = _row_tile(half, cdim, budget=1 << 19)
    nrh = half // tr

    def body(c_ref, mine_ref, theirs_ref, w_ref, m_ref, v_ref, g_ref, d_ref, m2_ref, v2_ref):
        is_mine = (pl.program_id(0) // nrh) == c_ref[0]
        g = jnp.where(is_mine, mine_ref[...], theirs_ref[...])
        d, m2, v2 = _adamw(w_ref[...], g, m_ref[...], v_ref[...])
        g_ref[...] = g
        d_ref[...] = d
        m2_ref[...] = m2
        v2_ref[...] = v2

    htile = pl.BlockSpec((tr, cdim), lambda i, c_ref: (i % nrh, 0))
    tile = pl.BlockSpec((tr, cdim), lambda i, c_ref: (i, 0))
    shp = jax.ShapeDtypeStruct((2 * half, cdim), F32)
    return _pc(body, name=name, out_shape=(shp, shp, shp, shp), grid=(2 * nrh,), nsp=1,
               in_specs=[htile, htile, tile, tile, tile], out_specs=(tile, tile, tile, tile),
               sem=("parallel",))(core, mine, theirs, w, m, v)


def _adamw_whole(name, g, w, m, v):
    def body(g_ref, w_ref, m_ref, v_ref, d_ref, m2_ref, v2_ref):
        d, m2, v2 = _adamw(w_ref[...], g_ref[...], m_ref[...], v_ref[...])
        d_ref[...] = d
        m2_ref[...] = m2
        v2_ref[...] = v2

    shp = jax.ShapeDtypeStruct(g.shape, F32)
    return _pc(body, name=name, out_shape=(shp, shp, shp))(g, w, m, v)


SMALL_LAYOUT = (
    ("sgu_w_s", 1024, 1, 0),
    ("sgu_b_s", 8, 1, 1024),
    ("sgu_norm_g", 1, 0, 0),
    ("sgu_norm_b", 1, 0, 1),
    ("hgrn_norm_g", 1, 0, 3),
    ("ln1_g", 1, 0, 4),
    ("ln1_b", 1, 0, 5),
    ("ffn_conv_b", 1, 2, 3),
    ("ln2_g", 1, 0, 6),
    ("ln2_b", 1, 0, 7),
)
LB_ROW = 2
LOSS_ROW = 8
PACK_SHAPES = ((16, D_MODEL), (N_GROUP * 128 + 16, 128), (8, D_FF))
GATH_DTYPES = (F32, BF16, F32)


def _small_allreduce_adamw(rows1024, dws, dbs, dcw, dcb, logits, m_logits, v_logits,
                           small_w, small_m, small_v):
    ns = len(SMALL_LAYOUT)
    nr = len(rows1024)
    nb = len(PACK_SHAPES)

    def body(*refs):
        row_refs = refs[:nr]
        dws_ref, dbs_ref, dcw_ref, dcb_ref = refs[nr:nr + 4]
        pos = nr + 4
        tot = refs[pos:pos + nb]
        pos += nb
        pack = refs[pos:pos + nb]
        sib = refs[pos + nb:pos + 2 * nb]
        gath = refs[pos + 2 * nb:pos + 3 * nb]
        d2d_send, d2d_recv, ici_send, ici_recv = refs[pos + 3 * nb:]

        x, y, c, peers = _chip_peers()
        me = 2 * x + y
        sibling = (x, y, 1 - c)

        pack[0][...] = jnp.zeros(PACK_SHAPES[0], F32)
        for k in range(nr):
            pack[0][k:k + 1, :] = row_refs[k][0:1, :]
        pack[1][0:N_GROUP * 128, :] = dws_ref[...]
        pack[1][N_GROUP * 128:N_GROUP * 128 + 8, :] = dbs_ref[...]
        pack[1][N_GROUP * 128 + 8:, :] = jnp.zeros((8, 128), F32)
        pack[2][...] = jnp.zeros(PACK_SHAPES[2], F32)
        pack[2][0:3, :] = dcw_ref[0:3, :]
        pack[2][3:4, :] = dcb_ref[0:1, :]

        d2d = [pltpu.make_async_remote_copy(
            src_ref=pack[b], dst_ref=sib[b], send_sem=d2d_send.at[b], recv_sem=d2d_recv.at[b],
            device_id=sibling, device_id_type=MESH) for b in range(nb)]
        for cp in d2d:
            cp.start()
        for cp in d2d:
            cp.wait()
        for b in range(nb):
            gath[b][me] = (pack[b][...] + sib[b][...]).astype(GATH_DTYPES[b])

        ici, ici_wait = [], []
        for b in range(nb):
            for k, (px, py) in enumerate(peers):
                sem = dict(send_sem=ici_send.at[b * 3 + k], recv_sem=ici_recv.at[b * 3 + k],
                           device_id=(px, py, c), device_id_type=MESH)
                ici.append(pltpu.make_async_remote_copy(src_ref=gath[b].at[me], dst_ref=gath[b].at[me], **sem))
                ici_wait.append(pltpu.make_async_remote_copy(
                    src_ref=gath[b].at[me], dst_ref=gath[b].at[2 * px + py], **sem))
        for cp in ici:
            cp.start()
        for cp in ici_wait:
            cp.wait_recv()
        for cp in ici:
            cp.wait_send()

        for b in range(nb):
            tot[b][...] = ((gath[b][0].astype(F32) + gath[b][1].astype(F32)) + gath[b][2].astype(F32)
                           ) + gath[b][3].astype(F32)

    def update(*refs):
        tot = refs[:nb]
        lg_ref, mlg_ref, vlg_ref = refs[nb:nb + 3]
        pos = nb + 3
        w_refs = refs[pos:pos + ns]
        m_refs = refs[pos + ns:pos + 2 * ns]
        v_refs = refs[pos + 2 * ns:pos + 3 * ns]
        pos += 3 * ns
        loss_ref = refs[pos]
        lg_outs = refs[pos + 1:pos + 5]
        outs = refs[pos + 5:pos + 5 + 4 * ns]

        loss_ref[...] = tot[0][LOSS_ROW:LOSS_ROW + 1, :]
        lb = _sig(lg_ref[0:1, :] - lg_ref[1:2, :])
        d0 = tot[0][LB_ROW:LB_ROW + 1, :] * lb * (1.0 - lb)
        rowid = lax.broadcasted_iota(jnp.int32, (2, D_MODEL), 0)
        g_lg = jnp.where(rowid == 0, d0, -d0)
        dl, ml, vl = _adamw(lg_ref[...], g_lg, mlg_ref[...], vlg_ref[...])
        lg_outs[0][...] = g_lg
        lg_outs[1][...] = dl
        lg_outs[2][...] = ml
        lg_outs[3][...] = vl
        for si, (_, rows, b, r0) in enumerate(SMALL_LAYOUT):
            g = tot[b][r0:r0 + rows, :]
            dl, ml, vl = _adamw(w_refs[si][...], g, m_refs[si][...], v_refs[si][...])
            outs[4 * si][...] = g
            outs[4 * si + 1][...] = dl
            outs[4 * si + 2][...] = ml
            outs[4 * si + 3][...] = vl

    scratch = [pltpu.VMEM(shp, F32) for shp in PACK_SHAPES]
    scratch += [pltpu.VMEM(shp, F32) for shp in PACK_SHAPES]
    scratch += [pltpu.VMEM((N_CHIP,) + shp, dt) for shp, dt in zip(PACK_SHAPES, GATH_DTYPES)]
    scratch += [pltpu.SemaphoreType.DMA((nb,)), pltpu.SemaphoreType.DMA((nb,)),
                pltpu.SemaphoreType.DMA((3 * nb,)), pltpu.SemaphoreType.DMA((3 * nb,))]
    vm = pl.BlockSpec(memory_space=pltpu.VMEM)
    tots = _pc(body, name="small_allreduce",
               out_shape=tuple(jax.ShapeDtypeStruct(shp, F32) for shp in PACK_SHAPES),
               in_specs=[vm] * (nr + 4), out_specs=tuple([vm] * nb),
               scratch=scratch)(*rows1024, dws, dbs, dcw, dcb)

    shapes = [jax.ShapeDtypeStruct((1, D_MODEL), F32)]
    shapes += [jax.ShapeDtypeStruct((2, D_MODEL), F32)] * 4
    for w in small_w:
        shapes += [jax.ShapeDtypeStruct(w.shape, F32)] * 4
    res = _pc(update, name="small_adamw", out_shape=tuple(shapes),
              in_specs=[vm] * (nb + 3 + 3 * ns), out_specs=tuple([vm] * len(shapes)),
              )(*tots, logits, m_logits, v_logits, *small_w, *small_m, *small_v)
    return res[0], tots[2], res[1:5], res[5:]


def kernel(x, p, w_in, sgu_w_s, sgu_b_s, sgu_norm_g, sgu_norm_b, hgrn_lb_logits, hgrn_norm_g, w_branch, w_out, ln1_g, ln1_b, ffn_w_up, ffn_conv_w, ffn_conv_b, ffn_w_down, ln2_g, ln2_b, ple_w_proj, ple_w_gate, loss_target, m_w_in, m_sgu_w_s, m_sgu_b_s, m_sgu_norm_g, m_sgu_norm_b, m_hgrn_lb_logits, m_hgrn_norm_g, m_w_branch, m_w_out, m_ln1_g, m_ln1_b, m_ffn_w_up, m_ffn_conv_w, m_ffn_conv_b, m_ffn_w_down, m_ln2_g, m_ln2_b, m_ple_w_proj, m_ple_w_gate, v_w_in, v_sgu_w_s, v_sgu_b_s, v_sgu_norm_g, v_sgu_norm_b, v_hgrn_lb_logits, v_hgrn_norm_g, v_w_branch, v_w_out, v_ln1_g, v_ln1_b, v_ffn_w_up, v_ffn_conv_w, v_ffn_conv_b, v_ffn_w_down, v_ln2_g, v_ln2_b, v_ple_w_proj, v_ple_w_gate):
    t = x.shape[1]
    x2 = x.reshape(t, D_MODEL)
    x2b = x2.astype(BF16)
    p2 = p.reshape(t, PLE_DIM)
    tgt = loss_target.reshape(t, D_MODEL)
    core = lax.axis_index("c").astype(jnp.int32).reshape(1)
    chip_id = (2 * lax.axis_index("x") + lax.axis_index("y")).astype(jnp.int32).reshape(1)

    big_w = [w_in[0], w_branch[0, 0], w_branch[0, 1], w_out[0], ffn_w_up[0], ffn_w_down[0],
             ple_w_proj[0], ple_w_gate[0]]
    big_m = [m_w_in[0], m_w_branch[0, 0], m_w_branch[0, 1], m_w_out[0], m_ffn_w_up[0],
             m_ffn_w_down[0], m_ple_w_proj[0], m_ple_w_gate[0]]
    big_v = [v_w_in[0], v_w_branch[0, 0], v_w_branch[0, 1], v_w_out[0], v_ffn_w_up[0],
             v_ffn_w_down[0], v_ple_w_proj[0], v_ple_w_gate[0]]
    def halves_of(i):
        w = big_w[i]
        return w.astype(BF16).reshape(2, w.shape[0] // 2, w.shape[1])

    def stacked(g, i):
        return g.reshape(N_CHIP, big_w[i].shape[0], big_w[i].shape[1])


    cid = jnp.arange(SGU_BLOCK) // CHUNK
    maskf = (cid[:, None] >= cid[None, :]).astype(F32)
    ws_masked = sgu_w_s[0] * maskf[None]
    wm = ws_masked.astype(BF16)
    wmt = jnp.transpose(ws_masked, (0, 2, 1)).astype(BF16)
    bsb = jnp.broadcast_to(sgu_b_s[0][:, :, None], (N_GROUP, SGU_BLOCK, 128))

    up_rows = big_w[4].shape[0] // 2
    up_blocks = [big_w[4][k * up_rows:(k + 1) * up_rows].astype(BF16).reshape(2, up_rows // 2, -1)
                 for k in range(2)]
    h, win_g, (up_g,) = _in_proj_gathering(x2b, halves_of(0), chip_id, 512,
                                           _gather_comm([up_blocks[0]], blocks={0: (2, 0, None)}))
    win_st = stacked(win_g, 0)
    ya, _ = _sgu_fwd(h, wm, bsb, sgu_norm_g, sgu_norm_b)
    (yb, o_all, st_all), mix_g = _hgrn_fwd(
        h, hgrn_lb_logits, hgrn_norm_g,
        comm=_gather_comm([halves_of(i) for i in (1, 2, 3)] + [up_blocks[1]], [ffn_conv_w[0]],
                          blocks={3: (2, 1, up_g)}))
    wb0, wb1, wo = [stacked(g, i).reshape(D_MODEL, D_MODEL) for g, i in zip(mix_g[:3], (1, 2, 3))]
    wup_st = stacked(mix_g[3], 4)
    convw = jnp.transpose(mix_g[4], (1, 0, 2)).reshape(3, D_FF)
    (r1, a_br, b_br, m_bf, x1b), _ = _mix_fwd(ya, yb, h, x2, wb0, wb1, wo, ln1_g, ln1_b, 256)
    h2, act, out_g = _ffn_up_act(x1b, wup_st, convw, ffn_conv_b, 512,
                                 _gather_comm([halves_of(i) for i in (5, 6, 7)]))
    wd = stacked(out_g[0], 5).reshape(D_FF, D_MODEL)
    wpp = jnp.transpose(stacked(out_g[1], 6), (1, 0, 2)).reshape(PLE_DIM, D_MODEL)
    wpg = stacked(out_g[2], 7).reshape(D_MODEL, D_MODEL)
    dr2, dpg, dpp, loss_acc, dg2, db2 = _out_fwd_bwd(
        act, x1b, r1, p2, tgt, wd, wpg, wpp, ln1_g, ln1_b, ln2_g, ln2_b, 256)

    dh2, dr1, dcw, dcb, dg1, db1 = _ffn_bwd(h2, dr2, wd, wup_st, dpg, wpg, r1, ln1_g, convw, 256)
    d_wd = _mm_tn("ffn_down_wgrad", act, dr2, FF_TILE, 512)
    d_wpg = _mm_tn("ple_gate_wgrad", x1b, dpg, 512, D_MODEL)
    d_wpp_st = _mm_tn("ple_proj_wgrad", p2, dpp, PLE_DIM, PLE_DIM, stacked=True)
    d_wup_st = _mm("ffn_up_wgrad", x1b, dh2, TN, (2, N_CHIP),
                   pl.BlockSpec((t, 512), lambda i, j: (0, i)),
                   pl.BlockSpec((None, t, FF_TILE), lambda i, j: (j // FF_NJ, 0, j % FF_NJ)),
                   jax.ShapeDtypeStruct((N_CHIP, D_MODEL, FF_TILE), BF16),
                   pl.BlockSpec((None, 512, FF_TILE), lambda i, j: (j, i, 0)))
    da_bf, db_bf, dh, dya, dyb = _mix_bwd(dr1, h, a_br, b_br, wo, wb0, wb1, 256)
    d_wo = _mm_tn("out_proj_wgrad", m_bf, dr1, 512, 512)
    d_wb0 = _mm_tn("branch0_wgrad", ya, da_bf, 512, D_MODEL)
    d_wb1 = _mm_tn("branch1_wgrad", yb, db_bf, 512, D_MODEL)
    grads_1 = [d_wb0.reshape(4, 256, D_MODEL), d_wb1.reshape(4, 256, D_MODEL),
               d_wo.reshape(4, 256, D_MODEL), d_wup_st, d_wd.reshape(4, D_FF // 4, D_MODEL),
               d_wpp_st, d_wpg.reshape(4, 256, D_MODEL)]
    (dh, dws, dbs, dgv, dbv), recv_a1 = _sgu_bwd(h, dya, wm, wmt, bsb, sgu_norm_g, sgu_norm_b, maskf, dh,
                                                 comm=_sibling_exchange_comm(grads_1))
    parts_1 = [_rs_add_halves("rs_add_halves%d" % (i + 1), g, r, core)
               for i, (g, r) in enumerate(zip(grads_1, recv_a1))]
    (dh, dlb, dgn), recv_b1 = _hgrn_bwd(h, o_all, dyb, st_all, hgrn_lb_logits, hgrn_norm_g, dh,
                                         comm=_chip_exchange_comm(parts_1))

    grads_0 = [_in_proj_wgrad(x2b, dh, 512, D_MODEL)]
    recv_a0 = _run_comm("rs_sibling_exchange0", _sibling_exchange_comm(grads_0))
    parts_0 = [_rs_add_halves("rs_add_halves0", grads_0[0], recv_a0[0], core)]
    gx, recv_b0 = _in_proj_xgrad(dh, win_st, dr1, 512, _chip_exchange_comm(parts_0))
    parts = parts_0 + parts_1
    recv_b = list(recv_b0) + list(recv_b1)
    halves = [_rs_sum_chips("rs_sum_chips%d" % i, pt, r, chip_id)
              for i, (pt, r) in enumerate(zip(parts, recv_b))]
    theirs = _rs_send_halves(halves)
    big_out = [_adamw_rows("adamw_big%d" % i, halves[i], theirs[i], big_w[i], big_m[i], big_v[i], core)
               for i in range(len(halves))]

    small_in = dict(sgu_w_s=(sgu_w_s, m_sgu_w_s, v_sgu_w_s), sgu_b_s=(sgu_b_s, m_sgu_b_s, v_sgu_b_s),
                    sgu_norm_g=(sgu_norm_g, m_sgu_norm_g, v_sgu_norm_g),
                    sgu_norm_b=(sgu_norm_b, m_sgu_norm_b, v_sgu_norm_b),
                    hgrn_norm_g=(hgrn_norm_g, m_hgrn_norm_g, v_hgrn_norm_g),
                    ln1_g=(ln1_g, m_ln1_g, v_ln1_g), ln1_b=(ln1_b, m_ln1_b, v_ln1_b),
                    ffn_conv_b=(ffn_conv_b, m_ffn_conv_b, v_ffn_conv_b),
                    ln2_g=(ln2_g, m_ln2_g, v_ln2_g), ln2_b=(ln2_b, m_ln2_b, v_ln2_b))

    def flat(name, arr):
        rows = dict((n, r) for n, r, _, _ in SMALL_LAYOUT)[name]
        return arr.reshape(rows, arr.size // rows)

    names = [n for n, _, _, _ in SMALL_LAYOUT]
    sw = [flat(n, small_in[n][0]) for n in names]
    sm = [flat(n, small_in[n][1]) for n in names]
    sv = [flat(n, small_in[n][2]) for n in names]
    loss_rows, dcw_tot, lg_out, small_out = _small_allreduce_adamw(
        [dgv, dbv, dlb, dgn, dg1, db1, dg2, db2, loss_acc], dws.reshape(N_GROUP * 128, 128), dbs, dcw, dcb,
        hgrn_lb_logits, m_hgrn_lb_logits, v_hgrn_lb_logits, sw, sm, sv)
    loss = loss_rows[0, 0]

    chip = 2 * lax.axis_index("x") + lax.axis_index("y")
    g_cw = lax.dynamic_slice(dcw_tot, (0, chip * (D_FF // 4)), (3, D_FF // 4))
    cw_out = _adamw_whole("adamw_conv_w", g_cw, ffn_conv_w[0], m_ffn_conv_w[0], v_ffn_conv_w[0])

    res = {}
    for si, n in enumerate(names):
        shp = small_in[n][0].shape
        res[n] = tuple(small_out[4 * si + k].reshape(shp) for k in range(4))
    res["hgrn_lb_logits"] = tuple(lg_out)
    res["ffn_conv_w"] = (g_cw[None],) + tuple(o[None] for o in cw_out)

    def big(i):
        return tuple(big_out[i])

    res["w_in"] = tuple(o[None] for o in big(0))
    res["w_branch"] = tuple(jnp.stack([o0, o1])[None] for o0, o1 in zip(big(1), big(2)))
    res["w_out"] = tuple(o[None] for o in big(3))
    res["ffn_w_up"] = tuple(o[None] for o in big(4))
    res["ffn_w_down"] = tuple(o[None] for o in big(5))
    res["ple_w_proj"] = tuple(o[None] for o in big(6))
    res["ple_w_gate"] = tuple(o[None] for o in big(7))

    order = ["w_in", "sgu_w_s", "sgu_b_s", "sgu_norm_g", "sgu_norm_b", "hgrn_lb_logits",
             "hgrn_norm_g", "w_branch", "w_out", "ln1_g", "ln1_b", "ffn_w_up", "ffn_conv_w",
             "ffn_conv_b", "ffn_w_down", "ln2_g", "ln2_b", "ple_w_proj", "ple_w_gate"]
    outs = [loss, gx]
    for k in range(4):
        outs += [res[n][k] for n in order]
    return tuple(outs)
```

```python
import jax
import jax.numpy as jnp
from jax import lax
from jax.experimental import pallas as pl
from jax.experimental.pallas import tpu as pltpu

F32 = jnp.float32
BF16 = jnp.bfloat16
HIGHEST = lax.Precision.HIGHEST
MESH = pl.DeviceIdType.MESH

D_MODEL = 1024
CHUNK = 64
SGU_BLOCK = 128
SGU_STEP_BLOCKS = 4
SGU_ROWS = SGU_STEP_BLOCKS * SGU_BLOCK
N_GROUP = 8
N_HEAD = 8
HEAD_DIM = 128
D_FF = 2816
PLE_DIM = 256
LN_EPS = 1e-5
RMS_EPS = 1e-6
ALPHA = 2.0 ** 0.25
N_CHIP = 4

ADAM_LR = 0.001
ADAM_B1 = 0.9
ADAM_B2 = 0.999
ADAM_EPS = 1e-08
ADAM_WD = 0.01
ADAM_STEP = 10

VMEM_LIMIT = 56 * 1024 * 1024

NN = (((1,), (0,)), ((), ()))
NT = (((1,), (1,)), ((), ()))
TN = (((0,), (0,)), ((), ()))


def _pc(body, *, name, out_shape, grid=None, in_specs=None, out_specs=None, scratch=(),
        sem=None, nsp=0, vmem=VMEM_LIMIT, aliases=None):
    params = dict(vmem_limit_bytes=vmem)
    if sem is not None:
        params["dimension_semantics"] = sem
    kw = dict(name=name, out_shape=out_shape, compiler_params=pltpu.CompilerParams(**params))
    if aliases:
        kw["input_output_aliases"] = aliases
    if nsp:
        kw["grid_spec"] = pltpu.PrefetchScalarGridSpec(
            num_scalar_prefetch=nsp, grid=grid, in_specs=in_specs, out_specs=out_specs,
            scratch_shapes=list(scratch))
    else:
        if grid is not None:
            kw["grid"] = grid
        if in_specs is not None:
            kw["in_specs"] = in_specs
            kw["out_specs"] = out_specs
        kw["scratch_shapes"] = list(scratch)
    return pl.pallas_call(body, **kw)


def _dot(a, b, dims=NN):
    return lax.dot_general(a.astype(BF16), b.astype(BF16), dims, preferred_element_type=F32)


def _dot32(a, b, dims=NN):
    return lax.dot_general(a, b, dims, precision=HIGHEST, preferred_element_type=F32)


def _sig(x):
    return 1.0 / (1.0 + jnp.exp(-x))


_GC = 0.7978845608028654
_GA = 0.044715


def _gelu(x):
    return 0.5 * x * (1.0 + jnp.tanh(_GC * (x + _GA * x * x * x)))


def _gelu_and_grad(x):
    t = jnp.tanh(_GC * (x + _GA * x * x * x))
    g = 0.5 * x * (1.0 + t)
    dg = 0.5 * (1.0 + t) + 0.5 * x * (1.0 - t * t) * _GC * (1.0 + 3.0 * _GA * x * x)
    return g, dg


def _ln_stats(r):
    mu = jnp.mean(r, axis=-1, keepdims=True)
    xc = r - mu
    var = jnp.mean(xc * xc, axis=-1, keepdims=True)
    rstd = lax.rsqrt(var + LN_EPS)
    return xc * rstd, rstd


def _ln_bwd(dxh, xh, rstd):
    m1 = jnp.mean(dxh, axis=-1, keepdims=True)
    m2 = jnp.mean(dxh * xh, axis=-1, keepdims=True)
    return rstd * (dxh - m1 - xh * m2)


def _colsum8(v):
    return jnp.broadcast_to(jnp.sum(v, axis=0, keepdims=True), (8, v.shape[1]))


def _adamw(w, g, m, v):
    m2 = ADAM_B1 * m + (1.0 - ADAM_B1) * g
    v2 = ADAM_B2 * v + (1.0 - ADAM_B2) * (g * g)
    m_hat = m2 / (1.0 - ADAM_B1 ** ADAM_STEP)
    v_hat = v2 / (1.0 - ADAM_B2 ** ADAM_STEP)
    delta = -ADAM_LR * (m_hat / (jnp.sqrt(v_hat) + ADAM_EPS) + ADAM_WD * w)
    return delta, m2, v2


def _row_tile(rows, cols, itemsize=4, budget=1 << 20, mult=8):
    best = mult
    for tr in range(mult, rows + 1, mult):
        if rows % tr == 0 and tr * cols * itemsize <= budget:
            best = tr
    return best


def _mm(name, a, b, dims, grid, a_spec, b_spec, out_shape, o_spec):
    out_dtype = out_shape.dtype

    def body(a_ref, b_ref, o_ref):
        o_ref[...] = _dot(a_ref[...], b_ref[...], dims).astype(out_dtype)

    return _pc(body, name=name, out_shape=out_shape, grid=grid, in_specs=[a_spec, b_spec],
               out_specs=o_spec, sem=("parallel", "parallel"))(a, b)


class _Comm:
    def __init__(self, ins, out_shapes, sems, start, finish, middle=None, finish_late=None, aliases=None):
        self.ins, self.out_shapes, self.sems = list(ins), list(out_shapes), list(sems)
        self.start, self.finish = start, finish
        self.middle, self.finish_late = middle, finish_late
        self.aliases = aliases or {}


def _hosted_call(body, comm, first, last, *, name, out_shape, grid, in_specs, out_specs, scratch, sem,
                 args, aliases=None, mid=None):
    n_in, n_out, n_scr = len(in_specs), len(out_shape), len(scratch)
    nci, nco = len(comm.ins), len(comm.out_shapes)

    def wrapped(*refs):
        pos = n_in
        own_in, c_in = refs[:pos], refs[pos:pos + nci]
        pos += nci
        own_out, c_out = refs[pos:pos + n_out], refs[pos + n_out:pos + n_out + nco]
        pos += n_out + nco
        own_scr, c_sem = refs[pos:pos + n_scr], refs[pos + n_scr:]

        @pl.when(first())
        def _():
            comm.start(c_in, c_out, c_sem)

        body(*own_in, *own_out, *own_scr)

        if mid is not None:
            @pl.when(mid())
            def _():
                comm.middle(c_in, c_out, c_sem)

        @pl.when(last())
        def _():
            (comm.finish if mid is None else comm.finish_late)(c_in, c_out, c_sem)

    return _pc(wrapped, name=name, out_shape=tuple(out_shape) + tuple(comm.out_shapes), grid=grid,
               in_specs=list(in_specs) + [ANY] * nci, out_specs=tuple(out_specs) + tuple([ANY] * nco),
               scratch=list(scratch) + comm.sems, sem=sem,
               aliases={**(aliases or {}), **{n_in + ci: n_out + co for ci, co in comm.aliases.items()}},
               )(*args, *comm.ins)


def _grid1_call(body, comm, n, *, name, out_shape, in_specs, out_specs, scratch, args, aliases=None):
    if comm is None:
        return _pc(body, name=name, out_shape=out_shape, grid=(n,), in_specs=in_specs, out_specs=out_specs,
                   scratch=scratch, sem=("arbitrary",), aliases=aliases)(*args), ()
    res = _hosted_call(body, comm, lambda: pl.program_id(0) == 0, lambda: pl.program_id(0) == n - 1,
                       name=name, out_shape=out_shape, grid=(n,), in_specs=in_specs,
                       out_specs=out_specs, scratch=scratch, sem=("arbitrary",), args=args, aliases=aliases)
    return res[:len(out_shape)], res[len(out_shape):]


def _run_comm(name, comm):
    nci, nco = len(comm.ins), len(comm.out_shapes)

    def body(*refs):
        c_in, c_out, c_sem = refs[:nci], refs[nci:nci + nco], refs[nci + nco:]
        comm.start(c_in, c_out, c_sem)
        comm.finish(c_in, c_out, c_sem)

    return _pc(body, name=name, out_shape=tuple(comm.out_shapes), in_specs=[ANY] * nci,
               out_specs=tuple([ANY] * nco), scratch=comm.sems)(*comm.ins)


def _mm_tn(name, a, b, tm, tn, stacked=False):
    t, m = a.shape
    _, n = b.shape
    if stacked:
        assert tm == m
        out_shape = jax.ShapeDtypeStruct((n // tn, m, tn), BF16)
        o_spec = pl.BlockSpec((None, tm, tn), lambda i, j: (j, 0, 0))
    else:
        out_shape = jax.ShapeDtypeStruct((m, n), BF16)
        o_spec = pl.BlockSpec((tm, tn), lambda i, j: (i, j))
    return _mm(name, a, b, TN, (m // tm, n // tn),
               pl.BlockSpec((t, tm), lambda i, j: (0, i)),
               pl.BlockSpec((t, tn), lambda i, j: (0, j)),
               out_shape, o_spec)


DH_SLOT = (2, 0, 1, 3)


def _dh_slot(j):
    return jnp.where(j == 3, 3, (j + 2) % 3)


def _in_proj_wgrad(x2b, dh, tm, tn):
    t, m = x2b.shape
    n = dh.shape[2]

    def body(a_ref, b_ref, o_ref):
        o_ref[...] = _dot(a_ref[...], b_ref[...], TN).astype(BF16)

    return _pc(body, name="in_proj_wgrad", out_shape=jax.ShapeDtypeStruct((N_CHIP, m, n), BF16),
               grid=(N_CHIP, m // tm, n // tn),
               in_specs=[pl.BlockSpec((t, tm), lambda j, i, k: (0, i)),
                         pl.BlockSpec((None, t, tn), lambda j, i, k: (_dh_slot(j), 0, k))],
               out_specs=pl.BlockSpec((None, tm, tn), lambda j, i, k: (j, i, k)),
               sem=("parallel", "parallel", "parallel"))(x2b, dh)


def _in_proj_xgrad(dh, win_st, dr1, tm, comm, buf):
    t = dr1.shape[0]
    ni = t // tm

    def body(a_ref, b_ref, add_ref, buf_ref, o_ref, acc):
        j = pl.program_id(1)
        prod = _dot(a_ref[...], b_ref[...], NT)

        @pl.when(j == 0)
        def _():
            acc[...] = prod + ALPHA * add_ref[...].astype(F32)

        @pl.when((j > 0) & (j < N_CHIP - 1))
        def _():
            acc[...] += prod

        @pl.when(j == N_CHIP - 1)
        def _():
            o_ref[...] = acc[...] + prod

    tile = pl.BlockSpec((tm, D_MODEL), lambda i, j: (i, 0))
    res = _hosted_call(body, comm,
                       lambda: (pl.program_id(0) == 0) & (pl.program_id(1) == 0),
                       lambda: (pl.program_id(0) == ni - 1) & (pl.program_id(1) == N_CHIP - 1),
                       name="in_proj_xgrad", out_shape=(jax.ShapeDtypeStruct((1, t, D_MODEL), F32),),
                       grid=(ni, N_CHIP),
                       in_specs=[pl.BlockSpec((None, tm, 2 * D_MODEL), lambda i, j: (_dh_slot(j), i, 0)),
                                 pl.BlockSpec((None, D_MODEL, 2 * D_MODEL), lambda i, j: (j, 0, 0)),
                                 tile, ANY],
                       out_specs=(pl.BlockSpec((None, tm, D_MODEL), lambda i, j: (0, i, 0)),),
                       scratch=[pltpu.VMEM((tm, D_MODEL), F32)],
                       sem=("arbitrary", "arbitrary"), args=[dh, win_st, dr1, buf], aliases={3: 0})
    return res[0], res[1:]


def _sgu_mixed(v, wm_ref, bsb_ref, gv, bv):
    gl, dgl = _gelu_and_grad(v)
    vh, rstd = _ln_stats(gl)
    vn = vh * gv + bv
    mixed = []
    for g in range(N_GROUP):
        sl = slice(g * 128, (g + 1) * 128)
        mixed.append(_dot(wm_ref[g], vn[:, sl]) + bsb_ref[g])
    return dgl, vh, rstd, vn, mixed


def _sgu_fwd(h, wm, bsb, gv, bv, comm=None):
    t = h.shape[0]

    def body(u_ref, v_ref, wm_ref, bsb_ref, gv_ref, bv_ref, ya_ref):
        for bb in range(SGU_STEP_BLOCKS):
            rows = slice(bb * SGU_BLOCK, (bb + 1) * SGU_BLOCK)
            u = u_ref[rows, :].astype(F32)
            _, _, _, _, mixed = _sgu_mixed(v_ref[rows, :].astype(F32), wm_ref, bsb_ref, gv_ref[...],
                                           bv_ref[...])
            gu = _gelu(u)
            for g in range(N_GROUP):
                sl = slice(g * 128, (g + 1) * 128)
                ya_ref[rows, sl] = (gu[:, sl] * mixed[g]).astype(BF16)

    full3 = pl.BlockSpec((N_GROUP, 128, 128), lambda i: (0, 0, 0))
    vec = pl.BlockSpec((1, D_MODEL), lambda i: (0, 0))
    (ya,), extra = _grid1_call(
        body, comm, t // SGU_ROWS, name="sgu_fwd",
        out_shape=(jax.ShapeDtypeStruct((t, D_MODEL), BF16),),
        in_specs=[pl.BlockSpec((SGU_ROWS, D_MODEL), lambda i: (i, 0)),
                  pl.BlockSpec((SGU_ROWS, D_MODEL), lambda i: (i, 1)),
                  full3, full3, vec, vec],
        out_specs=(pl.BlockSpec((SGU_ROWS, D_MODEL), lambda i: (i, 0)),),
        scratch=[], args=(h, h, wm, bsb, gv, bv))
    return ya, extra


def _sgu_bwd(h, dya, wm, wmt, bsb, gv, bv, maskf, dh_buf, comm=None):
    t = h.shape[0]
    nb = t // SGU_ROWS

    def body(u_ref, v_ref, dya_ref, wm_ref, wmt_ref, bsb_ref, gv_ref, bv_ref, mask_ref, dh_buf_ref,
             dh_ref, dws_ref, dbs_ref, dgv_ref, dbv_ref, dmix_acc):
        i = pl.program_id(0)

        @pl.when(i == 0)
        def _():
            dws_ref[...] = jnp.zeros_like(dws_ref)
            dgv_ref[...] = jnp.zeros_like(dgv_ref)
            dbv_ref[...] = jnp.zeros_like(dbv_ref)
            dmix_acc[...] = jnp.zeros_like(dmix_acc)

        gvv = gv_ref[...]
        for bb in range(SGU_STEP_BLOCKS):
            rows = slice(bb * SGU_BLOCK, (bb + 1) * SGU_BLOCK)
            u = u_ref[rows, :].astype(F32)
            dgl_v, vh, rstd, vn, mixed = _sgu_mixed(v_ref[rows, :].astype(F32), wm_ref, bsb_ref, gvv,
                                                    bv_ref[...])
            gu, dgl_u = _gelu_and_grad(u)
            dya_v = dya_ref[rows, :].astype(F32)
            dvn_parts = []
            for g in range(N_GROUP):
                sl = slice(g * 128, (g + 1) * 128)
                d_y = dya_v[:, sl]
                dh_ref[rows, sl] = (d_y * mixed[g] * dgl_u[:, sl]).astype(BF16)
                d_mixed = d_y * gu[:, sl]
                dmix_acc[g] += d_mixed
                dws_ref[g] += _dot(d_mixed, vn[:, sl], NT) * mask_ref[...]
                dvn_parts.append(_dot(wmt_ref[g], d_mixed))
            dvn = jnp.concatenate(dvn_parts, axis=1)
            dgv_ref[...] += _colsum8(dvn * vh)
            dbv_ref[...] += _colsum8(dvn)
            d_gl = _ln_bwd(dvn * gvv, vh, rstd)
            dh_ref[rows, D_MODEL:] = (d_gl * dgl_v).astype(BF16)

        @pl.when(i == nb - 1)
        def _():
            rowid = lax.broadcasted_iota(jnp.int32, (8, 128), 0)
            ones = jnp.ones((8, 128), F32)
            acc = jnp.zeros((8, 128), F32)
            for g in range(N_GROUP):
                rs = _dot32(ones, dmix_acc[g], NT)
                acc = jnp.where(rowid == g, rs, acc)
            dbs_ref[...] = acc

    full3 = pl.BlockSpec((N_GROUP, 128, 128), lambda i: (0, 0, 0))
    vec = pl.BlockSpec((1, D_MODEL), lambda i: (0, 0))
    acc8 = pl.BlockSpec((8, D_MODEL), lambda i: (0, 0))
    return _grid1_call(
        body, comm, nb, name="sgu_bwd",
        out_shape=(jax.ShapeDtypeStruct(dh_buf.shape, BF16),
                   jax.ShapeDtypeStruct((N_GROUP, 128, 128), F32),
                   jax.ShapeDtypeStruct((8, 128), F32),
                   jax.ShapeDtypeStruct((8, D_MODEL), F32),
                   jax.ShapeDtypeStruct((8, D_MODEL), F32)),
        in_specs=[pl.BlockSpec((SGU_ROWS, D_MODEL), lambda i: (i, 0)),
                  pl.BlockSpec((SGU_ROWS, D_MODEL), lambda i: (i, 1)),
                  pl.BlockSpec((SGU_ROWS, D_MODEL), lambda i: (i, 0)),
                  full3, full3, full3, vec, vec,
                  pl.BlockSpec((128, 128), lambda i: (0, 0)), ANY],
        out_specs=(pl.BlockSpec((None, SGU_ROWS, 2 * D_MODEL), lambda i: (DH_SLOT[0], i, 0)),
                   full3, pl.BlockSpec((8, 128), lambda i: (0, 0)), acc8, acc8),
        scratch=[pltpu.VMEM((N_GROUP, 128, 128), F32)],
        args=(h, h, dya, wm, wmt, bsb, gv, bv, maskf, dh_buf), aliases={9: 0})


def _tri_masks():
    row = lax.broadcasted_iota(jnp.int32, (CHUNK, CHUNK), 0)
    col = lax.broadcasted_iota(jnp.int32, (CHUNK, CHUNK), 1)
    return col <= row, col >= row


def _heads(v):
    return [v[:, hd * HEAD_DIM:(hd + 1) * HEAD_DIM] for hd in range(N_HEAD)]


def _tri_cumsum(tri_bf, v):
    hi = v.astype(BF16)
    r = v - hi.astype(F32)
    mid = r.astype(BF16)
    lo = (r - mid.astype(F32)).astype(BF16)
    return _dot(tri_bf, hi) + _dot(tri_bf, mid) + _dot(tri_bf, lo)


def _hgrn_chunk(q, fp, ii, lb, st_heads, causal, with_o=True):
    sg = _sig(fp)
    f = lb + (1.0 - lb) * sg
    k = 1.0 - f
    c = _tri_cumsum(causal.astype(BF16), jnp.log(f))
    ec = jnp.exp(c)
    en = jnp.exp(-c)
    sq = _sig(q)
    qt = q * sq * ec
    kt = k * en
    ecl = jnp.exp(c[CHUNK - 1:CHUNK, :])
    kk = kt * ecl
    qtb, ktb, iib, kkb = qt.astype(BF16), kt.astype(BF16), ii.astype(BF16), kk.astype(BF16)
    attn, o = [], []
    for hd, (qh, kh, ih) in enumerate(zip(_heads(qtb), _heads(ktb), _heads(iib))):
        a = jnp.where(causal, _dot(qh, kh, NT), 0.0).astype(BF16)
        attn.append(a)
        if with_o:
            o.append(_dot(a, ih) + _dot(qh, st_heads[hd], NT))
    return dict(sg=sg, f=f, k=k, ec=ec, en=en, sq=sq, ecl=ecl, kk=kk, qtb=qtb, ktb=ktb, iib=iib,
                kkb=kkb, attn=attn, o=o)


def _rms_heads(o_heads):
    rinv = [lax.rsqrt(jnp.mean(o * o, axis=-1, keepdims=True) + RMS_EPS) for o in o_heads]
    return rinv, jnp.concatenate([o * r for o, r in zip(o_heads, rinv)], axis=1)


HG_CHUNKS = 8
HG_ROWS = HG_CHUNKS * CHUNK


def _hgrn_fwd(h, logits, gn, comm=None):
    t = h.shape[0]
    nb = t // HG_ROWS

    def body(q_ref, f_ref, i_ref, og_ref, lg_ref, gn_ref, yb_ref, o_ref, st_ref, state):
        @pl.when(pl.program_id(0) == 0)
        def _():
            state[...] = jnp.zeros_like(state)

        causal, _ = _tri_masks()
        lb = _sig(lg_ref[0:1, :] - lg_ref[1:2, :])
        gnv = gn_ref[...]
        st = [state[hd] for hd in range(N_HEAD)]
        for cc in range(HG_CHUNKS):
            rows = slice(cc * CHUNK, (cc + 1) * CHUNK)
            og = og_ref[rows, :].astype(F32)
            r = _hgrn_chunk(q_ref[rows, :].astype(F32), f_ref[rows, :].astype(F32),
                            i_ref[rows, :].astype(F32), lb, [s.astype(BF16) for s in st], causal)
            o_bf = jnp.concatenate(r["o"], axis=1).astype(BF16)
            o_ref[rows, :] = o_bf
            _, on = _rms_heads(_heads(o_bf.astype(F32)))
            yb_ref[rows, :] = (on * gnv * (og * _sig(og))).astype(BF16)
            for hd in range(N_HEAD):
                st_ref[cc, hd] = st[hd]
            st = [s * e + _dot(ih, kh, TN)
                  for s, e, ih, kh in zip(st, _heads(r["ecl"]), _heads(r["iib"]), _heads(r["kkb"]))]
        for hd in range(N_HEAD):
            state[hd] = st[hd]

    def col(k):
        return pl.BlockSpec((HG_ROWS, D_MODEL), lambda ci: (ci, k))

    return _grid1_call(body, comm, nb, name="hgrn_fwd",
                       out_shape=(jax.ShapeDtypeStruct((t, D_MODEL), BF16),
                                  jax.ShapeDtypeStruct((t, D_MODEL), BF16),
                                  jax.ShapeDtypeStruct((t // CHUNK, N_HEAD, HEAD_DIM, HEAD_DIM), F32)),
                       in_specs=[col(2), col(3), col(4), col(5),
                                 pl.BlockSpec((2, D_MODEL), lambda ci: (0, 0)),
                                 pl.BlockSpec((1, D_MODEL), lambda ci: (0, 0))],
                       out_specs=(pl.BlockSpec((HG_ROWS, D_MODEL), lambda ci: (ci, 0)),
                                  pl.BlockSpec((HG_ROWS, D_MODEL), lambda ci: (ci, 0)),
                                  pl.BlockSpec((HG_CHUNKS, N_HEAD, HEAD_DIM, HEAD_DIM),
                                               lambda ci: (ci, 0, 0, 0))),
                       scratch=[pltpu.VMEM((N_HEAD, HEAD_DIM, HEAD_DIM), F32)],
                       args=(h, h, h, h, logits, gn))


def _hgrn_chunk_bwd(q, fp, ii, og, o_saved, dy, gnv, lb, st, dsn, causal, anti):
    stb = [s.astype(BF16) for s in st]
    dsnb = [s.astype(BF16) for s in dsn]
    r = _hgrn_chunk(q, fp, ii, lb, stb, causal, with_o=False)
    rinv, on = _rms_heads(_heads(o_saved))
    so = _sig(og)
    sil = og * so
    d_og = dy * on * gnv * (so * (1.0 + og * (1.0 - so)))
    d_on = dy * gnv * sil
    d_ob = jnp.concatenate(
        [ri * (dn - oh * jnp.mean(dn * oh, axis=-1, keepdims=True))
         for ri, dn, oh in zip(rinv, _heads(d_on), _heads(on))], axis=1).astype(BF16)
    d_i, d_qt, d_kt, d_kk, d_st, st_dsn = [], [], [], [], [], []
    ecl = _heads(r["ecl"])
    for hd, (dh, qh, kh, ih, kkh) in enumerate(zip(_heads(d_ob), _heads(r["qtb"]), _heads(r["ktb"]),
                                                   _heads(r["iib"]), _heads(r["kkb"]))):
        d_attn = jnp.where(causal, _dot(dh, ih, NT), 0.0).astype(BF16)
        d_i.append(_dot(r["attn"][hd], dh, TN) + _dot(kkh, dsnb[hd], NT))
        d_qt.append(_dot(d_attn, kh) + _dot(dh, stb[hd]))
        d_kt.append(_dot(d_attn, qh, TN))
        d_kk.append(_dot(ih, dsnb[hd]))
        d_st.append(_dot(dh, qh, TN) + dsn[hd] * ecl[hd])
        st_dsn.append(jnp.sum(st[hd] * dsn[hd], axis=0, keepdims=True))
    d_qt = jnp.concatenate(d_qt, axis=1)
    d_kt = jnp.concatenate(d_kt, axis=1)
    d_kk = jnp.concatenate(d_kk, axis=1)
    kk = r["kk"]
    d_cl = r["ecl"] * jnp.concatenate(st_dsn, axis=1) + jnp.sum(kk * d_kk, axis=0, keepdims=True)
    d_k = (d_kk * r["ecl"] + d_kt) * r["en"]
    d_c = d_qt * r["qtb"].astype(F32) - d_kt * r["ktb"].astype(F32) - d_kk * kk
    rowid = lax.broadcasted_iota(jnp.int32, (CHUNK, D_MODEL), 0)
    d_c = d_c + jnp.where(rowid == CHUNK - 1, d_cl, 0.0)
    d_lf = _tri_cumsum(anti.astype(BF16), d_c)
    d_f = d_lf / r["f"] - d_k
    sg, sq = r["sg"], r["sq"]
    d_q = d_qt * r["ec"] * (sq * (1.0 + q * (1.0 - sq)))
    d_fp = d_f * (1.0 - lb) * sg * (1.0 - sg)
    return (d_q, d_fp, jnp.concatenate(d_i, axis=1), d_og, d_st,
            _colsum8(dy * on * sil), _colsum8(d_f * (1.0 - sg)))


def _hgrn_bwd(h, o_all, dyb, st_all, logits, gn, dh_buf, comm=None):
    t = h.shape[0]
    nb = t // HG_ROWS

    def body(q_ref, f_ref, i_ref, og_ref, o_ref, dyb_ref, st_ref, lg_ref, gn_ref, dh_buf_ref,
             dh_ref, dlb_ref, dgn_ref, dstate):
        @pl.when(pl.program_id(0) == 0)
        def _():
            dstate[...] = jnp.zeros_like(dstate)
            dlb_ref[...] = jnp.zeros_like(dlb_ref)
            dgn_ref[...] = jnp.zeros_like(dgn_ref)

        causal, anti = _tri_masks()
        lb = _sig(lg_ref[0:1, :] - lg_ref[1:2, :])
        gnv = gn_ref[...]
        dsn = [dstate[hd] for hd in range(N_HEAD)]
        dgn_acc = jnp.zeros((8, D_MODEL), F32)
        dlb_acc = jnp.zeros((8, D_MODEL), F32)
        for cc in reversed(range(HG_CHUNKS)):
            rows = slice(cc * CHUNK, (cc + 1) * CHUNK)
            d_q, d_fp, d_i, d_og, dsn, dgn_c, dlb_c = _hgrn_chunk_bwd(
                q_ref[rows, :].astype(F32), f_ref[rows, :].astype(F32), i_ref[rows, :].astype(F32),
                og_ref[rows, :].astype(F32), o_ref[rows, :].astype(F32), dyb_ref[rows, :].astype(F32), gnv, lb,
                [st_ref[cc, hd] for hd in range(N_HEAD)], dsn, causal, anti)
            dgn_acc = dgn_acc + dgn_c
            dlb_acc = dlb_acc + dlb_c
            dh_ref[0, rows, :D_MODEL] = d_q.astype(BF16)
            dh_ref[0, rows, D_MODEL:] = d_fp.astype(BF16)
            dh_ref[1, rows, :D_MODEL] = d_i.astype(BF16)
            dh_ref[1, rows, D_MODEL:] = d_og.astype(BF16)
        dgn_ref[...] += dgn_acc
        dlb_ref[...] += dlb_acc
        for hd in range(N_HEAD):
            dstate[hd] = dsn[hd]

    def col(k):
        return pl.BlockSpec((HG_ROWS, D_MODEL), lambda ci: (nb - 1 - ci, k))

    acc8 = pl.BlockSpec((8, D_MODEL), lambda ci: (0, 0))
    pair = pl.BlockSpec((2, HG_ROWS, 2 * D_MODEL), lambda ci: (0, nb - 1 - ci, 0))
    return _grid1_call(body, comm, nb, name="hgrn_bwd",
                       out_shape=(jax.ShapeDtypeStruct(dh_buf.shape, BF16),
                                  jax.ShapeDtypeStruct((8, D_MODEL), F32),
                                  jax.ShapeDtypeStruct((8, D_MODEL), F32)),
                       in_specs=[col(2), col(3), col(4), col(5), col(0),
                                 pl.BlockSpec((HG_ROWS, D_MODEL), lambda ci: (nb - 1 - ci, 0)),
                                 pl.BlockSpec((HG_CHUNKS, N_HEAD, HEAD_DIM, HEAD_DIM),
                                              lambda ci: (nb - 1 - ci, 0, 0, 0)),
                                 pl.BlockSpec((2, D_MODEL), lambda ci: (0, 0)),
                                 pl.BlockSpec((1, D_MODEL), lambda ci: (0, 0)), ANY],
                       out_specs=(pair, acc8, acc8),
                       scratch=[pltpu.VMEM((N_HEAD, HEAD_DIM, HEAD_DIM), F32)],
                       args=(h, h, h, h, o_all, dyb, st_all, logits, gn, dh_buf), aliases={9: 0})


def _mix_fwd(ya, yb, h, x, wb0, wb1, wo, g1, b1, tm, comm=None):
    t = x.shape[0]

    def body(ya_ref, yb_ref, ga_ref, gb_ref, x_ref, wb0_ref, wb1_ref, wo_ref, g1_ref, b1_ref,
             r1_ref, a_ref, b_ref, m_ref, x1_ref):
        a = _dot(ya_ref[...], wb0_ref[...])
        b = _dot(yb_ref[...], wb1_ref[...])
        m = _sig(ga_ref[...].astype(F32)) * a + _sig(gb_ref[...].astype(F32)) * b
        r1 = ALPHA * x_ref[...] + _dot(m, wo_ref[...])
        xh, _ = _ln_stats(r1)
        r1_ref[...] = r1
        a_ref[...] = a.astype(BF16)
        b_ref[...] = b.astype(BF16)
        m_ref[...] = m.astype(BF16)
        x1_ref[...] = (xh * g1_ref[...] + b1_ref[...]).astype(BF16)

    tile = pl.BlockSpec((tm, D_MODEL), lambda i: (i, 0))
    wsp = pl.BlockSpec((D_MODEL, D_MODEL), lambda i: (0, 0))
    vec = pl.BlockSpec((1, D_MODEL), lambda i: (0, 0))
    f32o = jax.ShapeDtypeStruct((t, D_MODEL), F32)
    bfo = jax.ShapeDtypeStruct((t, D_MODEL), BF16)
    return _grid1_call(body, comm, t // tm, name="mix_fwd", out_shape=(f32o, bfo, bfo, bfo, bfo),
                       in_specs=[tile, tile,
                                 pl.BlockSpec((tm, D_MODEL), lambda i: (i, 6)),
                                 pl.BlockSpec((tm, D_MODEL), lambda i: (i, 7)),
                                 tile, wsp, wsp, wsp, vec, vec],
                       out_specs=(tile, tile, tile, tile, tile),
                       scratch=[], args=(ya, yb, h, h, x, wb0, wb1, wo, g1, b1))


def _mix_bwd(dr1, h, a, b, wo, wb0, wb1, tm):
    t = dr1.shape[0]

    def body(dr1_ref, ga_ref, gb_ref, a_ref, b_ref, wo_ref, wb0_ref, wb1_ref,
             da_ref, db_ref, dh3_ref, dya_ref, dyb_ref):
        d_m = _dot(dr1_ref[...], wo_ref[...], NT)
        sa = _sig(ga_ref[...].astype(F32))
        sb = _sig(gb_ref[...].astype(F32))
        d_a = (d_m * sa).astype(BF16)
        d_b = (d_m * sb).astype(BF16)
        da_ref[...] = d_a
        db_ref[...] = d_b
        dh3_ref[:, :D_MODEL] = (d_m * a_ref[...].astype(F32) * sa * (1.0 - sa)).astype(BF16)
        dh3_ref[:, D_MODEL:] = (d_m * b_ref[...].astype(F32) * sb * (1.0 - sb)).astype(BF16)
        dya_ref[...] = _dot(d_a, wb0_ref[...], NT).astype(BF16)
        dyb_ref[...] = _dot(d_b, wb1_ref[...], NT).astype(BF16)

    tile = pl.BlockSpec((tm, D_MODEL), lambda i: (i, 0))
    wsp = pl.BlockSpec((D_MODEL, D_MODEL), lambda i: (0, 0))
    f32o = jax.ShapeDtypeStruct((t, D_MODEL), F32)
    bfo = jax.ShapeDtypeStruct((t, D_MODEL), BF16)
    return _pc(body, name="mix_bwd",
               out_shape=(bfo, bfo, jax.ShapeDtypeStruct((N_CHIP, t, 2 * D_MODEL), BF16), bfo, bfo),
               grid=(t // tm,),
               in_specs=[tile,
                         pl.BlockSpec((tm, D_MODEL), lambda i: (i, 6)),
                         pl.BlockSpec((tm, D_MODEL), lambda i: (i, 7)),
                         tile, tile, wsp, wsp, wsp],
               out_specs=(tile, tile, pl.BlockSpec((None, tm, 2 * D_MODEL), lambda i: (DH_SLOT[3], i, 0)),
                          tile, tile),
               sem=("parallel",))(dr1, h, h, a, b, wo, wb0, wb1)


FF_TILE = 1408
FF_NJ = D_FF // FF_TILE


def _shift_down(v, k):
    return pltpu.roll(v, k, 0)


def _shift_up(v, k):
    return pltpu.roll(v, v.shape[0] - k, 0)


HALO = 16
FF_PIECES = ((0, 768), (768, FF_TILE))


def _ffn_up_act(x1b, wup_st, convw, convb, tm, comm):
    t = x1b.shape[0]
    ni = t // tm
    nth = tm // HALO

    def body(x_ref, xp_ref, wg_ref, wv_ref, cw_ref, cb_ref, h2_ref, act_ref):
        wg = wg_ref[...]
        gate = _dot(x_ref[...], wg).astype(BF16)
        val = _dot(x_ref[...], wv_ref[...]).astype(BF16)
        prev = (_dot(xp_ref[...], wg) * (pl.program_id(0) > 0).astype(F32)).astype(BF16)
        h2_ref[0] = gate
        h2_ref[1] = val
        ext = jnp.concatenate([prev.astype(F32), gate.astype(F32)], axis=0)
        gc = (cw_ref[0:1, :] * _shift_down(ext, 2) + cw_ref[1:2, :] * _shift_down(ext, 1)
              + cw_ref[2:3, :] * ext + cb_ref[...])[HALO:, :].astype(BF16)
        h2_ref[2] = gc
        act_ref[...] = (_gelu(gc.astype(F32)) * val.astype(F32)).astype(BF16)

    res = _hosted_call(
        body, comm,
        lambda: (pl.program_id(0) == 0) & (pl.program_id(1) == 0),
        lambda: (pl.program_id(0) == ni - 1) & (pl.program_id(1) == FF_NJ - 1),
        mid=lambda: (pl.program_id(0) == max(ni - 2, 0)) & (pl.program_id(1) == 0),
        name="ffn_up",
        out_shape=(jax.ShapeDtypeStruct((3, t, D_FF), BF16), jax.ShapeDtypeStruct((t, D_FF), BF16)),
        grid=(ni, FF_NJ),
        in_specs=[pl.BlockSpec((tm, D_MODEL), lambda i, j: (i, 0)),
                  pl.BlockSpec((HALO, D_MODEL), lambda i, j: (jnp.maximum(i * nth - 1, 0), 0)),
                  pl.BlockSpec((None, D_MODEL, FF_TILE), lambda i, j: (j, 0, 0)),
                  pl.BlockSpec((None, D_MODEL, FF_TILE), lambda i, j: (j + FF_NJ, 0, 0)),
                  pl.BlockSpec((3, FF_TILE), lambda i, j: (0, j)),
                  pl.BlockSpec((1, FF_TILE), lambda i, j: (0, j))],
        out_specs=(pl.BlockSpec((3, tm, FF_TILE), lambda i, j: (0, i, j)),
                   pl.BlockSpec((tm, FF_TILE), lambda i, j: (i, j))),
        scratch=[], sem=("arbitrary", "arbitrary"),
        args=(x1b, x1b, wup_st, wup_st, convw, convb))
    return res[0], res[1], res[2:]


def _out_fwd_bwd(act, x1b, r1, p2, tgt, wd, wpg, wpp, g1, b1, g2, b2, tm):
    t = r1.shape[0]

    def body(act_ref, x1b_ref, r1_ref, p_ref, tgt_ref, wd_ref, wpg_ref, wpp_ref,
             g1_ref, b1_ref, g2_ref, b2_ref,
             dr2_ref, dpg_ref, dpp_ref, loss_ref, dg2_ref, db2_ref):
        i = pl.program_id(0)

        @pl.when(i == 0)
        def _():
            loss_ref[...] = jnp.zeros_like(loss_ref)
            dg2_ref[...] = jnp.zeros_like(dg2_ref)
            db2_ref[...] = jnp.zeros_like(db2_ref)

        ffn = _dot(act_ref[...], wd_ref[...])
        pg = _dot(x1b_ref[...], wpg_ref[...])
        pp = _dot(p_ref[...], wpp_ref[...])
        s = _sig(pg)
        xh1, _ = _ln_stats(r1_ref[...])
        x1 = xh1 * g1_ref[...] + b1_ref[...]
        r2 = ALPHA * x1 + ffn + s * pp
        xh2, rstd2 = _ln_stats(r2)
        g2v = g2_ref[...]
        diff = xh2 * g2v + b2_ref[...] - tgt_ref[...]
        part = jnp.sum(jnp.sum(diff * diff, axis=1, keepdims=True), axis=0, keepdims=True)
        loss_ref[...] += jnp.broadcast_to(part * (0.5 / D_MODEL), loss_ref.shape)
        dy = diff * (1.0 / D_MODEL)
        dg2_ref[...] += _colsum8(dy * xh2)
        db2_ref[...] += _colsum8(dy)
        dr2 = _ln_bwd(dy * g2v, xh2, rstd2)
        dr2_ref[...] = dr2.astype(BF16)
        dpg_ref[...] = (dr2 * pp * s * (1.0 - s)).astype(BF16)
        dpp_ref[...] = (dr2 * s).astype(BF16)

    tile = pl.BlockSpec((tm, D_MODEL), lambda i: (i, 0))
    vec = pl.BlockSpec((1, D_MODEL), lambda i: (0, 0))
    acc8 = pl.BlockSpec((8, D_MODEL), lambda i: (0, 0))
    acc_shape = jax.ShapeDtypeStruct((8, D_MODEL), F32)
    return _pc(body, name="out_fwd_bwd",
               out_shape=(jax.ShapeDtypeStruct((t, D_MODEL), BF16),
                          jax.ShapeDtypeStruct((t, D_MODEL), BF16),
                          jax.ShapeDtypeStruct((t, D_MODEL), BF16),
                          acc_shape, acc_shape, acc_shape),
               grid=(t // tm,),
               in_specs=[pl.BlockSpec((tm, D_FF), lambda i: (i, 0)), tile, tile,
                         pl.BlockSpec((tm, PLE_DIM), lambda i: (i, 0)), tile,
                         pl.BlockSpec((D_FF, D_MODEL), lambda i: (0, 0)),
                         pl.BlockSpec((D_MODEL, D_MODEL), lambda i: (0, 0)),
                         pl.BlockSpec((PLE_DIM, D_MODEL), lambda i: (0, 0)),
                         vec, vec, vec, vec],
               out_specs=(tile, tile, tile, acc8, acc8, acc8),
               sem=("arbitrary",))(act, x1b, r1, p2, tgt, wd, wpg, wpp, g1, b1, g2, b2)


def _ffn_bwd(h2, dr2, wd, wup_st, dpg, wpg, r1, g1, convw, tm):
    t = r1.shape[0]
    ni = t // tm
    nth = tm // HALO
    last_halo = t // HALO - 1
    main_rows = slice(0, tm)

    def body(g_ref, gc_ref, gcn_ref, v_ref, vn_ref, dr2_ref, dr2n_ref, wd_ref, wug_ref, wuv_ref,
             cw_ref, dpg_ref, wpg_ref, r1_ref, g1_ref,
             dh2_ref, dr1_ref, dcw_ref, dcb_ref, dg1_ref, db1_ref, acc):
        i = pl.program_id(0)
        j = pl.program_id(1)

        @pl.when((i == 0) & (j == 0))
        def _():
            dcw_ref[...] = jnp.zeros_like(dcw_ref)
            dcb_ref[...] = jnp.zeros_like(dcb_ref)
            dg1_ref[...] = jnp.zeros_like(dg1_ref)
            db1_ref[...] = jnp.zeros_like(db1_ref)

        dr2v = dr2_ref[...].astype(BF16)
        dr2n = dr2n_ref[...].astype(BF16)
        more = (i < ni - 1).astype(F32)
        prod = None
        dcw_parts, dcb_parts = [], []
        for c0, c1 in FF_PIECES:
            pc = slice(c0, c1)
            da = _dot(dr2v, wd_ref[pc, :], NT)
            dnext = _dot(dr2n, wd_ref[pc, :], NT) * more
            gc = jnp.concatenate([gc_ref[:, pc].astype(F32), gcn_ref[:, pc].astype(F32)], axis=0)
            vext = jnp.concatenate([v_ref[:, pc].astype(F32), vn_ref[:, pc].astype(F32)], axis=0)
            dext = jnp.concatenate([da, dnext], axis=0)
            gl, dgl = _gelu_and_grad(gc)
            d_gc = dext * vext * dgl
            up1 = _shift_up(d_gc, 1)[main_rows, :]
            up2 = _shift_up(d_gc, 2)[main_rows, :]
            dm = d_gc[main_rows, :]
            d_gate = (cw_ref[2:3, pc] * dm + cw_ref[1:2, pc] * up1 + cw_ref[0:1, pc] * up2).astype(BF16)
            d_val = (da * gl[main_rows, :]).astype(BF16)
            dh2_ref[0, :, pc] = d_gate
            dh2_ref[1, :, pc] = d_val
            g = g_ref[:, pc].astype(F32)
            s0 = jnp.sum(g * up2, axis=0, keepdims=True)
            s1 = jnp.sum(g * up1, axis=0, keepdims=True)
            s2 = jnp.sum(g * dm, axis=0, keepdims=True)
            rowid = lax.broadcasted_iota(jnp.int32, (8, c1 - c0), 0)
            dcw_parts.append(jnp.where(rowid == 0, s0, jnp.where(rowid == 1, s1,
                                                                 jnp.where(rowid == 2, s2, 0.0))))
            dcb_parts.append(_colsum8(dm))
            part = _dot(d_gate, wug_ref[:, pc], NT) + _dot(d_val, wuv_ref[:, pc], NT)
            prod = part if prod is None else prod + part
        dcw_part = jnp.concatenate(dcw_parts, axis=1)
        dcb_part = jnp.concatenate(dcb_parts, axis=1)
        for jj in range(FF_NJ):
            @pl.when(j == jj)
            def _(jj=jj):
                cols = slice(jj * FF_TILE, (jj + 1) * FF_TILE)
                dcw_ref[:, cols] += dcw_part
                dcb_ref[:, cols] += dcb_part

        @pl.when(j == 0)
        def _():
            acc[...] = prod

        @pl.when(j > 0)
        def _():
            acc[...] += prod

        @pl.when(j == FF_NJ - 1)
        def _():
            d_x1 = acc[...] + _dot(dpg_ref[...], wpg_ref[...], NT) + ALPHA * dr2_ref[...].astype(F32)
            xh, rstd = _ln_stats(r1_ref[...])
            dg1_ref[...] += _colsum8(d_x1 * xh)
            db1_ref[...] += _colsum8(d_x1)
            dr1_ref[...] = _ln_bwd(d_x1 * g1_ref[...], xh, rstd).astype(BF16)

    def h2_main(part):
        return pl.BlockSpec((None, tm, FF_TILE), lambda i, j: (part, i, j))

    def h2_next(part):
        return pl.BlockSpec((None, HALO, FF_TILE),
                            lambda i, j: (part, jnp.minimum((i + 1) * nth, last_halo), j))

    tile = pl.BlockSpec((tm, D_MODEL), lambda i, j: (i, 0))
    acc8 = pl.BlockSpec((8, D_MODEL), lambda i, j: (0, 0))
    accff = pl.BlockSpec((8, D_FF), lambda i, j: (0, 0))
    acc_shape = jax.ShapeDtypeStruct((8, D_MODEL), F32)
    accff_shape = jax.ShapeDtypeStruct((8, D_FF), F32)
    return _pc(body, name="ffn_bwd",
               out_shape=(jax.ShapeDtypeStruct((2, t, D_FF), BF16),
                          jax.ShapeDtypeStruct((t, D_MODEL), BF16),
                          accff_shape, accff_shape, acc_shape, acc_shape),
               grid=(ni, FF_NJ),
               in_specs=[h2_main(0), h2_main(2), h2_next(2), h2_main(1), h2_next(1),
                         tile,
                         pl.BlockSpec((HALO, D_MODEL), lambda i, j: (jnp.minimum((i + 1) * nth, last_halo), 0)),
                         pl.BlockSpec((FF_TILE, D_MODEL), lambda i, j: (j, 0)),
                         pl.BlockSpec((None, D_MODEL, FF_TILE), lambda i, j: (j, 0, 0)),
                         pl.BlockSpec((None, D_MODEL, FF_TILE), lambda i, j: (j + FF_NJ, 0, 0)),
                         pl.BlockSpec((3, FF_TILE), lambda i, j: (0, j)),
                         tile, pl.BlockSpec((D_MODEL, D_MODEL), lambda i, j: (0, 0)),
                         tile, pl.BlockSpec((1, D_MODEL), lambda i, j: (0, 0))],
               out_specs=(pl.BlockSpec((2, tm, FF_TILE), lambda i, j: (0, i, j)),
                          tile, accff, accff, acc8, acc8),
               scratch=[pltpu.VMEM((tm, D_MODEL), F32)],
               sem=("arbitrary", "arbitrary"))(h2, h2, h2, h2, h2, dr2, dr2, wd, wup_st, wup_st,
                                               convw, dpg, wpg, r1, g1)


ANY = pl.BlockSpec(memory_space=pl.ANY)


def _chip_peers():
    x, y, c = lax.axis_index("x"), lax.axis_index("y"), lax.axis_index("c")
    return x, y, c, [(1 - x, y), (x, 1 - y), (1 - x, 1 - y)]


def _gather_comm(halved, whole=(), blocks=None):
    n, nw = len(halved), len(whole)
    blocks = blocks or {}

    def at(ti, ref, chip, *rest):
        return ref.at[(chip, blocks[ti][1]) + rest] if ti in blocks else ref.at[(chip,) + rest]

    def copies(ins, outs, sems):
        ici_send, ici_recv, d2d_send, d2d_recv, own_send, own_recv = sems
        x, y, c, peers = _chip_peers()
        me = 2 * x + y
        sibling = (x, y, 1 - c)
        own, ici, ici_wait, fwd, fwd_wait = [], [], [], [], []
        for ti in range(n + nw):
            src, dst = ins[ti], outs[ti]
            own.append(pltpu.make_async_remote_copy(
                src_ref=src, dst_ref=at(ti, dst, me), send_sem=own_send.at[ti], recv_sem=own_recv.at[ti],
                device_id=sibling, device_id_type=MESH))
            for k, (px, py) in enumerate(peers):
                pk = 2 * px + py
                sem = dict(send_sem=ici_send.at[ti * 3 + k], recv_sem=ici_recv.at[ti * 3 + k],
                           device_id=(px, py, c), device_id_type=MESH)
                if ti < n:
                    ici.append(pltpu.make_async_remote_copy(src_ref=src.at[c], dst_ref=at(ti, dst, me, c), **sem))
                    ici_wait.append(pltpu.make_async_remote_copy(src_ref=src.at[c], dst_ref=at(ti, dst, pk, c),
                                                                 **sem))
                    dsem = dict(send_sem=d2d_send.at[ti * 3 + k], recv_sem=d2d_recv.at[ti * 3 + k],
                                device_id=sibling, device_id_type=MESH)
                    fwd.append(pltpu.make_async_remote_copy(src_ref=at(ti, dst, pk, c), dst_ref=at(ti, dst, pk, c),
                                                            **dsem))
                    fwd_wait.append(pltpu.make_async_remote_copy(
                        src_ref=at(ti, dst, pk, 1 - c), dst_ref=at(ti, dst, pk, 1 - c), **dsem))
                else:
                    ici.append(pltpu.make_async_remote_copy(src_ref=src, dst_ref=dst.at[me], **sem))
                    ici_wait.append(pltpu.make_async_remote_copy(src_ref=src, dst_ref=dst.at[pk], **sem))
        return own, ici, ici_wait, fwd, fwd_wait

    def start(ins, outs, sems):
        own, ici, _, _, _ = copies(ins, outs, sems)
        for cp in own + ici:
            cp.start()

    def finish(ins, outs, sems):
        own, ici, ici_wait, fwd, fwd_wait = copies(ins, outs, sems)
        for i, cp in enumerate(ici_wait):
            cp.wait_recv()
            if i < len(fwd):
                fwd[i].start()
        for cp in fwd_wait + own:
            cp.wait_recv()
        for cp in own + ici + fwd:
            cp.wait_send()

    def middle(ins, outs, sems):
        _, _, ici_wait, fwd, _ = copies(ins, outs, sems)
        for i, cp in enumerate(ici_wait):
            cp.wait_recv()
            if i < len(fwd):
                fwd[i].start()

    def finish_late(ins, outs, sems):
        own, ici, _, fwd, fwd_wait = copies(ins, outs, sems)
        for cp in fwd_wait + own:
            cp.wait_recv()
        for cp in own + ici + fwd:
            cp.wait_send()

    srcs = list(halved) + list(whole)
    shapes = [jax.ShapeDtypeStruct((N_CHIP,) + ((blocks[ti][0],) if ti in blocks else ()) + s.shape, s.dtype)
              for ti, s in enumerate(srcs)]
    buffers = [(ti, blk[2]) for ti, blk in sorted(blocks.items()) if blk[2] is not None]
    aliases = {len(srcs) + bi: ti for bi, (ti, _) in enumerate(buffers)}
    return _Comm(srcs + [buf for _, buf in buffers], shapes,
                 [pltpu.SemaphoreType.DMA((3 * (n + nw),)), pltpu.SemaphoreType.DMA((3 * (n + nw),)),
                  pltpu.SemaphoreType.DMA((max(3 * n, 1),)), pltpu.SemaphoreType.DMA((max(3 * n, 1),)),
                  pltpu.SemaphoreType.DMA((n + nw,)), pltpu.SemaphoreType.DMA((n + nw,))],
                 start, finish, middle, finish_late, aliases)


def _sibling_exchange_comm(grads):
    n = len(grads)

    def copies(ins, outs, sems):
        send_sems, recv_sems = sems
        x, y, c = lax.axis_index("x"), lax.axis_index("y"), lax.axis_index("c")
        res = []
        for ti in range(n):
            half = ins[ti].shape[1] // 2
            res.append(pltpu.make_async_remote_copy(
                src_ref=ins[ti].at[:, pl.ds(pl.multiple_of((1 - c) * half, 16), half), :],
                dst_ref=outs[ti],
                send_sem=send_sems.at[ti], recv_sem=recv_sems.at[ti],
                device_id=(x, y, 1 - c), device_id_type=MESH))
        return res

    def start(ins, outs, sems):
        for cp in copies(ins, outs, sems):
            cp.start()

    def finish(ins, outs, sems):
        for cp in copies(ins, outs, sems):
            cp.wait()

    return _Comm(grads, [jax.ShapeDtypeStruct((N_CHIP, g.shape[1] // 2, g.shape[2]), g.dtype) for g in grads],
                 [pltpu.SemaphoreType.DMA((n,)), pltpu.SemaphoreType.DMA((n,))], start, finish)


def _in_proj_gathering(x2b, own, chip, tm, comm):
    t = x2b.shape[0]
    ni = t // tm
    half, cols = own.shape[1], own.shape[2]
    nci, nco = len(comm.ins), len(comm.out_shapes)

    def body(chip_ref, x_ref, own_ref, own_hbm, *rest):
        c_in = rest[:nci]
        h_ref, win_out = rest[nci:nci + 2]
        c_out = rest[nci + 2:nci + 2 + nco]
        w_scr, ici_send, ici_recv, d2d_send, d2d_recv, own_sems, ld_sems = rest[nci + 2 + nco:nci + 9 + nco]
        c_sem = rest[nci + 9 + nco:]
        s, i = pl.program_id(0), pl.program_id(1)
        x, y, c, peers = _chip_peers()
        me = 2 * x + y
        sibling = (x, y, 1 - c)

        def ici(k, slot):
            px, py = peers[k]
            return pltpu.make_async_remote_copy(
                src_ref=own_hbm.at[c], dst_ref=win_out.at[slot, c],
                send_sem=ici_send.at[k], recv_sem=ici_recv.at[k],
                device_id=(px, py, c), device_id_type=MESH)

        def forward(k, core):
            pk = 2 * peers[k][0] + peers[k][1]
            return pltpu.make_async_remote_copy(
                src_ref=win_out.at[pk, core], dst_ref=win_out.at[pk, core],
                send_sem=d2d_send.at[k], recv_sem=d2d_recv.at[k],
                device_id=sibling, device_id_type=MESH)

        place_own = pltpu.make_async_remote_copy(
            src_ref=own_hbm, dst_ref=win_out.at[me], send_sem=own_sems.at[0], recv_sem=own_sems.at[1],
            device_id=sibling, device_id_type=MESH)

        @pl.when((s == 0) & (i == 0))
        def _():
            for k in range(2):
                ici(k, me).start()
            place_own.start()

        @pl.when(s == 0)
        def _():
            xv = x_ref[...]
            h_ref[...] = (_dot(xv[:, :half], own_ref[0]) + _dot(xv[:, half:], own_ref[1])).astype(BF16)

        for k in range(3):
            @pl.when((s == k + 1) & (i == 0))
            def _(k=k):
                pk = 2 * peers[k][0] + peers[k][1]
                ici(k, pk).wait_recv()
                if k == 0:
                    ici(2, me).start()
                forward(k, c).start()
                forward(k, 1 - c).wait_recv()
                loads = [pltpu.make_async_copy(win_out.at[pk, hh], w_scr.at[hh], ld_sems.at[hh])
                         for hh in range(2)]
                for ld in loads:
                    ld.start()
                for ld in loads:
                    ld.wait()
                if k == 1:
                    comm.start(c_in, c_out, c_sem)

        @pl.when(s > 0)
        def _():
            xv = x_ref[...]
            h_ref[...] = (_dot(xv[:, :half], w_scr[0]) + _dot(xv[:, half:], w_scr[1])).astype(BF16)

        @pl.when((s == N_CHIP - 1) & (i == ni - 1))
        def _():
            place_own.wait()
            for k in range(3):
                ici(k, me).wait_send()
                forward(k, c).wait_send()
            comm.finish(c_in, c_out, c_sem)

    def shard_col(s, me):
        return jnp.where(s == 0, me, me ^ jnp.where(s == 1, 2, jnp.where(s == 2, 1, 3)))

    res = _pc(body, name="in_proj",
              out_shape=(jax.ShapeDtypeStruct((t, N_CHIP * cols), BF16),
                         jax.ShapeDtypeStruct((N_CHIP,) + own.shape, own.dtype)) + tuple(comm.out_shapes),
              grid=(N_CHIP, ni), nsp=1,
              in_specs=[pl.BlockSpec((tm, 2 * half), lambda s, i, chip_ref: (i, 0)),
                        pl.BlockSpec(own.shape, lambda s, i, chip_ref: (0, 0, 0)),
                        ANY] + [ANY] * nci,
              out_specs=(pl.BlockSpec((tm, cols), lambda s, i, chip_ref: (i, shard_col(s, chip_ref[0]))),
                         ANY) + tuple([ANY] * nco),
              scratch=[pltpu.VMEM(own.shape, own.dtype),
                       pltpu.SemaphoreType.DMA((3,)), pltpu.SemaphoreType.DMA((3,)),
                       pltpu.SemaphoreType.DMA((3,)), pltpu.SemaphoreType.DMA((3,)),
                       pltpu.SemaphoreType.DMA((2,)), pltpu.SemaphoreType.DMA((2,))] + comm.sems,
              sem=("arbitrary", "arbitrary"))(chip, x2b, own, own, *comm.ins)
    return res[0], res[1], res[2:]


def _rs_add_halves(name, grad, recv, core):
    _, r, cdim = grad.shape
    half = r // 2
    tr = _row_tile(half, cdim, mult=16)
    nr = half // tr

    def body(c_ref, g_ref, r_ref, o_ref):
        o_ref[...] = (g_ref[...].astype(F32) + r_ref[...].astype(F32)).astype(BF16)

    return _pc(body, name=name, out_shape=jax.ShapeDtypeStruct((N_CHIP, half, cdim), BF16),
               grid=(N_CHIP, nr), nsp=1,
               in_specs=[pl.BlockSpec((None, tr, cdim), lambda j, i, c_ref: (j, c_ref[0] * nr + i, 0)),
                         pl.BlockSpec((None, tr, cdim), lambda j, i, c_ref: (j, i, 0))],
               out_specs=pl.BlockSpec((None, tr, cdim), lambda j, i, c_ref: (j, i, 0)),
               sem=("parallel", "parallel"))(core, grad, recv)


def _chip_exchange_comm(parts):
    n = len(parts)

    def copies(ins, outs, sems):
        send_sems, recv_sems = sems
        x, y, c, peers = _chip_peers()
        return [pltpu.make_async_remote_copy(
            src_ref=ins[ti].at[2 * px + py], dst_ref=outs[ti].at[k],
            send_sem=send_sems.at[ti * 3 + k], recv_sem=recv_sems.at[ti * 3 + k],
            device_id=(px, py, c), device_id_type=MESH)
            for ti in range(n) for k, (px, py) in enumerate(peers)]

    def start(ins, outs, sems):
        for cp in copies(ins, outs, sems):
            cp.start()

    def finish(ins, outs, sems):
        for cp in copies(ins, outs, sems):
            cp.wait()

    return _Comm(parts, [jax.ShapeDtypeStruct((3,) + p.shape[1:], p.dtype) for p in parts],
                 [pltpu.SemaphoreType.DMA((3 * n,)), pltpu.SemaphoreType.DMA((3 * n,))], start, finish)


def _rs_sum_chips(name, part, recv, chip):
    _, half, cdim = recv.shape
    tr = _row_tile(half, cdim, mult=16)

    def body(chip_ref, p_ref, r_ref, o_ref):
        o_ref[...] = ((p_ref[...].astype(F32) + r_ref[0].astype(F32)) + r_ref[1].astype(F32)
                      ) + r_ref[2].astype(F32)

    return _pc(body, name=name, out_shape=jax.ShapeDtypeStruct((half, cdim), F32),
               grid=(half // tr,), nsp=1,
               in_specs=[pl.BlockSpec((None, tr, cdim), lambda i, chip_ref: (chip_ref[0], i, 0)),
                         pl.BlockSpec((3, tr, cdim), lambda i, chip_ref: (0, i, 0))],
               out_specs=pl.BlockSpec((tr, cdim), lambda i, chip_ref: (i, 0)),
               sem=("parallel",))(chip, part, recv)


def _rs_send_halves(halves):
    n = len(halves)

    def body(*refs):
        ins, outs = refs[:n], refs[n:2 * n]
        send_sems, recv_sems = refs[2 * n:]
        x, y, c = lax.axis_index("x"), lax.axis_index("y"), lax.axis_index("c")
        sends = []
        for ti in range(n):
            cp = pltpu.make_async_remote_copy(
                src_ref=ins[ti], dst_ref=outs[ti],
                send_sem=send_sems.at[ti], recv_sem=recv_sems.at[ti],
                device_id=(x, y, 1 - c), device_id_type=MESH)
            cp.start()
            sends.append(cp)
        for cp in sends:
            cp.wait()

    return _pc(body, name="rs_send_halves",
               out_shape=tuple(jax.ShapeDtypeStruct(hv.shape, hv.dtype) for hv in halves),
               in_specs=[ANY] * n, out_specs=tuple([ANY] * n),
               scratch=[pltpu.SemaphoreType.DMA((n,)), pltpu.SemaphoreType.DMA((n,))])(*halves)


def _adamw_rows(name, mine, theirs, w, m, v, core):
    half, cdim = mine.shape
    tr = _row_tile(half, cdim, budget=1 << 19)
    nrh = half // tr

    def body(c_ref, mine_ref, theirs_ref, w_ref, m_ref, v_ref, g_ref, d_ref, m2_ref, v2_ref):
        is_mine = (pl.program_id(0) // nrh) == c_ref[0]
        g = jnp.where(is_mine, mine_ref[...], theirs_ref[...])
        d, m2, v2 = _adamw(w_ref[...], g, m_ref[...], v_ref[...])
        g_ref[...] = g
        d_ref[...] = d
        m2_ref[...] = m2
        v2_ref[...] = v2

    htile = pl.BlockSpec((tr, cdim), lambda i, c_ref: (i % nrh, 0))
    tile = pl.BlockSpec((tr, cdim), lambda i, c_ref: (i, 0))
    shp = jax.ShapeDtypeStruct((2 * half, cdim), F32)
    return _pc(body, name=name, out_shape=(shp, shp, shp, shp), grid=(2 * nrh,), nsp=1,
               in_specs=[htile, htile, tile, tile, tile], out_specs=(tile, tile, tile, tile),
               sem=("parallel",))(core, mine, theirs, w, m, v)


def _adamw_whole(name, g, w, m, v):
    def body(g_ref, w_ref, m_ref, v_ref, d_ref, m2_ref, v2_ref):
        d, m2, v2 = _adamw(w_ref[...], g_ref[...], m_ref[...], v_ref[...])
        d_ref[...] = d
        m2_ref[...] = m2
        v2_ref[...] = v2

    shp = jax.ShapeDtypeStruct(g.shape, F32)
    return _pc(body, name=name, out_shape=(shp, shp, shp))(g, w, m, v)


SMALL_LAYOUT = (
    ("sgu_w_s", 1024, 1, 0),
    ("sgu_b_s", 8, 1, 1024),
    ("sgu_norm_g", 1, 0, 0),
    ("sgu_norm_b", 1, 0, 1),
    ("hgrn_norm_g", 1, 0, 3),
    ("ln1_g", 1, 0, 4),
    ("ln1_b", 1, 0, 5),
    ("ffn_conv_b", 1, 2, 3),
    ("ln2_g", 1, 0, 6),
    ("ln2_b", 1, 0, 7),
)
LB_ROW = 2
LOSS_ROW = 8
PACK_SHAPES = ((16, D_MODEL), (N_GROUP * 128 + 16, 128), (8, D_FF))
GATH_DTYPES = (F32, BF16, F32)


def _small_allreduce_adamw(rows1024, dws, dbs, dcw, dcb, logits, m_logits, v_logits,
                           small_w, small_m, small_v):
    ns = len(SMALL_LAYOUT)
    nr = len(rows1024)
    nb = len(PACK_SHAPES)

    def body(*refs):
        row_refs = refs[:nr]
        dws_ref, dbs_ref, dcw_ref, dcb_ref = refs[nr:nr + 4]
        pos = nr + 4
        tot = refs[pos:pos + nb]
        pos += nb
        pack = refs[pos:pos + nb]
        sib = refs[pos + nb:pos + 2 * nb]
        gath = refs[pos + 2 * nb:pos + 3 * nb]
        d2d_send, d2d_recv, ici_send, ici_recv = refs[pos + 3 * nb:]

        x, y, c, peers = _chip_peers()
        me = 2 * x + y
        sibling = (x, y, 1 - c)

        pack[0][...] = jnp.zeros(PACK_SHAPES[0], F32)
        for k in range(nr):
            pack[0][k:k + 1, :] = row_refs[k][0:1, :]
        pack[1][0:N_GROUP * 128, :] = dws_ref[...]
        pack[1][N_GROUP * 128:N_GROUP * 128 + 8, :] = dbs_ref[...]
        pack[1][N_GROUP * 128 + 8:, :] = jnp.zeros((8, 128), F32)
        pack[2][...] = jnp.zeros(PACK_SHAPES[2], F32)
        pack[2][0:3, :] = dcw_ref[0:3, :]
        pack[2][3:4, :] = dcb_ref[0:1, :]

        d2d = [pltpu.make_async_remote_copy(
            src_ref=pack[b], dst_ref=sib[b], send_sem=d2d_send.at[b], recv_sem=d2d_recv.at[b],
            device_id=sibling, device_id_type=MESH) for b in range(nb)]
        for cp in d2d:
            cp.start()
        for cp in d2d:
            cp.wait()
        for b in range(nb):
            gath[b][me] = (pack[b][...] + sib[b][...]).astype(GATH_DTYPES[b])

        ici, ici_wait = [], []
        for b in range(nb):
            for k, (px, py) in enumerate(peers):
                sem = dict(send_sem=ici_send.at[b * 3 + k], recv_sem=ici_recv.at[b * 3 + k],
                           device_id=(px, py, c), device_id_type=MESH)
                ici.append(pltpu.make_async_remote_copy(src_ref=gath[b].at[me], dst_ref=gath[b].at[me], **sem))
                ici_wait.append(pltpu.make_async_remote_copy(
                    src_ref=gath[b].at[me], dst_ref=gath[b].at[2 * px + py], **sem))
        for cp in ici:
            cp.start()
        for cp in ici_wait:
            cp.wait_recv()
        for cp in ici:
            cp.wait_send()

        for b in range(nb):
            tot[b][...] = ((gath[b][0].astype(F32) + gath[b][1].astype(F32)) + gath[b][2].astype(F32)
                           ) + gath[b][3].astype(F32)

    def update(*refs):
        tot = refs[:nb]
        lg_ref, mlg_ref, vlg_ref = refs[nb:nb + 3]
        pos = nb + 3
        w_refs = refs[pos:pos + ns]
        m_refs = refs[pos + ns:pos + 2 * ns]
        v_refs = refs[pos + 2 * ns:pos + 3 * ns]
        pos += 3 * ns
        loss_ref = refs[pos]
        lg_outs = refs[pos + 1:pos + 5]
        outs = refs[pos + 5:pos + 5 + 4 * ns]

        loss_ref[...] = tot[0][LOSS_ROW:LOSS_ROW + 1, :]
        lb = _sig(lg_ref[0:1, :] - lg_ref[1:2, :])
        d0 = tot[0][LB_ROW:LB_ROW + 1, :] * lb * (1.0 - lb)
        rowid = lax.broadcasted_iota(jnp.int32, (2, D_MODEL), 0)
        g_lg = jnp.where(rowid == 0, d0, -d0)
        dl, ml, vl = _adamw(lg_ref[...], g_lg, mlg_ref[...], vlg_ref[...])
        lg_outs[0][...] = g_lg
        lg_outs[1][...] = dl
        lg_outs[2][...] = ml
        lg_outs[3][...] = vl
        for si, (_, rows, b, r0) in enumerate(SMALL_LAYOUT):
            g = tot[b][r0:r0 + rows, :]
            dl, ml, vl = _adamw(w_refs[si][...], g, m_refs[si][...], v_refs[si][...])
            outs[4 * si][...] = g
            outs[4 * si + 1][...] = dl
            outs[4 * si + 2][...] = ml
            outs[4 * si + 3][...] = vl

    scratch = [pltpu.VMEM(shp, F32) for shp in PACK_SHAPES]
    scratch += [pltpu.VMEM(shp, F32) for shp in PACK_SHAPES]
    scratch += [pltpu.VMEM((N_CHIP,) + shp, dt) for shp, dt in zip(PACK_SHAPES, GATH_DTYPES)]
    scratch += [pltpu.SemaphoreType.DMA((nb,)), pltpu.SemaphoreType.DMA((nb,)),
                pltpu.SemaphoreType.DMA((3 * nb,)), pltpu.SemaphoreType.DMA((3 * nb,))]
    vm = pl.BlockSpec(memory_space=pltpu.VMEM)
    tots = _pc(body, name="small_allreduce",
               out_shape=tuple(jax.ShapeDtypeStruct(shp, F32) for shp in PACK_SHAPES),
               in_specs=[vm] * (nr + 4), out_specs=tuple([vm] * nb),
               scratch=scratch)(*rows1024, dws, dbs, dcw, dcb)

    shapes = [jax.ShapeDtypeStruct((1, D_MODEL), F32)]
    shapes += [jax.ShapeDtypeStruct((2, D_MODEL), F32)] * 4
    for w in small_w:
        shapes += [jax.ShapeDtypeStruct(w.shape, F32)] * 4
    res = _pc(update, name="small_adamw", out_shape=tuple(shapes),
              in_specs=[vm] * (nb + 3 + 3 * ns), out_specs=tuple([vm] * len(shapes)),
              )(*tots, logits, m_logits, v_logits, *small_w, *small_m, *small_v)
    return res[0], tots[2], res[1:5], res[5:]


def kernel(x, p, w_in, sgu_w_s, sgu_b_s, sgu_norm_g, sgu_norm_b, hgrn_lb_logits, hgrn_norm_g, w_branch, w_out, ln1_g, ln1_b, ffn_w_up, ffn_conv_w, ffn_conv_b, ffn_w_down, ln2_g, ln2_b, ple_w_proj, ple_w_gate, loss_target, m_w_in, m_sgu_w_s, m_sgu_b_s, m_sgu_norm_g, m_sgu_norm_b, m_hgrn_lb_logits, m_hgrn_norm_g, m_w_branch, m_w_out, m_ln1_g, m_ln1_b, m_ffn_w_up, m_ffn_conv_w, m_ffn_conv_b, m_ffn_w_down, m_ln2_g, m_ln2_b, m_ple_w_proj, m_ple_w_gate, v_w_in, v_sgu_w_s, v_sgu_b_s, v_sgu_norm_g, v_sgu_norm_b, v_hgrn_lb_logits, v_hgrn_norm_g, v_w_branch, v_w_out, v_ln1_g, v_ln1_b, v_ffn_w_up, v_ffn_conv_w, v_ffn_conv_b, v_ffn_w_down, v_ln2_g, v_ln2_b, v_ple_w_proj, v_ple_w_gate):
    t = x.shape[1]
    x2 = x.reshape(t, D_MODEL)
    x2b = x2.astype(BF16)
    p2 = p.reshape(t, PLE_DIM)
    tgt = loss_target.reshape(t, D_MODEL)
    core = lax.axis_index("c").astype(jnp.int32).reshape(1)
    chip_id = (2 * lax.axis_index("x") + lax.axis_index("y")).astype(jnp.int32).reshape(1)

    big_w = [w_in[0], w_branch[0, 0], w_branch[0, 1], w_out[0], ffn_w_up[0], ffn_w_down[0],
             ple_w_proj[0], ple_w_gate[0]]
    big_m = [m_w_in[0], m_w_branch[0, 0], m_w_branch[0, 1], m_w_out[0], m_ffn_w_up[0],
             m_ffn_w_down[0], m_ple_w_proj[0], m_ple_w_gate[0]]
    big_v = [v_w_in[0], v_w_branch[0, 0], v_w_branch[0, 1], v_w_out[0], v_ffn_w_up[0],
             v_ffn_w_down[0], v_ple_w_proj[0], v_ple_w_gate[0]]
    def halves_of(i):
        w = big_w[i]
        return w.astype(BF16).reshape(2, w.shape[0] // 2, w.shape[1])

    def stacked(g, i):
        return g.reshape(N_CHIP, big_w[i].shape[0], big_w[i].shape[1])


    cid = jnp.arange(SGU_BLOCK) // CHUNK
    maskf = (cid[:, None] >= cid[None, :]).astype(F32)
    ws_masked = sgu_w_s[0] * maskf[None]
    wm = ws_masked.astype(BF16)
    wmt = jnp.transpose(ws_masked, (0, 2, 1)).astype(BF16)
    bsb = jnp.broadcast_to(sgu_b_s[0][:, :, None], (N_GROUP, SGU_BLOCK, 128))

    up_rows = big_w[4].shape[0] // 2
    up_blocks = [big_w[4][k * up_rows:(k + 1) * up_rows].astype(BF16).reshape(2, up_rows // 2, -1)
                 for k in range(2)]
    h, win_g, (up_g,) = _in_proj_gathering(x2b, halves_of(0), chip_id, 512,
                                           _gather_comm([up_blocks[0]], blocks={0: (2, 0, None)}))
    win_st = stacked(win_g, 0)
    ya, _ = _sgu_fwd(h, wm, bsb, sgu_norm_g, sgu_norm_b)
    (yb, o_all, st_all), mix_g = _hgrn_fwd(
        h, hgrn_lb_logits, hgrn_norm_g,
        comm=_gather_comm([halves_of(i) for i in (1, 2, 3)] + [up_blocks[1]], [ffn_conv_w[0]],
                          blocks={3: (2, 1, up_g)}))
    wb0, wb1, wo = [stacked(g, i).reshape(D_MODEL, D_MODEL) for g, i in zip(mix_g[:3], (1, 2, 3))]
    wup_st = stacked(mix_g[3], 4)
    convw = jnp.transpose(mix_g[4], (1, 0, 2)).reshape(3, D_FF)
    (r1, a_br, b_br, m_bf, x1b), _ = _mix_fwd(ya, yb, h, x2, wb0, wb1, wo, ln1_g, ln1_b, 256)
    h2, act, out_g = _ffn_up_act(x1b, wup_st, convw, ffn_conv_b, 512,
                                 _gather_comm([halves_of(i) for i in (5, 6, 7)]))
    wd = stacked(out_g[0], 5).reshape(D_FF, D_MODEL)
    wpp = jnp.transpose(stacked(out_g[1], 6), (1, 0, 2)).reshape(PLE_DIM, D_MODEL)
    wpg = stacked(out_g[2], 7).reshape(D_MODEL, D_MODEL)
    dr2, dpg, dpp, loss_acc, dg2, db2 = _out_fwd_bwd(
        act, x1b, r1, p2, tgt, wd, wpg, wpp, ln1_g, ln1_b, ln2_g, ln2_b, 256)

    dh2, dr1, dcw, dcb, dg1, db1 = _ffn_bwd(h2, dr2, wd, wup_st, dpg, wpg, r1, ln1_g, convw, 256)
    d_wd = _mm_tn("ffn_down_wgrad", act, dr2, FF_TILE, 512)
    d_wpg = _mm_tn("ple_gate_wgrad", x1b, dpg, 512, D_MODEL)
    d_wpp_st = _mm_tn("ple_proj_wgrad", p2, dpp, PLE_DIM, PLE_DIM, stacked=True)
    d_wup_st = _mm("ffn_up_wgrad", x1b, dh2, TN, (2, N_CHIP),
                   pl.BlockSpec((t, 512), lambda i, j: (0, i)),
                   pl.BlockSpec((None, t, FF_TILE), lambda i, j: (j // FF_NJ, 0, j % FF_NJ)),
                   jax.ShapeDtypeStruct((N_CHIP, D_MODEL, FF_TILE), BF16),
                   pl.BlockSpec((None, 512, FF_TILE), lambda i, j: (j, i, 0)))
    da_bf, db_bf, dh, dya, dyb = _mix_bwd(dr1, h, a_br, b_br, wo, wb0, wb1, 256)
    d_wo = _mm_tn("out_proj_wgrad", m_bf, dr1, 512, 512)
    d_wb0 = _mm_tn("branch0_wgrad", ya, da_bf, 512, D_MODEL)
    d_wb1 = _mm_tn("branch1_wgrad", yb, db_bf, 512, D_MODEL)
    grads_1 = [d_wb0.reshape(4, 256, D_MODEL), d_wb1.reshape(4, 256, D_MODEL),
               d_wo.reshape(4, 256, D_MODEL), d_wup_st, d_wd.reshape(4, D_FF // 4, D_MODEL),
               d_wpp_st, d_wpg.reshape(4, 256, D_MODEL)]
    (dh, dws, dbs, dgv, dbv), recv_a1 = _sgu_bwd(h, dya, wm, wmt, bsb, sgu_norm_g, sgu_norm_b, maskf, dh,
                                                 comm=_sibling_exchange_comm(grads_1))
    parts_1 = [_rs_add_halves("rs_add_halves%d" % (i + 1), g, r, core)
               for i, (g, r) in enumerate(zip(grads_1, recv_a1))]
    (dh, dlb, dgn), recv_b1 = _hgrn_bwd(h, o_all, dyb, st_all, hgrn_lb_logits, hgrn_norm_g, dh,
                                         comm=_chip_exchange_comm(parts_1))

    grads_0 = [_in_proj_wgrad(x2b, dh, 512, D_MODEL)]
    recv_a0 = _run_comm("rs_sibling_exchange0", _sibling_exchange_comm(grads_0))
    parts_0 = [_rs_add_halves("rs_add_halves0", grads_0[0], recv_a0[0], core)]
    gx, recv_b0 = _in_proj_xgrad(dh, win_st, dr1, 512, _chip_exchange_comm(parts_0),
                                 r1.reshape(1, t, D_MODEL))
    parts = parts_0 + parts_1
    recv_b = list(recv_b0) + list(recv_b1)
    halves = [_rs_sum_chips("rs_sum_chips%d" % i, pt, r, chip_id)
              for i, (pt, r) in enumerate(zip(parts, recv_b))]
    theirs = _rs_send_halves(halves)
    big_out = [_adamw_rows("adamw_big%d" % i, halves[i], theirs[i], big_w[i], big_m[i], big_v[i], core)
               for i in range(len(halves))]

    small_in = dict(sgu_w_s=(sgu_w_s, m_sgu_w_s, v_sgu_w_s), sgu_b_s=(sgu_b_s, m_sgu_b_s, v_sgu_b_s),
                    sgu_norm_g=(sgu_norm_g, m_sgu_norm_g, v_sgu_norm_g),
                    sgu_norm_b=(sgu_norm_b, m_sgu_norm_b, v_sgu_norm_b),
                    hgrn_norm_g=(hgrn_norm_g, m_hgrn_norm_g, v_hgrn_norm_g),
                    ln1_g=(ln1_g, m_ln1_g, v_ln1_g), ln1_b=(ln1_b, m_ln1_b, v_ln1_b),
                    ffn_conv_b=(ffn_conv_b, m_ffn_conv_b, v_ffn_conv_b),
                    ln2_g=(ln2_g, m_ln2_g, v_ln2_g), ln2_b=(ln2_b, m_ln2_b, v_ln2_b))

    def flat(name, arr):
        rows = dict((n, r) for n, r, _, _ in SMALL_LAYOUT)[name]
        return arr.reshape(rows, arr.size // rows)

    names = [n for n, _, _, _ in SMALL_LAYOUT]
    sw = [flat(n, small_in[n][0]) for n in names]
    sm = [flat(n, small_in[n][1]) for n in names]
    sv = [flat(n, small_in[n][2]) for n in names]
    loss_rows, dcw_tot, lg_out, small_out = _small_allreduce_adamw(
        [dgv, dbv, dlb, dgn, dg1, db1, dg2, db2, loss_acc], dws.reshape(N_GROUP * 128, 128), dbs, dcw, dcb,
        hgrn_lb_logits, m_hgrn_lb_logits, v_hgrn_lb_logits, sw, sm, sv)
    loss = loss_rows[0, 0]

    chip = 2 * lax.axis_index("x") + lax.axis_index("y")
    g_cw = lax.dynamic_slice(dcw_tot, (0, chip * (D_FF // 4)), (3, D_FF // 4))
    cw_out = _adamw_whole("adamw_conv_w", g_cw, ffn_conv_w[0], m_ffn_conv_w[0], v_ffn_conv_w[0])

    res = {}
    for si, n in enumerate(names):
        shp = small_in[n][0].shape
        res[n] = tuple(small_out[4 * si + k].reshape(shp) for k in range(4))
    res["hgrn_lb_logits"] = tuple(lg_out)
    res["ffn_conv_w"] = (g_cw[None],) + tuple(o[None] for o in cw_out)

    def big(i):
        return tuple(big_out[i])

    res["w_in"] = tuple(o[None] for o in big(0))
    res["w_branch"] = tuple(jnp.stack([o0, o1])[None] for o0, o1 in zip(big(1), big(2)))
    res["w_out"] = tuple(o[None] for o in big(3))
    res["ffn_w_up"] = tuple(o[None] for o in big(4))
    res["ffn_w_down"] = tuple(o[None] for o in big(5))
    res["ple_w_proj"] = tuple(o[None] for o in big(6))
    res["ple_w_gate"] = tuple(o[None] for o in big(7))

    order = ["w_in", "sgu_w_s", "sgu_b_s", "sgu_norm_g", "sgu_norm_b", "hgrn_lb_logits",
             "hgrn_norm_g", "w_branch", "w_out", "ln1_g", "ln1_b", "ffn_w_up", "ffn_conv_w",
             "ffn_conv_b", "ffn_w_down", "ln2_g", "ln2_b", "ple_w_proj", "ple_w_gate"]
    outs = [loss, gx]
    for k in range(4):
        outs += [res[n][k] for n in order]
    return tuple(outs)
```

```python
import jax
import jax.numpy as jnp
from jax import lax
from jax.experimental import pallas as pl
from jax.experimental.pallas import tpu as pltpu

F32 = jnp.float32
BF16 = jnp.bfloat16
HIGHEST = lax.Precision.HIGHEST
MESH = pl.DeviceIdType.MESH

D_MODEL = 1024
CHUNK = 64
SGU_BLOCK = 128
SGU_STEP_BLOCKS = 4
SGU_ROWS = SGU_STEP_BLOCKS * SGU_BLOCK
N_GROUP = 8
N_HEAD = 8
HEAD_DIM = 128
D_FF = 2816
PLE_DIM = 256
LN_EPS = 1e-5
RMS_EPS = 1e-6
ALPHA = 2.0 ** 0.25
N_CHIP = 4

ADAM_LR = 0.001
ADAM_B1 = 0.9
ADAM_B2 = 0.999
ADAM_EPS = 1e-08
ADAM_WD = 0.01
ADAM_STEP = 10

VMEM_LIMIT = 56 * 1024 * 1024

NN = (((1,), (0,)), ((), ()))
NT = (((1,), (1,)), ((), ()))
TN = (((0,), (0,)), ((), ()))


def _pc(body, *, name, out_shape, grid=None, in_specs=None, out_specs=None, scratch=(),
        sem=None, nsp=0, vmem=VMEM_LIMIT, aliases=None):
    params = dict(vmem_limit_bytes=vmem)
    if sem is not None:
        params["dimension_semantics"] = sem
    kw = dict(name=name, out_shape=out_shape, compiler_params=pltpu.CompilerParams(**params))
    if aliases:
        kw["input_output_aliases"] = aliases
    if nsp:
        kw["grid_spec"] = pltpu.PrefetchScalarGridSpec(
            num_scalar_prefetch=nsp, grid=grid, in_specs=in_specs, out_specs=out_specs,
            scratch_shapes=list(scratch))
    else:
        if grid is not None:
            kw["grid"] = grid
        if in_specs is not None:
            kw["in_specs"] = in_specs
            kw["out_specs"] = out_specs
        kw["scratch_shapes"] = list(scratch)
    return pl.pallas_call(body, **kw)


def _dot(a, b, dims=NN):
    return lax.dot_general(a.astype(BF16), b.astype(BF16), dims, preferred_element_type=F32)


def _dot32(a, b, dims=NN):
    return lax.dot_general(a, b, dims, precision=HIGHEST, preferred_element_type=F32)


def _sig(x):
    return 1.0 / (1.0 + jnp.exp(-x))


_GC = 0.7978845608028654
_GA = 0.044715


def _gelu(x):
    return 0.5 * x * (1.0 + jnp.tanh(_GC * (x + _GA * x * x * x)))


def _gelu_and_grad(x):
    t = jnp.tanh(_GC * (x + _GA * x * x * x))
    g = 0.5 * x * (1.0 + t)
    dg = 0.5 * (1.0 + t) + 0.5 * x * (1.0 - t * t) * _GC * (1.0 + 3.0 * _GA * x * x)
    return g, dg


def _ln_stats(r):
    mu = jnp.mean(r, axis=-1, keepdims=True)
    xc = r - mu
    var = jnp.mean(xc * xc, axis=-1, keepdims=True)
    rstd = lax.rsqrt(var + LN_EPS)
    return xc * rstd, rstd


def _ln_bwd(dxh, xh, rstd):
    m1 = jnp.mean(dxh, axis=-1, keepdims=True)
    m2 = jnp.mean(dxh * xh, axis=-1, keepdims=True)
    return rstd * (dxh - m1 - xh * m2)


def _colsum8(v):
    return jnp.broadcast_to(jnp.sum(v, axis=0, keepdims=True), (8, v.shape[1]))


def _adamw(w, g, m, v):
    m2 = ADAM_B1 * m + (1.0 - ADAM_B1) * g
    v2 = ADAM_B2 * v + (1.0 - ADAM_B2) * (g * g)
    m_hat = m2 / (1.0 - ADAM_B1 ** ADAM_STEP)
    v_hat = v2 / (1.0 - ADAM_B2 ** ADAM_STEP)
    delta = -ADAM_LR * (m_hat / (jnp.sqrt(v_hat) + ADAM_EPS) + ADAM_WD * w)
    return delta, m2, v2


def _row_tile(rows, cols, itemsize=4, budget=1 << 20, mult=8):
    best = mult
    for tr in range(mult, rows + 1, mult):
        if rows % tr == 0 and tr * cols * itemsize <= budget:
            best = tr
    return best


def _mm(name, a, b, dims, grid, a_spec, b_spec, out_shape, o_spec):
    out_dtype = out_shape.dtype

    def body(a_ref, b_ref, o_ref):
        o_ref[...] = _dot(a_ref[...], b_ref[...], dims).astype(out_dtype)

    return _pc(body, name=name, out_shape=out_shape, grid=grid, in_specs=[a_spec, b_spec],
               out_specs=o_spec, sem=("parallel", "parallel"))(a, b)


class _Comm:
    def __init__(self, ins, out_shapes, sems, start, finish, middle=None, finish_late=None, aliases=None):
        self.ins, self.out_shapes, self.sems = list(ins), list(out_shapes), list(sems)
        self.start, self.finish = start, finish
        self.middle, self.finish_late = middle, finish_late
        self.aliases = aliases or {}


def _hosted_call(body, comm, first, last, *, name, out_shape, grid, in_specs, out_specs, scratch, sem,
                 args, aliases=None, mid=None):
    n_in, n_out, n_scr = len(in_specs), len(out_shape), len(scratch)
    nci, nco = len(comm.ins), len(comm.out_shapes)

    def wrapped(*refs):
        pos = n_in
        own_in, c_in = refs[:pos], refs[pos:pos + nci]
        pos += nci
        own_out, c_out = refs[pos:pos + n_out], refs[pos + n_out:pos + n_out + nco]
        pos += n_out + nco
        own_scr, c_sem = refs[pos:pos + n_scr], refs[pos + n_scr:]

        @pl.when(first())
        def _():
            comm.start(c_in, c_out, c_sem)

        body(*own_in, *own_out, *own_scr)

        if mid is not None:
            @pl.when(mid())
            def _():
                comm.middle(c_in, c_out, c_sem)

        @pl.when(last())
        def _():
            (comm.finish if mid is None else comm.finish_late)(c_in, c_out, c_sem)

    return _pc(wrapped, name=name, out_shape=tuple(out_shape) + tuple(comm.out_shapes), grid=grid,
               in_specs=list(in_specs) + [ANY] * nci, out_specs=tuple(out_specs) + tuple([ANY] * nco),
               scratch=list(scratch) + comm.sems, sem=sem,
               aliases={**(aliases or {}), **{n_in + ci: n_out + co for ci, co in comm.aliases.items()}},
               )(*args, *comm.ins)


def _grid1_call(body, comm, n, *, name, out_shape, in_specs, out_specs, scratch, args, aliases=None):
    if comm is None:
        return _pc(body, name=name, out_shape=out_shape, grid=(n,), in_specs=in_specs, out_specs=out_specs,
                   scratch=scratch, sem=("arbitrary",), aliases=aliases)(*args), ()
    res = _hosted_call(body, comm, lambda: pl.program_id(0) == 0, lambda: pl.program_id(0) == n - 1,
                       name=name, out_shape=out_shape, grid=(n,), in_specs=in_specs,
                       out_specs=out_specs, scratch=scratch, sem=("arbitrary",), args=args, aliases=aliases)
    return res[:len(out_shape)], res[len(out_shape):]


def _run_comm(name, comm):
    nci, nco = len(comm.ins), len(comm.out_shapes)

    def body(*refs):
        c_in, c_out, c_sem = refs[:nci], refs[nci:nci + nco], refs[nci + nco:]
        comm.start(c_in, c_out, c_sem)
        comm.finish(c_in, c_out, c_sem)

    return _pc(body, name=name, out_shape=tuple(comm.out_shapes), in_specs=[ANY] * nci,
               out_specs=tuple([ANY] * nco), scratch=comm.sems)(*comm.ins)


def _mm_tn(name, a, b, tm, tn, stacked=False):
    t, m = a.shape
    _, n = b.shape
    if stacked:
        assert tm == m
        out_shape = jax.ShapeDtypeStruct((n // tn, m, tn), BF16)
        o_spec = pl.BlockSpec((None, tm, tn), lambda i, j: (j, 0, 0))
    else:
        out_shape = jax.ShapeDtypeStruct((m, n), BF16)
        o_spec = pl.BlockSpec((tm, tn), lambda i, j: (i, j))
    return _mm(name, a, b, TN, (m // tm, n // tn),
               pl.BlockSpec((t, tm), lambda i, j: (0, i)),
               pl.BlockSpec((t, tn), lambda i, j: (0, j)),
               out_shape, o_spec)


DH_SLOT = (2, 0, 1, 3)


def _dh_slot(j):
    return jnp.where(j == 3, 3, (j + 2) % 3)


def _in_proj_wgrad(x2b, dh, tm, tn):
    t, m = x2b.shape
    n = dh.shape[2]

    def body(a_ref, b_ref, o_ref):
        o_ref[...] = _dot(a_ref[...], b_ref[...], TN).astype(BF16)

    return _pc(body, name="in_proj_wgrad", out_shape=jax.ShapeDtypeStruct((N_CHIP, m, n), BF16),
               grid=(N_CHIP, m // tm, n // tn),
               in_specs=[pl.BlockSpec((t, tm), lambda j, i, k: (0, i)),
                         pl.BlockSpec((None, t, tn), lambda j, i, k: (_dh_slot(j), 0, k))],
               out_specs=pl.BlockSpec((None, tm, tn), lambda j, i, k: (j, i, k)),
               sem=("parallel", "parallel", "parallel"))(x2b, dh)


def _in_proj_xgrad(dh, win_st, dr1, tm, comm, i0, ni, buf=None):
    t = dr1.shape[0]

    def body(a_ref, b_ref, add_ref, *rest):
        o_ref, acc = rest[-2:]
        j = pl.program_id(1)
        prod = _dot(a_ref[...], b_ref[...], NT)

        @pl.when(j == 0)
        def _():
            acc[...] = prod + ALPHA * add_ref[...].astype(F32)

        @pl.when((j > 0) & (j < N_CHIP - 1))
        def _():
            acc[...] += prod

        @pl.when(j == N_CHIP - 1)
        def _():
            o_ref[...] = acc[...] + prod

    tile = pl.BlockSpec((tm, D_MODEL), lambda i, j: (i + i0, 0))
    out_shape = (jax.ShapeDtypeStruct((1, t, D_MODEL), F32),)
    in_specs = [pl.BlockSpec((None, tm, 2 * D_MODEL), lambda i, j: (_dh_slot(j), i + i0, 0)),
                pl.BlockSpec((None, D_MODEL, 2 * D_MODEL), lambda i, j: (j, 0, 0)),
                tile]
    out_specs = (pl.BlockSpec((None, tm, D_MODEL), lambda i, j: (0, i + i0, 0)),)
    scratch = [pltpu.VMEM((tm, D_MODEL), F32)]
    if comm is None:
        res = _pc(body, name="in_proj_xgrad_tail", out_shape=out_shape, grid=(ni, N_CHIP),
                  in_specs=in_specs + [ANY], out_specs=out_specs, scratch=scratch,
                  sem=("arbitrary", "arbitrary"), aliases={3: 0})(dh, win_st, dr1, buf)
        return res[0], ()
    res = _hosted_call(body, comm,
                       lambda: (pl.program_id(0) == 0) & (pl.program_id(1) == 0),
                       lambda: (pl.program_id(0) == ni - 1) & (pl.program_id(1) == N_CHIP - 1),
                       name="in_proj_xgrad", out_shape=out_shape, grid=(ni, N_CHIP),
                       in_specs=in_specs, out_specs=out_specs, scratch=scratch,
                       sem=("arbitrary", "arbitrary"), args=[dh, win_st, dr1])
    return res[0], res[1:]


def _sgu_mixed(v, wm_ref, bsb_ref, gv, bv):
    gl, dgl = _gelu_and_grad(v)
    vh, rstd = _ln_stats(gl)
    vn = vh * gv + bv
    mixed = []
    for g in range(N_GROUP):
        sl = slice(g * 128, (g + 1) * 128)
        mixed.append(_dot(wm_ref[g], vn[:, sl]) + bsb_ref[g])
    return dgl, vh, rstd, vn, mixed


def _sgu_fwd(h, wm, bsb, gv, bv, comm=None):
    t = h.shape[0]

    def body(u_ref, v_ref, wm_ref, bsb_ref, gv_ref, bv_ref, ya_ref):
        for bb in range(SGU_STEP_BLOCKS):
            rows = slice(bb * SGU_BLOCK, (bb + 1) * SGU_BLOCK)
            u = u_ref[rows, :].astype(F32)
            _, _, _, _, mixed = _sgu_mixed(v_ref[rows, :].astype(F32), wm_ref, bsb_ref, gv_ref[...],
                                           bv_ref[...])
            gu = _gelu(u)
            for g in range(N_GROUP):
                sl = slice(g * 128, (g + 1) * 128)
                ya_ref[rows, sl] = (gu[:, sl] * mixed[g]).astype(BF16)

    full3 = pl.BlockSpec((N_GROUP, 128, 128), lambda i: (0, 0, 0))
    vec = pl.BlockSpec((1, D_MODEL), lambda i: (0, 0))
    (ya,), extra = _grid1_call(
        body, comm, t // SGU_ROWS, name="sgu_fwd",
        out_shape=(jax.ShapeDtypeStruct((t, D_MODEL), BF16),),
        in_specs=[pl.BlockSpec((SGU_ROWS, D_MODEL), lambda i: (i, 0)),
                  pl.BlockSpec((SGU_ROWS, D_MODEL), lambda i: (i, 1)),
                  full3, full3, vec, vec],
        out_specs=(pl.BlockSpec((SGU_ROWS, D_MODEL), lambda i: (i, 0)),),
        scratch=[], args=(h, h, wm, bsb, gv, bv))
    return ya, extra


def _sgu_bwd(h, dya, wm, wmt, bsb, gv, bv, maskf, dh_buf, comm=None):
    t = h.shape[0]
    nb = t // SGU_ROWS

    def body(u_ref, v_ref, dya_ref, wm_ref, wmt_ref, bsb_ref, gv_ref, bv_ref, mask_ref, dh_buf_ref,
             dh_ref, dws_ref, dbs_ref, dgv_ref, dbv_ref, dmix_acc):
        i = pl.program_id(0)

        @pl.when(i == 0)
        def _():
            dws_ref[...] = jnp.zeros_like(dws_ref)
            dgv_ref[...] = jnp.zeros_like(dgv_ref)
            dbv_ref[...] = jnp.zeros_like(dbv_ref)
            dmix_acc[...] = jnp.zeros_like(dmix_acc)

        gvv = gv_ref[...]
        for bb in range(SGU_STEP_BLOCKS):
            rows = slice(bb * SGU_BLOCK, (bb + 1) * SGU_BLOCK)
            u = u_ref[rows, :].astype(F32)
            dgl_v, vh, rstd, vn, mixed = _sgu_mixed(v_ref[rows, :].astype(F32), wm_ref, bsb_ref, gvv,
                                                    bv_ref[...])
            gu, dgl_u = _gelu_and_grad(u)
            dya_v = dya_ref[rows, :].astype(F32)
            dvn_parts = []
            for g in range(N_GROUP):
                sl = slice(g * 128, (g + 1) * 128)
                d_y = dya_v[:, sl]
                dh_ref[rows, sl] = (d_y * mixed[g] * dgl_u[:, sl]).astype(BF16)
                d_mixed = d_y * gu[:, sl]
                dmix_acc[g] += d_mixed
                dws_ref[g] += _dot(d_mixed, vn[:, sl], NT) * mask_ref[...]
                dvn_parts.append(_dot(wmt_ref[g], d_mixed))
            dvn = jnp.concatenate(dvn_parts, axis=1)
            dgv_ref[...] += _colsum8(dvn * vh)
            dbv_ref[...] += _colsum8(dvn)
            d_gl = _ln_bwd(dvn * gvv, vh, rstd)
            dh_ref[rows, D_MODEL:] = (d_gl * dgl_v).astype(BF16)

        @pl.when(i == nb - 1)
        def _():
            rowid = lax.broadcasted_iota(jnp.int32, (8, 128), 0)
            ones = jnp.ones((8, 128), F32)
            acc = jnp.zeros((8, 128), F32)
            for g in range(N_GROUP):
                rs = _dot32(ones, dmix_acc[g], NT)
                acc = jnp.where(rowid == g, rs, acc)
            dbs_ref[...] = acc

    full3 = pl.BlockSpec((N_GROUP, 128, 128), lambda i: (0, 0, 0))
    vec = pl.BlockSpec((1, D_MODEL), lambda i: (0, 0))
    acc8 = pl.BlockSpec((8, D_MODEL), lambda i: (0, 0))
    return _grid1_call(
        body, comm, nb, name="sgu_bwd",
        out_shape=(jax.ShapeDtypeStruct(dh_buf.shape, BF16),
                   jax.ShapeDtypeStruct((N_GROUP, 128, 128), F32),
                   jax.ShapeDtypeStruct((8, 128), F32),
                   jax.ShapeDtypeStruct((8, D_MODEL), F32),
                   jax.ShapeDtypeStruct((8, D_MODEL), F32)),
        in_specs=[pl.BlockSpec((SGU_ROWS, D_MODEL), lambda i: (i, 0)),
                  pl.BlockSpec((SGU_ROWS, D_MODEL), lambda i: (i, 1)),
                  pl.BlockSpec((SGU_ROWS, D_MODEL), lambda i: (i, 0)),
                  full3, full3, full3, vec, vec,
                  pl.BlockSpec((128, 128), lambda i: (0, 0)), ANY],
        out_specs=(pl.BlockSpec((None, SGU_ROWS, 2 * D_MODEL), lambda i: (DH_SLOT[0], i, 0)),
                   full3, pl.BlockSpec((8, 128), lambda i: (0, 0)), acc8, acc8),
        scratch=[pltpu.VMEM((N_GROUP, 128, 128), F32)],
        args=(h, h, dya, wm, wmt, bsb, gv, bv, maskf, dh_buf), aliases={9: 0})


def _tri_masks():
    row = lax.broadcasted_iota(jnp.int32, (CHUNK, CHUNK), 0)
    col = lax.broadcasted_iota(jnp.int32, (CHUNK, CHUNK), 1)
    return col <= row, col >= row


def _heads(v):
    return [v[:, hd * HEAD_DIM:(hd + 1) * HEAD_DIM] for hd in range(N_HEAD)]


def _tri_cumsum(tri_bf, v):
    hi = v.astype(BF16)
    r = v - hi.astype(F32)
    mid = r.astype(BF16)
    lo = (r - mid.astype(F32)).astype(BF16)
    return _dot(tri_bf, hi) + _dot(tri_bf, mid) + _dot(tri_bf, lo)


def _hgrn_chunk(q, fp, ii, lb, st_heads, causal, with_o=True):
    sg = _sig(fp)
    f = lb + (1.0 - lb) * sg
    k = 1.0 - f
    c = _tri_cumsum(causal.astype(BF16), jnp.log(f))
    ec = jnp.exp(c)
    en = jnp.exp(-c)
    sq = _sig(q)
    qt = q * sq * ec
    kt = k * en
    ecl = jnp.exp(c[CHUNK - 1:CHUNK, :])
    kk = kt * ecl
    qtb, ktb, iib, kkb = qt.astype(BF16), kt.astype(BF16), ii.astype(BF16), kk.astype(BF16)
    attn, o = [], []
    for hd, (qh, kh, ih) in enumerate(zip(_heads(qtb), _heads(ktb), _heads(iib))):
        a = jnp.where(causal, _dot(qh, kh, NT), 0.0).astype(BF16)
        attn.append(a)
        if with_o:
            o.append(_dot(a, ih) + _dot(qh, st_heads[hd], NT))
    return dict(sg=sg, f=f, k=k, ec=ec, en=en, sq=sq, ecl=ecl, kk=kk, qtb=qtb, ktb=ktb, iib=iib,
                kkb=kkb, attn=attn, o=o)


def _rms_heads(o_heads):
    rinv = [lax.rsqrt(jnp.mean(o * o, axis=-1, keepdims=True) + RMS_EPS) for o in o_heads]
    return rinv, jnp.concatenate([o * r for o, r in zip(o_heads, rinv)], axis=1)


HG_CHUNKS = 8
HG_ROWS = HG_CHUNKS * CHUNK


def _hgrn_fwd(h, logits, gn, comm=None):
    t = h.shape[0]
    nb = t // HG_ROWS

    def body(q_ref, f_ref, i_ref, og_ref, lg_ref, gn_ref, yb_ref, o_ref, st_ref, state):
        @pl.when(pl.program_id(0) == 0)
        def _():
            state[...] = jnp.zeros_like(state)

        causal, _ = _tri_masks()
        lb = _sig(lg_ref[0:1, :] - lg_ref[1:2, :])
        gnv = gn_ref[...]
        st = [state[hd] for hd in range(N_HEAD)]
        for cc in range(HG_CHUNKS):
            rows = slice(cc * CHUNK, (cc + 1) * CHUNK)
            og = og_ref[rows, :].astype(F32)
            r = _hgrn_chunk(q_ref[rows, :].astype(F32), f_ref[rows, :].astype(F32),
                            i_ref[rows, :].astype(F32), lb, [s.astype(BF16) for s in st], causal)
            o_bf = jnp.concatenate(r["o"], axis=1).astype(BF16)
            o_ref[rows, :] = o_bf
            _, on = _rms_heads(_heads(o_bf.astype(F32)))
            yb_ref[rows, :] = (on * gnv * (og * _sig(og))).astype(BF16)
            for hd in range(N_HEAD):
                st_ref[cc, hd] = st[hd]
            st = [s * e + _dot(ih, kh, TN)
                  for s, e, ih, kh in zip(st, _heads(r["ecl"]), _heads(r["iib"]), _heads(r["kkb"]))]
        for hd in range(N_HEAD):
            state[hd] = st[hd]

    def col(k):
        return pl.BlockSpec((HG_ROWS, D_MODEL), lambda ci: (ci, k))

    return _grid1_call(body, comm, nb, name="hgrn_fwd",
                       out_shape=(jax.ShapeDtypeStruct((t, D_MODEL), BF16),
                                  jax.ShapeDtypeStruct((t, D_MODEL), BF16),
                                  jax.ShapeDtypeStruct((t // CHUNK, N_HEAD, HEAD_DIM, HEAD_DIM), F32)),
                       in_specs=[col(2), col(3), col(4), col(5),
                                 pl.BlockSpec((2, D_MODEL), lambda ci: (0, 0)),
                                 pl.BlockSpec((1, D_MODEL), lambda ci: (0, 0))],
                       out_specs=(pl.BlockSpec((HG_ROWS, D_MODEL), lambda ci: (ci, 0)),
                                  pl.BlockSpec((HG_ROWS, D_MODEL), lambda ci: (ci, 0)),
                                  pl.BlockSpec((HG_CHUNKS, N_HEAD, HEAD_DIM, HEAD_DIM),
                                               lambda ci: (ci, 0, 0, 0))),
                       scratch=[pltpu.VMEM((N_HEAD, HEAD_DIM, HEAD_DIM), F32)],
                       args=(h, h, h, h, logits, gn))


def _hgrn_chunk_bwd(q, fp, ii, og, o_saved, dy, gnv, lb, st, dsn, causal, anti):
    stb = [s.astype(BF16) for s in st]
    dsnb = [s.astype(BF16) for s in dsn]
    r = _hgrn_chunk(q, fp, ii, lb, stb, causal, with_o=False)
    rinv, on = _rms_heads(_heads(o_saved))
    so = _sig(og)
    sil = og * so
    d_og = dy * on * gnv * (so * (1.0 + og * (1.0 - so)))
    d_on = dy * gnv * sil
    d_ob = jnp.concatenate(
        [ri * (dn - oh * jnp.mean(dn * oh, axis=-1, keepdims=True))
         for ri, dn, oh in zip(rinv, _heads(d_on), _heads(on))], axis=1).astype(BF16)
    d_i, d_qt, d_kt, d_kk, d_st, st_dsn = [], [], [], [], [], []
    ecl = _heads(r["ecl"])
    for hd, (dh, qh, kh, ih, kkh) in enumerate(zip(_heads(d_ob), _heads(r["qtb"]), _heads(r["ktb"]),
                                                   _heads(r["iib"]), _heads(r["kkb"]))):
        d_attn = jnp.where(causal, _dot(dh, ih, NT), 0.0).astype(BF16)
        d_i.append(_dot(r["attn"][hd], dh, TN) + _dot(kkh, dsnb[hd], NT))
        d_qt.append(_dot(d_attn, kh) + _dot(dh, stb[hd]))
        d_kt.append(_dot(d_attn, qh, TN))
        d_kk.append(_dot(ih, dsnb[hd]))
        d_st.append(_dot(dh, qh, TN) + dsn[hd] * ecl[hd])
        st_dsn.append(jnp.sum(st[hd] * dsn[hd], axis=0, keepdims=True))
    d_qt = jnp.concatenate(d_qt, axis=1)
    d_kt = jnp.concatenate(d_kt, axis=1)
    d_kk = jnp.concatenate(d_kk, axis=1)
    kk = r["kk"]
    d_cl = r["ecl"] * jnp.concatenate(st_dsn, axis=1) + jnp.sum(kk * d_kk, axis=0, keepdims=True)
    d_k = (d_kk * r["ecl"] + d_kt) * r["en"]
    d_c = d_qt * r["qtb"].astype(F32) - d_kt * r["ktb"].astype(F32) - d_kk * kk
    rowid = lax.broadcasted_iota(jnp.int32, (CHUNK, D_MODEL), 0)
    d_c = d_c + jnp.where(rowid == CHUNK - 1, d_cl, 0.0)
    d_lf = _tri_cumsum(anti.astype(BF16), d_c)
    d_f = d_lf / r["f"] - d_k
    sg, sq = r["sg"], r["sq"]
    d_q = d_qt * r["ec"] * (sq * (1.0 + q * (1.0 - sq)))
    d_fp = d_f * (1.0 - lb) * sg * (1.0 - sg)
    return (d_q, d_fp, jnp.concatenate(d_i, axis=1), d_og, d_st,
            _colsum8(dy * on * sil), _colsum8(d_f * (1.0 - sg)))


def _hgrn_bwd(h, o_all, dyb, st_all, logits, gn, dh_buf, comm=None):
    t = h.shape[0]
    nb = t // HG_ROWS

    def body(q_ref, f_ref, i_ref, og_ref, o_ref, dyb_ref, st_ref, lg_ref, gn_ref, dh_buf_ref,
             dh_ref, dlb_ref, dgn_ref, dstate):
        @pl.when(pl.program_id(0) == 0)
        def _():
            dstate[...] = jnp.zeros_like(dstate)
            dlb_ref[...] = jnp.zeros_like(dlb_ref)
            dgn_ref[...] = jnp.zeros_like(dgn_ref)

        causal, anti = _tri_masks()
        lb = _sig(lg_ref[0:1, :] - lg_ref[1:2, :])
        gnv = gn_ref[...]
        dsn = [dstate[hd] for hd in range(N_HEAD)]
        dgn_acc = jnp.zeros((8, D_MODEL), F32)
        dlb_acc = jnp.zeros((8, D_MODEL), F32)
        for cc in reversed(range(HG_CHUNKS)):
            rows = slice(cc * CHUNK, (cc + 1) * CHUNK)
            d_q, d_fp, d_i, d_og, dsn, dgn_c, dlb_c = _hgrn_chunk_bwd(
                q_ref[rows, :].astype(F32), f_ref[rows, :].astype(F32), i_ref[rows, :].astype(F32),
                og_ref[rows, :].astype(F32), o_ref[rows, :].astype(F32), dyb_ref[rows, :].astype(F32), gnv, lb,
                [st_ref[cc, hd] for hd in range(N_HEAD)], dsn, causal, anti)
            dgn_acc = dgn_acc + dgn_c
            dlb_acc = dlb_acc + dlb_c
            dh_ref[0, rows, :D_MODEL] = d_q.astype(BF16)
            dh_ref[0, rows, D_MODEL:] = d_fp.astype(BF16)
            dh_ref[1, rows, :D_MODEL] = d_i.astype(BF16)
            dh_ref[1, rows, D_MODEL:] = d_og.astype(BF16)
        dgn_ref[...] += dgn_acc
        dlb_ref[...] += dlb_acc
        for hd in range(N_HEAD):
            dstate[hd] = dsn[hd]

    def col(k):
        return pl.BlockSpec((HG_ROWS, D_MODEL), lambda ci: (nb - 1 - ci, k))

    acc8 = pl.BlockSpec((8, D_MODEL), lambda ci: (0, 0))
    pair = pl.BlockSpec((2, HG_ROWS, 2 * D_MODEL), lambda ci: (0, nb - 1 - ci, 0))
    return _grid1_call(body, comm, nb, name="hgrn_bwd",
                       out_shape=(jax.ShapeDtypeStruct(dh_buf.shape, BF16),
                                  jax.ShapeDtypeStruct((8, D_MODEL), F32),
                                  jax.ShapeDtypeStruct((8, D_MODEL), F32)),
                       in_specs=[col(2), col(3), col(4), col(5), col(0),
                                 pl.BlockSpec((HG_ROWS, D_MODEL), lambda ci: (nb - 1 - ci, 0)),
                                 pl.BlockSpec((HG_CHUNKS, N_HEAD, HEAD_DIM, HEAD_DIM),
                                              lambda ci: (nb - 1 - ci, 0, 0, 0)),
                                 pl.BlockSpec((2, D_MODEL), lambda ci: (0, 0)),
                                 pl.BlockSpec((1, D_MODEL), lambda ci: (0, 0)), ANY],
                       out_specs=(pair, acc8, acc8),
                       scratch=[pltpu.VMEM((N_HEAD, HEAD_DIM, HEAD_DIM), F32)],
                       args=(h, h, h, h, o_all, dyb, st_all, logits, gn, dh_buf), aliases={9: 0})


def _mix_fwd(ya, yb, h, x, wb0, wb1, wo, g1, b1, tm, comm=None):
    t = x.shape[0]

    def body(ya_ref, yb_ref, ga_ref, gb_ref, x_ref, wb0_ref, wb1_ref, wo_ref, g1_ref, b1_ref,
             r1_ref, a_ref, b_ref, m_ref, x1_ref):
        a = _dot(ya_ref[...], wb0_ref[...])
        b = _dot(yb_ref[...], wb1_ref[...])
        m = _sig(ga_ref[...].astype(F32)) * a + _sig(gb_ref[...].astype(F32)) * b
        r1 = ALPHA * x_ref[...] + _dot(m, wo_ref[...])
        xh, _ = _ln_stats(r1)
        r1_ref[...] = r1
        a_ref[...] = a.astype(BF16)
        b_ref[...] = b.astype(BF16)
        m_ref[...] = m.astype(BF16)
        x1_ref[...] = (xh * g1_ref[...] + b1_ref[...]).astype(BF16)

    tile = pl.BlockSpec((tm, D_MODEL), lambda i: (i, 0))
    wsp = pl.BlockSpec((D_MODEL, D_MODEL), lambda i: (0, 0))
    vec = pl.BlockSpec((1, D_MODEL), lambda i: (0, 0))
    f32o = jax.ShapeDtypeStruct((t, D_MODEL), F32)
    bfo = jax.ShapeDtypeStruct((t, D_MODEL), BF16)
    return _grid1_call(body, comm, t // tm, name="mix_fwd", out_shape=(f32o, bfo, bfo, bfo, bfo),
                       in_specs=[tile, tile,
                                 pl.BlockSpec((tm, D_MODEL), lambda i: (i, 6)),
                                 pl.BlockSpec((tm, D_MODEL), lambda i: (i, 7)),
                                 tile, wsp, wsp, wsp, vec, vec],
                       out_specs=(tile, tile, tile, tile, tile),
                       scratch=[], args=(ya, yb, h, h, x, wb0, wb1, wo, g1, b1))


def _mix_bwd(dr1, h, a, b, wo, wb0, wb1, tm):
    t = dr1.shape[0]

    def body(dr1_ref, ga_ref, gb_ref, a_ref, b_ref, wo_ref, wb0_ref, wb1_ref,
             da_ref, db_ref, dh3_ref, dya_ref, dyb_ref):
        d_m = _dot(dr1_ref[...], wo_ref[...], NT)
        sa = _sig(ga_ref[...].astype(F32))
        sb = _sig(gb_ref[...].astype(F32))
        d_a = (d_m * sa).astype(BF16)
        d_b = (d_m * sb).astype(BF16)
        da_ref[...] = d_a
        db_ref[...] = d_b
        dh3_ref[:, :D_MODEL] = (d_m * a_ref[...].astype(F32) * sa * (1.0 - sa)).astype(BF16)
        dh3_ref[:, D_MODEL:] = (d_m * b_ref[...].astype(F32) * sb * (1.0 - sb)).astype(BF16)
        dya_ref[...] = _dot(d_a, wb0_ref[...], NT).astype(BF16)
        dyb_ref[...] = _dot(d_b, wb1_ref[...], NT).astype(BF16)

    tile = pl.BlockSpec((tm, D_MODEL), lambda i: (i, 0))
    wsp = pl.BlockSpec((D_MODEL, D_MODEL), lambda i: (0, 0))
    f32o = jax.ShapeDtypeStruct((t, D_MODEL), F32)
    bfo = jax.ShapeDtypeStruct((t, D_MODEL), BF16)
    return _pc(body, name="mix_bwd",
               out_shape=(bfo, bfo, jax.ShapeDtypeStruct((N_CHIP, t, 2 * D_MODEL), BF16), bfo, bfo),
               grid=(t // tm,),
               in_specs=[tile,
                         pl.BlockSpec((tm, D_MODEL), lambda i: (i, 6)),
                         pl.BlockSpec((tm, D_MODEL), lambda i: (i, 7)),
                         tile, tile, wsp, wsp, wsp],
               out_specs=(tile, tile, pl.BlockSpec((None, tm, 2 * D_MODEL), lambda i: (DH_SLOT[3], i, 0)),
                          tile, tile),
               sem=("parallel",))(dr1, h, h, a, b, wo, wb0, wb1)


FF_TILE = 1408
FF_NJ = D_FF // FF_TILE


def _shift_down(v, k):
    return pltpu.roll(v, k, 0)


def _shift_up(v, k):
    return pltpu.roll(v, v.shape[0] - k, 0)


HALO = 16
FF_PIECES = ((0, 768), (768, FF_TILE))


def _ffn_up_act(x1b, wup_st, convw, convb, tm, comm):
    t = x1b.shape[0]
    ni = t // tm
    nth = tm // HALO

    def body(x_ref, xp_ref, wg_ref, wv_ref, cw_ref, cb_ref, h2_ref, act_ref):
        wg = wg_ref[...]
        gate = _dot(x_ref[...], wg).astype(BF16)
        val = _dot(x_ref[...], wv_ref[...]).astype(BF16)
        prev = (_dot(xp_ref[...], wg) * (pl.program_id(0) > 0).astype(F32)).astype(BF16)
        h2_ref[0] = gate
        h2_ref[1] = val
        ext = jnp.concatenate([prev.astype(F32), gate.astype(F32)], axis=0)
        gc = (cw_ref[0:1, :] * _shift_down(ext, 2) + cw_ref[1:2, :] * _shift_down(ext, 1)
              + cw_ref[2:3, :] * ext + cb_ref[...])[HALO:, :].astype(BF16)
        h2_ref[2] = gc
        act_ref[...] = (_gelu(gc.astype(F32)) * val.astype(F32)).astype(BF16)

    res = _hosted_call(
        body, comm,
        lambda: (pl.program_id(0) == 0) & (pl.program_id(1) == 0),
        lambda: (pl.program_id(0) == ni - 1) & (pl.program_id(1) == FF_NJ - 1),
        mid=lambda: (pl.program_id(0) == max(ni - 2, 0)) & (pl.program_id(1) == 0),
        name="ffn_up",
        out_shape=(jax.ShapeDtypeStruct((3, t, D_FF), BF16), jax.ShapeDtypeStruct((t, D_FF), BF16)),
        grid=(ni, FF_NJ),
        in_specs=[pl.BlockSpec((tm, D_MODEL), lambda i, j: (i, 0)),
                  pl.BlockSpec((HALO, D_MODEL), lambda i, j: (jnp.maximum(i * nth - 1, 0), 0)),
                  pl.BlockSpec((None, D_MODEL, FF_TILE), lambda i, j: (j, 0, 0)),
                  pl.BlockSpec((None, D_MODEL, FF_TILE), lambda i, j: (j + FF_NJ, 0, 0)),
                  pl.BlockSpec((3, FF_TILE), lambda i, j: (0, j)),
                  pl.BlockSpec((1, FF_TILE), lambda i, j: (0, j))],
        out_specs=(pl.BlockSpec((3, tm, FF_TILE), lambda i, j: (0, i, j)),
                   pl.BlockSpec((tm, FF_TILE), lambda i, j: (i, j))),
        scratch=[], sem=("arbitrary", "arbitrary"),
        args=(x1b, x1b, wup_st, wup_st, convw, convb))
    return res[0], res[1], res[2:]


def _out_fwd_bwd(act, x1b, r1, p2, tgt, wd, wpg, wpp, g1, b1, g2, b2, tm):
    t = r1.shape[0]

    def body(act_ref, x1b_ref, r1_ref, p_ref, tgt_ref, wd_ref, wpg_ref, wpp_ref,
             g1_ref, b1_ref, g2_ref, b2_ref,
             dr2_ref, dpg_ref, dpp_ref, loss_ref, dg2_ref, db2_ref):
        i = pl.program_id(0)

        @pl.when(i == 0)
        def _():
            loss_ref[...] = jnp.zeros_like(loss_ref)
            dg2_ref[...] = jnp.zeros_like(dg2_ref)
            db2_ref[...] = jnp.zeros_like(db2_ref)

        ffn = _dot(act_ref[...], wd_ref[...])
        pg = _dot(x1b_ref[...], wpg_ref[...])
        pp = _dot(p_ref[...], wpp_ref[...])
        s = _sig(pg)
        xh1, _ = _ln_stats(r1_ref[...])
        x1 = xh1 * g1_ref[...] + b1_ref[...]
        r2 = ALPHA * x1 + ffn + s * pp
        xh2, rstd2 = _ln_stats(r2)
        g2v = g2_ref[...]
        diff = xh2 * g2v + b2_ref[...] - tgt_ref[...]
        part = jnp.sum(jnp.sum(diff * diff, axis=1, keepdims=True), axis=0, keepdims=True)
        loss_ref[...] += jnp.broadcast_to(part * (0.5 / D_MODEL), loss_ref.shape)
        dy = diff * (1.0 / D_MODEL)
        dg2_ref[...] += _colsum8(dy * xh2)
        db2_ref[...] += _colsum8(dy)
        dr2 = _ln_bwd(dy * g2v, xh2, rstd2)
        dr2_ref[...] = dr2.astype(BF16)
        dpg_ref[...] = (dr2 * pp * s * (1.0 - s)).astype(BF16)
        dpp_ref[...] = (dr2 * s).astype(BF16)

    tile = pl.BlockSpec((tm, D_MODEL), lambda i: (i, 0))
    vec = pl.BlockSpec((1, D_MODEL), lambda i: (0, 0))
    acc8 = pl.BlockSpec((8, D_MODEL), lambda i: (0, 0))
    acc_shape = jax.ShapeDtypeStruct((8, D_MODEL), F32)
    return _pc(body, name="out_fwd_bwd",
               out_shape=(jax.ShapeDtypeStruct((t, D_MODEL), BF16),
                          jax.ShapeDtypeStruct((t, D_MODEL), BF16),
                          jax.ShapeDtypeStruct((t, D_MODEL), BF16),
                          acc_shape, acc_shape, acc_shape),
               grid=(t // tm,),
               in_specs=[pl.BlockSpec((tm, D_FF), lambda i: (i, 0)), tile, tile,
                         pl.BlockSpec((tm, PLE_DIM), lambda i: (i, 0)), tile,
                         pl.BlockSpec((D_FF, D_MODEL), lambda i: (0, 0)),
                         pl.BlockSpec((D_MODEL, D_MODEL), lambda i: (0, 0)),
                         pl.BlockSpec((PLE_DIM, D_MODEL), lambda i: (0, 0)),
                         vec, vec, vec, vec],
               out_specs=(tile, tile, tile, acc8, acc8, acc8),
               sem=("arbitrary",))(act, x1b, r1, p2, tgt, wd, wpg, wpp, g1, b1, g2, b2)


def _ffn_bwd(h2, dr2, wd, wup_st, dpg, wpg, r1, g1, convw, tm):
    t = r1.shape[0]
    ni = t // tm
    nth = tm // HALO
    last_halo = t // HALO - 1
    main_rows = slice(0, tm)

    def body(g_ref, gc_ref, gcn_ref, v_ref, vn_ref, dr2_ref, dr2n_ref, wd_ref, wug_ref, wuv_ref,
             cw_ref, dpg_ref, wpg_ref, r1_ref, g1_ref,
             dh2_ref, dr1_ref, dcw_ref, dcb_ref, dg1_ref, db1_ref, acc):
        i = pl.program_id(0)
        j = pl.program_id(1)

        @pl.when((i == 0) & (j == 0))
        def _():
            dcw_ref[...] = jnp.zeros_like(dcw_ref)
            dcb_ref[...] = jnp.zeros_like(dcb_ref)
            dg1_ref[...] = jnp.zeros_like(dg1_ref)
            db1_ref[...] = jnp.zeros_like(db1_ref)

        dr2v = dr2_ref[...].astype(BF16)
        dr2n = dr2n_ref[...].astype(BF16)
        more = (i < ni - 1).astype(F32)
        prod = None
        dcw_parts, dcb_parts = [], []
        for c0, c1 in FF_PIECES:
            pc = slice(c0, c1)
            da = _dot(dr2v, wd_ref[pc, :], NT)
            dnext = _dot(dr2n, wd_ref[pc, :], NT) * more
            gc = jnp.concatenate([gc_ref[:, pc].astype(F32), gcn_ref[:, pc].astype(F32)], axis=0)
            vext = jnp.concatenate([v_ref[:, pc].astype(F32), vn_ref[:, pc].astype(F32)], axis=0)
            dext = jnp.concatenate([da, dnext], axis=0)
            gl, dgl = _gelu_and_grad(gc)
            d_gc = dext * vext * dgl
            up1 = _shift_up(d_gc, 1)[main_rows, :]
            up2 = _shift_up(d_gc, 2)[main_rows, :]
            dm = d_gc[main_rows, :]
            d_gate = (cw_ref[2:3, pc] * dm + cw_ref[1:2, pc] * up1 + cw_ref[0:1, pc] * up2).astype(BF16)
            d_val = (da * gl[main_rows, :]).astype(BF16)
            dh2_ref[0, :, pc] = d_gate
            dh2_ref[1, :, pc] = d_val
            g = g_ref[:, pc].astype(F32)
            s0 = jnp.sum(g * up2, axis=0, keepdims=True)
            s1 = jnp.sum(g * up1, axis=0, keepdims=True)
            s2 = jnp.sum(g * dm, axis=0, keepdims=True)
            rowid = lax.broadcasted_iota(jnp.int32, (8, c1 - c0), 0)
            dcw_parts.append(jnp.where(rowid == 0, s0, jnp.where(rowid == 1, s1,
                                                                 jnp.where(rowid == 2, s2, 0.0))))
            dcb_parts.append(_colsum8(dm))
            part = _dot(d_gate, wug_ref[:, pc], NT) + _dot(d_val, wuv_ref[:, pc], NT)
            prod = part if prod is None else prod + part
        dcw_part = jnp.concatenate(dcw_parts, axis=1)
        dcb_part = jnp.concatenate(dcb_parts, axis=1)
        for jj in range(FF_NJ):
            @pl.when(j == jj)
            def _(jj=jj):
                cols = slice(jj * FF_TILE, (jj + 1) * FF_TILE)
                dcw_ref[:, cols] += dcw_part
                dcb_ref[:, cols] += dcb_part

        @pl.when(j == 0)
        def _():
            acc[...] = prod

        @pl.when(j > 0)
        def _():
            acc[...] += prod

        @pl.when(j == FF_NJ - 1)
        def _():
            d_x1 = acc[...] + _dot(dpg_ref[...], wpg_ref[...], NT) + ALPHA * dr2_ref[...].astype(F32)
            xh, rstd = _ln_stats(r1_ref[...])
            dg1_ref[...] += _colsum8(d_x1 * xh)
            db1_ref[...] += _colsum8(d_x1)
            dr1_ref[...] = _ln_bwd(d_x1 * g1_ref[...], xh, rstd).astype(BF16)

    def h2_main(part):
        return pl.BlockSpec((None, tm, FF_TILE), lambda i, j: (part, i, j))

    def h2_next(part):
        return pl.BlockSpec((None, HALO, FF_TILE),
                            lambda i, j: (part, jnp.minimum((i + 1) * nth, last_halo), j))

    tile = pl.BlockSpec((tm, D_MODEL), lambda i, j: (i, 0))
    acc8 = pl.BlockSpec((8, D_MODEL), lambda i, j: (0, 0))
    accff = pl.BlockSpec((8, D_FF), lambda i, j: (0, 0))
    acc_shape = jax.ShapeDtypeStruct((8, D_MODEL), F32)
    accff_shape = jax.ShapeDtypeStruct((8, D_FF), F32)
    return _pc(body, name="ffn_bwd",
               out_shape=(jax.ShapeDtypeStruct((2, t, D_FF), BF16),
                          jax.ShapeDtypeStruct((t, D_MODEL), BF16),
                          accff_shape, accff_shape, acc_shape, acc_shape),
               grid=(ni, FF_NJ),
               in_specs=[h2_main(0), h2_main(2), h2_next(2), h2_main(1), h2_next(1),
                         tile,
                         pl.BlockSpec((HALO, D_MODEL), lambda i, j: (jnp.minimum((i + 1) * nth, last_halo), 0)),
                         pl.BlockSpec((FF_TILE, D_MODEL), lambda i, j: (j, 0)),
                         pl.BlockSpec((None, D_MODEL, FF_TILE), lambda i, j: (j, 0, 0)),
                         pl.BlockSpec((None, D_MODEL, FF_TILE), lambda i, j: (j + FF_NJ, 0, 0)),
                         pl.BlockSpec((3, FF_TILE), lambda i, j: (0, j)),
                         tile, pl.BlockSpec((D_MODEL, D_MODEL), lambda i, j: (0, 0)),
                         tile, pl.BlockSpec((1, D_MODEL), lambda i, j: (0, 0))],
               out_specs=(pl.BlockSpec((2, tm, FF_TILE), lambda i, j: (0, i, j)),
                          tile, accff, accff, acc8, acc8),
               scratch=[pltpu.VMEM((tm, D_MODEL), F32)],
               sem=("arbitrary", "arbitrary"))(h2, h2, h2, h2, h2, dr2, dr2, wd, wup_st, wup_st,
                                               convw, dpg, wpg, r1, g1)


ANY = pl.BlockSpec(memory_space=pl.ANY)


def _chip_peers():
    x, y, c = lax.axis_index("x"), lax.axis_index("y"), lax.axis_index("c")
    return x, y, c, [(1 - x, y), (x, 1 - y), (1 - x, 1 - y)]


def _gather_comm(halved, whole=(), blocks=None):
    n, nw = len(halved), len(whole)
    blocks = blocks or {}

    def at(ti, ref, chip, *rest):
        return ref.at[(chip, blocks[ti][1]) + rest] if ti in blocks else ref.at[(chip,) + rest]

    def copies(ins, outs, sems):
        ici_send, ici_recv, d2d_send, d2d_recv, own_send, own_recv = sems
        x, y, c, peers = _chip_peers()
        me = 2 * x + y
        sibling = (x, y, 1 - c)
        own, ici, ici_wait, fwd, fwd_wait = [], [], [], [], []
        for ti in range(n + nw):
            src, dst = ins[ti], outs[ti]
            own.append(pltpu.make_async_remote_copy(
                src_ref=src, dst_ref=at(ti, dst, me), send_sem=own_send.at[ti], recv_sem=own_recv.at[ti],
                device_id=sibling, device_id_type=MESH))
            for k, (px, py) in enumerate(peers):
                pk = 2 * px + py
                sem = dict(send_sem=ici_send.at[ti * 3 + k], recv_sem=ici_recv.at[ti * 3 + k],
                           device_id=(px, py, c), device_id_type=MESH)
                if ti < n:
                    ici.append(pltpu.make_async_remote_copy(src_ref=src.at[c], dst_ref=at(ti, dst, me, c), **sem))
                    ici_wait.append(pltpu.make_async_remote_copy(src_ref=src.at[c], dst_ref=at(ti, dst, pk, c),
                                                                 **sem))
                    dsem = dict(send_sem=d2d_send.at[ti * 3 + k], recv_sem=d2d_recv.at[ti * 3 + k],
                                device_id=sibling, device_id_type=MESH)
                    fwd.append(pltpu.make_async_remote_copy(src_ref=at(ti, dst, pk, c), dst_ref=at(ti, dst, pk, c),
                                                            **dsem))
                    fwd_wait.append(pltpu.make_async_remote_copy(
                        src_ref=at(ti, dst, pk, 1 - c), dst_ref=at(ti, dst, pk, 1 - c), **dsem))
                else:
                    ici.append(pltpu.make_async_remote_copy(src_ref=src, dst_ref=dst.at[me], **sem))
                    ici_wait.append(pltpu.make_async_remote_copy(src_ref=src, dst_ref=dst.at[pk], **sem))
        return own, ici, ici_wait, fwd, fwd_wait

    def start(ins, outs, sems):
        own, ici, _, _, _ = copies(ins, outs, sems)
        for cp in own + ici:
            cp.start()

    def finish(ins, outs, sems):
        own, ici, ici_wait, fwd, fwd_wait = copies(ins, outs, sems)
        for i, cp in enumerate(ici_wait):
            cp.wait_recv()
            if i < len(fwd):
                fwd[i].start()
        for cp in fwd_wait + own:
            cp.wait_recv()
        for cp in own + ici + fwd:
            cp.wait_send()

    def middle(ins, outs, sems):
        _, _, ici_wait, fwd, _ = copies(ins, outs, sems)
        for i, cp in enumerate(ici_wait):
            cp.wait_recv()
            if i < len(fwd):
                fwd[i].start()

    def finish_late(ins, outs, sems):
        own, ici, _, fwd, fwd_wait = copies(ins, outs, sems)
        for cp in fwd_wait + own:
            cp.wait_recv()
        for cp in own + ici + fwd:
            cp.wait_send()

    srcs = list(halved) + list(whole)
    shapes = [jax.ShapeDtypeStruct((N_CHIP,) + ((blocks[ti][0],) if ti in blocks else ()) + s.shape, s.dtype)
              for ti, s in enumerate(srcs)]
    buffers = [(ti, blk[2]) for ti, blk in sorted(blocks.items()) if blk[2] is not None]
    aliases = {len(srcs) + bi: ti for bi, (ti, _) in enumerate(buffers)}
    return _Comm(srcs + [buf for _, buf in buffers], shapes,
                 [pltpu.SemaphoreType.DMA((3 * (n + nw),)), pltpu.SemaphoreType.DMA((3 * (n + nw),)),
                  pltpu.SemaphoreType.DMA((max(3 * n, 1),)), pltpu.SemaphoreType.DMA((max(3 * n, 1),)),
                  pltpu.SemaphoreType.DMA((n + nw,)), pltpu.SemaphoreType.DMA((n + nw,))],
                 start, finish, middle, finish_late, aliases)


def _sibling_exchange_comm(grads):
    n = len(grads)

    def copies(ins, outs, sems):
        send_sems, recv_sems = sems
        x, y, c = lax.axis_index("x"), lax.axis_index("y"), lax.axis_index("c")
        res = []
        for ti in range(n):
            half = ins[ti].shape[1] // 2
            res.append(pltpu.make_async_remote_copy(
                src_ref=ins[ti].at[:, pl.ds(pl.multiple_of((1 - c) * half, 16), half), :],
                dst_ref=outs[ti],
                send_sem=send_sems.at[ti], recv_sem=recv_sems.at[ti],
                device_id=(x, y, 1 - c), device_id_type=MESH))
        return res

    def start(ins, outs, sems):
        for cp in copies(ins, outs, sems):
            cp.start()

    def finish(ins, outs, sems):
        for cp in copies(ins, outs, sems):
            cp.wait()

    return _Comm(grads, [jax.ShapeDtypeStruct((N_CHIP, g.shape[1] // 2, g.shape[2]), g.dtype) for g in grads],
                 [pltpu.SemaphoreType.DMA((n,)), pltpu.SemaphoreType.DMA((n,))], start, finish)


def _in_proj_gathering(x2b, own, chip, tm, comm):
    t = x2b.shape[0]
    ni = t // tm
    half, cols = own.shape[1], own.shape[2]
    nci, nco = len(comm.ins), len(comm.out_shapes)

    def body(chip_ref, x_ref, own_ref, own_hbm, *rest):
        c_in = rest[:nci]
        h_ref, win_out = rest[nci:nci + 2]
        c_out = rest[nci + 2:nci + 2 + nco]
        w_scr, ici_send, ici_recv, d2d_send, d2d_recv, own_sems, ld_sems = rest[nci + 2 + nco:nci + 9 + nco]
        c_sem = rest[nci + 9 + nco:]
        s, i = pl.program_id(0), pl.program_id(1)
        x, y, c, peers = _chip_peers()
        me = 2 * x + y
        sibling = (x, y, 1 - c)

        def ici(k, slot):
            px, py = peers[k]
            return pltpu.make_async_remote_copy(
                src_ref=own_hbm.at[c], dst_ref=win_out.at[slot, c],
                send_sem=ici_send.at[k], recv_sem=ici_recv.at[k],
                device_id=(px, py, c), device_id_type=MESH)

        def forward(k, core):
            pk = 2 * peers[k][0] + peers[k][1]
            return pltpu.make_async_remote_copy(
                src_ref=win_out.at[pk, core], dst_ref=win_out.at[pk, core],
                send_sem=d2d_send.at[k], recv_sem=d2d_recv.at[k],
                device_id=sibling, device_id_type=MESH)

        place_own = pltpu.make_async_remote_copy(
            src_ref=own_hbm, dst_ref=win_out.at[me], send_sem=own_sems.at[0], recv_sem=own_sems.at[1],
            device_id=sibling, device_id_type=MESH)

        @pl.when((s == 0) & (i == 0))
        def _():
            for k in range(2):
                ici(k, me).start()
            place_own.start()

        @pl.when(s == 0)
        def _():
            xv = x_ref[...]
            h_ref[...] = (_dot(xv[:, :half], own_ref[0]) + _dot(xv[:, half:], own_ref[1])).astype(BF16)

        for k in range(3):
            @pl.when((s == k + 1) & (i == 0))
            def _(k=k):
                pk = 2 * peers[k][0] + peers[k][1]
                ici(k, pk).wait_recv()
                if k == 0:
                    ici(2, me).start()
                forward(k, c).start()
                forward(k, 1 - c).wait_recv()
                loads = [pltpu.make_async_copy(win_out.at[pk, hh], w_scr.at[hh], ld_sems.at[hh])
                         for hh in range(2)]
                for ld in loads:
                    ld.start()
                for ld in loads:
                    ld.wait()
                if k == 1:
                    comm.start(c_in, c_out, c_sem)

        @pl.when(s > 0)
        def _():
            xv = x_ref[...]
            h_ref[...] = (_dot(xv[:, :half], w_scr[0]) + _dot(xv[:, half:], w_scr[1])).astype(BF16)

        @pl.when((s == N_CHIP - 1) & (i == ni - 1))
        def _():
            place_own.wait()
            for k in range(3):
                ici(k, me).wait_send()
                forward(k, c).wait_send()
            comm.finish(c_in, c_out, c_sem)

    def shard_col(s, me):
        return jnp.where(s == 0, me, me ^ jnp.where(s == 1, 2, jnp.where(s == 2, 1, 3)))

    res = _pc(body, name="in_proj",
              out_shape=(jax.ShapeDtypeStruct((t, N_CHIP * cols), BF16),
                         jax.ShapeDtypeStruct((N_CHIP,) + own.shape, own.dtype)) + tuple(comm.out_shapes),
              grid=(N_CHIP, ni), nsp=1,
              in_specs=[pl.BlockSpec((tm, 2 * half), lambda s, i, chip_ref: (i, 0)),
                        pl.BlockSpec(own.shape, lambda s, i, chip_ref: (0, 0, 0)),
                        ANY] + [ANY] * nci,
              out_specs=(pl.BlockSpec((tm, cols), lambda s, i, chip_ref: (i, shard_col(s, chip_ref[0]))),
                         ANY) + tuple([ANY] * nco),
              scratch=[pltpu.VMEM(own.shape, own.dtype),
                       pltpu.SemaphoreType.DMA((3,)), pltpu.SemaphoreType.DMA((3,)),
                       pltpu.SemaphoreType.DMA((3,)), pltpu.SemaphoreType.DMA((3,)),
                       pltpu.SemaphoreType.DMA((2,)), pltpu.SemaphoreType.DMA((2,))] + comm.sems,
              sem=("arbitrary", "arbitrary"))(chip, x2b, own, own, *comm.ins)
    return res[0], res[1], res[2:]


def _rs_add_halves(name, grad, recv, core):
    _, r, cdim = grad.shape
    half = r // 2
    tr = _row_tile(half, cdim, mult=16)
    nr = half // tr

    def body(c_ref, g_ref, r_ref, o_ref):
        o_ref[...] = (g_ref[...].astype(F32) + r_ref[...].astype(F32)).astype(BF16)

    return _pc(body, name=name, out_shape=jax.ShapeDtypeStruct((N_CHIP, half, cdim), BF16),
               grid=(N_CHIP, nr), nsp=1,
               in_specs=[pl.BlockSpec((None, tr, cdim), lambda j, i, c_ref: (j, c_ref[0] * nr + i, 0)),
                         pl.BlockSpec((None, tr, cdim), lambda j, i, c_ref: (j, i, 0))],
               out_specs=pl.BlockSpec((None, tr, cdim), lambda j, i, c_ref: (j, i, 0)),
               sem=("parallel", "parallel"))(core, grad, recv)


def _chip_exchange_comm(parts):
    n = len(parts)

    def copies(ins, outs, sems):
        send_sems, recv_sems = sems
        x, y, c, peers = _chip_peers()
        return [pltpu.make_async_remote_copy(
            src_ref=ins[ti].at[2 * px + py], dst_ref=outs[ti].at[k],
            send_sem=send_sems.at[ti * 3 + k], recv_sem=recv_sems.at[ti * 3 + k],
            device_id=(px, py, c), device_id_type=MESH)
            for ti in range(n) for k, (px, py) in enumerate(peers)]

    def start(ins, outs, sems):
        for cp in copies(ins, outs, sems):
            cp.start()

    def finish(ins, outs, sems):
        for cp in copies(ins, outs, sems):
            cp.wait()

    return _Comm(parts, [jax.ShapeDtypeStruct((3,) + p.shape[1:], p.dtype) for p in parts],
                 [pltpu.SemaphoreType.DMA((3 * n,)), pltpu.SemaphoreType.DMA((3 * n,))], start, finish)


def _rs_sum_chips(name, part, recv, chip):
    _, half, cdim = recv.shape
    tr = _row_tile(half, cdim, mult=16)

    def body(chip_ref, p_ref, r_ref, o_ref):
        o_ref[...] = ((p_ref[...].astype(F32) + r_ref[0].astype(F32)) + r_ref[1].astype(F32)
                      ) + r_ref[2].astype(F32)

    return _pc(body, name=name, out_shape=jax.ShapeDtypeStruct((half, cdim), F32),
               grid=(half // tr,), nsp=1,
               in_specs=[pl.BlockSpec((None, tr, cdim), lambda i, chip_ref: (chip_ref[0], i, 0)),
                         pl.BlockSpec((3, tr, cdim), lambda i, chip_ref: (0, i, 0))],
               out_specs=pl.BlockSpec((tr, cdim), lambda i, chip_ref: (i, 0)),
               sem=("parallel",))(chip, part, recv)


def _rs_send_halves(halves):
    n = len(halves)

    def body(*refs):
        ins, outs = refs[:n], refs[n:2 * n]
        send_sems, recv_sems = refs[2 * n:]
        x, y, c = lax.axis_index("x"), lax.axis_index("y"), lax.axis_index("c")
        sends = []
        for ti in range(n):
            cp = pltpu.make_async_remote_copy(
                src_ref=ins[ti], dst_ref=outs[ti],
                send_sem=send_sems.at[ti], recv_sem=recv_sems.at[ti],
                device_id=(x, y, 1 - c), device_id_type=MESH)
            cp.start()
            sends.append(cp)
        for cp in sends:
            cp.wait()

    return _pc(body, name="rs_send_halves",
               out_shape=tuple(jax.ShapeDtypeStruct(hv.shape, hv.dtype) for hv in halves),
               in_specs=[ANY] * n, out_specs=tuple([ANY] * n),
               scratch=[pltpu.SemaphoreType.DMA((n,)), pltpu.SemaphoreType.DMA((n,))])(*halves)


def _adamw_rows(name, mine, theirs, w, m, v, core):
    half, cdim = mine.shape
    tr = _row_tile(half, cdim, budget=1 << 19)
    nrh = half // tr

    def body(c_ref, mine_ref, theirs_ref, w_ref, m_ref, v_ref, g_ref, d_ref, m2_ref, v2_ref):
        is_mine = (pl.program_id(0) // nrh) == c_ref[0]
        g = jnp.where(is_mine, mine_ref[...], theirs_ref[...])
        d, m2, v2 = _adamw(w_ref[...], g, m_ref[...], v_ref[...])
        g_ref[...] = g
        d_ref[...] = d
        m2_ref[...] = m2
        v2_ref[...] = v2

    htile = pl.BlockSpec((tr, cdim), lambda i, c_ref: (i % nrh, 0))
    tile = pl.BlockSpec((tr, cdim), lambda i, c_ref: (i, 0))
    shp = jax.ShapeDtypeStruct((2 * half, cdim), F32)
    return _pc(body, name=name, out_shape=(shp, shp, shp, shp), grid=(2 * nrh,), nsp=1,
               in_specs=[htile, htile, tile, tile, tile], out_specs=(tile, tile, tile, tile),
               sem=("parallel",))(core, mine, theirs, w, m, v)


def _adamw_whole(name, g, w, m, v):
    def body(g_ref, w_ref, m_ref, v_ref, d_ref, m2_ref, v2_ref):
        d, m2, v2 = _adamw(w_ref[...], g_ref[...], m_ref[...], v_ref[...])
        d_ref[...] = d
        m2_ref[...] = m2
        v2_ref[...] = v2

    shp = jax.ShapeDtypeStruct(g.shape, F32)
    return _pc(body, name=name, out_shape=(shp, shp, shp))(g, w, m, v)


SMALL_LAYOUT = (
    ("sgu_w_s", 1024, 1, 0),
    ("sgu_b_s", 8, 1, 1024),
    ("sgu_norm_g", 1, 0, 0),
    ("sgu_norm_b", 1, 0, 1),
    ("hgrn_norm_g", 1, 0, 3),
    ("ln1_g", 1, 0, 4),
    ("ln1_b", 1, 0, 5),
    ("ffn_conv_b", 1, 2, 3),
    ("ln2_g", 1, 0, 6),
    ("ln2_b", 1, 0, 7),
)
LB_ROW = 2
LOSS_ROW = 8
PACK_SHAPES = ((16, D_MODEL), (N_GROUP * 128 + 16, 128), (8, D_FF))
GATH_DTYPES = (F32, BF16, F32)


def _small_allreduce_adamw(rows1024, dws, dbs, dcw, dcb, logits, m_logits, v_logits,
                           small_w, small_m, small_v):
    ns = len(SMALL_LAYOUT)
    nr = len(rows1024)
    nb = len(PACK_SHAPES)

    def body(*refs):
        row_refs = refs[:nr]
        dws_ref, dbs_ref, dcw_ref, dcb_ref = refs[nr:nr + 4]
        pos = nr + 4
        tot = refs[pos:pos + nb]
        pos += nb
        pack = refs[pos:pos + nb]
        sib = refs[pos + nb:pos + 2 * nb]
        gath = refs[pos + 2 * nb:pos + 3 * nb]
        d2d_send, d2d_recv, ici_send, ici_recv = refs[pos + 3 * nb:]

        x, y, c, peers = _chip_peers()
        me = 2 * x + y
        sibling = (x, y, 1 - c)

        pack[0][...] = jnp.zeros(PACK_SHAPES[0], F32)
        for k in range(nr):
            pack[0][k:k + 1, :] = row_refs[k][0:1, :]
        pack[1][0:N_GROUP * 128, :] = dws_ref[...]
        pack[1][N_GROUP * 128:N_GROUP * 128 + 8, :] = dbs_ref[...]
        pack[1][N_GROUP * 128 + 8:, :] = jnp.zeros((8, 128), F32)
        pack[2][...] = jnp.zeros(PACK_SHAPES[2], F32)
        pack[2][0:3, :] = dcw_ref[0:3, :]
        pack[2][3:4, :] = dcb_ref[0:1, :]

        d2d = [pltpu.make_async_remote_copy(
            src_ref=pack[b], dst_ref=sib[b], send_sem=d2d_send.at[b], recv_sem=d2d_recv.at[b],
            device_id=sibling, device_id_type=MESH) for b in range(nb)]
        for cp in d2d:
            cp.start()
        for cp in d2d:
            cp.wait()
        for b in range(nb):
            gath[b][me] = (pack[b][...] + sib[b][...]).astype(GATH_DTYPES[b])

        ici, ici_wait = [], []
        for b in range(nb):
            for k, (px, py) in enumerate(peers):
                sem = dict(send_sem=ici_send.at[b * 3 + k], recv_sem=ici_recv.at[b * 3 + k],
                           device_id=(px, py, c), device_id_type=MESH)
                ici.append(pltpu.make_async_remote_copy(src_ref=gath[b].at[me], dst_ref=gath[b].at[me], **sem))
                ici_wait.append(pltpu.make_async_remote_copy(
                    src_ref=gath[b].at[me], dst_ref=gath[b].at[2 * px + py], **sem))
        for cp in ici:
            cp.start()
        for cp in ici_wait:
            cp.wait_recv()
        for cp in ici:
            cp.wait_send()

        for b in range(nb):
            tot[b][...] = ((gath[b][0].astype(F32) + gath[b][1].astype(F32)) + gath[b][2].astype(F32)
                           ) + gath[b][3].astype(F32)

    def update(*refs):
        tot = refs[:nb]
        lg_ref, mlg_ref, vlg_ref = refs[nb:nb + 3]
        pos = nb + 3
        w_refs = refs[pos:pos + ns]
        m_refs = refs[pos + ns:pos + 2 * ns]
        v_refs = refs[pos + 2 * ns:pos + 3 * ns]
        pos += 3 * ns
        loss_ref = refs[pos]
        lg_outs = refs[pos + 1:pos + 5]
        outs = refs[pos + 5:pos + 5 + 4 * ns]

        loss_ref[...] = tot[0][LOSS_ROW:LOSS_ROW + 1, :]
        lb = _sig(lg_ref[0:1, :] - lg_ref[1:2, :])
        d0 = tot[0][LB_ROW:LB_ROW + 1, :] * lb * (1.0 - lb)
        rowid = lax.broadcasted_iota(jnp.int32, (2, D_MODEL), 0)
        g_lg = jnp.where(rowid == 0, d0, -d0)
        dl, ml, vl = _adamw(lg_ref[...], g_lg, mlg_ref[...], vlg_ref[...])
        lg_outs[0][...] = g_lg
        lg_outs[1][...] = dl
        lg_outs[2][...] = ml
        lg_outs[3][...] = vl
        for si, (_, rows, b, r0) in enumerate(SMALL_LAYOUT):
            g = tot[b][r0:r0 + rows, :]
            dl, ml, vl = _adamw(w_refs[si][...], g, m_refs[si][...], v_refs[si][...])
            outs[4 * si][...] = g
            outs[4 * si + 1][...] = dl
            outs[4 * si + 2][...] = ml
            outs[4 * si + 3][...] = vl

    scratch = [pltpu.VMEM(shp, F32) for shp in PACK_SHAPES]
    scratch += [pltpu.VMEM(shp, F32) for shp in PACK_SHAPES]
    scratch += [pltpu.VMEM((N_CHIP,) + shp, dt) for shp, dt in zip(PACK_SHAPES, GATH_DTYPES)]
    scratch += [pltpu.SemaphoreType.DMA((nb,)), pltpu.SemaphoreType.DMA((nb,)),
                pltpu.SemaphoreType.DMA((3 * nb,)), pltpu.SemaphoreType.DMA((3 * nb,))]
    vm = pl.BlockSpec(memory_space=pltpu.VMEM)
    tots = _pc(body, name="small_allreduce",
               out_shape=tuple(jax.ShapeDtypeStruct(shp, F32) for shp in PACK_SHAPES),
               in_specs=[vm] * (nr + 4), out_specs=tuple([vm] * nb),
               scratch=scratch)(*rows1024, dws, dbs, dcw, dcb)

    shapes = [jax.ShapeDtypeStruct((1, D_MODEL), F32)]
    shapes += [jax.ShapeDtypeStruct((2, D_MODEL), F32)] * 4
    for w in small_w:
        shapes += [jax.ShapeDtypeStruct(w.shape, F32)] * 4
    res = _pc(update, name="small_adamw", out_shape=tuple(shapes),
              in_specs=[vm] * (nb + 3 + 3 * ns), out_specs=tuple([vm] * len(shapes)),
              )(*tots, logits, m_logits, v_logits, *small_w, *small_m, *small_v)
    return res[0], tots[2], res[1:5], res[5:]


def kernel(x, p, w_in, sgu_w_s, sgu_b_s, sgu_norm_g, sgu_norm_b, hgrn_lb_logits, hgrn_norm_g, w_branch, w_out, ln1_g, ln1_b, ffn_w_up, ffn_conv_w, ffn_conv_b, ffn_w_down, ln2_g, ln2_b, ple_w_proj, ple_w_gate, loss_target, m_w_in, m_sgu_w_s, m_sgu_b_s, m_sgu_norm_g, m_sgu_norm_b, m_hgrn_lb_logits, m_hgrn_norm_g, m_w_branch, m_w_out, m_ln1_g, m_ln1_b, m_ffn_w_up, m_ffn_conv_w, m_ffn_conv_b, m_ffn_w_down, m_ln2_g, m_ln2_b, m_ple_w_proj, m_ple_w_gate, v_w_in, v_sgu_w_s, v_sgu_b_s, v_sgu_norm_g, v_sgu_norm_b, v_hgrn_lb_logits, v_hgrn_norm_g, v_w_branch, v_w_out, v_ln1_g, v_ln1_b, v_ffn_w_up, v_ffn_conv_w, v_ffn_conv_b, v_ffn_w_down, v_ln2_g, v_ln2_b, v_ple_w_proj, v_ple_w_gate):
    t = x.shape[1]
    x2 = x.reshape(t, D_MODEL)
    x2b = x2.astype(BF16)
    p2 = p.reshape(t, PLE_DIM)
    tgt = loss_target.reshape(t, D_MODEL)
    core = lax.axis_index("c").astype(jnp.int32).reshape(1)
    chip_id = (2 * lax.axis_index("x") + lax.axis_index("y")).astype(jnp.int32).reshape(1)

    big_w = [w_in[0], w_branch[0, 0], w_branch[0, 1], w_out[0], ffn_w_up[0], ffn_w_down[0],
             ple_w_proj[0], ple_w_gate[0]]
    big_m = [m_w_in[0], m_w_branch[0, 0], m_w_branch[0, 1], m_w_out[0], m_ffn_w_up[0],
             m_ffn_w_down[0], m_ple_w_proj[0], m_ple_w_gate[0]]
    big_v = [v_w_in[0], v_w_branch[0, 0], v_w_branch[0, 1], v_w_out[0], v_ffn_w_up[0],
             v_ffn_w_down[0], v_ple_w_proj[0], v_ple_w_gate[0]]
    def halves_of(i):
        w = big_w[i]
        return w.astype(BF16).reshape(2, w.shape[0] // 2, w.shape[1])

    def stacked(g, i):
        return g.reshape(N_CHIP, big_w[i].shape[0], big_w[i].shape[1])


    cid = jnp.arange(SGU_BLOCK) // CHUNK
    maskf = (cid[:, None] >= cid[None, :]).astype(F32)
    ws_masked = sgu_w_s[0] * maskf[None]
    wm = ws_masked.astype(BF16)
    wmt = jnp.transpose(ws_masked, (0, 2, 1)).astype(BF16)
    bsb = jnp.broadcast_to(sgu_b_s[0][:, :, None], (N_GROUP, SGU_BLOCK, 128))

    up_rows = big_w[4].shape[0] // 2
    up_blocks = [big_w[4][k * up_rows:(k + 1) * up_rows].astype(BF16).reshape(2, up_rows // 2, -1)
                 for k in range(2)]
    h, win_g, (up_g,) = _in_proj_gathering(x2b, halves_of(0), chip_id, 512,
                                           _gather_comm([up_blocks[0]], blocks={0: (2, 0, None)}))
    win_st = stacked(win_g, 0)
    ya, _ = _sgu_fwd(h, wm, bsb, sgu_norm_g, sgu_norm_b)
    (yb, o_all, st_all), mix_g = _hgrn_fwd(
        h, hgrn_lb_logits, hgrn_norm_g,
        comm=_gather_comm([halves_of(i) for i in (1, 2, 3)] + [up_blocks[1]], [ffn_conv_w[0]],
                          blocks={3: (2, 1, up_g)}))
    wb0, wb1, wo = [stacked(g, i).reshape(D_MODEL, D_MODEL) for g, i in zip(mix_g[:3], (1, 2, 3))]
    wup_st = stacked(mix_g[3], 4)
    convw = jnp.transpose(mix_g[4], (1, 0, 2)).reshape(3, D_FF)
    (r1, a_br, b_br, m_bf, x1b), _ = _mix_fwd(ya, yb, h, x2, wb0, wb1, wo, ln1_g, ln1_b, 256)
    h2, act, out_g = _ffn_up_act(x1b, wup_st, convw, ffn_conv_b, 512,
                                 _gather_comm([halves_of(i) for i in (5, 6, 7)]))
    wd = stacked(out_g[0], 5).reshape(D_FF, D_MODEL)
    wpp = jnp.transpose(stacked(out_g[1], 6), (1, 0, 2)).reshape(PLE_DIM, D_MODEL)
    wpg = stacked(out_g[2], 7).reshape(D_MODEL, D_MODEL)
    dr2, dpg, dpp, loss_acc, dg2, db2 = _out_fwd_bwd(
        act, x1b, r1, p2, tgt, wd, wpg, wpp, ln1_g, ln1_b, ln2_g, ln2_b, 256)

    dh2, dr1, dcw, dcb, dg1, db1 = _ffn_bwd(h2, dr2, wd, wup_st, dpg, wpg, r1, ln1_g, convw, 256)
    d_wd = _mm_tn("ffn_down_wgrad", act, dr2, FF_TILE, 512)
    d_wpg = _mm_tn("ple_gate_wgrad", x1b, dpg, 512, D_MODEL)
    d_wpp_st = _mm_tn("ple_proj_wgrad", p2, dpp, PLE_DIM, PLE_DIM, stacked=True)
    d_wup_st = _mm("ffn_up_wgrad", x1b, dh2, TN, (2, N_CHIP),
                   pl.BlockSpec((t, 512), lambda i, j: (0, i)),
                   pl.BlockSpec((None, t, FF_TILE), lambda i, j: (j // FF_NJ, 0, j % FF_NJ)),
                   jax.ShapeDtypeStruct((N_CHIP, D_MODEL, FF_TILE), BF16),
                   pl.BlockSpec((None, 512, FF_TILE), lambda i, j: (j, i, 0)))
    da_bf, db_bf, dh, dya, dyb = _mix_bwd(dr1, h, a_br, b_br, wo, wb0, wb1, 256)
    d_wo = _mm_tn("out_proj_wgrad", m_bf, dr1, 512, 512)
    d_wb0 = _mm_tn("branch0_wgrad", ya, da_bf, 512, D_MODEL)
    d_wb1 = _mm_tn("branch1_wgrad", yb, db_bf, 512, D_MODEL)
    grads_1 = [d_wb0.reshape(4, 256, D_MODEL), d_wb1.reshape(4, 256, D_MODEL),
               d_wo.reshape(4, 256, D_MODEL), d_wup_st, d_wd.reshape(4, D_FF // 4, D_MODEL),
               d_wpp_st, d_wpg.reshape(4, 256, D_MODEL)]
    (dh, dws, dbs, dgv, dbv), recv_a1 = _sgu_bwd(h, dya, wm, wmt, bsb, sgu_norm_g, sgu_norm_b, maskf, dh,
                                                 comm=_sibling_exchange_comm(grads_1))
    parts_1 = [_rs_add_halves("rs_add_halves%d" % (i + 1), g, r, core)
               for i, (g, r) in enumerate(zip(grads_1, recv_a1))]
    (dh, dlb, dgn), recv_b1 = _hgrn_bwd(h, o_all, dyb, st_all, hgrn_lb_logits, hgrn_norm_g, dh,
                                         comm=_chip_exchange_comm(parts_1))

    grads_0 = [_in_proj_wgrad(x2b, dh, 512, D_MODEL)]
    recv_a0 = _run_comm("rs_sibling_exchange0", _sibling_exchange_comm(grads_0))
    parts_0 = [_rs_add_halves("rs_add_halves0", grads_0[0], recv_a0[0], core)]
    n_xt = t // 512
    gx_head, recv_b0 = _in_proj_xgrad(dh, win_st, dr1, 512, _chip_exchange_comm(parts_0), 0, n_xt - 1)
    gx, _ = _in_proj_xgrad(dh, win_st, dr1, 512, None, n_xt - 1, 1, gx_head)
    parts = parts_0 + parts_1
    recv_b = list(recv_b0) + list(recv_b1)
    halves = [_rs_sum_chips("rs_sum_chips%d" % i, pt, r, chip_id)
              for i, (pt, r) in enumerate(zip(parts, recv_b))]
    theirs = _rs_send_halves(halves)
    big_out = [_adamw_rows("adamw_big%d" % i, halves[i], theirs[i], big_w[i], big_m[i], big_v[i], core)
               for i in range(len(halves))]

    small_in = dict(sgu_w_s=(sgu_w_s, m_sgu_w_s, v_sgu_w_s), sgu_b_s=(sgu_b_s, m_sgu_b_s, v_sgu_b_s),
                    sgu_norm_g=(sgu_norm_g, m_sgu_norm_g, v_sgu_norm_g),
                    sgu_norm_b=(sgu_norm_b, m_sgu_norm_b, v_sgu_norm_b),
                    hgrn_norm_g=(hgrn_norm_g, m_hgrn_norm_g, v_hgrn_norm_g),
                    ln1_g=(ln1_g, m_ln1_g, v_ln1_g), ln1_b=(ln1_b, m_ln1_b, v_ln1_b),
                    ffn_conv_b=(ffn_conv_b, m_ffn_conv_b, v_ffn_conv_b),
                    ln2_g=(ln2_g, m_ln2_g, v_ln2_g), ln2_b=(ln2_b, m_ln2_b, v_ln2_b))

    def flat(name, arr):
        rows = dict((n, r) for n, r, _, _ in SMALL_LAYOUT)[name]
        return arr.reshape(rows, arr.size // rows)

    names = [n for n, _, _, _ in SMALL_LAYOUT]
    sw = [flat(n, small_in[n][0]) for n in names]
    sm = [flat(n, small_in[n][1]) for n in names]
    sv = [flat(n, small_in[n][2]) for n in names]
    loss_rows, dcw_tot, lg_out, small_out = _small_allreduce_adamw(
        [dgv, dbv, dlb, dgn, dg1, db1, dg2, db2, loss_acc], dws.reshape(N_GROUP * 128, 128), dbs, dcw, dcb,
        hgrn_lb_logits, m_hgrn_lb_logits, v_hgrn_lb_logits, sw, sm, sv)
    loss = loss_rows[0, 0]

    chip = 2 * lax.axis_index("x") + lax.axis_index("y")
    g_cw = lax.dynamic_slice(dcw_tot, (0, chip * (D_FF // 4)), (3, D_FF // 4))
    cw_out = _adamw_whole("adamw_conv_w", g_cw, ffn_conv_w[0], m_ffn_conv_w[0], v_ffn_conv_w[0])

    res = {}
    for si, n in enumerate(names):
        shp = small_in[n][0].shape
        res[n] = tuple(small_out[4 * si + k].reshape(shp) for k in range(4))
    res["hgrn_lb_logits"] = tuple(lg_out)
    res["ffn_conv_w"] = (g_cw[None],) + tuple(o[None] for o in cw_out)

    def big(i):
        return tuple(big_out[i])

    res["w_in"] = tuple(o[None] for o in big(0))
    res["w_branch"] = tuple(jnp.stack([o0, o1])[None] for o0, o1 in zip(big(1), big(2)))
    res["w_out"] = tuple(o[None] for o in big(3))
    res["ffn_w_up"] = tuple(o[None] for o in big(4))
    res["ffn_w_down"] = tuple(o[None] for o in big(5))
    res["ple_w_proj"] = tuple(o[None] for o in big(6))
    res["ple_w_gate"] = tuple(o[None] for o in big(7))

    order = ["w_in", "sgu_w_s", "sgu_b_s", "sgu_norm_g", "sgu_norm_b", "hgrn_lb_logits",
             "hgrn_norm_g", "w_branch", "w_out", "ln1_g", "ln1_b", "ffn_w_up", "ffn_conv_w",
             "ffn_conv_b", "ffn_w_down", "ln2_g", "ln2_b", "ple_w_proj", "ple_w_gate"]
    outs = [loss, gx]
    for k in range(4):
        outs += [res[n][k] for n in order]
    return tuple(outs)
```
